```python
import math
import jax, jax.numpy as jnp
from jax import lax
import numpy as np

D_MODEL = 1024
BATCH = 8
SEQ = 8192
DEPTH = 1

HEAD_DIM = 64
DIL_PATTERNS = ((128, 1), (512, 4), (2048, 16))
N_DIL_GROUPS = len(DIL_PATTERNS)
HEADS_PER_GROUP = 4
N_ATTN_HEADS = N_DIL_GROUPS * HEADS_PER_GROUP
ATTN_WIDTH = N_ATTN_HEADS * HEAD_DIM
ATTN_OUT_WIDTH = HEADS_PER_GROUP * HEAD_DIM
QBLK = 128
ROPE_THETA = 10000.0

CHUNK = 128
GMLP_GROUPS = 4
GMLP_GROUP_CH = 128
GMLP_WIDTH = GMLP_GROUPS * GMLP_GROUP_CH

D_FF = 4 * D_MODEL
EPS = 1e-6

Q0 = 0
K0 = Q0 + ATTN_WIDTH
V0 = K0 + ATTN_WIDTH
U0 = V0 + ATTN_WIDTH
Z0 = U0 + GMLP_WIDTH
GA0 = Z0 + GMLP_WIDTH
GB0 = GA0 + D_MODEL
IN_WIDTH = GB0 + D_MODEL

kernel_name = "hybrid_dilated_attn_gmlp_block"


def _rmsnorm(x, gain):
    xf = x.astype(jnp.float32)
    y = xf * lax.rsqrt(jnp.mean(xf * xf, axis=-1, keepdims=True) + EPS)
    return (y * gain.astype(jnp.float32)).astype(x.dtype)


def _layernorm(x, gain, bias):
    xf = x.astype(jnp.float32)
    mu = jnp.mean(xf, axis=-1, keepdims=True)
    var = jnp.mean(jnp.square(xf - mu), axis=-1, keepdims=True)
    y = (xf - mu) * lax.rsqrt(var + EPS)
    return (y * gain.astype(jnp.float32) + bias.astype(jnp.float32)).astype(x.dtype)


def _rope(x):
    S, Dh = x.shape[1], x.shape[-1]
    half = Dh // 2
    inv_freq = ROPE_THETA ** (-jnp.arange(half, dtype=jnp.float32) / half)
    ang = jnp.arange(S, dtype=jnp.float32)[:, None] * inv_freq[None, :]
    cos = jnp.cos(ang)[None, :, None, :]
    sin = jnp.sin(ang)[None, :, None, :]
    xf = x.astype(jnp.float32)
    x1, x2 = xf[..., :half], xf[..., half:]
    return jnp.concatenate([x1 * cos - x2 * sin, x2 * cos + x1 * sin], axis=-1).astype(x.dtype)


def _dilated_window_attention(q, k, v, dilation, n_back):
    B, S, H, Dh = q.shape
    L = S // dilation
    nb = -(-L // QBLK)
    Lp = nb * QBLK

    def to_sub(t):
        t = t.reshape(B, L, dilation, H, Dh).transpose(0, 2, 3, 1, 4)
        t = jnp.pad(t, ((0, 0), (0, 0), (0, 0), (0, Lp - L), (0, 0)))
        return t.reshape(B, dilation, H, nb, QBLK, Dh)

    def with_prev(t):
        prev = jnp.pad(t[:, :, :, :-1], ((0, 0), (0, 0), (0, 0), (1, 0), (0, 0), (0, 0)))
        return jnp.concatenate([prev, t], axis=4)

    qb, kb, vb = to_sub(q), to_sub(k), to_sub(v)
    kc, vc = with_prev(kb), with_prev(vb)
    s = jnp.einsum('brhnqe,brhnke->brhnqk', qb.astype(jnp.float32),
                   kc.astype(jnp.float32)) * (Dh ** -0.5)
    qi = jnp.arange(QBLK)[:, None]
    kj = jnp.arange(2 * QBLK)[None, :]
    dist = qi + QBLK - kj
    band = (dist >= 0) & (dist <= n_back)
    key_sub = jnp.arange(nb)[:, None, None] * QBLK + kj[None] - QBLK
    mask = band[None] & (key_sub >= 0)
    s = jnp.where(mask, s, jnp.float32(-1e30))
    m = jnp.max(s, axis=-1, keepdims=True)
    p = jnp.exp(s - m)
    den = jnp.sum(p, axis=-1, keepdims=True)
    o = jnp.einsum('brhnqk,brhnke->brhnqe', p, vc.astype(jnp.float32)) / den
    lse = (m + jnp.log(den))[..., 0]
    o = o.reshape(B, dilation, H, Lp, Dh)[:, :, :, :L].transpose(0, 3, 1, 2, 4).reshape(B, S, H, Dh)
    lse = lse.reshape(B, dilation, H, Lp)[:, :, :, :L].transpose(0, 3, 1, 2).reshape(B, S, H)
    return o, lse


def _mixer_dilated_attention(q, k, v):
    B, S = q.shape[0], q.shape[1]
    outs, lses = [], []
    for g, (window, dilation) in enumerate(DIL_PATTERNS):
        sl = slice(g * HEADS_PER_GROUP, (g + 1) * HEADS_PER_GROUP)
        o, lse = _dilated_window_attention(q[:, :, sl], k[:, :, sl], v[:, :, sl],
                                           dilation, window // dilation)
        outs.append(o)
        lses.append(lse)
    alpha = jax.nn.softmax(jnp.stack(lses, axis=0), axis=0)
    o = jnp.sum(alpha[..., None] * jnp.stack(outs, axis=0), axis=0)
    return o.reshape(B, S, ATTN_OUT_WIDTH).astype(q.dtype)


def _mixer_chunked_gmlp(u, z, ln_gain, ln_bias, w_spatial, b_spatial):
    B, S, _ = u.shape
    nc = S // CHUNK
    z = _layernorm(z, ln_gain, ln_bias)
    zc = z.reshape(B, nc, CHUNK, GMLP_GROUPS, GMLP_GROUP_CH)
    tril = jnp.tril(jnp.ones((CHUNK, CHUNK), dtype=bool))
    w = jnp.where(tril[None], w_spatial, jnp.zeros_like(w_spatial))
    sz = jnp.einsum('gij,bcjgd->bcigd', w.astype(jnp.float32), zc.astype(jnp.float32))
    sz = sz + b_spatial.T.astype(jnp.float32)[None, None, :, :, None]
    out = u.reshape(B, nc, CHUNK, GMLP_GROUPS, GMLP_GROUP_CH).astype(jnp.float32) * sz
    return out.reshape(B, S, GMLP_WIDTH).astype(u.dtype)


def _fwd_setup_inputs(seed: int = 0) -> dict:
    key = jax.random.key(seed)
    ks = jax.random.split(key, 16)
    f32 = jnp.float32

    def dense(k, fan_in, fan_out):
        return jax.random.normal(k, (DEPTH, fan_in, fan_out), f32) * (fan_in ** -0.5)

    def gain(k, n):
        return 1.0 + 0.02 * jax.random.normal(k, (DEPTH, n), f32)

    return {
        "x": jax.random.normal(ks[0], (BATCH, SEQ, D_MODEL), f32),
        "norm_pre_mix": gain(ks[1], D_MODEL),
        "w_in": dense(ks[2], D_MODEL, IN_WIDTH),
        "w_spatial": jax.random.normal(ks[3], (DEPTH, GMLP_GROUPS, CHUNK, CHUNK), f32) * (CHUNK ** -0.5),
        "b_spatial": 1.0 + 0.1 * jax.random.normal(ks[4], (DEPTH, GMLP_GROUPS, CHUNK), f32),
        "ln_v_gain": gain(ks[5], GMLP_WIDTH),
        "ln_v_bias": 0.02 * jax.random.normal(ks[6], (DEPTH, GMLP_WIDTH), f32),
        "w_branch_attn": dense(ks[7], ATTN_OUT_WIDTH, D_MODEL),
        "w_branch_gmlp": dense(ks[8], GMLP_WIDTH, D_MODEL),
        "w_out": dense(ks[9], D_MODEL, D_MODEL),
        "norm_post_mix": gain(ks[10], D_MODEL),
        "norm_pre_mlp": gain(ks[11], D_MODEL),
        "w_mlp_in": dense(ks[12], D_MODEL, D_FF),
        "w_mlp_out": dense(ks[13], D_FF, D_MODEL),
        "norm_post_mlp": gain(ks[14], D_MODEL),
    }


def _fwd_reference(x, norm_pre_mix, w_in, w_spatial, b_spatial, ln_v_gain, ln_v_bias,
              w_branch_attn, w_branch_gmlp, w_out, norm_post_mix, norm_pre_mlp,
              w_mlp_in, w_mlp_out, norm_post_mlp):
    B, S, D = x.shape
    for layer in range(DEPTH):
        h = _rmsnorm(x, norm_pre_mix[layer])
        proj = jnp.einsum('bsd,de->bse', h, w_in[layer])
        q = _rope(proj[..., Q0:K0].reshape(B, S, N_ATTN_HEADS, HEAD_DIM))
        k = _rope(proj[..., K0:V0].reshape(B, S, N_ATTN_HEADS, HEAD_DIM))
        v = proj[..., V0:U0].reshape(B, S, N_ATTN_HEADS, HEAD_DIM)
        u = jax.nn.gelu(proj[..., U0:Z0])
        z = jax.nn.gelu(proj[..., Z0:GA0])
        gate_a = jax.nn.sigmoid(proj[..., GA0:GB0])
        gate_b = jax.nn.sigmoid(proj[..., GB0:IN_WIDTH])

        y_attn = _mixer_dilated_attention(q, k, v)
        y_gmlp = _mixer_chunked_gmlp(u, z, ln_v_gain[layer], ln_v_bias[layer],
                                     w_spatial[layer], b_spatial[layer])
        merged = (gate_a * jnp.einsum('bse,ed->bsd', y_attn, w_branch_attn[layer])
                  + gate_b * jnp.einsum('bse,ed->bsd', y_gmlp, w_branch_gmlp[layer]))
        y = jnp.einsum('bsd,de->bse', merged, w_out[layer])
        x = x + _rmsnorm(y, norm_post_mix[layer])

        h = _rmsnorm(x, norm_pre_mlp[layer])
        a = jax.nn.relu(jnp.einsum('bsd,df->bsf', h, w_mlp_in[layer]))
        y = jnp.einsum('bsf,fd->bsd', a * a, w_mlp_out[layer])
        x = x + _rmsnorm(y, norm_post_mlp[layer])
    return x


import jax as _jax
import jax.numpy as _jnp

TWIN_FORMAT = 'train_step'
FWD_PARAMS = ['x', 'norm_pre_mix', 'w_in', 'w_spatial', 'b_spatial', 'ln_v_gain', 'ln_v_bias', 'w_branch_attn', 'w_branch_gmlp', 'w_out', 'norm_post_mix', 'norm_pre_mlp', 'w_mlp_in', 'w_mlp_out', 'norm_post_mlp']
TWIN_WEIGHTS = ['norm_pre_mix', 'w_in', 'w_spatial', 'b_spatial', 'ln_v_gain', 'ln_v_bias', 'w_branch_attn', 'w_branch_gmlp', 'w_out', 'norm_post_mix', 'norm_pre_mlp', 'w_mlp_in', 'w_mlp_out', 'norm_post_mlp']
TWIN_DIFF_INPUT = 'x'
TWIN_INPUTS = ['x', 'norm_pre_mix', 'w_in', 'w_spatial', 'b_spatial', 'ln_v_gain', 'ln_v_bias', 'w_branch_attn', 'w_branch_gmlp', 'w_out', 'norm_post_mix', 'norm_pre_mlp', 'w_mlp_in', 'w_mlp_out', 'norm_post_mlp', 'loss_target', 'm_norm_pre_mix', 'm_w_in', 'm_w_spatial', 'm_b_spatial', 'm_ln_v_gain', 'm_ln_v_bias', 'm_w_branch_attn', 'm_w_branch_gmlp', 'm_w_out', 'm_norm_post_mix', 'm_norm_pre_mlp', 'm_w_mlp_in', 'm_w_mlp_out', 'm_norm_post_mlp', 'v_norm_pre_mix', 'v_w_in', 'v_w_spatial', 'v_b_spatial', 'v_ln_v_gain', 'v_ln_v_bias', 'v_w_branch_attn', 'v_w_branch_gmlp', 'v_w_out', 'v_norm_post_mix', 'v_norm_pre_mlp', 'v_w_mlp_in', 'v_w_mlp_out', 'v_norm_post_mlp']
TWIN_OUTPUTS = ['loss', 'grad_x', 'grad_norm_pre_mix', 'grad_w_in', 'grad_w_spatial', 'grad_b_spatial', 'grad_ln_v_gain', 'grad_ln_v_bias', 'grad_w_branch_attn', 'grad_w_branch_gmlp', 'grad_w_out', 'grad_norm_post_mix', 'grad_norm_pre_mlp', 'grad_w_mlp_in', 'grad_w_mlp_out', 'grad_norm_post_mlp', 'delta_norm_pre_mix', 'delta_w_in', 'delta_w_spatial', 'delta_b_spatial', 'delta_ln_v_gain', 'delta_ln_v_bias', 'delta_w_branch_attn', 'delta_w_branch_gmlp', 'delta_w_out', 'delta_norm_post_mix', 'delta_norm_pre_mlp', 'delta_w_mlp_in', 'delta_w_mlp_out', 'delta_norm_post_mlp', 'new_m_norm_pre_mix', 'new_m_w_in', 'new_m_w_spatial', 'new_m_b_spatial', 'new_m_ln_v_gain', 'new_m_ln_v_bias', 'new_m_w_branch_attn', 'new_m_w_branch_gmlp', 'new_m_w_out', 'new_m_norm_post_mix', 'new_m_norm_pre_mlp', 'new_m_w_mlp_in', 'new_m_w_mlp_out', 'new_m_norm_post_mlp', 'new_v_norm_pre_mix', 'new_v_w_in', 'new_v_w_spatial', 'new_v_b_spatial', 'new_v_ln_v_gain', 'new_v_ln_v_bias', 'new_v_w_branch_attn', 'new_v_w_branch_gmlp', 'new_v_w_out', 'new_v_norm_post_mix', 'new_v_norm_pre_mlp', 'new_v_w_mlp_in', 'new_v_w_mlp_out', 'new_v_norm_post_mlp']
TWIN_LEAF_KINDS = {'loss': 'loss', 'grad_x': 'grad_x', 'grad_norm_pre_mix': 'grad_w', 'grad_w_in': 'grad_w', 'grad_w_spatial': 'grad_w', 'grad_b_spatial': 'grad_w', 'grad_ln_v_gain': 'grad_w', 'grad_ln_v_bias': 'grad_w', 'grad_w_branch_attn': 'grad_w', 'grad_w_branch_gmlp': 'grad_w', 'grad_w_out': 'grad_w', 'grad_norm_post_mix': 'grad_w', 'grad_norm_pre_mlp': 'grad_w', 'grad_w_mlp_in': 'grad_w', 'grad_w_mlp_out': 'grad_w', 'grad_norm_post_mlp': 'grad_w', 'delta_norm_pre_mix': 'delta_w', 'delta_w_in': 'delta_w', 'delta_w_spatial': 'delta_w', 'delta_b_spatial': 'delta_w', 'delta_ln_v_gain': 'delta_w', 'delta_ln_v_bias': 'delta_w', 'delta_w_branch_attn': 'delta_w', 'delta_w_branch_gmlp': 'delta_w', 'delta_w_out': 'delta_w', 'delta_norm_post_mix': 'delta_w', 'delta_norm_pre_mlp': 'delta_w', 'delta_w_mlp_in': 'delta_w', 'delta_w_mlp_out': 'delta_w', 'delta_norm_post_mlp': 'delta_w', 'new_m_norm_pre_mix': 'new_m', 'new_m_w_in': 'new_m', 'new_m_w_spatial': 'new_m', 'new_m_b_spatial': 'new_m', 'new_m_ln_v_gain': 'new_m', 'new_m_ln_v_bias': 'new_m', 'new_m_w_branch_attn': 'new_m', 'new_m_w_branch_gmlp': 'new_m', 'new_m_w_out': 'new_m', 'new_m_norm_post_mix': 'new_m', 'new_m_norm_pre_mlp': 'new_m', 'new_m_w_mlp_in': 'new_m', 'new_m_w_mlp_out': 'new_m', 'new_m_norm_post_mlp': 'new_m', 'new_v_norm_pre_mix': 'new_v', 'new_v_w_in': 'new_v', 'new_v_w_spatial': 'new_v', 'new_v_b_spatial': 'new_v', 'new_v_ln_v_gain': 'new_v', 'new_v_ln_v_bias': 'new_v', 'new_v_w_branch_attn': 'new_v', 'new_v_w_branch_gmlp': 'new_v', 'new_v_w_out': 'new_v', 'new_v_norm_post_mix': 'new_v', 'new_v_norm_pre_mlp': 'new_v', 'new_v_w_mlp_in': 'new_v', 'new_v_w_mlp_out': 'new_v', 'new_v_norm_post_mlp': 'new_v'}


def _forward(args):
    return _fwd_reference(*[args[k] for k in FWD_PARAMS])


def _output_shape():
    out = _jax.eval_shape(lambda: _forward(_fwd_setup_inputs(0)))
    return out.shape, out.dtype

N_MICROBATCH = 1
ADAM_LR = 0.001
ADAM_B1 = 0.9
ADAM_B2 = 0.999
ADAM_EPS = 1e-08
ADAM_WD = 0.01
ADAM_STEP = 10
PER_EXAMPLE_BATCH_AXIS = {'x': 0, 'loss_target': 0}
SHARED_INPUTS = []
_WEIGHT_DTYPES = {'norm_pre_mix': _jnp.float32, 'w_in': _jnp.float32, 'w_spatial': _jnp.float32, 'b_spatial': _jnp.float32, 'ln_v_gain': _jnp.float32, 'ln_v_bias': _jnp.float32, 'w_branch_attn': _jnp.float32, 'w_branch_gmlp': _jnp.float32, 'w_out': _jnp.float32, 'norm_post_mix': _jnp.float32, 'norm_pre_mlp': _jnp.float32, 'w_mlp_in': _jnp.float32, 'w_mlp_out': _jnp.float32, 'norm_post_mlp': _jnp.float32}
MOMENT_SCALE = {'norm_pre_mix': 1.175516e+00, 'w_in': 4.289113e-01, 'w_spatial': 4.384141e-01, 'b_spatial': 6.572774e-01, 'ln_v_gain': 5.949975e-01, 'ln_v_bias': 6.076546e-01, 'w_branch_attn': 1.746276e-01, 'w_branch_gmlp': 9.165052e+00, 'w_out': 8.962969e+00, 'norm_post_mix': 6.473038e+01, 'norm_pre_mlp': 3.382850e+00, 'w_mlp_in': 1.661894e+00, 'w_mlp_out': 9.291157e+00, 'norm_post_mlp': 6.632358e+01}


def _to_microbatches(a, axis):
    t = _jnp.moveaxis(a, axis, 0)
    t = t.reshape((N_MICROBATCH, t.shape[0] // N_MICROBATCH) + t.shape[1:])
    return _jnp.moveaxis(t, 1, axis + 1)


def setup_inputs(seed: int = 0) -> dict:
    inp = _fwd_setup_inputs(seed)
    key = _jax.random.fold_in(_jax.random.key(seed), 7919)
    shape, _ = _output_shape()
    out = dict(inp)
    out["loss_target"] = _jax.random.normal(_jax.random.fold_in(key, 0), shape, _jnp.float32)
    for i, name in enumerate(TWIN_WEIGHTS):
        w = inp[name].astype(_jnp.float32)
        if MOMENT_SCALE is None:
            s = _jnp.sqrt(_jnp.mean(_jnp.square(w)) + 1e-30)
        else:
            s = MOMENT_SCALE[name]
        km, kv = _jax.random.split(_jax.random.fold_in(key, i + 1))
        out[name] = w
        out["m_" + name] = s * _jax.random.normal(km, w.shape, _jnp.float32)
        out["v_" + name] = (s * s) * _jax.random.uniform(kv, w.shape, _jnp.float32, 0.5, 1.5)
    if N_MICROBATCH > 1:
        for name, axis in PER_EXAMPLE_BATCH_AXIS.items():
            out[name] = _to_microbatches(out[name], axis)
    return {'x': out['x'], 'norm_pre_mix': out['norm_pre_mix'], 'w_in': out['w_in'], 'w_spatial': out['w_spatial'], 'b_spatial': out['b_spatial'], 'ln_v_gain': out['ln_v_gain'], 'ln_v_bias': out['ln_v_bias'], 'w_branch_attn': out['w_branch_attn'], 'w_branch_gmlp': out['w_branch_gmlp'], 'w_out': out['w_out'], 'norm_post_mix': out['norm_post_mix'], 'norm_pre_mlp': out['norm_pre_mlp'], 'w_mlp_in': out['w_mlp_in'], 'w_mlp_out': out['w_mlp_out'], 'norm_post_mlp': out['norm_post_mlp'], 'loss_target': out['loss_target'], 'm_norm_pre_mix': out['m_norm_pre_mix'], 'm_w_in': out['m_w_in'], 'm_w_spatial': out['m_w_spatial'], 'm_b_spatial': out['m_b_spatial'], 'm_ln_v_gain': out['m_ln_v_gain'], 'm_ln_v_bias': out['m_ln_v_bias'], 'm_w_branch_attn': out['m_w_branch_attn'], 'm_w_branch_gmlp': out['m_w_branch_gmlp'], 'm_w_out': out['m_w_out'], 'm_norm_post_mix': out['m_norm_post_mix'], 'm_norm_pre_mlp': out['m_norm_pre_mlp'], 'm_w_mlp_in': out['m_w_mlp_in'], 'm_w_mlp_out': out['m_w_mlp_out'], 'm_norm_post_mlp': out['m_norm_post_mlp'], 'v_norm_pre_mix': out['v_norm_pre_mix'], 'v_w_in': out['v_w_in'], 'v_w_spatial': out['v_w_spatial'], 'v_b_spatial': out['v_b_spatial'], 'v_ln_v_gain': out['v_ln_v_gain'], 'v_ln_v_bias': out['v_ln_v_bias'], 'v_w_branch_attn': out['v_w_branch_attn'], 'v_w_branch_gmlp': out['v_w_branch_gmlp'], 'v_w_out': out['v_w_out'], 'v_norm_post_mix': out['v_norm_post_mix'], 'v_norm_pre_mlp': out['v_norm_pre_mlp'], 'v_w_mlp_in': out['v_w_mlp_in'], 'v_w_mlp_out': out['v_w_mlp_out'], 'v_norm_post_mlp': out['v_norm_post_mlp']}


def _loss(weights, diff, rest, loss_target):
    with _jax.named_scope("forward"):
        args = {**rest, TWIN_DIFF_INPUT: diff, **{k: w.astype(_WEIGHT_DTYPES[k]) for k, w in weights.items()}}
        y = _forward(args)
    with _jax.named_scope("loss_head"):
        err = _jnp.square(y.astype(_jnp.float32) - loss_target)
        return 0.5 * _jnp.sum(_jnp.mean(err, axis=-1)) if err.ndim else 0.5 * err


def _adamw(w, g, m, v):
    m = ADAM_B1 * m + (1.0 - ADAM_B1) * g
    v = ADAM_B2 * v + (1.0 - ADAM_B2) * _jnp.square(g)
    m_hat = m / (1.0 - ADAM_B1 ** ADAM_STEP)
    v_hat = v / (1.0 - ADAM_B2 ** ADAM_STEP)
    delta = -ADAM_LR * (m_hat / (_jnp.sqrt(v_hat) + ADAM_EPS) + ADAM_WD * w)
    return delta, m, v


def reference(x, norm_pre_mix, w_in, w_spatial, b_spatial, ln_v_gain, ln_v_bias, w_branch_attn, w_branch_gmlp, w_out, norm_post_mix, norm_pre_mlp, w_mlp_in, w_mlp_out, norm_post_mlp, loss_target, m_norm_pre_mix, m_w_in, m_w_spatial, m_b_spatial, m_ln_v_gain, m_ln_v_bias, m_w_branch_attn, m_w_branch_gmlp, m_w_out, m_norm_post_mix, m_norm_pre_mlp, m_w_mlp_in, m_w_mlp_out, m_norm_post_mlp, v_norm_pre_mix, v_w_in, v_w_spatial, v_b_spatial, v_ln_v_gain, v_ln_v_bias, v_w_branch_attn, v_w_branch_gmlp, v_w_out, v_norm_post_mix, v_norm_pre_mlp, v_w_mlp_in, v_w_mlp_out, v_norm_post_mlp):
    given = dict(x=x, norm_pre_mix=norm_pre_mix, w_in=w_in, w_spatial=w_spatial, b_spatial=b_spatial, ln_v_gain=ln_v_gain, ln_v_bias=ln_v_bias, w_branch_attn=w_branch_attn, w_branch_gmlp=w_branch_gmlp, w_out=w_out, norm_post_mix=norm_post_mix, norm_pre_mlp=norm_pre_mlp, w_mlp_in=w_mlp_in, w_mlp_out=w_mlp_out, norm_post_mlp=norm_post_mlp, loss_target=loss_target, m_norm_pre_mix=m_norm_pre_mix, m_w_in=m_w_in, m_w_spatial=m_w_spatial, m_b_spatial=m_b_spatial, m_ln_v_gain=m_ln_v_gain, m_ln_v_bias=m_ln_v_bias, m_w_branch_attn=m_w_branch_attn, m_w_branch_gmlp=m_w_branch_gmlp, m_w_out=m_w_out, m_norm_post_mix=m_norm_post_mix, m_norm_pre_mlp=m_norm_pre_mlp, m_w_mlp_in=m_w_mlp_in, m_w_mlp_out=m_w_mlp_out, m_norm_post_mlp=m_norm_post_mlp, v_norm_pre_mix=v_norm_pre_mix, v_w_in=v_w_in, v_w_spatial=v_w_spatial, v_b_spatial=v_b_spatial, v_ln_v_gain=v_ln_v_gain, v_ln_v_bias=v_ln_v_bias, v_w_branch_attn=v_w_branch_attn, v_w_branch_gmlp=v_w_branch_gmlp, v_w_out=v_w_out, v_norm_post_mix=v_norm_post_mix, v_norm_pre_mlp=v_norm_pre_mlp, v_w_mlp_in=v_w_mlp_in, v_w_mlp_out=v_w_mlp_out, v_norm_post_mlp=v_norm_post_mlp)
    weights = {n: given[n] for n in TWIN_WEIGHTS}
    shared = {n: given[n] for n in SHARED_INPUTS}
    per_example = {n: given[n] for n in ['x']}
    grad_fn = _jax.value_and_grad(_loss, argnums=(0, 1))

    def one_microbatch(ex, loss_target):
        ex = dict(ex)
        diff = ex.pop(TWIN_DIFF_INPUT)
        return grad_fn(weights, diff, {**shared, **ex}, loss_target)

    if N_MICROBATCH == 1:
        loss, (grad_w, grad_x) = one_microbatch(per_example, given["loss_target"])
    else:
        def body(carry, xs):
            loss_sum, grad_sum = carry
            l_k, (gw_k, gx_k) = one_microbatch(xs[0], xs[1])
            with _jax.named_scope("update"):
                return (loss_sum + l_k, _jax.tree.map(_jnp.add, grad_sum, gw_k)), gx_k

        init = (_jnp.zeros((), _jnp.float32), _jax.tree.map(_jnp.zeros_like, weights))
        (loss, grad_w), grad_x = _jax.lax.scan(body, init, (per_example, given["loss_target"]))
    with _jax.named_scope("update"):
        delta_w, new_m, new_v = {}, {}, {}
        for n in TWIN_WEIGHTS:
            delta_w[n], new_m[n], new_v[n] = _adamw(weights[n], grad_w[n], given["m_" + n], given["v_" + n])
    return (loss, grad_x, *[grad_w[n] for n in TWIN_WEIGHTS], *[delta_w[n] for n in TWIN_WEIGHTS],
            *[new_m[n] for n in TWIN_WEIGHTS], *[new_v[n] for n in TWIN_WEIGHTS])
```

```python
import functools
import math

import jax
import jax.numpy as jnp
from jax import lax
from jax.experimental import pallas as pl
from jax.experimental.pallas import tpu as pltpu

D = 1024
HD = 64
NSLOT = 4
GW = NSLOT * HD
DILS = (1, 4, 16)
QB = 128
AW = 3 * GW
GMLP_W = 512
CHUNK = 128
DFF = 4096
EPS = 1e-6
ROPE_THETA = 10000.0
SCALE = HD ** -0.5
NEG = -1e30

ADAM_LR = 0.001
ADAM_B1 = 0.9
ADAM_B2 = 0.999
ADAM_EPS = 1e-08
ADAM_WD = 0.01
ADAM_STEP = 10

BF = jnp.bfloat16
F32 = jnp.float32
MESH = pl.DeviceIdType.MESH
VMEM_LIMIT = 56 * 1024 * 1024

PK_LOSS, PK_G1, PK_G2, PK_G3, PK_G4, PK_LNG, PK_LNB, PK_BSP, PK_WSP, PKT_ROWS = 0, 8, 16, 24, 32, 40, 44, 48, 56, 568


def _cp(sem):
    return pltpu.CompilerParams(dimension_semantics=sem, vmem_limit_bytes=VMEM_LIMIT)


def _dot(a, b):
    return jnp.dot(a, b, preferred_element_type=F32)


def _dot_nt(a, b):
    return lax.dot_general(a, b, (((1,), (1,)), ((), ())), preferred_element_type=F32)


def _dot_tn(a, b):
    return lax.dot_general(a, b, (((0,), (0,)), ((), ())), preferred_element_type=F32)


def _gelu(x):
    return jax.nn.gelu(x, approximate=True)


def _gelu_grad(x):
    k = math.sqrt(2.0 / math.pi)
    t = jnp.tanh(k * (x + 0.044715 * x * x * x))
    return 0.5 * (1.0 + t) + 0.5 * x * (1.0 - t * t) * (k * (1.0 + 3.0 * 0.044715 * x * x))


def _swap_halves(t):
    w = t.shape[1]
    lane = lax.broadcasted_iota(jnp.int32, t.shape, 1)
    first = (lane & (HD - 1)) < (HD // 2)
    return jnp.where(first, pltpu.roll(t, w - HD // 2, 1), pltpu.roll(t, HD // 2, 1))


def _head_mask(hh):
    lane = lax.broadcasted_iota(jnp.int32, (1, GW), 1)
    return jnp.logical_and(lane >= hh * HD, lane < (hh + 1) * HD)


def _rms_stats(xf):
    return lax.rsqrt(jnp.mean(xf * xf, axis=-1, keepdims=True) + EPS)


def _rms_bwd(xf, r, gain, dout):
    n = xf * r
    t = dout * gain
    dx = r * (t - n * jnp.mean(t * n, axis=-1, keepdims=True))
    return dx, jnp.sum(dout * n, axis=0, keepdims=True)


def _rms_fwd(x, gain, name):
    S = x.shape[0]
    tm = min(512, S)

    def body(x_ref, g_ref, h_ref):
        xf = x_ref[...]
        h_ref[...] = (xf * _rms_stats(xf) * g_ref[...]).astype(BF)

    return pl.pallas_call(
        body, name=name, out_shape=jax.ShapeDtypeStruct((S, D), BF), grid=(S // tm,),
        in_specs=[pl.BlockSpec((tm, D), lambda i: (i, 0)), pl.BlockSpec((1, D), lambda i: (0, 0))],
        out_specs=pl.BlockSpec((tm, D), lambda i: (i, 0)), compiler_params=_cp(("parallel",)))(x, gain)


def _proj(h, w, epi, name, tabs=None, tn=256):
    S, K = h.shape
    N = w.shape[1]
    tm = min(2048, S)

    def body(*refs):
        if epi == "rope":
            h_ref, w_ref, c_ref, s_ref, o_ref = refs
        else:
            h_ref, w_ref, o_ref = refs
        y = _dot(h_ref[...], w_ref[...])
        if epi == "rope":
            y = y * c_ref[...] + _swap_halves(y) * s_ref[...]
        elif epi == "sigmoid":
            y = jax.nn.sigmoid(y)
        o_ref[...] = y.astype(BF)

    in_specs = [pl.BlockSpec((tm, K), lambda i, j: (i, 0)), pl.BlockSpec((K, tn), lambda i, j: (0, j))]
    args = [h, w]
    if epi == "rope":
        assert tn == GW
        in_specs += [pl.BlockSpec((tm, GW), lambda i, j: (i, 0))] * 2
        args += list(tabs)
    return pl.pallas_call(
        body, name=name, out_shape=jax.ShapeDtypeStruct((S, N), BF), grid=(S // tm, N // tn),
        in_specs=in_specs, out_specs=pl.BlockSpec((tm, tn), lambda i, j: (i, j)),
        compiler_params=_cp(("parallel", "arbitrary")))(*args)


def _attn_masks(i):
    row = lax.broadcasted_iota(jnp.int32, (QB, QB), 0)
    col = lax.broadcasted_iota(jnp.int32, (QB, QB), 1)
    return col <= row, col >= row


def _attn_fwd(qk, v, d, g, name):
    S = qk.shape[0]
    L = S // d
    nb = L // QB
    qk2 = qk.reshape(L, d * 2 * AW)
    v2 = v.reshape(L, d * AW)

    def body(q_ref, kp_ref, kc_ref, vp_ref, vc_ref, o_ref, lse_ref):
        i = pl.program_id(1)
        mask_c, mask_p = _attn_masks(i)
        mask_p = jnp.logical_and(mask_p, i > 0)
        q, kp, kc, vp, vc = q_ref[...], kp_ref[...], kc_ref[...], vp_ref[...], vc_ref[...]
        o = jnp.zeros((QB, GW), F32)
        lse = jnp.zeros((QB, GW), F32)
        for hh in range(NSLOT):
            hm = _head_mask(hh)
            qh = jnp.where(hm, q, 0)
            s_p = jnp.where(mask_p, _dot_nt(qh, kp) * SCALE, NEG)
            s_c = jnp.where(mask_c, _dot_nt(qh, kc) * SCALE, NEG)
            m = jnp.maximum(jnp.max(s_p, axis=1, keepdims=True), jnp.max(s_c, axis=1, keepdims=True))
            p_p = jnp.exp(s_p - m)
            p_c = jnp.exp(s_c - m)
            den = jnp.sum(p_p, axis=1, keepdims=True) + jnp.sum(p_c, axis=1, keepdims=True)
            acc = _dot(p_p.astype(BF), vp) + _dot(p_c.astype(BF), vc)
            o = o + jnp.where(hm, acc / den, 0.0)
            lse = lse + jnp.where(hm, m + jnp.log(den), 0.0)
        o_ref[...] = o
        lse_ref[...] = lse

    blk = (QB, GW)
    prev = lambda i: jnp.maximum(i - 1, 0)
    in_specs = [
        pl.BlockSpec(blk, lambda r, i: (i, r * 6 + g)),
        pl.BlockSpec(blk, lambda r, i: (prev(i), r * 6 + 3 + g)),
        pl.BlockSpec(blk, lambda r, i: (i, r * 6 + 3 + g)),
        pl.BlockSpec(blk, lambda r, i: (prev(i), r * 3 + g)),
        pl.BlockSpec(blk, lambda r, i: (i, r * 3 + g)),
    ]
    o, lse = pl.pallas_call(
        body, name=name, grid=(d, nb),
        out_shape=(jax.ShapeDtypeStruct((L, d * GW), F32), jax.ShapeDtypeStruct((L, d * GW), F32)),
        in_specs=in_specs,
        out_specs=(pl.BlockSpec(blk, lambda r, i: (i, r)), pl.BlockSpec(blk, lambda r, i: (i, r))),
        compiler_params=_cp(("parallel", "arbitrary")))(qk2, qk2, qk2, v2, v2)
    return o.reshape(S, GW), lse.reshape(S, GW)


def _attn_combine(os_, lses):
    S = os_[0].shape[0]
    tm = min(1024, S)

    def body(o0, o1, o2, l0, l1, l2, y_ref, lse_ref):
        a, b, c = l0[...], l1[...], l2[...]
        m = jnp.maximum(jnp.maximum(a, b), c)
        wa, wb, wc = jnp.exp(a - m), jnp.exp(b - m), jnp.exp(c - m)
        den = wa + wb + wc
        y_ref[...] = ((wa * o0[...] + wb * o1[...] + wc * o2[...]) / den).astype(BF)
        lse_ref[...] = m + jnp.log(den)

    spec = pl.BlockSpec((tm, GW), lambda i: (i, 0))
    return pl.pallas_call(
        body, name="attn_combine", grid=(S // tm,),
        out_shape=(jax.ShapeDtypeStruct((S, GW), BF), jax.ShapeDtypeStruct((S, GW), F32)),
        in_specs=[spec] * 6, out_specs=(spec, spec), compiler_params=_cp(("parallel",)))(*os_, *lses)


def _tril(upper=False):
    row = lax.broadcasted_iota(jnp.int32, (CHUNK, CHUNK), 0)
    col = lax.broadcasted_iota(jnp.int32, (CHUNK, CHUNK), 1)
    return row <= col if upper else col <= row


def _ln_fwd(z, gain, bias):
    mu = jnp.mean(z, axis=-1, keepdims=True)
    zc = z - mu
    rstd = lax.rsqrt(jnp.mean(zc * zc, axis=-1, keepdims=True) + EPS)
    zhat = zc * rstd
    return zhat, rstd, zhat * gain + bias


def _gmlp_fwd(puz, ln_g, ln_b, w_sp, b_sp_t):
    S = puz.shape[0]
    tm = min(512, S)
    nch = tm // CHUNK

    def body(p_ref, g_ref, b_ref, w_ref, bt_ref, o_ref):
        tril = _tril()
        ws = [jnp.where(tril, w_ref[gg], 0.0).astype(BF) for gg in range(4)]
        for ch in range(nch):
            rows = slice(ch * CHUNK, (ch + 1) * CHUNK)
            z = _gelu(p_ref[rows, GMLP_W:].astype(F32))
            _, _, zn = _ln_fwd(z, g_ref[...], b_ref[...])
            zn = zn.astype(BF)
            for gg in range(4):
                cols = slice(gg * CHUNK, (gg + 1) * CHUNK)
                sz = _dot(ws[gg], zn[:, cols]) + bt_ref[:, gg:gg + 1]
                u = _gelu(p_ref[rows, cols].astype(F32))
                o_ref[rows, cols] = (u * sz).astype(BF)

    return pl.pallas_call(
        body, name="gmlp_fwd", out_shape=jax.ShapeDtypeStruct((S, GMLP_W), BF), grid=(S // tm,),
        in_specs=[pl.BlockSpec((tm, 2 * GMLP_W), lambda i: (i, 0)),
                  pl.BlockSpec((1, GMLP_W), lambda i: (0, 0)), pl.BlockSpec((1, GMLP_W), lambda i: (0, 0)),
                  pl.BlockSpec((4, CHUNK, CHUNK), lambda i: (0, 0, 0)), pl.BlockSpec((CHUNK, 4), lambda i: (0, 0))],
        out_specs=pl.BlockSpec((tm, GMLP_W), lambda i: (i, 0)),
        compiler_params=_cp(("parallel",)))(puz, ln_g, ln_b, w_sp, b_sp_t)


def _merge_fwd(ya, yg, gates, wba, wbg, wout, x, g2, g3):
    S = x.shape[0]
    tm = min(512, S)

    def body(ya_ref, yg_ref, gt_ref, wba_ref, wbg_ref, wo_ref, x_ref, g2_ref, g3_ref,
             a_ref, gb_ref, mg_ref, y_ref, x2_ref, h3_ref):
        a = _dot(ya_ref[...], wba_ref[...])
        b = _dot(yg_ref[...], wbg_ref[...])
        a_ref[...] = a.astype(BF)
        gb_ref[...] = b.astype(BF)
        merged = (gt_ref[:, :D].astype(F32) * a + gt_ref[:, D:].astype(F32) * b).astype(BF)
        mg_ref[...] = merged
        y = _dot(merged, wo_ref[...])
        y_ref[...] = y
        x2 = x_ref[...] + y * _rms_stats(y) * g2_ref[...]
        x2_ref[...] = x2
        h3_ref[...] = (x2 * _rms_stats(x2) * g3_ref[...]).astype(BF)

    row = lambda w: pl.BlockSpec((tm, w), lambda i: (i, 0))
    full = lambda s: pl.BlockSpec(s, lambda i: (0, 0))
    return pl.pallas_call(
        body, name="merge_fwd", grid=(S // tm,),
        out_shape=(jax.ShapeDtypeStruct((S, D), BF), jax.ShapeDtypeStruct((S, D), BF), jax.ShapeDtypeStruct((S, D), BF),
                   jax.ShapeDtypeStruct((S, D), F32), jax.ShapeDtypeStruct((S, D), F32), jax.ShapeDtypeStruct((S, D), BF)),
        in_specs=[row(GW), row(GMLP_W), row(2 * D), full((GW, D)), full((GMLP_W, D)), full((D, D)), row(D),
                  full((1, D)), full((1, D))],
        out_specs=(row(D), row(D), row(D), row(D), row(D), row(D)),
        compiler_params=_cp(("parallel",)))(ya, yg, gates, wba, wbg, wout, x, g2, g3)


def _mlp_fwd(h3, wmi, wmo, x2, tgt, g4):
    S = x2.shape[0]
    tm = min(1024, S)
    tf = 512
    nf = DFF // tf

    def body(h_ref, wi_ref, wo_ref, x2_ref, t_ref, g4_ref, a_ref, dy2_ref, dout_ref, loss_ref, dg4_ref, acc_ref):
        i, c = pl.program_id(0), pl.program_id(1)

        @pl.when(jnp.logical_and(i == 0, c == 0))
        def _():
            loss_ref[...] = jnp.zeros_like(loss_ref)
            dg4_ref[...] = jnp.zeros_like(dg4_ref)

        a = jnp.maximum(_dot(h_ref[...], wi_ref[...]), 0.0)
        a_ref[...] = a.astype(BF)
        part = _dot((a * a).astype(BF), wo_ref[...])

        @pl.when(c == 0)
        def _():
            acc_ref[...] = part

        @pl.when(c > 0)
        def _():
            acc_ref[...] += part

        @pl.when(c == nf - 1)
        def _():
            y2 = acc_ref[...]
            r = _rms_stats(y2)
            out = x2_ref[...] + y2 * r * g4_ref[...]
            err = out - t_ref[...]
            sq = err * err
            tot = jnp.sum(jnp.sum(sq, axis=1, keepdims=True), axis=0, keepdims=True) * (0.5 / D)
            lane = lax.broadcasted_iota(jnp.int32, (1, 128), 1)
            loss_ref[...] += jnp.where(lane == 0, tot, 0.0)
            dout = err * (1.0 / D)
            dout_ref[...] = dout
            dy2, dg = _rms_bwd(y2, r, g4_ref[...], dout)
            dy2_ref[...] = dy2.astype(BF)
            dg4_ref[...] += dg

    row = pl.BlockSpec((tm, D), lambda i, c: (i, 0))
    return pl.pallas_call(
        body, name="mlp_fwd", grid=(S // tm, nf),
        out_shape=(jax.ShapeDtypeStruct((S, DFF), BF), jax.ShapeDtypeStruct((S, D), BF), jax.ShapeDtypeStruct((S, D), F32),
                   jax.ShapeDtypeStruct((1, 128), F32), jax.ShapeDtypeStruct((1, D), F32)),
        in_specs=[row, pl.BlockSpec((D, tf), lambda i, c: (0, c)), pl.BlockSpec((tf, D), lambda i, c: (c, 0)),
                  row, row, pl.BlockSpec((1, D), lambda i, c: (0, 0))],
        out_specs=(pl.BlockSpec((tm, tf), lambda i, c: (i, c)), row, row,
                   pl.BlockSpec((1, 128), lambda i, c: (0, 0)), pl.BlockSpec((1, D), lambda i, c: (0, 0))),
        scratch_shapes=[pltpu.VMEM((tm, D), F32)],
        compiler_params=_cp(("arbitrary", "arbitrary")))(h3, wmi, wmo, x2, tgt, g4)


def _mlp_bwd(dy2, a, wmo, wmi, x2, y, dout, g2, g3):
    S = x2.shape[0]
    tm = min(512, S)
    tf = 512
    nf = DFF // tf

    def body(dy2_ref, a_ref, wo_ref, wi_ref, x2_ref, y_ref, dout_ref, g2_ref, g3_ref,
             dpre_ref, dx2_ref, dy_ref, dg3_ref, dg2_ref, acc_ref):
        i, c = pl.program_id(0), pl.program_id(1)

        @pl.when(jnp.logical_and(i == 0, c == 0))
        def _():
            dg3_ref[...] = jnp.zeros_like(dg3_ref)
            dg2_ref[...] = jnp.zeros_like(dg2_ref)

        da2 = _dot_nt(dy2_ref[...], wo_ref[...])
        dpre = (2.0 * a_ref[...].astype(F32) * da2).astype(BF)
        dpre_ref[...] = dpre
        part = _dot_nt(dpre, wi_ref[...])

        @pl.when(c == 0)
        def _():
            acc_ref[...] = part

        @pl.when(c > 0)
        def _():
            acc_ref[...] += part

        @pl.when(c == nf - 1)
        def _():
            x2 = x2_ref[...]
            dx3, dg3 = _rms_bwd(x2, _rms_stats(x2), g3_ref[...], acc_ref[...])
            dx2 = dout_ref[...] + dx3
            dx2_ref[...] = dx2
            dg3_ref[...] += dg3
            yv = y_ref[...]
            dy, dg2 = _rms_bwd(yv, _rms_stats(yv), g2_ref[...], dx2)
            dy_ref[...] = dy.astype(BF)
            dg2_ref[...] += dg2

    row = pl.BlockSpec((tm, D), lambda i, c: (i, 0))
    vec = pl.BlockSpec((1, D), lambda i, c: (0, 0))
    return pl.pallas_call(
        body, name="mlp_bwd", grid=(S // tm, nf),
        out_shape=(jax.ShapeDtypeStruct((S, DFF), BF), jax.ShapeDtypeStruct((S, D), F32), jax.ShapeDtypeStruct((S, D), BF),
                   jax.ShapeDtypeStruct((1, D), F32), jax.ShapeDtypeStruct((1, D), F32)),
        in_specs=[row, pl.BlockSpec((tm, tf), lambda i, c: (i, c)), pl.BlockSpec((tf, D), lambda i, c: (c, 0)),
                  pl.BlockSpec((D, tf), lambda i, c: (0, c)), row, row, row, vec, vec],
        out_specs=(pl.BlockSpec((tm, tf), lambda i, c: (i, c)), row, row, vec, vec),
        scratch_shapes=[pltpu.VMEM((tm, D), F32)],
        compiler_params=_cp(("arbitrary", "arbitrary")))(dy2, a, wmo, wmi, x2, y, dout, g2, g3)


def _mm_tn(a, b, name, square_a=False, tm=1024, tn=1024, tk=1024):
    S, M = a.shape
    N = b.shape[1]
    tm, tn, tk = min(tm, M), min(tn, N), min(tk, S)
    nk = S // tk

    def body(a_ref, b_ref, o_ref):
        k = pl.program_id(2)
        av = a_ref[...]
        if square_a:
            av = av * av
        part = _dot_tn(av, b_ref[...])

        @pl.when(k == 0)
        def _():
            o_ref[...] = part

        @pl.when(k > 0)
        def _():
            o_ref[...] += part

    return pl.pallas_call(
        body, name=name, out_shape=jax.ShapeDtypeStruct((M, N), F32), grid=(M // tm, N // tn, nk),
        in_specs=[pl.BlockSpec((tk, tm), lambda i, j, k: (k, i)), pl.BlockSpec((tk, tn), lambda i, j, k: (k, j))],
        out_specs=pl.BlockSpec((tm, tn), lambda i, j, k: (i, j)),
        compiler_params=_cp(("parallel", "parallel", "arbitrary")))(a, b)


def _outproj_bwd(dy, wout, abr, gbr, gates, wba, wbg):
    S = dy.shape[0]
    tm = min(512, S)

    def body(dy_ref, wo_ref, a_ref, b_ref, gt_ref, wba_ref, wbg_ref, dgt_ref, da_ref, db_ref, dya_ref, dyg_ref):
        dm = _dot_nt(dy_ref[...], wo_ref[...])
        ga, gb = gt_ref[:, :D].astype(F32), gt_ref[:, D:].astype(F32)
        dgt_ref[:, :D] = (dm * a_ref[...].astype(F32) * ga * (1.0 - ga)).astype(BF)
        dgt_ref[:, D:] = (dm * b_ref[...].astype(F32) * gb * (1.0 - gb)).astype(BF)
        da = (dm * ga).astype(BF)
        db = (dm * gb).astype(BF)
        da_ref[...] = da
        db_ref[...] = db
        dya_ref[...] = _dot_nt(da, wba_ref[...]).astype(BF)
        dyg_ref[...] = _dot_nt(db, wbg_ref[...]).astype(BF)

    row = lambda w: pl.BlockSpec((tm, w), lambda i: (i, 0))
    full = lambda s: pl.BlockSpec(s, lambda i: (0, 0))
    return pl.pallas_call(
        body, name="outproj_bwd", grid=(S // tm,),
        out_shape=(jax.ShapeDtypeStruct((S, 2 * D), BF), jax.ShapeDtypeStruct((S, D), BF), jax.ShapeDtypeStruct((S, D), BF),
                   jax.ShapeDtypeStruct((S, GW), BF), jax.ShapeDtypeStruct((S, GMLP_W), BF)),
        in_specs=[row(D), full((D, D)), row(D), row(D), row(2 * D), full((GW, D)), full((GMLP_W, D))],
        out_specs=(row(2 * D), row(D), row(D), row(GW), row(GMLP_W)),
        compiler_params=_cp(("parallel",)))(dy, wout, abr, gbr, gates, wba, wbg)


def _gmlp_bwd(puz, dyg, ln_g, ln_b, w_sp, w_sp_t, b_sp_t):
    S = puz.shape[0]
    tm = min(512, S)
    nch = tm // CHUNK

    def body(p_ref, dy_ref, g_ref, b_ref, w_ref, wt_ref, bt_ref,
             dp_ref, dw_ref, dbs_ref, dg_ref, dbias_ref, dbacc_ref):
        i = pl.program_id(0)

        @pl.when(i == 0)
        def _():
            dw_ref[...] = jnp.zeros_like(dw_ref)
            dbacc_ref[...] = jnp.zeros_like(dbacc_ref)
            dg_ref[...] = jnp.zeros_like(dg_ref)
            dbias_ref[...] = jnp.zeros_like(dbias_ref)

        tril = _tril()
        ws = [jnp.where(tril, w_ref[gg], 0.0).astype(BF) for gg in range(4)]
        triu = _tril(upper=True)
        wts = [jnp.where(triu, wt_ref[gg], 0.0).astype(BF) for gg in range(4)]
        gain = g_ref[...]
        for ch in range(nch):
            rows = slice(ch * CHUNK, (ch + 1) * CHUNK)
            pz = p_ref[rows, GMLP_W:].astype(F32)
            z = _gelu(pz)
            zhat, rstd, zn = _ln_fwd(z, gain, b_ref[...])
            znb = zn.astype(BF)
            dzn_parts = []
            for gg in range(4):
                cols = slice(gg * CHUNK, (gg + 1) * CHUNK)
                pu = p_ref[rows, cols].astype(F32)
                u = _gelu(pu)
                sz = _dot(ws[gg], znb[:, cols]) + bt_ref[:, gg:gg + 1]
                dyv = dy_ref[rows, cols].astype(F32)
                dp_ref[rows, cols] = (dyv * sz * _gelu_grad(pu)).astype(BF)
                dsz = dyv * u
                dbacc_ref[gg] += dsz
                dszb = dsz.astype(BF)
                dw_ref[gg] += _dot_nt(dszb, znb[:, cols])
                dzn_parts.append(_dot(wts[gg], dszb))
            dzn = jnp.concatenate(dzn_parts, axis=1)
            dg_ref[...] += jnp.sum(dzn * zhat, axis=0, keepdims=True)
            dbias_ref[...] += jnp.sum(dzn, axis=0, keepdims=True)
            dzh = dzn * gain
            dz = rstd * (dzh - jnp.mean(dzh, axis=-1, keepdims=True)
                         - zhat * jnp.mean(dzh * zhat, axis=-1, keepdims=True))
            dp_ref[rows, GMLP_W:] = (dz * _gelu_grad(pz)).astype(BF)

        @pl.when(i == pl.num_programs(0) - 1)
        def _():
            for gg in range(4):
                dw_ref[gg] = jnp.where(tril, dw_ref[gg], 0.0)
                dbs_ref[gg] = jnp.sum(dbacc_ref[gg], axis=1, keepdims=True)

    full2 = lambda s: pl.BlockSpec(s, lambda i: (0, 0))
    full3 = lambda s: pl.BlockSpec(s, lambda i: (0, 0, 0))
    return pl.pallas_call(
        body, name="gmlp_bwd", grid=(S // tm,),
        out_shape=(jax.ShapeDtypeStruct((S, 2 * GMLP_W), BF), jax.ShapeDtypeStruct((4, CHUNK, CHUNK), F32),
                   jax.ShapeDtypeStruct((4, CHUNK, 1), F32), jax.ShapeDtypeStruct((1, GMLP_W), F32),
                   jax.ShapeDtypeStruct((1, GMLP_W), F32)),
        in_specs=[pl.BlockSpec((tm, 2 * GMLP_W), lambda i: (i, 0)), pl.BlockSpec((tm, GMLP_W), lambda i: (i, 0)),
                  full2((1, GMLP_W)), full2((1, GMLP_W)), full3((4, CHUNK, CHUNK)), full3((4, CHUNK, CHUNK)),
                  full2((CHUNK, 4))],
        out_specs=(pl.BlockSpec((tm, 2 * GMLP_W), lambda i: (i, 0)), full3((4, CHUNK, CHUNK)), full3((4, CHUNK, 1)),
                   full2((1, GMLP_W)), full2((1, GMLP_W))),
        scratch_shapes=[pltpu.VMEM((4, CHUNK, CHUNK), F32)],
        compiler_params=_cp(("arbitrary",)))(puz, dyg, ln_g, ln_b, w_sp, w_sp_t, b_sp_t)


def _attn_bwd(qk, v, dya, ya, lse, cos_t, sin_t, d, g, name, prev_bufs=None):
    S = qk.shape[0]
    L = S // d
    nb = L // QB
    view = lambda t: t.reshape(L, d * t.shape[1])
    qk2, v2, dya2, ya2, lse2, cos2, sin2 = (view(t) for t in (qk, v, dya, ya, lse, cos_t, sin_t))

    def body(*refs):
        (q_ref, qn_ref, kp_ref, kc_ref, vp_ref, vc_ref, dy_ref, dyn_ref, y_ref, yn_ref,
         l_ref, ln_ref, c_ref, s_ref) = refs[:14]
        dq_ref, dk_ref, dv_ref = refs[-3:]
        i = pl.program_id(1)
        mask_c, mask_p = _attn_masks(i)
        mask_prev = jnp.logical_and(mask_p, i > 0)
        mask_next = jnp.logical_and(mask_p, i < nb - 1)
        kp, kc, vp, vc = kp_ref[...], kc_ref[...], vp_ref[...], vc_ref[...]
        q_all, qn_all, dy_all, dyn_all = q_ref[...], qn_ref[...], dy_ref[...], dyn_ref[...]
        dyy = dy_all.astype(F32) * y_ref[...].astype(F32)
        dyy_n = dyn_all.astype(F32) * yn_ref[...].astype(F32)
        lse, lse_n = l_ref[...], ln_ref[...]
        dq = jnp.zeros((QB, GW), F32)
        dk = jnp.zeros((QB, GW), F32)
        dv = jnp.zeros((QB, GW), F32)
        for hh in range(NSLOT):
            hm = _head_mask(hh)
            q, qn = jnp.where(hm, q_all, 0), jnp.where(hm, qn_all, 0)
            dy, dyn = jnp.where(hm, dy_all, 0), jnp.where(hm, dyn_all, 0)
            dsum = jnp.sum(jnp.where(hm, dyy, 0.0), axis=1, keepdims=True)
            dsum_n = jnp.sum(jnp.where(hm, dyy_n, 0.0), axis=1, keepdims=True)
            lrow = jnp.max(jnp.where(hm, lse, -jnp.inf), axis=1, keepdims=True)
            lrow_n = jnp.max(jnp.where(hm, lse_n, -jnp.inf), axis=1, keepdims=True)
            p_a = jnp.where(mask_c, jnp.exp(_dot_nt(q, kc) * SCALE - lrow), 0.0)
            ds_a = (p_a * (_dot_nt(dy, vc) - dsum)).astype(BF)
            p_b = jnp.where(mask_prev, jnp.exp(_dot_nt(q, kp) * SCALE - lrow), 0.0)
            ds_b = (p_b * (_dot_nt(dy, vp) - dsum)).astype(BF)
            p_n = jnp.where(mask_next, jnp.exp(_dot_nt(qn, kc) * SCALE - lrow_n), 0.0)
            ds_n = (p_n * (_dot_nt(dyn, vc) - dsum_n)).astype(BF)
            dq = dq + jnp.where(hm, _dot(ds_a, kc) + _dot(ds_b, kp), 0.0)
            dk = dk + _dot_tn(ds_a, q) + _dot_tn(ds_n, qn)
            dv = dv + _dot_tn(p_a.astype(BF), dy) + _dot_tn(p_n.astype(BF), dyn)
        cosv, sinv = c_ref[...], s_ref[...]
        dq = dq * SCALE
        dk = dk * SCALE
        dq_ref[...] = (dq * cosv + _swap_halves(dq * sinv)).astype(BF)
        dk_ref[...] = (dk * cosv + _swap_halves(dk * sinv)).astype(BF)
        dv_ref[...] = dv.astype(BF)

    blk = (QB, GW)
    prev = lambda i: jnp.maximum(i - 1, 0)
    nxt = lambda i: jnp.minimum(i + 1, nb - 1)
    sp = lambda f: pl.BlockSpec(blk, f)
    in_specs = [
        sp(lambda r, i: (i, r * 6 + g)), sp(lambda r, i: (nxt(i), r * 6 + g)),
        sp(lambda r, i: (prev(i), r * 6 + 3 + g)), sp(lambda r, i: (i, r * 6 + 3 + g)),
        sp(lambda r, i: (prev(i), r * 3 + g)), sp(lambda r, i: (i, r * 3 + g)),
        sp(lambda r, i: (i, r)), sp(lambda r, i: (nxt(i), r)),
        sp(lambda r, i: (i, r)), sp(lambda r, i: (nxt(i), r)),
        sp(lambda r, i: (i, r)), sp(lambda r, i: (nxt(i), r)),
        sp(lambda r, i: (i, r)), sp(lambda r, i: (i, r)),
    ]
    args = [qk2, qk2, qk2, qk2, v2, v2, dya2, dya2, ya2, ya2, lse2, lse2, cos2, sin2]
    aliases = {}
    if prev_bufs is not None:
        in_specs += [pl.BlockSpec(memory_space=pl.ANY)] * 3
        args += [t.reshape(L, d * AW) for t in prev_bufs]
        aliases = {14: 0, 15: 1, 16: 2}
    out = sp(lambda r, i: (i, r * 3 + g))
    outs = pl.pallas_call(
        body, name=name, grid=(d, nb),
        out_shape=(jax.ShapeDtypeStruct((L, d * AW), BF),) * 3,
        in_specs=in_specs, out_specs=(out, out, out), input_output_aliases=aliases,
        compiler_params=_cp(("parallel", "arbitrary")))(*args)
    return tuple(t.reshape(S, AW) for t in outs)


def _inproj_bwd(dq, dk, dv, dpuz, dgates, wqk, wv, wuz, wg, x, dx2, g1):
    S = x.shape[0]
    tm = min(512, S)

    def body(dq_ref, dk_ref, dv_ref, dp_ref, dg_ref, wq_ref, wk_ref, wv_ref, wuz_ref, wg_ref,
             x_ref, dx2_ref, g1_ref, gx_ref, dg1_ref):
        i = pl.program_id(0)

        @pl.when(i == 0)
        def _():
            dg1_ref[...] = jnp.zeros_like(dg1_ref)

        dh = (_dot_nt(dq_ref[...], wq_ref[...]) + _dot_nt(dk_ref[...], wk_ref[...]) + _dot_nt(dv_ref[...], wv_ref[...])
              + _dot_nt(dp_ref[...], wuz_ref[...]) + _dot_nt(dg_ref[...], wg_ref[...]))
        xv = x_ref[...]
        dx1, dg1 = _rms_bwd(xv, _rms_stats(xv), g1_ref[...], dh)
        gx_ref[...] = dx2_ref[...] + dx1
        dg1_ref[...] += dg1

    row = lambda w: pl.BlockSpec((tm, w), lambda i: (i, 0))
    full = lambda s: pl.BlockSpec(s, lambda i: (0, 0))
    return pl.pallas_call(
        body, name="inproj_bwd", grid=(S // tm,),
        out_shape=(jax.ShapeDtypeStruct((S, D), F32), jax.ShapeDtypeStruct((1, D), F32)),
        in_specs=[row(AW), row(AW), row(AW), row(2 * GMLP_W), row(2 * D),
                  pl.BlockSpec((D, AW), lambda i: (0, 0)), pl.BlockSpec((D, AW), lambda i: (0, 1)),
                  full((D, AW)), full((D, 2 * GMLP_W)), full((D, 2 * D)), row(D), row(D), full((1, D))],
        out_specs=(row(D), full((1, D))),
        compiler_params=_cp(("arbitrary",)))(dq, dk, dv, dpuz, dgates, wqk, wqk, wv, wuz, wg, x, dx2, g1)


def _adam_math(w, g, m, v):
    m2 = ADAM_B1 * m + (1.0 - ADAM_B1) * g
    v2 = ADAM_B2 * v + (1.0 - ADAM_B2) * (g * g)
    m_hat = m2 / (1.0 - ADAM_B1 ** ADAM_STEP)
    v_hat = v2 / (1.0 - ADAM_B2 ** ADAM_STEP)
    delta = -ADAM_LR * (m_hat / (jnp.sqrt(v_hat) + ADAM_EPS) + ADAM_WD * w)
    return delta, m2, v2


def _adam_shard(own, recv, w, m, v, name):
    R, C = w.shape
    tr = min(256, R)

    def body(own_ref, r_ref, w_ref, m_ref, v_ref, g_ref, d_ref, m2_ref, v2_ref):
        g = own_ref[...] + r_ref[0].astype(F32) + r_ref[1].astype(F32) + r_ref[2].astype(F32)
        g_ref[...] = g
        d_ref[...], m2_ref[...], v2_ref[...] = _adam_math(w_ref[...], g, m_ref[...], v_ref[...])

    spec = pl.BlockSpec((tr, C), lambda i: (i, 0))
    out = jax.ShapeDtypeStruct((R, C), F32)
    return pl.pallas_call(
        body, name=name, grid=(R // tr,), out_shape=(out, out, out, out),
        in_specs=[spec, pl.BlockSpec((3, tr, C), lambda i: (0, i, 0)), spec, spec, spec],
        out_specs=(spec, spec, spec, spec), compiler_params=_cp(("parallel",)))(own, recv, w, m, v)


def _adam_small(gp, wp, mp, vp):
    def body(g_ref, w_ref, m_ref, v_ref, d_ref, m2_ref, v2_ref):
        d_ref[...], m2_ref[...], v2_ref[...] = _adam_math(w_ref[...], g_ref[...], m_ref[...], v_ref[...])

    out = jax.ShapeDtypeStruct(gp.shape, F32)
    return pl.pallas_call(body, name="adam_small", out_shape=(out, out, out),
                          compiler_params=pltpu.CompilerParams(vmem_limit_bytes=VMEM_LIMIT))(gp, wp, mp, vp)


def _rs_add(gblocks, recv, idx, name):
    _, R, C = gblocks.shape
    tr = min(256, R)

    def body(t_ref, g_ref, r_ref, own_ref, send_ref):
        j = pl.program_id(1)
        s = g_ref[...] + r_ref[...]

        @pl.when(j == 0)
        def _():
            own_ref[...] = s

        @pl.when(j > 0)
        def _():
            send_ref[...] = s.astype(BF)

    grid_spec = pltpu.PrefetchScalarGridSpec(
        num_scalar_prefetch=1, grid=(R // tr, 4),
        in_specs=[pl.BlockSpec((None, tr, C), lambda i, j, t: (t[j], i, 0)),
                  pl.BlockSpec((None, tr, C), lambda i, j, t: (t[4 + j], i, 0))],
        out_specs=[pl.BlockSpec((tr, C), lambda i, j, t: (i, 0)),
                   pl.BlockSpec((None, tr, C), lambda i, j, t: (jnp.maximum(j - 1, 0), i, 0))])
    return pl.pallas_call(
        body, name=name, grid_spec=grid_spec,
        out_shape=(jax.ShapeDtypeStruct((R, C), F32), jax.ShapeDtypeStruct((3, R, C), BF)),
        compiler_params=_cp(("parallel", "arbitrary")))(idx, gblocks, recv)


def _mesh_pos():
    return lax.axis_index("x"), lax.axis_index("y"), lax.axis_index("c")


def _all_gather_weights(shards):
    n = len(shards)

    def body(*refs):
        ins, outs = refs[:n], refs[n:2 * n]
        send_sems, recv_sems, loc_sems = refs[2 * n:]
        x, y, c = _mesh_pos()
        me, sib = (x, y, c), (x, y, 1 - c)
        chips = [(1 - x, y), (x, 1 - y), (1 - x, 1 - y)]

        def rcopy(k, s, block, to, src=None):
            dst = outs[k].at[4 * block[0] + 2 * block[1] + block[2]]
            return pltpu.make_async_remote_copy(
                src_ref=dst if src is None else src, dst_ref=dst, send_sem=send_sems.at[k, s],
                recv_sem=recv_sems.at[k, s], device_id=to, device_id_type=MESH)

        locs = [pltpu.make_async_copy(ins[k], outs[k].at[4 * x + 2 * y + c], loc_sems.at[k]) for k in range(n)]
        for cp in locs:
            cp.start()
        started = []
        for k in range(n):
            for j, chip in enumerate(chips):
                started.append(rcopy(k, 1 + j, me, (*chip, c), src=ins[k]))
                started[-1].start()
        for k in range(n):
            started.append(rcopy(k, 0, me, sib, src=ins[k]))
            started[-1].start()
        for k in range(n):
            for j, chip in enumerate(chips):
                rcopy(k, 1 + j, (*chip, c), me).wait_recv()
                started.append(rcopy(k, 4 + j, (*chip, c), sib))
                started[-1].start()
        for k in range(n):
            rcopy(k, 0, sib, me).wait_recv()
            for j, chip in enumerate(chips):
                rcopy(k, 4 + j, (*chip, 1 - c), me).wait_recv()
        for cp in started:
            cp.wait_send()
        for cp in locs:
            cp.wait()

    hbm = pl.BlockSpec(memory_space=pl.ANY)
    return pl.pallas_call(
        body, name="ag_weights",
        out_shape=tuple(jax.ShapeDtypeStruct((8,) + s.shape, s.dtype) for s in shards),
        in_specs=[hbm] * n, out_specs=(hbm,) * n,
        scratch_shapes=[pltpu.SemaphoreType.DMA((n, 7)), pltpu.SemaphoreType.DMA((n, 7)),
                        pltpu.SemaphoreType.DMA((n,))])(*shards)


def _rs_d2d(gblocks):
    n = len(gblocks)

    def body(*refs):
        ins, outs = refs[:n], refs[n:2 * n]
        send_sems, recv_sems = refs[2 * n:]
        x, y, c = _mesh_pos()
        copies = []
        for k in range(n):
            for kk in range(4):
                copies.append(pltpu.make_async_remote_copy(
                    src_ref=ins[k].at[2 * kk + 1 - c], dst_ref=outs[k].at[kk], send_sem=send_sems.at[k, kk],
                    recv_sem=recv_sems.at[k, kk], device_id=(x, y, 1 - c), device_id_type=MESH))
                copies[-1].start()
        for cp in copies:
            cp.wait()

    hbm = pl.BlockSpec(memory_space=pl.ANY)
    return pl.pallas_call(
        body, name="rs_d2d",
        out_shape=tuple(jax.ShapeDtypeStruct((4,) + g.shape[1:], g.dtype) for g in gblocks),
        in_specs=[hbm] * n, out_specs=(hbm,) * n,
        scratch_shapes=[pltpu.SemaphoreType.DMA((n, 4)), pltpu.SemaphoreType.DMA((n, 4))])(*gblocks)


def _rs_ici(sends):
    n = len(sends)

    def body(*refs):
        ins, outs = refs[:n], refs[n:2 * n]
        send_sems, recv_sems = refs[2 * n:]
        x, y, c = _mesh_pos()
        chips = [(1 - x, y), (x, 1 - y), (1 - x, 1 - y)]
        copies = []
        for k in range(n):
            for j, chip in enumerate(chips):
                copies.append(pltpu.make_async_remote_copy(
                    src_ref=ins[k].at[j], dst_ref=outs[k].at[j], send_sem=send_sems.at[k, j],
                    recv_sem=recv_sems.at[k, j], device_id=(*chip, c), device_id_type=MESH))
                copies[-1].start()
        for cp in copies:
            cp.wait()

    hbm = pl.BlockSpec(memory_space=pl.ANY)
    return pl.pallas_call(
        body, name="rs_ici",
        out_shape=tuple(jax.ShapeDtypeStruct(s.shape, s.dtype) for s in sends),
        in_specs=[hbm] * n, out_specs=(hbm,) * n,
        scratch_shapes=[pltpu.SemaphoreType.DMA((n, 3)), pltpu.SemaphoreType.DMA((n, 3))])(*sends)


def _small_allreduce(pkt):
    R = pkt.shape[0]

    def body(p_ref, o_ref, buf, send_sems, recv_sems):
        x, y, c = _mesh_pos()
        me, sib = (x, y, c), (x, y, 1 - c)
        chips = [(1 - x, y), (x, 1 - y), (1 - x, 1 - y)]

        def rcopy(s, block, to, src=None):
            dst = buf.at[4 * block[0] + 2 * block[1] + block[2]]
            return pltpu.make_async_remote_copy(
                src_ref=dst if src is None else src, dst_ref=dst, send_sem=send_sems.at[s],
                recv_sem=recv_sems.at[s], device_id=to, device_id_type=MESH)

        buf[4 * x + 2 * y + c] = p_ref[...]
        started = [rcopy(0, me, sib, src=p_ref)]
        started += [rcopy(1 + j, me, (*chip, c), src=p_ref) for j, chip in enumerate(chips)]
        for cp in started:
            cp.start()
        for j, chip in enumerate(chips):
            rcopy(1 + j, (*chip, c), me).wait_recv()
            started.append(rcopy(4 + j, (*chip, c), sib))
            started[-1].start()
        rcopy(0, sib, me).wait_recv()
        for j, chip in enumerate(chips):
            rcopy(4 + j, (*chip, 1 - c), me).wait_recv()
        for cp in started:
            cp.wait_send()
        acc = buf[0]
        for b in range(1, 8):
            acc = acc + buf[b]
        o_ref[...] = acc

    return pl.pallas_call(
        body, name="small_allreduce", out_shape=jax.ShapeDtypeStruct(pkt.shape, F32),
        in_specs=[pl.BlockSpec(memory_space=pltpu.VMEM)], out_specs=pl.BlockSpec(memory_space=pltpu.VMEM),
        scratch_shapes=[pltpu.VMEM((8, R, 128), F32), pltpu.SemaphoreType.DMA((7,)), pltpu.SemaphoreType.DMA((7,))],
        compiler_params=pltpu.CompilerParams(vmem_limit_bytes=VMEM_LIMIT))(pkt)


def _rope_tables(S):
    half = HD // 2
    inv_freq = ROPE_THETA ** (-jnp.arange(half, dtype=F32) / half)
    ang = jnp.arange(S, dtype=F32)[:, None] * inv_freq[None, :]
    cos, sin = jnp.cos(ang), jnp.sin(ang)
    cos_t = jnp.tile(jnp.concatenate([cos, cos], axis=1), (1, NSLOT))
    sin_t = jnp.tile(jnp.concatenate([-sin, sin], axis=1), (1, NSLOT))
    return cos_t, sin_t


def _local_step(x, tgt, wqk, wv, wuz, wg, wba, wbg, wout, wmi, wmo, g1, g2, g3, g4, ln_g, ln_b, w_sp, b_sp):
    S = x.shape[0]
    cos_t, sin_t = _rope_tables(S)
    b_sp_t = b_sp.T
    w_sp_t = w_sp.transpose(0, 2, 1)

    h1 = _rms_fwd(x, g1, "rms_pre_mix")
    qk = _proj(h1, wqk, "rope", "proj_qk", tabs=(cos_t, sin_t))
    v = _proj(h1, wv, "none", "proj_v")
    puz = _proj(h1, wuz, "none", "proj_uz")
    gates = _proj(h1, wg, "sigmoid", "proj_gates")
    fwd = [_attn_fwd(qk, v, d, g, "attn_fwd_d%d" % d) for g, d in enumerate(DILS)]
    ya, lse = _attn_combine([o for o, _ in fwd], [l for _, l in fwd])
    yg = _gmlp_fwd(puz, ln_g, ln_b, w_sp, b_sp_t)
    abr, gbr, merged, y, x2, h3 = _merge_fwd(ya, yg, gates, wba, wbg, wout, x, g2, g3)
    a, dy2, dout, loss_row, dg4 = _mlp_fwd(h3, wmi, wmo, x2, tgt, g4)

    dpre, dx2, dy, dg3, dg2 = _mlp_bwd(dy2, a, wmo, wmi, x2, y, dout, g2, g3)
    dwmo = _mm_tn(a, dy2, "dw_mlp_out", square_a=True)
    dwmi = _mm_tn(h3, dpre, "dw_mlp_in")
    dgates, da, db, dya, dyg = _outproj_bwd(dy, wout, abr, gbr, gates, wba, wbg)
    dwout = _mm_tn(merged, dy, "dw_out")
    dwba = _mm_tn(ya, da, "dw_branch_attn")
    dwbg = _mm_tn(yg, db, "dw_branch_gmlp")
    dpuz, dwsp, dbs, dlng, dlnb = _gmlp_bwd(puz, dyg, ln_g, ln_b, w_sp, w_sp_t, b_sp_t)
    bufs = None
    for g, d in enumerate(DILS):
        bufs = _attn_bwd(qk, v, dya, ya, lse, cos_t, sin_t, d, g, "attn_bwd_d%d" % d, prev_bufs=bufs)
    dq, dk, dv = bufs
    grad_x, dg1 = _inproj_bwd(dq, dk, dv, dpuz, dgates, wqk, wv, wuz, wg, x, dx2, g1)
    dwin = jnp.concatenate([_mm_tn(h1, t, "dw_in_%d" % n) for n, t in enumerate((dq, dk, dv, dpuz, dgates))], axis=1)
    small = dict(g1=dg1, g2=dg2, g3=dg3, g4=dg4, ln_g=dlng, ln_b=dlnb, b_sp=dbs.reshape(4, CHUNK), w_sp=dwsp)
    return loss_row, grad_x, (dwin, dwba, dwbg, dwout, dwmi, dwmo), small


def _pack_small(first_row, t):
    pad = lambda a, rows: jnp.concatenate([a, jnp.zeros((rows - a.shape[0], 128), F32)], axis=0)
    return jnp.concatenate([
        pad(first_row, 8), t["g1"].reshape(8, 128), t["g2"].reshape(8, 128), t["g3"].reshape(8, 128),
        t["g4"].reshape(8, 128), t["ln_g"].reshape(4, 128), t["ln_b"].reshape(4, 128),
        pad(t["b_sp"].reshape(4, 128), 8), t["w_sp"].reshape(4 * CHUNK, 128)], axis=0)


def _unpack_small(p):
    vec = lambda r, n: p[r:r + n // 128].reshape(1, n)
    return dict(g1=vec(PK_G1, D), g2=vec(PK_G2, D), g3=vec(PK_G3, D), g4=vec(PK_G4, D),
                ln_g=vec(PK_LNG, GMLP_W), ln_b=vec(PK_LNB, GMLP_W),
                b_sp=p[PK_BSP:PK_BSP + 4].reshape(1, 4, CHUNK), w_sp=p[PK_WSP:PKT_ROWS].reshape(1, 4, CHUNK, CHUNK))


def kernel(x, norm_pre_mix, w_in, w_spatial, b_spatial, ln_v_gain, ln_v_bias, w_branch_attn, w_branch_gmlp, w_out, norm_post_mix, norm_pre_mlp, w_mlp_in, w_mlp_out, norm_post_mlp, loss_target, m_norm_pre_mix, m_w_in, m_w_spatial, m_b_spatial, m_ln_v_gain, m_ln_v_bias, m_w_branch_attn, m_w_branch_gmlp, m_w_out, m_norm_post_mix, m_norm_pre_mlp, m_w_mlp_in, m_w_mlp_out, m_norm_post_mlp, v_norm_pre_mix, v_w_in, v_w_spatial, v_b_spatial, v_ln_v_gain, v_ln_v_bias, v_w_branch_attn, v_w_branch_gmlp, v_w_out, v_norm_post_mix, v_norm_pre_mlp, v_w_mlp_in, v_w_mlp_out, v_norm_post_mlp):
    mx, my, mc = _mesh_pos()

    big_w = [w_in[0], w_branch_attn[0], w_branch_gmlp[0], w_out[0], w_mlp_in[0], w_mlp_out[0]]
    big_m = [m_w_in[0], m_w_branch_attn[0], m_w_branch_gmlp[0], m_w_out[0], m_w_mlp_in[0], m_w_mlp_out[0]]
    big_v = [v_w_in[0], v_w_branch_attn[0], v_w_branch_gmlp[0], v_w_out[0], v_w_mlp_in[0], v_w_mlp_out[0]]
    col_sharded = [True, True, True, False, True, False]
    gathered = _all_gather_weights([w.astype(BF) for w in big_w])
    full = [t.transpose(1, 0, 2).reshape(t.shape[1], 8 * t.shape[2]) if cs else t.reshape(8 * t.shape[1], t.shape[2])
            for t, cs in zip(gathered, col_sharded)]
    win, wba, wbg, wout, wmi, wmo = full
    wqk, wv, wuz, wg = win[:, :2 * AW], win[:, 2 * AW:3 * AW], win[:, 3 * AW:3 * AW + 2 * GMLP_W], win[:, 3 * AW + 2 * GMLP_W:]

    loss_row, grad_x, dws, small = _local_step(
        x[0], loss_target[0], wqk, wv, wuz, wg, wba, wbg, wout, wmi, wmo,
        norm_pre_mix, norm_post_mix, norm_pre_mlp, norm_post_mlp, ln_v_gain, ln_v_bias, w_spatial[0], b_spatial[0])

    gblocks = [g.reshape(g.shape[0], 8, g.shape[1] // 8).transpose(1, 0, 2) if cs
               else g.reshape(8, g.shape[0] // 8, g.shape[1]) for g, cs in zip(dws, col_sharded)]
    from_sib = _rs_d2d(gblocks)
    rel = [(0, 0), (1, 0), (0, 1), (1, 1)]
    chip_of = [2 * (mx ^ dx) + (my ^ dy) for dx, dy in rel]
    idx = jnp.stack([2 * k + mc for k in chip_of] + chip_of).astype(jnp.int32)
    halves = [_rs_add(g, r, idx, "rs_add_%d" % n) for n, (g, r) in enumerate(zip(gblocks, from_sib))]
    from_chips = _rs_ici([s for _, s in halves])
    names = ["w_in", "w_branch_attn", "w_branch_gmlp", "w_out", "w_mlp_in", "w_mlp_out"]
    upd = {nm: _adam_shard(own, r, w, m, v, "adam_" + nm)
           for nm, (own, _), r, w, m, v in zip(names, halves, from_chips, big_w, big_m, big_v)}

    pkt = _small_allreduce(_pack_small(loss_row, small))
    small_w = dict(g1=norm_pre_mix, g2=norm_post_mix, g3=norm_pre_mlp, g4=norm_post_mlp, ln_g=ln_v_gain,
                   ln_b=ln_v_bias, b_sp=b_spatial[0], w_sp=w_spatial[0])
    small_m = dict(g1=m_norm_pre_mix, g2=m_norm_post_mix, g3=m_norm_pre_mlp, g4=m_norm_post_mlp, ln_g=m_ln_v_gain,
                   ln_b=m_ln_v_bias, b_sp=m_b_spatial[0], w_sp=m_w_spatial[0])
    small_v = dict(g1=v_norm_pre_mix, g2=v_norm_post_mix, g3=v_norm_pre_mlp, g4=v_norm_post_mlp, ln_g=v_ln_v_gain,
                   ln_b=v_ln_v_bias, b_sp=v_b_spatial[0], w_sp=v_w_spatial[0])
    zero_row = jnp.zeros((1, 128), F32)
    ones_row = jnp.ones((1, 128), F32)
    sd, sm, sv = _adam_small(pkt, _pack_small(zero_row, small_w), _pack_small(zero_row, small_m),
                             _pack_small(ones_row, small_v))
    loss = pkt[0, 0]
    sg, sd, sm, sv = (_unpack_small(p) for p in (pkt, sd, sm, sv))

    order = ["g1", "w_in", "w_sp", "b_sp", "ln_g", "ln_b", "w_branch_attn", "w_branch_gmlp", "w_out", "g2", "g3",
             "w_mlp_in", "w_mlp_out", "g4"]

    def pick(which):
        out = []
        for nm in order:
            if nm in upd:
                out.append(upd[nm][which][None])
            else:
                out.append((sg, sd, sm, sv)[which][nm])
        return out

    return (loss, grad_x[None], *pick(0), *pick(1), *pick(2), *pick(3))
```

```python
import functools
import math

import jax
import jax.numpy as jnp
from jax import lax
from jax.experimental import pallas as pl
from jax.experimental.pallas import tpu as pltpu

D = 1024
HD = 64
NSLOT = 4
GW = NSLOT * HD
DILS = (1, 4, 16)
QB = 128
AW = 3 * GW
GMLP_W = 512
CHUNK = 128
DFF = 4096
EPS = 1e-6
ROPE_THETA = 10000.0
SCALE = HD ** -0.5
NEG = -1e30

ADAM_LR = 0.001
ADAM_B1 = 0.9
ADAM_B2 = 0.999
ADAM_EPS = 1e-08
ADAM_WD = 0.01
ADAM_STEP = 10

BF = jnp.bfloat16
F32 = jnp.float32
MESH = pl.DeviceIdType.MESH
VMEM_LIMIT = 56 * 1024 * 1024

PK_LOSS, PK_G1, PK_G2, PK_G3, PK_G4, PK_LNG, PK_LNB, PK_BSP, PK_WSP, PKT_ROWS = 0, 8, 16, 24, 32, 40, 44, 48, 56, 568


def _cp(sem):
    return pltpu.CompilerParams(dimension_semantics=sem, vmem_limit_bytes=VMEM_LIMIT)


def _dot(a, b):
    return jnp.dot(a, b, preferred_element_type=F32)


def _dot_nt(a, b):
    return lax.dot_general(a, b, (((1,), (1,)), ((), ())), preferred_element_type=F32)


def _dot_tn(a, b):
    return lax.dot_general(a, b, (((0,), (0,)), ((), ())), preferred_element_type=F32)


def _gelu(x):
    return jax.nn.gelu(x, approximate=True)


def _gelu_grad(x):
    k = math.sqrt(2.0 / math.pi)
    t = jnp.tanh(k * (x + 0.044715 * x * x * x))
    return 0.5 * (1.0 + t) + 0.5 * x * (1.0 - t * t) * (k * (1.0 + 3.0 * 0.044715 * x * x))


def _swap_halves(t):
    w = t.shape[1]
    lane = lax.broadcasted_iota(jnp.int32, t.shape, 1)
    first = (lane & (HD - 1)) < (HD // 2)
    return jnp.where(first, pltpu.roll(t, w - HD // 2, 1), pltpu.roll(t, HD // 2, 1))


def _head_mask(hh):
    lane = lax.broadcasted_iota(jnp.int32, (1, GW), 1)
    return jnp.logical_and(lane >= hh * HD, lane < (hh + 1) * HD)


def _rms_stats(xf):
    return lax.rsqrt(jnp.mean(xf * xf, axis=-1, keepdims=True) + EPS)


def _rms_bwd(xf, r, gain, dout):
    n = xf * r
    t = dout * gain
    dx = r * (t - n * jnp.mean(t * n, axis=-1, keepdims=True))
    return dx, jnp.sum(dout * n, axis=0, keepdims=True)


def _rms_fwd(x, gain, name):
    S = x.shape[0]
    tm = min(512, S)

    def body(x_ref, g_ref, h_ref):
        xf = x_ref[...]
        h_ref[...] = (xf * _rms_stats(xf) * g_ref[...]).astype(BF)

    return pl.pallas_call(
        body, name=name, out_shape=jax.ShapeDtypeStruct((S, D), BF), grid=(S // tm,),
        in_specs=[pl.BlockSpec((tm, D), lambda i: (i, 0)), pl.BlockSpec((1, D), lambda i: (0, 0))],
        out_specs=pl.BlockSpec((tm, D), lambda i: (i, 0)), compiler_params=_cp(("parallel",)))(x, gain)


def _proj(h, w, epi, name, tn=256):
    S, K = h.shape
    N = w.shape[1]
    tm = min(2048, S)

    def body(h_ref, w_ref, o_ref):
        y = _dot(h_ref[...], w_ref[...])
        if epi == "sigmoid":
            y = jax.nn.sigmoid(y)
        o_ref[...] = y.astype(BF)

    return pl.pallas_call(
        body, name=name, out_shape=jax.ShapeDtypeStruct((S, N), BF), grid=(S // tm, N // tn),
        in_specs=[pl.BlockSpec((tm, K), lambda i, j: (i, 0)), pl.BlockSpec((K, tn), lambda i, j: (0, j))],
        out_specs=pl.BlockSpec((tm, tn), lambda i, j: (i, j)),
        compiler_params=_cp(("parallel", "arbitrary")))(h, w)


def _dilate_store(val, scr, o_ref, lead, d):
    rows = val.shape[0]
    if d == 1:
        o_ref[lead + (0,)] = val.astype(o_ref.dtype)
        return
    for hf in range(2):
        scr[hf, pl.ds(0, rows), :] = val[:, hf * 128:(hf + 1) * 128]
    for r in range(d):
        for hf in range(2):
            o_ref[lead + (r, slice(None), slice(hf * 128, (hf + 1) * 128))] = (
                scr[hf, pl.ds(r, rows // d, stride=d), :].astype(o_ref.dtype))


def _undilate_load(i_ref, lead, d, scr, rows):
    if d == 1:
        return i_ref[lead + (0,)].astype(F32)
    for r in range(d):
        for hf in range(2):
            scr[hf, pl.ds(r, rows // d, stride=d), :] = (
                i_ref[lead + (r, slice(None), slice(hf * 128, (hf + 1) * 128))].astype(F32))
    return jnp.concatenate([scr[0, pl.ds(0, rows), :], scr[1, pl.ds(0, rows), :]], axis=1)


def _rope_fwd(y, c_ref, s_ref):
    cosv = jnp.concatenate([c_ref[...]] * 2, axis=1)
    sinv = jnp.concatenate([s_ref[...]] * 2, axis=1)
    return y * cosv + _swap_halves(y) * sinv


def _rope_bwd(dy, c_ref, s_ref):
    cosv = jnp.concatenate([c_ref[...]] * 2, axis=1)
    sinv = jnp.concatenate([s_ref[...]] * 2, axis=1)
    return dy * cosv + _swap_halves(dy * sinv)


def _proj_qkv(h, wqkv, cos_t, sin_t, g, d, name):
    S, K = h.shape
    tm = min(2048, S)

    def body(h_ref, w_ref, c_ref, s_ref, o_ref, scr):
        j = pl.program_id(1)
        y = _dot(h_ref[...], w_ref[...])
        y = jnp.where(j < 2, _rope_fwd(y, c_ref, s_ref), y)
        _dilate_store(y, scr, o_ref, (), d)

    return pl.pallas_call(
        body, name=name, out_shape=jax.ShapeDtypeStruct((3, d, S // d, GW), BF), grid=(S // tm, 3),
        in_specs=[pl.BlockSpec((tm, K), lambda i, j: (i, 0)), pl.BlockSpec((K, GW), lambda i, j: (0, 3 * j + g)),
                  pl.BlockSpec((tm, 128), lambda i, j: (i, 0)), pl.BlockSpec((tm, 128), lambda i, j: (i, 0))],
        out_specs=pl.BlockSpec((None, d, tm // d, GW), lambda i, j: (j, 0, i, 0)),
        scratch_shapes=[pltpu.VMEM((2, tm, 128), F32)],
        compiler_params=_cp(("parallel", "arbitrary")))(h, wqkv, cos_t, sin_t)


def _attn_masks(i):
    row = lax.broadcasted_iota(jnp.int32, (QB, QB), 0)
    col = lax.broadcasted_iota(jnp.int32, (QB, QB), 1)
    return col <= row, col >= row


def _attn_fwd(qkv, name):
    _, d, L, _ = qkv.shape
    nb = L // QB

    def body(q_ref, kp_ref, kc_ref, vp_ref, vc_ref, o_ref, lse_ref):
        i = pl.program_id(1)
        mask_c, mask_p = _attn_masks(i)
        mask_p = jnp.logical_and(mask_p, i > 0)
        q, kp, kc, vp, vc = q_ref[...], kp_ref[...], kc_ref[...], vp_ref[...], vc_ref[...]
        o = jnp.zeros((QB, GW), F32)
        lse = jnp.zeros((QB, GW), F32)
        for hh in range(NSLOT):
            hm = _head_mask(hh)
            qh = jnp.where(hm, q, 0)
            s_p = jnp.where(mask_p, _dot_nt(qh, kp) * SCALE, NEG)
            s_c = jnp.where(mask_c, _dot_nt(qh, kc) * SCALE, NEG)
            m = jnp.maximum(jnp.max(s_p, axis=1, keepdims=True), jnp.max(s_c, axis=1, keepdims=True))
            p_p = jnp.exp(s_p - m)
            p_c = jnp.exp(s_c - m)
            den = jnp.sum(p_p, axis=1, keepdims=True) + jnp.sum(p_c, axis=1, keepdims=True)
            acc = _dot(p_p.astype(BF), vp) + _dot(p_c.astype(BF), vc)
            o = o + jnp.where(hm, acc / den, 0.0)
            lse = lse + jnp.where(hm, m + jnp.log(den), 0.0)
        o_ref[...] = o
        lse_ref[...] = lse

    prev = lambda i: jnp.maximum(i - 1, 0)
    sp = lambda f: pl.BlockSpec((None, None, QB, GW), f)
    out = pl.BlockSpec((None, QB, GW), lambda r, i: (r, i, 0))
    return pl.pallas_call(
        body, name=name, grid=(d, nb),
        out_shape=(jax.ShapeDtypeStruct((d, L, GW), F32), jax.ShapeDtypeStruct((d, L, GW), F32)),
        in_specs=[sp(lambda r, i: (0, r, i, 0)), sp(lambda r, i: (1, r, prev(i), 0)), sp(lambda r, i: (1, r, i, 0)),
                  sp(lambda r, i: (2, r, prev(i), 0)), sp(lambda r, i: (2, r, i, 0))],
        out_specs=(out, out), compiler_params=_cp(("parallel", "arbitrary")))(qkv, qkv, qkv, qkv, qkv)


def _attn_combine(os_, lses):
    S = os_[0].shape[1]
    tm = min(1024, S)

    def body(o0, o1, o2, l0, l1, l2, y_ref, j0, j1, j2, scr):
        os_nat = [_undilate_load(o, (), d, scr, tm) for o, d in zip((o0, o1, o2), DILS)]
        a, b, c = [_undilate_load(l, (), d, scr, tm) for l, d in zip((l0, l1, l2), DILS)]
        m = jnp.maximum(jnp.maximum(a, b), c)
        wa, wb, wc = jnp.exp(a - m), jnp.exp(b - m), jnp.exp(c - m)
        den = wa + wb + wc
        y_ref[...] = ((wa * os_nat[0] + wb * os_nat[1] + wc * os_nat[2]) / den).astype(BF)
        lse = m + jnp.log(den)
        for j_ref, d in zip((j0, j1, j2), DILS):
            _dilate_store(lse, scr, j_ref, (), d)

    dil = lambda d: pl.BlockSpec((d, tm // d, GW), lambda i: (0, i, 0))
    dshape = lambda d: jax.ShapeDtypeStruct((d, S // d, GW), F32)
    return pl.pallas_call(
        body, name="attn_combine", grid=(S // tm,),
        out_shape=(jax.ShapeDtypeStruct((S, GW), BF),) + tuple(dshape(d) for d in DILS),
        in_specs=[dil(d) for d in DILS] * 2,
        out_specs=(pl.BlockSpec((tm, GW), lambda i: (i, 0)),) + tuple(dil(d) for d in DILS),
        scratch_shapes=[pltpu.VMEM((2, tm, 128), F32)],
        compiler_params=_cp(("parallel",)))(*os_, *lses)


def _tril(upper=False):
    row = lax.broadcasted_iota(jnp.int32, (CHUNK, CHUNK), 0)
    col = lax.broadcasted_iota(jnp.int32, (CHUNK, CHUNK), 1)
    return row <= col if upper else col <= row


def _ln_fwd(z, gain, bias):
    mu = jnp.mean(z, axis=-1, keepdims=True)
    zc = z - mu
    rstd = lax.rsqrt(jnp.mean(zc * zc, axis=-1, keepdims=True) + EPS)
    zhat = zc * rstd
    return zhat, rstd, zhat * gain + bias


def _gmlp_fwd(puz, ln_g, ln_b, w_sp, b_sp_t):
    S = puz.shape[0]
    tm = min(512, S)
    nch = tm // CHUNK

    def body(p_ref, g_ref, b_ref, w_ref, bt_ref, o_ref):
        tril = _tril()
        ws = [jnp.where(tril, w_ref[gg], 0.0).astype(BF) for gg in range(4)]
        for ch in range(nch):
            rows = slice(ch * CHUNK, (ch + 1) * CHUNK)
            z = _gelu(p_ref[rows, GMLP_W:].astype(F32))
            _, _, zn = _ln_fwd(z, g_ref[...], b_ref[...])
            zn = zn.astype(BF)
            for gg in range(4):
                cols = slice(gg * CHUNK, (gg + 1) * CHUNK)
                sz = _dot(ws[gg], zn[:, cols]) + bt_ref[:, gg:gg + 1]
                u = _gelu(p_ref[rows, cols].astype(F32))
                o_ref[rows, cols] = (u * sz).astype(BF)

    return pl.pallas_call(
        body, name="gmlp_fwd", out_shape=jax.ShapeDtypeStruct((S, GMLP_W), BF), grid=(S // tm,),
        in_specs=[pl.BlockSpec((tm, 2 * GMLP_W), lambda i: (i, 0)),
                  pl.BlockSpec((1, GMLP_W), lambda i: (0, 0)), pl.BlockSpec((1, GMLP_W), lambda i: (0, 0)),
                  pl.BlockSpec((4, CHUNK, CHUNK), lambda i: (0, 0, 0)), pl.BlockSpec((CHUNK, 4), lambda i: (0, 0))],
        out_specs=pl.BlockSpec((tm, GMLP_W), lambda i: (i, 0)),
        compiler_params=_cp(("parallel",)))(puz, ln_g, ln_b, w_sp, b_sp_t)


def _merge_fwd(ya, yg, gates, wba, wbg, wout, x, g2, g3):
    S = x.shape[0]
    tm = min(512, S)

    def body(ya_ref, yg_ref, gt_ref, wba_ref, wbg_ref, wo_ref, x_ref, g2_ref, g3_ref,
             a_ref, gb_ref, mg_ref, y_ref, x2_ref, h3_ref):
        a = _dot(ya_ref[...], wba_ref[...])
        b = _dot(yg_ref[...], wbg_ref[...])
        a_ref[...] = a.astype(BF)
        gb_ref[...] = b.astype(BF)
        merged = (gt_ref[:, :D].astype(F32) * a + gt_ref[:, D:].astype(F32) * b).astype(BF)
        mg_ref[...] = merged
        y = _dot(merged, wo_ref[...])
        y_ref[...] = y
        x2 = x_ref[...] + y * _rms_stats(y) * g2_ref[...]
        x2_ref[...] = x2
        h3_ref[...] = (x2 * _rms_stats(x2) * g3_ref[...]).astype(BF)

    row = lambda w: pl.BlockSpec((tm, w), lambda i: (i, 0))
    full = lambda s: pl.BlockSpec(s, lambda i: (0, 0))
    return pl.pallas_call(
        body, name="merge_fwd", grid=(S // tm,),
        out_shape=(jax.ShapeDtypeStruct((S, D), BF), jax.ShapeDtypeStruct((S, D), BF), jax.ShapeDtypeStruct((S, D), BF),
                   jax.ShapeDtypeStruct((S, D), F32), jax.ShapeDtypeStruct((S, D), F32), jax.ShapeDtypeStruct((S, D), BF)),
        in_specs=[row(GW), row(GMLP_W), row(2 * D), full((GW, D)), full((GMLP_W, D)), full((D, D)), row(D),
                  full((1, D)), full((1, D))],
        out_specs=(row(D), row(D), row(D), row(D), row(D), row(D)),
        compiler_params=_cp(("parallel",)))(ya, yg, gates, wba, wbg, wout, x, g2, g3)


def _mlp_fwd(h3, wmi, wmo, x2, tgt, g4):
    S = x2.shape[0]
    tm = min(1024, S)
    tf = 512
    nf = DFF // tf

    def body(h_ref, wi_ref, wo_ref, x2_ref, t_ref, g4_ref, a_ref, dy2_ref, dout_ref, loss_ref, dg4_ref, acc_ref):
        i, c = pl.program_id(0), pl.program_id(1)

        @pl.when(jnp.logical_and(i == 0, c == 0))
        def _():
            loss_ref[...] = jnp.zeros_like(loss_ref)
            dg4_ref[...] = jnp.zeros_like(dg4_ref)

        a = jnp.maximum(_dot(h_ref[...], wi_ref[...]), 0.0)
        a_ref[...] = a.astype(BF)
        part = _dot((a * a).astype(BF), wo_ref[...])

        @pl.when(c == 0)
        def _():
            acc_ref[...] = part

        @pl.when(c > 0)
        def _():
            acc_ref[...] += part

        @pl.when(c == nf - 1)
        def _():
            y2 = acc_ref[...]
            r = _rms_stats(y2)
            out = x2_ref[...] + y2 * r * g4_ref[...]
            err = out - t_ref[...]
            sq = err * err
            tot = jnp.sum(jnp.sum(sq, axis=1, keepdims=True), axis=0, keepdims=True) * (0.5 / D)
            lane = lax.broadcasted_iota(jnp.int32, (1, 128), 1)
            loss_ref[...] += jnp.where(lane == 0, tot, 0.0)
            dout = err * (1.0 / D)
            dout_ref[...] = dout
            dy2, dg = _rms_bwd(y2, r, g4_ref[...], dout)
            dy2_ref[...] = dy2.astype(BF)
            dg4_ref[...] += dg

    row = pl.BlockSpec((tm, D), lambda i, c: (i, 0))
    return pl.pallas_call(
        body, name="mlp_fwd", grid=(S // tm, nf),
        out_shape=(jax.ShapeDtypeStruct((S, DFF), BF), jax.ShapeDtypeStruct((S, D), BF), jax.ShapeDtypeStruct((S, D), F32),
                   jax.ShapeDtypeStruct((1, 128), F32), jax.ShapeDtypeStruct((1, D), F32)),
        in_specs=[row, pl.BlockSpec((D, tf), lambda i, c: (0, c)), pl.BlockSpec((tf, D), lambda i, c: (c, 0)),
                  row, row, pl.BlockSpec((1, D), lambda i, c: (0, 0))],
        out_specs=(pl.BlockSpec((tm, tf), lambda i, c: (i, c)), row, row,
                   pl.BlockSpec((1, 128), lambda i, c: (0, 0)), pl.BlockSpec((1, D), lambda i, c: (0, 0))),
        scratch_shapes=[pltpu.VMEM((tm, D), F32)],
        compiler_params=_cp(("arbitrary", "arbitrary")))(h3, wmi, wmo, x2, tgt, g4)


def _mlp_bwd(dy2, a, wmo, wmi, x2, y, dout, g2, g3):
    S = x2.shape[0]
    tm = min(512, S)
    tf = 512
    nf = DFF // tf

    def body(dy2_ref, a_ref, wo_ref, wi_ref, x2_ref, y_ref, dout_ref, g2_ref, g3_ref,
             dpre_ref, dx2_ref, dy_ref, dg3_ref, dg2_ref, acc_ref):
        i, c = pl.program_id(0), pl.program_id(1)

        @pl.when(jnp.logical_and(i == 0, c == 0))
        def _():
            dg3_ref[...] = jnp.zeros_like(dg3_ref)
            dg2_ref[...] = jnp.zeros_like(dg2_ref)

        da2 = _dot_nt(dy2_ref[...], wo_ref[...])
        dpre = (2.0 * a_ref[...].astype(F32) * da2).astype(BF)
        dpre_ref[...] = dpre
        part = _dot_nt(dpre, wi_ref[...])

        @pl.when(c == 0)
        def _():
            acc_ref[...] = part

        @pl.when(c > 0)
        def _():
            acc_ref[...] += part

        @pl.when(c == nf - 1)
        def _():
            x2 = x2_ref[...]
            dx3, dg3 = _rms_bwd(x2, _rms_stats(x2), g3_ref[...], acc_ref[...])
            dx2 = dout_ref[...] + dx3
            dx2_ref[...] = dx2
            dg3_ref[...] += dg3
            yv = y_ref[...]
            dy, dg2 = _rms_bwd(yv, _rms_stats(yv), g2_ref[...], dx2)
            dy_ref[...] = dy.astype(BF)
            dg2_ref[...] += dg2

    row = pl.BlockSpec((tm, D), lambda i, c: (i, 0))
    vec = pl.BlockSpec((1, D), lambda i, c: (0, 0))
    return pl.pallas_call(
        body, name="mlp_bwd", grid=(S // tm, nf),
        out_shape=(jax.ShapeDtypeStruct((S, DFF), BF), jax.ShapeDtypeStruct((S, D), F32), jax.ShapeDtypeStruct((S, D), BF),
                   jax.ShapeDtypeStruct((1, D), F32), jax.ShapeDtypeStruct((1, D), F32)),
        in_specs=[row, pl.BlockSpec((tm, tf), lambda i, c: (i, c)), pl.BlockSpec((tf, D), lambda i, c: (c, 0)),
                  pl.BlockSpec((D, tf), lambda i, c: (0, c)), row, row, row, vec, vec],
        out_specs=(pl.BlockSpec((tm, tf), lambda i, c: (i, c)), row, row, vec, vec),
        scratch_shapes=[pltpu.VMEM((tm, D), F32)],
        compiler_params=_cp(("arbitrary", "arbitrary")))(dy2, a, wmo, wmi, x2, y, dout, g2, g3)


def _mm_tn(a, b, name, square_a=False, tm=1024, tn=1024, tk=1024):
    S, M = a.shape
    N = b.shape[1]
    tm, tk = min(tm, M), min(tk, S)
    tn = max(t for t in range(128, min(tn, N) + 1, 128) if N % t == 0)
    assert M % tm == 0 and S % tk == 0
    nk = S // tk

    def body(a_ref, b_ref, o_ref):
        k = pl.program_id(2)
        av = a_ref[...]
        if square_a:
            av = av * av
        part = _dot_tn(av, b_ref[...])

        @pl.when(k == 0)
        def _():
            o_ref[...] = part

        @pl.when(k > 0)
        def _():
            o_ref[...] += part

    return pl.pallas_call(
        body, name=name, out_shape=jax.ShapeDtypeStruct((M, N), F32), grid=(M // tm, N // tn, nk),
        in_specs=[pl.BlockSpec((tk, tm), lambda i, j, k: (k, i)), pl.BlockSpec((tk, tn), lambda i, j, k: (k, j))],
        out_specs=pl.BlockSpec((tm, tn), lambda i, j, k: (i, j)),
        compiler_params=_cp(("parallel", "parallel", "arbitrary")))(a, b)


def _outproj_bwd(dy, wout, abr, gbr, gates, wba, wbg, ya):
    S = dy.shape[0]
    tm = min(512, S)

    def body(dy_ref, wo_ref, a_ref, b_ref, gt_ref, wba_ref, wbg_ref, ya_ref,
             dgt_ref, da_ref, db_ref, dyg_ref, e0, e1, e2, s0, s1, s2, scr):
        dm = _dot_nt(dy_ref[...], wo_ref[...])
        ga, gb = gt_ref[:, :D].astype(F32), gt_ref[:, D:].astype(F32)
        dgt_ref[:, :D] = (dm * a_ref[...].astype(F32) * ga * (1.0 - ga)).astype(BF)
        dgt_ref[:, D:] = (dm * b_ref[...].astype(F32) * gb * (1.0 - gb)).astype(BF)
        da = (dm * ga).astype(BF)
        db = (dm * gb).astype(BF)
        da_ref[...] = da
        db_ref[...] = db
        dyg_ref[...] = _dot_nt(db, wbg_ref[...]).astype(BF)
        dya = _dot_nt(da, wba_ref[...]).astype(BF).astype(F32)
        dyy = dya * ya_ref[...].astype(F32)
        dsum = jnp.zeros((tm, GW), F32)
        for hh in range(NSLOT):
            hm = _head_mask(hh)
            dsum = dsum + jnp.where(hm, jnp.sum(jnp.where(hm, dyy, 0.0), axis=1, keepdims=True), 0.0)
        for e_ref, s_ref, d in zip((e0, e1, e2), (s0, s1, s2), DILS):
            _dilate_store(dya, scr, e_ref, (), d)
            _dilate_store(dsum, scr, s_ref, (), d)

    row = lambda w: pl.BlockSpec((tm, w), lambda i: (i, 0))
    full = lambda s: pl.BlockSpec(s, lambda i: (0, 0))
    dil = lambda d: pl.BlockSpec((d, tm // d, GW), lambda i: (0, i, 0))
    dshape = lambda d, t: jax.ShapeDtypeStruct((d, S // d, GW), t)
    return pl.pallas_call(
        body, name="outproj_bwd", grid=(S // tm,),
        out_shape=(jax.ShapeDtypeStruct((S, 2 * D), BF), jax.ShapeDtypeStruct((S, D), BF), jax.ShapeDtypeStruct((S, D), BF),
                   jax.ShapeDtypeStruct((S, GMLP_W), BF)) + tuple(dshape(d, BF) for d in DILS)
        + tuple(dshape(d, F32) for d in DILS),
        in_specs=[row(D), full((D, D)), row(D), row(D), row(2 * D), full((GW, D)), full((GMLP_W, D)), row(GW)],
        out_specs=(row(2 * D), row(D), row(D), row(GMLP_W)) + tuple(dil(d) for d in DILS) * 2,
        scratch_shapes=[pltpu.VMEM((2, tm, 128), F32)],
        compiler_params=_cp(("parallel",)))(dy, wout, abr, gbr, gates, wba, wbg, ya)


def _gmlp_bwd(puz, dyg, ln_g, ln_b, w_sp, w_sp_t, b_sp_t):
    S = puz.shape[0]
    tm = min(512, S)
    nch = tm // CHUNK

    def body(p_ref, dy_ref, g_ref, b_ref, w_ref, wt_ref, bt_ref,
             dp_ref, dw_ref, dbs_ref, dg_ref, dbias_ref, dbacc_ref):
        i = pl.program_id(0)

        @pl.when(i == 0)
        def _():
            dw_ref[...] = jnp.zeros_like(dw_ref)
            dbacc_ref[...] = jnp.zeros_like(dbacc_ref)
            dg_ref[...] = jnp.zeros_like(dg_ref)
            dbias_ref[...] = jnp.zeros_like(dbias_ref)

        tril = _tril()
        ws = [jnp.where(tril, w_ref[gg], 0.0).astype(BF) for gg in range(4)]
        triu = _tril(upper=True)
        wts = [jnp.where(triu, wt_ref[gg], 0.0).astype(BF) for gg in range(4)]
        gain = g_ref[...]
        for ch in range(nch):
            rows = slice(ch * CHUNK, (ch + 1) * CHUNK)
            pz = p_ref[rows, GMLP_W:].astype(F32)
            z = _gelu(pz)
            zhat, rstd, zn = _ln_fwd(z, gain, b_ref[...])
            znb = zn.astype(BF)
            dzn_parts = []
            for gg in range(4):
                cols = slice(gg * CHUNK, (gg + 1) * CHUNK)
                pu = p_ref[rows, cols].astype(F32)
                u = _gelu(pu)
                sz = _dot(ws[gg], znb[:, cols]) + bt_ref[:, gg:gg + 1]
                dyv = dy_ref[rows, cols].astype(F32)
                dp_ref[rows, cols] = (dyv * sz * _gelu_grad(pu)).astype(BF)
                dsz = dyv * u
                dbacc_ref[gg] += dsz
                dszb = dsz.astype(BF)
                dw_ref[gg] += _dot_nt(dszb, znb[:, cols])
                dzn_parts.append(_dot(wts[gg], dszb))
            dzn = jnp.concatenate(dzn_parts, axis=1)
            dg_ref[...] += jnp.sum(dzn * zhat, axis=0, keepdims=True)
            dbias_ref[...] += jnp.sum(dzn, axis=0, keepdims=True)
            dzh = dzn * gain
            dz = rstd * (dzh - jnp.mean(dzh, axis=-1, keepdims=True)
                         - zhat * jnp.mean(dzh * zhat, axis=-1, keepdims=True))
            dp_ref[rows, GMLP_W:] = (dz * _gelu_grad(pz)).astype(BF)

        @pl.when(i == pl.num_programs(0) - 1)
        def _():
            for gg in range(4):
                dw_ref[gg] = jnp.where(tril, dw_ref[gg], 0.0)
                dbs_ref[gg] = jnp.sum(dbacc_ref[gg], axis=1, keepdims=True)

    full2 = lambda s: pl.BlockSpec(s, lambda i: (0, 0))
    full3 = lambda s: pl.BlockSpec(s, lambda i: (0, 0, 0))
    return pl.pallas_call(
        body, name="gmlp_bwd", grid=(S // tm,),
        out_shape=(jax.ShapeDtypeStruct((S, 2 * GMLP_W), BF), jax.ShapeDtypeStruct((4, CHUNK, CHUNK), F32),
                   jax.ShapeDtypeStruct((4, CHUNK, 1), F32), jax.ShapeDtypeStruct((1, GMLP_W), F32),
                   jax.ShapeDtypeStruct((1, GMLP_W), F32)),
        in_specs=[pl.BlockSpec((tm, 2 * GMLP_W), lambda i: (i, 0)), pl.BlockSpec((tm, GMLP_W), lambda i: (i, 0)),
                  full2((1, GMLP_W)), full2((1, GMLP_W)), full3((4, CHUNK, CHUNK)), full3((4, CHUNK, CHUNK)),
                  full2((CHUNK, 4))],
        out_specs=(pl.BlockSpec((tm, 2 * GMLP_W), lambda i: (i, 0)), full3((4, CHUNK, CHUNK)), full3((4, CHUNK, 1)),
                   full2((1, GMLP_W)), full2((1, GMLP_W))),
        scratch_shapes=[pltpu.VMEM((4, CHUNK, CHUNK), F32)],
        compiler_params=_cp(("arbitrary",)))(puz, dyg, ln_g, ln_b, w_sp, w_sp_t, b_sp_t)


def _attn_bwd(qkv, dya, dsums, lse, name):
    _, d, L, _ = qkv.shape
    nb = L // QB

    def body(q_ref, qn_ref, kp_ref, kc_ref, vp_ref, vc_ref, dy_ref, dyn_ref, e_ref, en_ref, l_ref, ln_ref, o_ref):
        i = pl.program_id(1)
        mask_c, mask_p = _attn_masks(i)
        mask_prev = jnp.logical_and(mask_p, i > 0)
        mask_next = jnp.logical_and(mask_p, i < nb - 1)
        kp, kc, vp, vc = kp_ref[...], kc_ref[...], vp_ref[...], vc_ref[...]
        q_all, qn_all, dy_all, dyn_all = q_ref[...], qn_ref[...], dy_ref[...], dyn_ref[...]
        dyy, dyy_n = e_ref[...], en_ref[...]
        lse, lse_n = l_ref[...], ln_ref[...]
        dq = jnp.zeros((QB, GW), F32)
        dk = jnp.zeros((QB, GW), F32)
        dv = jnp.zeros((QB, GW), F32)
        for hh in range(NSLOT):
            hm = _head_mask(hh)
            q, qn = jnp.where(hm, q_all, 0), jnp.where(hm, qn_all, 0)
            dy, dyn = jnp.where(hm, dy_all, 0), jnp.where(hm, dyn_all, 0)
            dsum = jnp.max(jnp.where(hm, dyy, -jnp.inf), axis=1, keepdims=True)
            dsum_n = jnp.max(jnp.where(hm, dyy_n, -jnp.inf), axis=1, keepdims=True)
            lrow = jnp.max(jnp.where(hm, lse, -jnp.inf), axis=1, keepdims=True)
            lrow_n = jnp.max(jnp.where(hm, lse_n, -jnp.inf), axis=1, keepdims=True)
            p_a = jnp.where(mask_c, jnp.exp(_dot_nt(q, kc) * SCALE - lrow), 0.0)
            ds_a = (p_a * (_dot_nt(dy, vc) - dsum)).astype(BF)
            p_b = jnp.where(mask_prev, jnp.exp(_dot_nt(q, kp) * SCALE - lrow), 0.0)
            ds_b = (p_b * (_dot_nt(dy, vp) - dsum)).astype(BF)
            p_n = jnp.where(mask_next, jnp.exp(_dot_nt(qn, kc) * SCALE - lrow_n), 0.0)
            ds_n = (p_n * (_dot_nt(dyn, vc) - dsum_n)).astype(BF)
            dq = dq + jnp.where(hm, _dot(ds_a, kc) + _dot(ds_b, kp), 0.0)
            dk = dk + _dot_tn(ds_a, q) + _dot_tn(ds_n, qn)
            dv = dv + _dot_tn(p_a.astype(BF), dy) + _dot_tn(p_n.astype(BF), dyn)
        o_ref[0] = (dq * SCALE).astype(BF)
        o_ref[1] = (dk * SCALE).astype(BF)
        o_ref[2] = dv.astype(BF)

    prev = lambda i: jnp.maximum(i - 1, 0)
    nxt = lambda i: jnp.minimum(i + 1, nb - 1)
    sp4 = lambda f: pl.BlockSpec((None, None, QB, GW), f)
    sp3 = lambda f: pl.BlockSpec((None, QB, GW), f)
    cur3, nxt3 = sp3(lambda r, i: (r, i, 0)), sp3(lambda r, i: (r, nxt(i), 0))
    return pl.pallas_call(
        body, name=name, grid=(d, nb), out_shape=jax.ShapeDtypeStruct((3, d, L, GW), BF),
        in_specs=[sp4(lambda r, i: (0, r, i, 0)), sp4(lambda r, i: (0, r, nxt(i), 0)),
                  sp4(lambda r, i: (1, r, prev(i), 0)), sp4(lambda r, i: (1, r, i, 0)),
                  sp4(lambda r, i: (2, r, prev(i), 0)), sp4(lambda r, i: (2, r, i, 0)),
                  cur3, nxt3, cur3, nxt3, cur3, nxt3],
        out_specs=pl.BlockSpec((3, None, QB, GW), lambda r, i: (0, r, i, 0)),
        compiler_params=_cp(("parallel", "arbitrary")))(qkv, qkv, qkv, qkv, qkv, qkv, dya, dya, dsums, dsums, lse, lse)


def _inproj_bwd(dqkvs, dpuz, dgates, wqkv, wuz, wg, cos_t, sin_t, x, dx2, g1):
    S = x.shape[0]
    tm = min(512, S)

    def body(d0_ref, d1_ref, d2_ref, dp_ref, dg_ref, wqkv_ref, wuz_ref, wg_ref, c_ref, s_ref,
             x_ref, dx2_ref, g1_ref, gx_ref, dg1_ref, dn_ref, scr):
        i = pl.program_id(0)

        @pl.when(i == 0)
        def _():
            dg1_ref[...] = jnp.zeros_like(dg1_ref)

        for t in range(3):
            for g, (d_ref, d) in enumerate(zip((d0_ref, d1_ref, d2_ref), DILS)):
                piece = _undilate_load(d_ref, (t,), d, scr, tm)
                if t < 2:
                    piece = _rope_bwd(piece, c_ref, s_ref)
                dn_ref[:, (3 * t + g) * GW:(3 * t + g + 1) * GW] = piece.astype(BF)
        dh = (_dot_nt(dn_ref[...], wqkv_ref[...]) + _dot_nt(dp_ref[...], wuz_ref[...])
              + _dot_nt(dg_ref[...], wg_ref[...]))
        xv = x_ref[...]
        dx1, dg1 = _rms_bwd(xv, _rms_stats(xv), g1_ref[...], dh)
        gx_ref[...] = dx2_ref[...] + dx1
        dg1_ref[...] += dg1

    row = lambda w: pl.BlockSpec((tm, w), lambda i: (i, 0))
    full = lambda s: pl.BlockSpec(s, lambda i: (0, 0))
    dil = lambda d: pl.BlockSpec((3, d, tm // d, GW), lambda i: (0, 0, i, 0))
    return pl.pallas_call(
        body, name="inproj_bwd", grid=(S // tm,),
        out_shape=(jax.ShapeDtypeStruct((S, D), F32), jax.ShapeDtypeStruct((1, D), F32),
                   jax.ShapeDtypeStruct((S, 3 * AW), BF)),
        in_specs=[dil(d) for d in DILS] + [row(2 * GMLP_W), row(2 * D), full((D, 3 * AW)), full((D, 2 * GMLP_W)),
                                            full((D, 2 * D)), row(128), row(128), row(D), row(D), full((1, D))],
        out_specs=(row(D), full((1, D)), row(3 * AW)),
        scratch_shapes=[pltpu.VMEM((2, tm, 128), F32)],
        compiler_params=_cp(("arbitrary",)))(*dqkvs, dpuz, dgates, wqkv, wuz, wg, cos_t, sin_t, x, dx2, g1)


def _adam_math(w, g, m, v):
    m2 = ADAM_B1 * m + (1.0 - ADAM_B1) * g
    v2 = ADAM_B2 * v + (1.0 - ADAM_B2) * (g * g)
    m_hat = m2 / (1.0 - ADAM_B1 ** ADAM_STEP)
    v_hat = v2 / (1.0 - ADAM_B2 ** ADAM_STEP)
    delta = -ADAM_LR * (m_hat / (jnp.sqrt(v_hat) + ADAM_EPS) + ADAM_WD * w)
    return delta, m2, v2


def _adam_shard(own, recv, w, m, v, name):
    R, C = w.shape
    tr = min(256, R)

    def body(own_ref, r_ref, w_ref, m_ref, v_ref, g_ref, d_ref, m2_ref, v2_ref):
        g = own_ref[...] + r_ref[0].astype(F32) + r_ref[1].astype(F32) + r_ref[2].astype(F32)
        g_ref[...] = g
        d_ref[...], m2_ref[...], v2_ref[...] = _adam_math(w_ref[...], g, m_ref[...], v_ref[...])

    spec = pl.BlockSpec((tr, C), lambda i: (i, 0))
    out = jax.ShapeDtypeStruct((R, C), F32)
    return pl.pallas_call(
        body, name=name, grid=(R // tr,), out_shape=(out, out, out, out),
        in_specs=[spec, pl.BlockSpec((3, tr, C), lambda i: (0, i, 0)), spec, spec, spec],
        out_specs=(spec, spec, spec, spec), compiler_params=_cp(("parallel",)))(own, recv, w, m, v)


def _adam_small(gp, wp, mp, vp):
    def body(g_ref, w_ref, m_ref, v_ref, d_ref, m2_ref, v2_ref):
        d_ref[...], m2_ref[...], v2_ref[...] = _adam_math(w_ref[...], g_ref[...], m_ref[...], v_ref[...])

    out = jax.ShapeDtypeStruct(gp.shape, F32)
    return pl.pallas_call(body, name="adam_small", out_shape=(out, out, out),
                          compiler_params=pltpu.CompilerParams(vmem_limit_bytes=VMEM_LIMIT))(gp, wp, mp, vp)


def _rs_add(gblocks, recv, idx, name):
    _, R, C = gblocks.shape
    tr = min(256, R)

    def body(t_ref, g_ref, r_ref, own_ref, send_ref):
        j = pl.program_id(1)
        s = g_ref[...] + r_ref[...]

        @pl.when(j == 0)
        def _():
            own_ref[...] = s

        @pl.when(j > 0)
        def _():
            send_ref[...] = s.astype(BF)

    grid_spec = pltpu.PrefetchScalarGridSpec(
        num_scalar_prefetch=1, grid=(R // tr, 4),
        in_specs=[pl.BlockSpec((None, tr, C), lambda i, j, t: (t[j], i, 0)),
                  pl.BlockSpec((None, tr, C), lambda i, j, t: (t[4 + j], i, 0))],
        out_specs=[pl.BlockSpec((tr, C), lambda i, j, t: (i, 0)),
                   pl.BlockSpec((None, tr, C), lambda i, j, t: (jnp.maximum(j - 1, 0), i, 0))])
    return pl.pallas_call(
        body, name=name, grid_spec=grid_spec,
        out_shape=(jax.ShapeDtypeStruct((R, C), F32), jax.ShapeDtypeStruct((3, R, C), BF)),
        compiler_params=_cp(("parallel", "arbitrary")))(idx, gblocks, recv)


def _mesh_pos():
    return lax.axis_index("x"), lax.axis_index("y"), lax.axis_index("c")


def _all_gather_weights(shards):
    n = len(shards)

    def body(*refs):
        ins, outs = refs[:n], refs[n:2 * n]
        send_sems, recv_sems, loc_sems = refs[2 * n:]
        x, y, c = _mesh_pos()
        me, sib = (x, y, c), (x, y, 1 - c)
        chips = [(1 - x, y), (x, 1 - y), (1 - x, 1 - y)]

        def rcopy(k, s, block, to, src=None):
            dst = outs[k].at[4 * block[0] + 2 * block[1] + block[2]]
            return pltpu.make_async_remote_copy(
                src_ref=dst if src is None else src, dst_ref=dst, send_sem=send_sems.at[k, s],
                recv_sem=recv_sems.at[k, s], device_id=to, device_id_type=MESH)

        locs = [pltpu.make_async_copy(ins[k], outs[k].at[4 * x + 2 * y + c], loc_sems.at[k]) for k in range(n)]
        for cp in locs:
            cp.start()
        started = []
        for k in range(n):
            for j, chip in enumerate(chips):
                started.append(rcopy(k, 1 + j, me, (*chip, c), src=ins[k]))
                started[-1].start()
        for k in range(n):
            started.append(rcopy(k, 0, me, sib, src=ins[k]))
            started[-1].start()
        for k in range(n):
            for j, chip in enumerate(chips):
                rcopy(k, 1 + j, (*chip, c), me).wait_recv()
                started.append(rcopy(k, 4 + j, (*chip, c), sib))
                started[-1].start()
        for k in range(n):
            rcopy(k, 0, sib, me).wait_recv()
            for j, chip in enumerate(chips):
                rcopy(k, 4 + j, (*chip, 1 - c), me).wait_recv()
        for cp in started:
            cp.wait_send()
        for cp in locs:
            cp.wait()

    hbm = pl.BlockSpec(memory_space=pl.ANY)
    return pl.pallas_call(
        body, name="ag_weights",
        out_shape=tuple(jax.ShapeDtypeStruct((8,) + s.shape, s.dtype) for s in shards),
        in_specs=[hbm] * n, out_specs=(hbm,) * n,
        scratch_shapes=[pltpu.SemaphoreType.DMA((n, 7)), pltpu.SemaphoreType.DMA((n, 7)),
                        pltpu.SemaphoreType.DMA((n,))])(*shards)


def _rs_d2d(gblocks):
    n = len(gblocks)

    def body(*refs):
        ins, outs = refs[:n], refs[n:2 * n]
        send_sems, recv_sems = refs[2 * n:]
        x, y, c = _mesh_pos()
        copies = []
        for k in range(n):
            for kk in range(4):
                copies.append(pltpu.make_async_remote_copy(
                    src_ref=ins[k].at[2 * kk + 1 - c], dst_ref=outs[k].at[kk], send_sem=send_sems.at[k, kk],
                    recv_sem=recv_sems.at[k, kk], device_id=(x, y, 1 - c), device_id_type=MESH))
                copies[-1].start()
        for cp in copies:
            cp.wait()

    hbm = pl.BlockSpec(memory_space=pl.ANY)
    return pl.pallas_call(
        body, name="rs_d2d",
        out_shape=tuple(jax.ShapeDtypeStruct((4,) + g.shape[1:], g.dtype) for g in gblocks),
        in_specs=[hbm] * n, out_specs=(hbm,) * n,
        scratch_shapes=[pltpu.SemaphoreType.DMA((n, 4)), pltpu.SemaphoreType.DMA((n, 4))])(*gblocks)


def _rs_ici(sends):
    n = len(sends)

    def body(*refs):
        ins, outs = refs[:n], refs[n:2 * n]
        send_sems, recv_sems = refs[2 * n:]
        x, y, c = _mesh_pos()
        chips = [(1 - x, y), (x, 1 - y), (1 - x, 1 - y)]
        copies = []
        for k in range(n):
            for j, chip in enumerate(chips):
                copies.append(pltpu.make_async_remote_copy(
                    src_ref=ins[k].at[j], dst_ref=outs[k].at[j], send_sem=send_sems.at[k, j],
                    recv_sem=recv_sems.at[k, j], device_id=(*chip, c), device_id_type=MESH))
                copies[-1].start()
        for cp in copies:
            cp.wait()

    hbm = pl.BlockSpec(memory_space=pl.ANY)
    return pl.pallas_call(
        body, name="rs_ici",
        out_shape=tuple(jax.ShapeDtypeStruct(s.shape, s.dtype) for s in sends),
        in_specs=[hbm] * n, out_specs=(hbm,) * n,
        scratch_shapes=[pltpu.SemaphoreType.DMA((n, 3)), pltpu.SemaphoreType.DMA((n, 3))])(*sends)


def _small_allreduce(pkt):
    R = pkt.shape[0]

    def body(p_ref, o_ref, buf, send_sems, recv_sems):
        x, y, c = _mesh_pos()
        me, sib = (x, y, c), (x, y, 1 - c)
        chips = [(1 - x, y), (x, 1 - y), (1 - x, 1 - y)]

        def rcopy(s, block, to, src=None):
            dst = buf.at[4 * block[0] + 2 * block[1] + block[2]]
            return pltpu.make_async_remote_copy(
                src_ref=dst if src is None else src, dst_ref=dst, send_sem=send_sems.at[s],
                recv_sem=recv_sems.at[s], device_id=to, device_id_type=MESH)

        buf[4 * x + 2 * y + c] = p_ref[...]
        started = [rcopy(0, me, sib, src=p_ref)]
        started += [rcopy(1 + j, me, (*chip, c), src=p_ref) for j, chip in enumerate(chips)]
        for cp in started:
            cp.start()
        for j, chip in enumerate(chips):
            rcopy(1 + j, (*chip, c), me).wait_recv()
            started.append(rcopy(4 + j, (*chip, c), sib))
            started[-1].start()
        rcopy(0, sib, me).wait_recv()
        for j, chip in enumerate(chips):
            rcopy(4 + j, (*chip, 1 - c), me).wait_recv()
        for cp in started:
            cp.wait_send()
        acc = buf[0]
        for b in range(1, 8):
            acc = acc + buf[b]
        o_ref[...] = acc

    return pl.pallas_call(
        body, name="small_allreduce", out_shape=jax.ShapeDtypeStruct(pkt.shape, F32),
        in_specs=[pl.BlockSpec(memory_space=pltpu.VMEM)], out_specs=pl.BlockSpec(memory_space=pltpu.VMEM),
        scratch_shapes=[pltpu.VMEM((8, R, 128), F32), pltpu.SemaphoreType.DMA((7,)), pltpu.SemaphoreType.DMA((7,))],
        compiler_params=pltpu.CompilerParams(vmem_limit_bytes=VMEM_LIMIT))(pkt)


def _rope_tables(S):
    half = HD // 2
    inv_freq = ROPE_THETA ** (-jnp.arange(half, dtype=F32) / half)
    ang = jnp.arange(S, dtype=F32)[:, None] * inv_freq[None, :]
    cos, sin = jnp.cos(ang), jnp.sin(ang)
    return jnp.concatenate([cos, cos, cos, cos], axis=1), jnp.concatenate([-sin, sin, -sin, sin], axis=1)


def _local_step(x, tgt, wqkv, wuz, wg, wba, wbg, wout, wmi, wmo, g1, g2, g3, g4, ln_g, ln_b, w_sp, b_sp):
    S = x.shape[0]
    cos_t, sin_t = _rope_tables(S)
    b_sp_t = b_sp.T
    w_sp_t = w_sp.transpose(0, 2, 1)

    h1 = _rms_fwd(x, g1, "rms_pre_mix")
    qkvs = [_proj_qkv(h1, wqkv, cos_t, sin_t, g, d, "proj_qkv_d%d" % d) for g, d in enumerate(DILS)]
    puz = _proj(h1, wuz, "none", "proj_uz")
    gates = _proj(h1, wg, "sigmoid", "proj_gates")
    fwd = [_attn_fwd(t, "attn_fwd_d%d" % d) for t, d in zip(qkvs, DILS)]
    ya, *lses = _attn_combine([o for o, _ in fwd], [l for _, l in fwd])
    yg = _gmlp_fwd(puz, ln_g, ln_b, w_sp, b_sp_t)
    abr, gbr, merged, y, x2, h3 = _merge_fwd(ya, yg, gates, wba, wbg, wout, x, g2, g3)
    a, dy2, dout, loss_row, dg4 = _mlp_fwd(h3, wmi, wmo, x2, tgt, g4)

    dpre, dx2, dy, dg3, dg2 = _mlp_bwd(dy2, a, wmo, wmi, x2, y, dout, g2, g3)
    dwmo = _mm_tn(a, dy2, "dw_mlp_out", square_a=True)
    dwmi = _mm_tn(h3, dpre, "dw_mlp_in")
    dgates, da, db, dyg, *rest = _outproj_bwd(dy, wout, abr, gbr, gates, wba, wbg, ya)
    dyas, dsums = rest[:3], rest[3:]
    dwout = _mm_tn(merged, dy, "dw_out")
    dwba = _mm_tn(ya, da, "dw_branch_attn")
    dwbg = _mm_tn(yg, db, "dw_branch_gmlp")
    dpuz, dwsp, dbs, dlng, dlnb = _gmlp_bwd(puz, dyg, ln_g, ln_b, w_sp, w_sp_t, b_sp_t)
    dqkvs = [_attn_bwd(qkvs[g], dyas[g], dsums[g], lses[g], "attn_bwd_d%d" % d) for g, d in enumerate(DILS)]
    grad_x, dg1, dqkv = _inproj_bwd(dqkvs, dpuz, dgates, wqkv, wuz, wg, cos_t, sin_t, x, dx2, g1)
    dwin = jnp.concatenate([_mm_tn(h1, t, "dw_in_%d" % n) for n, t in enumerate((dqkv, dpuz, dgates))], axis=1)
    small = dict(g1=dg1, g2=dg2, g3=dg3, g4=dg4, ln_g=dlng, ln_b=dlnb, b_sp=dbs.reshape(4, CHUNK), w_sp=dwsp)
    return loss_row, grad_x, (dwin, dwba, dwbg, dwout, dwmi, dwmo), small


def _pack_small(first_row, t):
    pad = lambda a, rows: jnp.concatenate([a, jnp.zeros((rows - a.shape[0], 128), F32)], axis=0)
    return jnp.concatenate([
        pad(first_row, 8), t["g1"].reshape(8, 128), t["g2"].reshape(8, 128), t["g3"].reshape(8, 128),
        t["g4"].reshape(8, 128), t["ln_g"].reshape(4, 128), t["ln_b"].reshape(4, 128),
        pad(t["b_sp"].reshape(4, 128), 8), t["w_sp"].reshape(4 * CHUNK, 128)], axis=0)


def _unpack_small(p):
    vec = lambda r, n: p[r:r + n // 128].reshape(1, n)
    return dict(g1=vec(PK_G1, D), g2=vec(PK_G2, D), g3=vec(PK_G3, D), g4=vec(PK_G4, D),
                ln_g=vec(PK_LNG, GMLP_W), ln_b=vec(PK_LNB, GMLP_W),
                b_sp=p[PK_BSP:PK_BSP + 4].reshape(1, 4, CHUNK), w_sp=p[PK_WSP:PKT_ROWS].reshape(1, 4, CHUNK, CHUNK))


def kernel(x, norm_pre_mix, w_in, w_spatial, b_spatial, ln_v_gain, ln_v_bias, w_branch_attn, w_branch_gmlp, w_out, norm_post_mix, norm_pre_mlp, w_mlp_in, w_mlp_out, norm_post_mlp, loss_target, m_norm_pre_mix, m_w_in, m_w_spatial, m_b_spatial, m_ln_v_gain, m_ln_v_bias, m_w_branch_attn, m_w_branch_gmlp, m_w_out, m_norm_post_mix, m_norm_pre_mlp, m_w_mlp_in, m_w_mlp_out, m_norm_post_mlp, v_norm_pre_mix, v_w_in, v_w_spatial, v_b_spatial, v_ln_v_gain, v_ln_v_bias, v_w_branch_attn, v_w_branch_gmlp, v_w_out, v_norm_post_mix, v_norm_pre_mlp, v_w_mlp_in, v_w_mlp_out, v_norm_post_mlp):
    mx, my, mc = _mesh_pos()

    big_w = [w_in[0], w_branch_attn[0], w_branch_gmlp[0], w_out[0], w_mlp_in[0], w_mlp_out[0]]
    big_m = [m_w_in[0], m_w_branch_attn[0], m_w_branch_gmlp[0], m_w_out[0], m_w_mlp_in[0], m_w_mlp_out[0]]
    big_v = [v_w_in[0], v_w_branch_attn[0], v_w_branch_gmlp[0], v_w_out[0], v_w_mlp_in[0], v_w_mlp_out[0]]
    col_sharded = [True, True, True, False, True, False]
    gathered = _all_gather_weights([w.astype(BF) for w in big_w])
    full = [t.transpose(1, 0, 2).reshape(t.shape[1], 8 * t.shape[2]) if cs else t.reshape(8 * t.shape[1], t.shape[2])
            for t, cs in zip(gathered, col_sharded)]
    win, wba, wbg, wout, wmi, wmo = full
    wqkv, wuz, wg = win[:, :3 * AW], win[:, 3 * AW:3 * AW + 2 * GMLP_W], win[:, 3 * AW + 2 * GMLP_W:]

    loss_row, grad_x, dws, small = _local_step(
        x[0], loss_target[0], wqkv, wuz, wg, wba, wbg, wout, wmi, wmo,
        norm_pre_mix, norm_post_mix, norm_pre_mlp, norm_post_mlp, ln_v_gain, ln_v_bias, w_spatial[0], b_spatial[0])

    gblocks = [g.reshape(g.shape[0], 8, g.shape[1] // 8).transpose(1, 0, 2) if cs
               else g.reshape(8, g.shape[0] // 8, g.shape[1]) for g, cs in zip(dws, col_sharded)]
    from_sib = _rs_d2d(gblocks)
    rel = [(0, 0), (1, 0), (0, 1), (1, 1)]
    chip_of = [2 * (mx ^ dx) + (my ^ dy) for dx, dy in rel]
    idx = jnp.stack([2 * k + mc for k in chip_of] + chip_of).astype(jnp.int32)
    halves = [_rs_add(g, r, idx, "rs_add_%d" % n) for n, (g, r) in enumerate(zip(gblocks, from_sib))]
    from_chips = _rs_ici([s for _, s in halves])
    names = ["w_in", "w_branch_attn", "w_branch_gmlp", "w_out", "w_mlp_in", "w_mlp_out"]
    upd = {nm: _adam_shard(own, r, w, m, v, "adam_" + nm)
           for nm, (own, _), r, w, m, v in zip(names, halves, from_chips, big_w, big_m, big_v)}

    pkt = _small_allreduce(_pack_small(loss_row, small))
    small_w = dict(g1=norm_pre_mix, g2=norm_post_mix, g3=norm_pre_mlp, g4=norm_post_mlp, ln_g=ln_v_gain,
                   ln_b=ln_v_bias, b_sp=b_spatial[0], w_sp=w_spatial[0])
    small_m = dict(g1=m_norm_pre_mix, g2=m_norm_post_mix, g3=m_norm_pre_mlp, g4=m_norm_post_mlp, ln_g=m_ln_v_gain,
                   ln_b=m_ln_v_bias, b_sp=m_b_spatial[0], w_sp=m_w_spatial[0])
    small_v = dict(g1=v_norm_pre_mix, g2=v_norm_post_mix, g3=v_norm_pre_mlp, g4=v_norm_post_mlp, ln_g=v_ln_v_gain,
                   ln_b=v_ln_v_bias, b_sp=v_b_spatial[0], w_sp=v_w_spatial[0])
    zero_row = jnp.zeros((1, 128), F32)
    ones_row = jnp.ones((1, 128), F32)
    sd, sm, sv = _adam_small(pkt, _pack_small(zero_row, small_w), _pack_small(zero_row, small_m),
                             _pack_small(ones_row, small_v))
    loss = pkt[0, 0]
    sg, sd, sm, sv = (_unpack_small(p) for p in (pkt, sd, sm, sv))

    order = ["g1", "w_in", "w_sp", "b_sp", "ln_g", "ln_b", "w_branch_attn", "w_branch_gmlp", "w_out", "g2", "g3",
             "w_mlp_in", "w_mlp_out", "g4"]

    def pick(which):
        out = []
        for nm in order:
            if nm in upd:
                out.append(upd[nm][which][None])
            else:
                out.append((sg, sd, sm, sv)[which][nm])
        return out

    return (loss, grad_x[None], *pick(0), *pick(1), *pick(2), *pick(3))
```

```python
import functools
import math

import jax
import jax.numpy as jnp
from jax import lax
from jax.experimental import pallas as pl
from jax.experimental.pallas import tpu as pltpu

D = 1024
HD = 64
NSLOT = 4
GW = NSLOT * HD
DILS = (1, 4, 16)
QB = 128
ATTN_NSUB = 2
AW = 3 * GW
GMLP_W = 512
CHUNK = 128
DFF = 4096
EPS = 1e-6
ROPE_THETA = 10000.0
SCALE = HD ** -0.5
NEG = -1e30

ADAM_LR = 0.001
ADAM_B1 = 0.9
ADAM_B2 = 0.999
ADAM_EPS = 1e-08
ADAM_WD = 0.01
ADAM_STEP = 10

BF = jnp.bfloat16
F32 = jnp.float32
MESH = pl.DeviceIdType.MESH
VMEM_LIMIT = 56 * 1024 * 1024

PK_LOSS, PK_G1, PK_G2, PK_G3, PK_G4, PK_LNG, PK_LNB, PK_BSP, PK_WSP, PKT_ROWS = 0, 8, 16, 24, 32, 40, 44, 48, 56, 568


def _cp(sem):
    return pltpu.CompilerParams(dimension_semantics=sem, vmem_limit_bytes=VMEM_LIMIT)


def _dot(a, b):
    return jnp.dot(a, b, preferred_element_type=F32)


def _dot_nt(a, b):
    return lax.dot_general(a, b, (((1,), (1,)), ((), ())), preferred_element_type=F32)


def _dot_tn(a, b):
    return lax.dot_general(a, b, (((0,), (0,)), ((), ())), preferred_element_type=F32)


def _gelu(x):
    return jax.nn.gelu(x, approximate=True)


def _gelu_grad(x):
    k = math.sqrt(2.0 / math.pi)
    t = jnp.tanh(k * (x + 0.044715 * x * x * x))
    return 0.5 * (1.0 + t) + 0.5 * x * (1.0 - t * t) * (k * (1.0 + 3.0 * 0.044715 * x * x))


def _swap_halves(t):
    w = t.shape[1]
    lane = lax.broadcasted_iota(jnp.int32, t.shape, 1)
    first = (lane & (HD - 1)) < (HD // 2)
    return jnp.where(first, pltpu.roll(t, w - HD // 2, 1), pltpu.roll(t, HD // 2, 1))


def _head_mask(hh):
    lane = lax.broadcasted_iota(jnp.int32, (1, GW), 1)
    return jnp.logical_and(lane >= hh * HD, lane < (hh + 1) * HD)


def _rms_stats(xf):
    return lax.rsqrt(jnp.mean(xf * xf, axis=-1, keepdims=True) + EPS)


def _rms_bwd(xf, r, gain, dout):
    n = xf * r
    t = dout * gain
    dx = r * (t - n * jnp.mean(t * n, axis=-1, keepdims=True))
    return dx, jnp.sum(dout * n, axis=0, keepdims=True)


def _rms_fwd(x, gain, name):
    S = x.shape[0]
    tm = min(512, S)

    def body(x_ref, g_ref, h_ref):
        xf = x_ref[...]
        h_ref[...] = (xf * _rms_stats(xf) * g_ref[...]).astype(BF)

    return pl.pallas_call(
        body, name=name, out_shape=jax.ShapeDtypeStruct((S, D), BF), grid=(S // tm,),
        in_specs=[pl.BlockSpec((tm, D), lambda i: (i, 0)), pl.BlockSpec((1, D), lambda i: (0, 0))],
        out_specs=pl.BlockSpec((tm, D), lambda i: (i, 0)), compiler_params=_cp(("parallel",)))(x, gain)


def _proj(h, w, epi, name, tn=256):
    S, K = h.shape
    N = w.shape[1]
    tm = min(2048, S)

    def body(h_ref, w_ref, o_ref):
        y = _dot(h_ref[...], w_ref[...])
        if epi == "sigmoid":
            y = jax.nn.sigmoid(y)
        o_ref[...] = y.astype(BF)

    return pl.pallas_call(
        body, name=name, out_shape=jax.ShapeDtypeStruct((S, N), BF), grid=(S // tm, N // tn),
        in_specs=[pl.BlockSpec((tm, K), lambda i, j: (i, 0)), pl.BlockSpec((K, tn), lambda i, j: (0, j))],
        out_specs=pl.BlockSpec((tm, tn), lambda i, j: (i, j)),
        compiler_params=_cp(("parallel", "arbitrary")))(h, w)


def _dilate_store(val, scr, o_ref, lead, d):
    rows = val.shape[0]
    if d == 1:
        o_ref[lead + (0,)] = val.astype(o_ref.dtype)
        return
    for hf in range(2):
        scr[hf, pl.ds(0, rows), :] = val[:, hf * 128:(hf + 1) * 128]
    for r in range(d):
        for hf in range(2):
            o_ref[lead + (r, slice(None), slice(hf * 128, (hf + 1) * 128))] = (
                scr[hf, pl.ds(r, rows // d, stride=d), :].astype(o_ref.dtype))


def _undilate_load(i_ref, lead, d, scr, rows):
    if d == 1:
        return i_ref[lead + (0,)].astype(F32)
    for r in range(d):
        for hf in range(2):
            scr[hf, pl.ds(r, rows // d, stride=d), :] = (
                i_ref[lead + (r, slice(None), slice(hf * 128, (hf + 1) * 128))].astype(F32))
    return jnp.concatenate([scr[0, pl.ds(0, rows), :], scr[1, pl.ds(0, rows), :]], axis=1)


def _rope_fwd(y, c_ref, s_ref):
    cosv = jnp.concatenate([c_ref[...]] * 2, axis=1)
    sinv = jnp.concatenate([s_ref[...]] * 2, axis=1)
    return y * cosv + _swap_halves(y) * sinv


def _rope_bwd(dy, c_ref, s_ref):
    cosv = jnp.concatenate([c_ref[...]] * 2, axis=1)
    sinv = jnp.concatenate([s_ref[...]] * 2, axis=1)
    return dy * cosv + _swap_halves(dy * sinv)


def _proj_qkv(h, wqkv, cos_t, sin_t, g, d, name):
    S, K = h.shape
    tm = min(2048, S)

    def body(h_ref, w_ref, c_ref, s_ref, o_ref, scr):
        j = pl.program_id(1)
        y = _dot(h_ref[...], w_ref[...])
        y = jnp.where(j < 2, _rope_fwd(y, c_ref, s_ref), y)
        _dilate_store(y, scr, o_ref, (), d)

    return pl.pallas_call(
        body, name=name, out_shape=jax.ShapeDtypeStruct((3, d, S // d, GW), BF), grid=(S // tm, 3),
        in_specs=[pl.BlockSpec((tm, K), lambda i, j: (i, 0)), pl.BlockSpec((K, GW), lambda i, j: (0, 3 * j + g)),
                  pl.BlockSpec((tm, 128), lambda i, j: (i, 0)), pl.BlockSpec((tm, 128), lambda i, j: (i, 0))],
        out_specs=pl.BlockSpec((None, d, tm // d, GW), lambda i, j: (j, 0, i, 0)),
        scratch_shapes=[pltpu.VMEM((2, tm, 128), F32)],
        compiler_params=_cp(("parallel", "arbitrary")))(h, wqkv, cos_t, sin_t)


def _band_masks(first_step):
    row = lax.broadcasted_iota(jnp.int32, (QB, 2 * QB), 0)
    col = lax.broadcasted_iota(jnp.int32, (QB, 2 * QB), 1)
    band = jnp.logical_and(col >= row, col <= row + QB)
    return band, jnp.logical_and(band, jnp.logical_or(col >= QB, jnp.logical_not(first_step)))


def _attn_fwd(qkv, name):
    _, d, L, _ = qkv.shape
    nsub = min(ATTN_NSUB, L // QB)
    R = nsub * QB
    nsteps = L // R

    def body(q_ref, kp_ref, kc_ref, vp_ref, vc_ref, o_ref, lse_ref):
        i = pl.program_id(1)
        band, band_first = _band_masks(i == 0)
        kfull = jnp.concatenate([kp_ref[...], kc_ref[...]], axis=0)
        vfull = jnp.concatenate([vp_ref[...], vc_ref[...]], axis=0)
        for sb in range(nsub):
            rows = slice(sb * QB, (sb + 1) * QB)
            q = q_ref[rows, :]
            ks, vs = kfull[sb * QB:(sb + 2) * QB], vfull[sb * QB:(sb + 2) * QB]
            msk = band_first if sb == 0 else band
            o = jnp.zeros((QB, GW), F32)
            lse = jnp.zeros((QB, GW), F32)
            for hh in range(NSLOT):
                hm = _head_mask(hh)
                qh = jnp.where(hm, q, 0)
                sc = jnp.where(msk, _dot_nt(qh, ks) * SCALE, NEG)
                m = jnp.max(sc, axis=1, keepdims=True)
                p = jnp.exp(sc - m)
                den = jnp.sum(p, axis=1, keepdims=True)
                acc = _dot(p.astype(BF), vs)
                o = o + jnp.where(hm, acc / den, 0.0)
                lse = lse + jnp.where(hm, m + jnp.log(den), 0.0)
            o_ref[rows, :] = o
            lse_ref[rows, :] = lse

    prev = lambda i: jnp.maximum(i * nsub - 1, 0)
    cur = lambda t: pl.BlockSpec((None, None, R, GW), lambda r, i: (t, r, i, 0))
    prv = lambda t: pl.BlockSpec((None, None, QB, GW), lambda r, i: (t, r, prev(i), 0))
    out = pl.BlockSpec((None, R, GW), lambda r, i: (r, i, 0))
    return pl.pallas_call(
        body, name=name, grid=(d, nsteps),
        out_shape=(jax.ShapeDtypeStruct((d, L, GW), F32), jax.ShapeDtypeStruct((d, L, GW), F32)),
        in_specs=[cur(0), prv(1), cur(1), prv(2), cur(2)],
        out_specs=(out, out), compiler_params=_cp(("parallel", "arbitrary")))(qkv, qkv, qkv, qkv, qkv)


def _attn_combine(os_, lses):
    S = os_[0].shape[1]
    tm = min(1024, S)

    def body(o0, o1, o2, l0, l1, l2, y_ref, j0, j1, j2, scr):
        os_nat = [_undilate_load(o, (), d, scr, tm) for o, d in zip((o0, o1, o2), DILS)]
        a, b, c = [_undilate_load(l, (), d, scr, tm) for l, d in zip((l0, l1, l2), DILS)]
        m = jnp.maximum(jnp.maximum(a, b), c)
        wa, wb, wc = jnp.exp(a - m), jnp.exp(b - m), jnp.exp(c - m)
        den = wa + wb + wc
        y_ref[...] = ((wa * os_nat[0] + wb * os_nat[1] + wc * os_nat[2]) / den).astype(BF)
        lse = m + jnp.log(den)
        for j_ref, d in zip((j0, j1, j2), DILS):
            _dilate_store(lse, scr, j_ref, (), d)

    dil = lambda d: pl.BlockSpec((d, tm // d, GW), lambda i: (0, i, 0))
    dshape = lambda d: jax.ShapeDtypeStruct((d, S // d, GW), F32)
    return pl.pallas_call(
        body, name="attn_combine", grid=(S // tm,),
        out_shape=(jax.ShapeDtypeStruct((S, GW), BF),) + tuple(dshape(d) for d in DILS),
        in_specs=[dil(d) for d in DILS] * 2,
        out_specs=(pl.BlockSpec((tm, GW), lambda i: (i, 0)),) + tuple(dil(d) for d in DILS),
        scratch_shapes=[pltpu.VMEM((2, tm, 128), F32)],
        compiler_params=_cp(("parallel",)))(*os_, *lses)


def _tril(upper=False):
    row = lax.broadcasted_iota(jnp.int32, (CHUNK, CHUNK), 0)
    col = lax.broadcasted_iota(jnp.int32, (CHUNK, CHUNK), 1)
    return row <= col if upper else col <= row


def _ln_fwd(z, gain, bias):
    mu = jnp.mean(z, axis=-1, keepdims=True)
    zc = z - mu
    rstd = lax.rsqrt(jnp.mean(zc * zc, axis=-1, keepdims=True) + EPS)
    zhat = zc * rstd
    return zhat, rstd, zhat * gain + bias


def _gmlp_fwd(puz, ln_g, ln_b, w_sp, b_sp_t):
    S = puz.shape[0]
    tm = min(512, S)
    nch = tm // CHUNK

    def body(p_ref, g_ref, b_ref, w_ref, bt_ref, o_ref):
        tril = _tril()
        ws = [jnp.where(tril, w_ref[gg], 0.0).astype(BF) for gg in range(4)]
        for ch in range(nch):
            rows = slice(ch * CHUNK, (ch + 1) * CHUNK)
            z = _gelu(p_ref[rows, GMLP_W:].astype(F32))
            _, _, zn = _ln_fwd(z, g_ref[...], b_ref[...])
            zn = zn.astype(BF)
            for gg in range(4):
                cols = slice(gg * CHUNK, (gg + 1) * CHUNK)
                sz = _dot(ws[gg], zn[:, cols]) + bt_ref[:, gg:gg + 1]
                u = _gelu(p_ref[rows, cols].astype(F32))
                o_ref[rows, cols] = (u * sz).astype(BF)

    return pl.pallas_call(
        body, name="gmlp_fwd", out_shape=jax.ShapeDtypeStruct((S, GMLP_W), BF), grid=(S // tm,),
        in_specs=[pl.BlockSpec((tm, 2 * GMLP_W), lambda i: (i, 0)),
                  pl.BlockSpec((1, GMLP_W), lambda i: (0, 0)), pl.BlockSpec((1, GMLP_W), lambda i: (0, 0)),
                  pl.BlockSpec((4, CHUNK, CHUNK), lambda i: (0, 0, 0)), pl.BlockSpec((CHUNK, 4), lambda i: (0, 0))],
        out_specs=pl.BlockSpec((tm, GMLP_W), lambda i: (i, 0)),
        compiler_params=_cp(("parallel",)))(puz, ln_g, ln_b, w_sp, b_sp_t)


def _merge_fwd(ya, yg, gates, wba, wbg, wout, x, g2, g3):
    S = x.shape[0]
    tm = min(512, S)

    def body(ya_ref, yg_ref, gt_ref, wba_ref, wbg_ref, wo_ref, x_ref, g2_ref, g3_ref,
             a_ref, gb_ref, mg_ref, y_ref, x2_ref, h3_ref):
        a = _dot(ya_ref[...], wba_ref[...])
        b = _dot(yg_ref[...], wbg_ref[...])
        a_ref[...] = a.astype(BF)
        gb_ref[...] = b.astype(BF)
        merged = (gt_ref[:, :D].astype(F32) * a + gt_ref[:, D:].astype(F32) * b).astype(BF)
        mg_ref[...] = merged
        y = _dot(merged, wo_ref[...])
        y_ref[...] = y
        x2 = x_ref[...] + y * _rms_stats(y) * g2_ref[...]
        x2_ref[...] = x2
        h3_ref[...] = (x2 * _rms_stats(x2) * g3_ref[...]).astype(BF)

    row = lambda w: pl.BlockSpec((tm, w), lambda i: (i, 0))
    full = lambda s: pl.BlockSpec(s, lambda i: (0, 0))
    return pl.pallas_call(
        body, name="merge_fwd", grid=(S // tm,),
        out_shape=(jax.ShapeDtypeStruct((S, D), BF), jax.ShapeDtypeStruct((S, D), BF), jax.ShapeDtypeStruct((S, D), BF),
                   jax.ShapeDtypeStruct((S, D), F32), jax.ShapeDtypeStruct((S, D), F32), jax.ShapeDtypeStruct((S, D), BF)),
        in_specs=[row(GW), row(GMLP_W), row(2 * D), full((GW, D)), full((GMLP_W, D)), full((D, D)), row(D),
                  full((1, D)), full((1, D))],
        out_specs=(row(D), row(D), row(D), row(D), row(D), row(D)),
        compiler_params=_cp(("parallel",)))(ya, yg, gates, wba, wbg, wout, x, g2, g3)


def _mlp_fwd(h3, wmi, wmo, x2, tgt, g4):
    S = x2.shape[0]
    tm = min(1024, S)
    tf = 512
    nf = DFF // tf

    def body(h_ref, wi_ref, wo_ref, x2_ref, t_ref, g4_ref, a_ref, dy2_ref, dout_ref, loss_ref, dg4_ref, acc_ref):
        i, c = pl.program_id(0), pl.program_id(1)

        @pl.when(jnp.logical_and(i == 0, c == 0))
        def _():
            loss_ref[...] = jnp.zeros_like(loss_ref)
            dg4_ref[...] = jnp.zeros_like(dg4_ref)

        a = jnp.maximum(_dot(h_ref[...], wi_ref[...]), 0.0)
        a_ref[...] = a.astype(BF)
        part = _dot((a * a).astype(BF), wo_ref[...])

        @pl.when(c == 0)
        def _():
            acc_ref[...] = part

        @pl.when(c > 0)
        def _():
            acc_ref[...] += part

        @pl.when(c == nf - 1)
        def _():
            y2 = acc_ref[...]
            r = _rms_stats(y2)
            out = x2_ref[...] + y2 * r * g4_ref[...]
            err = out - t_ref[...]
            sq = err * err
            tot = jnp.sum(jnp.sum(sq, axis=1, keepdims=True), axis=0, keepdims=True) * (0.5 / D)
            lane = lax.broadcasted_iota(jnp.int32, (1, 128), 1)
            loss_ref[...] += jnp.where(lane == 0, tot, 0.0)
            dout = err * (1.0 / D)
            dout_ref[...] = dout
            dy2, dg = _rms_bwd(y2, r, g4_ref[...], dout)
            dy2_ref[...] = dy2.astype(BF)
            dg4_ref[...] += dg

    row = pl.BlockSpec((tm, D), lambda i, c: (i, 0))
    return pl.pallas_call(
        body, name="mlp_fwd", grid=(S // tm, nf),
        out_shape=(jax.ShapeDtypeStruct((S, DFF), BF), jax.ShapeDtypeStruct((S, D), BF), jax.ShapeDtypeStruct((S, D), F32),
                   jax.ShapeDtypeStruct((1, 128), F32), jax.ShapeDtypeStruct((1, D), F32)),
        in_specs=[row, pl.BlockSpec((D, tf), lambda i, c: (0, c)), pl.BlockSpec((tf, D), lambda i, c: (c, 0)),
                  row, row, pl.BlockSpec((1, D), lambda i, c: (0, 0))],
        out_specs=(pl.BlockSpec((tm, tf), lambda i, c: (i, c)), row, row,
                   pl.BlockSpec((1, 128), lambda i, c: (0, 0)), pl.BlockSpec((1, D), lambda i, c: (0, 0))),
        scratch_shapes=[pltpu.VMEM((tm, D), F32)],
        compiler_params=_cp(("arbitrary", "arbitrary")))(h3, wmi, wmo, x2, tgt, g4)


def _mlp_bwd(dy2, a, wmo, wmi, x2, y, dout, g2, g3):
    S = x2.shape[0]
    tm = min(512, S)
    tf = 1024
    nf = DFF // tf

    def body(dy2_ref, a_ref, wo_ref, wi_ref, x2_ref, y_ref, dout_ref, g2_ref, g3_ref,
             dpre_ref, dx2_ref, dy_ref, dg3_ref, dg2_ref, acc_ref):
        i, c = pl.program_id(0), pl.program_id(1)

        @pl.when(jnp.logical_and(i == 0, c == 0))
        def _():
            dg3_ref[...] = jnp.zeros_like(dg3_ref)
            dg2_ref[...] = jnp.zeros_like(dg2_ref)

        da2 = _dot_nt(dy2_ref[...], wo_ref[...])
        dpre = (2.0 * a_ref[...].astype(F32) * da2).astype(BF)
        dpre_ref[...] = dpre
        part = _dot_nt(dpre, wi_ref[...])

        @pl.when(c == 0)
        def _():
            acc_ref[...] = part

        @pl.when(c > 0)
        def _():
            acc_ref[...] += part

        @pl.when(c == nf - 1)
        def _():
            x2 = x2_ref[...]
            dx3, dg3 = _rms_bwd(x2, _rms_stats(x2), g3_ref[...], acc_ref[...])
            dx2 = dout_ref[...] + dx3
            dx2_ref[...] = dx2
            dg3_ref[...] += dg3
            yv = y_ref[...]
            dy, dg2 = _rms_bwd(yv, _rms_stats(yv), g2_ref[...], dx2)
            dy_ref[...] = dy.astype(BF)
            dg2_ref[...] += dg2

    row = pl.BlockSpec((tm, D), lambda i, c: (i, 0))
    vec = pl.BlockSpec((1, D), lambda i, c: (0, 0))
    return pl.pallas_call(
        body, name="mlp_bwd", grid=(S // tm, nf),
        out_shape=(jax.ShapeDtypeStruct((S, DFF), BF), jax.ShapeDtypeStruct((S, D), F32), jax.ShapeDtypeStruct((S, D), BF),
                   jax.ShapeDtypeStruct((1, D), F32), jax.ShapeDtypeStruct((1, D), F32)),
        in_specs=[row, pl.BlockSpec((tm, tf), lambda i, c: (i, c)), pl.BlockSpec((tf, D), lambda i, c: (c, 0)),
                  pl.BlockSpec((D, tf), lambda i, c: (0, c)), row, row, row, vec, vec],
        out_specs=(pl.BlockSpec((tm, tf), lambda i, c: (i, c)), row, row, vec, vec),
        scratch_shapes=[pltpu.VMEM((tm, D), F32)],
        compiler_params=_cp(("arbitrary", "arbitrary")))(dy2, a, wmo, wmi, x2, y, dout, g2, g3)


def _mm_tn(a, b, name, square_a=False, tm=1024, tn=1024, tk=2048):
    S, M = a.shape
    N = b.shape[1]
    tm, tk = min(tm, M), min(tk, S)
    tn = max(t for t in range(128, min(tn, N) + 1, 128) if N % t == 0)
    assert M % tm == 0 and S % tk == 0
    nk = S // tk

    def body(a_ref, b_ref, o_ref):
        k = pl.program_id(2)
        av = a_ref[...]
        if square_a:
            av = av * av
        part = _dot_tn(av, b_ref[...])

        @pl.when(k == 0)
        def _():
            o_ref[...] = part

        @pl.when(k > 0)
        def _():
            o_ref[...] += part

    return pl.pallas_call(
        body, name=name, out_shape=jax.ShapeDtypeStruct((M, N), F32), grid=(M // tm, N // tn, nk),
        in_specs=[pl.BlockSpec((tk, tm), lambda i, j, k: (k, i)), pl.BlockSpec((tk, tn), lambda i, j, k: (k, j))],
        out_specs=pl.BlockSpec((tm, tn), lambda i, j, k: (i, j)),
        compiler_params=_cp(("parallel", "parallel", "arbitrary")))(a, b)


def _outproj_bwd(dy, wout, abr, gbr, gates, wba, wbg, ya):
    S = dy.shape[0]
    tm = min(512, S)

    def body(dy_ref, wo_ref, a_ref, b_ref, gt_ref, wba_ref, wbg_ref, ya_ref,
             dgt_ref, da_ref, db_ref, dyg_ref, e0, e1, e2, s0, s1, s2, scr):
        dm = _dot_nt(dy_ref[...], wo_ref[...])
        ga, gb = gt_ref[:, :D].astype(F32), gt_ref[:, D:].astype(F32)
        dgt_ref[:, :D] = (dm * a_ref[...].astype(F32) * ga * (1.0 - ga)).astype(BF)
        dgt_ref[:, D:] = (dm * b_ref[...].astype(F32) * gb * (1.0 - gb)).astype(BF)
        da = (dm * ga).astype(BF)
        db = (dm * gb).astype(BF)
        da_ref[...] = da
        db_ref[...] = db
        dyg_ref[...] = _dot_nt(db, wbg_ref[...]).astype(BF)
        dya = _dot_nt(da, wba_ref[...]).astype(BF).astype(F32)
        dyy = dya * ya_ref[...].astype(F32)
        dsum = jnp.zeros((tm, GW), F32)
        for hh in range(NSLOT):
            hm = _head_mask(hh)
            dsum = dsum + jnp.where(hm, jnp.sum(jnp.where(hm, dyy, 0.0), axis=1, keepdims=True), 0.0)
        for e_ref, s_ref, d in zip((e0, e1, e2), (s0, s1, s2), DILS):
            _dilate_store(dya, scr, e_ref, (), d)
            _dilate_store(dsum, scr, s_ref, (), d)

    row = lambda w: pl.BlockSpec((tm, w), lambda i: (i, 0))
    full = lambda s: pl.BlockSpec(s, lambda i: (0, 0))
    dil = lambda d: pl.BlockSpec((d, tm // d, GW), lambda i: (0, i, 0))
    dshape = lambda d, t: jax.ShapeDtypeStruct((d, S // d, GW), t)
    return pl.pallas_call(
        body, name="outproj_bwd", grid=(S // tm,),
        out_shape=(jax.ShapeDtypeStruct((S, 2 * D), BF), jax.ShapeDtypeStruct((S, D), BF), jax.ShapeDtypeStruct((S, D), BF),
                   jax.ShapeDtypeStruct((S, GMLP_W), BF)) + tuple(dshape(d, BF) for d in DILS)
        + tuple(dshape(d, F32) for d in DILS),
        in_specs=[row(D), full((D, D)), row(D), row(D), row(2 * D), full((GW, D)), full((GMLP_W, D)), row(GW)],
        out_specs=(row(2 * D), row(D), row(D), row(GMLP_W)) + tuple(dil(d) for d in DILS) * 2,
        scratch_shapes=[pltpu.VMEM((2, tm, 128), F32)],
        compiler_params=_cp(("parallel",)))(dy, wout, abr, gbr, gates, wba, wbg, ya)


def _gmlp_bwd(puz, dyg, ln_g, ln_b, w_sp, w_sp_t, b_sp_t):
    S = puz.shape[0]
    tm = min(512, S)
    nch = tm // CHUNK

    def body(p_ref, dy_ref, g_ref, b_ref, w_ref, wt_ref, bt_ref,
             dp_ref, dw_ref, dbs_ref, dg_ref, dbias_ref, dbacc_ref):
        i = pl.program_id(0)

        @pl.when(i == 0)
        def _():
            dw_ref[...] = jnp.zeros_like(dw_ref)
            dbacc_ref[...] = jnp.zeros_like(dbacc_ref)
            dg_ref[...] = jnp.zeros_like(dg_ref)
            dbias_ref[...] = jnp.zeros_like(dbias_ref)

        tril = _tril()
        ws = [jnp.where(tril, w_ref[gg], 0.0).astype(BF) for gg in range(4)]
        triu = _tril(upper=True)
        wts = [jnp.where(triu, wt_ref[gg], 0.0).astype(BF) for gg in range(4)]
        gain = g_ref[...]
        for ch in range(nch):
            rows = slice(ch * CHUNK, (ch + 1) * CHUNK)
            pz = p_ref[rows, GMLP_W:].astype(F32)
            z = _gelu(pz)
            zhat, rstd, zn = _ln_fwd(z, gain, b_ref[...])
            znb = zn.astype(BF)
            dzn_parts = []
            for gg in range(4):
                cols = slice(gg * CHUNK, (gg + 1) * CHUNK)
                pu = p_ref[rows, cols].astype(F32)
                u = _gelu(pu)
                sz = _dot(ws[gg], znb[:, cols]) + bt_ref[:, gg:gg + 1]
                dyv = dy_ref[rows, cols].astype(F32)
                dp_ref[rows, cols] = (dyv * sz * _gelu_grad(pu)).astype(BF)
                dsz = dyv * u
                dbacc_ref[gg] += dsz
                dszb = dsz.astype(BF)
                dw_ref[gg] += _dot_nt(dszb, znb[:, cols])
                dzn_parts.append(_dot(wts[gg], dszb))
            dzn = jnp.concatenate(dzn_parts, axis=1)
            dg_ref[...] += jnp.sum(dzn * zhat, axis=0, keepdims=True)
            dbias_ref[...] += jnp.sum(dzn, axis=0, keepdims=True)
            dzh = dzn * gain
            dz = rstd * (dzh - jnp.mean(dzh, axis=-1, keepdims=True)
                         - zhat * jnp.mean(dzh * zhat, axis=-1, keepdims=True))
            dp_ref[rows, GMLP_W:] = (dz * _gelu_grad(pz)).astype(BF)

        @pl.when(i == pl.num_programs(0) - 1)
        def _():
            for gg in range(4):
                dw_ref[gg] = jnp.where(tril, dw_ref[gg], 0.0)
                dbs_ref[gg] = jnp.sum(dbacc_ref[gg], axis=1, keepdims=True)

    full2 = lambda s: pl.BlockSpec(s, lambda i: (0, 0))
    full3 = lambda s: pl.BlockSpec(s, lambda i: (0, 0, 0))
    return pl.pallas_call(
        body, name="gmlp_bwd", grid=(S // tm,),
        out_shape=(jax.ShapeDtypeStruct((S, 2 * GMLP_W), BF), jax.ShapeDtypeStruct((4, CHUNK, CHUNK), F32),
                   jax.ShapeDtypeStruct((4, CHUNK, 1), F32), jax.ShapeDtypeStruct((1, GMLP_W), F32),
                   jax.ShapeDtypeStruct((1, GMLP_W), F32)),
        in_specs=[pl.BlockSpec((tm, 2 * GMLP_W), lambda i: (i, 0)), pl.BlockSpec((tm, GMLP_W), lambda i: (i, 0)),
                  full2((1, GMLP_W)), full2((1, GMLP_W)), full3((4, CHUNK, CHUNK)), full3((4, CHUNK, CHUNK)),
                  full2((CHUNK, 4))],
        out_specs=(pl.BlockSpec((tm, 2 * GMLP_W), lambda i: (i, 0)), full3((4, CHUNK, CHUNK)), full3((4, CHUNK, 1)),
                   full2((1, GMLP_W)), full2((1, GMLP_W))),
        scratch_shapes=[pltpu.VMEM((4, CHUNK, CHUNK), F32)],
        compiler_params=_cp(("arbitrary",)))(puz, dyg, ln_g, ln_b, w_sp, w_sp_t, b_sp_t)


def _attn_bwd(qkv, dya, dsums, lse, name):
    _, d, L, _ = qkv.shape
    nsub = min(ATTN_NSUB, L // QB)
    R = nsub * QB
    nsteps = L // R

    def body(q_ref, qn_ref, kp_ref, kc_ref, vp_ref, vc_ref, dy_ref, dyn_ref, e_ref, en_ref, l_ref, ln_ref, o_ref):
        i = pl.program_id(1)
        band, band_first = _band_masks(i == 0)
        row = lax.broadcasted_iota(jnp.int32, (QB, QB), 0)
        col = lax.broadcasted_iota(jnp.int32, (QB, QB), 1)
        mask_next = jnp.logical_and(col >= row, i < nsteps - 1)
        kc, vc = kc_ref[...], vc_ref[...]
        kfull = jnp.concatenate([kp_ref[...], kc], axis=0)
        vfull = jnp.concatenate([vp_ref[...], vc], axis=0)
        k_last, v_last = kc[(nsub - 1) * QB:], vc[(nsub - 1) * QB:]
        q_ext = jnp.concatenate([q_ref[...], qn_ref[...]], axis=0)
        dy_ext = jnp.concatenate([dy_ref[...], dyn_ref[...]], axis=0)
        esum, esum_n, lse, lse_n = e_ref[...], en_ref[...], l_ref[...], ln_ref[...]
        dq = [jnp.zeros((QB, GW), F32) for _ in range(nsub)]
        dk = [jnp.zeros((QB, GW), F32) for _ in range(nsub)]
        dv = [jnp.zeros((QB, GW), F32) for _ in range(nsub)]
        for hh in range(NSLOT):
            hm = _head_mask(hh)
            q_h = jnp.where(hm, q_ext, 0)
            dy_h = jnp.where(hm, dy_ext, 0)
            rowstat = lambda t: jnp.max(jnp.where(hm, t, -jnp.inf), axis=1, keepdims=True)
            ps, dss = [], []
            for sb in range(nsub):
                rows = slice(sb * QB, (sb + 1) * QB)
                ks, vs = kfull[sb * QB:(sb + 2) * QB], vfull[sb * QB:(sb + 2) * QB]
                msk = band_first if sb == 0 else band
                p = jnp.where(msk, jnp.exp(_dot_nt(q_h[rows], ks) * SCALE - rowstat(lse[rows])), 0.0)
                ds = (p * (_dot_nt(dy_h[rows], vs) - rowstat(esum[rows]))).astype(BF)
                dq[sb] = dq[sb] + jnp.where(hm, _dot(ds, ks), 0.0)
                ps.append(p.astype(BF))
                dss.append(ds)
            nrows = slice(nsub * QB, (nsub + 1) * QB)
            p_n = jnp.where(mask_next, jnp.exp(_dot_nt(q_h[nrows], k_last) * SCALE - rowstat(lse_n)), 0.0)
            ds_n = (p_n * (_dot_nt(dy_h[nrows], v_last) - rowstat(esum_n))).astype(BF)
            p_n = p_n.astype(BF)
            for sb in range(nsub):
                two = slice(sb * QB, (sb + 2) * QB)
                nxt_p = ps[sb + 1][:, :QB] if sb + 1 < nsub else p_n
                nxt_ds = dss[sb + 1][:, :QB] if sb + 1 < nsub else ds_n
                dk[sb] = dk[sb] + _dot_tn(jnp.concatenate([dss[sb][:, QB:], nxt_ds], axis=0), q_h[two])
                dv[sb] = dv[sb] + _dot_tn(jnp.concatenate([ps[sb][:, QB:], nxt_p], axis=0), dy_h[two])
        for sb in range(nsub):
            rows = slice(sb * QB, (sb + 1) * QB)
            o_ref[0, rows, :] = (dq[sb] * SCALE).astype(BF)
            o_ref[1, rows, :] = (dk[sb] * SCALE).astype(BF)
            o_ref[2, rows, :] = dv[sb].astype(BF)

    prev = lambda i: jnp.maximum(i * nsub - 1, 0)
    nxt = lambda i: jnp.minimum((i + 1) * nsub, L // QB - 1)
    cur4 = lambda t: pl.BlockSpec((None, None, R, GW), lambda r, i: (t, r, i, 0))
    prv4 = lambda t: pl.BlockSpec((None, None, QB, GW), lambda r, i: (t, r, prev(i), 0))
    nxt4 = lambda t: pl.BlockSpec((None, None, QB, GW), lambda r, i: (t, r, nxt(i), 0))
    cur3 = pl.BlockSpec((None, R, GW), lambda r, i: (r, i, 0))
    nxt3 = pl.BlockSpec((None, QB, GW), lambda r, i: (r, nxt(i), 0))
    return pl.pallas_call(
        body, name=name, grid=(d, nsteps), out_shape=jax.ShapeDtypeStruct((3, d, L, GW), BF),
        in_specs=[cur4(0), nxt4(0), prv4(1), cur4(1), prv4(2), cur4(2), cur3, nxt3, cur3, nxt3, cur3, nxt3],
        out_specs=pl.BlockSpec((3, None, R, GW), lambda r, i: (0, r, i, 0)),
        compiler_params=_cp(("parallel", "arbitrary")))(qkv, qkv, qkv, qkv, qkv, qkv, dya, dya, dsums, dsums, lse, lse)


def _inproj_bwd(dqkvs, dpuz, dgates, wqkv, wuz, wg, cos_t, sin_t, x, dx2, g1):
    S = x.shape[0]
    tm = min(512, S)

    def body(d0_ref, d1_ref, d2_ref, dp_ref, dg_ref, wqkv_ref, wuz_ref, wg_ref, c_ref, s_ref,
             x_ref, dx2_ref, g1_ref, gx_ref, dg1_ref, dn_ref, scr):
        i = pl.program_id(0)

        @pl.when(i == 0)
        def _():
            dg1_ref[...] = jnp.zeros_like(dg1_ref)

        for t in range(3):
            for g, (d_ref, d) in enumerate(zip((d0_ref, d1_ref, d2_ref), DILS)):
                piece = _undilate_load(d_ref, (t,), d, scr, tm)
                if t < 2:
                    piece = _rope_bwd(piece, c_ref, s_ref)
                dn_ref[:, (3 * t + g) * GW:(3 * t + g + 1) * GW] = piece.astype(BF)
        dh = (_dot_nt(dn_ref[...], wqkv_ref[...]) + _dot_nt(dp_ref[...], wuz_ref[...])
              + _dot_nt(dg_ref[...], wg_ref[...]))
        xv = x_ref[...]
        dx1, dg1 = _rms_bwd(xv, _rms_stats(xv), g1_ref[...], dh)
        gx_ref[...] = dx2_ref[...] + dx1
        dg1_ref[...] += dg1

    row = lambda w: pl.BlockSpec((tm, w), lambda i: (i, 0))
    full = lambda s: pl.BlockSpec(s, lambda i: (0, 0))
    dil = lambda d: pl.BlockSpec((3, d, tm // d, GW), lambda i: (0, 0, i, 0))
    return pl.pallas_call(
        body, name="inproj_bwd", grid=(S // tm,),
        out_shape=(jax.ShapeDtypeStruct((S, D), F32), jax.ShapeDtypeStruct((1, D), F32),
                   jax.ShapeDtypeStruct((S, 3 * AW), BF)),
        in_specs=[dil(d) for d in DILS] + [row(2 * GMLP_W), row(2 * D), full((D, 3 * AW)), full((D, 2 * GMLP_W)),
                                            full((D, 2 * D)), row(128), row(128), row(D), row(D), full((1, D))],
        out_specs=(row(D), full((1, D)), row(3 * AW)),
        scratch_shapes=[pltpu.VMEM((2, tm, 128), F32)],
        compiler_params=_cp(("arbitrary",)))(*dqkvs, dpuz, dgates, wqkv, wuz, wg, cos_t, sin_t, x, dx2, g1)


def _adam_math(w, g, m, v):
    m2 = ADAM_B1 * m + (1.0 - ADAM_B1) * g
    v2 = ADAM_B2 * v + (1.0 - ADAM_B2) * (g * g)
    m_hat = m2 / (1.0 - ADAM_B1 ** ADAM_STEP)
    v_hat = v2 / (1.0 - ADAM_B2 ** ADAM_STEP)
    delta = -ADAM_LR * (m_hat / (jnp.sqrt(v_hat) + ADAM_EPS) + ADAM_WD * w)
    return delta, m2, v2


def _adam_shard(own, recv, w, m, v, name):
    R, C = w.shape
    tr = min(256, R)

    def body(own_ref, r_ref, w_ref, m_ref, v_ref, g_ref, d_ref, m2_ref, v2_ref):
        g = own_ref[...] + r_ref[0].astype(F32) + r_ref[1].astype(F32) + r_ref[2].astype(F32)
        g_ref[...] = g
        d_ref[...], m2_ref[...], v2_ref[...] = _adam_math(w_ref[...], g, m_ref[...], v_ref[...])

    spec = pl.BlockSpec((tr, C), lambda i: (i, 0))
    out = jax.ShapeDtypeStruct((R, C), F32)
    return pl.pallas_call(
        body, name=name, grid=(R // tr,), out_shape=(out, out, out, out),
        in_specs=[spec, pl.BlockSpec((3, tr, C), lambda i: (0, i, 0)), spec, spec, spec],
        out_specs=(spec, spec, spec, spec), compiler_params=_cp(("parallel",)))(own, recv, w, m, v)


def _adam_small(gp, wp, mp, vp):
    def body(g_ref, w_ref, m_ref, v_ref, d_ref, m2_ref, v2_ref):
        d_ref[...], m2_ref[...], v2_ref[...] = _adam_math(w_ref[...], g_ref[...], m_ref[...], v_ref[...])

    out = jax.ShapeDtypeStruct(gp.shape, F32)
    return pl.pallas_call(body, name="adam_small", out_shape=(out, out, out),
                          compiler_params=pltpu.CompilerParams(vmem_limit_bytes=VMEM_LIMIT))(gp, wp, mp, vp)


def _rs_add(gblocks, recv, idx, name):
    _, R, C = gblocks.shape
    tr = min(256, R)

    def body(t_ref, g_ref, r_ref, own_ref, send_ref):
        j = pl.program_id(1)
        s = g_ref[...] + r_ref[...]

        @pl.when(j == 0)
        def _():
            own_ref[...] = s

        @pl.when(j > 0)
        def _():
            send_ref[...] = s.astype(BF)

    grid_spec = pltpu.PrefetchScalarGridSpec(
        num_scalar_prefetch=1, grid=(R // tr, 4),
        in_specs=[pl.BlockSpec((None, tr, C), lambda i, j, t: (t[j], i, 0)),
                  pl.BlockSpec((None, tr, C), lambda i, j, t: (t[4 + j], i, 0))],
        out_specs=[pl.BlockSpec((tr, C), lambda i, j, t: (i, 0)),
                   pl.BlockSpec((None, tr, C), lambda i, j, t: (jnp.maximum(j - 1, 0), i, 0))])
    return pl.pallas_call(
        body, name=name, grid_spec=grid_spec,
        out_shape=(jax.ShapeDtypeStruct((R, C), F32), jax.ShapeDtypeStruct((3, R, C), BF)),
        compiler_params=_cp(("parallel", "arbitrary")))(idx, gblocks, recv)


def _mesh_pos():
    return lax.axis_index("x"), lax.axis_index("y"), lax.axis_index("c")


def _all_gather_weights(shards):
    n = len(shards)

    def body(*refs):
        ins, outs = refs[:n], refs[n:2 * n]
        send_sems, recv_sems, loc_sems = refs[2 * n:]
        x, y, c = _mesh_pos()
        me, sib = (x, y, c), (x, y, 1 - c)
        chips = [(1 - x, y), (x, 1 - y), (1 - x, 1 - y)]

        def rcopy(k, s, block, to, src=None):
            dst = outs[k].at[4 * block[0] + 2 * block[1] + block[2]]
            return pltpu.make_async_remote_copy(
                src_ref=dst if src is None else src, dst_ref=dst, send_sem=send_sems.at[k, s],
                recv_sem=recv_sems.at[k, s], device_id=to, device_id_type=MESH)

        locs = [pltpu.make_async_copy(ins[k], outs[k].at[4 * x + 2 * y + c], loc_sems.at[k]) for k in range(n)]
        for cp in locs:
            cp.start()
        started = []
        for k in range(n):
            for j, chip in enumerate(chips):
                started.append(rcopy(k, 1 + j, me, (*chip, c), src=ins[k]))
                started[-1].start()
        for k in range(n):
            started.append(rcopy(k, 0, me, sib, src=ins[k]))
            started[-1].start()
        for k in range(n):
            for j, chip in enumerate(chips):
                rcopy(k, 1 + j, (*chip, c), me).wait_recv()
                started.append(rcopy(k, 4 + j, (*chip, c), sib))
                started[-1].start()
        for k in range(n):
            rcopy(k, 0, sib, me).wait_recv()
            for j, chip in enumerate(chips):
                rcopy(k, 4 + j, (*chip, 1 - c), me).wait_recv()
        for cp in started:
            cp.wait_send()
        for cp in locs:
            cp.wait()

    hbm = pl.BlockSpec(memory_space=pl.ANY)
    return pl.pallas_call(
        body, name="ag_weights",
        out_shape=tuple(jax.ShapeDtypeStruct((8,) + s.shape, s.dtype) for s in shards),
        in_specs=[hbm] * n, out_specs=(hbm,) * n,
        scratch_shapes=[pltpu.SemaphoreType.DMA((n, 7)), pltpu.SemaphoreType.DMA((n, 7)),
                        pltpu.SemaphoreType.DMA((n,))])(*shards)


def _rs_d2d(gblocks):
    n = len(gblocks)

    def body(*refs):
        ins, outs = refs[:n], refs[n:2 * n]
        send_sems, recv_sems = refs[2 * n:]
        x, y, c = _mesh_pos()
        copies = []
        for k in range(n):
            for kk in range(4):
                copies.append(pltpu.make_async_remote_copy(
                    src_ref=ins[k].at[2 * kk + 1 - c], dst_ref=outs[k].at[kk], send_sem=send_sems.at[k, kk],
                    recv_sem=recv_sems.at[k, kk], device_id=(x, y, 1 - c), device_id_type=MESH))
                copies[-1].start()
        for cp in copies:
            cp.wait()

    hbm = pl.BlockSpec(memory_space=pl.ANY)
    return pl.pallas_call(
        body, name="rs_d2d",
        out_shape=tuple(jax.ShapeDtypeStruct((4,) + g.shape[1:], g.dtype) for g in gblocks),
        in_specs=[hbm] * n, out_specs=(hbm,) * n,
        scratch_shapes=[pltpu.SemaphoreType.DMA((n, 4)), pltpu.SemaphoreType.DMA((n, 4))])(*gblocks)


def _rs_ici(sends):
    n = len(sends)

    def body(*refs):
        ins, outs = refs[:n], refs[n:2 * n]
        send_sems, recv_sems = refs[2 * n:]
        x, y, c = _mesh_pos()
        chips = [(1 - x, y), (x, 1 - y), (1 - x, 1 - y)]
        copies = []
        for k in range(n):
            for j, chip in enumerate(chips):
                copies.append(pltpu.make_async_remote_copy(
                    src_ref=ins[k].at[j], dst_ref=outs[k].at[j], send_sem=send_sems.at[k, j],
                    recv_sem=recv_sems.at[k, j], device_id=(*chip, c), device_id_type=MESH))
                copies[-1].start()
        for cp in copies:
            cp.wait()

    hbm = pl.BlockSpec(memory_space=pl.ANY)
    return pl.pallas_call(
        body, name="rs_ici",
        out_shape=tuple(jax.ShapeDtypeStruct(s.shape, s.dtype) for s in sends),
        in_specs=[hbm] * n, out_specs=(hbm,) * n,
        scratch_shapes=[pltpu.SemaphoreType.DMA((n, 3)), pltpu.SemaphoreType.DMA((n, 3))])(*sends)


def _small_allreduce(pkt):
    R = pkt.shape[0]

    def body(p_ref, o_ref, buf, send_sems, recv_sems):
        x, y, c = _mesh_pos()
        me, sib = (x, y, c), (x, y, 1 - c)
        chips = [(1 - x, y), (x, 1 - y), (1 - x, 1 - y)]

        def rcopy(s, block, to, src=None):
            dst = buf.at[4 * block[0] + 2 * block[1] + block[2]]
            return pltpu.make_async_remote_copy(
                src_ref=dst if src is None else src, dst_ref=dst, send_sem=send_sems.at[s],
                recv_sem=recv_sems.at[s], device_id=to, device_id_type=MESH)

        buf[4 * x + 2 * y + c] = p_ref[...]
        started = [rcopy(0, me, sib, src=p_ref)]
        started += [rcopy(1 + j, me, (*chip, c), src=p_ref) for j, chip in enumerate(chips)]
        for cp in started:
            cp.start()
        for j, chip in enumerate(chips):
            rcopy(1 + j, (*chip, c), me).wait_recv()
            started.append(rcopy(4 + j, (*chip, c), sib))
            started[-1].start()
        rcopy(0, sib, me).wait_recv()
        for j, chip in enumerate(chips):
            rcopy(4 + j, (*chip, 1 - c), me).wait_recv()
        for cp in started:
            cp.wait_send()
        acc = buf[0]
        for b in range(1, 8):
            acc = acc + buf[b]
        o_ref[...] = acc

    return pl.pallas_call(
        body, name="small_allreduce", out_shape=jax.ShapeDtypeStruct(pkt.shape, F32),
        in_specs=[pl.BlockSpec(memory_space=pltpu.VMEM)], out_specs=pl.BlockSpec(memory_space=pltpu.VMEM),
        scratch_shapes=[pltpu.VMEM((8, R, 128), F32), pltpu.SemaphoreType.DMA((7,)), pltpu.SemaphoreType.DMA((7,))],
        compiler_params=pltpu.CompilerParams(vmem_limit_bytes=VMEM_LIMIT))(pkt)


def _rope_tables(S):
    half = HD // 2
    inv_freq = ROPE_THETA ** (-jnp.arange(half, dtype=F32) / half)
    ang = jnp.arange(S, dtype=F32)[:, None] * inv_freq[None, :]
    cos, sin = jnp.cos(ang), jnp.sin(ang)
    return jnp.concatenate([cos, cos, cos, cos], axis=1), jnp.concatenate([-sin, sin, -sin, sin], axis=1)


def _local_step(x, tgt, wqkv, wuz, wg, wba, wbg, wout, wmi, wmo, g1, g2, g3, g4, ln_g, ln_b, w_sp, b_sp):
    S = x.shape[0]
    cos_t, sin_t = _rope_tables(S)
    b_sp_t = b_sp.T
    w_sp_t = w_sp.transpose(0, 2, 1)

    h1 = _rms_fwd(x, g1, "rms_pre_mix")
    qkvs = [_proj_qkv(h1, wqkv, cos_t, sin_t, g, d, "proj_qkv_d%d" % d) for g, d in enumerate(DILS)]
    puz = _proj(h1, wuz, "none", "proj_uz")
    gates = _proj(h1, wg, "sigmoid", "proj_gates")
    fwd = [_attn_fwd(t, "attn_fwd_d%d" % d) for t, d in zip(qkvs, DILS)]
    ya, *lses = _attn_combine([o for o, _ in fwd], [l for _, l in fwd])
    yg = _gmlp_fwd(puz, ln_g, ln_b, w_sp, b_sp_t)
    abr, gbr, merged, y, x2, h3 = _merge_fwd(ya, yg, gates, wba, wbg, wout, x, g2, g3)
    a, dy2, dout, loss_row, dg4 = _mlp_fwd(h3, wmi, wmo, x2, tgt, g4)

    dpre, dx2, dy, dg3, dg2 = _mlp_bwd(dy2, a, wmo, wmi, x2, y, dout, g2, g3)
    dwmo = _mm_tn(a, dy2, "dw_mlp_out", square_a=True)
    dwmi = _mm_tn(h3, dpre, "dw_mlp_in")
    dgates, da, db, dyg, *rest = _outproj_bwd(dy, wout, abr, gbr, gates, wba, wbg, ya)
    dyas, dsums = rest[:3], rest[3:]
    dwout = _mm_tn(merged, dy, "dw_out")
    dwba = _mm_tn(ya, da, "dw_branch_attn")
    dwbg = _mm_tn(yg, db, "dw_branch_gmlp")
    dpuz, dwsp, dbs, dlng, dlnb = _gmlp_bwd(puz, dyg, ln_g, ln_b, w_sp, w_sp_t, b_sp_t)
    dqkvs = [_attn_bwd(qkvs[g], dyas[g], dsums[g], lses[g], "attn_bwd_d%d" % d) for g, d in enumerate(DILS)]
    grad_x, dg1, dqkv = _inproj_bwd(dqkvs, dpuz, dgates, wqkv, wuz, wg, cos_t, sin_t, x, dx2, g1)
    dwin = jnp.concatenate([_mm_tn(h1, t, "dw_in_%d" % n) for n, t in enumerate((dqkv, dpuz, dgates))], axis=1)
    small = dict(g1=dg1, g2=dg2, g3=dg3, g4=dg4, ln_g=dlng, ln_b=dlnb, b_sp=dbs.reshape(4, CHUNK), w_sp=dwsp)
    return loss_row, grad_x, (dwin, dwba, dwbg, dwout, dwmi, dwmo), small


def _pack_small(first_row, t):
    pad = lambda a, rows: jnp.concatenate([a, jnp.zeros((rows - a.shape[0], 128), F32)], axis=0)
    return jnp.concatenate([
        pad(first_row, 8), t["g1"].reshape(8, 128), t["g2"].reshape(8, 128), t["g3"].reshape(8, 128),
        t["g4"].reshape(8, 128), t["ln_g"].reshape(4, 128), t["ln_b"].reshape(4, 128),
        pad(t["b_sp"].reshape(4, 128), 8), t["w_sp"].reshape(4 * CHUNK, 128)], axis=0)


def _unpack_small(p):
    vec = lambda r, n: p[r:r + n // 128].reshape(1, n)
    return dict(g1=vec(PK_G1, D), g2=vec(PK_G2, D), g3=vec(PK_G3, D), g4=vec(PK_G4, D),
                ln_g=vec(PK_LNG, GMLP_W), ln_b=vec(PK_LNB, GMLP_W),
                b_sp=p[PK_BSP:PK_BSP + 4].reshape(1, 4, CHUNK), w_sp=p[PK_WSP:PKT_ROWS].reshape(1, 4, CHUNK, CHUNK))


def kernel(x, norm_pre_mix, w_in, w_spatial, b_spatial, ln_v_gain, ln_v_bias, w_branch_attn, w_branch_gmlp, w_out, norm_post_mix, norm_pre_mlp, w_mlp_in, w_mlp_out, norm_post_mlp, loss_target, m_norm_pre_mix, m_w_in, m_w_spatial, m_b_spatial, m_ln_v_gain, m_ln_v_bias, m_w_branch_attn, m_w_branch_gmlp, m_w_out, m_norm_post_mix, m_norm_pre_mlp, m_w_mlp_in, m_w_mlp_out, m_norm_post_mlp, v_norm_pre_mix, v_w_in, v_w_spatial, v_b_spatial, v_ln_v_gain, v_ln_v_bias, v_w_branch_attn, v_w_branch_gmlp, v_w_out, v_norm_post_mix, v_norm_pre_mlp, v_w_mlp_in, v_w_mlp_out, v_norm_post_mlp):
    mx, my, mc = _mesh_pos()

    big_w = [w_in[0], w_branch_attn[0], w_branch_gmlp[0], w_out[0], w_mlp_in[0], w_mlp_out[0]]
    big_m = [m_w_in[0], m_w_branch_attn[0], m_w_branch_gmlp[0], m_w_out[0], m_w_mlp_in[0], m_w_mlp_out[0]]
    big_v = [v_w_in[0], v_w_branch_attn[0], v_w_branch_gmlp[0], v_w_out[0], v_w_mlp_in[0], v_w_mlp_out[0]]
    col_sharded = [True, True, True, False, True, False]
    gathered = _all_gather_weights([w.astype(BF) for w in big_w])
    full = [t.transpose(1, 0, 2).reshape(t.shape[1], 8 * t.shape[2]) if cs else t.reshape(8 * t.shape[1], t.shape[2])
            for t, cs in zip(gathered, col_sharded)]
    win, wba, wbg, wout, wmi, wmo = full
    wqkv, wuz, wg = win[:, :3 * AW], win[:, 3 * AW:3 * AW + 2 * GMLP_W], win[:, 3 * AW + 2 * GMLP_W:]

    loss_row, grad_x, dws, small = _local_step(
        x[0], loss_target[0], wqkv, wuz, wg, wba, wbg, wout, wmi, wmo,
        norm_pre_mix, norm_post_mix, norm_pre_mlp, norm_post_mlp, ln_v_gain, ln_v_bias, w_spatial[0], b_spatial[0])

    gblocks = [g.reshape(g.shape[0], 8, g.shape[1] // 8).transpose(1, 0, 2) if cs
               else g.reshape(8, g.shape[0] // 8, g.shape[1]) for g, cs in zip(dws, col_sharded)]
    from_sib = _rs_d2d(gblocks)
    rel = [(0, 0), (1, 0), (0, 1), (1, 1)]
    chip_of = [2 * (mx ^ dx) + (my ^ dy) for dx, dy in rel]
    idx = jnp.stack([2 * k + mc for k in chip_of] + chip_of).astype(jnp.int32)
    halves = [_rs_add(g, r, idx, "rs_add_%d" % n) for n, (g, r) in enumerate(zip(gblocks, from_sib))]
    from_chips = _rs_ici([s for _, s in halves])
    names = ["w_in", "w_branch_attn", "w_branch_gmlp", "w_out", "w_mlp_in", "w_mlp_out"]
    upd = {nm: _adam_shard(own, r, w, m, v, "adam_" + nm)
           for nm, (own, _), r, w, m, v in zip(names, halves, from_chips, big_w, big_m, big_v)}

    pkt = _small_allreduce(_pack_small(loss_row, small))
    small_w = dict(g1=norm_pre_mix, g2=norm_post_mix, g3=norm_pre_mlp, g4=norm_post_mlp, ln_g=ln_v_gain,
                   ln_b=ln_v_bias, b_sp=b_spatial[0], w_sp=w_spatial[0])
    small_m = dict(g1=m_norm_pre_mix, g2=m_norm_post_mix, g3=m_norm_pre_mlp, g4=m_norm_post_mlp, ln_g=m_ln_v_gain,
                   ln_b=m_ln_v_bias, b_sp=m_b_spatial[0], w_sp=m_w_spatial[0])
    small_v = dict(g1=v_norm_pre_mix, g2=v_norm_post_mix, g3=v_norm_pre_mlp, g4=v_norm_post_mlp, ln_g=v_ln_v_gain,
                   ln_b=v_ln_v_bias, b_sp=v_b_spatial[0], w_sp=v_w_spatial[0])
    zero_row = jnp.zeros((1, 128), F32)
    ones_row = jnp.ones((1, 128), F32)
    sd, sm, sv = _adam_small(pkt, _pack_small(zero_row, small_w), _pack_small(zero_row, small_m),
                             _pack_small(ones_row, small_v))
    loss = pkt[0, 0]
    sg, sd, sm, sv = (_unpack_small(p) for p in (pkt, sd, sm, sv))

    order = ["g1", "w_in", "w_sp", "b_sp", "ln_g", "ln_b", "w_branch_attn", "w_branch_gmlp", "w_out", "g2", "g3",
             "w_mlp_in", "w_mlp_out", "g4"]

    def pick(which):
        out = []
        for nm in order:
            if nm in upd:
                out.append(upd[nm][which][None])
            else:
                out.append((sg, sd, sm, sv)[which][nm])
        return out

    return (loss, grad_x[None], *pick(0), *pick(1), *pick(2), *pick(3))
```

```python
import functools
import math

import jax
import jax.numpy as jnp
from jax import lax
from jax.experimental import pallas as pl
from jax.experimental.pallas import tpu as pltpu

D = 1024
HD = 64
NSLOT = 4
GW = NSLOT * HD
DILS = (1, 4, 16)
QB = 128
ATTN_NSUB = 2
AW = 3 * GW
GMLP_W = 512
CHUNK = 128
DFF = 4096
EPS = 1e-6
ROPE_THETA = 10000.0
SCALE = HD ** -0.5
NEG = -1e30

ADAM_LR = 0.001
ADAM_B1 = 0.9
ADAM_B2 = 0.999
ADAM_EPS = 1e-08
ADAM_WD = 0.01
ADAM_STEP = 10

BF = jnp.bfloat16
F32 = jnp.float32
MESH = pl.DeviceIdType.MESH
VMEM_LIMIT = 56 * 1024 * 1024

PK_LOSS, PK_G1, PK_G2, PK_G3, PK_G4, PK_LNG, PK_LNB, PK_BSP, PK_WSP, PKT_ROWS = 0, 8, 16, 24, 32, 40, 44, 48, 56, 568


def _cp(sem):
    return pltpu.CompilerParams(dimension_semantics=sem, vmem_limit_bytes=VMEM_LIMIT)


def _dot(a, b):
    return jnp.dot(a, b, preferred_element_type=F32)


def _dot_nt(a, b):
    return lax.dot_general(a, b, (((1,), (1,)), ((), ())), preferred_element_type=F32)


def _dot_tn(a, b):
    return lax.dot_general(a, b, (((0,), (0,)), ((), ())), preferred_element_type=F32)


def _gelu(x):
    return jax.nn.gelu(x, approximate=True)


def _gelu_grad(x):
    k = math.sqrt(2.0 / math.pi)
    t = jnp.tanh(k * (x + 0.044715 * x * x * x))
    return 0.5 * (1.0 + t) + 0.5 * x * (1.0 - t * t) * (k * (1.0 + 3.0 * 0.044715 * x * x))


def _swap_halves(t):
    w = t.shape[1]
    lane = lax.broadcasted_iota(jnp.int32, t.shape, 1)
    first = (lane & (HD - 1)) < (HD // 2)
    return jnp.where(first, pltpu.roll(t, w - HD // 2, 1), pltpu.roll(t, HD // 2, 1))


def _head_mask(hh):
    lane = lax.broadcasted_iota(jnp.int32, (1, GW), 1)
    return jnp.logical_and(lane >= hh * HD, lane < (hh + 1) * HD)


def _rms_stats(xf):
    return lax.rsqrt(jnp.mean(xf * xf, axis=-1, keepdims=True) + EPS)


def _rms_bwd(xf, r, gain, dout):
    n = xf * r
    t = dout * gain
    dx = r * (t - n * jnp.mean(t * n, axis=-1, keepdims=True))
    return dx, jnp.sum(dout * n, axis=0, keepdims=True)


def _rms_fwd(x, gain, name):
    S = x.shape[0]
    tm = min(512, S)

    def body(x_ref, g_ref, h_ref):
        xf = x_ref[...]
        h_ref[...] = (xf * _rms_stats(xf) * g_ref[...]).astype(BF)

    return pl.pallas_call(
        body, name=name, out_shape=jax.ShapeDtypeStruct((S, D), BF), grid=(S // tm,),
        in_specs=[pl.BlockSpec((tm, D), lambda i: (i, 0)), pl.BlockSpec((1, D), lambda i: (0, 0))],
        out_specs=pl.BlockSpec((tm, D), lambda i: (i, 0)), compiler_params=_cp(("parallel",)))(x, gain)


def _proj(h, w, epi, name, tn=256):
    S, K = h.shape
    N = w.shape[1]
    tm = min(2048, S)

    def body(h_ref, w_ref, o_ref):
        y = _dot(h_ref[...], w_ref[...])
        if epi == "sigmoid":
            y = jax.nn.sigmoid(y)
        o_ref[...] = y.astype(BF)

    return pl.pallas_call(
        body, name=name, out_shape=jax.ShapeDtypeStruct((S, N), BF), grid=(S // tm, N // tn),
        in_specs=[pl.BlockSpec((tm, K), lambda i, j: (i, 0)), pl.BlockSpec((K, tn), lambda i, j: (0, j))],
        out_specs=pl.BlockSpec((tm, tn), lambda i, j: (i, j)),
        compiler_params=_cp(("parallel", "arbitrary")))(h, w)


def _dilate_store(val, scr, o_ref, lead, d):
    rows = val.shape[0]
    if d == 1:
        o_ref[lead + (0,)] = val.astype(o_ref.dtype)
        return
    for hf in range(2):
        scr[hf, pl.ds(0, rows), :] = val[:, hf * 128:(hf + 1) * 128]
    for r in range(d):
        for hf in range(2):
            o_ref[lead + (r, slice(None), slice(hf * 128, (hf + 1) * 128))] = (
                scr[hf, pl.ds(r, rows // d, stride=d), :].astype(o_ref.dtype))


def _undilate_load(i_ref, lead, d, scr, rows):
    if d == 1:
        return i_ref[lead + (0,)].astype(F32)
    for r in range(d):
        for hf in range(2):
            scr[hf, pl.ds(r, rows // d, stride=d), :] = (
                i_ref[lead + (r, slice(None), slice(hf * 128, (hf + 1) * 128))].astype(F32))
    return jnp.concatenate([scr[0, pl.ds(0, rows), :], scr[1, pl.ds(0, rows), :]], axis=1)


def _rope_fwd(y, c_ref, s_ref):
    cosv = jnp.concatenate([c_ref[...]] * 2, axis=1)
    sinv = jnp.concatenate([s_ref[...]] * 2, axis=1)
    return y * cosv + _swap_halves(y) * sinv


def _rope_bwd(dy, c_ref, s_ref):
    cosv = jnp.concatenate([c_ref[...]] * 2, axis=1)
    sinv = jnp.concatenate([s_ref[...]] * 2, axis=1)
    return dy * cosv + _swap_halves(dy * sinv)


def _proj_qkv(h, wqkv, cos_t, sin_t, g, d, name):
    S, K = h.shape
    tm = min(2048, S)

    def body(h_ref, w_ref, c_ref, s_ref, o_ref, scr):
        j = pl.program_id(1)
        y = _dot(h_ref[...], w_ref[...])
        y = jnp.where(j < 2, _rope_fwd(y, c_ref, s_ref), y)
        _dilate_store(y, scr, o_ref, (), d)

    return pl.pallas_call(
        body, name=name, out_shape=jax.ShapeDtypeStruct((3, d, S // d, GW), BF), grid=(S // tm, 3),
        in_specs=[pl.BlockSpec((tm, K), lambda i, j: (i, 0)), pl.BlockSpec((K, GW), lambda i, j: (0, 3 * j + g)),
                  pl.BlockSpec((tm, 128), lambda i, j: (i, 0)), pl.BlockSpec((tm, 128), lambda i, j: (i, 0))],
        out_specs=pl.BlockSpec((None, d, tm // d, GW), lambda i, j: (j, 0, i, 0)),
        scratch_shapes=[pltpu.VMEM((2, tm, 128), F32)],
        compiler_params=_cp(("parallel", "arbitrary")))(h, wqkv, cos_t, sin_t)


def _band_masks(first_step):
    row = lax.broadcasted_iota(jnp.int32, (QB, 2 * QB), 0)
    col = lax.broadcasted_iota(jnp.int32, (QB, 2 * QB), 1)
    band = jnp.logical_and(col >= row, col <= row + QB)
    return band, jnp.logical_and(band, jnp.logical_or(col >= QB, jnp.logical_not(first_step)))


def _attn_fwd(qkv, name):
    _, d, L, _ = qkv.shape
    nsub = min(ATTN_NSUB, L // QB)
    R = nsub * QB
    nsteps = L // R

    def body(q_ref, kp_ref, kc_ref, vp_ref, vc_ref, o_ref, lse_ref):
        i = pl.program_id(1)
        band, band_first = _band_masks(i == 0)
        kfull = jnp.concatenate([kp_ref[...], kc_ref[...]], axis=0)
        vfull = jnp.concatenate([vp_ref[...], vc_ref[...]], axis=0)
        chains = [(sb, hh) for sb in range(nsub) for hh in range(NSLOT)]
        win = lambda t, sb: t[sb * QB:(sb + 2) * QB]
        scores = []
        for sb, hh in chains:
            qh = jnp.where(_head_mask(hh), q_ref[sb * QB:(sb + 1) * QB, :], 0)
            scores.append(_dot_nt(qh, win(kfull, sb)))
        soft = []
        for (sb, hh), sc in zip(chains, scores):
            sc = jnp.where(band_first if sb == 0 else band, sc * SCALE, NEG)
            m = jnp.max(sc, axis=1, keepdims=True)
            p = jnp.exp(sc - m)
            den = jnp.sum(p, axis=1, keepdims=True)
            soft.append((p.astype(BF), den, m + jnp.log(den)))
        accs = [_dot(p, win(vfull, sb)) for (sb, hh), (p, _, _) in zip(chains, soft)]
        for sb in range(nsub):
            o = jnp.zeros((QB, GW), F32)
            lse = jnp.zeros((QB, GW), F32)
            for hh in range(NSLOT):
                hm = _head_mask(hh)
                _, den, lrow = soft[sb * NSLOT + hh]
                o = o + jnp.where(hm, accs[sb * NSLOT + hh] / den, 0.0)
                lse = lse + jnp.where(hm, lrow, 0.0)
            o_ref[sb * QB:(sb + 1) * QB, :] = o
            lse_ref[sb * QB:(sb + 1) * QB, :] = lse

    prev = lambda i: jnp.maximum(i * nsub - 1, 0)
    cur = lambda t: pl.BlockSpec((None, None, R, GW), lambda r, i: (t, r, i, 0))
    prv = lambda t: pl.BlockSpec((None, None, QB, GW), lambda r, i: (t, r, prev(i), 0))
    out = pl.BlockSpec((None, R, GW), lambda r, i: (r, i, 0))
    return pl.pallas_call(
        body, name=name, grid=(d, nsteps),
        out_shape=(jax.ShapeDtypeStruct((d, L, GW), F32), jax.ShapeDtypeStruct((d, L, GW), F32)),
        in_specs=[cur(0), prv(1), cur(1), prv(2), cur(2)],
        out_specs=(out, out), compiler_params=_cp(("parallel", "arbitrary")))(qkv, qkv, qkv, qkv, qkv)


def _attn_combine(os_, lses):
    S = os_[0].shape[1]
    tm = min(1024, S)

    def body(o0, o1, o2, l0, l1, l2, y_ref, j0, j1, j2, scr):
        os_nat = [_undilate_load(o, (), d, scr, tm) for o, d in zip((o0, o1, o2), DILS)]
        a, b, c = [_undilate_load(l, (), d, scr, tm) for l, d in zip((l0, l1, l2), DILS)]
        m = jnp.maximum(jnp.maximum(a, b), c)
        wa, wb, wc = jnp.exp(a - m), jnp.exp(b - m), jnp.exp(c - m)
        den = wa + wb + wc
        y_ref[...] = ((wa * os_nat[0] + wb * os_nat[1] + wc * os_nat[2]) / den).astype(BF)
        lse = m + jnp.log(den)
        for j_ref, d in zip((j0, j1, j2), DILS):
            _dilate_store(lse, scr, j_ref, (), d)

    dil = lambda d: pl.BlockSpec((d, tm // d, GW), lambda i: (0, i, 0))
    dshape = lambda d: jax.ShapeDtypeStruct((d, S // d, GW), F32)
    return pl.pallas_call(
        body, name="attn_combine", grid=(S // tm,),
        out_shape=(jax.ShapeDtypeStruct((S, GW), BF),) + tuple(dshape(d) for d in DILS),
        in_specs=[dil(d) for d in DILS] * 2,
        out_specs=(pl.BlockSpec((tm, GW), lambda i: (i, 0)),) + tuple(dil(d) for d in DILS),
        scratch_shapes=[pltpu.VMEM((2, tm, 128), F32)],
        compiler_params=_cp(("parallel",)))(*os_, *lses)


def _tril(upper=False):
    row = lax.broadcasted_iota(jnp.int32, (CHUNK, CHUNK), 0)
    col = lax.broadcasted_iota(jnp.int32, (CHUNK, CHUNK), 1)
    return row <= col if upper else col <= row


def _ln_fwd(z, gain, bias):
    mu = jnp.mean(z, axis=-1, keepdims=True)
    zc = z - mu
    rstd = lax.rsqrt(jnp.mean(zc * zc, axis=-1, keepdims=True) + EPS)
    zhat = zc * rstd
    return zhat, rstd, zhat * gain + bias


def _gmlp_fwd(puz, ln_g, ln_b, w_sp, b_sp_t):
    S = puz.shape[0]
    tm = min(512, S)
    nch = tm // CHUNK

    def body(p_ref, g_ref, b_ref, w_ref, bt_ref, o_ref):
        tril = _tril()
        ws = [jnp.where(tril, w_ref[gg], 0.0).astype(BF) for gg in range(4)]
        for ch in range(nch):
            rows = slice(ch * CHUNK, (ch + 1) * CHUNK)
            z = _gelu(p_ref[rows, GMLP_W:].astype(F32))
            _, _, zn = _ln_fwd(z, g_ref[...], b_ref[...])
            zn = zn.astype(BF)
            for gg in range(4):
                cols = slice(gg * CHUNK, (gg + 1) * CHUNK)
                sz = _dot(ws[gg], zn[:, cols]) + bt_ref[:, gg:gg + 1]
                u = _gelu(p_ref[rows, cols].astype(F32))
                o_ref[rows, cols] = (u * sz).astype(BF)

    return pl.pallas_call(
        body, name="gmlp_fwd", out_shape=jax.ShapeDtypeStruct((S, GMLP_W), BF), grid=(S // tm,),
        in_specs=[pl.BlockSpec((tm, 2 * GMLP_W), lambda i: (i, 0)),
                  pl.BlockSpec((1, GMLP_W), lambda i: (0, 0)), pl.BlockSpec((1, GMLP_W), lambda i: (0, 0)),
                  pl.BlockSpec((4, CHUNK, CHUNK), lambda i: (0, 0, 0)), pl.BlockSpec((CHUNK, 4), lambda i: (0, 0))],
        out_specs=pl.BlockSpec((tm, GMLP_W), lambda i: (i, 0)),
        compiler_params=_cp(("parallel",)))(puz, ln_g, ln_b, w_sp, b_sp_t)


def _merge_fwd(ya, yg, gates, wba, wbg, wout, x, g2, g3):
    S = x.shape[0]
    tm = min(512, S)

    def body(ya_ref, yg_ref, gt_ref, wba_ref, wbg_ref, wo_ref, x_ref, g2_ref, g3_ref,
             a_ref, gb_ref, mg_ref, y_ref, x2_ref, h3_ref):
        a = _dot(ya_ref[...], wba_ref[...])
        b = _dot(yg_ref[...], wbg_ref[...])
        a_ref[...] = a.astype(BF)
        gb_ref[...] = b.astype(BF)
        merged = (gt_ref[:, :D].astype(F32) * a + gt_ref[:, D:].astype(F32) * b).astype(BF)
        mg_ref[...] = merged
        y = _dot(merged, wo_ref[...])
        y_ref[...] = y
        x2 = x_ref[...] + y * _rms_stats(y) * g2_ref[...]
        x2_ref[...] = x2
        h3_ref[...] = (x2 * _rms_stats(x2) * g3_ref[...]).astype(BF)

    row = lambda w: pl.BlockSpec((tm, w), lambda i: (i, 0))
    full = lambda s: pl.BlockSpec(s, lambda i: (0, 0))
    return pl.pallas_call(
        body, name="merge_fwd", grid=(S // tm,),
        out_shape=(jax.ShapeDtypeStruct((S, D), BF), jax.ShapeDtypeStruct((S, D), BF), jax.ShapeDtypeStruct((S, D), BF),
                   jax.ShapeDtypeStruct((S, D), F32), jax.ShapeDtypeStruct((S, D), F32), jax.ShapeDtypeStruct((S, D), BF)),
        in_specs=[row(GW), row(GMLP_W), row(2 * D), full((GW, D)), full((GMLP_W, D)), full((D, D)), row(D),
                  full((1, D)), full((1, D))],
        out_specs=(row(D), row(D), row(D), row(D), row(D), row(D)),
        compiler_params=_cp(("parallel",)))(ya, yg, gates, wba, wbg, wout, x, g2, g3)


def _mlp_fwd(h3, wmi, wmo, x2, tgt, g4):
    S = x2.shape[0]
    tm = min(1024, S)
    tf = 512
    nf = DFF // tf

    def body(h_ref, wi_ref, wo_ref, x2_ref, t_ref, g4_ref, a_ref, dy2_ref, dout_ref, loss_ref, dg4_ref, acc_ref):
        i, c = pl.program_id(0), pl.program_id(1)

        @pl.when(jnp.logical_and(i == 0, c == 0))
        def _():
            loss_ref[...] = jnp.zeros_like(loss_ref)
            dg4_ref[...] = jnp.zeros_like(dg4_ref)

        a = jnp.maximum(_dot(h_ref[...], wi_ref[...]), 0.0)
        a_ref[...] = a.astype(BF)
        part = _dot((a * a).astype(BF), wo_ref[...])

        @pl.when(c == 0)
        def _():
            acc_ref[...] = part

        @pl.when(c > 0)
        def _():
            acc_ref[...] += part

        @pl.when(c == nf - 1)
        def _():
            y2 = acc_ref[...]
            r = _rms_stats(y2)
            out = x2_ref[...] + y2 * r * g4_ref[...]
            err = out - t_ref[...]
            sq = err * err
            tot = jnp.sum(jnp.sum(sq, axis=1, keepdims=True), axis=0, keepdims=True) * (0.5 / D)
            lane = lax.broadcasted_iota(jnp.int32, (1, 128), 1)
            loss_ref[...] += jnp.where(lane == 0, tot, 0.0)
            dout = err * (1.0 / D)
            dout_ref[...] = dout
            dy2, dg = _rms_bwd(y2, r, g4_ref[...], dout)
            dy2_ref[...] = dy2.astype(BF)
            dg4_ref[...] += dg

    row = pl.BlockSpec((tm, D), lambda i, c: (i, 0))
    return pl.pallas_call(
        body, name="mlp_fwd", grid=(S // tm, nf),
        out_shape=(jax.ShapeDtypeStruct((S, DFF), BF), jax.ShapeDtypeStruct((S, D), BF), jax.ShapeDtypeStruct((S, D), F32),
                   jax.ShapeDtypeStruct((1, 128), F32), jax.ShapeDtypeStruct((1, D), F32)),
        in_specs=[row, pl.BlockSpec((D, tf), lambda i, c: (0, c)), pl.BlockSpec((tf, D), lambda i, c: (c, 0)),
                  row, row, pl.BlockSpec((1, D), lambda i, c: (0, 0))],
        out_specs=(pl.BlockSpec((tm, tf), lambda i, c: (i, c)), row, row,
                   pl.BlockSpec((1, 128), lambda i, c: (0, 0)), pl.BlockSpec((1, D), lambda i, c: (0, 0))),
        scratch_shapes=[pltpu.VMEM((tm, D), F32)],
        compiler_params=_cp(("arbitrary", "arbitrary")))(h3, wmi, wmo, x2, tgt, g4)


def _mlp_bwd(dy2, a, wmo, wmi, x2, y, dout, g2, g3):
    S = x2.shape[0]
    tm = min(512, S)
    tf = 1024
    nf = DFF // tf

    def body(dy2_ref, a_ref, wo_ref, wi_ref, x2_ref, y_ref, dout_ref, g2_ref, g3_ref,
             dpre_ref, dx2_ref, dy_ref, dg3_ref, dg2_ref, acc_ref):
        i, c = pl.program_id(0), pl.program_id(1)

        @pl.when(jnp.logical_and(i == 0, c == 0))
        def _():
            dg3_ref[...] = jnp.zeros_like(dg3_ref)
            dg2_ref[...] = jnp.zeros_like(dg2_ref)

        da2 = _dot_nt(dy2_ref[...], wo_ref[...])
        dpre = (2.0 * a_ref[...].astype(F32) * da2).astype(BF)
        dpre_ref[...] = dpre
        part = _dot_nt(dpre, wi_ref[...])

        @pl.when(c == 0)
        def _():
            acc_ref[...] = part

        @pl.when(c > 0)
        def _():
            acc_ref[...] += part

        @pl.when(c == nf - 1)
        def _():
            x2 = x2_ref[...]
            dx3, dg3 = _rms_bwd(x2, _rms_stats(x2), g3_ref[...], acc_ref[...])
            dx2 = dout_ref[...] + dx3
            dx2_ref[...] = dx2
            dg3_ref[...] += dg3
            yv = y_ref[...]
            dy, dg2 = _rms_bwd(yv, _rms_stats(yv), g2_ref[...], dx2)
            dy_ref[...] = dy.astype(BF)
            dg2_ref[...] += dg2

    row = pl.BlockSpec((tm, D), lambda i, c: (i, 0))
    vec = pl.BlockSpec((1, D), lambda i, c: (0, 0))
    return pl.pallas_call(
        body, name="mlp_bwd", grid=(S // tm, nf),
        out_shape=(jax.ShapeDtypeStruct((S, DFF), BF), jax.ShapeDtypeStruct((S, D), F32), jax.ShapeDtypeStruct((S, D), BF),
                   jax.ShapeDtypeStruct((1, D), F32), jax.ShapeDtypeStruct((1, D), F32)),
        in_specs=[row, pl.BlockSpec((tm, tf), lambda i, c: (i, c)), pl.BlockSpec((tf, D), lambda i, c: (c, 0)),
                  pl.BlockSpec((D, tf), lambda i, c: (0, c)), row, row, row, vec, vec],
        out_specs=(pl.BlockSpec((tm, tf), lambda i, c: (i, c)), row, row, vec, vec),
        scratch_shapes=[pltpu.VMEM((tm, D), F32)],
        compiler_params=_cp(("arbitrary", "arbitrary")))(dy2, a, wmo, wmi, x2, y, dout, g2, g3)


def _mm_tn(a, b, name, square_a=False, tm=1024, tn=1024, tk=2048):
    S, M = a.shape
    N = b.shape[1]
    tm, tk = min(tm, M), min(tk, S)
    tn = max(t for t in range(128, min(tn, N) + 1, 128) if N % t == 0)
    assert M % tm == 0 and S % tk == 0
    nk = S // tk

    def body(a_ref, b_ref, o_ref):
        k = pl.program_id(2)
        av = a_ref[...]
        if square_a:
            av = av * av
        part = _dot_tn(av, b_ref[...])

        @pl.when(k == 0)
        def _():
            o_ref[...] = part

        @pl.when(k > 0)
        def _():
            o_ref[...] += part

    return pl.pallas_call(
        body, name=name, out_shape=jax.ShapeDtypeStruct((M, N), F32), grid=(M // tm, N // tn, nk),
        in_specs=[pl.BlockSpec((tk, tm), lambda i, j, k: (k, i)), pl.BlockSpec((tk, tn), lambda i, j, k: (k, j))],
        out_specs=pl.BlockSpec((tm, tn), lambda i, j, k: (i, j)),
        compiler_params=_cp(("parallel", "parallel", "arbitrary")))(a, b)


def _outproj_bwd(dy, wout, abr, gbr, gates, wba, wbg, ya):
    S = dy.shape[0]
    tm = min(512, S)

    def body(dy_ref, wo_ref, a_ref, b_ref, gt_ref, wba_ref, wbg_ref, ya_ref,
             dgt_ref, da_ref, db_ref, dyg_ref, e0, e1, e2, s0, s1, s2, scr):
        dm = _dot_nt(dy_ref[...], wo_ref[...])
        ga, gb = gt_ref[:, :D].astype(F32), gt_ref[:, D:].astype(F32)
        dgt_ref[:, :D] = (dm * a_ref[...].astype(F32) * ga * (1.0 - ga)).astype(BF)
        dgt_ref[:, D:] = (dm * b_ref[...].astype(F32) * gb * (1.0 - gb)).astype(BF)
        da = (dm * ga).astype(BF)
        db = (dm * gb).astype(BF)
        da_ref[...] = da
        db_ref[...] = db
        dyg_ref[...] = _dot_nt(db, wbg_ref[...]).astype(BF)
        dya = _dot_nt(da, wba_ref[...]).astype(BF).astype(F32)
        dyy = dya * ya_ref[...].astype(F32)
        dsum = jnp.zeros((tm, GW), F32)
        for hh in range(NSLOT):
            hm = _head_mask(hh)
            dsum = dsum + jnp.where(hm, jnp.sum(jnp.where(hm, dyy, 0.0), axis=1, keepdims=True), 0.0)
        for e_ref, s_ref, d in zip((e0, e1, e2), (s0, s1, s2), DILS):
            _dilate_store(dya, scr, e_ref, (), d)
            _dilate_store(dsum, scr, s_ref, (), d)

    row = lambda w: pl.BlockSpec((tm, w), lambda i: (i, 0))
    full = lambda s: pl.BlockSpec(s, lambda i: (0, 0))
    dil = lambda d: pl.BlockSpec((d, tm // d, GW), lambda i: (0, i, 0))
    dshape = lambda d, t: jax.ShapeDtypeStruct((d, S // d, GW), t)
    return pl.pallas_call(
        body, name="outproj_bwd", grid=(S // tm,),
        out_shape=(jax.ShapeDtypeStruct((S, 2 * D), BF), jax.ShapeDtypeStruct((S, D), BF), jax.ShapeDtypeStruct((S, D), BF),
                   jax.ShapeDtypeStruct((S, GMLP_W), BF)) + tuple(dshape(d, BF) for d in DILS)
        + tuple(dshape(d, F32) for d in DILS),
        in_specs=[row(D), full((D, D)), row(D), row(D), row(2 * D), full((GW, D)), full((GMLP_W, D)), row(GW)],
        out_specs=(row(2 * D), row(D), row(D), row(GMLP_W)) + tuple(dil(d) for d in DILS) * 2,
        scratch_shapes=[pltpu.VMEM((2, tm, 128), F32)],
        compiler_params=_cp(("parallel",)))(dy, wout, abr, gbr, gates, wba, wbg, ya)


def _gmlp_bwd(puz, dyg, ln_g, ln_b, w_sp, w_sp_t, b_sp_t):
    S = puz.shape[0]
    tm = min(512, S)
    nch = tm // CHUNK

    def body(p_ref, dy_ref, g_ref, b_ref, w_ref, wt_ref, bt_ref,
             dp_ref, dw_ref, dbs_ref, dg_ref, dbias_ref, dbacc_ref):
        i = pl.program_id(0)

        @pl.when(i == 0)
        def _():
            dw_ref[...] = jnp.zeros_like(dw_ref)
            dbacc_ref[...] = jnp.zeros_like(dbacc_ref)
            dg_ref[...] = jnp.zeros_like(dg_ref)
            dbias_ref[...] = jnp.zeros_like(dbias_ref)

        tril = _tril()
        ws = [jnp.where(tril, w_ref[gg], 0.0).astype(BF) for gg in range(4)]
        triu = _tril(upper=True)
        wts = [jnp.where(triu, wt_ref[gg], 0.0).astype(BF) for gg in range(4)]
        gain = g_ref[...]
        for ch in range(nch):
            rows = slice(ch * CHUNK, (ch + 1) * CHUNK)
            pz = p_ref[rows, GMLP_W:].astype(F32)
            z = _gelu(pz)
            zhat, rstd, zn = _ln_fwd(z, gain, b_ref[...])
            znb = zn.astype(BF)
            dzn_parts = []
            for gg in range(4):
                cols = slice(gg * CHUNK, (gg + 1) * CHUNK)
                pu = p_ref[rows, cols].astype(F32)
                u = _gelu(pu)
                sz = _dot(ws[gg], znb[:, cols]) + bt_ref[:, gg:gg + 1]
                dyv = dy_ref[rows, cols].astype(F32)
                dp_ref[rows, cols] = (dyv * sz * _gelu_grad(pu)).astype(BF)
                dsz = dyv * u
                dbacc_ref[gg] += dsz
                dszb = dsz.astype(BF)
                dw_ref[gg] += _dot_nt(dszb, znb[:, cols])
                dzn_parts.append(_dot(wts[gg], dszb))
            dzn = jnp.concatenate(dzn_parts, axis=1)
            dg_ref[...] += jnp.sum(dzn * zhat, axis=0, keepdims=True)
            dbias_ref[...] += jnp.sum(dzn, axis=0, keepdims=True)
            dzh = dzn * gain
            dz = rstd * (dzh - jnp.mean(dzh, axis=-1, keepdims=True)
                         - zhat * jnp.mean(dzh * zhat, axis=-1, keepdims=True))
            dp_ref[rows, GMLP_W:] = (dz * _gelu_grad(pz)).astype(BF)

        @pl.when(i == pl.num_programs(0) - 1)
        def _():
            for gg in range(4):
                dw_ref[gg] = jnp.where(tril, dw_ref[gg], 0.0)
                dbs_ref[gg] = jnp.sum(dbacc_ref[gg], axis=1, keepdims=True)

    full2 = lambda s: pl.BlockSpec(s, lambda i: (0, 0))
    full3 = lambda s: pl.BlockSpec(s, lambda i: (0, 0, 0))
    return pl.pallas_call(
        body, name="gmlp_bwd", grid=(S // tm,),
        out_shape=(jax.ShapeDtypeStruct((S, 2 * GMLP_W), BF), jax.ShapeDtypeStruct((4, CHUNK, CHUNK), F32),
                   jax.ShapeDtypeStruct((4, CHUNK, 1), F32), jax.ShapeDtypeStruct((1, GMLP_W), F32),
                   jax.ShapeDtypeStruct((1, GMLP_W), F32)),
        in_specs=[pl.BlockSpec((tm, 2 * GMLP_W), lambda i: (i, 0)), pl.BlockSpec((tm, GMLP_W), lambda i: (i, 0)),
                  full2((1, GMLP_W)), full2((1, GMLP_W)), full3((4, CHUNK, CHUNK)), full3((4, CHUNK, CHUNK)),
                  full2((CHUNK, 4))],
        out_specs=(pl.BlockSpec((tm, 2 * GMLP_W), lambda i: (i, 0)), full3((4, CHUNK, CHUNK)), full3((4, CHUNK, 1)),
                   full2((1, GMLP_W)), full2((1, GMLP_W))),
        scratch_shapes=[pltpu.VMEM((4, CHUNK, CHUNK), F32)],
        compiler_params=_cp(("arbitrary",)))(puz, dyg, ln_g, ln_b, w_sp, w_sp_t, b_sp_t)


def _attn_bwd(qkv, dya, dsums, lse, name):
    _, d, L, _ = qkv.shape
    nsub = min(ATTN_NSUB, L // QB)
    R = nsub * QB
    nsteps = L // R

    def body(q_ref, qn_ref, kp_ref, kc_ref, vp_ref, vc_ref, dy_ref, dyn_ref, e_ref, en_ref, l_ref, ln_ref, o_ref):
        i = pl.program_id(1)
        band, band_first = _band_masks(i == 0)
        row = lax.broadcasted_iota(jnp.int32, (QB, QB), 0)
        col = lax.broadcasted_iota(jnp.int32, (QB, QB), 1)
        mask_next = jnp.logical_and(col >= row, i < nsteps - 1)
        kc, vc = kc_ref[...], vc_ref[...]
        kfull = jnp.concatenate([kp_ref[...], kc], axis=0)
        vfull = jnp.concatenate([vp_ref[...], vc], axis=0)
        k_last, v_last = kc[(nsub - 1) * QB:], vc[(nsub - 1) * QB:]
        q_ext = jnp.concatenate([q_ref[...], qn_ref[...]], axis=0)
        dy_ext = jnp.concatenate([dy_ref[...], dyn_ref[...]], axis=0)
        esum, esum_n, lse, lse_n = e_ref[...], en_ref[...], l_ref[...], ln_ref[...]
        win = lambda t, sb: t[sb * QB:(sb + 2) * QB]
        blk = lambda t, sb: t[sb * QB:(sb + 1) * QB]
        hms = [_head_mask(hh) for hh in range(NSLOT)]
        q_hs = [jnp.where(hm, q_ext, 0) for hm in hms]
        dy_hs = [jnp.where(hm, dy_ext, 0) for hm in hms]
        raw = []
        for hh in range(NSLOT):
            tiles = [(_dot_nt(blk(q_hs[hh], sb), win(kfull, sb)), _dot_nt(blk(dy_hs[hh], sb), win(vfull, sb)))
                     for sb in range(nsub)]
            tiles.append((_dot_nt(blk(q_hs[hh], nsub), k_last), _dot_nt(blk(dy_hs[hh], nsub), v_last)))
            raw.append(tiles)
        ps, dss = [], []
        for hh in range(NSLOT):
            rowstat = lambda t: jnp.max(jnp.where(hms[hh], t, -jnp.inf), axis=1, keepdims=True)
            p_h, ds_h = [], []
            for sb in range(nsub + 1):
                sc, dp = raw[hh][sb]
                if sb < nsub:
                    msk, lrow, erow = (band_first if sb == 0 else band), rowstat(blk(lse, sb)), rowstat(blk(esum, sb))
                else:
                    msk, lrow, erow = mask_next, rowstat(lse_n), rowstat(esum_n)
                p = jnp.where(msk, jnp.exp(sc * SCALE - lrow), 0.0)
                p_h.append(p.astype(BF))
                ds_h.append((p * (dp - erow)).astype(BF))
            ps.append(p_h)
            dss.append(ds_h)
        dq = [jnp.zeros((QB, GW), F32) for _ in range(nsub)]
        dk = [jnp.zeros((QB, GW), F32) for _ in range(nsub)]
        dv = [jnp.zeros((QB, GW), F32) for _ in range(nsub)]
        for hh in range(NSLOT):
            for sb in range(nsub):
                dq[sb] = dq[sb] + jnp.where(hms[hh], _dot(dss[hh][sb], win(kfull, sb)), 0.0)
                nxt = lambda t: t[sb + 1][:, :QB] if sb + 1 < nsub else t[nsub]
                dk[sb] = dk[sb] + _dot_tn(jnp.concatenate([dss[hh][sb][:, QB:], nxt(dss[hh])], axis=0), win(q_hs[hh], sb))
                dv[sb] = dv[sb] + _dot_tn(jnp.concatenate([ps[hh][sb][:, QB:], nxt(ps[hh])], axis=0), win(dy_hs[hh], sb))
        for sb in range(nsub):
            rows = slice(sb * QB, (sb + 1) * QB)
            o_ref[0, rows, :] = (dq[sb] * SCALE).astype(BF)
            o_ref[1, rows, :] = (dk[sb] * SCALE).astype(BF)
            o_ref[2, rows, :] = dv[sb].astype(BF)

    prev = lambda i: jnp.maximum(i * nsub - 1, 0)
    nxt = lambda i: jnp.minimum((i + 1) * nsub, L // QB - 1)
    cur4 = lambda t: pl.BlockSpec((None, None, R, GW), lambda r, i: (t, r, i, 0))
    prv4 = lambda t: pl.BlockSpec((None, None, QB, GW), lambda r, i: (t, r, prev(i), 0))
    nxt4 = lambda t: pl.BlockSpec((None, None, QB, GW), lambda r, i: (t, r, nxt(i), 0))
    cur3 = pl.BlockSpec((None, R, GW), lambda r, i: (r, i, 0))
    nxt3 = pl.BlockSpec((None, QB, GW), lambda r, i: (r, nxt(i), 0))
    return pl.pallas_call(
        body, name=name, grid=(d, nsteps), out_shape=jax.ShapeDtypeStruct((3, d, L, GW), BF),
        in_specs=[cur4(0), nxt4(0), prv4(1), cur4(1), prv4(2), cur4(2), cur3, nxt3, cur3, nxt3, cur3, nxt3],
        out_specs=pl.BlockSpec((3, None, R, GW), lambda r, i: (0, r, i, 0)),
        compiler_params=_cp(("parallel", "arbitrary")))(qkv, qkv, qkv, qkv, qkv, qkv, dya, dya, dsums, dsums, lse, lse)


def _inproj_bwd(dqkvs, dpuz, dgates, wqkv, wuz, wg, cos_t, sin_t, x, dx2, g1):
    S = x.shape[0]
    tm = min(512, S)

    def body(d0_ref, d1_ref, d2_ref, dp_ref, dg_ref, wqkv_ref, wuz_ref, wg_ref, c_ref, s_ref,
             x_ref, dx2_ref, g1_ref, gx_ref, dg1_ref, dn_ref, scr):
        i = pl.program_id(0)

        @pl.when(i == 0)
        def _():
            dg1_ref[...] = jnp.zeros_like(dg1_ref)

        for t in range(3):
            for g, (d_ref, d) in enumerate(zip((d0_ref, d1_ref, d2_ref), DILS)):
                piece = _undilate_load(d_ref, (t,), d, scr, tm)
                if t < 2:
                    piece = _rope_bwd(piece, c_ref, s_ref)
                dn_ref[:, (3 * t + g) * GW:(3 * t + g + 1) * GW] = piece.astype(BF)
        dh = (_dot_nt(dn_ref[...], wqkv_ref[...]) + _dot_nt(dp_ref[...], wuz_ref[...])
              + _dot_nt(dg_ref[...], wg_ref[...]))
        xv = x_ref[...]
        dx1, dg1 = _rms_bwd(xv, _rms_stats(xv), g1_ref[...], dh)
        gx_ref[...] = dx2_ref[...] + dx1
        dg1_ref[...] += dg1

    row = lambda w: pl.BlockSpec((tm, w), lambda i: (i, 0))
    full = lambda s: pl.BlockSpec(s, lambda i: (0, 0))
    dil = lambda d: pl.BlockSpec((3, d, tm // d, GW), lambda i: (0, 0, i, 0))
    return pl.pallas_call(
        body, name="inproj_bwd", grid=(S // tm,),
        out_shape=(jax.ShapeDtypeStruct((S, D), F32), jax.ShapeDtypeStruct((1, D), F32),
                   jax.ShapeDtypeStruct((S, 3 * AW), BF)),
        in_specs=[dil(d) for d in DILS] + [row(2 * GMLP_W), row(2 * D), full((D, 3 * AW)), full((D, 2 * GMLP_W)),
                                            full((D, 2 * D)), row(128), row(128), row(D), row(D), full((1, D))],
        out_specs=(row(D), full((1, D)), row(3 * AW)),
        scratch_shapes=[pltpu.VMEM((2, tm, 128), F32)],
        compiler_params=_cp(("arbitrary",)))(*dqkvs, dpuz, dgates, wqkv, wuz, wg, cos_t, sin_t, x, dx2, g1)


def _adam_math(w, g, m, v):
    m2 = ADAM_B1 * m + (1.0 - ADAM_B1) * g
    v2 = ADAM_B2 * v + (1.0 - ADAM_B2) * (g * g)
    m_hat = m2 / (1.0 - ADAM_B1 ** ADAM_STEP)
    v_hat = v2 / (1.0 - ADAM_B2 ** ADAM_STEP)
    delta = -ADAM_LR * (m_hat / (jnp.sqrt(v_hat) + ADAM_EPS) + ADAM_WD * w)
    return delta, m2, v2


def _adam_shard(own, recv, w, m, v, name):
    R, C = w.shape
    tr = min(256, R)

    def body(own_ref, r_ref, w_ref, m_ref, v_ref, g_ref, d_ref, m2_ref, v2_ref):
        g = own_ref[...] + r_ref[0].astype(F32) + r_ref[1].astype(F32) + r_ref[2].astype(F32)
        g_ref[...] = g
        d_ref[...], m2_ref[...], v2_ref[...] = _adam_math(w_ref[...], g, m_ref[...], v_ref[...])

    spec = pl.BlockSpec((tr, C), lambda i: (i, 0))
    out = jax.ShapeDtypeStruct((R, C), F32)
    return pl.pallas_call(
        body, name=name, grid=(R // tr,), out_shape=(out, out, out, out),
        in_specs=[spec, pl.BlockSpec((3, tr, C), lambda i: (0, i, 0)), spec, spec, spec],
        out_specs=(spec, spec, spec, spec), compiler_params=_cp(("parallel",)))(own, recv, w, m, v)


def _adam_small(gp, wp, mp, vp):
    def body(g_ref, w_ref, m_ref, v_ref, d_ref, m2_ref, v2_ref):
        d_ref[...], m2_ref[...], v2_ref[...] = _adam_math(w_ref[...], g_ref[...], m_ref[...], v_ref[...])

    out = jax.ShapeDtypeStruct(gp.shape, F32)
    return pl.pallas_call(body, name="adam_small", out_shape=(out, out, out),
                          compiler_params=pltpu.CompilerParams(vmem_limit_bytes=VMEM_LIMIT))(gp, wp, mp, vp)


def _rs_add(gblocks, recv, idx, name):
    _, R, C = gblocks.shape
    tr = min(256, R)

    def body(t_ref, g_ref, r_ref, own_ref, send_ref):
        j = pl.program_id(1)
        s = g_ref[...] + r_ref[...]

        @pl.when(j == 0)
        def _():
            own_ref[...] = s

        @pl.when(j > 0)
        def _():
            send_ref[...] = s.astype(BF)

    grid_spec = pltpu.PrefetchScalarGridSpec(
        num_scalar_prefetch=1, grid=(R // tr, 4),
        in_specs=[pl.BlockSpec((None, tr, C), lambda i, j, t: (t[j], i, 0)),
                  pl.BlockSpec((None, tr, C), lambda i, j, t: (t[4 + j], i, 0))],
        out_specs=[pl.BlockSpec((tr, C), lambda i, j, t: (i, 0)),
                   pl.BlockSpec((None, tr, C), lambda i, j, t: (jnp.maximum(j - 1, 0), i, 0))])
    return pl.pallas_call(
        body, name=name, grid_spec=grid_spec,
        out_shape=(jax.ShapeDtypeStruct((R, C), F32), jax.ShapeDtypeStruct((3, R, C), BF)),
        compiler_params=_cp(("parallel", "arbitrary")))(idx, gblocks, recv)


def _mesh_pos():
    return lax.axis_index("x"), lax.axis_index("y"), lax.axis_index("c")


def _all_gather_weights(shards):
    n = len(shards)

    def body(*refs):
        ins, outs = refs[:n], refs[n:2 * n]
        send_sems, recv_sems, loc_sems = refs[2 * n:]
        x, y, c = _mesh_pos()
        me, sib = (x, y, c), (x, y, 1 - c)
        chips = [(1 - x, y), (x, 1 - y), (1 - x, 1 - y)]

        def rcopy(k, s, block, to, src=None):
            dst = outs[k].at[4 * block[0] + 2 * block[1] + block[2]]
            return pltpu.make_async_remote_copy(
                src_ref=dst if src is None else src, dst_ref=dst, send_sem=send_sems.at[k, s],
                recv_sem=recv_sems.at[k, s], device_id=to, device_id_type=MESH)

        locs = [pltpu.make_async_copy(ins[k], outs[k].at[4 * x + 2 * y + c], loc_sems.at[k]) for k in range(n)]
        for cp in locs:
            cp.start()
        started = []
        for k in range(n):
            for j, chip in enumerate(chips):
                started.append(rcopy(k, 1 + j, me, (*chip, c), src=ins[k]))
                started[-1].start()
        for k in range(n):
            started.append(rcopy(k, 0, me, sib, src=ins[k]))
            started[-1].start()
        for k in range(n):
            for j, chip in enumerate(chips):
                rcopy(k, 1 + j, (*chip, c), me).wait_recv()
                started.append(rcopy(k, 4 + j, (*chip, c), sib))
                started[-1].start()
        for k in range(n):
            rcopy(k, 0, sib, me).wait_recv()
            for j, chip in enumerate(chips):
                rcopy(k, 4 + j, (*chip, 1 - c), me).wait_recv()
        for cp in started:
            cp.wait_send()
        for cp in locs:
            cp.wait()

    hbm = pl.BlockSpec(memory_space=pl.ANY)
    return pl.pallas_call(
        body, name="ag_weights",
        out_shape=tuple(jax.ShapeDtypeStruct((8,) + s.shape, s.dtype) for s in shards),
        in_specs=[hbm] * n, out_specs=(hbm,) * n,
        scratch_shapes=[pltpu.SemaphoreType.DMA((n, 7)), pltpu.SemaphoreType.DMA((n, 7)),
                        pltpu.SemaphoreType.DMA((n,))])(*shards)


def _rs_d2d(gblocks):
    n = len(gblocks)

    def body(*refs):
        ins, outs = refs[:n], refs[n:2 * n]
        send_sems, recv_sems = refs[2 * n:]
        x, y, c = _mesh_pos()
        copies = []
        for k in range(n):
            for kk in range(4):
                copies.append(pltpu.make_async_remote_copy(
                    src_ref=ins[k].at[2 * kk + 1 - c], dst_ref=outs[k].at[kk], send_sem=send_sems.at[k, kk],
                    recv_sem=recv_sems.at[k, kk], device_id=(x, y, 1 - c), device_id_type=MESH))
                copies[-1].start()
        for cp in copies:
            cp.wait()

    hbm = pl.BlockSpec(memory_space=pl.ANY)
    return pl.pallas_call(
        body, name="rs_d2d",
        out_shape=tuple(jax.ShapeDtypeStruct((4,) + g.shape[1:], g.dtype) for g in gblocks),
        in_specs=[hbm] * n, out_specs=(hbm,) * n,
        scratch_shapes=[pltpu.SemaphoreType.DMA((n, 4)), pltpu.SemaphoreType.DMA((n, 4))])(*gblocks)


def _rs_ici(sends):
    n = len(sends)

    def body(*refs):
        ins, outs = refs[:n], refs[n:2 * n]
        send_sems, recv_sems = refs[2 * n:]
        x, y, c = _mesh_pos()
        chips = [(1 - x, y), (x, 1 - y), (1 - x, 1 - y)]
        copies = []
        for k in range(n):
            for j, chip in enumerate(chips):
                copies.append(pltpu.make_async_remote_copy(
                    src_ref=ins[k].at[j], dst_ref=outs[k].at[j], send_sem=send_sems.at[k, j],
                    recv_sem=recv_sems.at[k, j], device_id=(*chip, c), device_id_type=MESH))
                copies[-1].start()
        for cp in copies:
            cp.wait()

    hbm = pl.BlockSpec(memory_space=pl.ANY)
    return pl.pallas_call(
        body, name="rs_ici",
        out_shape=tuple(jax.ShapeDtypeStruct(s.shape, s.dtype) for s in sends),
        in_specs=[hbm] * n, out_specs=(hbm,) * n,
        scratch_shapes=[pltpu.SemaphoreType.DMA((n, 3)), pltpu.SemaphoreType.DMA((n, 3))])(*sends)


def _small_allreduce(pkt):
    R = pkt.shape[0]

    def body(p_ref, o_ref, buf, send_sems, recv_sems):
        x, y, c = _mesh_pos()
        me, sib = (x, y, c), (x, y, 1 - c)
        chips = [(1 - x, y), (x, 1 - y), (1 - x, 1 - y)]

        def rcopy(s, block, to, src=None):
            dst = buf.at[4 * block[0] + 2 * block[1] + block[2]]
            return pltpu.make_async_remote_copy(
                src_ref=dst if src is None else src, dst_ref=dst, send_sem=send_sems.at[s],
                recv_sem=recv_sems.at[s], device_id=to, device_id_type=MESH)

        buf[4 * x + 2 * y + c] = p_ref[...]
        started = [rcopy(0, me, sib, src=p_ref)]
        started += [rcopy(1 + j, me, (*chip, c), src=p_ref) for j, chip in enumerate(chips)]
        for cp in started:
            cp.start()
        for j, chip in enumerate(chips):
            rcopy(1 + j, (*chip, c), me).wait_recv()
            started.append(rcopy(4 + j, (*chip, c), sib))
            started[-1].start()
        rcopy(0, sib, me).wait_recv()
        for j, chip in enumerate(chips):
            rcopy(4 + j, (*chip, 1 - c), me).wait_recv()
        for cp in started:
            cp.wait_send()
        acc = buf[0]
        for b in range(1, 8):
            acc = acc + buf[b]
        o_ref[...] = acc

    return pl.pallas_call(
        body, name="small_allreduce", out_shape=jax.ShapeDtypeStruct(pkt.shape, F32),
        in_specs=[pl.BlockSpec(memory_space=pltpu.VMEM)], out_specs=pl.BlockSpec(memory_space=pltpu.VMEM),
        scratch_shapes=[pltpu.VMEM((8, R, 128), F32), pltpu.SemaphoreType.DMA((7,)), pltpu.SemaphoreType.DMA((7,))],
        compiler_params=pltpu.CompilerParams(vmem_limit_bytes=VMEM_LIMIT))(pkt)


def _rope_tables(S):
    half = HD // 2
    inv_freq = ROPE_THETA ** (-jnp.arange(half, dtype=F32) / half)
    ang = jnp.arange(S, dtype=F32)[:, None] * inv_freq[None, :]
    cos, sin = jnp.cos(ang), jnp.sin(ang)
    return jnp.concatenate([cos, cos, cos, cos], axis=1), jnp.concatenate([-sin, sin, -sin, sin], axis=1)


def _local_step(x, tgt, wqkv, wuz, wg, wba, wbg, wout, wmi, wmo, g1, g2, g3, g4, ln_g, ln_b, w_sp, b_sp):
    S = x.shape[0]
    cos_t, sin_t = _rope_tables(S)
    b_sp_t = b_sp.T
    w_sp_t = w_sp.transpose(0, 2, 1)

    h1 = _rms_fwd(x, g1, "rms_pre_mix")
    qkvs = [_proj_qkv(h1, wqkv, cos_t, sin_t, g, d, "proj_qkv_d%d" % d) for g, d in enumerate(DILS)]
    puz = _proj(h1, wuz, "none", "proj_uz")
    gates = _proj(h1, wg, "sigmoid", "proj_gates")
    fwd = [_attn_fwd(t, "attn_fwd_d%d" % d) for t, d in zip(qkvs, DILS)]
    ya, *lses = _attn_combine([o for o, _ in fwd], [l for _, l in fwd])
    yg = _gmlp_fwd(puz, ln_g, ln_b, w_sp, b_sp_t)
    abr, gbr, merged, y, x2, h3 = _merge_fwd(ya, yg, gates, wba, wbg, wout, x, g2, g3)
    a, dy2, dout, loss_row, dg4 = _mlp_fwd(h3, wmi, wmo, x2, tgt, g4)

    dpre, dx2, dy, dg3, dg2 = _mlp_bwd(dy2, a, wmo, wmi, x2, y, dout, g2, g3)
    dwmo = _mm_tn(a, dy2, "dw_mlp_out", square_a=True)
    dwmi = _mm_tn(h3, dpre, "dw_mlp_in")
    dgates, da, db, dyg, *rest = _outproj_bwd(dy, wout, abr, gbr, gates, wba, wbg, ya)
    dyas, dsums = rest[:3], rest[3:]
    dwout = _mm_tn(merged, dy, "dw_out")
    dwba = _mm_tn(ya, da, "dw_branch_attn")
    dwbg = _mm_tn(yg, db, "dw_branch_gmlp")
    dpuz, dwsp, dbs, dlng, dlnb = _gmlp_bwd(puz, dyg, ln_g, ln_b, w_sp, w_sp_t, b_sp_t)
    dqkvs = [_attn_bwd(qkvs[g], dyas[g], dsums[g], lses[g], "attn_bwd_d%d" % d) for g, d in enumerate(DILS)]
    grad_x, dg1, dqkv = _inproj_bwd(dqkvs, dpuz, dgates, wqkv, wuz, wg, cos_t, sin_t, x, dx2, g1)
    dwin = jnp.concatenate([_mm_tn(h1, t, "dw_in_%d" % n) for n, t in enumerate((dqkv, dpuz, dgates))], axis=1)
    small = dict(g1=dg1, g2=dg2, g3=dg3, g4=dg4, ln_g=dlng, ln_b=dlnb, b_sp=dbs.reshape(4, CHUNK), w_sp=dwsp)
    return loss_row, grad_x, (dwin, dwba, dwbg, dwout, dwmi, dwmo), small


def _pack_small(first_row, t):
    pad = lambda a, rows: jnp.concatenate([a, jnp.zeros((rows - a.shape[0], 128), F32)], axis=0)
    return jnp.concatenate([
        pad(first_row, 8), t["g1"].reshape(8, 128), t["g2"].reshape(8, 128), t["g3"].reshape(8, 128),
        t["g4"].reshape(8, 128), t["ln_g"].reshape(4, 128), t["ln_b"].reshape(4, 128),
        pad(t["b_sp"].reshape(4, 128), 8), t["w_sp"].reshape(4 * CHUNK, 128)], axis=0)


def _unpack_small(p):
    vec = lambda r, n: p[r:r + n // 128].reshape(1, n)
    return dict(g1=vec(PK_G1, D), g2=vec(PK_G2, D), g3=vec(PK_G3, D), g4=vec(PK_G4, D),
                ln_g=vec(PK_LNG, GMLP_W), ln_b=vec(PK_LNB, GMLP_W),
                b_sp=p[PK_BSP:PK_BSP + 4].reshape(1, 4, CHUNK), w_sp=p[PK_WSP:PKT_ROWS].reshape(1, 4, CHUNK, CHUNK))


def kernel(x, norm_pre_mix, w_in, w_spatial, b_spatial, ln_v_gain, ln_v_bias, w_branch_attn, w_branch_gmlp, w_out, norm_post_mix, norm_pre_mlp, w_mlp_in, w_mlp_out, norm_post_mlp, loss_target, m_norm_pre_mix, m_w_in, m_w_spatial, m_b_spatial, m_ln_v_gain, m_ln_v_bias, m_w_branch_attn, m_w_branch_gmlp, m_w_out, m_norm_post_mix, m_norm_pre_mlp, m_w_mlp_in, m_w_mlp_out, m_norm_post_mlp, v_norm_pre_mix, v_w_in, v_w_spatial, v_b_spatial, v_ln_v_gain, v_ln_v_bias, v_w_branch_attn, v_w_branch_gmlp, v_w_out, v_norm_post_mix, v_norm_pre_mlp, v_w_mlp_in, v_w_mlp_out, v_norm_post_mlp):
    mx, my, mc = _mesh_pos()

    big_w = [w_in[0], w_branch_attn[0], w_branch_gmlp[0], w_out[0], w_mlp_in[0], w_mlp_out[0]]
    big_m = [m_w_in[0], m_w_branch_attn[0], m_w_branch_gmlp[0], m_w_out[0], m_w_mlp_in[0], m_w_mlp_out[0]]
    big_v = [v_w_in[0], v_w_branch_attn[0], v_w_branch_gmlp[0], v_w_out[0], v_w_mlp_in[0], v_w_mlp_out[0]]
    col_sharded = [True, True, True, False, True, False]
    gathered = _all_gather_weights([w.astype(BF) for w in big_w])
    full = [t.transpose(1, 0, 2).reshape(t.shape[1], 8 * t.shape[2]) if cs else t.reshape(8 * t.shape[1], t.shape[2])
            for t, cs in zip(gathered, col_sharded)]
    win, wba, wbg, wout, wmi, wmo = full
    wqkv, wuz, wg = win[:, :3 * AW], win[:, 3 * AW:3 * AW + 2 * GMLP_W], win[:, 3 * AW + 2 * GMLP_W:]

    loss_row, grad_x, dws, small = _local_step(
        x[0], loss_target[0], wqkv, wuz, wg, wba, wbg, wout, wmi, wmo,
        norm_pre_mix, norm_post_mix, norm_pre_mlp, norm_post_mlp, ln_v_gain, ln_v_bias, w_spatial[0], b_spatial[0])

    gblocks = [g.reshape(g.shape[0], 8, g.shape[1] // 8).transpose(1, 0, 2) if cs
               else g.reshape(8, g.shape[0] // 8, g.shape[1]) for g, cs in zip(dws, col_sharded)]
    from_sib = _rs_d2d(gblocks)
    rel = [(0, 0), (1, 0), (0, 1), (1, 1)]
    chip_of = [2 * (mx ^ dx) + (my ^ dy) for dx, dy in rel]
    idx = jnp.stack([2 * k + mc for k in chip_of] + chip_of).astype(jnp.int32)
    halves = [_rs_add(g, r, idx, "rs_add_%d" % n) for n, (g, r) in enumerate(zip(gblocks, from_sib))]
    from_chips = _rs_ici([s for _, s in halves])
    names = ["w_in", "w_branch_attn", "w_branch_gmlp", "w_out", "w_mlp_in", "w_mlp_out"]
    upd = {nm: _adam_shard(own, r, w, m, v, "adam_" + nm)
           for nm, (own, _), r, w, m, v in zip(names, halves, from_chips, big_w, big_m, big_v)}

    pkt = _small_allreduce(_pack_small(loss_row, small))
    small_w = dict(g1=norm_pre_mix, g2=norm_post_mix, g3=norm_pre_mlp, g4=norm_post_mlp, ln_g=ln_v_gain,
                   ln_b=ln_v_bias, b_sp=b_spatial[0], w_sp=w_spatial[0])
    small_m = dict(g1=m_norm_pre_mix, g2=m_norm_post_mix, g3=m_norm_pre_mlp, g4=m_norm_post_mlp, ln_g=m_ln_v_gain,
                   ln_b=m_ln_v_bias, b_sp=m_b_spatial[0], w_sp=m_w_spatial[0])
    small_v = dict(g1=v_norm_pre_mix, g2=v_norm_post_mix, g3=v_norm_pre_mlp, g4=v_norm_post_mlp, ln_g=v_ln_v_gain,
                   ln_b=v_ln_v_bias, b_sp=v_b_spatial[0], w_sp=v_w_spatial[0])
    zero_row = jnp.zeros((1, 128), F32)
    ones_row = jnp.ones((1, 128), F32)
    sd, sm, sv = _adam_small(pkt, _pack_small(zero_row, small_w), _pack_small(zero_row, small_m),
                             _pack_small(ones_row, small_v))
    loss = pkt[0, 0]
    sg, sd, sm, sv = (_unpack_small(p) for p in (pkt, sd, sm, sv))

    order = ["g1", "w_in", "w_sp", "b_sp", "ln_g", "ln_b", "w_branch_attn", "w_branch_gmlp", "w_out", "g2", "g3",
             "w_mlp_in", "w_mlp_out", "g4"]

    def pick(which):
        out = []
        for nm in order:
            if nm in upd:
                out.append(upd[nm][which][None])
            else:
                out.append((sg, sd, sm, sv)[which][nm])
        return out

    return (loss, grad_x[None], *pick(0), *pick(1), *pick(2), *pick(3))
```

```python
import functools
import math

import jax
import jax.numpy as jnp
from jax import lax
from jax.experimental import pallas as pl
from jax.experimental.pallas import tpu as pltpu

D = 1024
HD = 64
NSLOT = 4
GW = NSLOT * HD
DILS = (1, 4, 16)
QB = 128
ATTN_NSUB = 2
AW = 3 * GW
GMLP_W = 512
CHUNK = 128
DFF = 4096
EPS = 1e-6
ROPE_THETA = 10000.0
SCALE = HD ** -0.5
NEG = -1e30

ADAM_LR = 0.001
ADAM_B1 = 0.9
ADAM_B2 = 0.999
ADAM_EPS = 1e-08
ADAM_WD = 0.01
ADAM_STEP = 10

BF = jnp.bfloat16
F32 = jnp.float32
MESH = pl.DeviceIdType.MESH
VMEM_LIMIT = 56 * 1024 * 1024

PK_LOSS, PK_G1, PK_G2, PK_G3, PK_G4, PK_LNG, PK_LNB, PK_BSP, PK_WSP, PKT_ROWS = 0, 8, 16, 24, 32, 40, 44, 48, 56, 568


def _cp(sem):
    return pltpu.CompilerParams(dimension_semantics=sem, vmem_limit_bytes=VMEM_LIMIT)


def _dot(a, b):
    return jnp.dot(a, b, preferred_element_type=F32)


def _dot_nt(a, b):
    return lax.dot_general(a, b, (((1,), (1,)), ((), ())), preferred_element_type=F32)


def _dot_tn(a, b):
    return lax.dot_general(a, b, (((0,), (0,)), ((), ())), preferred_element_type=F32)


def _gelu(x):
    return jax.nn.gelu(x, approximate=True)


def _gelu_grad(x):
    k = math.sqrt(2.0 / math.pi)
    t = jnp.tanh(k * (x + 0.044715 * x * x * x))
    return 0.5 * (1.0 + t) + 0.5 * x * (1.0 - t * t) * (k * (1.0 + 3.0 * 0.044715 * x * x))


def _swap_halves(t):
    w = t.shape[1]
    lane = lax.broadcasted_iota(jnp.int32, t.shape, 1)
    first = (lane & (HD - 1)) < (HD // 2)
    return jnp.where(first, pltpu.roll(t, w - HD // 2, 1), pltpu.roll(t, HD // 2, 1))


def _head_mask(hh):
    lane = lax.broadcasted_iota(jnp.int32, (1, GW), 1)
    return jnp.logical_and(lane >= hh * HD, lane < (hh + 1) * HD)


def _rms_stats(xf):
    return lax.rsqrt(jnp.mean(xf * xf, axis=-1, keepdims=True) + EPS)


def _rms_bwd(xf, r, gain, dout):
    n = xf * r
    t = dout * gain
    dx = r * (t - n * jnp.mean(t * n, axis=-1, keepdims=True))
    return dx, jnp.sum(dout * n, axis=0, keepdims=True)


def _rms_fwd(x, gain, name):
    S = x.shape[0]
    tm = min(512, S)

    def body(x_ref, g_ref, h_ref):
        xf = x_ref[...]
        h_ref[...] = (xf * _rms_stats(xf) * g_ref[...]).astype(BF)

    return pl.pallas_call(
        body, name=name, out_shape=jax.ShapeDtypeStruct((S, D), BF), grid=(S // tm,),
        in_specs=[pl.BlockSpec((tm, D), lambda i: (i, 0)), pl.BlockSpec((1, D), lambda i: (0, 0))],
        out_specs=pl.BlockSpec((tm, D), lambda i: (i, 0)), compiler_params=_cp(("parallel",)))(x, gain)


def _proj(h, w, epi, name, tn=256):
    S, K = h.shape
    N = w.shape[1]
    tm = min(2048, S)

    def body(h_ref, w_ref, o_ref):
        y = _dot(h_ref[...], w_ref[...])
        if epi == "sigmoid":
            y = jax.nn.sigmoid(y)
        o_ref[...] = y.astype(BF)

    return pl.pallas_call(
        body, name=name, out_shape=jax.ShapeDtypeStruct((S, N), BF), grid=(S // tm, N // tn),
        in_specs=[pl.BlockSpec((tm, K), lambda i, j: (i, 0)), pl.BlockSpec((K, tn), lambda i, j: (0, j))],
        out_specs=pl.BlockSpec((tm, tn), lambda i, j: (i, j)),
        compiler_params=_cp(("parallel", "arbitrary")))(h, w)


def _dilate_store(val, scr, o_ref, lead, d):
    rows = val.shape[0]
    if d == 1:
        o_ref[lead + (0,)] = val.astype(o_ref.dtype)
        return
    for hf in range(2):
        scr[hf, pl.ds(0, rows), :] = val[:, hf * 128:(hf + 1) * 128]
    for r in range(d):
        for hf in range(2):
            o_ref[lead + (r, slice(None), slice(hf * 128, (hf + 1) * 128))] = (
                scr[hf, pl.ds(r, rows // d, stride=d), :].astype(o_ref.dtype))


def _undilate_load(i_ref, lead, d, scr, rows):
    if d == 1:
        return i_ref[lead + (0,)].astype(F32)
    for r in range(d):
        for hf in range(2):
            scr[hf, pl.ds(r, rows // d, stride=d), :] = (
                i_ref[lead + (r, slice(None), slice(hf * 128, (hf + 1) * 128))].astype(F32))
    return jnp.concatenate([scr[0, pl.ds(0, rows), :], scr[1, pl.ds(0, rows), :]], axis=1)


def _rope_fwd(y, c_ref, s_ref):
    cosv = jnp.concatenate([c_ref[...]] * 2, axis=1)
    sinv = jnp.concatenate([s_ref[...]] * 2, axis=1)
    return y * cosv + _swap_halves(y) * sinv


def _rope_bwd(dy, c_ref, s_ref):
    cosv = jnp.concatenate([c_ref[...]] * 2, axis=1)
    sinv = jnp.concatenate([s_ref[...]] * 2, axis=1)
    return dy * cosv + _swap_halves(dy * sinv)


def _proj_qkv(h, wqkv, cos_t, sin_t, g, d, name):
    S, K = h.shape
    tm = min(2048, S)

    def body(h_ref, w_ref, c_ref, s_ref, o_ref, scr):
        j = pl.program_id(1)
        y = _dot(h_ref[...], w_ref[...])
        y = jnp.where(j < 2, _rope_fwd(y, c_ref, s_ref), y)
        _dilate_store(y, scr, o_ref, (), d)

    return pl.pallas_call(
        body, name=name, out_shape=jax.ShapeDtypeStruct((3, d, S // d, GW), BF), grid=(S // tm, 3),
        in_specs=[pl.BlockSpec((tm, K), lambda i, j: (i, 0)), pl.BlockSpec((K, GW), lambda i, j: (0, 3 * j + g)),
                  pl.BlockSpec((tm, 128), lambda i, j: (i, 0)), pl.BlockSpec((tm, 128), lambda i, j: (i, 0))],
        out_specs=pl.BlockSpec((None, d, tm // d, GW), lambda i, j: (j, 0, i, 0)),
        scratch_shapes=[pltpu.VMEM((2, tm, 128), F32)],
        compiler_params=_cp(("parallel", "arbitrary")))(h, wqkv, cos_t, sin_t)


def _band_masks(first_step):
    row = lax.broadcasted_iota(jnp.int32, (QB, 2 * QB), 0)
    col = lax.broadcasted_iota(jnp.int32, (QB, 2 * QB), 1)
    band = jnp.logical_and(col >= row, col <= row + QB)
    return band, jnp.logical_and(band, jnp.logical_or(col >= QB, jnp.logical_not(first_step)))


def _attn_fwd(qkv, name):
    _, d, L, _ = qkv.shape
    nsub = min(ATTN_NSUB, L // QB)
    R = nsub * QB
    nsteps = L // R

    def body(q_ref, kp_ref, kc_ref, vp_ref, vc_ref, o_ref, lse_ref):
        i = pl.program_id(1)
        band, band_first = _band_masks(i == 0)
        kfull = jnp.concatenate([kp_ref[...], kc_ref[...]], axis=0)
        vfull = jnp.concatenate([vp_ref[...], vc_ref[...]], axis=0)
        chains = [(sb, hh) for sb in range(nsub) for hh in range(NSLOT)]
        win = lambda t, sb: t[sb * QB:(sb + 2) * QB]
        scores = []
        for sb, hh in chains:
            qh = jnp.where(_head_mask(hh), q_ref[sb * QB:(sb + 1) * QB, :], 0)
            scores.append(_dot_nt(qh, win(kfull, sb)))
        soft = []
        for (sb, hh), sc in zip(chains, scores):
            sc = jnp.where(band_first if sb == 0 else band, sc * SCALE, NEG)
            m = jnp.max(sc, axis=1, keepdims=True)
            p = jnp.exp(sc - m)
            den = jnp.sum(p, axis=1, keepdims=True)
            soft.append((p.astype(BF), den, m + jnp.log(den)))
        accs = [_dot(p, win(vfull, sb)) for (sb, hh), (p, _, _) in zip(chains, soft)]
        for sb in range(nsub):
            o = jnp.zeros((QB, GW), F32)
            lse = jnp.zeros((QB, GW), F32)
            for hh in range(NSLOT):
                hm = _head_mask(hh)
                _, den, lrow = soft[sb * NSLOT + hh]
                o = o + jnp.where(hm, accs[sb * NSLOT + hh] / den, 0.0)
                lse = lse + jnp.where(hm, lrow, 0.0)
            o_ref[sb * QB:(sb + 1) * QB, :] = o
            lse_ref[sb * QB:(sb + 1) * QB, :] = lse

    prev = lambda i: jnp.maximum(i * nsub - 1, 0)
    cur = lambda t: pl.BlockSpec((None, None, R, GW), lambda r, i: (t, r, i, 0))
    prv = lambda t: pl.BlockSpec((None, None, QB, GW), lambda r, i: (t, r, prev(i), 0))
    out = pl.BlockSpec((None, R, GW), lambda r, i: (r, i, 0))
    return pl.pallas_call(
        body, name=name, grid=(d, nsteps),
        out_shape=(jax.ShapeDtypeStruct((d, L, GW), F32), jax.ShapeDtypeStruct((d, L, GW), F32)),
        in_specs=[cur(0), prv(1), cur(1), prv(2), cur(2)],
        out_specs=(out, out), compiler_params=_cp(("parallel", "arbitrary")))(qkv, qkv, qkv, qkv, qkv)


def _attn_combine(os_, lses):
    S = os_[0].shape[1]
    tm = min(1024, S)

    def body(o0, o1, o2, l0, l1, l2, y_ref, j0, j1, j2, scr):
        os_nat = [_undilate_load(o, (), d, scr, tm) for o, d in zip((o0, o1, o2), DILS)]
        a, b, c = [_undilate_load(l, (), d, scr, tm) for l, d in zip((l0, l1, l2), DILS)]
        m = jnp.maximum(jnp.maximum(a, b), c)
        wa, wb, wc = jnp.exp(a - m), jnp.exp(b - m), jnp.exp(c - m)
        den = wa + wb + wc
        y_ref[...] = ((wa * os_nat[0] + wb * os_nat[1] + wc * os_nat[2]) / den).astype(BF)
        lse = m + jnp.log(den)
        for j_ref, d in zip((j0, j1, j2), DILS):
            _dilate_store(lse, scr, j_ref, (), d)

    dil = lambda d: pl.BlockSpec((d, tm // d, GW), lambda i: (0, i, 0))
    dshape = lambda d: jax.ShapeDtypeStruct((d, S // d, GW), F32)
    return pl.pallas_call(
        body, name="attn_combine", grid=(S // tm,),
        out_shape=(jax.ShapeDtypeStruct((S, GW), BF),) + tuple(dshape(d) for d in DILS),
        in_specs=[dil(d) for d in DILS] * 2,
        out_specs=(pl.BlockSpec((tm, GW), lambda i: (i, 0)),) + tuple(dil(d) for d in DILS),
        scratch_shapes=[pltpu.VMEM((2, tm, 128), F32)],
        compiler_params=_cp(("parallel",)))(*os_, *lses)


def _tril(upper=False):
    row = lax.broadcasted_iota(jnp.int32, (CHUNK, CHUNK), 0)
    col = lax.broadcasted_iota(jnp.int32, (CHUNK, CHUNK), 1)
    return row <= col if upper else col <= row


def _ln_fwd(z, gain, bias):
    mu = jnp.mean(z, axis=-1, keepdims=True)
    zc = z - mu
    rstd = lax.rsqrt(jnp.mean(zc * zc, axis=-1, keepdims=True) + EPS)
    zhat = zc * rstd
    return zhat, rstd, zhat * gain + bias


def _gmlp_fwd(puz, ln_g, ln_b, w_sp, b_sp_t):
    S = puz.shape[0]
    tm = min(512, S)
    nch = tm // CHUNK

    def body(p_ref, g_ref, b_ref, w_ref, bt_ref, o_ref):
        tril = _tril()
        ws = [jnp.where(tril, w_ref[gg], 0.0).astype(BF) for gg in range(4)]
        for ch in range(nch):
            rows = slice(ch * CHUNK, (ch + 1) * CHUNK)
            z = _gelu(p_ref[rows, GMLP_W:].astype(F32))
            _, _, zn = _ln_fwd(z, g_ref[...], b_ref[...])
            zn = zn.astype(BF)
            for gg in range(4):
                cols = slice(gg * CHUNK, (gg + 1) * CHUNK)
                sz = _dot(ws[gg], zn[:, cols]) + bt_ref[:, gg:gg + 1]
                u = _gelu(p_ref[rows, cols].astype(F32))
                o_ref[rows, cols] = (u * sz).astype(BF)

    return pl.pallas_call(
        body, name="gmlp_fwd", out_shape=jax.ShapeDtypeStruct((S, GMLP_W), BF), grid=(S // tm,),
        in_specs=[pl.BlockSpec((tm, 2 * GMLP_W), lambda i: (i, 0)),
                  pl.BlockSpec((1, GMLP_W), lambda i: (0, 0)), pl.BlockSpec((1, GMLP_W), lambda i: (0, 0)),
                  pl.BlockSpec((4, CHUNK, CHUNK), lambda i: (0, 0, 0)), pl.BlockSpec((CHUNK, 4), lambda i: (0, 0))],
        out_specs=pl.BlockSpec((tm, GMLP_W), lambda i: (i, 0)),
        compiler_params=_cp(("parallel",)))(puz, ln_g, ln_b, w_sp, b_sp_t)


def _merge_fwd(ya, yg, gates, wba, wbg, wout, x, g2, g3):
    S = x.shape[0]
    tm = min(512, S)

    def body(ya_ref, yg_ref, gt_ref, wba_ref, wbg_ref, wo_ref, x_ref, g2_ref, g3_ref,
             a_ref, gb_ref, mg_ref, y_ref, x2_ref, h3_ref):
        a = _dot(ya_ref[...], wba_ref[...])
        b = _dot(yg_ref[...], wbg_ref[...])
        a_ref[...] = a.astype(BF)
        gb_ref[...] = b.astype(BF)
        merged = (gt_ref[:, :D].astype(F32) * a + gt_ref[:, D:].astype(F32) * b).astype(BF)
        mg_ref[...] = merged
        y = _dot(merged, wo_ref[...])
        y_ref[...] = y
        x2 = x_ref[...] + y * _rms_stats(y) * g2_ref[...]
        x2_ref[...] = x2
        h3_ref[...] = (x2 * _rms_stats(x2) * g3_ref[...]).astype(BF)

    row = lambda w: pl.BlockSpec((tm, w), lambda i: (i, 0))
    full = lambda s: pl.BlockSpec(s, lambda i: (0, 0))
    return pl.pallas_call(
        body, name="merge_fwd", grid=(S // tm,),
        out_shape=(jax.ShapeDtypeStruct((S, D), BF), jax.ShapeDtypeStruct((S, D), BF), jax.ShapeDtypeStruct((S, D), BF),
                   jax.ShapeDtypeStruct((S, D), F32), jax.ShapeDtypeStruct((S, D), F32), jax.ShapeDtypeStruct((S, D), BF)),
        in_specs=[row(GW), row(GMLP_W), row(2 * D), full((GW, D)), full((GMLP_W, D)), full((D, D)), row(D),
                  full((1, D)), full((1, D))],
        out_specs=(row(D), row(D), row(D), row(D), row(D), row(D)),
        compiler_params=_cp(("parallel",)))(ya, yg, gates, wba, wbg, wout, x, g2, g3)


def _mlp_fwd(h3, wmi, wmo, x2, tgt, g4):
    S = x2.shape[0]
    tm = min(1024, S)
    tf = 512
    nf = DFF // tf

    def body(h_ref, wi_ref, wo_ref, x2_ref, t_ref, g4_ref, a_ref, dy2_ref, dout_ref, loss_ref, dg4_ref, acc_ref):
        i, c = pl.program_id(0), pl.program_id(1)

        @pl.when(jnp.logical_and(i == 0, c == 0))
        def _():
            loss_ref[...] = jnp.zeros_like(loss_ref)
            dg4_ref[...] = jnp.zeros_like(dg4_ref)

        a = jnp.maximum(_dot(h_ref[...], wi_ref[...]), 0.0)
        a_ref[...] = a.astype(BF)
        part = _dot((a * a).astype(BF), wo_ref[...])

        @pl.when(c == 0)
        def _():
            acc_ref[...] = part

        @pl.when(c > 0)
        def _():
            acc_ref[...] += part

        @pl.when(c == nf - 1)
        def _():
            y2 = acc_ref[...]
            r = _rms_stats(y2)
            out = x2_ref[...] + y2 * r * g4_ref[...]
            err = out - t_ref[...]
            sq = err * err
            tot = jnp.sum(jnp.sum(sq, axis=1, keepdims=True), axis=0, keepdims=True) * (0.5 / D)
            lane = lax.broadcasted_iota(jnp.int32, (1, 128), 1)
            loss_ref[...] += jnp.where(lane == 0, tot, 0.0)
            dout = err * (1.0 / D)
            dout_ref[...] = dout
            dy2, dg = _rms_bwd(y2, r, g4_ref[...], dout)
            dy2_ref[...] = dy2.astype(BF)
            dg4_ref[...] += dg

    row = pl.BlockSpec((tm, D), lambda i, c: (i, 0))
    return pl.pallas_call(
        body, name="mlp_fwd", grid=(S // tm, nf),
        out_shape=(jax.ShapeDtypeStruct((S, DFF), BF), jax.ShapeDtypeStruct((S, D), BF), jax.ShapeDtypeStruct((S, D), F32),
                   jax.ShapeDtypeStruct((1, 128), F32), jax.ShapeDtypeStruct((1, D), F32)),
        in_specs=[row, pl.BlockSpec((D, tf), lambda i, c: (0, c)), pl.BlockSpec((tf, D), lambda i, c: (c, 0)),
                  row, row, pl.BlockSpec((1, D), lambda i, c: (0, 0))],
        out_specs=(pl.BlockSpec((tm, tf), lambda i, c: (i, c)), row, row,
                   pl.BlockSpec((1, 128), lambda i, c: (0, 0)), pl.BlockSpec((1, D), lambda i, c: (0, 0))),
        scratch_shapes=[pltpu.VMEM((tm, D), F32)],
        compiler_params=_cp(("arbitrary", "arbitrary")))(h3, wmi, wmo, x2, tgt, g4)


def _mlp_bwd(dy2, a, wmo, wmi, x2, y, dout, g2, g3):
    S = x2.shape[0]
    tm = min(512, S)
    tf = 1024
    nf = DFF // tf

    def body(dy2_ref, a_ref, wo_ref, wi_ref, x2_ref, y_ref, dout_ref, g2_ref, g3_ref,
             dpre_ref, dx2_ref, dy_ref, dg3_ref, dg2_ref, acc_ref):
        i, c = pl.program_id(0), pl.program_id(1)

        @pl.when(jnp.logical_and(i == 0, c == 0))
        def _():
            dg3_ref[...] = jnp.zeros_like(dg3_ref)
            dg2_ref[...] = jnp.zeros_like(dg2_ref)

        da2 = _dot_nt(dy2_ref[...], wo_ref[...])
        dpre = (2.0 * a_ref[...].astype(F32) * da2).astype(BF)
        dpre_ref[...] = dpre
        part = _dot_nt(dpre, wi_ref[...])

        @pl.when(c == 0)
        def _():
            acc_ref[...] = part

        @pl.when(c > 0)
        def _():
            acc_ref[...] += part

        @pl.when(c == nf - 1)
        def _():
            x2 = x2_ref[...]
            dx3, dg3 = _rms_bwd(x2, _rms_stats(x2), g3_ref[...], acc_ref[...])
            dx2 = dout_ref[...] + dx3
            dx2_ref[...] = dx2
            dg3_ref[...] += dg3
            yv = y_ref[...]
            dy, dg2 = _rms_bwd(yv, _rms_stats(yv), g2_ref[...], dx2)
            dy_ref[...] = dy.astype(BF)
            dg2_ref[...] += dg2

    row = pl.BlockSpec((tm, D), lambda i, c: (i, 0))
    vec = pl.BlockSpec((1, D), lambda i, c: (0, 0))
    return pl.pallas_call(
        body, name="mlp_bwd", grid=(S // tm, nf),
        out_shape=(jax.ShapeDtypeStruct((S, DFF), BF), jax.ShapeDtypeStruct((S, D), F32), jax.ShapeDtypeStruct((S, D), BF),
                   jax.ShapeDtypeStruct((1, D), F32), jax.ShapeDtypeStruct((1, D), F32)),
        in_specs=[row, pl.BlockSpec((tm, tf), lambda i, c: (i, c)), pl.BlockSpec((tf, D), lambda i, c: (c, 0)),
                  pl.BlockSpec((D, tf), lambda i, c: (0, c)), row, row, row, vec, vec],
        out_specs=(pl.BlockSpec((tm, tf), lambda i, c: (i, c)), row, row, vec, vec),
        scratch_shapes=[pltpu.VMEM((tm, D), F32)],
        compiler_params=_cp(("arbitrary", "arbitrary")))(dy2, a, wmo, wmi, x2, y, dout, g2, g3)


def _mm_tn(a, b, name, square_a=False, tm=1024, tn=1024, tk=2048):
    S, M = a.shape
    N = b.shape[1]
    tm, tk = min(tm, M), min(tk, S)
    tn = max(t for t in range(128, min(tn, N) + 1, 128) if N % t == 0)
    assert M % tm == 0 and S % tk == 0
    nk = S // tk

    def body(a_ref, b_ref, o_ref):
        k = pl.program_id(2)
        av = a_ref[...]
        if square_a:
            av = av * av
        part = _dot_tn(av, b_ref[...])

        @pl.when(k == 0)
        def _():
            o_ref[...] = part

        @pl.when(k > 0)
        def _():
            o_ref[...] += part

    return pl.pallas_call(
        body, name=name, out_shape=jax.ShapeDtypeStruct((M, N), F32), grid=(M // tm, N // tn, nk),
        in_specs=[pl.BlockSpec((tk, tm), lambda i, j, k: (k, i)), pl.BlockSpec((tk, tn), lambda i, j, k: (k, j))],
        out_specs=pl.BlockSpec((tm, tn), lambda i, j, k: (i, j)),
        compiler_params=_cp(("parallel", "parallel", "arbitrary")))(a, b)


def _outproj_bwd(dy, wout, abr, gbr, gates, wba, wbg, ya):
    S = dy.shape[0]
    tm = min(512, S)

    def body(dy_ref, wo_ref, a_ref, b_ref, gt_ref, wba_ref, wbg_ref, ya_ref,
             dgt_ref, da_ref, db_ref, dyg_ref, e0, e1, e2, s0, s1, s2, scr):
        dm = _dot_nt(dy_ref[...], wo_ref[...])
        ga, gb = gt_ref[:, :D].astype(F32), gt_ref[:, D:].astype(F32)
        dgt_ref[:, :D] = (dm * a_ref[...].astype(F32) * ga * (1.0 - ga)).astype(BF)
        dgt_ref[:, D:] = (dm * b_ref[...].astype(F32) * gb * (1.0 - gb)).astype(BF)
        da = (dm * ga).astype(BF)
        db = (dm * gb).astype(BF)
        da_ref[...] = da
        db_ref[...] = db
        dyg_ref[...] = _dot_nt(db, wbg_ref[...]).astype(BF)
        dya = _dot_nt(da, wba_ref[...]).astype(BF).astype(F32)
        dyy = dya * ya_ref[...].astype(F32)
        dsum = jnp.zeros((tm, GW), F32)
        for hh in range(NSLOT):
            hm = _head_mask(hh)
            dsum = dsum + jnp.where(hm, jnp.sum(jnp.where(hm, dyy, 0.0), axis=1, keepdims=True), 0.0)
        for e_ref, s_ref, d in zip((e0, e1, e2), (s0, s1, s2), DILS):
            _dilate_store(dya, scr, e_ref, (), d)
            _dilate_store(dsum, scr, s_ref, (), d)

    row = lambda w: pl.BlockSpec((tm, w), lambda i: (i, 0))
    full = lambda s: pl.BlockSpec(s, lambda i: (0, 0))
    dil = lambda d: pl.BlockSpec((d, tm // d, GW), lambda i: (0, i, 0))
    dshape = lambda d, t: jax.ShapeDtypeStruct((d, S // d, GW), t)
    return pl.pallas_call(
        body, name="outproj_bwd", grid=(S // tm,),
        out_shape=(jax.ShapeDtypeStruct((S, 2 * D), BF), jax.ShapeDtypeStruct((S, D), BF), jax.ShapeDtypeStruct((S, D), BF),
                   jax.ShapeDtypeStruct((S, GMLP_W), BF)) + tuple(dshape(d, BF) for d in DILS)
        + tuple(dshape(d, F32) for d in DILS),
        in_specs=[row(D), full((D, D)), row(D), row(D), row(2 * D), full((GW, D)), full((GMLP_W, D)), row(GW)],
        out_specs=(row(2 * D), row(D), row(D), row(GMLP_W)) + tuple(dil(d) for d in DILS) * 2,
        scratch_shapes=[pltpu.VMEM((2, tm, 128), F32)],
        compiler_params=_cp(("parallel",)))(dy, wout, abr, gbr, gates, wba, wbg, ya)


def _gmlp_bwd(puz, dyg, ln_g, ln_b, w_sp, w_sp_t, b_sp_t):
    S = puz.shape[0]
    tm = min(512, S)
    nch = tm // CHUNK

    def body(p_ref, dy_ref, g_ref, b_ref, w_ref, wt_ref, bt_ref,
             dp_ref, dw_ref, dbs_ref, dg_ref, dbias_ref, dbacc_ref):
        i = pl.program_id(0)

        @pl.when(i == 0)
        def _():
            dw_ref[...] = jnp.zeros_like(dw_ref)
            dbacc_ref[...] = jnp.zeros_like(dbacc_ref)
            dg_ref[...] = jnp.zeros_like(dg_ref)
            dbias_ref[...] = jnp.zeros_like(dbias_ref)

        tril = _tril()
        ws = [jnp.where(tril, w_ref[gg], 0.0).astype(BF) for gg in range(4)]
        triu = _tril(upper=True)
        wts = [jnp.where(triu, wt_ref[gg], 0.0).astype(BF) for gg in range(4)]
        gain = g_ref[...]
        for ch in range(nch):
            rows = slice(ch * CHUNK, (ch + 1) * CHUNK)
            pz = p_ref[rows, GMLP_W:].astype(F32)
            z = _gelu(pz)
            zhat, rstd, zn = _ln_fwd(z, gain, b_ref[...])
            znb = zn.astype(BF)
            dzn_parts = []
            for gg in range(4):
                cols = slice(gg * CHUNK, (gg + 1) * CHUNK)
                pu = p_ref[rows, cols].astype(F32)
                u = _gelu(pu)
                sz = _dot(ws[gg], znb[:, cols]) + bt_ref[:, gg:gg + 1]
                dyv = dy_ref[rows, cols].astype(F32)
                dp_ref[rows, cols] = (dyv * sz * _gelu_grad(pu)).astype(BF)
                dsz = dyv * u
                dbacc_ref[gg] += dsz
                dszb = dsz.astype(BF)
                dw_ref[gg] += _dot_nt(dszb, znb[:, cols])
                dzn_parts.append(_dot(wts[gg], dszb))
            dzn = jnp.concatenate(dzn_parts, axis=1)
            dg_ref[...] += jnp.sum(dzn * zhat, axis=0, keepdims=True)
            dbias_ref[...] += jnp.sum(dzn, axis=0, keepdims=True)
            dzh = dzn * gain
            dz = rstd * (dzh - jnp.mean(dzh, axis=-1, keepdims=True)
                         - zhat * jnp.mean(dzh * zhat, axis=-1, keepdims=True))
            dp_ref[rows, GMLP_W:] = (dz * _gelu_grad(pz)).astype(BF)

        @pl.when(i == pl.num_programs(0) - 1)
        def _():
            for gg in range(4):
                dw_ref[gg] = jnp.where(tril, dw_ref[gg], 0.0)
                dbs_ref[gg] = jnp.sum(dbacc_ref[gg], axis=1, keepdims=True)

    full2 = lambda s: pl.BlockSpec(s, lambda i: (0, 0))
    full3 = lambda s: pl.BlockSpec(s, lambda i: (0, 0, 0))
    return pl.pallas_call(
        body, name="gmlp_bwd", grid=(S // tm,),
        out_shape=(jax.ShapeDtypeStruct((S, 2 * GMLP_W), BF), jax.ShapeDtypeStruct((4, CHUNK, CHUNK), F32),
                   jax.ShapeDtypeStruct((4, CHUNK, 1), F32), jax.ShapeDtypeStruct((1, GMLP_W), F32),
                   jax.ShapeDtypeStruct((1, GMLP_W), F32)),
        in_specs=[pl.BlockSpec((tm, 2 * GMLP_W), lambda i: (i, 0)), pl.BlockSpec((tm, GMLP_W), lambda i: (i, 0)),
                  full2((1, GMLP_W)), full2((1, GMLP_W)), full3((4, CHUNK, CHUNK)), full3((4, CHUNK, CHUNK)),
                  full2((CHUNK, 4))],
        out_specs=(pl.BlockSpec((tm, 2 * GMLP_W), lambda i: (i, 0)), full3((4, CHUNK, CHUNK)), full3((4, CHUNK, 1)),
                   full2((1, GMLP_W)), full2((1, GMLP_W))),
        scratch_shapes=[pltpu.VMEM((4, CHUNK, CHUNK), F32)],
        compiler_params=_cp(("arbitrary",)))(puz, dyg, ln_g, ln_b, w_sp, w_sp_t, b_sp_t)


def _attn_bwd(qkv, dya, dsums, lse, name):
    _, d, L, _ = qkv.shape
    nsub = min(ATTN_NSUB, L // QB)
    R = nsub * QB
    nsteps = L // R

    def body(q_ref, qn_ref, kp_ref, kc_ref, vp_ref, vc_ref, dy_ref, dyn_ref, e_ref, en_ref, l_ref, ln_ref, o_ref):
        i = pl.program_id(1)
        band, band_first = _band_masks(i == 0)
        row = lax.broadcasted_iota(jnp.int32, (QB, QB), 0)
        col = lax.broadcasted_iota(jnp.int32, (QB, QB), 1)
        mask_next = jnp.logical_and(col >= row, i < nsteps - 1)
        kc, vc = kc_ref[...], vc_ref[...]
        kfull = jnp.concatenate([kp_ref[...], kc], axis=0)
        vfull = jnp.concatenate([vp_ref[...], vc], axis=0)
        k_last, v_last = kc[(nsub - 1) * QB:], vc[(nsub - 1) * QB:]
        q_ext = jnp.concatenate([q_ref[...], qn_ref[...]], axis=0)
        dy_ext = jnp.concatenate([dy_ref[...], dyn_ref[...]], axis=0)
        esum, esum_n, lse, lse_n = e_ref[...], en_ref[...], l_ref[...], ln_ref[...]
        win = lambda t, sb: t[sb * QB:(sb + 2) * QB]
        blk = lambda t, sb: t[sb * QB:(sb + 1) * QB]
        hms = [_head_mask(hh) for hh in range(NSLOT)]
        q_hs = [jnp.where(hm, q_ext, 0) for hm in hms]
        dy_hs = [jnp.where(hm, dy_ext, 0) for hm in hms]
        raw = []
        for hh in range(NSLOT):
            tiles = [(_dot_nt(blk(q_hs[hh], sb), win(kfull, sb)), _dot_nt(blk(dy_hs[hh], sb), win(vfull, sb)))
                     for sb in range(nsub)]
            tiles.append((_dot_nt(blk(q_hs[hh], nsub), k_last), _dot_nt(blk(dy_hs[hh], nsub), v_last)))
            raw.append(tiles)
        ps, dss = [], []
        for hh in range(NSLOT):
            rowstat = lambda t: jnp.max(jnp.where(hms[hh], t, -jnp.inf), axis=1, keepdims=True)
            p_h, ds_h = [], []
            for sb in range(nsub + 1):
                sc, dp = raw[hh][sb]
                if sb < nsub:
                    msk, lrow, erow = (band_first if sb == 0 else band), rowstat(blk(lse, sb)), rowstat(blk(esum, sb))
                else:
                    msk, lrow, erow = mask_next, rowstat(lse_n), rowstat(esum_n)
                p = jnp.where(msk, jnp.exp(sc * SCALE - lrow), 0.0)
                p_h.append(p.astype(BF))
                ds_h.append((p * (dp - erow)).astype(BF))
            ps.append(p_h)
            dss.append(ds_h)
        dq = [jnp.zeros((QB, GW), F32) for _ in range(nsub)]
        dk = [jnp.zeros((QB, GW), F32) for _ in range(nsub)]
        dv = [jnp.zeros((QB, GW), F32) for _ in range(nsub)]
        for hh in range(NSLOT):
            for sb in range(nsub):
                dq[sb] = dq[sb] + jnp.where(hms[hh], _dot(dss[hh][sb], win(kfull, sb)), 0.0)
                nxt = lambda t: t[sb + 1][:, :QB] if sb + 1 < nsub else t[nsub]
                dk[sb] = dk[sb] + _dot_tn(jnp.concatenate([dss[hh][sb][:, QB:], nxt(dss[hh])], axis=0), win(q_hs[hh], sb))
                dv[sb] = dv[sb] + _dot_tn(jnp.concatenate([ps[hh][sb][:, QB:], nxt(ps[hh])], axis=0), win(dy_hs[hh], sb))
        for sb in range(nsub):
            rows = slice(sb * QB, (sb + 1) * QB)
            o_ref[0, rows, :] = (dq[sb] * SCALE).astype(BF)
            o_ref[1, rows, :] = (dk[sb] * SCALE).astype(BF)
            o_ref[2, rows, :] = dv[sb].astype(BF)

    prev = lambda i: jnp.maximum(i * nsub - 1, 0)
    nxt = lambda i: jnp.minimum((i + 1) * nsub, L // QB - 1)
    cur4 = lambda t: pl.BlockSpec((None, None, R, GW), lambda r, i: (t, r, i, 0))
    prv4 = lambda t: pl.BlockSpec((None, None, QB, GW), lambda r, i: (t, r, prev(i), 0))
    nxt4 = lambda t: pl.BlockSpec((None, None, QB, GW), lambda r, i: (t, r, nxt(i), 0))
    cur3 = pl.BlockSpec((None, R, GW), lambda r, i: (r, i, 0))
    nxt3 = pl.BlockSpec((None, QB, GW), lambda r, i: (r, nxt(i), 0))
    return pl.pallas_call(
        body, name=name, grid=(d, nsteps), out_shape=jax.ShapeDtypeStruct((3, d, L, GW), BF),
        in_specs=[cur4(0), nxt4(0), prv4(1), cur4(1), prv4(2), cur4(2), cur3, nxt3, cur3, nxt3, cur3, nxt3],
        out_specs=pl.BlockSpec((3, None, R, GW), lambda r, i: (0, r, i, 0)),
        compiler_params=_cp(("parallel", "arbitrary")))(qkv, qkv, qkv, qkv, qkv, qkv, dya, dya, dsums, dsums, lse, lse)


def _inproj_bwd(dqkvs, dpuz, dgates, wqkv, wuz, wg, cos_t, sin_t, x, dx2, g1):
    S = x.shape[0]
    tm = min(512, S)

    def body(d0_ref, d1_ref, d2_ref, dp_ref, dg_ref, wqkv_ref, wuz_ref, wg_ref, c_ref, s_ref,
             x_ref, dx2_ref, g1_ref, gx_ref, dg1_ref, dn_ref, scr):
        i = pl.program_id(0)

        @pl.when(i == 0)
        def _():
            dg1_ref[...] = jnp.zeros_like(dg1_ref)

        for t in range(3):
            for g, (d_ref, d) in enumerate(zip((d0_ref, d1_ref, d2_ref), DILS)):
                piece = _undilate_load(d_ref, (t,), d, scr, tm)
                if t < 2:
                    piece = _rope_bwd(piece, c_ref, s_ref)
                dn_ref[:, (3 * t + g) * GW:(3 * t + g + 1) * GW] = piece.astype(BF)
        dh = (_dot_nt(dn_ref[...], wqkv_ref[...]) + _dot_nt(dp_ref[...], wuz_ref[...])
              + _dot_nt(dg_ref[...], wg_ref[...]))
        xv = x_ref[...]
        dx1, dg1 = _rms_bwd(xv, _rms_stats(xv), g1_ref[...], dh)
        gx_ref[...] = dx2_ref[...] + dx1
        dg1_ref[...] += dg1

    row = lambda w: pl.BlockSpec((tm, w), lambda i: (i, 0))
    full = lambda s: pl.BlockSpec(s, lambda i: (0, 0))
    dil = lambda d: pl.BlockSpec((3, d, tm // d, GW), lambda i: (0, 0, i, 0))
    return pl.pallas_call(
        body, name="inproj_bwd", grid=(S // tm,),
        out_shape=(jax.ShapeDtypeStruct((S, D), F32), jax.ShapeDtypeStruct((1, D), F32),
                   jax.ShapeDtypeStruct((S, 3 * AW), BF)),
        in_specs=[dil(d) for d in DILS] + [row(2 * GMLP_W), row(2 * D), full((D, 3 * AW)), full((D, 2 * GMLP_W)),
                                            full((D, 2 * D)), row(128), row(128), row(D), row(D), full((1, D))],
        out_specs=(row(D), full((1, D)), row(3 * AW)),
        scratch_shapes=[pltpu.VMEM((2, tm, 128), F32)],
        compiler_params=_cp(("arbitrary",)))(*dqkvs, dpuz, dgates, wqkv, wuz, wg, cos_t, sin_t, x, dx2, g1)


def _adam_math(w, g, m, v):
    m2 = ADAM_B1 * m + (1.0 - ADAM_B1) * g
    v2 = ADAM_B2 * v + (1.0 - ADAM_B2) * (g * g)
    m_hat = m2 / (1.0 - ADAM_B1 ** ADAM_STEP)
    v_hat = v2 / (1.0 - ADAM_B2 ** ADAM_STEP)
    delta = -ADAM_LR * (m_hat / (jnp.sqrt(v_hat) + ADAM_EPS) + ADAM_WD * w)
    return delta, m2, v2


def _adam_shard(own, recv, w, m, v, name):
    R, C = w.shape
    tr = min(256, R)

    def body(own_ref, r_ref, w_ref, m_ref, v_ref, g_ref, d_ref, m2_ref, v2_ref):
        g = own_ref[...] + r_ref[0].astype(F32) + r_ref[1].astype(F32) + r_ref[2].astype(F32)
        g_ref[...] = g
        d_ref[...], m2_ref[...], v2_ref[...] = _adam_math(w_ref[...], g, m_ref[...], v_ref[...])

    spec = pl.BlockSpec((tr, C), lambda i: (i, 0))
    out = jax.ShapeDtypeStruct((R, C), F32)
    return pl.pallas_call(
        body, name=name, grid=(R // tr,), out_shape=(out, out, out, out),
        in_specs=[spec, pl.BlockSpec((3, tr, C), lambda i: (0, i, 0)), spec, spec, spec],
        out_specs=(spec, spec, spec, spec), compiler_params=_cp(("parallel",)))(own, recv, w, m, v)


def _adam_small(gp, wp, mp, vp):
    def body(g_ref, w_ref, m_ref, v_ref, d_ref, m2_ref, v2_ref):
        d_ref[...], m2_ref[...], v2_ref[...] = _adam_math(w_ref[...], g_ref[...], m_ref[...], v_ref[...])

    out = jax.ShapeDtypeStruct(gp.shape, F32)
    return pl.pallas_call(body, name="adam_small", out_shape=(out, out, out),
                          compiler_params=pltpu.CompilerParams(vmem_limit_bytes=VMEM_LIMIT))(gp, wp, mp, vp)


def _rs_add(gblocks, recv, idx, name):
    _, R, C = gblocks.shape
    tr = min(256, R)

    def body(t_ref, g_ref, r_ref, own_ref, send_ref):
        j = pl.program_id(1)
        s = g_ref[...] + r_ref[...]

        @pl.when(j == 0)
        def _():
            own_ref[...] = s

        @pl.when(j > 0)
        def _():
            send_ref[...] = s.astype(BF)

    grid_spec = pltpu.PrefetchScalarGridSpec(
        num_scalar_prefetch=1, grid=(R // tr, 4),
        in_specs=[pl.BlockSpec((None, tr, C), lambda i, j, t: (t[j], i, 0)),
                  pl.BlockSpec((None, tr, C), lambda i, j, t: (t[4 + j], i, 0))],
        out_specs=[pl.BlockSpec((tr, C), lambda i, j, t: (i, 0)),
                   pl.BlockSpec((None, tr, C), lambda i, j, t: (jnp.maximum(j - 1, 0), i, 0))])
    return pl.pallas_call(
        body, name=name, grid_spec=grid_spec,
        out_shape=(jax.ShapeDtypeStruct((R, C), F32), jax.ShapeDtypeStruct((3, R, C), BF)),
        compiler_params=_cp(("parallel", "arbitrary")))(idx, gblocks, recv)


def _mesh_pos():
    return lax.axis_index("x"), lax.axis_index("y"), lax.axis_index("c")


def _all_gather_weights(shards):
    n = len(shards)

    def body(*refs):
        ins, outs = refs[:n], refs[n:2 * n]
        send_sems, recv_sems, loc_sems = refs[2 * n:]
        x, y, c = _mesh_pos()
        me, sib = (x, y, c), (x, y, 1 - c)
        chips = [(1 - x, y), (x, 1 - y), (1 - x, 1 - y)]

        def rcopy(k, s, block, to, src=None):
            dst = outs[k].at[4 * block[0] + 2 * block[1] + block[2]]
            return pltpu.make_async_remote_copy(
                src_ref=dst if src is None else src, dst_ref=dst, send_sem=send_sems.at[k, s],
                recv_sem=recv_sems.at[k, s], device_id=to, device_id_type=MESH)

        locs = [pltpu.make_async_copy(ins[k], outs[k].at[4 * x + 2 * y + c], loc_sems.at[k]) for k in range(n)]
        for cp in locs:
            cp.start()
        started = []
        for k in range(n):
            for j, chip in enumerate(chips):
                started.append(rcopy(k, 1 + j, me, (*chip, c), src=ins[k]))
                started[-1].start()
        for k in range(n):
            started.append(rcopy(k, 0, me, sib, src=ins[k]))
            started[-1].start()
        for k in range(n):
            for j, chip in enumerate(chips):
                rcopy(k, 1 + j, (*chip, c), me).wait_recv()
                started.append(rcopy(k, 4 + j, (*chip, c), sib))
                started[-1].start()
        for k in range(n):
            rcopy(k, 0, sib, me).wait_recv()
            for j, chip in enumerate(chips):
                rcopy(k, 4 + j, (*chip, 1 - c), me).wait_recv()
        for cp in started:
            cp.wait_send()
        for cp in locs:
            cp.wait()

    hbm = pl.BlockSpec(memory_space=pl.ANY)
    return pl.pallas_call(
        body, name="ag_weights",
        out_shape=tuple(jax.ShapeDtypeStruct((8,) + s.shape, s.dtype) for s in shards),
        in_specs=[hbm] * n, out_specs=(hbm,) * n,
        scratch_shapes=[pltpu.SemaphoreType.DMA((n, 7)), pltpu.SemaphoreType.DMA((n, 7)),
                        pltpu.SemaphoreType.DMA((n,))])(*shards)


def _rs_d2d(gblocks):
    n = len(gblocks)

    def body(*refs):
        ins, outs = refs[:n], refs[n:2 * n]
        send_sems, recv_sems = refs[2 * n:]
        x, y, c = _mesh_pos()
        copies = []
        for k in range(n):
            for kk in range(4):
                copies.append(pltpu.make_async_remote_copy(
                    src_ref=ins[k].at[2 * kk + 1 - c], dst_ref=outs[k].at[kk], send_sem=send_sems.at[k, kk],
                    recv_sem=recv_sems.at[k, kk], device_id=(x, y, 1 - c), device_id_type=MESH))
                copies[-1].start()
        for cp in copies:
            cp.wait()

    hbm = pl.BlockSpec(memory_space=pl.ANY)
    return pl.pallas_call(
        body, name="rs_d2d",
        out_shape=tuple(jax.ShapeDtypeStruct((4,) + g.shape[1:], g.dtype) for g in gblocks),
        in_specs=[hbm] * n, out_specs=(hbm,) * n,
        scratch_shapes=[pltpu.SemaphoreType.DMA((n, 4)), pltpu.SemaphoreType.DMA((n, 4))])(*gblocks)


def _rs_ici(sends):
    n = len(sends)

    def body(*refs):
        ins, outs = refs[:n], refs[n:2 * n]
        send_sems, recv_sems = refs[2 * n:]
        x, y, c = _mesh_pos()
        chips = [(1 - x, y), (x, 1 - y), (1 - x, 1 - y)]
        copies = []
        for k in range(n):
            for j, chip in enumerate(chips):
                copies.append(pltpu.make_async_remote_copy(
                    src_ref=ins[k].at[j], dst_ref=outs[k].at[j], send_sem=send_sems.at[k, j],
                    recv_sem=recv_sems.at[k, j], device_id=(*chip, c), device_id_type=MESH))
                copies[-1].start()
        for cp in copies:
            cp.wait()

    hbm = pl.BlockSpec(memory_space=pl.ANY)
    return pl.pallas_call(
        body, name="rs_ici",
        out_shape=tuple(jax.ShapeDtypeStruct(s.shape, s.dtype) for s in sends),
        in_specs=[hbm] * n, out_specs=(hbm,) * n,
        scratch_shapes=[pltpu.SemaphoreType.DMA((n, 3)), pltpu.SemaphoreType.DMA((n, 3))])(*sends)


_HBM = pl.BlockSpec(memory_space=pltpu.HBM)
_SEM = pl.BlockSpec(memory_space=pltpu.SEMAPHORE)
_EFFECT = pltpu.SideEffectType.DATAFLOW_SIDE_EFFECTING
_RELATIONS = [(dx, dy, dc) for dx in (0, 1) for dy in (0, 1) for dc in (0, 1)][1:]


def _flip(v, d):
    return 1 - v if d else v


def _plan_gather(n):
    def plan(x, y, c):
        return [(k, None, 4 * x + 2 * y + c, (_flip(x, dx), _flip(y, dy), _flip(c, dc)))
                for k in range(n) for dx, dy, dc in _RELATIONS]
    return plan


def _plan_d2d(n):
    def plan(x, y, c):
        return [(k, 2 * kk + 1 - c, kk, (x, y, 1 - c)) for k in range(n) for kk in range(4)]
    return plan


def _plan_ici(n):
    def plan(x, y, c):
        return [(k, j, j, (_flip(x, dx), _flip(y, dy), c))
                for k in range(n) for j, (dx, dy) in enumerate(((1, 0), (0, 1), (1, 1)))]
    return plan


def _plan_copies(plan, src_refs, land_refs, send_sems, recv_sems):
    x, y, c = _mesh_pos()
    return [pltpu.make_async_remote_copy(
        src_ref=src_refs[k] if si is None else src_refs[k].at[si], dst_ref=land_refs[k].at[di],
        send_sem=send_sems.at[n], recv_sem=recv_sems.at[n], device_id=dev, device_id_type=MESH)
        for n, (k, si, di, dev) in enumerate(plan(x, y, c))]


def _exchange_start(srcs, land_shapes, plan, ncopies, name, after):
    n = len(srcs)

    def body(*refs):
        src_refs, land_refs = refs[:n], refs[n:2 * n]
        send_sems, recv_sems = refs[2 * n + 1], refs[2 * n + 2]
        token = refs[-1]
        for cp in _plan_copies(plan, src_refs, land_refs, send_sems, recv_sems):
            cp.start()
        token[...] = jnp.zeros_like(token)

    lands = [pltpu.with_memory_space_constraint(lax.empty(s, a.dtype), pltpu.HBM) for s, a in zip(land_shapes, srcs)]
    srcs = [pltpu.with_memory_space_constraint(a, pltpu.HBM) for a in srcs]
    outs = pl.pallas_call(
        body, name=name,
        out_shape=(pltpu.SemaphoreType.DMA((ncopies,)), pltpu.SemaphoreType.DMA((ncopies,)))
        + tuple(pltpu.HBM(a.shape, a.dtype) for a in srcs) + tuple(pltpu.HBM(a.shape, a.dtype) for a in lands)
        + (jax.ShapeDtypeStruct((8, 128), F32),),
        in_specs=[_HBM] * (2 * n) + [pl.BlockSpec(memory_space=pl.ANY)],
        out_specs=(_SEM, _SEM) + (_HBM,) * (2 * n) + (pl.BlockSpec(memory_space=pltpu.VMEM),),
        input_output_aliases={i: 2 + i for i in range(2 * n)},
        compiler_params=pltpu.CompilerParams(has_side_effects=_EFFECT))(*srcs, *lands, after)
    return (outs[0], outs[1], list(outs[2:2 + n]), list(outs[2 + n:2 + 2 * n])), outs[-1]


def _exchange_wait(handle, plan, name, after):
    send_sems, recv_sems, srcs, lands = handle
    n = len(srcs)

    def body(*refs):
        src_refs, land_refs = refs[:n], refs[n:2 * n]
        for cp in _plan_copies(plan, src_refs, land_refs, refs[2 * n], refs[2 * n + 1]):
            cp.wait_send()
            cp.wait_recv()

    outs = pl.pallas_call(
        body, name=name,
        out_shape=tuple(pltpu.HBM(a.shape, a.dtype) for a in srcs) + tuple(pltpu.HBM(a.shape, a.dtype) for a in lands),
        in_specs=[_HBM] * (2 * n) + [_SEM, _SEM, pl.BlockSpec(memory_space=pl.ANY)],
        out_specs=(_HBM,) * (2 * n), input_output_aliases={i: i for i in range(2 * n)},
        compiler_params=pltpu.CompilerParams(has_side_effects=_EFFECT))(*srcs, *lands, send_sems, recv_sems, after)
    return list(outs[:n]), list(outs[n:])


def _small_allreduce(pkt):
    R = pkt.shape[0]

    def body(p_ref, o_ref, buf, send_sems, recv_sems):
        x, y, c = _mesh_pos()
        me, sib = (x, y, c), (x, y, 1 - c)
        chips = [(1 - x, y), (x, 1 - y), (1 - x, 1 - y)]

        def rcopy(s, block, to, src=None):
            dst = buf.at[4 * block[0] + 2 * block[1] + block[2]]
            return pltpu.make_async_remote_copy(
                src_ref=dst if src is None else src, dst_ref=dst, send_sem=send_sems.at[s],
                recv_sem=recv_sems.at[s], device_id=to, device_id_type=MESH)

        buf[4 * x + 2 * y + c] = p_ref[...]
        started = [rcopy(0, me, sib, src=p_ref)]
        started += [rcopy(1 + j, me, (*chip, c), src=p_ref) for j, chip in enumerate(chips)]
        for cp in started:
            cp.start()
        for j, chip in enumerate(chips):
            rcopy(1 + j, (*chip, c), me).wait_recv()
            started.append(rcopy(4 + j, (*chip, c), sib))
            started[-1].start()
        rcopy(0, sib, me).wait_recv()
        for j, chip in enumerate(chips):
            rcopy(4 + j, (*chip, 1 - c), me).wait_recv()
        for cp in started:
            cp.wait_send()
        acc = buf[0]
        for b in range(1, 8):
            acc = acc + buf[b]
        o_ref[...] = acc

    return pl.pallas_call(
        body, name="small_allreduce", out_shape=jax.ShapeDtypeStruct(pkt.shape, F32),
        in_specs=[pl.BlockSpec(memory_space=pltpu.VMEM)], out_specs=pl.BlockSpec(memory_space=pltpu.VMEM),
        scratch_shapes=[pltpu.VMEM((8, R, 128), F32), pltpu.SemaphoreType.DMA((7,)), pltpu.SemaphoreType.DMA((7,))],
        compiler_params=pltpu.CompilerParams(vmem_limit_bytes=VMEM_LIMIT))(pkt)


def _rope_tables(S):
    half = HD // 2
    inv_freq = ROPE_THETA ** (-jnp.arange(half, dtype=F32) / half)
    ang = jnp.arange(S, dtype=F32)[:, None] * inv_freq[None, :]
    cos, sin = jnp.cos(ang), jnp.sin(ang)
    return jnp.concatenate([cos, cos, cos, cos], axis=1), jnp.concatenate([-sin, sin, -sin, sin], axis=1)


def _to_blocks(g, col_sharded):
    if col_sharded:
        return g.reshape(g.shape[0], 8, g.shape[1] // 8).transpose(1, 0, 2)
    return g.reshape(8, g.shape[0] // 8, g.shape[1])


def _from_blocks(t, col_sharded):
    if col_sharded:
        return t.transpose(1, 0, 2).reshape(t.shape[1], 8 * t.shape[2])
    return t.reshape(8 * t.shape[1], t.shape[2])


class _NoComm:
    def __init__(self, late_weights):
        self._late = late_weights
        self.mlp_grads = None

    def start_token(self):
        return 0.0

    def late_weights(self, after):
        return self._late

    def mlp_grads_ready(self, dwmi, dwmo):
        self.mlp_grads = (dwmi, dwmo)

    def after_outproj_bwd(self, after):
        return 0.0


class _FsdpComm:
    def __init__(self, late_shards, col_sharded, after, idx, block):
        self._shards, self._col, self._idx, self._block = late_shards, col_sharded, idx, block
        n = len(late_shards)
        self._gather, self._token = _exchange_start(
            late_shards, [(8,) + s.shape for s in late_shards], _plan_gather(n), 7 * n, "ag_late_start", after)

    def start_token(self):
        return self._token[0, 0]

    def late_weights(self, after):
        shards, lands = _exchange_wait(self._gather, _plan_gather(len(self._col)), "ag_late_wait", after)
        lands = [lax.dynamic_update_index_in_dim(t, s, self._block, 0) for t, s in zip(lands, shards)]
        return [_from_blocks(t, cs) for t, cs in zip(lands, self._col)]

    def mlp_grads_ready(self, dwmi, dwmo):
        gblocks = [_to_blocks(dwmi, True), _to_blocks(dwmo, False)]
        self._d2d, _ = _exchange_start(gblocks, [(4,) + g.shape[1:] for g in gblocks], _plan_d2d(2), 8,
                                       "rs_mlp_d2d_start", self._token)

    def after_outproj_bwd(self, after):
        gblocks, from_sib = _exchange_wait(self._d2d, _plan_d2d(2), "rs_mlp_d2d_wait", after)
        halves = [_rs_add(g, r, self._idx, "rs_add_mlp_%d" % n) for n, (g, r) in enumerate(zip(gblocks, from_sib))]
        self._own = [own for own, _ in halves]
        sends = [s for _, s in halves]
        self._ici, token = _exchange_start(sends, [s.shape for s in sends], _plan_ici(2), 6, "rs_mlp_ici_start",
                                           self._token)
        return token[0, 0]

    def finish(self, after):
        return self._own, _exchange_wait(self._ici, _plan_ici(2), "rs_mlp_ici_wait", after)[1]


def _local_step(x, tgt, wqkv, wuz, wg, comm, g1, g2, g3, g4, ln_g, ln_b, w_sp, b_sp):
    S = x.shape[0]
    cos_t, sin_t = _rope_tables(S)
    b_sp_t = b_sp.T
    w_sp_t = w_sp.transpose(0, 2, 1)

    h1 = _rms_fwd(x, g1 + comm.start_token(), "rms_pre_mix")
    qkvs = [_proj_qkv(h1, wqkv, cos_t, sin_t, g, d, "proj_qkv_d%d" % d) for g, d in enumerate(DILS)]
    puz = _proj(h1, wuz, "none", "proj_uz")
    gates = _proj(h1, wg, "sigmoid", "proj_gates")
    fwd = [_attn_fwd(t, "attn_fwd_d%d" % d) for t, d in zip(qkvs, DILS)]
    ya, *lses = _attn_combine([o for o, _ in fwd], [l for _, l in fwd])
    yg = _gmlp_fwd(puz, ln_g, ln_b, w_sp, b_sp_t)
    wba, wbg, wout, wmi, wmo = comm.late_weights(after=yg)
    abr, gbr, merged, y, x2, h3 = _merge_fwd(ya, yg, gates, wba, wbg, wout, x, g2, g3)
    a, dy2, dout, loss_row, dg4 = _mlp_fwd(h3, wmi, wmo, x2, tgt, g4)

    dpre, dx2, dy, dg3, dg2 = _mlp_bwd(dy2, a, wmo, wmi, x2, y, dout, g2, g3)
    dwmo = _mm_tn(a, dy2, "dw_mlp_out", square_a=True)
    dwmi = _mm_tn(h3, dpre, "dw_mlp_in")
    comm.mlp_grads_ready(dwmi, dwmo)
    dgates, da, db, dyg, *rest = _outproj_bwd(dy, wout, abr, gbr, gates, wba, wbg, ya)
    dyas, dsums = rest[:3], rest[3:]
    dwout = _mm_tn(merged, dy, "dw_out")
    dwba = _mm_tn(ya, da, "dw_branch_attn")
    dwbg = _mm_tn(yg, db, "dw_branch_gmlp")
    dpuz, dwsp, dbs, dlng, dlnb = _gmlp_bwd(puz, dyg, ln_g + comm.after_outproj_bwd(after=dwbg), ln_b, w_sp, w_sp_t,
                                            b_sp_t)
    dqkvs = [_attn_bwd(qkvs[g], dyas[g], dsums[g], lses[g], "attn_bwd_d%d" % d) for g, d in enumerate(DILS)]
    grad_x, dg1, dqkv = _inproj_bwd(dqkvs, dpuz, dgates, wqkv, wuz, wg, cos_t, sin_t, x, dx2, g1)
    dwin = jnp.concatenate([_mm_tn(h1, t, "dw_in_%d" % n) for n, t in enumerate((dqkv, dpuz, dgates))], axis=1)
    small = dict(g1=dg1, g2=dg2, g3=dg3, g4=dg4, ln_g=dlng, ln_b=dlnb, b_sp=dbs.reshape(4, CHUNK), w_sp=dwsp)
    return loss_row, grad_x, (dwin, dwba, dwbg, dwout), small


def _pack_small(first_row, t):
    pad = lambda a, rows: jnp.concatenate([a, jnp.zeros((rows - a.shape[0], 128), F32)], axis=0)
    return jnp.concatenate([
        pad(first_row, 8), t["g1"].reshape(8, 128), t["g2"].reshape(8, 128), t["g3"].reshape(8, 128),
        t["g4"].reshape(8, 128), t["ln_g"].reshape(4, 128), t["ln_b"].reshape(4, 128),
        pad(t["b_sp"].reshape(4, 128), 8), t["w_sp"].reshape(4 * CHUNK, 128)], axis=0)


def _unpack_small(p):
    vec = lambda r, n: p[r:r + n // 128].reshape(1, n)
    return dict(g1=vec(PK_G1, D), g2=vec(PK_G2, D), g3=vec(PK_G3, D), g4=vec(PK_G4, D),
                ln_g=vec(PK_LNG, GMLP_W), ln_b=vec(PK_LNB, GMLP_W),
                b_sp=p[PK_BSP:PK_BSP + 4].reshape(1, 4, CHUNK), w_sp=p[PK_WSP:PKT_ROWS].reshape(1, 4, CHUNK, CHUNK))


def kernel(x, norm_pre_mix, w_in, w_spatial, b_spatial, ln_v_gain, ln_v_bias, w_branch_attn, w_branch_gmlp, w_out, norm_post_mix, norm_pre_mlp, w_mlp_in, w_mlp_out, norm_post_mlp, loss_target, m_norm_pre_mix, m_w_in, m_w_spatial, m_b_spatial, m_ln_v_gain, m_ln_v_bias, m_w_branch_attn, m_w_branch_gmlp, m_w_out, m_norm_post_mix, m_norm_pre_mlp, m_w_mlp_in, m_w_mlp_out, m_norm_post_mlp, v_norm_pre_mix, v_w_in, v_w_spatial, v_b_spatial, v_ln_v_gain, v_ln_v_bias, v_w_branch_attn, v_w_branch_gmlp, v_w_out, v_norm_post_mix, v_norm_pre_mlp, v_w_mlp_in, v_w_mlp_out, v_norm_post_mlp):
    mx, my, mc = _mesh_pos()
    rel = [(0, 0), (1, 0), (0, 1), (1, 1)]
    chip_of = [2 * (mx ^ dx) + (my ^ dy) for dx, dy in rel]
    idx = jnp.stack([2 * k + mc for k in chip_of] + chip_of).astype(jnp.int32)

    (g_win,) = _all_gather_weights([w_in[0].astype(BF)])
    win = _from_blocks(g_win, True)
    wqkv, wuz, wg = win[:, :3 * AW], win[:, 3 * AW:3 * AW + 2 * GMLP_W], win[:, 3 * AW + 2 * GMLP_W:]
    late = [w_branch_attn[0], w_branch_gmlp[0], w_out[0], w_mlp_in[0], w_mlp_out[0]]
    comm = _FsdpComm([w.astype(BF) for w in late], [True, True, False, True, False], g_win, idx, 4 * mx + 2 * my + mc)

    loss_row, grad_x, dws, small = _local_step(
        x[0], loss_target[0], wqkv, wuz, wg, comm,
        norm_pre_mix, norm_post_mix, norm_pre_mlp, norm_post_mlp, ln_v_gain, ln_v_bias, w_spatial[0], b_spatial[0])

    gblocks = [_to_blocks(g, cs) for g, cs in zip(dws, [True, True, True, False])]
    from_sib = _rs_d2d(gblocks)
    halves = [_rs_add(g, r, idx, "rs_add_%d" % n) for n, (g, r) in enumerate(zip(gblocks, from_sib))]
    from_chips = _rs_ici([s for _, s in halves])
    mlp_own, mlp_chips = comm.finish(after=from_chips[0])
    owns = [own for own, _ in halves] + mlp_own
    names = ["w_in", "w_branch_attn", "w_branch_gmlp", "w_out", "w_mlp_in", "w_mlp_out"]
    big_w = [w_in[0], w_branch_attn[0], w_branch_gmlp[0], w_out[0], w_mlp_in[0], w_mlp_out[0]]
    big_m = [m_w_in[0], m_w_branch_attn[0], m_w_branch_gmlp[0], m_w_out[0], m_w_mlp_in[0], m_w_mlp_out[0]]
    big_v = [v_w_in[0], v_w_branch_attn[0], v_w_branch_gmlp[0], v_w_out[0], v_w_mlp_in[0], v_w_mlp_out[0]]
    upd = {nm: _adam_shard(own, r, w, m, v, "adam_" + nm)
           for nm, own, r, w, m, v in zip(names, owns, list(from_chips) + list(mlp_chips), big_w, big_m, big_v)}

    pkt = _small_allreduce(_pack_small(loss_row, small))
    small_w = dict(g1=norm_pre_mix, g2=norm_post_mix, g3=norm_pre_mlp, g4=norm_post_mlp, ln_g=ln_v_gain,
                   ln_b=ln_v_bias, b_sp=b_spatial[0], w_sp=w_spatial[0])
    small_m = dict(g1=m_norm_pre_mix, g2=m_norm_post_mix, g3=m_norm_pre_mlp, g4=m_norm_post_mlp, ln_g=m_ln_v_gain,
                   ln_b=m_ln_v_bias, b_sp=m_b_spatial[0], w_sp=m_w_spatial[0])
    small_v = dict(g1=v_norm_pre_mix, g2=v_norm_post_mix, g3=v_norm_pre_mlp, g4=v_norm_post_mlp, ln_g=v_ln_v_gain,
                   ln_b=v_ln_v_bias, b_sp=v_b_spatial[0], w_sp=v_w_spatial[0])
    zero_row = jnp.zeros((1, 128), F32)
    ones_row = jnp.ones((1, 128), F32)
    sd, sm, sv = _adam_small(pkt, _pack_small(zero_row, small_w), _pack_small(zero_row, small_m),
                             _pack_small(ones_row, small_v))
    loss = pkt[0, 0]
    sg, sd, sm, sv = (_unpack_small(p) for p in (pkt, sd, sm, sv))

    order = ["g1", "w_in", "w_sp", "b_sp", "ln_g", "ln_b", "w_branch_attn", "w_branch_gmlp", "w_out", "g2", "g3",
             "w_mlp_in", "w_mlp_out", "g4"]

    def pick(which):
        out = []
        for nm in order:
            if nm in upd:
                out.append(upd[nm][which][None])
            else:
                out.append((sg, sd, sm, sv)[which][nm])
        return out

    return (loss, grad_x[None], *pick(0), *pick(1), *pick(2), *pick(3))
```

```python
import functools
import math

import jax
import jax.numpy as jnp
from jax import lax
from jax.experimental import pallas as pl
from jax.experimental.pallas import tpu as pltpu

D = 1024
HD = 64
NSLOT = 4
GW = NSLOT * HD
DILS = (1, 4, 16)
QB = 128
ATTN_NSUB = 2
AW = 3 * GW
GMLP_W = 512
CHUNK = 128
DFF = 4096
EPS = 1e-6
ROPE_THETA = 10000.0
SCALE = HD ** -0.5
NEG = -1e30

ADAM_LR = 0.001
ADAM_B1 = 0.9
ADAM_B2 = 0.999
ADAM_EPS = 1e-08
ADAM_WD = 0.01
ADAM_STEP = 10

BF = jnp.bfloat16
F32 = jnp.float32
MESH = pl.DeviceIdType.MESH
VMEM_LIMIT = 56 * 1024 * 1024

PK_LOSS, PK_G1, PK_G2, PK_G3, PK_G4, PK_LNG, PK_LNB, PK_BSP, PK_WSP, PKT_ROWS = 0, 8, 16, 24, 32, 40, 44, 48, 56, 568


def _cp(sem):
    return pltpu.CompilerParams(dimension_semantics=sem, vmem_limit_bytes=VMEM_LIMIT)


def _dot(a, b):
    return jnp.dot(a, b, preferred_element_type=F32)


def _dot_nt(a, b):
    return lax.dot_general(a, b, (((1,), (1,)), ((), ())), preferred_element_type=F32)


def _dot_tn(a, b):
    return lax.dot_general(a, b, (((0,), (0,)), ((), ())), preferred_element_type=F32)


def _gelu(x):
    return jax.nn.gelu(x, approximate=True)


def _gelu_grad(x):
    k = math.sqrt(2.0 / math.pi)
    t = jnp.tanh(k * (x + 0.044715 * x * x * x))
    return 0.5 * (1.0 + t) + 0.5 * x * (1.0 - t * t) * (k * (1.0 + 3.0 * 0.044715 * x * x))


def _swap_halves(t):
    w = t.shape[1]
    lane = lax.broadcasted_iota(jnp.int32, t.shape, 1)
    first = (lane & (HD - 1)) < (HD // 2)
    return jnp.where(first, pltpu.roll(t, w - HD // 2, 1), pltpu.roll(t, HD // 2, 1))


def _head_mask(hh):
    lane = lax.broadcasted_iota(jnp.int32, (1, GW), 1)
    return jnp.logical_and(lane >= hh * HD, lane < (hh + 1) * HD)


def _rms_stats(xf):
    return lax.rsqrt(jnp.mean(xf * xf, axis=-1, keepdims=True) + EPS)


def _rms_bwd(xf, r, gain, dout):
    n = xf * r
    t = dout * gain
    dx = r * (t - n * jnp.mean(t * n, axis=-1, keepdims=True))
    return dx, jnp.sum(dout * n, axis=0, keepdims=True)


def _rms_fwd(x, gain, name):
    S = x.shape[0]
    tm = min(512, S)

    def body(x_ref, g_ref, h_ref):
        xf = x_ref[...]
        h_ref[...] = (xf * _rms_stats(xf) * g_ref[...]).astype(BF)

    return pl.pallas_call(
        body, name=name, out_shape=jax.ShapeDtypeStruct((S, D), BF), grid=(S // tm,),
        in_specs=[pl.BlockSpec((tm, D), lambda i: (i, 0)), pl.BlockSpec((1, D), lambda i: (0, 0))],
        out_specs=pl.BlockSpec((tm, D), lambda i: (i, 0)), compiler_params=_cp(("parallel",)))(x, gain)


def _proj(h, w, epi, name, tn=256):
    S, K = h.shape
    N = w.shape[1]
    tm = min(2048, S)

    def body(h_ref, w_ref, o_ref):
        y = _dot(h_ref[...], w_ref[...])
        if epi == "sigmoid":
            y = jax.nn.sigmoid(y)
        o_ref[...] = y.astype(BF)

    return pl.pallas_call(
        body, name=name, out_shape=jax.ShapeDtypeStruct((S, N), BF), grid=(S // tm, N // tn),
        in_specs=[pl.BlockSpec((tm, K), lambda i, j: (i, 0)), pl.BlockSpec((K, tn), lambda i, j: (0, j))],
        out_specs=pl.BlockSpec((tm, tn), lambda i, j: (i, j)),
        compiler_params=_cp(("parallel", "arbitrary")))(h, w)


def _dilate_store(val, scr, o_ref, lead, d):
    rows = val.shape[0]
    if d == 1:
        o_ref[lead + (0,)] = val.astype(o_ref.dtype)
        return
    for hf in range(2):
        scr[hf, pl.ds(0, rows), :] = val[:, hf * 128:(hf + 1) * 128]
    for r in range(d):
        for hf in range(2):
            o_ref[lead + (r, slice(None), slice(hf * 128, (hf + 1) * 128))] = (
                scr[hf, pl.ds(r, rows // d, stride=d), :].astype(o_ref.dtype))


def _undilate_load(i_ref, lead, d, scr, rows):
    if d == 1:
        return i_ref[lead + (0,)].astype(F32)
    for r in range(d):
        for hf in range(2):
            scr[hf, pl.ds(r, rows // d, stride=d), :] = (
                i_ref[lead + (r, slice(None), slice(hf * 128, (hf + 1) * 128))].astype(F32))
    return jnp.concatenate([scr[0, pl.ds(0, rows), :], scr[1, pl.ds(0, rows), :]], axis=1)


def _rope_fwd(y, c_ref, s_ref):
    cosv = jnp.concatenate([c_ref[...]] * 2, axis=1)
    sinv = jnp.concatenate([s_ref[...]] * 2, axis=1)
    return y * cosv + _swap_halves(y) * sinv


def _rope_bwd(dy, c_ref, s_ref):
    cosv = jnp.concatenate([c_ref[...]] * 2, axis=1)
    sinv = jnp.concatenate([s_ref[...]] * 2, axis=1)
    return dy * cosv + _swap_halves(dy * sinv)


def _proj_qkv(h, wqkv, cos_t, sin_t, g, d, name):
    S, K = h.shape
    tm = min(2048, S)

    def body(h_ref, w_ref, c_ref, s_ref, o_ref, scr):
        j = pl.program_id(1)
        y = _dot(h_ref[...], w_ref[...])
        y = jnp.where(j < 2, _rope_fwd(y, c_ref, s_ref), y)
        _dilate_store(y, scr, o_ref, (), d)

    return pl.pallas_call(
        body, name=name, out_shape=jax.ShapeDtypeStruct((3, d, S // d, GW), BF), grid=(S // tm, 3),
        in_specs=[pl.BlockSpec((tm, K), lambda i, j: (i, 0)), pl.BlockSpec((K, GW), lambda i, j: (0, 3 * j + g)),
                  pl.BlockSpec((tm, 128), lambda i, j: (i, 0)), pl.BlockSpec((tm, 128), lambda i, j: (i, 0))],
        out_specs=pl.BlockSpec((None, d, tm // d, GW), lambda i, j: (j, 0, i, 0)),
        scratch_shapes=[pltpu.VMEM((2, tm, 128), F32)],
        compiler_params=_cp(("parallel", "arbitrary")))(h, wqkv, cos_t, sin_t)


def _band_masks(first_step):
    row = lax.broadcasted_iota(jnp.int32, (QB, 2 * QB), 0)
    col = lax.broadcasted_iota(jnp.int32, (QB, 2 * QB), 1)
    band = jnp.logical_and(col >= row, col <= row + QB)
    return band, jnp.logical_and(band, jnp.logical_or(col >= QB, jnp.logical_not(first_step)))


def _attn_fwd(qkv, name):
    _, d, L, _ = qkv.shape
    nsub = min(ATTN_NSUB, L // QB)
    R = nsub * QB
    nsteps = L // R

    def body(q_ref, kp_ref, kc_ref, vp_ref, vc_ref, o_ref, lse_ref):
        i = pl.program_id(1)
        band, band_first = _band_masks(i == 0)
        kfull = jnp.concatenate([kp_ref[...], kc_ref[...]], axis=0)
        vfull = jnp.concatenate([vp_ref[...], vc_ref[...]], axis=0)
        chains = [(sb, hh) for sb in range(nsub) for hh in range(NSLOT)]
        win = lambda t, sb: t[sb * QB:(sb + 2) * QB]
        scores = []
        for sb, hh in chains:
            qh = jnp.where(_head_mask(hh), q_ref[sb * QB:(sb + 1) * QB, :], 0)
            scores.append(_dot_nt(qh, win(kfull, sb)))
        soft = []
        for (sb, hh), sc in zip(chains, scores):
            sc = jnp.where(band_first if sb == 0 else band, sc * SCALE, NEG)
            m = jnp.max(sc, axis=1, keepdims=True)
            p = jnp.exp(sc - m)
            den = jnp.sum(p, axis=1, keepdims=True)
            soft.append((p.astype(BF), den, m + jnp.log(den)))
        accs = [_dot(p, win(vfull, sb)) for (sb, hh), (p, _, _) in zip(chains, soft)]
        for sb in range(nsub):
            o = jnp.zeros((QB, GW), F32)
            lse = jnp.zeros((QB, GW), F32)
            for hh in range(NSLOT):
                hm = _head_mask(hh)
                _, den, lrow = soft[sb * NSLOT + hh]
                o = o + jnp.where(hm, accs[sb * NSLOT + hh] / den, 0.0)
                lse = lse + jnp.where(hm, lrow, 0.0)
            o_ref[sb * QB:(sb + 1) * QB, :] = o
            lse_ref[sb * QB:(sb + 1) * QB, :] = lse

    prev = lambda i: jnp.maximum(i * nsub - 1, 0)
    cur = lambda t: pl.BlockSpec((None, None, R, GW), lambda r, i: (t, r, i, 0))
    prv = lambda t: pl.BlockSpec((None, None, QB, GW), lambda r, i: (t, r, prev(i), 0))
    out = pl.BlockSpec((None, R, GW), lambda r, i: (r, i, 0))
    return pl.pallas_call(
        body, name=name, grid=(d, nsteps),
        out_shape=(jax.ShapeDtypeStruct((d, L, GW), F32), jax.ShapeDtypeStruct((d, L, GW), F32)),
        in_specs=[cur(0), prv(1), cur(1), prv(2), cur(2)],
        out_specs=(out, out), compiler_params=_cp(("parallel", "arbitrary")))(qkv, qkv, qkv, qkv, qkv)


def _attn_combine(os_, lses):
    S = os_[0].shape[1]
    tm = min(1024, S)

    def body(o0, o1, o2, l0, l1, l2, y_ref, j0, j1, j2, scr):
        os_nat = [_undilate_load(o, (), d, scr, tm) for o, d in zip((o0, o1, o2), DILS)]
        a, b, c = [_undilate_load(l, (), d, scr, tm) for l, d in zip((l0, l1, l2), DILS)]
        m = jnp.maximum(jnp.maximum(a, b), c)
        wa, wb, wc = jnp.exp(a - m), jnp.exp(b - m), jnp.exp(c - m)
        den = wa + wb + wc
        y_ref[...] = ((wa * os_nat[0] + wb * os_nat[1] + wc * os_nat[2]) / den).astype(BF)
        lse = m + jnp.log(den)
        for j_ref, d in zip((j0, j1, j2), DILS):
            _dilate_store(lse, scr, j_ref, (), d)

    dil = lambda d: pl.BlockSpec((d, tm // d, GW), lambda i: (0, i, 0))
    dshape = lambda d: jax.ShapeDtypeStruct((d, S // d, GW), F32)
    return pl.pallas_call(
        body, name="attn_combine", grid=(S // tm,),
        out_shape=(jax.ShapeDtypeStruct((S, GW), BF),) + tuple(dshape(d) for d in DILS),
        in_specs=[dil(d) for d in DILS] * 2,
        out_specs=(pl.BlockSpec((tm, GW), lambda i: (i, 0)),) + tuple(dil(d) for d in DILS),
        scratch_shapes=[pltpu.VMEM((2, tm, 128), F32)],
        compiler_params=_cp(("parallel",)))(*os_, *lses)


def _tril(upper=False):
    row = lax.broadcasted_iota(jnp.int32, (CHUNK, CHUNK), 0)
    col = lax.broadcasted_iota(jnp.int32, (CHUNK, CHUNK), 1)
    return row <= col if upper else col <= row


def _ln_fwd(z, gain, bias):
    mu = jnp.mean(z, axis=-1, keepdims=True)
    zc = z - mu
    rstd = lax.rsqrt(jnp.mean(zc * zc, axis=-1, keepdims=True) + EPS)
    zhat = zc * rstd
    return zhat, rstd, zhat * gain + bias


def _gmlp_fwd(puz, ln_g, ln_b, w_sp, b_sp_t):
    S = puz.shape[0]
    tm = min(512, S)
    nch = tm // CHUNK

    def body(p_ref, g_ref, b_ref, w_ref, bt_ref, o_ref):
        tril = _tril()
        ws = [jnp.where(tril, w_ref[gg], 0.0).astype(BF) for gg in range(4)]
        for ch in range(nch):
            rows = slice(ch * CHUNK, (ch + 1) * CHUNK)
            z = _gelu(p_ref[rows, GMLP_W:].astype(F32))
            _, _, zn = _ln_fwd(z, g_ref[...], b_ref[...])
            zn = zn.astype(BF)
            for gg in range(4):
                cols = slice(gg * CHUNK, (gg + 1) * CHUNK)
                sz = _dot(ws[gg], zn[:, cols]) + bt_ref[:, gg:gg + 1]
                u = _gelu(p_ref[rows, cols].astype(F32))
                o_ref[rows, cols] = (u * sz).astype(BF)

    return pl.pallas_call(
        body, name="gmlp_fwd", out_shape=jax.ShapeDtypeStruct((S, GMLP_W), BF), grid=(S // tm,),
        in_specs=[pl.BlockSpec((tm, 2 * GMLP_W), lambda i: (i, 0)),
                  pl.BlockSpec((1, GMLP_W), lambda i: (0, 0)), pl.BlockSpec((1, GMLP_W), lambda i: (0, 0)),
                  pl.BlockSpec((4, CHUNK, CHUNK), lambda i: (0, 0, 0)), pl.BlockSpec((CHUNK, 4), lambda i: (0, 0))],
        out_specs=pl.BlockSpec((tm, GMLP_W), lambda i: (i, 0)),
        compiler_params=_cp(("parallel",)))(puz, ln_g, ln_b, w_sp, b_sp_t)


def _merge_fwd(ya, yg, gates, wba, wbg, wout, x, g2, g3):
    S = x.shape[0]
    tm = min(512, S)

    def body(ya_ref, yg_ref, gt_ref, wba_ref, wbg_ref, wo_ref, x_ref, g2_ref, g3_ref,
             a_ref, gb_ref, mg_ref, y_ref, x2_ref, h3_ref):
        a = _dot(ya_ref[...], wba_ref[...])
        b = _dot(yg_ref[...], wbg_ref[...])
        a_ref[...] = a.astype(BF)
        gb_ref[...] = b.astype(BF)
        merged = (gt_ref[:, :D].astype(F32) * a + gt_ref[:, D:].astype(F32) * b).astype(BF)
        mg_ref[...] = merged
        y = _dot(merged, wo_ref[...])
        y_ref[...] = y
        x2 = x_ref[...] + y * _rms_stats(y) * g2_ref[...]
        x2_ref[...] = x2
        h3_ref[...] = (x2 * _rms_stats(x2) * g3_ref[...]).astype(BF)

    row = lambda w: pl.BlockSpec((tm, w), lambda i: (i, 0))
    full = lambda s: pl.BlockSpec(s, lambda i: (0, 0))
    return pl.pallas_call(
        body, name="merge_fwd", grid=(S // tm,),
        out_shape=(jax.ShapeDtypeStruct((S, D), BF), jax.ShapeDtypeStruct((S, D), BF), jax.ShapeDtypeStruct((S, D), BF),
                   jax.ShapeDtypeStruct((S, D), F32), jax.ShapeDtypeStruct((S, D), F32), jax.ShapeDtypeStruct((S, D), BF)),
        in_specs=[row(GW), row(GMLP_W), row(2 * D), full((GW, D)), full((GMLP_W, D)), full((D, D)), row(D),
                  full((1, D)), full((1, D))],
        out_specs=(row(D), row(D), row(D), row(D), row(D), row(D)),
        compiler_params=_cp(("parallel",)))(ya, yg, gates, wba, wbg, wout, x, g2, g3)


def _mlp_fwd(h3, wmi, wmo, x2, tgt, g4):
    S = x2.shape[0]
    tm = min(1024, S)
    tf = 512
    nf = DFF // tf

    def body(h_ref, wi_ref, wo_ref, x2_ref, t_ref, g4_ref, a_ref, dy2_ref, dout_ref, loss_ref, dg4_ref, acc_ref):
        i, c = pl.program_id(0), pl.program_id(1)

        @pl.when(jnp.logical_and(i == 0, c == 0))
        def _():
            loss_ref[...] = jnp.zeros_like(loss_ref)
            dg4_ref[...] = jnp.zeros_like(dg4_ref)

        a = jnp.maximum(_dot(h_ref[...], wi_ref[...]), 0.0)
        a_ref[...] = a.astype(BF)
        part = _dot((a * a).astype(BF), wo_ref[...])

        @pl.when(c == 0)
        def _():
            acc_ref[...] = part

        @pl.when(c > 0)
        def _():
            acc_ref[...] += part

        @pl.when(c == nf - 1)
        def _():
            y2 = acc_ref[...]
            r = _rms_stats(y2)
            out = x2_ref[...] + y2 * r * g4_ref[...]
            err = out - t_ref[...]
            sq = err * err
            tot = jnp.sum(jnp.sum(sq, axis=1, keepdims=True), axis=0, keepdims=True) * (0.5 / D)
            lane = lax.broadcasted_iota(jnp.int32, (1, 128), 1)
            loss_ref[...] += jnp.where(lane == 0, tot, 0.0)
            dout = err * (1.0 / D)
            dout_ref[...] = dout
            dy2, dg = _rms_bwd(y2, r, g4_ref[...], dout)
            dy2_ref[...] = dy2.astype(BF)
            dg4_ref[...] += dg

    row = pl.BlockSpec((tm, D), lambda i, c: (i, 0))
    return pl.pallas_call(
        body, name="mlp_fwd", grid=(S // tm, nf),
        out_shape=(jax.ShapeDtypeStruct((S, DFF), BF), jax.ShapeDtypeStruct((S, D), BF), jax.ShapeDtypeStruct((S, D), F32),
                   jax.ShapeDtypeStruct((1, 128), F32), jax.ShapeDtypeStruct((1, D), F32)),
        in_specs=[row, pl.BlockSpec((D, tf), lambda i, c: (0, c)), pl.BlockSpec((tf, D), lambda i, c: (c, 0)),
                  row, row, pl.BlockSpec((1, D), lambda i, c: (0, 0))],
        out_specs=(pl.BlockSpec((tm, tf), lambda i, c: (i, c)), row, row,
                   pl.BlockSpec((1, 128), lambda i, c: (0, 0)), pl.BlockSpec((1, D), lambda i, c: (0, 0))),
        scratch_shapes=[pltpu.VMEM((tm, D), F32)],
        compiler_params=_cp(("arbitrary", "arbitrary")))(h3, wmi, wmo, x2, tgt, g4)


def _mlp_bwd(dy2, a, wmo, wmi, x2, y, dout, g2, g3):
    S = x2.shape[0]
    tm = min(512, S)
    tf = 1024
    nf = DFF // tf

    def body(dy2_ref, a_ref, wo_ref, wi_ref, x2_ref, y_ref, dout_ref, g2_ref, g3_ref,
             dpre_ref, dx2_ref, dy_ref, dg3_ref, dg2_ref, acc_ref):
        i, c = pl.program_id(0), pl.program_id(1)

        @pl.when(jnp.logical_and(i == 0, c == 0))
        def _():
            dg3_ref[...] = jnp.zeros_like(dg3_ref)
            dg2_ref[...] = jnp.zeros_like(dg2_ref)

        da2 = _dot_nt(dy2_ref[...], wo_ref[...])
        dpre = (2.0 * a_ref[...].astype(F32) * da2).astype(BF)
        dpre_ref[...] = dpre
        part = _dot_nt(dpre, wi_ref[...])

        @pl.when(c == 0)
        def _():
            acc_ref[...] = part

        @pl.when(c > 0)
        def _():
            acc_ref[...] += part

        @pl.when(c == nf - 1)
        def _():
            x2 = x2_ref[...]
            dx3, dg3 = _rms_bwd(x2, _rms_stats(x2), g3_ref[...], acc_ref[...])
            dx2 = dout_ref[...] + dx3
            dx2_ref[...] = dx2
            dg3_ref[...] += dg3
            yv = y_ref[...]
            dy, dg2 = _rms_bwd(yv, _rms_stats(yv), g2_ref[...], dx2)
            dy_ref[...] = dy.astype(BF)
            dg2_ref[...] += dg2

    row = pl.BlockSpec((tm, D), lambda i, c: (i, 0))
    vec = pl.BlockSpec((1, D), lambda i, c: (0, 0))
    return pl.pallas_call(
        body, name="mlp_bwd", grid=(S // tm, nf),
        out_shape=(jax.ShapeDtypeStruct((S, DFF), BF), jax.ShapeDtypeStruct((S, D), F32), jax.ShapeDtypeStruct((S, D), BF),
                   jax.ShapeDtypeStruct((1, D), F32), jax.ShapeDtypeStruct((1, D), F32)),
        in_specs=[row, pl.BlockSpec((tm, tf), lambda i, c: (i, c)), pl.BlockSpec((tf, D), lambda i, c: (c, 0)),
                  pl.BlockSpec((D, tf), lambda i, c: (0, c)), row, row, row, vec, vec],
        out_specs=(pl.BlockSpec((tm, tf), lambda i, c: (i, c)), row, row, vec, vec),
        scratch_shapes=[pltpu.VMEM((tm, D), F32)],
        compiler_params=_cp(("arbitrary", "arbitrary")))(dy2, a, wmo, wmi, x2, y, dout, g2, g3)


def _mm_tn(a, b, name, square_a=False, tm=1024, tn=1024, tk=2048):
    S, M = a.shape
    N = b.shape[1]
    tm, tk = min(tm, M), min(tk, S)
    tn = max(t for t in range(128, min(tn, N) + 1, 128) if N % t == 0)
    assert M % tm == 0 and S % tk == 0
    nk = S // tk

    def body(a_ref, b_ref, o_ref):
        k = pl.program_id(2)
        av = a_ref[...]
        if square_a:
            av = av * av
        part = _dot_tn(av, b_ref[...])

        @pl.when(k == 0)
        def _():
            o_ref[...] = part

        @pl.when(k > 0)
        def _():
            o_ref[...] += part

    return pl.pallas_call(
        body, name=name, out_shape=jax.ShapeDtypeStruct((M, N), F32), grid=(M // tm, N // tn, nk),
        in_specs=[pl.BlockSpec((tk, tm), lambda i, j, k: (k, i)), pl.BlockSpec((tk, tn), lambda i, j, k: (k, j))],
        out_specs=pl.BlockSpec((tm, tn), lambda i, j, k: (i, j)),
        compiler_params=_cp(("parallel", "parallel", "arbitrary")))(a, b)


def _outproj_bwd(dy, wout, abr, gbr, gates, wba, wbg, ya, tie):
    S = dy.shape[0]
    tm = min(512, S)

    def body(dy_ref, wo_ref, a_ref, b_ref, gt_ref, wba_ref, wbg_ref, ya_ref, tie_ref,
             dgt_ref, da_ref, db_ref, dyg_ref, e0, e1, e2, s0, s1, s2, scr):
        dm = _dot_nt(dy_ref[...], wo_ref[...])
        ga, gb = gt_ref[:, :D].astype(F32), gt_ref[:, D:].astype(F32)
        dgt_ref[:, :D] = (dm * a_ref[...].astype(F32) * ga * (1.0 - ga)).astype(BF)
        dgt_ref[:, D:] = (dm * b_ref[...].astype(F32) * gb * (1.0 - gb)).astype(BF)
        da = (dm * ga).astype(BF)
        db = (dm * gb).astype(BF)
        da_ref[...] = da
        db_ref[...] = db
        dyg_ref[...] = _dot_nt(db, wbg_ref[...]).astype(BF)
        dya = _dot_nt(da, wba_ref[...]).astype(BF).astype(F32)
        dyy = dya * ya_ref[...].astype(F32)
        dsum = jnp.zeros((tm, GW), F32)
        for hh in range(NSLOT):
            hm = _head_mask(hh)
            dsum = dsum + jnp.where(hm, jnp.sum(jnp.where(hm, dyy, 0.0), axis=1, keepdims=True), 0.0)
        for e_ref, s_ref, d in zip((e0, e1, e2), (s0, s1, s2), DILS):
            _dilate_store(dya, scr, e_ref, (), d)
            _dilate_store(dsum, scr, s_ref, (), d)

    row = lambda w: pl.BlockSpec((tm, w), lambda i: (i, 0))
    full = lambda s: pl.BlockSpec(s, lambda i: (0, 0))
    dil = lambda d: pl.BlockSpec((d, tm // d, GW), lambda i: (0, i, 0))
    dshape = lambda d, t: jax.ShapeDtypeStruct((d, S // d, GW), t)
    return pl.pallas_call(
        body, name="outproj_bwd", grid=(S // tm,),
        out_shape=(jax.ShapeDtypeStruct((S, 2 * D), BF), jax.ShapeDtypeStruct((S, D), BF), jax.ShapeDtypeStruct((S, D), BF),
                   jax.ShapeDtypeStruct((S, GMLP_W), BF)) + tuple(dshape(d, BF) for d in DILS)
        + tuple(dshape(d, F32) for d in DILS),
        in_specs=[row(D), full((D, D)), row(D), row(D), row(2 * D), full((GW, D)), full((GMLP_W, D)), row(GW),
                  pl.BlockSpec(memory_space=pl.ANY)],
        out_specs=(row(2 * D), row(D), row(D), row(GMLP_W)) + tuple(dil(d) for d in DILS) * 2,
        scratch_shapes=[pltpu.VMEM((2, tm, 128), F32)],
        compiler_params=_cp(("parallel",)))(dy, wout, abr, gbr, gates, wba, wbg, ya, tie)


def _gmlp_bwd(puz, dyg, ln_g, ln_b, w_sp, w_sp_t, b_sp_t):
    S = puz.shape[0]
    tm = min(512, S)
    nch = tm // CHUNK

    def body(p_ref, dy_ref, g_ref, b_ref, w_ref, wt_ref, bt_ref,
             dp_ref, dw_ref, dbs_ref, dg_ref, dbias_ref, dbacc_ref):
        i = pl.program_id(0)

        @pl.when(i == 0)
        def _():
            dw_ref[...] = jnp.zeros_like(dw_ref)
            dbacc_ref[...] = jnp.zeros_like(dbacc_ref)
            dg_ref[...] = jnp.zeros_like(dg_ref)
            dbias_ref[...] = jnp.zeros_like(dbias_ref)

        tril = _tril()
        ws = [jnp.where(tril, w_ref[gg], 0.0).astype(BF) for gg in range(4)]
        triu = _tril(upper=True)
        wts = [jnp.where(triu, wt_ref[gg], 0.0).astype(BF) for gg in range(4)]
        gain = g_ref[...]
        for ch in range(nch):
            rows = slice(ch * CHUNK, (ch + 1) * CHUNK)
            pz = p_ref[rows, GMLP_W:].astype(F32)
            z = _gelu(pz)
            zhat, rstd, zn = _ln_fwd(z, gain, b_ref[...])
            znb = zn.astype(BF)
            dzn_parts = []
            for gg in range(4):
                cols = slice(gg * CHUNK, (gg + 1) * CHUNK)
                pu = p_ref[rows, cols].astype(F32)
                u = _gelu(pu)
                sz = _dot(ws[gg], znb[:, cols]) + bt_ref[:, gg:gg + 1]
                dyv = dy_ref[rows, cols].astype(F32)
                dp_ref[rows, cols] = (dyv * sz * _gelu_grad(pu)).astype(BF)
                dsz = dyv * u
                dbacc_ref[gg] += dsz
                dszb = dsz.astype(BF)
                dw_ref[gg] += _dot_nt(dszb, znb[:, cols])
                dzn_parts.append(_dot(wts[gg], dszb))
            dzn = jnp.concatenate(dzn_parts, axis=1)
            dg_ref[...] += jnp.sum(dzn * zhat, axis=0, keepdims=True)
            dbias_ref[...] += jnp.sum(dzn, axis=0, keepdims=True)
            dzh = dzn * gain
            dz = rstd * (dzh - jnp.mean(dzh, axis=-1, keepdims=True)
                         - zhat * jnp.mean(dzh * zhat, axis=-1, keepdims=True))
            dp_ref[rows, GMLP_W:] = (dz * _gelu_grad(pz)).astype(BF)

        @pl.when(i == pl.num_programs(0) - 1)
        def _():
            for gg in range(4):
                dw_ref[gg] = jnp.where(tril, dw_ref[gg], 0.0)
                dbs_ref[gg] = jnp.sum(dbacc_ref[gg], axis=1, keepdims=True)

    full2 = lambda s: pl.BlockSpec(s, lambda i: (0, 0))
    full3 = lambda s: pl.BlockSpec(s, lambda i: (0, 0, 0))
    return pl.pallas_call(
        body, name="gmlp_bwd", grid=(S // tm,),
        out_shape=(jax.ShapeDtypeStruct((S, 2 * GMLP_W), BF), jax.ShapeDtypeStruct((4, CHUNK, CHUNK), F32),
                   jax.ShapeDtypeStruct((4, CHUNK, 1), F32), jax.ShapeDtypeStruct((1, GMLP_W), F32),
                   jax.ShapeDtypeStruct((1, GMLP_W), F32)),
        in_specs=[pl.BlockSpec((tm, 2 * GMLP_W), lambda i: (i, 0)), pl.BlockSpec((tm, GMLP_W), lambda i: (i, 0)),
                  full2((1, GMLP_W)), full2((1, GMLP_W)), full3((4, CHUNK, CHUNK)), full3((4, CHUNK, CHUNK)),
                  full2((CHUNK, 4))],
        out_specs=(pl.BlockSpec((tm, 2 * GMLP_W), lambda i: (i, 0)), full3((4, CHUNK, CHUNK)), full3((4, CHUNK, 1)),
                   full2((1, GMLP_W)), full2((1, GMLP_W))),
        scratch_shapes=[pltpu.VMEM((4, CHUNK, CHUNK), F32)],
        compiler_params=_cp(("arbitrary",)))(puz, dyg, ln_g, ln_b, w_sp, w_sp_t, b_sp_t)


def _attn_bwd(qkv, dya, dsums, lse, name):
    _, d, L, _ = qkv.shape
    nsub = min(ATTN_NSUB, L // QB)
    R = nsub * QB
    nsteps = L // R

    def body(q_ref, qn_ref, kp_ref, kc_ref, vp_ref, vc_ref, dy_ref, dyn_ref, e_ref, en_ref, l_ref, ln_ref, o_ref):
        i = pl.program_id(1)
        band, band_first = _band_masks(i == 0)
        row = lax.broadcasted_iota(jnp.int32, (QB, QB), 0)
        col = lax.broadcasted_iota(jnp.int32, (QB, QB), 1)
        mask_next = jnp.logical_and(col >= row, i < nsteps - 1)
        kc, vc = kc_ref[...], vc_ref[...]
        kfull = jnp.concatenate([kp_ref[...], kc], axis=0)
        vfull = jnp.concatenate([vp_ref[...], vc], axis=0)
        k_last, v_last = kc[(nsub - 1) * QB:], vc[(nsub - 1) * QB:]
        q_ext = jnp.concatenate([q_ref[...], qn_ref[...]], axis=0)
        dy_ext = jnp.concatenate([dy_ref[...], dyn_ref[...]], axis=0)
        esum, esum_n, lse, lse_n = e_ref[...], en_ref[...], l_ref[...], ln_ref[...]
        win = lambda t, sb: t[sb * QB:(sb + 2) * QB]
        blk = lambda t, sb: t[sb * QB:(sb + 1) * QB]
        hms = [_head_mask(hh) for hh in range(NSLOT)]
        q_hs = [jnp.where(hm, q_ext, 0) for hm in hms]
        dy_hs = [jnp.where(hm, dy_ext, 0) for hm in hms]
        raw = []
        for hh in range(NSLOT):
            tiles = [(_dot_nt(blk(q_hs[hh], sb), win(kfull, sb)), _dot_nt(blk(dy_hs[hh], sb), win(vfull, sb)))
                     for sb in range(nsub)]
            tiles.append((_dot_nt(blk(q_hs[hh], nsub), k_last), _dot_nt(blk(dy_hs[hh], nsub), v_last)))
            raw.append(tiles)
        ps, dss = [], []
        for hh in range(NSLOT):
            rowstat = lambda t: jnp.max(jnp.where(hms[hh], t, -jnp.inf), axis=1, keepdims=True)
            p_h, ds_h = [], []
            for sb in range(nsub + 1):
                sc, dp = raw[hh][sb]
                if sb < nsub:
                    msk, lrow, erow = (band_first if sb == 0 else band), rowstat(blk(lse, sb)), rowstat(blk(esum, sb))
                else:
                    msk, lrow, erow = mask_next, rowstat(lse_n), rowstat(esum_n)
                p = jnp.where(msk, jnp.exp(sc * SCALE - lrow), 0.0)
                p_h.append(p.astype(BF))
                ds_h.append((p * (dp - erow)).astype(BF))
            ps.append(p_h)
            dss.append(ds_h)
        dq = [jnp.zeros((QB, GW), F32) for _ in range(nsub)]
        dk = [jnp.zeros((QB, GW), F32) for _ in range(nsub)]
        dv = [jnp.zeros((QB, GW), F32) for _ in range(nsub)]
        for hh in range(NSLOT):
            for sb in range(nsub):
                dq[sb] = dq[sb] + jnp.where(hms[hh], _dot(dss[hh][sb], win(kfull, sb)), 0.0)
                nxt = lambda t: t[sb + 1][:, :QB] if sb + 1 < nsub else t[nsub]
                dk[sb] = dk[sb] + _dot_tn(jnp.concatenate([dss[hh][sb][:, QB:], nxt(dss[hh])], axis=0), win(q_hs[hh], sb))
                dv[sb] = dv[sb] + _dot_tn(jnp.concatenate([ps[hh][sb][:, QB:], nxt(ps[hh])], axis=0), win(dy_hs[hh], sb))
        for sb in range(nsub):
            rows = slice(sb * QB, (sb + 1) * QB)
            o_ref[0, rows, :] = (dq[sb] * SCALE).astype(BF)
            o_ref[1, rows, :] = (dk[sb] * SCALE).astype(BF)
            o_ref[2, rows, :] = dv[sb].astype(BF)

    prev = lambda i: jnp.maximum(i * nsub - 1, 0)
    nxt = lambda i: jnp.minimum((i + 1) * nsub, L // QB - 1)
    cur4 = lambda t: pl.BlockSpec((None, None, R, GW), lambda r, i: (t, r, i, 0))
    prv4 = lambda t: pl.BlockSpec((None, None, QB, GW), lambda r, i: (t, r, prev(i), 0))
    nxt4 = lambda t: pl.BlockSpec((None, None, QB, GW), lambda r, i: (t, r, nxt(i), 0))
    cur3 = pl.BlockSpec((None, R, GW), lambda r, i: (r, i, 0))
    nxt3 = pl.BlockSpec((None, QB, GW), lambda r, i: (r, nxt(i), 0))
    return pl.pallas_call(
        body, name=name, grid=(d, nsteps), out_shape=jax.ShapeDtypeStruct((3, d, L, GW), BF),
        in_specs=[cur4(0), nxt4(0), prv4(1), cur4(1), prv4(2), cur4(2), cur3, nxt3, cur3, nxt3, cur3, nxt3],
        out_specs=pl.BlockSpec((3, None, R, GW), lambda r, i: (0, r, i, 0)),
        compiler_params=_cp(("parallel", "arbitrary")))(qkv, qkv, qkv, qkv, qkv, qkv, dya, dya, dsums, dsums, lse, lse)


def _inproj_bwd(dqkvs, dpuz, dgates, wqkv, wuz, wg, cos_t, sin_t, x, dx2, g1):
    S = x.shape[0]
    tm = min(512, S)

    def body(d0_ref, d1_ref, d2_ref, dp_ref, dg_ref, wqkv_ref, wuz_ref, wg_ref, c_ref, s_ref,
             x_ref, dx2_ref, g1_ref, gx_ref, dg1_ref, dn_ref, scr):
        i = pl.program_id(0)

        @pl.when(i == 0)
        def _():
            dg1_ref[...] = jnp.zeros_like(dg1_ref)

        for t in range(3):
            for g, (d_ref, d) in enumerate(zip((d0_ref, d1_ref, d2_ref), DILS)):
                piece = _undilate_load(d_ref, (t,), d, scr, tm)
                if t < 2:
                    piece = _rope_bwd(piece, c_ref, s_ref)
                dn_ref[:, (3 * t + g) * GW:(3 * t + g + 1) * GW] = piece.astype(BF)
        dh = (_dot_nt(dn_ref[...], wqkv_ref[...]) + _dot_nt(dp_ref[...], wuz_ref[...])
              + _dot_nt(dg_ref[...], wg_ref[...]))
        xv = x_ref[...]
        dx1, dg1 = _rms_bwd(xv, _rms_stats(xv), g1_ref[...], dh)
        gx_ref[...] = dx2_ref[...] + dx1
        dg1_ref[...] += dg1

    row = lambda w: pl.BlockSpec((tm, w), lambda i: (i, 0))
    full = lambda s: pl.BlockSpec(s, lambda i: (0, 0))
    dil = lambda d: pl.BlockSpec((3, d, tm // d, GW), lambda i: (0, 0, i, 0))
    return pl.pallas_call(
        body, name="inproj_bwd", grid=(S // tm,),
        out_shape=(jax.ShapeDtypeStruct((S, D), F32), jax.ShapeDtypeStruct((1, D), F32),
                   jax.ShapeDtypeStruct((S, 3 * AW), BF)),
        in_specs=[dil(d) for d in DILS] + [row(2 * GMLP_W), row(2 * D), full((D, 3 * AW)), full((D, 2 * GMLP_W)),
                                            full((D, 2 * D)), row(128), row(128), row(D), row(D), full((1, D))],
        out_specs=(row(D), full((1, D)), row(3 * AW)),
        scratch_shapes=[pltpu.VMEM((2, tm, 128), F32)],
        compiler_params=_cp(("arbitrary",)))(*dqkvs, dpuz, dgates, wqkv, wuz, wg, cos_t, sin_t, x, dx2, g1)


def _adam_math(w, g, m, v):
    m2 = ADAM_B1 * m + (1.0 - ADAM_B1) * g
    v2 = ADAM_B2 * v + (1.0 - ADAM_B2) * (g * g)
    m_hat = m2 / (1.0 - ADAM_B1 ** ADAM_STEP)
    v_hat = v2 / (1.0 - ADAM_B2 ** ADAM_STEP)
    delta = -ADAM_LR * (m_hat / (jnp.sqrt(v_hat) + ADAM_EPS) + ADAM_WD * w)
    return delta, m2, v2


def _adam_shard(own, recv, w, m, v, name):
    R, C = w.shape
    tr = min(256, R)

    def body(own_ref, r_ref, w_ref, m_ref, v_ref, g_ref, d_ref, m2_ref, v2_ref):
        g = own_ref[...] + r_ref[0].astype(F32) + r_ref[1].astype(F32) + r_ref[2].astype(F32)
        g_ref[...] = g
        d_ref[...], m2_ref[...], v2_ref[...] = _adam_math(w_ref[...], g, m_ref[...], v_ref[...])

    spec = pl.BlockSpec((tr, C), lambda i: (i, 0))
    out = jax.ShapeDtypeStruct((R, C), F32)
    return pl.pallas_call(
        body, name=name, grid=(R // tr,), out_shape=(out, out, out, out),
        in_specs=[spec, pl.BlockSpec((3, tr, C), lambda i: (0, i, 0)), spec, spec, spec],
        out_specs=(spec, spec, spec, spec), compiler_params=_cp(("parallel",)))(own, recv, w, m, v)


def _adam_small(gp, wp, mp, vp):
    def body(g_ref, w_ref, m_ref, v_ref, d_ref, m2_ref, v2_ref):
        d_ref[...], m2_ref[...], v2_ref[...] = _adam_math(w_ref[...], g_ref[...], m_ref[...], v_ref[...])

    out = jax.ShapeDtypeStruct(gp.shape, F32)
    return pl.pallas_call(body, name="adam_small", out_shape=(out, out, out),
                          compiler_params=pltpu.CompilerParams(vmem_limit_bytes=VMEM_LIMIT))(gp, wp, mp, vp)


def _rs_add(gblocks, recv, idx, name):
    _, R, C = gblocks.shape
    tr = min(256, R)

    def body(t_ref, g_ref, r_ref, own_ref, send_ref):
        j = pl.program_id(1)
        s = g_ref[...] + r_ref[...]

        @pl.when(j == 0)
        def _():
            own_ref[...] = s

        @pl.when(j > 0)
        def _():
            send_ref[...] = s.astype(BF)

    grid_spec = pltpu.PrefetchScalarGridSpec(
        num_scalar_prefetch=1, grid=(R // tr, 4),
        in_specs=[pl.BlockSpec((None, tr, C), lambda i, j, t: (t[j], i, 0)),
                  pl.BlockSpec((None, tr, C), lambda i, j, t: (t[4 + j], i, 0))],
        out_specs=[pl.BlockSpec((tr, C), lambda i, j, t: (i, 0)),
                   pl.BlockSpec((None, tr, C), lambda i, j, t: (jnp.maximum(j - 1, 0), i, 0))])
    return pl.pallas_call(
        body, name=name, grid_spec=grid_spec,
        out_shape=(jax.ShapeDtypeStruct((R, C), F32), jax.ShapeDtypeStruct((3, R, C), BF)),
        compiler_params=_cp(("parallel", "arbitrary")))(idx, gblocks, recv)


def _mesh_pos():
    return lax.axis_index("x"), lax.axis_index("y"), lax.axis_index("c")


def _all_gather_weights(shards):
    n = len(shards)

    def body(*refs):
        ins, outs = refs[:n], refs[n:2 * n]
        send_sems, recv_sems, loc_sems = refs[2 * n:]
        x, y, c = _mesh_pos()
        me, sib = (x, y, c), (x, y, 1 - c)
        chips = [(1 - x, y), (x, 1 - y), (1 - x, 1 - y)]

        def rcopy(k, s, block, to, src=None):
            dst = outs[k].at[4 * block[0] + 2 * block[1] + block[2]]
            return pltpu.make_async_remote_copy(
                src_ref=dst if src is None else src, dst_ref=dst, send_sem=send_sems.at[k, s],
                recv_sem=recv_sems.at[k, s], device_id=to, device_id_type=MESH)

        locs = [pltpu.make_async_copy(ins[k], outs[k].at[4 * x + 2 * y + c], loc_sems.at[k]) for k in range(n)]
        for cp in locs:
            cp.start()
        started = []
        for k in range(n):
            for j, chip in enumerate(chips):
                started.append(rcopy(k, 1 + j, me, (*chip, c), src=ins[k]))
                started[-1].start()
        for k in range(n):
            started.append(rcopy(k, 0, me, sib, src=ins[k]))
            started[-1].start()
        for k in range(n):
            for j, chip in enumerate(chips):
                rcopy(k, 1 + j, (*chip, c), me).wait_recv()
                started.append(rcopy(k, 4 + j, (*chip, c), sib))
                started[-1].start()
        for k in range(n):
            rcopy(k, 0, sib, me).wait_recv()
            for j, chip in enumerate(chips):
                rcopy(k, 4 + j, (*chip, 1 - c), me).wait_recv()
        for cp in started:
            cp.wait_send()
        for cp in locs:
            cp.wait()

    hbm = pl.BlockSpec(memory_space=pl.ANY)
    return pl.pallas_call(
        body, name="ag_weights",
        out_shape=tuple(jax.ShapeDtypeStruct((8,) + s.shape, s.dtype) for s in shards),
        in_specs=[hbm] * n, out_specs=(hbm,) * n,
        scratch_shapes=[pltpu.SemaphoreType.DMA((n, 7)), pltpu.SemaphoreType.DMA((n, 7)),
                        pltpu.SemaphoreType.DMA((n,))])(*shards)


def _rs_d2d(gblocks):
    n = len(gblocks)

    def body(*refs):
        ins, outs = refs[:n], refs[n:2 * n]
        send_sems, recv_sems = refs[2 * n:]
        x, y, c = _mesh_pos()
        copies = []
        for k in range(n):
            for kk in range(4):
                copies.append(pltpu.make_async_remote_copy(
                    src_ref=ins[k].at[2 * kk + 1 - c], dst_ref=outs[k].at[kk], send_sem=send_sems.at[k, kk],
                    recv_sem=recv_sems.at[k, kk], device_id=(x, y, 1 - c), device_id_type=MESH))
                copies[-1].start()
        for cp in copies:
            cp.wait()

    hbm = pl.BlockSpec(memory_space=pl.ANY)
    return pl.pallas_call(
        body, name="rs_d2d",
        out_shape=tuple(jax.ShapeDtypeStruct((4,) + g.shape[1:], g.dtype) for g in gblocks),
        in_specs=[hbm] * n, out_specs=(hbm,) * n,
        scratch_shapes=[pltpu.SemaphoreType.DMA((n, 4)), pltpu.SemaphoreType.DMA((n, 4))])(*gblocks)


def _rs_ici(sends):
    n = len(sends)

    def body(*refs):
        ins, outs = refs[:n], refs[n:2 * n]
        send_sems, recv_sems = refs[2 * n:]
        x, y, c = _mesh_pos()
        chips = [(1 - x, y), (x, 1 - y), (1 - x, 1 - y)]
        copies = []
        for k in range(n):
            for j, chip in enumerate(chips):
                copies.append(pltpu.make_async_remote_copy(
                    src_ref=ins[k].at[j], dst_ref=outs[k].at[j], send_sem=send_sems.at[k, j],
                    recv_sem=recv_sems.at[k, j], device_id=(*chip, c), device_id_type=MESH))
                copies[-1].start()
        for cp in copies:
            cp.wait()

    hbm = pl.BlockSpec(memory_space=pl.ANY)
    return pl.pallas_call(
        body, name="rs_ici",
        out_shape=tuple(jax.ShapeDtypeStruct(s.shape, s.dtype) for s in sends),
        in_specs=[hbm] * n, out_specs=(hbm,) * n,
        scratch_shapes=[pltpu.SemaphoreType.DMA((n, 3)), pltpu.SemaphoreType.DMA((n, 3))])(*sends)


_HBM = pl.BlockSpec(memory_space=pltpu.HBM)
_SEM = pl.BlockSpec(memory_space=pltpu.SEMAPHORE)
_EFFECT = pltpu.SideEffectType.DATAFLOW_SIDE_EFFECTING
_RELATIONS = [(dx, dy, dc) for dx in (0, 1) for dy in (0, 1) for dc in (0, 1)][1:]


def _flip(v, d):
    return 1 - v if d else v


def _plan_gather(n):
    def plan(x, y, c):
        return [(k, None, 4 * x + 2 * y + c, (_flip(x, dx), _flip(y, dy), _flip(c, dc)))
                for k in range(n) for dx, dy, dc in _RELATIONS]
    return plan


def _plan_d2d(n):
    def plan(x, y, c):
        return [(k, 2 * kk + 1 - c, kk, (x, y, 1 - c)) for k in range(n) for kk in range(4)]
    return plan


def _plan_ici(n):
    def plan(x, y, c):
        return [(k, j, j, (_flip(x, dx), _flip(y, dy), c))
                for k in range(n) for j, (dx, dy) in enumerate(((1, 0), (0, 1), (1, 1)))]
    return plan


def _plan_copies(plan, src_refs, land_refs, send_sems, recv_sems):
    x, y, c = _mesh_pos()
    return [pltpu.make_async_remote_copy(
        src_ref=src_refs[k] if si is None else src_refs[k].at[si], dst_ref=land_refs[k].at[di],
        send_sem=send_sems.at[n], recv_sem=recv_sems.at[n], device_id=dev, device_id_type=MESH)
        for n, (k, si, di, dev) in enumerate(plan(x, y, c))]


def _exchange_start(srcs, land_shapes, plan, ncopies, name, after):
    n = len(srcs)

    def body(*refs):
        src_refs, land_refs = refs[:n], refs[n:2 * n]
        send_sems, recv_sems = refs[2 * n + len(after)], refs[2 * n + len(after) + 1]
        token = refs[-1]
        for cp in _plan_copies(plan, src_refs, land_refs, send_sems, recv_sems):
            cp.start()
        token[...] = jnp.zeros_like(token)

    lands = [pltpu.with_memory_space_constraint(lax.empty(s, a.dtype), pltpu.HBM) for s, a in zip(land_shapes, srcs)]
    srcs = [pltpu.with_memory_space_constraint(a, pltpu.HBM) for a in srcs]
    outs = pl.pallas_call(
        body, name=name,
        out_shape=(pltpu.SemaphoreType.DMA((ncopies,)), pltpu.SemaphoreType.DMA((ncopies,)))
        + tuple(pltpu.HBM(a.shape, a.dtype) for a in srcs) + tuple(pltpu.HBM(a.shape, a.dtype) for a in lands)
        + (jax.ShapeDtypeStruct((8, 128), F32),),
        in_specs=[_HBM] * (2 * n) + [pl.BlockSpec(memory_space=pl.ANY)] * len(after),
        out_specs=(_SEM, _SEM) + (_HBM,) * (2 * n) + (pl.BlockSpec(memory_space=pltpu.VMEM),),
        input_output_aliases={i: 2 + i for i in range(2 * n)},
        compiler_params=pltpu.CompilerParams(has_side_effects=_EFFECT))(*srcs, *lands, *after)
    return (outs[0], outs[1], list(outs[2:2 + n]), list(outs[2 + n:2 + 2 * n])), outs[-1]


def _exchange_wait(handle, plan, name, after):
    send_sems, recv_sems, srcs, lands = handle
    n = len(srcs)

    def body(*refs):
        src_refs, land_refs = refs[:n], refs[n:2 * n]
        for cp in _plan_copies(plan, src_refs, land_refs, refs[2 * n], refs[2 * n + 1]):
            cp.wait_send()
            cp.wait_recv()

    outs = pl.pallas_call(
        body, name=name,
        out_shape=tuple(pltpu.HBM(a.shape, a.dtype) for a in srcs) + tuple(pltpu.HBM(a.shape, a.dtype) for a in lands),
        in_specs=[_HBM] * (2 * n) + [_SEM, _SEM] + [pl.BlockSpec(memory_space=pl.ANY)] * len(after),
        out_specs=(_HBM,) * (2 * n), input_output_aliases={i: i for i in range(2 * n)},
        compiler_params=pltpu.CompilerParams(has_side_effects=_EFFECT))(*srcs, *lands, send_sems, recv_sems, *after)
    return list(outs[:n]), list(outs[n:])


def _small_allreduce(pkt):
    R = pkt.shape[0]

    def body(p_ref, o_ref, buf, send_sems, recv_sems):
        x, y, c = _mesh_pos()
        me, sib = (x, y, c), (x, y, 1 - c)
        chips = [(1 - x, y), (x, 1 - y), (1 - x, 1 - y)]

        def rcopy(s, block, to, src=None):
            dst = buf.at[4 * block[0] + 2 * block[1] + block[2]]
            return pltpu.make_async_remote_copy(
                src_ref=dst if src is None else src, dst_ref=dst, send_sem=send_sems.at[s],
                recv_sem=recv_sems.at[s], device_id=to, device_id_type=MESH)

        buf[4 * x + 2 * y + c] = p_ref[...]
        started = [rcopy(0, me, sib, src=p_ref)]
        started += [rcopy(1 + j, me, (*chip, c), src=p_ref) for j, chip in enumerate(chips)]
        for cp in started:
            cp.start()
        for j, chip in enumerate(chips):
            rcopy(1 + j, (*chip, c), me).wait_recv()
            started.append(rcopy(4 + j, (*chip, c), sib))
            started[-1].start()
        rcopy(0, sib, me).wait_recv()
        for j, chip in enumerate(chips):
            rcopy(4 + j, (*chip, 1 - c), me).wait_recv()
        for cp in started:
            cp.wait_send()
        acc = buf[0]
        for b in range(1, 8):
            acc = acc + buf[b]
        o_ref[...] = acc

    return pl.pallas_call(
        body, name="small_allreduce", out_shape=jax.ShapeDtypeStruct(pkt.shape, F32),
        in_specs=[pl.BlockSpec(memory_space=pltpu.VMEM)], out_specs=pl.BlockSpec(memory_space=pltpu.VMEM),
        scratch_shapes=[pltpu.VMEM((8, R, 128), F32), pltpu.SemaphoreType.DMA((7,)), pltpu.SemaphoreType.DMA((7,))],
        compiler_params=pltpu.CompilerParams(vmem_limit_bytes=VMEM_LIMIT))(pkt)


def _rope_tables(S):
    half = HD // 2
    inv_freq = ROPE_THETA ** (-jnp.arange(half, dtype=F32) / half)
    ang = jnp.arange(S, dtype=F32)[:, None] * inv_freq[None, :]
    cos, sin = jnp.cos(ang), jnp.sin(ang)
    return jnp.concatenate([cos, cos, cos, cos], axis=1), jnp.concatenate([-sin, sin, -sin, sin], axis=1)


def _to_blocks(g, col_sharded):
    if col_sharded:
        return g.reshape(g.shape[0], 8, g.shape[1] // 8).transpose(1, 0, 2)
    return g.reshape(8, g.shape[0] // 8, g.shape[1])


def _from_blocks(t, col_sharded):
    if col_sharded:
        return t.transpose(1, 0, 2).reshape(t.shape[1], 8 * t.shape[2])
    return t.reshape(8 * t.shape[1], t.shape[2])


class _NoComm:
    def __init__(self, late_weights):
        self._late = late_weights
        self.mlp_grads = None

    def start_token(self):
        return 0.0

    def late_weights(self, after):
        return self._late

    def mlp_grads_ready(self, dwmi, dwmo):
        self.mlp_grads = (dwmi, dwmo)
        return jnp.zeros((8, 128), F32)

    def after_outproj_bwd(self, after):
        return 0.0


class _FsdpComm:
    def __init__(self, late_shards, col_sharded, after, idx, block):
        self._shards, self._col, self._idx, self._block = late_shards, col_sharded, idx, block
        n = len(late_shards)
        self._gather, self._token = _exchange_start(
            late_shards, [(8,) + s.shape for s in late_shards], _plan_gather(n), 7 * n, "ag_late_start", (after,))

    def start_token(self):
        return self._token[0, 0]

    def late_weights(self, after):
        shards, lands = _exchange_wait(self._gather, _plan_gather(len(self._col)), "ag_late_wait", after)
        lands = [lax.dynamic_update_index_in_dim(t, s, self._block, 0) for t, s in zip(lands, shards)]
        return [_from_blocks(t, cs) for t, cs in zip(lands, self._col)]

    def mlp_grads_ready(self, dwmi, dwmo):
        gblocks = [_to_blocks(dwmi, True), _to_blocks(dwmo, False)]
        self._d2d, token = _exchange_start(gblocks, [(4,) + g.shape[1:] for g in gblocks], _plan_d2d(2), 8,
                                           "rs_mlp_d2d_start", (self._token,))
        return token

    def after_outproj_bwd(self, after):
        gblocks, from_sib = _exchange_wait(self._d2d, _plan_d2d(2), "rs_mlp_d2d_wait", after)
        halves = [_rs_add(g, r, self._idx, "rs_add_mlp_%d" % n) for n, (g, r) in enumerate(zip(gblocks, from_sib))]
        self._own = [own for own, _ in halves]
        sends = [s for _, s in halves]
        self._ici, token = _exchange_start(sends, [s.shape for s in sends], _plan_ici(2), 6, "rs_mlp_ici_start",
                                           (self._token,))
        return token[0, 0]

    def finish(self, after):
        return self._own, _exchange_wait(self._ici, _plan_ici(2), "rs_mlp_ici_wait", after)[1]


def _local_step(x, tgt, wqkv, wuz, wg, comm, g1, g2, g3, g4, ln_g, ln_b, w_sp, b_sp):
    S = x.shape[0]
    cos_t, sin_t = _rope_tables(S)
    b_sp_t = b_sp.T
    w_sp_t = w_sp.transpose(0, 2, 1)

    h1 = _rms_fwd(x, g1 + comm.start_token(), "rms_pre_mix")
    qkvs = [_proj_qkv(h1, wqkv, cos_t, sin_t, g, d, "proj_qkv_d%d" % d) for g, d in enumerate(DILS)]
    puz = _proj(h1, wuz, "none", "proj_uz")
    gates = _proj(h1, wg, "sigmoid", "proj_gates")
    fwd = [_attn_fwd(t, "attn_fwd_d%d" % d) for t, d in zip(qkvs, DILS)]
    ya, *lses = _attn_combine([o for o, _ in fwd], [l for _, l in fwd])
    yg = _gmlp_fwd(puz, ln_g, ln_b, w_sp, b_sp_t)
    wba, wbg, wout, wmi, wmo = comm.late_weights(after=(ya, yg, gates))
    abr, gbr, merged, y, x2, h3 = _merge_fwd(ya, yg, gates, wba, wbg, wout, x, g2, g3)
    a, dy2, dout, loss_row, dg4 = _mlp_fwd(h3, wmi, wmo, x2, tgt, g4)

    dpre, dx2, dy, dg3, dg2 = _mlp_bwd(dy2, a, wmo, wmi, x2, y, dout, g2, g3)
    dwmo = _mm_tn(a, dy2, "dw_mlp_out", square_a=True)
    dwmi = _mm_tn(h3, dpre, "dw_mlp_in")
    tie = comm.mlp_grads_ready(dwmi, dwmo)
    dgates, da, db, dyg, *rest = _outproj_bwd(dy, wout, abr, gbr, gates, wba, wbg, ya, tie)
    dyas, dsums = rest[:3], rest[3:]
    dwout = _mm_tn(merged, dy, "dw_out")
    dwba = _mm_tn(ya, da, "dw_branch_attn")
    dwbg = _mm_tn(yg, db, "dw_branch_gmlp")
    dpuz, dwsp, dbs, dlng, dlnb = _gmlp_bwd(puz, dyg, ln_g + comm.after_outproj_bwd(after=(dyg,)), ln_b, w_sp, w_sp_t,
                                            b_sp_t)
    dqkvs = [_attn_bwd(qkvs[g], dyas[g], dsums[g], lses[g], "attn_bwd_d%d" % d) for g, d in enumerate(DILS)]
    grad_x, dg1, dqkv = _inproj_bwd(dqkvs, dpuz, dgates, wqkv, wuz, wg, cos_t, sin_t, x, dx2, g1)
    dwin = jnp.concatenate([_mm_tn(h1, t, "dw_in_%d" % n) for n, t in enumerate((dqkv, dpuz, dgates))], axis=1)
    small = dict(g1=dg1, g2=dg2, g3=dg3, g4=dg4, ln_g=dlng, ln_b=dlnb, b_sp=dbs.reshape(4, CHUNK), w_sp=dwsp)
    return loss_row, grad_x, (dwin, dwba, dwbg, dwout), small


def _pack_small(first_row, t):
    pad = lambda a, rows: jnp.concatenate([a, jnp.zeros((rows - a.shape[0], 128), F32)], axis=0)
    return jnp.concatenate([
        pad(first_row, 8), t["g1"].reshape(8, 128), t["g2"].reshape(8, 128), t["g3"].reshape(8, 128),
        t["g4"].reshape(8, 128), t["ln_g"].reshape(4, 128), t["ln_b"].reshape(4, 128),
        pad(t["b_sp"].reshape(4, 128), 8), t["w_sp"].reshape(4 * CHUNK, 128)], axis=0)


def _unpack_small(p):
    vec = lambda r, n: p[r:r + n // 128].reshape(1, n)
    return dict(g1=vec(PK_G1, D), g2=vec(PK_G2, D), g3=vec(PK_G3, D), g4=vec(PK_G4, D),
                ln_g=vec(PK_LNG, GMLP_W), ln_b=vec(PK_LNB, GMLP_W),
                b_sp=p[PK_BSP:PK_BSP + 4].reshape(1, 4, CHUNK), w_sp=p[PK_WSP:PKT_ROWS].reshape(1, 4, CHUNK, CHUNK))


def kernel(x, norm_pre_mix, w_in, w_spatial, b_spatial, ln_v_gain, ln_v_bias, w_branch_attn, w_branch_gmlp, w_out, norm_post_mix, norm_pre_mlp, w_mlp_in, w_mlp_out, norm_post_mlp, loss_target, m_norm_pre_mix, m_w_in, m_w_spatial, m_b_spatial, m_ln_v_gain, m_ln_v_bias, m_w_branch_attn, m_w_branch_gmlp, m_w_out, m_norm_post_mix, m_norm_pre_mlp, m_w_mlp_in, m_w_mlp_out, m_norm_post_mlp, v_norm_pre_mix, v_w_in, v_w_spatial, v_b_spatial, v_ln_v_gain, v_ln_v_bias, v_w_branch_attn, v_w_branch_gmlp, v_w_out, v_norm_post_mix, v_norm_pre_mlp, v_w_mlp_in, v_w_mlp_out, v_norm_post_mlp):
    mx, my, mc = _mesh_pos()
    rel = [(0, 0), (1, 0), (0, 1), (1, 1)]
    chip_of = [2 * (mx ^ dx) + (my ^ dy) for dx, dy in rel]
    idx = jnp.stack([2 * k + mc for k in chip_of] + chip_of).astype(jnp.int32)

    (g_win,) = _all_gather_weights([w_in[0].astype(BF)])
    win = _from_blocks(g_win, True)
    wqkv, wuz, wg = win[:, :3 * AW], win[:, 3 * AW:3 * AW + 2 * GMLP_W], win[:, 3 * AW + 2 * GMLP_W:]
    late = [w_branch_attn[0], w_branch_gmlp[0], w_out[0], w_mlp_in[0], w_mlp_out[0]]
    comm = _FsdpComm([w.astype(BF) for w in late], [True, True, False, True, False], g_win, idx, 4 * mx + 2 * my + mc)

    loss_row, grad_x, dws, small = _local_step(
        x[0], loss_target[0], wqkv, wuz, wg, comm,
        norm_pre_mix, norm_post_mix, norm_pre_mlp, norm_post_mlp, ln_v_gain, ln_v_bias, w_spatial[0], b_spatial[0])

    gblocks = [_to_blocks(g, cs) for g, cs in zip(dws, [True, True, True, False])]
    from_sib = _rs_d2d(gblocks)
    halves = [_rs_add(g, r, idx, "rs_add_%d" % n) for n, (g, r) in enumerate(zip(gblocks, from_sib))]
    from_chips = _rs_ici([s for _, s in halves])
    mlp_own, mlp_chips = comm.finish(after=(from_chips[0],))
    owns = [own for own, _ in halves] + mlp_own
    names = ["w_in", "w_branch_attn", "w_branch_gmlp", "w_out", "w_mlp_in", "w_mlp_out"]
    big_w = [w_in[0], w_branch_attn[0], w_branch_gmlp[0], w_out[0], w_mlp_in[0], w_mlp_out[0]]
    big_m = [m_w_in[0], m_w_branch_attn[0], m_w_branch_gmlp[0], m_w_out[0], m_w_mlp_in[0], m_w_mlp_out[0]]
    big_v = [v_w_in[0], v_w_branch_attn[0], v_w_branch_gmlp[0], v_w_out[0], v_w_mlp_in[0], v_w_mlp_out[0]]
    upd = {nm: _adam_shard(own, r, w, m, v, "adam_" + nm)
           for nm, own, r, w, m, v in zip(names, owns, list(from_chips) + list(mlp_chips), big_w, big_m, big_v)}

    pkt = _small_allreduce(_pack_small(loss_row, small))
    small_w = dict(g1=norm_pre_mix, g2=norm_post_mix, g3=norm_pre_mlp, g4=norm_post_mlp, ln_g=ln_v_gain,
                   ln_b=ln_v_bias, b_sp=b_spatial[0], w_sp=w_spatial[0])
    small_m = dict(g1=m_norm_pre_mix, g2=m_norm_post_mix, g3=m_norm_pre_mlp, g4=m_norm_post_mlp, ln_g=m_ln_v_gain,
                   ln_b=m_ln_v_bias, b_sp=m_b_spatial[0], w_sp=m_w_spatial[0])
    small_v = dict(g1=v_norm_pre_mix, g2=v_norm_post_mix, g3=v_norm_pre_mlp, g4=v_norm_post_mlp, ln_g=v_ln_v_gain,
                   ln_b=v_ln_v_bias, b_sp=v_b_spatial[0], w_sp=v_w_spatial[0])
    zero_row = jnp.zeros((1, 128), F32)
    ones_row = jnp.ones((1, 128), F32)
    sd, sm, sv = _adam_small(pkt, _pack_small(zero_row, small_w), _pack_small(zero_row, small_m),
                             _pack_small(ones_row, small_v))
    loss = pkt[0, 0]
    sg, sd, sm, sv = (_unpack_small(p) for p in (pkt, sd, sm, sv))

    order = ["g1", "w_in", "w_sp", "b_sp", "ln_g", "ln_b", "w_branch_attn", "w_branch_gmlp", "w_out", "g2", "g3",
             "w_mlp_in", "w_mlp_out", "g4"]

    def pick(which):
        out = []
        for nm in order:
            if nm in upd:
                out.append(upd[nm][which][None])
            else:
                out.append((sg, sd, sm, sv)[which][nm])
        return out

    return (loss, grad_x[None], *pick(0), *pick(1), *pick(2), *pick(3))
```

```python
import functools
import math

import jax
import jax.numpy as jnp
from jax import lax
from jax.experimental import pallas as pl
from jax.experimental.pallas import tpu as pltpu

D = 1024
HD = 64
NSLOT = 4
GW = NSLOT * HD
DILS = (1, 4, 16)
QB = 128
ATTN_NSUB = 2
MLP_TM_FWD = 512
MLP_TM_BWD = 256
AW = 3 * GW
GMLP_W = 512
CHUNK = 128
DFF = 4096
EPS = 1e-6
ROPE_THETA = 10000.0
SCALE = HD ** -0.5
NEG = -1e30

ADAM_LR = 0.001
ADAM_B1 = 0.9
ADAM_B2 = 0.999
ADAM_EPS = 1e-08
ADAM_WD = 0.01
ADAM_STEP = 10

BF = jnp.bfloat16
F32 = jnp.float32
MESH = pl.DeviceIdType.MESH
VMEM_LIMIT = 56 * 1024 * 1024

PK_LOSS, PK_G1, PK_G2, PK_G3, PK_G4, PK_LNG, PK_LNB, PK_BSP, PK_WSP, PKT_ROWS = 0, 8, 16, 24, 32, 40, 44, 48, 56, 568


def _cp(sem):
    return pltpu.CompilerParams(dimension_semantics=sem, vmem_limit_bytes=VMEM_LIMIT)


def _dot(a, b):
    return jnp.dot(a, b, preferred_element_type=F32)


def _dot_nt(a, b):
    return lax.dot_general(a, b, (((1,), (1,)), ((), ())), preferred_element_type=F32)


def _dot_tn(a, b):
    return lax.dot_general(a, b, (((0,), (0,)), ((), ())), preferred_element_type=F32)


def _gelu(x):
    return jax.nn.gelu(x, approximate=True)


def _gelu_grad(x):
    k = math.sqrt(2.0 / math.pi)
    t = jnp.tanh(k * (x + 0.044715 * x * x * x))
    return 0.5 * (1.0 + t) + 0.5 * x * (1.0 - t * t) * (k * (1.0 + 3.0 * 0.044715 * x * x))


def _swap_halves(t):
    w = t.shape[1]
    lane = lax.broadcasted_iota(jnp.int32, t.shape, 1)
    first = (lane & (HD - 1)) < (HD // 2)
    return jnp.where(first, pltpu.roll(t, w - HD // 2, 1), pltpu.roll(t, HD // 2, 1))


def _head_mask(hh):
    lane = lax.broadcasted_iota(jnp.int32, (1, GW), 1)
    return jnp.logical_and(lane >= hh * HD, lane < (hh + 1) * HD)


def _rms_stats(xf):
    return lax.rsqrt(jnp.mean(xf * xf, axis=-1, keepdims=True) + EPS)


def _rms_bwd(xf, r, gain, dout):
    n = xf * r
    t = dout * gain
    dx = r * (t - n * jnp.mean(t * n, axis=-1, keepdims=True))
    return dx, jnp.sum(dout * n, axis=0, keepdims=True)


def _rms_fwd(x, gain, name):
    S = x.shape[0]
    tm = min(512, S)

    def body(x_ref, g_ref, h_ref):
        xf = x_ref[...]
        h_ref[...] = (xf * _rms_stats(xf) * g_ref[...]).astype(BF)

    return pl.pallas_call(
        body, name=name, out_shape=jax.ShapeDtypeStruct((S, D), BF), grid=(S // tm,),
        in_specs=[pl.BlockSpec((tm, D), lambda i: (i, 0)), pl.BlockSpec((1, D), lambda i: (0, 0))],
        out_specs=pl.BlockSpec((tm, D), lambda i: (i, 0)), compiler_params=_cp(("parallel",)))(x, gain)


def _proj(h, w, epi, name, tn=256):
    S, K = h.shape
    N = w.shape[1]
    tm = min(2048, S)

    def body(h_ref, w_ref, o_ref):
        y = _dot(h_ref[...], w_ref[...])
        if epi == "sigmoid":
            y = jax.nn.sigmoid(y)
        o_ref[...] = y.astype(BF)

    return pl.pallas_call(
        body, name=name, out_shape=jax.ShapeDtypeStruct((S, N), BF), grid=(S // tm, N // tn),
        in_specs=[pl.BlockSpec((tm, K), lambda i, j: (i, 0)), pl.BlockSpec((K, tn), lambda i, j: (0, j))],
        out_specs=pl.BlockSpec((tm, tn), lambda i, j: (i, j)),
        compiler_params=_cp(("parallel", "arbitrary")))(h, w)


def _dilate_store(val, scr, o_ref, lead, d):
    rows = val.shape[0]
    if d == 1:
        o_ref[lead + (0,)] = val.astype(o_ref.dtype)
        return
    for hf in range(2):
        scr[hf, pl.ds(0, rows), :] = val[:, hf * 128:(hf + 1) * 128]
    for r in range(d):
        for hf in range(2):
            o_ref[lead + (r, slice(None), slice(hf * 128, (hf + 1) * 128))] = (
                scr[hf, pl.ds(r, rows // d, stride=d), :].astype(o_ref.dtype))


def _undilate_load(i_ref, lead, d, scr, rows):
    if d == 1:
        return i_ref[lead + (0,)].astype(F32)
    for r in range(d):
        for hf in range(2):
            scr[hf, pl.ds(r, rows // d, stride=d), :] = (
                i_ref[lead + (r, slice(None), slice(hf * 128, (hf + 1) * 128))].astype(F32))
    return jnp.concatenate([scr[0, pl.ds(0, rows), :], scr[1, pl.ds(0, rows), :]], axis=1)


def _rope_fwd(y, c_ref, s_ref):
    cosv = jnp.concatenate([c_ref[...]] * 2, axis=1)
    sinv = jnp.concatenate([s_ref[...]] * 2, axis=1)
    return y * cosv + _swap_halves(y) * sinv


def _rope_bwd(dy, c_ref, s_ref):
    cosv = jnp.concatenate([c_ref[...]] * 2, axis=1)
    sinv = jnp.concatenate([s_ref[...]] * 2, axis=1)
    return dy * cosv + _swap_halves(dy * sinv)


def _proj_qkv(h, wqkv, cos_t, sin_t, g, d, name):
    S, K = h.shape
    tm = min(2048, S)

    def body(h_ref, w_ref, c_ref, s_ref, o_ref, scr):
        j = pl.program_id(1)
        y = _dot(h_ref[...], w_ref[...])
        y = jnp.where(j < 2, _rope_fwd(y, c_ref, s_ref), y)
        _dilate_store(y, scr, o_ref, (), d)

    return pl.pallas_call(
        body, name=name, out_shape=jax.ShapeDtypeStruct((3, d, S // d, GW), BF), grid=(S // tm, 3),
        in_specs=[pl.BlockSpec((tm, K), lambda i, j: (i, 0)), pl.BlockSpec((K, GW), lambda i, j: (0, 3 * j + g)),
                  pl.BlockSpec((tm, 128), lambda i, j: (i, 0)), pl.BlockSpec((tm, 128), lambda i, j: (i, 0))],
        out_specs=pl.BlockSpec((None, d, tm // d, GW), lambda i, j: (j, 0, i, 0)),
        scratch_shapes=[pltpu.VMEM((2, tm, 128), F32)],
        compiler_params=_cp(("parallel", "arbitrary")))(h, wqkv, cos_t, sin_t)


def _band_masks(first_step):
    row = lax.broadcasted_iota(jnp.int32, (QB, 2 * QB), 0)
    col = lax.broadcasted_iota(jnp.int32, (QB, 2 * QB), 1)
    band = jnp.logical_and(col >= row, col <= row + QB)
    return band, jnp.logical_and(band, jnp.logical_or(col >= QB, jnp.logical_not(first_step)))


def _attn_fwd(qkv, name):
    _, d, L, _ = qkv.shape
    nsub = min(ATTN_NSUB, L // QB)
    R = nsub * QB
    nsteps = L // R

    def body(q_ref, kp_ref, kc_ref, vp_ref, vc_ref, o_ref, lse_ref):
        i = pl.program_id(1)
        band, band_first = _band_masks(i == 0)
        kfull = jnp.concatenate([kp_ref[...], kc_ref[...]], axis=0)
        vfull = jnp.concatenate([vp_ref[...], vc_ref[...]], axis=0)
        chains = [(sb, hh) for sb in range(nsub) for hh in range(NSLOT)]
        win = lambda t, sb: t[sb * QB:(sb + 2) * QB]
        scores = []
        for sb, hh in chains:
            qh = jnp.where(_head_mask(hh), q_ref[sb * QB:(sb + 1) * QB, :], 0)
            scores.append(_dot_nt(qh, win(kfull, sb)))
        soft = []
        for (sb, hh), sc in zip(chains, scores):
            sc = jnp.where(band_first if sb == 0 else band, sc * SCALE, NEG)
            m = jnp.max(sc, axis=1, keepdims=True)
            p = jnp.exp(sc - m)
            den = jnp.sum(p, axis=1, keepdims=True)
            soft.append((p.astype(BF), den, m + jnp.log(den)))
        accs = [_dot(p, win(vfull, sb)) for (sb, hh), (p, _, _) in zip(chains, soft)]
        for sb in range(nsub):
            o = jnp.zeros((QB, GW), F32)
            lse = jnp.zeros((QB, GW), F32)
            for hh in range(NSLOT):
                hm = _head_mask(hh)
                _, den, lrow = soft[sb * NSLOT + hh]
                o = o + jnp.where(hm, accs[sb * NSLOT + hh] / den, 0.0)
                lse = lse + jnp.where(hm, lrow, 0.0)
            o_ref[sb * QB:(sb + 1) * QB, :] = o
            lse_ref[sb * QB:(sb + 1) * QB, :] = lse

    prev = lambda i: jnp.maximum(i * nsub - 1, 0)
    cur = lambda t: pl.BlockSpec((None, None, R, GW), lambda r, i: (t, r, i, 0))
    prv = lambda t: pl.BlockSpec((None, None, QB, GW), lambda r, i: (t, r, prev(i), 0))
    out = pl.BlockSpec((None, R, GW), lambda r, i: (r, i, 0))
    return pl.pallas_call(
        body, name=name, grid=(d, nsteps),
        out_shape=(jax.ShapeDtypeStruct((d, L, GW), F32), jax.ShapeDtypeStruct((d, L, GW), F32)),
        in_specs=[cur(0), prv(1), cur(1), prv(2), cur(2)],
        out_specs=(out, out), compiler_params=_cp(("parallel", "arbitrary")))(qkv, qkv, qkv, qkv, qkv)


def _attn_combine(os_, lses):
    S = os_[0].shape[1]
    tm = min(1024, S)

    def body(o0, o1, o2, l0, l1, l2, y_ref, j0, j1, j2, scr):
        os_nat = [_undilate_load(o, (), d, scr, tm) for o, d in zip((o0, o1, o2), DILS)]
        a, b, c = [_undilate_load(l, (), d, scr, tm) for l, d in zip((l0, l1, l2), DILS)]
        m = jnp.maximum(jnp.maximum(a, b), c)
        wa, wb, wc = jnp.exp(a - m), jnp.exp(b - m), jnp.exp(c - m)
        den = wa + wb + wc
        y_ref[...] = ((wa * os_nat[0] + wb * os_nat[1] + wc * os_nat[2]) / den).astype(BF)
        lse = m + jnp.log(den)
        for j_ref, d in zip((j0, j1, j2), DILS):
            _dilate_store(lse, scr, j_ref, (), d)

    dil = lambda d: pl.BlockSpec((d, tm // d, GW), lambda i: (0, i, 0))
    dshape = lambda d: jax.ShapeDtypeStruct((d, S // d, GW), F32)
    return pl.pallas_call(
        body, name="attn_combine", grid=(S // tm,),
        out_shape=(jax.ShapeDtypeStruct((S, GW), BF),) + tuple(dshape(d) for d in DILS),
        in_specs=[dil(d) for d in DILS] * 2,
        out_specs=(pl.BlockSpec((tm, GW), lambda i: (i, 0)),) + tuple(dil(d) for d in DILS),
        scratch_shapes=[pltpu.VMEM((2, tm, 128), F32)],
        compiler_params=_cp(("parallel",)))(*os_, *lses)


def _tril(upper=False):
    row = lax.broadcasted_iota(jnp.int32, (CHUNK, CHUNK), 0)
    col = lax.broadcasted_iota(jnp.int32, (CHUNK, CHUNK), 1)
    return row <= col if upper else col <= row


def _ln_fwd(z, gain, bias):
    mu = jnp.mean(z, axis=-1, keepdims=True)
    zc = z - mu
    rstd = lax.rsqrt(jnp.mean(zc * zc, axis=-1, keepdims=True) + EPS)
    zhat = zc * rstd
    return zhat, rstd, zhat * gain + bias


def _gmlp_fwd(puz, ln_g, ln_b, w_sp, b_sp_t):
    S = puz.shape[0]
    tm = min(512, S)
    nch = tm // CHUNK

    def body(p_ref, g_ref, b_ref, w_ref, bt_ref, o_ref):
        tril = _tril()
        ws = [jnp.where(tril, w_ref[gg], 0.0).astype(BF) for gg in range(4)]
        for ch in range(nch):
            rows = slice(ch * CHUNK, (ch + 1) * CHUNK)
            z = _gelu(p_ref[rows, GMLP_W:].astype(F32))
            _, _, zn = _ln_fwd(z, g_ref[...], b_ref[...])
            zn = zn.astype(BF)
            for gg in range(4):
                cols = slice(gg * CHUNK, (gg + 1) * CHUNK)
                sz = _dot(ws[gg], zn[:, cols]) + bt_ref[:, gg:gg + 1]
                u = _gelu(p_ref[rows, cols].astype(F32))
                o_ref[rows, cols] = (u * sz).astype(BF)

    return pl.pallas_call(
        body, name="gmlp_fwd", out_shape=jax.ShapeDtypeStruct((S, GMLP_W), BF), grid=(S // tm,),
        in_specs=[pl.BlockSpec((tm, 2 * GMLP_W), lambda i: (i, 0)),
                  pl.BlockSpec((1, GMLP_W), lambda i: (0, 0)), pl.BlockSpec((1, GMLP_W), lambda i: (0, 0)),
                  pl.BlockSpec((4, CHUNK, CHUNK), lambda i: (0, 0, 0)), pl.BlockSpec((CHUNK, 4), lambda i: (0, 0))],
        out_specs=pl.BlockSpec((tm, GMLP_W), lambda i: (i, 0)),
        compiler_params=_cp(("parallel",)))(puz, ln_g, ln_b, w_sp, b_sp_t)


def _merge_fwd(ya, yg, gates, wba, wbg, wout, x, g2, g3):
    S = x.shape[0]
    tm = min(512, S)

    def body(ya_ref, yg_ref, gt_ref, wba_ref, wbg_ref, wo_ref, x_ref, g2_ref, g3_ref,
             a_ref, gb_ref, mg_ref, y_ref, x2_ref, h3_ref):
        a = _dot(ya_ref[...], wba_ref[...])
        b = _dot(yg_ref[...], wbg_ref[...])
        a_ref[...] = a.astype(BF)
        gb_ref[...] = b.astype(BF)
        merged = (gt_ref[:, :D].astype(F32) * a + gt_ref[:, D:].astype(F32) * b).astype(BF)
        mg_ref[...] = merged
        y = _dot(merged, wo_ref[...])
        y_ref[...] = y
        x2 = x_ref[...] + y * _rms_stats(y) * g2_ref[...]
        x2_ref[...] = x2
        h3_ref[...] = (x2 * _rms_stats(x2) * g3_ref[...]).astype(BF)

    row = lambda w: pl.BlockSpec((tm, w), lambda i: (i, 0))
    full = lambda s: pl.BlockSpec(s, lambda i: (0, 0))
    return pl.pallas_call(
        body, name="merge_fwd", grid=(S // tm,),
        out_shape=(jax.ShapeDtypeStruct((S, D), BF), jax.ShapeDtypeStruct((S, D), BF), jax.ShapeDtypeStruct((S, D), BF),
                   jax.ShapeDtypeStruct((S, D), F32), jax.ShapeDtypeStruct((S, D), F32), jax.ShapeDtypeStruct((S, D), BF)),
        in_specs=[row(GW), row(GMLP_W), row(2 * D), full((GW, D)), full((GMLP_W, D)), full((D, D)), row(D),
                  full((1, D)), full((1, D))],
        out_specs=(row(D), row(D), row(D), row(D), row(D), row(D)),
        compiler_params=_cp(("parallel",)))(ya, yg, gates, wba, wbg, wout, x, g2, g3)


def _resident(shape):
    return pl.BlockSpec(shape, lambda i: (0,) * len(shape), pipeline_mode=pl.Buffered(1))


def _mlp_fwd(h3, wmi, wmo, x2, tgt, g4):
    S = x2.shape[0]
    tm = min(MLP_TM_FWD, S)

    def body(h_ref, wi_ref, wo_ref, x2_ref, t_ref, g4_ref, a_ref, dy2_ref, dout_ref, loss_ref, dg4_ref):
        @pl.when(pl.program_id(0) == 0)
        def _():
            loss_ref[...] = jnp.zeros_like(loss_ref)
            dg4_ref[...] = jnp.zeros_like(dg4_ref)

        a = jnp.maximum(_dot(h_ref[...], wi_ref[...]), 0.0)
        a_ref[...] = a.astype(BF)
        y2 = _dot((a * a).astype(BF), wo_ref[...])
        r = _rms_stats(y2)
        out = x2_ref[...] + y2 * r * g4_ref[...]
        err = out - t_ref[...]
        tot = jnp.sum(jnp.sum(err * err, axis=1, keepdims=True), axis=0, keepdims=True) * (0.5 / D)
        lane = lax.broadcasted_iota(jnp.int32, (1, 128), 1)
        loss_ref[...] += jnp.where(lane == 0, tot, 0.0)
        dout = err * (1.0 / D)
        dout_ref[...] = dout
        dy2, dg = _rms_bwd(y2, r, g4_ref[...], dout)
        dy2_ref[...] = dy2.astype(BF)
        dg4_ref[...] += dg

    row = pl.BlockSpec((tm, D), lambda i: (i, 0))
    return pl.pallas_call(
        body, name="mlp_fwd", grid=(S // tm,),
        out_shape=(jax.ShapeDtypeStruct((S, DFF), BF), jax.ShapeDtypeStruct((S, D), BF), jax.ShapeDtypeStruct((S, D), F32),
                   jax.ShapeDtypeStruct((1, 128), F32), jax.ShapeDtypeStruct((1, D), F32)),
        in_specs=[row, _resident((D, DFF)), _resident((DFF, D)), row, row, pl.BlockSpec((1, D), lambda i: (0, 0))],
        out_specs=(pl.BlockSpec((tm, DFF), lambda i: (i, 0)), row, row,
                   pl.BlockSpec((1, 128), lambda i: (0, 0)), pl.BlockSpec((1, D), lambda i: (0, 0))),
        compiler_params=_cp(("arbitrary",)))(h3, wmi, wmo, x2, tgt, g4)


def _mlp_bwd(dy2, a, wmo, wmi, x2, y, dout, g2, g3):
    S = x2.shape[0]
    tm = min(MLP_TM_BWD, S)

    def body(dy2_ref, a_ref, wo_ref, wi_ref, x2_ref, y_ref, dout_ref, g2_ref, g3_ref,
             dpre_ref, dx2_ref, dy_ref, dg3_ref, dg2_ref):
        @pl.when(pl.program_id(0) == 0)
        def _():
            dg3_ref[...] = jnp.zeros_like(dg3_ref)
            dg2_ref[...] = jnp.zeros_like(dg2_ref)

        da2 = _dot_nt(dy2_ref[...], wo_ref[...])
        dpre = (2.0 * a_ref[...].astype(F32) * da2).astype(BF)
        dpre_ref[...] = dpre
        dh3 = _dot_nt(dpre, wi_ref[...])
        x2 = x2_ref[...]
        dx3, dg3 = _rms_bwd(x2, _rms_stats(x2), g3_ref[...], dh3)
        dx2 = dout_ref[...] + dx3
        dx2_ref[...] = dx2
        dg3_ref[...] += dg3
        yv = y_ref[...]
        dy, dg2 = _rms_bwd(yv, _rms_stats(yv), g2_ref[...], dx2)
        dy_ref[...] = dy.astype(BF)
        dg2_ref[...] += dg2

    row = pl.BlockSpec((tm, D), lambda i: (i, 0))
    wide = pl.BlockSpec((tm, DFF), lambda i: (i, 0))
    vec = pl.BlockSpec((1, D), lambda i: (0, 0))
    return pl.pallas_call(
        body, name="mlp_bwd", grid=(S // tm,),
        out_shape=(jax.ShapeDtypeStruct((S, DFF), BF), jax.ShapeDtypeStruct((S, D), F32), jax.ShapeDtypeStruct((S, D), BF),
                   jax.ShapeDtypeStruct((1, D), F32), jax.ShapeDtypeStruct((1, D), F32)),
        in_specs=[row, wide, _resident((DFF, D)), _resident((D, DFF)), row, row, row, vec, vec],
        out_specs=(wide, row, row, vec, vec),
        compiler_params=_cp(("arbitrary",)))(dy2, a, wmo, wmi, x2, y, dout, g2, g3)


def _mm_tn(a, b, name, square_a=False, tm=1024, tn=1024, tk=2048):
    S, M = a.shape
    N = b.shape[1]
    tm, tk = min(tm, M), min(tk, S)
    tn = max(t for t in range(128, min(tn, N) + 1, 128) if N % t == 0)
    assert M % tm == 0 and S % tk == 0
    nk = S // tk

    def body(a_ref, b_ref, o_ref):
        k = pl.program_id(2)
        av = a_ref[...]
        if square_a:
            av = av * av
        part = _dot_tn(av, b_ref[...])

        @pl.when(k == 0)
        def _():
            o_ref[...] = part

        @pl.when(k > 0)
        def _():
            o_ref[...] += part

    return pl.pallas_call(
        body, name=name, out_shape=jax.ShapeDtypeStruct((M, N), F32), grid=(M // tm, N // tn, nk),
        in_specs=[pl.BlockSpec((tk, tm), lambda i, j, k: (k, i)), pl.BlockSpec((tk, tn), lambda i, j, k: (k, j))],
        out_specs=pl.BlockSpec((tm, tn), lambda i, j, k: (i, j)),
        compiler_params=_cp(("parallel", "parallel", "arbitrary")))(a, b)


def _outproj_bwd(dy, wout, abr, gbr, gates, wba, wbg, ya, tie):
    S = dy.shape[0]
    tm = min(512, S)

    def body(dy_ref, wo_ref, a_ref, b_ref, gt_ref, wba_ref, wbg_ref, ya_ref, tie_ref,
             dgt_ref, da_ref, db_ref, dyg_ref, e0, e1, e2, s0, s1, s2, scr):
        dm = _dot_nt(dy_ref[...], wo_ref[...])
        ga, gb = gt_ref[:, :D].astype(F32), gt_ref[:, D:].astype(F32)
        dgt_ref[:, :D] = (dm * a_ref[...].astype(F32) * ga * (1.0 - ga)).astype(BF)
        dgt_ref[:, D:] = (dm * b_ref[...].astype(F32) * gb * (1.0 - gb)).astype(BF)
        da = (dm * ga).astype(BF)
        db = (dm * gb).astype(BF)
        da_ref[...] = da
        db_ref[...] = db
        dyg_ref[...] = _dot_nt(db, wbg_ref[...]).astype(BF)
        dya = _dot_nt(da, wba_ref[...]).astype(BF).astype(F32)
        dyy = dya * ya_ref[...].astype(F32)
        dsum = jnp.zeros((tm, GW), F32)
        for hh in range(NSLOT):
            hm = _head_mask(hh)
            dsum = dsum + jnp.where(hm, jnp.sum(jnp.where(hm, dyy, 0.0), axis=1, keepdims=True), 0.0)
        for e_ref, s_ref, d in zip((e0, e1, e2), (s0, s1, s2), DILS):
            _dilate_store(dya, scr, e_ref, (), d)
            _dilate_store(dsum, scr, s_ref, (), d)

    row = lambda w: pl.BlockSpec((tm, w), lambda i: (i, 0))
    full = lambda s: pl.BlockSpec(s, lambda i: (0, 0))
    dil = lambda d: pl.BlockSpec((d, tm // d, GW), lambda i: (0, i, 0))
    dshape = lambda d, t: jax.ShapeDtypeStruct((d, S // d, GW), t)
    return pl.pallas_call(
        body, name="outproj_bwd", grid=(S // tm,),
        out_shape=(jax.ShapeDtypeStruct((S, 2 * D), BF), jax.ShapeDtypeStruct((S, D), BF), jax.ShapeDtypeStruct((S, D), BF),
                   jax.ShapeDtypeStruct((S, GMLP_W), BF)) + tuple(dshape(d, BF) for d in DILS)
        + tuple(dshape(d, F32) for d in DILS),
        in_specs=[row(D), full((D, D)), row(D), row(D), row(2 * D), full((GW, D)), full((GMLP_W, D)), row(GW),
                  pl.BlockSpec(memory_space=pl.ANY)],
        out_specs=(row(2 * D), row(D), row(D), row(GMLP_W)) + tuple(dil(d) for d in DILS) * 2,
        scratch_shapes=[pltpu.VMEM((2, tm, 128), F32)],
        compiler_params=_cp(("parallel",)))(dy, wout, abr, gbr, gates, wba, wbg, ya, tie)


def _gmlp_bwd(puz, dyg, ln_g, ln_b, w_sp, w_sp_t, b_sp_t):
    S = puz.shape[0]
    tm = min(512, S)
    nch = tm // CHUNK

    def body(p_ref, dy_ref, g_ref, b_ref, w_ref, wt_ref, bt_ref,
             dp_ref, dw_ref, dbs_ref, dg_ref, dbias_ref, dbacc_ref):
        i = pl.program_id(0)

        @pl.when(i == 0)
        def _():
            dw_ref[...] = jnp.zeros_like(dw_ref)
            dbacc_ref[...] = jnp.zeros_like(dbacc_ref)
            dg_ref[...] = jnp.zeros_like(dg_ref)
            dbias_ref[...] = jnp.zeros_like(dbias_ref)

        tril = _tril()
        ws = [jnp.where(tril, w_ref[gg], 0.0).astype(BF) for gg in range(4)]
        triu = _tril(upper=True)
        wts = [jnp.where(triu, wt_ref[gg], 0.0).astype(BF) for gg in range(4)]
        gain = g_ref[...]
        for ch in range(nch):
            rows = slice(ch * CHUNK, (ch + 1) * CHUNK)
            pz = p_ref[rows, GMLP_W:].astype(F32)
            z = _gelu(pz)
            zhat, rstd, zn = _ln_fwd(z, gain, b_ref[...])
            znb = zn.astype(BF)
            dzn_parts = []
            for gg in range(4):
                cols = slice(gg * CHUNK, (gg + 1) * CHUNK)
                pu = p_ref[rows, cols].astype(F32)
                u = _gelu(pu)
                sz = _dot(ws[gg], znb[:, cols]) + bt_ref[:, gg:gg + 1]
                dyv = dy_ref[rows, cols].astype(F32)
                dp_ref[rows, cols] = (dyv * sz * _gelu_grad(pu)).astype(BF)
                dsz = dyv * u
                dbacc_ref[gg] += dsz
                dszb = dsz.astype(BF)
                dw_ref[gg] += _dot_nt(dszb, znb[:, cols])
                dzn_parts.append(_dot(wts[gg], dszb))
            dzn = jnp.concatenate(dzn_parts, axis=1)
            dg_ref[...] += jnp.sum(dzn * zhat, axis=0, keepdims=True)
            dbias_ref[...] += jnp.sum(dzn, axis=0, keepdims=True)
            dzh = dzn * gain
            dz = rstd * (dzh - jnp.mean(dzh, axis=-1, keepdims=True)
                         - zhat * jnp.mean(dzh * zhat, axis=-1, keepdims=True))
            dp_ref[rows, GMLP_W:] = (dz * _gelu_grad(pz)).astype(BF)

        @pl.when(i == pl.num_programs(0) - 1)
        def _():
            for gg in range(4):
                dw_ref[gg] = jnp.where(tril, dw_ref[gg], 0.0)
                dbs_ref[gg] = jnp.sum(dbacc_ref[gg], axis=1, keepdims=True)

    full2 = lambda s: pl.BlockSpec(s, lambda i: (0, 0))
    full3 = lambda s: pl.BlockSpec(s, lambda i: (0, 0, 0))
    return pl.pallas_call(
        body, name="gmlp_bwd", grid=(S // tm,),
        out_shape=(jax.ShapeDtypeStruct((S, 2 * GMLP_W), BF), jax.ShapeDtypeStruct((4, CHUNK, CHUNK), F32),
                   jax.ShapeDtypeStruct((4, CHUNK, 1), F32), jax.ShapeDtypeStruct((1, GMLP_W), F32),
                   jax.ShapeDtypeStruct((1, GMLP_W), F32)),
        in_specs=[pl.BlockSpec((tm, 2 * GMLP_W), lambda i: (i, 0)), pl.BlockSpec((tm, GMLP_W), lambda i: (i, 0)),
                  full2((1, GMLP_W)), full2((1, GMLP_W)), full3((4, CHUNK, CHUNK)), full3((4, CHUNK, CHUNK)),
                  full2((CHUNK, 4))],
        out_specs=(pl.BlockSpec((tm, 2 * GMLP_W), lambda i: (i, 0)), full3((4, CHUNK, CHUNK)), full3((4, CHUNK, 1)),
                   full2((1, GMLP_W)), full2((1, GMLP_W))),
        scratch_shapes=[pltpu.VMEM((4, CHUNK, CHUNK), F32)],
        compiler_params=_cp(("arbitrary",)))(puz, dyg, ln_g, ln_b, w_sp, w_sp_t, b_sp_t)


def _attn_bwd(qkv, dya, dsums, lse, name):
    _, d, L, _ = qkv.shape
    nsub = min(ATTN_NSUB, L // QB)
    R = nsub * QB
    nsteps = L // R

    def body(q_ref, qn_ref, kp_ref, kc_ref, vp_ref, vc_ref, dy_ref, dyn_ref, e_ref, en_ref, l_ref, ln_ref, o_ref):
        i = pl.program_id(1)
        band, band_first = _band_masks(i == 0)
        row = lax.broadcasted_iota(jnp.int32, (QB, QB), 0)
        col = lax.broadcasted_iota(jnp.int32, (QB, QB), 1)
        mask_next = jnp.logical_and(col >= row, i < nsteps - 1)
        kc, vc = kc_ref[...], vc_ref[...]
        kfull = jnp.concatenate([kp_ref[...], kc], axis=0)
        vfull = jnp.concatenate([vp_ref[...], vc], axis=0)
        k_last, v_last = kc[(nsub - 1) * QB:], vc[(nsub - 1) * QB:]
        q_ext = jnp.concatenate([q_ref[...], qn_ref[...]], axis=0)
        dy_ext = jnp.concatenate([dy_ref[...], dyn_ref[...]], axis=0)
        esum, esum_n, lse, lse_n = e_ref[...], en_ref[...], l_ref[...], ln_ref[...]
        win = lambda t, sb: t[sb * QB:(sb + 2) * QB]
        blk = lambda t, sb: t[sb * QB:(sb + 1) * QB]
        hms = [_head_mask(hh) for hh in range(NSLOT)]
        q_hs = [jnp.where(hm, q_ext, 0) for hm in hms]
        dy_hs = [jnp.where(hm, dy_ext, 0) for hm in hms]
        raw = []
        for hh in range(NSLOT):
            tiles = [(_dot_nt(blk(q_hs[hh], sb), win(kfull, sb)), _dot_nt(blk(dy_hs[hh], sb), win(vfull, sb)))
                     for sb in range(nsub)]
            tiles.append((_dot_nt(blk(q_hs[hh], nsub), k_last), _dot_nt(blk(dy_hs[hh], nsub), v_last)))
            raw.append(tiles)
        ps, dss = [], []
        for hh in range(NSLOT):
            rowstat = lambda t: jnp.max(jnp.where(hms[hh], t, -jnp.inf), axis=1, keepdims=True)
            p_h, ds_h = [], []
            for sb in range(nsub + 1):
                sc, dp = raw[hh][sb]
                if sb < nsub:
                    msk, lrow, erow = (band_first if sb == 0 else band), rowstat(blk(lse, sb)), rowstat(blk(esum, sb))
                else:
                    msk, lrow, erow = mask_next, rowstat(lse_n), rowstat(esum_n)
                p = jnp.where(msk, jnp.exp(sc * SCALE - lrow), 0.0)
                p_h.append(p.astype(BF))
                ds_h.append((p * (dp - erow)).astype(BF))
            ps.append(p_h)
            dss.append(ds_h)
        dq = [jnp.zeros((QB, GW), F32) for _ in range(nsub)]
        dk = [jnp.zeros((QB, GW), F32) for _ in range(nsub)]
        dv = [jnp.zeros((QB, GW), F32) for _ in range(nsub)]
        for hh in range(NSLOT):
            for sb in range(nsub):
                dq[sb] = dq[sb] + jnp.where(hms[hh], _dot(dss[hh][sb], win(kfull, sb)), 0.0)
                nxt = lambda t: t[sb + 1][:, :QB] if sb + 1 < nsub else t[nsub]
                dk[sb] = dk[sb] + _dot_tn(jnp.concatenate([dss[hh][sb][:, QB:], nxt(dss[hh])], axis=0), win(q_hs[hh], sb))
                dv[sb] = dv[sb] + _dot_tn(jnp.concatenate([ps[hh][sb][:, QB:], nxt(ps[hh])], axis=0), win(dy_hs[hh], sb))
        for sb in range(nsub):
            rows = slice(sb * QB, (sb + 1) * QB)
            o_ref[0, rows, :] = (dq[sb] * SCALE).astype(BF)
            o_ref[1, rows, :] = (dk[sb] * SCALE).astype(BF)
            o_ref[2, rows, :] = dv[sb].astype(BF)

    prev = lambda i: jnp.maximum(i * nsub - 1, 0)
    nxt = lambda i: jnp.minimum((i + 1) * nsub, L // QB - 1)
    cur4 = lambda t: pl.BlockSpec((None, None, R, GW), lambda r, i: (t, r, i, 0))
    prv4 = lambda t: pl.BlockSpec((None, None, QB, GW), lambda r, i: (t, r, prev(i), 0))
    nxt4 = lambda t: pl.BlockSpec((None, None, QB, GW), lambda r, i: (t, r, nxt(i), 0))
    cur3 = pl.BlockSpec((None, R, GW), lambda r, i: (r, i, 0))
    nxt3 = pl.BlockSpec((None, QB, GW), lambda r, i: (r, nxt(i), 0))
    return pl.pallas_call(
        body, name=name, grid=(d, nsteps), out_shape=jax.ShapeDtypeStruct((3, d, L, GW), BF),
        in_specs=[cur4(0), nxt4(0), prv4(1), cur4(1), prv4(2), cur4(2), cur3, nxt3, cur3, nxt3, cur3, nxt3],
        out_specs=pl.BlockSpec((3, None, R, GW), lambda r, i: (0, r, i, 0)),
        compiler_params=_cp(("parallel", "arbitrary")))(qkv, qkv, qkv, qkv, qkv, qkv, dya, dya, dsums, dsums, lse, lse)


def _inproj_bwd(dqkvs, dpuz, dgates, wqkv, wuz, wg, cos_t, sin_t, x, dx2, g1):
    S = x.shape[0]
    tm = min(512, S)

    def body(d0_ref, d1_ref, d2_ref, dp_ref, dg_ref, wqkv_ref, wuz_ref, wg_ref, c_ref, s_ref,
             x_ref, dx2_ref, g1_ref, gx_ref, dg1_ref, dn_ref, scr):
        i = pl.program_id(0)

        @pl.when(i == 0)
        def _():
            dg1_ref[...] = jnp.zeros_like(dg1_ref)

        for t in range(3):
            for g, (d_ref, d) in enumerate(zip((d0_ref, d1_ref, d2_ref), DILS)):
                piece = _undilate_load(d_ref, (t,), d, scr, tm)
                if t < 2:
                    piece = _rope_bwd(piece, c_ref, s_ref)
                dn_ref[:, (3 * t + g) * GW:(3 * t + g + 1) * GW] = piece.astype(BF)
        dh = (_dot_nt(dn_ref[...], wqkv_ref[...]) + _dot_nt(dp_ref[...], wuz_ref[...])
              + _dot_nt(dg_ref[...], wg_ref[...]))
        xv = x_ref[...]
        dx1, dg1 = _rms_bwd(xv, _rms_stats(xv), g1_ref[...], dh)
        gx_ref[...] = dx2_ref[...] + dx1
        dg1_ref[...] += dg1

    row = lambda w: pl.BlockSpec((tm, w), lambda i: (i, 0))
    full = lambda s: pl.BlockSpec(s, lambda i: (0, 0))
    dil = lambda d: pl.BlockSpec((3, d, tm // d, GW), lambda i: (0, 0, i, 0))
    return pl.pallas_call(
        body, name="inproj_bwd", grid=(S // tm,),
        out_shape=(jax.ShapeDtypeStruct((S, D), F32), jax.ShapeDtypeStruct((1, D), F32),
                   jax.ShapeDtypeStruct((S, 3 * AW), BF)),
        in_specs=[dil(d) for d in DILS] + [row(2 * GMLP_W), row(2 * D), full((D, 3 * AW)), full((D, 2 * GMLP_W)),
                                            full((D, 2 * D)), row(128), row(128), row(D), row(D), full((1, D))],
        out_specs=(row(D), full((1, D)), row(3 * AW)),
        scratch_shapes=[pltpu.VMEM((2, tm, 128), F32)],
        compiler_params=_cp(("arbitrary",)))(*dqkvs, dpuz, dgates, wqkv, wuz, wg, cos_t, sin_t, x, dx2, g1)


def _adam_math(w, g, m, v):
    m2 = ADAM_B1 * m + (1.0 - ADAM_B1) * g
    v2 = ADAM_B2 * v + (1.0 - ADAM_B2) * (g * g)
    m_hat = m2 / (1.0 - ADAM_B1 ** ADAM_STEP)
    v_hat = v2 / (1.0 - ADAM_B2 ** ADAM_STEP)
    delta = -ADAM_LR * (m_hat / (jnp.sqrt(v_hat) + ADAM_EPS) + ADAM_WD * w)
    return delta, m2, v2


def _adam_shard(own, recv, w, m, v, name):
    R, C = w.shape
    tr = min(256, R)

    def body(own_ref, r_ref, w_ref, m_ref, v_ref, g_ref, d_ref, m2_ref, v2_ref):
        g = own_ref[...] + r_ref[0].astype(F32) + r_ref[1].astype(F32) + r_ref[2].astype(F32)
        g_ref[...] = g
        d_ref[...], m2_ref[...], v2_ref[...] = _adam_math(w_ref[...], g, m_ref[...], v_ref[...])

    spec = pl.BlockSpec((tr, C), lambda i: (i, 0))
    out = jax.ShapeDtypeStruct((R, C), F32)
    return pl.pallas_call(
        body, name=name, grid=(R // tr,), out_shape=(out, out, out, out),
        in_specs=[spec, pl.BlockSpec((3, tr, C), lambda i: (0, i, 0)), spec, spec, spec],
        out_specs=(spec, spec, spec, spec), compiler_params=_cp(("parallel",)))(own, recv, w, m, v)


def _adam_small(gp, wp, mp, vp):
    def body(g_ref, w_ref, m_ref, v_ref, d_ref, m2_ref, v2_ref):
        d_ref[...], m2_ref[...], v2_ref[...] = _adam_math(w_ref[...], g_ref[...], m_ref[...], v_ref[...])

    out = jax.ShapeDtypeStruct(gp.shape, F32)
    return pl.pallas_call(body, name="adam_small", out_shape=(out, out, out),
                          compiler_params=pltpu.CompilerParams(vmem_limit_bytes=VMEM_LIMIT))(gp, wp, mp, vp)


def _rs_add(gblocks, recv, idx, name):
    _, R, C = gblocks.shape
    tr = min(256, R)

    def body(t_ref, g_ref, r_ref, own_ref, send_ref):
        j = pl.program_id(1)
        s = g_ref[...] + r_ref[...]

        @pl.when(j == 0)
        def _():
            own_ref[...] = s

        @pl.when(j > 0)
        def _():
            send_ref[...] = s.astype(BF)

    grid_spec = pltpu.PrefetchScalarGridSpec(
        num_scalar_prefetch=1, grid=(R // tr, 4),
        in_specs=[pl.BlockSpec((None, tr, C), lambda i, j, t: (t[j], i, 0)),
                  pl.BlockSpec((None, tr, C), lambda i, j, t: (t[4 + j], i, 0))],
        out_specs=[pl.BlockSpec((tr, C), lambda i, j, t: (i, 0)),
                   pl.BlockSpec((None, tr, C), lambda i, j, t: (jnp.maximum(j - 1, 0), i, 0))])
    return pl.pallas_call(
        body, name=name, grid_spec=grid_spec,
        out_shape=(jax.ShapeDtypeStruct((R, C), F32), jax.ShapeDtypeStruct((3, R, C), BF)),
        compiler_params=_cp(("parallel", "arbitrary")))(idx, gblocks, recv)


def _mesh_pos():
    return lax.axis_index("x"), lax.axis_index("y"), lax.axis_index("c")


def _all_gather_weights(shards):
    n = len(shards)

    def body(*refs):
        ins, outs = refs[:n], refs[n:2 * n]
        send_sems, recv_sems, loc_sems = refs[2 * n:]
        x, y, c = _mesh_pos()
        me, sib = (x, y, c), (x, y, 1 - c)
        chips = [(1 - x, y), (x, 1 - y), (1 - x, 1 - y)]

        def rcopy(k, s, block, to, src=None):
            dst = outs[k].at[4 * block[0] + 2 * block[1] + block[2]]
            return pltpu.make_async_remote_copy(
                src_ref=dst if src is None else src, dst_ref=dst, send_sem=send_sems.at[k, s],
                recv_sem=recv_sems.at[k, s], device_id=to, device_id_type=MESH)

        locs = [pltpu.make_async_copy(ins[k], outs[k].at[4 * x + 2 * y + c], loc_sems.at[k]) for k in range(n)]
        for cp in locs:
            cp.start()
        started = []
        for k in range(n):
            for j, chip in enumerate(chips):
                started.append(rcopy(k, 1 + j, me, (*chip, c), src=ins[k]))
                started[-1].start()
        for k in range(n):
            started.append(rcopy(k, 0, me, sib, src=ins[k]))
            started[-1].start()
        for k in range(n):
            for j, chip in enumerate(chips):
                rcopy(k, 1 + j, (*chip, c), me).wait_recv()
                started.append(rcopy(k, 4 + j, (*chip, c), sib))
                started[-1].start()
        for k in range(n):
            rcopy(k, 0, sib, me).wait_recv()
            for j, chip in enumerate(chips):
                rcopy(k, 4 + j, (*chip, 1 - c), me).wait_recv()
        for cp in started:
            cp.wait_send()
        for cp in locs:
            cp.wait()

    hbm = pl.BlockSpec(memory_space=pl.ANY)
    return pl.pallas_call(
        body, name="ag_weights",
        out_shape=tuple(jax.ShapeDtypeStruct((8,) + s.shape, s.dtype) for s in shards),
        in_specs=[hbm] * n, out_specs=(hbm,) * n,
        scratch_shapes=[pltpu.SemaphoreType.DMA((n, 7)), pltpu.SemaphoreType.DMA((n, 7)),
                        pltpu.SemaphoreType.DMA((n,))])(*shards)


def _rs_d2d(gblocks):
    n = len(gblocks)

    def body(*refs):
        ins, outs = refs[:n], refs[n:2 * n]
        send_sems, recv_sems = refs[2 * n:]
        x, y, c = _mesh_pos()
        copies = []
        for k in range(n):
            for kk in range(4):
                copies.append(pltpu.make_async_remote_copy(
                    src_ref=ins[k].at[2 * kk + 1 - c], dst_ref=outs[k].at[kk], send_sem=send_sems.at[k, kk],
                    recv_sem=recv_sems.at[k, kk], device_id=(x, y, 1 - c), device_id_type=MESH))
                copies[-1].start()
        for cp in copies:
            cp.wait()

    hbm = pl.BlockSpec(memory_space=pl.ANY)
    return pl.pallas_call(
        body, name="rs_d2d",
        out_shape=tuple(jax.ShapeDtypeStruct((4,) + g.shape[1:], g.dtype) for g in gblocks),
        in_specs=[hbm] * n, out_specs=(hbm,) * n,
        scratch_shapes=[pltpu.SemaphoreType.DMA((n, 4)), pltpu.SemaphoreType.DMA((n, 4))])(*gblocks)


def _rs_ici(sends):
    n = len(sends)

    def body(*refs):
        ins, outs = refs[:n], refs[n:2 * n]
        send_sems, recv_sems = refs[2 * n:]
        x, y, c = _mesh_pos()
        chips = [(1 - x, y), (x, 1 - y), (1 - x, 1 - y)]
        copies = []
        for k in range(n):
            for j, chip in enumerate(chips):
                copies.append(pltpu.make_async_remote_copy(
                    src_ref=ins[k].at[j], dst_ref=outs[k].at[j], send_sem=send_sems.at[k, j],
                    recv_sem=recv_sems.at[k, j], device_id=(*chip, c), device_id_type=MESH))
                copies[-1].start()
        for cp in copies:
            cp.wait()

    hbm = pl.BlockSpec(memory_space=pl.ANY)
    return pl.pallas_call(
        body, name="rs_ici",
        out_shape=tuple(jax.ShapeDtypeStruct(s.shape, s.dtype) for s in sends),
        in_specs=[hbm] * n, out_specs=(hbm,) * n,
        scratch_shapes=[pltpu.SemaphoreType.DMA((n, 3)), pltpu.SemaphoreType.DMA((n, 3))])(*sends)


_HBM = pl.BlockSpec(memory_space=pltpu.HBM)
_SEM = pl.BlockSpec(memory_space=pltpu.SEMAPHORE)
_EFFECT = pltpu.SideEffectType.DATAFLOW_SIDE_EFFECTING
_RELATIONS = [(dx, dy, dc) for dx in (0, 1) for dy in (0, 1) for dc in (0, 1)][1:]


def _flip(v, d):
    return 1 - v if d else v


def _plan_gather(n):
    def plan(x, y, c):
        return [(k, None, 4 * x + 2 * y + c, (_flip(x, dx), _flip(y, dy), _flip(c, dc)))
                for k in range(n) for dx, dy, dc in _RELATIONS]
    return plan


def _plan_d2d(n):
    def plan(x, y, c):
        return [(k, 2 * kk + 1 - c, kk, (x, y, 1 - c)) for k in range(n) for kk in range(4)]
    return plan


def _plan_ici(n):
    def plan(x, y, c):
        return [(k, j, j, (_flip(x, dx), _flip(y, dy), c))
                for k in range(n) for j, (dx, dy) in enumerate(((1, 0), (0, 1), (1, 1)))]
    return plan


def _plan_copies(plan, src_refs, land_refs, send_sems, recv_sems):
    x, y, c = _mesh_pos()
    return [pltpu.make_async_remote_copy(
        src_ref=src_refs[k] if si is None else src_refs[k].at[si], dst_ref=land_refs[k].at[di],
        send_sem=send_sems.at[n], recv_sem=recv_sems.at[n], device_id=dev, device_id_type=MESH)
        for n, (k, si, di, dev) in enumerate(plan(x, y, c))]


def _exchange_start(srcs, land_shapes, plan, ncopies, name, after):
    n = len(srcs)

    def body(*refs):
        src_refs, land_refs = refs[:n], refs[n:2 * n]
        send_sems, recv_sems = refs[2 * n + len(after)], refs[2 * n + len(after) + 1]
        token = refs[-1]
        for cp in _plan_copies(plan, src_refs, land_refs, send_sems, recv_sems):
            cp.start()
        token[...] = jnp.zeros_like(token)

    lands = [pltpu.with_memory_space_constraint(lax.empty(s, a.dtype), pltpu.HBM) for s, a in zip(land_shapes, srcs)]
    srcs = [pltpu.with_memory_space_constraint(a, pltpu.HBM) for a in srcs]
    outs = pl.pallas_call(
        body, name=name,
        out_shape=(pltpu.SemaphoreType.DMA((ncopies,)), pltpu.SemaphoreType.DMA((ncopies,)))
        + tuple(pltpu.HBM(a.shape, a.dtype) for a in srcs) + tuple(pltpu.HBM(a.shape, a.dtype) for a in lands)
        + (jax.ShapeDtypeStruct((8, 128), F32),),
        in_specs=[_HBM] * (2 * n) + [pl.BlockSpec(memory_space=pl.ANY)] * len(after),
        out_specs=(_SEM, _SEM) + (_HBM,) * (2 * n) + (pl.BlockSpec(memory_space=pltpu.VMEM),),
        input_output_aliases={i: 2 + i for i in range(2 * n)},
        compiler_params=pltpu.CompilerParams(has_side_effects=_EFFECT))(*srcs, *lands, *after)
    return (outs[0], outs[1], list(outs[2:2 + n]), list(outs[2 + n:2 + 2 * n])), outs[-1]


def _exchange_wait(handle, plan, name, after):
    send_sems, recv_sems, srcs, lands = handle
    n = len(srcs)

    def body(*refs):
        src_refs, land_refs = refs[:n], refs[n:2 * n]
        for cp in _plan_copies(plan, src_refs, land_refs, refs[2 * n], refs[2 * n + 1]):
            cp.wait_send()
            cp.wait_recv()

    outs = pl.pallas_call(
        body, name=name,
        out_shape=tuple(pltpu.HBM(a.shape, a.dtype) for a in srcs) + tuple(pltpu.HBM(a.shape, a.dtype) for a in lands),
        in_specs=[_HBM] * (2 * n) + [_SEM, _SEM] + [pl.BlockSpec(memory_space=pl.ANY)] * len(after),
        out_specs=(_HBM,) * (2 * n), input_output_aliases={i: i for i in range(2 * n)},
        compiler_params=pltpu.CompilerParams(has_side_effects=_EFFECT))(*srcs, *lands, send_sems, recv_sems, *after)
    return list(outs[:n]), list(outs[n:])


def _small_allreduce(pkt):
    R = pkt.shape[0]

    def body(p_ref, o_ref, buf, send_sems, recv_sems):
        x, y, c = _mesh_pos()
        me, sib = (x, y, c), (x, y, 1 - c)
        chips = [(1 - x, y), (x, 1 - y), (1 - x, 1 - y)]

        def rcopy(s, block, to, src=None):
            dst = buf.at[4 * block[0] + 2 * block[1] + block[2]]
            return pltpu.make_async_remote_copy(
                src_ref=dst if src is None else src, dst_ref=dst, send_sem=send_sems.at[s],
                recv_sem=recv_sems.at[s], device_id=to, device_id_type=MESH)

        buf[4 * x + 2 * y + c] = p_ref[...]
        started = [rcopy(0, me, sib, src=p_ref)]
        started += [rcopy(1 + j, me, (*chip, c), src=p_ref) for j, chip in enumerate(chips)]
        for cp in started:
            cp.start()
        for j, chip in enumerate(chips):
            rcopy(1 + j, (*chip, c), me).wait_recv()
            started.append(rcopy(4 + j, (*chip, c), sib))
            started[-1].start()
        rcopy(0, sib, me).wait_recv()
        for j, chip in enumerate(chips):
            rcopy(4 + j, (*chip, 1 - c), me).wait_recv()
        for cp in started:
            cp.wait_send()
        acc = buf[0]
        for b in range(1, 8):
            acc = acc + buf[b]
        o_ref[...] = acc

    return pl.pallas_call(
        body, name="small_allreduce", out_shape=jax.ShapeDtypeStruct(pkt.shape, F32),
        in_specs=[pl.BlockSpec(memory_space=pltpu.VMEM)], out_specs=pl.BlockSpec(memory_space=pltpu.VMEM),
        scratch_shapes=[pltpu.VMEM((8, R, 128), F32), pltpu.SemaphoreType.DMA((7,)), pltpu.SemaphoreType.DMA((7,))],
        compiler_params=pltpu.CompilerParams(vmem_limit_bytes=VMEM_LIMIT))(pkt)


def _rope_tables(S):
    half = HD // 2
    inv_freq = ROPE_THETA ** (-jnp.arange(half, dtype=F32) / half)
    ang = jnp.arange(S, dtype=F32)[:, None] * inv_freq[None, :]
    cos, sin = jnp.cos(ang), jnp.sin(ang)
    return jnp.concatenate([cos, cos, cos, cos], axis=1), jnp.concatenate([-sin, sin, -sin, sin], axis=1)


def _to_blocks(g, col_sharded):
    if col_sharded:
        return g.reshape(g.shape[0], 8, g.shape[1] // 8).transpose(1, 0, 2)
    return g.reshape(8, g.shape[0] // 8, g.shape[1])


def _from_blocks(t, col_sharded):
    if col_sharded:
        return t.transpose(1, 0, 2).reshape(t.shape[1], 8 * t.shape[2])
    return t.reshape(8 * t.shape[1], t.shape[2])


class _NoComm:
    def __init__(self, late_weights):
        self._late = late_weights
        self.mlp_grads = None

    def start_token(self):
        return 0.0

    def late_weights(self, after):
        return self._late

    def mlp_grads_ready(self, dwmi, dwmo):
        self.mlp_grads = (dwmi, dwmo)
        return jnp.zeros((8, 128), F32)

    def after_outproj_bwd(self, after):
        return 0.0


class _FsdpComm:
    def __init__(self, late_shards, col_sharded, after, idx, block):
        self._shards, self._col, self._idx, self._block = late_shards, col_sharded, idx, block
        n = len(late_shards)
        self._gather, self._token = _exchange_start(
            late_shards, [(8,) + s.shape for s in late_shards], _plan_gather(n), 7 * n, "ag_late_start", (after,))

    def start_token(self):
        return self._token[0, 0]

    def late_weights(self, after):
        shards, lands = _exchange_wait(self._gather, _plan_gather(len(self._col)), "ag_late_wait", after)
        lands = [lax.dynamic_update_index_in_dim(t, s, self._block, 0) for t, s in zip(lands, shards)]
        return [_from_blocks(t, cs) for t, cs in zip(lands, self._col)]

    def mlp_grads_ready(self, dwmi, dwmo):
        gblocks = [_to_blocks(dwmi, True), _to_blocks(dwmo, False)]
        self._d2d, token = _exchange_start(gblocks, [(4,) + g.shape[1:] for g in gblocks], _plan_d2d(2), 8,
                                           "rs_mlp_d2d_start", (self._token,))
        return token

    def after_outproj_bwd(self, after):
        gblocks, from_sib = _exchange_wait(self._d2d, _plan_d2d(2), "rs_mlp_d2d_wait", after)
        halves = [_rs_add(g, r, self._idx, "rs_add_mlp_%d" % n) for n, (g, r) in enumerate(zip(gblocks, from_sib))]
        self._own = [own for own, _ in halves]
        sends = [s for _, s in halves]
        self._ici, token = _exchange_start(sends, [s.shape for s in sends], _plan_ici(2), 6, "rs_mlp_ici_start",
                                           (self._token,))
        return token[0, 0]

    def finish(self, after):
        return self._own, _exchange_wait(self._ici, _plan_ici(2), "rs_mlp_ici_wait", after)[1]


def _local_step(x, tgt, wqkv, wuz, wg, comm, g1, g2, g3, g4, ln_g, ln_b, w_sp, b_sp):
    S = x.shape[0]
    cos_t, sin_t = _rope_tables(S)
    b_sp_t = b_sp.T
    w_sp_t = w_sp.transpose(0, 2, 1)

    h1 = _rms_fwd(x, g1 + comm.start_token(), "rms_pre_mix")
    qkvs = [_proj_qkv(h1, wqkv, cos_t, sin_t, g, d, "proj_qkv_d%d" % d) for g, d in enumerate(DILS)]
    puz = _proj(h1, wuz, "none", "proj_uz")
    gates = _proj(h1, wg, "sigmoid", "proj_gates")
    fwd = [_attn_fwd(t, "attn_fwd_d%d" % d) for t, d in zip(qkvs, DILS)]
    ya, *lses = _attn_combine([o for o, _ in fwd], [l for _, l in fwd])
    yg = _gmlp_fwd(puz, ln_g, ln_b, w_sp, b_sp_t)
    wba, wbg, wout, wmi, wmo = comm.late_weights(after=(ya, yg, gates))
    abr, gbr, merged, y, x2, h3 = _merge_fwd(ya, yg, gates, wba, wbg, wout, x, g2, g3)
    a, dy2, dout, loss_row, dg4 = _mlp_fwd(h3, wmi, wmo, x2, tgt, g4)

    dpre, dx2, dy, dg3, dg2 = _mlp_bwd(dy2, a, wmo, wmi, x2, y, dout, g2, g3)
    dwmo = _mm_tn(a, dy2, "dw_mlp_out", square_a=True)
    dwmi = _mm_tn(h3, dpre, "dw_mlp_in")
    tie = comm.mlp_grads_ready(dwmi, dwmo)
    dgates, da, db, dyg, *rest = _outproj_bwd(dy, wout, abr, gbr, gates, wba, wbg, ya, tie)
    dyas, dsums = rest[:3], rest[3:]
    dwout = _mm_tn(merged, dy, "dw_out")
    dwba = _mm_tn(ya, da, "dw_branch_attn")
    dwbg = _mm_tn(yg, db, "dw_branch_gmlp")
    dpuz, dwsp, dbs, dlng, dlnb = _gmlp_bwd(puz, dyg, ln_g + comm.after_outproj_bwd(after=(dyg,)), ln_b, w_sp, w_sp_t,
                                            b_sp_t)
    dqkvs = [_attn_bwd(qkvs[g], dyas[g], dsums[g], lses[g], "attn_bwd_d%d" % d) for g, d in enumerate(DILS)]
    grad_x, dg1, dqkv = _inproj_bwd(dqkvs, dpuz, dgates, wqkv, wuz, wg, cos_t, sin_t, x, dx2, g1)
    dwin = jnp.concatenate([_mm_tn(h1, t, "dw_in_%d" % n) for n, t in enumerate((dqkv, dpuz, dgates))], axis=1)
    small = dict(g1=dg1, g2=dg2, g3=dg3, g4=dg4, ln_g=dlng, ln_b=dlnb, b_sp=dbs.reshape(4, CHUNK), w_sp=dwsp)
    return loss_row, grad_x, (dwin, dwba, dwbg, dwout), small


def _pack_small(first_row, t):
    pad = lambda a, rows: jnp.concatenate([a, jnp.zeros((rows - a.shape[0], 128), F32)], axis=0)
    return jnp.concatenate([
        pad(first_row, 8), t["g1"].reshape(8, 128), t["g2"].reshape(8, 128), t["g3"].reshape(8, 128),
        t["g4"].reshape(8, 128), t["ln_g"].reshape(4, 128), t["ln_b"].reshape(4, 128),
        pad(t["b_sp"].reshape(4, 128), 8), t["w_sp"].reshape(4 * CHUNK, 128)], axis=0)


def _unpack_small(p):
    vec = lambda r, n: p[r:r + n // 128].reshape(1, n)
    return dict(g1=vec(PK_G1, D), g2=vec(PK_G2, D), g3=vec(PK_G3, D), g4=vec(PK_G4, D),
                ln_g=vec(PK_LNG, GMLP_W), ln_b=vec(PK_LNB, GMLP_W),
                b_sp=p[PK_BSP:PK_BSP + 4].reshape(1, 4, CHUNK), w_sp=p[PK_WSP:PKT_ROWS].reshape(1, 4, CHUNK, CHUNK))


def kernel(x, norm_pre_mix, w_in, w_spatial, b_spatial, ln_v_gain, ln_v_bias, w_branch_attn, w_branch_gmlp, w_out, norm_post_mix, norm_pre_mlp, w_mlp_in, w_mlp_out, norm_post_mlp, loss_target, m_norm_pre_mix, m_w_in, m_w_spatial, m_b_spatial, m_ln_v_gain, m_ln_v_bias, m_w_branch_attn, m_w_branch_gmlp, m_w_out, m_norm_post_mix, m_norm_pre_mlp, m_w_mlp_in, m_w_mlp_out, m_norm_post_mlp, v_norm_pre_mix, v_w_in, v_w_spatial, v_b_spatial, v_ln_v_gain, v_ln_v_bias, v_w_branch_attn, v_w_branch_gmlp, v_w_out, v_norm_post_mix, v_norm_pre_mlp, v_w_mlp_in, v_w_mlp_out, v_norm_post_mlp):
    mx, my, mc = _mesh_pos()
    rel = [(0, 0), (1, 0), (0, 1), (1, 1)]
    chip_of = [2 * (mx ^ dx) + (my ^ dy) for dx, dy in rel]
    idx = jnp.stack([2 * k + mc for k in chip_of] + chip_of).astype(jnp.int32)

    (g_win,) = _all_gather_weights([w_in[0].astype(BF)])
    win = _from_blocks(g_win, True)
    wqkv, wuz, wg = win[:, :3 * AW], win[:, 3 * AW:3 * AW + 2 * GMLP_W], win[:, 3 * AW + 2 * GMLP_W:]
    late = [w_branch_attn[0], w_branch_gmlp[0], w_out[0], w_mlp_in[0], w_mlp_out[0]]
    comm = _FsdpComm([w.astype(BF) for w in late], [True, True, False, True, False], g_win, idx, 4 * mx + 2 * my + mc)

    loss_row, grad_x, dws, small = _local_step(
        x[0], loss_target[0], wqkv, wuz, wg, comm,
        norm_pre_mix, norm_post_mix, norm_pre_mlp, norm_post_mlp, ln_v_gain, ln_v_bias, w_spatial[0], b_spatial[0])

    gblocks = [_to_blocks(g, cs) for g, cs in zip(dws, [True, True, True, False])]
    from_sib = _rs_d2d(gblocks)
    halves = [_rs_add(g, r, idx, "rs_add_%d" % n) for n, (g, r) in enumerate(zip(gblocks, from_sib))]
    from_chips = _rs_ici([s for _, s in halves])
    mlp_own, mlp_chips = comm.finish(after=(from_chips[0],))
    owns = [own for own, _ in halves] + mlp_own
    names = ["w_in", "w_branch_attn", "w_branch_gmlp", "w_out", "w_mlp_in", "w_mlp_out"]
    big_w = [w_in[0], w_branch_attn[0], w_branch_gmlp[0], w_out[0], w_mlp_in[0], w_mlp_out[0]]
    big_m = [m_w_in[0], m_w_branch_attn[0], m_w_branch_gmlp[0], m_w_out[0], m_w_mlp_in[0], m_w_mlp_out[0]]
    big_v = [v_w_in[0], v_w_branch_attn[0], v_w_branch_gmlp[0], v_w_out[0], v_w_mlp_in[0], v_w_mlp_out[0]]
    upd = {nm: _adam_shard(own, r, w, m, v, "adam_" + nm)
           for nm, own, r, w, m, v in zip(names, owns, list(from_chips) + list(mlp_chips), big_w, big_m, big_v)}

    pkt = _small_allreduce(_pack_small(loss_row, small))
    small_w = dict(g1=norm_pre_mix, g2=norm_post_mix, g3=norm_pre_mlp, g4=norm_post_mlp, ln_g=ln_v_gain,
                   ln_b=ln_v_bias, b_sp=b_spatial[0], w_sp=w_spatial[0])
    small_m = dict(g1=m_norm_pre_mix, g2=m_norm_post_mix, g3=m_norm_pre_mlp, g4=m_norm_post_mlp, ln_g=m_ln_v_gain,
                   ln_b=m_ln_v_bias, b_sp=m_b_spatial[0], w_sp=m_w_spatial[0])
    small_v = dict(g1=v_norm_pre_mix, g2=v_norm_post_mix, g3=v_norm_pre_mlp, g4=v_norm_post_mlp, ln_g=v_ln_v_gain,
                   ln_b=v_ln_v_bias, b_sp=v_b_spatial[0], w_sp=v_w_spatial[0])
    zero_row = jnp.zeros((1, 128), F32)
    ones_row = jnp.ones((1, 128), F32)
    sd, sm, sv = _adam_small(pkt, _pack_small(zero_row, small_w), _pack_small(zero_row, small_m),
                             _pack_small(ones_row, small_v))
    loss = pkt[0, 0]
    sg, sd, sm, sv = (_unpack_small(p) for p in (pkt, sd, sm, sv))

    order = ["g1", "w_in", "w_sp", "b_sp", "ln_g", "ln_b", "w_branch_attn", "w_branch_gmlp", "w_out", "g2", "g3",
             "w_mlp_in", "w_mlp_out", "g4"]

    def pick(which):
        out = []
        for nm in order:
            if nm in upd:
                out.append(upd[nm][which][None])
            else:
                out.append((sg, sd, sm, sv)[which][nm])
        return out

    return (loss, grad_x[None], *pick(0), *pick(1), *pick(2), *pick(3))
```

```python
import functools
import math

import jax
import jax.numpy as jnp
from jax import lax
from jax.experimental import pallas as pl
from jax.experimental.pallas import tpu as pltpu

D = 1024
HD = 64
NSLOT = 4
GW = NSLOT * HD
DILS = (1, 4, 16)
QB = 128
ATTN_NSUB = 2
MLP_TM_FWD = 512
MLP_TM_BWD = 256
AW = 3 * GW
GMLP_W = 512
CHUNK = 128
DFF = 4096
EPS = 1e-6
ROPE_THETA = 10000.0
SCALE = HD ** -0.5
NEG = -1e30

ADAM_LR = 0.001
ADAM_B1 = 0.9
ADAM_B2 = 0.999
ADAM_EPS = 1e-08
ADAM_WD = 0.01
ADAM_STEP = 10

BF = jnp.bfloat16
F32 = jnp.float32
MESH = pl.DeviceIdType.MESH
VMEM_LIMIT = 56 * 1024 * 1024

PK_LOSS, PK_G1, PK_G2, PK_G3, PK_G4, PK_LNG, PK_LNB, PK_BSP, PK_WSP, PKT_ROWS = 0, 8, 16, 24, 32, 40, 44, 48, 56, 568


def _cp(sem):
    return pltpu.CompilerParams(dimension_semantics=sem, vmem_limit_bytes=VMEM_LIMIT)


def _dot(a, b):
    return jnp.dot(a, b, preferred_element_type=F32)


def _dot_nt(a, b):
    return lax.dot_general(a, b, (((1,), (1,)), ((), ())), preferred_element_type=F32)


def _dot_tn(a, b):
    return lax.dot_general(a, b, (((0,), (0,)), ((), ())), preferred_element_type=F32)


def _gelu(x):
    return jax.nn.gelu(x, approximate=True)


def _gelu_grad(x):
    k = math.sqrt(2.0 / math.pi)
    t = jnp.tanh(k * (x + 0.044715 * x * x * x))
    return 0.5 * (1.0 + t) + 0.5 * x * (1.0 - t * t) * (k * (1.0 + 3.0 * 0.044715 * x * x))


def _swap_halves(t):
    w = t.shape[1]
    lane = lax.broadcasted_iota(jnp.int32, t.shape, 1)
    first = (lane & (HD - 1)) < (HD // 2)
    return jnp.where(first, pltpu.roll(t, w - HD // 2, 1), pltpu.roll(t, HD // 2, 1))


def _head_mask(hh):
    lane = lax.broadcasted_iota(jnp.int32, (1, GW), 1)
    return jnp.logical_and(lane >= hh * HD, lane < (hh + 1) * HD)


def _rms_stats(xf):
    return lax.rsqrt(jnp.mean(xf * xf, axis=-1, keepdims=True) + EPS)


def _rms_bwd(xf, r, gain, dout):
    n = xf * r
    t = dout * gain
    dx = r * (t - n * jnp.mean(t * n, axis=-1, keepdims=True))
    return dx, jnp.sum(dout * n, axis=0, keepdims=True)


def _rms_fwd(x, gain, name):
    S = x.shape[0]
    tm = min(512, S)

    def body(x_ref, g_ref, h_ref):
        xf = x_ref[...]
        h_ref[...] = (xf * _rms_stats(xf) * g_ref[...]).astype(BF)

    return pl.pallas_call(
        body, name=name, out_shape=jax.ShapeDtypeStruct((S, D), BF), grid=(S // tm,),
        in_specs=[pl.BlockSpec((tm, D), lambda i: (i, 0)), pl.BlockSpec((1, D), lambda i: (0, 0))],
        out_specs=pl.BlockSpec((tm, D), lambda i: (i, 0)), compiler_params=_cp(("parallel",)))(x, gain)


def _proj(h, w, epi, name, tn=256):
    S, K = h.shape
    N = w.shape[1]
    tm = min(2048, S)

    def body(h_ref, w_ref, o_ref):
        y = _dot(h_ref[...], w_ref[...])
        if epi == "sigmoid":
            y = jax.nn.sigmoid(y)
        o_ref[...] = y.astype(BF)

    return pl.pallas_call(
        body, name=name, out_shape=jax.ShapeDtypeStruct((S, N), BF), grid=(S // tm, N // tn),
        in_specs=[pl.BlockSpec((tm, K), lambda i, j: (i, 0)), pl.BlockSpec((K, tn), lambda i, j: (0, j))],
        out_specs=pl.BlockSpec((tm, tn), lambda i, j: (i, j)),
        compiler_params=_cp(("parallel", "arbitrary")))(h, w)


def _dilate_store(val, scr, o_ref, lead, d):
    rows = val.shape[0]
    if d == 1:
        o_ref[lead + (0,)] = val.astype(o_ref.dtype)
        return
    for hf in range(2):
        scr[hf, pl.ds(0, rows), :] = val[:, hf * 128:(hf + 1) * 128]
    for r in range(d):
        for hf in range(2):
            o_ref[lead + (r, slice(None), slice(hf * 128, (hf + 1) * 128))] = (
                scr[hf, pl.ds(r, rows // d, stride=d), :].astype(o_ref.dtype))


def _undilate_load(i_ref, lead, d, scr, rows):
    if d == 1:
        return i_ref[lead + (0,)].astype(F32)
    for r in range(d):
        for hf in range(2):
            scr[hf, pl.ds(r, rows // d, stride=d), :] = (
                i_ref[lead + (r, slice(None), slice(hf * 128, (hf + 1) * 128))].astype(F32))
    return jnp.concatenate([scr[0, pl.ds(0, rows), :], scr[1, pl.ds(0, rows), :]], axis=1)


def _rope_fwd(y, c_ref, s_ref):
    cosv = jnp.concatenate([c_ref[...]] * 2, axis=1)
    sinv = jnp.concatenate([s_ref[...]] * 2, axis=1)
    return y * cosv + _swap_halves(y) * sinv


def _rope_bwd(dy, c_ref, s_ref):
    cosv = jnp.concatenate([c_ref[...]] * 2, axis=1)
    sinv = jnp.concatenate([s_ref[...]] * 2, axis=1)
    return dy * cosv + _swap_halves(dy * sinv)


def _proj_qkv(h, wqkv, cos_t, sin_t, g, d, name):
    S, K = h.shape
    tm = min(2048, S)

    def body(h_ref, w_ref, c_ref, s_ref, o_ref, scr):
        j = pl.program_id(1)
        y = _dot(h_ref[...], w_ref[...])
        y = jnp.where(j < 2, _rope_fwd(y, c_ref, s_ref), y)
        _dilate_store(y, scr, o_ref, (), d)

    return pl.pallas_call(
        body, name=name, out_shape=jax.ShapeDtypeStruct((3, d, S // d, GW), BF), grid=(S // tm, 3),
        in_specs=[pl.BlockSpec((tm, K), lambda i, j: (i, 0)), pl.BlockSpec((K, GW), lambda i, j: (0, 3 * j + g)),
                  pl.BlockSpec((tm, 128), lambda i, j: (i, 0)), pl.BlockSpec((tm, 128), lambda i, j: (i, 0))],
        out_specs=pl.BlockSpec((None, d, tm // d, GW), lambda i, j: (j, 0, i, 0)),
        scratch_shapes=[pltpu.VMEM((2, tm, 128), F32)],
        compiler_params=_cp(("parallel", "arbitrary")))(h, wqkv, cos_t, sin_t)


def _band_masks(first_step):
    row = lax.broadcasted_iota(jnp.int32, (QB, 2 * QB), 0)
    col = lax.broadcasted_iota(jnp.int32, (QB, 2 * QB), 1)
    band = jnp.logical_and(col >= row, col <= row + QB)
    return band, jnp.logical_and(band, jnp.logical_or(col >= QB, jnp.logical_not(first_step)))


def _attn_fwd(qkv, name):
    _, d, L, _ = qkv.shape
    nsub = min(ATTN_NSUB, L // QB)
    R = nsub * QB
    nsteps = L // R

    def body(q_ref, kp_ref, kc_ref, vp_ref, vc_ref, o_ref, lse_ref):
        i = pl.program_id(1)
        band, band_first = _band_masks(i == 0)
        kfull = jnp.concatenate([kp_ref[...], kc_ref[...]], axis=0)
        vfull = jnp.concatenate([vp_ref[...], vc_ref[...]], axis=0)
        chains = [(sb, hh) for sb in range(nsub) for hh in range(NSLOT)]
        win = lambda t, sb: t[sb * QB:(sb + 2) * QB]
        scores = []
        for sb, hh in chains:
            qh = jnp.where(_head_mask(hh), q_ref[sb * QB:(sb + 1) * QB, :], 0)
            scores.append(_dot_nt(qh, win(kfull, sb)))
        soft = []
        for (sb, hh), sc in zip(chains, scores):
            sc = jnp.where(band_first if sb == 0 else band, sc * SCALE, NEG)
            m = jnp.max(sc, axis=1, keepdims=True)
            p = jnp.exp(sc - m)
            den = jnp.sum(p, axis=1, keepdims=True)
            soft.append((p.astype(BF), den, m + jnp.log(den)))
        accs = [_dot(p, win(vfull, sb)) for (sb, hh), (p, _, _) in zip(chains, soft)]
        for sb in range(nsub):
            o = jnp.zeros((QB, GW), F32)
            lse = jnp.zeros((QB, GW), F32)
            for hh in range(NSLOT):
                hm = _head_mask(hh)
                _, den, lrow = soft[sb * NSLOT + hh]
                o = o + jnp.where(hm, accs[sb * NSLOT + hh] / den, 0.0)
                lse = lse + jnp.where(hm, lrow, 0.0)
            o_ref[sb * QB:(sb + 1) * QB, :] = o
            lse_ref[sb * QB:(sb + 1) * QB, :] = lse

    prev = lambda i: jnp.maximum(i * nsub - 1, 0)
    cur = lambda t: pl.BlockSpec((None, None, R, GW), lambda r, i: (t, r, i, 0))
    prv = lambda t: pl.BlockSpec((None, None, QB, GW), lambda r, i: (t, r, prev(i), 0))
    out = pl.BlockSpec((None, R, GW), lambda r, i: (r, i, 0))
    return pl.pallas_call(
        body, name=name, grid=(d, nsteps),
        out_shape=(jax.ShapeDtypeStruct((d, L, GW), F32), jax.ShapeDtypeStruct((d, L, GW), F32)),
        in_specs=[cur(0), prv(1), cur(1), prv(2), cur(2)],
        out_specs=(out, out), compiler_params=_cp(("parallel", "arbitrary")))(qkv, qkv, qkv, qkv, qkv)


def _attn_combine(os_, lses):
    S = os_[0].shape[1]
    tm = min(1024, S)

    def body(o0, o1, o2, l0, l1, l2, y_ref, j0, j1, j2, scr):
        os_nat = [_undilate_load(o, (), d, scr, tm) for o, d in zip((o0, o1, o2), DILS)]
        a, b, c = [_undilate_load(l, (), d, scr, tm) for l, d in zip((l0, l1, l2), DILS)]
        m = jnp.maximum(jnp.maximum(a, b), c)
        wa, wb, wc = jnp.exp(a - m), jnp.exp(b - m), jnp.exp(c - m)
        den = wa + wb + wc
        y_ref[...] = ((wa * os_nat[0] + wb * os_nat[1] + wc * os_nat[2]) / den).astype(BF)
        lse = m + jnp.log(den)
        for j_ref, d in zip((j0, j1, j2), DILS):
            _dilate_store(lse, scr, j_ref, (), d)

    dil = lambda d: pl.BlockSpec((d, tm // d, GW), lambda i: (0, i, 0))
    dshape = lambda d: jax.ShapeDtypeStruct((d, S // d, GW), F32)
    return pl.pallas_call(
        body, name="attn_combine", grid=(S // tm,),
        out_shape=(jax.ShapeDtypeStruct((S, GW), BF),) + tuple(dshape(d) for d in DILS),
        in_specs=[dil(d) for d in DILS] * 2,
        out_specs=(pl.BlockSpec((tm, GW), lambda i: (i, 0)),) + tuple(dil(d) for d in DILS),
        scratch_shapes=[pltpu.VMEM((2, tm, 128), F32)],
        compiler_params=_cp(("parallel",)))(*os_, *lses)


def _tril(upper=False):
    row = lax.broadcasted_iota(jnp.int32, (CHUNK, CHUNK), 0)
    col = lax.broadcasted_iota(jnp.int32, (CHUNK, CHUNK), 1)
    return row <= col if upper else col <= row


def _ln_fwd(z, gain, bias):
    mu = jnp.mean(z, axis=-1, keepdims=True)
    zc = z - mu
    rstd = lax.rsqrt(jnp.mean(zc * zc, axis=-1, keepdims=True) + EPS)
    zhat = zc * rstd
    return zhat, rstd, zhat * gain + bias


def _gmlp_fwd(puz, ln_g, ln_b, w_sp, b_sp_t):
    S = puz.shape[0]
    tm = min(512, S)
    nch = tm // CHUNK

    def body(p_ref, g_ref, b_ref, w_ref, bt_ref, o_ref):
        tril = _tril()
        ws = [jnp.where(tril, w_ref[gg], 0.0).astype(BF) for gg in range(4)]
        for ch in range(nch):
            rows = slice(ch * CHUNK, (ch + 1) * CHUNK)
            z = _gelu(p_ref[rows, GMLP_W:].astype(F32))
            _, _, zn = _ln_fwd(z, g_ref[...], b_ref[...])
            zn = zn.astype(BF)
            for gg in range(4):
                cols = slice(gg * CHUNK, (gg + 1) * CHUNK)
                sz = _dot(ws[gg], zn[:, cols]) + bt_ref[:, gg:gg + 1]
                u = _gelu(p_ref[rows, cols].astype(F32))
                o_ref[rows, cols] = (u * sz).astype(BF)

    return pl.pallas_call(
        body, name="gmlp_fwd", out_shape=jax.ShapeDtypeStruct((S, GMLP_W), BF), grid=(S // tm,),
        in_specs=[pl.BlockSpec((tm, 2 * GMLP_W), lambda i: (i, 0)),
                  pl.BlockSpec((1, GMLP_W), lambda i: (0, 0)), pl.BlockSpec((1, GMLP_W), lambda i: (0, 0)),
                  pl.BlockSpec((4, CHUNK, CHUNK), lambda i: (0, 0, 0)), pl.BlockSpec((CHUNK, 4), lambda i: (0, 0))],
        out_specs=pl.BlockSpec((tm, GMLP_W), lambda i: (i, 0)),
        compiler_params=_cp(("parallel",)))(puz, ln_g, ln_b, w_sp, b_sp_t)


def _merge_fwd(ya, yg, gates, wba, wbg, wout, x, g2, g3):
    S = x.shape[0]
    tm = min(512, S)

    def body(ya_ref, yg_ref, gt_ref, wba_ref, wbg_ref, wo_ref, x_ref, g2_ref, g3_ref,
             a_ref, gb_ref, mg_ref, y_ref, x2_ref, h3_ref):
        a = _dot(ya_ref[...], wba_ref[...])
        b = _dot(yg_ref[...], wbg_ref[...])
        a_ref[...] = a.astype(BF)
        gb_ref[...] = b.astype(BF)
        merged = (gt_ref[:, :D].astype(F32) * a + gt_ref[:, D:].astype(F32) * b).astype(BF)
        mg_ref[...] = merged
        y = _dot(merged, wo_ref[...])
        y_ref[...] = y
        x2 = x_ref[...] + y * _rms_stats(y) * g2_ref[...]
        x2_ref[...] = x2
        h3_ref[...] = (x2 * _rms_stats(x2) * g3_ref[...]).astype(BF)

    row = lambda w: pl.BlockSpec((tm, w), lambda i: (i, 0))
    full = lambda s: pl.BlockSpec(s, lambda i: (0, 0))
    return pl.pallas_call(
        body, name="merge_fwd", grid=(S // tm,),
        out_shape=(jax.ShapeDtypeStruct((S, D), BF), jax.ShapeDtypeStruct((S, D), BF), jax.ShapeDtypeStruct((S, D), BF),
                   jax.ShapeDtypeStruct((S, D), F32), jax.ShapeDtypeStruct((S, D), F32), jax.ShapeDtypeStruct((S, D), BF)),
        in_specs=[row(GW), row(GMLP_W), row(2 * D), full((GW, D)), full((GMLP_W, D)), full((D, D)), row(D),
                  full((1, D)), full((1, D))],
        out_specs=(row(D), row(D), row(D), row(D), row(D), row(D)),
        compiler_params=_cp(("parallel",)))(ya, yg, gates, wba, wbg, wout, x, g2, g3)


def _resident(shape):
    return pl.BlockSpec(shape, lambda i: (0,) * len(shape), pipeline_mode=pl.Buffered(1))


def _mlp_fwd(h3, wmi, wmo, x2, tgt, g4):
    S = x2.shape[0]
    tm = min(MLP_TM_FWD, S)

    def body(h_ref, wi_ref, wo_ref, x2_ref, t_ref, g4_ref, a_ref, dy2_ref, dout_ref, loss_ref, dg4_ref):
        @pl.when(pl.program_id(0) == 0)
        def _():
            loss_ref[...] = jnp.zeros_like(loss_ref)
            dg4_ref[...] = jnp.zeros_like(dg4_ref)

        a = jnp.maximum(_dot(h_ref[...], wi_ref[...]), 0.0)
        a_ref[...] = a.astype(BF)
        y2 = _dot((a * a).astype(BF), wo_ref[...])
        r = _rms_stats(y2)
        out = x2_ref[...] + y2 * r * g4_ref[...]
        err = out - t_ref[...]
        tot = jnp.sum(jnp.sum(err * err, axis=1, keepdims=True), axis=0, keepdims=True) * (0.5 / D)
        lane = lax.broadcasted_iota(jnp.int32, (1, 128), 1)
        loss_ref[...] += jnp.where(lane == 0, tot, 0.0)
        dout = err * (1.0 / D)
        dout_ref[...] = dout
        dy2, dg = _rms_bwd(y2, r, g4_ref[...], dout)
        dy2_ref[...] = dy2.astype(BF)
        dg4_ref[...] += dg

    row = pl.BlockSpec((tm, D), lambda i: (i, 0))
    return pl.pallas_call(
        body, name="mlp_fwd", grid=(S // tm,),
        out_shape=(jax.ShapeDtypeStruct((S, DFF), BF), jax.ShapeDtypeStruct((S, D), BF), jax.ShapeDtypeStruct((S, D), F32),
                   jax.ShapeDtypeStruct((1, 128), F32), jax.ShapeDtypeStruct((1, D), F32)),
        in_specs=[row, _resident((D, DFF)), _resident((DFF, D)), row, row, pl.BlockSpec((1, D), lambda i: (0, 0))],
        out_specs=(pl.BlockSpec((tm, DFF), lambda i: (i, 0)), row, row,
                   pl.BlockSpec((1, 128), lambda i: (0, 0)), pl.BlockSpec((1, D), lambda i: (0, 0))),
        compiler_params=_cp(("arbitrary",)))(h3, wmi, wmo, x2, tgt, g4)


def _mlp_bwd(dy2, a, wmo, wmi, x2, y, dout, g2, g3):
    S = x2.shape[0]
    tm = min(MLP_TM_BWD, S)

    def body(dy2_ref, a_ref, wo_ref, wi_ref, x2_ref, y_ref, dout_ref, g2_ref, g3_ref,
             dpre_ref, dx2_ref, dy_ref, dg3_ref, dg2_ref):
        @pl.when(pl.program_id(0) == 0)
        def _():
            dg3_ref[...] = jnp.zeros_like(dg3_ref)
            dg2_ref[...] = jnp.zeros_like(dg2_ref)

        da2 = _dot_nt(dy2_ref[...], wo_ref[...])
        dpre = (2.0 * a_ref[...].astype(F32) * da2).astype(BF)
        dpre_ref[...] = dpre
        dh3 = _dot_nt(dpre, wi_ref[...])
        x2 = x2_ref[...]
        dx3, dg3 = _rms_bwd(x2, _rms_stats(x2), g3_ref[...], dh3)
        dx2 = dout_ref[...] + dx3
        dx2_ref[...] = dx2
        dg3_ref[...] += dg3
        yv = y_ref[...]
        dy, dg2 = _rms_bwd(yv, _rms_stats(yv), g2_ref[...], dx2)
        dy_ref[...] = dy.astype(BF)
        dg2_ref[...] += dg2

    row = pl.BlockSpec((tm, D), lambda i: (i, 0))
    wide = pl.BlockSpec((tm, DFF), lambda i: (i, 0))
    vec = pl.BlockSpec((1, D), lambda i: (0, 0))
    return pl.pallas_call(
        body, name="mlp_bwd", grid=(S // tm,),
        out_shape=(jax.ShapeDtypeStruct((S, DFF), BF), jax.ShapeDtypeStruct((S, D), F32), jax.ShapeDtypeStruct((S, D), BF),
                   jax.ShapeDtypeStruct((1, D), F32), jax.ShapeDtypeStruct((1, D), F32)),
        in_specs=[row, wide, _resident((DFF, D)), _resident((D, DFF)), row, row, row, vec, vec],
        out_specs=(wide, row, row, vec, vec),
        compiler_params=_cp(("arbitrary",)))(dy2, a, wmo, wmi, x2, y, dout, g2, g3)


def _mm_tn(a, b, name, square_a=False, tm=1024, tn=1024, tk=2048, tie=None):
    S, M = a.shape
    N = b.shape[1]
    tm, tk = min(tm, M), min(tk, S)
    tn = max(t for t in range(128, min(tn, N) + 1, 128) if N % t == 0)
    assert M % tm == 0 and S % tk == 0
    nk = S // tk
    ties = () if tie is None else (tie,)

    def body(a_ref, b_ref, *rest):
        o_ref = rest[-1]
        k = pl.program_id(2)
        av = a_ref[...]
        if square_a:
            av = av * av
        part = _dot_tn(av, b_ref[...])

        @pl.when(k == 0)
        def _():
            o_ref[...] = part

        @pl.when(k > 0)
        def _():
            o_ref[...] += part

    return pl.pallas_call(
        body, name=name, out_shape=jax.ShapeDtypeStruct((M, N), F32), grid=(M // tm, N // tn, nk),
        in_specs=[pl.BlockSpec((tk, tm), lambda i, j, k: (k, i)), pl.BlockSpec((tk, tn), lambda i, j, k: (k, j))]
        + [pl.BlockSpec(memory_space=pl.ANY)] * len(ties),
        out_specs=pl.BlockSpec((tm, tn), lambda i, j, k: (i, j)),
        compiler_params=_cp(("parallel", "parallel", "arbitrary")))(a, b, *ties)


def _outproj_bwd(dy, wout, abr, gbr, gates, wba, wbg, ya, tie):
    S = dy.shape[0]
    tm = min(512, S)

    def body(dy_ref, wo_ref, a_ref, b_ref, gt_ref, wba_ref, wbg_ref, ya_ref, tie_ref,
             dgt_ref, da_ref, db_ref, dyg_ref, e0, e1, e2, s0, s1, s2, scr):
        dm = _dot_nt(dy_ref[...], wo_ref[...])
        ga, gb = gt_ref[:, :D].astype(F32), gt_ref[:, D:].astype(F32)
        dgt_ref[:, :D] = (dm * a_ref[...].astype(F32) * ga * (1.0 - ga)).astype(BF)
        dgt_ref[:, D:] = (dm * b_ref[...].astype(F32) * gb * (1.0 - gb)).astype(BF)
        da = (dm * ga).astype(BF)
        db = (dm * gb).astype(BF)
        da_ref[...] = da
        db_ref[...] = db
        dyg_ref[...] = _dot_nt(db, wbg_ref[...]).astype(BF)
        dya = _dot_nt(da, wba_ref[...]).astype(BF).astype(F32)
        dyy = dya * ya_ref[...].astype(F32)
        dsum = jnp.zeros((tm, GW), F32)
        for hh in range(NSLOT):
            hm = _head_mask(hh)
            dsum = dsum + jnp.where(hm, jnp.sum(jnp.where(hm, dyy, 0.0), axis=1, keepdims=True), 0.0)
        for e_ref, s_ref, d in zip((e0, e1, e2), (s0, s1, s2), DILS):
            _dilate_store(dya, scr, e_ref, (), d)
            _dilate_store(dsum, scr, s_ref, (), d)

    row = lambda w: pl.BlockSpec((tm, w), lambda i: (i, 0))
    full = lambda s: pl.BlockSpec(s, lambda i: (0, 0))
    dil = lambda d: pl.BlockSpec((d, tm // d, GW), lambda i: (0, i, 0))
    dshape = lambda d, t: jax.ShapeDtypeStruct((d, S // d, GW), t)
    return pl.pallas_call(
        body, name="outproj_bwd", grid=(S // tm,),
        out_shape=(jax.ShapeDtypeStruct((S, 2 * D), BF), jax.ShapeDtypeStruct((S, D), BF), jax.ShapeDtypeStruct((S, D), BF),
                   jax.ShapeDtypeStruct((S, GMLP_W), BF)) + tuple(dshape(d, BF) for d in DILS)
        + tuple(dshape(d, F32) for d in DILS),
        in_specs=[row(D), full((D, D)), row(D), row(D), row(2 * D), full((GW, D)), full((GMLP_W, D)), row(GW),
                  pl.BlockSpec(memory_space=pl.ANY)],
        out_specs=(row(2 * D), row(D), row(D), row(GMLP_W)) + tuple(dil(d) for d in DILS) * 2,
        scratch_shapes=[pltpu.VMEM((2, tm, 128), F32)],
        compiler_params=_cp(("parallel",)))(dy, wout, abr, gbr, gates, wba, wbg, ya, tie)


def _gmlp_bwd(puz, dyg, ln_g, ln_b, w_sp, w_sp_t, b_sp_t):
    S = puz.shape[0]
    tm = min(512, S)
    nch = tm // CHUNK

    def body(p_ref, dy_ref, g_ref, b_ref, w_ref, wt_ref, bt_ref,
             dp_ref, dw_ref, dbs_ref, dg_ref, dbias_ref, dbacc_ref):
        i = pl.program_id(0)

        @pl.when(i == 0)
        def _():
            dw_ref[...] = jnp.zeros_like(dw_ref)
            dbacc_ref[...] = jnp.zeros_like(dbacc_ref)
            dg_ref[...] = jnp.zeros_like(dg_ref)
            dbias_ref[...] = jnp.zeros_like(dbias_ref)

        tril = _tril()
        ws = [jnp.where(tril, w_ref[gg], 0.0).astype(BF) for gg in range(4)]
        triu = _tril(upper=True)
        wts = [jnp.where(triu, wt_ref[gg], 0.0).astype(BF) for gg in range(4)]
        gain = g_ref[...]
        for ch in range(nch):
            rows = slice(ch * CHUNK, (ch + 1) * CHUNK)
            pz = p_ref[rows, GMLP_W:].astype(F32)
            z = _gelu(pz)
            zhat, rstd, zn = _ln_fwd(z, gain, b_ref[...])
            znb = zn.astype(BF)
            dzn_parts = []
            for gg in range(4):
                cols = slice(gg * CHUNK, (gg + 1) * CHUNK)
                pu = p_ref[rows, cols].astype(F32)
                u = _gelu(pu)
                sz = _dot(ws[gg], znb[:, cols]) + bt_ref[:, gg:gg + 1]
                dyv = dy_ref[rows, cols].astype(F32)
                dp_ref[rows, cols] = (dyv * sz * _gelu_grad(pu)).astype(BF)
                dsz = dyv * u
                dbacc_ref[gg] += dsz
                dszb = dsz.astype(BF)
                dw_ref[gg] += _dot_nt(dszb, znb[:, cols])
                dzn_parts.append(_dot(wts[gg], dszb))
            dzn = jnp.concatenate(dzn_parts, axis=1)
            dg_ref[...] += jnp.sum(dzn * zhat, axis=0, keepdims=True)
            dbias_ref[...] += jnp.sum(dzn, axis=0, keepdims=True)
            dzh = dzn * gain
            dz = rstd * (dzh - jnp.mean(dzh, axis=-1, keepdims=True)
                         - zhat * jnp.mean(dzh * zhat, axis=-1, keepdims=True))
            dp_ref[rows, GMLP_W:] = (dz * _gelu_grad(pz)).astype(BF)

        @pl.when(i == pl.num_programs(0) - 1)
        def _():
            for gg in range(4):
                dw_ref[gg] = jnp.where(tril, dw_ref[gg], 0.0)
                dbs_ref[gg] = jnp.sum(dbacc_ref[gg], axis=1, keepdims=True)

    full2 = lambda s: pl.BlockSpec(s, lambda i: (0, 0))
    full3 = lambda s: pl.BlockSpec(s, lambda i: (0, 0, 0))
    return pl.pallas_call(
        body, name="gmlp_bwd", grid=(S // tm,),
        out_shape=(jax.ShapeDtypeStruct((S, 2 * GMLP_W), BF), jax.ShapeDtypeStruct((4, CHUNK, CHUNK), F32),
                   jax.ShapeDtypeStruct((4, CHUNK, 1), F32), jax.ShapeDtypeStruct((1, GMLP_W), F32),
                   jax.ShapeDtypeStruct((1, GMLP_W), F32)),
        in_specs=[pl.BlockSpec((tm, 2 * GMLP_W), lambda i: (i, 0)), pl.BlockSpec((tm, GMLP_W), lambda i: (i, 0)),
                  full2((1, GMLP_W)), full2((1, GMLP_W)), full3((4, CHUNK, CHUNK)), full3((4, CHUNK, CHUNK)),
                  full2((CHUNK, 4))],
        out_specs=(pl.BlockSpec((tm, 2 * GMLP_W), lambda i: (i, 0)), full3((4, CHUNK, CHUNK)), full3((4, CHUNK, 1)),
                   full2((1, GMLP_W)), full2((1, GMLP_W))),
        scratch_shapes=[pltpu.VMEM((4, CHUNK, CHUNK), F32)],
        compiler_params=_cp(("arbitrary",)))(puz, dyg, ln_g, ln_b, w_sp, w_sp_t, b_sp_t)


def _attn_bwd(qkv, dya, dsums, lse, name):
    _, d, L, _ = qkv.shape
    nsub = min(ATTN_NSUB, L // QB)
    R = nsub * QB
    nsteps = L // R

    def body(q_ref, qn_ref, kp_ref, kc_ref, vp_ref, vc_ref, dy_ref, dyn_ref, e_ref, en_ref, l_ref, ln_ref, o_ref):
        i = pl.program_id(1)
        band, band_first = _band_masks(i == 0)
        row = lax.broadcasted_iota(jnp.int32, (QB, QB), 0)
        col = lax.broadcasted_iota(jnp.int32, (QB, QB), 1)
        mask_next = jnp.logical_and(col >= row, i < nsteps - 1)
        kc, vc = kc_ref[...], vc_ref[...]
        kfull = jnp.concatenate([kp_ref[...], kc], axis=0)
        vfull = jnp.concatenate([vp_ref[...], vc], axis=0)
        k_last, v_last = kc[(nsub - 1) * QB:], vc[(nsub - 1) * QB:]
        q_ext = jnp.concatenate([q_ref[...], qn_ref[...]], axis=0)
        dy_ext = jnp.concatenate([dy_ref[...], dyn_ref[...]], axis=0)
        esum, esum_n, lse, lse_n = e_ref[...], en_ref[...], l_ref[...], ln_ref[...]
        win = lambda t, sb: t[sb * QB:(sb + 2) * QB]
        blk = lambda t, sb: t[sb * QB:(sb + 1) * QB]
        hms = [_head_mask(hh) for hh in range(NSLOT)]
        q_hs = [jnp.where(hm, q_ext, 0) for hm in hms]
        dy_hs = [jnp.where(hm, dy_ext, 0) for hm in hms]
        raw = []
        for hh in range(NSLOT):
            tiles = [(_dot_nt(blk(q_hs[hh], sb), win(kfull, sb)), _dot_nt(blk(dy_hs[hh], sb), win(vfull, sb)))
                     for sb in range(nsub)]
            tiles.append((_dot_nt(blk(q_hs[hh], nsub), k_last), _dot_nt(blk(dy_hs[hh], nsub), v_last)))
            raw.append(tiles)
        ps, dss = [], []
        for hh in range(NSLOT):
            rowstat = lambda t: jnp.max(jnp.where(hms[hh], t, -jnp.inf), axis=1, keepdims=True)
            p_h, ds_h = [], []
            for sb in range(nsub + 1):
                sc, dp = raw[hh][sb]
                if sb < nsub:
                    msk, lrow, erow = (band_first if sb == 0 else band), rowstat(blk(lse, sb)), rowstat(blk(esum, sb))
                else:
                    msk, lrow, erow = mask_next, rowstat(lse_n), rowstat(esum_n)
                p = jnp.where(msk, jnp.exp(sc * SCALE - lrow), 0.0)
                p_h.append(p.astype(BF))
                ds_h.append((p * (dp - erow)).astype(BF))
            ps.append(p_h)
            dss.append(ds_h)
        dq = [jnp.zeros((QB, GW), F32) for _ in range(nsub)]
        dk = [jnp.zeros((QB, GW), F32) for _ in range(nsub)]
        dv = [jnp.zeros((QB, GW), F32) for _ in range(nsub)]
        for hh in range(NSLOT):
            for sb in range(nsub):
                dq[sb] = dq[sb] + jnp.where(hms[hh], _dot(dss[hh][sb], win(kfull, sb)), 0.0)
                nxt = lambda t: t[sb + 1][:, :QB] if sb + 1 < nsub else t[nsub]
                dk[sb] = dk[sb] + _dot_tn(jnp.concatenate([dss[hh][sb][:, QB:], nxt(dss[hh])], axis=0), win(q_hs[hh], sb))
                dv[sb] = dv[sb] + _dot_tn(jnp.concatenate([ps[hh][sb][:, QB:], nxt(ps[hh])], axis=0), win(dy_hs[hh], sb))
        for sb in range(nsub):
            rows = slice(sb * QB, (sb + 1) * QB)
            o_ref[0, rows, :] = (dq[sb] * SCALE).astype(BF)
            o_ref[1, rows, :] = (dk[sb] * SCALE).astype(BF)
            o_ref[2, rows, :] = dv[sb].astype(BF)

    prev = lambda i: jnp.maximum(i * nsub - 1, 0)
    nxt = lambda i: jnp.minimum((i + 1) * nsub, L // QB - 1)
    cur4 = lambda t: pl.BlockSpec((None, None, R, GW), lambda r, i: (t, r, i, 0))
    prv4 = lambda t: pl.BlockSpec((None, None, QB, GW), lambda r, i: (t, r, prev(i), 0))
    nxt4 = lambda t: pl.BlockSpec((None, None, QB, GW), lambda r, i: (t, r, nxt(i), 0))
    cur3 = pl.BlockSpec((None, R, GW), lambda r, i: (r, i, 0))
    nxt3 = pl.BlockSpec((None, QB, GW), lambda r, i: (r, nxt(i), 0))
    return pl.pallas_call(
        body, name=name, grid=(d, nsteps), out_shape=jax.ShapeDtypeStruct((3, d, L, GW), BF),
        in_specs=[cur4(0), nxt4(0), prv4(1), cur4(1), prv4(2), cur4(2), cur3, nxt3, cur3, nxt3, cur3, nxt3],
        out_specs=pl.BlockSpec((3, None, R, GW), lambda r, i: (0, r, i, 0)),
        compiler_params=_cp(("parallel", "arbitrary")))(qkv, qkv, qkv, qkv, qkv, qkv, dya, dya, dsums, dsums, lse, lse)


def _inproj_bwd(dqkvs, dpuz, dgates, wqkv, wuz, wg, cos_t, sin_t, x, dx2, g1):
    S = x.shape[0]
    tm = min(512, S)

    def body(d0_ref, d1_ref, d2_ref, dp_ref, dg_ref, wqkv_ref, wuz_ref, wg_ref, c_ref, s_ref,
             x_ref, dx2_ref, g1_ref, gx_ref, dg1_ref, dn_ref, scr):
        i = pl.program_id(0)

        @pl.when(i == 0)
        def _():
            dg1_ref[...] = jnp.zeros_like(dg1_ref)

        for t in range(3):
            for g, (d_ref, d) in enumerate(zip((d0_ref, d1_ref, d2_ref), DILS)):
                piece = _undilate_load(d_ref, (t,), d, scr, tm)
                if t < 2:
                    piece = _rope_bwd(piece, c_ref, s_ref)
                dn_ref[:, (3 * t + g) * GW:(3 * t + g + 1) * GW] = piece.astype(BF)
        dh = (_dot_nt(dn_ref[...], wqkv_ref[...]) + _dot_nt(dp_ref[...], wuz_ref[...])
              + _dot_nt(dg_ref[...], wg_ref[...]))
        xv = x_ref[...]
        dx1, dg1 = _rms_bwd(xv, _rms_stats(xv), g1_ref[...], dh)
        gx_ref[...] = dx2_ref[...] + dx1
        dg1_ref[...] += dg1

    row = lambda w: pl.BlockSpec((tm, w), lambda i: (i, 0))
    full = lambda s: pl.BlockSpec(s, lambda i: (0, 0))
    dil = lambda d: pl.BlockSpec((3, d, tm // d, GW), lambda i: (0, 0, i, 0))
    return pl.pallas_call(
        body, name="inproj_bwd", grid=(S // tm,),
        out_shape=(jax.ShapeDtypeStruct((S, D), F32), jax.ShapeDtypeStruct((1, D), F32),
                   jax.ShapeDtypeStruct((S, 3 * AW), BF)),
        in_specs=[dil(d) for d in DILS] + [row(2 * GMLP_W), row(2 * D), full((D, 3 * AW)), full((D, 2 * GMLP_W)),
                                            full((D, 2 * D)), row(128), row(128), row(D), row(D), full((1, D))],
        out_specs=(row(D), full((1, D)), row(3 * AW)),
        scratch_shapes=[pltpu.VMEM((2, tm, 128), F32)],
        compiler_params=_cp(("arbitrary",)))(*dqkvs, dpuz, dgates, wqkv, wuz, wg, cos_t, sin_t, x, dx2, g1)


def _adam_math(w, g, m, v):
    m2 = ADAM_B1 * m + (1.0 - ADAM_B1) * g
    v2 = ADAM_B2 * v + (1.0 - ADAM_B2) * (g * g)
    m_hat = m2 / (1.0 - ADAM_B1 ** ADAM_STEP)
    v_hat = v2 / (1.0 - ADAM_B2 ** ADAM_STEP)
    delta = -ADAM_LR * (m_hat / (jnp.sqrt(v_hat) + ADAM_EPS) + ADAM_WD * w)
    return delta, m2, v2


def _adam_shard(own, recv, w, m, v, name):
    R, C = w.shape
    tr = min(256, R)

    def body(own_ref, r_ref, w_ref, m_ref, v_ref, g_ref, d_ref, m2_ref, v2_ref):
        g = own_ref[...] + r_ref[0].astype(F32) + r_ref[1].astype(F32) + r_ref[2].astype(F32)
        g_ref[...] = g
        d_ref[...], m2_ref[...], v2_ref[...] = _adam_math(w_ref[...], g, m_ref[...], v_ref[...])

    spec = pl.BlockSpec((tr, C), lambda i: (i, 0))
    out = jax.ShapeDtypeStruct((R, C), F32)
    return pl.pallas_call(
        body, name=name, grid=(R // tr,), out_shape=(out, out, out, out),
        in_specs=[spec, pl.BlockSpec((3, tr, C), lambda i: (0, i, 0)), spec, spec, spec],
        out_specs=(spec, spec, spec, spec), compiler_params=_cp(("parallel",)))(own, recv, w, m, v)


def _adam_small(gp, wp, mp, vp):
    def body(g_ref, w_ref, m_ref, v_ref, d_ref, m2_ref, v2_ref):
        d_ref[...], m2_ref[...], v2_ref[...] = _adam_math(w_ref[...], g_ref[...], m_ref[...], v_ref[...])

    out = jax.ShapeDtypeStruct(gp.shape, F32)
    return pl.pallas_call(body, name="adam_small", out_shape=(out, out, out),
                          compiler_params=pltpu.CompilerParams(vmem_limit_bytes=VMEM_LIMIT))(gp, wp, mp, vp)


def _rs_add(gblocks, recv, idx, name):
    _, R, C = gblocks.shape
    tr = min(256, R)

    def body(t_ref, g_ref, r_ref, own_ref, send_ref):
        j = pl.program_id(1)
        s = g_ref[...] + r_ref[...]

        @pl.when(j == 0)
        def _():
            own_ref[...] = s

        @pl.when(j > 0)
        def _():
            send_ref[...] = s.astype(BF)

    grid_spec = pltpu.PrefetchScalarGridSpec(
        num_scalar_prefetch=1, grid=(R // tr, 4),
        in_specs=[pl.BlockSpec((None, tr, C), lambda i, j, t: (t[j], i, 0)),
                  pl.BlockSpec((None, tr, C), lambda i, j, t: (t[4 + j], i, 0))],
        out_specs=[pl.BlockSpec((tr, C), lambda i, j, t: (i, 0)),
                   pl.BlockSpec((None, tr, C), lambda i, j, t: (jnp.maximum(j - 1, 0), i, 0))])
    return pl.pallas_call(
        body, name=name, grid_spec=grid_spec,
        out_shape=(jax.ShapeDtypeStruct((R, C), F32), jax.ShapeDtypeStruct((3, R, C), BF)),
        compiler_params=_cp(("parallel", "arbitrary")))(idx, gblocks, recv)


def _mesh_pos():
    return lax.axis_index("x"), lax.axis_index("y"), lax.axis_index("c")


def _all_gather_weights(shards):
    n = len(shards)

    def body(*refs):
        ins, outs = refs[:n], refs[n:2 * n]
        send_sems, recv_sems, loc_sems = refs[2 * n:]
        x, y, c = _mesh_pos()
        me, sib = (x, y, c), (x, y, 1 - c)
        chips = [(1 - x, y), (x, 1 - y), (1 - x, 1 - y)]

        def rcopy(k, s, block, to, src=None):
            dst = outs[k].at[4 * block[0] + 2 * block[1] + block[2]]
            return pltpu.make_async_remote_copy(
                src_ref=dst if src is None else src, dst_ref=dst, send_sem=send_sems.at[k, s],
                recv_sem=recv_sems.at[k, s], device_id=to, device_id_type=MESH)

        locs = [pltpu.make_async_copy(ins[k], outs[k].at[4 * x + 2 * y + c], loc_sems.at[k]) for k in range(n)]
        for cp in locs:
            cp.start()
        started = []
        for k in range(n):
            for j, chip in enumerate(chips):
                started.append(rcopy(k, 1 + j, me, (*chip, c), src=ins[k]))
                started[-1].start()
        for k in range(n):
            started.append(rcopy(k, 0, me, sib, src=ins[k]))
            started[-1].start()
        for k in range(n):
            for j, chip in enumerate(chips):
                rcopy(k, 1 + j, (*chip, c), me).wait_recv()
                started.append(rcopy(k, 4 + j, (*chip, c), sib))
                started[-1].start()
        for k in range(n):
            rcopy(k, 0, sib, me).wait_recv()
            for j, chip in enumerate(chips):
                rcopy(k, 4 + j, (*chip, 1 - c), me).wait_recv()
        for cp in started:
            cp.wait_send()
        for cp in locs:
            cp.wait()

    hbm = pl.BlockSpec(memory_space=pl.ANY)
    return pl.pallas_call(
        body, name="ag_weights",
        out_shape=tuple(jax.ShapeDtypeStruct((8,) + s.shape, s.dtype) for s in shards),
        in_specs=[hbm] * n, out_specs=(hbm,) * n,
        scratch_shapes=[pltpu.SemaphoreType.DMA((n, 7)), pltpu.SemaphoreType.DMA((n, 7)),
                        pltpu.SemaphoreType.DMA((n,))])(*shards)


def _rs_d2d(gblocks, tie):
    n = len(gblocks)

    def body(*refs):
        ins, outs = refs[:n], refs[n + 1:2 * n + 1]
        send_sems, recv_sems = refs[2 * n + 1:]
        x, y, c = _mesh_pos()
        copies = []
        for k in range(n):
            for kk in range(4):
                copies.append(pltpu.make_async_remote_copy(
                    src_ref=ins[k].at[2 * kk + 1 - c], dst_ref=outs[k].at[kk], send_sem=send_sems.at[k, kk],
                    recv_sem=recv_sems.at[k, kk], device_id=(x, y, 1 - c), device_id_type=MESH))
                copies[-1].start()
        for cp in copies:
            cp.wait()

    hbm = pl.BlockSpec(memory_space=pl.ANY)
    return pl.pallas_call(
        body, name="rs_d2d",
        out_shape=tuple(jax.ShapeDtypeStruct((4,) + g.shape[1:], g.dtype) for g in gblocks),
        in_specs=[hbm] * (n + 1), out_specs=(hbm,) * n,
        scratch_shapes=[pltpu.SemaphoreType.DMA((n, 4)), pltpu.SemaphoreType.DMA((n, 4))])(*gblocks, tie)


def _rs_ici(sends):
    n = len(sends)

    def body(*refs):
        ins, outs = refs[:n], refs[n:2 * n]
        send_sems, recv_sems = refs[2 * n:]
        x, y, c = _mesh_pos()
        chips = [(1 - x, y), (x, 1 - y), (1 - x, 1 - y)]
        copies = []
        for k in range(n):
            for j, chip in enumerate(chips):
                copies.append(pltpu.make_async_remote_copy(
                    src_ref=ins[k].at[j], dst_ref=outs[k].at[j], send_sem=send_sems.at[k, j],
                    recv_sem=recv_sems.at[k, j], device_id=(*chip, c), device_id_type=MESH))
                copies[-1].start()
        for cp in copies:
            cp.wait()

    hbm = pl.BlockSpec(memory_space=pl.ANY)
    return pl.pallas_call(
        body, name="rs_ici",
        out_shape=tuple(jax.ShapeDtypeStruct(s.shape, s.dtype) for s in sends),
        in_specs=[hbm] * n, out_specs=(hbm,) * n,
        scratch_shapes=[pltpu.SemaphoreType.DMA((n, 3)), pltpu.SemaphoreType.DMA((n, 3))])(*sends)


_HBM = pl.BlockSpec(memory_space=pltpu.HBM)
_SEM = pl.BlockSpec(memory_space=pltpu.SEMAPHORE)
_EFFECT = pltpu.SideEffectType.DATAFLOW_SIDE_EFFECTING
_RELATIONS = [(dx, dy, dc) for dx in (0, 1) for dy in (0, 1) for dc in (0, 1)][1:]


def _flip(v, d):
    return 1 - v if d else v


def _plan_gather(n):
    def plan(x, y, c):
        return [(k, None, 4 * x + 2 * y + c, (_flip(x, dx), _flip(y, dy), _flip(c, dc)))
                for k in range(n) for dx, dy, dc in _RELATIONS]
    return plan


def _plan_d2d(n):
    def plan(x, y, c):
        return [(k, 2 * kk + 1 - c, kk, (x, y, 1 - c)) for k in range(n) for kk in range(4)]
    return plan


def _plan_ici(n):
    def plan(x, y, c):
        return [(k, j, j, (_flip(x, dx), _flip(y, dy), c))
                for k in range(n) for j, (dx, dy) in enumerate(((1, 0), (0, 1), (1, 1)))]
    return plan


def _plan_copies(plan, src_refs, land_refs, send_sems, recv_sems):
    x, y, c = _mesh_pos()
    return [pltpu.make_async_remote_copy(
        src_ref=src_refs[k] if si is None else src_refs[k].at[si], dst_ref=land_refs[k].at[di],
        send_sem=send_sems.at[n], recv_sem=recv_sems.at[n], device_id=dev, device_id_type=MESH)
        for n, (k, si, di, dev) in enumerate(plan(x, y, c))]


def _exchange_start(srcs, land_shapes, plan, ncopies, name, after):
    n = len(srcs)

    def body(*refs):
        src_refs, land_refs = refs[:n], refs[n:2 * n]
        send_sems, recv_sems = refs[2 * n + len(after)], refs[2 * n + len(after) + 1]
        token = refs[-1]
        for cp in _plan_copies(plan, src_refs, land_refs, send_sems, recv_sems):
            cp.start()
        token[...] = jnp.zeros_like(token)

    lands = [pltpu.with_memory_space_constraint(lax.empty(s, a.dtype), pltpu.HBM) for s, a in zip(land_shapes, srcs)]
    srcs = [pltpu.with_memory_space_constraint(a, pltpu.HBM) for a in srcs]
    outs = pl.pallas_call(
        body, name=name,
        out_shape=(pltpu.SemaphoreType.DMA((ncopies,)), pltpu.SemaphoreType.DMA((ncopies,)))
        + tuple(pltpu.HBM(a.shape, a.dtype) for a in srcs) + tuple(pltpu.HBM(a.shape, a.dtype) for a in lands)
        + (jax.ShapeDtypeStruct((8, 128), F32),),
        in_specs=[_HBM] * (2 * n) + [pl.BlockSpec(memory_space=pl.ANY)] * len(after),
        out_specs=(_SEM, _SEM) + (_HBM,) * (2 * n) + (pl.BlockSpec(memory_space=pltpu.VMEM),),
        input_output_aliases={i: 2 + i for i in range(2 * n)},
        compiler_params=pltpu.CompilerParams(has_side_effects=_EFFECT))(*srcs, *lands, *after)
    return (outs[0], outs[1], list(outs[2:2 + n]), list(outs[2 + n:2 + 2 * n])), outs[-1]


def _exchange_wait(handle, plan, name, after):
    send_sems, recv_sems, srcs, lands = handle
    n = len(srcs)

    def body(*refs):
        src_refs, land_refs = refs[:n], refs[n:2 * n]
        for cp in _plan_copies(plan, src_refs, land_refs, refs[2 * n], refs[2 * n + 1]):
            cp.wait_send()
            cp.wait_recv()

    outs = pl.pallas_call(
        body, name=name,
        out_shape=tuple(pltpu.HBM(a.shape, a.dtype) for a in srcs) + tuple(pltpu.HBM(a.shape, a.dtype) for a in lands),
        in_specs=[_HBM] * (2 * n) + [_SEM, _SEM] + [pl.BlockSpec(memory_space=pl.ANY)] * len(after),
        out_specs=(_HBM,) * (2 * n), input_output_aliases={i: i for i in range(2 * n)},
        compiler_params=pltpu.CompilerParams(has_side_effects=_EFFECT))(*srcs, *lands, send_sems, recv_sems, *after)
    return list(outs[:n]), list(outs[n:])


def _small_allreduce(pkt):
    R = pkt.shape[0]

    def body(p_ref, o_ref, buf, send_sems, recv_sems):
        x, y, c = _mesh_pos()
        me, sib = (x, y, c), (x, y, 1 - c)
        chips = [(1 - x, y), (x, 1 - y), (1 - x, 1 - y)]

        def rcopy(s, block, to, src=None):
            dst = buf.at[4 * block[0] + 2 * block[1] + block[2]]
            return pltpu.make_async_remote_copy(
                src_ref=dst if src is None else src, dst_ref=dst, send_sem=send_sems.at[s],
                recv_sem=recv_sems.at[s], device_id=to, device_id_type=MESH)

        buf[4 * x + 2 * y + c] = p_ref[...]
        started = [rcopy(0, me, sib, src=p_ref)]
        started += [rcopy(1 + j, me, (*chip, c), src=p_ref) for j, chip in enumerate(chips)]
        for cp in started:
            cp.start()
        for j, chip in enumerate(chips):
            rcopy(1 + j, (*chip, c), me).wait_recv()
            started.append(rcopy(4 + j, (*chip, c), sib))
            started[-1].start()
        rcopy(0, sib, me).wait_recv()
        for j, chip in enumerate(chips):
            rcopy(4 + j, (*chip, 1 - c), me).wait_recv()
        for cp in started:
            cp.wait_send()
        acc = buf[0]
        for b in range(1, 8):
            acc = acc + buf[b]
        o_ref[...] = acc

    return pl.pallas_call(
        body, name="small_allreduce", out_shape=jax.ShapeDtypeStruct(pkt.shape, F32),
        in_specs=[pl.BlockSpec(memory_space=pltpu.VMEM)], out_specs=pl.BlockSpec(memory_space=pltpu.VMEM),
        scratch_shapes=[pltpu.VMEM((8, R, 128), F32), pltpu.SemaphoreType.DMA((7,)), pltpu.SemaphoreType.DMA((7,))],
        compiler_params=pltpu.CompilerParams(vmem_limit_bytes=VMEM_LIMIT))(pkt)


def _rope_tables(S):
    half = HD // 2
    inv_freq = ROPE_THETA ** (-jnp.arange(half, dtype=F32) / half)
    ang = jnp.arange(S, dtype=F32)[:, None] * inv_freq[None, :]
    cos, sin = jnp.cos(ang), jnp.sin(ang)
    return jnp.concatenate([cos, cos, cos, cos], axis=1), jnp.concatenate([-sin, sin, -sin, sin], axis=1)


def _to_blocks(g, col_sharded):
    if col_sharded:
        return g.reshape(g.shape[0], 8, g.shape[1] // 8).transpose(1, 0, 2)
    return g.reshape(8, g.shape[0] // 8, g.shape[1])


def _from_blocks(t, col_sharded):
    if col_sharded:
        return t.transpose(1, 0, 2).reshape(t.shape[1], 8 * t.shape[2])
    return t.reshape(8 * t.shape[1], t.shape[2])


class _NoComm:
    def __init__(self, late_weights):
        self._late = late_weights
        self.grads = {}

    def start_token(self):
        return 0.0

    def late_weights(self, after):
        return self._late

    def rs_start(self, key, grads, col_sharded):
        self.grads[key] = grads
        return jnp.zeros((8, 128), F32)

    def rs_mid(self, key, after):
        return jnp.zeros((8, 128), F32)


class _FsdpComm:
    def __init__(self, late_shards, col_sharded, after, idx, block):
        self._col, self._idx, self._block, self._rs = col_sharded, idx, block, {}
        n = len(late_shards)
        self._gather, self._token = _exchange_start(
            late_shards, [(8,) + s.shape for s in late_shards], _plan_gather(n), 7 * n, "ag_late_start", (after,))

    def start_token(self):
        return self._token[0, 0]

    def late_weights(self, after):
        shards, lands = _exchange_wait(self._gather, _plan_gather(len(self._col)), "ag_late_wait", after)
        lands = [lax.dynamic_update_index_in_dim(t, s, self._block, 0) for t, s in zip(lands, shards)]
        return [_from_blocks(t, cs) for t, cs in zip(lands, self._col)]

    def rs_start(self, key, grads, col_sharded):
        gblocks = [_to_blocks(g, cs) for g, cs in zip(grads, col_sharded)]
        n = len(gblocks)
        d2d, token = _exchange_start(gblocks, [(4,) + g.shape[1:] for g in gblocks], _plan_d2d(n), 4 * n,
                                     "rs_%s_d2d_start" % key, (self._token,))
        self._rs[key] = dict(n=n, d2d=d2d)
        return token

    def rs_mid(self, key, after):
        st = self._rs[key]
        gblocks, from_sib = _exchange_wait(st["d2d"], _plan_d2d(st["n"]), "rs_%s_d2d_wait" % key, after)
        halves = [_rs_add(g, r, self._idx, "rs_add_%s_%d" % (key, k)) for k, (g, r) in enumerate(zip(gblocks, from_sib))]
        st["own"] = [own for own, _ in halves]
        sends = [send for _, send in halves]
        st["ici"], token = _exchange_start(sends, [t.shape for t in sends], _plan_ici(st["n"]), 3 * st["n"],
                                           "rs_%s_ici_start" % key, (self._token,))
        return token

    def rs_end(self, key, after):
        st = self._rs[key]
        return st["own"], _exchange_wait(st["ici"], _plan_ici(st["n"]), "rs_%s_ici_wait" % key, after)[1]


def _local_step(x, tgt, wqkv, wuz, wg, comm, g1, g2, g3, g4, ln_g, ln_b, w_sp, b_sp):
    S = x.shape[0]
    cos_t, sin_t = _rope_tables(S)
    b_sp_t = b_sp.T
    w_sp_t = w_sp.transpose(0, 2, 1)

    h1 = _rms_fwd(x, g1 + comm.start_token(), "rms_pre_mix")
    qkvs = [_proj_qkv(h1, wqkv, cos_t, sin_t, g, d, "proj_qkv_d%d" % d) for g, d in enumerate(DILS)]
    puz = _proj(h1, wuz, "none", "proj_uz")
    gates = _proj(h1, wg, "sigmoid", "proj_gates")
    fwd = [_attn_fwd(t, "attn_fwd_d%d" % d) for t, d in zip(qkvs, DILS)]
    ya, *lses = _attn_combine([o for o, _ in fwd], [l for _, l in fwd])
    yg = _gmlp_fwd(puz, ln_g, ln_b, w_sp, b_sp_t)
    wba, wbg, wout, wmi, wmo = comm.late_weights(after=(ya, yg, gates))
    abr, gbr, merged, y, x2, h3 = _merge_fwd(ya, yg, gates, wba, wbg, wout, x, g2, g3)
    a, dy2, dout, loss_row, dg4 = _mlp_fwd(h3, wmi, wmo, x2, tgt, g4)

    dpre, dx2, dy, dg3, dg2 = _mlp_bwd(dy2, a, wmo, wmi, x2, y, dout, g2, g3)
    dwmo = _mm_tn(a, dy2, "dw_mlp_out", square_a=True)
    dwmi = _mm_tn(h3, dpre, "dw_mlp_in")
    tie = comm.rs_start("mlp", [dwmi, dwmo], [True, False])
    dgates, da, db, dyg, *rest = _outproj_bwd(dy, wout, abr, gbr, gates, wba, wbg, ya, tie)
    dyas, dsums = rest[:3], rest[3:]
    tie = comm.rs_mid("mlp", after=(dyg,))
    dpuz, dwsp, dbs, dlng, dlnb = _gmlp_bwd(puz, dyg, ln_g + tie[0, 0], ln_b, w_sp, w_sp_t, b_sp_t)
    dqkvs = [_attn_bwd(qkvs[g], dyas[g], dsums[g], lses[g], "attn_bwd_d%d" % d) for g, d in enumerate(DILS)]
    grad_x, dg1, dqkv = _inproj_bwd(dqkvs, dpuz, dgates, wqkv, wuz, wg, cos_t, sin_t, x, dx2, g1)
    dwin = jnp.concatenate([_mm_tn(h1, t, "dw_in_%d" % n) for n, t in enumerate((dqkv, dpuz, dgates))], axis=1)
    tie = comm.rs_start("win", [dwin], [True])
    dwout = _mm_tn(merged, dy, "dw_out", tie=tie)
    dwba = _mm_tn(ya, da, "dw_branch_attn", tie=tie)
    dwbg = _mm_tn(yg, db, "dw_branch_gmlp", tie=tie)
    tie = comm.rs_mid("win", after=(dwout, dwba, dwbg))
    small = dict(g1=dg1, g2=dg2, g3=dg3, g4=dg4, ln_g=dlng, ln_b=dlnb, b_sp=dbs.reshape(4, CHUNK), w_sp=dwsp)
    return loss_row, grad_x, (dwba, dwbg, dwout), small, tie


def _pack_small(first_row, t):
    pad = lambda a, rows: jnp.concatenate([a, jnp.zeros((rows - a.shape[0], 128), F32)], axis=0)
    return jnp.concatenate([
        pad(first_row, 8), t["g1"].reshape(8, 128), t["g2"].reshape(8, 128), t["g3"].reshape(8, 128),
        t["g4"].reshape(8, 128), t["ln_g"].reshape(4, 128), t["ln_b"].reshape(4, 128),
        pad(t["b_sp"].reshape(4, 128), 8), t["w_sp"].reshape(4 * CHUNK, 128)], axis=0)


def _unpack_small(p):
    vec = lambda r, n: p[r:r + n // 128].reshape(1, n)
    return dict(g1=vec(PK_G1, D), g2=vec(PK_G2, D), g3=vec(PK_G3, D), g4=vec(PK_G4, D),
                ln_g=vec(PK_LNG, GMLP_W), ln_b=vec(PK_LNB, GMLP_W),
                b_sp=p[PK_BSP:PK_BSP + 4].reshape(1, 4, CHUNK), w_sp=p[PK_WSP:PKT_ROWS].reshape(1, 4, CHUNK, CHUNK))


def kernel(x, norm_pre_mix, w_in, w_spatial, b_spatial, ln_v_gain, ln_v_bias, w_branch_attn, w_branch_gmlp, w_out, norm_post_mix, norm_pre_mlp, w_mlp_in, w_mlp_out, norm_post_mlp, loss_target, m_norm_pre_mix, m_w_in, m_w_spatial, m_b_spatial, m_ln_v_gain, m_ln_v_bias, m_w_branch_attn, m_w_branch_gmlp, m_w_out, m_norm_post_mix, m_norm_pre_mlp, m_w_mlp_in, m_w_mlp_out, m_norm_post_mlp, v_norm_pre_mix, v_w_in, v_w_spatial, v_b_spatial, v_ln_v_gain, v_ln_v_bias, v_w_branch_attn, v_w_branch_gmlp, v_w_out, v_norm_post_mix, v_norm_pre_mlp, v_w_mlp_in, v_w_mlp_out, v_norm_post_mlp):
    mx, my, mc = _mesh_pos()
    rel = [(0, 0), (1, 0), (0, 1), (1, 1)]
    chip_of = [2 * (mx ^ dx) + (my ^ dy) for dx, dy in rel]
    idx = jnp.stack([2 * k + mc for k in chip_of] + chip_of).astype(jnp.int32)

    (g_win,) = _all_gather_weights([w_in[0].astype(BF)])
    win = _from_blocks(g_win, True)
    wqkv, wuz, wg = win[:, :3 * AW], win[:, 3 * AW:3 * AW + 2 * GMLP_W], win[:, 3 * AW + 2 * GMLP_W:]
    late = [w_branch_attn[0], w_branch_gmlp[0], w_out[0], w_mlp_in[0], w_mlp_out[0]]
    comm = _FsdpComm([w.astype(BF) for w in late], [True, True, False, True, False], g_win, idx, 4 * mx + 2 * my + mc)

    loss_row, grad_x, dws, small, tie = _local_step(
        x[0], loss_target[0], wqkv, wuz, wg, comm,
        norm_pre_mix, norm_post_mix, norm_pre_mlp, norm_post_mlp, ln_v_gain, ln_v_bias, w_spatial[0], b_spatial[0])

    gblocks = [_to_blocks(g, cs) for g, cs in zip(dws, [True, True, False])]
    from_sib = _rs_d2d(gblocks, tie)
    halves = [_rs_add(g, r, idx, "rs_add_%d" % n) for n, (g, r) in enumerate(zip(gblocks, from_sib))]
    from_chips = _rs_ici([s for _, s in halves])
    mlp_own, mlp_chips = comm.rs_end("mlp", after=(from_chips[0],))
    adam = lambda nm, own, r, w, m, v: _adam_shard(own, r, w[0], m[0], v[0], "adam_" + nm)
    upd = {
        "w_branch_attn": adam("w_branch_attn", halves[0][0], from_chips[0], w_branch_attn, m_w_branch_attn, v_w_branch_attn),
        "w_branch_gmlp": adam("w_branch_gmlp", halves[1][0], from_chips[1], w_branch_gmlp, m_w_branch_gmlp, v_w_branch_gmlp),
        "w_out": adam("w_out", halves[2][0], from_chips[2], w_out, m_w_out, v_w_out),
        "w_mlp_in": adam("w_mlp_in", mlp_own[0], mlp_chips[0], w_mlp_in, m_w_mlp_in, v_w_mlp_in),
        "w_mlp_out": adam("w_mlp_out", mlp_own[1], mlp_chips[1], w_mlp_out, m_w_mlp_out, v_w_mlp_out),
    }

    pkt = _small_allreduce(_pack_small(loss_row, small))
    small_w = dict(g1=norm_pre_mix, g2=norm_post_mix, g3=norm_pre_mlp, g4=norm_post_mlp, ln_g=ln_v_gain,
                   ln_b=ln_v_bias, b_sp=b_spatial[0], w_sp=w_spatial[0])
    small_m = dict(g1=m_norm_pre_mix, g2=m_norm_post_mix, g3=m_norm_pre_mlp, g4=m_norm_post_mlp, ln_g=m_ln_v_gain,
                   ln_b=m_ln_v_bias, b_sp=m_b_spatial[0], w_sp=m_w_spatial[0])
    small_v = dict(g1=v_norm_pre_mix, g2=v_norm_post_mix, g3=v_norm_pre_mlp, g4=v_norm_post_mlp, ln_g=v_ln_v_gain,
                   ln_b=v_ln_v_bias, b_sp=v_b_spatial[0], w_sp=v_w_spatial[0])
    zero_row = jnp.zeros((1, 128), F32)
    ones_row = jnp.ones((1, 128), F32)
    sd, sm, sv = _adam_small(pkt, _pack_small(zero_row, small_w), _pack_small(zero_row, small_m),
                             _pack_small(ones_row, small_v))
    loss = pkt[0, 0]
    win_own, win_chips = comm.rs_end("win", after=(sd, upd["w_mlp_out"][0]))
    upd["w_in"] = adam("w_in", win_own[0], win_chips[0], w_in, m_w_in, v_w_in)
    sg, sd, sm, sv = (_unpack_small(p) for p in (pkt, sd, sm, sv))

    order = ["g1", "w_in", "w_sp", "b_sp", "ln_g", "ln_b", "w_branch_attn", "w_branch_gmlp", "w_out", "g2", "g3",
             "w_mlp_in", "w_mlp_out", "g4"]

    def pick(which):
        out = []
        for nm in order:
            if nm in upd:
                out.append(upd[nm][which][None])
            else:
                out.append((sg, sd, sm, sv)[which][nm])
        return out

    return (loss, grad_x[None], *pick(0), *pick(1), *pick(2), *pick(3))
```

```python
import functools
import math

import jax
import jax.numpy as jnp
from jax import lax
from jax.experimental import pallas as pl
from jax.experimental.pallas import tpu as pltpu

D = 1024
HD = 64
NSLOT = 4
GW = NSLOT * HD
DILS = (1, 4, 16)
QB = 128
ATTN_NSUB = 2
MLP_TM_FWD = 512
MLP_TM_BWD = 256
AW = 3 * GW
GMLP_W = 512
CHUNK = 128
DFF = 4096
EPS = 1e-6
ROPE_THETA = 10000.0
SCALE = HD ** -0.5
NEG = -1e30

ADAM_LR = 0.001
ADAM_B1 = 0.9
ADAM_B2 = 0.999
ADAM_EPS = 1e-08
ADAM_WD = 0.01
ADAM_STEP = 10

BF = jnp.bfloat16
F32 = jnp.float32
MESH = pl.DeviceIdType.MESH
VMEM_LIMIT = 56 * 1024 * 1024

PK_LOSS, PK_G1, PK_G2, PK_G3, PK_G4, PK_LNG, PK_LNB, PK_BSP, PK_WSP, PKT_ROWS = 0, 8, 16, 24, 32, 40, 44, 48, 56, 568


def _cp(sem):
    return pltpu.CompilerParams(dimension_semantics=sem, vmem_limit_bytes=VMEM_LIMIT)


def _dot(a, b):
    return jnp.dot(a, b, preferred_element_type=F32)


def _dot_nt(a, b):
    return lax.dot_general(a, b, (((1,), (1,)), ((), ())), preferred_element_type=F32)


def _dot_tn(a, b):
    return lax.dot_general(a, b, (((0,), (0,)), ((), ())), preferred_element_type=F32)


def _gelu(x):
    return jax.nn.gelu(x, approximate=True)


def _gelu_grad(x):
    k = math.sqrt(2.0 / math.pi)
    t = jnp.tanh(k * (x + 0.044715 * x * x * x))
    return 0.5 * (1.0 + t) + 0.5 * x * (1.0 - t * t) * (k * (1.0 + 3.0 * 0.044715 * x * x))


def _swap_halves(t):
    w = t.shape[1]
    lane = lax.broadcasted_iota(jnp.int32, t.shape, 1)
    first = (lane & (HD - 1)) < (HD // 2)
    return jnp.where(first, pltpu.roll(t, w - HD // 2, 1), pltpu.roll(t, HD // 2, 1))


def _head_mask(hh):
    lane = lax.broadcasted_iota(jnp.int32, (1, GW), 1)
    return jnp.logical_and(lane >= hh * HD, lane < (hh + 1) * HD)


def _rms_stats(xf):
    return lax.rsqrt(jnp.mean(xf * xf, axis=-1, keepdims=True) + EPS)


def _rms_bwd(xf, r, gain, dout):
    n = xf * r
    t = dout * gain
    dx = r * (t - n * jnp.mean(t * n, axis=-1, keepdims=True))
    return dx, jnp.sum(dout * n, axis=0, keepdims=True)


def _rms_fwd(x, gain, name):
    S = x.shape[0]
    tm = min(512, S)

    def body(x_ref, g_ref, h_ref):
        xf = x_ref[...]
        h_ref[...] = (xf * _rms_stats(xf) * g_ref[...]).astype(BF)

    return pl.pallas_call(
        body, name=name, out_shape=jax.ShapeDtypeStruct((S, D), BF), grid=(S // tm,),
        in_specs=[pl.BlockSpec((tm, D), lambda i: (i, 0)), pl.BlockSpec((1, D), lambda i: (0, 0))],
        out_specs=pl.BlockSpec((tm, D), lambda i: (i, 0)), compiler_params=_cp(("parallel",)))(x, gain)


def _proj(h, w, epi, name, tn=256):
    S, K = h.shape
    N = w.shape[1]
    tm = min(2048, S)

    def body(h_ref, w_ref, o_ref):
        y = _dot(h_ref[...], w_ref[...])
        if epi == "sigmoid":
            y = jax.nn.sigmoid(y)
        o_ref[...] = y.astype(BF)

    return pl.pallas_call(
        body, name=name, out_shape=jax.ShapeDtypeStruct((S, N), BF), grid=(S // tm, N // tn),
        in_specs=[pl.BlockSpec((tm, K), lambda i, j: (i, 0)), pl.BlockSpec((K, tn), lambda i, j: (0, j))],
        out_specs=pl.BlockSpec((tm, tn), lambda i, j: (i, j)),
        compiler_params=_cp(("parallel", "arbitrary")))(h, w)


def _dilate_store(val, scr, o_ref, lead, d):
    rows = val.shape[0]
    if d == 1:
        o_ref[lead + (0,)] = val.astype(o_ref.dtype)
        return
    for hf in range(2):
        scr[hf, pl.ds(0, rows), :] = val[:, hf * 128:(hf + 1) * 128]
    for r in range(d):
        for hf in range(2):
            o_ref[lead + (r, slice(None), slice(hf * 128, (hf + 1) * 128))] = (
                scr[hf, pl.ds(r, rows // d, stride=d), :].astype(o_ref.dtype))


def _undilate_load(i_ref, lead, d, scr, rows):
    if d == 1:
        return i_ref[lead + (0,)].astype(F32)
    for r in range(d):
        for hf in range(2):
            scr[hf, pl.ds(r, rows // d, stride=d), :] = (
                i_ref[lead + (r, slice(None), slice(hf * 128, (hf + 1) * 128))].astype(F32))
    return jnp.concatenate([scr[0, pl.ds(0, rows), :], scr[1, pl.ds(0, rows), :]], axis=1)


def _rope_fwd(y, c_ref, s_ref):
    cosv = jnp.concatenate([c_ref[...]] * 2, axis=1)
    sinv = jnp.concatenate([s_ref[...]] * 2, axis=1)
    return y * cosv + _swap_halves(y) * sinv


def _rope_bwd(dy, c_ref, s_ref):
    cosv = jnp.concatenate([c_ref[...]] * 2, axis=1)
    sinv = jnp.concatenate([s_ref[...]] * 2, axis=1)
    return dy * cosv + _swap_halves(dy * sinv)


def _proj_qkv(h, wqkv, cos_t, sin_t, g, d, name):
    S, K = h.shape
    tm = min(2048, S)

    def body(h_ref, w_ref, c_ref, s_ref, o_ref, scr):
        j = pl.program_id(1)
        y = _dot(h_ref[...], w_ref[...])
        y = jnp.where(j < 2, _rope_fwd(y, c_ref, s_ref), y)
        _dilate_store(y, scr, o_ref, (), d)

    return pl.pallas_call(
        body, name=name, out_shape=jax.ShapeDtypeStruct((3, d, S // d, GW), BF), grid=(S // tm, 3),
        in_specs=[pl.BlockSpec((tm, K), lambda i, j: (i, 0)), pl.BlockSpec((K, GW), lambda i, j: (0, 3 * j + g)),
                  pl.BlockSpec((tm, 128), lambda i, j: (i, 0)), pl.BlockSpec((tm, 128), lambda i, j: (i, 0))],
        out_specs=pl.BlockSpec((None, d, tm // d, GW), lambda i, j: (j, 0, i, 0)),
        scratch_shapes=[pltpu.VMEM((2, tm, 128), F32)],
        compiler_params=_cp(("parallel", "arbitrary")))(h, wqkv, cos_t, sin_t)


def _band_masks(first_step):
    row = lax.broadcasted_iota(jnp.int32, (QB, 2 * QB), 0)
    col = lax.broadcasted_iota(jnp.int32, (QB, 2 * QB), 1)
    band = jnp.logical_and(col >= row, col <= row + QB)
    return band, jnp.logical_and(band, jnp.logical_or(col >= QB, jnp.logical_not(first_step)))


def _attn_fwd(qkv, name):
    _, d, L, _ = qkv.shape
    nsub = min(ATTN_NSUB, L // QB)
    R = nsub * QB
    nsteps = L // R

    def body(q_ref, kp_ref, kc_ref, vp_ref, vc_ref, o_ref, lse_ref):
        i = pl.program_id(1)
        band, band_first = _band_masks(i == 0)
        kfull = jnp.concatenate([kp_ref[...], kc_ref[...]], axis=0)
        vfull = jnp.concatenate([vp_ref[...], vc_ref[...]], axis=0)
        chains = [(sb, hh) for sb in range(nsub) for hh in range(NSLOT)]
        win = lambda t, sb: t[sb * QB:(sb + 2) * QB]
        scores = []
        for sb, hh in chains:
            qh = jnp.where(_head_mask(hh), q_ref[sb * QB:(sb + 1) * QB, :], 0)
            scores.append(_dot_nt(qh, win(kfull, sb)))
        soft = []
        for (sb, hh), sc in zip(chains, scores):
            sc = jnp.where(band_first if sb == 0 else band, sc * SCALE, NEG)
            m = jnp.max(sc, axis=1, keepdims=True)
            p = jnp.exp(sc - m)
            den = jnp.sum(p, axis=1, keepdims=True)
            soft.append((p.astype(BF), den, m + jnp.log(den)))
        accs = [_dot(p, win(vfull, sb)) for (sb, hh), (p, _, _) in zip(chains, soft)]
        for sb in range(nsub):
            o = jnp.zeros((QB, GW), F32)
            lse = jnp.zeros((QB, GW), F32)
            for hh in range(NSLOT):
                hm = _head_mask(hh)
                _, den, lrow = soft[sb * NSLOT + hh]
                o = o + jnp.where(hm, accs[sb * NSLOT + hh] / den, 0.0)
                lse = lse + jnp.where(hm, lrow, 0.0)
            o_ref[sb * QB:(sb + 1) * QB, :] = o
            lse_ref[sb * QB:(sb + 1) * QB, :] = lse

    prev = lambda i: jnp.maximum(i * nsub - 1, 0)
    cur = lambda t: pl.BlockSpec((None, None, R, GW), lambda r, i: (t, r, i, 0))
    prv = lambda t: pl.BlockSpec((None, None, QB, GW), lambda r, i: (t, r, prev(i), 0))
    out = pl.BlockSpec((None, R, GW), lambda r, i: (r, i, 0))
    return pl.pallas_call(
        body, name=name, grid=(d, nsteps),
        out_shape=(jax.ShapeDtypeStruct((d, L, GW), F32), jax.ShapeDtypeStruct((d, L, GW), F32)),
        in_specs=[cur(0), prv(1), cur(1), prv(2), cur(2)],
        out_specs=(out, out), compiler_params=_cp(("parallel", "arbitrary")))(qkv, qkv, qkv, qkv, qkv)


def _attn_combine(os_, lses):
    S = os_[0].shape[1]
    tm = min(1024, S)

    def body(o0, o1, o2, l0, l1, l2, y_ref, j0, j1, j2, scr):
        os_nat = [_undilate_load(o, (), d, scr, tm) for o, d in zip((o0, o1, o2), DILS)]
        a, b, c = [_undilate_load(l, (), d, scr, tm) for l, d in zip((l0, l1, l2), DILS)]
        m = jnp.maximum(jnp.maximum(a, b), c)
        wa, wb, wc = jnp.exp(a - m), jnp.exp(b - m), jnp.exp(c - m)
        den = wa + wb + wc
        y_ref[...] = ((wa * os_nat[0] + wb * os_nat[1] + wc * os_nat[2]) / den).astype(BF)
        lse = m + jnp.log(den)
        for j_ref, d in zip((j0, j1, j2), DILS):
            _dilate_store(lse, scr, j_ref, (), d)

    dil = lambda d: pl.BlockSpec((d, tm // d, GW), lambda i: (0, i, 0))
    dshape = lambda d: jax.ShapeDtypeStruct((d, S // d, GW), F32)
    return pl.pallas_call(
        body, name="attn_combine", grid=(S // tm,),
        out_shape=(jax.ShapeDtypeStruct((S, GW), BF),) + tuple(dshape(d) for d in DILS),
        in_specs=[dil(d) for d in DILS] * 2,
        out_specs=(pl.BlockSpec((tm, GW), lambda i: (i, 0)),) + tuple(dil(d) for d in DILS),
        scratch_shapes=[pltpu.VMEM((2, tm, 128), F32)],
        compiler_params=_cp(("parallel",)))(*os_, *lses)


def _tril(upper=False):
    row = lax.broadcasted_iota(jnp.int32, (CHUNK, CHUNK), 0)
    col = lax.broadcasted_iota(jnp.int32, (CHUNK, CHUNK), 1)
    return row <= col if upper else col <= row


def _ln_fwd(z, gain, bias):
    mu = jnp.mean(z, axis=-1, keepdims=True)
    zc = z - mu
    rstd = lax.rsqrt(jnp.mean(zc * zc, axis=-1, keepdims=True) + EPS)
    zhat = zc * rstd
    return zhat, rstd, zhat * gain + bias


def _gmlp_fwd(puz, ln_g, ln_b, w_sp, b_sp_t):
    S = puz.shape[0]
    tm = min(512, S)
    nch = tm // CHUNK

    def body(p_ref, g_ref, b_ref, w_ref, bt_ref, o_ref):
        tril = _tril()
        ws = [jnp.where(tril, w_ref[gg], 0.0).astype(BF) for gg in range(4)]
        for ch in range(nch):
            rows = slice(ch * CHUNK, (ch + 1) * CHUNK)
            z = _gelu(p_ref[rows, GMLP_W:].astype(F32))
            _, _, zn = _ln_fwd(z, g_ref[...], b_ref[...])
            zn = zn.astype(BF)
            for gg in range(4):
                cols = slice(gg * CHUNK, (gg + 1) * CHUNK)
                sz = _dot(ws[gg], zn[:, cols]) + bt_ref[:, gg:gg + 1]
                u = _gelu(p_ref[rows, cols].astype(F32))
                o_ref[rows, cols] = (u * sz).astype(BF)

    return pl.pallas_call(
        body, name="gmlp_fwd", out_shape=jax.ShapeDtypeStruct((S, GMLP_W), BF), grid=(S // tm,),
        in_specs=[pl.BlockSpec((tm, 2 * GMLP_W), lambda i: (i, 0)),
                  pl.BlockSpec((1, GMLP_W), lambda i: (0, 0)), pl.BlockSpec((1, GMLP_W), lambda i: (0, 0)),
                  pl.BlockSpec((4, CHUNK, CHUNK), lambda i: (0, 0, 0)), pl.BlockSpec((CHUNK, 4), lambda i: (0, 0))],
        out_specs=pl.BlockSpec((tm, GMLP_W), lambda i: (i, 0)),
        compiler_params=_cp(("parallel",)))(puz, ln_g, ln_b, w_sp, b_sp_t)


def _merge_fwd(ya, yg, gates, wba, wbg, wout, x, g2, g3):
    S = x.shape[0]
    tm = min(512, S)

    def body(ya_ref, yg_ref, gt_ref, wba_ref, wbg_ref, wo_ref, x_ref, g2_ref, g3_ref,
             a_ref, gb_ref, mg_ref, y_ref, x2_ref, h3_ref):
        a = _dot(ya_ref[...], wba_ref[...])
        b = _dot(yg_ref[...], wbg_ref[...])
        a_ref[...] = a.astype(BF)
        gb_ref[...] = b.astype(BF)
        merged = (gt_ref[:, :D].astype(F32) * a + gt_ref[:, D:].astype(F32) * b).astype(BF)
        mg_ref[...] = merged
        y = _dot(merged, wo_ref[...])
        y_ref[...] = y
        x2 = x_ref[...] + y * _rms_stats(y) * g2_ref[...]
        x2_ref[...] = x2
        h3_ref[...] = (x2 * _rms_stats(x2) * g3_ref[...]).astype(BF)

    row = lambda w: pl.BlockSpec((tm, w), lambda i: (i, 0))
    full = lambda s: pl.BlockSpec(s, lambda i: (0, 0))
    return pl.pallas_call(
        body, name="merge_fwd", grid=(S // tm,),
        out_shape=(jax.ShapeDtypeStruct((S, D), BF), jax.ShapeDtypeStruct((S, D), BF), jax.ShapeDtypeStruct((S, D), BF),
                   jax.ShapeDtypeStruct((S, D), F32), jax.ShapeDtypeStruct((S, D), F32), jax.ShapeDtypeStruct((S, D), BF)),
        in_specs=[row(GW), row(GMLP_W), row(2 * D), full((GW, D)), full((GMLP_W, D)), full((D, D)), row(D),
                  full((1, D)), full((1, D))],
        out_specs=(row(D), row(D), row(D), row(D), row(D), row(D)),
        compiler_params=_cp(("parallel",)))(ya, yg, gates, wba, wbg, wout, x, g2, g3)


def _resident(shape):
    return pl.BlockSpec(shape, lambda i: (0,) * len(shape), pipeline_mode=pl.Buffered(1))


def _mlp_fwd(h3, wmi, wmo, x2, tgt, g4):
    S = x2.shape[0]
    tm = min(MLP_TM_FWD, S)

    def body(h_ref, wi_ref, wo_ref, x2_ref, t_ref, g4_ref, a_ref, dy2_ref, dout_ref, loss_ref, dg4_ref):
        @pl.when(pl.program_id(0) == 0)
        def _():
            loss_ref[...] = jnp.zeros_like(loss_ref)
            dg4_ref[...] = jnp.zeros_like(dg4_ref)

        a = jnp.maximum(_dot(h_ref[...], wi_ref[...]), 0.0)
        a_ref[...] = a.astype(BF)
        y2 = _dot((a * a).astype(BF), wo_ref[...])
        r = _rms_stats(y2)
        out = x2_ref[...] + y2 * r * g4_ref[...]
        err = out - t_ref[...]
        tot = jnp.sum(jnp.sum(err * err, axis=1, keepdims=True), axis=0, keepdims=True) * (0.5 / D)
        lane = lax.broadcasted_iota(jnp.int32, (1, 128), 1)
        loss_ref[...] += jnp.where(lane == 0, tot, 0.0)
        dout = err * (1.0 / D)
        dout_ref[...] = dout
        dy2, dg = _rms_bwd(y2, r, g4_ref[...], dout)
        dy2_ref[...] = dy2.astype(BF)
        dg4_ref[...] += dg

    row = pl.BlockSpec((tm, D), lambda i: (i, 0))
    return pl.pallas_call(
        body, name="mlp_fwd", grid=(S // tm,),
        out_shape=(jax.ShapeDtypeStruct((S, DFF), BF), jax.ShapeDtypeStruct((S, D), BF), jax.ShapeDtypeStruct((S, D), F32),
                   jax.ShapeDtypeStruct((1, 128), F32), jax.ShapeDtypeStruct((1, D), F32)),
        in_specs=[row, _resident((D, DFF)), _resident((DFF, D)), row, row, pl.BlockSpec((1, D), lambda i: (0, 0))],
        out_specs=(pl.BlockSpec((tm, DFF), lambda i: (i, 0)), row, row,
                   pl.BlockSpec((1, 128), lambda i: (0, 0)), pl.BlockSpec((1, D), lambda i: (0, 0))),
        compiler_params=_cp(("arbitrary",)))(h3, wmi, wmo, x2, tgt, g4)


def _mlp_bwd(dy2, a, wmo, wmi, x2, y, dout, g2, g3):
    S = x2.shape[0]
    tm = min(MLP_TM_BWD, S)

    def body(dy2_ref, a_ref, wo_ref, wi_ref, x2_ref, y_ref, dout_ref, g2_ref, g3_ref,
             dpre_ref, dx2_ref, dy_ref, dg3_ref, dg2_ref):
        @pl.when(pl.program_id(0) == 0)
        def _():
            dg3_ref[...] = jnp.zeros_like(dg3_ref)
            dg2_ref[...] = jnp.zeros_like(dg2_ref)

        da2 = _dot_nt(dy2_ref[...], wo_ref[...])
        dpre = (2.0 * a_ref[...].astype(F32) * da2).astype(BF)
        dpre_ref[...] = dpre
        dh3 = _dot_nt(dpre, wi_ref[...])
        x2 = x2_ref[...]
        dx3, dg3 = _rms_bwd(x2, _rms_stats(x2), g3_ref[...], dh3)
        dx2 = dout_ref[...] + dx3
        dx2_ref[...] = dx2
        dg3_ref[...] += dg3
        yv = y_ref[...]
        dy, dg2 = _rms_bwd(yv, _rms_stats(yv), g2_ref[...], dx2)
        dy_ref[...] = dy.astype(BF)
        dg2_ref[...] += dg2

    row = pl.BlockSpec((tm, D), lambda i: (i, 0))
    wide = pl.BlockSpec((tm, DFF), lambda i: (i, 0))
    vec = pl.BlockSpec((1, D), lambda i: (0, 0))
    return pl.pallas_call(
        body, name="mlp_bwd", grid=(S // tm,),
        out_shape=(jax.ShapeDtypeStruct((S, DFF), BF), jax.ShapeDtypeStruct((S, D), F32), jax.ShapeDtypeStruct((S, D), BF),
                   jax.ShapeDtypeStruct((1, D), F32), jax.ShapeDtypeStruct((1, D), F32)),
        in_specs=[row, wide, _resident((DFF, D)), _resident((D, DFF)), row, row, row, vec, vec],
        out_specs=(wide, row, row, vec, vec),
        compiler_params=_cp(("arbitrary",)))(dy2, a, wmo, wmi, x2, y, dout, g2, g3)


def _mm_tn(a, b, name, square_a=False, tm=1024, tn=1024, tk=2048, tie=None):
    S, M = a.shape
    N = b.shape[1]
    tm, tk = min(tm, M), min(tk, S)
    tn = max(t for t in range(128, min(tn, N) + 1, 128) if N % t == 0)
    assert M % tm == 0 and S % tk == 0
    nk = S // tk
    ties = () if tie is None else (tie,)

    def body(a_ref, b_ref, *rest):
        o_ref = rest[-1]
        k = pl.program_id(2)
        av = a_ref[...]
        if square_a:
            av = av * av
        part = _dot_tn(av, b_ref[...])

        @pl.when(k == 0)
        def _():
            o_ref[...] = part

        @pl.when(k > 0)
        def _():
            o_ref[...] += part

    return pl.pallas_call(
        body, name=name, out_shape=jax.ShapeDtypeStruct((M, N), F32), grid=(M // tm, N // tn, nk),
        in_specs=[pl.BlockSpec((tk, tm), lambda i, j, k: (k, i)), pl.BlockSpec((tk, tn), lambda i, j, k: (k, j))]
        + [pl.BlockSpec(memory_space=pl.ANY)] * len(ties),
        out_specs=pl.BlockSpec((tm, tn), lambda i, j, k: (i, j)),
        compiler_params=_cp(("parallel", "parallel", "arbitrary")))(a, b, *ties)


def _outproj_bwd(dy, wout, abr, gbr, gates, wba, wbg, ya, tie):
    S = dy.shape[0]
    tm = min(512, S)

    def body(dy_ref, wo_ref, a_ref, b_ref, gt_ref, wba_ref, wbg_ref, ya_ref, tie_ref,
             dgt_ref, da_ref, db_ref, dyg_ref, e0, e1, e2, s0, s1, s2, scr):
        dm = _dot_nt(dy_ref[...], wo_ref[...])
        ga, gb = gt_ref[:, :D].astype(F32), gt_ref[:, D:].astype(F32)
        dgt_ref[:, :D] = (dm * a_ref[...].astype(F32) * ga * (1.0 - ga)).astype(BF)
        dgt_ref[:, D:] = (dm * b_ref[...].astype(F32) * gb * (1.0 - gb)).astype(BF)
        da = (dm * ga).astype(BF)
        db = (dm * gb).astype(BF)
        da_ref[...] = da
        db_ref[...] = db
        dyg_ref[...] = _dot_nt(db, wbg_ref[...]).astype(BF)
        dya = _dot_nt(da, wba_ref[...]).astype(BF).astype(F32)
        dyy = dya * ya_ref[...].astype(F32)
        dsum = jnp.zeros((tm, GW), F32)
        for hh in range(NSLOT):
            hm = _head_mask(hh)
            dsum = dsum + jnp.where(hm, jnp.sum(jnp.where(hm, dyy, 0.0), axis=1, keepdims=True), 0.0)
        for e_ref, s_ref, d in zip((e0, e1, e2), (s0, s1, s2), DILS):
            _dilate_store(dya, scr, e_ref, (), d)
            _dilate_store(dsum, scr, s_ref, (), d)

    row = lambda w: pl.BlockSpec((tm, w), lambda i: (i, 0))
    full = lambda s: pl.BlockSpec(s, lambda i: (0, 0))
    dil = lambda d: pl.BlockSpec((d, tm // d, GW), lambda i: (0, i, 0))
    dshape = lambda d, t: jax.ShapeDtypeStruct((d, S // d, GW), t)
    return pl.pallas_call(
        body, name="outproj_bwd", grid=(S // tm,),
        out_shape=(jax.ShapeDtypeStruct((S, 2 * D), BF), jax.ShapeDtypeStruct((S, D), BF), jax.ShapeDtypeStruct((S, D), BF),
                   jax.ShapeDtypeStruct((S, GMLP_W), BF)) + tuple(dshape(d, BF) for d in DILS)
        + tuple(dshape(d, F32) for d in DILS),
        in_specs=[row(D), full((D, D)), row(D), row(D), row(2 * D), full((GW, D)), full((GMLP_W, D)), row(GW),
                  pl.BlockSpec(memory_space=pl.ANY)],
        out_specs=(row(2 * D), row(D), row(D), row(GMLP_W)) + tuple(dil(d) for d in DILS) * 2,
        scratch_shapes=[pltpu.VMEM((2, tm, 128), F32)],
        compiler_params=_cp(("parallel",)))(dy, wout, abr, gbr, gates, wba, wbg, ya, tie)


def _gmlp_bwd(puz, dyg, ln_g, ln_b, w_sp, w_sp_t, b_sp_t):
    S = puz.shape[0]
    tm = min(512, S)
    nch = tm // CHUNK

    def body(p_ref, dy_ref, g_ref, b_ref, w_ref, wt_ref, bt_ref,
             dp_ref, dw_ref, dbs_ref, dg_ref, dbias_ref, dbacc_ref):
        i = pl.program_id(0)

        @pl.when(i == 0)
        def _():
            dw_ref[...] = jnp.zeros_like(dw_ref)
            dbacc_ref[...] = jnp.zeros_like(dbacc_ref)
            dg_ref[...] = jnp.zeros_like(dg_ref)
            dbias_ref[...] = jnp.zeros_like(dbias_ref)

        tril = _tril()
        ws = [jnp.where(tril, w_ref[gg], 0.0).astype(BF) for gg in range(4)]
        triu = _tril(upper=True)
        wts = [jnp.where(triu, wt_ref[gg], 0.0).astype(BF) for gg in range(4)]
        gain = g_ref[...]
        for ch in range(nch):
            rows = slice(ch * CHUNK, (ch + 1) * CHUNK)
            pz = p_ref[rows, GMLP_W:].astype(F32)
            z = _gelu(pz)
            zhat, rstd, zn = _ln_fwd(z, gain, b_ref[...])
            znb = zn.astype(BF)
            dzn_parts = []
            for gg in range(4):
                cols = slice(gg * CHUNK, (gg + 1) * CHUNK)
                pu = p_ref[rows, cols].astype(F32)
                u = _gelu(pu)
                sz = _dot(ws[gg], znb[:, cols]) + bt_ref[:, gg:gg + 1]
                dyv = dy_ref[rows, cols].astype(F32)
                dp_ref[rows, cols] = (dyv * sz * _gelu_grad(pu)).astype(BF)
                dsz = dyv * u
                dbacc_ref[gg] += dsz
                dszb = dsz.astype(BF)
                dw_ref[gg] += _dot_nt(dszb, znb[:, cols])
                dzn_parts.append(_dot(wts[gg], dszb))
            dzn = jnp.concatenate(dzn_parts, axis=1)
            dg_ref[...] += jnp.sum(dzn * zhat, axis=0, keepdims=True)
            dbias_ref[...] += jnp.sum(dzn, axis=0, keepdims=True)
            dzh = dzn * gain
            dz = rstd * (dzh - jnp.mean(dzh, axis=-1, keepdims=True)
                         - zhat * jnp.mean(dzh * zhat, axis=-1, keepdims=True))
            dp_ref[rows, GMLP_W:] = (dz * _gelu_grad(pz)).astype(BF)

        @pl.when(i == pl.num_programs(0) - 1)
        def _():
            for gg in range(4):
                dw_ref[gg] = jnp.where(tril, dw_ref[gg], 0.0)
                dbs_ref[gg] = jnp.sum(dbacc_ref[gg], axis=1, keepdims=True)

    full2 = lambda s: pl.BlockSpec(s, lambda i: (0, 0))
    full3 = lambda s: pl.BlockSpec(s, lambda i: (0, 0, 0))
    return pl.pallas_call(
        body, name="gmlp_bwd", grid=(S // tm,),
        out_shape=(jax.ShapeDtypeStruct((S, 2 * GMLP_W), BF), jax.ShapeDtypeStruct((4, CHUNK, CHUNK), F32),
                   jax.ShapeDtypeStruct((4, CHUNK, 1), F32), jax.ShapeDtypeStruct((1, GMLP_W), F32),
                   jax.ShapeDtypeStruct((1, GMLP_W), F32)),
        in_specs=[pl.BlockSpec((tm, 2 * GMLP_W), lambda i: (i, 0)), pl.BlockSpec((tm, GMLP_W), lambda i: (i, 0)),
                  full2((1, GMLP_W)), full2((1, GMLP_W)), full3((4, CHUNK, CHUNK)), full3((4, CHUNK, CHUNK)),
                  full2((CHUNK, 4))],
        out_specs=(pl.BlockSpec((tm, 2 * GMLP_W), lambda i: (i, 0)), full3((4, CHUNK, CHUNK)), full3((4, CHUNK, 1)),
                   full2((1, GMLP_W)), full2((1, GMLP_W))),
        scratch_shapes=[pltpu.VMEM((4, CHUNK, CHUNK), F32)],
        compiler_params=_cp(("arbitrary",)))(puz, dyg, ln_g, ln_b, w_sp, w_sp_t, b_sp_t)


def _attn_bwd(qkv, dya, dsums, lse, name):
    _, d, L, _ = qkv.shape
    nsub = min(ATTN_NSUB, L // QB)
    R = nsub * QB
    nsteps = L // R

    def body(q_ref, qn_ref, kp_ref, kc_ref, vp_ref, vc_ref, dy_ref, dyn_ref, e_ref, en_ref, l_ref, ln_ref, o_ref):
        i = pl.program_id(1)
        band, band_first = _band_masks(i == 0)
        row = lax.broadcasted_iota(jnp.int32, (QB, QB), 0)
        col = lax.broadcasted_iota(jnp.int32, (QB, QB), 1)
        mask_next = jnp.logical_and(col >= row, i < nsteps - 1)
        kc, vc = kc_ref[...], vc_ref[...]
        kfull = jnp.concatenate([kp_ref[...], kc], axis=0)
        vfull = jnp.concatenate([vp_ref[...], vc], axis=0)
        k_last, v_last = kc[(nsub - 1) * QB:], vc[(nsub - 1) * QB:]
        q_ext = jnp.concatenate([q_ref[...], qn_ref[...]], axis=0)
        dy_ext = jnp.concatenate([dy_ref[...], dyn_ref[...]], axis=0)
        esum, esum_n, lse, lse_n = e_ref[...], en_ref[...], l_ref[...], ln_ref[...]
        win = lambda t, sb: t[sb * QB:(sb + 2) * QB]
        blk = lambda t, sb: t[sb * QB:(sb + 1) * QB]
        hms = [_head_mask(hh) for hh in range(NSLOT)]
        q_hs = [jnp.where(hm, q_ext, 0) for hm in hms]
        dy_hs = [jnp.where(hm, dy_ext, 0) for hm in hms]
        raw = []
        for hh in range(NSLOT):
            tiles = [(_dot_nt(blk(q_hs[hh], sb), win(kfull, sb)), _dot_nt(blk(dy_hs[hh], sb), win(vfull, sb)))
                     for sb in range(nsub)]
            tiles.append((_dot_nt(blk(q_hs[hh], nsub), k_last), _dot_nt(blk(dy_hs[hh], nsub), v_last)))
            raw.append(tiles)
        ps, dss = [], []
        for hh in range(NSLOT):
            rowstat = lambda t: jnp.max(jnp.where(hms[hh], t, -jnp.inf), axis=1, keepdims=True)
            p_h, ds_h = [], []
            for sb in range(nsub + 1):
                sc, dp = raw[hh][sb]
                if sb < nsub:
                    msk, lrow, erow = (band_first if sb == 0 else band), rowstat(blk(lse, sb)), rowstat(blk(esum, sb))
                else:
                    msk, lrow, erow = mask_next, rowstat(lse_n), rowstat(esum_n)
                p = jnp.where(msk, jnp.exp(sc * SCALE - lrow), 0.0)
                p_h.append(p.astype(BF))
                ds_h.append((p * (dp - erow)).astype(BF))
            ps.append(p_h)
            dss.append(ds_h)
        dq = [jnp.zeros((QB, GW), F32) for _ in range(nsub)]
        dk = [jnp.zeros((QB, GW), F32) for _ in range(nsub)]
        dv = [jnp.zeros((QB, GW), F32) for _ in range(nsub)]
        for hh in range(NSLOT):
            for sb in range(nsub):
                dq[sb] = dq[sb] + jnp.where(hms[hh], _dot(dss[hh][sb], win(kfull, sb)), 0.0)
                nxt = lambda t: t[sb + 1][:, :QB] if sb + 1 < nsub else t[nsub]
                dk[sb] = dk[sb] + _dot_tn(jnp.concatenate([dss[hh][sb][:, QB:], nxt(dss[hh])], axis=0), win(q_hs[hh], sb))
                dv[sb] = dv[sb] + _dot_tn(jnp.concatenate([ps[hh][sb][:, QB:], nxt(ps[hh])], axis=0), win(dy_hs[hh], sb))
        for sb in range(nsub):
            rows = slice(sb * QB, (sb + 1) * QB)
            o_ref[0, rows, :] = (dq[sb] * SCALE).astype(BF)
            o_ref[1, rows, :] = (dk[sb] * SCALE).astype(BF)
            o_ref[2, rows, :] = dv[sb].astype(BF)

    prev = lambda i: jnp.maximum(i * nsub - 1, 0)
    nxt = lambda i: jnp.minimum((i + 1) * nsub, L // QB - 1)
    cur4 = lambda t: pl.BlockSpec((None, None, R, GW), lambda r, i: (t, r, i, 0))
    prv4 = lambda t: pl.BlockSpec((None, None, QB, GW), lambda r, i: (t, r, prev(i), 0))
    nxt4 = lambda t: pl.BlockSpec((None, None, QB, GW), lambda r, i: (t, r, nxt(i), 0))
    cur3 = pl.BlockSpec((None, R, GW), lambda r, i: (r, i, 0))
    nxt3 = pl.BlockSpec((None, QB, GW), lambda r, i: (r, nxt(i), 0))
    return pl.pallas_call(
        body, name=name, grid=(d, nsteps), out_shape=jax.ShapeDtypeStruct((3, d, L, GW), BF),
        in_specs=[cur4(0), nxt4(0), prv4(1), cur4(1), prv4(2), cur4(2), cur3, nxt3, cur3, nxt3, cur3, nxt3],
        out_specs=pl.BlockSpec((3, None, R, GW), lambda r, i: (0, r, i, 0)),
        compiler_params=_cp(("parallel", "arbitrary")))(qkv, qkv, qkv, qkv, qkv, qkv, dya, dya, dsums, dsums, lse, lse)


def _inproj_bwd(dqkvs, dpuz, dgates, wqkv, wuz, wg, cos_t, sin_t, x, dx2, g1):
    S = x.shape[0]
    tm = min(512, S)

    def body(d0_ref, d1_ref, d2_ref, dp_ref, dg_ref, wqkv_ref, wuz_ref, wg_ref, c_ref, s_ref,
             x_ref, dx2_ref, g1_ref, gx_ref, dg1_ref, dn_ref, scr):
        i = pl.program_id(0)

        @pl.when(i == 0)
        def _():
            dg1_ref[...] = jnp.zeros_like(dg1_ref)

        for t in range(3):
            for g, (d_ref, d) in enumerate(zip((d0_ref, d1_ref, d2_ref), DILS)):
                piece = _undilate_load(d_ref, (t,), d, scr, tm)
                if t < 2:
                    piece = _rope_bwd(piece, c_ref, s_ref)
                dn_ref[:, (3 * t + g) * GW:(3 * t + g + 1) * GW] = piece.astype(BF)
        dh = (_dot_nt(dn_ref[...], wqkv_ref[...]) + _dot_nt(dp_ref[...], wuz_ref[...])
              + _dot_nt(dg_ref[...], wg_ref[...]))
        xv = x_ref[...]
        dx1, dg1 = _rms_bwd(xv, _rms_stats(xv), g1_ref[...], dh)
        gx_ref[...] = dx2_ref[...] + dx1
        dg1_ref[...] += dg1

    row = lambda w: pl.BlockSpec((tm, w), lambda i: (i, 0))
    full = lambda s: pl.BlockSpec(s, lambda i: (0, 0))
    dil = lambda d: pl.BlockSpec((3, d, tm // d, GW), lambda i: (0, 0, i, 0))
    return pl.pallas_call(
        body, name="inproj_bwd", grid=(S // tm,),
        out_shape=(jax.ShapeDtypeStruct((S, D), F32), jax.ShapeDtypeStruct((1, D), F32),
                   jax.ShapeDtypeStruct((S, 3 * AW), BF)),
        in_specs=[dil(d) for d in DILS] + [row(2 * GMLP_W), row(2 * D), full((D, 3 * AW)), full((D, 2 * GMLP_W)),
                                            full((D, 2 * D)), row(128), row(128), row(D), row(D), full((1, D))],
        out_specs=(row(D), full((1, D)), row(3 * AW)),
        scratch_shapes=[pltpu.VMEM((2, tm, 128), F32)],
        compiler_params=_cp(("arbitrary",)))(*dqkvs, dpuz, dgates, wqkv, wuz, wg, cos_t, sin_t, x, dx2, g1)


def _adam_math(w, g, m, v):
    m2 = ADAM_B1 * m + (1.0 - ADAM_B1) * g
    v2 = ADAM_B2 * v + (1.0 - ADAM_B2) * (g * g)
    m_hat = m2 / (1.0 - ADAM_B1 ** ADAM_STEP)
    v_hat = v2 / (1.0 - ADAM_B2 ** ADAM_STEP)
    delta = -ADAM_LR * (m_hat / (jnp.sqrt(v_hat) + ADAM_EPS) + ADAM_WD * w)
    return delta, m2, v2


def _adam_shard(own, recv, w, m, v, name):
    R, C = w.shape
    tr = min(256, R)

    def body(own_ref, r_ref, w_ref, m_ref, v_ref, g_ref, d_ref, m2_ref, v2_ref):
        g = own_ref[...] + r_ref[0].astype(F32) + r_ref[1].astype(F32) + r_ref[2].astype(F32)
        g_ref[...] = g
        d_ref[...], m2_ref[...], v2_ref[...] = _adam_math(w_ref[...], g, m_ref[...], v_ref[...])

    spec = pl.BlockSpec((tr, C), lambda i: (i, 0))
    out = jax.ShapeDtypeStruct((R, C), F32)
    return pl.pallas_call(
        body, name=name, grid=(R // tr,), out_shape=(out, out, out, out),
        in_specs=[spec, pl.BlockSpec((3, tr, C), lambda i: (0, i, 0)), spec, spec, spec],
        out_specs=(spec, spec, spec, spec), compiler_params=_cp(("parallel",)))(own, recv, w, m, v)


def _adam_small(gp, wp, mp, vp):
    def body(g_ref, w_ref, m_ref, v_ref, d_ref, m2_ref, v2_ref):
        d_ref[...], m2_ref[...], v2_ref[...] = _adam_math(w_ref[...], g_ref[...], m_ref[...], v_ref[...])

    out = jax.ShapeDtypeStruct(gp.shape, F32)
    return pl.pallas_call(body, name="adam_small", out_shape=(out, out, out),
                          compiler_params=pltpu.CompilerParams(vmem_limit_bytes=VMEM_LIMIT))(gp, wp, mp, vp)


def _rs_add(gblocks, recv, idx, name):
    _, R, C = gblocks.shape
    tr = min(256, R)

    def body(t_ref, g_ref, r_ref, own_ref, send_ref):
        j = pl.program_id(1)
        s = g_ref[...] + r_ref[...]

        @pl.when(j == 0)
        def _():
            own_ref[...] = s

        @pl.when(j > 0)
        def _():
            send_ref[...] = s.astype(BF)

    grid_spec = pltpu.PrefetchScalarGridSpec(
        num_scalar_prefetch=1, grid=(R // tr, 4),
        in_specs=[pl.BlockSpec((None, tr, C), lambda i, j, t: (t[j], i, 0)),
                  pl.BlockSpec((None, tr, C), lambda i, j, t: (t[4 + j], i, 0))],
        out_specs=[pl.BlockSpec((tr, C), lambda i, j, t: (i, 0)),
                   pl.BlockSpec((None, tr, C), lambda i, j, t: (jnp.maximum(j - 1, 0), i, 0))])
    return pl.pallas_call(
        body, name=name, grid_spec=grid_spec,
        out_shape=(jax.ShapeDtypeStruct((R, C), F32), jax.ShapeDtypeStruct((3, R, C), BF)),
        compiler_params=_cp(("parallel", "arbitrary")))(idx, gblocks, recv)


def _mesh_pos():
    return lax.axis_index("x"), lax.axis_index("y"), lax.axis_index("c")


def _all_gather_weights(shards):
    n = len(shards)

    def body(*refs):
        ins, outs = refs[:n], refs[n:2 * n]
        send_sems, recv_sems, loc_sems = refs[2 * n:]
        x, y, c = _mesh_pos()
        me, sib = (x, y, c), (x, y, 1 - c)
        chips = [(1 - x, y), (x, 1 - y), (1 - x, 1 - y)]

        def rcopy(k, s, block, to, src=None):
            dst = outs[k].at[4 * block[0] + 2 * block[1] + block[2]]
            return pltpu.make_async_remote_copy(
                src_ref=dst if src is None else src, dst_ref=dst, send_sem=send_sems.at[k, s],
                recv_sem=recv_sems.at[k, s], device_id=to, device_id_type=MESH)

        locs = [pltpu.make_async_copy(ins[k], outs[k].at[4 * x + 2 * y + c], loc_sems.at[k]) for k in range(n)]
        for cp in locs:
            cp.start()
        started = []
        for k in range(n):
            for j, chip in enumerate(chips):
                started.append(rcopy(k, 1 + j, me, (*chip, c), src=ins[k]))
                started[-1].start()
        for k in range(n):
            started.append(rcopy(k, 0, me, sib, src=ins[k]))
            started[-1].start()
        for k in range(n):
            for j, chip in enumerate(chips):
                rcopy(k, 1 + j, (*chip, c), me).wait_recv()
                started.append(rcopy(k, 4 + j, (*chip, c), sib))
                started[-1].start()
        for k in range(n):
            rcopy(k, 0, sib, me).wait_recv()
            for j, chip in enumerate(chips):
                rcopy(k, 4 + j, (*chip, 1 - c), me).wait_recv()
        for cp in started:
            cp.wait_send()
        for cp in locs:
            cp.wait()

    hbm = pl.BlockSpec(memory_space=pl.ANY)
    return pl.pallas_call(
        body, name="ag_weights",
        out_shape=tuple(jax.ShapeDtypeStruct((8,) + s.shape, s.dtype) for s in shards),
        in_specs=[hbm] * n, out_specs=(hbm,) * n,
        scratch_shapes=[pltpu.SemaphoreType.DMA((n, 7)), pltpu.SemaphoreType.DMA((n, 7)),
                        pltpu.SemaphoreType.DMA((n,))])(*shards)


_HBM = pl.BlockSpec(memory_space=pltpu.HBM)
_SEM = pl.BlockSpec(memory_space=pltpu.SEMAPHORE)
_EFFECT = pltpu.SideEffectType.DATAFLOW_SIDE_EFFECTING
_RELATIONS = [(dx, dy, dc) for dx in (0, 1) for dy in (0, 1) for dc in (0, 1)][1:]


def _flip(v, d):
    return 1 - v if d else v


def _plan_gather(n):
    def plan(x, y, c):
        return [(k, None, 4 * x + 2 * y + c, (_flip(x, dx), _flip(y, dy), _flip(c, dc)))
                for k in range(n) for dx, dy, dc in _RELATIONS]
    return plan


def _plan_d2d(n):
    def plan(x, y, c):
        return [(k, 2 * kk + 1 - c, kk, (x, y, 1 - c)) for k in range(n) for kk in range(4)]
    return plan


def _plan_ici(n):
    def plan(x, y, c):
        return [(k, j, j, (_flip(x, dx), _flip(y, dy), c))
                for k in range(n) for j, (dx, dy) in enumerate(((1, 0), (0, 1), (1, 1)))]
    return plan


def _plan_copies(plan, src_refs, land_refs, send_sems, recv_sems):
    x, y, c = _mesh_pos()
    return [pltpu.make_async_remote_copy(
        src_ref=src_refs[k] if si is None else src_refs[k].at[si], dst_ref=land_refs[k].at[di],
        send_sem=send_sems.at[n], recv_sem=recv_sems.at[n], device_id=dev, device_id_type=MESH)
        for n, (k, si, di, dev) in enumerate(plan(x, y, c))]


def _exchange_start(srcs, land_shapes, plan, ncopies, name, after):
    n = len(srcs)

    def body(*refs):
        src_refs, land_refs = refs[:n], refs[n:2 * n]
        send_sems, recv_sems = refs[2 * n + len(after)], refs[2 * n + len(after) + 1]
        token = refs[-1]
        for cp in _plan_copies(plan, src_refs, land_refs, send_sems, recv_sems):
            cp.start()
        token[...] = jnp.zeros_like(token)

    lands = [pltpu.with_memory_space_constraint(lax.empty(s, a.dtype), pltpu.HBM) for s, a in zip(land_shapes, srcs)]
    srcs = [pltpu.with_memory_space_constraint(a, pltpu.HBM) for a in srcs]
    outs = pl.pallas_call(
        body, name=name,
        out_shape=(pltpu.SemaphoreType.DMA((ncopies,)), pltpu.SemaphoreType.DMA((ncopies,)))
        + tuple(pltpu.HBM(a.shape, a.dtype) for a in srcs) + tuple(pltpu.HBM(a.shape, a.dtype) for a in lands)
        + (jax.ShapeDtypeStruct((8, 128), F32),),
        in_specs=[_HBM] * (2 * n) + [pl.BlockSpec(memory_space=pl.ANY)] * len(after),
        out_specs=(_SEM, _SEM) + (_HBM,) * (2 * n) + (pl.BlockSpec(memory_space=pltpu.VMEM),),
        input_output_aliases={i: 2 + i for i in range(2 * n)},
        compiler_params=pltpu.CompilerParams(has_side_effects=_EFFECT))(*srcs, *lands, *after)
    return (outs[0], outs[1], list(outs[2:2 + n]), list(outs[2 + n:2 + 2 * n])), outs[-1]


def _exchange_wait(handle, plan, name, after):
    send_sems, recv_sems, srcs, lands = handle
    n = len(srcs)

    def body(*refs):
        src_refs, land_refs = refs[:n], refs[n:2 * n]
        for cp in _plan_copies(plan, src_refs, land_refs, refs[2 * n], refs[2 * n + 1]):
            cp.wait_send()
            cp.wait_recv()

    outs = pl.pallas_call(
        body, name=name,
        out_shape=tuple(pltpu.HBM(a.shape, a.dtype) for a in srcs) + tuple(pltpu.HBM(a.shape, a.dtype) for a in lands),
        in_specs=[_HBM] * (2 * n) + [_SEM, _SEM] + [pl.BlockSpec(memory_space=pl.ANY)] * len(after),
        out_specs=(_HBM,) * (2 * n), input_output_aliases={i: i for i in range(2 * n)},
        compiler_params=pltpu.CompilerParams(has_side_effects=_EFFECT))(*srcs, *lands, send_sems, recv_sems, *after)
    return list(outs[:n]), list(outs[n:])


def _small_allreduce(pkt):
    R = pkt.shape[0]

    def body(p_ref, o_ref, buf, send_sems, recv_sems):
        x, y, c = _mesh_pos()
        me, sib = (x, y, c), (x, y, 1 - c)
        chips = [(1 - x, y), (x, 1 - y), (1 - x, 1 - y)]

        def rcopy(s, block, to, src=None):
            dst = buf.at[4 * block[0] + 2 * block[1] + block[2]]
            return pltpu.make_async_remote_copy(
                src_ref=dst if src is None else src, dst_ref=dst, send_sem=send_sems.at[s],
                recv_sem=recv_sems.at[s], device_id=to, device_id_type=MESH)

        buf[4 * x + 2 * y + c] = p_ref[...]
        started = [rcopy(0, me, sib, src=p_ref)]
        started += [rcopy(1 + j, me, (*chip, c), src=p_ref) for j, chip in enumerate(chips)]
        for cp in started:
            cp.start()
        for j, chip in enumerate(chips):
            rcopy(1 + j, (*chip, c), me).wait_recv()
            started.append(rcopy(4 + j, (*chip, c), sib))
            started[-1].start()
        rcopy(0, sib, me).wait_recv()
        for j, chip in enumerate(chips):
            rcopy(4 + j, (*chip, 1 - c), me).wait_recv()
        for cp in started:
            cp.wait_send()
        acc = buf[0]
        for b in range(1, 8):
            acc = acc + buf[b]
        o_ref[...] = acc

    return pl.pallas_call(
        body, name="small_allreduce", out_shape=jax.ShapeDtypeStruct(pkt.shape, F32),
        in_specs=[pl.BlockSpec(memory_space=pltpu.VMEM)], out_specs=pl.BlockSpec(memory_space=pltpu.VMEM),
        scratch_shapes=[pltpu.VMEM((8, R, 128), F32), pltpu.SemaphoreType.DMA((7,)), pltpu.SemaphoreType.DMA((7,))],
        compiler_params=pltpu.CompilerParams(vmem_limit_bytes=VMEM_LIMIT))(pkt)


def _rope_tables(S):
    half = HD // 2
    inv_freq = ROPE_THETA ** (-jnp.arange(half, dtype=F32) / half)
    ang = jnp.arange(S, dtype=F32)[:, None] * inv_freq[None, :]
    cos, sin = jnp.cos(ang), jnp.sin(ang)
    return jnp.concatenate([cos, cos, cos, cos], axis=1), jnp.concatenate([-sin, sin, -sin, sin], axis=1)


def _to_blocks(g, col_sharded):
    if col_sharded:
        return g.reshape(g.shape[0], 8, g.shape[1] // 8).transpose(1, 0, 2)
    return g.reshape(8, g.shape[0] // 8, g.shape[1])


def _from_blocks(t, col_sharded):
    if col_sharded:
        return t.transpose(1, 0, 2).reshape(t.shape[1], 8 * t.shape[2])
    return t.reshape(8 * t.shape[1], t.shape[2])


class _NoComm:
    def __init__(self, late_weights):
        self._late = late_weights
        self.grads = {}

    def start_token(self):
        return 0.0

    def late_weights(self, after):
        return self._late

    def rs_start(self, key, grads, col_sharded):
        self.grads[key] = grads
        return jnp.zeros((8, 128), F32)

    def rs_mid(self, key, after):
        return jnp.zeros((8, 128), F32)


class _FsdpComm:
    def __init__(self, late_shards, col_sharded, after, idx, block):
        self._col, self._idx, self._block, self._rs = col_sharded, idx, block, {}
        n = len(late_shards)
        self._gather, self._token = _exchange_start(
            late_shards, [(8,) + s.shape for s in late_shards], _plan_gather(n), 7 * n, "ag_late_start", (after,))

    def start_token(self):
        return self._token[0, 0]

    def late_weights(self, after):
        shards, lands = _exchange_wait(self._gather, _plan_gather(len(self._col)), "ag_late_wait", after)
        lands = [lax.dynamic_update_index_in_dim(t, s, self._block, 0) for t, s in zip(lands, shards)]
        return [_from_blocks(t, cs) for t, cs in zip(lands, self._col)]

    def rs_start(self, key, grads, col_sharded):
        gblocks = [_to_blocks(g, cs) for g, cs in zip(grads, col_sharded)]
        n = len(gblocks)
        d2d, token = _exchange_start(gblocks, [(4,) + g.shape[1:] for g in gblocks], _plan_d2d(n), 4 * n,
                                     "rs_%s_d2d_start" % key, (self._token,))
        self._rs[key] = dict(n=n, d2d=d2d)
        return token

    def rs_mid(self, key, after):
        st = self._rs[key]
        gblocks, from_sib = _exchange_wait(st["d2d"], _plan_d2d(st["n"]), "rs_%s_d2d_wait" % key, after)
        halves = [_rs_add(g, r, self._idx, "rs_add_%s_%d" % (key, k)) for k, (g, r) in enumerate(zip(gblocks, from_sib))]
        st["own"] = [own for own, _ in halves]
        sends = [send for _, send in halves]
        st["ici"], token = _exchange_start(sends, [t.shape for t in sends], _plan_ici(st["n"]), 3 * st["n"],
                                           "rs_%s_ici_start" % key, (self._token,))
        return token

    def rs_end(self, key, after):
        st = self._rs[key]
        return st["own"], _exchange_wait(st["ici"], _plan_ici(st["n"]), "rs_%s_ici_wait" % key, after)[1]


def _local_step(x, tgt, wqkv, wuz, wg, comm, g1, g2, g3, g4, ln_g, ln_b, w_sp, b_sp):
    S = x.shape[0]
    cos_t, sin_t = _rope_tables(S)
    b_sp_t = b_sp.T
    w_sp_t = w_sp.transpose(0, 2, 1)

    h1 = _rms_fwd(x, g1 + comm.start_token(), "rms_pre_mix")
    qkvs = [_proj_qkv(h1, wqkv, cos_t, sin_t, g, d, "proj_qkv_d%d" % d) for g, d in enumerate(DILS)]
    puz = _proj(h1, wuz, "none", "proj_uz")
    gates = _proj(h1, wg, "sigmoid", "proj_gates")
    fwd = [_attn_fwd(t, "attn_fwd_d%d" % d) for t, d in zip(qkvs, DILS)]
    ya, *lses = _attn_combine([o for o, _ in fwd], [l for _, l in fwd])
    yg = _gmlp_fwd(puz, ln_g, ln_b, w_sp, b_sp_t)
    wba, wbg, wout, wmi, wmo = comm.late_weights(after=(ya, yg, gates))
    abr, gbr, merged, y, x2, h3 = _merge_fwd(ya, yg, gates, wba, wbg, wout, x, g2, g3)
    a, dy2, dout, loss_row, dg4 = _mlp_fwd(h3, wmi, wmo, x2, tgt, g4)

    dpre, dx2, dy, dg3, dg2 = _mlp_bwd(dy2, a, wmo, wmi, x2, y, dout, g2, g3)
    dwmo = _mm_tn(a, dy2, "dw_mlp_out", square_a=True)
    dwmi = _mm_tn(h3, dpre, "dw_mlp_in")
    tie = comm.rs_start("mlp", [dwmi, dwmo], [True, False])
    dgates, da, db, dyg, *rest = _outproj_bwd(dy, wout, abr, gbr, gates, wba, wbg, ya, tie)
    dyas, dsums = rest[:3], rest[3:]
    dwout = _mm_tn(merged, dy, "dw_out")
    dwba = _mm_tn(ya, da, "dw_branch_attn")
    dwbg = _mm_tn(yg, db, "dw_branch_gmlp")
    tie = comm.rs_mid("mlp", after=(dyg,))
    tie2 = comm.rs_start("mid", [dwba, dwbg, dwout], [True, True, False])
    dpuz, dwsp, dbs, dlng, dlnb = _gmlp_bwd(puz, dyg, ln_g + (tie[0, 0] + tie2[0, 0]), ln_b, w_sp, w_sp_t, b_sp_t)
    dqkvs = [_attn_bwd(qkvs[g], dyas[g], dsums[g], lses[g], "attn_bwd_d%d" % d) for g, d in enumerate(DILS)]
    tie = comm.rs_mid("mid", after=tuple(dqkvs))
    grad_x, dg1, dqkv = _inproj_bwd(dqkvs, dpuz, dgates, wqkv, wuz, wg, cos_t, sin_t, x, dx2, g1 + tie[0, 0])
    dwin = jnp.concatenate([_mm_tn(h1, t, "dw_in_%d" % n) for n, t in enumerate((dqkv, dpuz, dgates))], axis=1)
    small = dict(g1=dg1, g2=dg2, g3=dg3, g4=dg4, ln_g=dlng, ln_b=dlnb, b_sp=dbs.reshape(4, CHUNK), w_sp=dwsp)
    return loss_row, grad_x, dwin, small


def _pack_small(first_row, t):
    pad = lambda a, rows: jnp.concatenate([a, jnp.zeros((rows - a.shape[0], 128), F32)], axis=0)
    return jnp.concatenate([
        pad(first_row, 8), t["g1"].reshape(8, 128), t["g2"].reshape(8, 128), t["g3"].reshape(8, 128),
        t["g4"].reshape(8, 128), t["ln_g"].reshape(4, 128), t["ln_b"].reshape(4, 128),
        pad(t["b_sp"].reshape(4, 128), 8), t["w_sp"].reshape(4 * CHUNK, 128)], axis=0)


def _unpack_small(p):
    vec = lambda r, n: p[r:r + n // 128].reshape(1, n)
    return dict(g1=vec(PK_G1, D), g2=vec(PK_G2, D), g3=vec(PK_G3, D), g4=vec(PK_G4, D),
                ln_g=vec(PK_LNG, GMLP_W), ln_b=vec(PK_LNB, GMLP_W),
                b_sp=p[PK_BSP:PK_BSP + 4].reshape(1, 4, CHUNK), w_sp=p[PK_WSP:PKT_ROWS].reshape(1, 4, CHUNK, CHUNK))


def kernel(x, norm_pre_mix, w_in, w_spatial, b_spatial, ln_v_gain, ln_v_bias, w_branch_attn, w_branch_gmlp, w_out, norm_post_mix, norm_pre_mlp, w_mlp_in, w_mlp_out, norm_post_mlp, loss_target, m_norm_pre_mix, m_w_in, m_w_spatial, m_b_spatial, m_ln_v_gain, m_ln_v_bias, m_w_branch_attn, m_w_branch_gmlp, m_w_out, m_norm_post_mix, m_norm_pre_mlp, m_w_mlp_in, m_w_mlp_out, m_norm_post_mlp, v_norm_pre_mix, v_w_in, v_w_spatial, v_b_spatial, v_ln_v_gain, v_ln_v_bias, v_w_branch_attn, v_w_branch_gmlp, v_w_out, v_norm_post_mix, v_norm_pre_mlp, v_w_mlp_in, v_w_mlp_out, v_norm_post_mlp):
    mx, my, mc = _mesh_pos()
    rel = [(0, 0), (1, 0), (0, 1), (1, 1)]
    chip_of = [2 * (mx ^ dx) + (my ^ dy) for dx, dy in rel]
    idx = jnp.stack([2 * k + mc for k in chip_of] + chip_of).astype(jnp.int32)

    (g_win,) = _all_gather_weights([w_in[0].astype(BF)])
    win = _from_blocks(g_win, True)
    wqkv, wuz, wg = win[:, :3 * AW], win[:, 3 * AW:3 * AW + 2 * GMLP_W], win[:, 3 * AW + 2 * GMLP_W:]
    late = [w_branch_attn[0], w_branch_gmlp[0], w_out[0], w_mlp_in[0], w_mlp_out[0]]
    comm = _FsdpComm([w.astype(BF) for w in late], [True, True, False, True, False], g_win, idx, 4 * mx + 2 * my + mc)

    loss_row, grad_x, dwin, small = _local_step(
        x[0], loss_target[0], wqkv, wuz, wg, comm,
        norm_pre_mix, norm_post_mix, norm_pre_mlp, norm_post_mlp, ln_v_gain, ln_v_bias, w_spatial[0], b_spatial[0])

    tie = comm.rs_start("win", [dwin], [True])
    pkt = _small_allreduce(_pack_small(loss_row + tie[:1], small))
    tie = comm.rs_mid("win", after=(pkt,))
    small_w = dict(g1=norm_pre_mix, g2=norm_post_mix, g3=norm_pre_mlp, g4=norm_post_mlp, ln_g=ln_v_gain,
                   ln_b=ln_v_bias, b_sp=b_spatial[0], w_sp=w_spatial[0])
    small_m = dict(g1=m_norm_pre_mix, g2=m_norm_post_mix, g3=m_norm_pre_mlp, g4=m_norm_post_mlp, ln_g=m_ln_v_gain,
                   ln_b=m_ln_v_bias, b_sp=m_b_spatial[0], w_sp=m_w_spatial[0])
    small_v = dict(g1=v_norm_pre_mix, g2=v_norm_post_mix, g3=v_norm_pre_mlp, g4=v_norm_post_mlp, ln_g=v_ln_v_gain,
                   ln_b=v_ln_v_bias, b_sp=v_b_spatial[0], w_sp=v_w_spatial[0])
    zero_row = jnp.zeros((1, 128), F32)
    ones_row = jnp.ones((1, 128), F32)
    sd, sm, sv = _adam_small(pkt, _pack_small(zero_row, small_w), _pack_small(zero_row, small_m),
                             _pack_small(ones_row, small_v))
    mlp_own, mlp_chips = comm.rs_end("mlp", after=(tie,))
    mid_own, mid_chips = comm.rs_end("mid", after=(tie,))
    adam = lambda nm, own, r, w, m, v: _adam_shard(own, r, w[0], m[0], v[0], "adam_" + nm)
    upd = {
        "w_branch_attn": adam("w_branch_attn", mid_own[0], mid_chips[0], w_branch_attn, m_w_branch_attn, v_w_branch_attn),
        "w_branch_gmlp": adam("w_branch_gmlp", mid_own[1], mid_chips[1], w_branch_gmlp, m_w_branch_gmlp, v_w_branch_gmlp),
        "w_out": adam("w_out", mid_own[2], mid_chips[2], w_out, m_w_out, v_w_out),
        "w_mlp_in": adam("w_mlp_in", mlp_own[0], mlp_chips[0], w_mlp_in, m_w_mlp_in, v_w_mlp_in),
        "w_mlp_out": adam("w_mlp_out", mlp_own[1], mlp_chips[1], w_mlp_out, m_w_mlp_out, v_w_mlp_out),
    }
    loss = pkt[0, 0]
    win_own, win_chips = comm.rs_end("win", after=(sd, upd["w_mlp_in"][0], upd["w_mlp_out"][0], upd["w_out"][0]))
    upd["w_in"] = adam("w_in", win_own[0], win_chips[0], w_in, m_w_in, v_w_in)
    sg, sd, sm, sv = (_unpack_small(p) for p in (pkt, sd, sm, sv))

    order = ["g1", "w_in", "w_sp", "b_sp", "ln_g", "ln_b", "w_branch_attn", "w_branch_gmlp", "w_out", "g2", "g3",
             "w_mlp_in", "w_mlp_out", "g4"]

    def pick(which):
        out = []
        for nm in order:
            if nm in upd:
                out.append(upd[nm][which][None])
            else:
                out.append((sg, sd, sm, sv)[which][nm])
        return out

    return (loss, grad_x[None], *pick(0), *pick(1), *pick(2), *pick(3))
```

```python
import functools
import math

import jax
import jax.numpy as jnp
from jax import lax
from jax.experimental import pallas as pl
from jax.experimental.pallas import tpu as pltpu

D = 1024
HD = 64
NSLOT = 4
GW = NSLOT * HD
DILS = (1, 4, 16)
QB = 128
ATTN_NSUB = 2
MLP_TM_FWD = 512
MLP_TM_BWD = 256
AW = 3 * GW
GMLP_W = 512
CHUNK = 128
DFF = 4096
EPS = 1e-6
ROPE_THETA = 10000.0
SCALE = HD ** -0.5
NEG = -1e30

ADAM_LR = 0.001
ADAM_B1 = 0.9
ADAM_B2 = 0.999
ADAM_EPS = 1e-08
ADAM_WD = 0.01
ADAM_STEP = 10

BF = jnp.bfloat16
F32 = jnp.float32
MESH = pl.DeviceIdType.MESH
VMEM_LIMIT = 56 * 1024 * 1024


def _cp(sem):
    return pltpu.CompilerParams(dimension_semantics=sem, vmem_limit_bytes=VMEM_LIMIT)


def _dot(a, b):
    return jnp.dot(a, b, preferred_element_type=F32)


def _dot_nt(a, b):
    return lax.dot_general(a, b, (((1,), (1,)), ((), ())), preferred_element_type=F32)


def _dot_tn(a, b):
    return lax.dot_general(a, b, (((0,), (0,)), ((), ())), preferred_element_type=F32)


def _gelu(x):
    return jax.nn.gelu(x, approximate=True)


def _gelu_grad(x):
    k = math.sqrt(2.0 / math.pi)
    t = jnp.tanh(k * (x + 0.044715 * x * x * x))
    return 0.5 * (1.0 + t) + 0.5 * x * (1.0 - t * t) * (k * (1.0 + 3.0 * 0.044715 * x * x))


def _swap_halves(t):
    w = t.shape[1]
    lane = lax.broadcasted_iota(jnp.int32, t.shape, 1)
    first = (lane & (HD - 1)) < (HD // 2)
    return jnp.where(first, pltpu.roll(t, w - HD // 2, 1), pltpu.roll(t, HD // 2, 1))


def _head_mask(hh):
    lane = lax.broadcasted_iota(jnp.int32, (1, GW), 1)
    return jnp.logical_and(lane >= hh * HD, lane < (hh + 1) * HD)


def _rms_stats(xf):
    return lax.rsqrt(jnp.mean(xf * xf, axis=-1, keepdims=True) + EPS)


def _rms_bwd(xf, r, gain, dout):
    n = xf * r
    t = dout * gain
    dx = r * (t - n * jnp.mean(t * n, axis=-1, keepdims=True))
    return dx, jnp.sum(dout * n, axis=0, keepdims=True)


def _rms_fwd(x, gain, name):
    S = x.shape[0]
    tm = min(512, S)

    def body(x_ref, g_ref, h_ref):
        xf = x_ref[...]
        h_ref[...] = (xf * _rms_stats(xf) * g_ref[...]).astype(BF)

    return pl.pallas_call(
        body, name=name, out_shape=jax.ShapeDtypeStruct((S, D), BF), grid=(S // tm,),
        in_specs=[pl.BlockSpec((tm, D), lambda i: (i, 0)), pl.BlockSpec((1, D), lambda i: (0, 0))],
        out_specs=pl.BlockSpec((tm, D), lambda i: (i, 0)), compiler_params=_cp(("parallel",)))(x, gain)


def _proj(h, w, epi, name, tm):
    S, K = h.shape
    N = w.shape[1]
    tm = min(tm, S)

    def body(h_ref, w_ref, o_ref):
        y = _dot(h_ref[...], w_ref[...])
        if epi == "sigmoid":
            y = jax.nn.sigmoid(y)
        o_ref[...] = y.astype(BF)

    return pl.pallas_call(
        body, name=name, out_shape=jax.ShapeDtypeStruct((S, N), BF), grid=(S // tm,),
        in_specs=[pl.BlockSpec((tm, K), lambda i: (i, 0)), _resident((K, N))],
        out_specs=pl.BlockSpec((tm, N), lambda i: (i, 0)),
        compiler_params=_cp(("parallel",)))(h, w)


def _dilate_store(val, scr, o_ref, lead, d):
    rows = val.shape[0]
    if d == 1:
        o_ref[lead + (0,)] = val.astype(o_ref.dtype)
        return
    for hf in range(2):
        scr[hf, pl.ds(0, rows), :] = val[:, hf * 128:(hf + 1) * 128]
    for r in range(d):
        for hf in range(2):
            o_ref[lead + (r, slice(None), slice(hf * 128, (hf + 1) * 128))] = (
                scr[hf, pl.ds(r, rows // d, stride=d), :].astype(o_ref.dtype))


def _undilate_load(i_ref, lead, d, scr, rows):
    if d == 1:
        return i_ref[lead + (0,)].astype(F32)
    for r in range(d):
        for hf in range(2):
            scr[hf, pl.ds(r, rows // d, stride=d), :] = (
                i_ref[lead + (r, slice(None), slice(hf * 128, (hf + 1) * 128))].astype(F32))
    return jnp.concatenate([scr[0, pl.ds(0, rows), :], scr[1, pl.ds(0, rows), :]], axis=1)


def _rope_fwd(y, c_ref, s_ref):
    cosv = jnp.concatenate([c_ref[...]] * 2, axis=1)
    sinv = jnp.concatenate([s_ref[...]] * 2, axis=1)
    return y * cosv + _swap_halves(y) * sinv


def _rope_bwd(dy, c_ref, s_ref):
    cosv = jnp.concatenate([c_ref[...]] * 2, axis=1)
    sinv = jnp.concatenate([s_ref[...]] * 2, axis=1)
    return dy * cosv + _swap_halves(dy * sinv)


def _proj_qkv(h, wqkv, cos_t, sin_t, g, d, name):
    S, K = h.shape
    tm = min(2048, S)

    def body(h_ref, w_ref, c_ref, s_ref, o_ref, scr):
        j = pl.program_id(1)
        y = _dot(h_ref[...], w_ref[...])
        y = jnp.where(j < 2, _rope_fwd(y, c_ref, s_ref), y)
        _dilate_store(y, scr, o_ref, (), d)

    return pl.pallas_call(
        body, name=name, out_shape=jax.ShapeDtypeStruct((3, d, S // d, GW), BF), grid=(S // tm, 3),
        in_specs=[pl.BlockSpec((tm, K), lambda i, j: (i, 0)), pl.BlockSpec((K, GW), lambda i, j: (0, 3 * j + g)),
                  pl.BlockSpec((tm, 128), lambda i, j: (i, 0)), pl.BlockSpec((tm, 128), lambda i, j: (i, 0))],
        out_specs=pl.BlockSpec((None, d, tm // d, GW), lambda i, j: (j, 0, i, 0)),
        scratch_shapes=[pltpu.VMEM((2, tm, 128), F32)],
        compiler_params=_cp(("parallel", "arbitrary")))(h, wqkv, cos_t, sin_t)


def _band_masks(first_step):
    row = lax.broadcasted_iota(jnp.int32, (QB, 2 * QB), 0)
    col = lax.broadcasted_iota(jnp.int32, (QB, 2 * QB), 1)
    band = jnp.logical_and(col >= row, col <= row + QB)
    return band, jnp.logical_and(band, jnp.logical_or(col >= QB, jnp.logical_not(first_step)))


def _attn_fwd(qkv, name):
    _, d, L, _ = qkv.shape
    nsub = min(ATTN_NSUB, L // QB)
    R = nsub * QB
    nsteps = L // R

    def body(q_ref, kp_ref, kc_ref, vp_ref, vc_ref, o_ref, lse_ref):
        i = pl.program_id(1)
        band, band_first = _band_masks(i == 0)
        kfull = jnp.concatenate([kp_ref[...], kc_ref[...]], axis=0)
        vfull = jnp.concatenate([vp_ref[...], vc_ref[...]], axis=0)
        chains = [(sb, hh) for sb in range(nsub) for hh in range(NSLOT)]
        win = lambda t, sb: t[sb * QB:(sb + 2) * QB]
        scores = []
        for sb, hh in chains:
            qh = jnp.where(_head_mask(hh), q_ref[sb * QB:(sb + 1) * QB, :], 0)
            scores.append(_dot_nt(qh, win(kfull, sb)))
        soft = []
        for (sb, hh), sc in zip(chains, scores):
            sc = jnp.where(band_first if sb == 0 else band, sc * SCALE, NEG)
            m = jnp.max(sc, axis=1, keepdims=True)
            p = jnp.exp(sc - m)
            den = jnp.sum(p, axis=1, keepdims=True)
            soft.append((p.astype(BF), den, m + jnp.log(den)))
        accs = [_dot(p, win(vfull, sb)) for (sb, hh), (p, _, _) in zip(chains, soft)]
        for sb in range(nsub):
            o = jnp.zeros((QB, GW), F32)
            lse = jnp.zeros((QB, GW), F32)
            for hh in range(NSLOT):
                hm = _head_mask(hh)
                _, den, lrow = soft[sb * NSLOT + hh]
                o = o + jnp.where(hm, accs[sb * NSLOT + hh] / den, 0.0)
                lse = lse + jnp.where(hm, lrow, 0.0)
            o_ref[sb * QB:(sb + 1) * QB, :] = o
            lse_ref[sb * QB:(sb + 1) * QB, :] = lse

    prev = lambda i: jnp.maximum(i * nsub - 1, 0)
    cur = lambda t: pl.BlockSpec((None, None, R, GW), lambda r, i: (t, r, i, 0))
    prv = lambda t: pl.BlockSpec((None, None, QB, GW), lambda r, i: (t, r, prev(i), 0))
    out = pl.BlockSpec((None, R, GW), lambda r, i: (r, i, 0))
    return pl.pallas_call(
        body, name=name, grid=(d, nsteps),
        out_shape=(jax.ShapeDtypeStruct((d, L, GW), F32), jax.ShapeDtypeStruct((d, L, GW), F32)),
        in_specs=[cur(0), prv(1), cur(1), prv(2), cur(2)],
        out_specs=(out, out), compiler_params=_cp(("parallel", "arbitrary")))(qkv, qkv, qkv, qkv, qkv)


def _attn_combine(os_, lses):
    S = os_[0].shape[1]
    tm = min(1024, S)

    def body(o0, o1, o2, l0, l1, l2, y_ref, j0, j1, j2, scr):
        os_nat = [_undilate_load(o, (), d, scr, tm) for o, d in zip((o0, o1, o2), DILS)]
        a, b, c = [_undilate_load(l, (), d, scr, tm) for l, d in zip((l0, l1, l2), DILS)]
        m = jnp.maximum(jnp.maximum(a, b), c)
        wa, wb, wc = jnp.exp(a - m), jnp.exp(b - m), jnp.exp(c - m)
        den = wa + wb + wc
        y_ref[...] = ((wa * os_nat[0] + wb * os_nat[1] + wc * os_nat[2]) / den).astype(BF)
        lse = m + jnp.log(den)
        for j_ref, d in zip((j0, j1, j2), DILS):
            _dilate_store(lse, scr, j_ref, (), d)

    dil = lambda d: pl.BlockSpec((d, tm // d, GW), lambda i: (0, i, 0))
    dshape = lambda d: jax.ShapeDtypeStruct((d, S // d, GW), F32)
    return pl.pallas_call(
        body, name="attn_combine", grid=(S // tm,),
        out_shape=(jax.ShapeDtypeStruct((S, GW), BF),) + tuple(dshape(d) for d in DILS),
        in_specs=[dil(d) for d in DILS] * 2,
        out_specs=(pl.BlockSpec((tm, GW), lambda i: (i, 0)),) + tuple(dil(d) for d in DILS),
        scratch_shapes=[pltpu.VMEM((2, tm, 128), F32)],
        compiler_params=_cp(("parallel",)))(*os_, *lses)


def _tril(upper=False):
    row = lax.broadcasted_iota(jnp.int32, (CHUNK, CHUNK), 0)
    col = lax.broadcasted_iota(jnp.int32, (CHUNK, CHUNK), 1)
    return row <= col if upper else col <= row


def _ln_fwd(z, gain, bias):
    mu = jnp.mean(z, axis=-1, keepdims=True)
    zc = z - mu
    rstd = lax.rsqrt(jnp.mean(zc * zc, axis=-1, keepdims=True) + EPS)
    zhat = zc * rstd
    return zhat, rstd, zhat * gain + bias


def _gmlp_fwd(puz, ln_g, ln_b, w_sp, b_sp_t):
    S = puz.shape[0]
    tm = min(512, S)
    nch = tm // CHUNK

    def body(p_ref, g_ref, b_ref, w_ref, bt_ref, o_ref):
        tril = _tril()
        ws = [jnp.where(tril, w_ref[gg], 0.0).astype(BF) for gg in range(4)]
        for ch in range(nch):
            rows = slice(ch * CHUNK, (ch + 1) * CHUNK)
            z = _gelu(p_ref[rows, GMLP_W:].astype(F32))
            _, _, zn = _ln_fwd(z, g_ref[...], b_ref[...])
            zn = zn.astype(BF)
            for gg in range(4):
                cols = slice(gg * CHUNK, (gg + 1) * CHUNK)
                sz = _dot(ws[gg], zn[:, cols]) + bt_ref[:, gg:gg + 1]
                u = _gelu(p_ref[rows, cols].astype(F32))
                o_ref[rows, cols] = (u * sz).astype(BF)

    return pl.pallas_call(
        body, name="gmlp_fwd", out_shape=jax.ShapeDtypeStruct((S, GMLP_W), BF), grid=(S // tm,),
        in_specs=[pl.BlockSpec((tm, 2 * GMLP_W), lambda i: (i, 0)),
                  pl.BlockSpec((1, GMLP_W), lambda i: (0, 0)), pl.BlockSpec((1, GMLP_W), lambda i: (0, 0)),
                  pl.BlockSpec((4, CHUNK, CHUNK), lambda i: (0, 0, 0)), pl.BlockSpec((CHUNK, 4), lambda i: (0, 0))],
        out_specs=pl.BlockSpec((tm, GMLP_W), lambda i: (i, 0)),
        compiler_params=_cp(("parallel",)))(puz, ln_g, ln_b, w_sp, b_sp_t)


def _merge_fwd(ya, yg, gates, wba, wbg, wout, x, g2, g3):
    S = x.shape[0]
    tm = min(512, S)

    def body(ya_ref, yg_ref, gt_ref, wba_ref, wbg_ref, wo_ref, x_ref, g2_ref, g3_ref,
             a_ref, gb_ref, mg_ref, y_ref, x2_ref, h3_ref):
        a = _dot(ya_ref[...], wba_ref[...])
        b = _dot(yg_ref[...], wbg_ref[...])
        a_ref[...] = a.astype(BF)
        gb_ref[...] = b.astype(BF)
        merged = (gt_ref[:, :D].astype(F32) * a + gt_ref[:, D:].astype(F32) * b).astype(BF)
        mg_ref[...] = merged
        y = _dot(merged, wo_ref[...])
        y_ref[...] = y
        x2 = x_ref[...] + y * _rms_stats(y) * g2_ref[...]
        x2_ref[...] = x2
        h3_ref[...] = (x2 * _rms_stats(x2) * g3_ref[...]).astype(BF)

    row = lambda w: pl.BlockSpec((tm, w), lambda i: (i, 0))
    full = lambda s: pl.BlockSpec(s, lambda i: (0, 0))
    return pl.pallas_call(
        body, name="merge_fwd", grid=(S // tm,),
        out_shape=(jax.ShapeDtypeStruct((S, D), BF), jax.ShapeDtypeStruct((S, D), BF), jax.ShapeDtypeStruct((S, D), BF),
                   jax.ShapeDtypeStruct((S, D), F32), jax.ShapeDtypeStruct((S, D), F32), jax.ShapeDtypeStruct((S, D), BF)),
        in_specs=[row(GW), row(GMLP_W), row(2 * D), full((GW, D)), full((GMLP_W, D)), full((D, D)), row(D),
                  full((1, D)), full((1, D))],
        out_specs=(row(D), row(D), row(D), row(D), row(D), row(D)),
        compiler_params=_cp(("parallel",)))(ya, yg, gates, wba, wbg, wout, x, g2, g3)


def _resident(shape):
    return pl.BlockSpec(shape, lambda i: (0,) * len(shape), pipeline_mode=pl.Buffered(1))


def _mlp_fwd(h3, wmi, wmo, x2, tgt, g4):
    S = x2.shape[0]
    tm = min(MLP_TM_FWD, S)

    def body(h_ref, wi_ref, wo_ref, x2_ref, t_ref, g4_ref, a_ref, dy2_ref, dout_ref, loss_ref, dg4_ref):
        @pl.when(pl.program_id(0) == 0)
        def _():
            loss_ref[...] = jnp.zeros_like(loss_ref)
            dg4_ref[...] = jnp.zeros_like(dg4_ref)

        a = jnp.maximum(_dot(h_ref[...], wi_ref[...]), 0.0)
        a_ref[...] = a.astype(BF)
        y2 = _dot((a * a).astype(BF), wo_ref[...])
        r = _rms_stats(y2)
        out = x2_ref[...] + y2 * r * g4_ref[...]
        err = out - t_ref[...]
        tot = jnp.sum(jnp.sum(err * err, axis=1, keepdims=True), axis=0, keepdims=True) * (0.5 / D)
        lane = lax.broadcasted_iota(jnp.int32, (1, 128), 1)
        loss_ref[...] += jnp.where(lane == 0, tot, 0.0)
        dout = err * (1.0 / D)
        dout_ref[...] = dout
        dy2, dg = _rms_bwd(y2, r, g4_ref[...], dout)
        dy2_ref[...] = dy2.astype(BF)
        dg4_ref[...] += dg

    row = pl.BlockSpec((tm, D), lambda i: (i, 0))
    return pl.pallas_call(
        body, name="mlp_fwd", grid=(S // tm,),
        out_shape=(jax.ShapeDtypeStruct((S, DFF), BF), jax.ShapeDtypeStruct((S, D), BF), jax.ShapeDtypeStruct((S, D), F32),
                   jax.ShapeDtypeStruct((1, 128), F32), jax.ShapeDtypeStruct((1, D), F32)),
        in_specs=[row, _resident((D, DFF)), _resident((DFF, D)), row, row, pl.BlockSpec((1, D), lambda i: (0, 0))],
        out_specs=(pl.BlockSpec((tm, DFF), lambda i: (i, 0)), row, row,
                   pl.BlockSpec((1, 128), lambda i: (0, 0)), pl.BlockSpec((1, D), lambda i: (0, 0))),
        compiler_params=_cp(("arbitrary",)))(h3, wmi, wmo, x2, tgt, g4)


def _mlp_bwd(dy2, a, wmo, wmi, x2, y, dout, g2, g3):
    S = x2.shape[0]
    tm = min(MLP_TM_BWD, S)

    def body(dy2_ref, a_ref, wo_ref, wi_ref, x2_ref, y_ref, dout_ref, g2_ref, g3_ref,
             dpre_ref, dx2_ref, dy_ref, dg3_ref, dg2_ref):
        @pl.when(pl.program_id(0) == 0)
        def _():
            dg3_ref[...] = jnp.zeros_like(dg3_ref)
            dg2_ref[...] = jnp.zeros_like(dg2_ref)

        da2 = _dot_nt(dy2_ref[...], wo_ref[...])
        dpre = (2.0 * a_ref[...].astype(F32) * da2).astype(BF)
        dpre_ref[...] = dpre
        dh3 = _dot_nt(dpre, wi_ref[...])
        x2 = x2_ref[...]
        dx3, dg3 = _rms_bwd(x2, _rms_stats(x2), g3_ref[...], dh3)
        dx2 = dout_ref[...] + dx3
        dx2_ref[...] = dx2
        dg3_ref[...] += dg3
        yv = y_ref[...]
        dy, dg2 = _rms_bwd(yv, _rms_stats(yv), g2_ref[...], dx2)
        dy_ref[...] = dy.astype(BF)
        dg2_ref[...] += dg2

    row = pl.BlockSpec((tm, D), lambda i: (i, 0))
    wide = pl.BlockSpec((tm, DFF), lambda i: (i, 0))
    vec = pl.BlockSpec((1, D), lambda i: (0, 0))
    return pl.pallas_call(
        body, name="mlp_bwd", grid=(S // tm,),
        out_shape=(jax.ShapeDtypeStruct((S, DFF), BF), jax.ShapeDtypeStruct((S, D), F32), jax.ShapeDtypeStruct((S, D), BF),
                   jax.ShapeDtypeStruct((1, D), F32), jax.ShapeDtypeStruct((1, D), F32)),
        in_specs=[row, wide, _resident((DFF, D)), _resident((D, DFF)), row, row, row, vec, vec],
        out_specs=(wide, row, row, vec, vec),
        compiler_params=_cp(("arbitrary",)))(dy2, a, wmo, wmi, x2, y, dout, g2, g3)


def _mm_tn(a, b, name, square_a=False, tm=1024, tn=1024, tk=2048, tie=None):
    S, M = a.shape
    N = b.shape[1]
    tm, tk = min(tm, M), min(tk, S)
    tn = max(t for t in range(128, min(tn, N) + 1, 128) if N % t == 0)
    assert M % tm == 0 and S % tk == 0
    nk = S // tk
    ties = () if tie is None else (tie,)

    def body(a_ref, b_ref, *rest):
        o_ref = rest[-1]
        k = pl.program_id(2)
        av = a_ref[...]
        if square_a:
            av = av * av
        part = _dot_tn(av, b_ref[...])

        @pl.when(k == 0)
        def _():
            o_ref[...] = part

        @pl.when(k > 0)
        def _():
            o_ref[...] += part

    return pl.pallas_call(
        body, name=name, out_shape=jax.ShapeDtypeStruct((M, N), F32), grid=(M // tm, N // tn, nk),
        in_specs=[pl.BlockSpec((tk, tm), lambda i, j, k: (k, i)), pl.BlockSpec((tk, tn), lambda i, j, k: (k, j))]
        + [pl.BlockSpec(memory_space=pl.ANY)] * len(ties),
        out_specs=pl.BlockSpec((tm, tn), lambda i, j, k: (i, j)),
        compiler_params=_cp(("parallel", "parallel", "arbitrary")))(a, b, *ties)


def _outproj_bwd(dy, wout, abr, gbr, gates, wba, wbg, ya, tie):
    S = dy.shape[0]
    tm = min(512, S)

    def body(dy_ref, wo_ref, a_ref, b_ref, gt_ref, wba_ref, wbg_ref, ya_ref, tie_ref,
             dgt_ref, da_ref, db_ref, dyg_ref, e0, e1, e2, s0, s1, s2, scr):
        dm = _dot_nt(dy_ref[...], wo_ref[...])
        ga, gb = gt_ref[:, :D].astype(F32), gt_ref[:, D:].astype(F32)
        dgt_ref[:, :D] = (dm * a_ref[...].astype(F32) * ga * (1.0 - ga)).astype(BF)
        dgt_ref[:, D:] = (dm * b_ref[...].astype(F32) * gb * (1.0 - gb)).astype(BF)
        da = (dm * ga).astype(BF)
        db = (dm * gb).astype(BF)
        da_ref[...] = da
        db_ref[...] = db
        dyg_ref[...] = _dot_nt(db, wbg_ref[...]).astype(BF)
        dya = _dot_nt(da, wba_ref[...]).astype(BF).astype(F32)
        dyy = dya * ya_ref[...].astype(F32)
        dsum = jnp.zeros((tm, GW), F32)
        for hh in range(NSLOT):
            hm = _head_mask(hh)
            dsum = dsum + jnp.where(hm, jnp.sum(jnp.where(hm, dyy, 0.0), axis=1, keepdims=True), 0.0)
        for e_ref, s_ref, d in zip((e0, e1, e2), (s0, s1, s2), DILS):
            _dilate_store(dya, scr, e_ref, (), d)
            _dilate_store(dsum, scr, s_ref, (), d)

    row = lambda w: pl.BlockSpec((tm, w), lambda i: (i, 0))
    full = lambda s: pl.BlockSpec(s, lambda i: (0, 0))
    dil = lambda d: pl.BlockSpec((d, tm // d, GW), lambda i: (0, i, 0))
    dshape = lambda d, t: jax.ShapeDtypeStruct((d, S // d, GW), t)
    return pl.pallas_call(
        body, name="outproj_bwd", grid=(S // tm,),
        out_shape=(jax.ShapeDtypeStruct((S, 2 * D), BF), jax.ShapeDtypeStruct((S, D), BF), jax.ShapeDtypeStruct((S, D), BF),
                   jax.ShapeDtypeStruct((S, GMLP_W), BF)) + tuple(dshape(d, BF) for d in DILS)
        + tuple(dshape(d, F32) for d in DILS),
        in_specs=[row(D), full((D, D)), row(D), row(D), row(2 * D), full((GW, D)), full((GMLP_W, D)), row(GW),
                  pl.BlockSpec(memory_space=pl.ANY)],
        out_specs=(row(2 * D), row(D), row(D), row(GMLP_W)) + tuple(dil(d) for d in DILS) * 2,
        scratch_shapes=[pltpu.VMEM((2, tm, 128), F32)],
        compiler_params=_cp(("parallel",)))(dy, wout, abr, gbr, gates, wba, wbg, ya, tie)


def _gmlp_bwd(puz, dyg, ln_g, ln_b, w_sp, w_sp_t, b_sp_t):
    S = puz.shape[0]
    tm = min(512, S)
    nch = tm // CHUNK

    def body(p_ref, dy_ref, g_ref, b_ref, w_ref, wt_ref, bt_ref,
             dp_ref, dw_ref, dbs_ref, dg_ref, dbias_ref, dbacc_ref):
        i = pl.program_id(0)

        @pl.when(i == 0)
        def _():
            dw_ref[...] = jnp.zeros_like(dw_ref)
            dbacc_ref[...] = jnp.zeros_like(dbacc_ref)
            dg_ref[...] = jnp.zeros_like(dg_ref)
            dbias_ref[...] = jnp.zeros_like(dbias_ref)

        tril = _tril()
        ws = [jnp.where(tril, w_ref[gg], 0.0).astype(BF) for gg in range(4)]
        triu = _tril(upper=True)
        wts = [jnp.where(triu, wt_ref[gg], 0.0).astype(BF) for gg in range(4)]
        gain = g_ref[...]
        for ch in range(nch):
            rows = slice(ch * CHUNK, (ch + 1) * CHUNK)
            pz = p_ref[rows, GMLP_W:].astype(F32)
            z = _gelu(pz)
            zhat, rstd, zn = _ln_fwd(z, gain, b_ref[...])
            znb = zn.astype(BF)
            dzn_parts = []
            for gg in range(4):
                cols = slice(gg * CHUNK, (gg + 1) * CHUNK)
                pu = p_ref[rows, cols].astype(F32)
                u = _gelu(pu)
                sz = _dot(ws[gg], znb[:, cols]) + bt_ref[:, gg:gg + 1]
                dyv = dy_ref[rows, cols].astype(F32)
                dp_ref[rows, cols] = (dyv * sz * _gelu_grad(pu)).astype(BF)
                dsz = dyv * u
                dbacc_ref[gg] += dsz
                dszb = dsz.astype(BF)
                dw_ref[gg] += _dot_nt(dszb, znb[:, cols])
                dzn_parts.append(_dot(wts[gg], dszb))
            dzn = jnp.concatenate(dzn_parts, axis=1)
            dg_ref[...] += jnp.sum(dzn * zhat, axis=0, keepdims=True)
            dbias_ref[...] += jnp.sum(dzn, axis=0, keepdims=True)
            dzh = dzn * gain
            dz = rstd * (dzh - jnp.mean(dzh, axis=-1, keepdims=True)
                         - zhat * jnp.mean(dzh * zhat, axis=-1, keepdims=True))
            dp_ref[rows, GMLP_W:] = (dz * _gelu_grad(pz)).astype(BF)

        @pl.when(i == pl.num_programs(0) - 1)
        def _():
            for gg in range(4):
                dw_ref[gg] = jnp.where(tril, dw_ref[gg], 0.0)
                dbs_ref[gg] = jnp.sum(dbacc_ref[gg], axis=1, keepdims=True)

    full2 = lambda s: pl.BlockSpec(s, lambda i: (0, 0))
    full3 = lambda s: pl.BlockSpec(s, lambda i: (0, 0, 0))
    return pl.pallas_call(
        body, name="gmlp_bwd", grid=(S // tm,),
        out_shape=(jax.ShapeDtypeStruct((S, 2 * GMLP_W), BF), jax.ShapeDtypeStruct((4, CHUNK, CHUNK), F32),
                   jax.ShapeDtypeStruct((4, CHUNK, 1), F32), jax.ShapeDtypeStruct((1, GMLP_W), F32),
                   jax.ShapeDtypeStruct((1, GMLP_W), F32)),
        in_specs=[pl.BlockSpec((tm, 2 * GMLP_W), lambda i: (i, 0)), pl.BlockSpec((tm, GMLP_W), lambda i: (i, 0)),
                  full2((1, GMLP_W)), full2((1, GMLP_W)), full3((4, CHUNK, CHUNK)), full3((4, CHUNK, CHUNK)),
                  full2((CHUNK, 4))],
        out_specs=(pl.BlockSpec((tm, 2 * GMLP_W), lambda i: (i, 0)), full3((4, CHUNK, CHUNK)), full3((4, CHUNK, 1)),
                   full2((1, GMLP_W)), full2((1, GMLP_W))),
        scratch_shapes=[pltpu.VMEM((4, CHUNK, CHUNK), F32)],
        compiler_params=_cp(("arbitrary",)))(puz, dyg, ln_g, ln_b, w_sp, w_sp_t, b_sp_t)


def _attn_bwd(qkv, dya, dsums, lse, name):
    _, d, L, _ = qkv.shape
    nsub = min(ATTN_NSUB, L // QB)
    R = nsub * QB
    nsteps = L // R

    def body(q_ref, qn_ref, kp_ref, kc_ref, vp_ref, vc_ref, dy_ref, dyn_ref, e_ref, en_ref, l_ref, ln_ref, o_ref):
        i = pl.program_id(1)
        band, band_first = _band_masks(i == 0)
        row = lax.broadcasted_iota(jnp.int32, (QB, QB), 0)
        col = lax.broadcasted_iota(jnp.int32, (QB, QB), 1)
        mask_next = jnp.logical_and(col >= row, i < nsteps - 1)
        kc, vc = kc_ref[...], vc_ref[...]
        kfull = jnp.concatenate([kp_ref[...], kc], axis=0)
        vfull = jnp.concatenate([vp_ref[...], vc], axis=0)
        k_last, v_last = kc[(nsub - 1) * QB:], vc[(nsub - 1) * QB:]
        q_ext = jnp.concatenate([q_ref[...], qn_ref[...]], axis=0)
        dy_ext = jnp.concatenate([dy_ref[...], dyn_ref[...]], axis=0)
        esum, esum_n, lse, lse_n = e_ref[...], en_ref[...], l_ref[...], ln_ref[...]
        win = lambda t, sb: t[sb * QB:(sb + 2) * QB]
        blk = lambda t, sb: t[sb * QB:(sb + 1) * QB]
        hms = [_head_mask(hh) for hh in range(NSLOT)]
        q_hs = [jnp.where(hm, q_ext, 0) for hm in hms]
        dy_hs = [jnp.where(hm, dy_ext, 0) for hm in hms]
        raw = []
        for hh in range(NSLOT):
            tiles = [(_dot_nt(blk(q_hs[hh], sb), win(kfull, sb)), _dot_nt(blk(dy_hs[hh], sb), win(vfull, sb)))
                     for sb in range(nsub)]
            tiles.append((_dot_nt(blk(q_hs[hh], nsub), k_last), _dot_nt(blk(dy_hs[hh], nsub), v_last)))
            raw.append(tiles)
        ps, dss = [], []
        for hh in range(NSLOT):
            rowstat = lambda t: jnp.max(jnp.where(hms[hh], t, -jnp.inf), axis=1, keepdims=True)
            p_h, ds_h = [], []
            for sb in range(nsub + 1):
                sc, dp = raw[hh][sb]
                if sb < nsub:
                    msk, lrow, erow = (band_first if sb == 0 else band), rowstat(blk(lse, sb)), rowstat(blk(esum, sb))
                else:
                    msk, lrow, erow = mask_next, rowstat(lse_n), rowstat(esum_n)
                p = jnp.where(msk, jnp.exp(sc * SCALE - lrow), 0.0)
                p_h.append(p.astype(BF))
                ds_h.append((p * (dp - erow)).astype(BF))
            ps.append(p_h)
            dss.append(ds_h)
        dq = [jnp.zeros((QB, GW), F32) for _ in range(nsub)]
        dk = [jnp.zeros((QB, GW), F32) for _ in range(nsub)]
        dv = [jnp.zeros((QB, GW), F32) for _ in range(nsub)]
        for hh in range(NSLOT):
            for sb in range(nsub):
                dq[sb] = dq[sb] + jnp.where(hms[hh], _dot(dss[hh][sb], win(kfull, sb)), 0.0)
                nxt = lambda t: t[sb + 1][:, :QB] if sb + 1 < nsub else t[nsub]
                dk[sb] = dk[sb] + _dot_tn(jnp.concatenate([dss[hh][sb][:, QB:], nxt(dss[hh])], axis=0), win(q_hs[hh], sb))
                dv[sb] = dv[sb] + _dot_tn(jnp.concatenate([ps[hh][sb][:, QB:], nxt(ps[hh])], axis=0), win(dy_hs[hh], sb))
        for sb in range(nsub):
            rows = slice(sb * QB, (sb + 1) * QB)
            o_ref[0, rows, :] = (dq[sb] * SCALE).astype(BF)
            o_ref[1, rows, :] = (dk[sb] * SCALE).astype(BF)
            o_ref[2, rows, :] = dv[sb].astype(BF)

    prev = lambda i: jnp.maximum(i * nsub - 1, 0)
    nxt = lambda i: jnp.minimum((i + 1) * nsub, L // QB - 1)
    cur4 = lambda t: pl.BlockSpec((None, None, R, GW), lambda r, i: (t, r, i, 0))
    prv4 = lambda t: pl.BlockSpec((None, None, QB, GW), lambda r, i: (t, r, prev(i), 0))
    nxt4 = lambda t: pl.BlockSpec((None, None, QB, GW), lambda r, i: (t, r, nxt(i), 0))
    cur3 = pl.BlockSpec((None, R, GW), lambda r, i: (r, i, 0))
    nxt3 = pl.BlockSpec((None, QB, GW), lambda r, i: (r, nxt(i), 0))
    return pl.pallas_call(
        body, name=name, grid=(d, nsteps), out_shape=jax.ShapeDtypeStruct((3, d, L, GW), BF),
        in_specs=[cur4(0), nxt4(0), prv4(1), cur4(1), prv4(2), cur4(2), cur3, nxt3, cur3, nxt3, cur3, nxt3],
        out_specs=pl.BlockSpec((3, None, R, GW), lambda r, i: (0, r, i, 0)),
        compiler_params=_cp(("parallel", "arbitrary")))(qkv, qkv, qkv, qkv, qkv, qkv, dya, dya, dsums, dsums, lse, lse)


def _inproj_bwd(dqkvs, dpuz, dgates, wqkv, wuz, wg, cos_t, sin_t, x, dx2, g1):
    S = x.shape[0]
    tm = min(512, S)

    def body(d0_ref, d1_ref, d2_ref, dp_ref, dg_ref, wqkv_ref, wuz_ref, wg_ref, c_ref, s_ref,
             x_ref, dx2_ref, g1_ref, gx_ref, dg1_ref, dn_ref, scr):
        i = pl.program_id(0)

        @pl.when(i == 0)
        def _():
            dg1_ref[...] = jnp.zeros_like(dg1_ref)

        for t in range(3):
            for g, (d_ref, d) in enumerate(zip((d0_ref, d1_ref, d2_ref), DILS)):
                piece = _undilate_load(d_ref, (t,), d, scr, tm)
                if t < 2:
                    piece = _rope_bwd(piece, c_ref, s_ref)
                dn_ref[:, (3 * t + g) * GW:(3 * t + g + 1) * GW] = piece.astype(BF)
        dh = (_dot_nt(dn_ref[...], wqkv_ref[...]) + _dot_nt(dp_ref[...], wuz_ref[...])
              + _dot_nt(dg_ref[...], wg_ref[...]))
        xv = x_ref[...]
        dx1, dg1 = _rms_bwd(xv, _rms_stats(xv), g1_ref[...], dh)
        gx_ref[...] = dx2_ref[...] + dx1
        dg1_ref[...] += dg1

    row = lambda w: pl.BlockSpec((tm, w), lambda i: (i, 0))
    full = lambda s: pl.BlockSpec(s, lambda i: (0, 0))
    dil = lambda d: pl.BlockSpec((3, d, tm // d, GW), lambda i: (0, 0, i, 0))
    return pl.pallas_call(
        body, name="inproj_bwd", grid=(S // tm,),
        out_shape=(jax.ShapeDtypeStruct((S, D), F32), jax.ShapeDtypeStruct((1, D), F32),
                   jax.ShapeDtypeStruct((S, 3 * AW), BF)),
        in_specs=[dil(d) for d in DILS] + [row(2 * GMLP_W), row(2 * D), full((D, 3 * AW)), full((D, 2 * GMLP_W)),
                                            full((D, 2 * D)), row(128), row(128), row(D), row(D), full((1, D))],
        out_specs=(row(D), full((1, D)), row(3 * AW)),
        scratch_shapes=[pltpu.VMEM((2, tm, 128), F32)],
        compiler_params=_cp(("arbitrary",)))(*dqkvs, dpuz, dgates, wqkv, wuz, wg, cos_t, sin_t, x, dx2, g1)


def _adam_math(w, g, m, v):
    m2 = ADAM_B1 * m + (1.0 - ADAM_B1) * g
    v2 = ADAM_B2 * v + (1.0 - ADAM_B2) * (g * g)
    m_hat = m2 / (1.0 - ADAM_B1 ** ADAM_STEP)
    v_hat = v2 / (1.0 - ADAM_B2 ** ADAM_STEP)
    delta = -ADAM_LR * (m_hat / (jnp.sqrt(v_hat) + ADAM_EPS) + ADAM_WD * w)
    return delta, m2, v2


def _adam_shard(own, recv, w, m, v, name):
    R, C = w.shape
    tr = min(256, R)

    def body(own_ref, r_ref, w_ref, m_ref, v_ref, g_ref, d_ref, m2_ref, v2_ref):
        g = own_ref[...] + r_ref[0].astype(F32) + r_ref[1].astype(F32) + r_ref[2].astype(F32)
        g_ref[...] = g
        d_ref[...], m2_ref[...], v2_ref[...] = _adam_math(w_ref[...], g, m_ref[...], v_ref[...])

    spec = pl.BlockSpec((tr, C), lambda i: (i, 0))
    out = jax.ShapeDtypeStruct((R, C), F32)
    return pl.pallas_call(
        body, name=name, grid=(R // tr,), out_shape=(out, out, out, out),
        in_specs=[spec, pl.BlockSpec((3, tr, C), lambda i: (0, i, 0)), spec, spec, spec],
        out_specs=(spec, spec, spec, spec), compiler_params=_cp(("parallel",)))(own, recv, w, m, v)


def _rs_add(gblocks, recv, idx, name):
    _, R, C = gblocks.shape
    tr = min(256, R)

    def body(t_ref, g_ref, r_ref, own_ref, send_ref):
        j = pl.program_id(1)
        s = g_ref[...] + r_ref[...]

        @pl.when(j == 0)
        def _():
            own_ref[...] = s

        @pl.when(j > 0)
        def _():
            send_ref[...] = s.astype(BF)

    grid_spec = pltpu.PrefetchScalarGridSpec(
        num_scalar_prefetch=1, grid=(R // tr, 4),
        in_specs=[pl.BlockSpec((None, tr, C), lambda i, j, t: (t[j], i, 0)),
                  pl.BlockSpec((None, tr, C), lambda i, j, t: (t[4 + j], i, 0))],
        out_specs=[pl.BlockSpec((tr, C), lambda i, j, t: (i, 0)),
                   pl.BlockSpec((None, tr, C), lambda i, j, t: (jnp.maximum(j - 1, 0), i, 0))])
    return pl.pallas_call(
        body, name=name, grid_spec=grid_spec,
        out_shape=(jax.ShapeDtypeStruct((R, C), F32), jax.ShapeDtypeStruct((3, R, C), BF)),
        compiler_params=_cp(("parallel", "arbitrary")))(idx, gblocks, recv)


def _mesh_pos():
    return lax.axis_index("x"), lax.axis_index("y"), lax.axis_index("c")


def _all_gather_weights(shards):
    n = len(shards)

    def body(*refs):
        ins, outs = refs[:n], refs[n:2 * n]
        send_sems, recv_sems, loc_sems = refs[2 * n:]
        x, y, c = _mesh_pos()
        me, sib = (x, y, c), (x, y, 1 - c)
        chips = [(1 - x, y), (x, 1 - y), (1 - x, 1 - y)]

        def rcopy(k, s, block, to, src=None):
            dst = outs[k].at[4 * block[0] + 2 * block[1] + block[2]]
            return pltpu.make_async_remote_copy(
                src_ref=dst if src is None else src, dst_ref=dst, send_sem=send_sems.at[k, s],
                recv_sem=recv_sems.at[k, s], device_id=to, device_id_type=MESH)

        locs = [pltpu.make_async_copy(ins[k], outs[k].at[4 * x + 2 * y + c], loc_sems.at[k]) for k in range(n)]
        for cp in locs:
            cp.start()
        started = []
        for k in range(n):
            for j, chip in enumerate(chips):
                started.append(rcopy(k, 1 + j, me, (*chip, c), src=ins[k]))
                started[-1].start()
        for k in range(n):
            started.append(rcopy(k, 0, me, sib, src=ins[k]))
            started[-1].start()
        for k in range(n):
            for j, chip in enumerate(chips):
                rcopy(k, 1 + j, (*chip, c), me).wait_recv()
                started.append(rcopy(k, 4 + j, (*chip, c), sib))
                started[-1].start()
        for k in range(n):
            rcopy(k, 0, sib, me).wait_recv()
            for j, chip in enumerate(chips):
                rcopy(k, 4 + j, (*chip, 1 - c), me).wait_recv()
        for cp in started:
            cp.wait_send()
        for cp in locs:
            cp.wait()

    hbm = pl.BlockSpec(memory_space=pl.ANY)
    return pl.pallas_call(
        body, name="ag_weights",
        out_shape=tuple(jax.ShapeDtypeStruct((8,) + s.shape, s.dtype) for s in shards),
        in_specs=[hbm] * n, out_specs=(hbm,) * n,
        scratch_shapes=[pltpu.SemaphoreType.DMA((n, 7)), pltpu.SemaphoreType.DMA((n, 7)),
                        pltpu.SemaphoreType.DMA((n,))])(*shards)


_HBM = pl.BlockSpec(memory_space=pltpu.HBM)
_SEM = pl.BlockSpec(memory_space=pltpu.SEMAPHORE)
_EFFECT = pltpu.SideEffectType.DATAFLOW_SIDE_EFFECTING
_RELATIONS = [(dx, dy, dc) for dx in (0, 1) for dy in (0, 1) for dc in (0, 1)][1:]


def _flip(v, d):
    return 1 - v if d else v


def _plan_gather(n):
    def plan(x, y, c):
        return [(k, None, 4 * x + 2 * y + c, (_flip(x, dx), _flip(y, dy), _flip(c, dc)))
                for k in range(n) for dx, dy, dc in _RELATIONS]
    return plan


def _plan_d2d(n):
    def plan(x, y, c):
        return [(k, 2 * kk + 1 - c, kk, (x, y, 1 - c)) for k in range(n) for kk in range(4)]
    return plan


def _plan_ici(n):
    def plan(x, y, c):
        return [(k, j, j, (_flip(x, dx), _flip(y, dy), c))
                for k in range(n) for j, (dx, dy) in enumerate(((1, 0), (0, 1), (1, 1)))]
    return plan


def _plan_copies(plan, src_refs, land_refs, send_sems, recv_sems):
    x, y, c = _mesh_pos()
    return [pltpu.make_async_remote_copy(
        src_ref=src_refs[k] if si is None else src_refs[k].at[si], dst_ref=land_refs[k].at[di],
        send_sem=send_sems.at[n], recv_sem=recv_sems.at[n], device_id=dev, device_id_type=MESH)
        for n, (k, si, di, dev) in enumerate(plan(x, y, c))]


def _exchange_start(srcs, land_shapes, plan, ncopies, name, after):
    n = len(srcs)

    def body(*refs):
        src_refs, land_refs = refs[:n], refs[n:2 * n]
        send_sems, recv_sems = refs[2 * n + len(after)], refs[2 * n + len(after) + 1]
        token = refs[-1]
        for cp in _plan_copies(plan, src_refs, land_refs, send_sems, recv_sems):
            cp.start()
        token[...] = jnp.zeros_like(token)

    lands = [pltpu.with_memory_space_constraint(lax.empty(s, a.dtype), pltpu.HBM) for s, a in zip(land_shapes, srcs)]
    srcs = [pltpu.with_memory_space_constraint(a, pltpu.HBM) for a in srcs]
    outs = pl.pallas_call(
        body, name=name,
        out_shape=(pltpu.SemaphoreType.DMA((ncopies,)), pltpu.SemaphoreType.DMA((ncopies,)))
        + tuple(pltpu.HBM(a.shape, a.dtype) for a in srcs) + tuple(pltpu.HBM(a.shape, a.dtype) for a in lands)
        + (jax.ShapeDtypeStruct((8, 128), F32),),
        in_specs=[_HBM] * (2 * n) + [pl.BlockSpec(memory_space=pl.ANY)] * len(after),
        out_specs=(_SEM, _SEM) + (_HBM,) * (2 * n) + (pl.BlockSpec(memory_space=pltpu.VMEM),),
        input_output_aliases={i: 2 + i for i in range(2 * n)},
        compiler_params=pltpu.CompilerParams(has_side_effects=_EFFECT))(*srcs, *lands, *after)
    return (outs[0], outs[1], list(outs[2:2 + n]), list(outs[2 + n:2 + 2 * n])), outs[-1]


def _exchange_wait(handle, plan, name, after):
    send_sems, recv_sems, srcs, lands = handle
    n = len(srcs)

    def body(*refs):
        src_refs, land_refs = refs[:n], refs[n:2 * n]
        for cp in _plan_copies(plan, src_refs, land_refs, refs[2 * n], refs[2 * n + 1]):
            cp.wait_send()
            cp.wait_recv()

    outs = pl.pallas_call(
        body, name=name,
        out_shape=tuple(pltpu.HBM(a.shape, a.dtype) for a in srcs) + tuple(pltpu.HBM(a.shape, a.dtype) for a in lands),
        in_specs=[_HBM] * (2 * n) + [_SEM, _SEM] + [pl.BlockSpec(memory_space=pl.ANY)] * len(after),
        out_specs=(_HBM,) * (2 * n), input_output_aliases={i: i for i in range(2 * n)},
        compiler_params=pltpu.CompilerParams(has_side_effects=_EFFECT))(*srcs, *lands, send_sems, recv_sems, *after)
    return list(outs[:n]), list(outs[n:])


SMALL = ("g1", "g2", "g3", "g4", "ln_g", "ln_b", "b_sp", "w_sp")


def _small_step(loss_row, grads, ws, ms, vs):
    parts = [loss_row] + [grads[k] for k in SMALL]
    n, ns = len(parts), len(SMALL)

    def body(*refs):
        p_refs = refs[:n]
        w_refs, m_refs, v_refs = (refs[n + i * ns:n + (i + 1) * ns] for i in range(3))
        o = n + 3 * ns
        loss_ref = refs[o]
        g_out, d_out, m_out, v_out = (refs[o + 1 + i * ns:o + 1 + (i + 1) * ns] for i in range(4))
        bufs = refs[o + 1 + 4 * ns:o + 1 + 4 * ns + n]
        send_sems, recv_sems = refs[-2:]
        x, y, c = _mesh_pos()
        me, sib = (x, y, c), (x, y, 1 - c)
        chips = [(1 - x, y), (x, 1 - y), (1 - x, 1 - y)]

        def rcopy(k, s, block, to, src=None):
            dst = bufs[k].at[4 * block[0] + 2 * block[1] + block[2]]
            return pltpu.make_async_remote_copy(
                src_ref=dst if src is None else src, dst_ref=dst, send_sem=send_sems.at[k, s],
                recv_sem=recv_sems.at[k, s], device_id=to, device_id_type=MESH)

        started = []
        for k in range(n):
            bufs[k][4 * x + 2 * y + c] = p_refs[k][...]
            started.append(rcopy(k, 0, me, sib, src=p_refs[k]))
            started += [rcopy(k, 1 + j, me, (*chip, c), src=p_refs[k]) for j, chip in enumerate(chips)]
        for cp in started:
            cp.start()
        for k in range(n):
            for j, chip in enumerate(chips):
                rcopy(k, 1 + j, (*chip, c), me).wait_recv()
                started.append(rcopy(k, 4 + j, (*chip, c), sib))
                started[-1].start()
        for k in range(n):
            rcopy(k, 0, sib, me).wait_recv()
            for j, chip in enumerate(chips):
                rcopy(k, 4 + j, (*chip, 1 - c), me).wait_recv()
        for cp in started:
            cp.wait_send()
        sums = []
        for k in range(n):
            acc = bufs[k][0]
            for b in range(1, 8):
                acc = acc + bufs[k][b]
            sums.append(acc)
        loss_ref[...] = sums[0]
        for i in range(ns):
            g_out[i][...] = sums[1 + i]
            d_out[i][...], m_out[i][...], v_out[i][...] = _adam_math(w_refs[i][...], sums[1 + i], m_refs[i][...],
                                                                     v_refs[i][...])

    args = parts + [t[k] for t in (ws, ms, vs) for k in SMALL]
    shapes = [jax.ShapeDtypeStruct(loss_row.shape, F32)] + [jax.ShapeDtypeStruct(grads[k].shape, F32) for k in SMALL] * 4
    vmem = pl.BlockSpec(memory_space=pltpu.VMEM)
    outs = pl.pallas_call(
        body, name="small_step", out_shape=tuple(shapes), in_specs=[vmem] * len(args), out_specs=(vmem,) * len(shapes),
        scratch_shapes=[pltpu.VMEM((8,) + p.shape, F32) for p in parts]
        + [pltpu.SemaphoreType.DMA((n, 7)), pltpu.SemaphoreType.DMA((n, 7))],
        compiler_params=pltpu.CompilerParams(vmem_limit_bytes=VMEM_LIMIT))(*args)
    groups = [dict(zip(SMALL, outs[1 + i * ns:1 + (i + 1) * ns])) for i in range(4)]
    return (outs[0], *groups)


def _rope_tables(S):
    half = HD // 2
    inv_freq = ROPE_THETA ** (-jnp.arange(half, dtype=F32) / half)
    ang = jnp.arange(S, dtype=F32)[:, None] * inv_freq[None, :]
    cos, sin = jnp.cos(ang), jnp.sin(ang)
    return jnp.concatenate([cos, cos, cos, cos], axis=1), jnp.concatenate([-sin, sin, -sin, sin], axis=1)


def _to_blocks(g, col_sharded):
    if col_sharded:
        return g.reshape(g.shape[0], 8, g.shape[1] // 8).transpose(1, 0, 2)
    return g.reshape(8, g.shape[0] // 8, g.shape[1])


def _from_blocks(t, col_sharded):
    if col_sharded:
        return t.transpose(1, 0, 2).reshape(t.shape[1], 8 * t.shape[2])
    return t.reshape(8 * t.shape[1], t.shape[2])


class _NoComm:
    def __init__(self, late_weights):
        self._late = late_weights
        self.grads = {}

    def start_token(self):
        return 0.0

    def late_weights(self, after):
        return self._late

    def rs_start(self, key, grads, col_sharded):
        self.grads[key] = grads
        return jnp.zeros((8, 128), F32)

    def rs_mid(self, key, after):
        return jnp.zeros((8, 128), F32)


class _FsdpComm:
    def __init__(self, late_shards, col_sharded, after, idx, block):
        self._col, self._idx, self._block, self._rs = col_sharded, idx, block, {}
        n = len(late_shards)
        self._gather, self._token = _exchange_start(
            late_shards, [(8,) + s.shape for s in late_shards], _plan_gather(n), 7 * n, "ag_late_start", (after,))

    def start_token(self):
        return self._token[0, 0]

    def late_weights(self, after):
        shards, lands = _exchange_wait(self._gather, _plan_gather(len(self._col)), "ag_late_wait", after)
        lands = [lax.dynamic_update_index_in_dim(t, s, self._block, 0) for t, s in zip(lands, shards)]
        return [_from_blocks(t, cs) for t, cs in zip(lands, self._col)]

    def rs_start(self, key, grads, col_sharded):
        gblocks = [_to_blocks(g, cs) for g, cs in zip(grads, col_sharded)]
        n = len(gblocks)
        d2d, token = _exchange_start(gblocks, [(4,) + g.shape[1:] for g in gblocks], _plan_d2d(n), 4 * n,
                                     "rs_%s_d2d_start" % key, (self._token,))
        self._rs[key] = dict(n=n, d2d=d2d)
        return token

    def rs_mid(self, key, after):
        st = self._rs[key]
        gblocks, from_sib = _exchange_wait(st["d2d"], _plan_d2d(st["n"]), "rs_%s_d2d_wait" % key, after)
        halves = [_rs_add(g, r, self._idx, "rs_add_%s_%d" % (key, k)) for k, (g, r) in enumerate(zip(gblocks, from_sib))]
        st["own"] = [own for own, _ in halves]
        sends = [send for _, send in halves]
        st["ici"], token = _exchange_start(sends, [t.shape for t in sends], _plan_ici(st["n"]), 3 * st["n"],
                                           "rs_%s_ici_start" % key, (self._token,))
        return token

    def rs_end(self, key, after):
        st = self._rs[key]
        return st["own"], _exchange_wait(st["ici"], _plan_ici(st["n"]), "rs_%s_ici_wait" % key, after)[1]


def _local_step(x, tgt, wqkv, wuz, wg, comm, g1, g2, g3, g4, ln_g, ln_b, w_sp, b_sp):
    S = x.shape[0]
    cos_t, sin_t = _rope_tables(S)
    b_sp_t = b_sp.T
    w_sp_t = w_sp.transpose(0, 2, 1)

    h1 = _rms_fwd(x, g1 + comm.start_token(), "rms_pre_mix")
    qkvs = [_proj_qkv(h1, wqkv, cos_t, sin_t, g, d, "proj_qkv_d%d" % d) for g, d in enumerate(DILS)]
    puz = _proj(h1, wuz, "none", "proj_uz", 1024)
    gates = _proj(h1, wg, "sigmoid", "proj_gates", 512)
    fwd = [_attn_fwd(t, "attn_fwd_d%d" % d) for t, d in zip(qkvs, DILS)]
    ya, *lses = _attn_combine([o for o, _ in fwd], [l for _, l in fwd])
    yg = _gmlp_fwd(puz, ln_g, ln_b, w_sp, b_sp_t)
    wba, wbg, wout, wmi, wmo = comm.late_weights(after=(ya, yg, gates))
    abr, gbr, merged, y, x2, h3 = _merge_fwd(ya, yg, gates, wba, wbg, wout, x, g2, g3)
    a, dy2, dout, loss_row, dg4 = _mlp_fwd(h3, wmi, wmo, x2, tgt, g4)

    dpre, dx2, dy, dg3, dg2 = _mlp_bwd(dy2, a, wmo, wmi, x2, y, dout, g2, g3)
    dwmo = _mm_tn(a, dy2, "dw_mlp_out", square_a=True)
    dwmi = _mm_tn(h3, dpre, "dw_mlp_in")
    tie = comm.rs_start("mlp", [dwmi, dwmo], [True, False])
    dgates, da, db, dyg, *rest = _outproj_bwd(dy, wout, abr, gbr, gates, wba, wbg, ya, tie)
    dyas, dsums = rest[:3], rest[3:]
    dwout = _mm_tn(merged, dy, "dw_out")
    dwba = _mm_tn(ya, da, "dw_branch_attn")
    dwbg = _mm_tn(yg, db, "dw_branch_gmlp")
    tie = comm.rs_mid("mlp", after=(dyg,))
    tie2 = comm.rs_start("mid", [dwba, dwbg, dwout], [True, True, False])
    dpuz, dwsp, dbs, dlng, dlnb = _gmlp_bwd(puz, dyg, ln_g + (tie[0, 0] + tie2[0, 0]), ln_b, w_sp, w_sp_t, b_sp_t)
    dqkvs = [_attn_bwd(qkvs[g], dyas[g], dsums[g], lses[g], "attn_bwd_d%d" % d) for g, d in enumerate(DILS)]
    tie = comm.rs_mid("mid", after=tuple(dqkvs))
    grad_x, dg1, dqkv = _inproj_bwd(dqkvs, dpuz, dgates, wqkv, wuz, wg, cos_t, sin_t, x, dx2, g1 + tie[0, 0])
    dwin = jnp.concatenate([_mm_tn(h1, t, "dw_in_%d" % n) for n, t in enumerate((dqkv, dpuz, dgates))], axis=1)
    small = dict(g1=dg1, g2=dg2, g3=dg3, g4=dg4, ln_g=dlng, ln_b=dlnb, b_sp=dbs.reshape(4, CHUNK),
                 w_sp=dwsp.reshape(4 * CHUNK, CHUNK))
    return loss_row, grad_x, dwin, small


def kernel(x, norm_pre_mix, w_in, w_spatial, b_spatial, ln_v_gain, ln_v_bias, w_branch_attn, w_branch_gmlp, w_out, norm_post_mix, norm_pre_mlp, w_mlp_in, w_mlp_out, norm_post_mlp, loss_target, m_norm_pre_mix, m_w_in, m_w_spatial, m_b_spatial, m_ln_v_gain, m_ln_v_bias, m_w_branch_attn, m_w_branch_gmlp, m_w_out, m_norm_post_mix, m_norm_pre_mlp, m_w_mlp_in, m_w_mlp_out, m_norm_post_mlp, v_norm_pre_mix, v_w_in, v_w_spatial, v_b_spatial, v_ln_v_gain, v_ln_v_bias, v_w_branch_attn, v_w_branch_gmlp, v_w_out, v_norm_post_mix, v_norm_pre_mlp, v_w_mlp_in, v_w_mlp_out, v_norm_post_mlp):
    mx, my, mc = _mesh_pos()
    rel = [(0, 0), (1, 0), (0, 1), (1, 1)]
    chip_of = [2 * (mx ^ dx) + (my ^ dy) for dx, dy in rel]
    idx = jnp.stack([2 * k + mc for k in chip_of] + chip_of).astype(jnp.int32)

    (g_win,) = _all_gather_weights([w_in[0].astype(BF)])
    win = _from_blocks(g_win, True)
    wqkv, wuz, wg = win[:, :3 * AW], win[:, 3 * AW:3 * AW + 2 * GMLP_W], win[:, 3 * AW + 2 * GMLP_W:]
    late = [w_branch_attn[0], w_branch_gmlp[0], w_out[0], w_mlp_in[0], w_mlp_out[0]]
    comm = _FsdpComm([w.astype(BF) for w in late], [True, True, False, True, False], g_win, idx, 4 * mx + 2 * my + mc)

    loss_row, grad_x, dwin, small = _local_step(
        x[0], loss_target[0], wqkv, wuz, wg, comm,
        norm_pre_mix, norm_post_mix, norm_pre_mlp, norm_post_mlp, ln_v_gain, ln_v_bias, w_spatial[0], b_spatial[0])

    tie = comm.rs_start("win", [dwin], [True])
    flat = lambda t: t.reshape(-1, t.shape[-1])
    small_w = dict(zip(SMALL, map(flat, (norm_pre_mix, norm_post_mix, norm_pre_mlp, norm_post_mlp, ln_v_gain, ln_v_bias,
                                         b_spatial, w_spatial))))
    small_m = dict(zip(SMALL, map(flat, (m_norm_pre_mix, m_norm_post_mix, m_norm_pre_mlp, m_norm_post_mlp, m_ln_v_gain,
                                         m_ln_v_bias, m_b_spatial, m_w_spatial))))
    small_v = dict(zip(SMALL, map(flat, (v_norm_pre_mix, v_norm_post_mix, v_norm_pre_mlp, v_norm_post_mlp, v_ln_v_gain,
                                         v_ln_v_bias, v_b_spatial, v_w_spatial))))
    loss_out, sg, sd, sm, sv = _small_step(loss_row + tie[:1], small, small_w, small_m, small_v)
    tie = comm.rs_mid("win", after=(loss_out,))
    mlp_own, mlp_chips = comm.rs_end("mlp", after=(tie,))
    mid_own, mid_chips = comm.rs_end("mid", after=(tie,))
    adam = lambda nm, own, r, w, m, v: _adam_shard(own, r, w[0], m[0], v[0], "adam_" + nm)
    upd = {
        "w_branch_attn": adam("w_branch_attn", mid_own[0], mid_chips[0], w_branch_attn, m_w_branch_attn, v_w_branch_attn),
        "w_branch_gmlp": adam("w_branch_gmlp", mid_own[1], mid_chips[1], w_branch_gmlp, m_w_branch_gmlp, v_w_branch_gmlp),
        "w_out": adam("w_out", mid_own[2], mid_chips[2], w_out, m_w_out, v_w_out),
        "w_mlp_in": adam("w_mlp_in", mlp_own[0], mlp_chips[0], w_mlp_in, m_w_mlp_in, v_w_mlp_in),
        "w_mlp_out": adam("w_mlp_out", mlp_own[1], mlp_chips[1], w_mlp_out, m_w_mlp_out, v_w_mlp_out),
    }
    loss = loss_out[0, 0]
    win_own, win_chips = comm.rs_end("win", after=(upd["w_mlp_in"][0], upd["w_mlp_out"][0], upd["w_out"][0]))
    upd["w_in"] = adam("w_in", win_own[0], win_chips[0], w_in, m_w_in, v_w_in)
    order = ["g1", "w_in", "w_sp", "b_sp", "ln_g", "ln_b", "w_branch_attn", "w_branch_gmlp", "w_out", "g2", "g3",
             "w_mlp_in", "w_mlp_out", "g4"]
    small_shape = dict(g1=norm_pre_mix.shape, g2=norm_post_mix.shape, g3=norm_pre_mlp.shape, g4=norm_post_mlp.shape,
                       ln_g=ln_v_gain.shape, ln_b=ln_v_bias.shape, b_sp=b_spatial.shape, w_sp=w_spatial.shape)

    def pick(which):
        return [upd[nm][which][None] if nm in upd else (sg, sd, sm, sv)[which][nm].reshape(small_shape[nm])
                for nm in order]

    return (loss, grad_x[None], *pick(0), *pick(1), *pick(2), *pick(3))
```

```python
import functools
import math

import jax
import jax.numpy as jnp
from jax import lax
from jax.experimental import pallas as pl
from jax.experimental.pallas import tpu as pltpu

D = 1024
HD = 64
NSLOT = 4
GW = NSLOT * HD
DILS = (1, 4, 16)
QB = 128
ATTN_NSUB = 4
MLP_TM_FWD = 512
MLP_TM_BWD = 256
AW = 3 * GW
GMLP_W = 512
CHUNK = 128
DFF = 4096
EPS = 1e-6
ROPE_THETA = 10000.0
SCALE = HD ** -0.5
NEG = -1e30

ADAM_LR = 0.001
ADAM_B1 = 0.9
ADAM_B2 = 0.999
ADAM_EPS = 1e-08
ADAM_WD = 0.01
ADAM_STEP = 10

BF = jnp.bfloat16
F32 = jnp.float32
MESH = pl.DeviceIdType.MESH
VMEM_LIMIT = 56 * 1024 * 1024


def _cp(sem):
    return pltpu.CompilerParams(dimension_semantics=sem, vmem_limit_bytes=VMEM_LIMIT)


def _dot(a, b):
    return jnp.dot(a, b, preferred_element_type=F32)


def _dot_nt(a, b):
    return lax.dot_general(a, b, (((1,), (1,)), ((), ())), preferred_element_type=F32)


def _dot_tn(a, b):
    return lax.dot_general(a, b, (((0,), (0,)), ((), ())), preferred_element_type=F32)


def _gelu(x):
    return jax.nn.gelu(x, approximate=True)


def _gelu_grad(x):
    k = math.sqrt(2.0 / math.pi)
    t = jnp.tanh(k * (x + 0.044715 * x * x * x))
    return 0.5 * (1.0 + t) + 0.5 * x * (1.0 - t * t) * (k * (1.0 + 3.0 * 0.044715 * x * x))


def _swap_halves(t):
    w = t.shape[1]
    lane = lax.broadcasted_iota(jnp.int32, t.shape, 1)
    first = (lane & (HD - 1)) < (HD // 2)
    return jnp.where(first, pltpu.roll(t, w - HD // 2, 1), pltpu.roll(t, HD // 2, 1))


def _head_mask(hh):
    lane = lax.broadcasted_iota(jnp.int32, (1, GW), 1)
    return jnp.logical_and(lane >= hh * HD, lane < (hh + 1) * HD)


def _rms_stats(xf):
    return lax.rsqrt(jnp.mean(xf * xf, axis=-1, keepdims=True) + EPS)


def _rms_bwd(xf, r, gain, dout):
    n = xf * r
    t = dout * gain
    dx = r * (t - n * jnp.mean(t * n, axis=-1, keepdims=True))
    return dx, jnp.sum(dout * n, axis=0, keepdims=True)


def _rms_fwd(x, gain, name):
    S = x.shape[0]
    tm = min(512, S)

    def body(x_ref, g_ref, h_ref):
        xf = x_ref[...]
        h_ref[...] = (xf * _rms_stats(xf) * g_ref[...]).astype(BF)

    return pl.pallas_call(
        body, name=name, out_shape=jax.ShapeDtypeStruct((S, D), BF), grid=(S // tm,),
        in_specs=[pl.BlockSpec((tm, D), lambda i: (i, 0)), pl.BlockSpec((1, D), lambda i: (0, 0))],
        out_specs=pl.BlockSpec((tm, D), lambda i: (i, 0)), compiler_params=_cp(("parallel",)))(x, gain)


def _proj(h, w, epi, name, tm):
    S, K = h.shape
    N = w.shape[1]
    tm = min(tm, S)

    def body(h_ref, w_ref, o_ref):
        y = _dot(h_ref[...], w_ref[...])
        if epi == "sigmoid":
            y = jax.nn.sigmoid(y)
        o_ref[...] = y.astype(BF)

    return pl.pallas_call(
        body, name=name, out_shape=jax.ShapeDtypeStruct((S, N), BF), grid=(S // tm,),
        in_specs=[pl.BlockSpec((tm, K), lambda i: (i, 0)), _resident((K, N))],
        out_specs=pl.BlockSpec((tm, N), lambda i: (i, 0)),
        compiler_params=_cp(("parallel",)))(h, w)


def _dilate_store(val, scr, o_ref, lead, d):
    rows = val.shape[0]
    if d == 1:
        o_ref[lead + (0,)] = val.astype(o_ref.dtype)
        return
    for hf in range(2):
        scr[hf, pl.ds(0, rows), :] = val[:, hf * 128:(hf + 1) * 128]
    for r in range(d):
        for hf in range(2):
            o_ref[lead + (r, slice(None), slice(hf * 128, (hf + 1) * 128))] = (
                scr[hf, pl.ds(r, rows // d, stride=d), :].astype(o_ref.dtype))


def _undilate_load(i_ref, lead, d, scr, rows):
    if d == 1:
        return i_ref[lead + (0,)].astype(F32)
    for r in range(d):
        for hf in range(2):
            scr[hf, pl.ds(r, rows // d, stride=d), :] = (
                i_ref[lead + (r, slice(None), slice(hf * 128, (hf + 1) * 128))].astype(F32))
    return jnp.concatenate([scr[0, pl.ds(0, rows), :], scr[1, pl.ds(0, rows), :]], axis=1)


def _rope_fwd(y, c_ref, s_ref):
    cosv = jnp.concatenate([c_ref[...]] * 2, axis=1)
    sinv = jnp.concatenate([s_ref[...]] * 2, axis=1)
    return y * cosv + _swap_halves(y) * sinv


def _rope_bwd(dy, c_ref, s_ref):
    cosv = jnp.concatenate([c_ref[...]] * 2, axis=1)
    sinv = jnp.concatenate([s_ref[...]] * 2, axis=1)
    return dy * cosv + _swap_halves(dy * sinv)


def _proj_qkv(h, wqkv, cos_t, sin_t, g, d, name):
    S, K = h.shape
    tm = min(2048, S)

    def body(h_ref, w_ref, c_ref, s_ref, o_ref, scr):
        j = pl.program_id(1)
        y = _dot(h_ref[...], w_ref[...])

        @pl.when(j < 2)
        def _():
            _dilate_store(_rope_fwd(y, c_ref, s_ref), scr, o_ref, (), d)

        @pl.when(j == 2)
        def _():
            _dilate_store(y, scr, o_ref, (), d)

    return pl.pallas_call(
        body, name=name, out_shape=jax.ShapeDtypeStruct((3, d, S // d, GW), BF), grid=(S // tm, 3),
        in_specs=[pl.BlockSpec((tm, K), lambda i, j: (i, 0)), pl.BlockSpec((K, GW), lambda i, j: (0, 3 * j + g)),
                  pl.BlockSpec((tm, 128), lambda i, j: (i, 0)), pl.BlockSpec((tm, 128), lambda i, j: (i, 0))],
        out_specs=pl.BlockSpec((None, d, tm // d, GW), lambda i, j: (j, 0, i, 0)),
        scratch_shapes=[pltpu.VMEM((2, tm, 128), F32)],
        compiler_params=_cp(("parallel", "arbitrary")))(h, wqkv, cos_t, sin_t)


def _band_masks(first_step):
    row = lax.broadcasted_iota(jnp.int32, (QB, 2 * QB), 0)
    col = lax.broadcasted_iota(jnp.int32, (QB, 2 * QB), 1)
    band = jnp.logical_and(col >= row, col <= row + QB)
    return band, jnp.logical_and(band, jnp.logical_or(col >= QB, jnp.logical_not(first_step)))


def _attn_fwd(qkv, name):
    _, d, L, _ = qkv.shape
    nsub = min(ATTN_NSUB, L // QB)
    R = nsub * QB
    nsteps = L // R

    def body(q_ref, kp_ref, kc_ref, vp_ref, vc_ref, o_ref, lse_ref):
        i = pl.program_id(1)
        band, band_first = _band_masks(i == 0)
        kfull = jnp.concatenate([kp_ref[...], kc_ref[...]], axis=0)
        vfull = jnp.concatenate([vp_ref[...], vc_ref[...]], axis=0)
        chains = [(sb, hh) for sb in range(nsub) for hh in range(NSLOT)]
        win = lambda t, sb: t[sb * QB:(sb + 2) * QB]
        scores = []
        for sb, hh in chains:
            qh = jnp.where(_head_mask(hh), q_ref[sb * QB:(sb + 1) * QB, :], 0)
            scores.append(_dot_nt(qh, win(kfull, sb)))
        soft = []
        for (sb, hh), sc in zip(chains, scores):
            sc = jnp.where(band_first if sb == 0 else band, sc * SCALE, NEG)
            m = jnp.max(sc, axis=1, keepdims=True)
            p = jnp.exp(sc - m)
            den = jnp.sum(p, axis=1, keepdims=True)
            soft.append((p.astype(BF), den, m + jnp.log(den)))
        accs = [_dot(p, win(vfull, sb)) for (sb, hh), (p, _, _) in zip(chains, soft)]
        for sb in range(nsub):
            o = jnp.zeros((QB, GW), F32)
            lse = jnp.zeros((QB, GW), F32)
            for hh in range(NSLOT):
                hm = _head_mask(hh)
                _, den, lrow = soft[sb * NSLOT + hh]
                o = o + jnp.where(hm, accs[sb * NSLOT + hh] / den, 0.0)
                lse = lse + jnp.where(hm, lrow, 0.0)
            o_ref[sb * QB:(sb + 1) * QB, :] = o
            lse_ref[sb * QB:(sb + 1) * QB, :] = lse

    prev = lambda i: jnp.maximum(i * nsub - 1, 0)
    cur = lambda t: pl.BlockSpec((None, None, R, GW), lambda r, i: (t, r, i, 0))
    prv = lambda t: pl.BlockSpec((None, None, QB, GW), lambda r, i: (t, r, prev(i), 0))
    out = pl.BlockSpec((None, R, GW), lambda r, i: (r, i, 0))
    return pl.pallas_call(
        body, name=name, grid=(d, nsteps),
        out_shape=(jax.ShapeDtypeStruct((d, L, GW), F32), jax.ShapeDtypeStruct((d, L, GW), F32)),
        in_specs=[cur(0), prv(1), cur(1), prv(2), cur(2)],
        out_specs=(out, out), compiler_params=_cp(("parallel", "arbitrary")))(qkv, qkv, qkv, qkv, qkv)


def _attn_combine(os_, lses):
    S = os_[0].shape[1]
    tm = min(1024, S)

    def body(o0, o1, o2, l0, l1, l2, y_ref, j0, j1, j2, scr):
        os_nat = [_undilate_load(o, (), d, scr, tm) for o, d in zip((o0, o1, o2), DILS)]
        a, b, c = [_undilate_load(l, (), d, scr, tm) for l, d in zip((l0, l1, l2), DILS)]
        m = jnp.maximum(jnp.maximum(a, b), c)
        wa, wb, wc = jnp.exp(a - m), jnp.exp(b - m), jnp.exp(c - m)
        den = wa + wb + wc
        y_ref[...] = ((wa * os_nat[0] + wb * os_nat[1] + wc * os_nat[2]) / den).astype(BF)
        lse = m + jnp.log(den)
        for j_ref, d in zip((j0, j1, j2), DILS):
            _dilate_store(lse, scr, j_ref, (), d)

    dil = lambda d: pl.BlockSpec((d, tm // d, GW), lambda i: (0, i, 0))
    dshape = lambda d: jax.ShapeDtypeStruct((d, S // d, GW), F32)
    return pl.pallas_call(
        body, name="attn_combine", grid=(S // tm,),
        out_shape=(jax.ShapeDtypeStruct((S, GW), BF),) + tuple(dshape(d) for d in DILS),
        in_specs=[dil(d) for d in DILS] * 2,
        out_specs=(pl.BlockSpec((tm, GW), lambda i: (i, 0)),) + tuple(dil(d) for d in DILS),
        scratch_shapes=[pltpu.VMEM((2, tm, 128), F32)],
        compiler_params=_cp(("parallel",)))(*os_, *lses)


def _tril(upper=False):
    row = lax.broadcasted_iota(jnp.int32, (CHUNK, CHUNK), 0)
    col = lax.broadcasted_iota(jnp.int32, (CHUNK, CHUNK), 1)
    return row <= col if upper else col <= row


def _ln_fwd(z, gain, bias):
    mu = jnp.mean(z, axis=-1, keepdims=True)
    zc = z - mu
    rstd = lax.rsqrt(jnp.mean(zc * zc, axis=-1, keepdims=True) + EPS)
    zhat = zc * rstd
    return zhat, rstd, zhat * gain + bias


def _gmlp_fwd(puz, ln_g, ln_b, w_sp, b_sp_t):
    S = puz.shape[0]
    tm = min(512, S)
    nch = tm // CHUNK

    def body(p_ref, g_ref, b_ref, w_ref, bt_ref, o_ref):
        tril = _tril()
        ws = [jnp.where(tril, w_ref[gg], 0.0).astype(BF) for gg in range(4)]
        for ch in range(nch):
            rows = slice(ch * CHUNK, (ch + 1) * CHUNK)
            z = _gelu(p_ref[rows, GMLP_W:].astype(F32))
            _, _, zn = _ln_fwd(z, g_ref[...], b_ref[...])
            zn = zn.astype(BF)
            for gg in range(4):
                cols = slice(gg * CHUNK, (gg + 1) * CHUNK)
                sz = _dot(ws[gg], zn[:, cols]) + bt_ref[:, gg:gg + 1]
                u = _gelu(p_ref[rows, cols].astype(F32))
                o_ref[rows, cols] = (u * sz).astype(BF)

    return pl.pallas_call(
        body, name="gmlp_fwd", out_shape=jax.ShapeDtypeStruct((S, GMLP_W), BF), grid=(S // tm,),
        in_specs=[pl.BlockSpec((tm, 2 * GMLP_W), lambda i: (i, 0)),
                  pl.BlockSpec((1, GMLP_W), lambda i: (0, 0)), pl.BlockSpec((1, GMLP_W), lambda i: (0, 0)),
                  pl.BlockSpec((4, CHUNK, CHUNK), lambda i: (0, 0, 0)), pl.BlockSpec((CHUNK, 4), lambda i: (0, 0))],
        out_specs=pl.BlockSpec((tm, GMLP_W), lambda i: (i, 0)),
        compiler_params=_cp(("parallel",)))(puz, ln_g, ln_b, w_sp, b_sp_t)


def _merge_fwd(ya, yg, gates, wba, wbg, wout, x, g2, g3):
    S = x.shape[0]
    tm = min(512, S)

    def body(ya_ref, yg_ref, gt_ref, wba_ref, wbg_ref, wo_ref, x_ref, g2_ref, g3_ref,
             mg_ref, y_ref, x2_ref, h3_ref):
        a = _dot(ya_ref[...], wba_ref[...])
        b = _dot(yg_ref[...], wbg_ref[...])
        merged = (gt_ref[:, :D].astype(F32) * a + gt_ref[:, D:].astype(F32) * b).astype(BF)
        mg_ref[...] = merged
        y = _dot(merged, wo_ref[...])
        y_ref[...] = y
        x2 = x_ref[...] + y * _rms_stats(y) * g2_ref[...]
        x2_ref[...] = x2
        h3_ref[...] = (x2 * _rms_stats(x2) * g3_ref[...]).astype(BF)

    row = lambda w: pl.BlockSpec((tm, w), lambda i: (i, 0))
    full = lambda s: pl.BlockSpec(s, lambda i: (0, 0))
    return pl.pallas_call(
        body, name="merge_fwd", grid=(S // tm,),
        out_shape=(jax.ShapeDtypeStruct((S, D), BF), jax.ShapeDtypeStruct((S, D), F32), jax.ShapeDtypeStruct((S, D), F32),
                   jax.ShapeDtypeStruct((S, D), BF)),
        in_specs=[row(GW), row(GMLP_W), row(2 * D), full((GW, D)), full((GMLP_W, D)), full((D, D)), row(D),
                  full((1, D)), full((1, D))],
        out_specs=(row(D), row(D), row(D), row(D)),
        compiler_params=_cp(("parallel",)))(ya, yg, gates, wba, wbg, wout, x, g2, g3)


def _resident(shape):
    return pl.BlockSpec(shape, lambda i: (0,) * len(shape), pipeline_mode=pl.Buffered(1))


def _mlp_fwd(h3, wmi, wmo, x2, tgt, g4):
    S = x2.shape[0]
    tm = min(MLP_TM_FWD, S)

    def body(h_ref, wi_ref, wo_ref, x2_ref, t_ref, g4_ref, a_ref, dy2_ref, dout_ref, loss_ref, dg4_ref):
        @pl.when(pl.program_id(0) == 0)
        def _():
            loss_ref[...] = jnp.zeros_like(loss_ref)
            dg4_ref[...] = jnp.zeros_like(dg4_ref)

        a = jnp.maximum(_dot(h_ref[...], wi_ref[...]), 0.0)
        a_ref[...] = a.astype(BF)
        y2 = _dot((a * a).astype(BF), wo_ref[...])
        r = _rms_stats(y2)
        out = x2_ref[...] + y2 * r * g4_ref[...]
        err = out - t_ref[...]
        tot = jnp.sum(jnp.sum(err * err, axis=1, keepdims=True), axis=0, keepdims=True) * (0.5 / D)
        lane = lax.broadcasted_iota(jnp.int32, (1, 128), 1)
        loss_ref[...] += jnp.where(lane == 0, tot, 0.0)
        dout = err * (1.0 / D)
        dout_ref[...] = dout
        dy2, dg = _rms_bwd(y2, r, g4_ref[...], dout)
        dy2_ref[...] = dy2.astype(BF)
        dg4_ref[...] += dg

    row = pl.BlockSpec((tm, D), lambda i: (i, 0))
    return pl.pallas_call(
        body, name="mlp_fwd", grid=(S // tm,),
        out_shape=(jax.ShapeDtypeStruct((S, DFF), BF), jax.ShapeDtypeStruct((S, D), BF), jax.ShapeDtypeStruct((S, D), F32),
                   jax.ShapeDtypeStruct((1, 128), F32), jax.ShapeDtypeStruct((1, D), F32)),
        in_specs=[row, _resident((D, DFF)), _resident((DFF, D)), row, row, pl.BlockSpec((1, D), lambda i: (0, 0))],
        out_specs=(pl.BlockSpec((tm, DFF), lambda i: (i, 0)), row, row,
                   pl.BlockSpec((1, 128), lambda i: (0, 0)), pl.BlockSpec((1, D), lambda i: (0, 0))),
        compiler_params=_cp(("arbitrary",)))(h3, wmi, wmo, x2, tgt, g4)


def _mlp_bwd(dy2, a, wmo, wmi, x2, y, dout, g2, g3):
    S = x2.shape[0]
    tm = min(MLP_TM_BWD, S)

    def body(dy2_ref, a_ref, wo_ref, wi_ref, x2_ref, y_ref, dout_ref, g2_ref, g3_ref,
             dpre_ref, dx2_ref, dy_ref, dg3_ref, dg2_ref):
        @pl.when(pl.program_id(0) == 0)
        def _():
            dg3_ref[...] = jnp.zeros_like(dg3_ref)
            dg2_ref[...] = jnp.zeros_like(dg2_ref)

        da2 = _dot_nt(dy2_ref[...], wo_ref[...])
        dpre = (2.0 * a_ref[...].astype(F32) * da2).astype(BF)
        dpre_ref[...] = dpre
        dh3 = _dot_nt(dpre, wi_ref[...])
        x2 = x2_ref[...]
        dx3, dg3 = _rms_bwd(x2, _rms_stats(x2), g3_ref[...], dh3)
        dx2 = dout_ref[...] + dx3
        dx2_ref[...] = dx2
        dg3_ref[...] += dg3
        yv = y_ref[...]
        dy, dg2 = _rms_bwd(yv, _rms_stats(yv), g2_ref[...], dx2)
        dy_ref[...] = dy.astype(BF)
        dg2_ref[...] += dg2

    row = pl.BlockSpec((tm, D), lambda i: (i, 0))
    wide = pl.BlockSpec((tm, DFF), lambda i: (i, 0))
    vec = pl.BlockSpec((1, D), lambda i: (0, 0))
    return pl.pallas_call(
        body, name="mlp_bwd", grid=(S // tm,),
        out_shape=(jax.ShapeDtypeStruct((S, DFF), BF), jax.ShapeDtypeStruct((S, D), F32), jax.ShapeDtypeStruct((S, D), BF),
                   jax.ShapeDtypeStruct((1, D), F32), jax.ShapeDtypeStruct((1, D), F32)),
        in_specs=[row, wide, _resident((DFF, D)), _resident((D, DFF)), row, row, row, vec, vec],
        out_specs=(wide, row, row, vec, vec),
        compiler_params=_cp(("arbitrary",)))(dy2, a, wmo, wmi, x2, y, dout, g2, g3)


def _mm_tn(a, b, name, square_a=False, tm=1024, tn=1024, tk=2048, tie=None, col_blocks=None):
    S, M = a.shape
    N = b.shape[1]
    tm, tk = min(tm, M), min(tk, S)
    tn = N // col_blocks if col_blocks else max(t for t in range(128, min(tn, N) + 1, 128) if N % t == 0)
    assert M % tm == 0 and S % tk == 0 and N % tn == 0
    nk = S // tk
    ties = () if tie is None else (tie,)

    def body(a_ref, b_ref, *rest):
        o_ref = rest[-1]
        k = pl.program_id(2)
        av = a_ref[...]
        if square_a:
            av = av * av
        part = _dot_tn(av, b_ref[...])

        @pl.when(k == 0)
        def _():
            o_ref[...] = part

        @pl.when(k > 0)
        def _():
            o_ref[...] += part

    if col_blocks:
        out_shape, out_spec = (col_blocks, M, tn), pl.BlockSpec((None, tm, tn), lambda i, j, k: (j, i, 0))
    else:
        out_shape, out_spec = (M, N), pl.BlockSpec((tm, tn), lambda i, j, k: (i, j))
    return pl.pallas_call(
        body, name=name, out_shape=jax.ShapeDtypeStruct(out_shape, F32), grid=(M // tm, N // tn, nk),
        in_specs=[pl.BlockSpec((tk, tm), lambda i, j, k: (k, i)), pl.BlockSpec((tk, tn), lambda i, j, k: (k, j))]
        + [pl.BlockSpec(memory_space=pl.ANY)] * len(ties),
        out_specs=out_spec,
        compiler_params=_cp(("parallel", "parallel", "arbitrary")))(a, b, *ties)


def _outproj_bwd(dy, wout, ya, yg, gates, wba, wbg, tie):
    S = dy.shape[0]
    tm = min(512, S)

    def body(dy_ref, wo_ref, ya_ref, yg_ref, gt_ref, wba_ref, wbg_ref, tie_ref,
             dgt_ref, da_ref, db_ref, dyg_ref, e0, e1, e2, s0, s1, s2, scr):
        dm = _dot_nt(dy_ref[...], wo_ref[...])
        ga, gb = gt_ref[:, :D].astype(F32), gt_ref[:, D:].astype(F32)
        dgt_ref[:, :D] = (dm * _dot(ya_ref[...], wba_ref[...]) * ga * (1.0 - ga)).astype(BF)
        dgt_ref[:, D:] = (dm * _dot(yg_ref[...], wbg_ref[...]) * gb * (1.0 - gb)).astype(BF)
        da = (dm * ga).astype(BF)
        db = (dm * gb).astype(BF)
        da_ref[...] = da
        db_ref[...] = db
        dyg_ref[...] = _dot_nt(db, wbg_ref[...]).astype(BF)
        dya = _dot_nt(da, wba_ref[...]).astype(BF).astype(F32)
        dyy = dya * ya_ref[...].astype(F32)
        dsum = jnp.zeros((tm, GW), F32)
        for hh in range(NSLOT):
            hm = _head_mask(hh)
            dsum = dsum + jnp.where(hm, jnp.sum(jnp.where(hm, dyy, 0.0), axis=1, keepdims=True), 0.0)
        for e_ref, s_ref, d in zip((e0, e1, e2), (s0, s1, s2), DILS):
            _dilate_store(dya, scr, e_ref, (), d)
            _dilate_store(dsum, scr, s_ref, (), d)

    row = lambda w: pl.BlockSpec((tm, w), lambda i: (i, 0))
    full = lambda s: pl.BlockSpec(s, lambda i: (0, 0))
    dil = lambda d: pl.BlockSpec((d, tm // d, GW), lambda i: (0, i, 0))
    dshape = lambda d, t: jax.ShapeDtypeStruct((d, S // d, GW), t)
    return pl.pallas_call(
        body, name="outproj_bwd", grid=(S // tm,),
        out_shape=(jax.ShapeDtypeStruct((S, 2 * D), BF), jax.ShapeDtypeStruct((S, D), BF), jax.ShapeDtypeStruct((S, D), BF),
                   jax.ShapeDtypeStruct((S, GMLP_W), BF)) + tuple(dshape(d, BF) for d in DILS)
        + tuple(dshape(d, F32) for d in DILS),
        in_specs=[row(D), full((D, D)), row(GW), row(GMLP_W), row(2 * D), full((GW, D)), full((GMLP_W, D)),
                  pl.BlockSpec(memory_space=pl.ANY)],
        out_specs=(row(2 * D), row(D), row(D), row(GMLP_W)) + tuple(dil(d) for d in DILS) * 2,
        scratch_shapes=[pltpu.VMEM((2, tm, 128), F32)],
        compiler_params=_cp(("parallel",)))(dy, wout, ya, yg, gates, wba, wbg, tie)


def _gmlp_bwd(puz, dyg, ln_g, ln_b, w_sp, w_sp_t, b_sp_t):
    S = puz.shape[0]
    tm = min(512, S)
    nch = tm // CHUNK

    def body(p_ref, dy_ref, g_ref, b_ref, w_ref, wt_ref, bt_ref,
             dp_ref, dw_ref, dbs_ref, dg_ref, dbias_ref, dbacc_ref):
        i = pl.program_id(0)

        @pl.when(i == 0)
        def _():
            dw_ref[...] = jnp.zeros_like(dw_ref)
            dbacc_ref[...] = jnp.zeros_like(dbacc_ref)
            dg_ref[...] = jnp.zeros_like(dg_ref)
            dbias_ref[...] = jnp.zeros_like(dbias_ref)

        tril = _tril()
        ws = [jnp.where(tril, w_ref[gg], 0.0).astype(BF) for gg in range(4)]
        triu = _tril(upper=True)
        wts = [jnp.where(triu, wt_ref[gg], 0.0).astype(BF) for gg in range(4)]
        gain = g_ref[...]
        for ch in range(nch):
            rows = slice(ch * CHUNK, (ch + 1) * CHUNK)
            pz = p_ref[rows, GMLP_W:].astype(F32)
            z = _gelu(pz)
            zhat, rstd, zn = _ln_fwd(z, gain, b_ref[...])
            znb = zn.astype(BF)
            dzn_parts = []
            for gg in range(4):
                cols = slice(gg * CHUNK, (gg + 1) * CHUNK)
                pu = p_ref[rows, cols].astype(F32)
                u = _gelu(pu)
                sz = _dot(ws[gg], znb[:, cols]) + bt_ref[:, gg:gg + 1]
                dyv = dy_ref[rows, cols].astype(F32)
                dp_ref[rows, cols] = (dyv * sz * _gelu_grad(pu)).astype(BF)
                dsz = dyv * u
                dbacc_ref[gg] += dsz
                dszb = dsz.astype(BF)
                dw_ref[gg] += _dot_nt(dszb, znb[:, cols])
                dzn_parts.append(_dot(wts[gg], dszb))
            dzn = jnp.concatenate(dzn_parts, axis=1)
            dg_ref[...] += jnp.sum(dzn * zhat, axis=0, keepdims=True)
            dbias_ref[...] += jnp.sum(dzn, axis=0, keepdims=True)
            dzh = dzn * gain
            dz = rstd * (dzh - jnp.mean(dzh, axis=-1, keepdims=True)
                         - zhat * jnp.mean(dzh * zhat, axis=-1, keepdims=True))
            dp_ref[rows, GMLP_W:] = (dz * _gelu_grad(pz)).astype(BF)

        @pl.when(i == pl.num_programs(0) - 1)
        def _():
            for gg in range(4):
                dw_ref[gg] = jnp.where(tril, dw_ref[gg], 0.0)
                dbs_ref[gg] = jnp.sum(dbacc_ref[gg], axis=1, keepdims=True)

    full2 = lambda s: pl.BlockSpec(s, lambda i: (0, 0))
    full3 = lambda s: pl.BlockSpec(s, lambda i: (0, 0, 0))
    return pl.pallas_call(
        body, name="gmlp_bwd", grid=(S // tm,),
        out_shape=(jax.ShapeDtypeStruct((S, 2 * GMLP_W), BF), jax.ShapeDtypeStruct((4, CHUNK, CHUNK), F32),
                   jax.ShapeDtypeStruct((4, CHUNK, 1), F32), jax.ShapeDtypeStruct((1, GMLP_W), F32),
                   jax.ShapeDtypeStruct((1, GMLP_W), F32)),
        in_specs=[pl.BlockSpec((tm, 2 * GMLP_W), lambda i: (i, 0)), pl.BlockSpec((tm, GMLP_W), lambda i: (i, 0)),
                  full2((1, GMLP_W)), full2((1, GMLP_W)), full3((4, CHUNK, CHUNK)), full3((4, CHUNK, CHUNK)),
                  full2((CHUNK, 4))],
        out_specs=(pl.BlockSpec((tm, 2 * GMLP_W), lambda i: (i, 0)), full3((4, CHUNK, CHUNK)), full3((4, CHUNK, 1)),
                   full2((1, GMLP_W)), full2((1, GMLP_W))),
        scratch_shapes=[pltpu.VMEM((4, CHUNK, CHUNK), F32)],
        compiler_params=_cp(("arbitrary",)))(puz, dyg, ln_g, ln_b, w_sp, w_sp_t, b_sp_t)


def _attn_bwd(qkv, dya, dsums, lse, name):
    _, d, L, _ = qkv.shape
    nsub = min(ATTN_NSUB, L // QB)
    R = nsub * QB
    nsteps = L // R

    def body(q_ref, qn_ref, kp_ref, kc_ref, vp_ref, vc_ref, dy_ref, dyn_ref, e_ref, en_ref, l_ref, ln_ref, o_ref):
        i = pl.program_id(1)
        band, band_first = _band_masks(i == 0)
        row = lax.broadcasted_iota(jnp.int32, (QB, QB), 0)
        col = lax.broadcasted_iota(jnp.int32, (QB, QB), 1)
        mask_next = jnp.logical_and(col >= row, i < nsteps - 1)
        kc, vc = kc_ref[...], vc_ref[...]
        kfull = jnp.concatenate([kp_ref[...], kc], axis=0)
        vfull = jnp.concatenate([vp_ref[...], vc], axis=0)
        k_last, v_last = kc[(nsub - 1) * QB:], vc[(nsub - 1) * QB:]
        q_ext = jnp.concatenate([q_ref[...], qn_ref[...]], axis=0)
        dy_ext = jnp.concatenate([dy_ref[...], dyn_ref[...]], axis=0)
        esum, esum_n, lse, lse_n = e_ref[...], en_ref[...], l_ref[...], ln_ref[...]
        win = lambda t, sb: t[sb * QB:(sb + 2) * QB]
        blk = lambda t, sb: t[sb * QB:(sb + 1) * QB]
        hms = [_head_mask(hh) for hh in range(NSLOT)]
        q_hs = [jnp.where(hm, q_ext, 0) for hm in hms]
        dy_hs = [jnp.where(hm, dy_ext, 0) for hm in hms]
        raw = []
        for hh in range(NSLOT):
            tiles = [(_dot_nt(blk(q_hs[hh], sb), win(kfull, sb)), _dot_nt(blk(dy_hs[hh], sb), win(vfull, sb)))
                     for sb in range(nsub)]
            tiles.append((_dot_nt(blk(q_hs[hh], nsub), k_last), _dot_nt(blk(dy_hs[hh], nsub), v_last)))
            raw.append(tiles)
        ps, dss = [], []
        for hh in range(NSLOT):
            rowstat = lambda t: jnp.max(jnp.where(hms[hh], t, -jnp.inf), axis=1, keepdims=True)
            p_h, ds_h = [], []
            for sb in range(nsub + 1):
                sc, dp = raw[hh][sb]
                if sb < nsub:
                    msk, lrow, erow = (band_first if sb == 0 else band), rowstat(blk(lse, sb)), rowstat(blk(esum, sb))
                else:
                    msk, lrow, erow = mask_next, rowstat(lse_n), rowstat(esum_n)
                p = jnp.where(msk, jnp.exp(sc * SCALE - lrow), 0.0)
                p_h.append(p.astype(BF))
                ds_h.append((p * (dp - erow)).astype(BF))
            ps.append(p_h)
            dss.append(ds_h)
        dq = [jnp.zeros((QB, GW), F32) for _ in range(nsub)]
        dk = [jnp.zeros((QB, GW), F32) for _ in range(nsub)]
        dv = [jnp.zeros((QB, GW), F32) for _ in range(nsub)]
        for hh in range(NSLOT):
            for sb in range(nsub):
                dq[sb] = dq[sb] + jnp.where(hms[hh], _dot(dss[hh][sb], win(kfull, sb)), 0.0)
                nxt = lambda t: t[sb + 1][:, :QB] if sb + 1 < nsub else t[nsub]
                dk[sb] = dk[sb] + _dot_tn(jnp.concatenate([dss[hh][sb][:, QB:], nxt(dss[hh])], axis=0), win(q_hs[hh], sb))
                dv[sb] = dv[sb] + _dot_tn(jnp.concatenate([ps[hh][sb][:, QB:], nxt(ps[hh])], axis=0), win(dy_hs[hh], sb))
        for sb in range(nsub):
            rows = slice(sb * QB, (sb + 1) * QB)
            o_ref[0, rows, :] = (dq[sb] * SCALE).astype(BF)
            o_ref[1, rows, :] = (dk[sb] * SCALE).astype(BF)
            o_ref[2, rows, :] = dv[sb].astype(BF)

    prev = lambda i: jnp.maximum(i * nsub - 1, 0)
    nxt = lambda i: jnp.minimum((i + 1) * nsub, L // QB - 1)
    cur4 = lambda t: pl.BlockSpec((None, None, R, GW), lambda r, i: (t, r, i, 0))
    prv4 = lambda t: pl.BlockSpec((None, None, QB, GW), lambda r, i: (t, r, prev(i), 0))
    nxt4 = lambda t: pl.BlockSpec((None, None, QB, GW), lambda r, i: (t, r, nxt(i), 0))
    cur3 = pl.BlockSpec((None, R, GW), lambda r, i: (r, i, 0))
    nxt3 = pl.BlockSpec((None, QB, GW), lambda r, i: (r, nxt(i), 0))
    return pl.pallas_call(
        body, name=name, grid=(d, nsteps), out_shape=jax.ShapeDtypeStruct((3, d, L, GW), BF),
        in_specs=[cur4(0), nxt4(0), prv4(1), cur4(1), prv4(2), cur4(2), cur3, nxt3, cur3, nxt3, cur3, nxt3],
        out_specs=pl.BlockSpec((3, None, R, GW), lambda r, i: (0, r, i, 0)),
        compiler_params=_cp(("parallel", "arbitrary")))(qkv, qkv, qkv, qkv, qkv, qkv, dya, dya, dsums, dsums, lse, lse)


def _inproj_bwd(dqkvs, dpuz, dgates, wqkv, wuz, wg, cos_t, sin_t, x, dx2, g1):
    S = x.shape[0]
    tm = min(512, S)

    def body(d0_ref, d1_ref, d2_ref, dp_ref, dg_ref, wqkv_ref, wuz_ref, wg_ref, c_ref, s_ref,
             x_ref, dx2_ref, g1_ref, gx_ref, dg1_ref, dn_ref, scr):
        i = pl.program_id(0)

        @pl.when(i == 0)
        def _():
            dg1_ref[...] = jnp.zeros_like(dg1_ref)

        for t in range(3):
            for g, (d_ref, d) in enumerate(zip((d0_ref, d1_ref, d2_ref), DILS)):
                piece = _undilate_load(d_ref, (t,), d, scr, tm)
                if t < 2:
                    piece = _rope_bwd(piece, c_ref, s_ref)
                dn_ref[:, (3 * t + g) * GW:(3 * t + g + 1) * GW] = piece.astype(BF)
        dh = (_dot_nt(dn_ref[...], wqkv_ref[...]) + _dot_nt(dp_ref[...], wuz_ref[...])
              + _dot_nt(dg_ref[...], wg_ref[...]))
        xv = x_ref[...]
        dx1, dg1 = _rms_bwd(xv, _rms_stats(xv), g1_ref[...], dh)
        gx_ref[...] = dx2_ref[...] + dx1
        dg1_ref[...] += dg1

    row = lambda w: pl.BlockSpec((tm, w), lambda i: (i, 0))
    full = lambda s: pl.BlockSpec(s, lambda i: (0, 0))
    dil = lambda d: pl.BlockSpec((3, d, tm // d, GW), lambda i: (0, 0, i, 0))
    return pl.pallas_call(
        body, name="inproj_bwd", grid=(S // tm,),
        out_shape=(jax.ShapeDtypeStruct((S, D), F32), jax.ShapeDtypeStruct((1, D), F32),
                   jax.ShapeDtypeStruct((S, 3 * AW), BF)),
        in_specs=[dil(d) for d in DILS] + [row(2 * GMLP_W), row(2 * D), full((D, 3 * AW)), full((D, 2 * GMLP_W)),
                                            full((D, 2 * D)), row(128), row(128), row(D), row(D), full((1, D))],
        out_specs=(row(D), full((1, D)), row(3 * AW)),
        scratch_shapes=[pltpu.VMEM((2, tm, 128), F32)],
        compiler_params=_cp(("arbitrary",)))(*dqkvs, dpuz, dgates, wqkv, wuz, wg, cos_t, sin_t, x, dx2, g1)


def _adam_math(w, g, m, v):
    m2 = ADAM_B1 * m + (1.0 - ADAM_B1) * g
    v2 = ADAM_B2 * v + (1.0 - ADAM_B2) * (g * g)
    m_hat = m2 / (1.0 - ADAM_B1 ** ADAM_STEP)
    v_hat = v2 / (1.0 - ADAM_B2 ** ADAM_STEP)
    delta = -ADAM_LR * (m_hat / (jnp.sqrt(v_hat) + ADAM_EPS) + ADAM_WD * w)
    return delta, m2, v2


def _adam_shard(own, recv, w, m, v, name):
    R, C = w.shape
    tr = min(256, R)

    def body(own_ref, r_ref, w_ref, m_ref, v_ref, g_ref, d_ref, m2_ref, v2_ref):
        g = own_ref[...] + r_ref[0].astype(F32) + r_ref[1].astype(F32) + r_ref[2].astype(F32)
        g_ref[...] = g
        d_ref[...], m2_ref[...], v2_ref[...] = _adam_math(w_ref[...], g, m_ref[...], v_ref[...])

    spec = pl.BlockSpec((tr, C), lambda i: (i, 0))
    out = jax.ShapeDtypeStruct((R, C), F32)
    return pl.pallas_call(
        body, name=name, grid=(R // tr,), out_shape=(out, out, out, out),
        in_specs=[spec, pl.BlockSpec((3, tr, C), lambda i: (0, i, 0)), spec, spec, spec],
        out_specs=(spec, spec, spec, spec), compiler_params=_cp(("parallel",)))(own, recv, w, m, v)


def _rs_add(gblocks, recv, idx, name):
    _, R, C = gblocks.shape
    tr = min(256, R)

    def body(t_ref, g_ref, r_ref, own_ref, send_ref):
        j = pl.program_id(1)
        s = g_ref[...] + r_ref[...]

        @pl.when(j == 0)
        def _():
            own_ref[...] = s

        @pl.when(j > 0)
        def _():
            send_ref[...] = s.astype(BF)

    grid_spec = pltpu.PrefetchScalarGridSpec(
        num_scalar_prefetch=1, grid=(R // tr, 4),
        in_specs=[pl.BlockSpec((None, tr, C), lambda i, j, t: (t[j], i, 0)),
                  pl.BlockSpec((None, tr, C), lambda i, j, t: (t[4 + j], i, 0))],
        out_specs=[pl.BlockSpec((tr, C), lambda i, j, t: (i, 0)),
                   pl.BlockSpec((None, tr, C), lambda i, j, t: (jnp.maximum(j - 1, 0), i, 0))])
    return pl.pallas_call(
        body, name=name, grid_spec=grid_spec,
        out_shape=(jax.ShapeDtypeStruct((R, C), F32), jax.ShapeDtypeStruct((3, R, C), BF)),
        compiler_params=_cp(("parallel", "arbitrary")))(idx, gblocks, recv)


def _mesh_pos():
    return lax.axis_index("x"), lax.axis_index("y"), lax.axis_index("c")


def _all_gather_weights(shards):
    n = len(shards)

    def body(*refs):
        ins, outs = refs[:n], refs[n:2 * n]
        send_sems, recv_sems, loc_sems = refs[2 * n:]
        x, y, c = _mesh_pos()
        me, sib = (x, y, c), (x, y, 1 - c)
        chips = [(1 - x, y), (x, 1 - y), (1 - x, 1 - y)]

        def rcopy(k, s, block, to, src=None):
            dst = outs[k].at[4 * block[0] + 2 * block[1] + block[2]]
            return pltpu.make_async_remote_copy(
                src_ref=dst if src is None else src, dst_ref=dst, send_sem=send_sems.at[k, s],
                recv_sem=recv_sems.at[k, s], device_id=to, device_id_type=MESH)

        locs = [pltpu.make_async_copy(ins[k], outs[k].at[4 * x + 2 * y + c], loc_sems.at[k]) for k in range(n)]
        for cp in locs:
            cp.start()
        started = []
        for k in range(n):
            for j, chip in enumerate(chips):
                started.append(rcopy(k, 1 + j, me, (*chip, c), src=ins[k]))
                started[-1].start()
        for k in range(n):
            started.append(rcopy(k, 0, me, sib, src=ins[k]))
            started[-1].start()
        for k in range(n):
            for j, chip in enumerate(chips):
                rcopy(k, 1 + j, (*chip, c), me).wait_recv()
                started.append(rcopy(k, 4 + j, (*chip, c), sib))
                started[-1].start()
        for k in range(n):
            rcopy(k, 0, sib, me).wait_recv()
            for j, chip in enumerate(chips):
                rcopy(k, 4 + j, (*chip, 1 - c), me).wait_recv()
        for cp in started:
            cp.wait_send()
        for cp in locs:
            cp.wait()

    hbm = pl.BlockSpec(memory_space=pl.ANY)
    return pl.pallas_call(
        body, name="ag_weights",
        out_shape=tuple(jax.ShapeDtypeStruct((8,) + s.shape, s.dtype) for s in shards),
        in_specs=[hbm] * n, out_specs=(hbm,) * n,
        scratch_shapes=[pltpu.SemaphoreType.DMA((n, 7)), pltpu.SemaphoreType.DMA((n, 7)),
                        pltpu.SemaphoreType.DMA((n,))])(*shards)


_HBM = pl.BlockSpec(memory_space=pltpu.HBM)
_SEM = pl.BlockSpec(memory_space=pltpu.SEMAPHORE)
_EFFECT = pltpu.SideEffectType.DATAFLOW_SIDE_EFFECTING
_RELATIONS = [(dx, dy, dc) for dx in (0, 1) for dy in (0, 1) for dc in (0, 1)][1:]


def _flip(v, d):
    return 1 - v if d else v


def _plan_gather(n):
    def plan(x, y, c):
        return [(k, None, 4 * x + 2 * y + c, (_flip(x, dx), _flip(y, dy), _flip(c, dc)))
                for k in range(n) for dx, dy, dc in _RELATIONS]
    return plan


def _plan_d2d(n):
    def plan(x, y, c):
        return [(k, 2 * kk + 1 - c, kk, (x, y, 1 - c)) for k in range(n) for kk in range(4)]
    return plan


def _plan_ici(n):
    def plan(x, y, c):
        return [(k, j, j, (_flip(x, dx), _flip(y, dy), c))
                for k in range(n) for j, (dx, dy) in enumerate(((1, 0), (0, 1), (1, 1)))]
    return plan


def _plan_copies(plan, src_refs, land_refs, send_sems, recv_sems):
    x, y, c = _mesh_pos()
    return [pltpu.make_async_remote_copy(
        src_ref=src_refs[k] if si is None else src_refs[k].at[si], dst_ref=land_refs[k].at[di],
        send_sem=send_sems.at[n], recv_sem=recv_sems.at[n], device_id=dev, device_id_type=MESH)
        for n, (k, si, di, dev) in enumerate(plan(x, y, c))]


def _exchange_start(srcs, land_shapes, plan, ncopies, name, after):
    n = len(srcs)

    def body(*refs):
        src_refs, land_refs = refs[:n], refs[n:2 * n]
        send_sems, recv_sems = refs[2 * n + len(after)], refs[2 * n + len(after) + 1]
        token = refs[-1]
        for cp in _plan_copies(plan, src_refs, land_refs, send_sems, recv_sems):
            cp.start()
        token[...] = jnp.zeros_like(token)

    lands = [pltpu.with_memory_space_constraint(lax.empty(s, a.dtype), pltpu.HBM) for s, a in zip(land_shapes, srcs)]
    srcs = [pltpu.with_memory_space_constraint(a, pltpu.HBM) for a in srcs]
    outs = pl.pallas_call(
        body, name=name,
        out_shape=(pltpu.SemaphoreType.DMA((ncopies,)), pltpu.SemaphoreType.DMA((ncopies,)))
        + tuple(pltpu.HBM(a.shape, a.dtype) for a in srcs) + tuple(pltpu.HBM(a.shape, a.dtype) for a in lands)
        + (jax.ShapeDtypeStruct((8, 128), F32),),
        in_specs=[_HBM] * (2 * n) + [pl.BlockSpec(memory_space=pl.ANY)] * len(after),
        out_specs=(_SEM, _SEM) + (_HBM,) * (2 * n) + (pl.BlockSpec(memory_space=pltpu.VMEM),),
        input_output_aliases={i: 2 + i for i in range(2 * n)},
        compiler_params=pltpu.CompilerParams(has_side_effects=_EFFECT))(*srcs, *lands, *after)
    return (outs[0], outs[1], list(outs[2:2 + n]), list(outs[2 + n:2 + 2 * n])), outs[-1]


def _exchange_wait(handle, plan, name, after):
    send_sems, recv_sems, srcs, lands = handle
    n = len(srcs)

    def body(*refs):
        src_refs, land_refs = refs[:n], refs[n:2 * n]
        for cp in _plan_copies(plan, src_refs, land_refs, refs[2 * n], refs[2 * n + 1]):
            cp.wait_send()
            cp.wait_recv()

    outs = pl.pallas_call(
        body, name=name,
        out_shape=tuple(pltpu.HBM(a.shape, a.dtype) for a in srcs) + tuple(pltpu.HBM(a.shape, a.dtype) for a in lands),
        in_specs=[_HBM] * (2 * n) + [_SEM, _SEM] + [pl.BlockSpec(memory_space=pl.ANY)] * len(after),
        out_specs=(_HBM,) * (2 * n), input_output_aliases={i: i for i in range(2 * n)},
        compiler_params=pltpu.CompilerParams(has_side_effects=_EFFECT))(*srcs, *lands, send_sems, recv_sems, *after)
    return list(outs[:n]), list(outs[n:])


SMALL = ("g1", "g2", "g3", "g4", "ln_g", "ln_b", "b_sp", "w_sp")


def _small_step(loss_row, grads, ws, ms, vs):
    parts = [loss_row] + [grads[k] for k in SMALL]
    n, ns = len(parts), len(SMALL)

    def body(*refs):
        p_refs = refs[:n]
        w_refs, m_refs, v_refs = (refs[n + i * ns:n + (i + 1) * ns] for i in range(3))
        o = n + 3 * ns
        loss_ref = refs[o]
        g_out, d_out, m_out, v_out = (refs[o + 1 + i * ns:o + 1 + (i + 1) * ns] for i in range(4))
        bufs = refs[o + 1 + 4 * ns:o + 1 + 4 * ns + n]
        send_sems, recv_sems = refs[-2:]
        x, y, c = _mesh_pos()
        me, sib = (x, y, c), (x, y, 1 - c)
        chips = [(1 - x, y), (x, 1 - y), (1 - x, 1 - y)]

        def rcopy(k, s, block, to, src=None):
            dst = bufs[k].at[4 * block[0] + 2 * block[1] + block[2]]
            return pltpu.make_async_remote_copy(
                src_ref=dst if src is None else src, dst_ref=dst, send_sem=send_sems.at[k, s],
                recv_sem=recv_sems.at[k, s], device_id=to, device_id_type=MESH)

        started = []
        for k in range(n):
            bufs[k][4 * x + 2 * y + c] = p_refs[k][...]
            started.append(rcopy(k, 0, me, sib, src=p_refs[k]))
            started += [rcopy(k, 1 + j, me, (*chip, c), src=p_refs[k]) for j, chip in enumerate(chips)]
        for cp in started:
            cp.start()
        for k in range(n):
            for j, chip in enumerate(chips):
                rcopy(k, 1 + j, (*chip, c), me).wait_recv()
                started.append(rcopy(k, 4 + j, (*chip, c), sib))
                started[-1].start()
        for k in range(n):
            rcopy(k, 0, sib, me).wait_recv()
            for j, chip in enumerate(chips):
                rcopy(k, 4 + j, (*chip, 1 - c), me).wait_recv()
        for cp in started:
            cp.wait_send()
        sums = []
        for k in range(n):
            acc = bufs[k][0]
            for b in range(1, 8):
                acc = acc + bufs[k][b]
            sums.append(acc)
        loss_ref[...] = sums[0]
        for i in range(ns):
            g_out[i][...] = sums[1 + i]
            d_out[i][...], m_out[i][...], v_out[i][...] = _adam_math(w_refs[i][...], sums[1 + i], m_refs[i][...],
                                                                     v_refs[i][...])

    args = parts + [t[k] for t in (ws, ms, vs) for k in SMALL]
    shapes = [jax.ShapeDtypeStruct(loss_row.shape, F32)] + [jax.ShapeDtypeStruct(grads[k].shape, F32) for k in SMALL] * 4
    vmem = pl.BlockSpec(memory_space=pltpu.VMEM)
    outs = pl.pallas_call(
        body, name="small_step", out_shape=tuple(shapes), in_specs=[vmem] * len(args), out_specs=(vmem,) * len(shapes),
        scratch_shapes=[pltpu.VMEM((8,) + p.shape, F32) for p in parts]
        + [pltpu.SemaphoreType.DMA((n, 7)), pltpu.SemaphoreType.DMA((n, 7))],
        compiler_params=pltpu.CompilerParams(vmem_limit_bytes=VMEM_LIMIT))(*args)
    groups = [dict(zip(SMALL, outs[1 + i * ns:1 + (i + 1) * ns])) for i in range(4)]
    return (outs[0], *groups)


def _rope_tables(S):
    half = HD // 2
    inv_freq = ROPE_THETA ** (-jnp.arange(half, dtype=F32) / half)
    ang = jnp.arange(S, dtype=F32)[:, None] * inv_freq[None, :]
    cos, sin = jnp.cos(ang), jnp.sin(ang)
    return jnp.concatenate([cos, cos, cos, cos], axis=1), jnp.concatenate([-sin, sin, -sin, sin], axis=1)


def _to_blocks(g, col_sharded):
    if col_sharded:
        return g.reshape(g.shape[0], 8, g.shape[1] // 8).transpose(1, 0, 2)
    return g.reshape(8, g.shape[0] // 8, g.shape[1])


def _from_blocks(t, col_sharded):
    if col_sharded:
        return t.transpose(1, 0, 2).reshape(t.shape[1], 8 * t.shape[2])
    return t.reshape(8 * t.shape[1], t.shape[2])


class _NoComm:
    def __init__(self, late_weights):
        self._late = late_weights
        self.grads = {}

    def start_token(self):
        return 0.0

    def late_weights(self, after):
        return self._late

    def rs_start(self, key, gblocks):
        self.grads[key] = gblocks
        return jnp.zeros((8, 128), F32)

    def rs_mid(self, key, after):
        return jnp.zeros((8, 128), F32)


class _FsdpComm:
    def __init__(self, late_shards, col_sharded, after, idx, block):
        self._col, self._idx, self._block, self._rs = col_sharded, idx, block, {}
        n = len(late_shards)
        self._gather, self._token = _exchange_start(
            late_shards, [(8,) + s.shape for s in late_shards], _plan_gather(n), 7 * n, "ag_late_start", (after,))

    def start_token(self):
        return self._token[0, 0]

    def late_weights(self, after):
        shards, lands = _exchange_wait(self._gather, _plan_gather(len(self._col)), "ag_late_wait", after)
        lands = [lax.dynamic_update_index_in_dim(t, s, self._block, 0) for t, s in zip(lands, shards)]
        return [_from_blocks(t, cs) for t, cs in zip(lands, self._col)]

    def rs_start(self, key, gblocks):
        n = len(gblocks)
        d2d, token = _exchange_start(gblocks, [(4,) + g.shape[1:] for g in gblocks], _plan_d2d(n), 4 * n,
                                     "rs_%s_d2d_start" % key, (self._token,))
        self._rs[key] = dict(n=n, d2d=d2d)
        return token

    def rs_mid(self, key, after):
        st = self._rs[key]
        gblocks, from_sib = _exchange_wait(st["d2d"], _plan_d2d(st["n"]), "rs_%s_d2d_wait" % key, after)
        halves = [_rs_add(g, r, self._idx, "rs_add_%s_%d" % (key, k)) for k, (g, r) in enumerate(zip(gblocks, from_sib))]
        st["own"] = [own for own, _ in halves]
        sends = [send for _, send in halves]
        st["ici"], token = _exchange_start(sends, [t.shape for t in sends], _plan_ici(st["n"]), 3 * st["n"],
                                           "rs_%s_ici_start" % key, (self._token,))
        return token

    def rs_end(self, key, after):
        st = self._rs[key]
        return st["own"], _exchange_wait(st["ici"], _plan_ici(st["n"]), "rs_%s_ici_wait" % key, after)[1]


def _local_step(x, tgt, wqkv, wuz, wg, comm, g1, g2, g3, g4, ln_g, ln_b, w_sp, b_sp):
    S = x.shape[0]
    cos_t, sin_t = _rope_tables(S)
    b_sp_t = b_sp.T
    w_sp_t = w_sp.transpose(0, 2, 1)

    h1 = _rms_fwd(x, g1 + comm.start_token(), "rms_pre_mix")
    qkvs = [_proj_qkv(h1, wqkv, cos_t, sin_t, g, d, "proj_qkv_d%d" % d) for g, d in enumerate(DILS)]
    puz = _proj(h1, wuz, "none", "proj_uz", 1024)
    gates = _proj(h1, wg, "sigmoid", "proj_gates", 512)
    fwd = [_attn_fwd(t, "attn_fwd_d%d" % d) for t, d in zip(qkvs, DILS)]
    ya, *lses = _attn_combine([o for o, _ in fwd], [l for _, l in fwd])
    yg = _gmlp_fwd(puz, ln_g, ln_b, w_sp, b_sp_t)
    wba, wbg, wout, wmi, wmo = comm.late_weights(after=(ya, yg, gates))
    merged, y, x2, h3 = _merge_fwd(ya, yg, gates, wba, wbg, wout, x, g2, g3)
    a, dy2, dout, loss_row, dg4 = _mlp_fwd(h3, wmi, wmo, x2, tgt, g4)

    dpre, dx2, dy, dg3, dg2 = _mlp_bwd(dy2, a, wmo, wmi, x2, y, dout, g2, g3)
    dwmo = _mm_tn(a, dy2, "dw_mlp_out", square_a=True)
    dwmi = _mm_tn(h3, dpre, "dw_mlp_in", col_blocks=8)
    tie = comm.rs_start("mlp", [dwmi, _to_blocks(dwmo, False)])
    dgates, da, db, dyg, *rest = _outproj_bwd(dy, wout, ya, yg, gates, wba, wbg, tie)
    dyas, dsums = rest[:3], rest[3:]
    dwout = _mm_tn(merged, dy, "dw_out")
    dwba = _mm_tn(ya, da, "dw_branch_attn")
    dwbg = _mm_tn(yg, db, "dw_branch_gmlp")
    tie = comm.rs_mid("mlp", after=(dyg,))
    tie2 = comm.rs_start("mid", [_to_blocks(dwba, True), _to_blocks(dwbg, True), _to_blocks(dwout, False)])
    dpuz, dwsp, dbs, dlng, dlnb = _gmlp_bwd(puz, dyg, ln_g + (tie[0, 0] + tie2[0, 0]), ln_b, w_sp, w_sp_t, b_sp_t)
    dqkvs = [_attn_bwd(qkvs[g], dyas[g], dsums[g], lses[g], "attn_bwd_d%d" % d) for g, d in enumerate(DILS)]
    tie = comm.rs_mid("mid", after=tuple(dqkvs))
    grad_x, dg1, dqkv = _inproj_bwd(dqkvs, dpuz, dgates, wqkv, wuz, wg, cos_t, sin_t, x, dx2, g1 + tie[0, 0])
    dwin = jnp.concatenate([_mm_tn(h1, t, "dw_in_%d" % n) for n, t in enumerate((dqkv, dpuz, dgates))], axis=1)
    small = dict(g1=dg1, g2=dg2, g3=dg3, g4=dg4, ln_g=dlng, ln_b=dlnb, b_sp=dbs.reshape(4, CHUNK),
                 w_sp=dwsp.reshape(4 * CHUNK, CHUNK))
    return loss_row, grad_x, dwin, small


def kernel(x, norm_pre_mix, w_in, w_spatial, b_spatial, ln_v_gain, ln_v_bias, w_branch_attn, w_branch_gmlp, w_out, norm_post_mix, norm_pre_mlp, w_mlp_in, w_mlp_out, norm_post_mlp, loss_target, m_norm_pre_mix, m_w_in, m_w_spatial, m_b_spatial, m_ln_v_gain, m_ln_v_bias, m_w_branch_attn, m_w_branch_gmlp, m_w_out, m_norm_post_mix, m_norm_pre_mlp, m_w_mlp_in, m_w_mlp_out, m_norm_post_mlp, v_norm_pre_mix, v_w_in, v_w_spatial, v_b_spatial, v_ln_v_gain, v_ln_v_bias, v_w_branch_attn, v_w_branch_gmlp, v_w_out, v_norm_post_mix, v_norm_pre_mlp, v_w_mlp_in, v_w_mlp_out, v_norm_post_mlp):
    mx, my, mc = _mesh_pos()
    rel = [(0, 0), (1, 0), (0, 1), (1, 1)]
    chip_of = [2 * (mx ^ dx) + (my ^ dy) for dx, dy in rel]
    idx = jnp.stack([2 * k + mc for k in chip_of] + chip_of).astype(jnp.int32)

    (g_win,) = _all_gather_weights([w_in[0].astype(BF)])
    win = _from_blocks(g_win, True)
    wqkv, wuz, wg = win[:, :3 * AW], win[:, 3 * AW:3 * AW + 2 * GMLP_W], win[:, 3 * AW + 2 * GMLP_W:]
    late = [w_branch_attn[0], w_branch_gmlp[0], w_out[0], w_mlp_in[0], w_mlp_out[0]]
    comm = _FsdpComm([w.astype(BF) for w in late], [True, True, False, True, False], g_win, idx, 4 * mx + 2 * my + mc)

    loss_row, grad_x, dwin, small = _local_step(
        x[0], loss_target[0], wqkv, wuz, wg, comm,
        norm_pre_mix, norm_post_mix, norm_pre_mlp, norm_post_mlp, ln_v_gain, ln_v_bias, w_spatial[0], b_spatial[0])

    tie = comm.rs_start("win", [_to_blocks(dwin, True)])
    flat = lambda t: t.reshape(-1, t.shape[-1])
    small_w = dict(zip(SMALL, map(flat, (norm_pre_mix, norm_post_mix, norm_pre_mlp, norm_post_mlp, ln_v_gain, ln_v_bias,
                                         b_spatial, w_spatial))))
    small_m = dict(zip(SMALL, map(flat, (m_norm_pre_mix, m_norm_post_mix, m_norm_pre_mlp, m_norm_post_mlp, m_ln_v_gain,
                                         m_ln_v_bias, m_b_spatial, m_w_spatial))))
    small_v = dict(zip(SMALL, map(flat, (v_norm_pre_mix, v_norm_post_mix, v_norm_pre_mlp, v_norm_post_mlp, v_ln_v_gain,
                                         v_ln_v_bias, v_b_spatial, v_w_spatial))))
    loss_out, sg, sd, sm, sv = _small_step(loss_row + tie[:1], small, small_w, small_m, small_v)
    tie = comm.rs_mid("win", after=(loss_out,))
    mlp_own, mlp_chips = comm.rs_end("mlp", after=(tie,))
    mid_own, mid_chips = comm.rs_end("mid", after=(tie,))
    adam = lambda nm, own, r, w, m, v: _adam_shard(own, r, w[0], m[0], v[0], "adam_" + nm)
    upd = {
        "w_branch_attn": adam("w_branch_attn", mid_own[0], mid_chips[0], w_branch_attn, m_w_branch_attn, v_w_branch_attn),
        "w_branch_gmlp": adam("w_branch_gmlp", mid_own[1], mid_chips[1], w_branch_gmlp, m_w_branch_gmlp, v_w_branch_gmlp),
        "w_out": adam("w_out", mid_own[2], mid_chips[2], w_out, m_w_out, v_w_out),
        "w_mlp_in": adam("w_mlp_in", mlp_own[0], mlp_chips[0], w_mlp_in, m_w_mlp_in, v_w_mlp_in),
        "w_mlp_out": adam("w_mlp_out", mlp_own[1], mlp_chips[1], w_mlp_out, m_w_mlp_out, v_w_mlp_out),
    }
    loss = loss_out[0, 0]
    win_own, win_chips = comm.rs_end("win", after=(upd["w_mlp_in"][0], upd["w_mlp_out"][0], upd["w_out"][0]))
    upd["w_in"] = adam("w_in", win_own[0], win_chips[0], w_in, m_w_in, v_w_in)
    order = ["g1", "w_in", "w_sp", "b_sp", "ln_g", "ln_b", "w_branch_attn", "w_branch_gmlp", "w_out", "g2", "g3",
             "w_mlp_in", "w_mlp_out", "g4"]
    small_shape = dict(g1=norm_pre_mix.shape, g2=norm_post_mix.shape, g3=norm_pre_mlp.shape, g4=norm_post_mlp.shape,
                       ln_g=ln_v_gain.shape, ln_b=ln_v_bias.shape, b_sp=b_spatial.shape, w_sp=w_spatial.shape)

    def pick(which):
        return [upd[nm][which][None] if nm in upd else (sg, sd, sm, sv)[which][nm].reshape(small_shape[nm])
                for nm in order]

    return (loss, grad_x[None], *pick(0), *pick(1), *pick(2), *pick(3))
```

```python
import functools
import math

import jax
import jax.numpy as jnp
from jax import lax
from jax.experimental import pallas as pl
from jax.experimental.pallas import tpu as pltpu

D = 1024
HD = 64
NSLOT = 4
GW = NSLOT * HD
DILS = (1, 4, 16)
QB = 128
ATTN_NSUB = 4
MLP_TM_FWD = 512
MLP_TM_BWD = 256
AW = 3 * GW
GMLP_W = 512
CHUNK = 128
DFF = 4096
EPS = 1e-6
ROPE_THETA = 10000.0
SCALE = HD ** -0.5
NEG = -1e30

ADAM_LR = 0.001
ADAM_B1 = 0.9
ADAM_B2 = 0.999
ADAM_EPS = 1e-08
ADAM_WD = 0.01
ADAM_STEP = 10

BF = jnp.bfloat16
F32 = jnp.float32
MESH = pl.DeviceIdType.MESH
VMEM_LIMIT = 56 * 1024 * 1024


def _cp(sem):
    return pltpu.CompilerParams(dimension_semantics=sem, vmem_limit_bytes=VMEM_LIMIT)


def _dot(a, b):
    return jnp.dot(a, b, preferred_element_type=F32)


def _dot_nt(a, b):
    return lax.dot_general(a, b, (((1,), (1,)), ((), ())), preferred_element_type=F32)


def _dot_tn(a, b):
    return lax.dot_general(a, b, (((0,), (0,)), ((), ())), preferred_element_type=F32)


def _gelu(x):
    return jax.nn.gelu(x, approximate=True)


def _gelu_grad(x):
    k = math.sqrt(2.0 / math.pi)
    t = jnp.tanh(k * (x + 0.044715 * x * x * x))
    return 0.5 * (1.0 + t) + 0.5 * x * (1.0 - t * t) * (k * (1.0 + 3.0 * 0.044715 * x * x))


def _swap_halves(t):
    w = t.shape[1]
    lane = lax.broadcasted_iota(jnp.int32, t.shape, 1)
    first = (lane & (HD - 1)) < (HD // 2)
    return jnp.where(first, pltpu.roll(t, w - HD // 2, 1), pltpu.roll(t, HD // 2, 1))


def _head_mask(hh):
    lane = lax.broadcasted_iota(jnp.int32, (1, GW), 1)
    return jnp.logical_and(lane >= hh * HD, lane < (hh + 1) * HD)


def _rms_stats(xf):
    return lax.rsqrt(jnp.mean(xf * xf, axis=-1, keepdims=True) + EPS)


def _rms_bwd(xf, r, gain, dout):
    n = xf * r
    t = dout * gain
    dx = r * (t - n * jnp.mean(t * n, axis=-1, keepdims=True))
    return dx, jnp.sum(dout * n, axis=0, keepdims=True)


def _rms_fwd(x, gain, name):
    S = x.shape[0]
    tm = min(512, S)

    def body(x_ref, g_ref, h_ref):
        xf = x_ref[...]
        h_ref[...] = (xf * _rms_stats(xf) * g_ref[...]).astype(BF)

    return pl.pallas_call(
        body, name=name, out_shape=jax.ShapeDtypeStruct((S, D), BF), grid=(S // tm,),
        in_specs=[pl.BlockSpec((tm, D), lambda i: (i, 0)), pl.BlockSpec((1, D), lambda i: (0, 0))],
        out_specs=pl.BlockSpec((tm, D), lambda i: (i, 0)), compiler_params=_cp(("parallel",)))(x, gain)


def _proj(h, w, epi, name, tm):
    S, K = h.shape
    N = w.shape[1]
    tm = min(tm, S)

    def body(h_ref, w_ref, o_ref):
        y = _dot(h_ref[...], w_ref[...])
        if epi == "sigmoid":
            y = jax.nn.sigmoid(y)
        o_ref[...] = y.astype(BF)

    return pl.pallas_call(
        body, name=name, out_shape=jax.ShapeDtypeStruct((S, N), BF), grid=(S // tm,),
        in_specs=[pl.BlockSpec((tm, K), lambda i: (i, 0)), _resident((K, N))],
        out_specs=pl.BlockSpec((tm, N), lambda i: (i, 0)),
        compiler_params=_cp(("parallel",)))(h, w)


def _dilate_store(val, scr, o_ref, lead, d):
    rows = val.shape[0]
    if d == 1:
        o_ref[lead + (0,)] = val.astype(o_ref.dtype)
        return
    for hf in range(2):
        scr[hf, pl.ds(0, rows), :] = val[:, hf * 128:(hf + 1) * 128]
    for r in range(d):
        for hf in range(2):
            o_ref[lead + (r, slice(None), slice(hf * 128, (hf + 1) * 128))] = (
                scr[hf, pl.ds(r, rows // d, stride=d), :].astype(o_ref.dtype))


def _undilate_load(i_ref, lead, d, scr, rows):
    if d == 1:
        return i_ref[lead + (0,)].astype(F32)
    for r in range(d):
        for hf in range(2):
            scr[hf, pl.ds(r, rows // d, stride=d), :] = (
                i_ref[lead + (r, slice(None), slice(hf * 128, (hf + 1) * 128))].astype(F32))
    return jnp.concatenate([scr[0, pl.ds(0, rows), :], scr[1, pl.ds(0, rows), :]], axis=1)


def _rope_fwd(y, c_ref, s_ref):
    cosv = jnp.concatenate([c_ref[...]] * 2, axis=1)
    sinv = jnp.concatenate([s_ref[...]] * 2, axis=1)
    return y * cosv + _swap_halves(y) * sinv


def _rope_bwd(dy, c_ref, s_ref):
    cosv = jnp.concatenate([c_ref[...]] * 2, axis=1)
    sinv = jnp.concatenate([s_ref[...]] * 2, axis=1)
    return dy * cosv + _swap_halves(dy * sinv)


def _proj_qkv(h, wqkv, cos_t, sin_t, g, d, name):
    S, K = h.shape
    tm = min(2048, S)

    def body(h_ref, w_ref, c_ref, s_ref, o_ref, scr):
        j = pl.program_id(1)
        y = _dot(h_ref[...], w_ref[...])
        y = jnp.where(j < 2, _rope_fwd(y, c_ref, s_ref), y)
        _dilate_store(y, scr, o_ref, (), d)

    return pl.pallas_call(
        body, name=name, out_shape=jax.ShapeDtypeStruct((3, d, S // d, GW), BF), grid=(S // tm, 3),
        in_specs=[pl.BlockSpec((tm, K), lambda i, j: (i, 0)), pl.BlockSpec((K, GW), lambda i, j: (0, 3 * j + g)),
                  pl.BlockSpec((tm, 128), lambda i, j: (i, 0)), pl.BlockSpec((tm, 128), lambda i, j: (i, 0))],
        out_specs=pl.BlockSpec((None, d, tm // d, GW), lambda i, j: (j, 0, i, 0)),
        scratch_shapes=[pltpu.VMEM((2, tm, 128), F32)],
        compiler_params=_cp(("parallel", "arbitrary")))(h, wqkv, cos_t, sin_t)


def _band_masks(first_step):
    row = lax.broadcasted_iota(jnp.int32, (QB, 2 * QB), 0)
    col = lax.broadcasted_iota(jnp.int32, (QB, 2 * QB), 1)
    band = jnp.logical_and(col >= row, col <= row + QB)
    return band, jnp.logical_and(band, jnp.logical_or(col >= QB, jnp.logical_not(first_step)))


def _attn_fwd(qkv, name):
    _, d, L, _ = qkv.shape
    nsub = min(ATTN_NSUB, L // QB)
    R = nsub * QB
    nsteps = L // R

    def body(q_ref, kp_ref, kc_ref, vp_ref, vc_ref, o_ref, lse_ref):
        i = pl.program_id(1)
        band, band_first = _band_masks(i == 0)
        kfull = jnp.concatenate([kp_ref[...], kc_ref[...]], axis=0)
        vfull = jnp.concatenate([vp_ref[...], vc_ref[...]], axis=0)
        chains = [(sb, hh) for sb in range(nsub) for hh in range(NSLOT)]
        win = lambda t, sb: t[sb * QB:(sb + 2) * QB]
        scores = []
        for sb, hh in chains:
            qh = jnp.where(_head_mask(hh), q_ref[sb * QB:(sb + 1) * QB, :], 0)
            scores.append(_dot_nt(qh, win(kfull, sb)))
        soft = []
        for (sb, hh), sc in zip(chains, scores):
            sc = jnp.where(band_first if sb == 0 else band, sc * SCALE, NEG)
            m = jnp.max(sc, axis=1, keepdims=True)
            p = jnp.exp(sc - m)
            den = jnp.sum(p, axis=1, keepdims=True)
            soft.append((p.astype(BF), den, m + jnp.log(den)))
        accs = [_dot(p, win(vfull, sb)) for (sb, hh), (p, _, _) in zip(chains, soft)]
        for sb in range(nsub):
            o = jnp.zeros((QB, GW), F32)
            lse = jnp.zeros((QB, GW), F32)
            for hh in range(NSLOT):
                hm = _head_mask(hh)
                _, den, lrow = soft[sb * NSLOT + hh]
                o = o + jnp.where(hm, accs[sb * NSLOT + hh] / den, 0.0)
                lse = lse + jnp.where(hm, lrow, 0.0)
            o_ref[sb * QB:(sb + 1) * QB, :] = o
            lse_ref[sb * QB:(sb + 1) * QB, :] = lse

    prev = lambda i: jnp.maximum(i * nsub - 1, 0)
    cur = lambda t: pl.BlockSpec((None, None, R, GW), lambda r, i: (t, r, i, 0))
    prv = lambda t: pl.BlockSpec((None, None, QB, GW), lambda r, i: (t, r, prev(i), 0))
    out = pl.BlockSpec((None, R, GW), lambda r, i: (r, i, 0))
    return pl.pallas_call(
        body, name=name, grid=(d, nsteps),
        out_shape=(jax.ShapeDtypeStruct((d, L, GW), F32), jax.ShapeDtypeStruct((d, L, GW), F32)),
        in_specs=[cur(0), prv(1), cur(1), prv(2), cur(2)],
        out_specs=(out, out), compiler_params=_cp(("parallel", "arbitrary")))(qkv, qkv, qkv, qkv, qkv)


def _attn_combine(os_, lses):
    S = os_[0].shape[1]
    tm = min(1024, S)

    def body(o0, o1, o2, l0, l1, l2, y_ref, j0, j1, j2, scr):
        os_nat = [_undilate_load(o, (), d, scr, tm) for o, d in zip((o0, o1, o2), DILS)]
        a, b, c = [_undilate_load(l, (), d, scr, tm) for l, d in zip((l0, l1, l2), DILS)]
        m = jnp.maximum(jnp.maximum(a, b), c)
        wa, wb, wc = jnp.exp(a - m), jnp.exp(b - m), jnp.exp(c - m)
        den = wa + wb + wc
        y_ref[...] = ((wa * os_nat[0] + wb * os_nat[1] + wc * os_nat[2]) / den).astype(BF)
        lse = m + jnp.log(den)
        for j_ref, d in zip((j0, j1, j2), DILS):
            _dilate_store(lse, scr, j_ref, (), d)

    dil = lambda d: pl.BlockSpec((d, tm // d, GW), lambda i: (0, i, 0))
    dshape = lambda d: jax.ShapeDtypeStruct((d, S // d, GW), F32)
    return pl.pallas_call(
        body, name="attn_combine", grid=(S // tm,),
        out_shape=(jax.ShapeDtypeStruct((S, GW), BF),) + tuple(dshape(d) for d in DILS),
        in_specs=[dil(d) for d in DILS] * 2,
        out_specs=(pl.BlockSpec((tm, GW), lambda i: (i, 0)),) + tuple(dil(d) for d in DILS),
        scratch_shapes=[pltpu.VMEM((2, tm, 128), F32)],
        compiler_params=_cp(("parallel",)))(*os_, *lses)


def _tril(upper=False):
    row = lax.broadcasted_iota(jnp.int32, (CHUNK, CHUNK), 0)
    col = lax.broadcasted_iota(jnp.int32, (CHUNK, CHUNK), 1)
    return row <= col if upper else col <= row


def _ln_fwd(z, gain, bias):
    mu = jnp.mean(z, axis=-1, keepdims=True)
    zc = z - mu
    rstd = lax.rsqrt(jnp.mean(zc * zc, axis=-1, keepdims=True) + EPS)
    zhat = zc * rstd
    return zhat, rstd, zhat * gain + bias


def _gmlp_fwd(puz, ln_g, ln_b, w_sp, b_sp_t):
    S = puz.shape[0]
    tm = min(512, S)
    nch = tm // CHUNK

    def body(p_ref, g_ref, b_ref, w_ref, bt_ref, o_ref):
        tril = _tril()
        ws = [jnp.where(tril, w_ref[gg], 0.0).astype(BF) for gg in range(4)]
        for ch in range(nch):
            rows = slice(ch * CHUNK, (ch + 1) * CHUNK)
            z = _gelu(p_ref[rows, GMLP_W:].astype(F32))
            _, _, zn = _ln_fwd(z, g_ref[...], b_ref[...])
            zn = zn.astype(BF)
            for gg in range(4):
                cols = slice(gg * CHUNK, (gg + 1) * CHUNK)
                sz = _dot(ws[gg], zn[:, cols]) + bt_ref[:, gg:gg + 1]
                u = _gelu(p_ref[rows, cols].astype(F32))
                o_ref[rows, cols] = (u * sz).astype(BF)

    return pl.pallas_call(
        body, name="gmlp_fwd", out_shape=jax.ShapeDtypeStruct((S, GMLP_W), BF), grid=(S // tm,),
        in_specs=[pl.BlockSpec((tm, 2 * GMLP_W), lambda i: (i, 0)),
                  pl.BlockSpec((1, GMLP_W), lambda i: (0, 0)), pl.BlockSpec((1, GMLP_W), lambda i: (0, 0)),
                  pl.BlockSpec((4, CHUNK, CHUNK), lambda i: (0, 0, 0)), pl.BlockSpec((CHUNK, 4), lambda i: (0, 0))],
        out_specs=pl.BlockSpec((tm, GMLP_W), lambda i: (i, 0)),
        compiler_params=_cp(("parallel",)))(puz, ln_g, ln_b, w_sp, b_sp_t)


def _merge_fwd(ya, yg, gates, wba, wbg, wout, x, g2, g3):
    S = x.shape[0]
    tm = min(512, S)

    def body(ya_ref, yg_ref, gt_ref, wba_ref, wbg_ref, wo_ref, x_ref, g2_ref, g3_ref,
             mg_ref, y_ref, x2_ref, h3_ref):
        a = _dot(ya_ref[...], wba_ref[...])
        b = _dot(yg_ref[...], wbg_ref[...])
        merged = (gt_ref[:, :D].astype(F32) * a + gt_ref[:, D:].astype(F32) * b).astype(BF)
        mg_ref[...] = merged
        y = _dot(merged, wo_ref[...])
        y_ref[...] = y
        x2 = x_ref[...] + y * _rms_stats(y) * g2_ref[...]
        x2_ref[...] = x2
        h3_ref[...] = (x2 * _rms_stats(x2) * g3_ref[...]).astype(BF)

    row = lambda w: pl.BlockSpec((tm, w), lambda i: (i, 0))
    full = lambda s: pl.BlockSpec(s, lambda i: (0, 0))
    return pl.pallas_call(
        body, name="merge_fwd", grid=(S // tm,),
        out_shape=(jax.ShapeDtypeStruct((S, D), BF), jax.ShapeDtypeStruct((S, D), F32), jax.ShapeDtypeStruct((S, D), F32),
                   jax.ShapeDtypeStruct((S, D), BF)),
        in_specs=[row(GW), row(GMLP_W), row(2 * D), full((GW, D)), full((GMLP_W, D)), full((D, D)), row(D),
                  full((1, D)), full((1, D))],
        out_specs=(row(D), row(D), row(D), row(D)),
        compiler_params=_cp(("parallel",)))(ya, yg, gates, wba, wbg, wout, x, g2, g3)


def _resident(shape):
    return pl.BlockSpec(shape, lambda i: (0,) * len(shape), pipeline_mode=pl.Buffered(1))


def _mlp_fwd(h3, wmi, wmo, x2, tgt, g4):
    S = x2.shape[0]
    tm = min(MLP_TM_FWD, S)

    def body(h_ref, wi_ref, wo_ref, x2_ref, t_ref, g4_ref, a_ref, dy2_ref, dout_ref, loss_ref, dg4_ref):
        @pl.when(pl.program_id(0) == 0)
        def _():
            loss_ref[...] = jnp.zeros_like(loss_ref)
            dg4_ref[...] = jnp.zeros_like(dg4_ref)

        a = jnp.maximum(_dot(h_ref[...], wi_ref[...]), 0.0)
        a_ref[...] = a.astype(BF)
        y2 = _dot((a * a).astype(BF), wo_ref[...])
        r = _rms_stats(y2)
        out = x2_ref[...] + y2 * r * g4_ref[...]
        err = out - t_ref[...]
        tot = jnp.sum(jnp.sum(err * err, axis=1, keepdims=True), axis=0, keepdims=True) * (0.5 / D)
        lane = lax.broadcasted_iota(jnp.int32, (1, 128), 1)
        loss_ref[...] += jnp.where(lane == 0, tot, 0.0)
        dout = err * (1.0 / D)
        dout_ref[...] = dout
        dy2, dg = _rms_bwd(y2, r, g4_ref[...], dout)
        dy2_ref[...] = dy2.astype(BF)
        dg4_ref[...] += dg

    row = pl.BlockSpec((tm, D), lambda i: (i, 0))
    return pl.pallas_call(
        body, name="mlp_fwd", grid=(S // tm,),
        out_shape=(jax.ShapeDtypeStruct((S, DFF), BF), jax.ShapeDtypeStruct((S, D), BF), jax.ShapeDtypeStruct((S, D), F32),
                   jax.ShapeDtypeStruct((1, 128), F32), jax.ShapeDtypeStruct((1, D), F32)),
        in_specs=[row, _resident((D, DFF)), _resident((DFF, D)), row, row, pl.BlockSpec((1, D), lambda i: (0, 0))],
        out_specs=(pl.BlockSpec((tm, DFF), lambda i: (i, 0)), row, row,
                   pl.BlockSpec((1, 128), lambda i: (0, 0)), pl.BlockSpec((1, D), lambda i: (0, 0))),
        compiler_params=_cp(("arbitrary",)))(h3, wmi, wmo, x2, tgt, g4)


def _mlp_bwd(dy2, a, wmo, wmi, x2, y, dout, g2, g3):
    S = x2.shape[0]
    tm = min(MLP_TM_BWD, S)

    def body(dy2_ref, a_ref, wo_ref, wi_ref, x2_ref, y_ref, dout_ref, g2_ref, g3_ref,
             dpre_ref, dx2_ref, dy_ref, dg3_ref, dg2_ref):
        @pl.when(pl.program_id(0) == 0)
        def _():
            dg3_ref[...] = jnp.zeros_like(dg3_ref)
            dg2_ref[...] = jnp.zeros_like(dg2_ref)

        da2 = _dot_nt(dy2_ref[...], wo_ref[...])
        dpre = (2.0 * a_ref[...].astype(F32) * da2).astype(BF)
        dpre_ref[...] = dpre
        dh3 = _dot_nt(dpre, wi_ref[...])
        x2 = x2_ref[...]
        dx3, dg3 = _rms_bwd(x2, _rms_stats(x2), g3_ref[...], dh3)
        dx2 = dout_ref[...] + dx3
        dx2_ref[...] = dx2
        dg3_ref[...] += dg3
        yv = y_ref[...]
        dy, dg2 = _rms_bwd(yv, _rms_stats(yv), g2_ref[...], dx2)
        dy_ref[...] = dy.astype(BF)
        dg2_ref[...] += dg2

    row = pl.BlockSpec((tm, D), lambda i: (i, 0))
    wide = pl.BlockSpec((tm, DFF), lambda i: (i, 0))
    vec = pl.BlockSpec((1, D), lambda i: (0, 0))
    return pl.pallas_call(
        body, name="mlp_bwd", grid=(S // tm,),
        out_shape=(jax.ShapeDtypeStruct((S, DFF), BF), jax.ShapeDtypeStruct((S, D), F32), jax.ShapeDtypeStruct((S, D), BF),
                   jax.ShapeDtypeStruct((1, D), F32), jax.ShapeDtypeStruct((1, D), F32)),
        in_specs=[row, wide, _resident((DFF, D)), _resident((D, DFF)), row, row, row, vec, vec],
        out_specs=(wide, row, row, vec, vec),
        compiler_params=_cp(("arbitrary",)))(dy2, a, wmo, wmi, x2, y, dout, g2, g3)


def _mm_tn(a, b, name, square_a=False, tm=1024, tn=1024, tk=2048, tie=None, col_blocks=None):
    S, M = a.shape
    N = b.shape[1]
    tm, tk = min(tm, M), min(tk, S)
    tn = max(t for t in range(128, min(tn, N) + 1, 128) if N % t == 0)
    cb = N // col_blocks if col_blocks else tn
    assert M % tm == 0 and S % tk == 0 and tn % cb == 0
    nk = S // tk
    ties = () if tie is None else (tie,)

    def body(a_ref, b_ref, *rest):
        o_ref = rest[-1]
        k = pl.program_id(2)
        av = a_ref[...]
        if square_a:
            av = av * av
        part = _dot_tn(av, b_ref[...])
        if col_blocks:
            part = jnp.stack([part[:, t * cb:(t + 1) * cb] for t in range(tn // cb)])

        @pl.when(k == 0)
        def _():
            o_ref[...] = part

        @pl.when(k > 0)
        def _():
            o_ref[...] += part

    if col_blocks:
        out_shape, out_spec = (col_blocks, M, cb), pl.BlockSpec((tn // cb, tm, cb), lambda i, j, k: (j, i, 0))
    else:
        out_shape, out_spec = (M, N), pl.BlockSpec((tm, tn), lambda i, j, k: (i, j))
    return pl.pallas_call(
        body, name=name, out_shape=jax.ShapeDtypeStruct(out_shape, F32), grid=(M // tm, N // tn, nk),
        in_specs=[pl.BlockSpec((tk, tm), lambda i, j, k: (k, i)), pl.BlockSpec((tk, tn), lambda i, j, k: (k, j))]
        + [pl.BlockSpec(memory_space=pl.ANY)] * len(ties),
        out_specs=out_spec,
        compiler_params=_cp(("parallel", "parallel", "arbitrary")))(a, b, *ties)


def _outproj_bwd(dy, wout, ya, yg, gates, wba, wbg, tie):
    S = dy.shape[0]
    tm = min(512, S)

    def body(dy_ref, wo_ref, ya_ref, yg_ref, gt_ref, wba_ref, wbg_ref, tie_ref,
             dgt_ref, da_ref, db_ref, dyg_ref, e0, e1, e2, s0, s1, s2, scr):
        dm = _dot_nt(dy_ref[...], wo_ref[...])
        ga, gb = gt_ref[:, :D].astype(F32), gt_ref[:, D:].astype(F32)
        dgt_ref[:, :D] = (dm * _dot(ya_ref[...], wba_ref[...]) * ga * (1.0 - ga)).astype(BF)
        dgt_ref[:, D:] = (dm * _dot(yg_ref[...], wbg_ref[...]) * gb * (1.0 - gb)).astype(BF)
        da = (dm * ga).astype(BF)
        db = (dm * gb).astype(BF)
        da_ref[...] = da
        db_ref[...] = db
        dyg_ref[...] = _dot_nt(db, wbg_ref[...]).astype(BF)
        dya = _dot_nt(da, wba_ref[...]).astype(BF).astype(F32)
        dyy = dya * ya_ref[...].astype(F32)
        dsum = jnp.zeros((tm, GW), F32)
        for hh in range(NSLOT):
            hm = _head_mask(hh)
            dsum = dsum + jnp.where(hm, jnp.sum(jnp.where(hm, dyy, 0.0), axis=1, keepdims=True), 0.0)
        for e_ref, s_ref, d in zip((e0, e1, e2), (s0, s1, s2), DILS):
            _dilate_store(dya, scr, e_ref, (), d)
            _dilate_store(dsum, scr, s_ref, (), d)

    row = lambda w: pl.BlockSpec((tm, w), lambda i: (i, 0))
    full = lambda s: pl.BlockSpec(s, lambda i: (0, 0))
    dil = lambda d: pl.BlockSpec((d, tm // d, GW), lambda i: (0, i, 0))
    dshape = lambda d, t: jax.ShapeDtypeStruct((d, S // d, GW), t)
    return pl.pallas_call(
        body, name="outproj_bwd", grid=(S // tm,),
        out_shape=(jax.ShapeDtypeStruct((S, 2 * D), BF), jax.ShapeDtypeStruct((S, D), BF), jax.ShapeDtypeStruct((S, D), BF),
                   jax.ShapeDtypeStruct((S, GMLP_W), BF)) + tuple(dshape(d, BF) for d in DILS)
        + tuple(dshape(d, F32) for d in DILS),
        in_specs=[row(D), full((D, D)), row(GW), row(GMLP_W), row(2 * D), full((GW, D)), full((GMLP_W, D)),
                  pl.BlockSpec(memory_space=pl.ANY)],
        out_specs=(row(2 * D), row(D), row(D), row(GMLP_W)) + tuple(dil(d) for d in DILS) * 2,
        scratch_shapes=[pltpu.VMEM((2, tm, 128), F32)],
        compiler_params=_cp(("parallel",)))(dy, wout, ya, yg, gates, wba, wbg, tie)


def _gmlp_bwd(puz, dyg, ln_g, ln_b, w_sp, w_sp_t, b_sp_t):
    S = puz.shape[0]
    tm = min(512, S)
    nch = tm // CHUNK

    def body(p_ref, dy_ref, g_ref, b_ref, w_ref, wt_ref, bt_ref,
             dp_ref, dw_ref, dbs_ref, dg_ref, dbias_ref, dbacc_ref):
        i = pl.program_id(0)

        @pl.when(i == 0)
        def _():
            dw_ref[...] = jnp.zeros_like(dw_ref)
            dbacc_ref[...] = jnp.zeros_like(dbacc_ref)
            dg_ref[...] = jnp.zeros_like(dg_ref)
            dbias_ref[...] = jnp.zeros_like(dbias_ref)

        tril = _tril()
        ws = [jnp.where(tril, w_ref[gg], 0.0).astype(BF) for gg in range(4)]
        triu = _tril(upper=True)
        wts = [jnp.where(triu, wt_ref[gg], 0.0).astype(BF) for gg in range(4)]
        gain = g_ref[...]
        for ch in range(nch):
            rows = slice(ch * CHUNK, (ch + 1) * CHUNK)
            pz = p_ref[rows, GMLP_W:].astype(F32)
            z = _gelu(pz)
            zhat, rstd, zn = _ln_fwd(z, gain, b_ref[...])
            znb = zn.astype(BF)
            dzn_parts = []
            for gg in range(4):
                cols = slice(gg * CHUNK, (gg + 1) * CHUNK)
                pu = p_ref[rows, cols].astype(F32)
                u = _gelu(pu)
                sz = _dot(ws[gg], znb[:, cols]) + bt_ref[:, gg:gg + 1]
                dyv = dy_ref[rows, cols].astype(F32)
                dp_ref[rows, cols] = (dyv * sz * _gelu_grad(pu)).astype(BF)
                dsz = dyv * u
                dbacc_ref[gg] += dsz
                dszb = dsz.astype(BF)
                dw_ref[gg] += _dot_nt(dszb, znb[:, cols])
                dzn_parts.append(_dot(wts[gg], dszb))
            dzn = jnp.concatenate(dzn_parts, axis=1)
            dg_ref[...] += jnp.sum(dzn * zhat, axis=0, keepdims=True)
            dbias_ref[...] += jnp.sum(dzn, axis=0, keepdims=True)
            dzh = dzn * gain
            dz = rstd * (dzh - jnp.mean(dzh, axis=-1, keepdims=True)
                         - zhat * jnp.mean(dzh * zhat, axis=-1, keepdims=True))
            dp_ref[rows, GMLP_W:] = (dz * _gelu_grad(pz)).astype(BF)

        @pl.when(i == pl.num_programs(0) - 1)
        def _():
            for gg in range(4):
                dw_ref[gg] = jnp.where(tril, dw_ref[gg], 0.0)
                dbs_ref[gg] = jnp.sum(dbacc_ref[gg], axis=1, keepdims=True)

    full2 = lambda s: pl.BlockSpec(s, lambda i: (0, 0))
    full3 = lambda s: pl.BlockSpec(s, lambda i: (0, 0, 0))
    return pl.pallas_call(
        body, name="gmlp_bwd", grid=(S // tm,),
        out_shape=(jax.ShapeDtypeStruct((S, 2 * GMLP_W), BF), jax.ShapeDtypeStruct((4, CHUNK, CHUNK), F32),
                   jax.ShapeDtypeStruct((4, CHUNK, 1), F32), jax.ShapeDtypeStruct((1, GMLP_W), F32),
                   jax.ShapeDtypeStruct((1, GMLP_W), F32)),
        in_specs=[pl.BlockSpec((tm, 2 * GMLP_W), lambda i: (i, 0)), pl.BlockSpec((tm, GMLP_W), lambda i: (i, 0)),
                  full2((1, GMLP_W)), full2((1, GMLP_W)), full3((4, CHUNK, CHUNK)), full3((4, CHUNK, CHUNK)),
                  full2((CHUNK, 4))],
        out_specs=(pl.BlockSpec((tm, 2 * GMLP_W), lambda i: (i, 0)), full3((4, CHUNK, CHUNK)), full3((4, CHUNK, 1)),
                   full2((1, GMLP_W)), full2((1, GMLP_W))),
        scratch_shapes=[pltpu.VMEM((4, CHUNK, CHUNK), F32)],
        compiler_params=_cp(("arbitrary",)))(puz, dyg, ln_g, ln_b, w_sp, w_sp_t, b_sp_t)


def _attn_bwd(qkv, dya, dsums, lse, name):
    _, d, L, _ = qkv.shape
    nsub = min(ATTN_NSUB, L // QB)
    R = nsub * QB
    nsteps = L // R

    def body(q_ref, qn_ref, kp_ref, kc_ref, vp_ref, vc_ref, dy_ref, dyn_ref, e_ref, en_ref, l_ref, ln_ref, o_ref):
        i = pl.program_id(1)
        band, band_first = _band_masks(i == 0)
        row = lax.broadcasted_iota(jnp.int32, (QB, QB), 0)
        col = lax.broadcasted_iota(jnp.int32, (QB, QB), 1)
        mask_next = jnp.logical_and(col >= row, i < nsteps - 1)
        kc, vc = kc_ref[...], vc_ref[...]
        kfull = jnp.concatenate([kp_ref[...], kc], axis=0)
        vfull = jnp.concatenate([vp_ref[...], vc], axis=0)
        k_last, v_last = kc[(nsub - 1) * QB:], vc[(nsub - 1) * QB:]
        q_ext = jnp.concatenate([q_ref[...], qn_ref[...]], axis=0)
        dy_ext = jnp.concatenate([dy_ref[...], dyn_ref[...]], axis=0)
        esum, esum_n, lse, lse_n = e_ref[...], en_ref[...], l_ref[...], ln_ref[...]
        win = lambda t, sb: t[sb * QB:(sb + 2) * QB]
        blk = lambda t, sb: t[sb * QB:(sb + 1) * QB]
        hms = [_head_mask(hh) for hh in range(NSLOT)]
        q_hs = [jnp.where(hm, q_ext, 0) for hm in hms]
        dy_hs = [jnp.where(hm, dy_ext, 0) for hm in hms]
        raw = []
        for hh in range(NSLOT):
            tiles = [(_dot_nt(blk(q_hs[hh], sb), win(kfull, sb)), _dot_nt(blk(dy_hs[hh], sb), win(vfull, sb)))
                     for sb in range(nsub)]
            tiles.append((_dot_nt(blk(q_hs[hh], nsub), k_last), _dot_nt(blk(dy_hs[hh], nsub), v_last)))
            raw.append(tiles)
        ps, dss = [], []
        for hh in range(NSLOT):
            rowstat = lambda t: jnp.max(jnp.where(hms[hh], t, -jnp.inf), axis=1, keepdims=True)
            p_h, ds_h = [], []
            for sb in range(nsub + 1):
                sc, dp = raw[hh][sb]
                if sb < nsub:
                    msk, lrow, erow = (band_first if sb == 0 else band), rowstat(blk(lse, sb)), rowstat(blk(esum, sb))
                else:
                    msk, lrow, erow = mask_next, rowstat(lse_n), rowstat(esum_n)
                p = jnp.where(msk, jnp.exp(sc * SCALE - lrow), 0.0)
                p_h.append(p.astype(BF))
                ds_h.append((p * (dp - erow)).astype(BF))
            ps.append(p_h)
            dss.append(ds_h)
        dq = [jnp.zeros((QB, GW), F32) for _ in range(nsub)]
        dk = [jnp.zeros((QB, GW), F32) for _ in range(nsub)]
        dv = [jnp.zeros((QB, GW), F32) for _ in range(nsub)]
        for hh in range(NSLOT):
            for sb in range(nsub):
                dq[sb] = dq[sb] + jnp.where(hms[hh], _dot(dss[hh][sb], win(kfull, sb)), 0.0)
                nxt = lambda t: t[sb + 1][:, :QB] if sb + 1 < nsub else t[nsub]
                dk[sb] = dk[sb] + _dot_tn(jnp.concatenate([dss[hh][sb][:, QB:], nxt(dss[hh])], axis=0), win(q_hs[hh], sb))
                dv[sb] = dv[sb] + _dot_tn(jnp.concatenate([ps[hh][sb][:, QB:], nxt(ps[hh])], axis=0), win(dy_hs[hh], sb))
        for sb in range(nsub):
            rows = slice(sb * QB, (sb + 1) * QB)
            o_ref[0, rows, :] = (dq[sb] * SCALE).astype(BF)
            o_ref[1, rows, :] = (dk[sb] * SCALE).astype(BF)
            o_ref[2, rows, :] = dv[sb].astype(BF)

    prev = lambda i: jnp.maximum(i * nsub - 1, 0)
    nxt = lambda i: jnp.minimum((i + 1) * nsub, L // QB - 1)
    cur4 = lambda t: pl.BlockSpec((None, None, R, GW), lambda r, i: (t, r, i, 0))
    prv4 = lambda t: pl.BlockSpec((None, None, QB, GW), lambda r, i: (t, r, prev(i), 0))
    nxt4 = lambda t: pl.BlockSpec((None, None, QB, GW), lambda r, i: (t, r, nxt(i), 0))
    cur3 = pl.BlockSpec((None, R, GW), lambda r, i: (r, i, 0))
    nxt3 = pl.BlockSpec((None, QB, GW), lambda r, i: (r, nxt(i), 0))
    return pl.pallas_call(
        body, name=name, grid=(d, nsteps), out_shape=jax.ShapeDtypeStruct((3, d, L, GW), BF),
        in_specs=[cur4(0), nxt4(0), prv4(1), cur4(1), prv4(2), cur4(2), cur3, nxt3, cur3, nxt3, cur3, nxt3],
        out_specs=pl.BlockSpec((3, None, R, GW), lambda r, i: (0, r, i, 0)),
        compiler_params=_cp(("parallel", "arbitrary")))(qkv, qkv, qkv, qkv, qkv, qkv, dya, dya, dsums, dsums, lse, lse)


def _inproj_bwd(dqkvs, dpuz, dgates, wqkv, wuz, wg, cos_t, sin_t, x, dx2, g1):
    S = x.shape[0]
    tm = min(512, S)

    def body(d0_ref, d1_ref, d2_ref, dp_ref, dg_ref, wqkv_ref, wuz_ref, wg_ref, c_ref, s_ref,
             x_ref, dx2_ref, g1_ref, gx_ref, dg1_ref, dn_ref, scr):
        i = pl.program_id(0)

        @pl.when(i == 0)
        def _():
            dg1_ref[...] = jnp.zeros_like(dg1_ref)

        for t in range(3):
            for g, (d_ref, d) in enumerate(zip((d0_ref, d1_ref, d2_ref), DILS)):
                piece = _undilate_load(d_ref, (t,), d, scr, tm)
                if t < 2:
                    piece = _rope_bwd(piece, c_ref, s_ref)
                dn_ref[:, (3 * t + g) * GW:(3 * t + g + 1) * GW] = piece.astype(BF)
        dh = (_dot_nt(dn_ref[...], wqkv_ref[...]) + _dot_nt(dp_ref[...], wuz_ref[...])
              + _dot_nt(dg_ref[...], wg_ref[...]))
        xv = x_ref[...]
        dx1, dg1 = _rms_bwd(xv, _rms_stats(xv), g1_ref[...], dh)
        gx_ref[...] = dx2_ref[...] + dx1
        dg1_ref[...] += dg1

    row = lambda w: pl.BlockSpec((tm, w), lambda i: (i, 0))
    full = lambda s: pl.BlockSpec(s, lambda i: (0, 0))
    dil = lambda d: pl.BlockSpec((3, d, tm // d, GW), lambda i: (0, 0, i, 0))
    return pl.pallas_call(
        body, name="inproj_bwd", grid=(S // tm,),
        out_shape=(jax.ShapeDtypeStruct((S, D), F32), jax.ShapeDtypeStruct((1, D), F32),
                   jax.ShapeDtypeStruct((S, 3 * AW), BF)),
        in_specs=[dil(d) for d in DILS] + [row(2 * GMLP_W), row(2 * D), full((D, 3 * AW)), full((D, 2 * GMLP_W)),
                                            full((D, 2 * D)), row(128), row(128), row(D), row(D), full((1, D))],
        out_specs=(row(D), full((1, D)), row(3 * AW)),
        scratch_shapes=[pltpu.VMEM((2, tm, 128), F32)],
        compiler_params=_cp(("arbitrary",)))(*dqkvs, dpuz, dgates, wqkv, wuz, wg, cos_t, sin_t, x, dx2, g1)


def _adam_math(w, g, m, v):
    m2 = ADAM_B1 * m + (1.0 - ADAM_B1) * g
    v2 = ADAM_B2 * v + (1.0 - ADAM_B2) * (g * g)
    m_hat = m2 / (1.0 - ADAM_B1 ** ADAM_STEP)
    v_hat = v2 / (1.0 - ADAM_B2 ** ADAM_STEP)
    delta = -ADAM_LR * (m_hat / (jnp.sqrt(v_hat) + ADAM_EPS) + ADAM_WD * w)
    return delta, m2, v2


def _adam_shard(own, recv, w, m, v, name):
    R, C = w.shape
    tr = min(256, R)

    def body(own_ref, r_ref, w_ref, m_ref, v_ref, g_ref, d_ref, m2_ref, v2_ref):
        g = own_ref[...] + r_ref[0].astype(F32) + r_ref[1].astype(F32) + r_ref[2].astype(F32)
        g_ref[...] = g
        d_ref[...], m2_ref[...], v2_ref[...] = _adam_math(w_ref[...], g, m_ref[...], v_ref[...])

    spec = pl.BlockSpec((tr, C), lambda i: (i, 0))
    out = jax.ShapeDtypeStruct((R, C), F32)
    return pl.pallas_call(
        body, name=name, grid=(R // tr,), out_shape=(out, out, out, out),
        in_specs=[spec, pl.BlockSpec((3, tr, C), lambda i: (0, i, 0)), spec, spec, spec],
        out_specs=(spec, spec, spec, spec), compiler_params=_cp(("parallel",)))(own, recv, w, m, v)


def _rs_add(gblocks, recv, idx, name):
    _, R, C = gblocks.shape
    tr = min(256, R)

    def body(t_ref, g_ref, r_ref, own_ref, send_ref):
        j = pl.program_id(1)
        s = g_ref[...] + r_ref[...]

        @pl.when(j == 0)
        def _():
            own_ref[...] = s

        @pl.when(j > 0)
        def _():
            send_ref[...] = s.astype(BF)

    grid_spec = pltpu.PrefetchScalarGridSpec(
        num_scalar_prefetch=1, grid=(R // tr, 4),
        in_specs=[pl.BlockSpec((None, tr, C), lambda i, j, t: (t[j], i, 0)),
                  pl.BlockSpec((None, tr, C), lambda i, j, t: (t[4 + j], i, 0))],
        out_specs=[pl.BlockSpec((tr, C), lambda i, j, t: (i, 0)),
                   pl.BlockSpec((None, tr, C), lambda i, j, t: (jnp.maximum(j - 1, 0), i, 0))])
    return pl.pallas_call(
        body, name=name, grid_spec=grid_spec,
        out_shape=(jax.ShapeDtypeStruct((R, C), F32), jax.ShapeDtypeStruct((3, R, C), BF)),
        compiler_params=_cp(("parallel", "arbitrary")))(idx, gblocks, recv)


def _mesh_pos():
    return lax.axis_index("x"), lax.axis_index("y"), lax.axis_index("c")


def _all_gather_weights(shards):
    n = len(shards)

    def body(*refs):
        ins, outs = refs[:n], refs[n:2 * n]
        send_sems, recv_sems, loc_sems = refs[2 * n:]
        x, y, c = _mesh_pos()
        me, sib = (x, y, c), (x, y, 1 - c)
        chips = [(1 - x, y), (x, 1 - y), (1 - x, 1 - y)]

        def rcopy(k, s, block, to, src=None):
            dst = outs[k].at[4 * block[0] + 2 * block[1] + block[2]]
            return pltpu.make_async_remote_copy(
                src_ref=dst if src is None else src, dst_ref=dst, send_sem=send_sems.at[k, s],
                recv_sem=recv_sems.at[k, s], device_id=to, device_id_type=MESH)

        locs = [pltpu.make_async_copy(ins[k], outs[k].at[4 * x + 2 * y + c], loc_sems.at[k]) for k in range(n)]
        for cp in locs:
            cp.start()
        started = []
        for k in range(n):
            for j, chip in enumerate(chips):
                started.append(rcopy(k, 1 + j, me, (*chip, c), src=ins[k]))
                started[-1].start()
        for k in range(n):
            started.append(rcopy(k, 0, me, sib, src=ins[k]))
            started[-1].start()
        for k in range(n):
            for j, chip in enumerate(chips):
                rcopy(k, 1 + j, (*chip, c), me).wait_recv()
                started.append(rcopy(k, 4 + j, (*chip, c), sib))
                started[-1].start()
        for k in range(n):
            rcopy(k, 0, sib, me).wait_recv()
            for j, chip in enumerate(chips):
                rcopy(k, 4 + j, (*chip, 1 - c), me).wait_recv()
        for cp in started:
            cp.wait_send()
        for cp in locs:
            cp.wait()

    hbm = pl.BlockSpec(memory_space=pl.ANY)
    return pl.pallas_call(
        body, name="ag_weights",
        out_shape=tuple(jax.ShapeDtypeStruct((8,) + s.shape, s.dtype) for s in shards),
        in_specs=[hbm] * n, out_specs=(hbm,) * n,
        scratch_shapes=[pltpu.SemaphoreType.DMA((n, 7)), pltpu.SemaphoreType.DMA((n, 7)),
                        pltpu.SemaphoreType.DMA((n,))])(*shards)


_HBM = pl.BlockSpec(memory_space=pltpu.HBM)
_SEM = pl.BlockSpec(memory_space=pltpu.SEMAPHORE)
_EFFECT = pltpu.SideEffectType.DATAFLOW_SIDE_EFFECTING
_RELATIONS = [(dx, dy, dc) for dx in (0, 1) for dy in (0, 1) for dc in (0, 1)][1:]


def _flip(v, d):
    return 1 - v if d else v


def _plan_gather(n):
    def plan(x, y, c):
        return [(k, None, 4 * x + 2 * y + c, (_flip(x, dx), _flip(y, dy), _flip(c, dc)))
                for k in range(n) for dx, dy, dc in _RELATIONS]
    return plan


def _plan_d2d(n):
    def plan(x, y, c):
        return [(k, 2 * kk + 1 - c, kk, (x, y, 1 - c)) for k in range(n) for kk in range(4)]
    return plan


def _plan_ici(n):
    def plan(x, y, c):
        return [(k, j, j, (_flip(x, dx), _flip(y, dy), c))
                for k in range(n) for j, (dx, dy) in enumerate(((1, 0), (0, 1), (1, 1)))]
    return plan


def _plan_copies(plan, src_refs, land_refs, send_sems, recv_sems):
    x, y, c = _mesh_pos()
    return [pltpu.make_async_remote_copy(
        src_ref=src_refs[k] if si is None else src_refs[k].at[si], dst_ref=land_refs[k].at[di],
        send_sem=send_sems.at[n], recv_sem=recv_sems.at[n], device_id=dev, device_id_type=MESH)
        for n, (k, si, di, dev) in enumerate(plan(x, y, c))]


def _exchange_start(srcs, land_shapes, plan, ncopies, name, after):
    n = len(srcs)

    def body(*refs):
        src_refs, land_refs = refs[:n], refs[n:2 * n]
        send_sems, recv_sems = refs[2 * n + len(after)], refs[2 * n + len(after) + 1]
        token = refs[-1]
        for cp in _plan_copies(plan, src_refs, land_refs, send_sems, recv_sems):
            cp.start()
        token[...] = jnp.zeros_like(token)

    lands = [pltpu.with_memory_space_constraint(lax.empty(s, a.dtype), pltpu.HBM) for s, a in zip(land_shapes, srcs)]
    srcs = [pltpu.with_memory_space_constraint(a, pltpu.HBM) for a in srcs]
    outs = pl.pallas_call(
        body, name=name,
        out_shape=(pltpu.SemaphoreType.DMA((ncopies,)), pltpu.SemaphoreType.DMA((ncopies,)))
        + tuple(pltpu.HBM(a.shape, a.dtype) for a in srcs) + tuple(pltpu.HBM(a.shape, a.dtype) for a in lands)
        + (jax.ShapeDtypeStruct((8, 128), F32),),
        in_specs=[_HBM] * (2 * n) + [pl.BlockSpec(memory_space=pl.ANY)] * len(after),
        out_specs=(_SEM, _SEM) + (_HBM,) * (2 * n) + (pl.BlockSpec(memory_space=pltpu.VMEM),),
        input_output_aliases={i: 2 + i for i in range(2 * n)},
        compiler_params=pltpu.CompilerParams(has_side_effects=_EFFECT))(*srcs, *lands, *after)
    return (outs[0], outs[1], list(outs[2:2 + n]), list(outs[2 + n:2 + 2 * n])), outs[-1]


def _exchange_wait(handle, plan, name, after):
    send_sems, recv_sems, srcs, lands = handle
    n = len(srcs)

    def body(*refs):
        src_refs, land_refs = refs[:n], refs[n:2 * n]
        for cp in _plan_copies(plan, src_refs, land_refs, refs[2 * n], refs[2 * n + 1]):
            cp.wait_send()
            cp.wait_recv()

    outs = pl.pallas_call(
        body, name=name,
        out_shape=tuple(pltpu.HBM(a.shape, a.dtype) for a in srcs) + tuple(pltpu.HBM(a.shape, a.dtype) for a in lands),
        in_specs=[_HBM] * (2 * n) + [_SEM, _SEM] + [pl.BlockSpec(memory_space=pl.ANY)] * len(after),
        out_specs=(_HBM,) * (2 * n), input_output_aliases={i: i for i in range(2 * n)},
        compiler_params=pltpu.CompilerParams(has_side_effects=_EFFECT))(*srcs, *lands, send_sems, recv_sems, *after)
    return list(outs[:n]), list(outs[n:])


SMALL = ("g1", "g2", "g3", "g4", "ln_g", "ln_b", "b_sp", "w_sp")


def _small_step(loss_row, grads, ws, ms, vs):
    parts = [loss_row] + [grads[k] for k in SMALL]
    n, ns = len(parts), len(SMALL)

    def body(*refs):
        p_refs = refs[:n]
        w_refs, m_refs, v_refs = (refs[n + i * ns:n + (i + 1) * ns] for i in range(3))
        o = n + 3 * ns
        loss_ref = refs[o]
        g_out, d_out, m_out, v_out = (refs[o + 1 + i * ns:o + 1 + (i + 1) * ns] for i in range(4))
        bufs = refs[o + 1 + 4 * ns:o + 1 + 4 * ns + n]
        send_sems, recv_sems = refs[-2:]
        x, y, c = _mesh_pos()
        me, sib = (x, y, c), (x, y, 1 - c)
        chips = [(1 - x, y), (x, 1 - y), (1 - x, 1 - y)]

        def rcopy(k, s, block, to, src=None):
            dst = bufs[k].at[4 * block[0] + 2 * block[1] + block[2]]
            return pltpu.make_async_remote_copy(
                src_ref=dst if src is None else src, dst_ref=dst, send_sem=send_sems.at[k, s],
                recv_sem=recv_sems.at[k, s], device_id=to, device_id_type=MESH)

        started = []
        for k in range(n):
            bufs[k][4 * x + 2 * y + c] = p_refs[k][...]
            started.append(rcopy(k, 0, me, sib, src=p_refs[k]))
            started += [rcopy(k, 1 + j, me, (*chip, c), src=p_refs[k]) for j, chip in enumerate(chips)]
        for cp in started:
            cp.start()
        for k in range(n):
            for j, chip in enumerate(chips):
                rcopy(k, 1 + j, (*chip, c), me).wait_recv()
                started.append(rcopy(k, 4 + j, (*chip, c), sib))
                started[-1].start()
        for k in range(n):
            rcopy(k, 0, sib, me).wait_recv()
            for j, chip in enumerate(chips):
                rcopy(k, 4 + j, (*chip, 1 - c), me).wait_recv()
        for cp in started:
            cp.wait_send()
        sums = []
        for k in range(n):
            acc = bufs[k][0]
            for b in range(1, 8):
                acc = acc + bufs[k][b]
            sums.append(acc)
        loss_ref[...] = sums[0]
        for i in range(ns):
            g_out[i][...] = sums[1 + i]
            d_out[i][...], m_out[i][...], v_out[i][...] = _adam_math(w_refs[i][...], sums[1 + i], m_refs[i][...],
                                                                     v_refs[i][...])

    args = parts + [t[k] for t in (ws, ms, vs) for k in SMALL]
    shapes = [jax.ShapeDtypeStruct(loss_row.shape, F32)] + [jax.ShapeDtypeStruct(grads[k].shape, F32) for k in SMALL] * 4
    vmem = pl.BlockSpec(memory_space=pltpu.VMEM)
    outs = pl.pallas_call(
        body, name="small_step", out_shape=tuple(shapes), in_specs=[vmem] * len(args), out_specs=(vmem,) * len(shapes),
        scratch_shapes=[pltpu.VMEM((8,) + p.shape, F32) for p in parts]
        + [pltpu.SemaphoreType.DMA((n, 7)), pltpu.SemaphoreType.DMA((n, 7))],
        compiler_params=pltpu.CompilerParams(vmem_limit_bytes=VMEM_LIMIT))(*args)
    groups = [dict(zip(SMALL, outs[1 + i * ns:1 + (i + 1) * ns])) for i in range(4)]
    return (outs[0], *groups)


def _rope_tables(S):
    half = HD // 2
    inv_freq = ROPE_THETA ** (-jnp.arange(half, dtype=F32) / half)
    ang = jnp.arange(S, dtype=F32)[:, None] * inv_freq[None, :]
    cos, sin = jnp.cos(ang), jnp.sin(ang)
    return jnp.concatenate([cos, cos, cos, cos], axis=1), jnp.concatenate([-sin, sin, -sin, sin], axis=1)


def _to_blocks(g, col_sharded):
    if col_sharded:
        return g.reshape(g.shape[0], 8, g.shape[1] // 8).transpose(1, 0, 2)
    return g.reshape(8, g.shape[0] // 8, g.shape[1])


def _from_blocks(t, col_sharded):
    if col_sharded:
        return t.transpose(1, 0, 2).reshape(t.shape[1], 8 * t.shape[2])
    return t.reshape(8 * t.shape[1], t.shape[2])


class _NoComm:
    def __init__(self, late_weights):
        self._late = late_weights
        self.grads = {}

    def start_token(self):
        return 0.0

    def late_weights(self, after):
        return self._late

    def rs_start(self, key, gblocks):
        self.grads[key] = gblocks
        return jnp.zeros((8, 128), F32)

    def rs_mid(self, key, after):
        return jnp.zeros((8, 128), F32)


class _FsdpComm:
    def __init__(self, late_shards, col_sharded, after, idx, block):
        self._col, self._idx, self._block, self._rs = col_sharded, idx, block, {}
        n = len(late_shards)
        self._gather, self._token = _exchange_start(
            late_shards, [(8,) + s.shape for s in late_shards], _plan_gather(n), 7 * n, "ag_late_start", (after,))

    def start_token(self):
        return self._token[0, 0]

    def late_weights(self, after):
        shards, lands = _exchange_wait(self._gather, _plan_gather(len(self._col)), "ag_late_wait", after)
        lands = [lax.dynamic_update_index_in_dim(t, s, self._block, 0) for t, s in zip(lands, shards)]
        return [_from_blocks(t, cs) for t, cs in zip(lands, self._col)]

    def rs_start(self, key, gblocks):
        n = len(gblocks)
        d2d, token = _exchange_start(gblocks, [(4,) + g.shape[1:] for g in gblocks], _plan_d2d(n), 4 * n,
                                     "rs_%s_d2d_start" % key, (self._token,))
        self._rs[key] = dict(n=n, d2d=d2d)
        return token

    def rs_mid(self, key, after):
        st = self._rs[key]
        gblocks, from_sib = _exchange_wait(st["d2d"], _plan_d2d(st["n"]), "rs_%s_d2d_wait" % key, after)
        halves = [_rs_add(g, r, self._idx, "rs_add_%s_%d" % (key, k)) for k, (g, r) in enumerate(zip(gblocks, from_sib))]
        st["own"] = [own for own, _ in halves]
        sends = [send for _, send in halves]
        st["ici"], token = _exchange_start(sends, [t.shape for t in sends], _plan_ici(st["n"]), 3 * st["n"],
                                           "rs_%s_ici_start" % key, (self._token,))
        return token

    def rs_end(self, key, after):
        st = self._rs[key]
        return st["own"], _exchange_wait(st["ici"], _plan_ici(st["n"]), "rs_%s_ici_wait" % key, after)[1]


def _local_step(x, tgt, wqkv, wuz, wg, comm, g1, g2, g3, g4, ln_g, ln_b, w_sp, b_sp):
    S = x.shape[0]
    cos_t, sin_t = _rope_tables(S)
    b_sp_t = b_sp.T
    w_sp_t = w_sp.transpose(0, 2, 1)

    h1 = _rms_fwd(x, g1 + comm.start_token(), "rms_pre_mix")
    qkvs = [_proj_qkv(h1, wqkv, cos_t, sin_t, g, d, "proj_qkv_d%d" % d) for g, d in enumerate(DILS)]
    puz = _proj(h1, wuz, "none", "proj_uz", 1024)
    gates = _proj(h1, wg, "sigmoid", "proj_gates", 512)
    fwd = [_attn_fwd(t, "attn_fwd_d%d" % d) for t, d in zip(qkvs, DILS)]
    ya, *lses = _attn_combine([o for o, _ in fwd], [l for _, l in fwd])
    yg = _gmlp_fwd(puz, ln_g, ln_b, w_sp, b_sp_t)
    wba, wbg, wout, wmi, wmo = comm.late_weights(after=(ya, yg, gates))
    merged, y, x2, h3 = _merge_fwd(ya, yg, gates, wba, wbg, wout, x, g2, g3)
    a, dy2, dout, loss_row, dg4 = _mlp_fwd(h3, wmi, wmo, x2, tgt, g4)

    dpre, dx2, dy, dg3, dg2 = _mlp_bwd(dy2, a, wmo, wmi, x2, y, dout, g2, g3)
    dwmo = _mm_tn(a, dy2, "dw_mlp_out", square_a=True)
    dwmi = _mm_tn(h3, dpre, "dw_mlp_in", col_blocks=8)
    tie = comm.rs_start("mlp", [dwmi, _to_blocks(dwmo, False)])
    dgates, da, db, dyg, *rest = _outproj_bwd(dy, wout, ya, yg, gates, wba, wbg, tie)
    dyas, dsums = rest[:3], rest[3:]
    tie = comm.rs_mid("mlp", after=(dyg,))
    dpuz, dwsp, dbs, dlng, dlnb = _gmlp_bwd(puz, dyg, ln_g + tie[0, 0], ln_b, w_sp, w_sp_t, b_sp_t)
    dqkvs = [_attn_bwd(qkvs[g], dyas[g], dsums[g], lses[g], "attn_bwd_d%d" % d) for g, d in enumerate(DILS)]
    grad_x, dg1, dqkv = _inproj_bwd(dqkvs, dpuz, dgates, wqkv, wuz, wg, cos_t, sin_t, x, dx2, g1)
    dwin = jnp.concatenate([_mm_tn(h1, t, "dw_in_%d" % n) for n, t in enumerate((dqkv, dpuz, dgates))], axis=1)
    tie = comm.rs_start("win", [_to_blocks(dwin, True)])
    dwout = _mm_tn(merged, dy, "dw_out", tie=tie)
    dwba = _mm_tn(ya, da, "dw_branch_attn", tie=tie)
    dwbg = _mm_tn(yg, db, "dw_branch_gmlp", tie=tie)
    comm.rs_mid("win", after=(dwout, dwba, dwbg))
    tie = comm.rs_start("mid", [_to_blocks(dwba, True), _to_blocks(dwbg, True), _to_blocks(dwout, False)])
    small = dict(g1=dg1, g2=dg2, g3=dg3, g4=dg4, ln_g=dlng, ln_b=dlnb, b_sp=dbs.reshape(4, CHUNK),
                 w_sp=dwsp.reshape(4 * CHUNK, CHUNK))
    return loss_row + tie[:1], grad_x, small


def kernel(x, norm_pre_mix, w_in, w_spatial, b_spatial, ln_v_gain, ln_v_bias, w_branch_attn, w_branch_gmlp, w_out, norm_post_mix, norm_pre_mlp, w_mlp_in, w_mlp_out, norm_post_mlp, loss_target, m_norm_pre_mix, m_w_in, m_w_spatial, m_b_spatial, m_ln_v_gain, m_ln_v_bias, m_w_branch_attn, m_w_branch_gmlp, m_w_out, m_norm_post_mix, m_norm_pre_mlp, m_w_mlp_in, m_w_mlp_out, m_norm_post_mlp, v_norm_pre_mix, v_w_in, v_w_spatial, v_b_spatial, v_ln_v_gain, v_ln_v_bias, v_w_branch_attn, v_w_branch_gmlp, v_w_out, v_norm_post_mix, v_norm_pre_mlp, v_w_mlp_in, v_w_mlp_out, v_norm_post_mlp):
    mx, my, mc = _mesh_pos()
    rel = [(0, 0), (1, 0), (0, 1), (1, 1)]
    chip_of = [2 * (mx ^ dx) + (my ^ dy) for dx, dy in rel]
    idx = jnp.stack([2 * k + mc for k in chip_of] + chip_of).astype(jnp.int32)

    (g_win,) = _all_gather_weights([w_in[0].astype(BF)])
    win = _from_blocks(g_win, True)
    wqkv, wuz, wg = win[:, :3 * AW], win[:, 3 * AW:3 * AW + 2 * GMLP_W], win[:, 3 * AW + 2 * GMLP_W:]
    late = [w_branch_attn[0], w_branch_gmlp[0], w_out[0], w_mlp_in[0], w_mlp_out[0]]
    comm = _FsdpComm([w.astype(BF) for w in late], [True, True, False, True, False], g_win, idx, 4 * mx + 2 * my + mc)

    loss_row, grad_x, small = _local_step(
        x[0], loss_target[0], wqkv, wuz, wg, comm,
        norm_pre_mix, norm_post_mix, norm_pre_mlp, norm_post_mlp, ln_v_gain, ln_v_bias, w_spatial[0], b_spatial[0])

    flat = lambda t: t.reshape(-1, t.shape[-1])
    small_w = dict(zip(SMALL, map(flat, (norm_pre_mix, norm_post_mix, norm_pre_mlp, norm_post_mlp, ln_v_gain, ln_v_bias,
                                         b_spatial, w_spatial))))
    small_m = dict(zip(SMALL, map(flat, (m_norm_pre_mix, m_norm_post_mix, m_norm_pre_mlp, m_norm_post_mlp, m_ln_v_gain,
                                         m_ln_v_bias, m_b_spatial, m_w_spatial))))
    small_v = dict(zip(SMALL, map(flat, (v_norm_pre_mix, v_norm_post_mix, v_norm_pre_mlp, v_norm_post_mlp, v_ln_v_gain,
                                         v_ln_v_bias, v_b_spatial, v_w_spatial))))
    loss_out, sg, sd, sm, sv = _small_step(loss_row, small, small_w, small_m, small_v)
    tie = comm.rs_mid("mid", after=(loss_out,))
    mlp_own, mlp_chips = comm.rs_end("mlp", after=(tie,))
    adam = lambda nm, own, r, w, m, v: _adam_shard(own, r, w[0], m[0], v[0], "adam_" + nm)
    upd = {
        "w_mlp_in": adam("w_mlp_in", mlp_own[0], mlp_chips[0], w_mlp_in, m_w_mlp_in, v_w_mlp_in),
        "w_mlp_out": adam("w_mlp_out", mlp_own[1], mlp_chips[1], w_mlp_out, m_w_mlp_out, v_w_mlp_out),
    }
    loss = loss_out[0, 0]
    win_own, win_chips = comm.rs_end("win", after=(upd["w_mlp_in"][0], upd["w_mlp_out"][0]))
    upd["w_in"] = adam("w_in", win_own[0], win_chips[0], w_in, m_w_in, v_w_in)
    mid_own, mid_chips = comm.rs_end("mid", after=(upd["w_in"][0],))
    upd["w_branch_attn"] = adam("w_branch_attn", mid_own[0], mid_chips[0], w_branch_attn, m_w_branch_attn, v_w_branch_attn)
    upd["w_branch_gmlp"] = adam("w_branch_gmlp", mid_own[1], mid_chips[1], w_branch_gmlp, m_w_branch_gmlp, v_w_branch_gmlp)
    upd["w_out"] = adam("w_out", mid_own[2], mid_chips[2], w_out, m_w_out, v_w_out)
    order = ["g1", "w_in", "w_sp", "b_sp", "ln_g", "ln_b", "w_branch_attn", "w_branch_gmlp", "w_out", "g2", "g3",
             "w_mlp_in", "w_mlp_out", "g4"]
    small_shape = dict(g1=norm_pre_mix.shape, g2=norm_post_mix.shape, g3=norm_pre_mlp.shape, g4=norm_post_mlp.shape,
                       ln_g=ln_v_gain.shape, ln_b=ln_v_bias.shape, b_sp=b_spatial.shape, w_sp=w_spatial.shape)

    def pick(which):
        return [upd[nm][which][None] if nm in upd else (sg, sd, sm, sv)[which][nm].reshape(small_shape[nm])
                for nm in order]

    return (loss, grad_x[None], *pick(0), *pick(1), *pick(2), *pick(3))
```

```python
import functools
import math

import jax
import jax.numpy as jnp
from jax import lax
from jax.experimental import pallas as pl
from jax.experimental.pallas import tpu as pltpu

D = 1024
HD = 64
NSLOT = 4
GW = NSLOT * HD
DILS = (1, 4, 16)
QB = 128
ATTN_NSUB = 4
MLP_TM_FWD = 512
MLP_TM_BWD = 256
AW = 3 * GW
GMLP_W = 512
CHUNK = 128
DFF = 4096
EPS = 1e-6
ROPE_THETA = 10000.0
SCALE = HD ** -0.5
NEG = -1e30

ADAM_LR = 0.001
ADAM_B1 = 0.9
ADAM_B2 = 0.999
ADAM_EPS = 1e-08
ADAM_WD = 0.01
ADAM_STEP = 10

BF = jnp.bfloat16
F32 = jnp.float32
MESH = pl.DeviceIdType.MESH
VMEM_LIMIT = 56 * 1024 * 1024


def _cp(sem):
    return pltpu.CompilerParams(dimension_semantics=sem, vmem_limit_bytes=VMEM_LIMIT)


def _dot(a, b):
    return jnp.dot(a, b, preferred_element_type=F32)


def _dot_nt(a, b):
    return lax.dot_general(a, b, (((1,), (1,)), ((), ())), preferred_element_type=F32)


def _dot_tn(a, b):
    return lax.dot_general(a, b, (((0,), (0,)), ((), ())), preferred_element_type=F32)


def _gelu(x):
    return jax.nn.gelu(x, approximate=True)


def _gelu_grad(x):
    k = math.sqrt(2.0 / math.pi)
    t = jnp.tanh(k * (x + 0.044715 * x * x * x))
    return 0.5 * (1.0 + t) + 0.5 * x * (1.0 - t * t) * (k * (1.0 + 3.0 * 0.044715 * x * x))


def _swap_halves(t):
    w = t.shape[1]
    lane = lax.broadcasted_iota(jnp.int32, t.shape, 1)
    first = (lane & (HD - 1)) < (HD // 2)
    return jnp.where(first, pltpu.roll(t, w - HD // 2, 1), pltpu.roll(t, HD // 2, 1))


def _head_mask(hh):
    lane = lax.broadcasted_iota(jnp.int32, (1, GW), 1)
    return jnp.logical_and(lane >= hh * HD, lane < (hh + 1) * HD)


def _rms_stats(xf):
    return lax.rsqrt(jnp.mean(xf * xf, axis=-1, keepdims=True) + EPS)


def _rms_bwd(xf, r, gain, dout):
    n = xf * r
    t = dout * gain
    dx = r * (t - n * jnp.mean(t * n, axis=-1, keepdims=True))
    return dx, jnp.sum(dout * n, axis=0, keepdims=True)


def _rms_fwd(x, gain, name):
    S = x.shape[0]
    tm = min(512, S)

    def body(x_ref, g_ref, h_ref):
        xf = x_ref[...]
        h_ref[...] = (xf * _rms_stats(xf) * g_ref[...]).astype(BF)

    return pl.pallas_call(
        body, name=name, out_shape=jax.ShapeDtypeStruct((S, D), BF), grid=(S // tm,),
        in_specs=[pl.BlockSpec((tm, D), lambda i: (i, 0)), pl.BlockSpec((1, D), lambda i: (0, 0))],
        out_specs=pl.BlockSpec((tm, D), lambda i: (i, 0)), compiler_params=_cp(("parallel",)))(x, gain)


def _proj(h, w, epi, name, tm):
    S, K = h.shape
    N = w.shape[1]
    tm = min(tm, S)

    def body(h_ref, w_ref, o_ref):
        y = _dot(h_ref[...], w_ref[...])
        if epi == "sigmoid":
            y = jax.nn.sigmoid(y)
        o_ref[...] = y.astype(BF)

    return pl.pallas_call(
        body, name=name, out_shape=jax.ShapeDtypeStruct((S, N), BF), grid=(S // tm,),
        in_specs=[pl.BlockSpec((tm, K), lambda i: (i, 0)), _resident((K, N))],
        out_specs=pl.BlockSpec((tm, N), lambda i: (i, 0)),
        compiler_params=_cp(("parallel",)))(h, w)


def _dilate_store(val, scr, o_ref, lead, d):
    rows = val.shape[0]
    if d == 1:
        o_ref[lead + (0,)] = val.astype(o_ref.dtype)
        return
    for hf in range(2):
        scr[hf, pl.ds(0, rows), :] = val[:, hf * 128:(hf + 1) * 128]
    for r in range(d):
        for hf in range(2):
            o_ref[lead + (r, slice(None), slice(hf * 128, (hf + 1) * 128))] = (
                scr[hf, pl.ds(r, rows // d, stride=d), :].astype(o_ref.dtype))


def _undilate_load(i_ref, lead, d, scr, rows):
    if d == 1:
        return i_ref[lead + (0,)].astype(F32)
    for r in range(d):
        for hf in range(2):
            scr[hf, pl.ds(r, rows // d, stride=d), :] = (
                i_ref[lead + (r, slice(None), slice(hf * 128, (hf + 1) * 128))].astype(F32))
    return jnp.concatenate([scr[0, pl.ds(0, rows), :], scr[1, pl.ds(0, rows), :]], axis=1)


def _rope_fwd(y, c_ref, s_ref):
    cosv = jnp.concatenate([c_ref[...]] * 2, axis=1)
    sinv = jnp.concatenate([s_ref[...]] * 2, axis=1)
    return y * cosv + _swap_halves(y) * sinv


def _rope_bwd(dy, c_ref, s_ref):
    cosv = jnp.concatenate([c_ref[...]] * 2, axis=1)
    sinv = jnp.concatenate([s_ref[...]] * 2, axis=1)
    return dy * cosv + _swap_halves(dy * sinv)


def _proj_qkv(h, wqkv, cos_t, sin_t, g, d, name):
    S, K = h.shape
    tm = min(2048, S)

    def body(h_ref, w_ref, c_ref, s_ref, o_ref, scr):
        j = pl.program_id(1)
        y = _dot(h_ref[...], w_ref[...])
        y = jnp.where(j < 2, _rope_fwd(y, c_ref, s_ref), y)
        _dilate_store(y, scr, o_ref, (), d)

    return pl.pallas_call(
        body, name=name, out_shape=jax.ShapeDtypeStruct((3, d, S // d, GW), BF), grid=(S // tm, 3),
        in_specs=[pl.BlockSpec((tm, K), lambda i, j: (i, 0)), pl.BlockSpec((K, GW), lambda i, j: (0, 3 * j + g)),
                  pl.BlockSpec((tm, 128), lambda i, j: (i, 0)), pl.BlockSpec((tm, 128), lambda i, j: (i, 0))],
        out_specs=pl.BlockSpec((None, d, tm // d, GW), lambda i, j: (j, 0, i, 0)),
        scratch_shapes=[pltpu.VMEM((2, tm, 128), F32)],
        compiler_params=_cp(("parallel", "arbitrary")))(h, wqkv, cos_t, sin_t)


def _band_masks(first_step):
    row = lax.broadcasted_iota(jnp.int32, (QB, 2 * QB), 0)
    col = lax.broadcasted_iota(jnp.int32, (QB, 2 * QB), 1)
    band = jnp.logical_and(col >= row, col <= row + QB)
    return band, jnp.logical_and(band, jnp.logical_or(col >= QB, jnp.logical_not(first_step)))


def _attn_fwd(qkv, name):
    _, d, L, _ = qkv.shape
    nsub = min(ATTN_NSUB, L // QB)
    R = nsub * QB
    nsteps = L // R

    def body(q_ref, kp_ref, kc_ref, vp_ref, vc_ref, o_ref, lse_ref):
        i = pl.program_id(1)
        band, band_first = _band_masks(i == 0)
        kfull = jnp.concatenate([kp_ref[...], kc_ref[...]], axis=0)
        vfull = jnp.concatenate([vp_ref[...], vc_ref[...]], axis=0)
        chains = [(sb, hh) for sb in range(nsub) for hh in range(NSLOT)]
        win = lambda t, sb: t[sb * QB:(sb + 2) * QB]
        scores = []
        for sb, hh in chains:
            qh = jnp.where(_head_mask(hh), q_ref[sb * QB:(sb + 1) * QB, :], 0)
            scores.append(_dot_nt(qh, win(kfull, sb)))
        soft = []
        for (sb, hh), sc in zip(chains, scores):
            sc = jnp.where(band_first if sb == 0 else band, sc * SCALE, NEG)
            m = jnp.max(sc, axis=1, keepdims=True)
            p = jnp.exp(sc - m)
            den = jnp.sum(p, axis=1, keepdims=True)
            soft.append((p.astype(BF), den, m + jnp.log(den)))
        accs = [_dot(p, win(vfull, sb)) for (sb, hh), (p, _, _) in zip(chains, soft)]
        for sb in range(nsub):
            o = jnp.zeros((QB, GW), F32)
            lse = jnp.zeros((QB, GW), F32)
            for hh in range(NSLOT):
                hm = _head_mask(hh)
                _, den, lrow = soft[sb * NSLOT + hh]
                o = o + jnp.where(hm, accs[sb * NSLOT + hh] / den, 0.0)
                lse = lse + jnp.where(hm, lrow, 0.0)
            o_ref[sb * QB:(sb + 1) * QB, :] = o
            lse_ref[sb * QB:(sb + 1) * QB, :] = lse

    prev = lambda i: jnp.maximum(i * nsub - 1, 0)
    cur = lambda t: pl.BlockSpec((None, None, R, GW), lambda r, i: (t, r, i, 0))
    prv = lambda t: pl.BlockSpec((None, None, QB, GW), lambda r, i: (t, r, prev(i), 0))
    out = pl.BlockSpec((None, R, GW), lambda r, i: (r, i, 0))
    return pl.pallas_call(
        body, name=name, grid=(d, nsteps),
        out_shape=(jax.ShapeDtypeStruct((d, L, GW), F32), jax.ShapeDtypeStruct((d, L, GW), F32)),
        in_specs=[cur(0), prv(1), cur(1), prv(2), cur(2)],
        out_specs=(out, out), compiler_params=_cp(("parallel", "arbitrary")))(qkv, qkv, qkv, qkv, qkv)


def _attn_combine(os_, lses):
    S = os_[0].shape[1]
    tm = min(1024, S)

    def body(o0, o1, o2, l0, l1, l2, y_ref, j0, j1, j2, scr):
        os_nat = [_undilate_load(o, (), d, scr, tm) for o, d in zip((o0, o1, o2), DILS)]
        a, b, c = [_undilate_load(l, (), d, scr, tm) for l, d in zip((l0, l1, l2), DILS)]
        m = jnp.maximum(jnp.maximum(a, b), c)
        wa, wb, wc = jnp.exp(a - m), jnp.exp(b - m), jnp.exp(c - m)
        den = wa + wb + wc
        y_ref[...] = ((wa * os_nat[0] + wb * os_nat[1] + wc * os_nat[2]) / den).astype(BF)
        lse = m + jnp.log(den)
        for j_ref, d in zip((j0, j1, j2), DILS):
            _dilate_store(lse, scr, j_ref, (), d)

    dil = lambda d: pl.BlockSpec((d, tm // d, GW), lambda i: (0, i, 0))
    dshape = lambda d: jax.ShapeDtypeStruct((d, S // d, GW), F32)
    return pl.pallas_call(
        body, name="attn_combine", grid=(S // tm,),
        out_shape=(jax.ShapeDtypeStruct((S, GW), BF),) + tuple(dshape(d) for d in DILS),
        in_specs=[dil(d) for d in DILS] * 2,
        out_specs=(pl.BlockSpec((tm, GW), lambda i: (i, 0)),) + tuple(dil(d) for d in DILS),
        scratch_shapes=[pltpu.VMEM((2, tm, 128), F32)],
        compiler_params=_cp(("parallel",)))(*os_, *lses)


def _tril(upper=False):
    row = lax.broadcasted_iota(jnp.int32, (CHUNK, CHUNK), 0)
    col = lax.broadcasted_iota(jnp.int32, (CHUNK, CHUNK), 1)
    return row <= col if upper else col <= row


def _ln_fwd(z, gain, bias):
    mu = jnp.mean(z, axis=-1, keepdims=True)
    zc = z - mu
    rstd = lax.rsqrt(jnp.mean(zc * zc, axis=-1, keepdims=True) + EPS)
    zhat = zc * rstd
    return zhat, rstd, zhat * gain + bias


def _gmlp_fwd(puz, ln_g, ln_b, w_sp, b_sp_t):
    S = puz.shape[0]
    tm = min(512, S)
    nch = tm // CHUNK

    def body(p_ref, g_ref, b_ref, w_ref, bt_ref, o_ref):
        tril = _tril()
        ws = [jnp.where(tril, w_ref[gg], 0.0).astype(BF) for gg in range(4)]
        for ch in range(nch):
            rows = slice(ch * CHUNK, (ch + 1) * CHUNK)
            z = _gelu(p_ref[rows, GMLP_W:].astype(F32))
            _, _, zn = _ln_fwd(z, g_ref[...], b_ref[...])
            zn = zn.astype(BF)
            for gg in range(4):
                cols = slice(gg * CHUNK, (gg + 1) * CHUNK)
                sz = _dot(ws[gg], zn[:, cols]) + bt_ref[:, gg:gg + 1]
                u = _gelu(p_ref[rows, cols].astype(F32))
                o_ref[rows, cols] = (u * sz).astype(BF)

    return pl.pallas_call(
        body, name="gmlp_fwd", out_shape=jax.ShapeDtypeStruct((S, GMLP_W), BF), grid=(S // tm,),
        in_specs=[pl.BlockSpec((tm, 2 * GMLP_W), lambda i: (i, 0)),
                  pl.BlockSpec((1, GMLP_W), lambda i: (0, 0)), pl.BlockSpec((1, GMLP_W), lambda i: (0, 0)),
                  pl.BlockSpec((4, CHUNK, CHUNK), lambda i: (0, 0, 0)), pl.BlockSpec((CHUNK, 4), lambda i: (0, 0))],
        out_specs=pl.BlockSpec((tm, GMLP_W), lambda i: (i, 0)),
        compiler_params=_cp(("parallel",)))(puz, ln_g, ln_b, w_sp, b_sp_t)


def _merge_fwd(ya, yg, gates, wba, wbg, wout, x, g2, g3):
    S = x.shape[0]
    tm = min(512, S)

    def body(ya_ref, yg_ref, gt_ref, wba_ref, wbg_ref, wo_ref, x_ref, g2_ref, g3_ref,
             mg_ref, y_ref, x2_ref, h3_ref):
        a = _dot(ya_ref[...], wba_ref[...])
        b = _dot(yg_ref[...], wbg_ref[...])
        merged = (gt_ref[:, :D].astype(F32) * a + gt_ref[:, D:].astype(F32) * b).astype(BF)
        mg_ref[...] = merged
        y = _dot(merged, wo_ref[...])
        y_ref[...] = y
        x2 = x_ref[...] + y * _rms_stats(y) * g2_ref[...]
        x2_ref[...] = x2
        h3_ref[...] = (x2 * _rms_stats(x2) * g3_ref[...]).astype(BF)

    row = lambda w: pl.BlockSpec((tm, w), lambda i: (i, 0))
    full = lambda s: pl.BlockSpec(s, lambda i: (0, 0))
    return pl.pallas_call(
        body, name="merge_fwd", grid=(S // tm,),
        out_shape=(jax.ShapeDtypeStruct((S, D), BF), jax.ShapeDtypeStruct((S, D), F32), jax.ShapeDtypeStruct((S, D), F32),
                   jax.ShapeDtypeStruct((S, D), BF)),
        in_specs=[row(GW), row(GMLP_W), row(2 * D), full((GW, D)), full((GMLP_W, D)), full((D, D)), row(D),
                  full((1, D)), full((1, D))],
        out_specs=(row(D), row(D), row(D), row(D)),
        compiler_params=_cp(("parallel",)))(ya, yg, gates, wba, wbg, wout, x, g2, g3)


def _resident(shape):
    return pl.BlockSpec(shape, lambda i: (0,) * len(shape), pipeline_mode=pl.Buffered(1))


def _mlp_fwd(h3, wmi, wmo, x2, tgt, g4):
    S = x2.shape[0]
    tm = min(MLP_TM_FWD, S)

    def body(h_ref, wi_ref, wo_ref, x2_ref, t_ref, g4_ref, a_ref, dy2_ref, dout_ref, loss_ref, dg4_ref):
        @pl.when(pl.program_id(0) == 0)
        def _():
            loss_ref[...] = jnp.zeros_like(loss_ref)
            dg4_ref[...] = jnp.zeros_like(dg4_ref)

        a = jnp.maximum(_dot(h_ref[...], wi_ref[...]), 0.0)
        a_ref[...] = a.astype(BF)
        y2 = _dot((a * a).astype(BF), wo_ref[...])
        r = _rms_stats(y2)
        out = x2_ref[...] + y2 * r * g4_ref[...]
        err = out - t_ref[...]
        tot = jnp.sum(jnp.sum(err * err, axis=1, keepdims=True), axis=0, keepdims=True) * (0.5 / D)
        lane = lax.broadcasted_iota(jnp.int32, (1, 128), 1)
        loss_ref[...] += jnp.where(lane == 0, tot, 0.0)
        dout = err * (1.0 / D)
        dout_ref[...] = dout
        dy2, dg = _rms_bwd(y2, r, g4_ref[...], dout)
        dy2_ref[...] = dy2.astype(BF)
        dg4_ref[...] += dg

    row = pl.BlockSpec((tm, D), lambda i: (i, 0))
    return pl.pallas_call(
        body, name="mlp_fwd", grid=(S // tm,),
        out_shape=(jax.ShapeDtypeStruct((S, DFF), BF), jax.ShapeDtypeStruct((S, D), BF), jax.ShapeDtypeStruct((S, D), F32),
                   jax.ShapeDtypeStruct((1, 128), F32), jax.ShapeDtypeStruct((1, D), F32)),
        in_specs=[row, _resident((D, DFF)), _resident((DFF, D)), row, row, pl.BlockSpec((1, D), lambda i: (0, 0))],
        out_specs=(pl.BlockSpec((tm, DFF), lambda i: (i, 0)), row, row,
                   pl.BlockSpec((1, 128), lambda i: (0, 0)), pl.BlockSpec((1, D), lambda i: (0, 0))),
        compiler_params=_cp(("arbitrary",)))(h3, wmi, wmo, x2, tgt, g4)


def _mlp_bwd(dy2, a, wmo, wmi, x2, y, dout, g2, g3):
    S = x2.shape[0]
    tm = min(MLP_TM_BWD, S)

    def body(dy2_ref, a_ref, wo_ref, wi_ref, x2_ref, y_ref, dout_ref, g2_ref, g3_ref,
             dpre_ref, dx2_ref, dy_ref, dg3_ref, dg2_ref):
        @pl.when(pl.program_id(0) == 0)
        def _():
            dg3_ref[...] = jnp.zeros_like(dg3_ref)
            dg2_ref[...] = jnp.zeros_like(dg2_ref)

        da2 = _dot_nt(dy2_ref[...], wo_ref[...])
        dpre = (2.0 * a_ref[...].astype(F32) * da2).astype(BF)
        dpre_ref[...] = dpre
        dh3 = _dot_nt(dpre, wi_ref[...])
        x2 = x2_ref[...]
        dx3, dg3 = _rms_bwd(x2, _rms_stats(x2), g3_ref[...], dh3)
        dx2 = dout_ref[...] + dx3
        dx2_ref[...] = dx2
        dg3_ref[...] += dg3
        yv = y_ref[...]
        dy, dg2 = _rms_bwd(yv, _rms_stats(yv), g2_ref[...], dx2)
        dy_ref[...] = dy.astype(BF)
        dg2_ref[...] += dg2

    row = pl.BlockSpec((tm, D), lambda i: (i, 0))
    wide = pl.BlockSpec((tm, DFF), lambda i: (i, 0))
    vec = pl.BlockSpec((1, D), lambda i: (0, 0))
    return pl.pallas_call(
        body, name="mlp_bwd", grid=(S // tm,),
        out_shape=(jax.ShapeDtypeStruct((S, DFF), BF), jax.ShapeDtypeStruct((S, D), F32), jax.ShapeDtypeStruct((S, D), BF),
                   jax.ShapeDtypeStruct((1, D), F32), jax.ShapeDtypeStruct((1, D), F32)),
        in_specs=[row, wide, _resident((DFF, D)), _resident((D, DFF)), row, row, row, vec, vec],
        out_specs=(wide, row, row, vec, vec),
        compiler_params=_cp(("arbitrary",)))(dy2, a, wmo, wmi, x2, y, dout, g2, g3)


def _mm_tn(a, b, name, square_a=False, tm=1024, tn=1024, tk=2048, tie=None, col_blocks=None):
    S, M = a.shape
    N = b.shape[1]
    tm, tk = min(tm, M), min(tk, S)
    tn = max(t for t in range(128, min(tn, N) + 1, 128) if N % t == 0)
    cb = N // col_blocks if col_blocks else tn
    assert M % tm == 0 and S % tk == 0 and tn % cb == 0
    nk = S // tk
    ties = () if tie is None else (tie,)

    def body(a_ref, b_ref, *rest):
        o_ref = rest[-1]
        k = pl.program_id(2)
        av = a_ref[...]
        if square_a:
            av = av * av
        part = _dot_tn(av, b_ref[...])
        if col_blocks:
            part = jnp.stack([part[:, t * cb:(t + 1) * cb] for t in range(tn // cb)])

        @pl.when(k == 0)
        def _():
            o_ref[...] = part

        @pl.when(k > 0)
        def _():
            o_ref[...] += part

    if col_blocks:
        out_shape, out_spec = (col_blocks, M, cb), pl.BlockSpec((tn // cb, tm, cb), lambda i, j, k: (j, i, 0))
    else:
        out_shape, out_spec = (M, N), pl.BlockSpec((tm, tn), lambda i, j, k: (i, j))
    return pl.pallas_call(
        body, name=name, out_shape=jax.ShapeDtypeStruct(out_shape, F32), grid=(M // tm, N // tn, nk),
        in_specs=[pl.BlockSpec((tk, tm), lambda i, j, k: (k, i)), pl.BlockSpec((tk, tn), lambda i, j, k: (k, j))]
        + [pl.BlockSpec(memory_space=pl.ANY)] * len(ties),
        out_specs=out_spec,
        compiler_params=_cp(("parallel", "parallel", "arbitrary")))(a, b, *ties)


def _outproj_bwd(dy, wout, ya, yg, gates, wba, wbg, tie):
    S = dy.shape[0]
    tm = min(512, S)

    def body(dy_ref, wo_ref, ya_ref, yg_ref, gt_ref, wba_ref, wbg_ref, tie_ref,
             dgt_ref, da_ref, db_ref, dyg_ref, e0, e1, e2, s0, s1, s2, scr):
        dm = _dot_nt(dy_ref[...], wo_ref[...])
        ga, gb = gt_ref[:, :D].astype(F32), gt_ref[:, D:].astype(F32)
        dgt_ref[:, :D] = (dm * _dot(ya_ref[...], wba_ref[...]) * ga * (1.0 - ga)).astype(BF)
        dgt_ref[:, D:] = (dm * _dot(yg_ref[...], wbg_ref[...]) * gb * (1.0 - gb)).astype(BF)
        da = (dm * ga).astype(BF)
        db = (dm * gb).astype(BF)
        da_ref[...] = da
        db_ref[...] = db
        dyg_ref[...] = _dot_nt(db, wbg_ref[...]).astype(BF)
        dya = _dot_nt(da, wba_ref[...]).astype(BF).astype(F32)
        dyy = dya * ya_ref[...].astype(F32)
        dsum = jnp.zeros((tm, GW), F32)
        for hh in range(NSLOT):
            hm = _head_mask(hh)
            dsum = dsum + jnp.where(hm, jnp.sum(jnp.where(hm, dyy, 0.0), axis=1, keepdims=True), 0.0)
        for e_ref, s_ref, d in zip((e0, e1, e2), (s0, s1, s2), DILS):
            _dilate_store(dya, scr, e_ref, (), d)
            _dilate_store(dsum, scr, s_ref, (), d)

    row = lambda w: pl.BlockSpec((tm, w), lambda i: (i, 0))
    full = lambda s: pl.BlockSpec(s, lambda i: (0, 0))
    dil = lambda d: pl.BlockSpec((d, tm // d, GW), lambda i: (0, i, 0))
    dshape = lambda d, t: jax.ShapeDtypeStruct((d, S // d, GW), t)
    return pl.pallas_call(
        body, name="outproj_bwd", grid=(S // tm,),
        out_shape=(jax.ShapeDtypeStruct((S, 2 * D), BF), jax.ShapeDtypeStruct((S, D), BF), jax.ShapeDtypeStruct((S, D), BF),
                   jax.ShapeDtypeStruct((S, GMLP_W), BF)) + tuple(dshape(d, BF) for d in DILS)
        + tuple(dshape(d, F32) for d in DILS),
        in_specs=[row(D), full((D, D)), row(GW), row(GMLP_W), row(2 * D), full((GW, D)), full((GMLP_W, D)),
                  pl.BlockSpec(memory_space=pl.ANY)],
        out_specs=(row(2 * D), row(D), row(D), row(GMLP_W)) + tuple(dil(d) for d in DILS) * 2,
        scratch_shapes=[pltpu.VMEM((2, tm, 128), F32)],
        compiler_params=_cp(("parallel",)))(dy, wout, ya, yg, gates, wba, wbg, tie)


def _gmlp_bwd(puz, dyg, ln_g, ln_b, w_sp, w_sp_t, b_sp_t):
    S = puz.shape[0]
    tm = min(512, S)
    nch = tm // CHUNK

    def body(p_ref, dy_ref, g_ref, b_ref, w_ref, wt_ref, bt_ref,
             dp_ref, dw_ref, dbs_ref, dg_ref, dbias_ref, dbacc_ref):
        i = pl.program_id(0)

        @pl.when(i == 0)
        def _():
            dw_ref[...] = jnp.zeros_like(dw_ref)
            dbacc_ref[...] = jnp.zeros_like(dbacc_ref)
            dg_ref[...] = jnp.zeros_like(dg_ref)
            dbias_ref[...] = jnp.zeros_like(dbias_ref)

        tril = _tril()
        ws = [jnp.where(tril, w_ref[gg], 0.0).astype(BF) for gg in range(4)]
        triu = _tril(upper=True)
        wts = [jnp.where(triu, wt_ref[gg], 0.0).astype(BF) for gg in range(4)]
        gain = g_ref[...]
        for ch in range(nch):
            rows = slice(ch * CHUNK, (ch + 1) * CHUNK)
            pz = p_ref[rows, GMLP_W:].astype(F32)
            z = _gelu(pz)
            zhat, rstd, zn = _ln_fwd(z, gain, b_ref[...])
            znb = zn.astype(BF)
            dzn_parts = []
            for gg in range(4):
                cols = slice(gg * CHUNK, (gg + 1) * CHUNK)
                pu = p_ref[rows, cols].astype(F32)
                u = _gelu(pu)
                sz = _dot(ws[gg], znb[:, cols]) + bt_ref[:, gg:gg + 1]
                dyv = dy_ref[rows, cols].astype(F32)
                dp_ref[rows, cols] = (dyv * sz * _gelu_grad(pu)).astype(BF)
                dsz = dyv * u
                dbacc_ref[gg] += dsz
                dszb = dsz.astype(BF)
                dw_ref[gg] += _dot_nt(dszb, znb[:, cols])
                dzn_parts.append(_dot(wts[gg], dszb))
            dzn = jnp.concatenate(dzn_parts, axis=1)
            dg_ref[...] += jnp.sum(dzn * zhat, axis=0, keepdims=True)
            dbias_ref[...] += jnp.sum(dzn, axis=0, keepdims=True)
            dzh = dzn * gain
            dz = rstd * (dzh - jnp.mean(dzh, axis=-1, keepdims=True)
                         - zhat * jnp.mean(dzh * zhat, axis=-1, keepdims=True))
            dp_ref[rows, GMLP_W:] = (dz * _gelu_grad(pz)).astype(BF)

        @pl.when(i == pl.num_programs(0) - 1)
        def _():
            for gg in range(4):
                dw_ref[gg] = jnp.where(tril, dw_ref[gg], 0.0)
                dbs_ref[gg] = jnp.sum(dbacc_ref[gg], axis=1, keepdims=True)

    full2 = lambda s: pl.BlockSpec(s, lambda i: (0, 0))
    full3 = lambda s: pl.BlockSpec(s, lambda i: (0, 0, 0))
    return pl.pallas_call(
        body, name="gmlp_bwd", grid=(S // tm,),
        out_shape=(jax.ShapeDtypeStruct((S, 2 * GMLP_W), BF), jax.ShapeDtypeStruct((4, CHUNK, CHUNK), F32),
                   jax.ShapeDtypeStruct((4, CHUNK, 1), F32), jax.ShapeDtypeStruct((1, GMLP_W), F32),
                   jax.ShapeDtypeStruct((1, GMLP_W), F32)),
        in_specs=[pl.BlockSpec((tm, 2 * GMLP_W), lambda i: (i, 0)), pl.BlockSpec((tm, GMLP_W), lambda i: (i, 0)),
                  full2((1, GMLP_W)), full2((1, GMLP_W)), full3((4, CHUNK, CHUNK)), full3((4, CHUNK, CHUNK)),
                  full2((CHUNK, 4))],
        out_specs=(pl.BlockSpec((tm, 2 * GMLP_W), lambda i: (i, 0)), full3((4, CHUNK, CHUNK)), full3((4, CHUNK, 1)),
                   full2((1, GMLP_W)), full2((1, GMLP_W))),
        scratch_shapes=[pltpu.VMEM((4, CHUNK, CHUNK), F32)],
        compiler_params=_cp(("arbitrary",)))(puz, dyg, ln_g, ln_b, w_sp, w_sp_t, b_sp_t)


def _attn_bwd(qkv, dya, dsums, lse, name):
    _, d, L, _ = qkv.shape
    nsub = min(ATTN_NSUB, L // QB)
    R = nsub * QB
    nsteps = L // R

    def body(q_ref, qn_ref, kp_ref, kc_ref, vp_ref, vc_ref, dy_ref, dyn_ref, e_ref, en_ref, l_ref, ln_ref, o_ref):
        i = pl.program_id(1)
        band, band_first = _band_masks(i == 0)
        row = lax.broadcasted_iota(jnp.int32, (QB, QB), 0)
        col = lax.broadcasted_iota(jnp.int32, (QB, QB), 1)
        mask_next = jnp.logical_and(col >= row, i < nsteps - 1)
        kc, vc = kc_ref[...], vc_ref[...]
        kfull = jnp.concatenate([kp_ref[...], kc], axis=0)
        vfull = jnp.concatenate([vp_ref[...], vc], axis=0)
        k_last, v_last = kc[(nsub - 1) * QB:], vc[(nsub - 1) * QB:]
        q_ext = jnp.concatenate([q_ref[...], qn_ref[...]], axis=0)
        dy_ext = jnp.concatenate([dy_ref[...], dyn_ref[...]], axis=0)
        esum, esum_n, lse, lse_n = e_ref[...], en_ref[...], l_ref[...], ln_ref[...]
        win = lambda t, sb: t[sb * QB:(sb + 2) * QB]
        blk = lambda t, sb: t[sb * QB:(sb + 1) * QB]
        hms = [_head_mask(hh) for hh in range(NSLOT)]
        q_hs = [jnp.where(hm, q_ext, 0) for hm in hms]
        dy_hs = [jnp.where(hm, dy_ext, 0) for hm in hms]
        raw = []
        for hh in range(NSLOT):
            tiles = [(_dot_nt(blk(q_hs[hh], sb), win(kfull, sb)), _dot_nt(blk(dy_hs[hh], sb), win(vfull, sb)))
                     for sb in range(nsub)]
            tiles.append((_dot_nt(blk(q_hs[hh], nsub), k_last), _dot_nt(blk(dy_hs[hh], nsub), v_last)))
            raw.append(tiles)
        ps, dss = [], []
        for hh in range(NSLOT):
            rowstat = lambda t: jnp.max(jnp.where(hms[hh], t, -jnp.inf), axis=1, keepdims=True)
            p_h, ds_h = [], []
            for sb in range(nsub + 1):
                sc, dp = raw[hh][sb]
                if sb < nsub:
                    msk, lrow, erow = (band_first if sb == 0 else band), rowstat(blk(lse, sb)), rowstat(blk(esum, sb))
                else:
                    msk, lrow, erow = mask_next, rowstat(lse_n), rowstat(esum_n)
                p = jnp.where(msk, jnp.exp(sc * SCALE - lrow), 0.0)
                p_h.append(p.astype(BF))
                ds_h.append((p * (dp - erow)).astype(BF))
            ps.append(p_h)
            dss.append(ds_h)
        dq = [jnp.zeros((QB, GW), F32) for _ in range(nsub)]
        dk = [jnp.zeros((QB, GW), F32) for _ in range(nsub)]
        dv = [jnp.zeros((QB, GW), F32) for _ in range(nsub)]
        for hh in range(NSLOT):
            for sb in range(nsub):
                dq[sb] = dq[sb] + jnp.where(hms[hh], _dot(dss[hh][sb], win(kfull, sb)), 0.0)
                nxt = lambda t: t[sb + 1][:, :QB] if sb + 1 < nsub else t[nsub]
                dk[sb] = dk[sb] + _dot_tn(jnp.concatenate([dss[hh][sb][:, QB:], nxt(dss[hh])], axis=0), win(q_hs[hh], sb))
                dv[sb] = dv[sb] + _dot_tn(jnp.concatenate([ps[hh][sb][:, QB:], nxt(ps[hh])], axis=0), win(dy_hs[hh], sb))
        for sb in range(nsub):
            rows = slice(sb * QB, (sb + 1) * QB)
            o_ref[0, rows, :] = (dq[sb] * SCALE).astype(BF)
            o_ref[1, rows, :] = (dk[sb] * SCALE).astype(BF)
            o_ref[2, rows, :] = dv[sb].astype(BF)

    prev = lambda i: jnp.maximum(i * nsub - 1, 0)
    nxt = lambda i: jnp.minimum((i + 1) * nsub, L // QB - 1)
    cur4 = lambda t: pl.BlockSpec((None, None, R, GW), lambda r, i: (t, r, i, 0))
    prv4 = lambda t: pl.BlockSpec((None, None, QB, GW), lambda r, i: (t, r, prev(i), 0))
    nxt4 = lambda t: pl.BlockSpec((None, None, QB, GW), lambda r, i: (t, r, nxt(i), 0))
    cur3 = pl.BlockSpec((None, R, GW), lambda r, i: (r, i, 0))
    nxt3 = pl.BlockSpec((None, QB, GW), lambda r, i: (r, nxt(i), 0))
    return pl.pallas_call(
        body, name=name, grid=(d, nsteps), out_shape=jax.ShapeDtypeStruct((3, d, L, GW), BF),
        in_specs=[cur4(0), nxt4(0), prv4(1), cur4(1), prv4(2), cur4(2), cur3, nxt3, cur3, nxt3, cur3, nxt3],
        out_specs=pl.BlockSpec((3, None, R, GW), lambda r, i: (0, r, i, 0)),
        compiler_params=_cp(("parallel", "arbitrary")))(qkv, qkv, qkv, qkv, qkv, qkv, dya, dya, dsums, dsums, lse, lse)


def _inproj_bwd(dqkvs, dpuz, dgates, wqkv, wuz, wg, cos_t, sin_t, x, dx2, g1):
    S = x.shape[0]
    tm = min(512, S)

    def body(d0_ref, d1_ref, d2_ref, dp_ref, dg_ref, wqkv_ref, wuz_ref, wg_ref, c_ref, s_ref,
             x_ref, dx2_ref, g1_ref, gx_ref, dg1_ref, dn_ref, scr):
        i = pl.program_id(0)

        @pl.when(i == 0)
        def _():
            dg1_ref[...] = jnp.zeros_like(dg1_ref)

        for t in range(3):
            for g, (d_ref, d) in enumerate(zip((d0_ref, d1_ref, d2_ref), DILS)):
                piece = _undilate_load(d_ref, (t,), d, scr, tm)
                if t < 2:
                    piece = _rope_bwd(piece, c_ref, s_ref)
                dn_ref[:, (3 * t + g) * GW:(3 * t + g + 1) * GW] = piece.astype(BF)
        dh = (_dot_nt(dn_ref[...], wqkv_ref[...]) + _dot_nt(dp_ref[...], wuz_ref[...])
              + _dot_nt(dg_ref[...], wg_ref[...]))
        xv = x_ref[...]
        dx1, dg1 = _rms_bwd(xv, _rms_stats(xv), g1_ref[...], dh)
        gx_ref[...] = dx2_ref[...] + dx1
        dg1_ref[...] += dg1

    row = lambda w: pl.BlockSpec((tm, w), lambda i: (i, 0))
    full = lambda s: pl.BlockSpec(s, lambda i: (0, 0))
    dil = lambda d: pl.BlockSpec((3, d, tm // d, GW), lambda i: (0, 0, i, 0))
    return pl.pallas_call(
        body, name="inproj_bwd", grid=(S // tm,),
        out_shape=(jax.ShapeDtypeStruct((S, D), F32), jax.ShapeDtypeStruct((1, D), F32),
                   jax.ShapeDtypeStruct((S, 3 * AW), BF)),
        in_specs=[dil(d) for d in DILS] + [row(2 * GMLP_W), row(2 * D), full((D, 3 * AW)), full((D, 2 * GMLP_W)),
                                            full((D, 2 * D)), row(128), row(128), row(D), row(D), full((1, D))],
        out_specs=(row(D), full((1, D)), row(3 * AW)),
        scratch_shapes=[pltpu.VMEM((2, tm, 128), F32)],
        compiler_params=_cp(("arbitrary",)))(*dqkvs, dpuz, dgates, wqkv, wuz, wg, cos_t, sin_t, x, dx2, g1)


def _adam_math(w, g, m, v):
    m2 = ADAM_B1 * m + (1.0 - ADAM_B1) * g
    v2 = ADAM_B2 * v + (1.0 - ADAM_B2) * (g * g)
    m_hat = m2 / (1.0 - ADAM_B1 ** ADAM_STEP)
    v_hat = v2 / (1.0 - ADAM_B2 ** ADAM_STEP)
    delta = -ADAM_LR * (m_hat / (jnp.sqrt(v_hat) + ADAM_EPS) + ADAM_WD * w)
    return delta, m2, v2


def _adam_shard(own, recv, w, m, v, name):
    R, C = w.shape
    tr = min(256, R)

    def body(own_ref, r_ref, w_ref, m_ref, v_ref, g_ref, d_ref, m2_ref, v2_ref):
        g = own_ref[...] + r_ref[0].astype(F32) + r_ref[1].astype(F32) + r_ref[2].astype(F32)
        g_ref[...] = g
        d_ref[...], m2_ref[...], v2_ref[...] = _adam_math(w_ref[...], g, m_ref[...], v_ref[...])

    spec = pl.BlockSpec((tr, C), lambda i: (i, 0))
    out = jax.ShapeDtypeStruct((R, C), F32)
    return pl.pallas_call(
        body, name=name, grid=(R // tr,), out_shape=(out, out, out, out),
        in_specs=[spec, pl.BlockSpec((3, tr, C), lambda i: (0, i, 0)), spec, spec, spec],
        out_specs=(spec, spec, spec, spec), compiler_params=_cp(("parallel",)))(own, recv, w, m, v)


def _rs_add(gblocks, recv, idx, name):
    _, R, C = gblocks.shape
    tr = min(256, R)

    def body(t_ref, g_ref, r_ref, own_ref, send_ref):
        j = pl.program_id(1)
        s = g_ref[...] + r_ref[...]

        @pl.when(j == 0)
        def _():
            own_ref[...] = s

        @pl.when(j > 0)
        def _():
            send_ref[...] = s.astype(BF)

    grid_spec = pltpu.PrefetchScalarGridSpec(
        num_scalar_prefetch=1, grid=(R // tr, 4),
        in_specs=[pl.BlockSpec((None, tr, C), lambda i, j, t: (t[j], i, 0)),
                  pl.BlockSpec((None, tr, C), lambda i, j, t: (t[4 + j], i, 0))],
        out_specs=[pl.BlockSpec((tr, C), lambda i, j, t: (i, 0)),
                   pl.BlockSpec((None, tr, C), lambda i, j, t: (jnp.maximum(j - 1, 0), i, 0))])
    return pl.pallas_call(
        body, name=name, grid_spec=grid_spec,
        out_shape=(jax.ShapeDtypeStruct((R, C), F32), jax.ShapeDtypeStruct((3, R, C), BF)),
        compiler_params=_cp(("parallel", "arbitrary")))(idx, gblocks, recv)


def _mesh_pos():
    return lax.axis_index("x"), lax.axis_index("y"), lax.axis_index("c")


def _all_gather_weights(shards):
    n = len(shards)

    def body(*refs):
        ins, outs = refs[:n], refs[n:2 * n]
        send_sems, recv_sems, loc_sems = refs[2 * n:]
        x, y, c = _mesh_pos()
        me, sib = (x, y, c), (x, y, 1 - c)
        chips = [(1 - x, y), (x, 1 - y), (1 - x, 1 - y)]

        def rcopy(k, s, block, to, src=None):
            dst = outs[k].at[4 * block[0] + 2 * block[1] + block[2]]
            return pltpu.make_async_remote_copy(
                src_ref=dst if src is None else src, dst_ref=dst, send_sem=send_sems.at[k, s],
                recv_sem=recv_sems.at[k, s], device_id=to, device_id_type=MESH)

        locs = [pltpu.make_async_copy(ins[k], outs[k].at[4 * x + 2 * y + c], loc_sems.at[k]) for k in range(n)]
        for cp in locs:
            cp.start()
        started = []
        for k in range(n):
            for j, chip in enumerate(chips):
                started.append(rcopy(k, 1 + j, me, (*chip, c), src=ins[k]))
                started[-1].start()
        for k in range(n):
            started.append(rcopy(k, 0, me, sib, src=ins[k]))
            started[-1].start()
        for k in range(n):
            for j, chip in enumerate(chips):
                rcopy(k, 1 + j, (*chip, c), me).wait_recv()
                started.append(rcopy(k, 4 + j, (*chip, c), sib))
                started[-1].start()
        for k in range(n):
            rcopy(k, 0, sib, me).wait_recv()
            for j, chip in enumerate(chips):
                rcopy(k, 4 + j, (*chip, 1 - c), me).wait_recv()
        for cp in started:
            cp.wait_send()
        for cp in locs:
            cp.wait()

    hbm = pl.BlockSpec(memory_space=pl.ANY)
    return pl.pallas_call(
        body, name="ag_weights",
        out_shape=tuple(jax.ShapeDtypeStruct((8,) + s.shape, s.dtype) for s in shards),
        in_specs=[hbm] * n, out_specs=(hbm,) * n,
        scratch_shapes=[pltpu.SemaphoreType.DMA((n, 7)), pltpu.SemaphoreType.DMA((n, 7)),
                        pltpu.SemaphoreType.DMA((n,))])(*shards)


_HBM = pl.BlockSpec(memory_space=pltpu.HBM)
_SEM = pl.BlockSpec(memory_space=pltpu.SEMAPHORE)
_EFFECT = pltpu.SideEffectType.DATAFLOW_SIDE_EFFECTING
_RELATIONS = [(dx, dy, dc) for dx in (0, 1) for dy in (0, 1) for dc in (0, 1)][1:]


def _flip(v, d):
    return 1 - v if d else v


def _plan_gather(n):
    def plan(x, y, c):
        return [(k, None, 4 * x + 2 * y + c, (_flip(x, dx), _flip(y, dy), _flip(c, dc)))
                for k in range(n) for dx, dy, dc in _RELATIONS]
    return plan


def _plan_d2d(n):
    def plan(x, y, c):
        return [(k, 2 * kk + 1 - c, kk, (x, y, 1 - c)) for k in range(n) for kk in range(4)]
    return plan


def _plan_ici(n):
    def plan(x, y, c):
        return [(k, j, j, (_flip(x, dx), _flip(y, dy), c))
                for k in range(n) for j, (dx, dy) in enumerate(((1, 0), (0, 1), (1, 1)))]
    return plan


def _plan_copies(plan, src_refs, land_refs, send_sems, recv_sems):
    x, y, c = _mesh_pos()
    return [pltpu.make_async_remote_copy(
        src_ref=src_refs[k] if si is None else src_refs[k].at[si], dst_ref=land_refs[k].at[di],
        send_sem=send_sems.at[n], recv_sem=recv_sems.at[n], device_id=dev, device_id_type=MESH)
        for n, (k, si, di, dev) in enumerate(plan(x, y, c))]


def _exchange_start(srcs, land_shapes, plan, ncopies, name, after):
    n = len(srcs)

    def body(*refs):
        src_refs, land_refs = refs[:n], refs[n:2 * n]
        send_sems, recv_sems = refs[2 * n + len(after)], refs[2 * n + len(after) + 1]
        token = refs[-1]
        for cp in _plan_copies(plan, src_refs, land_refs, send_sems, recv_sems):
            cp.start()
        token[...] = jnp.zeros_like(token)

    lands = [pltpu.with_memory_space_constraint(lax.empty(s, a.dtype), pltpu.HBM) for s, a in zip(land_shapes, srcs)]
    srcs = [pltpu.with_memory_space_constraint(a, pltpu.HBM) for a in srcs]
    outs = pl.pallas_call(
        body, name=name,
        out_shape=(pltpu.SemaphoreType.DMA((ncopies,)), pltpu.SemaphoreType.DMA((ncopies,)))
        + tuple(pltpu.HBM(a.shape, a.dtype) for a in srcs) + tuple(pltpu.HBM(a.shape, a.dtype) for a in lands)
        + (jax.ShapeDtypeStruct((8, 128), F32),),
        in_specs=[_HBM] * (2 * n) + [pl.BlockSpec(memory_space=pl.ANY)] * len(after),
        out_specs=(_SEM, _SEM) + (_HBM,) * (2 * n) + (pl.BlockSpec(memory_space=pltpu.VMEM),),
        input_output_aliases={i: 2 + i for i in range(2 * n)},
        compiler_params=pltpu.CompilerParams(has_side_effects=_EFFECT))(*srcs, *lands, *after)
    return (outs[0], outs[1], list(outs[2:2 + n]), list(outs[2 + n:2 + 2 * n])), outs[-1]


def _exchange_wait(handle, plan, name, after):
    send_sems, recv_sems, srcs, lands = handle
    n = len(srcs)

    def body(*refs):
        src_refs, land_refs = refs[:n], refs[n:2 * n]
        for cp in _plan_copies(plan, src_refs, land_refs, refs[2 * n], refs[2 * n + 1]):
            cp.wait_send()
            cp.wait_recv()

    outs = pl.pallas_call(
        body, name=name,
        out_shape=tuple(pltpu.HBM(a.shape, a.dtype) for a in srcs) + tuple(pltpu.HBM(a.shape, a.dtype) for a in lands),
        in_specs=[_HBM] * (2 * n) + [_SEM, _SEM] + [pl.BlockSpec(memory_space=pl.ANY)] * len(after),
        out_specs=(_HBM,) * (2 * n), input_output_aliases={i: i for i in range(2 * n)},
        compiler_params=pltpu.CompilerParams(has_side_effects=_EFFECT))(*srcs, *lands, send_sems, recv_sems, *after)
    return list(outs[:n]), list(outs[n:])


SMALL = ("g1", "g2", "g3", "g4", "ln_g", "ln_b", "b_sp", "w_sp")


def _small_step(loss_row, grads, ws, ms, vs):
    parts = [loss_row] + [grads[k] for k in SMALL]
    n, ns = len(parts), len(SMALL)

    def body(*refs):
        p_refs = refs[:n]
        w_refs, m_refs, v_refs = (refs[n + i * ns:n + (i + 1) * ns] for i in range(3))
        o = n + 3 * ns
        loss_ref = refs[o]
        g_out, d_out, m_out, v_out = (refs[o + 1 + i * ns:o + 1 + (i + 1) * ns] for i in range(4))
        bufs = refs[o + 1 + 4 * ns:o + 1 + 4 * ns + n]
        send_sems, recv_sems = refs[-2:]
        x, y, c = _mesh_pos()
        me, sib = (x, y, c), (x, y, 1 - c)
        chips = [(1 - x, y), (x, 1 - y), (1 - x, 1 - y)]

        def rcopy(k, s, block, to, src=None):
            dst = bufs[k].at[4 * block[0] + 2 * block[1] + block[2]]
            return pltpu.make_async_remote_copy(
                src_ref=dst if src is None else src, dst_ref=dst, send_sem=send_sems.at[k, s],
                recv_sem=recv_sems.at[k, s], device_id=to, device_id_type=MESH)

        started = []
        for k in range(n):
            bufs[k][4 * x + 2 * y + c] = p_refs[k][...]
            started.append(rcopy(k, 0, me, sib, src=p_refs[k]))
            started += [rcopy(k, 1 + j, me, (*chip, c), src=p_refs[k]) for j, chip in enumerate(chips)]
        for cp in started:
            cp.start()
        for k in range(n):
            for j, chip in enumerate(chips):
                rcopy(k, 1 + j, (*chip, c), me).wait_recv()
                started.append(rcopy(k, 4 + j, (*chip, c), sib))
                started[-1].start()
        for k in range(n):
            rcopy(k, 0, sib, me).wait_recv()
            for j, chip in enumerate(chips):
                rcopy(k, 4 + j, (*chip, 1 - c), me).wait_recv()
        for cp in started:
            cp.wait_send()
        sums = []
        for k in range(n):
            acc = bufs[k][0]
            for b in range(1, 8):
                acc = acc + bufs[k][b]
            sums.append(acc)
        loss_ref[...] = sums[0]
        for i in range(ns):
            g_out[i][...] = sums[1 + i]
            d_out[i][...], m_out[i][...], v_out[i][...] = _adam_math(w_refs[i][...], sums[1 + i], m_refs[i][...],
                                                                     v_refs[i][...])

    args = parts + [t[k] for t in (ws, ms, vs) for k in SMALL]
    shapes = [jax.ShapeDtypeStruct(loss_row.shape, F32)] + [jax.ShapeDtypeStruct(grads[k].shape, F32) for k in SMALL] * 4
    vmem = pl.BlockSpec(memory_space=pltpu.VMEM)
    outs = pl.pallas_call(
        body, name="small_step", out_shape=tuple(shapes), in_specs=[vmem] * len(args), out_specs=(vmem,) * len(shapes),
        scratch_shapes=[pltpu.VMEM((8,) + p.shape, F32) for p in parts]
        + [pltpu.SemaphoreType.DMA((n, 7)), pltpu.SemaphoreType.DMA((n, 7))],
        compiler_params=pltpu.CompilerParams(vmem_limit_bytes=VMEM_LIMIT))(*args)
    groups = [dict(zip(SMALL, outs[1 + i * ns:1 + (i + 1) * ns])) for i in range(4)]
    return (outs[0], *groups)


def _rope_tables(S):
    half = HD // 2
    inv_freq = ROPE_THETA ** (-jnp.arange(half, dtype=F32) / half)
    ang = jnp.arange(S, dtype=F32)[:, None] * inv_freq[None, :]
    cos, sin = jnp.cos(ang), jnp.sin(ang)
    return jnp.concatenate([cos, cos, cos, cos], axis=1), jnp.concatenate([-sin, sin, -sin, sin], axis=1)


def _to_blocks(g, col_sharded):
    if col_sharded:
        return g.reshape(g.shape[0], 8, g.shape[1] // 8).transpose(1, 0, 2)
    return g.reshape(8, g.shape[0] // 8, g.shape[1])


def _from_blocks(t, col_sharded):
    if col_sharded:
        return t.transpose(1, 0, 2).reshape(t.shape[1], 8 * t.shape[2])
    return t.reshape(8 * t.shape[1], t.shape[2])


class _NoComm:
    def __init__(self, late_weights):
        self._late = late_weights
        self.grads = {}

    def start_token(self):
        return 0.0

    def late_weights(self, after):
        return self._late

    def rs_start(self, key, gblocks, after=()):
        self.grads[key] = gblocks
        return jnp.zeros((8, 128), F32)

    def rs_mid(self, key, after):
        return jnp.zeros((8, 128), F32)


class _FsdpComm:
    def __init__(self, late_shards, col_sharded, after, idx, block):
        self._col, self._idx, self._block, self._rs = col_sharded, idx, block, {}
        n = len(late_shards)
        self._gather, self._token = _exchange_start(
            late_shards, [(8,) + s.shape for s in late_shards], _plan_gather(n), 7 * n, "ag_late_start", (after,))

    def start_token(self):
        return self._token[0, 0]

    def late_weights(self, after):
        shards, lands = _exchange_wait(self._gather, _plan_gather(len(self._col)), "ag_late_wait", after)
        lands = [lax.dynamic_update_index_in_dim(t, s, self._block, 0) for t, s in zip(lands, shards)]
        return [_from_blocks(t, cs) for t, cs in zip(lands, self._col)]

    def rs_start(self, key, gblocks, after=()):
        n = len(gblocks)
        d2d, token = _exchange_start(gblocks, [(4,) + g.shape[1:] for g in gblocks], _plan_d2d(n), 4 * n,
                                     "rs_%s_d2d_start" % key, (self._token,) + tuple(after))
        self._rs[key] = dict(n=n, d2d=d2d)
        return token

    def rs_mid(self, key, after):
        st = self._rs[key]
        gblocks, from_sib = _exchange_wait(st["d2d"], _plan_d2d(st["n"]), "rs_%s_d2d_wait" % key, after)
        halves = [_rs_add(g, r, self._idx, "rs_add_%s_%d" % (key, k)) for k, (g, r) in enumerate(zip(gblocks, from_sib))]
        st["own"] = [own for own, _ in halves]
        sends = [send for _, send in halves]
        st["ici"], token = _exchange_start(sends, [t.shape for t in sends], _plan_ici(st["n"]), 3 * st["n"],
                                           "rs_%s_ici_start" % key, (self._token,))
        return token

    def rs_end(self, key, after):
        st = self._rs[key]
        return st["own"], _exchange_wait(st["ici"], _plan_ici(st["n"]), "rs_%s_ici_wait" % key, after)[1]


def _local_step(x, tgt, wqkv, wuz, wg, comm, g1, g2, g3, g4, ln_g, ln_b, w_sp, b_sp):
    S = x.shape[0]
    cos_t, sin_t = _rope_tables(S)
    b_sp_t = b_sp.T
    w_sp_t = w_sp.transpose(0, 2, 1)

    h1 = _rms_fwd(x, g1 + comm.start_token(), "rms_pre_mix")
    qkvs = [_proj_qkv(h1, wqkv, cos_t, sin_t, g, d, "proj_qkv_d%d" % d) for g, d in enumerate(DILS)]
    puz = _proj(h1, wuz, "none", "proj_uz", 1024)
    gates = _proj(h1, wg, "sigmoid", "proj_gates", 512)
    fwd = [_attn_fwd(t, "attn_fwd_d%d" % d) for t, d in zip(qkvs, DILS)]
    ya, *lses = _attn_combine([o for o, _ in fwd], [l for _, l in fwd])
    yg = _gmlp_fwd(puz, ln_g, ln_b, w_sp, b_sp_t)
    wba, wbg, wout, wmi, wmo = comm.late_weights(after=(ya, yg, gates))
    merged, y, x2, h3 = _merge_fwd(ya, yg, gates, wba, wbg, wout, x, g2, g3)
    a, dy2, dout, loss_row, dg4 = _mlp_fwd(h3, wmi, wmo, x2, tgt, g4)

    dpre, dx2, dy, dg3, dg2 = _mlp_bwd(dy2, a, wmo, wmi, x2, y, dout, g2, g3)
    dwmo = _mm_tn(a, dy2, "dw_mlp_out", square_a=True)
    dwmi = _mm_tn(h3, dpre, "dw_mlp_in", col_blocks=8)
    tie = comm.rs_start("mlp", [dwmi, _to_blocks(dwmo, False)])
    dgates, da, db, dyg, *rest = _outproj_bwd(dy, wout, ya, yg, gates, wba, wbg, tie)
    dyas, dsums = rest[:3], rest[3:]
    tie = comm.rs_mid("mlp", after=(dyg,))
    dpuz, dwsp, dbs, dlng, dlnb = _gmlp_bwd(puz, dyg, ln_g + tie[0, 0], ln_b, w_sp, w_sp_t, b_sp_t)
    dqkvs = [_attn_bwd(qkvs[g], dyas[g], dsums[g], lses[g], "attn_bwd_d%d" % d) for g, d in enumerate(DILS)]
    grad_x, dg1, dqkv = _inproj_bwd(dqkvs, dpuz, dgates, wqkv, wuz, wg, cos_t, sin_t, x, dx2, g1)
    dwin = jnp.concatenate([_mm_tn(h1, t, "dw_in_%d" % n) for n, t in enumerate((dqkv, dpuz, dgates))], axis=1)
    dwout = _mm_tn(merged, dy, "dw_out")
    dwba = _mm_tn(ya, da, "dw_branch_attn")
    dwbg = _mm_tn(yg, db, "dw_branch_gmlp")
    tie = comm.rs_start("win", [_to_blocks(dwin, True)], after=(dwout, dwba, dwbg))
    small = dict(g1=dg1, g2=dg2, g3=dg3, g4=dg4, ln_g=dlng, ln_b=dlnb, b_sp=dbs.reshape(4, CHUNK),
                 w_sp=dwsp.reshape(4 * CHUNK, CHUNK))
    mid = [_to_blocks(dwba, True), _to_blocks(dwbg, True), _to_blocks(dwout, False)]
    return loss_row + tie[:1], grad_x, small, mid


def kernel(x, norm_pre_mix, w_in, w_spatial, b_spatial, ln_v_gain, ln_v_bias, w_branch_attn, w_branch_gmlp, w_out, norm_post_mix, norm_pre_mlp, w_mlp_in, w_mlp_out, norm_post_mlp, loss_target, m_norm_pre_mix, m_w_in, m_w_spatial, m_b_spatial, m_ln_v_gain, m_ln_v_bias, m_w_branch_attn, m_w_branch_gmlp, m_w_out, m_norm_post_mix, m_norm_pre_mlp, m_w_mlp_in, m_w_mlp_out, m_norm_post_mlp, v_norm_pre_mix, v_w_in, v_w_spatial, v_b_spatial, v_ln_v_gain, v_ln_v_bias, v_w_branch_attn, v_w_branch_gmlp, v_w_out, v_norm_post_mix, v_norm_pre_mlp, v_w_mlp_in, v_w_mlp_out, v_norm_post_mlp):
    mx, my, mc = _mesh_pos()
    rel = [(0, 0), (1, 0), (0, 1), (1, 1)]
    chip_of = [2 * (mx ^ dx) + (my ^ dy) for dx, dy in rel]
    idx = jnp.stack([2 * k + mc for k in chip_of] + chip_of).astype(jnp.int32)

    (g_win,) = _all_gather_weights([w_in[0].astype(BF)])
    win = _from_blocks(g_win, True)
    wqkv, wuz, wg = win[:, :3 * AW], win[:, 3 * AW:3 * AW + 2 * GMLP_W], win[:, 3 * AW + 2 * GMLP_W:]
    late = [w_branch_attn[0], w_branch_gmlp[0], w_out[0], w_mlp_in[0], w_mlp_out[0]]
    comm = _FsdpComm([w.astype(BF) for w in late], [True, True, False, True, False], g_win, idx, 4 * mx + 2 * my + mc)

    loss_row, grad_x, small, mid = _local_step(
        x[0], loss_target[0], wqkv, wuz, wg, comm,
        norm_pre_mix, norm_post_mix, norm_pre_mlp, norm_post_mlp, ln_v_gain, ln_v_bias, w_spatial[0], b_spatial[0])

    flat = lambda t: t.reshape(-1, t.shape[-1])
    small_w = dict(zip(SMALL, map(flat, (norm_pre_mix, norm_post_mix, norm_pre_mlp, norm_post_mlp, ln_v_gain, ln_v_bias,
                                         b_spatial, w_spatial))))
    small_m = dict(zip(SMALL, map(flat, (m_norm_pre_mix, m_norm_post_mix, m_norm_pre_mlp, m_norm_post_mlp, m_ln_v_gain,
                                         m_ln_v_bias, m_b_spatial, m_w_spatial))))
    small_v = dict(zip(SMALL, map(flat, (v_norm_pre_mix, v_norm_post_mix, v_norm_pre_mlp, v_norm_post_mlp, v_ln_v_gain,
                                         v_ln_v_bias, v_b_spatial, v_w_spatial))))
    loss_out, sg, sd, sm, sv = _small_step(loss_row, small, small_w, small_m, small_v)
    tie = comm.rs_mid("win", after=(loss_out,))
    tie = comm.rs_start("mid", mid, after=(tie,))
    mlp_own, mlp_chips = comm.rs_end("mlp", after=(tie,))
    adam = lambda nm, own, r, w, m, v: _adam_shard(own, r, w[0], m[0], v[0], "adam_" + nm)
    upd = {
        "w_mlp_in": adam("w_mlp_in", mlp_own[0], mlp_chips[0], w_mlp_in, m_w_mlp_in, v_w_mlp_in),
        "w_mlp_out": adam("w_mlp_out", mlp_own[1], mlp_chips[1], w_mlp_out, m_w_mlp_out, v_w_mlp_out),
    }
    loss = loss_out[0, 0]
    tie = comm.rs_mid("mid", after=(upd["w_mlp_in"][0], upd["w_mlp_out"][0]))
    win_own, win_chips = comm.rs_end("win", after=(tie,))
    upd["w_in"] = adam("w_in", win_own[0], win_chips[0], w_in, m_w_in, v_w_in)
    mid_own, mid_chips = comm.rs_end("mid", after=(upd["w_in"][0],))
    upd["w_branch_attn"] = adam("w_branch_attn", mid_own[0], mid_chips[0], w_branch_attn, m_w_branch_attn, v_w_branch_attn)
    upd["w_branch_gmlp"] = adam("w_branch_gmlp", mid_own[1], mid_chips[1], w_branch_gmlp, m_w_branch_gmlp, v_w_branch_gmlp)
    upd["w_out"] = adam("w_out", mid_own[2], mid_chips[2], w_out, m_w_out, v_w_out)
    order = ["g1", "w_in", "w_sp", "b_sp", "ln_g", "ln_b", "w_branch_attn", "w_branch_gmlp", "w_out", "g2", "g3",
             "w_mlp_in", "w_mlp_out", "g4"]
    small_shape = dict(g1=norm_pre_mix.shape, g2=norm_post_mix.shape, g3=norm_pre_mlp.shape, g4=norm_post_mlp.shape,
                       ln_g=ln_v_gain.shape, ln_b=ln_v_bias.shape, b_sp=b_spatial.shape, w_sp=w_spatial.shape)

    def pick(which):
        return [upd[nm][which][None] if nm in upd else (sg, sd, sm, sv)[which][nm].reshape(small_shape[nm])
                for nm in order]

    return (loss, grad_x[None], *pick(0), *pick(1), *pick(2), *pick(3))
```

```python
import functools
import math

import jax
import jax.numpy as jnp
from jax import lax
from jax.experimental import pallas as pl
from jax.experimental.pallas import tpu as pltpu

D = 1024
HD = 64
NSLOT = 4
GW = NSLOT * HD
DILS = (1, 4, 16)
QB = 128
ATTN_NSUB = 4
MLP_TM_FWD = 512
MLP_TM_BWD = 256
AW = 3 * GW
GMLP_W = 512
CHUNK = 128
DFF = 4096
EPS = 1e-6
ROPE_THETA = 10000.0
SCALE = HD ** -0.5
NEG = -1e30

ADAM_LR = 0.001
ADAM_B1 = 0.9
ADAM_B2 = 0.999
ADAM_EPS = 1e-08
ADAM_WD = 0.01
ADAM_STEP = 10

BF = jnp.bfloat16
F32 = jnp.float32
MESH = pl.DeviceIdType.MESH
VMEM_LIMIT = 56 * 1024 * 1024


def _cp(sem):
    return pltpu.CompilerParams(dimension_semantics=sem, vmem_limit_bytes=VMEM_LIMIT)


def _dot(a, b):
    return jnp.dot(a, b, preferred_element_type=F32)


def _dot_nt(a, b):
    return lax.dot_general(a, b, (((1,), (1,)), ((), ())), preferred_element_type=F32)


def _dot_tn(a, b):
    return lax.dot_general(a, b, (((0,), (0,)), ((), ())), preferred_element_type=F32)


def _gelu(x):
    return jax.nn.gelu(x, approximate=True)


def _gelu_grad(x):
    k = math.sqrt(2.0 / math.pi)
    t = jnp.tanh(k * (x + 0.044715 * x * x * x))
    return 0.5 * (1.0 + t) + 0.5 * x * (1.0 - t * t) * (k * (1.0 + 3.0 * 0.044715 * x * x))


def _swap_halves(t):
    w = t.shape[1]
    lane = lax.broadcasted_iota(jnp.int32, t.shape, 1)
    first = (lane & (HD - 1)) < (HD // 2)
    return jnp.where(first, pltpu.roll(t, w - HD // 2, 1), pltpu.roll(t, HD // 2, 1))


def _head_mask(hh):
    lane = lax.broadcasted_iota(jnp.int32, (1, GW), 1)
    return jnp.logical_and(lane >= hh * HD, lane < (hh + 1) * HD)


def _rms_stats(xf):
    return lax.rsqrt(jnp.mean(xf * xf, axis=-1, keepdims=True) + EPS)


def _rms_bwd(xf, r, gain, dout):
    n = xf * r
    t = dout * gain
    dx = r * (t - n * jnp.mean(t * n, axis=-1, keepdims=True))
    return dx, jnp.sum(dout * n, axis=0, keepdims=True)


def _rms_fwd(x, gain, name):
    S = x.shape[0]
    tm = min(512, S)

    def body(x_ref, g_ref, h_ref):
        xf = x_ref[...]
        h_ref[...] = (xf * _rms_stats(xf) * g_ref[...]).astype(BF)

    return pl.pallas_call(
        body, name=name, out_shape=jax.ShapeDtypeStruct((S, D), BF), grid=(S // tm,),
        in_specs=[pl.BlockSpec((tm, D), lambda i: (i, 0)), pl.BlockSpec((1, D), lambda i: (0, 0))],
        out_specs=pl.BlockSpec((tm, D), lambda i: (i, 0)), compiler_params=_cp(("parallel",)))(x, gain)


def _proj(h, w_t, epi, name, tm):
    S, K = h.shape
    N = w_t.shape[0]
    tm = min(tm, S)

    def body(h_ref, w_ref, o_ref):
        y = _dot_nt(h_ref[...], w_ref[...])
        if epi == "sigmoid":
            y = jax.nn.sigmoid(y)
        o_ref[...] = y.astype(BF)

    return pl.pallas_call(
        body, name=name, out_shape=jax.ShapeDtypeStruct((S, N), BF), grid=(S // tm,),
        in_specs=[pl.BlockSpec((tm, K), lambda i: (i, 0)), _resident((N, K))],
        out_specs=pl.BlockSpec((tm, N), lambda i: (i, 0)),
        compiler_params=_cp(("parallel",)))(h, w_t)


def _dilate_store(val, scr, o_ref, lead, d):
    rows = val.shape[0]
    if d == 1:
        o_ref[lead + (0,)] = val.astype(o_ref.dtype)
        return
    for hf in range(2):
        scr[hf, pl.ds(0, rows), :] = val[:, hf * 128:(hf + 1) * 128]
    for r in range(d):
        for hf in range(2):
            o_ref[lead + (r, slice(None), slice(hf * 128, (hf + 1) * 128))] = (
                scr[hf, pl.ds(r, rows // d, stride=d), :].astype(o_ref.dtype))


def _undilate_load(i_ref, lead, d, scr, rows):
    if d == 1:
        return i_ref[lead + (0,)].astype(F32)
    for r in range(d):
        for hf in range(2):
            scr[hf, pl.ds(r, rows // d, stride=d), :] = (
                i_ref[lead + (r, slice(None), slice(hf * 128, (hf + 1) * 128))].astype(F32))
    return jnp.concatenate([scr[0, pl.ds(0, rows), :], scr[1, pl.ds(0, rows), :]], axis=1)


def _rope_fwd(y, c_ref, s_ref):
    cosv = jnp.concatenate([c_ref[...]] * 2, axis=1)
    sinv = jnp.concatenate([s_ref[...]] * 2, axis=1)
    return y * cosv + _swap_halves(y) * sinv


def _rope_bwd(dy, c_ref, s_ref):
    cosv = jnp.concatenate([c_ref[...]] * 2, axis=1)
    sinv = jnp.concatenate([s_ref[...]] * 2, axis=1)
    return dy * cosv + _swap_halves(dy * sinv)


def _proj_qkv(h, wqkv_t, cos_t, sin_t, g, d, name):
    S, K = h.shape
    tm = min(2048, S)

    def body(h_ref, w_ref, c_ref, s_ref, o_ref, scr):
        j = pl.program_id(1)
        y = _dot_nt(h_ref[...], w_ref[...])
        y = jnp.where(j < 2, _rope_fwd(y, c_ref, s_ref), y)
        _dilate_store(y, scr, o_ref, (), d)

    return pl.pallas_call(
        body, name=name, out_shape=jax.ShapeDtypeStruct((3, d, S // d, GW), BF), grid=(S // tm, 3),
        in_specs=[pl.BlockSpec((tm, K), lambda i, j: (i, 0)), pl.BlockSpec((GW, K), lambda i, j: (3 * j + g, 0)),
                  pl.BlockSpec((tm, 128), lambda i, j: (i, 0)), pl.BlockSpec((tm, 128), lambda i, j: (i, 0))],
        out_specs=pl.BlockSpec((None, d, tm // d, GW), lambda i, j: (j, 0, i, 0)),
        scratch_shapes=[pltpu.VMEM((2, tm, 128), F32)],
        compiler_params=_cp(("parallel", "arbitrary")))(h, wqkv_t, cos_t, sin_t)


def _band_masks(first_step):
    row = lax.broadcasted_iota(jnp.int32, (QB, 2 * QB), 0)
    col = lax.broadcasted_iota(jnp.int32, (QB, 2 * QB), 1)
    band = jnp.logical_and(col >= row, col <= row + QB)
    return band, jnp.logical_and(band, jnp.logical_or(col >= QB, jnp.logical_not(first_step)))


def _attn_fwd(qkv, name):
    _, d, L, _ = qkv.shape
    nsub = min(ATTN_NSUB, L // QB)
    R = nsub * QB
    nsteps = L // R

    def body(q_ref, kp_ref, kc_ref, vp_ref, vc_ref, o_ref, lse_ref):
        i = pl.program_id(1)
        band, band_first = _band_masks(i == 0)
        kfull = jnp.concatenate([kp_ref[...], kc_ref[...]], axis=0)
        vfull = jnp.concatenate([vp_ref[...], vc_ref[...]], axis=0)
        chains = [(sb, hh) for sb in range(nsub) for hh in range(NSLOT)]
        win = lambda t, sb: t[sb * QB:(sb + 2) * QB]
        scores = []
        for sb, hh in chains:
            qh = jnp.where(_head_mask(hh), q_ref[sb * QB:(sb + 1) * QB, :], 0)
            scores.append(_dot_nt(qh, win(kfull, sb)))
        soft = []
        for (sb, hh), sc in zip(chains, scores):
            sc = jnp.where(band_first if sb == 0 else band, sc * SCALE, NEG)
            m = jnp.max(sc, axis=1, keepdims=True)
            p = jnp.exp(sc - m)
            den = jnp.sum(p, axis=1, keepdims=True)
            soft.append((p.astype(BF), den, m + jnp.log(den)))
        accs = [_dot(p, win(vfull, sb)) for (sb, hh), (p, _, _) in zip(chains, soft)]
        for sb in range(nsub):
            o = jnp.zeros((QB, GW), F32)
            lse = jnp.zeros((QB, GW), F32)
            for hh in range(NSLOT):
                hm = _head_mask(hh)
                _, den, lrow = soft[sb * NSLOT + hh]
                o = o + jnp.where(hm, accs[sb * NSLOT + hh] / den, 0.0)
                lse = lse + jnp.where(hm, lrow, 0.0)
            o_ref[sb * QB:(sb + 1) * QB, :] = o
            lse_ref[sb * QB:(sb + 1) * QB, :] = lse

    prev = lambda i: jnp.maximum(i * nsub - 1, 0)
    cur = lambda t: pl.BlockSpec((None, None, R, GW), lambda r, i: (t, r, i, 0))
    prv = lambda t: pl.BlockSpec((None, None, QB, GW), lambda r, i: (t, r, prev(i), 0))
    out = pl.BlockSpec((None, R, GW), lambda r, i: (r, i, 0))
    return pl.pallas_call(
        body, name=name, grid=(d, nsteps),
        out_shape=(jax.ShapeDtypeStruct((d, L, GW), F32), jax.ShapeDtypeStruct((d, L, GW), F32)),
        in_specs=[cur(0), prv(1), cur(1), prv(2), cur(2)],
        out_specs=(out, out), compiler_params=_cp(("parallel", "arbitrary")))(qkv, qkv, qkv, qkv, qkv)


def _attn_combine(os_, lses):
    S = os_[0].shape[1]
    tm = min(1024, S)

    def body(o0, o1, o2, l0, l1, l2, y_ref, j0, j1, j2, scr):
        os_nat = [_undilate_load(o, (), d, scr, tm) for o, d in zip((o0, o1, o2), DILS)]
        a, b, c = [_undilate_load(l, (), d, scr, tm) for l, d in zip((l0, l1, l2), DILS)]
        m = jnp.maximum(jnp.maximum(a, b), c)
        wa, wb, wc = jnp.exp(a - m), jnp.exp(b - m), jnp.exp(c - m)
        den = wa + wb + wc
        y_ref[...] = ((wa * os_nat[0] + wb * os_nat[1] + wc * os_nat[2]) / den).astype(BF)
        lse = m + jnp.log(den)
        for j_ref, d in zip((j0, j1, j2), DILS):
            _dilate_store(lse, scr, j_ref, (), d)

    dil = lambda d: pl.BlockSpec((d, tm // d, GW), lambda i: (0, i, 0))
    dshape = lambda d: jax.ShapeDtypeStruct((d, S // d, GW), F32)
    return pl.pallas_call(
        body, name="attn_combine", grid=(S // tm,),
        out_shape=(jax.ShapeDtypeStruct((S, GW), BF),) + tuple(dshape(d) for d in DILS),
        in_specs=[dil(d) for d in DILS] * 2,
        out_specs=(pl.BlockSpec((tm, GW), lambda i: (i, 0)),) + tuple(dil(d) for d in DILS),
        scratch_shapes=[pltpu.VMEM((2, tm, 128), F32)],
        compiler_params=_cp(("parallel",)))(*os_, *lses)


def _tril(upper=False):
    row = lax.broadcasted_iota(jnp.int32, (CHUNK, CHUNK), 0)
    col = lax.broadcasted_iota(jnp.int32, (CHUNK, CHUNK), 1)
    return row <= col if upper else col <= row


def _ln_fwd(z, gain, bias):
    mu = jnp.mean(z, axis=-1, keepdims=True)
    zc = z - mu
    rstd = lax.rsqrt(jnp.mean(zc * zc, axis=-1, keepdims=True) + EPS)
    zhat = zc * rstd
    return zhat, rstd, zhat * gain + bias


def _gmlp_fwd(puz, ln_g, ln_b, w_sp, b_sp_t):
    S = puz.shape[0]
    tm = min(512, S)
    nch = tm // CHUNK

    def body(p_ref, g_ref, b_ref, w_ref, bt_ref, o_ref):
        tril = _tril()
        ws = [jnp.where(tril, w_ref[gg], 0.0).astype(BF) for gg in range(4)]
        for ch in range(nch):
            rows = slice(ch * CHUNK, (ch + 1) * CHUNK)
            z = _gelu(p_ref[rows, GMLP_W:].astype(F32))
            _, _, zn = _ln_fwd(z, g_ref[...], b_ref[...])
            zn = zn.astype(BF)
            for gg in range(4):
                cols = slice(gg * CHUNK, (gg + 1) * CHUNK)
                sz = _dot(ws[gg], zn[:, cols]) + bt_ref[:, gg:gg + 1]
                u = _gelu(p_ref[rows, cols].astype(F32))
                o_ref[rows, cols] = (u * sz).astype(BF)

    return pl.pallas_call(
        body, name="gmlp_fwd", out_shape=jax.ShapeDtypeStruct((S, GMLP_W), BF), grid=(S // tm,),
        in_specs=[pl.BlockSpec((tm, 2 * GMLP_W), lambda i: (i, 0)),
                  pl.BlockSpec((1, GMLP_W), lambda i: (0, 0)), pl.BlockSpec((1, GMLP_W), lambda i: (0, 0)),
                  pl.BlockSpec((4, CHUNK, CHUNK), lambda i: (0, 0, 0)), pl.BlockSpec((CHUNK, 4), lambda i: (0, 0))],
        out_specs=pl.BlockSpec((tm, GMLP_W), lambda i: (i, 0)),
        compiler_params=_cp(("parallel",)))(puz, ln_g, ln_b, w_sp, b_sp_t)


def _merge_fwd(ya, yg, gates, wba, wbg, wout, x, g2, g3):
    S = x.shape[0]
    tm = min(512, S)

    def body(ya_ref, yg_ref, gt_ref, wba_ref, wbg_ref, wo_ref, x_ref, g2_ref, g3_ref,
             mg_ref, y_ref, x2_ref, h3_ref):
        a = _dot(ya_ref[...], wba_ref[...])
        b = _dot(yg_ref[...], wbg_ref[...])
        merged = (gt_ref[:, :D].astype(F32) * a + gt_ref[:, D:].astype(F32) * b).astype(BF)
        mg_ref[...] = merged
        y = _dot(merged, wo_ref[...])
        y_ref[...] = y
        x2 = x_ref[...] + y * _rms_stats(y) * g2_ref[...]
        x2_ref[...] = x2
        h3_ref[...] = (x2 * _rms_stats(x2) * g3_ref[...]).astype(BF)

    row = lambda w: pl.BlockSpec((tm, w), lambda i: (i, 0))
    full = lambda s: pl.BlockSpec(s, lambda i: (0, 0))
    return pl.pallas_call(
        body, name="merge_fwd", grid=(S // tm,),
        out_shape=(jax.ShapeDtypeStruct((S, D), BF), jax.ShapeDtypeStruct((S, D), F32), jax.ShapeDtypeStruct((S, D), F32),
                   jax.ShapeDtypeStruct((S, D), BF)),
        in_specs=[row(GW), row(GMLP_W), row(2 * D), full((GW, D)), full((GMLP_W, D)), full((D, D)), row(D),
                  full((1, D)), full((1, D))],
        out_specs=(row(D), row(D), row(D), row(D)),
        compiler_params=_cp(("parallel",)))(ya, yg, gates, wba, wbg, wout, x, g2, g3)


def _resident(shape):
    return pl.BlockSpec(shape, lambda i: (0,) * len(shape), pipeline_mode=pl.Buffered(1))


def _mlp_fwd(h3, wmi, wmo, x2, tgt, g4):
    S = x2.shape[0]
    tm = min(MLP_TM_FWD, S)

    def body(h_ref, wi_ref, wo_ref, x2_ref, t_ref, g4_ref, a_ref, dy2_ref, dout_ref, loss_ref, dg4_ref):
        @pl.when(pl.program_id(0) == 0)
        def _():
            loss_ref[...] = jnp.zeros_like(loss_ref)
            dg4_ref[...] = jnp.zeros_like(dg4_ref)

        a = jnp.maximum(_dot(h_ref[...], wi_ref[...]), 0.0)
        a_ref[...] = a.astype(BF)
        y2 = _dot((a * a).astype(BF), wo_ref[...])
        r = _rms_stats(y2)
        out = x2_ref[...] + y2 * r * g4_ref[...]
        err = out - t_ref[...]
        tot = jnp.sum(jnp.sum(err * err, axis=1, keepdims=True), axis=0, keepdims=True) * (0.5 / D)
        lane = lax.broadcasted_iota(jnp.int32, (1, 128), 1)
        loss_ref[...] += jnp.where(lane == 0, tot, 0.0)
        dout = err * (1.0 / D)
        dout_ref[...] = dout
        dy2, dg = _rms_bwd(y2, r, g4_ref[...], dout)
        dy2_ref[...] = dy2.astype(BF)
        dg4_ref[...] += dg

    row = pl.BlockSpec((tm, D), lambda i: (i, 0))
    return pl.pallas_call(
        body, name="mlp_fwd", grid=(S // tm,),
        out_shape=(jax.ShapeDtypeStruct((S, DFF), BF), jax.ShapeDtypeStruct((S, D), BF), jax.ShapeDtypeStruct((S, D), F32),
                   jax.ShapeDtypeStruct((1, 128), F32), jax.ShapeDtypeStruct((1, D), F32)),
        in_specs=[row, _resident((D, DFF)), _resident((DFF, D)), row, row, pl.BlockSpec((1, D), lambda i: (0, 0))],
        out_specs=(pl.BlockSpec((tm, DFF), lambda i: (i, 0)), row, row,
                   pl.BlockSpec((1, 128), lambda i: (0, 0)), pl.BlockSpec((1, D), lambda i: (0, 0))),
        compiler_params=_cp(("arbitrary",)))(h3, wmi, wmo, x2, tgt, g4)


def _mlp_bwd(dy2, a, wmo, wmi, x2, y, dout, g2, g3):
    S = x2.shape[0]
    tm = min(MLP_TM_BWD, S)

    def body(dy2_ref, a_ref, wo_ref, wi_ref, x2_ref, y_ref, dout_ref, g2_ref, g3_ref,
             dpre_ref, dx2_ref, dy_ref, dg3_ref, dg2_ref):
        @pl.when(pl.program_id(0) == 0)
        def _():
            dg3_ref[...] = jnp.zeros_like(dg3_ref)
            dg2_ref[...] = jnp.zeros_like(dg2_ref)

        da2 = _dot_nt(dy2_ref[...], wo_ref[...])
        dpre = (2.0 * a_ref[...].astype(F32) * da2).astype(BF)
        dpre_ref[...] = dpre
        dh3 = _dot_nt(dpre, wi_ref[...])
        x2 = x2_ref[...]
        dx3, dg3 = _rms_bwd(x2, _rms_stats(x2), g3_ref[...], dh3)
        dx2 = dout_ref[...] + dx3
        dx2_ref[...] = dx2
        dg3_ref[...] += dg3
        yv = y_ref[...]
        dy, dg2 = _rms_bwd(yv, _rms_stats(yv), g2_ref[...], dx2)
        dy_ref[...] = dy.astype(BF)
        dg2_ref[...] += dg2

    row = pl.BlockSpec((tm, D), lambda i: (i, 0))
    wide = pl.BlockSpec((tm, DFF), lambda i: (i, 0))
    vec = pl.BlockSpec((1, D), lambda i: (0, 0))
    return pl.pallas_call(
        body, name="mlp_bwd", grid=(S // tm,),
        out_shape=(jax.ShapeDtypeStruct((S, DFF), BF), jax.ShapeDtypeStruct((S, D), F32), jax.ShapeDtypeStruct((S, D), BF),
                   jax.ShapeDtypeStruct((1, D), F32), jax.ShapeDtypeStruct((1, D), F32)),
        in_specs=[row, wide, _resident((DFF, D)), _resident((D, DFF)), row, row, row, vec, vec],
        out_specs=(wide, row, row, vec, vec),
        compiler_params=_cp(("arbitrary",)))(dy2, a, wmo, wmi, x2, y, dout, g2, g3)


def _mm_tn(a, b, name, square_a=False, tm=1024, tn=1024, tk=2048, tie=None, col_blocks=None):
    S, M = a.shape
    N = b.shape[1]
    tk = min(tk, S)
    tm = max(t for t in range(128, min(tm, M) + 1, 128) if M % t == 0)
    tn = max(t for t in range(128, min(tn, N) + 1, 128) if N % t == 0)
    cb = N // col_blocks if col_blocks else tn
    assert M % tm == 0 and S % tk == 0 and tn % cb == 0
    nk = S // tk
    ties = () if tie is None else (tie,)

    def body(a_ref, b_ref, *rest):
        o_ref = rest[-1]
        k = pl.program_id(2)
        av = a_ref[...]
        if square_a:
            av = av * av
        part = _dot_tn(av, b_ref[...])
        if col_blocks:
            part = jnp.stack([part[:, t * cb:(t + 1) * cb] for t in range(tn // cb)])

        @pl.when(k == 0)
        def _():
            o_ref[...] = part

        @pl.when(k > 0)
        def _():
            o_ref[...] += part

    if col_blocks:
        out_shape, out_spec = (col_blocks, M, cb), pl.BlockSpec((tn // cb, tm, cb), lambda i, j, k: (j, i, 0))
    else:
        out_shape, out_spec = (M, N), pl.BlockSpec((tm, tn), lambda i, j, k: (i, j))
    return pl.pallas_call(
        body, name=name, out_shape=jax.ShapeDtypeStruct(out_shape, F32), grid=(M // tm, N // tn, nk),
        in_specs=[pl.BlockSpec((tk, tm), lambda i, j, k: (k, i)), pl.BlockSpec((tk, tn), lambda i, j, k: (k, j))]
        + [pl.BlockSpec(memory_space=pl.ANY)] * len(ties),
        out_specs=out_spec,
        compiler_params=_cp(("parallel", "parallel", "arbitrary")))(a, b, *ties)


def _outproj_bwd(dy, wout, ya, yg, gates, wba, wbg, tie):
    S = dy.shape[0]
    tm = min(512, S)

    def body(dy_ref, wo_ref, ya_ref, yg_ref, gt_ref, wba_ref, wbg_ref, tie_ref,
             dgt_ref, da_ref, db_ref, dyg_ref, e0, e1, e2, s0, s1, s2, scr):
        dm = _dot_nt(dy_ref[...], wo_ref[...])
        ga, gb = gt_ref[:, :D].astype(F32), gt_ref[:, D:].astype(F32)
        dgt_ref[:, :D] = (dm * _dot(ya_ref[...], wba_ref[...]) * ga * (1.0 - ga)).astype(BF)
        dgt_ref[:, D:] = (dm * _dot(yg_ref[...], wbg_ref[...]) * gb * (1.0 - gb)).astype(BF)
        da = (dm * ga).astype(BF)
        db = (dm * gb).astype(BF)
        da_ref[...] = da
        db_ref[...] = db
        dyg_ref[...] = _dot_nt(db, wbg_ref[...]).astype(BF)
        dya = _dot_nt(da, wba_ref[...]).astype(BF).astype(F32)
        dyy = dya * ya_ref[...].astype(F32)
        dsum = jnp.zeros((tm, GW), F32)
        for hh in range(NSLOT):
            hm = _head_mask(hh)
            dsum = dsum + jnp.where(hm, jnp.sum(jnp.where(hm, dyy, 0.0), axis=1, keepdims=True), 0.0)
        for e_ref, s_ref, d in zip((e0, e1, e2), (s0, s1, s2), DILS):
            _dilate_store(dya, scr, e_ref, (), d)
            _dilate_store(dsum, scr, s_ref, (), d)

    row = lambda w: pl.BlockSpec((tm, w), lambda i: (i, 0))
    full = lambda s: pl.BlockSpec(s, lambda i: (0, 0))
    dil = lambda d: pl.BlockSpec((d, tm // d, GW), lambda i: (0, i, 0))
    dshape = lambda d, t: jax.ShapeDtypeStruct((d, S // d, GW), t)
    return pl.pallas_call(
        body, name="outproj_bwd", grid=(S // tm,),
        out_shape=(jax.ShapeDtypeStruct((S, 2 * D), BF), jax.ShapeDtypeStruct((S, D), BF), jax.ShapeDtypeStruct((S, D), BF),
                   jax.ShapeDtypeStruct((S, GMLP_W), BF)) + tuple(dshape(d, BF) for d in DILS)
        + tuple(dshape(d, F32) for d in DILS),
        in_specs=[row(D), full((D, D)), row(GW), row(GMLP_W), row(2 * D), full((GW, D)), full((GMLP_W, D)),
                  pl.BlockSpec(memory_space=pl.ANY)],
        out_specs=(row(2 * D), row(D), row(D), row(GMLP_W)) + tuple(dil(d) for d in DILS) * 2,
        scratch_shapes=[pltpu.VMEM((2, tm, 128), F32)],
        compiler_params=_cp(("parallel",)))(dy, wout, ya, yg, gates, wba, wbg, tie)


def _gmlp_bwd(puz, dyg, ln_g, ln_b, w_sp, w_sp_t, b_sp_t):
    S = puz.shape[0]
    tm = min(512, S)
    nch = tm // CHUNK

    def body(p_ref, dy_ref, g_ref, b_ref, w_ref, wt_ref, bt_ref,
             dp_ref, dw_ref, dbs_ref, dg_ref, dbias_ref, dbacc_ref):
        i = pl.program_id(0)

        @pl.when(i == 0)
        def _():
            dw_ref[...] = jnp.zeros_like(dw_ref)
            dbacc_ref[...] = jnp.zeros_like(dbacc_ref)
            dg_ref[...] = jnp.zeros_like(dg_ref)
            dbias_ref[...] = jnp.zeros_like(dbias_ref)

        tril = _tril()
        ws = [jnp.where(tril, w_ref[gg], 0.0).astype(BF) for gg in range(4)]
        triu = _tril(upper=True)
        wts = [jnp.where(triu, wt_ref[gg], 0.0).astype(BF) for gg in range(4)]
        gain = g_ref[...]
        for ch in range(nch):
            rows = slice(ch * CHUNK, (ch + 1) * CHUNK)
            pz = p_ref[rows, GMLP_W:].astype(F32)
            z = _gelu(pz)
            zhat, rstd, zn = _ln_fwd(z, gain, b_ref[...])
            znb = zn.astype(BF)
            dzn_parts = []
            for gg in range(4):
                cols = slice(gg * CHUNK, (gg + 1) * CHUNK)
                pu = p_ref[rows, cols].astype(F32)
                u = _gelu(pu)
                sz = _dot(ws[gg], znb[:, cols]) + bt_ref[:, gg:gg + 1]
                dyv = dy_ref[rows, cols].astype(F32)
                dp_ref[rows, cols] = (dyv * sz * _gelu_grad(pu)).astype(BF)
                dsz = dyv * u
                dbacc_ref[gg] += dsz
                dszb = dsz.astype(BF)
                dw_ref[gg] += _dot_nt(dszb, znb[:, cols])
                dzn_parts.append(_dot(wts[gg], dszb))
            dzn = jnp.concatenate(dzn_parts, axis=1)
            dg_ref[...] += jnp.sum(dzn * zhat, axis=0, keepdims=True)
            dbias_ref[...] += jnp.sum(dzn, axis=0, keepdims=True)
            dzh = dzn * gain
            dz = rstd * (dzh - jnp.mean(dzh, axis=-1, keepdims=True)
                         - zhat * jnp.mean(dzh * zhat, axis=-1, keepdims=True))
            dp_ref[rows, GMLP_W:] = (dz * _gelu_grad(pz)).astype(BF)

        @pl.when(i == pl.num_programs(0) - 1)
        def _():
            for gg in range(4):
                dw_ref[gg] = jnp.where(tril, dw_ref[gg], 0.0)
                dbs_ref[gg] = jnp.sum(dbacc_ref[gg], axis=1, keepdims=True)

    full2 = lambda s: pl.BlockSpec(s, lambda i: (0, 0))
    full3 = lambda s: pl.BlockSpec(s, lambda i: (0, 0, 0))
    return pl.pallas_call(
        body, name="gmlp_bwd", grid=(S // tm,),
        out_shape=(jax.ShapeDtypeStruct((S, 2 * GMLP_W), BF), jax.ShapeDtypeStruct((4, CHUNK, CHUNK), F32),
                   jax.ShapeDtypeStruct((4, CHUNK, 1), F32), jax.ShapeDtypeStruct((1, GMLP_W), F32),
                   jax.ShapeDtypeStruct((1, GMLP_W), F32)),
        in_specs=[pl.BlockSpec((tm, 2 * GMLP_W), lambda i: (i, 0)), pl.BlockSpec((tm, GMLP_W), lambda i: (i, 0)),
                  full2((1, GMLP_W)), full2((1, GMLP_W)), full3((4, CHUNK, CHUNK)), full3((4, CHUNK, CHUNK)),
                  full2((CHUNK, 4))],
        out_specs=(pl.BlockSpec((tm, 2 * GMLP_W), lambda i: (i, 0)), full3((4, CHUNK, CHUNK)), full3((4, CHUNK, 1)),
                   full2((1, GMLP_W)), full2((1, GMLP_W))),
        scratch_shapes=[pltpu.VMEM((4, CHUNK, CHUNK), F32)],
        compiler_params=_cp(("arbitrary",)))(puz, dyg, ln_g, ln_b, w_sp, w_sp_t, b_sp_t)


def _attn_bwd(qkv, dya, dsums, lse, name):
    _, d, L, _ = qkv.shape
    nsub = min(ATTN_NSUB, L // QB)
    R = nsub * QB
    nsteps = L // R

    def body(q_ref, qn_ref, kp_ref, kc_ref, vp_ref, vc_ref, dy_ref, dyn_ref, e_ref, en_ref, l_ref, ln_ref, o_ref):
        i = pl.program_id(1)
        band, band_first = _band_masks(i == 0)
        row = lax.broadcasted_iota(jnp.int32, (QB, QB), 0)
        col = lax.broadcasted_iota(jnp.int32, (QB, QB), 1)
        mask_next = jnp.logical_and(col >= row, i < nsteps - 1)
        kc, vc = kc_ref[...], vc_ref[...]
        kfull = jnp.concatenate([kp_ref[...], kc], axis=0)
        vfull = jnp.concatenate([vp_ref[...], vc], axis=0)
        k_last, v_last = kc[(nsub - 1) * QB:], vc[(nsub - 1) * QB:]
        q_ext = jnp.concatenate([q_ref[...], qn_ref[...]], axis=0)
        dy_ext = jnp.concatenate([dy_ref[...], dyn_ref[...]], axis=0)
        esum, esum_n, lse, lse_n = e_ref[...], en_ref[...], l_ref[...], ln_ref[...]
        win = lambda t, sb: t[sb * QB:(sb + 2) * QB]
        blk = lambda t, sb: t[sb * QB:(sb + 1) * QB]
        hms = [_head_mask(hh) for hh in range(NSLOT)]
        q_hs = [jnp.where(hm, q_ext, 0) for hm in hms]
        dy_hs = [jnp.where(hm, dy_ext, 0) for hm in hms]
        raw = []
        for hh in range(NSLOT):
            tiles = [(_dot_nt(blk(q_hs[hh], sb), win(kfull, sb)), _dot_nt(blk(dy_hs[hh], sb), win(vfull, sb)))
                     for sb in range(nsub)]
            tiles.append((_dot_nt(blk(q_hs[hh], nsub), k_last), _dot_nt(blk(dy_hs[hh], nsub), v_last)))
            raw.append(tiles)
        ps, dss = [], []
        for hh in range(NSLOT):
            rowstat = lambda t: jnp.max(jnp.where(hms[hh], t, -jnp.inf), axis=1, keepdims=True)
            p_h, ds_h = [], []
            for sb in range(nsub + 1):
                sc, dp = raw[hh][sb]
                if sb < nsub:
                    msk, lrow, erow = (band_first if sb == 0 else band), rowstat(blk(lse, sb)), rowstat(blk(esum, sb))
                else:
                    msk, lrow, erow = mask_next, rowstat(lse_n), rowstat(esum_n)
                p = jnp.where(msk, jnp.exp(sc * SCALE - lrow), 0.0)
                p_h.append(p.astype(BF))
                ds_h.append((p * (dp - erow)).astype(BF))
            ps.append(p_h)
            dss.append(ds_h)
        dq = [jnp.zeros((QB, GW), F32) for _ in range(nsub)]
        dk = [jnp.zeros((QB, GW), F32) for _ in range(nsub)]
        dv = [jnp.zeros((QB, GW), F32) for _ in range(nsub)]
        for hh in range(NSLOT):
            for sb in range(nsub):
                dq[sb] = dq[sb] + jnp.where(hms[hh], _dot(dss[hh][sb], win(kfull, sb)), 0.0)
                nxt = lambda t: t[sb + 1][:, :QB] if sb + 1 < nsub else t[nsub]
                dk[sb] = dk[sb] + _dot_tn(jnp.concatenate([dss[hh][sb][:, QB:], nxt(dss[hh])], axis=0), win(q_hs[hh], sb))
                dv[sb] = dv[sb] + _dot_tn(jnp.concatenate([ps[hh][sb][:, QB:], nxt(ps[hh])], axis=0), win(dy_hs[hh], sb))
        for sb in range(nsub):
            rows = slice(sb * QB, (sb + 1) * QB)
            o_ref[0, rows, :] = (dq[sb] * SCALE).astype(BF)
            o_ref[1, rows, :] = (dk[sb] * SCALE).astype(BF)
            o_ref[2, rows, :] = dv[sb].astype(BF)

    prev = lambda i: jnp.maximum(i * nsub - 1, 0)
    nxt = lambda i: jnp.minimum((i + 1) * nsub, L // QB - 1)
    cur4 = lambda t: pl.BlockSpec((None, None, R, GW), lambda r, i: (t, r, i, 0))
    prv4 = lambda t: pl.BlockSpec((None, None, QB, GW), lambda r, i: (t, r, prev(i), 0))
    nxt4 = lambda t: pl.BlockSpec((None, None, QB, GW), lambda r, i: (t, r, nxt(i), 0))
    cur3 = pl.BlockSpec((None, R, GW), lambda r, i: (r, i, 0))
    nxt3 = pl.BlockSpec((None, QB, GW), lambda r, i: (r, nxt(i), 0))
    return pl.pallas_call(
        body, name=name, grid=(d, nsteps), out_shape=jax.ShapeDtypeStruct((3, d, L, GW), BF),
        in_specs=[cur4(0), nxt4(0), prv4(1), cur4(1), prv4(2), cur4(2), cur3, nxt3, cur3, nxt3, cur3, nxt3],
        out_specs=pl.BlockSpec((3, None, R, GW), lambda r, i: (0, r, i, 0)),
        compiler_params=_cp(("parallel", "arbitrary")))(qkv, qkv, qkv, qkv, qkv, qkv, dya, dya, dsums, dsums, lse, lse)


def _inproj_bwd(dqkvs, dpuz, dgates, wqkv_t, wuz_t, wg_t, cos_t, sin_t, x, dx2, g1):
    S = x.shape[0]
    tm = min(512, S)

    def body(d0_ref, d1_ref, d2_ref, dp_ref, dg_ref, wqkv_ref, wuz_ref, wg_ref, c_ref, s_ref,
             x_ref, dx2_ref, g1_ref, gx_ref, dg1_ref, dn_ref, scr):
        i = pl.program_id(0)

        @pl.when(i == 0)
        def _():
            dg1_ref[...] = jnp.zeros_like(dg1_ref)

        for t in range(3):
            for g, (d_ref, d) in enumerate(zip((d0_ref, d1_ref, d2_ref), DILS)):
                piece = _undilate_load(d_ref, (t,), d, scr, tm)
                if t < 2:
                    piece = _rope_bwd(piece, c_ref, s_ref)
                dn_ref[:, (3 * t + g) * GW:(3 * t + g + 1) * GW] = piece.astype(BF)
        dh = _dot(dn_ref[...], wqkv_ref[...]) + _dot(dp_ref[...], wuz_ref[...]) + _dot(dg_ref[...], wg_ref[...])
        xv = x_ref[...]
        dx1, dg1 = _rms_bwd(xv, _rms_stats(xv), g1_ref[...], dh)
        gx_ref[...] = dx2_ref[...] + dx1
        dg1_ref[...] += dg1

    row = lambda w: pl.BlockSpec((tm, w), lambda i: (i, 0))
    full = lambda s: pl.BlockSpec(s, lambda i: (0, 0))
    dil = lambda d: pl.BlockSpec((3, d, tm // d, GW), lambda i: (0, 0, i, 0))
    return pl.pallas_call(
        body, name="inproj_bwd", grid=(S // tm,),
        out_shape=(jax.ShapeDtypeStruct((S, D), F32), jax.ShapeDtypeStruct((1, D), F32),
                   jax.ShapeDtypeStruct((S, 3 * AW), BF)),
        in_specs=[dil(d) for d in DILS] + [row(2 * GMLP_W), row(2 * D), full((3 * AW, D)), full((2 * GMLP_W, D)),
                                            full((2 * D, D)), row(128), row(128), row(D), row(D), full((1, D))],
        out_specs=(row(D), full((1, D)), row(3 * AW)),
        scratch_shapes=[pltpu.VMEM((2, tm, 128), F32)],
        compiler_params=_cp(("arbitrary",)))(*dqkvs, dpuz, dgates, wqkv_t, wuz_t, wg_t, cos_t, sin_t, x, dx2, g1)


def _row_tile(rows, cap=256):
    return max(t for t in range(16, cap + 1, 16) if rows % t == 0)


def _adam_math(w, g, m, v):
    m2 = ADAM_B1 * m + (1.0 - ADAM_B1) * g
    v2 = ADAM_B2 * v + (1.0 - ADAM_B2) * (g * g)
    m_hat = m2 / (1.0 - ADAM_B1 ** ADAM_STEP)
    v_hat = v2 / (1.0 - ADAM_B2 ** ADAM_STEP)
    delta = -ADAM_LR * (m_hat / (jnp.sqrt(v_hat) + ADAM_EPS) + ADAM_WD * w)
    return delta, m2, v2


def _adam_shard(own, recv, w, m, v, name):
    R, C = w.shape
    tr = _row_tile(R)

    def body(own_ref, r_ref, w_ref, m_ref, v_ref, g_ref, d_ref, m2_ref, v2_ref):
        g = own_ref[...] + r_ref[0].astype(F32) + r_ref[1].astype(F32) + r_ref[2].astype(F32)
        g_ref[...] = g
        d_ref[...], m2_ref[...], v2_ref[...] = _adam_math(w_ref[...], g, m_ref[...], v_ref[...])

    spec = pl.BlockSpec((tr, C), lambda i: (i, 0))
    out = jax.ShapeDtypeStruct((R, C), F32)
    return pl.pallas_call(
        body, name=name, grid=(R // tr,), out_shape=(out, out, out, out),
        in_specs=[spec, pl.BlockSpec((3, tr, C), lambda i: (0, i, 0)), spec, spec, spec],
        out_specs=(spec, spec, spec, spec), compiler_params=_cp(("parallel",)))(own, recv, w, m, v)


def _rs_add(gblocks, recv, idx, name):
    _, R, C = gblocks.shape
    tr = _row_tile(R)

    def body(t_ref, g_ref, r_ref, own_ref, send_ref):
        j = pl.program_id(1)
        s = g_ref[...] + r_ref[...]

        @pl.when(j == 0)
        def _():
            own_ref[...] = s

        @pl.when(j > 0)
        def _():
            send_ref[...] = s.astype(BF)

    grid_spec = pltpu.PrefetchScalarGridSpec(
        num_scalar_prefetch=1, grid=(R // tr, 4),
        in_specs=[pl.BlockSpec((None, tr, C), lambda i, j, t: (t[j], i, 0)),
                  pl.BlockSpec((None, tr, C), lambda i, j, t: (t[4 + j], i, 0))],
        out_specs=[pl.BlockSpec((tr, C), lambda i, j, t: (i, 0)),
                   pl.BlockSpec((None, tr, C), lambda i, j, t: (jnp.maximum(j - 1, 0), i, 0))])
    return pl.pallas_call(
        body, name=name, grid_spec=grid_spec,
        out_shape=(jax.ShapeDtypeStruct((R, C), F32), jax.ShapeDtypeStruct((3, R, C), BF)),
        compiler_params=_cp(("parallel", "arbitrary")))(idx, gblocks, recv)


def _mesh_pos():
    return lax.axis_index("x"), lax.axis_index("y"), lax.axis_index("c")


def _all_gather_weights(shards):
    n = len(shards)

    def body(*refs):
        ins, outs = refs[:n], refs[n:2 * n]
        send_sems, recv_sems, loc_sems = refs[2 * n:]
        x, y, c = _mesh_pos()
        me, sib = (x, y, c), (x, y, 1 - c)
        chips = [(1 - x, y), (x, 1 - y), (1 - x, 1 - y)]

        def rcopy(k, s, block, to, src=None):
            dst = outs[k].at[4 * block[0] + 2 * block[1] + block[2]]
            return pltpu.make_async_remote_copy(
                src_ref=dst if src is None else src, dst_ref=dst, send_sem=send_sems.at[k, s],
                recv_sem=recv_sems.at[k, s], device_id=to, device_id_type=MESH)

        locs = [pltpu.make_async_copy(ins[k], outs[k].at[4 * x + 2 * y + c], loc_sems.at[k]) for k in range(n)]
        for cp in locs:
            cp.start()
        started = []
        for k in range(n):
            for j, chip in enumerate(chips):
                started.append(rcopy(k, 1 + j, me, (*chip, c), src=ins[k]))
                started[-1].start()
        for k in range(n):
            started.append(rcopy(k, 0, me, sib, src=ins[k]))
            started[-1].start()
        for k in range(n):
            for j, chip in enumerate(chips):
                rcopy(k, 1 + j, (*chip, c), me).wait_recv()
                started.append(rcopy(k, 4 + j, (*chip, c), sib))
                started[-1].start()
        for k in range(n):
            rcopy(k, 0, sib, me).wait_recv()
            for j, chip in enumerate(chips):
                rcopy(k, 4 + j, (*chip, 1 - c), me).wait_recv()
        for cp in started:
            cp.wait_send()
        for cp in locs:
            cp.wait()

    hbm = pl.BlockSpec(memory_space=pl.ANY)
    return pl.pallas_call(
        body, name="ag_weights",
        out_shape=tuple(jax.ShapeDtypeStruct((8,) + s.shape, s.dtype) for s in shards),
        in_specs=[hbm] * n, out_specs=(hbm,) * n,
        scratch_shapes=[pltpu.SemaphoreType.DMA((n, 7)), pltpu.SemaphoreType.DMA((n, 7)),
                        pltpu.SemaphoreType.DMA((n,))])(*shards)


_HBM = pl.BlockSpec(memory_space=pltpu.HBM)
_SEM = pl.BlockSpec(memory_space=pltpu.SEMAPHORE)
_EFFECT = pltpu.SideEffectType.DATAFLOW_SIDE_EFFECTING
_RELATIONS = [(dx, dy, dc) for dx in (0, 1) for dy in (0, 1) for dc in (0, 1)][1:]


def _flip(v, d):
    return 1 - v if d else v


def _plan_gather(n):
    def plan(x, y, c):
        return [(k, None, 4 * x + 2 * y + c, (_flip(x, dx), _flip(y, dy), _flip(c, dc)))
                for k in range(n) for dx, dy, dc in _RELATIONS]
    return plan


def _plan_d2d(n):
    def plan(x, y, c):
        return [(k, 2 * kk + 1 - c, kk, (x, y, 1 - c)) for k in range(n) for kk in range(4)]
    return plan


def _plan_ici(n):
    def plan(x, y, c):
        return [(k, j, j, (_flip(x, dx), _flip(y, dy), c))
                for k in range(n) for j, (dx, dy) in enumerate(((1, 0), (0, 1), (1, 1)))]
    return plan


def _plan_copies(plan, src_refs, land_refs, send_sems, recv_sems):
    x, y, c = _mesh_pos()
    return [pltpu.make_async_remote_copy(
        src_ref=src_refs[k] if si is None else src_refs[k].at[si], dst_ref=land_refs[k].at[di],
        send_sem=send_sems.at[n], recv_sem=recv_sems.at[n], device_id=dev, device_id_type=MESH)
        for n, (k, si, di, dev) in enumerate(plan(x, y, c))]


def _exchange_start(srcs, land_shapes, plan, ncopies, name, after):
    n = len(srcs)

    def body(*refs):
        src_refs, land_refs = refs[:n], refs[n:2 * n]
        send_sems, recv_sems = refs[2 * n + len(after)], refs[2 * n + len(after) + 1]
        token = refs[-1]
        for cp in _plan_copies(plan, src_refs, land_refs, send_sems, recv_sems):
            cp.start()
        token[...] = jnp.zeros_like(token)

    lands = [pltpu.with_memory_space_constraint(lax.empty(s, a.dtype), pltpu.HBM) for s, a in zip(land_shapes, srcs)]
    srcs = [pltpu.with_memory_space_constraint(a, pltpu.HBM) for a in srcs]
    outs = pl.pallas_call(
        body, name=name,
        out_shape=(pltpu.SemaphoreType.DMA((ncopies,)), pltpu.SemaphoreType.DMA((ncopies,)))
        + tuple(pltpu.HBM(a.shape, a.dtype) for a in srcs) + tuple(pltpu.HBM(a.shape, a.dtype) for a in lands)
        + (jax.ShapeDtypeStruct((8, 128), F32),),
        in_specs=[_HBM] * (2 * n) + [pl.BlockSpec(memory_space=pl.ANY)] * len(after),
        out_specs=(_SEM, _SEM) + (_HBM,) * (2 * n) + (pl.BlockSpec(memory_space=pltpu.VMEM),),
        input_output_aliases={i: 2 + i for i in range(2 * n)},
        compiler_params=pltpu.CompilerParams(has_side_effects=_EFFECT))(*srcs, *lands, *after)
    return (outs[0], outs[1], list(outs[2:2 + n]), list(outs[2 + n:2 + 2 * n])), outs[-1]


def _exchange_wait(handle, plan, name, after):
    send_sems, recv_sems, srcs, lands = handle
    n = len(srcs)

    def body(*refs):
        src_refs, land_refs = refs[:n], refs[n:2 * n]
        for cp in _plan_copies(plan, src_refs, land_refs, refs[2 * n], refs[2 * n + 1]):
            cp.wait_send()
            cp.wait_recv()

    outs = pl.pallas_call(
        body, name=name,
        out_shape=tuple(pltpu.HBM(a.shape, a.dtype) for a in srcs) + tuple(pltpu.HBM(a.shape, a.dtype) for a in lands),
        in_specs=[_HBM] * (2 * n) + [_SEM, _SEM] + [pl.BlockSpec(memory_space=pl.ANY)] * len(after),
        out_specs=(_HBM,) * (2 * n), input_output_aliases={i: i for i in range(2 * n)},
        compiler_params=pltpu.CompilerParams(has_side_effects=_EFFECT))(*srcs, *lands, send_sems, recv_sems, *after)
    return list(outs[:n]), list(outs[n:])


SMALL = ("g1", "g2", "g3", "g4", "ln_g", "ln_b", "b_sp", "w_sp")


def _small_step(loss_row, grads, ws, ms, vs):
    parts = [loss_row] + [grads[k] for k in SMALL]
    n, ns = len(parts), len(SMALL)

    def body(*refs):
        p_refs = refs[:n]
        w_refs, m_refs, v_refs = (refs[n + i * ns:n + (i + 1) * ns] for i in range(3))
        o = n + 3 * ns
        loss_ref = refs[o]
        g_out, d_out, m_out, v_out = (refs[o + 1 + i * ns:o + 1 + (i + 1) * ns] for i in range(4))
        bufs = refs[o + 1 + 4 * ns:o + 1 + 4 * ns + n]
        send_sems, recv_sems = refs[-2:]
        x, y, c = _mesh_pos()
        me, sib = (x, y, c), (x, y, 1 - c)
        chips = [(1 - x, y), (x, 1 - y), (1 - x, 1 - y)]

        def rcopy(k, s, block, to, src=None):
            dst = bufs[k].at[4 * block[0] + 2 * block[1] + block[2]]
            return pltpu.make_async_remote_copy(
                src_ref=dst if src is None else src, dst_ref=dst, send_sem=send_sems.at[k, s],
                recv_sem=recv_sems.at[k, s], device_id=to, device_id_type=MESH)

        started = []
        for k in range(n):
            bufs[k][4 * x + 2 * y + c] = p_refs[k][...]
            started.append(rcopy(k, 0, me, sib, src=p_refs[k]))
            started += [rcopy(k, 1 + j, me, (*chip, c), src=p_refs[k]) for j, chip in enumerate(chips)]
        for cp in started:
            cp.start()
        for k in range(n):
            for j, chip in enumerate(chips):
                rcopy(k, 1 + j, (*chip, c), me).wait_recv()
                started.append(rcopy(k, 4 + j, (*chip, c), sib))
                started[-1].start()
        for k in range(n):
            rcopy(k, 0, sib, me).wait_recv()
            for j, chip in enumerate(chips):
                rcopy(k, 4 + j, (*chip, 1 - c), me).wait_recv()
        for cp in started:
            cp.wait_send()
        sums = []
        for k in range(n):
            acc = bufs[k][0]
            for b in range(1, 8):
                acc = acc + bufs[k][b]
            sums.append(acc)
        loss_ref[...] = sums[0]
        for i in range(ns):
            g_out[i][...] = sums[1 + i]
            d_out[i][...], m_out[i][...], v_out[i][...] = _adam_math(w_refs[i][...], sums[1 + i], m_refs[i][...],
                                                                     v_refs[i][...])

    args = parts + [t[k] for t in (ws, ms, vs) for k in SMALL]
    shapes = [jax.ShapeDtypeStruct(loss_row.shape, F32)] + [jax.ShapeDtypeStruct(grads[k].shape, F32) for k in SMALL] * 4
    vmem = pl.BlockSpec(memory_space=pltpu.VMEM)
    outs = pl.pallas_call(
        body, name="small_step", out_shape=tuple(shapes), in_specs=[vmem] * len(args), out_specs=(vmem,) * len(shapes),
        scratch_shapes=[pltpu.VMEM((8,) + p.shape, F32) for p in parts]
        + [pltpu.SemaphoreType.DMA((n, 7)), pltpu.SemaphoreType.DMA((n, 7))],
        compiler_params=pltpu.CompilerParams(vmem_limit_bytes=VMEM_LIMIT))(*args)
    groups = [dict(zip(SMALL, outs[1 + i * ns:1 + (i + 1) * ns])) for i in range(4)]
    return (outs[0], *groups)


def _rope_tables(S):
    half = HD // 2
    inv_freq = jnp.tile(ROPE_THETA ** (-jnp.arange(half, dtype=F32) / half), 4)
    sign = jnp.tile(jnp.concatenate([-jnp.ones(half, F32), jnp.ones(half, F32)]), 2)
    ang = jnp.arange(S, dtype=F32)[:, None] * inv_freq[None, :]
    return jnp.cos(ang), jnp.sin(ang) * sign[None, :]


def _to_blocks(g, col_sharded):
    if col_sharded:
        return g.reshape(g.shape[0], 8, g.shape[1] // 8).transpose(1, 0, 2)
    return g.reshape(8, g.shape[0] // 8, g.shape[1])


def _from_blocks(t, col_sharded):
    if col_sharded:
        return t.transpose(1, 0, 2).reshape(t.shape[1], 8 * t.shape[2])
    return t.reshape(8 * t.shape[1], t.shape[2])


class _NoComm:
    def __init__(self, late_weights):
        self._late = late_weights
        self.grads = {}

    def start_token(self):
        return 0.0

    def late_weights(self, after):
        return self._late

    def rs_start(self, key, gblocks, after=()):
        self.grads[key] = gblocks
        return jnp.zeros((8, 128), F32)

    def rs_mid(self, key, after):
        return jnp.zeros((8, 128), F32)


class _FsdpComm:
    def __init__(self, late_shards, col_sharded, after, idx, block):
        self._col, self._idx, self._block, self._rs = col_sharded, idx, block, {}
        n = len(late_shards)
        self._gather, self._token = _exchange_start(
            late_shards, [(8,) + s.shape for s in late_shards], _plan_gather(n), 7 * n, "ag_late_start", (after,))

    def start_token(self):
        return self._token[0, 0]

    def late_weights(self, after):
        shards, lands = _exchange_wait(self._gather, _plan_gather(len(self._col)), "ag_late_wait", after)
        lands = [lax.dynamic_update_index_in_dim(t, s, self._block, 0) for t, s in zip(lands, shards)]
        return [_from_blocks(t, cs) for t, cs in zip(lands, self._col)]

    def rs_start(self, key, gblocks, after=()):
        n = len(gblocks)
        d2d, token = _exchange_start(gblocks, [(4,) + g.shape[1:] for g in gblocks], _plan_d2d(n), 4 * n,
                                     "rs_%s_d2d_start" % key, (self._token,) + tuple(after))
        self._rs[key] = dict(n=n, d2d=d2d)
        return token

    def rs_mid(self, key, after):
        st = self._rs[key]
        gblocks, from_sib = _exchange_wait(st["d2d"], _plan_d2d(st["n"]), "rs_%s_d2d_wait" % key, after)
        halves = [_rs_add(g, r, self._idx, "rs_add_%s_%d" % (key, k)) for k, (g, r) in enumerate(zip(gblocks, from_sib))]
        st["own"] = [own for own, _ in halves]
        sends = [send for _, send in halves]
        st["ici"], token = _exchange_start(sends, [t.shape for t in sends], _plan_ici(st["n"]), 3 * st["n"],
                                           "rs_%s_ici_start" % key, (self._token,))
        return token

    def rs_end(self, key, after):
        st = self._rs[key]
        return st["own"], _exchange_wait(st["ici"], _plan_ici(st["n"]), "rs_%s_ici_wait" % key, after)[1]


def _local_step(x, tgt, wqkv_t, wuz_t, wg_t, comm, g1, g2, g3, g4, ln_g, ln_b, w_sp, b_sp):
    S = x.shape[0]
    cos_t, sin_t = _rope_tables(S)
    b_sp_t = b_sp.T
    w_sp_t = w_sp.transpose(0, 2, 1)

    h1 = _rms_fwd(x, g1 + comm.start_token(), "rms_pre_mix")
    qkvs = [_proj_qkv(h1, wqkv_t, cos_t, sin_t, g, d, "proj_qkv_d%d" % d) for g, d in enumerate(DILS)]
    puz = _proj(h1, wuz_t, "none", "proj_uz", 1024)
    gates = _proj(h1, wg_t, "sigmoid", "proj_gates", 512)
    fwd = [_attn_fwd(t, "attn_fwd_d%d" % d) for t, d in zip(qkvs, DILS)]
    ya, *lses = _attn_combine([o for o, _ in fwd], [l for _, l in fwd])
    yg = _gmlp_fwd(puz, ln_g, ln_b, w_sp, b_sp_t)
    wba, wbg, wout, wmi, wmo = comm.late_weights(after=(ya, yg, gates))
    merged, y, x2, h3 = _merge_fwd(ya, yg, gates, wba, wbg, wout, x, g2, g3)
    a, dy2, dout, loss_row, dg4 = _mlp_fwd(h3, wmi, wmo, x2, tgt, g4)

    dpre, dx2, dy, dg3, dg2 = _mlp_bwd(dy2, a, wmo, wmi, x2, y, dout, g2, g3)
    dwmo = _mm_tn(a, dy2, "dw_mlp_out", square_a=True)
    dwmi = _mm_tn(h3, dpre, "dw_mlp_in", col_blocks=8)
    tie = comm.rs_start("mlp", [dwmi, _to_blocks(dwmo, False)])
    dgates, da, db, dyg, *rest = _outproj_bwd(dy, wout, ya, yg, gates, wba, wbg, tie)
    dyas, dsums = rest[:3], rest[3:]
    tie = comm.rs_mid("mlp", after=(dyg,))
    dpuz, dwsp, dbs, dlng, dlnb = _gmlp_bwd(puz, dyg, ln_g + tie[0, 0], ln_b, w_sp, w_sp_t, b_sp_t)
    dqkvs = [_attn_bwd(qkvs[g], dyas[g], dsums[g], lses[g], "attn_bwd_d%d" % d) for g, d in enumerate(DILS)]
    grad_x, dg1, dqkv = _inproj_bwd(dqkvs, dpuz, dgates, wqkv_t, wuz_t, wg_t, cos_t, sin_t, x, dx2, g1)
    dwin_t = jnp.concatenate([_mm_tn(t, h1, "dw_in_%d" % n) for n, t in enumerate((dqkv, dpuz, dgates))], axis=0)
    dwout = _mm_tn(merged, dy, "dw_out")
    dwba = _mm_tn(ya, da, "dw_branch_attn")
    dwbg = _mm_tn(yg, db, "dw_branch_gmlp")
    tie = comm.rs_start("win", [_to_blocks(dwin_t, False)], after=(dwout, dwba, dwbg))
    small = dict(g1=dg1, g2=dg2, g3=dg3, g4=dg4, ln_g=dlng, ln_b=dlnb, b_sp=dbs.reshape(4, CHUNK),
                 w_sp=dwsp.reshape(4 * CHUNK, CHUNK))
    mid = [_to_blocks(dwba, True), _to_blocks(dwbg, True), _to_blocks(dwout, False)]
    return loss_row + tie[:1], grad_x, small, mid


def kernel(x, norm_pre_mix, w_in, w_spatial, b_spatial, ln_v_gain, ln_v_bias, w_branch_attn, w_branch_gmlp, w_out, norm_post_mix, norm_pre_mlp, w_mlp_in, w_mlp_out, norm_post_mlp, loss_target, m_norm_pre_mix, m_w_in, m_w_spatial, m_b_spatial, m_ln_v_gain, m_ln_v_bias, m_w_branch_attn, m_w_branch_gmlp, m_w_out, m_norm_post_mix, m_norm_pre_mlp, m_w_mlp_in, m_w_mlp_out, m_norm_post_mlp, v_norm_pre_mix, v_w_in, v_w_spatial, v_b_spatial, v_ln_v_gain, v_ln_v_bias, v_w_branch_attn, v_w_branch_gmlp, v_w_out, v_norm_post_mix, v_norm_pre_mlp, v_w_mlp_in, v_w_mlp_out, v_norm_post_mlp):
    mx, my, mc = _mesh_pos()
    rel = [(0, 0), (1, 0), (0, 1), (1, 1)]
    chip_of = [2 * (mx ^ dx) + (my ^ dy) for dx, dy in rel]
    idx = jnp.stack([2 * k + mc for k in chip_of] + chip_of).astype(jnp.int32)

    w_in_t, m_w_in_t, v_w_in_t = (t[0].T for t in (w_in, m_w_in, v_w_in))
    (g_win,) = _all_gather_weights([w_in_t.astype(BF)])
    win_t = _from_blocks(g_win, False)
    wqkv_t, wuz_t, wg_t = win_t[:3 * AW], win_t[3 * AW:3 * AW + 2 * GMLP_W], win_t[3 * AW + 2 * GMLP_W:]
    late = [w_branch_attn[0], w_branch_gmlp[0], w_out[0], w_mlp_in[0], w_mlp_out[0]]
    comm = _FsdpComm([w.astype(BF) for w in late], [True, True, False, True, False], g_win, idx, 4 * mx + 2 * my + mc)

    loss_row, grad_x, small, mid = _local_step(
        x[0], loss_target[0], wqkv_t, wuz_t, wg_t, comm,
        norm_pre_mix, norm_post_mix, norm_pre_mlp, norm_post_mlp, ln_v_gain, ln_v_bias, w_spatial[0], b_spatial[0])

    flat = lambda t: t.reshape(-1, t.shape[-1])
    small_w = dict(zip(SMALL, map(flat, (norm_pre_mix, norm_post_mix, norm_pre_mlp, norm_post_mlp, ln_v_gain, ln_v_bias,
                                         b_spatial, w_spatial))))
    small_m = dict(zip(SMALL, map(flat, (m_norm_pre_mix, m_norm_post_mix, m_norm_pre_mlp, m_norm_post_mlp, m_ln_v_gain,
                                         m_ln_v_bias, m_b_spatial, m_w_spatial))))
    small_v = dict(zip(SMALL, map(flat, (v_norm_pre_mix, v_norm_post_mix, v_norm_pre_mlp, v_norm_post_mlp, v_ln_v_gain,
                                         v_ln_v_bias, v_b_spatial, v_w_spatial))))
    loss_out, sg, sd, sm, sv = _small_step(loss_row, small, small_w, small_m, small_v)
    tie = comm.rs_mid("win", after=(loss_out,))
    tie = comm.rs_start("mid", mid, after=(tie,))
    mlp_own, mlp_chips = comm.rs_end("mlp", after=(tie,))
    adam = lambda nm, own, r, w, m, v: _adam_shard(own, r, w[0], m[0], v[0], "adam_" + nm)
    upd = {
        "w_mlp_in": adam("w_mlp_in", mlp_own[0], mlp_chips[0], w_mlp_in, m_w_mlp_in, v_w_mlp_in),
        "w_mlp_out": adam("w_mlp_out", mlp_own[1], mlp_chips[1], w_mlp_out, m_w_mlp_out, v_w_mlp_out),
    }
    loss = loss_out[0, 0]
    tie = comm.rs_mid("mid", after=(upd["w_mlp_in"][0], upd["w_mlp_out"][0]))
    win_own, win_chips = comm.rs_end("win", after=(tie,))
    upd["w_in"] = tuple(t.T for t in _adam_shard(win_own[0], win_chips[0], w_in_t, m_w_in_t, v_w_in_t, "adam_w_in"))
    mid_own, mid_chips = comm.rs_end("mid", after=(upd["w_in"][0],))
    upd["w_branch_attn"] = adam("w_branch_attn", mid_own[0], mid_chips[0], w_branch_attn, m_w_branch_attn, v_w_branch_attn)
    upd["w_branch_gmlp"] = adam("w_branch_gmlp", mid_own[1], mid_chips[1], w_branch_gmlp, m_w_branch_gmlp, v_w_branch_gmlp)
    upd["w_out"] = adam("w_out", mid_own[2], mid_chips[2], w_out, m_w_out, v_w_out)
    order = ["g1", "w_in", "w_sp", "b_sp", "ln_g", "ln_b", "w_branch_attn", "w_branch_gmlp", "w_out", "g2", "g3",
             "w_mlp_in", "w_mlp_out", "g4"]
    small_shape = dict(g1=norm_pre_mix.shape, g2=norm_post_mix.shape, g3=norm_pre_mlp.shape, g4=norm_post_mlp.shape,
                       ln_g=ln_v_gain.shape, ln_b=ln_v_bias.shape, b_sp=b_spatial.shape, w_sp=w_spatial.shape)

    def pick(which):
        return [upd[nm][which][None] if nm in upd else (sg, sd, sm, sv)[which][nm].reshape(small_shape[nm])
                for nm in order]

    return (loss, grad_x[None], *pick(0), *pick(1), *pick(2), *pick(3))
```

```python
import functools
import math

import jax
import jax.numpy as jnp
from jax import lax
from jax.experimental import pallas as pl
from jax.experimental.pallas import tpu as pltpu

D = 1024
HD = 64
NSLOT = 4
GW = NSLOT * HD
DILS = (1, 4, 16)
QB = 128
ATTN_NSUB = 4
MLP_TM_FWD = 512
MLP_TM_BWD = 256
AW = 3 * GW
GMLP_W = 512
CHUNK = 128
DFF = 4096
EPS = 1e-6
ROPE_THETA = 10000.0
SCALE = HD ** -0.5
NEG = -1e30

ADAM_LR = 0.001
ADAM_B1 = 0.9
ADAM_B2 = 0.999
ADAM_EPS = 1e-08
ADAM_WD = 0.01
ADAM_STEP = 10

BF = jnp.bfloat16
F32 = jnp.float32
MESH = pl.DeviceIdType.MESH
VMEM_LIMIT = 56 * 1024 * 1024


def _cp(sem):
    return pltpu.CompilerParams(dimension_semantics=sem, vmem_limit_bytes=VMEM_LIMIT)


def _dot(a, b):
    return jnp.dot(a, b, preferred_element_type=F32)


def _dot_nt(a, b):
    return lax.dot_general(a, b, (((1,), (1,)), ((), ())), preferred_element_type=F32)


def _dot_tn(a, b):
    return lax.dot_general(a, b, (((0,), (0,)), ((), ())), preferred_element_type=F32)


def _gelu(x):
    return jax.nn.gelu(x, approximate=True)


def _gelu_grad(x):
    k = math.sqrt(2.0 / math.pi)
    t = jnp.tanh(k * (x + 0.044715 * x * x * x))
    return 0.5 * (1.0 + t) + 0.5 * x * (1.0 - t * t) * (k * (1.0 + 3.0 * 0.044715 * x * x))


def _swap_halves(t):
    w = t.shape[1]
    lane = lax.broadcasted_iota(jnp.int32, t.shape, 1)
    first = (lane & (HD - 1)) < (HD // 2)
    return jnp.where(first, pltpu.roll(t, w - HD // 2, 1), pltpu.roll(t, HD // 2, 1))


def _head_mask(hh):
    lane = lax.broadcasted_iota(jnp.int32, (1, GW), 1)
    return jnp.logical_and(lane >= hh * HD, lane < (hh + 1) * HD)


def _rms_stats(xf):
    return lax.rsqrt(jnp.mean(xf * xf, axis=-1, keepdims=True) + EPS)


def _rms_bwd(xf, r, gain, dout):
    n = xf * r
    t = dout * gain
    dx = r * (t - n * jnp.mean(t * n, axis=-1, keepdims=True))
    return dx, jnp.sum(dout * n, axis=0, keepdims=True)


def _rms_fwd(x, gain, name):
    S = x.shape[0]
    tm = min(512, S)

    def body(x_ref, g_ref, h_ref):
        xf = x_ref[...]
        h_ref[...] = (xf * _rms_stats(xf) * g_ref[...]).astype(BF)

    return pl.pallas_call(
        body, name=name, out_shape=jax.ShapeDtypeStruct((S, D), BF), grid=(S // tm,),
        in_specs=[pl.BlockSpec((tm, D), lambda i: (i, 0)), pl.BlockSpec((1, D), lambda i: (0, 0))],
        out_specs=pl.BlockSpec((tm, D), lambda i: (i, 0)), compiler_params=_cp(("parallel",)))(x, gain)


def _proj(h, w_t, epi, name, tm):
    S, K = h.shape
    N = w_t.shape[0]
    tm = min(tm, S)

    def body(h_ref, w_ref, o_ref):
        y = _dot_nt(h_ref[...], w_ref[...])
        if epi == "sigmoid":
            y = jax.nn.sigmoid(y)
        o_ref[...] = y.astype(BF)

    return pl.pallas_call(
        body, name=name, out_shape=jax.ShapeDtypeStruct((S, N), BF), grid=(S // tm,),
        in_specs=[pl.BlockSpec((tm, K), lambda i: (i, 0)), _resident((N, K))],
        out_specs=pl.BlockSpec((tm, N), lambda i: (i, 0)),
        compiler_params=_cp(("parallel",)))(h, w_t)


def _dilate_store(val, scr, o_ref, lead, d):
    rows = val.shape[0]
    if d == 1:
        o_ref[lead + (0,)] = val.astype(o_ref.dtype)
        return
    for hf in range(2):
        scr[hf, pl.ds(0, rows), :] = val[:, hf * 128:(hf + 1) * 128]
    for r in range(d):
        for hf in range(2):
            o_ref[lead + (r, slice(None), slice(hf * 128, (hf + 1) * 128))] = (
                scr[hf, pl.ds(r, rows // d, stride=d), :].astype(o_ref.dtype))


def _undilate_load(i_ref, lead, d, scr, rows):
    if d == 1:
        return i_ref[lead + (0,)].astype(F32)
    for r in range(d):
        for hf in range(2):
            scr[hf, pl.ds(r, rows // d, stride=d), :] = (
                i_ref[lead + (r, slice(None), slice(hf * 128, (hf + 1) * 128))].astype(F32))
    return jnp.concatenate([scr[0, pl.ds(0, rows), :], scr[1, pl.ds(0, rows), :]], axis=1)


def _rope_fwd(y, c_ref, s_ref):
    cosv = jnp.concatenate([c_ref[...]] * 2, axis=1)
    sinv = jnp.concatenate([s_ref[...]] * 2, axis=1)
    return y * cosv + _swap_halves(y) * sinv


def _rope_bwd(dy, c_ref, s_ref):
    cosv = jnp.concatenate([c_ref[...]] * 2, axis=1)
    sinv = jnp.concatenate([s_ref[...]] * 2, axis=1)
    return dy * cosv + _swap_halves(dy * sinv)


def _proj_qkv(h, wqkv_t, cos_t, sin_t, g, d, name):
    S, K = h.shape
    tm = min(2048, S)

    def body(h_ref, w_ref, c_ref, s_ref, o_ref, scr):
        j = pl.program_id(1)
        y = _dot_nt(h_ref[...], w_ref[...])
        y = jnp.where(j < 2, _rope_fwd(y, c_ref, s_ref), y)
        _dilate_store(y, scr, o_ref, (), d)

    return pl.pallas_call(
        body, name=name, out_shape=jax.ShapeDtypeStruct((3, d, S // d, GW), BF), grid=(S // tm, 3),
        in_specs=[pl.BlockSpec((tm, K), lambda i, j: (i, 0)), pl.BlockSpec((GW, K), lambda i, j: (3 * j + g, 0)),
                  pl.BlockSpec((tm, 128), lambda i, j: (i, 0)), pl.BlockSpec((tm, 128), lambda i, j: (i, 0))],
        out_specs=pl.BlockSpec((None, d, tm // d, GW), lambda i, j: (j, 0, i, 0)),
        scratch_shapes=[pltpu.VMEM((2, tm, 128), F32)],
        compiler_params=_cp(("parallel", "arbitrary")))(h, wqkv_t, cos_t, sin_t)


def _band_masks(first_step):
    row = lax.broadcasted_iota(jnp.int32, (QB, 2 * QB), 0)
    col = lax.broadcasted_iota(jnp.int32, (QB, 2 * QB), 1)
    band = jnp.logical_and(col >= row, col <= row + QB)
    return band, jnp.logical_and(band, jnp.logical_or(col >= QB, jnp.logical_not(first_step)))


def _attn_fwd(qkv, name):
    _, d, L, _ = qkv.shape
    nsub = min(ATTN_NSUB, L // QB)
    R = nsub * QB
    nsteps = L // R

    def body(q_ref, kp_ref, kc_ref, vp_ref, vc_ref, o_ref, lse_ref):
        i = pl.program_id(1)
        band, band_first = _band_masks(i == 0)
        kfull = jnp.concatenate([kp_ref[...], kc_ref[...]], axis=0)
        vfull = jnp.concatenate([vp_ref[...], vc_ref[...]], axis=0)
        chains = [(sb, hh) for sb in range(nsub) for hh in range(NSLOT)]
        win = lambda t, sb: t[sb * QB:(sb + 2) * QB]
        scores = []
        for sb, hh in chains:
            qh = jnp.where(_head_mask(hh), q_ref[sb * QB:(sb + 1) * QB, :], 0)
            scores.append(_dot_nt(qh, win(kfull, sb)))
        soft = []
        for (sb, hh), sc in zip(chains, scores):
            sc = jnp.where(band_first if sb == 0 else band, sc * SCALE, NEG)
            m = jnp.max(sc, axis=1, keepdims=True)
            p = jnp.exp(sc - m)
            den = jnp.sum(p, axis=1, keepdims=True)
            soft.append((p.astype(BF), den, m + jnp.log(den)))
        accs = [_dot(p, win(vfull, sb)) for (sb, hh), (p, _, _) in zip(chains, soft)]
        for sb in range(nsub):
            o = jnp.zeros((QB, GW), F32)
            lse = jnp.zeros((QB, GW), F32)
            for hh in range(NSLOT):
                hm = _head_mask(hh)
                _, den, lrow = soft[sb * NSLOT + hh]
                o = o + jnp.where(hm, accs[sb * NSLOT + hh] / den, 0.0)
                lse = lse + jnp.where(hm, lrow, 0.0)
            o_ref[sb * QB:(sb + 1) * QB, :] = o
            lse_ref[sb * QB:(sb + 1) * QB, :] = lse

    prev = lambda i: jnp.maximum(i * nsub - 1, 0)
    cur = lambda t: pl.BlockSpec((None, None, R, GW), lambda r, i: (t, r, i, 0))
    prv = lambda t: pl.BlockSpec((None, None, QB, GW), lambda r, i: (t, r, prev(i), 0))
    out = pl.BlockSpec((None, R, GW), lambda r, i: (r, i, 0))
    return pl.pallas_call(
        body, name=name, grid=(d, nsteps),
        out_shape=(jax.ShapeDtypeStruct((d, L, GW), F32), jax.ShapeDtypeStruct((d, L, GW), F32)),
        in_specs=[cur(0), prv(1), cur(1), prv(2), cur(2)],
        out_specs=(out, out), compiler_params=_cp(("parallel", "arbitrary")))(qkv, qkv, qkv, qkv, qkv)


def _attn_combine(os_, lses):
    S = os_[0].shape[1]
    tm = min(1024, S)

    def body(o0, o1, o2, l0, l1, l2, y_ref, j0, j1, j2, scr):
        os_nat = [_undilate_load(o, (), d, scr, tm) for o, d in zip((o0, o1, o2), DILS)]
        a, b, c = [_undilate_load(l, (), d, scr, tm) for l, d in zip((l0, l1, l2), DILS)]
        m = jnp.maximum(jnp.maximum(a, b), c)
        wa, wb, wc = jnp.exp(a - m), jnp.exp(b - m), jnp.exp(c - m)
        den = wa + wb + wc
        y_ref[...] = ((wa * os_nat[0] + wb * os_nat[1] + wc * os_nat[2]) / den).astype(BF)
        lse = m + jnp.log(den)
        for j_ref, d in zip((j0, j1, j2), DILS):
            _dilate_store(lse, scr, j_ref, (), d)

    dil = lambda d: pl.BlockSpec((d, tm // d, GW), lambda i: (0, i, 0))
    dshape = lambda d: jax.ShapeDtypeStruct((d, S // d, GW), F32)
    return pl.pallas_call(
        body, name="attn_combine", grid=(S // tm,),
        out_shape=(jax.ShapeDtypeStruct((S, GW), BF),) + tuple(dshape(d) for d in DILS),
        in_specs=[dil(d) for d in DILS] * 2,
        out_specs=(pl.BlockSpec((tm, GW), lambda i: (i, 0)),) + tuple(dil(d) for d in DILS),
        scratch_shapes=[pltpu.VMEM((2, tm, 128), F32)],
        compiler_params=_cp(("parallel",)))(*os_, *lses)


def _tril(upper=False):
    row = lax.broadcasted_iota(jnp.int32, (CHUNK, CHUNK), 0)
    col = lax.broadcasted_iota(jnp.int32, (CHUNK, CHUNK), 1)
    return row <= col if upper else col <= row


def _ln_fwd(z, gain, bias):
    mu = jnp.mean(z, axis=-1, keepdims=True)
    zc = z - mu
    rstd = lax.rsqrt(jnp.mean(zc * zc, axis=-1, keepdims=True) + EPS)
    zhat = zc * rstd
    return zhat, rstd, zhat * gain + bias


def _gmlp_fwd(puz, ln_g, ln_b, w_sp, b_sp_t):
    S = puz.shape[0]
    tm = min(512, S)
    nch = tm // CHUNK

    def body(p_ref, g_ref, b_ref, w_ref, bt_ref, o_ref):
        tril = _tril()
        ws = [jnp.where(tril, w_ref[gg], 0.0).astype(BF) for gg in range(4)]
        for ch in range(nch):
            rows = slice(ch * CHUNK, (ch + 1) * CHUNK)
            z = _gelu(p_ref[rows, GMLP_W:].astype(F32))
            _, _, zn = _ln_fwd(z, g_ref[...], b_ref[...])
            zn = zn.astype(BF)
            for gg in range(4):
                cols = slice(gg * CHUNK, (gg + 1) * CHUNK)
                sz = _dot(ws[gg], zn[:, cols]) + bt_ref[:, gg:gg + 1]
                u = _gelu(p_ref[rows, cols].astype(F32))
                o_ref[rows, cols] = (u * sz).astype(BF)

    return pl.pallas_call(
        body, name="gmlp_fwd", out_shape=jax.ShapeDtypeStruct((S, GMLP_W), BF), grid=(S // tm,),
        in_specs=[pl.BlockSpec((tm, 2 * GMLP_W), lambda i: (i, 0)),
                  pl.BlockSpec((1, GMLP_W), lambda i: (0, 0)), pl.BlockSpec((1, GMLP_W), lambda i: (0, 0)),
                  pl.BlockSpec((4, CHUNK, CHUNK), lambda i: (0, 0, 0)), pl.BlockSpec((CHUNK, 4), lambda i: (0, 0))],
        out_specs=pl.BlockSpec((tm, GMLP_W), lambda i: (i, 0)),
        compiler_params=_cp(("parallel",)))(puz, ln_g, ln_b, w_sp, b_sp_t)


def _merge_fwd(ya, yg, gates, wba, wbg, wout, x, g2, g3):
    S = x.shape[0]
    tm = min(512, S)

    def body(ya_ref, yg_ref, gt_ref, wba_ref, wbg_ref, wo_ref, x_ref, g2_ref, g3_ref,
             mg_ref, y_ref, x2_ref, h3_ref):
        a = _dot(ya_ref[...], wba_ref[...])
        b = _dot(yg_ref[...], wbg_ref[...])
        merged = (gt_ref[:, :D].astype(F32) * a + gt_ref[:, D:].astype(F32) * b).astype(BF)
        mg_ref[...] = merged
        y = _dot(merged, wo_ref[...])
        y_ref[...] = y
        x2 = x_ref[...] + y * _rms_stats(y) * g2_ref[...]
        x2_ref[...] = x2
        h3_ref[...] = (x2 * _rms_stats(x2) * g3_ref[...]).astype(BF)

    row = lambda w: pl.BlockSpec((tm, w), lambda i: (i, 0))
    full = lambda s: pl.BlockSpec(s, lambda i: (0, 0))
    return pl.pallas_call(
        body, name="merge_fwd", grid=(S // tm,),
        out_shape=(jax.ShapeDtypeStruct((S, D), BF), jax.ShapeDtypeStruct((S, D), F32), jax.ShapeDtypeStruct((S, D), F32),
                   jax.ShapeDtypeStruct((S, D), BF)),
        in_specs=[row(GW), row(GMLP_W), row(2 * D), full((GW, D)), full((GMLP_W, D)), full((D, D)), row(D),
                  full((1, D)), full((1, D))],
        out_specs=(row(D), row(D), row(D), row(D)),
        compiler_params=_cp(("parallel",)))(ya, yg, gates, wba, wbg, wout, x, g2, g3)


def _resident(shape):
    return pl.BlockSpec(shape, lambda i: (0,) * len(shape), pipeline_mode=pl.Buffered(1))


def _mlp_fwd(h3, wmi, wmo, x2, tgt, g4):
    S = x2.shape[0]
    tm = min(MLP_TM_FWD, S)

    def body(h_ref, wi_ref, wo_ref, x2_ref, t_ref, g4_ref, a_ref, dy2_ref, dout_ref, loss_ref, dg4_ref):
        @pl.when(pl.program_id(0) == 0)
        def _():
            loss_ref[...] = jnp.zeros_like(loss_ref)
            dg4_ref[...] = jnp.zeros_like(dg4_ref)

        a = jnp.maximum(_dot(h_ref[...], wi_ref[...]), 0.0)
        a_ref[...] = a.astype(BF)
        y2 = _dot((a * a).astype(BF), wo_ref[...])
        r = _rms_stats(y2)
        out = x2_ref[...] + y2 * r * g4_ref[...]
        err = out - t_ref[...]
        tot = jnp.sum(jnp.sum(err * err, axis=1, keepdims=True), axis=0, keepdims=True) * (0.5 / D)
        lane = lax.broadcasted_iota(jnp.int32, (1, 128), 1)
        loss_ref[...] += jnp.where(lane == 0, tot, 0.0)
        dout = err * (1.0 / D)
        dout_ref[...] = dout
        dy2, dg = _rms_bwd(y2, r, g4_ref[...], dout)
        dy2_ref[...] = dy2.astype(BF)
        dg4_ref[...] += dg

    row = pl.BlockSpec((tm, D), lambda i: (i, 0))
    return pl.pallas_call(
        body, name="mlp_fwd", grid=(S // tm,),
        out_shape=(jax.ShapeDtypeStruct((S, DFF), BF), jax.ShapeDtypeStruct((S, D), BF), jax.ShapeDtypeStruct((S, D), F32),
                   jax.ShapeDtypeStruct((1, 128), F32), jax.ShapeDtypeStruct((1, D), F32)),
        in_specs=[row, _resident((D, DFF)), _resident((DFF, D)), row, row, pl.BlockSpec((1, D), lambda i: (0, 0))],
        out_specs=(pl.BlockSpec((tm, DFF), lambda i: (i, 0)), row, row,
                   pl.BlockSpec((1, 128), lambda i: (0, 0)), pl.BlockSpec((1, D), lambda i: (0, 0))),
        compiler_params=_cp(("arbitrary",)))(h3, wmi, wmo, x2, tgt, g4)


def _mlp_bwd(dy2, a, wmo, wmi, x2, y, dout, g2, g3):
    S = x2.shape[0]
    tm = min(MLP_TM_BWD, S)

    def body(dy2_ref, a_ref, wo_ref, wi_ref, x2_ref, y_ref, dout_ref, g2_ref, g3_ref,
             dpre_ref, dx2_ref, dy_ref, dg3_ref, dg2_ref):
        @pl.when(pl.program_id(0) == 0)
        def _():
            dg3_ref[...] = jnp.zeros_like(dg3_ref)
            dg2_ref[...] = jnp.zeros_like(dg2_ref)

        da2 = _dot_nt(dy2_ref[...], wo_ref[...])
        dpre = (2.0 * a_ref[...].astype(F32) * da2).astype(BF)
        dpre_ref[...] = dpre
        dh3 = _dot_nt(dpre, wi_ref[...])
        x2 = x2_ref[...]
        dx3, dg3 = _rms_bwd(x2, _rms_stats(x2), g3_ref[...], dh3)
        dx2 = dout_ref[...] + dx3
        dx2_ref[...] = dx2
        dg3_ref[...] += dg3
        yv = y_ref[...]
        dy, dg2 = _rms_bwd(yv, _rms_stats(yv), g2_ref[...], dx2)
        dy_ref[...] = dy.astype(BF)
        dg2_ref[...] += dg2

    row = pl.BlockSpec((tm, D), lambda i: (i, 0))
    wide = pl.BlockSpec((tm, DFF), lambda i: (i, 0))
    vec = pl.BlockSpec((1, D), lambda i: (0, 0))
    return pl.pallas_call(
        body, name="mlp_bwd", grid=(S // tm,),
        out_shape=(jax.ShapeDtypeStruct((S, DFF), BF), jax.ShapeDtypeStruct((S, D), F32), jax.ShapeDtypeStruct((S, D), BF),
                   jax.ShapeDtypeStruct((1, D), F32), jax.ShapeDtypeStruct((1, D), F32)),
        in_specs=[row, wide, _resident((DFF, D)), _resident((D, DFF)), row, row, row, vec, vec],
        out_specs=(wide, row, row, vec, vec),
        compiler_params=_cp(("arbitrary",)))(dy2, a, wmo, wmi, x2, y, dout, g2, g3)


def _mm_tn(a, b, name, square_a=False, tm=1024, tn=1024, tk=2048, tie=None, col_blocks=None):
    S, M = a.shape
    N = b.shape[1]
    tk = min(tk, S)
    tm = max(t for t in range(128, min(tm, M) + 1, 128) if M % t == 0)
    tn = max(t for t in range(128, min(tn, N) + 1, 128) if N % t == 0)
    cb = N // col_blocks if col_blocks else tn
    assert M % tm == 0 and S % tk == 0 and tn % cb == 0
    nk = S // tk
    ties = () if tie is None else (tie,)

    def body(a_ref, b_ref, *rest):
        o_ref = rest[-1]
        k = pl.program_id(2)
        av = a_ref[...]
        if square_a:
            av = av * av
        part = _dot_tn(av, b_ref[...])
        if col_blocks:
            part = jnp.stack([part[:, t * cb:(t + 1) * cb] for t in range(tn // cb)])

        @pl.when(k == 0)
        def _():
            o_ref[...] = part

        @pl.when(k > 0)
        def _():
            o_ref[...] += part

    if col_blocks:
        out_shape, out_spec = (col_blocks, M, cb), pl.BlockSpec((tn // cb, tm, cb), lambda i, j, k: (j, i, 0))
    else:
        out_shape, out_spec = (M, N), pl.BlockSpec((tm, tn), lambda i, j, k: (i, j))
    return pl.pallas_call(
        body, name=name, out_shape=jax.ShapeDtypeStruct(out_shape, F32), grid=(M // tm, N // tn, nk),
        in_specs=[pl.BlockSpec((tk, tm), lambda i, j, k: (k, i)), pl.BlockSpec((tk, tn), lambda i, j, k: (k, j))]
        + [pl.BlockSpec(memory_space=pl.ANY)] * len(ties),
        out_specs=out_spec,
        compiler_params=_cp(("parallel", "parallel", "arbitrary")))(a, b, *ties)


def _outproj_bwd(dy, wout, ya, yg, gates, wba, wbg, tie):
    S = dy.shape[0]
    tm = min(512, S)

    def body(dy_ref, wo_ref, ya_ref, yg_ref, gt_ref, wba_ref, wbg_ref, tie_ref,
             dgt_ref, da_ref, db_ref, dyg_ref, e0, e1, e2, s0, s1, s2, scr):
        dm = _dot_nt(dy_ref[...], wo_ref[...])
        ga, gb = gt_ref[:, :D].astype(F32), gt_ref[:, D:].astype(F32)
        dgt_ref[:, :D] = (dm * _dot(ya_ref[...], wba_ref[...]) * ga * (1.0 - ga)).astype(BF)
        dgt_ref[:, D:] = (dm * _dot(yg_ref[...], wbg_ref[...]) * gb * (1.0 - gb)).astype(BF)
        da = (dm * ga).astype(BF)
        db = (dm * gb).astype(BF)
        da_ref[...] = da
        db_ref[...] = db
        dyg_ref[...] = _dot_nt(db, wbg_ref[...]).astype(BF)
        dya = _dot_nt(da, wba_ref[...]).astype(BF).astype(F32)
        dyy = dya * ya_ref[...].astype(F32)
        dsum = jnp.zeros((tm, GW), F32)
        for hh in range(NSLOT):
            hm = _head_mask(hh)
            dsum = dsum + jnp.where(hm, jnp.sum(jnp.where(hm, dyy, 0.0), axis=1, keepdims=True), 0.0)
        for e_ref, s_ref, d in zip((e0, e1, e2), (s0, s1, s2), DILS):
            _dilate_store(dya, scr, e_ref, (), d)
            _dilate_store(dsum, scr, s_ref, (), d)

    row = lambda w: pl.BlockSpec((tm, w), lambda i: (i, 0))
    full = lambda s: pl.BlockSpec(s, lambda i: (0, 0))
    dil = lambda d: pl.BlockSpec((d, tm // d, GW), lambda i: (0, i, 0))
    dshape = lambda d, t: jax.ShapeDtypeStruct((d, S // d, GW), t)
    return pl.pallas_call(
        body, name="outproj_bwd", grid=(S // tm,),
        out_shape=(jax.ShapeDtypeStruct((S, 2 * D), BF), jax.ShapeDtypeStruct((S, D), BF), jax.ShapeDtypeStruct((S, D), BF),
                   jax.ShapeDtypeStruct((S, GMLP_W), BF)) + tuple(dshape(d, BF) for d in DILS)
        + tuple(dshape(d, F32) for d in DILS),
        in_specs=[row(D), full((D, D)), row(GW), row(GMLP_W), row(2 * D), full((GW, D)), full((GMLP_W, D)),
                  pl.BlockSpec(memory_space=pl.ANY)],
        out_specs=(row(2 * D), row(D), row(D), row(GMLP_W)) + tuple(dil(d) for d in DILS) * 2,
        scratch_shapes=[pltpu.VMEM((2, tm, 128), F32)],
        compiler_params=_cp(("parallel",)))(dy, wout, ya, yg, gates, wba, wbg, tie)


def _gmlp_bwd(puz, dyg, ln_g, ln_b, w_sp, w_sp_t, b_sp_t):
    S = puz.shape[0]
    tm = min(512, S)
    nch = tm // CHUNK

    def body(p_ref, dy_ref, g_ref, b_ref, w_ref, wt_ref, bt_ref,
             dp_ref, dw_ref, dbs_ref, dg_ref, dbias_ref, dbacc_ref):
        i = pl.program_id(0)

        @pl.when(i == 0)
        def _():
            dw_ref[...] = jnp.zeros_like(dw_ref)
            dbacc_ref[...] = jnp.zeros_like(dbacc_ref)
            dg_ref[...] = jnp.zeros_like(dg_ref)
            dbias_ref[...] = jnp.zeros_like(dbias_ref)

        tril = _tril()
        ws = [jnp.where(tril, w_ref[gg], 0.0).astype(BF) for gg in range(4)]
        triu = _tril(upper=True)
        wts = [jnp.where(triu, wt_ref[gg], 0.0).astype(BF) for gg in range(4)]
        gain = g_ref[...]
        for ch in range(nch):
            rows = slice(ch * CHUNK, (ch + 1) * CHUNK)
            pz = p_ref[rows, GMLP_W:].astype(F32)
            z = _gelu(pz)
            zhat, rstd, zn = _ln_fwd(z, gain, b_ref[...])
            znb = zn.astype(BF)
            dzn_parts = []
            for gg in range(4):
                cols = slice(gg * CHUNK, (gg + 1) * CHUNK)
                pu = p_ref[rows, cols].astype(F32)
                u = _gelu(pu)
                sz = _dot(ws[gg], znb[:, cols]) + bt_ref[:, gg:gg + 1]
                dyv = dy_ref[rows, cols].astype(F32)
                dp_ref[rows, cols] = (dyv * sz * _gelu_grad(pu)).astype(BF)
                dsz = dyv * u
                dbacc_ref[gg] += dsz
                dszb = dsz.astype(BF)
                dw_ref[gg] += _dot_nt(dszb, znb[:, cols])
                dzn_parts.append(_dot(wts[gg], dszb))
            dzn = jnp.concatenate(dzn_parts, axis=1)
            dg_ref[...] += jnp.sum(dzn * zhat, axis=0, keepdims=True)
            dbias_ref[...] += jnp.sum(dzn, axis=0, keepdims=True)
            dzh = dzn * gain
            dz = rstd * (dzh - jnp.mean(dzh, axis=-1, keepdims=True)
                         - zhat * jnp.mean(dzh * zhat, axis=-1, keepdims=True))
            dp_ref[rows, GMLP_W:] = (dz * _gelu_grad(pz)).astype(BF)

        @pl.when(i == pl.num_programs(0) - 1)
        def _():
            for gg in range(4):
                dw_ref[gg] = jnp.where(tril, dw_ref[gg], 0.0)
                dbs_ref[gg] = jnp.sum(dbacc_ref[gg], axis=1, keepdims=True)

    full2 = lambda s: pl.BlockSpec(s, lambda i: (0, 0))
    full3 = lambda s: pl.BlockSpec(s, lambda i: (0, 0, 0))
    return pl.pallas_call(
        body, name="gmlp_bwd", grid=(S // tm,),
        out_shape=(jax.ShapeDtypeStruct((S, 2 * GMLP_W), BF), jax.ShapeDtypeStruct((4, CHUNK, CHUNK), F32),
                   jax.ShapeDtypeStruct((4, CHUNK, 1), F32), jax.ShapeDtypeStruct((1, GMLP_W), F32),
                   jax.ShapeDtypeStruct((1, GMLP_W), F32)),
        in_specs=[pl.BlockSpec((tm, 2 * GMLP_W), lambda i: (i, 0)), pl.BlockSpec((tm, GMLP_W), lambda i: (i, 0)),
                  full2((1, GMLP_W)), full2((1, GMLP_W)), full3((4, CHUNK, CHUNK)), full3((4, CHUNK, CHUNK)),
                  full2((CHUNK, 4))],
        out_specs=(pl.BlockSpec((tm, 2 * GMLP_W), lambda i: (i, 0)), full3((4, CHUNK, CHUNK)), full3((4, CHUNK, 1)),
                   full2((1, GMLP_W)), full2((1, GMLP_W))),
        scratch_shapes=[pltpu.VMEM((4, CHUNK, CHUNK), F32)],
        compiler_params=_cp(("arbitrary",)))(puz, dyg, ln_g, ln_b, w_sp, w_sp_t, b_sp_t)


def _attn_bwd(qkv, dya, dsums, lse, name):
    _, d, L, _ = qkv.shape
    nsub = min(ATTN_NSUB, L // QB)
    R = nsub * QB
    nsteps = L // R

    def body(q_ref, qn_ref, kp_ref, kc_ref, vp_ref, vc_ref, dy_ref, dyn_ref, e_ref, en_ref, l_ref, ln_ref, o_ref):
        i = pl.program_id(1)
        band, band_first = _band_masks(i == 0)
        row = lax.broadcasted_iota(jnp.int32, (QB, QB), 0)
        col = lax.broadcasted_iota(jnp.int32, (QB, QB), 1)
        mask_next = jnp.logical_and(col >= row, i < nsteps - 1)
        kc, vc = kc_ref[...], vc_ref[...]
        kfull = jnp.concatenate([kp_ref[...], kc], axis=0)
        vfull = jnp.concatenate([vp_ref[...], vc], axis=0)
        k_last, v_last = kc[(nsub - 1) * QB:], vc[(nsub - 1) * QB:]
        q_ext = jnp.concatenate([q_ref[...], qn_ref[...]], axis=0)
        dy_ext = jnp.concatenate([dy_ref[...], dyn_ref[...]], axis=0)
        esum, esum_n, lse, lse_n = e_ref[...], en_ref[...], l_ref[...], ln_ref[...]
        win = lambda t, sb: t[sb * QB:(sb + 2) * QB]
        blk = lambda t, sb: t[sb * QB:(sb + 1) * QB]
        hms = [_head_mask(hh) for hh in range(NSLOT)]
        q_hs = [jnp.where(hm, q_ext, 0) for hm in hms]
        dy_hs = [jnp.where(hm, dy_ext, 0) for hm in hms]
        raw = []
        for hh in range(NSLOT):
            tiles = [(_dot_nt(blk(q_hs[hh], sb), win(kfull, sb)), _dot_nt(blk(dy_hs[hh], sb), win(vfull, sb)))
                     for sb in range(nsub)]
            tiles.append((_dot_nt(blk(q_hs[hh], nsub), k_last), _dot_nt(blk(dy_hs[hh], nsub), v_last)))
            raw.append(tiles)
        ps, dss = [], []
        for hh in range(NSLOT):
            rowstat = lambda t: jnp.max(jnp.where(hms[hh], t, -jnp.inf), axis=1, keepdims=True)
            p_h, ds_h = [], []
            for sb in range(nsub + 1):
                sc, dp = raw[hh][sb]
                if sb < nsub:
                    msk, lrow, erow = (band_first if sb == 0 else band), rowstat(blk(lse, sb)), rowstat(blk(esum, sb))
                else:
                    msk, lrow, erow = mask_next, rowstat(lse_n), rowstat(esum_n)
                p = jnp.where(msk, jnp.exp(sc * SCALE - lrow), 0.0)
                p_h.append(p.astype(BF))
                ds_h.append((p * (dp - erow)).astype(BF))
            ps.append(p_h)
            dss.append(ds_h)
        dq = [jnp.zeros((QB, GW), F32) for _ in range(nsub)]
        dk = [jnp.zeros((QB, GW), F32) for _ in range(nsub)]
        dv = [jnp.zeros((QB, GW), F32) for _ in range(nsub)]
        for hh in range(NSLOT):
            for sb in range(nsub):
                dq[sb] = dq[sb] + jnp.where(hms[hh], _dot(dss[hh][sb], win(kfull, sb)), 0.0)
                nxt = lambda t: t[sb + 1][:, :QB] if sb + 1 < nsub else t[nsub]
                dk[sb] = dk[sb] + _dot_tn(jnp.concatenate([dss[hh][sb][:, QB:], nxt(dss[hh])], axis=0), win(q_hs[hh], sb))
                dv[sb] = dv[sb] + _dot_tn(jnp.concatenate([ps[hh][sb][:, QB:], nxt(ps[hh])], axis=0), win(dy_hs[hh], sb))
        for sb in range(nsub):
            rows = slice(sb * QB, (sb + 1) * QB)
            o_ref[0, rows, :] = (dq[sb] * SCALE).astype(BF)
            o_ref[1, rows, :] = (dk[sb] * SCALE).astype(BF)
            o_ref[2, rows, :] = dv[sb].astype(BF)

    prev = lambda i: jnp.maximum(i * nsub - 1, 0)
    nxt = lambda i: jnp.minimum((i + 1) * nsub, L // QB - 1)
    cur4 = lambda t: pl.BlockSpec((None, None, R, GW), lambda r, i: (t, r, i, 0))
    prv4 = lambda t: pl.BlockSpec((None, None, QB, GW), lambda r, i: (t, r, prev(i), 0))
    nxt4 = lambda t: pl.BlockSpec((None, None, QB, GW), lambda r, i: (t, r, nxt(i), 0))
    cur3 = pl.BlockSpec((None, R, GW), lambda r, i: (r, i, 0))
    nxt3 = pl.BlockSpec((None, QB, GW), lambda r, i: (r, nxt(i), 0))
    return pl.pallas_call(
        body, name=name, grid=(d, nsteps), out_shape=jax.ShapeDtypeStruct((3, d, L, GW), BF),
        in_specs=[cur4(0), nxt4(0), prv4(1), cur4(1), prv4(2), cur4(2), cur3, nxt3, cur3, nxt3, cur3, nxt3],
        out_specs=pl.BlockSpec((3, None, R, GW), lambda r, i: (0, r, i, 0)),
        compiler_params=_cp(("parallel", "arbitrary")))(qkv, qkv, qkv, qkv, qkv, qkv, dya, dya, dsums, dsums, lse, lse)


def _inproj_bwd(dqkvs, dpuz, dgates, wqkv_t, wuz_t, wg_t, cos_t, sin_t, x, dx2, g1):
    S = x.shape[0]
    tm = min(512, S)

    def body(d0_ref, d1_ref, d2_ref, dp_ref, dg_ref, wqkv_ref, wuz_ref, wg_ref, c_ref, s_ref,
             x_ref, dx2_ref, g1_ref, gx_ref, dg1_ref, dn_ref, scr):
        i = pl.program_id(0)

        @pl.when(i == 0)
        def _():
            dg1_ref[...] = jnp.zeros_like(dg1_ref)

        for t in range(3):
            for g, (d_ref, d) in enumerate(zip((d0_ref, d1_ref, d2_ref), DILS)):
                piece = _undilate_load(d_ref, (t,), d, scr, tm)
                if t < 2:
                    piece = _rope_bwd(piece, c_ref, s_ref)
                dn_ref[:, (3 * t + g) * GW:(3 * t + g + 1) * GW] = piece.astype(BF)
        dh = _dot(dn_ref[...], wqkv_ref[...]) + _dot(dp_ref[...], wuz_ref[...]) + _dot(dg_ref[...], wg_ref[...])
        xv = x_ref[...]
        dx1, dg1 = _rms_bwd(xv, _rms_stats(xv), g1_ref[...], dh)
        gx_ref[...] = dx2_ref[...] + dx1
        dg1_ref[...] += dg1

    row = lambda w: pl.BlockSpec((tm, w), lambda i: (i, 0))
    full = lambda s: pl.BlockSpec(s, lambda i: (0, 0))
    dil = lambda d: pl.BlockSpec((3, d, tm // d, GW), lambda i: (0, 0, i, 0))
    return pl.pallas_call(
        body, name="inproj_bwd", grid=(S // tm,),
        out_shape=(jax.ShapeDtypeStruct((S, D), F32), jax.ShapeDtypeStruct((1, D), F32),
                   jax.ShapeDtypeStruct((S, 3 * AW), BF)),
        in_specs=[dil(d) for d in DILS] + [row(2 * GMLP_W), row(2 * D), full((3 * AW, D)), full((2 * GMLP_W, D)),
                                            full((2 * D, D)), row(128), row(128), row(D), row(D), full((1, D))],
        out_specs=(row(D), full((1, D)), row(3 * AW)),
        scratch_shapes=[pltpu.VMEM((2, tm, 128), F32)],
        compiler_params=_cp(("arbitrary",)))(*dqkvs, dpuz, dgates, wqkv_t, wuz_t, wg_t, cos_t, sin_t, x, dx2, g1)


def _row_tile(rows, cap=256):
    return max(t for t in range(16, cap + 1, 16) if rows % t == 0)


def _adam_math(w, g, m, v):
    m2 = ADAM_B1 * m + (1.0 - ADAM_B1) * g
    v2 = ADAM_B2 * v + (1.0 - ADAM_B2) * (g * g)
    m_hat = m2 / (1.0 - ADAM_B1 ** ADAM_STEP)
    v_hat = v2 / (1.0 - ADAM_B2 ** ADAM_STEP)
    delta = -ADAM_LR * (m_hat / (jnp.sqrt(v_hat) + ADAM_EPS) + ADAM_WD * w)
    return delta, m2, v2


def _adam_shard(own, recv, w, m, v, name):
    R, C = w.shape
    tr = _row_tile(R)

    def body(own_ref, r_ref, w_ref, m_ref, v_ref, g_ref, d_ref, m2_ref, v2_ref):
        g = own_ref[...] + r_ref[0].astype(F32) + r_ref[1].astype(F32) + r_ref[2].astype(F32)
        g_ref[...] = g
        d_ref[...], m2_ref[...], v2_ref[...] = _adam_math(w_ref[...], g, m_ref[...], v_ref[...])

    spec = pl.BlockSpec((tr, C), lambda i: (i, 0))
    out = jax.ShapeDtypeStruct((R, C), F32)
    return pl.pallas_call(
        body, name=name, grid=(R // tr,), out_shape=(out, out, out, out),
        in_specs=[spec, pl.BlockSpec((3, tr, C), lambda i: (0, i, 0)), spec, spec, spec],
        out_specs=(spec, spec, spec, spec), compiler_params=_cp(("parallel",)))(own, recv, w, m, v)


def _rs_add(gblocks, recv, idx, name):
    _, R, C = gblocks.shape
    tr = _row_tile(R)

    def body(t_ref, g_ref, r_ref, own_ref, send_ref):
        j = pl.program_id(1)
        s = g_ref[...] + r_ref[...]

        @pl.when(j == 0)
        def _():
            own_ref[...] = s

        @pl.when(j > 0)
        def _():
            send_ref[...] = s.astype(BF)

    grid_spec = pltpu.PrefetchScalarGridSpec(
        num_scalar_prefetch=1, grid=(R // tr, 4),
        in_specs=[pl.BlockSpec((None, tr, C), lambda i, j, t: (t[j], i, 0)),
                  pl.BlockSpec((None, tr, C), lambda i, j, t: (t[4 + j], i, 0))],
        out_specs=[pl.BlockSpec((tr, C), lambda i, j, t: (i, 0)),
                   pl.BlockSpec((None, tr, C), lambda i, j, t: (jnp.maximum(j - 1, 0), i, 0))])
    return pl.pallas_call(
        body, name=name, grid_spec=grid_spec,
        out_shape=(jax.ShapeDtypeStruct((R, C), F32), jax.ShapeDtypeStruct((3, R, C), BF)),
        compiler_params=_cp(("parallel", "arbitrary")))(idx, gblocks, recv)


def _mesh_pos():
    return lax.axis_index("x"), lax.axis_index("y"), lax.axis_index("c")


_HBM = pl.BlockSpec(memory_space=pltpu.HBM)
_SEM = pl.BlockSpec(memory_space=pltpu.SEMAPHORE)
_EFFECT = pltpu.SideEffectType.DATAFLOW_SIDE_EFFECTING
_RELATIONS = [(dx, dy, dc) for dx in (0, 1) for dy in (0, 1) for dc in (0, 1)][1:]


def _flip(v, d):
    return 1 - v if d else v


def _plan_gather(n):
    def plan(x, y, c):
        return [(k, None, 4 * x + 2 * y + c, (_flip(x, dx), _flip(y, dy), _flip(c, dc)))
                for k in range(n) for dx, dy, dc in _RELATIONS]
    return plan


def _plan_gather_near(x, y, c):
    chips = [(1 - x, y), (x, 1 - y), (1 - x, 1 - y)]
    return [(0, None, 4 * x + 2 * y + c, (x, y, 1 - c))] + [(0, None, 4 * x + 2 * y + c, (*chip, c)) for chip in chips]


def _plan_gather_pass(from_landing):
    def plan(x, y, c):
        blocks = [4 * cx + 2 * cy + c for cx, cy in ((1 - x, y), (x, 1 - y), (1 - x, 1 - y))]
        return [(0, b if from_landing else None, b, (x, y, 1 - c)) for b in blocks]
    return plan


def _plan_d2d(n):
    def plan(x, y, c):
        return [(k, 2 * kk + 1 - c, kk, (x, y, 1 - c)) for k in range(n) for kk in range(4)]
    return plan


def _plan_ici(n):
    def plan(x, y, c):
        return [(k, j, j, (_flip(x, dx), _flip(y, dy), c))
                for k in range(n) for j, (dx, dy) in enumerate(((1, 0), (0, 1), (1, 1)))]
    return plan


def _plan_copies(plan, src_refs, land_refs, send_sems, recv_sems):
    x, y, c = _mesh_pos()
    return [pltpu.make_async_remote_copy(
        src_ref=src_refs[k] if si is None else src_refs[k].at[si], dst_ref=land_refs[k].at[di],
        send_sem=send_sems.at[n], recv_sem=recv_sems.at[n], device_id=dev, device_id_type=MESH)
        for n, (k, si, di, dev) in enumerate(plan(x, y, c))]


def _exchange_start(srcs, land_shapes, plan, ncopies, name, after):
    n = len(srcs)

    def body(*refs):
        src_refs, land_refs = refs[:n], refs[n:2 * n]
        send_sems, recv_sems = refs[2 * n + len(after)], refs[2 * n + len(after) + 1]
        token = refs[-1]
        for cp in _plan_copies(plan, src_refs, land_refs, send_sems, recv_sems):
            cp.start()
        token[...] = jnp.zeros_like(token)

    lands = [pltpu.with_memory_space_constraint(lax.empty(s, a.dtype), pltpu.HBM) for s, a in zip(land_shapes, srcs)]
    srcs = [pltpu.with_memory_space_constraint(a, pltpu.HBM) for a in srcs]
    outs = pl.pallas_call(
        body, name=name,
        out_shape=(pltpu.SemaphoreType.DMA((ncopies,)), pltpu.SemaphoreType.DMA((ncopies,)))
        + tuple(pltpu.HBM(a.shape, a.dtype) for a in srcs) + tuple(pltpu.HBM(a.shape, a.dtype) for a in lands)
        + (jax.ShapeDtypeStruct((8, 128), F32),),
        in_specs=[_HBM] * (2 * n) + [pl.BlockSpec(memory_space=pl.ANY)] * len(after),
        out_specs=(_SEM, _SEM) + (_HBM,) * (2 * n) + (pl.BlockSpec(memory_space=pltpu.VMEM),),
        input_output_aliases={i: 2 + i for i in range(2 * n)},
        compiler_params=pltpu.CompilerParams(has_side_effects=_EFFECT))(*srcs, *lands, *after)
    return (outs[0], outs[1], list(outs[2:2 + n]), list(outs[2 + n:2 + 2 * n])), outs[-1]


def _exchange_forward(handle, plan, plan_fwd, nfwd, name, after):
    send_sems, recv_sems, srcs, lands = handle
    n = len(srcs)

    def body(*refs):
        src_refs, land_refs = refs[:n], refs[n:2 * n]
        outs = refs[2 * n + 2 + len(after):]
        for cp in _plan_copies(plan, src_refs, land_refs, refs[2 * n], refs[2 * n + 1]):
            cp.wait_send()
            cp.wait_recv()
        for cp in _plan_copies(plan_fwd, land_refs, land_refs, outs[0], outs[1]):
            cp.start()

    outs = pl.pallas_call(
        body, name=name,
        out_shape=(pltpu.SemaphoreType.DMA((nfwd,)), pltpu.SemaphoreType.DMA((nfwd,)))
        + tuple(pltpu.HBM(a.shape, a.dtype) for a in srcs) + tuple(pltpu.HBM(a.shape, a.dtype) for a in lands),
        in_specs=[_HBM] * (2 * n) + [_SEM, _SEM] + [pl.BlockSpec(memory_space=pl.ANY)] * len(after),
        out_specs=(_SEM, _SEM) + (_HBM,) * (2 * n), input_output_aliases={i: 2 + i for i in range(2 * n)},
        compiler_params=pltpu.CompilerParams(has_side_effects=_EFFECT))(*srcs, *lands, send_sems, recv_sems, *after)
    return outs[0], outs[1], list(outs[2:2 + n]), list(outs[2 + n:2 + 2 * n])


def _exchange_wait(handle, plan, name, after):
    send_sems, recv_sems, srcs, lands = handle
    n = len(srcs)

    def body(*refs):
        src_refs, land_refs = refs[:n], refs[n:2 * n]
        for cp in _plan_copies(plan, src_refs, land_refs, refs[2 * n], refs[2 * n + 1]):
            cp.wait_send()
            cp.wait_recv()

    outs = pl.pallas_call(
        body, name=name,
        out_shape=tuple(pltpu.HBM(a.shape, a.dtype) for a in srcs) + tuple(pltpu.HBM(a.shape, a.dtype) for a in lands),
        in_specs=[_HBM] * (2 * n) + [_SEM, _SEM] + [pl.BlockSpec(memory_space=pl.ANY)] * len(after),
        out_specs=(_HBM,) * (2 * n), input_output_aliases={i: i for i in range(2 * n)},
        compiler_params=pltpu.CompilerParams(has_side_effects=_EFFECT))(*srcs, *lands, send_sems, recv_sems, *after)
    return list(outs[:n]), list(outs[n:])


SMALL = ("g1", "g2", "g3", "g4", "ln_g", "ln_b", "b_sp", "w_sp")


def _small_step(loss_row, grads, ws, ms, vs):
    parts = [loss_row] + [grads[k] for k in SMALL]
    n, ns = len(parts), len(SMALL)

    def body(*refs):
        p_refs = refs[:n]
        w_refs, m_refs, v_refs = (refs[n + i * ns:n + (i + 1) * ns] for i in range(3))
        o = n + 3 * ns
        loss_ref = refs[o]
        g_out, d_out, m_out, v_out = (refs[o + 1 + i * ns:o + 1 + (i + 1) * ns] for i in range(4))
        bufs = refs[o + 1 + 4 * ns:o + 1 + 4 * ns + n]
        send_sems, recv_sems = refs[-2:]
        x, y, c = _mesh_pos()
        me, sib = (x, y, c), (x, y, 1 - c)
        chips = [(1 - x, y), (x, 1 - y), (1 - x, 1 - y)]

        def rcopy(k, s, block, to, src=None):
            dst = bufs[k].at[4 * block[0] + 2 * block[1] + block[2]]
            return pltpu.make_async_remote_copy(
                src_ref=dst if src is None else src, dst_ref=dst, send_sem=send_sems.at[k, s],
                recv_sem=recv_sems.at[k, s], device_id=to, device_id_type=MESH)

        started = []
        for k in range(n):
            bufs[k][4 * x + 2 * y + c] = p_refs[k][...]
            started.append(rcopy(k, 0, me, sib, src=p_refs[k]))
            started += [rcopy(k, 1 + j, me, (*chip, c), src=p_refs[k]) for j, chip in enumerate(chips)]
        for cp in started:
            cp.start()
        for k in range(n):
            for j, chip in enumerate(chips):
                rcopy(k, 1 + j, (*chip, c), me).wait_recv()
                started.append(rcopy(k, 4 + j, (*chip, c), sib))
                started[-1].start()
        for k in range(n):
            rcopy(k, 0, sib, me).wait_recv()
            for j, chip in enumerate(chips):
                rcopy(k, 4 + j, (*chip, 1 - c), me).wait_recv()
        for cp in started:
            cp.wait_send()
        sums = []
        for k in range(n):
            acc = bufs[k][0]
            for b in range(1, 8):
                acc = acc + bufs[k][b]
            sums.append(acc)
        loss_ref[...] = sums[0]
        for i in range(ns):
            g_out[i][...] = sums[1 + i]
            d_out[i][...], m_out[i][...], v_out[i][...] = _adam_math(w_refs[i][...], sums[1 + i], m_refs[i][...],
                                                                     v_refs[i][...])

    args = parts + [t[k] for t in (ws, ms, vs) for k in SMALL]
    shapes = [jax.ShapeDtypeStruct(loss_row.shape, F32)] + [jax.ShapeDtypeStruct(grads[k].shape, F32) for k in SMALL] * 4
    vmem = pl.BlockSpec(memory_space=pltpu.VMEM)
    outs = pl.pallas_call(
        body, name="small_step", out_shape=tuple(shapes), in_specs=[vmem] * len(args), out_specs=(vmem,) * len(shapes),
        scratch_shapes=[pltpu.VMEM((8,) + p.shape, F32) for p in parts]
        + [pltpu.SemaphoreType.DMA((n, 7)), pltpu.SemaphoreType.DMA((n, 7))],
        compiler_params=pltpu.CompilerParams(vmem_limit_bytes=VMEM_LIMIT))(*args)
    groups = [dict(zip(SMALL, outs[1 + i * ns:1 + (i + 1) * ns])) for i in range(4)]
    return (outs[0], *groups)


def _rope_tables(S):
    half = HD // 2
    inv_freq = jnp.tile(ROPE_THETA ** (-jnp.arange(half, dtype=F32) / half), 4)
    sign = jnp.tile(jnp.concatenate([-jnp.ones(half, F32), jnp.ones(half, F32)]), 2)
    ang = jnp.arange(S, dtype=F32)[:, None] * inv_freq[None, :]
    return jnp.cos(ang), jnp.sin(ang) * sign[None, :]


def _to_blocks(g, col_sharded):
    if col_sharded:
        return g.reshape(g.shape[0], 8, g.shape[1] // 8).transpose(1, 0, 2)
    return g.reshape(8, g.shape[0] // 8, g.shape[1])


def _from_blocks(t, col_sharded):
    if col_sharded:
        return t.transpose(1, 0, 2).reshape(t.shape[1], 8 * t.shape[2])
    return t.reshape(8 * t.shape[1], t.shape[2])


class _NoComm:
    def __init__(self, late_weights):
        self._late = late_weights
        self.grads = {}

    def late_weights(self, after):
        return self._late

    def rs_start(self, key, gblocks, after=()):
        self.grads[key] = gblocks
        return jnp.zeros((8, 128), F32)

    def rs_mid(self, key, after):
        return jnp.zeros((8, 128), F32)


class _FsdpComm:
    def __init__(self, late_shards, col_sharded, after, idx, block):
        self._col, self._idx, self._block, self._rs = col_sharded, idx, block, {}
        n = len(late_shards)
        self._gather, self._token = _exchange_start(
            late_shards, [(8,) + s.shape for s in late_shards], _plan_gather(n), 7 * n, "ag_late_start", (after,))

    def late_weights(self, after):
        shards, lands = _exchange_wait(self._gather, _plan_gather(len(self._col)), "ag_late_wait", after)
        lands = [lax.dynamic_update_index_in_dim(t, s, self._block, 0) for t, s in zip(lands, shards)]
        return [_from_blocks(t, cs) for t, cs in zip(lands, self._col)]

    def rs_start(self, key, gblocks, after=()):
        n = len(gblocks)
        d2d, token = _exchange_start(gblocks, [(4,) + g.shape[1:] for g in gblocks], _plan_d2d(n), 4 * n,
                                     "rs_%s_d2d_start" % key, (self._token,) + tuple(after))
        self._rs[key] = dict(n=n, d2d=d2d)
        return token

    def rs_mid(self, key, after):
        st = self._rs[key]
        gblocks, from_sib = _exchange_wait(st["d2d"], _plan_d2d(st["n"]), "rs_%s_d2d_wait" % key, after)
        halves = [_rs_add(g, r, self._idx, "rs_add_%s_%d" % (key, k)) for k, (g, r) in enumerate(zip(gblocks, from_sib))]
        st["own"] = [own for own, _ in halves]
        sends = [send for _, send in halves]
        st["ici"], token = _exchange_start(sends, [t.shape for t in sends], _plan_ici(st["n"]), 3 * st["n"],
                                           "rs_%s_ici_start" % key, (self._token,))
        return token

    def rs_end(self, key, after):
        st = self._rs[key]
        return st["own"], _exchange_wait(st["ici"], _plan_ici(st["n"]), "rs_%s_ici_wait" % key, after)[1]


def _local_step(x, tgt, h1, cos_t, sin_t, wqkv_t, wuz_t, wg_t, comm, g1, g2, g3, g4, ln_g, ln_b, w_sp, b_sp):
    b_sp_t = b_sp.T
    w_sp_t = w_sp.transpose(0, 2, 1)

    qkvs = [_proj_qkv(h1, wqkv_t, cos_t, sin_t, g, d, "proj_qkv_d%d" % d) for g, d in enumerate(DILS)]
    puz = _proj(h1, wuz_t, "none", "proj_uz", 1024)
    gates = _proj(h1, wg_t, "sigmoid", "proj_gates", 512)
    fwd = [_attn_fwd(t, "attn_fwd_d%d" % d) for t, d in zip(qkvs, DILS)]
    ya, *lses = _attn_combine([o for o, _ in fwd], [l for _, l in fwd])
    yg = _gmlp_fwd(puz, ln_g, ln_b, w_sp, b_sp_t)
    wba, wbg, wout, wmi, wmo = comm.late_weights(after=(ya, yg, gates))
    merged, y, x2, h3 = _merge_fwd(ya, yg, gates, wba, wbg, wout, x, g2, g3)
    a, dy2, dout, loss_row, dg4 = _mlp_fwd(h3, wmi, wmo, x2, tgt, g4)

    dpre, dx2, dy, dg3, dg2 = _mlp_bwd(dy2, a, wmo, wmi, x2, y, dout, g2, g3)
    dwmo = _mm_tn(a, dy2, "dw_mlp_out", square_a=True)
    dwmi = _mm_tn(h3, dpre, "dw_mlp_in", col_blocks=8)
    tie = comm.rs_start("mlp", [dwmi, _to_blocks(dwmo, False)])
    dgates, da, db, dyg, *rest = _outproj_bwd(dy, wout, ya, yg, gates, wba, wbg, tie)
    dyas, dsums = rest[:3], rest[3:]
    tie = comm.rs_mid("mlp", after=(dyg,))
    dpuz, dwsp, dbs, dlng, dlnb = _gmlp_bwd(puz, dyg, ln_g + tie[0, 0], ln_b, w_sp, w_sp_t, b_sp_t)
    dqkvs = [_attn_bwd(qkvs[g], dyas[g], dsums[g], lses[g], "attn_bwd_d%d" % d) for g, d in enumerate(DILS)]
    grad_x, dg1, dqkv = _inproj_bwd(dqkvs, dpuz, dgates, wqkv_t, wuz_t, wg_t, cos_t, sin_t, x, dx2, g1)
    dwin_t = jnp.concatenate([_mm_tn(t, h1, "dw_in_%d" % n) for n, t in enumerate((dqkv, dpuz, dgates))], axis=0)
    dwout = _mm_tn(merged, dy, "dw_out")
    dwba = _mm_tn(ya, da, "dw_branch_attn")
    dwbg = _mm_tn(yg, db, "dw_branch_gmlp")
    tie = comm.rs_start("win", [_to_blocks(dwin_t, False)], after=(dwout, dwba, dwbg))
    small = dict(g1=dg1, g2=dg2, g3=dg3, g4=dg4, ln_g=dlng, ln_b=dlnb, b_sp=dbs.reshape(4, CHUNK),
                 w_sp=dwsp.reshape(4 * CHUNK, CHUNK))
    mid = [_to_blocks(dwba, True), _to_blocks(dwbg, True), _to_blocks(dwout, False)]
    return loss_row + tie[:1], grad_x, small, mid


def kernel(x, norm_pre_mix, w_in, w_spatial, b_spatial, ln_v_gain, ln_v_bias, w_branch_attn, w_branch_gmlp, w_out, norm_post_mix, norm_pre_mlp, w_mlp_in, w_mlp_out, norm_post_mlp, loss_target, m_norm_pre_mix, m_w_in, m_w_spatial, m_b_spatial, m_ln_v_gain, m_ln_v_bias, m_w_branch_attn, m_w_branch_gmlp, m_w_out, m_norm_post_mix, m_norm_pre_mlp, m_w_mlp_in, m_w_mlp_out, m_norm_post_mlp, v_norm_pre_mix, v_w_in, v_w_spatial, v_b_spatial, v_ln_v_gain, v_ln_v_bias, v_w_branch_attn, v_w_branch_gmlp, v_w_out, v_norm_post_mix, v_norm_pre_mlp, v_w_mlp_in, v_w_mlp_out, v_norm_post_mlp):
    mx, my, mc = _mesh_pos()
    rel = [(0, 0), (1, 0), (0, 1), (1, 1)]
    chip_of = [2 * (mx ^ dx) + (my ^ dy) for dx, dy in rel]
    idx = jnp.stack([2 * k + mc for k in chip_of] + chip_of).astype(jnp.int32)

    w_in_t, m_w_in_t, v_w_in_t = (t[0].T for t in (w_in, m_w_in, v_w_in))
    shard = w_in_t.astype(BF)
    gather, token = _exchange_start([shard], [(8,) + shard.shape], _plan_gather_near, 4, "ag_win_start", ())
    h1 = _rms_fwd(x[0], norm_pre_mix + token[0, 0], "rms_pre_mix")
    cos_t, sin_t = _rope_tables(x.shape[1])
    gather = _exchange_forward(gather, _plan_gather_near, _plan_gather_pass(True), 3, "ag_win_pass", (h1, cos_t, sin_t))
    (shard,), (g_win,) = _exchange_wait(gather, _plan_gather_pass(False), "ag_win_wait", ())
    g_win = lax.dynamic_update_index_in_dim(g_win, shard, 4 * mx + 2 * my + mc, 0)
    win_t = _from_blocks(g_win, False)
    wqkv_t, wuz_t, wg_t = win_t[:3 * AW], win_t[3 * AW:3 * AW + 2 * GMLP_W], win_t[3 * AW + 2 * GMLP_W:]
    late = [w_branch_attn[0], w_branch_gmlp[0], w_out[0], w_mlp_in[0], w_mlp_out[0]]
    comm = _FsdpComm([w.astype(BF) for w in late], [True, True, False, True, False], g_win, idx, 4 * mx + 2 * my + mc)

    loss_row, grad_x, small, mid = _local_step(
        x[0], loss_target[0], h1, cos_t, sin_t, wqkv_t, wuz_t, wg_t, comm,
        norm_pre_mix, norm_post_mix, norm_pre_mlp, norm_post_mlp, ln_v_gain, ln_v_bias, w_spatial[0], b_spatial[0])

    flat = lambda t: t.reshape(-1, t.shape[-1])
    small_w = dict(zip(SMALL, map(flat, (norm_pre_mix, norm_post_mix, norm_pre_mlp, norm_post_mlp, ln_v_gain, ln_v_bias,
                                         b_spatial, w_spatial))))
    small_m = dict(zip(SMALL, map(flat, (m_norm_pre_mix, m_norm_post_mix, m_norm_pre_mlp, m_norm_post_mlp, m_ln_v_gain,
                                         m_ln_v_bias, m_b_spatial, m_w_spatial))))
    small_v = dict(zip(SMALL, map(flat, (v_norm_pre_mix, v_norm_post_mix, v_norm_pre_mlp, v_norm_post_mlp, v_ln_v_gain,
                                         v_ln_v_bias, v_b_spatial, v_w_spatial))))
    loss_out, sg, sd, sm, sv = _small_step(loss_row, small, small_w, small_m, small_v)
    tie = comm.rs_mid("win", after=(loss_out,))
    tie = comm.rs_start("mid", mid, after=(tie,))
    mlp_own, mlp_chips = comm.rs_end("mlp", after=(tie,))
    adam = lambda nm, own, r, w, m, v: _adam_shard(own, r, w[0], m[0], v[0], "adam_" + nm)
    upd = {
        "w_mlp_in": adam("w_mlp_in", mlp_own[0], mlp_chips[0], w_mlp_in, m_w_mlp_in, v_w_mlp_in),
        "w_mlp_out": adam("w_mlp_out", mlp_own[1], mlp_chips[1], w_mlp_out, m_w_mlp_out, v_w_mlp_out),
    }
    loss = loss_out[0, 0]
    tie = comm.rs_mid("mid", after=(upd["w_mlp_in"][0], upd["w_mlp_out"][0]))
    win_own, win_chips = comm.rs_end("win", after=(tie,))
    upd["w_in"] = tuple(t.T for t in _adam_shard(win_own[0], win_chips[0], w_in_t, m_w_in_t, v_w_in_t, "adam_w_in"))
    mid_own, mid_chips = comm.rs_end("mid", after=(upd["w_in"][0],))
    upd["w_branch_attn"] = adam("w_branch_attn", mid_own[0], mid_chips[0], w_branch_attn, m_w_branch_attn, v_w_branch_attn)
    upd["w_branch_gmlp"] = adam("w_branch_gmlp", mid_own[1], mid_chips[1], w_branch_gmlp, m_w_branch_gmlp, v_w_branch_gmlp)
    upd["w_out"] = adam("w_out", mid_own[2], mid_chips[2], w_out, m_w_out, v_w_out)
    order = ["g1", "w_in", "w_sp", "b_sp", "ln_g", "ln_b", "w_branch_attn", "w_branch_gmlp", "w_out", "g2", "g3",
             "w_mlp_in", "w_mlp_out", "g4"]
    small_shape = dict(g1=norm_pre_mix.shape, g2=norm_post_mix.shape, g3=norm_pre_mlp.shape, g4=norm_post_mlp.shape,
                       ln_g=ln_v_gain.shape, ln_b=ln_v_bias.shape, b_sp=b_spatial.shape, w_sp=w_spatial.shape)

    def pick(which):
        return [upd[nm][which][None] if nm in upd else (sg, sd, sm, sv)[which][nm].reshape(small_shape[nm])
                for nm in order]

    return (loss, grad_x[None], *pick(0), *pick(1), *pick(2), *pick(3))
```

```python
import functools
import math

import jax
import jax.numpy as jnp
from jax import lax
from jax.experimental import pallas as pl
from jax.experimental.pallas import tpu as pltpu

D = 1024
HD = 64
NSLOT = 4
GW = NSLOT * HD
DILS = (1, 4, 16)
QB = 128
ATTN_NSUB = 4
MLP_TM_FWD = 512
MLP_TM_BWD = 256
AW = 3 * GW
GMLP_W = 512
CHUNK = 128
DFF = 4096
EPS = 1e-6
ROPE_THETA = 10000.0
SCALE = HD ** -0.5
NEG = -1e30

ADAM_LR = 0.001
ADAM_B1 = 0.9
ADAM_B2 = 0.999
ADAM_EPS = 1e-08
ADAM_WD = 0.01
ADAM_STEP = 10

BF = jnp.bfloat16
F32 = jnp.float32
MESH = pl.DeviceIdType.MESH
VMEM_LIMIT = 56 * 1024 * 1024


def _cp(sem):
    return pltpu.CompilerParams(dimension_semantics=sem, vmem_limit_bytes=VMEM_LIMIT)


def _dot(a, b):
    return jnp.dot(a, b, preferred_element_type=F32)


def _dot_nt(a, b):
    return lax.dot_general(a, b, (((1,), (1,)), ((), ())), preferred_element_type=F32)


def _dot_tn(a, b):
    return lax.dot_general(a, b, (((0,), (0,)), ((), ())), preferred_element_type=F32)


def _gelu(x):
    return jax.nn.gelu(x, approximate=True)


def _gelu_grad(x):
    k = math.sqrt(2.0 / math.pi)
    t = jnp.tanh(k * (x + 0.044715 * x * x * x))
    return 0.5 * (1.0 + t) + 0.5 * x * (1.0 - t * t) * (k * (1.0 + 3.0 * 0.044715 * x * x))


def _swap_halves(t):
    w = t.shape[1]
    lane = lax.broadcasted_iota(jnp.int32, t.shape, 1)
    first = (lane & (HD - 1)) < (HD // 2)
    return jnp.where(first, pltpu.roll(t, w - HD // 2, 1), pltpu.roll(t, HD // 2, 1))


def _head_mask(hh):
    lane = lax.broadcasted_iota(jnp.int32, (1, GW), 1)
    return jnp.logical_and(lane >= hh * HD, lane < (hh + 1) * HD)


def _rms_stats(xf):
    return lax.rsqrt(jnp.mean(xf * xf, axis=-1, keepdims=True) + EPS)


def _rms_bwd(xf, r, gain, dout):
    n = xf * r
    t = dout * gain
    dx = r * (t - n * jnp.mean(t * n, axis=-1, keepdims=True))
    return dx, jnp.sum(dout * n, axis=0, keepdims=True)


def _rms_fwd(x, gain, name):
    S = x.shape[0]
    tm = min(512, S)

    def body(x_ref, g_ref, h_ref):
        xf = x_ref[...]
        h_ref[...] = (xf * _rms_stats(xf) * g_ref[...]).astype(BF)

    return pl.pallas_call(
        body, name=name, out_shape=jax.ShapeDtypeStruct((S, D), BF), grid=(S // tm,),
        in_specs=[pl.BlockSpec((tm, D), lambda i: (i, 0)), pl.BlockSpec((1, D), lambda i: (0, 0))],
        out_specs=pl.BlockSpec((tm, D), lambda i: (i, 0)), compiler_params=_cp(("parallel",)))(x, gain)


def _proj(h, w_t, epi, name, tm):
    S, K = h.shape
    N = w_t.shape[0]
    tm = min(tm, S)

    def body(h_ref, w_ref, o_ref):
        y = _dot_nt(h_ref[...], w_ref[...])
        if epi == "sigmoid":
            y = jax.nn.sigmoid(y)
        o_ref[...] = y.astype(BF)

    return pl.pallas_call(
        body, name=name, out_shape=jax.ShapeDtypeStruct((S, N), BF), grid=(S // tm,),
        in_specs=[pl.BlockSpec((tm, K), lambda i: (i, 0)), _resident((N, K))],
        out_specs=pl.BlockSpec((tm, N), lambda i: (i, 0)),
        compiler_params=_cp(("parallel",)))(h, w_t)


def _dilate_store(val, scr, o_ref, lead, d):
    rows = val.shape[0]
    if d == 1:
        o_ref[lead + (0,)] = val.astype(o_ref.dtype)
        return
    for hf in range(2):
        scr[hf, pl.ds(0, rows), :] = val[:, hf * 128:(hf + 1) * 128]
    for r in range(d):
        for hf in range(2):
            o_ref[lead + (r, slice(None), slice(hf * 128, (hf + 1) * 128))] = (
                scr[hf, pl.ds(r, rows // d, stride=d), :].astype(o_ref.dtype))


def _undilate_load(i_ref, lead, d, scr, rows):
    if d == 1:
        return i_ref[lead + (0,)].astype(F32)
    for r in range(d):
        for hf in range(2):
            scr[hf, pl.ds(r, rows // d, stride=d), :] = (
                i_ref[lead + (r, slice(None), slice(hf * 128, (hf + 1) * 128))].astype(F32))
    return jnp.concatenate([scr[0, pl.ds(0, rows), :], scr[1, pl.ds(0, rows), :]], axis=1)


def _rope_fwd(y, c_ref, s_ref):
    cosv = jnp.concatenate([c_ref[...]] * 2, axis=1)
    sinv = jnp.concatenate([s_ref[...]] * 2, axis=1)
    return y * cosv + _swap_halves(y) * sinv


def _rope_bwd(dy, c_ref, s_ref):
    cosv = jnp.concatenate([c_ref[...]] * 2, axis=1)
    sinv = jnp.concatenate([s_ref[...]] * 2, axis=1)
    return dy * cosv + _swap_halves(dy * sinv)


def _proj_qkv(h, wqkv_t, cos_t, sin_t, g, d, name, tie):
    S, K = h.shape
    tm = min(2048, S)

    def body(h_ref, w_ref, c_ref, s_ref, tie_ref, o_ref, scr):
        j = pl.program_id(1)
        y = _dot_nt(h_ref[...], w_ref[...])
        y = jnp.where(j < 2, _rope_fwd(y, c_ref, s_ref), y)
        _dilate_store(y, scr, o_ref, (), d)

    return pl.pallas_call(
        body, name=name, out_shape=jax.ShapeDtypeStruct((3, d, S // d, GW), BF), grid=(S // tm, 3),
        in_specs=[pl.BlockSpec((tm, K), lambda i, j: (i, 0)), pl.BlockSpec((GW, K), lambda i, j: (3 * j + g, 0)),
                  pl.BlockSpec((tm, 128), lambda i, j: (i, 0)), pl.BlockSpec((tm, 128), lambda i, j: (i, 0)),
                  pl.BlockSpec(memory_space=pl.ANY)],
        out_specs=pl.BlockSpec((None, d, tm // d, GW), lambda i, j: (j, 0, i, 0)),
        scratch_shapes=[pltpu.VMEM((2, tm, 128), F32)],
        compiler_params=_cp(("parallel", "arbitrary")))(h, wqkv_t, cos_t, sin_t, tie)


def _band_masks(first_step):
    row = lax.broadcasted_iota(jnp.int32, (QB, 2 * QB), 0)
    col = lax.broadcasted_iota(jnp.int32, (QB, 2 * QB), 1)
    band = jnp.logical_and(col >= row, col <= row + QB)
    return band, jnp.logical_and(band, jnp.logical_or(col >= QB, jnp.logical_not(first_step)))


def _attn_fwd(qkv, name):
    _, d, L, _ = qkv.shape
    nsub = min(ATTN_NSUB, L // QB)
    R = nsub * QB
    nsteps = L // R

    def body(q_ref, kp_ref, kc_ref, vp_ref, vc_ref, o_ref, lse_ref):
        i = pl.program_id(1)
        band, band_first = _band_masks(i == 0)
        kfull = jnp.concatenate([kp_ref[...], kc_ref[...]], axis=0)
        vfull = jnp.concatenate([vp_ref[...], vc_ref[...]], axis=0)
        chains = [(sb, hh) for sb in range(nsub) for hh in range(NSLOT)]
        win = lambda t, sb: t[sb * QB:(sb + 2) * QB]
        scores = []
        for sb, hh in chains:
            qh = jnp.where(_head_mask(hh), q_ref[sb * QB:(sb + 1) * QB, :], 0)
            scores.append(_dot_nt(qh, win(kfull, sb)))
        soft = []
        for (sb, hh), sc in zip(chains, scores):
            sc = jnp.where(band_first if sb == 0 else band, sc * SCALE, NEG)
            m = jnp.max(sc, axis=1, keepdims=True)
            p = jnp.exp(sc - m)
            den = jnp.sum(p, axis=1, keepdims=True)
            soft.append((p.astype(BF), den, m + jnp.log(den)))
        accs = [_dot(p, win(vfull, sb)) for (sb, hh), (p, _, _) in zip(chains, soft)]
        for sb in range(nsub):
            o = jnp.zeros((QB, GW), F32)
            lse = jnp.zeros((QB, GW), F32)
            for hh in range(NSLOT):
                hm = _head_mask(hh)
                _, den, lrow = soft[sb * NSLOT + hh]
                o = o + jnp.where(hm, accs[sb * NSLOT + hh] / den, 0.0)
                lse = lse + jnp.where(hm, lrow, 0.0)
            o_ref[sb * QB:(sb + 1) * QB, :] = o
            lse_ref[sb * QB:(sb + 1) * QB, :] = lse

    prev = lambda i: jnp.maximum(i * nsub - 1, 0)
    cur = lambda t: pl.BlockSpec((None, None, R, GW), lambda r, i: (t, r, i, 0))
    prv = lambda t: pl.BlockSpec((None, None, QB, GW), lambda r, i: (t, r, prev(i), 0))
    out = pl.BlockSpec((None, R, GW), lambda r, i: (r, i, 0))
    return pl.pallas_call(
        body, name=name, grid=(d, nsteps),
        out_shape=(jax.ShapeDtypeStruct((d, L, GW), F32), jax.ShapeDtypeStruct((d, L, GW), F32)),
        in_specs=[cur(0), prv(1), cur(1), prv(2), cur(2)],
        out_specs=(out, out), compiler_params=_cp(("parallel", "arbitrary")))(qkv, qkv, qkv, qkv, qkv)


def _attn_combine(os_, lses):
    S = os_[0].shape[1]
    tm = min(1024, S)

    def body(o0, o1, o2, l0, l1, l2, y_ref, j0, j1, j2, scr):
        os_nat = [_undilate_load(o, (), d, scr, tm) for o, d in zip((o0, o1, o2), DILS)]
        a, b, c = [_undilate_load(l, (), d, scr, tm) for l, d in zip((l0, l1, l2), DILS)]
        m = jnp.maximum(jnp.maximum(a, b), c)
        wa, wb, wc = jnp.exp(a - m), jnp.exp(b - m), jnp.exp(c - m)
        den = wa + wb + wc
        y_ref[...] = ((wa * os_nat[0] + wb * os_nat[1] + wc * os_nat[2]) / den).astype(BF)
        lse = m + jnp.log(den)
        for j_ref, d in zip((j0, j1, j2), DILS):
            _dilate_store(lse, scr, j_ref, (), d)

    dil = lambda d: pl.BlockSpec((d, tm // d, GW), lambda i: (0, i, 0))
    dshape = lambda d: jax.ShapeDtypeStruct((d, S // d, GW), F32)
    return pl.pallas_call(
        body, name="attn_combine", grid=(S // tm,),
        out_shape=(jax.ShapeDtypeStruct((S, GW), BF),) + tuple(dshape(d) for d in DILS),
        in_specs=[dil(d) for d in DILS] * 2,
        out_specs=(pl.BlockSpec((tm, GW), lambda i: (i, 0)),) + tuple(dil(d) for d in DILS),
        scratch_shapes=[pltpu.VMEM((2, tm, 128), F32)],
        compiler_params=_cp(("parallel",)))(*os_, *lses)


def _tril(upper=False):
    row = lax.broadcasted_iota(jnp.int32, (CHUNK, CHUNK), 0)
    col = lax.broadcasted_iota(jnp.int32, (CHUNK, CHUNK), 1)
    return row <= col if upper else col <= row


def _ln_fwd(z, gain, bias):
    mu = jnp.mean(z, axis=-1, keepdims=True)
    zc = z - mu
    rstd = lax.rsqrt(jnp.mean(zc * zc, axis=-1, keepdims=True) + EPS)
    zhat = zc * rstd
    return zhat, rstd, zhat * gain + bias


def _gmlp_fwd(puz, ln_g, ln_b, w_sp, b_sp_t):
    S = puz.shape[0]
    tm = min(512, S)
    nch = tm // CHUNK

    def body(p_ref, g_ref, b_ref, w_ref, bt_ref, o_ref):
        tril = _tril()
        ws = [jnp.where(tril, w_ref[gg], 0.0).astype(BF) for gg in range(4)]
        for ch in range(nch):
            rows = slice(ch * CHUNK, (ch + 1) * CHUNK)
            z = _gelu(p_ref[rows, GMLP_W:].astype(F32))
            _, _, zn = _ln_fwd(z, g_ref[...], b_ref[...])
            zn = zn.astype(BF)
            for gg in range(4):
                cols = slice(gg * CHUNK, (gg + 1) * CHUNK)
                sz = _dot(ws[gg], zn[:, cols]) + bt_ref[:, gg:gg + 1]
                u = _gelu(p_ref[rows, cols].astype(F32))
                o_ref[rows, cols] = (u * sz).astype(BF)

    return pl.pallas_call(
        body, name="gmlp_fwd", out_shape=jax.ShapeDtypeStruct((S, GMLP_W), BF), grid=(S // tm,),
        in_specs=[pl.BlockSpec((tm, 2 * GMLP_W), lambda i: (i, 0)),
                  pl.BlockSpec((1, GMLP_W), lambda i: (0, 0)), pl.BlockSpec((1, GMLP_W), lambda i: (0, 0)),
                  pl.BlockSpec((4, CHUNK, CHUNK), lambda i: (0, 0, 0)), pl.BlockSpec((CHUNK, 4), lambda i: (0, 0))],
        out_specs=pl.BlockSpec((tm, GMLP_W), lambda i: (i, 0)),
        compiler_params=_cp(("parallel",)))(puz, ln_g, ln_b, w_sp, b_sp_t)


def _merge_fwd(ya, yg, gates, wba, wbg, wout, x, g2, g3):
    S = x.shape[0]
    tm = min(512, S)

    def body(ya_ref, yg_ref, gt_ref, wba_ref, wbg_ref, wo_ref, x_ref, g2_ref, g3_ref,
             mg_ref, y_ref, x2_ref, h3_ref):
        a = _dot(ya_ref[...], wba_ref[...])
        b = _dot(yg_ref[...], wbg_ref[...])
        merged = (gt_ref[:, :D].astype(F32) * a + gt_ref[:, D:].astype(F32) * b).astype(BF)
        mg_ref[...] = merged
        y = _dot(merged, wo_ref[...])
        y_ref[...] = y
        x2 = x_ref[...] + y * _rms_stats(y) * g2_ref[...]
        x2_ref[...] = x2
        h3_ref[...] = (x2 * _rms_stats(x2) * g3_ref[...]).astype(BF)

    row = lambda w: pl.BlockSpec((tm, w), lambda i: (i, 0))
    full = lambda s: pl.BlockSpec(s, lambda i: (0, 0))
    return pl.pallas_call(
        body, name="merge_fwd", grid=(S // tm,),
        out_shape=(jax.ShapeDtypeStruct((S, D), BF), jax.ShapeDtypeStruct((S, D), F32), jax.ShapeDtypeStruct((S, D), F32),
                   jax.ShapeDtypeStruct((S, D), BF)),
        in_specs=[row(GW), row(GMLP_W), row(2 * D), full((GW, D)), full((GMLP_W, D)), full((D, D)), row(D),
                  full((1, D)), full((1, D))],
        out_specs=(row(D), row(D), row(D), row(D)),
        compiler_params=_cp(("parallel",)))(ya, yg, gates, wba, wbg, wout, x, g2, g3)


def _resident(shape):
    return pl.BlockSpec(shape, lambda i: (0,) * len(shape), pipeline_mode=pl.Buffered(1))


def _mlp_fwd(h3, wmi, wmo, x2, tgt, g4):
    S = x2.shape[0]
    tm = min(MLP_TM_FWD, S)

    def body(h_ref, wi_ref, wo_ref, x2_ref, t_ref, g4_ref, a_ref, dy2_ref, dout_ref, loss_ref, dg4_ref):
        @pl.when(pl.program_id(0) == 0)
        def _():
            loss_ref[...] = jnp.zeros_like(loss_ref)
            dg4_ref[...] = jnp.zeros_like(dg4_ref)

        a = jnp.maximum(_dot(h_ref[...], wi_ref[...]), 0.0)
        a_ref[...] = a.astype(BF)
        y2 = _dot((a * a).astype(BF), wo_ref[...])
        r = _rms_stats(y2)
        out = x2_ref[...] + y2 * r * g4_ref[...]
        err = out - t_ref[...]
        tot = jnp.sum(jnp.sum(err * err, axis=1, keepdims=True), axis=0, keepdims=True) * (0.5 / D)
        lane = lax.broadcasted_iota(jnp.int32, (1, 128), 1)
        loss_ref[...] += jnp.where(lane == 0, tot, 0.0)
        dout = err * (1.0 / D)
        dout_ref[...] = dout
        dy2, dg = _rms_bwd(y2, r, g4_ref[...], dout)
        dy2_ref[...] = dy2.astype(BF)
        dg4_ref[...] += dg

    row = pl.BlockSpec((tm, D), lambda i: (i, 0))
    return pl.pallas_call(
        body, name="mlp_fwd", grid=(S // tm,),
        out_shape=(jax.ShapeDtypeStruct((S, DFF), BF), jax.ShapeDtypeStruct((S, D), BF), jax.ShapeDtypeStruct((S, D), F32),
                   jax.ShapeDtypeStruct((1, 128), F32), jax.ShapeDtypeStruct((1, D), F32)),
        in_specs=[row, _resident((D, DFF)), _resident((DFF, D)), row, row, pl.BlockSpec((1, D), lambda i: (0, 0))],
        out_specs=(pl.BlockSpec((tm, DFF), lambda i: (i, 0)), row, row,
                   pl.BlockSpec((1, 128), lambda i: (0, 0)), pl.BlockSpec((1, D), lambda i: (0, 0))),
        compiler_params=_cp(("arbitrary",)))(h3, wmi, wmo, x2, tgt, g4)


def _mlp_bwd(dy2, a, wmo, wmi, x2, y, dout, g2, g3):
    S = x2.shape[0]
    tm = min(MLP_TM_BWD, S)

    def body(dy2_ref, a_ref, wo_ref, wi_ref, x2_ref, y_ref, dout_ref, g2_ref, g3_ref,
             dpre_ref, dx2_ref, dy_ref, dg3_ref, dg2_ref):
        @pl.when(pl.program_id(0) == 0)
        def _():
            dg3_ref[...] = jnp.zeros_like(dg3_ref)
            dg2_ref[...] = jnp.zeros_like(dg2_ref)

        da2 = _dot_nt(dy2_ref[...], wo_ref[...])
        dpre = (2.0 * a_ref[...].astype(F32) * da2).astype(BF)
        dpre_ref[...] = dpre
        dh3 = _dot_nt(dpre, wi_ref[...])
        x2 = x2_ref[...]
        dx3, dg3 = _rms_bwd(x2, _rms_stats(x2), g3_ref[...], dh3)
        dx2 = dout_ref[...] + dx3
        dx2_ref[...] = dx2
        dg3_ref[...] += dg3
        yv = y_ref[...]
        dy, dg2 = _rms_bwd(yv, _rms_stats(yv), g2_ref[...], dx2)
        dy_ref[...] = dy.astype(BF)
        dg2_ref[...] += dg2

    row = pl.BlockSpec((tm, D), lambda i: (i, 0))
    wide = pl.BlockSpec((tm, DFF), lambda i: (i, 0))
    vec = pl.BlockSpec((1, D), lambda i: (0, 0))
    return pl.pallas_call(
        body, name="mlp_bwd", grid=(S // tm,),
        out_shape=(jax.ShapeDtypeStruct((S, DFF), BF), jax.ShapeDtypeStruct((S, D), F32), jax.ShapeDtypeStruct((S, D), BF),
                   jax.ShapeDtypeStruct((1, D), F32), jax.ShapeDtypeStruct((1, D), F32)),
        in_specs=[row, wide, _resident((DFF, D)), _resident((D, DFF)), row, row, row, vec, vec],
        out_specs=(wide, row, row, vec, vec),
        compiler_params=_cp(("arbitrary",)))(dy2, a, wmo, wmi, x2, y, dout, g2, g3)


def _mm_tn(a, b, name, square_a=False, tm=1024, tn=1024, tk=2048, tie=None, col_blocks=None):
    S, M = a.shape
    N = b.shape[1]
    tk = min(tk, S)
    tm = max(t for t in range(128, min(tm, M) + 1, 128) if M % t == 0)
    tn = max(t for t in range(128, min(tn, N) + 1, 128) if N % t == 0)
    cb = N // col_blocks if col_blocks else tn
    assert M % tm == 0 and S % tk == 0 and tn % cb == 0
    nk = S // tk
    ties = () if tie is None else (tie,)

    def body(a_ref, b_ref, *rest):
        o_ref = rest[-1]
        k = pl.program_id(2)
        av = a_ref[...]
        if square_a:
            av = av * av
        part = _dot_tn(av, b_ref[...])
        if col_blocks:
            part = jnp.stack([part[:, t * cb:(t + 1) * cb] for t in range(tn // cb)])

        @pl.when(k == 0)
        def _():
            o_ref[...] = part

        @pl.when(k > 0)
        def _():
            o_ref[...] += part

    if col_blocks:
        out_shape, out_spec = (col_blocks, M, cb), pl.BlockSpec((tn // cb, tm, cb), lambda i, j, k: (j, i, 0))
    else:
        out_shape, out_spec = (M, N), pl.BlockSpec((tm, tn), lambda i, j, k: (i, j))
    return pl.pallas_call(
        body, name=name, out_shape=jax.ShapeDtypeStruct(out_shape, F32), grid=(M // tm, N // tn, nk),
        in_specs=[pl.BlockSpec((tk, tm), lambda i, j, k: (k, i)), pl.BlockSpec((tk, tn), lambda i, j, k: (k, j))]
        + [pl.BlockSpec(memory_space=pl.ANY)] * len(ties),
        out_specs=out_spec,
        compiler_params=_cp(("parallel", "parallel", "arbitrary")))(a, b, *ties)


def _outproj_bwd(dy, wout, ya, yg, gates, wba, wbg, tie):
    S = dy.shape[0]
    tm = min(512, S)

    def body(dy_ref, wo_ref, ya_ref, yg_ref, gt_ref, wba_ref, wbg_ref, tie_ref,
             dgt_ref, da_ref, db_ref, dyg_ref, e0, e1, e2, s0, s1, s2, scr):
        dm = _dot_nt(dy_ref[...], wo_ref[...])
        ga, gb = gt_ref[:, :D].astype(F32), gt_ref[:, D:].astype(F32)
        dgt_ref[:, :D] = (dm * _dot(ya_ref[...], wba_ref[...]) * ga * (1.0 - ga)).astype(BF)
        dgt_ref[:, D:] = (dm * _dot(yg_ref[...], wbg_ref[...]) * gb * (1.0 - gb)).astype(BF)
        da = (dm * ga).astype(BF)
        db = (dm * gb).astype(BF)
        da_ref[...] = da
        db_ref[...] = db
        dyg_ref[...] = _dot_nt(db, wbg_ref[...]).astype(BF)
        dya = _dot_nt(da, wba_ref[...]).astype(BF).astype(F32)
        dyy = dya * ya_ref[...].astype(F32)
        dsum = jnp.zeros((tm, GW), F32)
        for hh in range(NSLOT):
            hm = _head_mask(hh)
            dsum = dsum + jnp.where(hm, jnp.sum(jnp.where(hm, dyy, 0.0), axis=1, keepdims=True), 0.0)
        for e_ref, s_ref, d in zip((e0, e1, e2), (s0, s1, s2), DILS):
            _dilate_store(dya, scr, e_ref, (), d)
            _dilate_store(dsum, scr, s_ref, (), d)

    row = lambda w: pl.BlockSpec((tm, w), lambda i: (i, 0))
    full = lambda s: pl.BlockSpec(s, lambda i: (0, 0))
    dil = lambda d: pl.BlockSpec((d, tm // d, GW), lambda i: (0, i, 0))
    dshape = lambda d, t: jax.ShapeDtypeStruct((d, S // d, GW), t)
    return pl.pallas_call(
        body, name="outproj_bwd", grid=(S // tm,),
        out_shape=(jax.ShapeDtypeStruct((S, 2 * D), BF), jax.ShapeDtypeStruct((S, D), BF), jax.ShapeDtypeStruct((S, D), BF),
                   jax.ShapeDtypeStruct((S, GMLP_W), BF)) + tuple(dshape(d, BF) for d in DILS)
        + tuple(dshape(d, F32) for d in DILS),
        in_specs=[row(D), full((D, D)), row(GW), row(GMLP_W), row(2 * D), full((GW, D)), full((GMLP_W, D)),
                  pl.BlockSpec(memory_space=pl.ANY)],
        out_specs=(row(2 * D), row(D), row(D), row(GMLP_W)) + tuple(dil(d) for d in DILS) * 2,
        scratch_shapes=[pltpu.VMEM((2, tm, 128), F32)],
        compiler_params=_cp(("parallel",)))(dy, wout, ya, yg, gates, wba, wbg, tie)


def _gmlp_bwd(puz, dyg, ln_g, ln_b, w_sp, w_sp_t, b_sp_t):
    S = puz.shape[0]
    tm = min(512, S)
    nch = tm // CHUNK

    def body(p_ref, dy_ref, g_ref, b_ref, w_ref, wt_ref, bt_ref,
             dp_ref, dw_ref, dbs_ref, dg_ref, dbias_ref, dbacc_ref):
        i = pl.program_id(0)

        @pl.when(i == 0)
        def _():
            dw_ref[...] = jnp.zeros_like(dw_ref)
            dbacc_ref[...] = jnp.zeros_like(dbacc_ref)
            dg_ref[...] = jnp.zeros_like(dg_ref)
            dbias_ref[...] = jnp.zeros_like(dbias_ref)

        tril = _tril()
        ws = [jnp.where(tril, w_ref[gg], 0.0).astype(BF) for gg in range(4)]
        triu = _tril(upper=True)
        wts = [jnp.where(triu, wt_ref[gg], 0.0).astype(BF) for gg in range(4)]
        gain = g_ref[...]
        for ch in range(nch):
            rows = slice(ch * CHUNK, (ch + 1) * CHUNK)
            pz = p_ref[rows, GMLP_W:].astype(F32)
            z = _gelu(pz)
            zhat, rstd, zn = _ln_fwd(z, gain, b_ref[...])
            znb = zn.astype(BF)
            dzn_parts = []
            for gg in range(4):
                cols = slice(gg * CHUNK, (gg + 1) * CHUNK)
                pu = p_ref[rows, cols].astype(F32)
                u = _gelu(pu)
                sz = _dot(ws[gg], znb[:, cols]) + bt_ref[:, gg:gg + 1]
                dyv = dy_ref[rows, cols].astype(F32)
                dp_ref[rows, cols] = (dyv * sz * _gelu_grad(pu)).astype(BF)
                dsz = dyv * u
                dbacc_ref[gg] += dsz
                dszb = dsz.astype(BF)
                dw_ref[gg] += _dot_nt(dszb, znb[:, cols])
                dzn_parts.append(_dot(wts[gg], dszb))
            dzn = jnp.concatenate(dzn_parts, axis=1)
            dg_ref[...] += jnp.sum(dzn * zhat, axis=0, keepdims=True)
            dbias_ref[...] += jnp.sum(dzn, axis=0, keepdims=True)
            dzh = dzn * gain
            dz = rstd * (dzh - jnp.mean(dzh, axis=-1, keepdims=True)
                         - zhat * jnp.mean(dzh * zhat, axis=-1, keepdims=True))
            dp_ref[rows, GMLP_W:] = (dz * _gelu_grad(pz)).astype(BF)

        @pl.when(i == pl.num_programs(0) - 1)
        def _():
            for gg in range(4):
                dw_ref[gg] = jnp.where(tril, dw_ref[gg], 0.0)
                dbs_ref[gg] = jnp.sum(dbacc_ref[gg], axis=1, keepdims=True)

    full2 = lambda s: pl.BlockSpec(s, lambda i: (0, 0))
    full3 = lambda s: pl.BlockSpec(s, lambda i: (0, 0, 0))
    return pl.pallas_call(
        body, name="gmlp_bwd", grid=(S // tm,),
        out_shape=(jax.ShapeDtypeStruct((S, 2 * GMLP_W), BF), jax.ShapeDtypeStruct((4, CHUNK, CHUNK), F32),
                   jax.ShapeDtypeStruct((4, CHUNK, 1), F32), jax.ShapeDtypeStruct((1, GMLP_W), F32),
                   jax.ShapeDtypeStruct((1, GMLP_W), F32)),
        in_specs=[pl.BlockSpec((tm, 2 * GMLP_W), lambda i: (i, 0)), pl.BlockSpec((tm, GMLP_W), lambda i: (i, 0)),
                  full2((1, GMLP_W)), full2((1, GMLP_W)), full3((4, CHUNK, CHUNK)), full3((4, CHUNK, CHUNK)),
                  full2((CHUNK, 4))],
        out_specs=(pl.BlockSpec((tm, 2 * GMLP_W), lambda i: (i, 0)), full3((4, CHUNK, CHUNK)), full3((4, CHUNK, 1)),
                   full2((1, GMLP_W)), full2((1, GMLP_W))),
        scratch_shapes=[pltpu.VMEM((4, CHUNK, CHUNK), F32)],
        compiler_params=_cp(("arbitrary",)))(puz, dyg, ln_g, ln_b, w_sp, w_sp_t, b_sp_t)


def _attn_bwd(qkv, dya, dsums, lse, name):
    _, d, L, _ = qkv.shape
    nsub = min(ATTN_NSUB, L // QB)
    R = nsub * QB
    nsteps = L // R

    def body(q_ref, qn_ref, kp_ref, kc_ref, vp_ref, vc_ref, dy_ref, dyn_ref, e_ref, en_ref, l_ref, ln_ref, o_ref):
        i = pl.program_id(1)
        band, band_first = _band_masks(i == 0)
        row = lax.broadcasted_iota(jnp.int32, (QB, QB), 0)
        col = lax.broadcasted_iota(jnp.int32, (QB, QB), 1)
        mask_next = jnp.logical_and(col >= row, i < nsteps - 1)
        kc, vc = kc_ref[...], vc_ref[...]
        kfull = jnp.concatenate([kp_ref[...], kc], axis=0)
        vfull = jnp.concatenate([vp_ref[...], vc], axis=0)
        k_last, v_last = kc[(nsub - 1) * QB:], vc[(nsub - 1) * QB:]
        q_ext = jnp.concatenate([q_ref[...], qn_ref[...]], axis=0)
        dy_ext = jnp.concatenate([dy_ref[...], dyn_ref[...]], axis=0)
        esum, esum_n, lse, lse_n = e_ref[...], en_ref[...], l_ref[...], ln_ref[...]
        win = lambda t, sb: t[sb * QB:(sb + 2) * QB]
        blk = lambda t, sb: t[sb * QB:(sb + 1) * QB]
        hms = [_head_mask(hh) for hh in range(NSLOT)]
        q_hs = [jnp.where(hm, q_ext, 0) for hm in hms]
        dy_hs = [jnp.where(hm, dy_ext, 0) for hm in hms]
        raw = []
        for hh in range(NSLOT):
            tiles = [(_dot_nt(blk(q_hs[hh], sb), win(kfull, sb)), _dot_nt(blk(dy_hs[hh], sb), win(vfull, sb)))
                     for sb in range(nsub)]
            tiles.append((_dot_nt(blk(q_hs[hh], nsub), k_last), _dot_nt(blk(dy_hs[hh], nsub), v_last)))
            raw.append(tiles)
        ps, dss = [], []
        for hh in range(NSLOT):
            rowstat = lambda t: jnp.max(jnp.where(hms[hh], t, -jnp.inf), axis=1, keepdims=True)
            p_h, ds_h = [], []
            for sb in range(nsub + 1):
                sc, dp = raw[hh][sb]
                if sb < nsub:
                    msk, lrow, erow = (band_first if sb == 0 else band), rowstat(blk(lse, sb)), rowstat(blk(esum, sb))
                else:
                    msk, lrow, erow = mask_next, rowstat(lse_n), rowstat(esum_n)
                p = jnp.where(msk, jnp.exp(sc * SCALE - lrow), 0.0)
                p_h.append(p.astype(BF))
                ds_h.append((p * (dp - erow)).astype(BF))
            ps.append(p_h)
            dss.append(ds_h)
        dq = [jnp.zeros((QB, GW), F32) for _ in range(nsub)]
        dk = [jnp.zeros((QB, GW), F32) for _ in range(nsub)]
        dv = [jnp.zeros((QB, GW), F32) for _ in range(nsub)]
        for hh in range(NSLOT):
            for sb in range(nsub):
                dq[sb] = dq[sb] + jnp.where(hms[hh], _dot(dss[hh][sb], win(kfull, sb)), 0.0)
                nxt = lambda t: t[sb + 1][:, :QB] if sb + 1 < nsub else t[nsub]
                dk[sb] = dk[sb] + _dot_tn(jnp.concatenate([dss[hh][sb][:, QB:], nxt(dss[hh])], axis=0), win(q_hs[hh], sb))
                dv[sb] = dv[sb] + _dot_tn(jnp.concatenate([ps[hh][sb][:, QB:], nxt(ps[hh])], axis=0), win(dy_hs[hh], sb))
        for sb in range(nsub):
            rows = slice(sb * QB, (sb + 1) * QB)
            o_ref[0, rows, :] = (dq[sb] * SCALE).astype(BF)
            o_ref[1, rows, :] = (dk[sb] * SCALE).astype(BF)
            o_ref[2, rows, :] = dv[sb].astype(BF)

    prev = lambda i: jnp.maximum(i * nsub - 1, 0)
    nxt = lambda i: jnp.minimum((i + 1) * nsub, L // QB - 1)
    cur4 = lambda t: pl.BlockSpec((None, None, R, GW), lambda r, i: (t, r, i, 0))
    prv4 = lambda t: pl.BlockSpec((None, None, QB, GW), lambda r, i: (t, r, prev(i), 0))
    nxt4 = lambda t: pl.BlockSpec((None, None, QB, GW), lambda r, i: (t, r, nxt(i), 0))
    cur3 = pl.BlockSpec((None, R, GW), lambda r, i: (r, i, 0))
    nxt3 = pl.BlockSpec((None, QB, GW), lambda r, i: (r, nxt(i), 0))
    return pl.pallas_call(
        body, name=name, grid=(d, nsteps), out_shape=jax.ShapeDtypeStruct((3, d, L, GW), BF),
        in_specs=[cur4(0), nxt4(0), prv4(1), cur4(1), prv4(2), cur4(2), cur3, nxt3, cur3, nxt3, cur3, nxt3],
        out_specs=pl.BlockSpec((3, None, R, GW), lambda r, i: (0, r, i, 0)),
        compiler_params=_cp(("parallel", "arbitrary")))(qkv, qkv, qkv, qkv, qkv, qkv, dya, dya, dsums, dsums, lse, lse)


def _inproj_bwd(dqkvs, dpuz, dgates, wqkv_t, wuz_t, wg_t, cos_t, sin_t, x, dx2, g1):
    S = x.shape[0]
    tm = min(512, S)

    def body(d0_ref, d1_ref, d2_ref, dp_ref, dg_ref, wqkv_ref, wuz_ref, wg_ref, c_ref, s_ref,
             x_ref, dx2_ref, g1_ref, gx_ref, dg1_ref, dn_ref, scr):
        i = pl.program_id(0)

        @pl.when(i == 0)
        def _():
            dg1_ref[...] = jnp.zeros_like(dg1_ref)

        for t in range(3):
            for g, (d_ref, d) in enumerate(zip((d0_ref, d1_ref, d2_ref), DILS)):
                piece = _undilate_load(d_ref, (t,), d, scr, tm)
                if t < 2:
                    piece = _rope_bwd(piece, c_ref, s_ref)
                dn_ref[:, (3 * t + g) * GW:(3 * t + g + 1) * GW] = piece.astype(BF)
        dh = _dot(dn_ref[...], wqkv_ref[...]) + _dot(dp_ref[...], wuz_ref[...]) + _dot(dg_ref[...], wg_ref[...])
        xv = x_ref[...]
        dx1, dg1 = _rms_bwd(xv, _rms_stats(xv), g1_ref[...], dh)
        gx_ref[...] = dx2_ref[...] + dx1
        dg1_ref[...] += dg1

    row = lambda w: pl.BlockSpec((tm, w), lambda i: (i, 0))
    full = lambda s: pl.BlockSpec(s, lambda i: (0, 0))
    dil = lambda d: pl.BlockSpec((3, d, tm // d, GW), lambda i: (0, 0, i, 0))
    return pl.pallas_call(
        body, name="inproj_bwd", grid=(S // tm,),
        out_shape=(jax.ShapeDtypeStruct((S, D), F32), jax.ShapeDtypeStruct((1, D), F32),
                   jax.ShapeDtypeStruct((S, 3 * AW), BF)),
        in_specs=[dil(d) for d in DILS] + [row(2 * GMLP_W), row(2 * D), full((3 * AW, D)), full((2 * GMLP_W, D)),
                                            full((2 * D, D)), row(128), row(128), row(D), row(D), full((1, D))],
        out_specs=(row(D), full((1, D)), row(3 * AW)),
        scratch_shapes=[pltpu.VMEM((2, tm, 128), F32)],
        compiler_params=_cp(("arbitrary",)))(*dqkvs, dpuz, dgates, wqkv_t, wuz_t, wg_t, cos_t, sin_t, x, dx2, g1)


def _row_tile(rows, cap=256):
    return max(t for t in range(16, cap + 1, 16) if rows % t == 0)


def _adam_math(w, g, m, v):
    m2 = ADAM_B1 * m + (1.0 - ADAM_B1) * g
    v2 = ADAM_B2 * v + (1.0 - ADAM_B2) * (g * g)
    m_hat = m2 / (1.0 - ADAM_B1 ** ADAM_STEP)
    v_hat = v2 / (1.0 - ADAM_B2 ** ADAM_STEP)
    delta = -ADAM_LR * (m_hat / (jnp.sqrt(v_hat) + ADAM_EPS) + ADAM_WD * w)
    return delta, m2, v2


def _adam_shard(own, recv, w, m, v, name):
    R, C = w.shape
    tr = _row_tile(R)

    def body(own_ref, r_ref, w_ref, m_ref, v_ref, g_ref, d_ref, m2_ref, v2_ref):
        g = own_ref[...] + r_ref[0].astype(F32) + r_ref[1].astype(F32) + r_ref[2].astype(F32)
        g_ref[...] = g
        d_ref[...], m2_ref[...], v2_ref[...] = _adam_math(w_ref[...], g, m_ref[...], v_ref[...])

    spec = pl.BlockSpec((tr, C), lambda i: (i, 0))
    out = jax.ShapeDtypeStruct((R, C), F32)
    return pl.pallas_call(
        body, name=name, grid=(R // tr,), out_shape=(out, out, out, out),
        in_specs=[spec, pl.BlockSpec((3, tr, C), lambda i: (0, i, 0)), spec, spec, spec],
        out_specs=(spec, spec, spec, spec), compiler_params=_cp(("parallel",)))(own, recv, w, m, v)


def _rs_add(gblocks, recv, idx, name):
    _, R, C = gblocks.shape
    tr = _row_tile(R)

    def body(t_ref, g_ref, r_ref, own_ref, send_ref):
        j = pl.program_id(1)
        s = g_ref[...] + r_ref[...]

        @pl.when(j == 0)
        def _():
            own_ref[...] = s

        @pl.when(j > 0)
        def _():
            send_ref[...] = s.astype(BF)

    grid_spec = pltpu.PrefetchScalarGridSpec(
        num_scalar_prefetch=1, grid=(R // tr, 4),
        in_specs=[pl.BlockSpec((None, tr, C), lambda i, j, t: (t[j], i, 0)),
                  pl.BlockSpec((None, tr, C), lambda i, j, t: (t[4 + j], i, 0))],
        out_specs=[pl.BlockSpec((tr, C), lambda i, j, t: (i, 0)),
                   pl.BlockSpec((None, tr, C), lambda i, j, t: (jnp.maximum(j - 1, 0), i, 0))])
    return pl.pallas_call(
        body, name=name, grid_spec=grid_spec,
        out_shape=(jax.ShapeDtypeStruct((R, C), F32), jax.ShapeDtypeStruct((3, R, C), BF)),
        compiler_params=_cp(("parallel", "arbitrary")))(idx, gblocks, recv)


def _mesh_pos():
    return lax.axis_index("x"), lax.axis_index("y"), lax.axis_index("c")


_HBM = pl.BlockSpec(memory_space=pltpu.HBM)
_SEM = pl.BlockSpec(memory_space=pltpu.SEMAPHORE)
_EFFECT = pltpu.SideEffectType.DATAFLOW_SIDE_EFFECTING
_RELATIONS = [(dx, dy, dc) for dx in (0, 1) for dy in (0, 1) for dc in (0, 1)][1:]


def _flip(v, d):
    return 1 - v if d else v


def _plan_gather(n):
    def plan(x, y, c):
        return [(k, None, 4 * x + 2 * y + c, (_flip(x, dx), _flip(y, dy), _flip(c, dc)))
                for k in range(n) for dx, dy, dc in _RELATIONS]
    return plan


def _plan_gather_near(x, y, c):
    chips = [(1 - x, y), (x, 1 - y), (1 - x, 1 - y)]
    return [(0, None, 4 * x + 2 * y + c, (x, y, 1 - c))] + [(0, None, 4 * x + 2 * y + c, (*chip, c)) for chip in chips]


def _plan_gather_pass(from_landing):
    def plan(x, y, c):
        blocks = [4 * cx + 2 * cy + c for cx, cy in ((1 - x, y), (x, 1 - y), (1 - x, 1 - y))]
        return [(0, b if from_landing else None, b, (x, y, 1 - c)) for b in blocks]
    return plan


def _plan_d2d(n):
    def plan(x, y, c):
        return [(k, 2 * kk + 1 - c, kk, (x, y, 1 - c)) for k in range(n) for kk in range(4)]
    return plan


def _plan_ici(n):
    def plan(x, y, c):
        return [(k, j, j, (_flip(x, dx), _flip(y, dy), c))
                for k in range(n) for j, (dx, dy) in enumerate(((1, 0), (0, 1), (1, 1)))]
    return plan


def _plan_copies(plan, src_refs, land_refs, send_sems, recv_sems):
    x, y, c = _mesh_pos()
    return [pltpu.make_async_remote_copy(
        src_ref=src_refs[k] if si is None else src_refs[k].at[si], dst_ref=land_refs[k].at[di],
        send_sem=send_sems.at[n], recv_sem=recv_sems.at[n], device_id=dev, device_id_type=MESH)
        for n, (k, si, di, dev) in enumerate(plan(x, y, c))]


def _exchange_start(srcs, land_shapes, plan, ncopies, name, after):
    n = len(srcs)

    def body(*refs):
        src_refs, land_refs = refs[:n], refs[n:2 * n]
        send_sems, recv_sems = refs[2 * n + len(after)], refs[2 * n + len(after) + 1]
        token = refs[-1]
        for cp in _plan_copies(plan, src_refs, land_refs, send_sems, recv_sems):
            cp.start()
        token[...] = jnp.zeros_like(token)

    lands = [pltpu.with_memory_space_constraint(lax.empty(s, a.dtype), pltpu.HBM) for s, a in zip(land_shapes, srcs)]
    srcs = [pltpu.with_memory_space_constraint(a, pltpu.HBM) for a in srcs]
    outs = pl.pallas_call(
        body, name=name,
        out_shape=(pltpu.SemaphoreType.DMA((ncopies,)), pltpu.SemaphoreType.DMA((ncopies,)))
        + tuple(pltpu.HBM(a.shape, a.dtype) for a in srcs) + tuple(pltpu.HBM(a.shape, a.dtype) for a in lands)
        + (jax.ShapeDtypeStruct((8, 128), F32),),
        in_specs=[_HBM] * (2 * n) + [pl.BlockSpec(memory_space=pl.ANY)] * len(after),
        out_specs=(_SEM, _SEM) + (_HBM,) * (2 * n) + (pl.BlockSpec(memory_space=pltpu.VMEM),),
        input_output_aliases={i: 2 + i for i in range(2 * n)},
        compiler_params=pltpu.CompilerParams(has_side_effects=_EFFECT))(*srcs, *lands, *after)
    return (outs[0], outs[1], list(outs[2:2 + n]), list(outs[2 + n:2 + 2 * n])), outs[-1]


def _exchange_forward(handle, plan, plan_fwd, nfwd, name, after):
    send_sems, recv_sems, srcs, lands = handle
    n = len(srcs)

    def body(*refs):
        src_refs, land_refs = refs[:n], refs[n:2 * n]
        outs = refs[2 * n + 2 + len(after):]
        for cp in _plan_copies(plan, src_refs, land_refs, refs[2 * n], refs[2 * n + 1]):
            cp.wait_send()
            cp.wait_recv()
        for cp in _plan_copies(plan_fwd, land_refs, land_refs, outs[0], outs[1]):
            cp.start()

    outs = pl.pallas_call(
        body, name=name,
        out_shape=(pltpu.SemaphoreType.DMA((nfwd,)), pltpu.SemaphoreType.DMA((nfwd,)))
        + tuple(pltpu.HBM(a.shape, a.dtype) for a in srcs) + tuple(pltpu.HBM(a.shape, a.dtype) for a in lands),
        in_specs=[_HBM] * (2 * n) + [_SEM, _SEM] + [pl.BlockSpec(memory_space=pl.ANY)] * len(after),
        out_specs=(_SEM, _SEM) + (_HBM,) * (2 * n), input_output_aliases={i: 2 + i for i in range(2 * n)},
        compiler_params=pltpu.CompilerParams(has_side_effects=_EFFECT))(*srcs, *lands, send_sems, recv_sems, *after)
    return outs[0], outs[1], list(outs[2:2 + n]), list(outs[2 + n:2 + 2 * n])


def _exchange_wait(handle, plan, name, after):
    send_sems, recv_sems, srcs, lands = handle
    n = len(srcs)

    def body(*refs):
        src_refs, land_refs = refs[:n], refs[n:2 * n]
        for cp in _plan_copies(plan, src_refs, land_refs, refs[2 * n], refs[2 * n + 1]):
            cp.wait_send()
            cp.wait_recv()

    outs = pl.pallas_call(
        body, name=name,
        out_shape=tuple(pltpu.HBM(a.shape, a.dtype) for a in srcs) + tuple(pltpu.HBM(a.shape, a.dtype) for a in lands),
        in_specs=[_HBM] * (2 * n) + [_SEM, _SEM] + [pl.BlockSpec(memory_space=pl.ANY)] * len(after),
        out_specs=(_HBM,) * (2 * n), input_output_aliases={i: i for i in range(2 * n)},
        compiler_params=pltpu.CompilerParams(has_side_effects=_EFFECT))(*srcs, *lands, send_sems, recv_sems, *after)
    return list(outs[:n]), list(outs[n:])


SMALL = ("g1", "g2", "g3", "g4", "ln_g", "ln_b", "b_sp", "w_sp")


def _small_step(loss_row, grads, ws, ms, vs):
    parts = [loss_row] + [grads[k] for k in SMALL]
    n, ns = len(parts), len(SMALL)

    def body(*refs):
        p_refs = refs[:n]
        w_refs, m_refs, v_refs = (refs[n + i * ns:n + (i + 1) * ns] for i in range(3))
        o = n + 3 * ns
        loss_ref = refs[o]
        g_out, d_out, m_out, v_out = (refs[o + 1 + i * ns:o + 1 + (i + 1) * ns] for i in range(4))
        bufs = refs[o + 1 + 4 * ns:o + 1 + 4 * ns + n]
        send_sems, recv_sems = refs[-2:]
        x, y, c = _mesh_pos()
        me, sib = (x, y, c), (x, y, 1 - c)
        chips = [(1 - x, y), (x, 1 - y), (1 - x, 1 - y)]

        def rcopy(k, s, block, to, src=None):
            dst = bufs[k].at[4 * block[0] + 2 * block[1] + block[2]]
            return pltpu.make_async_remote_copy(
                src_ref=dst if src is None else src, dst_ref=dst, send_sem=send_sems.at[k, s],
                recv_sem=recv_sems.at[k, s], device_id=to, device_id_type=MESH)

        started = []
        for k in range(n):
            bufs[k][4 * x + 2 * y + c] = p_refs[k][...]
            started.append(rcopy(k, 0, me, sib, src=p_refs[k]))
            started += [rcopy(k, 1 + j, me, (*chip, c), src=p_refs[k]) for j, chip in enumerate(chips)]
        for cp in started:
            cp.start()
        for k in range(n):
            for j, chip in enumerate(chips):
                rcopy(k, 1 + j, (*chip, c), me).wait_recv()
                started.append(rcopy(k, 4 + j, (*chip, c), sib))
                started[-1].start()
        for k in range(n):
            rcopy(k, 0, sib, me).wait_recv()
            for j, chip in enumerate(chips):
                rcopy(k, 4 + j, (*chip, 1 - c), me).wait_recv()
        for cp in started:
            cp.wait_send()
        sums = []
        for k in range(n):
            acc = bufs[k][0]
            for b in range(1, 8):
                acc = acc + bufs[k][b]
            sums.append(acc)
        loss_ref[...] = sums[0]
        for i in range(ns):
            g_out[i][...] = sums[1 + i]
            d_out[i][...], m_out[i][...], v_out[i][...] = _adam_math(w_refs[i][...], sums[1 + i], m_refs[i][...],
                                                                     v_refs[i][...])

    args = parts + [t[k] for t in (ws, ms, vs) for k in SMALL]
    shapes = [jax.ShapeDtypeStruct(loss_row.shape, F32)] + [jax.ShapeDtypeStruct(grads[k].shape, F32) for k in SMALL] * 4
    vmem = pl.BlockSpec(memory_space=pltpu.VMEM)
    outs = pl.pallas_call(
        body, name="small_step", out_shape=tuple(shapes), in_specs=[vmem] * len(args), out_specs=(vmem,) * len(shapes),
        scratch_shapes=[pltpu.VMEM((8,) + p.shape, F32) for p in parts]
        + [pltpu.SemaphoreType.DMA((n, 7)), pltpu.SemaphoreType.DMA((n, 7))],
        compiler_params=pltpu.CompilerParams(vmem_limit_bytes=VMEM_LIMIT))(*args)
    groups = [dict(zip(SMALL, outs[1 + i * ns:1 + (i + 1) * ns])) for i in range(4)]
    return (outs[0], *groups)


def _rope_tables(S):
    half = HD // 2
    inv_freq = jnp.tile(ROPE_THETA ** (-jnp.arange(half, dtype=F32) / half), 4)
    sign = jnp.tile(jnp.concatenate([-jnp.ones(half, F32), jnp.ones(half, F32)]), 2)
    ang = jnp.arange(S, dtype=F32)[:, None] * inv_freq[None, :]
    return jnp.cos(ang), jnp.sin(ang) * sign[None, :]


def _to_blocks(g, col_sharded):
    if col_sharded:
        return g.reshape(g.shape[0], 8, g.shape[1] // 8).transpose(1, 0, 2)
    return g.reshape(8, g.shape[0] // 8, g.shape[1])


def _from_blocks(t, col_sharded):
    if col_sharded:
        return t.transpose(1, 0, 2).reshape(t.shape[1], 8 * t.shape[2])
    return t.reshape(8 * t.shape[1], t.shape[2])


class _NoComm:
    def __init__(self, late_weights):
        self._late = late_weights
        self.grads = {}

    def start_tie(self):
        return jnp.zeros((8, 128), F32)

    def late_weights(self, after):
        return self._late

    def rs_start(self, key, gblocks, after=()):
        self.grads[key] = gblocks
        return jnp.zeros((8, 128), F32)

    def rs_mid(self, key, after):
        return jnp.zeros((8, 128), F32)


class _FsdpComm:
    def __init__(self, late_shards, col_sharded, after, idx, block):
        self._col, self._idx, self._block, self._rs = col_sharded, idx, block, {}
        n = len(late_shards)
        self._gather, self._token = _exchange_start(
            late_shards, [(8,) + s.shape for s in late_shards], _plan_gather(n), 7 * n, "ag_late_start", (after,))

    def start_tie(self):
        return self._token

    def late_weights(self, after):
        shards, lands = _exchange_wait(self._gather, _plan_gather(len(self._col)), "ag_late_wait", after)
        lands = [lax.dynamic_update_index_in_dim(t, s, self._block, 0) for t, s in zip(lands, shards)]
        return [_from_blocks(t, cs) for t, cs in zip(lands, self._col)]

    def rs_start(self, key, gblocks, after=()):
        n = len(gblocks)
        d2d, token = _exchange_start(gblocks, [(4,) + g.shape[1:] for g in gblocks], _plan_d2d(n), 4 * n,
                                     "rs_%s_d2d_start" % key, (self._token,) + tuple(after))
        self._rs[key] = dict(n=n, d2d=d2d)
        return token

    def rs_mid(self, key, after):
        st = self._rs[key]
        gblocks, from_sib = _exchange_wait(st["d2d"], _plan_d2d(st["n"]), "rs_%s_d2d_wait" % key, after)
        halves = [_rs_add(g, r, self._idx, "rs_add_%s_%d" % (key, k)) for k, (g, r) in enumerate(zip(gblocks, from_sib))]
        st["own"] = [own for own, _ in halves]
        sends = [send for _, send in halves]
        st["ici"], token = _exchange_start(sends, [t.shape for t in sends], _plan_ici(st["n"]), 3 * st["n"],
                                           "rs_%s_ici_start" % key, (self._token,))
        return token

    def rs_end(self, key, after):
        st = self._rs[key]
        return st["own"], _exchange_wait(st["ici"], _plan_ici(st["n"]), "rs_%s_ici_wait" % key, after)[1]


def _local_step(x, tgt, h1, cos_t, sin_t, wqkv_t, wuz_t, wg_t, comm, g1, g2, g3, g4, ln_g, ln_b, w_sp, b_sp):
    b_sp_t = b_sp.T
    w_sp_t = w_sp.transpose(0, 2, 1)

    qkvs = [_proj_qkv(h1, wqkv_t, cos_t, sin_t, g, d, "proj_qkv_d%d" % d, comm.start_tie()) for g, d in enumerate(DILS)]
    puz = _proj(h1, wuz_t, "none", "proj_uz", 1024)
    gates = _proj(h1, wg_t, "sigmoid", "proj_gates", 512)
    fwd = [_attn_fwd(t, "attn_fwd_d%d" % d) for t, d in zip(qkvs, DILS)]
    ya, *lses = _attn_combine([o for o, _ in fwd], [l for _, l in fwd])
    yg = _gmlp_fwd(puz, ln_g, ln_b, w_sp, b_sp_t)
    wba, wbg, wout, wmi, wmo = comm.late_weights(after=(ya, yg, gates))
    merged, y, x2, h3 = _merge_fwd(ya, yg, gates, wba, wbg, wout, x, g2, g3)
    a, dy2, dout, loss_row, dg4 = _mlp_fwd(h3, wmi, wmo, x2, tgt, g4)

    dpre, dx2, dy, dg3, dg2 = _mlp_bwd(dy2, a, wmo, wmi, x2, y, dout, g2, g3)
    dwmo = _mm_tn(a, dy2, "dw_mlp_out", square_a=True)
    dwmi = _mm_tn(h3, dpre, "dw_mlp_in", col_blocks=8)
    tie = comm.rs_start("mlp", [dwmi, _to_blocks(dwmo, False)])
    dgates, da, db, dyg, *rest = _outproj_bwd(dy, wout, ya, yg, gates, wba, wbg, tie)
    dyas, dsums = rest[:3], rest[3:]
    tie = comm.rs_mid("mlp", after=(dyg,))
    dpuz, dwsp, dbs, dlng, dlnb = _gmlp_bwd(puz, dyg, ln_g + tie[0, 0], ln_b, w_sp, w_sp_t, b_sp_t)
    dqkvs = [_attn_bwd(qkvs[g], dyas[g], dsums[g], lses[g], "attn_bwd_d%d" % d) for g, d in enumerate(DILS)]
    grad_x, dg1, dqkv = _inproj_bwd(dqkvs, dpuz, dgates, wqkv_t, wuz_t, wg_t, cos_t, sin_t, x, dx2, g1)
    dwin_t = jnp.concatenate([_mm_tn(t, h1, "dw_in_%d" % n) for n, t in enumerate((dqkv, dpuz, dgates))], axis=0)
    dwout = _mm_tn(merged, dy, "dw_out")
    dwba = _mm_tn(ya, da, "dw_branch_attn")
    dwbg = _mm_tn(yg, db, "dw_branch_gmlp")
    tie = comm.rs_start("win", [_to_blocks(dwin_t, False)], after=(dwout, dwba, dwbg))
    small = dict(g1=dg1, g2=dg2, g3=dg3, g4=dg4, ln_g=dlng, ln_b=dlnb, b_sp=dbs.reshape(4, CHUNK),
                 w_sp=dwsp.reshape(4 * CHUNK, CHUNK))
    mid = [_to_blocks(dwba, True), _to_blocks(dwbg, True), _to_blocks(dwout, False)]
    return loss_row + tie[:1], grad_x, small, mid


def kernel(x, norm_pre_mix, w_in, w_spatial, b_spatial, ln_v_gain, ln_v_bias, w_branch_attn, w_branch_gmlp, w_out, norm_post_mix, norm_pre_mlp, w_mlp_in, w_mlp_out, norm_post_mlp, loss_target, m_norm_pre_mix, m_w_in, m_w_spatial, m_b_spatial, m_ln_v_gain, m_ln_v_bias, m_w_branch_attn, m_w_branch_gmlp, m_w_out, m_norm_post_mix, m_norm_pre_mlp, m_w_mlp_in, m_w_mlp_out, m_norm_post_mlp, v_norm_pre_mix, v_w_in, v_w_spatial, v_b_spatial, v_ln_v_gain, v_ln_v_bias, v_w_branch_attn, v_w_branch_gmlp, v_w_out, v_norm_post_mix, v_norm_pre_mlp, v_w_mlp_in, v_w_mlp_out, v_norm_post_mlp):
    mx, my, mc = _mesh_pos()
    rel = [(0, 0), (1, 0), (0, 1), (1, 1)]
    chip_of = [2 * (mx ^ dx) + (my ^ dy) for dx, dy in rel]
    idx = jnp.stack([2 * k + mc for k in chip_of] + chip_of).astype(jnp.int32)

    w_in_t, m_w_in_t, v_w_in_t = (t[0].T for t in (w_in, m_w_in, v_w_in))
    shard = w_in_t.astype(BF)
    gather, token = _exchange_start([shard], [(8,) + shard.shape], _plan_gather_near, 4, "ag_win_start", ())
    h1 = _rms_fwd(x[0], norm_pre_mix + token[0, 0], "rms_pre_mix")
    cos_t, sin_t = _rope_tables(x.shape[1])
    gather = _exchange_forward(gather, _plan_gather_near, _plan_gather_pass(True), 3, "ag_win_pass", (h1, cos_t, sin_t))
    (shard,), (g_win,) = _exchange_wait(gather, _plan_gather_pass(False), "ag_win_wait", ())
    g_win = lax.dynamic_update_index_in_dim(g_win, shard, 4 * mx + 2 * my + mc, 0)
    win_t = _from_blocks(g_win, False)
    wqkv_t, wuz_t, wg_t = win_t[:3 * AW], win_t[3 * AW:3 * AW + 2 * GMLP_W], win_t[3 * AW + 2 * GMLP_W:]
    late = [w_branch_attn[0], w_branch_gmlp[0], w_out[0], w_mlp_in[0], w_mlp_out[0]]
    comm = _FsdpComm([w.astype(BF) for w in late], [True, True, False, True, False], g_win, idx, 4 * mx + 2 * my + mc)

    loss_row, grad_x, small, mid = _local_step(
        x[0], loss_target[0], h1, cos_t, sin_t, wqkv_t, wuz_t, wg_t, comm,
        norm_pre_mix, norm_post_mix, norm_pre_mlp, norm_post_mlp, ln_v_gain, ln_v_bias, w_spatial[0], b_spatial[0])

    flat = lambda t: t.reshape(-1, t.shape[-1])
    small_w = dict(zip(SMALL, map(flat, (norm_pre_mix, norm_post_mix, norm_pre_mlp, norm_post_mlp, ln_v_gain, ln_v_bias,
                                         b_spatial, w_spatial))))
    small_m = dict(zip(SMALL, map(flat, (m_norm_pre_mix, m_norm_post_mix, m_norm_pre_mlp, m_norm_post_mlp, m_ln_v_gain,
                                         m_ln_v_bias, m_b_spatial, m_w_spatial))))
    small_v = dict(zip(SMALL, map(flat, (v_norm_pre_mix, v_norm_post_mix, v_norm_pre_mlp, v_norm_post_mlp, v_ln_v_gain,
                                         v_ln_v_bias, v_b_spatial, v_w_spatial))))
    loss_out, sg, sd, sm, sv = _small_step(loss_row, small, small_w, small_m, small_v)
    tie = comm.rs_mid("win", after=(loss_out,))
    tie = comm.rs_start("mid", mid, after=(tie,))
    mlp_own, mlp_chips = comm.rs_end("mlp", after=(tie,))
    adam = lambda nm, own, r, w, m, v: _adam_shard(own, r, w[0], m[0], v[0], "adam_" + nm)
    upd = {
        "w_mlp_in": adam("w_mlp_in", mlp_own[0], mlp_chips[0], w_mlp_in, m_w_mlp_in, v_w_mlp_in),
        "w_mlp_out": adam("w_mlp_out", mlp_own[1], mlp_chips[1], w_mlp_out, m_w_mlp_out, v_w_mlp_out),
    }
    loss = loss_out[0, 0]
    tie = comm.rs_mid("mid", after=(upd["w_mlp_in"][0], upd["w_mlp_out"][0]))
    win_own, win_chips = comm.rs_end("win", after=(tie,))
    upd["w_in"] = tuple(t.T for t in _adam_shard(win_own[0], win_chips[0], w_in_t, m_w_in_t, v_w_in_t, "adam_w_in"))
    mid_own, mid_chips = comm.rs_end("mid", after=(upd["w_in"][0],))
    upd["w_branch_attn"] = adam("w_branch_attn", mid_own[0], mid_chips[0], w_branch_attn, m_w_branch_attn, v_w_branch_attn)
    upd["w_branch_gmlp"] = adam("w_branch_gmlp", mid_own[1], mid_chips[1], w_branch_gmlp, m_w_branch_gmlp, v_w_branch_gmlp)
    upd["w_out"] = adam("w_out", mid_own[2], mid_chips[2], w_out, m_w_out, v_w_out)
    order = ["g1", "w_in", "w_sp", "b_sp", "ln_g", "ln_b", "w_branch_attn", "w_branch_gmlp", "w_out", "g2", "g3",
             "w_mlp_in", "w_mlp_out", "g4"]
    small_shape = dict(g1=norm_pre_mix.shape, g2=norm_post_mix.shape, g3=norm_pre_mlp.shape, g4=norm_post_mlp.shape,
                       ln_g=ln_v_gain.shape, ln_b=ln_v_bias.shape, b_sp=b_spatial.shape, w_sp=w_spatial.shape)

    def pick(which):
        return [upd[nm][which][None] if nm in upd else (sg, sd, sm, sv)[which][nm].reshape(small_shape[nm])
                for nm in order]

    return (loss, grad_x[None], *pick(0), *pick(1), *pick(2), *pick(3))
```

```python
import functools
import math

import jax
import jax.numpy as jnp
from jax import lax
from jax.experimental import pallas as pl
from jax.experimental.pallas import tpu as pltpu

D = 1024
HD = 64
NSLOT = 4
GW = NSLOT * HD
DILS = (1, 4, 16)
QB = 128
ATTN_NSUB = 4
PROJ_QKV_TM = 1024
MLP_TM_FWD = 512
MLP_TM_BWD = 256
AW = 3 * GW
GMLP_W = 512
CHUNK = 128
DFF = 4096
EPS = 1e-6
ROPE_THETA = 10000.0
SCALE = HD ** -0.5
NEG = -1e30

ADAM_LR = 0.001
ADAM_B1 = 0.9
ADAM_B2 = 0.999
ADAM_EPS = 1e-08
ADAM_WD = 0.01
ADAM_STEP = 10

BF = jnp.bfloat16
F32 = jnp.float32
MESH = pl.DeviceIdType.MESH
VMEM_LIMIT = 56 * 1024 * 1024


def _cp(sem):
    return pltpu.CompilerParams(dimension_semantics=sem, vmem_limit_bytes=VMEM_LIMIT)


def _dot(a, b):
    return jnp.dot(a, b, preferred_element_type=F32)


def _dot_nt(a, b):
    return lax.dot_general(a, b, (((1,), (1,)), ((), ())), preferred_element_type=F32)


def _dot_tn(a, b):
    return lax.dot_general(a, b, (((0,), (0,)), ((), ())), preferred_element_type=F32)


def _gelu(x):
    return jax.nn.gelu(x, approximate=True)


def _gelu_grad(x):
    k = math.sqrt(2.0 / math.pi)
    t = jnp.tanh(k * (x + 0.044715 * x * x * x))
    return 0.5 * (1.0 + t) + 0.5 * x * (1.0 - t * t) * (k * (1.0 + 3.0 * 0.044715 * x * x))


def _swap_halves(t):
    w = t.shape[1]
    lane = lax.broadcasted_iota(jnp.int32, t.shape, 1)
    first = (lane & (HD - 1)) < (HD // 2)
    return jnp.where(first, pltpu.roll(t, w - HD // 2, 1), pltpu.roll(t, HD // 2, 1))


def _head_mask(hh):
    lane = lax.broadcasted_iota(jnp.int32, (1, GW), 1)
    return jnp.logical_and(lane >= hh * HD, lane < (hh + 1) * HD)


def _rms_stats(xf):
    return lax.rsqrt(jnp.mean(xf * xf, axis=-1, keepdims=True) + EPS)


def _rms_bwd(xf, r, gain, dout):
    n = xf * r
    t = dout * gain
    dx = r * (t - n * jnp.mean(t * n, axis=-1, keepdims=True))
    return dx, jnp.sum(dout * n, axis=0, keepdims=True)


def _rms_fwd(x, gain, name):
    S = x.shape[0]
    tm = min(512, S)

    def body(x_ref, g_ref, h_ref):
        xf = x_ref[...]
        h_ref[...] = (xf * _rms_stats(xf) * g_ref[...]).astype(BF)

    return pl.pallas_call(
        body, name=name, out_shape=jax.ShapeDtypeStruct((S, D), BF), grid=(S // tm,),
        in_specs=[pl.BlockSpec((tm, D), lambda i: (i, 0)), pl.BlockSpec((1, D), lambda i: (0, 0))],
        out_specs=pl.BlockSpec((tm, D), lambda i: (i, 0)), compiler_params=_cp(("parallel",)))(x, gain)


def _proj(h, w_t, epi, name, tm):
    S, K = h.shape
    N = w_t.shape[0]
    tm = min(tm, S)

    def body(h_ref, w_ref, o_ref):
        y = _dot_nt(h_ref[...], w_ref[...])
        if epi == "sigmoid":
            y = jax.nn.sigmoid(y)
        o_ref[...] = y.astype(BF)

    return pl.pallas_call(
        body, name=name, out_shape=jax.ShapeDtypeStruct((S, N), BF), grid=(S // tm,),
        in_specs=[pl.BlockSpec((tm, K), lambda i: (i, 0)), _resident((N, K))],
        out_specs=pl.BlockSpec((tm, N), lambda i: (i, 0)),
        compiler_params=_cp(("parallel",)))(h, w_t)


def _dilate_store(val, scr, o_ref, lead, d):
    rows = val.shape[0]
    if d == 1:
        o_ref[lead + (0,)] = val.astype(o_ref.dtype)
        return
    for hf in range(2):
        scr[hf, pl.ds(0, rows), :] = val[:, hf * 128:(hf + 1) * 128]
    for r in range(d):
        for hf in range(2):
            o_ref[lead + (r, slice(None), slice(hf * 128, (hf + 1) * 128))] = (
                scr[hf, pl.ds(r, rows // d, stride=d), :].astype(o_ref.dtype))


def _undilate_load(i_ref, lead, d, scr, rows):
    if d == 1:
        return i_ref[lead + (0,)].astype(F32)
    for r in range(d):
        for hf in range(2):
            scr[hf, pl.ds(r, rows // d, stride=d), :] = (
                i_ref[lead + (r, slice(None), slice(hf * 128, (hf + 1) * 128))].astype(F32))
    return jnp.concatenate([scr[0, pl.ds(0, rows), :], scr[1, pl.ds(0, rows), :]], axis=1)


def _rope_fwd(y, c_ref, s_ref):
    cosv = jnp.concatenate([c_ref[...]] * 2, axis=1)
    sinv = jnp.concatenate([s_ref[...]] * 2, axis=1)
    return y * cosv + _swap_halves(y) * sinv


def _rope_bwd(dy, c_ref, s_ref):
    cosv = jnp.concatenate([c_ref[...]] * 2, axis=1)
    sinv = jnp.concatenate([s_ref[...]] * 2, axis=1)
    return dy * cosv + _swap_halves(dy * sinv)


def _proj_qkv(h, wqkv_t, cos_t, sin_t, g, d, name, tie):
    S, K = h.shape
    tm = min(PROJ_QKV_TM, S)

    def body(h_ref, wq_ref, wk_ref, wv_ref, c_ref, s_ref, tie_ref, o_ref, scr):
        hv = h_ref[...]
        ys = [_dot_nt(hv, w_ref[...]) for w_ref in (wq_ref, wk_ref, wv_ref)]
        for t in range(3):
            y = _rope_fwd(ys[t], c_ref, s_ref) if t < 2 else ys[t]
            _dilate_store(y, scr.at[t], o_ref, (t,), d)

    w_spec = lambda t: pl.BlockSpec((GW, K), lambda i: (3 * t + g, 0))
    return pl.pallas_call(
        body, name=name, out_shape=jax.ShapeDtypeStruct((3, d, S // d, GW), BF), grid=(S // tm,),
        in_specs=[pl.BlockSpec((tm, K), lambda i: (i, 0)), w_spec(0), w_spec(1), w_spec(2),
                  pl.BlockSpec((tm, 128), lambda i: (i, 0)), pl.BlockSpec((tm, 128), lambda i: (i, 0)),
                  pl.BlockSpec(memory_space=pl.ANY)],
        out_specs=pl.BlockSpec((3, d, tm // d, GW), lambda i: (0, 0, i, 0)),
        scratch_shapes=[pltpu.VMEM((3, 2, tm, 128), F32)],
        compiler_params=_cp(("parallel",)))(h, wqkv_t, wqkv_t, wqkv_t, cos_t, sin_t, tie)


def _band_masks(first_step):
    row = lax.broadcasted_iota(jnp.int32, (QB, 2 * QB), 0)
    col = lax.broadcasted_iota(jnp.int32, (QB, 2 * QB), 1)
    band = jnp.logical_and(col >= row, col <= row + QB)
    return band, jnp.logical_and(band, jnp.logical_or(col >= QB, jnp.logical_not(first_step)))


def _attn_fwd(qkv, name):
    _, d, L, _ = qkv.shape
    nsub = min(ATTN_NSUB, L // QB)
    R = nsub * QB
    nsteps = L // R

    def body(q_ref, kp_ref, kc_ref, vp_ref, vc_ref, o_ref, lse_ref):
        i = pl.program_id(1)
        band, band_first = _band_masks(i == 0)
        kfull = jnp.concatenate([kp_ref[...], kc_ref[...]], axis=0)
        vfull = jnp.concatenate([vp_ref[...], vc_ref[...]], axis=0)
        chains = [(sb, hh) for sb in range(nsub) for hh in range(NSLOT)]
        win = lambda t, sb: t[sb * QB:(sb + 2) * QB]
        scores = []
        for sb, hh in chains:
            qh = jnp.where(_head_mask(hh), q_ref[sb * QB:(sb + 1) * QB, :], 0)
            scores.append(_dot_nt(qh, win(kfull, sb)))
        soft = []
        for (sb, hh), sc in zip(chains, scores):
            sc = jnp.where(band_first if sb == 0 else band, sc * SCALE, NEG)
            m = jnp.max(sc, axis=1, keepdims=True)
            p = jnp.exp(sc - m)
            den = jnp.sum(p, axis=1, keepdims=True)
            soft.append((p.astype(BF), den, m + jnp.log(den)))
        accs = [_dot(p, win(vfull, sb)) for (sb, hh), (p, _, _) in zip(chains, soft)]
        for sb in range(nsub):
            o = jnp.zeros((QB, GW), F32)
            lse = jnp.zeros((QB, GW), F32)
            for hh in range(NSLOT):
                hm = _head_mask(hh)
                _, den, lrow = soft[sb * NSLOT + hh]
                o = o + jnp.where(hm, accs[sb * NSLOT + hh] / den, 0.0)
                lse = lse + jnp.where(hm, lrow, 0.0)
            o_ref[sb * QB:(sb + 1) * QB, :] = o
            lse_ref[sb * QB:(sb + 1) * QB, :] = lse

    prev = lambda i: jnp.maximum(i * nsub - 1, 0)
    cur = lambda t: pl.BlockSpec((None, None, R, GW), lambda r, i: (t, r, i, 0))
    prv = lambda t: pl.BlockSpec((None, None, QB, GW), lambda r, i: (t, r, prev(i), 0))
    out = pl.BlockSpec((None, R, GW), lambda r, i: (r, i, 0))
    return pl.pallas_call(
        body, name=name, grid=(d, nsteps),
        out_shape=(jax.ShapeDtypeStruct((d, L, GW), F32), jax.ShapeDtypeStruct((d, L, GW), F32)),
        in_specs=[cur(0), prv(1), cur(1), prv(2), cur(2)],
        out_specs=(out, out), compiler_params=_cp(("parallel", "arbitrary")))(qkv, qkv, qkv, qkv, qkv)


def _attn_combine(os_, lses):
    S = os_[0].shape[1]
    tm = min(1024, S)

    def body(o0, o1, o2, l0, l1, l2, y_ref, j0, j1, j2, scr):
        os_nat = [_undilate_load(o, (), d, scr, tm) for o, d in zip((o0, o1, o2), DILS)]
        a, b, c = [_undilate_load(l, (), d, scr, tm) for l, d in zip((l0, l1, l2), DILS)]
        m = jnp.maximum(jnp.maximum(a, b), c)
        wa, wb, wc = jnp.exp(a - m), jnp.exp(b - m), jnp.exp(c - m)
        den = wa + wb + wc
        y_ref[...] = ((wa * os_nat[0] + wb * os_nat[1] + wc * os_nat[2]) / den).astype(BF)
        lse = m + jnp.log(den)
        for j_ref, d in zip((j0, j1, j2), DILS):
            _dilate_store(lse, scr, j_ref, (), d)

    dil = lambda d: pl.BlockSpec((d, tm // d, GW), lambda i: (0, i, 0))
    dshape = lambda d: jax.ShapeDtypeStruct((d, S // d, GW), F32)
    return pl.pallas_call(
        body, name="attn_combine", grid=(S // tm,),
        out_shape=(jax.ShapeDtypeStruct((S, GW), BF),) + tuple(dshape(d) for d in DILS),
        in_specs=[dil(d) for d in DILS] * 2,
        out_specs=(pl.BlockSpec((tm, GW), lambda i: (i, 0)),) + tuple(dil(d) for d in DILS),
        scratch_shapes=[pltpu.VMEM((2, tm, 128), F32)],
        compiler_params=_cp(("parallel",)))(*os_, *lses)


def _tril(upper=False):
    row = lax.broadcasted_iota(jnp.int32, (CHUNK, CHUNK), 0)
    col = lax.broadcasted_iota(jnp.int32, (CHUNK, CHUNK), 1)
    return row <= col if upper else col <= row


def _ln_fwd(z, gain, bias):
    mu = jnp.mean(z, axis=-1, keepdims=True)
    zc = z - mu
    rstd = lax.rsqrt(jnp.mean(zc * zc, axis=-1, keepdims=True) + EPS)
    zhat = zc * rstd
    return zhat, rstd, zhat * gain + bias


def _gmlp_fwd(puz, ln_g, ln_b, w_sp, b_sp_t):
    S = puz.shape[0]
    tm = min(512, S)
    nch = tm // CHUNK

    def body(p_ref, g_ref, b_ref, w_ref, bt_ref, o_ref):
        tril = _tril()
        ws = [jnp.where(tril, w_ref[gg], 0.0).astype(BF) for gg in range(4)]
        for ch in range(nch):
            rows = slice(ch * CHUNK, (ch + 1) * CHUNK)
            z = _gelu(p_ref[rows, GMLP_W:].astype(F32))
            _, _, zn = _ln_fwd(z, g_ref[...], b_ref[...])
            zn = zn.astype(BF)
            for gg in range(4):
                cols = slice(gg * CHUNK, (gg + 1) * CHUNK)
                sz = _dot(ws[gg], zn[:, cols]) + bt_ref[:, gg:gg + 1]
                u = _gelu(p_ref[rows, cols].astype(F32))
                o_ref[rows, cols] = (u * sz).astype(BF)

    return pl.pallas_call(
        body, name="gmlp_fwd", out_shape=jax.ShapeDtypeStruct((S, GMLP_W), BF), grid=(S // tm,),
        in_specs=[pl.BlockSpec((tm, 2 * GMLP_W), lambda i: (i, 0)),
                  pl.BlockSpec((1, GMLP_W), lambda i: (0, 0)), pl.BlockSpec((1, GMLP_W), lambda i: (0, 0)),
                  pl.BlockSpec((4, CHUNK, CHUNK), lambda i: (0, 0, 0)), pl.BlockSpec((CHUNK, 4), lambda i: (0, 0))],
        out_specs=pl.BlockSpec((tm, GMLP_W), lambda i: (i, 0)),
        compiler_params=_cp(("parallel",)))(puz, ln_g, ln_b, w_sp, b_sp_t)


def _merge_fwd(ya, yg, gates, wba, wbg, wout, x, g2, g3):
    S = x.shape[0]
    tm = min(512, S)

    def body(ya_ref, yg_ref, gt_ref, wba_ref, wbg_ref, wo_ref, x_ref, g2_ref, g3_ref,
             mg_ref, y_ref, x2_ref, h3_ref):
        a = _dot(ya_ref[...], wba_ref[...])
        b = _dot(yg_ref[...], wbg_ref[...])
        merged = (gt_ref[:, :D].astype(F32) * a + gt_ref[:, D:].astype(F32) * b).astype(BF)
        mg_ref[...] = merged
        y = _dot(merged, wo_ref[...])
        y_ref[...] = y
        x2 = x_ref[...] + y * _rms_stats(y) * g2_ref[...]
        x2_ref[...] = x2
        h3_ref[...] = (x2 * _rms_stats(x2) * g3_ref[...]).astype(BF)

    row = lambda w: pl.BlockSpec((tm, w), lambda i: (i, 0))
    full = lambda s: pl.BlockSpec(s, lambda i: (0, 0))
    return pl.pallas_call(
        body, name="merge_fwd", grid=(S // tm,),
        out_shape=(jax.ShapeDtypeStruct((S, D), BF), jax.ShapeDtypeStruct((S, D), F32), jax.ShapeDtypeStruct((S, D), F32),
                   jax.ShapeDtypeStruct((S, D), BF)),
        in_specs=[row(GW), row(GMLP_W), row(2 * D), full((GW, D)), full((GMLP_W, D)), full((D, D)), row(D),
                  full((1, D)), full((1, D))],
        out_specs=(row(D), row(D), row(D), row(D)),
        compiler_params=_cp(("parallel",)))(ya, yg, gates, wba, wbg, wout, x, g2, g3)


def _resident(shape):
    return pl.BlockSpec(shape, lambda i: (0,) * len(shape), pipeline_mode=pl.Buffered(1))


def _mlp_fwd(h3, wmi, wmo, x2, tgt, g4):
    S = x2.shape[0]
    tm = min(MLP_TM_FWD, S)

    def body(h_ref, wi_ref, wo_ref, x2_ref, t_ref, g4_ref, a_ref, dy2_ref, dout_ref, loss_ref, dg4_ref):
        @pl.when(pl.program_id(0) == 0)
        def _():
            loss_ref[...] = jnp.zeros_like(loss_ref)
            dg4_ref[...] = jnp.zeros_like(dg4_ref)

        halves = [slice(hh * (tm // 2), (hh + 1) * (tm // 2)) for hh in range(2)]
        acts = []
        for rows in halves:
            a = jnp.maximum(_dot(h_ref[rows, :], wi_ref[...]), 0.0)
            a_ref[rows, :] = a.astype(BF)
            acts.append((a * a).astype(BF))
        y2s = [_dot(a2, wo_ref[...]) for a2 in acts]
        lane = lax.broadcasted_iota(jnp.int32, (1, 128), 1)
        for rows, y2 in zip(halves, y2s):
            r = _rms_stats(y2)
            out = x2_ref[rows, :] + y2 * r * g4_ref[...]
            err = out - t_ref[rows, :]
            tot = jnp.sum(jnp.sum(err * err, axis=1, keepdims=True), axis=0, keepdims=True) * (0.5 / D)
            loss_ref[...] += jnp.where(lane == 0, tot, 0.0)
            dout = err * (1.0 / D)
            dout_ref[rows, :] = dout
            dy2, dg = _rms_bwd(y2, r, g4_ref[...], dout)
            dy2_ref[rows, :] = dy2.astype(BF)
            dg4_ref[...] += dg

    row = pl.BlockSpec((tm, D), lambda i: (i, 0))
    return pl.pallas_call(
        body, name="mlp_fwd", grid=(S // tm,),
        out_shape=(jax.ShapeDtypeStruct((S, DFF), BF), jax.ShapeDtypeStruct((S, D), BF), jax.ShapeDtypeStruct((S, D), F32),
                   jax.ShapeDtypeStruct((1, 128), F32), jax.ShapeDtypeStruct((1, D), F32)),
        in_specs=[row, _resident((D, DFF)), _resident((DFF, D)), row, row, pl.BlockSpec((1, D), lambda i: (0, 0))],
        out_specs=(pl.BlockSpec((tm, DFF), lambda i: (i, 0)), row, row,
                   pl.BlockSpec((1, 128), lambda i: (0, 0)), pl.BlockSpec((1, D), lambda i: (0, 0))),
        compiler_params=_cp(("arbitrary",)))(h3, wmi, wmo, x2, tgt, g4)


def _mlp_bwd(dy2, a, wmo, wmi, x2, y, dout, g2, g3):
    S = x2.shape[0]
    tm = min(MLP_TM_BWD, S)

    def body(dy2_ref, a_ref, wo_ref, wi_ref, x2_ref, y_ref, dout_ref, g2_ref, g3_ref,
             dpre_ref, dx2_ref, dy_ref, dg3_ref, dg2_ref):
        @pl.when(pl.program_id(0) == 0)
        def _():
            dg3_ref[...] = jnp.zeros_like(dg3_ref)
            dg2_ref[...] = jnp.zeros_like(dg2_ref)

        da2 = _dot_nt(dy2_ref[...], wo_ref[...])
        dpre = (2.0 * a_ref[...].astype(F32) * da2).astype(BF)
        dpre_ref[...] = dpre
        dh3 = _dot_nt(dpre, wi_ref[...])
        x2 = x2_ref[...]
        dx3, dg3 = _rms_bwd(x2, _rms_stats(x2), g3_ref[...], dh3)
        dx2 = dout_ref[...] + dx3
        dx2_ref[...] = dx2
        dg3_ref[...] += dg3
        yv = y_ref[...]
        dy, dg2 = _rms_bwd(yv, _rms_stats(yv), g2_ref[...], dx2)
        dy_ref[...] = dy.astype(BF)
        dg2_ref[...] += dg2

    row = pl.BlockSpec((tm, D), lambda i: (i, 0))
    wide = pl.BlockSpec((tm, DFF), lambda i: (i, 0))
    vec = pl.BlockSpec((1, D), lambda i: (0, 0))
    return pl.pallas_call(
        body, name="mlp_bwd", grid=(S // tm,),
        out_shape=(jax.ShapeDtypeStruct((S, DFF), BF), jax.ShapeDtypeStruct((S, D), F32), jax.ShapeDtypeStruct((S, D), BF),
                   jax.ShapeDtypeStruct((1, D), F32), jax.ShapeDtypeStruct((1, D), F32)),
        in_specs=[row, wide, _resident((DFF, D)), _resident((D, DFF)), row, row, row, vec, vec],
        out_specs=(wide, row, row, vec, vec),
        compiler_params=_cp(("arbitrary",)))(dy2, a, wmo, wmi, x2, y, dout, g2, g3)


def _mm_tn(a, b, name, square_a=False, tm=1024, tn=1024, tk=2048, tie=None, col_blocks=None):
    S, M = a.shape
    N = b.shape[1]
    tk = min(tk, S)
    tm = max(t for t in range(128, min(tm, M) + 1, 128) if M % t == 0)
    tn = max(t for t in range(128, min(tn, N) + 1, 128) if N % t == 0)
    cb = N // col_blocks if col_blocks else tn
    assert M % tm == 0 and S % tk == 0 and tn % cb == 0
    nk = S // tk
    ties = () if tie is None else (tie,)

    def body(a_ref, b_ref, *rest):
        o_ref = rest[-1]
        k = pl.program_id(2)
        av = a_ref[...]
        if square_a:
            av = av * av
        part = _dot_tn(av, b_ref[...])
        if col_blocks:
            part = jnp.stack([part[:, t * cb:(t + 1) * cb] for t in range(tn // cb)])

        @pl.when(k == 0)
        def _():
            o_ref[...] = part

        @pl.when(k > 0)
        def _():
            o_ref[...] += part

    if col_blocks:
        out_shape, out_spec = (col_blocks, M, cb), pl.BlockSpec((tn // cb, tm, cb), lambda i, j, k: (j, i, 0))
    else:
        out_shape, out_spec = (M, N), pl.BlockSpec((tm, tn), lambda i, j, k: (i, j))
    return pl.pallas_call(
        body, name=name, out_shape=jax.ShapeDtypeStruct(out_shape, F32), grid=(M // tm, N // tn, nk),
        in_specs=[pl.BlockSpec((tk, tm), lambda i, j, k: (k, i)), pl.BlockSpec((tk, tn), lambda i, j, k: (k, j))]
        + [pl.BlockSpec(memory_space=pl.ANY)] * len(ties),
        out_specs=out_spec,
        compiler_params=_cp(("parallel", "parallel", "arbitrary")))(a, b, *ties)


def _outproj_bwd(dy, wout, ya, yg, gates, wba, wbg, tie):
    S = dy.shape[0]
    tm = min(512, S)

    def body(dy_ref, wo_ref, ya_ref, yg_ref, gt_ref, wba_ref, wbg_ref, tie_ref,
             dgt_ref, da_ref, db_ref, dyg_ref, e0, e1, e2, s0, s1, s2, scr):
        dm = _dot_nt(dy_ref[...], wo_ref[...])
        ga, gb = gt_ref[:, :D].astype(F32), gt_ref[:, D:].astype(F32)
        dgt_ref[:, :D] = (dm * _dot(ya_ref[...], wba_ref[...]) * ga * (1.0 - ga)).astype(BF)
        dgt_ref[:, D:] = (dm * _dot(yg_ref[...], wbg_ref[...]) * gb * (1.0 - gb)).astype(BF)
        da = (dm * ga).astype(BF)
        db = (dm * gb).astype(BF)
        da_ref[...] = da
        db_ref[...] = db
        dyg_ref[...] = _dot_nt(db, wbg_ref[...]).astype(BF)
        dya = _dot_nt(da, wba_ref[...]).astype(BF).astype(F32)
        dyy = dya * ya_ref[...].astype(F32)
        dsum = jnp.zeros((tm, GW), F32)
        for hh in range(NSLOT):
            hm = _head_mask(hh)
            dsum = dsum + jnp.where(hm, jnp.sum(jnp.where(hm, dyy, 0.0), axis=1, keepdims=True), 0.0)
        for e_ref, s_ref, d in zip((e0, e1, e2), (s0, s1, s2), DILS):
            _dilate_store(dya, scr, e_ref, (), d)
            _dilate_store(dsum, scr, s_ref, (), d)

    row = lambda w: pl.BlockSpec((tm, w), lambda i: (i, 0))
    full = lambda s: pl.BlockSpec(s, lambda i: (0, 0))
    dil = lambda d: pl.BlockSpec((d, tm // d, GW), lambda i: (0, i, 0))
    dshape = lambda d, t: jax.ShapeDtypeStruct((d, S // d, GW), t)
    return pl.pallas_call(
        body, name="outproj_bwd", grid=(S // tm,),
        out_shape=(jax.ShapeDtypeStruct((S, 2 * D), BF), jax.ShapeDtypeStruct((S, D), BF), jax.ShapeDtypeStruct((S, D), BF),
                   jax.ShapeDtypeStruct((S, GMLP_W), BF)) + tuple(dshape(d, BF) for d in DILS)
        + tuple(dshape(d, F32) for d in DILS),
        in_specs=[row(D), full((D, D)), row(GW), row(GMLP_W), row(2 * D), full((GW, D)), full((GMLP_W, D)),
                  pl.BlockSpec(memory_space=pl.ANY)],
        out_specs=(row(2 * D), row(D), row(D), row(GMLP_W)) + tuple(dil(d) for d in DILS) * 2,
        scratch_shapes=[pltpu.VMEM((2, tm, 128), F32)],
        compiler_params=_cp(("parallel",)))(dy, wout, ya, yg, gates, wba, wbg, tie)


def _gmlp_bwd(puz, dyg, ln_g, ln_b, w_sp, w_sp_t, b_sp_t):
    S = puz.shape[0]
    tm = min(512, S)
    nch = tm // CHUNK

    def body(p_ref, dy_ref, g_ref, b_ref, w_ref, wt_ref, bt_ref,
             dp_ref, dw_ref, dbs_ref, dg_ref, dbias_ref, dbacc_ref):
        i = pl.program_id(0)

        @pl.when(i == 0)
        def _():
            dw_ref[...] = jnp.zeros_like(dw_ref)
            dbacc_ref[...] = jnp.zeros_like(dbacc_ref)
            dg_ref[...] = jnp.zeros_like(dg_ref)
            dbias_ref[...] = jnp.zeros_like(dbias_ref)

        tril = _tril()
        ws = [jnp.where(tril, w_ref[gg], 0.0).astype(BF) for gg in range(4)]
        triu = _tril(upper=True)
        wts = [jnp.where(triu, wt_ref[gg], 0.0).astype(BF) for gg in range(4)]
        gain = g_ref[...]
        for ch in range(nch):
            rows = slice(ch * CHUNK, (ch + 1) * CHUNK)
            pz = p_ref[rows, GMLP_W:].astype(F32)
            z = _gelu(pz)
            zhat, rstd, zn = _ln_fwd(z, gain, b_ref[...])
            znb = zn.astype(BF)
            dzn_parts = []
            for gg in range(4):
                cols = slice(gg * CHUNK, (gg + 1) * CHUNK)
                pu = p_ref[rows, cols].astype(F32)
                u = _gelu(pu)
                sz = _dot(ws[gg], znb[:, cols]) + bt_ref[:, gg:gg + 1]
                dyv = dy_ref[rows, cols].astype(F32)
                dp_ref[rows, cols] = (dyv * sz * _gelu_grad(pu)).astype(BF)
                dsz = dyv * u
                dbacc_ref[gg] += dsz
                dszb = dsz.astype(BF)
                dw_ref[gg] += _dot_nt(dszb, znb[:, cols])
                dzn_parts.append(_dot(wts[gg], dszb))
            dzn = jnp.concatenate(dzn_parts, axis=1)
            dg_ref[...] += jnp.sum(dzn * zhat, axis=0, keepdims=True)
            dbias_ref[...] += jnp.sum(dzn, axis=0, keepdims=True)
            dzh = dzn * gain
            dz = rstd * (dzh - jnp.mean(dzh, axis=-1, keepdims=True)
                         - zhat * jnp.mean(dzh * zhat, axis=-1, keepdims=True))
            dp_ref[rows, GMLP_W:] = (dz * _gelu_grad(pz)).astype(BF)

        @pl.when(i == pl.num_programs(0) - 1)
        def _():
            for gg in range(4):
                dw_ref[gg] = jnp.where(tril, dw_ref[gg], 0.0)
                dbs_ref[gg] = jnp.sum(dbacc_ref[gg], axis=1, keepdims=True)

    full2 = lambda s: pl.BlockSpec(s, lambda i: (0, 0))
    full3 = lambda s: pl.BlockSpec(s, lambda i: (0, 0, 0))
    return pl.pallas_call(
        body, name="gmlp_bwd", grid=(S // tm,),
        out_shape=(jax.ShapeDtypeStruct((S, 2 * GMLP_W), BF), jax.ShapeDtypeStruct((4, CHUNK, CHUNK), F32),
                   jax.ShapeDtypeStruct((4, CHUNK, 1), F32), jax.ShapeDtypeStruct((1, GMLP_W), F32),
                   jax.ShapeDtypeStruct((1, GMLP_W), F32)),
        in_specs=[pl.BlockSpec((tm, 2 * GMLP_W), lambda i: (i, 0)), pl.BlockSpec((tm, GMLP_W), lambda i: (i, 0)),
                  full2((1, GMLP_W)), full2((1, GMLP_W)), full3((4, CHUNK, CHUNK)), full3((4, CHUNK, CHUNK)),
                  full2((CHUNK, 4))],
        out_specs=(pl.BlockSpec((tm, 2 * GMLP_W), lambda i: (i, 0)), full3((4, CHUNK, CHUNK)), full3((4, CHUNK, 1)),
                   full2((1, GMLP_W)), full2((1, GMLP_W))),
        scratch_shapes=[pltpu.VMEM((4, CHUNK, CHUNK), F32)],
        compiler_params=_cp(("arbitrary",)))(puz, dyg, ln_g, ln_b, w_sp, w_sp_t, b_sp_t)


def _attn_bwd(qkv, dya, dsums, lse, name):
    _, d, L, _ = qkv.shape
    nsub = min(ATTN_NSUB, L // QB)
    R = nsub * QB
    nsteps = L // R

    def body(q_ref, qn_ref, kp_ref, kc_ref, vp_ref, vc_ref, dy_ref, dyn_ref, e_ref, en_ref, l_ref, ln_ref, o_ref):
        i = pl.program_id(1)
        band, band_first = _band_masks(i == 0)
        row = lax.broadcasted_iota(jnp.int32, (QB, QB), 0)
        col = lax.broadcasted_iota(jnp.int32, (QB, QB), 1)
        mask_next = jnp.logical_and(col >= row, i < nsteps - 1)
        kc, vc = kc_ref[...], vc_ref[...]
        kfull = jnp.concatenate([kp_ref[...], kc], axis=0)
        vfull = jnp.concatenate([vp_ref[...], vc], axis=0)
        k_last, v_last = kc[(nsub - 1) * QB:], vc[(nsub - 1) * QB:]
        q_ext = jnp.concatenate([q_ref[...], qn_ref[...]], axis=0)
        dy_ext = jnp.concatenate([dy_ref[...], dyn_ref[...]], axis=0)
        esum, esum_n, lse, lse_n = e_ref[...], en_ref[...], l_ref[...], ln_ref[...]
        win = lambda t, sb: t[sb * QB:(sb + 2) * QB]
        blk = lambda t, sb: t[sb * QB:(sb + 1) * QB]
        hms = [_head_mask(hh) for hh in range(NSLOT)]
        q_hs = [jnp.where(hm, q_ext, 0) for hm in hms]
        dy_hs = [jnp.where(hm, dy_ext, 0) for hm in hms]
        raw = []
        for hh in range(NSLOT):
            tiles = [(_dot_nt(blk(q_hs[hh], sb), win(kfull, sb)), _dot_nt(blk(dy_hs[hh], sb), win(vfull, sb)))
                     for sb in range(nsub)]
            tiles.append((_dot_nt(blk(q_hs[hh], nsub), k_last), _dot_nt(blk(dy_hs[hh], nsub), v_last)))
            raw.append(tiles)
        ps, dss = [], []
        for hh in range(NSLOT):
            rowstat = lambda t: jnp.max(jnp.where(hms[hh], t, -jnp.inf), axis=1, keepdims=True)
            p_h, ds_h = [], []
            for sb in range(nsub + 1):
                sc, dp = raw[hh][sb]
                if sb < nsub:
                    msk, lrow, erow = (band_first if sb == 0 else band), rowstat(blk(lse, sb)), rowstat(blk(esum, sb))
                else:
                    msk, lrow, erow = mask_next, rowstat(lse_n), rowstat(esum_n)
                p = jnp.where(msk, jnp.exp(sc * SCALE - lrow), 0.0)
                p_h.append(p.astype(BF))
                ds_h.append((p * (dp - erow)).astype(BF))
            ps.append(p_h)
            dss.append(ds_h)
        dq = [jnp.zeros((QB, GW), F32) for _ in range(nsub)]
        dk = [jnp.zeros((QB, GW), F32) for _ in range(nsub)]
        dv = [jnp.zeros((QB, GW), F32) for _ in range(nsub)]
        for hh in range(NSLOT):
            for sb in range(nsub):
                dq[sb] = dq[sb] + jnp.where(hms[hh], _dot(dss[hh][sb], win(kfull, sb)), 0.0)
                nxt = lambda t: t[sb + 1][:, :QB] if sb + 1 < nsub else t[nsub]
                dk[sb] = dk[sb] + _dot_tn(jnp.concatenate([dss[hh][sb][:, QB:], nxt(dss[hh])], axis=0), win(q_hs[hh], sb))
                dv[sb] = dv[sb] + _dot_tn(jnp.concatenate([ps[hh][sb][:, QB:], nxt(ps[hh])], axis=0), win(dy_hs[hh], sb))
        for sb in range(nsub):
            rows = slice(sb * QB, (sb + 1) * QB)
            o_ref[0, rows, :] = (dq[sb] * SCALE).astype(BF)
            o_ref[1, rows, :] = (dk[sb] * SCALE).astype(BF)
            o_ref[2, rows, :] = dv[sb].astype(BF)

    prev = lambda i: jnp.maximum(i * nsub - 1, 0)
    nxt = lambda i: jnp.minimum((i + 1) * nsub, L // QB - 1)
    cur4 = lambda t: pl.BlockSpec((None, None, R, GW), lambda r, i: (t, r, i, 0))
    prv4 = lambda t: pl.BlockSpec((None, None, QB, GW), lambda r, i: (t, r, prev(i), 0))
    nxt4 = lambda t: pl.BlockSpec((None, None, QB, GW), lambda r, i: (t, r, nxt(i), 0))
    cur3 = pl.BlockSpec((None, R, GW), lambda r, i: (r, i, 0))
    nxt3 = pl.BlockSpec((None, QB, GW), lambda r, i: (r, nxt(i), 0))
    return pl.pallas_call(
        body, name=name, grid=(d, nsteps), out_shape=jax.ShapeDtypeStruct((3, d, L, GW), BF),
        in_specs=[cur4(0), nxt4(0), prv4(1), cur4(1), prv4(2), cur4(2), cur3, nxt3, cur3, nxt3, cur3, nxt3],
        out_specs=pl.BlockSpec((3, None, R, GW), lambda r, i: (0, r, i, 0)),
        compiler_params=_cp(("parallel", "arbitrary")))(qkv, qkv, qkv, qkv, qkv, qkv, dya, dya, dsums, dsums, lse, lse)


def _inproj_bwd(dqkvs, dpuz, dgates, wqkv_t, wuz_t, wg_t, cos_t, sin_t, x, dx2, g1):
    S = x.shape[0]
    tm = min(512, S)

    def body(d0_ref, d1_ref, d2_ref, dp_ref, dg_ref, wqkv_ref, wuz_ref, wg_ref, c_ref, s_ref,
             x_ref, dx2_ref, g1_ref, gx_ref, dg1_ref, dn_ref, scr):
        i = pl.program_id(0)

        @pl.when(i == 0)
        def _():
            dg1_ref[...] = jnp.zeros_like(dg1_ref)

        dh = _dot(dp_ref[...], wuz_ref[...]) + _dot(dg_ref[...], wg_ref[...])
        for t in range(3):
            for g, (d_ref, d) in enumerate(zip((d0_ref, d1_ref, d2_ref), DILS)):
                piece = _undilate_load(d_ref, (t,), d, scr, tm)
                if t < 2:
                    piece = _rope_bwd(piece, c_ref, s_ref)
                dn_ref[:, (3 * t + g) * GW:(3 * t + g + 1) * GW] = piece.astype(BF)
        dh = dh + _dot(dn_ref[...], wqkv_ref[...])
        xv = x_ref[...]
        dx1, dg1 = _rms_bwd(xv, _rms_stats(xv), g1_ref[...], dh)
        gx_ref[...] = dx2_ref[...] + dx1
        dg1_ref[...] += dg1

    row = lambda w: pl.BlockSpec((tm, w), lambda i: (i, 0))
    full = lambda s: pl.BlockSpec(s, lambda i: (0, 0))
    dil = lambda d: pl.BlockSpec((3, d, tm // d, GW), lambda i: (0, 0, i, 0))
    return pl.pallas_call(
        body, name="inproj_bwd", grid=(S // tm,),
        out_shape=(jax.ShapeDtypeStruct((S, D), F32), jax.ShapeDtypeStruct((1, D), F32),
                   jax.ShapeDtypeStruct((S, 3 * AW), BF)),
        in_specs=[dil(d) for d in DILS] + [row(2 * GMLP_W), row(2 * D), full((3 * AW, D)), full((2 * GMLP_W, D)),
                                            full((2 * D, D)), row(128), row(128), row(D), row(D), full((1, D))],
        out_specs=(row(D), full((1, D)), row(3 * AW)),
        scratch_shapes=[pltpu.VMEM((2, tm, 128), F32)],
        compiler_params=_cp(("arbitrary",)))(*dqkvs, dpuz, dgates, wqkv_t, wuz_t, wg_t, cos_t, sin_t, x, dx2, g1)


def _row_tile(rows, cap=256):
    return max(t for t in range(16, cap + 1, 16) if rows % t == 0)


def _adam_math(w, g, m, v):
    m2 = ADAM_B1 * m + (1.0 - ADAM_B1) * g
    v2 = ADAM_B2 * v + (1.0 - ADAM_B2) * (g * g)
    m_hat = m2 / (1.0 - ADAM_B1 ** ADAM_STEP)
    v_hat = v2 / (1.0 - ADAM_B2 ** ADAM_STEP)
    delta = -ADAM_LR * (m_hat / (jnp.sqrt(v_hat) + ADAM_EPS) + ADAM_WD * w)
    return delta, m2, v2


def _adam_shard(own, recv, w, m, v, name):
    R, C = w.shape
    tr = _row_tile(R)

    def body(own_ref, r_ref, w_ref, m_ref, v_ref, g_ref, d_ref, m2_ref, v2_ref):
        g = own_ref[...] + r_ref[0].astype(F32) + r_ref[1].astype(F32) + r_ref[2].astype(F32)
        g_ref[...] = g
        d_ref[...], m2_ref[...], v2_ref[...] = _adam_math(w_ref[...], g, m_ref[...], v_ref[...])

    spec = pl.BlockSpec((tr, C), lambda i: (i, 0))
    out = jax.ShapeDtypeStruct((R, C), F32)
    return pl.pallas_call(
        body, name=name, grid=(R // tr,), out_shape=(out, out, out, out),
        in_specs=[spec, pl.BlockSpec((3, tr, C), lambda i: (0, i, 0)), spec, spec, spec],
        out_specs=(spec, spec, spec, spec), compiler_params=_cp(("parallel",)))(own, recv, w, m, v)


def _rs_add(gblocks, recv, idx, name):
    _, R, C = gblocks.shape
    tr = _row_tile(R)

    def body(t_ref, g_ref, r_ref, own_ref, send_ref):
        j = pl.program_id(1)
        s = g_ref[...] + r_ref[...]

        @pl.when(j == 0)
        def _():
            own_ref[...] = s

        @pl.when(j > 0)
        def _():
            send_ref[...] = s.astype(BF)

    grid_spec = pltpu.PrefetchScalarGridSpec(
        num_scalar_prefetch=1, grid=(R // tr, 4),
        in_specs=[pl.BlockSpec((None, tr, C), lambda i, j, t: (t[j], i, 0)),
                  pl.BlockSpec((None, tr, C), lambda i, j, t: (t[4 + j], i, 0))],
        out_specs=[pl.BlockSpec((tr, C), lambda i, j, t: (i, 0)),
                   pl.BlockSpec((None, tr, C), lambda i, j, t: (jnp.maximum(j - 1, 0), i, 0))])
    return pl.pallas_call(
        body, name=name, grid_spec=grid_spec,
        out_shape=(jax.ShapeDtypeStruct((R, C), F32), jax.ShapeDtypeStruct((3, R, C), BF)),
        compiler_params=_cp(("parallel", "arbitrary")))(idx, gblocks, recv)


def _mesh_pos():
    return lax.axis_index("x"), lax.axis_index("y"), lax.axis_index("c")


_HBM = pl.BlockSpec(memory_space=pltpu.HBM)
_SEM = pl.BlockSpec(memory_space=pltpu.SEMAPHORE)
_EFFECT = pltpu.SideEffectType.DATAFLOW_SIDE_EFFECTING
_RELATIONS = [(dx, dy, dc) for dx in (0, 1) for dy in (0, 1) for dc in (0, 1)][1:]


def _flip(v, d):
    return 1 - v if d else v


def _plan_gather(n):
    def plan(x, y, c):
        return [(k, None, 4 * x + 2 * y + c, (_flip(x, dx), _flip(y, dy), _flip(c, dc)))
                for k in range(n) for dx, dy, dc in _RELATIONS]
    return plan


def _plan_gather_near(x, y, c):
    chips = [(1 - x, y), (x, 1 - y), (1 - x, 1 - y)]
    return [(0, None, 4 * x + 2 * y + c, (x, y, 1 - c))] + [(0, None, 4 * x + 2 * y + c, (*chip, c)) for chip in chips]


def _plan_gather_pass(from_landing):
    def plan(x, y, c):
        blocks = [4 * cx + 2 * cy + c for cx, cy in ((1 - x, y), (x, 1 - y), (1 - x, 1 - y))]
        return [(0, b if from_landing else None, b, (x, y, 1 - c)) for b in blocks]
    return plan


def _plan_d2d(n):
    def plan(x, y, c):
        return [(k, 2 * kk + 1 - c, kk, (x, y, 1 - c)) for k in range(n) for kk in range(4)]
    return plan


def _plan_ici(n):
    def plan(x, y, c):
        return [(k, j, j, (_flip(x, dx), _flip(y, dy), c))
                for k in range(n) for j, (dx, dy) in enumerate(((1, 0), (0, 1), (1, 1)))]
    return plan


def _plan_copies(plan, src_refs, land_refs, send_sems, recv_sems):
    x, y, c = _mesh_pos()
    return [pltpu.make_async_remote_copy(
        src_ref=src_refs[k] if si is None else src_refs[k].at[si], dst_ref=land_refs[k].at[di],
        send_sem=send_sems.at[n], recv_sem=recv_sems.at[n], device_id=dev, device_id_type=MESH)
        for n, (k, si, di, dev) in enumerate(plan(x, y, c))]


def _exchange_start(srcs, land_shapes, plan, ncopies, name, after):
    n = len(srcs)

    def body(*refs):
        src_refs, land_refs = refs[:n], refs[n:2 * n]
        send_sems, recv_sems = refs[2 * n + len(after)], refs[2 * n + len(after) + 1]
        token = refs[-1]
        for cp in _plan_copies(plan, src_refs, land_refs, send_sems, recv_sems):
            cp.start()
        token[...] = jnp.zeros_like(token)

    lands = [pltpu.with_memory_space_constraint(lax.empty(s, a.dtype), pltpu.HBM) for s, a in zip(land_shapes, srcs)]
    srcs = [pltpu.with_memory_space_constraint(a, pltpu.HBM) for a in srcs]
    outs = pl.pallas_call(
        body, name=name,
        out_shape=(pltpu.SemaphoreType.DMA((ncopies,)), pltpu.SemaphoreType.DMA((ncopies,)))
        + tuple(pltpu.HBM(a.shape, a.dtype) for a in srcs) + tuple(pltpu.HBM(a.shape, a.dtype) for a in lands)
        + (jax.ShapeDtypeStruct((8, 128), F32),),
        in_specs=[_HBM] * (2 * n) + [pl.BlockSpec(memory_space=pl.ANY)] * len(after),
        out_specs=(_SEM, _SEM) + (_HBM,) * (2 * n) + (pl.BlockSpec(memory_space=pltpu.VMEM),),
        input_output_aliases={i: 2 + i for i in range(2 * n)},
        compiler_params=pltpu.CompilerParams(has_side_effects=_EFFECT))(*srcs, *lands, *after)
    return (outs[0], outs[1], list(outs[2:2 + n]), list(outs[2 + n:2 + 2 * n])), outs[-1]


def _exchange_forward(handle, plan, plan_fwd, nfwd, name, after):
    send_sems, recv_sems, srcs, lands = handle
    n = len(srcs)

    def body(*refs):
        src_refs, land_refs = refs[:n], refs[n:2 * n]
        outs = refs[2 * n + 2 + len(after):]
        for cp in _plan_copies(plan, src_refs, land_refs, refs[2 * n], refs[2 * n + 1]):
            cp.wait_send()
            cp.wait_recv()
        for cp in _plan_copies(plan_fwd, land_refs, land_refs, outs[0], outs[1]):
            cp.start()

    outs = pl.pallas_call(
        body, name=name,
        out_shape=(pltpu.SemaphoreType.DMA((nfwd,)), pltpu.SemaphoreType.DMA((nfwd,)))
        + tuple(pltpu.HBM(a.shape, a.dtype) for a in srcs) + tuple(pltpu.HBM(a.shape, a.dtype) for a in lands),
        in_specs=[_HBM] * (2 * n) + [_SEM, _SEM] + [pl.BlockSpec(memory_space=pl.ANY)] * len(after),
        out_specs=(_SEM, _SEM) + (_HBM,) * (2 * n), input_output_aliases={i: 2 + i for i in range(2 * n)},
        compiler_params=pltpu.CompilerParams(has_side_effects=_EFFECT))(*srcs, *lands, send_sems, recv_sems, *after)
    return outs[0], outs[1], list(outs[2:2 + n]), list(outs[2 + n:2 + 2 * n])


def _exchange_wait(handle, plan, name, after):
    send_sems, recv_sems, srcs, lands = handle
    n = len(srcs)

    def body(*refs):
        src_refs, land_refs = refs[:n], refs[n:2 * n]
        for cp in _plan_copies(plan, src_refs, land_refs, refs[2 * n], refs[2 * n + 1]):
            cp.wait_send()
            cp.wait_recv()

    outs = pl.pallas_call(
        body, name=name,
        out_shape=tuple(pltpu.HBM(a.shape, a.dtype) for a in srcs) + tuple(pltpu.HBM(a.shape, a.dtype) for a in lands),
        in_specs=[_HBM] * (2 * n) + [_SEM, _SEM] + [pl.BlockSpec(memory_space=pl.ANY)] * len(after),
        out_specs=(_HBM,) * (2 * n), input_output_aliases={i: i for i in range(2 * n)},
        compiler_params=pltpu.CompilerParams(has_side_effects=_EFFECT))(*srcs, *lands, send_sems, recv_sems, *after)
    return list(outs[:n]), list(outs[n:])


SMALL = ("g1", "g2", "g3", "g4", "ln_g", "ln_b", "b_sp", "w_sp")


def _small_step(loss_row, grads, ws, ms, vs):
    parts = [loss_row] + [grads[k] for k in SMALL]
    n, ns = len(parts), len(SMALL)

    def body(*refs):
        p_refs = refs[:n]
        w_refs, m_refs, v_refs = (refs[n + i * ns:n + (i + 1) * ns] for i in range(3))
        o = n + 3 * ns
        loss_ref = refs[o]
        g_out, d_out, m_out, v_out = (refs[o + 1 + i * ns:o + 1 + (i + 1) * ns] for i in range(4))
        bufs = refs[o + 1 + 4 * ns:o + 1 + 4 * ns + n]
        send_sems, recv_sems = refs[-2:]
        x, y, c = _mesh_pos()
        me, sib = (x, y, c), (x, y, 1 - c)
        chips = [(1 - x, y), (x, 1 - y), (1 - x, 1 - y)]

        def rcopy(k, s, block, to, src=None):
            dst = bufs[k].at[4 * block[0] + 2 * block[1] + block[2]]
            return pltpu.make_async_remote_copy(
                src_ref=dst if src is None else src, dst_ref=dst, send_sem=send_sems.at[k, s],
                recv_sem=recv_sems.at[k, s], device_id=to, device_id_type=MESH)

        started = []
        for k in range(n):
            bufs[k][4 * x + 2 * y + c] = p_refs[k][...]
            started.append(rcopy(k, 0, me, sib, src=p_refs[k]))
            started += [rcopy(k, 1 + j, me, (*chip, c), src=p_refs[k]) for j, chip in enumerate(chips)]
        for cp in started:
            cp.start()
        for k in range(n):
            for j, chip in enumerate(chips):
                rcopy(k, 1 + j, (*chip, c), me).wait_recv()
                started.append(rcopy(k, 4 + j, (*chip, c), sib))
                started[-1].start()
        for k in range(n):
            rcopy(k, 0, sib, me).wait_recv()
            for j, chip in enumerate(chips):
                rcopy(k, 4 + j, (*chip, 1 - c), me).wait_recv()
        for cp in started:
            cp.wait_send()
        sums = []
        for k in range(n):
            acc = bufs[k][0]
            for b in range(1, 8):
                acc = acc + bufs[k][b]
            sums.append(acc)
        loss_ref[...] = sums[0]
        for i in range(ns):
            g_out[i][...] = sums[1 + i]
            d_out[i][...], m_out[i][...], v_out[i][...] = _adam_math(w_refs[i][...], sums[1 + i], m_refs[i][...],
                                                                     v_refs[i][...])

    args = parts + [t[k] for t in (ws, ms, vs) for k in SMALL]
    shapes = [jax.ShapeDtypeStruct(loss_row.shape, F32)] + [jax.ShapeDtypeStruct(grads[k].shape, F32) for k in SMALL] * 4
    vmem = pl.BlockSpec(memory_space=pltpu.VMEM)
    outs = pl.pallas_call(
        body, name="small_step", out_shape=tuple(shapes), in_specs=[vmem] * len(args), out_specs=(vmem,) * len(shapes),
        scratch_shapes=[pltpu.VMEM((8,) + p.shape, F32) for p in parts]
        + [pltpu.SemaphoreType.DMA((n, 7)), pltpu.SemaphoreType.DMA((n, 7))],
        compiler_params=pltpu.CompilerParams(vmem_limit_bytes=VMEM_LIMIT))(*args)
    groups = [dict(zip(SMALL, outs[1 + i * ns:1 + (i + 1) * ns])) for i in range(4)]
    return (outs[0], *groups)


def _rope_tables(S):
    half = HD // 2
    inv_freq = jnp.tile(ROPE_THETA ** (-jnp.arange(half, dtype=F32) / half), 4)
    sign = jnp.tile(jnp.concatenate([-jnp.ones(half, F32), jnp.ones(half, F32)]), 2)
    ang = jnp.arange(S, dtype=F32)[:, None] * inv_freq[None, :]
    return jnp.cos(ang), jnp.sin(ang) * sign[None, :]


def _to_blocks(g, col_sharded):
    if col_sharded:
        return g.reshape(g.shape[0], 8, g.shape[1] // 8).transpose(1, 0, 2)
    return g.reshape(8, g.shape[0] // 8, g.shape[1])


def _from_blocks(t, col_sharded):
    if col_sharded:
        return t.transpose(1, 0, 2).reshape(t.shape[1], 8 * t.shape[2])
    return t.reshape(8 * t.shape[1], t.shape[2])


class _NoComm:
    def __init__(self, late_weights):
        self._late = late_weights
        self.grads = {}

    def start_tie(self):
        return jnp.zeros((8, 128), F32)

    def late_weights(self, after):
        return self._late

    def rs_start(self, key, gblocks, after=()):
        self.grads[key] = gblocks
        return jnp.zeros((8, 128), F32)

    def rs_mid(self, key, after):
        return jnp.zeros((8, 128), F32)


class _FsdpComm:
    def __init__(self, late_shards, col_sharded, after, idx, block):
        self._col, self._idx, self._block, self._rs = col_sharded, idx, block, {}
        n = len(late_shards)
        self._gather, self._token = _exchange_start(
            late_shards, [(8,) + s.shape for s in late_shards], _plan_gather(n), 7 * n, "ag_late_start", (after,))

    def start_tie(self):
        return self._token

    def late_weights(self, after):
        shards, lands = _exchange_wait(self._gather, _plan_gather(len(self._col)), "ag_late_wait", after)
        lands = [lax.dynamic_update_index_in_dim(t, s, self._block, 0) for t, s in zip(lands, shards)]
        return [_from_blocks(t, cs) for t, cs in zip(lands, self._col)]

    def rs_start(self, key, gblocks, after=()):
        n = len(gblocks)
        d2d, token = _exchange_start(gblocks, [(4,) + g.shape[1:] for g in gblocks], _plan_d2d(n), 4 * n,
                                     "rs_%s_d2d_start" % key, (self._token,) + tuple(after))
        self._rs[key] = dict(n=n, d2d=d2d)
        return token

    def rs_mid(self, key, after):
        st = self._rs[key]
        gblocks, from_sib = _exchange_wait(st["d2d"], _plan_d2d(st["n"]), "rs_%s_d2d_wait" % key, after)
        halves = [_rs_add(g, r, self._idx, "rs_add_%s_%d" % (key, k)) for k, (g, r) in enumerate(zip(gblocks, from_sib))]
        st["own"] = [own for own, _ in halves]
        sends = [send for _, send in halves]
        st["ici"], token = _exchange_start(sends, [t.shape for t in sends], _plan_ici(st["n"]), 3 * st["n"],
                                           "rs_%s_ici_start" % key, (self._token,))
        return token

    def rs_end(self, key, after):
        st = self._rs[key]
        return st["own"], _exchange_wait(st["ici"], _plan_ici(st["n"]), "rs_%s_ici_wait" % key, after)[1]


def _local_step(x, tgt, h1, cos_t, sin_t, wqkv_t, wuz_t, wg_t, comm, g1, g2, g3, g4, ln_g, ln_b, w_sp, b_sp):
    b_sp_t = b_sp.T
    w_sp_t = w_sp.transpose(0, 2, 1)

    qkvs = [_proj_qkv(h1, wqkv_t, cos_t, sin_t, g, d, "proj_qkv_d%d" % d, comm.start_tie()) for g, d in enumerate(DILS)]
    puz = _proj(h1, wuz_t, "none", "proj_uz", 1024)
    gates = _proj(h1, wg_t, "sigmoid", "proj_gates", 512)
    fwd = [_attn_fwd(t, "attn_fwd_d%d" % d) for t, d in zip(qkvs, DILS)]
    ya, *lses = _attn_combine([o for o, _ in fwd], [l for _, l in fwd])
    yg = _gmlp_fwd(puz, ln_g, ln_b, w_sp, b_sp_t)
    wba, wbg, wout, wmi, wmo = comm.late_weights(after=(ya, yg, gates))
    merged, y, x2, h3 = _merge_fwd(ya, yg, gates, wba, wbg, wout, x, g2, g3)
    a, dy2, dout, loss_row, dg4 = _mlp_fwd(h3, wmi, wmo, x2, tgt, g4)

    dpre, dx2, dy, dg3, dg2 = _mlp_bwd(dy2, a, wmo, wmi, x2, y, dout, g2, g3)
    dwmo = _mm_tn(a, dy2, "dw_mlp_out", square_a=True)
    dwmi = _mm_tn(h3, dpre, "dw_mlp_in", col_blocks=8)
    tie = comm.rs_start("mlp", [dwmi, _to_blocks(dwmo, False)])
    dgates, da, db, dyg, *rest = _outproj_bwd(dy, wout, ya, yg, gates, wba, wbg, tie)
    dyas, dsums = rest[:3], rest[3:]
    tie = comm.rs_mid("mlp", after=(dyg,))
    dpuz, dwsp, dbs, dlng, dlnb = _gmlp_bwd(puz, dyg, ln_g + tie[0, 0], ln_b, w_sp, w_sp_t, b_sp_t)
    dqkvs = [_attn_bwd(qkvs[g], dyas[g], dsums[g], lses[g], "attn_bwd_d%d" % d) for g, d in enumerate(DILS)]
    grad_x, dg1, dqkv = _inproj_bwd(dqkvs, dpuz, dgates, wqkv_t, wuz_t, wg_t, cos_t, sin_t, x, dx2, g1)
    dwin_t = jnp.concatenate([_mm_tn(t, h1, "dw_in_%d" % n) for n, t in enumerate((dqkv, dpuz, dgates))], axis=0)
    dwout = _mm_tn(merged, dy, "dw_out")
    dwba = _mm_tn(ya, da, "dw_branch_attn")
    dwbg = _mm_tn(yg, db, "dw_branch_gmlp")
    tie = comm.rs_start("win", [_to_blocks(dwin_t, False)], after=(dwout, dwba, dwbg))
    small = dict(g1=dg1, g2=dg2, g3=dg3, g4=dg4, ln_g=dlng, ln_b=dlnb, b_sp=dbs.reshape(4, CHUNK),
                 w_sp=dwsp.reshape(4 * CHUNK, CHUNK))
    mid = [_to_blocks(dwba, True), _to_blocks(dwbg, True), _to_blocks(dwout, False)]
    return loss_row + tie[:1], grad_x, small, mid


def kernel(x, norm_pre_mix, w_in, w_spatial, b_spatial, ln_v_gain, ln_v_bias, w_branch_attn, w_branch_gmlp, w_out, norm_post_mix, norm_pre_mlp, w_mlp_in, w_mlp_out, norm_post_mlp, loss_target, m_norm_pre_mix, m_w_in, m_w_spatial, m_b_spatial, m_ln_v_gain, m_ln_v_bias, m_w_branch_attn, m_w_branch_gmlp, m_w_out, m_norm_post_mix, m_norm_pre_mlp, m_w_mlp_in, m_w_mlp_out, m_norm_post_mlp, v_norm_pre_mix, v_w_in, v_w_spatial, v_b_spatial, v_ln_v_gain, v_ln_v_bias, v_w_branch_attn, v_w_branch_gmlp, v_w_out, v_norm_post_mix, v_norm_pre_mlp, v_w_mlp_in, v_w_mlp_out, v_norm_post_mlp):
    mx, my, mc = _mesh_pos()
    rel = [(0, 0), (1, 0), (0, 1), (1, 1)]
    chip_of = [2 * (mx ^ dx) + (my ^ dy) for dx, dy in rel]
    idx = jnp.stack([2 * k + mc for k in chip_of] + chip_of).astype(jnp.int32)

    w_in_t, m_w_in_t, v_w_in_t = (t[0].T for t in (w_in, m_w_in, v_w_in))
    shard = w_in_t.astype(BF)
    gather, token = _exchange_start([shard], [(8,) + shard.shape], _plan_gather_near, 4, "ag_win_start", ())
    h1 = _rms_fwd(x[0], norm_pre_mix + token[0, 0], "rms_pre_mix")
    cos_t, sin_t = _rope_tables(x.shape[1])
    gather = _exchange_forward(gather, _plan_gather_near, _plan_gather_pass(True), 3, "ag_win_pass", (h1, cos_t, sin_t))
    (shard,), (g_win,) = _exchange_wait(gather, _plan_gather_pass(False), "ag_win_wait", ())
    g_win = lax.dynamic_update_index_in_dim(g_win, shard, 4 * mx + 2 * my + mc, 0)
    win_t = _from_blocks(g_win, False)
    wqkv_t, wuz_t, wg_t = win_t[:3 * AW], win_t[3 * AW:3 * AW + 2 * GMLP_W], win_t[3 * AW + 2 * GMLP_W:]
    late = [w_branch_attn[0], w_branch_gmlp[0], w_out[0], w_mlp_in[0], w_mlp_out[0]]
    comm = _FsdpComm([w.astype(BF) for w in late], [True, True, False, True, False], g_win, idx, 4 * mx + 2 * my + mc)

    loss_row, grad_x, small, mid = _local_step(
        x[0], loss_target[0], h1, cos_t, sin_t, wqkv_t, wuz_t, wg_t, comm,
        norm_pre_mix, norm_post_mix, norm_pre_mlp, norm_post_mlp, ln_v_gain, ln_v_bias, w_spatial[0], b_spatial[0])

    flat = lambda t: t.reshape(-1, t.shape[-1])
    small_w = dict(zip(SMALL, map(flat, (norm_pre_mix, norm_post_mix, norm_pre_mlp, norm_post_mlp, ln_v_gain, ln_v_bias,
                                         b_spatial, w_spatial))))
    small_m = dict(zip(SMALL, map(flat, (m_norm_pre_mix, m_norm_post_mix, m_norm_pre_mlp, m_norm_post_mlp, m_ln_v_gain,
                                         m_ln_v_bias, m_b_spatial, m_w_spatial))))
    small_v = dict(zip(SMALL, map(flat, (v_norm_pre_mix, v_norm_post_mix, v_norm_pre_mlp, v_norm_post_mlp, v_ln_v_gain,
                                         v_ln_v_bias, v_b_spatial, v_w_spatial))))
    loss_out, sg, sd, sm, sv = _small_step(loss_row, small, small_w, small_m, small_v)
    tie = comm.rs_mid("win", after=(loss_out,))
    tie = comm.rs_start("mid", mid, after=(tie,))
    mlp_own, mlp_chips = comm.rs_end("mlp", after=(tie,))
    adam = lambda nm, own, r, w, m, v: _adam_shard(own, r, w[0], m[0], v[0], "adam_" + nm)
    upd = {
        "w_mlp_in": adam("w_mlp_in", mlp_own[0], mlp_chips[0], w_mlp_in, m_w_mlp_in, v_w_mlp_in),
        "w_mlp_out": adam("w_mlp_out", mlp_own[1], mlp_chips[1], w_mlp_out, m_w_mlp_out, v_w_mlp_out),
    }
    loss = loss_out[0, 0]
    tie = comm.rs_mid("mid", after=(upd["w_mlp_in"][0], upd["w_mlp_out"][0]))
    win_own, win_chips = comm.rs_end("win", after=(tie,))
    upd["w_in"] = tuple(t.T for t in _adam_shard(win_own[0], win_chips[0], w_in_t, m_w_in_t, v_w_in_t, "adam_w_in"))
    mid_own, mid_chips = comm.rs_end("mid", after=(upd["w_in"][0],))
    upd["w_branch_attn"] = adam("w_branch_attn", mid_own[0], mid_chips[0], w_branch_attn, m_w_branch_attn, v_w_branch_attn)
    upd["w_branch_gmlp"] = adam("w_branch_gmlp", mid_own[1], mid_chips[1], w_branch_gmlp, m_w_branch_gmlp, v_w_branch_gmlp)
    upd["w_out"] = adam("w_out", mid_own[2], mid_chips[2], w_out, m_w_out, v_w_out)
    order = ["g1", "w_in", "w_sp", "b_sp", "ln_g", "ln_b", "w_branch_attn", "w_branch_gmlp", "w_out", "g2", "g3",
             "w_mlp_in", "w_mlp_out", "g4"]
    small_shape = dict(g1=norm_pre_mix.shape, g2=norm_post_mix.shape, g3=norm_pre_mlp.shape, g4=norm_post_mlp.shape,
                       ln_g=ln_v_gain.shape, ln_b=ln_v_bias.shape, b_sp=b_spatial.shape, w_sp=w_spatial.shape)

    def pick(which):
        return [upd[nm][which][None] if nm in upd else (sg, sd, sm, sv)[which][nm].reshape(small_shape[nm])
                for nm in order]

    return (loss, grad_x[None], *pick(0), *pick(1), *pick(2), *pick(3))
```

```python
import functools
import math

import jax
import jax.numpy as jnp
from jax import lax
from jax.experimental import pallas as pl
from jax.experimental.pallas import tpu as pltpu

D = 1024
HD = 64
NSLOT = 4
GW = NSLOT * HD
DILS = (1, 4, 16)
QB = 128
ATTN_NSUB = 4
PROJ_QKV_TM = 1024
MLP_TM_FWD = 512
MLP_TM_BWD = 256
AW = 3 * GW
W_UZ0, W_G0 = 3 * AW, 3 * AW + 1024
GMLP_W = 512
CHUNK = 128
DFF = 4096
EPS = 1e-6
ROPE_THETA = 10000.0
SCALE = HD ** -0.5
NEG = -1e30

ADAM_LR = 0.001
ADAM_B1 = 0.9
ADAM_B2 = 0.999
ADAM_EPS = 1e-08
ADAM_WD = 0.01
ADAM_STEP = 10

BF = jnp.bfloat16
F32 = jnp.float32
MESH = pl.DeviceIdType.MESH
VMEM_LIMIT = 56 * 1024 * 1024


def _cp(sem):
    return pltpu.CompilerParams(dimension_semantics=sem, vmem_limit_bytes=VMEM_LIMIT)


def _dot(a, b):
    return jnp.dot(a, b, preferred_element_type=F32)


def _dot_nt(a, b):
    return lax.dot_general(a, b, (((1,), (1,)), ((), ())), preferred_element_type=F32)


def _dot_tn(a, b):
    return lax.dot_general(a, b, (((0,), (0,)), ((), ())), preferred_element_type=F32)


def _gelu(x):
    return jax.nn.gelu(x, approximate=True)


def _gelu_grad(x):
    k = math.sqrt(2.0 / math.pi)
    t = jnp.tanh(k * (x + 0.044715 * x * x * x))
    return 0.5 * (1.0 + t) + 0.5 * x * (1.0 - t * t) * (k * (1.0 + 3.0 * 0.044715 * x * x))


def _swap_halves(t):
    w = t.shape[1]
    lane = lax.broadcasted_iota(jnp.int32, t.shape, 1)
    first = (lane & (HD - 1)) < (HD // 2)
    return jnp.where(first, pltpu.roll(t, w - HD // 2, 1), pltpu.roll(t, HD // 2, 1))


def _head_mask(hh):
    lane = lax.broadcasted_iota(jnp.int32, (1, GW), 1)
    return jnp.logical_and(lane >= hh * HD, lane < (hh + 1) * HD)


def _rms_stats(xf):
    return lax.rsqrt(jnp.mean(xf * xf, axis=-1, keepdims=True) + EPS)


def _rms_bwd(xf, r, gain, dout):
    n = xf * r
    t = dout * gain
    dx = r * (t - n * jnp.mean(t * n, axis=-1, keepdims=True))
    return dx, jnp.sum(dout * n, axis=0, keepdims=True)


def _rms_fwd(x, gain, name):
    S = x.shape[0]
    tm = min(512, S)

    def body(x_ref, g_ref, h_ref):
        xf = x_ref[...]
        h_ref[...] = (xf * _rms_stats(xf) * g_ref[...]).astype(BF)

    return pl.pallas_call(
        body, name=name, out_shape=jax.ShapeDtypeStruct((S, D), BF), grid=(S // tm,),
        in_specs=[pl.BlockSpec((tm, D), lambda i: (i, 0)), pl.BlockSpec((1, D), lambda i: (0, 0))],
        out_specs=pl.BlockSpec((tm, D), lambda i: (i, 0)), compiler_params=_cp(("parallel",)))(x, gain)


def _proj(h, w_t, row0, N, epi, name, tm):
    S, K = h.shape
    tm = min(tm, S)
    nblk = N // GW

    def body(h_ref, *refs):
        hv = h_ref[...]
        o_ref = refs[-1]
        for b in range(nblk):
            y = _dot_nt(hv, refs[b][...])
            if epi == "sigmoid":
                y = jax.nn.sigmoid(y)
            o_ref[:, b * GW:(b + 1) * GW] = y.astype(BF)

    blocks = [pl.BlockSpec((GW, K), functools.partial(lambda i, b: (b, 0), b=row0 // GW + b),
                           pipeline_mode=pl.Buffered(1)) for b in range(nblk)]
    return pl.pallas_call(
        body, name=name, out_shape=jax.ShapeDtypeStruct((S, N), BF), grid=(S // tm,),
        in_specs=[pl.BlockSpec((tm, K), lambda i: (i, 0))] + blocks,
        out_specs=pl.BlockSpec((tm, N), lambda i: (i, 0)),
        compiler_params=_cp(("parallel",)))(h, *([w_t] * nblk))


def _dilate_store(val, scr, o_ref, lead, d):
    rows = val.shape[0]
    if d == 1:
        o_ref[lead + (0,)] = val.astype(o_ref.dtype)
        return
    for hf in range(2):
        scr[hf, pl.ds(0, rows), :] = val[:, hf * 128:(hf + 1) * 128]
    for r in range(d):
        for hf in range(2):
            o_ref[lead + (r, slice(None), slice(hf * 128, (hf + 1) * 128))] = (
                scr[hf, pl.ds(r, rows // d, stride=d), :].astype(o_ref.dtype))


def _undilate_load(i_ref, lead, d, scr, rows):
    if d == 1:
        return i_ref[lead + (0,)].astype(F32)
    for r in range(d):
        for hf in range(2):
            scr[hf, pl.ds(r, rows // d, stride=d), :] = (
                i_ref[lead + (r, slice(None), slice(hf * 128, (hf + 1) * 128))].astype(F32))
    return jnp.concatenate([scr[0, pl.ds(0, rows), :], scr[1, pl.ds(0, rows), :]], axis=1)


def _rope_fwd(y, c_ref, s_ref):
    cosv = jnp.concatenate([c_ref[...]] * 2, axis=1)
    sinv = jnp.concatenate([s_ref[...]] * 2, axis=1)
    return y * cosv + _swap_halves(y) * sinv


def _rope_bwd(dy, c_ref, s_ref):
    cosv = jnp.concatenate([c_ref[...]] * 2, axis=1)
    sinv = jnp.concatenate([s_ref[...]] * 2, axis=1)
    return dy * cosv + _swap_halves(dy * sinv)


def _proj_qkv(h, wqkv_t, cos_t, sin_t, g, d, name, tie):
    S, K = h.shape
    tm = min(PROJ_QKV_TM, S)

    def body(h_ref, wq_ref, wk_ref, wv_ref, c_ref, s_ref, tie_ref, o_ref, scr):
        hv = h_ref[...]
        ys = [_dot_nt(hv, w_ref[...]) for w_ref in (wq_ref, wk_ref, wv_ref)]
        for t in range(3):
            y = _rope_fwd(ys[t], c_ref, s_ref) if t < 2 else ys[t]
            _dilate_store(y, scr.at[t], o_ref, (t,), d)

    w_spec = lambda t: pl.BlockSpec((GW, K), lambda i: (3 * t + g, 0))
    return pl.pallas_call(
        body, name=name, out_shape=jax.ShapeDtypeStruct((3, d, S // d, GW), BF), grid=(S // tm,),
        in_specs=[pl.BlockSpec((tm, K), lambda i: (i, 0)), w_spec(0), w_spec(1), w_spec(2),
                  pl.BlockSpec((tm, 128), lambda i: (i, 0)), pl.BlockSpec((tm, 128), lambda i: (i, 0)),
                  pl.BlockSpec(memory_space=pl.ANY)],
        out_specs=pl.BlockSpec((3, d, tm // d, GW), lambda i: (0, 0, i, 0)),
        scratch_shapes=[pltpu.VMEM((3, 2, tm, 128), F32)],
        compiler_params=_cp(("parallel",)))(h, wqkv_t, wqkv_t, wqkv_t, cos_t, sin_t, tie)


def _band_masks(first_step):
    row = lax.broadcasted_iota(jnp.int32, (QB, 2 * QB), 0)
    col = lax.broadcasted_iota(jnp.int32, (QB, 2 * QB), 1)
    band = jnp.logical_and(col >= row, col <= row + QB)
    return band, jnp.logical_and(band, jnp.logical_or(col >= QB, jnp.logical_not(first_step)))


def _attn_fwd(qkv, name):
    _, d, L, _ = qkv.shape
    nsub = min(ATTN_NSUB, L // QB)
    R = nsub * QB
    nsteps = L // R

    def body(q_ref, kp_ref, kc_ref, vp_ref, vc_ref, o_ref, lse_ref):
        i = pl.program_id(1)
        band, band_first = _band_masks(i == 0)
        kfull = jnp.concatenate([kp_ref[...], kc_ref[...]], axis=0)
        vfull = jnp.concatenate([vp_ref[...], vc_ref[...]], axis=0)
        chains = [(sb, hh) for sb in range(nsub) for hh in range(NSLOT)]
        win = lambda t, sb: t[sb * QB:(sb + 2) * QB]
        scores = []
        for sb, hh in chains:
            qh = jnp.where(_head_mask(hh), q_ref[sb * QB:(sb + 1) * QB, :], 0)
            scores.append(_dot_nt(qh, win(kfull, sb)))
        soft = []
        for (sb, hh), sc in zip(chains, scores):
            sc = jnp.where(band_first if sb == 0 else band, sc * SCALE, NEG)
            m = jnp.max(sc, axis=1, keepdims=True)
            p = jnp.exp(sc - m)
            den = jnp.sum(p, axis=1, keepdims=True)
            soft.append((p.astype(BF), den, m + jnp.log(den)))
        accs = [_dot(p, win(vfull, sb)) for (sb, hh), (p, _, _) in zip(chains, soft)]
        for sb in range(nsub):
            o = jnp.zeros((QB, GW), F32)
            lse = jnp.zeros((QB, GW), F32)
            for hh in range(NSLOT):
                hm = _head_mask(hh)
                _, den, lrow = soft[sb * NSLOT + hh]
                o = o + jnp.where(hm, accs[sb * NSLOT + hh] / den, 0.0)
                lse = lse + jnp.where(hm, lrow, 0.0)
            o_ref[sb * QB:(sb + 1) * QB, :] = o
            lse_ref[sb * QB:(sb + 1) * QB, :] = lse

    prev = lambda i: jnp.maximum(i * nsub - 1, 0)
    cur = lambda t: pl.BlockSpec((None, None, R, GW), lambda r, i: (t, r, i, 0))
    prv = lambda t: pl.BlockSpec((None, None, QB, GW), lambda r, i: (t, r, prev(i), 0))
    out = pl.BlockSpec((None, R, GW), lambda r, i: (r, i, 0))
    return pl.pallas_call(
        body, name=name, grid=(d, nsteps),
        out_shape=(jax.ShapeDtypeStruct((d, L, GW), F32), jax.ShapeDtypeStruct((d, L, GW), F32)),
        in_specs=[cur(0), prv(1), cur(1), prv(2), cur(2)],
        out_specs=(out, out), compiler_params=_cp(("parallel", "arbitrary")))(qkv, qkv, qkv, qkv, qkv)


def _attn_combine(os_, lses):
    S = os_[0].shape[1]
    tm = min(1024, S)

    def body(o0, o1, o2, l0, l1, l2, y_ref, j0, j1, j2, scr):
        os_nat = [_undilate_load(o, (), d, scr, tm) for o, d in zip((o0, o1, o2), DILS)]
        a, b, c = [_undilate_load(l, (), d, scr, tm) for l, d in zip((l0, l1, l2), DILS)]
        m = jnp.maximum(jnp.maximum(a, b), c)
        wa, wb, wc = jnp.exp(a - m), jnp.exp(b - m), jnp.exp(c - m)
        den = wa + wb + wc
        y_ref[...] = ((wa * os_nat[0] + wb * os_nat[1] + wc * os_nat[2]) / den).astype(BF)
        lse = m + jnp.log(den)
        for j_ref, d in zip((j0, j1, j2), DILS):
            _dilate_store(lse, scr, j_ref, (), d)

    dil = lambda d: pl.BlockSpec((d, tm // d, GW), lambda i: (0, i, 0))
    dshape = lambda d: jax.ShapeDtypeStruct((d, S // d, GW), F32)
    return pl.pallas_call(
        body, name="attn_combine", grid=(S // tm,),
        out_shape=(jax.ShapeDtypeStruct((S, GW), BF),) + tuple(dshape(d) for d in DILS),
        in_specs=[dil(d) for d in DILS] * 2,
        out_specs=(pl.BlockSpec((tm, GW), lambda i: (i, 0)),) + tuple(dil(d) for d in DILS),
        scratch_shapes=[pltpu.VMEM((2, tm, 128), F32)],
        compiler_params=_cp(("parallel",)))(*os_, *lses)


def _tril(upper=False):
    row = lax.broadcasted_iota(jnp.int32, (CHUNK, CHUNK), 0)
    col = lax.broadcasted_iota(jnp.int32, (CHUNK, CHUNK), 1)
    return row <= col if upper else col <= row


def _ln_fwd(z, gain, bias):
    mu = jnp.mean(z, axis=-1, keepdims=True)
    zc = z - mu
    rstd = lax.rsqrt(jnp.mean(zc * zc, axis=-1, keepdims=True) + EPS)
    zhat = zc * rstd
    return zhat, rstd, zhat * gain + bias


def _gmlp_fwd(puz, ln_g, ln_b, w_sp, b_sp_t):
    S = puz.shape[0]
    tm = min(512, S)
    nch = tm // CHUNK

    def body(p_ref, g_ref, b_ref, w_ref, bt_ref, o_ref):
        tril = _tril()
        ws = [jnp.where(tril, w_ref[gg], 0.0).astype(BF) for gg in range(4)]
        for ch in range(nch):
            rows = slice(ch * CHUNK, (ch + 1) * CHUNK)
            z = _gelu(p_ref[rows, GMLP_W:].astype(F32))
            _, _, zn = _ln_fwd(z, g_ref[...], b_ref[...])
            zn = zn.astype(BF)
            for gg in range(4):
                cols = slice(gg * CHUNK, (gg + 1) * CHUNK)
                sz = _dot(ws[gg], zn[:, cols]) + bt_ref[:, gg:gg + 1]
                u = _gelu(p_ref[rows, cols].astype(F32))
                o_ref[rows, cols] = (u * sz).astype(BF)

    return pl.pallas_call(
        body, name="gmlp_fwd", out_shape=jax.ShapeDtypeStruct((S, GMLP_W), BF), grid=(S // tm,),
        in_specs=[pl.BlockSpec((tm, 2 * GMLP_W), lambda i: (i, 0)),
                  pl.BlockSpec((1, GMLP_W), lambda i: (0, 0)), pl.BlockSpec((1, GMLP_W), lambda i: (0, 0)),
                  pl.BlockSpec((4, CHUNK, CHUNK), lambda i: (0, 0, 0)), pl.BlockSpec((CHUNK, 4), lambda i: (0, 0))],
        out_specs=pl.BlockSpec((tm, GMLP_W), lambda i: (i, 0)),
        compiler_params=_cp(("parallel",)))(puz, ln_g, ln_b, w_sp, b_sp_t)


def _merge_fwd(ya, yg, gates, wba, wbg, wout, x, g2, g3):
    S = x.shape[0]
    tm = min(512, S)

    def body(ya_ref, yg_ref, gt_ref, wba_ref, wbg_ref, wo_ref, x_ref, g2_ref, g3_ref,
             mg_ref, y_ref, x2_ref, h3_ref):
        a = _dot(ya_ref[...], wba_ref[...])
        b = _dot(yg_ref[...], wbg_ref[...])
        merged = (gt_ref[:, :D].astype(F32) * a + gt_ref[:, D:].astype(F32) * b).astype(BF)
        mg_ref[...] = merged
        y = _dot(merged, wo_ref[...])
        y_ref[...] = y
        x2 = x_ref[...] + y * _rms_stats(y) * g2_ref[...]
        x2_ref[...] = x2
        h3_ref[...] = (x2 * _rms_stats(x2) * g3_ref[...]).astype(BF)

    row = lambda w: pl.BlockSpec((tm, w), lambda i: (i, 0))
    full = lambda s: pl.BlockSpec(s, lambda i: (0, 0))
    return pl.pallas_call(
        body, name="merge_fwd", grid=(S // tm,),
        out_shape=(jax.ShapeDtypeStruct((S, D), BF), jax.ShapeDtypeStruct((S, D), F32), jax.ShapeDtypeStruct((S, D), F32),
                   jax.ShapeDtypeStruct((S, D), BF)),
        in_specs=[row(GW), row(GMLP_W), row(2 * D), full((GW, D)), full((GMLP_W, D)), full((D, D)), row(D),
                  full((1, D)), full((1, D))],
        out_specs=(row(D), row(D), row(D), row(D)),
        compiler_params=_cp(("parallel",)))(ya, yg, gates, wba, wbg, wout, x, g2, g3)


def _resident(shape):
    return pl.BlockSpec(shape, lambda i: (0,) * len(shape), pipeline_mode=pl.Buffered(1))


def _mlp_fwd(h3, wmi, wmo, x2, tgt, g4):
    S = x2.shape[0]
    tm = min(MLP_TM_FWD, S)

    def body(h_ref, wi_ref, wo_ref, x2_ref, t_ref, g4_ref, a_ref, dy2_ref, dout_ref, loss_ref, dg4_ref):
        @pl.when(pl.program_id(0) == 0)
        def _():
            loss_ref[...] = jnp.zeros_like(loss_ref)
            dg4_ref[...] = jnp.zeros_like(dg4_ref)

        halves = [slice(hh * (tm // 2), (hh + 1) * (tm // 2)) for hh in range(2)]
        acts = []
        for rows in halves:
            a = jnp.maximum(_dot(h_ref[rows, :], wi_ref[...]), 0.0)
            a_ref[rows, :] = a.astype(BF)
            acts.append((a * a).astype(BF))
        y2s = [_dot(a2, wo_ref[...]) for a2 in acts]
        lane = lax.broadcasted_iota(jnp.int32, (1, 128), 1)
        for rows, y2 in zip(halves, y2s):
            r = _rms_stats(y2)
            out = x2_ref[rows, :] + y2 * r * g4_ref[...]
            err = out - t_ref[rows, :]
            tot = jnp.sum(jnp.sum(err * err, axis=1, keepdims=True), axis=0, keepdims=True) * (0.5 / D)
            loss_ref[...] += jnp.where(lane == 0, tot, 0.0)
            dout = err * (1.0 / D)
            dout_ref[rows, :] = dout
            dy2, dg = _rms_bwd(y2, r, g4_ref[...], dout)
            dy2_ref[rows, :] = dy2.astype(BF)
            dg4_ref[...] += dg

    row = pl.BlockSpec((tm, D), lambda i: (i, 0))
    return pl.pallas_call(
        body, name="mlp_fwd", grid=(S // tm,),
        out_shape=(jax.ShapeDtypeStruct((S, DFF), BF), jax.ShapeDtypeStruct((S, D), BF), jax.ShapeDtypeStruct((S, D), F32),
                   jax.ShapeDtypeStruct((1, 128), F32), jax.ShapeDtypeStruct((1, D), F32)),
        in_specs=[row, _resident((D, DFF)), _resident((DFF, D)), row, row, pl.BlockSpec((1, D), lambda i: (0, 0))],
        out_specs=(pl.BlockSpec((tm, DFF), lambda i: (i, 0)), row, row,
                   pl.BlockSpec((1, 128), lambda i: (0, 0)), pl.BlockSpec((1, D), lambda i: (0, 0))),
        compiler_params=_cp(("arbitrary",)))(h3, wmi, wmo, x2, tgt, g4)


def _mlp_bwd(dy2, a, wmo, wmi, x2, y, dout, g2, g3):
    S = x2.shape[0]
    tm = min(MLP_TM_BWD, S)

    def body(dy2_ref, a_ref, wo_ref, wi_ref, x2_ref, y_ref, dout_ref, g2_ref, g3_ref,
             dpre_ref, dx2_ref, dy_ref, dg3_ref, dg2_ref):
        @pl.when(pl.program_id(0) == 0)
        def _():
            dg3_ref[...] = jnp.zeros_like(dg3_ref)
            dg2_ref[...] = jnp.zeros_like(dg2_ref)

        da2 = _dot_nt(dy2_ref[...], wo_ref[...])
        dpre = (2.0 * a_ref[...].astype(F32) * da2).astype(BF)
        dpre_ref[...] = dpre
        dh3 = _dot_nt(dpre, wi_ref[...])
        x2 = x2_ref[...]
        dx3, dg3 = _rms_bwd(x2, _rms_stats(x2), g3_ref[...], dh3)
        dx2 = dout_ref[...] + dx3
        dx2_ref[...] = dx2
        dg3_ref[...] += dg3
        yv = y_ref[...]
        dy, dg2 = _rms_bwd(yv, _rms_stats(yv), g2_ref[...], dx2)
        dy_ref[...] = dy.astype(BF)
        dg2_ref[...] += dg2

    row = pl.BlockSpec((tm, D), lambda i: (i, 0))
    wide = pl.BlockSpec((tm, DFF), lambda i: (i, 0))
    vec = pl.BlockSpec((1, D), lambda i: (0, 0))
    return pl.pallas_call(
        body, name="mlp_bwd", grid=(S // tm,),
        out_shape=(jax.ShapeDtypeStruct((S, DFF), BF), jax.ShapeDtypeStruct((S, D), F32), jax.ShapeDtypeStruct((S, D), BF),
                   jax.ShapeDtypeStruct((1, D), F32), jax.ShapeDtypeStruct((1, D), F32)),
        in_specs=[row, wide, _resident((DFF, D)), _resident((D, DFF)), row, row, row, vec, vec],
        out_specs=(wide, row, row, vec, vec),
        compiler_params=_cp(("arbitrary",)))(dy2, a, wmo, wmi, x2, y, dout, g2, g3)


def _mm_tn(a, b, name, square_a=False, tm=1024, tn=1024, tk=2048, tie=None, col_blocks=None):
    S, M = a.shape
    N = b.shape[1]
    tk = min(tk, S)
    tm = max(t for t in range(128, min(tm, M) + 1, 128) if M % t == 0)
    tn = max(t for t in range(128, min(tn, N) + 1, 128) if N % t == 0)
    cb = N // col_blocks if col_blocks else tn
    assert M % tm == 0 and S % tk == 0 and tn % cb == 0
    nk = S // tk
    ties = () if tie is None else (tie,)

    def body(a_ref, b_ref, *rest):
        o_ref = rest[-1]
        k = pl.program_id(2)
        av = a_ref[...]
        if square_a:
            av = av * av
        part = _dot_tn(av, b_ref[...])
        if col_blocks:
            part = jnp.stack([part[:, t * cb:(t + 1) * cb] for t in range(tn // cb)])

        @pl.when(k == 0)
        def _():
            o_ref[...] = part

        @pl.when(k > 0)
        def _():
            o_ref[...] += part

    if col_blocks:
        out_shape, out_spec = (col_blocks, M, cb), pl.BlockSpec((tn // cb, tm, cb), lambda i, j, k: (j, i, 0))
    else:
        out_shape, out_spec = (M, N), pl.BlockSpec((tm, tn), lambda i, j, k: (i, j))
    return pl.pallas_call(
        body, name=name, out_shape=jax.ShapeDtypeStruct(out_shape, F32), grid=(M // tm, N // tn, nk),
        in_specs=[pl.BlockSpec((tk, tm), lambda i, j, k: (k, i)), pl.BlockSpec((tk, tn), lambda i, j, k: (k, j))]
        + [pl.BlockSpec(memory_space=pl.ANY)] * len(ties),
        out_specs=out_spec,
        compiler_params=_cp(("parallel", "parallel", "arbitrary")))(a, b, *ties)


def _outproj_bwd(dy, wout, ya, yg, gates, wba, wbg, tie):
    S = dy.shape[0]
    tm = min(512, S)

    def body(dy_ref, wo_ref, ya_ref, yg_ref, gt_ref, wba_ref, wbg_ref, tie_ref,
             dgt_ref, da_ref, db_ref, dyg_ref, e0, e1, e2, s0, s1, s2, scr):
        dm = _dot_nt(dy_ref[...], wo_ref[...])
        ga, gb = gt_ref[:, :D].astype(F32), gt_ref[:, D:].astype(F32)
        dgt_ref[:, :D] = (dm * _dot(ya_ref[...], wba_ref[...]) * ga * (1.0 - ga)).astype(BF)
        dgt_ref[:, D:] = (dm * _dot(yg_ref[...], wbg_ref[...]) * gb * (1.0 - gb)).astype(BF)
        da = (dm * ga).astype(BF)
        db = (dm * gb).astype(BF)
        da_ref[...] = da
        db_ref[...] = db
        dyg_ref[...] = _dot_nt(db, wbg_ref[...]).astype(BF)
        dya = _dot_nt(da, wba_ref[...]).astype(BF).astype(F32)
        dyy = dya * ya_ref[...].astype(F32)
        dsum = jnp.zeros((tm, GW), F32)
        for hh in range(NSLOT):
            hm = _head_mask(hh)
            dsum = dsum + jnp.where(hm, jnp.sum(jnp.where(hm, dyy, 0.0), axis=1, keepdims=True), 0.0)
        for e_ref, s_ref, d in zip((e0, e1, e2), (s0, s1, s2), DILS):
            _dilate_store(dya, scr, e_ref, (), d)
            _dilate_store(dsum, scr, s_ref, (), d)

    row = lambda w: pl.BlockSpec((tm, w), lambda i: (i, 0))
    full = lambda s: pl.BlockSpec(s, lambda i: (0, 0))
    dil = lambda d: pl.BlockSpec((d, tm // d, GW), lambda i: (0, i, 0))
    dshape = lambda d, t: jax.ShapeDtypeStruct((d, S // d, GW), t)
    return pl.pallas_call(
        body, name="outproj_bwd", grid=(S // tm,),
        out_shape=(jax.ShapeDtypeStruct((S, 2 * D), BF), jax.ShapeDtypeStruct((S, D), BF), jax.ShapeDtypeStruct((S, D), BF),
                   jax.ShapeDtypeStruct((S, GMLP_W), BF)) + tuple(dshape(d, BF) for d in DILS)
        + tuple(dshape(d, F32) for d in DILS),
        in_specs=[row(D), full((D, D)), row(GW), row(GMLP_W), row(2 * D), full((GW, D)), full((GMLP_W, D)),
                  pl.BlockSpec(memory_space=pl.ANY)],
        out_specs=(row(2 * D), row(D), row(D), row(GMLP_W)) + tuple(dil(d) for d in DILS) * 2,
        scratch_shapes=[pltpu.VMEM((2, tm, 128), F32)],
        compiler_params=_cp(("parallel",)))(dy, wout, ya, yg, gates, wba, wbg, tie)


def _gmlp_bwd(puz, dyg, ln_g, ln_b, w_sp, w_sp_t, b_sp_t):
    S = puz.shape[0]
    tm = min(512, S)
    nch = tm // CHUNK

    def body(p_ref, dy_ref, g_ref, b_ref, w_ref, wt_ref, bt_ref,
             dp_ref, dw_ref, dbs_ref, dg_ref, dbias_ref, dbacc_ref):
        i = pl.program_id(0)

        @pl.when(i == 0)
        def _():
            dw_ref[...] = jnp.zeros_like(dw_ref)
            dbacc_ref[...] = jnp.zeros_like(dbacc_ref)
            dg_ref[...] = jnp.zeros_like(dg_ref)
            dbias_ref[...] = jnp.zeros_like(dbias_ref)

        tril = _tril()
        ws = [jnp.where(tril, w_ref[gg], 0.0).astype(BF) for gg in range(4)]
        triu = _tril(upper=True)
        wts = [jnp.where(triu, wt_ref[gg], 0.0).astype(BF) for gg in range(4)]
        gain = g_ref[...]
        for ch in range(nch):
            rows = slice(ch * CHUNK, (ch + 1) * CHUNK)
            pz = p_ref[rows, GMLP_W:].astype(F32)
            z = _gelu(pz)
            zhat, rstd, zn = _ln_fwd(z, gain, b_ref[...])
            znb = zn.astype(BF)
            dzn_parts = []
            for gg in range(4):
                cols = slice(gg * CHUNK, (gg + 1) * CHUNK)
                pu = p_ref[rows, cols].astype(F32)
                u = _gelu(pu)
                sz = _dot(ws[gg], znb[:, cols]) + bt_ref[:, gg:gg + 1]
                dyv = dy_ref[rows, cols].astype(F32)
                dp_ref[rows, cols] = (dyv * sz * _gelu_grad(pu)).astype(BF)
                dsz = dyv * u
                dbacc_ref[gg] += dsz
                dszb = dsz.astype(BF)
                dw_ref[gg] += _dot_nt(dszb, znb[:, cols])
                dzn_parts.append(_dot(wts[gg], dszb))
            dzn = jnp.concatenate(dzn_parts, axis=1)
            dg_ref[...] += jnp.sum(dzn * zhat, axis=0, keepdims=True)
            dbias_ref[...] += jnp.sum(dzn, axis=0, keepdims=True)
            dzh = dzn * gain
            dz = rstd * (dzh - jnp.mean(dzh, axis=-1, keepdims=True)
                         - zhat * jnp.mean(dzh * zhat, axis=-1, keepdims=True))
            dp_ref[rows, GMLP_W:] = (dz * _gelu_grad(pz)).astype(BF)

        @pl.when(i == pl.num_programs(0) - 1)
        def _():
            for gg in range(4):
                dw_ref[gg] = jnp.where(tril, dw_ref[gg], 0.0)
                dbs_ref[gg] = jnp.sum(dbacc_ref[gg], axis=1, keepdims=True)

    full2 = lambda s: pl.BlockSpec(s, lambda i: (0, 0))
    full3 = lambda s: pl.BlockSpec(s, lambda i: (0, 0, 0))
    return pl.pallas_call(
        body, name="gmlp_bwd", grid=(S // tm,),
        out_shape=(jax.ShapeDtypeStruct((S, 2 * GMLP_W), BF), jax.ShapeDtypeStruct((4, CHUNK, CHUNK), F32),
                   jax.ShapeDtypeStruct((4, CHUNK, 1), F32), jax.ShapeDtypeStruct((1, GMLP_W), F32),
                   jax.ShapeDtypeStruct((1, GMLP_W), F32)),
        in_specs=[pl.BlockSpec((tm, 2 * GMLP_W), lambda i: (i, 0)), pl.BlockSpec((tm, GMLP_W), lambda i: (i, 0)),
                  full2((1, GMLP_W)), full2((1, GMLP_W)), full3((4, CHUNK, CHUNK)), full3((4, CHUNK, CHUNK)),
                  full2((CHUNK, 4))],
        out_specs=(pl.BlockSpec((tm, 2 * GMLP_W), lambda i: (i, 0)), full3((4, CHUNK, CHUNK)), full3((4, CHUNK, 1)),
                   full2((1, GMLP_W)), full2((1, GMLP_W))),
        scratch_shapes=[pltpu.VMEM((4, CHUNK, CHUNK), F32)],
        compiler_params=_cp(("arbitrary",)))(puz, dyg, ln_g, ln_b, w_sp, w_sp_t, b_sp_t)


def _attn_bwd(qkv, dya, dsums, lse, name):
    _, d, L, _ = qkv.shape
    nsub = min(ATTN_NSUB, L // QB)
    R = nsub * QB
    nsteps = L // R

    def body(q_ref, qn_ref, kp_ref, kc_ref, vp_ref, vc_ref, dy_ref, dyn_ref, e_ref, en_ref, l_ref, ln_ref, o_ref):
        i = pl.program_id(1)
        band, band_first = _band_masks(i == 0)
        row = lax.broadcasted_iota(jnp.int32, (QB, QB), 0)
        col = lax.broadcasted_iota(jnp.int32, (QB, QB), 1)
        mask_next = jnp.logical_and(col >= row, i < nsteps - 1)
        kc, vc = kc_ref[...], vc_ref[...]
        kfull = jnp.concatenate([kp_ref[...], kc], axis=0)
        vfull = jnp.concatenate([vp_ref[...], vc], axis=0)
        k_last, v_last = kc[(nsub - 1) * QB:], vc[(nsub - 1) * QB:]
        q_ext = jnp.concatenate([q_ref[...], qn_ref[...]], axis=0)
        dy_ext = jnp.concatenate([dy_ref[...], dyn_ref[...]], axis=0)
        esum, esum_n, lse, lse_n = e_ref[...], en_ref[...], l_ref[...], ln_ref[...]
        win = lambda t, sb: t[sb * QB:(sb + 2) * QB]
        blk = lambda t, sb: t[sb * QB:(sb + 1) * QB]
        hms = [_head_mask(hh) for hh in range(NSLOT)]
        q_hs = [jnp.where(hm, q_ext, 0) for hm in hms]
        dy_hs = [jnp.where(hm, dy_ext, 0) for hm in hms]
        raw = []
        for hh in range(NSLOT):
            tiles = [(_dot_nt(blk(q_hs[hh], sb), win(kfull, sb)), _dot_nt(blk(dy_hs[hh], sb), win(vfull, sb)))
                     for sb in range(nsub)]
            tiles.append((_dot_nt(blk(q_hs[hh], nsub), k_last), _dot_nt(blk(dy_hs[hh], nsub), v_last)))
            raw.append(tiles)
        ps, dss = [], []
        for hh in range(NSLOT):
            rowstat = lambda t: jnp.max(jnp.where(hms[hh], t, -jnp.inf), axis=1, keepdims=True)
            p_h, ds_h = [], []
            for sb in range(nsub + 1):
                sc, dp = raw[hh][sb]
                if sb < nsub:
                    msk, lrow, erow = (band_first if sb == 0 else band), rowstat(blk(lse, sb)), rowstat(blk(esum, sb))
                else:
                    msk, lrow, erow = mask_next, rowstat(lse_n), rowstat(esum_n)
                p = jnp.where(msk, jnp.exp(sc * SCALE - lrow), 0.0)
                p_h.append(p.astype(BF))
                ds_h.append((p * (dp - erow)).astype(BF))
            ps.append(p_h)
            dss.append(ds_h)
        dq = [jnp.zeros((QB, GW), F32) for _ in range(nsub)]
        dk = [jnp.zeros((QB, GW), F32) for _ in range(nsub)]
        dv = [jnp.zeros((QB, GW), F32) for _ in range(nsub)]
        for hh in range(NSLOT):
            for sb in range(nsub):
                dq[sb] = dq[sb] + jnp.where(hms[hh], _dot(dss[hh][sb], win(kfull, sb)), 0.0)
                nxt = lambda t: t[sb + 1][:, :QB] if sb + 1 < nsub else t[nsub]
                dk[sb] = dk[sb] + _dot_tn(jnp.concatenate([dss[hh][sb][:, QB:], nxt(dss[hh])], axis=0), win(q_hs[hh], sb))
                dv[sb] = dv[sb] + _dot_tn(jnp.concatenate([ps[hh][sb][:, QB:], nxt(ps[hh])], axis=0), win(dy_hs[hh], sb))
        for sb in range(nsub):
            rows = slice(sb * QB, (sb + 1) * QB)
            o_ref[0, rows, :] = (dq[sb] * SCALE).astype(BF)
            o_ref[1, rows, :] = (dk[sb] * SCALE).astype(BF)
            o_ref[2, rows, :] = dv[sb].astype(BF)

    prev = lambda i: jnp.maximum(i * nsub - 1, 0)
    nxt = lambda i: jnp.minimum((i + 1) * nsub, L // QB - 1)
    cur4 = lambda t: pl.BlockSpec((None, None, R, GW), lambda r, i: (t, r, i, 0))
    prv4 = lambda t: pl.BlockSpec((None, None, QB, GW), lambda r, i: (t, r, prev(i), 0))
    nxt4 = lambda t: pl.BlockSpec((None, None, QB, GW), lambda r, i: (t, r, nxt(i), 0))
    cur3 = pl.BlockSpec((None, R, GW), lambda r, i: (r, i, 0))
    nxt3 = pl.BlockSpec((None, QB, GW), lambda r, i: (r, nxt(i), 0))
    return pl.pallas_call(
        body, name=name, grid=(d, nsteps), out_shape=jax.ShapeDtypeStruct((3, d, L, GW), BF),
        in_specs=[cur4(0), nxt4(0), prv4(1), cur4(1), prv4(2), cur4(2), cur3, nxt3, cur3, nxt3, cur3, nxt3],
        out_specs=pl.BlockSpec((3, None, R, GW), lambda r, i: (0, r, i, 0)),
        compiler_params=_cp(("parallel", "arbitrary")))(qkv, qkv, qkv, qkv, qkv, qkv, dya, dya, dsums, dsums, lse, lse)


def _inproj_bwd(dqkvs, dpuz, dgates, win_t, cos_t, sin_t, x, dx2, g1):
    S = x.shape[0]
    tm = min(512, S)

    def body(d0_ref, d1_ref, d2_ref, dp_ref, dg_ref, w_ref, c_ref, s_ref,
             x_ref, dx2_ref, g1_ref, gx_ref, dg1_ref, dn_ref, scr):
        i = pl.program_id(0)

        @pl.when(i == 0)
        def _():
            dg1_ref[...] = jnp.zeros_like(dg1_ref)

        dh = _dot(dp_ref[...], w_ref[W_UZ0:W_G0, :]) + _dot(dg_ref[...], w_ref[W_G0:, :])
        for t in range(3):
            for g, (d_ref, d) in enumerate(zip((d0_ref, d1_ref, d2_ref), DILS)):
                piece = _undilate_load(d_ref, (t,), d, scr, tm)
                if t < 2:
                    piece = _rope_bwd(piece, c_ref, s_ref)
                dn_ref[:, (3 * t + g) * GW:(3 * t + g + 1) * GW] = piece.astype(BF)
        dh = dh + _dot(dn_ref[...], w_ref[:W_UZ0, :])
        xv = x_ref[...]
        dx1, dg1 = _rms_bwd(xv, _rms_stats(xv), g1_ref[...], dh)
        gx_ref[...] = dx2_ref[...] + dx1
        dg1_ref[...] += dg1

    row = lambda w: pl.BlockSpec((tm, w), lambda i: (i, 0))
    full = lambda s: pl.BlockSpec(s, lambda i: (0, 0))
    dil = lambda d: pl.BlockSpec((3, d, tm // d, GW), lambda i: (0, 0, i, 0))
    return pl.pallas_call(
        body, name="inproj_bwd", grid=(S // tm,),
        out_shape=(jax.ShapeDtypeStruct((S, D), F32), jax.ShapeDtypeStruct((1, D), F32),
                   jax.ShapeDtypeStruct((S, 3 * AW), BF)),
        in_specs=[dil(d) for d in DILS] + [row(2 * GMLP_W), row(2 * D), _resident(win_t.shape), row(128), row(128), row(D), row(D),
                                            full((1, D))],
        out_specs=(row(D), full((1, D)), row(3 * AW)),
        scratch_shapes=[pltpu.VMEM((2, tm, 128), F32)],
        compiler_params=_cp(("arbitrary",)))(*dqkvs, dpuz, dgates, win_t, cos_t, sin_t, x, dx2, g1)


def _row_tile(rows, cap=256):
    return max(t for t in range(16, cap + 1, 16) if rows % t == 0)


def _adam_math(w, g, m, v):
    m2 = ADAM_B1 * m + (1.0 - ADAM_B1) * g
    v2 = ADAM_B2 * v + (1.0 - ADAM_B2) * (g * g)
    m_hat = m2 / (1.0 - ADAM_B1 ** ADAM_STEP)
    v_hat = v2 / (1.0 - ADAM_B2 ** ADAM_STEP)
    delta = -ADAM_LR * (m_hat / (jnp.sqrt(v_hat) + ADAM_EPS) + ADAM_WD * w)
    return delta, m2, v2


def _adam_shard(own, recv, w, m, v, name):
    R, C = w.shape
    tr = _row_tile(R)

    def body(own_ref, r_ref, w_ref, m_ref, v_ref, g_ref, d_ref, m2_ref, v2_ref):
        g = own_ref[...] + r_ref[0].astype(F32) + r_ref[1].astype(F32) + r_ref[2].astype(F32)
        g_ref[...] = g
        d_ref[...], m2_ref[...], v2_ref[...] = _adam_math(w_ref[...], g, m_ref[...], v_ref[...])

    spec = pl.BlockSpec((tr, C), lambda i: (i, 0))
    out = jax.ShapeDtypeStruct((R, C), F32)
    return pl.pallas_call(
        body, name=name, grid=(R // tr,), out_shape=(out, out, out, out),
        in_specs=[spec, pl.BlockSpec((3, tr, C), lambda i: (0, i, 0)), spec, spec, spec],
        out_specs=(spec, spec, spec, spec), compiler_params=_cp(("parallel",)))(own, recv, w, m, v)


def _rs_add(gblocks, recv, idx, name):
    _, R, C = gblocks.shape
    tr = _row_tile(R)

    def body(t_ref, g_ref, r_ref, own_ref, send_ref):
        j = pl.program_id(1)
        s = g_ref[...] + r_ref[...]

        @pl.when(j == 0)
        def _():
            own_ref[...] = s

        @pl.when(j > 0)
        def _():
            send_ref[...] = s.astype(BF)

    grid_spec = pltpu.PrefetchScalarGridSpec(
        num_scalar_prefetch=1, grid=(R // tr, 4),
        in_specs=[pl.BlockSpec((None, tr, C), lambda i, j, t: (t[j], i, 0)),
                  pl.BlockSpec((None, tr, C), lambda i, j, t: (t[4 + j], i, 0))],
        out_specs=[pl.BlockSpec((tr, C), lambda i, j, t: (i, 0)),
                   pl.BlockSpec((None, tr, C), lambda i, j, t: (jnp.maximum(j - 1, 0), i, 0))])
    return pl.pallas_call(
        body, name=name, grid_spec=grid_spec,
        out_shape=(jax.ShapeDtypeStruct((R, C), F32), jax.ShapeDtypeStruct((3, R, C), BF)),
        compiler_params=_cp(("parallel", "arbitrary")))(idx, gblocks, recv)


def _mesh_pos():
    return lax.axis_index("x"), lax.axis_index("y"), lax.axis_index("c")


_HBM = pl.BlockSpec(memory_space=pltpu.HBM)
_SEM = pl.BlockSpec(memory_space=pltpu.SEMAPHORE)
_EFFECT = pltpu.SideEffectType.DATAFLOW_SIDE_EFFECTING
_RELATIONS = [(dx, dy, dc) for dx in (0, 1) for dy in (0, 1) for dc in (0, 1)][1:]


def _flip(v, d):
    return 1 - v if d else v


def _plan_gather(n):
    def plan(x, y, c):
        return [(k, None, 4 * x + 2 * y + c, (_flip(x, dx), _flip(y, dy), _flip(c, dc)))
                for k in range(n) for dx, dy, dc in _RELATIONS]
    return plan


def _plan_gather_near(x, y, c):
    chips = [(1 - x, y), (x, 1 - y), (1 - x, 1 - y)]
    return [(0, None, 4 * x + 2 * y + c, (x, y, 1 - c))] + [(0, None, 4 * x + 2 * y + c, (*chip, c)) for chip in chips]


def _plan_gather_pass(from_landing):
    def plan(x, y, c):
        blocks = [4 * cx + 2 * cy + c for cx, cy in ((1 - x, y), (x, 1 - y), (1 - x, 1 - y))]
        return [(0, b if from_landing else None, b, (x, y, 1 - c)) for b in blocks]
    return plan


def _plan_d2d(n):
    def plan(x, y, c):
        return [(k, 2 * kk + 1 - c, kk, (x, y, 1 - c)) for k in range(n) for kk in range(4)]
    return plan


def _plan_ici(n):
    def plan(x, y, c):
        return [(k, j, j, (_flip(x, dx), _flip(y, dy), c))
                for k in range(n) for j, (dx, dy) in enumerate(((1, 0), (0, 1), (1, 1)))]
    return plan


def _plan_copies(plan, src_refs, land_refs, send_sems, recv_sems):
    x, y, c = _mesh_pos()
    return [pltpu.make_async_remote_copy(
        src_ref=src_refs[k] if si is None else src_refs[k].at[si], dst_ref=land_refs[k].at[di],
        send_sem=send_sems.at[n], recv_sem=recv_sems.at[n], device_id=dev, device_id_type=MESH)
        for n, (k, si, di, dev) in enumerate(plan(x, y, c))]


def _exchange_start(srcs, land_shapes, plan, ncopies, name, after):
    n = len(srcs)

    def body(*refs):
        src_refs, land_refs = refs[:n], refs[n:2 * n]
        send_sems, recv_sems = refs[2 * n + len(after)], refs[2 * n + len(after) + 1]
        token = refs[-1]
        for cp in _plan_copies(plan, src_refs, land_refs, send_sems, recv_sems):
            cp.start()
        token[...] = jnp.zeros_like(token)

    lands = [pltpu.with_memory_space_constraint(lax.empty(s, a.dtype), pltpu.HBM) for s, a in zip(land_shapes, srcs)]
    srcs = [pltpu.with_memory_space_constraint(a, pltpu.HBM) for a in srcs]
    outs = pl.pallas_call(
        body, name=name,
        out_shape=(pltpu.SemaphoreType.DMA((ncopies,)), pltpu.SemaphoreType.DMA((ncopies,)))
        + tuple(pltpu.HBM(a.shape, a.dtype) for a in srcs) + tuple(pltpu.HBM(a.shape, a.dtype) for a in lands)
        + (jax.ShapeDtypeStruct((8, 128), F32),),
        in_specs=[_HBM] * (2 * n) + [pl.BlockSpec(memory_space=pl.ANY)] * len(after),
        out_specs=(_SEM, _SEM) + (_HBM,) * (2 * n) + (pl.BlockSpec(memory_space=pltpu.VMEM),),
        input_output_aliases={i: 2 + i for i in range(2 * n)},
        compiler_params=pltpu.CompilerParams(has_side_effects=_EFFECT))(*srcs, *lands, *after)
    return (outs[0], outs[1], list(outs[2:2 + n]), list(outs[2 + n:2 + 2 * n])), outs[-1]


def _exchange_forward(handle, plan, plan_fwd, nfwd, name, after):
    send_sems, recv_sems, srcs, lands = handle
    n = len(srcs)

    def body(*refs):
        src_refs, land_refs = refs[:n], refs[n:2 * n]
        outs = refs[2 * n + 2 + len(after):]
        for cp in _plan_copies(plan, src_refs, land_refs, refs[2 * n], refs[2 * n + 1]):
            cp.wait_send()
            cp.wait_recv()
        for cp in _plan_copies(plan_fwd, land_refs, land_refs, outs[0], outs[1]):
            cp.start()

    outs = pl.pallas_call(
        body, name=name,
        out_shape=(pltpu.SemaphoreType.DMA((nfwd,)), pltpu.SemaphoreType.DMA((nfwd,)))
        + tuple(pltpu.HBM(a.shape, a.dtype) for a in srcs) + tuple(pltpu.HBM(a.shape, a.dtype) for a in lands),
        in_specs=[_HBM] * (2 * n) + [_SEM, _SEM] + [pl.BlockSpec(memory_space=pl.ANY)] * len(after),
        out_specs=(_SEM, _SEM) + (_HBM,) * (2 * n), input_output_aliases={i: 2 + i for i in range(2 * n)},
        compiler_params=pltpu.CompilerParams(has_side_effects=_EFFECT))(*srcs, *lands, send_sems, recv_sems, *after)
    return outs[0], outs[1], list(outs[2:2 + n]), list(outs[2 + n:2 + 2 * n])


def _exchange_wait(handle, plan, name, after):
    send_sems, recv_sems, srcs, lands = handle
    n = len(srcs)

    def body(*refs):
        src_refs, land_refs = refs[:n], refs[n:2 * n]
        for cp in _plan_copies(plan, src_refs, land_refs, refs[2 * n], refs[2 * n + 1]):
            cp.wait_send()
            cp.wait_recv()

    outs = pl.pallas_call(
        body, name=name,
        out_shape=tuple(pltpu.HBM(a.shape, a.dtype) for a in srcs) + tuple(pltpu.HBM(a.shape, a.dtype) for a in lands),
        in_specs=[_HBM] * (2 * n) + [_SEM, _SEM] + [pl.BlockSpec(memory_space=pl.ANY)] * len(after),
        out_specs=(_HBM,) * (2 * n), input_output_aliases={i: i for i in range(2 * n)},
        compiler_params=pltpu.CompilerParams(has_side_effects=_EFFECT))(*srcs, *lands, send_sems, recv_sems, *after)
    return list(outs[:n]), list(outs[n:])


SMALL = ("g1", "g2", "g3", "g4", "ln_g", "ln_b", "b_sp", "w_sp")


def _small_step(loss_row, grads, ws, ms, vs):
    parts = [loss_row] + [grads[k] for k in SMALL]
    n, ns = len(parts), len(SMALL)

    def body(*refs):
        p_refs = refs[:n]
        w_refs, m_refs, v_refs = (refs[n + i * ns:n + (i + 1) * ns] for i in range(3))
        o = n + 3 * ns
        loss_ref = refs[o]
        g_out, d_out, m_out, v_out = (refs[o + 1 + i * ns:o + 1 + (i + 1) * ns] for i in range(4))
        bufs = refs[o + 1 + 4 * ns:o + 1 + 4 * ns + n]
        send_sems, recv_sems = refs[-2:]
        x, y, c = _mesh_pos()
        me, sib = (x, y, c), (x, y, 1 - c)
        chips = [(1 - x, y), (x, 1 - y), (1 - x, 1 - y)]

        def rcopy(k, s, block, to, src=None):
            dst = bufs[k].at[4 * block[0] + 2 * block[1] + block[2]]
            return pltpu.make_async_remote_copy(
                src_ref=dst if src is None else src, dst_ref=dst, send_sem=send_sems.at[k, s],
                recv_sem=recv_sems.at[k, s], device_id=to, device_id_type=MESH)

        started = []
        for k in range(n):
            bufs[k][4 * x + 2 * y + c] = p_refs[k][...]
            started.append(rcopy(k, 0, me, sib, src=p_refs[k]))
            started += [rcopy(k, 1 + j, me, (*chip, c), src=p_refs[k]) for j, chip in enumerate(chips)]
        for cp in started:
            cp.start()
        for k in range(n):
            for j, chip in enumerate(chips):
                rcopy(k, 1 + j, (*chip, c), me).wait_recv()
                started.append(rcopy(k, 4 + j, (*chip, c), sib))
                started[-1].start()
        for k in range(n):
            rcopy(k, 0, sib, me).wait_recv()
            for j, chip in enumerate(chips):
                rcopy(k, 4 + j, (*chip, 1 - c), me).wait_recv()
        for cp in started:
            cp.wait_send()
        sums = []
        for k in range(n):
            acc = bufs[k][0]
            for b in range(1, 8):
                acc = acc + bufs[k][b]
            sums.append(acc)
        loss_ref[...] = sums[0]
        for i in range(ns):
            g_out[i][...] = sums[1 + i]
            d_out[i][...], m_out[i][...], v_out[i][...] = _adam_math(w_refs[i][...], sums[1 + i], m_refs[i][...],
                                                                     v_refs[i][...])

    args = parts + [t[k] for t in (ws, ms, vs) for k in SMALL]
    shapes = [jax.ShapeDtypeStruct(loss_row.shape, F32)] + [jax.ShapeDtypeStruct(grads[k].shape, F32) for k in SMALL] * 4
    vmem = pl.BlockSpec(memory_space=pltpu.VMEM)
    outs = pl.pallas_call(
        body, name="small_step", out_shape=tuple(shapes), in_specs=[vmem] * len(args), out_specs=(vmem,) * len(shapes),
        scratch_shapes=[pltpu.VMEM((8,) + p.shape, F32) for p in parts]
        + [pltpu.SemaphoreType.DMA((n, 7)), pltpu.SemaphoreType.DMA((n, 7))],
        compiler_params=pltpu.CompilerParams(vmem_limit_bytes=VMEM_LIMIT))(*args)
    groups = [dict(zip(SMALL, outs[1 + i * ns:1 + (i + 1) * ns])) for i in range(4)]
    return (outs[0], *groups)


def _rope_tables(S):
    half = HD // 2
    inv_freq = jnp.tile(ROPE_THETA ** (-jnp.arange(half, dtype=F32) / half), 4)
    sign = jnp.tile(jnp.concatenate([-jnp.ones(half, F32), jnp.ones(half, F32)]), 2)
    ang = jnp.arange(S, dtype=F32)[:, None] * inv_freq[None, :]
    return jnp.cos(ang), jnp.sin(ang) * sign[None, :]


def _to_blocks(g, col_sharded):
    if col_sharded:
        return g.reshape(g.shape[0], 8, g.shape[1] // 8).transpose(1, 0, 2)
    return g.reshape(8, g.shape[0] // 8, g.shape[1])


def _from_blocks(t, col_sharded):
    if col_sharded:
        return t.transpose(1, 0, 2).reshape(t.shape[1], 8 * t.shape[2])
    return t.reshape(8 * t.shape[1], t.shape[2])


class _NoComm:
    def __init__(self, late_weights):
        self._late = late_weights
        self.grads = {}

    def start_tie(self):
        return jnp.zeros((8, 128), F32)

    def late_weights(self, after):
        return self._late

    def rs_start(self, key, gblocks, after=()):
        self.grads[key] = gblocks
        return jnp.zeros((8, 128), F32)

    def rs_mid(self, key, after):
        return jnp.zeros((8, 128), F32)


class _FsdpComm:
    def __init__(self, late_shards, col_sharded, after, idx, block):
        self._col, self._idx, self._block, self._rs = col_sharded, idx, block, {}
        n = len(late_shards)
        self._gather, self._token = _exchange_start(
            late_shards, [(8,) + s.shape for s in late_shards], _plan_gather(n), 7 * n, "ag_late_start", (after,))

    def start_tie(self):
        return self._token

    def late_weights(self, after):
        shards, lands = _exchange_wait(self._gather, _plan_gather(len(self._col)), "ag_late_wait", after)
        lands = [lax.dynamic_update_index_in_dim(t, s, self._block, 0) for t, s in zip(lands, shards)]
        return [_from_blocks(t, cs) for t, cs in zip(lands, self._col)]

    def rs_start(self, key, gblocks, after=()):
        n = len(gblocks)
        d2d, token = _exchange_start(gblocks, [(4,) + g.shape[1:] for g in gblocks], _plan_d2d(n), 4 * n,
                                     "rs_%s_d2d_start" % key, (self._token,) + tuple(after))
        self._rs[key] = dict(n=n, d2d=d2d)
        return token

    def rs_mid(self, key, after):
        st = self._rs[key]
        gblocks, from_sib = _exchange_wait(st["d2d"], _plan_d2d(st["n"]), "rs_%s_d2d_wait" % key, after)
        halves = [_rs_add(g, r, self._idx, "rs_add_%s_%d" % (key, k)) for k, (g, r) in enumerate(zip(gblocks, from_sib))]
        st["own"] = [own for own, _ in halves]
        sends = [send for _, send in halves]
        st["ici"], token = _exchange_start(sends, [t.shape for t in sends], _plan_ici(st["n"]), 3 * st["n"],
                                           "rs_%s_ici_start" % key, (self._token,))
        return token

    def rs_end(self, key, after):
        st = self._rs[key]
        return st["own"], _exchange_wait(st["ici"], _plan_ici(st["n"]), "rs_%s_ici_wait" % key, after)[1]


def _local_step(x, tgt, h1, cos_t, sin_t, win_t, comm, g1, g2, g3, g4, ln_g, ln_b, w_sp, b_sp):
    b_sp_t = b_sp.T
    w_sp_t = w_sp.transpose(0, 2, 1)

    qkvs = [_proj_qkv(h1, win_t, cos_t, sin_t, g, d, "proj_qkv_d%d" % d, comm.start_tie()) for g, d in enumerate(DILS)]
    puz = _proj(h1, win_t, W_UZ0, W_G0 - W_UZ0, "none", "proj_uz", 1024)
    gates = _proj(h1, win_t, W_G0, 2 * D, "sigmoid", "proj_gates", 512)
    fwd = [_attn_fwd(t, "attn_fwd_d%d" % d) for t, d in zip(qkvs, DILS)]
    ya, *lses = _attn_combine([o for o, _ in fwd], [l for _, l in fwd])
    yg = _gmlp_fwd(puz, ln_g, ln_b, w_sp, b_sp_t)
    wba, wbg, wout, wmi, wmo = comm.late_weights(after=(ya, yg, gates))
    merged, y, x2, h3 = _merge_fwd(ya, yg, gates, wba, wbg, wout, x, g2, g3)
    a, dy2, dout, loss_row, dg4 = _mlp_fwd(h3, wmi, wmo, x2, tgt, g4)

    dpre, dx2, dy, dg3, dg2 = _mlp_bwd(dy2, a, wmo, wmi, x2, y, dout, g2, g3)
    dwmo = _mm_tn(a, dy2, "dw_mlp_out", square_a=True)
    dwmi = _mm_tn(h3, dpre, "dw_mlp_in", col_blocks=8)
    tie = comm.rs_start("mlp", [dwmi, _to_blocks(dwmo, False)])
    dgates, da, db, dyg, *rest = _outproj_bwd(dy, wout, ya, yg, gates, wba, wbg, tie)
    dyas, dsums = rest[:3], rest[3:]
    tie = comm.rs_mid("mlp", after=(dyg,))
    dpuz, dwsp, dbs, dlng, dlnb = _gmlp_bwd(puz, dyg, ln_g + tie[0, 0], ln_b, w_sp, w_sp_t, b_sp_t)
    dqkvs = [_attn_bwd(qkvs[g], dyas[g], dsums[g], lses[g], "attn_bwd_d%d" % d) for g, d in enumerate(DILS)]
    grad_x, dg1, dqkv = _inproj_bwd(dqkvs, dpuz, dgates, win_t, cos_t, sin_t, x, dx2, g1)
    dwin_t = jnp.concatenate([_mm_tn(t, h1, "dw_in_%d" % n) for n, t in enumerate((dqkv, dpuz, dgates))], axis=0)
    dwout = _mm_tn(merged, dy, "dw_out")
    dwba = _mm_tn(ya, da, "dw_branch_attn")
    dwbg = _mm_tn(yg, db, "dw_branch_gmlp")
    tie = comm.rs_start("tail", [_to_blocks(dwin_t, False), _to_blocks(dwba, True), _to_blocks(dwbg, True),
                                 _to_blocks(dwout, False)])
    small = dict(g1=dg1, g2=dg2, g3=dg3, g4=dg4, ln_g=dlng, ln_b=dlnb, b_sp=dbs.reshape(4, CHUNK),
                 w_sp=dwsp.reshape(4 * CHUNK, CHUNK))
    return loss_row + tie[:1], grad_x, small


def kernel(x, norm_pre_mix, w_in, w_spatial, b_spatial, ln_v_gain, ln_v_bias, w_branch_attn, w_branch_gmlp, w_out, norm_post_mix, norm_pre_mlp, w_mlp_in, w_mlp_out, norm_post_mlp, loss_target, m_norm_pre_mix, m_w_in, m_w_spatial, m_b_spatial, m_ln_v_gain, m_ln_v_bias, m_w_branch_attn, m_w_branch_gmlp, m_w_out, m_norm_post_mix, m_norm_pre_mlp, m_w_mlp_in, m_w_mlp_out, m_norm_post_mlp, v_norm_pre_mix, v_w_in, v_w_spatial, v_b_spatial, v_ln_v_gain, v_ln_v_bias, v_w_branch_attn, v_w_branch_gmlp, v_w_out, v_norm_post_mix, v_norm_pre_mlp, v_w_mlp_in, v_w_mlp_out, v_norm_post_mlp):
    mx, my, mc = _mesh_pos()
    rel = [(0, 0), (1, 0), (0, 1), (1, 1)]
    chip_of = [2 * (mx ^ dx) + (my ^ dy) for dx, dy in rel]
    idx = jnp.stack([2 * k + mc for k in chip_of] + chip_of).astype(jnp.int32)

    w_in_t, m_w_in_t, v_w_in_t = (t[0].T for t in (w_in, m_w_in, v_w_in))
    shard = w_in_t.astype(BF)
    gather, token = _exchange_start([shard], [(8,) + shard.shape], _plan_gather_near, 4, "ag_win_start", ())
    h1 = _rms_fwd(x[0], norm_pre_mix + token[0, 0], "rms_pre_mix")
    cos_t, sin_t = _rope_tables(x.shape[1])
    gather = _exchange_forward(gather, _plan_gather_near, _plan_gather_pass(True), 3, "ag_win_pass", (h1, cos_t, sin_t))
    (shard,), (g_win,) = _exchange_wait(gather, _plan_gather_pass(False), "ag_win_wait", ())
    g_win = lax.dynamic_update_index_in_dim(g_win, shard, 4 * mx + 2 * my + mc, 0)
    win_t = _from_blocks(g_win, False)
    late = [w_branch_attn[0], w_branch_gmlp[0], w_out[0], w_mlp_in[0], w_mlp_out[0]]
    comm = _FsdpComm([w.astype(BF) for w in late], [True, True, False, True, False], g_win, idx, 4 * mx + 2 * my + mc)

    loss_row, grad_x, small = _local_step(
        x[0], loss_target[0], h1, cos_t, sin_t, win_t, comm,
        norm_pre_mix, norm_post_mix, norm_pre_mlp, norm_post_mlp, ln_v_gain, ln_v_bias, w_spatial[0], b_spatial[0])

    flat = lambda t: t.reshape(-1, t.shape[-1])
    small_w = dict(zip(SMALL, map(flat, (norm_pre_mix, norm_post_mix, norm_pre_mlp, norm_post_mlp, ln_v_gain, ln_v_bias,
                                         b_spatial, w_spatial))))
    small_m = dict(zip(SMALL, map(flat, (m_norm_pre_mix, m_norm_post_mix, m_norm_pre_mlp, m_norm_post_mlp, m_ln_v_gain,
                                         m_ln_v_bias, m_b_spatial, m_w_spatial))))
    small_v = dict(zip(SMALL, map(flat, (v_norm_pre_mix, v_norm_post_mix, v_norm_pre_mlp, v_norm_post_mlp, v_ln_v_gain,
                                         v_ln_v_bias, v_b_spatial, v_w_spatial))))
    loss_out, sg, sd, sm, sv = _small_step(loss_row, small, small_w, small_m, small_v)
    tie = comm.rs_mid("tail", after=(loss_out,))
    mlp_own, mlp_chips = comm.rs_end("mlp", after=(tie,))
    adam = lambda nm, own, r, w, m, v: _adam_shard(own, r, w[0], m[0], v[0], "adam_" + nm)
    upd = {
        "w_mlp_in": adam("w_mlp_in", mlp_own[0], mlp_chips[0], w_mlp_in, m_w_mlp_in, v_w_mlp_in),
        "w_mlp_out": adam("w_mlp_out", mlp_own[1], mlp_chips[1], w_mlp_out, m_w_mlp_out, v_w_mlp_out),
    }
    loss = loss_out[0, 0]
    own, chips = comm.rs_end("tail", after=(upd["w_mlp_in"][0], upd["w_mlp_out"][0]))
    upd["w_in"] = tuple(t.T for t in _adam_shard(own[0], chips[0], w_in_t, m_w_in_t, v_w_in_t, "adam_w_in"))
    upd["w_branch_attn"] = adam("w_branch_attn", own[1], chips[1], w_branch_attn, m_w_branch_attn, v_w_branch_attn)
    upd["w_branch_gmlp"] = adam("w_branch_gmlp", own[2], chips[2], w_branch_gmlp, m_w_branch_gmlp, v_w_branch_gmlp)
    upd["w_out"] = adam("w_out", own[3], chips[3], w_out, m_w_out, v_w_out)
    order = ["g1", "w_in", "w_sp", "b_sp", "ln_g", "ln_b", "w_branch_attn", "w_branch_gmlp", "w_out", "g2", "g3",
             "w_mlp_in", "w_mlp_out", "g4"]
    small_shape = dict(g1=norm_pre_mix.shape, g2=norm_post_mix.shape, g3=norm_pre_mlp.shape, g4=norm_post_mlp.shape,
                       ln_g=ln_v_gain.shape, ln_b=ln_v_bias.shape, b_sp=b_spatial.shape, w_sp=w_spatial.shape)

    def pick(which):
        return [upd[nm][which][None] if nm in upd else (sg, sd, sm, sv)[which][nm].reshape(small_shape[nm])
                for nm in order]

    return (loss, grad_x[None], *pick(0), *pick(1), *pick(2), *pick(3))
```

```python
import functools
import math

import jax
import jax.numpy as jnp
from jax import lax
from jax.experimental import pallas as pl
from jax.experimental.pallas import tpu as pltpu

D = 1024
HD = 64
NSLOT = 4
GW = NSLOT * HD
DILS = (1, 4, 16)
QB = 128
ATTN_NSUB = 4
PROJ_QKV_TM = 1024
MLP_TM_FWD = 512
MLP_TM_BWD = 256
AW = 3 * GW
W_UZ0, W_G0 = 3 * AW, 3 * AW + 1024
GMLP_W = 512
CHUNK = 128
DFF = 4096
EPS = 1e-6
ROPE_THETA = 10000.0
SCALE = HD ** -0.5
NEG = -1e30

ADAM_LR = 0.001
ADAM_B1 = 0.9
ADAM_B2 = 0.999
ADAM_EPS = 1e-08
ADAM_WD = 0.01
ADAM_STEP = 10

BF = jnp.bfloat16
F32 = jnp.float32
MESH = pl.DeviceIdType.MESH
VMEM_LIMIT = 56 * 1024 * 1024


def _cp(sem):
    return pltpu.CompilerParams(dimension_semantics=sem, vmem_limit_bytes=VMEM_LIMIT)


def _dot(a, b):
    return jnp.dot(a, b, preferred_element_type=F32)


def _dot_nt(a, b):
    return lax.dot_general(a, b, (((1,), (1,)), ((), ())), preferred_element_type=F32)


def _dot_tn(a, b):
    return lax.dot_general(a, b, (((0,), (0,)), ((), ())), preferred_element_type=F32)


def _gelu(x):
    return jax.nn.gelu(x, approximate=True)


def _gelu_grad(x):
    k = math.sqrt(2.0 / math.pi)
    t = jnp.tanh(k * (x + 0.044715 * x * x * x))
    return 0.5 * (1.0 + t) + 0.5 * x * (1.0 - t * t) * (k * (1.0 + 3.0 * 0.044715 * x * x))


def _swap_halves(t):
    w = t.shape[1]
    lane = lax.broadcasted_iota(jnp.int32, t.shape, 1)
    first = (lane & (HD - 1)) < (HD // 2)
    return jnp.where(first, pltpu.roll(t, w - HD // 2, 1), pltpu.roll(t, HD // 2, 1))


def _head_mask(hh):
    lane = lax.broadcasted_iota(jnp.int32, (1, GW), 1)
    return jnp.logical_and(lane >= hh * HD, lane < (hh + 1) * HD)


def _rms_stats(xf):
    return lax.rsqrt(jnp.mean(xf * xf, axis=-1, keepdims=True) + EPS)


def _rms_bwd(xf, r, gain, dout):
    n = xf * r
    t = dout * gain
    dx = r * (t - n * jnp.mean(t * n, axis=-1, keepdims=True))
    return dx, jnp.sum(dout * n, axis=0, keepdims=True)


def _rms_fwd(x, gain, name):
    S = x.shape[0]
    tm = min(512, S)

    def body(x_ref, g_ref, h_ref):
        xf = x_ref[...]
        h_ref[...] = (xf * _rms_stats(xf) * g_ref[...]).astype(BF)

    return pl.pallas_call(
        body, name=name, out_shape=jax.ShapeDtypeStruct((S, D), BF), grid=(S // tm,),
        in_specs=[pl.BlockSpec((tm, D), lambda i: (i, 0)), pl.BlockSpec((1, D), lambda i: (0, 0))],
        out_specs=pl.BlockSpec((tm, D), lambda i: (i, 0)), compiler_params=_cp(("parallel",)))(x, gain)


def _proj(h, w_t, row0, N, epi, name, tm):
    S, K = h.shape
    tm = min(tm, S)
    nblk = N // GW

    def body(h_ref, *refs):
        hv = h_ref[...]
        o_ref = refs[-1]
        for b in range(nblk):
            y = _dot_nt(hv, refs[b][...])
            if epi == "sigmoid":
                y = jax.nn.sigmoid(y)
            o_ref[:, b * GW:(b + 1) * GW] = y.astype(BF)

    blocks = [pl.BlockSpec((GW, K), functools.partial(lambda i, b: (b, 0), b=row0 // GW + b),
                           pipeline_mode=pl.Buffered(1)) for b in range(nblk)]
    return pl.pallas_call(
        body, name=name, out_shape=jax.ShapeDtypeStruct((S, N), BF), grid=(S // tm,),
        in_specs=[pl.BlockSpec((tm, K), lambda i: (i, 0))] + blocks,
        out_specs=pl.BlockSpec((tm, N), lambda i: (i, 0)),
        compiler_params=_cp(("parallel",)))(h, *([w_t] * nblk))


def _dilate_store(val, scr, o_ref, lead, d):
    rows = val.shape[0]
    if d == 1:
        o_ref[lead + (0,)] = val.astype(o_ref.dtype)
        return
    for hf in range(2):
        scr[hf, pl.ds(0, rows), :] = val[:, hf * 128:(hf + 1) * 128]
    for r in range(d):
        for hf in range(2):
            o_ref[lead + (r, slice(None), slice(hf * 128, (hf + 1) * 128))] = (
                scr[hf, pl.ds(r, rows // d, stride=d), :].astype(o_ref.dtype))


def _undilate_load(i_ref, lead, d, scr, rows):
    if d == 1:
        return i_ref[lead + (0,)].astype(F32)
    for r in range(d):
        for hf in range(2):
            scr[hf, pl.ds(r, rows // d, stride=d), :] = (
                i_ref[lead + (r, slice(None), slice(hf * 128, (hf + 1) * 128))].astype(F32))
    return jnp.concatenate([scr[0, pl.ds(0, rows), :], scr[1, pl.ds(0, rows), :]], axis=1)


def _rope_fwd(y, c_ref, s_ref):
    cosv = jnp.concatenate([c_ref[...]] * 2, axis=1)
    sinv = jnp.concatenate([s_ref[...]] * 2, axis=1)
    return y * cosv + _swap_halves(y) * sinv


def _rope_bwd(dy, c_ref, s_ref):
    cosv = jnp.concatenate([c_ref[...]] * 2, axis=1)
    sinv = jnp.concatenate([s_ref[...]] * 2, axis=1)
    return dy * cosv + _swap_halves(dy * sinv)


def _proj_qkv(h, wqkv_t, cos_t, sin_t, g, d, name, tie):
    S, K = h.shape
    tm = min(PROJ_QKV_TM, S)

    def body(h_ref, wq_ref, wk_ref, wv_ref, c_ref, s_ref, tie_ref, o_ref, scr):
        hv = h_ref[...]
        ys = [_dot_nt(hv, w_ref[...]) for w_ref in (wq_ref, wk_ref, wv_ref)]
        for t in range(3):
            y = _rope_fwd(ys[t], c_ref, s_ref) if t < 2 else ys[t]
            _dilate_store(y, scr.at[t], o_ref, (t,), d)

    w_spec = lambda t: pl.BlockSpec((GW, K), lambda i: (3 * t + g, 0))
    return pl.pallas_call(
        body, name=name, out_shape=jax.ShapeDtypeStruct((3, d, S // d, GW), BF), grid=(S // tm,),
        in_specs=[pl.BlockSpec((tm, K), lambda i: (i, 0)), w_spec(0), w_spec(1), w_spec(2),
                  pl.BlockSpec((tm, 128), lambda i: (i, 0)), pl.BlockSpec((tm, 128), lambda i: (i, 0)),
                  pl.BlockSpec(memory_space=pl.ANY)],
        out_specs=pl.BlockSpec((3, d, tm // d, GW), lambda i: (0, 0, i, 0)),
        scratch_shapes=[pltpu.VMEM((3, 2, tm, 128), F32)],
        compiler_params=_cp(("parallel",)))(h, wqkv_t, wqkv_t, wqkv_t, cos_t, sin_t, tie)


def _band_masks(first_step):
    row = lax.broadcasted_iota(jnp.int32, (QB, 2 * QB), 0)
    col = lax.broadcasted_iota(jnp.int32, (QB, 2 * QB), 1)
    band = jnp.logical_and(col >= row, col <= row + QB)
    return band, jnp.logical_and(band, jnp.logical_or(col >= QB, jnp.logical_not(first_step)))


def _attn_fwd(qkv, name):
    _, d, L, _ = qkv.shape
    nsub = min(ATTN_NSUB, L // QB)
    R = nsub * QB
    nsteps = L // R

    def body(q_ref, kp_ref, kc_ref, vp_ref, vc_ref, o_ref, lse_ref):
        i = pl.program_id(1)
        band, band_first = _band_masks(i == 0)
        kfull = jnp.concatenate([kp_ref[...], kc_ref[...]], axis=0)
        vfull = jnp.concatenate([vp_ref[...], vc_ref[...]], axis=0)
        chains = [(sb, hh) for sb in range(nsub) for hh in range(NSLOT)]
        win = lambda t, sb: t[sb * QB:(sb + 2) * QB]
        scores = []
        for sb, hh in chains:
            qh = jnp.where(_head_mask(hh), q_ref[sb * QB:(sb + 1) * QB, :], 0)
            scores.append(_dot_nt(qh, win(kfull, sb)))
        soft = []
        for (sb, hh), sc in zip(chains, scores):
            sc = jnp.where(band_first if sb == 0 else band, sc * SCALE, NEG)
            m = jnp.max(sc, axis=1, keepdims=True)
            p = jnp.exp(sc - m)
            den = jnp.sum(p, axis=1, keepdims=True)
            soft.append((p.astype(BF), den, m + jnp.log(den)))
        accs = [_dot(p, win(vfull, sb)) for (sb, hh), (p, _, _) in zip(chains, soft)]
        for sb in range(nsub):
            o = jnp.zeros((QB, GW), F32)
            lse = jnp.zeros((QB, GW), F32)
            for hh in range(NSLOT):
                hm = _head_mask(hh)
                _, den, lrow = soft[sb * NSLOT + hh]
                o = o + jnp.where(hm, accs[sb * NSLOT + hh] / den, 0.0)
                lse = lse + jnp.where(hm, lrow, 0.0)
            o_ref[sb * QB:(sb + 1) * QB, :] = o
            lse_ref[sb * QB:(sb + 1) * QB, :] = lse

    prev = lambda i: jnp.maximum(i * nsub - 1, 0)
    cur = lambda t: pl.BlockSpec((None, None, R, GW), lambda r, i: (t, r, i, 0))
    prv = lambda t: pl.BlockSpec((None, None, QB, GW), lambda r, i: (t, r, prev(i), 0))
    out = pl.BlockSpec((None, R, GW), lambda r, i: (r, i, 0))
    return pl.pallas_call(
        body, name=name, grid=(d, nsteps),
        out_shape=(jax.ShapeDtypeStruct((d, L, GW), F32), jax.ShapeDtypeStruct((d, L, GW), F32)),
        in_specs=[cur(0), prv(1), cur(1), prv(2), cur(2)],
        out_specs=(out, out), compiler_params=_cp(("parallel", "arbitrary")))(qkv, qkv, qkv, qkv, qkv)


def _attn_combine(os_, lses):
    S = os_[0].shape[1]
    tm = min(1024, S)

    def body(o0, o1, o2, l0, l1, l2, y_ref, j0, j1, j2, scr):
        os_nat = [_undilate_load(o, (), d, scr, tm) for o, d in zip((o0, o1, o2), DILS)]
        a, b, c = [_undilate_load(l, (), d, scr, tm) for l, d in zip((l0, l1, l2), DILS)]
        m = jnp.maximum(jnp.maximum(a, b), c)
        wa, wb, wc = jnp.exp(a - m), jnp.exp(b - m), jnp.exp(c - m)
        den = wa + wb + wc
        y_ref[...] = ((wa * os_nat[0] + wb * os_nat[1] + wc * os_nat[2]) / den).astype(BF)
        lse = m + jnp.log(den)
        for j_ref, d in zip((j0, j1, j2), DILS):
            _dilate_store(lse, scr, j_ref, (), d)

    dil = lambda d: pl.BlockSpec((d, tm // d, GW), lambda i: (0, i, 0))
    dshape = lambda d: jax.ShapeDtypeStruct((d, S // d, GW), F32)
    return pl.pallas_call(
        body, name="attn_combine", grid=(S // tm,),
        out_shape=(jax.ShapeDtypeStruct((S, GW), BF),) + tuple(dshape(d) for d in DILS),
        in_specs=[dil(d) for d in DILS] * 2,
        out_specs=(pl.BlockSpec((tm, GW), lambda i: (i, 0)),) + tuple(dil(d) for d in DILS),
        scratch_shapes=[pltpu.VMEM((2, tm, 128), F32)],
        compiler_params=_cp(("parallel",)))(*os_, *lses)


def _tril(upper=False):
    row = lax.broadcasted_iota(jnp.int32, (CHUNK, CHUNK), 0)
    col = lax.broadcasted_iota(jnp.int32, (CHUNK, CHUNK), 1)
    return row <= col if upper else col <= row


def _ln_fwd(z, gain, bias):
    mu = jnp.mean(z, axis=-1, keepdims=True)
    zc = z - mu
    rstd = lax.rsqrt(jnp.mean(zc * zc, axis=-1, keepdims=True) + EPS)
    zhat = zc * rstd
    return zhat, rstd, zhat * gain + bias


def _gmlp_fwd(puz, ln_g, ln_b, w_sp, b_sp_t):
    S = puz.shape[0]
    tm = min(512, S)
    nch = tm // CHUNK

    def body(p_ref, g_ref, b_ref, w_ref, bt_ref, o_ref):
        tril = _tril()
        ws = [jnp.where(tril, w_ref[gg], 0.0).astype(BF) for gg in range(4)]
        for ch in range(nch):
            rows = slice(ch * CHUNK, (ch + 1) * CHUNK)
            z = _gelu(p_ref[rows, GMLP_W:].astype(F32))
            _, _, zn = _ln_fwd(z, g_ref[...], b_ref[...])
            zn = zn.astype(BF)
            for gg in range(4):
                cols = slice(gg * CHUNK, (gg + 1) * CHUNK)
                sz = _dot(ws[gg], zn[:, cols]) + bt_ref[:, gg:gg + 1]
                u = _gelu(p_ref[rows, cols].astype(F32))
                o_ref[rows, cols] = (u * sz).astype(BF)

    return pl.pallas_call(
        body, name="gmlp_fwd", out_shape=jax.ShapeDtypeStruct((S, GMLP_W), BF), grid=(S // tm,),
        in_specs=[pl.BlockSpec((tm, 2 * GMLP_W), lambda i: (i, 0)),
                  pl.BlockSpec((1, GMLP_W), lambda i: (0, 0)), pl.BlockSpec((1, GMLP_W), lambda i: (0, 0)),
                  pl.BlockSpec((4, CHUNK, CHUNK), lambda i: (0, 0, 0)), pl.BlockSpec((CHUNK, 4), lambda i: (0, 0))],
        out_specs=pl.BlockSpec((tm, GMLP_W), lambda i: (i, 0)),
        compiler_params=_cp(("parallel",)))(puz, ln_g, ln_b, w_sp, b_sp_t)


def _merge_fwd(ya, yg, gates, wba, wbg, wout, x, g2, g3):
    S = x.shape[0]
    tm = min(512, S)

    def body(ya_ref, yg_ref, gt_ref, wba_ref, wbg_ref, wo_ref, x_ref, g2_ref, g3_ref,
             mg_ref, y_ref, x2_ref, h3_ref):
        a = _dot(ya_ref[...], wba_ref[...])
        b = _dot(yg_ref[...], wbg_ref[...])
        merged = (gt_ref[:, :D].astype(F32) * a + gt_ref[:, D:].astype(F32) * b).astype(BF)
        mg_ref[...] = merged
        y = _dot(merged, wo_ref[...])
        y_ref[...] = y
        x2 = x_ref[...] + y * _rms_stats(y) * g2_ref[...]
        x2_ref[...] = x2
        h3_ref[...] = (x2 * _rms_stats(x2) * g3_ref[...]).astype(BF)

    row = lambda w: pl.BlockSpec((tm, w), lambda i: (i, 0))
    full = lambda s: pl.BlockSpec(s, lambda i: (0, 0))
    return pl.pallas_call(
        body, name="merge_fwd", grid=(S // tm,),
        out_shape=(jax.ShapeDtypeStruct((S, D), BF), jax.ShapeDtypeStruct((S, D), F32), jax.ShapeDtypeStruct((S, D), F32),
                   jax.ShapeDtypeStruct((S, D), BF)),
        in_specs=[row(GW), row(GMLP_W), row(2 * D), full((GW, D)), full((GMLP_W, D)), full((D, D)), row(D),
                  full((1, D)), full((1, D))],
        out_specs=(row(D), row(D), row(D), row(D)),
        compiler_params=_cp(("parallel",)))(ya, yg, gates, wba, wbg, wout, x, g2, g3)


def _resident(shape):
    return pl.BlockSpec(shape, lambda i: (0,) * len(shape), pipeline_mode=pl.Buffered(1))


def _mlp_fwd(h3, wmi, wmo, x2, tgt, g4):
    S = x2.shape[0]
    tm = min(MLP_TM_FWD, S)

    def body(h_ref, wi_ref, wo_ref, x2_ref, t_ref, g4_ref, a_ref, dy2_ref, dout_ref, loss_ref, dg4_ref):
        @pl.when(pl.program_id(0) == 0)
        def _():
            loss_ref[...] = jnp.zeros_like(loss_ref)
            dg4_ref[...] = jnp.zeros_like(dg4_ref)

        halves = [slice(hh * (tm // 2), (hh + 1) * (tm // 2)) for hh in range(2)]
        acts = []
        for rows in halves:
            a = jnp.maximum(_dot(h_ref[rows, :], wi_ref[...]), 0.0)
            a_ref[rows, :] = a.astype(BF)
            acts.append((a * a).astype(BF))
        y2s = [_dot(a2, wo_ref[...]) for a2 in acts]
        lane = lax.broadcasted_iota(jnp.int32, (1, 128), 1)
        for rows, y2 in zip(halves, y2s):
            r = _rms_stats(y2)
            out = x2_ref[rows, :] + y2 * r * g4_ref[...]
            err = out - t_ref[rows, :]
            tot = jnp.sum(jnp.sum(err * err, axis=1, keepdims=True), axis=0, keepdims=True) * (0.5 / D)
            loss_ref[...] += jnp.where(lane == 0, tot, 0.0)
            dout = err * (1.0 / D)
            dout_ref[rows, :] = dout
            dy2, dg = _rms_bwd(y2, r, g4_ref[...], dout)
            dy2_ref[rows, :] = dy2.astype(BF)
            dg4_ref[...] += dg

    row = pl.BlockSpec((tm, D), lambda i: (i, 0))
    return pl.pallas_call(
        body, name="mlp_fwd", grid=(S // tm,),
        out_shape=(jax.ShapeDtypeStruct((S, DFF), BF), jax.ShapeDtypeStruct((S, D), BF), jax.ShapeDtypeStruct((S, D), F32),
                   jax.ShapeDtypeStruct((1, 128), F32), jax.ShapeDtypeStruct((1, D), F32)),
        in_specs=[row, _resident((D, DFF)), _resident((DFF, D)), row, row, pl.BlockSpec((1, D), lambda i: (0, 0))],
        out_specs=(pl.BlockSpec((tm, DFF), lambda i: (i, 0)), row, row,
                   pl.BlockSpec((1, 128), lambda i: (0, 0)), pl.BlockSpec((1, D), lambda i: (0, 0))),
        compiler_params=_cp(("arbitrary",)))(h3, wmi, wmo, x2, tgt, g4)


def _mlp_bwd(dy2, a, wmo, wmi, x2, y, dout, g2, g3):
    S = x2.shape[0]
    tm = min(MLP_TM_BWD, S)

    def body(dy2_ref, a_ref, wo_ref, wi_ref, x2_ref, y_ref, dout_ref, g2_ref, g3_ref,
             dpre_ref, dx2_ref, dy_ref, dg3_ref, dg2_ref):
        @pl.when(pl.program_id(0) == 0)
        def _():
            dg3_ref[...] = jnp.zeros_like(dg3_ref)
            dg2_ref[...] = jnp.zeros_like(dg2_ref)

        da2 = _dot_nt(dy2_ref[...], wo_ref[...])
        dpre = (2.0 * a_ref[...].astype(F32) * da2).astype(BF)
        dpre_ref[...] = dpre
        dh3 = _dot_nt(dpre, wi_ref[...])
        x2 = x2_ref[...]
        dx3, dg3 = _rms_bwd(x2, _rms_stats(x2), g3_ref[...], dh3)
        dx2 = dout_ref[...] + dx3
        dx2_ref[...] = dx2
        dg3_ref[...] += dg3
        yv = y_ref[...]
        dy, dg2 = _rms_bwd(yv, _rms_stats(yv), g2_ref[...], dx2)
        dy_ref[...] = dy.astype(BF)
        dg2_ref[...] += dg2

    row = pl.BlockSpec((tm, D), lambda i: (i, 0))
    wide = pl.BlockSpec((tm, DFF), lambda i: (i, 0))
    vec = pl.BlockSpec((1, D), lambda i: (0, 0))
    return pl.pallas_call(
        body, name="mlp_bwd", grid=(S // tm,),
        out_shape=(jax.ShapeDtypeStruct((S, DFF), BF), jax.ShapeDtypeStruct((S, D), F32), jax.ShapeDtypeStruct((S, D), BF),
                   jax.ShapeDtypeStruct((1, D), F32), jax.ShapeDtypeStruct((1, D), F32)),
        in_specs=[row, wide, _resident((DFF, D)), _resident((D, DFF)), row, row, row, vec, vec],
        out_specs=(wide, row, row, vec, vec),
        compiler_params=_cp(("arbitrary",)))(dy2, a, wmo, wmi, x2, y, dout, g2, g3)


def _mm_tn(a, b, name, square_a=False, tm=1024, tn=1024, tk=2048, tie=None, col_blocks=None):
    S, M = a.shape
    N = b.shape[1]
    tk = min(tk, S)
    tm = max(t for t in range(128, min(tm, M) + 1, 128) if M % t == 0)
    tn = max(t for t in range(128, min(tn, N) + 1, 128) if N % t == 0)
    cb = N // col_blocks if col_blocks else tn
    assert M % tm == 0 and S % tk == 0 and tn % cb == 0
    nk = S // tk
    ties = () if tie is None else (tie,)

    def body(a_ref, b_ref, *rest):
        o_ref = rest[-1]
        k = pl.program_id(2)
        av = a_ref[...]
        if square_a:
            av = av * av
        part = _dot_tn(av, b_ref[...])
        if col_blocks:
            part = jnp.stack([part[:, t * cb:(t + 1) * cb] for t in range(tn // cb)])

        @pl.when(k == 0)
        def _():
            o_ref[...] = part

        @pl.when(k > 0)
        def _():
            o_ref[...] += part

    if col_blocks:
        out_shape, out_spec = (col_blocks, M, cb), pl.BlockSpec((tn // cb, tm, cb), lambda i, j, k: (j, i, 0))
    else:
        out_shape, out_spec = (M, N), pl.BlockSpec((tm, tn), lambda i, j, k: (i, j))
    return pl.pallas_call(
        body, name=name, out_shape=jax.ShapeDtypeStruct(out_shape, F32), grid=(M // tm, N // tn, nk),
        in_specs=[pl.BlockSpec((tk, tm), lambda i, j, k: (k, i)), pl.BlockSpec((tk, tn), lambda i, j, k: (k, j))]
        + [pl.BlockSpec(memory_space=pl.ANY)] * len(ties),
        out_specs=out_spec,
        compiler_params=_cp(("parallel", "parallel", "arbitrary")))(a, b, *ties)


def _outproj_bwd(dy, wout, ya, yg, gates, wba, wbg, tie):
    S = dy.shape[0]
    tm = min(512, S)

    def body(dy_ref, wo_ref, ya_ref, yg_ref, gt_ref, wba_ref, wbg_ref, tie_ref,
             dgt_ref, da_ref, db_ref, dyg_ref, e0, e1, e2, s0, s1, s2, scr):
        dm = _dot_nt(dy_ref[...], wo_ref[...])
        ga, gb = gt_ref[:, :D].astype(F32), gt_ref[:, D:].astype(F32)
        dgt_ref[:, :D] = (dm * _dot(ya_ref[...], wba_ref[...]) * ga * (1.0 - ga)).astype(BF)
        dgt_ref[:, D:] = (dm * _dot(yg_ref[...], wbg_ref[...]) * gb * (1.0 - gb)).astype(BF)
        da = (dm * ga).astype(BF)
        db = (dm * gb).astype(BF)
        da_ref[...] = da
        db_ref[...] = db
        dyg_ref[...] = _dot_nt(db, wbg_ref[...]).astype(BF)
        dya = _dot_nt(da, wba_ref[...]).astype(BF).astype(F32)
        dyy = dya * ya_ref[...].astype(F32)
        dsum = jnp.zeros((tm, GW), F32)
        for hh in range(NSLOT):
            hm = _head_mask(hh)
            dsum = dsum + jnp.where(hm, jnp.sum(jnp.where(hm, dyy, 0.0), axis=1, keepdims=True), 0.0)
        for e_ref, s_ref, d in zip((e0, e1, e2), (s0, s1, s2), DILS):
            _dilate_store(dya, scr, e_ref, (), d)
            _dilate_store(dsum, scr, s_ref, (), d)

    row = lambda w: pl.BlockSpec((tm, w), lambda i: (i, 0))
    full = lambda s: pl.BlockSpec(s, lambda i: (0, 0))
    dil = lambda d: pl.BlockSpec((d, tm // d, GW), lambda i: (0, i, 0))
    dshape = lambda d, t: jax.ShapeDtypeStruct((d, S // d, GW), t)
    return pl.pallas_call(
        body, name="outproj_bwd", grid=(S // tm,),
        out_shape=(jax.ShapeDtypeStruct((S, 2 * D), BF), jax.ShapeDtypeStruct((S, D), BF), jax.ShapeDtypeStruct((S, D), BF),
                   jax.ShapeDtypeStruct((S, GMLP_W), BF)) + tuple(dshape(d, BF) for d in DILS)
        + tuple(dshape(d, F32) for d in DILS),
        in_specs=[row(D), full((D, D)), row(GW), row(GMLP_W), row(2 * D), full((GW, D)), full((GMLP_W, D)),
                  pl.BlockSpec(memory_space=pl.ANY)],
        out_specs=(row(2 * D), row(D), row(D), row(GMLP_W)) + tuple(dil(d) for d in DILS) * 2,
        scratch_shapes=[pltpu.VMEM((2, tm, 128), F32)],
        compiler_params=_cp(("parallel",)))(dy, wout, ya, yg, gates, wba, wbg, tie)


def _gmlp_bwd(puz, dyg, ln_g, ln_b, w_sp, w_sp_t, b_sp_t):
    S = puz.shape[0]
    tm = min(512, S)
    nch = tm // CHUNK

    def body(p_ref, dy_ref, g_ref, b_ref, w_ref, wt_ref, bt_ref,
             dp_ref, dw_ref, dbs_ref, dg_ref, dbias_ref, dbacc_ref):
        i = pl.program_id(0)

        @pl.when(i == 0)
        def _():
            dw_ref[...] = jnp.zeros_like(dw_ref)
            dbacc_ref[...] = jnp.zeros_like(dbacc_ref)
            dg_ref[...] = jnp.zeros_like(dg_ref)
            dbias_ref[...] = jnp.zeros_like(dbias_ref)

        tril = _tril()
        ws = [jnp.where(tril, w_ref[gg], 0.0).astype(BF) for gg in range(4)]
        triu = _tril(upper=True)
        wts = [jnp.where(triu, wt_ref[gg], 0.0).astype(BF) for gg in range(4)]
        gain = g_ref[...]
        for ch in range(nch):
            rows = slice(ch * CHUNK, (ch + 1) * CHUNK)
            pz = p_ref[rows, GMLP_W:].astype(F32)
            z = _gelu(pz)
            zhat, rstd, zn = _ln_fwd(z, gain, b_ref[...])
            znb = zn.astype(BF)
            dzn_parts = []
            for gg in range(4):
                cols = slice(gg * CHUNK, (gg + 1) * CHUNK)
                pu = p_ref[rows, cols].astype(F32)
                u = _gelu(pu)
                sz = _dot(ws[gg], znb[:, cols]) + bt_ref[:, gg:gg + 1]
                dyv = dy_ref[rows, cols].astype(F32)
                dp_ref[rows, cols] = (dyv * sz * _gelu_grad(pu)).astype(BF)
                dsz = dyv * u
                dbacc_ref[gg] += dsz
                dszb = dsz.astype(BF)
                dw_ref[gg] += _dot_nt(dszb, znb[:, cols])
                dzn_parts.append(_dot(wts[gg], dszb))
            dzn = jnp.concatenate(dzn_parts, axis=1)
            dg_ref[...] += jnp.sum(dzn * zhat, axis=0, keepdims=True)
            dbias_ref[...] += jnp.sum(dzn, axis=0, keepdims=True)
            dzh = dzn * gain
            dz = rstd * (dzh - jnp.mean(dzh, axis=-1, keepdims=True)
                         - zhat * jnp.mean(dzh * zhat, axis=-1, keepdims=True))
            dp_ref[rows, GMLP_W:] = (dz * _gelu_grad(pz)).astype(BF)

        @pl.when(i == pl.num_programs(0) - 1)
        def _():
            for gg in range(4):
                dw_ref[gg] = jnp.where(tril, dw_ref[gg], 0.0)
                dbs_ref[gg] = jnp.sum(dbacc_ref[gg], axis=1, keepdims=True)

    full2 = lambda s: pl.BlockSpec(s, lambda i: (0, 0))
    full3 = lambda s: pl.BlockSpec(s, lambda i: (0, 0, 0))
    return pl.pallas_call(
        body, name="gmlp_bwd", grid=(S // tm,),
        out_shape=(jax.ShapeDtypeStruct((S, 2 * GMLP_W), BF), jax.ShapeDtypeStruct((4, CHUNK, CHUNK), F32),
                   jax.ShapeDtypeStruct((4, CHUNK, 1), F32), jax.ShapeDtypeStruct((1, GMLP_W), F32),
                   jax.ShapeDtypeStruct((1, GMLP_W), F32)),
        in_specs=[pl.BlockSpec((tm, 2 * GMLP_W), lambda i: (i, 0)), pl.BlockSpec((tm, GMLP_W), lambda i: (i, 0)),
                  full2((1, GMLP_W)), full2((1, GMLP_W)), full3((4, CHUNK, CHUNK)), full3((4, CHUNK, CHUNK)),
                  full2((CHUNK, 4))],
        out_specs=(pl.BlockSpec((tm, 2 * GMLP_W), lambda i: (i, 0)), full3((4, CHUNK, CHUNK)), full3((4, CHUNK, 1)),
                   full2((1, GMLP_W)), full2((1, GMLP_W))),
        scratch_shapes=[pltpu.VMEM((4, CHUNK, CHUNK), F32)],
        compiler_params=_cp(("arbitrary",)))(puz, dyg, ln_g, ln_b, w_sp, w_sp_t, b_sp_t)


def _attn_bwd(qkv, dya, dsums, lse, name):
    _, d, L, _ = qkv.shape
    nsub = min(ATTN_NSUB, L // QB)
    R = nsub * QB
    nsteps = L // R

    def body(q_ref, qn_ref, kp_ref, kc_ref, vp_ref, vc_ref, dy_ref, dyn_ref, e_ref, en_ref, l_ref, ln_ref, o_ref):
        i = pl.program_id(1)
        band, band_first = _band_masks(i == 0)
        row = lax.broadcasted_iota(jnp.int32, (QB, QB), 0)
        col = lax.broadcasted_iota(jnp.int32, (QB, QB), 1)
        mask_next = jnp.logical_and(col >= row, i < nsteps - 1)
        kc, vc = kc_ref[...], vc_ref[...]
        kfull = jnp.concatenate([kp_ref[...], kc], axis=0)
        vfull = jnp.concatenate([vp_ref[...], vc], axis=0)
        k_last, v_last = kc[(nsub - 1) * QB:], vc[(nsub - 1) * QB:]
        q_ext = jnp.concatenate([q_ref[...], qn_ref[...]], axis=0)
        dy_ext = jnp.concatenate([dy_ref[...], dyn_ref[...]], axis=0)
        esum, esum_n, lse, lse_n = e_ref[...], en_ref[...], l_ref[...], ln_ref[...]
        win = lambda t, sb: t[sb * QB:(sb + 2) * QB]
        blk = lambda t, sb: t[sb * QB:(sb + 1) * QB]
        hms = [_head_mask(hh) for hh in range(NSLOT)]
        q_hs = [jnp.where(hm, q_ext, 0) for hm in hms]
        dy_hs = [jnp.where(hm, dy_ext, 0) for hm in hms]
        raw = []
        for hh in range(NSLOT):
            tiles = [(_dot_nt(blk(q_hs[hh], sb), win(kfull, sb)), _dot_nt(blk(dy_hs[hh], sb), win(vfull, sb)))
                     for sb in range(nsub)]
            tiles.append((_dot_nt(blk(q_hs[hh], nsub), k_last), _dot_nt(blk(dy_hs[hh], nsub), v_last)))
            raw.append(tiles)
        ps, dss = [], []
        for hh in range(NSLOT):
            rowstat = lambda t: jnp.max(jnp.where(hms[hh], t, -jnp.inf), axis=1, keepdims=True)
            p_h, ds_h = [], []
            for sb in range(nsub + 1):
                sc, dp = raw[hh][sb]
                if sb < nsub:
                    msk, lrow, erow = (band_first if sb == 0 else band), rowstat(blk(lse, sb)), rowstat(blk(esum, sb))
                else:
                    msk, lrow, erow = mask_next, rowstat(lse_n), rowstat(esum_n)
                p = jnp.where(msk, jnp.exp(sc * SCALE - lrow), 0.0)
                p_h.append(p.astype(BF))
                ds_h.append((p * (dp - erow)).astype(BF))
            ps.append(p_h)
            dss.append(ds_h)
        dq = [jnp.zeros((QB, GW), F32) for _ in range(nsub)]
        dk = [jnp.zeros((QB, GW), F32) for _ in range(nsub)]
        dv = [jnp.zeros((QB, GW), F32) for _ in range(nsub)]
        for hh in range(NSLOT):
            for sb in range(nsub):
                dq[sb] = dq[sb] + jnp.where(hms[hh], _dot(dss[hh][sb], win(kfull, sb)), 0.0)
                nxt = lambda t: t[sb + 1][:, :QB] if sb + 1 < nsub else t[nsub]
                dk[sb] = dk[sb] + _dot_tn(jnp.concatenate([dss[hh][sb][:, QB:], nxt(dss[hh])], axis=0), win(q_hs[hh], sb))
                dv[sb] = dv[sb] + _dot_tn(jnp.concatenate([ps[hh][sb][:, QB:], nxt(ps[hh])], axis=0), win(dy_hs[hh], sb))
        for sb in range(nsub):
            rows = slice(sb * QB, (sb + 1) * QB)
            o_ref[0, rows, :] = (dq[sb] * SCALE).astype(BF)
            o_ref[1, rows, :] = (dk[sb] * SCALE).astype(BF)
            o_ref[2, rows, :] = dv[sb].astype(BF)

    prev = lambda i: jnp.maximum(i * nsub - 1, 0)
    nxt = lambda i: jnp.minimum((i + 1) * nsub, L // QB - 1)
    cur4 = lambda t: pl.BlockSpec((None, None, R, GW), lambda r, i: (t, r, i, 0))
    prv4 = lambda t: pl.BlockSpec((None, None, QB, GW), lambda r, i: (t, r, prev(i), 0))
    nxt4 = lambda t: pl.BlockSpec((None, None, QB, GW), lambda r, i: (t, r, nxt(i), 0))
    cur3 = pl.BlockSpec((None, R, GW), lambda r, i: (r, i, 0))
    nxt3 = pl.BlockSpec((None, QB, GW), lambda r, i: (r, nxt(i), 0))
    return pl.pallas_call(
        body, name=name, grid=(d, nsteps), out_shape=jax.ShapeDtypeStruct((3, d, L, GW), BF),
        in_specs=[cur4(0), nxt4(0), prv4(1), cur4(1), prv4(2), cur4(2), cur3, nxt3, cur3, nxt3, cur3, nxt3],
        out_specs=pl.BlockSpec((3, None, R, GW), lambda r, i: (0, r, i, 0)),
        compiler_params=_cp(("parallel", "arbitrary")))(qkv, qkv, qkv, qkv, qkv, qkv, dya, dya, dsums, dsums, lse, lse)


def _inproj_bwd(dqkvs, dpuz, dgates, win_t, cos_t, sin_t, x, dx2, g1):
    S = x.shape[0]
    tm = min(512, S)

    def body(d0_ref, d1_ref, d2_ref, dp_ref, dg_ref, w_ref, c_ref, s_ref,
             x_ref, dx2_ref, g1_ref, gx_ref, dg1_ref, dn_ref, scr):
        i = pl.program_id(0)

        @pl.when(i == 0)
        def _():
            dg1_ref[...] = jnp.zeros_like(dg1_ref)

        dh = _dot(dp_ref[...], w_ref[W_UZ0:W_G0, :]) + _dot(dg_ref[...], w_ref[W_G0:, :])
        for t in range(3):
            for g, (d_ref, d) in enumerate(zip((d0_ref, d1_ref, d2_ref), DILS)):
                piece = _undilate_load(d_ref, (t,), d, scr, tm)
                if t < 2:
                    piece = _rope_bwd(piece, c_ref, s_ref)
                dn_ref[:, (3 * t + g) * GW:(3 * t + g + 1) * GW] = piece.astype(BF)
        dh = dh + _dot(dn_ref[...], w_ref[:W_UZ0, :])
        xv = x_ref[...]
        dx1, dg1 = _rms_bwd(xv, _rms_stats(xv), g1_ref[...], dh)
        gx_ref[...] = dx2_ref[...] + dx1
        dg1_ref[...] += dg1

    row = lambda w: pl.BlockSpec((tm, w), lambda i: (i, 0))
    full = lambda s: pl.BlockSpec(s, lambda i: (0, 0))
    dil = lambda d: pl.BlockSpec((3, d, tm // d, GW), lambda i: (0, 0, i, 0))
    return pl.pallas_call(
        body, name="inproj_bwd", grid=(S // tm,),
        out_shape=(jax.ShapeDtypeStruct((S, D), F32), jax.ShapeDtypeStruct((1, D), F32),
                   jax.ShapeDtypeStruct((S, 3 * AW), BF)),
        in_specs=[dil(d) for d in DILS] + [row(2 * GMLP_W), row(2 * D), _resident(win_t.shape), row(128), row(128), row(D), row(D),
                                            full((1, D))],
        out_specs=(row(D), full((1, D)), row(3 * AW)),
        scratch_shapes=[pltpu.VMEM((2, tm, 128), F32)],
        compiler_params=_cp(("arbitrary",)))(*dqkvs, dpuz, dgates, win_t, cos_t, sin_t, x, dx2, g1)


def _row_tile(rows, cap=256):
    return max(t for t in range(16, cap + 1, 16) if rows % t == 0)


def _adam_math(w, g, m, v):
    m2 = ADAM_B1 * m + (1.0 - ADAM_B1) * g
    v2 = ADAM_B2 * v + (1.0 - ADAM_B2) * (g * g)
    m_hat = m2 / (1.0 - ADAM_B1 ** ADAM_STEP)
    v_hat = v2 / (1.0 - ADAM_B2 ** ADAM_STEP)
    delta = -ADAM_LR * (m_hat / (jnp.sqrt(v_hat) + ADAM_EPS) + ADAM_WD * w)
    return delta, m2, v2


def _adam_shard(own, recv, w, m, v, name):
    R, C = w.shape
    tr = _row_tile(R)

    def body(own_ref, r_ref, w_ref, m_ref, v_ref, g_ref, d_ref, m2_ref, v2_ref):
        g = own_ref[...] + r_ref[0].astype(F32) + r_ref[1].astype(F32) + r_ref[2].astype(F32)
        g_ref[...] = g
        d_ref[...], m2_ref[...], v2_ref[...] = _adam_math(w_ref[...], g, m_ref[...], v_ref[...])

    spec = pl.BlockSpec((tr, C), lambda i: (i, 0))
    out = jax.ShapeDtypeStruct((R, C), F32)
    return pl.pallas_call(
        body, name=name, grid=(R // tr,), out_shape=(out, out, out, out),
        in_specs=[spec, pl.BlockSpec((3, tr, C), lambda i: (0, i, 0)), spec, spec, spec],
        out_specs=(spec, spec, spec, spec), compiler_params=_cp(("parallel",)))(own, recv, w, m, v)


def _rs_add(gblocks, recv, idx, name):
    _, R, C = gblocks.shape
    tr = _row_tile(R)

    def body(t_ref, g_ref, r_ref, own_ref, send_ref):
        j = pl.program_id(1)
        s = g_ref[...] + r_ref[...]

        @pl.when(j == 0)
        def _():
            own_ref[...] = s

        @pl.when(j > 0)
        def _():
            send_ref[...] = s.astype(BF)

    grid_spec = pltpu.PrefetchScalarGridSpec(
        num_scalar_prefetch=1, grid=(R // tr, 4),
        in_specs=[pl.BlockSpec((None, tr, C), lambda i, j, t: (t[j], i, 0)),
                  pl.BlockSpec((None, tr, C), lambda i, j, t: (t[4 + j], i, 0))],
        out_specs=[pl.BlockSpec((tr, C), lambda i, j, t: (i, 0)),
                   pl.BlockSpec((None, tr, C), lambda i, j, t: (jnp.maximum(j - 1, 0), i, 0))])
    return pl.pallas_call(
        body, name=name, grid_spec=grid_spec,
        out_shape=(jax.ShapeDtypeStruct((R, C), F32), jax.ShapeDtypeStruct((3, R, C), BF)),
        compiler_params=_cp(("parallel", "arbitrary")))(idx, gblocks, recv)


def _mesh_pos():
    return lax.axis_index("x"), lax.axis_index("y"), lax.axis_index("c")


_HBM = pl.BlockSpec(memory_space=pltpu.HBM)
_SEM = pl.BlockSpec(memory_space=pltpu.SEMAPHORE)
_EFFECT = pltpu.SideEffectType.DATAFLOW_SIDE_EFFECTING
_RELATIONS = [(dx, dy, dc) for dx in (0, 1) for dy in (0, 1) for dc in (0, 1)][1:]


def _flip(v, d):
    return 1 - v if d else v


def _plan_gather(n):
    def plan(x, y, c):
        return [(k, None, 4 * x + 2 * y + c, (_flip(x, dx), _flip(y, dy), _flip(c, dc)))
                for k in range(n) for dx, dy, dc in _RELATIONS]
    return plan


def _plan_gather_near(x, y, c):
    chips = [(1 - x, y), (x, 1 - y), (1 - x, 1 - y)]
    return [(0, None, 4 * x + 2 * y + c, (x, y, 1 - c))] + [(0, None, 4 * x + 2 * y + c, (*chip, c)) for chip in chips]


def _plan_gather_pass(from_landing):
    def plan(x, y, c):
        blocks = [4 * cx + 2 * cy + c for cx, cy in ((1 - x, y), (x, 1 - y), (1 - x, 1 - y))]
        return [(0, b if from_landing else None, b, (x, y, 1 - c)) for b in blocks]
    return plan


def _plan_d2d(n):
    def plan(x, y, c):
        return [(k, 2 * kk + 1 - c, kk, (x, y, 1 - c)) for k in range(n) for kk in range(4)]
    return plan


def _plan_ici(n):
    def plan(x, y, c):
        return [(k, j, j, (_flip(x, dx), _flip(y, dy), c))
                for k in range(n) for j, (dx, dy) in enumerate(((1, 0), (0, 1), (1, 1)))]
    return plan


def _plan_copies(plan, src_refs, land_refs, send_sems, recv_sems):
    x, y, c = _mesh_pos()
    return [pltpu.make_async_remote_copy(
        src_ref=src_refs[k] if si is None else src_refs[k].at[si], dst_ref=land_refs[k].at[di],
        send_sem=send_sems.at[n], recv_sem=recv_sems.at[n], device_id=dev, device_id_type=MESH)
        for n, (k, si, di, dev) in enumerate(plan(x, y, c))]


def _exchange_start(srcs, land_shapes, plan, ncopies, name, after):
    n = len(srcs)

    def body(*refs):
        src_refs, land_refs = refs[:n], refs[n:2 * n]
        send_sems, recv_sems = refs[2 * n + len(after)], refs[2 * n + len(after) + 1]
        token = refs[-1]
        for cp in _plan_copies(plan, src_refs, land_refs, send_sems, recv_sems):
            cp.start()
        token[...] = jnp.zeros_like(token)

    lands = [pltpu.with_memory_space_constraint(lax.empty(s, a.dtype), pltpu.HBM) for s, a in zip(land_shapes, srcs)]
    srcs = [pltpu.with_memory_space_constraint(a, pltpu.HBM) for a in srcs]
    outs = pl.pallas_call(
        body, name=name,
        out_shape=(pltpu.SemaphoreType.DMA((ncopies,)), pltpu.SemaphoreType.DMA((ncopies,)))
        + tuple(pltpu.HBM(a.shape, a.dtype) for a in srcs) + tuple(pltpu.HBM(a.shape, a.dtype) for a in lands)
        + (jax.ShapeDtypeStruct((8, 128), F32),),
        in_specs=[_HBM] * (2 * n) + [pl.BlockSpec(memory_space=pl.ANY)] * len(after),
        out_specs=(_SEM, _SEM) + (_HBM,) * (2 * n) + (pl.BlockSpec(memory_space=pltpu.VMEM),),
        input_output_aliases={i: 2 + i for i in range(2 * n)},
        compiler_params=pltpu.CompilerParams(has_side_effects=_EFFECT))(*srcs, *lands, *after)
    return (outs[0], outs[1], list(outs[2:2 + n]), list(outs[2 + n:2 + 2 * n])), outs[-1]


def _exchange_forward(handle, plan, plan_fwd, nfwd, name, after):
    send_sems, recv_sems, srcs, lands = handle
    n = len(srcs)

    def body(*refs):
        src_refs, land_refs = refs[:n], refs[n:2 * n]
        outs = refs[2 * n + 2 + len(after):]
        for cp in _plan_copies(plan, src_refs, land_refs, refs[2 * n], refs[2 * n + 1]):
            cp.wait_send()
            cp.wait_recv()
        for cp in _plan_copies(plan_fwd, land_refs, land_refs, outs[0], outs[1]):
            cp.start()

    outs = pl.pallas_call(
        body, name=name,
        out_shape=(pltpu.SemaphoreType.DMA((nfwd,)), pltpu.SemaphoreType.DMA((nfwd,)))
        + tuple(pltpu.HBM(a.shape, a.dtype) for a in srcs) + tuple(pltpu.HBM(a.shape, a.dtype) for a in lands),
        in_specs=[_HBM] * (2 * n) + [_SEM, _SEM] + [pl.BlockSpec(memory_space=pl.ANY)] * len(after),
        out_specs=(_SEM, _SEM) + (_HBM,) * (2 * n), input_output_aliases={i: 2 + i for i in range(2 * n)},
        compiler_params=pltpu.CompilerParams(has_side_effects=_EFFECT))(*srcs, *lands, send_sems, recv_sems, *after)
    return outs[0], outs[1], list(outs[2:2 + n]), list(outs[2 + n:2 + 2 * n])


def _exchange_wait(handle, plan, name, after):
    send_sems, recv_sems, srcs, lands = handle
    n = len(srcs)

    def body(*refs):
        src_refs, land_refs = refs[:n], refs[n:2 * n]
        for cp in _plan_copies(plan, src_refs, land_refs, refs[2 * n], refs[2 * n + 1]):
            cp.wait_send()
            cp.wait_recv()

    outs = pl.pallas_call(
        body, name=name,
        out_shape=tuple(pltpu.HBM(a.shape, a.dtype) for a in srcs) + tuple(pltpu.HBM(a.shape, a.dtype) for a in lands),
        in_specs=[_HBM] * (2 * n) + [_SEM, _SEM] + [pl.BlockSpec(memory_space=pl.ANY)] * len(after),
        out_specs=(_HBM,) * (2 * n), input_output_aliases={i: i for i in range(2 * n)},
        compiler_params=pltpu.CompilerParams(has_side_effects=_EFFECT))(*srcs, *lands, send_sems, recv_sems, *after)
    return list(outs[:n]), list(outs[n:])


SMALL = ("g1", "g2", "g3", "g4", "ln_g", "ln_b", "b_sp", "w_sp")


def _small_step(loss_row, grads, ws, ms, vs):
    parts = [loss_row] + [grads[k] for k in SMALL]
    n, ns = len(parts), len(SMALL)

    def body(*refs):
        p_refs = refs[:n]
        w_refs, m_refs, v_refs = (refs[n + i * ns:n + (i + 1) * ns] for i in range(3))
        o = n + 3 * ns
        loss_ref = refs[o]
        g_out, d_out, m_out, v_out = (refs[o + 1 + i * ns:o + 1 + (i + 1) * ns] for i in range(4))
        bufs = refs[o + 1 + 4 * ns:o + 1 + 4 * ns + n]
        send_sems, recv_sems = refs[-2:]
        x, y, c = _mesh_pos()
        me, sib = (x, y, c), (x, y, 1 - c)
        chips = [(1 - x, y), (x, 1 - y), (1 - x, 1 - y)]

        def rcopy(k, s, block, to, src=None):
            dst = bufs[k].at[4 * block[0] + 2 * block[1] + block[2]]
            return pltpu.make_async_remote_copy(
                src_ref=dst if src is None else src, dst_ref=dst, send_sem=send_sems.at[k, s],
                recv_sem=recv_sems.at[k, s], device_id=to, device_id_type=MESH)

        started = []
        for k in range(n):
            bufs[k][4 * x + 2 * y + c] = p_refs[k][...]
            started.append(rcopy(k, 0, me, sib, src=p_refs[k]))
            started += [rcopy(k, 1 + j, me, (*chip, c), src=p_refs[k]) for j, chip in enumerate(chips)]
        for cp in started:
            cp.start()
        for k in range(n):
            for j, chip in enumerate(chips):
                rcopy(k, 1 + j, (*chip, c), me).wait_recv()
                started.append(rcopy(k, 4 + j, (*chip, c), sib))
                started[-1].start()
        for k in range(n):
            rcopy(k, 0, sib, me).wait_recv()
            for j, chip in enumerate(chips):
                rcopy(k, 4 + j, (*chip, 1 - c), me).wait_recv()
        for cp in started:
            cp.wait_send()
        sums = []
        for k in range(n):
            acc = bufs[k][0]
            for b in range(1, 8):
                acc = acc + bufs[k][b]
            sums.append(acc)
        loss_ref[...] = sums[0]
        for i in range(ns):
            g_out[i][...] = sums[1 + i]
            d_out[i][...], m_out[i][...], v_out[i][...] = _adam_math(w_refs[i][...], sums[1 + i], m_refs[i][...],
                                                                     v_refs[i][...])

    args = parts + [t[k] for t in (ws, ms, vs) for k in SMALL]
    shapes = [jax.ShapeDtypeStruct(loss_row.shape, F32)] + [jax.ShapeDtypeStruct(grads[k].shape, F32) for k in SMALL] * 4
    vmem = pl.BlockSpec(memory_space=pltpu.VMEM)
    outs = pl.pallas_call(
        body, name="small_step", out_shape=tuple(shapes), in_specs=[vmem] * len(args), out_specs=(vmem,) * len(shapes),
        scratch_shapes=[pltpu.VMEM((8,) + p.shape, F32) for p in parts]
        + [pltpu.SemaphoreType.DMA((n, 7)), pltpu.SemaphoreType.DMA((n, 7))],
        compiler_params=pltpu.CompilerParams(vmem_limit_bytes=VMEM_LIMIT))(*args)
    groups = [dict(zip(SMALL, outs[1 + i * ns:1 + (i + 1) * ns])) for i in range(4)]
    return (outs[0], *groups)


def _rope_tables(S):
    half = HD // 2
    inv_freq = jnp.tile(ROPE_THETA ** (-jnp.arange(half, dtype=F32) / half), 4)
    sign = jnp.tile(jnp.concatenate([-jnp.ones(half, F32), jnp.ones(half, F32)]), 2)
    ang = jnp.arange(S, dtype=F32)[:, None] * inv_freq[None, :]
    return jnp.cos(ang), jnp.sin(ang) * sign[None, :]


def _to_blocks(g, col_sharded):
    if col_sharded:
        return g.reshape(g.shape[0], 8, g.shape[1] // 8).transpose(1, 0, 2)
    return g.reshape(8, g.shape[0] // 8, g.shape[1])


def _from_blocks(t, col_sharded):
    if col_sharded:
        return t.transpose(1, 0, 2).reshape(t.shape[1], 8 * t.shape[2])
    return t.reshape(8 * t.shape[1], t.shape[2])


class _NoComm:
    def __init__(self, late_weights):
        self._late = late_weights
        self.grads = {}

    def start_tie(self):
        return jnp.zeros((8, 128), F32)

    def late_weights(self, after):
        return self._late

    def rs_start(self, key, gblocks, after=()):
        self.grads[key] = gblocks
        return jnp.zeros((8, 128), F32)

    def rs_mid(self, key, after):
        return jnp.zeros((8, 128), F32)


class _FsdpComm:
    def __init__(self, late_shards, col_sharded, after, idx, block):
        self._col, self._idx, self._block, self._rs = col_sharded, idx, block, {}
        n = len(late_shards)
        self._gather, self._token = _exchange_start(
            late_shards, [(8,) + s.shape for s in late_shards], _plan_gather(n), 7 * n, "ag_late_start", (after,))

    def start_tie(self):
        return self._token

    def late_weights(self, after):
        shards, lands = _exchange_wait(self._gather, _plan_gather(len(self._col)), "ag_late_wait", after)
        lands = [lax.dynamic_update_index_in_dim(t, s, self._block, 0) for t, s in zip(lands, shards)]
        return [_from_blocks(t, cs) for t, cs in zip(lands, self._col)]

    def rs_start(self, key, gblocks, after=()):
        n = len(gblocks)
        d2d, token = _exchange_start(gblocks, [(4,) + g.shape[1:] for g in gblocks], _plan_d2d(n), 4 * n,
                                     "rs_%s_d2d_start" % key, (self._token,) + tuple(after))
        self._rs[key] = dict(n=n, d2d=d2d)
        return token

    def rs_mid(self, key, after):
        st = self._rs[key]
        gblocks, from_sib = _exchange_wait(st["d2d"], _plan_d2d(st["n"]), "rs_%s_d2d_wait" % key, after)
        halves = [_rs_add(g, r, self._idx, "rs_add_%s_%d" % (key, k)) for k, (g, r) in enumerate(zip(gblocks, from_sib))]
        st["own"] = [own for own, _ in halves]
        sends = [send for _, send in halves]
        st["ici"], token = _exchange_start(sends, [t.shape for t in sends], _plan_ici(st["n"]), 3 * st["n"],
                                           "rs_%s_ici_start" % key, (self._token,))
        return token

    def rs_end(self, key, after):
        st = self._rs[key]
        return st["own"], _exchange_wait(st["ici"], _plan_ici(st["n"]), "rs_%s_ici_wait" % key, after)[1]


def _local_step(x, tgt, h1, cos_t, sin_t, win_t, comm, g1, g2, g3, g4, ln_g, ln_b, w_sp, b_sp):
    b_sp_t = b_sp.T
    w_sp_t = w_sp.transpose(0, 2, 1)

    qkvs = [_proj_qkv(h1, win_t, cos_t, sin_t, g, d, "proj_qkv_d%d" % d, comm.start_tie()) for g, d in enumerate(DILS)]
    puz = _proj(h1, win_t, W_UZ0, W_G0 - W_UZ0, "none", "proj_uz", 1024)
    gates = _proj(h1, win_t, W_G0, 2 * D, "sigmoid", "proj_gates", 512)
    fwd = [_attn_fwd(t, "attn_fwd_d%d" % d) for t, d in zip(qkvs, DILS)]
    ya, *lses = _attn_combine([o for o, _ in fwd], [l for _, l in fwd])
    yg = _gmlp_fwd(puz, ln_g, ln_b, w_sp, b_sp_t)
    wba, wbg, wout, wmi, wmo = comm.late_weights(after=(ya, yg, gates))
    merged, y, x2, h3 = _merge_fwd(ya, yg, gates, wba, wbg, wout, x, g2, g3)
    a, dy2, dout, loss_row, dg4 = _mlp_fwd(h3, wmi, wmo, x2, tgt, g4)

    dpre, dx2, dy, dg3, dg2 = _mlp_bwd(dy2, a, wmo, wmi, x2, y, dout, g2, g3)
    dwmo = _mm_tn(a, dy2, "dw_mlp_out", square_a=True)
    dwmi = _mm_tn(h3, dpre, "dw_mlp_in", col_blocks=8)
    tie = comm.rs_start("mlp", [dwmi, _to_blocks(dwmo, False)])
    dgates, da, db, dyg, *rest = _outproj_bwd(dy, wout, ya, yg, gates, wba, wbg, tie)
    dyas, dsums = rest[:3], rest[3:]
    tie = comm.rs_mid("mlp", after=(dyg,))
    dpuz, dwsp, dbs, dlng, dlnb = _gmlp_bwd(puz, dyg, ln_g + tie[0, 0], ln_b, w_sp, w_sp_t, b_sp_t)
    dqkvs = [_attn_bwd(qkvs[g], dyas[g], dsums[g], lses[g], "attn_bwd_d%d" % d) for g, d in enumerate(DILS)]
    grad_x, dg1, dqkv = _inproj_bwd(dqkvs, dpuz, dgates, win_t, cos_t, sin_t, x, dx2, g1)
    dwin_t = jnp.concatenate([_mm_tn(t, h1, "dw_in_%d" % n) for n, t in enumerate((dqkv, dpuz, dgates))], axis=0)
    tie = comm.rs_start("win", [_to_blocks(dwin_t, False)])
    dwout = _mm_tn(merged, dy, "dw_out", tie=tie)
    dwba = _mm_tn(ya, da, "dw_branch_attn", tie=tie)
    dwbg = _mm_tn(yg, db, "dw_branch_gmlp", tie=tie)
    tie = comm.rs_mid("win", after=(dwout, dwba, dwbg))
    tie = comm.rs_start("mid", [_to_blocks(dwba, True), _to_blocks(dwbg, True), _to_blocks(dwout, False)], after=(tie,))
    small = dict(g1=dg1, g2=dg2, g3=dg3, g4=dg4, ln_g=dlng, ln_b=dlnb, b_sp=dbs.reshape(4, CHUNK),
                 w_sp=dwsp.reshape(4 * CHUNK, CHUNK))
    return loss_row + tie[:1], grad_x, small


def kernel(x, norm_pre_mix, w_in, w_spatial, b_spatial, ln_v_gain, ln_v_bias, w_branch_attn, w_branch_gmlp, w_out, norm_post_mix, norm_pre_mlp, w_mlp_in, w_mlp_out, norm_post_mlp, loss_target, m_norm_pre_mix, m_w_in, m_w_spatial, m_b_spatial, m_ln_v_gain, m_ln_v_bias, m_w_branch_attn, m_w_branch_gmlp, m_w_out, m_norm_post_mix, m_norm_pre_mlp, m_w_mlp_in, m_w_mlp_out, m_norm_post_mlp, v_norm_pre_mix, v_w_in, v_w_spatial, v_b_spatial, v_ln_v_gain, v_ln_v_bias, v_w_branch_attn, v_w_branch_gmlp, v_w_out, v_norm_post_mix, v_norm_pre_mlp, v_w_mlp_in, v_w_mlp_out, v_norm_post_mlp):
    mx, my, mc = _mesh_pos()
    rel = [(0, 0), (1, 0), (0, 1), (1, 1)]
    chip_of = [2 * (mx ^ dx) + (my ^ dy) for dx, dy in rel]
    idx = jnp.stack([2 * k + mc for k in chip_of] + chip_of).astype(jnp.int32)

    w_in_t, m_w_in_t, v_w_in_t = (t[0].T for t in (w_in, m_w_in, v_w_in))
    shard = w_in_t.astype(BF)
    gather, token = _exchange_start([shard], [(8,) + shard.shape], _plan_gather_near, 4, "ag_win_start", ())
    h1 = _rms_fwd(x[0], norm_pre_mix + token[0, 0], "rms_pre_mix")
    cos_t, sin_t = _rope_tables(x.shape[1])
    gather = _exchange_forward(gather, _plan_gather_near, _plan_gather_pass(True), 3, "ag_win_pass", (h1, cos_t, sin_t))
    (shard,), (g_win,) = _exchange_wait(gather, _plan_gather_pass(False), "ag_win_wait", ())
    g_win = lax.dynamic_update_index_in_dim(g_win, shard, 4 * mx + 2 * my + mc, 0)
    win_t = _from_blocks(g_win, False)
    late = [w_branch_attn[0], w_branch_gmlp[0], w_out[0], w_mlp_in[0], w_mlp_out[0]]
    comm = _FsdpComm([w.astype(BF) for w in late], [True, True, False, True, False], g_win, idx, 4 * mx + 2 * my + mc)

    loss_row, grad_x, small = _local_step(
        x[0], loss_target[0], h1, cos_t, sin_t, win_t, comm,
        norm_pre_mix, norm_post_mix, norm_pre_mlp, norm_post_mlp, ln_v_gain, ln_v_bias, w_spatial[0], b_spatial[0])

    flat = lambda t: t.reshape(-1, t.shape[-1])
    small_w = dict(zip(SMALL, map(flat, (norm_pre_mix, norm_post_mix, norm_pre_mlp, norm_post_mlp, ln_v_gain, ln_v_bias,
                                         b_spatial, w_spatial))))
    small_m = dict(zip(SMALL, map(flat, (m_norm_pre_mix, m_norm_post_mix, m_norm_pre_mlp, m_norm_post_mlp, m_ln_v_gain,
                                         m_ln_v_bias, m_b_spatial, m_w_spatial))))
    small_v = dict(zip(SMALL, map(flat, (v_norm_pre_mix, v_norm_post_mix, v_norm_pre_mlp, v_norm_post_mlp, v_ln_v_gain,
                                         v_ln_v_bias, v_b_spatial, v_w_spatial))))
    loss_out, sg, sd, sm, sv = _small_step(loss_row, small, small_w, small_m, small_v)
    tie = comm.rs_mid("mid", after=(loss_out,))
    mlp_own, mlp_chips = comm.rs_end("mlp", after=(tie,))
    adam = lambda nm, own, r, w, m, v: _adam_shard(own, r, w[0], m[0], v[0], "adam_" + nm)
    upd = {
        "w_mlp_in": adam("w_mlp_in", mlp_own[0], mlp_chips[0], w_mlp_in, m_w_mlp_in, v_w_mlp_in),
        "w_mlp_out": adam("w_mlp_out", mlp_own[1], mlp_chips[1], w_mlp_out, m_w_mlp_out, v_w_mlp_out),
    }
    loss = loss_out[0, 0]
    win_own, win_chips = comm.rs_end("win", after=(upd["w_mlp_in"][0], upd["w_mlp_out"][0]))
    upd["w_in"] = tuple(t.T for t in _adam_shard(win_own[0], win_chips[0], w_in_t, m_w_in_t, v_w_in_t, "adam_w_in"))
    mid_own, mid_chips = comm.rs_end("mid", after=(upd["w_in"][0],))
    upd["w_branch_attn"] = adam("w_branch_attn", mid_own[0], mid_chips[0], w_branch_attn, m_w_branch_attn, v_w_branch_attn)
    upd["w_branch_gmlp"] = adam("w_branch_gmlp", mid_own[1], mid_chips[1], w_branch_gmlp, m_w_branch_gmlp, v_w_branch_gmlp)
    upd["w_out"] = adam("w_out", mid_own[2], mid_chips[2], w_out, m_w_out, v_w_out)
    order = ["g1", "w_in", "w_sp", "b_sp", "ln_g", "ln_b", "w_branch_attn", "w_branch_gmlp", "w_out", "g2", "g3",
             "w_mlp_in", "w_mlp_out", "g4"]
    small_shape = dict(g1=norm_pre_mix.shape, g2=norm_post_mix.shape, g3=norm_pre_mlp.shape, g4=norm_post_mlp.shape,
                       ln_g=ln_v_gain.shape, ln_b=ln_v_bias.shape, b_sp=b_spatial.shape, w_sp=w_spatial.shape)

    def pick(which):
        return [upd[nm][which][None] if nm in upd else (sg, sd, sm, sv)[which][nm].reshape(small_shape[nm])
                for nm in order]

    return (loss, grad_x[None], *pick(0), *pick(1), *pick(2), *pick(3))
```

```python
import functools
import math

import jax
import jax.numpy as jnp
from jax import lax
from jax.experimental import pallas as pl
from jax.experimental.pallas import tpu as pltpu

D = 1024
HD = 64
NSLOT = 4
GW = NSLOT * HD
DILS = (1, 4, 16)
QB = 128
ATTN_NSUB = 4
PROJ_QKV_TM = 1024
MLP_TM_FWD = 512
MLP_TM_BWD = 256
AW = 3 * GW
W_UZ0, W_G0 = 3 * AW, 3 * AW + 1024
GMLP_W = 512
CHUNK = 128
DFF = 4096
EPS = 1e-6
ROPE_THETA = 10000.0
SCALE = HD ** -0.5
NEG = -1e30

ADAM_LR = 0.001
ADAM_B1 = 0.9
ADAM_B2 = 0.999
ADAM_EPS = 1e-08
ADAM_WD = 0.01
ADAM_STEP = 10

BF = jnp.bfloat16
F32 = jnp.float32
MESH = pl.DeviceIdType.MESH
VMEM_LIMIT = 56 * 1024 * 1024


def _cp(sem):
    return pltpu.CompilerParams(dimension_semantics=sem, vmem_limit_bytes=VMEM_LIMIT)


def _dot(a, b):
    return jnp.dot(a, b, preferred_element_type=F32)


def _dot_nt(a, b):
    return lax.dot_general(a, b, (((1,), (1,)), ((), ())), preferred_element_type=F32)


def _dot_tn(a, b):
    return lax.dot_general(a, b, (((0,), (0,)), ((), ())), preferred_element_type=F32)


def _gelu(x):
    return jax.nn.gelu(x, approximate=True)


def _gelu_grad(x):
    k = math.sqrt(2.0 / math.pi)
    t = jnp.tanh(k * (x + 0.044715 * x * x * x))
    return 0.5 * (1.0 + t) + 0.5 * x * (1.0 - t * t) * (k * (1.0 + 3.0 * 0.044715 * x * x))


def _swap_halves(t):
    w = t.shape[1]
    lane = lax.broadcasted_iota(jnp.int32, t.shape, 1)
    first = (lane & (HD - 1)) < (HD // 2)
    return jnp.where(first, pltpu.roll(t, w - HD // 2, 1), pltpu.roll(t, HD // 2, 1))


def _head_mask(hh):
    lane = lax.broadcasted_iota(jnp.int32, (1, GW), 1)
    return jnp.logical_and(lane >= hh * HD, lane < (hh + 1) * HD)


def _rms_stats(xf):
    return lax.rsqrt(jnp.mean(xf * xf, axis=-1, keepdims=True) + EPS)


def _rms_bwd(xf, r, gain, dout):
    n = xf * r
    t = dout * gain
    dx = r * (t - n * jnp.mean(t * n, axis=-1, keepdims=True))
    return dx, jnp.sum(dout * n, axis=0, keepdims=True)


def _rms_fwd(x, gain, name):
    S = x.shape[0]
    tm = min(512, S)

    def body(x_ref, g_ref, h_ref):
        xf = x_ref[...]
        h_ref[...] = (xf * _rms_stats(xf) * g_ref[...]).astype(BF)

    return pl.pallas_call(
        body, name=name, out_shape=jax.ShapeDtypeStruct((S, D), BF), grid=(S // tm,),
        in_specs=[pl.BlockSpec((tm, D), lambda i: (i, 0)), pl.BlockSpec((1, D), lambda i: (0, 0))],
        out_specs=pl.BlockSpec((tm, D), lambda i: (i, 0)), compiler_params=_cp(("parallel",)))(x, gain)


def _proj(h, w_t, row0, N, epi, name, tm):
    S, K = h.shape
    tm = min(tm, S)
    nblk = N // GW

    def body(h_ref, *refs):
        hv = h_ref[...]
        o_ref = refs[-1]
        for b in range(nblk):
            y = _dot_nt(hv, refs[b][...])
            if epi == "sigmoid":
                y = jax.nn.sigmoid(y)
            o_ref[:, b * GW:(b + 1) * GW] = y.astype(BF)

    blocks = [pl.BlockSpec((GW, K), functools.partial(lambda i, b: (b, 0), b=row0 // GW + b),
                           pipeline_mode=pl.Buffered(1)) for b in range(nblk)]
    return pl.pallas_call(
        body, name=name, out_shape=jax.ShapeDtypeStruct((S, N), BF), grid=(S // tm,),
        in_specs=[pl.BlockSpec((tm, K), lambda i: (i, 0))] + blocks,
        out_specs=pl.BlockSpec((tm, N), lambda i: (i, 0)),
        compiler_params=_cp(("parallel",)))(h, *([w_t] * nblk))


def _dilate_store(val, scr, o_ref, lead, d):
    rows = val.shape[0]
    if d == 1:
        o_ref[lead + (0,)] = val.astype(o_ref.dtype)
        return
    for hf in range(2):
        scr[hf, pl.ds(0, rows), :] = val[:, hf * 128:(hf + 1) * 128]
    for r in range(d):
        for hf in range(2):
            o_ref[lead + (r, slice(None), slice(hf * 128, (hf + 1) * 128))] = (
                scr[hf, pl.ds(r, rows // d, stride=d), :].astype(o_ref.dtype))


def _undilate_load(i_ref, lead, d, scr, rows):
    if d == 1:
        return i_ref[lead + (0,)].astype(F32)
    for r in range(d):
        for hf in range(2):
            scr[hf, pl.ds(r, rows // d, stride=d), :] = (
                i_ref[lead + (r, slice(None), slice(hf * 128, (hf + 1) * 128))].astype(F32))
    return jnp.concatenate([scr[0, pl.ds(0, rows), :], scr[1, pl.ds(0, rows), :]], axis=1)


def _rope_fwd(y, c_ref, s_ref):
    cosv = jnp.concatenate([c_ref[...]] * 2, axis=1)
    sinv = jnp.concatenate([s_ref[...]] * 2, axis=1)
    return y * cosv + _swap_halves(y) * sinv


def _rope_bwd(dy, c_ref, s_ref):
    cosv = jnp.concatenate([c_ref[...]] * 2, axis=1)
    sinv = jnp.concatenate([s_ref[...]] * 2, axis=1)
    return dy * cosv + _swap_halves(dy * sinv)


def _proj_qkv(h, wqkv_t, cos_t, sin_t, g, d, name, tie):
    S, K = h.shape
    tm = min(PROJ_QKV_TM, S)

    def body(h_ref, wq_ref, wk_ref, wv_ref, c_ref, s_ref, tie_ref, o_ref, scr):
        hv = h_ref[...]
        ys = [_dot_nt(hv, w_ref[...]) for w_ref in (wq_ref, wk_ref, wv_ref)]
        for t in range(3):
            y = _rope_fwd(ys[t], c_ref, s_ref) if t < 2 else ys[t]
            _dilate_store(y, scr.at[t], o_ref, (t,), d)

    w_spec = lambda t: pl.BlockSpec((GW, K), lambda i: (3 * t + g, 0))
    return pl.pallas_call(
        body, name=name, out_shape=jax.ShapeDtypeStruct((3, d, S // d, GW), BF), grid=(S // tm,),
        in_specs=[pl.BlockSpec((tm, K), lambda i: (i, 0)), w_spec(0), w_spec(1), w_spec(2),
                  pl.BlockSpec((tm, 128), lambda i: (i, 0)), pl.BlockSpec((tm, 128), lambda i: (i, 0)),
                  pl.BlockSpec(memory_space=pl.ANY)],
        out_specs=pl.BlockSpec((3, d, tm // d, GW), lambda i: (0, 0, i, 0)),
        scratch_shapes=[pltpu.VMEM((3, 2, tm, 128), F32)],
        compiler_params=_cp(("parallel",)))(h, wqkv_t, wqkv_t, wqkv_t, cos_t, sin_t, tie)


def _band_masks(first_step):
    row = lax.broadcasted_iota(jnp.int32, (QB, 2 * QB), 0)
    col = lax.broadcasted_iota(jnp.int32, (QB, 2 * QB), 1)
    band = jnp.logical_and(col >= row, col <= row + QB)
    return band, jnp.logical_and(band, jnp.logical_or(col >= QB, jnp.logical_not(first_step)))


def _attn_fwd(qkv, name):
    _, d, L, _ = qkv.shape
    nsub = min(ATTN_NSUB, L // QB)
    R = nsub * QB
    nsteps = L // R

    def body(q_ref, kp_ref, kc_ref, vp_ref, vc_ref, o_ref, lse_ref):
        i = pl.program_id(1)
        band, band_first = _band_masks(i == 0)
        kfull = jnp.concatenate([kp_ref[...], kc_ref[...]], axis=0)
        vfull = jnp.concatenate([vp_ref[...], vc_ref[...]], axis=0)
        chains = [(sb, hh) for sb in range(nsub) for hh in range(NSLOT)]
        win = lambda t, sb: t[sb * QB:(sb + 2) * QB]
        scores = []
        for sb, hh in chains:
            qh = jnp.where(_head_mask(hh), q_ref[sb * QB:(sb + 1) * QB, :], 0)
            scores.append(_dot_nt(qh, win(kfull, sb)))
        soft = []
        for (sb, hh), sc in zip(chains, scores):
            sc = jnp.where(band_first if sb == 0 else band, sc * SCALE, NEG)
            m = jnp.max(sc, axis=1, keepdims=True)
            p = jnp.exp(sc - m)
            den = jnp.sum(p, axis=1, keepdims=True)
            soft.append((p.astype(BF), den, m + jnp.log(den)))
        accs = [_dot(p, win(vfull, sb)) for (sb, hh), (p, _, _) in zip(chains, soft)]
        for sb in range(nsub):
            o = jnp.zeros((QB, GW), F32)
            lse = jnp.zeros((QB, GW), F32)
            for hh in range(NSLOT):
                hm = _head_mask(hh)
                _, den, lrow = soft[sb * NSLOT + hh]
                o = o + jnp.where(hm, accs[sb * NSLOT + hh] / den, 0.0)
                lse = lse + jnp.where(hm, lrow, 0.0)
            o_ref[sb * QB:(sb + 1) * QB, :] = o
            lse_ref[sb * QB:(sb + 1) * QB, :] = lse

    prev = lambda i: jnp.maximum(i * nsub - 1, 0)
    cur = lambda t: pl.BlockSpec((None, None, R, GW), lambda r, i: (t, r, i, 0))
    prv = lambda t: pl.BlockSpec((None, None, QB, GW), lambda r, i: (t, r, prev(i), 0))
    out = pl.BlockSpec((None, R, GW), lambda r, i: (r, i, 0))
    return pl.pallas_call(
        body, name=name, grid=(d, nsteps),
        out_shape=(jax.ShapeDtypeStruct((d, L, GW), F32), jax.ShapeDtypeStruct((d, L, GW), F32)),
        in_specs=[cur(0), prv(1), cur(1), prv(2), cur(2)],
        out_specs=(out, out), compiler_params=_cp(("parallel", "arbitrary")))(qkv, qkv, qkv, qkv, qkv)


def _attn_combine(os_, lses):
    S = os_[0].shape[1]
    tm = min(1024, S)

    def body(o0, o1, o2, l0, l1, l2, y_ref, j0, j1, j2, scr):
        os_nat = [_undilate_load(o, (), d, scr, tm) for o, d in zip((o0, o1, o2), DILS)]
        a, b, c = [_undilate_load(l, (), d, scr, tm) for l, d in zip((l0, l1, l2), DILS)]
        m = jnp.maximum(jnp.maximum(a, b), c)
        wa, wb, wc = jnp.exp(a - m), jnp.exp(b - m), jnp.exp(c - m)
        den = wa + wb + wc
        y_ref[...] = ((wa * os_nat[0] + wb * os_nat[1] + wc * os_nat[2]) / den).astype(BF)
        lse = m + jnp.log(den)
        for j_ref, d in zip((j0, j1, j2), DILS):
            _dilate_store(lse, scr, j_ref, (), d)

    dil = lambda d: pl.BlockSpec((d, tm // d, GW), lambda i: (0, i, 0))
    dshape = lambda d: jax.ShapeDtypeStruct((d, S // d, GW), F32)
    return pl.pallas_call(
        body, name="attn_combine", grid=(S // tm,),
        out_shape=(jax.ShapeDtypeStruct((S, GW), BF),) + tuple(dshape(d) for d in DILS),
        in_specs=[dil(d) for d in DILS] * 2,
        out_specs=(pl.BlockSpec((tm, GW), lambda i: (i, 0)),) + tuple(dil(d) for d in DILS),
        scratch_shapes=[pltpu.VMEM((2, tm, 128), F32)],
        compiler_params=_cp(("parallel",)))(*os_, *lses)


def _tril(upper=False):
    row = lax.broadcasted_iota(jnp.int32, (CHUNK, CHUNK), 0)
    col = lax.broadcasted_iota(jnp.int32, (CHUNK, CHUNK), 1)
    return row <= col if upper else col <= row


def _ln_fwd(z, gain, bias):
    mu = jnp.mean(z, axis=-1, keepdims=True)
    zc = z - mu
    rstd = lax.rsqrt(jnp.mean(zc * zc, axis=-1, keepdims=True) + EPS)
    zhat = zc * rstd
    return zhat, rstd, zhat * gain + bias


def _gmlp_fwd(puz, ln_g, ln_b, w_sp, b_sp_t):
    S = puz.shape[0]
    tm = min(512, S)
    nch = tm // CHUNK

    def body(p_ref, g_ref, b_ref, w_ref, bt_ref, o_ref):
        tril = _tril()
        ws = [jnp.where(tril, w_ref[gg], 0.0).astype(BF) for gg in range(4)]
        for ch in range(nch):
            rows = slice(ch * CHUNK, (ch + 1) * CHUNK)
            z = _gelu(p_ref[rows, GMLP_W:].astype(F32))
            _, _, zn = _ln_fwd(z, g_ref[...], b_ref[...])
            zn = zn.astype(BF)
            for gg in range(4):
                cols = slice(gg * CHUNK, (gg + 1) * CHUNK)
                sz = _dot(ws[gg], zn[:, cols]) + bt_ref[:, gg:gg + 1]
                u = _gelu(p_ref[rows, cols].astype(F32))
                o_ref[rows, cols] = (u * sz).astype(BF)

    return pl.pallas_call(
        body, name="gmlp_fwd", out_shape=jax.ShapeDtypeStruct((S, GMLP_W), BF), grid=(S // tm,),
        in_specs=[pl.BlockSpec((tm, 2 * GMLP_W), lambda i: (i, 0)),
                  pl.BlockSpec((1, GMLP_W), lambda i: (0, 0)), pl.BlockSpec((1, GMLP_W), lambda i: (0, 0)),
                  pl.BlockSpec((4, CHUNK, CHUNK), lambda i: (0, 0, 0)), pl.BlockSpec((CHUNK, 4), lambda i: (0, 0))],
        out_specs=pl.BlockSpec((tm, GMLP_W), lambda i: (i, 0)),
        compiler_params=_cp(("parallel",)))(puz, ln_g, ln_b, w_sp, b_sp_t)


def _merge_fwd(ya, yg, gates, wba, wbg, wout, x, g2, g3):
    S = x.shape[0]
    tm = min(512, S)

    def body(ya_ref, yg_ref, gt_ref, wba_ref, wbg_ref, wo_ref, x_ref, g2_ref, g3_ref,
             mg_ref, y_ref, x2_ref, h3_ref):
        a = _dot(ya_ref[...], wba_ref[...])
        b = _dot(yg_ref[...], wbg_ref[...])
        merged = (gt_ref[:, :D].astype(F32) * a + gt_ref[:, D:].astype(F32) * b).astype(BF)
        mg_ref[...] = merged
        y = _dot(merged, wo_ref[...])
        y_ref[...] = y
        x2 = x_ref[...] + y * _rms_stats(y) * g2_ref[...]
        x2_ref[...] = x2
        h3_ref[...] = (x2 * _rms_stats(x2) * g3_ref[...]).astype(BF)

    row = lambda w: pl.BlockSpec((tm, w), lambda i: (i, 0))
    full = lambda s: pl.BlockSpec(s, lambda i: (0, 0))
    return pl.pallas_call(
        body, name="merge_fwd", grid=(S // tm,),
        out_shape=(jax.ShapeDtypeStruct((S, D), BF), jax.ShapeDtypeStruct((S, D), F32), jax.ShapeDtypeStruct((S, D), F32),
                   jax.ShapeDtypeStruct((S, D), BF)),
        in_specs=[row(GW), row(GMLP_W), row(2 * D), full((GW, D)), full((GMLP_W, D)), full((D, D)), row(D),
                  full((1, D)), full((1, D))],
        out_specs=(row(D), row(D), row(D), row(D)),
        compiler_params=_cp(("parallel",)))(ya, yg, gates, wba, wbg, wout, x, g2, g3)


def _resident(shape):
    return pl.BlockSpec(shape, lambda i: (0,) * len(shape), pipeline_mode=pl.Buffered(1))


def _mlp_fwd(h3, wmi, wmo, x2, tgt, g4):
    S = x2.shape[0]
    tm = min(MLP_TM_FWD, S)

    def body(h_ref, wi_ref, wo_ref, x2_ref, t_ref, g4_ref, a_ref, dy2_ref, dout_ref, loss_ref, dg4_ref):
        @pl.when(pl.program_id(0) == 0)
        def _():
            loss_ref[...] = jnp.zeros_like(loss_ref)
            dg4_ref[...] = jnp.zeros_like(dg4_ref)

        halves = [slice(hh * (tm // 2), (hh + 1) * (tm // 2)) for hh in range(2)]
        acts = []
        for rows in halves:
            a = jnp.maximum(_dot(h_ref[rows, :], wi_ref[...]), 0.0)
            a_ref[rows, :] = a.astype(BF)
            acts.append((a * a).astype(BF))
        y2s = [_dot(a2, wo_ref[...]) for a2 in acts]
        lane = lax.broadcasted_iota(jnp.int32, (1, 128), 1)
        for rows, y2 in zip(halves, y2s):
            r = _rms_stats(y2)
            out = x2_ref[rows, :] + y2 * r * g4_ref[...]
            err = out - t_ref[rows, :]
            tot = jnp.sum(jnp.sum(err * err, axis=1, keepdims=True), axis=0, keepdims=True) * (0.5 / D)
            loss_ref[...] += jnp.where(lane == 0, tot, 0.0)
            dout = err * (1.0 / D)
            dout_ref[rows, :] = dout
            dy2, dg = _rms_bwd(y2, r, g4_ref[...], dout)
            dy2_ref[rows, :] = dy2.astype(BF)
            dg4_ref[...] += dg

    row = pl.BlockSpec((tm, D), lambda i: (i, 0))
    return pl.pallas_call(
        body, name="mlp_fwd", grid=(S // tm,),
        out_shape=(jax.ShapeDtypeStruct((S, DFF), BF), jax.ShapeDtypeStruct((S, D), BF), jax.ShapeDtypeStruct((S, D), F32),
                   jax.ShapeDtypeStruct((1, 128), F32), jax.ShapeDtypeStruct((1, D), F32)),
        in_specs=[row, _resident((D, DFF)), _resident((DFF, D)), row, row, pl.BlockSpec((1, D), lambda i: (0, 0))],
        out_specs=(pl.BlockSpec((tm, DFF), lambda i: (i, 0)), row, row,
                   pl.BlockSpec((1, 128), lambda i: (0, 0)), pl.BlockSpec((1, D), lambda i: (0, 0))),
        compiler_params=_cp(("arbitrary",)))(h3, wmi, wmo, x2, tgt, g4)


def _mlp_bwd(dy2, a, wmo, wmi, x2, y, dout, g2, g3):
    S = x2.shape[0]
    tm = min(MLP_TM_BWD, S)

    def body(dy2_ref, a_ref, wo_ref, wi_ref, x2_ref, y_ref, dout_ref, g2_ref, g3_ref,
             dpre_ref, dx2_ref, dy_ref, dg3_ref, dg2_ref):
        @pl.when(pl.program_id(0) == 0)
        def _():
            dg3_ref[...] = jnp.zeros_like(dg3_ref)
            dg2_ref[...] = jnp.zeros_like(dg2_ref)

        da2 = _dot_nt(dy2_ref[...], wo_ref[...])
        dpre = (2.0 * a_ref[...].astype(F32) * da2).astype(BF)
        dpre_ref[...] = dpre
        dh3 = _dot_nt(dpre, wi_ref[...])
        x2 = x2_ref[...]
        dx3, dg3 = _rms_bwd(x2, _rms_stats(x2), g3_ref[...], dh3)
        dx2 = dout_ref[...] + dx3
        dx2_ref[...] = dx2
        dg3_ref[...] += dg3
        yv = y_ref[...]
        dy, dg2 = _rms_bwd(yv, _rms_stats(yv), g2_ref[...], dx2)
        dy_ref[...] = dy.astype(BF)
        dg2_ref[...] += dg2

    row = pl.BlockSpec((tm, D), lambda i: (i, 0))
    wide = pl.BlockSpec((tm, DFF), lambda i: (i, 0))
    vec = pl.BlockSpec((1, D), lambda i: (0, 0))
    return pl.pallas_call(
        body, name="mlp_bwd", grid=(S // tm,),
        out_shape=(jax.ShapeDtypeStruct((S, DFF), BF), jax.ShapeDtypeStruct((S, D), F32), jax.ShapeDtypeStruct((S, D), BF),
                   jax.ShapeDtypeStruct((1, D), F32), jax.ShapeDtypeStruct((1, D), F32)),
        in_specs=[row, wide, _resident((DFF, D)), _resident((D, DFF)), row, row, row, vec, vec],
        out_specs=(wide, row, row, vec, vec),
        compiler_params=_cp(("arbitrary",)))(dy2, a, wmo, wmi, x2, y, dout, g2, g3)


def _mm_tn(a, b, name, square_a=False, tm=1024, tn=1024, tk=2048, tie=None, col_blocks=None):
    S, M = a.shape
    N = b.shape[1]
    tk = min(tk, S)
    tm = max(t for t in range(128, min(tm, M) + 1, 128) if M % t == 0)
    tn = max(t for t in range(128, min(tn, N) + 1, 128) if N % t == 0)
    cb = N // col_blocks if col_blocks else tn
    assert M % tm == 0 and S % tk == 0 and tn % cb == 0
    nk = S // tk
    ties = () if tie is None else (tie,)

    def body(a_ref, b_ref, *rest):
        o_ref = rest[-1]
        k = pl.program_id(2)
        av = a_ref[...]
        if square_a:
            av = av * av
        part = _dot_tn(av, b_ref[...])
        if col_blocks:
            part = jnp.stack([part[:, t * cb:(t + 1) * cb] for t in range(tn // cb)])

        @pl.when(k == 0)
        def _():
            o_ref[...] = part

        @pl.when(k > 0)
        def _():
            o_ref[...] += part

    if col_blocks:
        out_shape, out_spec = (col_blocks, M, cb), pl.BlockSpec((tn // cb, tm, cb), lambda i, j, k: (j, i, 0))
    else:
        out_shape, out_spec = (M, N), pl.BlockSpec((tm, tn), lambda i, j, k: (i, j))
    return pl.pallas_call(
        body, name=name, out_shape=jax.ShapeDtypeStruct(out_shape, F32), grid=(M // tm, N // tn, nk),
        in_specs=[pl.BlockSpec((tk, tm), lambda i, j, k: (k, i)), pl.BlockSpec((tk, tn), lambda i, j, k: (k, j))]
        + [pl.BlockSpec(memory_space=pl.ANY)] * len(ties),
        out_specs=out_spec,
        compiler_params=_cp(("parallel", "parallel", "arbitrary")))(a, b, *ties)


def _outproj_bwd(dy, wout, ya, yg, gates, wba, wbg, tie):
    S = dy.shape[0]
    tm = min(512, S)

    def body(dy_ref, wo_ref, ya_ref, yg_ref, gt_ref, wba_ref, wbg_ref, tie_ref,
             dgt_ref, da_ref, db_ref, dyg_ref, e0, e1, e2, s0, s1, s2, scr):
        dm = _dot_nt(dy_ref[...], wo_ref[...])
        ga, gb = gt_ref[:, :D].astype(F32), gt_ref[:, D:].astype(F32)
        dgt_ref[:, :D] = (dm * _dot(ya_ref[...], wba_ref[...]) * ga * (1.0 - ga)).astype(BF)
        dgt_ref[:, D:] = (dm * _dot(yg_ref[...], wbg_ref[...]) * gb * (1.0 - gb)).astype(BF)
        da = (dm * ga).astype(BF)
        db = (dm * gb).astype(BF)
        da_ref[...] = da
        db_ref[...] = db
        dyg_ref[...] = _dot_nt(db, wbg_ref[...]).astype(BF)
        dya = _dot_nt(da, wba_ref[...]).astype(BF).astype(F32)
        dyy = dya * ya_ref[...].astype(F32)
        dsum = jnp.zeros((tm, GW), F32)
        for hh in range(NSLOT):
            hm = _head_mask(hh)
            dsum = dsum + jnp.where(hm, jnp.sum(jnp.where(hm, dyy, 0.0), axis=1, keepdims=True), 0.0)
        for e_ref, s_ref, d in zip((e0, e1, e2), (s0, s1, s2), DILS):
            _dilate_store(dya, scr, e_ref, (), d)
            _dilate_store(dsum, scr, s_ref, (), d)

    row = lambda w: pl.BlockSpec((tm, w), lambda i: (i, 0))
    full = lambda s: pl.BlockSpec(s, lambda i: (0, 0))
    dil = lambda d: pl.BlockSpec((d, tm // d, GW), lambda i: (0, i, 0))
    dshape = lambda d, t: jax.ShapeDtypeStruct((d, S // d, GW), t)
    return pl.pallas_call(
        body, name="outproj_bwd", grid=(S // tm,),
        out_shape=(jax.ShapeDtypeStruct((S, 2 * D), BF), jax.ShapeDtypeStruct((S, D), BF), jax.ShapeDtypeStruct((S, D), BF),
                   jax.ShapeDtypeStruct((S, GMLP_W), BF)) + tuple(dshape(d, BF) for d in DILS)
        + tuple(dshape(d, F32) for d in DILS),
        in_specs=[row(D), full((D, D)), row(GW), row(GMLP_W), row(2 * D), full((GW, D)), full((GMLP_W, D)),
                  pl.BlockSpec(memory_space=pl.ANY)],
        out_specs=(row(2 * D), row(D), row(D), row(GMLP_W)) + tuple(dil(d) for d in DILS) * 2,
        scratch_shapes=[pltpu.VMEM((2, tm, 128), F32)],
        compiler_params=_cp(("parallel",)))(dy, wout, ya, yg, gates, wba, wbg, tie)


def _gmlp_bwd(puz, dyg, ln_g, ln_b, w_sp, w_sp_t, b_sp_t):
    S = puz.shape[0]
    tm = min(512, S)
    nch = tm // CHUNK

    def body(p_ref, dy_ref, g_ref, b_ref, w_ref, wt_ref, bt_ref,
             dp_ref, dw_ref, dbs_ref, dg_ref, dbias_ref, dbacc_ref):
        i = pl.program_id(0)

        @pl.when(i == 0)
        def _():
            dw_ref[...] = jnp.zeros_like(dw_ref)
            dbacc_ref[...] = jnp.zeros_like(dbacc_ref)
            dg_ref[...] = jnp.zeros_like(dg_ref)
            dbias_ref[...] = jnp.zeros_like(dbias_ref)

        tril = _tril()
        ws = [jnp.where(tril, w_ref[gg], 0.0).astype(BF) for gg in range(4)]
        triu = _tril(upper=True)
        wts = [jnp.where(triu, wt_ref[gg], 0.0).astype(BF) for gg in range(4)]
        gain = g_ref[...]
        for ch in range(nch):
            rows = slice(ch * CHUNK, (ch + 1) * CHUNK)
            pz = p_ref[rows, GMLP_W:].astype(F32)
            z = _gelu(pz)
            zhat, rstd, zn = _ln_fwd(z, gain, b_ref[...])
            znb = zn.astype(BF)
            dzn_parts = []
            for gg in range(4):
                cols = slice(gg * CHUNK, (gg + 1) * CHUNK)
                pu = p_ref[rows, cols].astype(F32)
                u = _gelu(pu)
                sz = _dot(ws[gg], znb[:, cols]) + bt_ref[:, gg:gg + 1]
                dyv = dy_ref[rows, cols].astype(F32)
                dp_ref[rows, cols] = (dyv * sz * _gelu_grad(pu)).astype(BF)
                dsz = dyv * u
                dbacc_ref[gg] += dsz
                dszb = dsz.astype(BF)
                dw_ref[gg] += _dot_nt(dszb, znb[:, cols])
                dzn_parts.append(_dot(wts[gg], dszb))
            dzn = jnp.concatenate(dzn_parts, axis=1)
            dg_ref[...] += jnp.sum(dzn * zhat, axis=0, keepdims=True)
            dbias_ref[...] += jnp.sum(dzn, axis=0, keepdims=True)
            dzh = dzn * gain
            dz = rstd * (dzh - jnp.mean(dzh, axis=-1, keepdims=True)
                         - zhat * jnp.mean(dzh * zhat, axis=-1, keepdims=True))
            dp_ref[rows, GMLP_W:] = (dz * _gelu_grad(pz)).astype(BF)

        @pl.when(i == pl.num_programs(0) - 1)
        def _():
            for gg in range(4):
                dw_ref[gg] = jnp.where(tril, dw_ref[gg], 0.0)
                dbs_ref[gg] = jnp.sum(dbacc_ref[gg], axis=1, keepdims=True)

    full2 = lambda s: pl.BlockSpec(s, lambda i: (0, 0))
    full3 = lambda s: pl.BlockSpec(s, lambda i: (0, 0, 0))
    return pl.pallas_call(
        body, name="gmlp_bwd", grid=(S // tm,),
        out_shape=(jax.ShapeDtypeStruct((S, 2 * GMLP_W), BF), jax.ShapeDtypeStruct((4, CHUNK, CHUNK), F32),
                   jax.ShapeDtypeStruct((4, CHUNK, 1), F32), jax.ShapeDtypeStruct((1, GMLP_W), F32),
                   jax.ShapeDtypeStruct((1, GMLP_W), F32)),
        in_specs=[pl.BlockSpec((tm, 2 * GMLP_W), lambda i: (i, 0)), pl.BlockSpec((tm, GMLP_W), lambda i: (i, 0)),
                  full2((1, GMLP_W)), full2((1, GMLP_W)), full3((4, CHUNK, CHUNK)), full3((4, CHUNK, CHUNK)),
                  full2((CHUNK, 4))],
        out_specs=(pl.BlockSpec((tm, 2 * GMLP_W), lambda i: (i, 0)), full3((4, CHUNK, CHUNK)), full3((4, CHUNK, 1)),
                   full2((1, GMLP_W)), full2((1, GMLP_W))),
        scratch_shapes=[pltpu.VMEM((4, CHUNK, CHUNK), F32)],
        compiler_params=_cp(("arbitrary",)))(puz, dyg, ln_g, ln_b, w_sp, w_sp_t, b_sp_t)


def _attn_bwd(qkv, dya, dsums, lse, name):
    _, d, L, _ = qkv.shape
    nsub = min(ATTN_NSUB, L // QB)
    R = nsub * QB
    nsteps = L // R

    def body(q_ref, qn_ref, kp_ref, kc_ref, vp_ref, vc_ref, dy_ref, dyn_ref, e_ref, en_ref, l_ref, ln_ref, o_ref):
        i = pl.program_id(1)
        band, band_first = _band_masks(i == 0)
        row = lax.broadcasted_iota(jnp.int32, (QB, QB), 0)
        col = lax.broadcasted_iota(jnp.int32, (QB, QB), 1)
        mask_next = jnp.logical_and(col >= row, i < nsteps - 1)
        kc, vc = kc_ref[...], vc_ref[...]
        kfull = jnp.concatenate([kp_ref[...], kc], axis=0)
        vfull = jnp.concatenate([vp_ref[...], vc], axis=0)
        k_last, v_last = kc[(nsub - 1) * QB:], vc[(nsub - 1) * QB:]
        q_ext = jnp.concatenate([q_ref[...], qn_ref[...]], axis=0)
        dy_ext = jnp.concatenate([dy_ref[...], dyn_ref[...]], axis=0)
        esum, esum_n, lse, lse_n = e_ref[...], en_ref[...], l_ref[...], ln_ref[...]
        win = lambda t, sb: t[sb * QB:(sb + 2) * QB]
        blk = lambda t, sb: t[sb * QB:(sb + 1) * QB]
        hms = [_head_mask(hh) for hh in range(NSLOT)]
        q_hs = [jnp.where(hm, q_ext, 0) for hm in hms]
        dy_hs = [jnp.where(hm, dy_ext, 0) for hm in hms]
        raw = []
        for hh in range(NSLOT):
            tiles = [(_dot_nt(blk(q_hs[hh], sb), win(kfull, sb)), _dot_nt(blk(dy_hs[hh], sb), win(vfull, sb)))
                     for sb in range(nsub)]
            tiles.append((_dot_nt(blk(q_hs[hh], nsub), k_last), _dot_nt(blk(dy_hs[hh], nsub), v_last)))
            raw.append(tiles)
        ps, dss = [], []
        for hh in range(NSLOT):
            rowstat = lambda t: jnp.max(jnp.where(hms[hh], t, -jnp.inf), axis=1, keepdims=True)
            p_h, ds_h = [], []
            for sb in range(nsub + 1):
                sc, dp = raw[hh][sb]
                if sb < nsub:
                    msk, lrow, erow = (band_first if sb == 0 else band), rowstat(blk(lse, sb)), rowstat(blk(esum, sb))
                else:
                    msk, lrow, erow = mask_next, rowstat(lse_n), rowstat(esum_n)
                p = jnp.where(msk, jnp.exp(sc * SCALE - lrow), 0.0)
                p_h.append(p.astype(BF))
                ds_h.append((p * (dp - erow)).astype(BF))
            ps.append(p_h)
            dss.append(ds_h)
        dq = [jnp.zeros((QB, GW), F32) for _ in range(nsub)]
        dk = [jnp.zeros((QB, GW), F32) for _ in range(nsub)]
        dv = [jnp.zeros((QB, GW), F32) for _ in range(nsub)]
        for hh in range(NSLOT):
            for sb in range(nsub):
                dq[sb] = dq[sb] + jnp.where(hms[hh], _dot(dss[hh][sb], win(kfull, sb)), 0.0)
                nxt = lambda t: t[sb + 1][:, :QB] if sb + 1 < nsub else t[nsub]
                dk[sb] = dk[sb] + _dot_tn(jnp.concatenate([dss[hh][sb][:, QB:], nxt(dss[hh])], axis=0), win(q_hs[hh], sb))
                dv[sb] = dv[sb] + _dot_tn(jnp.concatenate([ps[hh][sb][:, QB:], nxt(ps[hh])], axis=0), win(dy_hs[hh], sb))
        for sb in range(nsub):
            rows = slice(sb * QB, (sb + 1) * QB)
            o_ref[0, rows, :] = (dq[sb] * SCALE).astype(BF)
            o_ref[1, rows, :] = (dk[sb] * SCALE).astype(BF)
            o_ref[2, rows, :] = dv[sb].astype(BF)

    prev = lambda i: jnp.maximum(i * nsub - 1, 0)
    nxt = lambda i: jnp.minimum((i + 1) * nsub, L // QB - 1)
    cur4 = lambda t: pl.BlockSpec((None, None, R, GW), lambda r, i: (t, r, i, 0))
    prv4 = lambda t: pl.BlockSpec((None, None, QB, GW), lambda r, i: (t, r, prev(i), 0))
    nxt4 = lambda t: pl.BlockSpec((None, None, QB, GW), lambda r, i: (t, r, nxt(i), 0))
    cur3 = pl.BlockSpec((None, R, GW), lambda r, i: (r, i, 0))
    nxt3 = pl.BlockSpec((None, QB, GW), lambda r, i: (r, nxt(i), 0))
    return pl.pallas_call(
        body, name=name, grid=(d, nsteps), out_shape=jax.ShapeDtypeStruct((3, d, L, GW), BF),
        in_specs=[cur4(0), nxt4(0), prv4(1), cur4(1), prv4(2), cur4(2), cur3, nxt3, cur3, nxt3, cur3, nxt3],
        out_specs=pl.BlockSpec((3, None, R, GW), lambda r, i: (0, r, i, 0)),
        compiler_params=_cp(("parallel", "arbitrary")))(qkv, qkv, qkv, qkv, qkv, qkv, dya, dya, dsums, dsums, lse, lse)


def _inproj_bwd(dqkvs, dpuz, dgates, win_t, cos_t, sin_t, x, dx2, g1):
    S = x.shape[0]
    tm = min(512, S)

    def body(d0_ref, d1_ref, d2_ref, dp_ref, dg_ref, w_ref, c_ref, s_ref,
             x_ref, dx2_ref, g1_ref, gx_ref, dg1_ref, dn_ref, scr):
        i = pl.program_id(0)

        @pl.when(i == 0)
        def _():
            dg1_ref[...] = jnp.zeros_like(dg1_ref)

        dh = _dot(dp_ref[...], w_ref[W_UZ0:W_G0, :]) + _dot(dg_ref[...], w_ref[W_G0:, :])
        for t in range(3):
            for g, (d_ref, d) in enumerate(zip((d0_ref, d1_ref, d2_ref), DILS)):
                piece = _undilate_load(d_ref, (t,), d, scr, tm)
                if t < 2:
                    piece = _rope_bwd(piece, c_ref, s_ref)
                dn_ref[:, (3 * t + g) * GW:(3 * t + g + 1) * GW] = piece.astype(BF)
        dh = dh + _dot(dn_ref[...], w_ref[:W_UZ0, :])
        xv = x_ref[...]
        dx1, dg1 = _rms_bwd(xv, _rms_stats(xv), g1_ref[...], dh)
        gx_ref[...] = dx2_ref[...] + dx1
        dg1_ref[...] += dg1

    row = lambda w: pl.BlockSpec((tm, w), lambda i: (i, 0))
    full = lambda s: pl.BlockSpec(s, lambda i: (0, 0))
    dil = lambda d: pl.BlockSpec((3, d, tm // d, GW), lambda i: (0, 0, i, 0))
    return pl.pallas_call(
        body, name="inproj_bwd", grid=(S // tm,),
        out_shape=(jax.ShapeDtypeStruct((S, D), F32), jax.ShapeDtypeStruct((1, D), F32),
                   jax.ShapeDtypeStruct((S, 3 * AW), BF)),
        in_specs=[dil(d) for d in DILS] + [row(2 * GMLP_W), row(2 * D), _resident(win_t.shape), row(128), row(128), row(D), row(D),
                                            full((1, D))],
        out_specs=(row(D), full((1, D)), row(3 * AW)),
        scratch_shapes=[pltpu.VMEM((2, tm, 128), F32)],
        compiler_params=_cp(("arbitrary",)))(*dqkvs, dpuz, dgates, win_t, cos_t, sin_t, x, dx2, g1)


def _row_tile(rows, cap=256):
    return max(t for t in range(16, cap + 1, 16) if rows % t == 0)


def _adam_math(w, g, m, v):
    m2 = ADAM_B1 * m + (1.0 - ADAM_B1) * g
    v2 = ADAM_B2 * v + (1.0 - ADAM_B2) * (g * g)
    m_hat = m2 / (1.0 - ADAM_B1 ** ADAM_STEP)
    v_hat = v2 / (1.0 - ADAM_B2 ** ADAM_STEP)
    delta = -ADAM_LR * (m_hat / (jnp.sqrt(v_hat) + ADAM_EPS) + ADAM_WD * w)
    return delta, m2, v2


def _adam_shard(own, recv, w, m, v, name):
    R, C = w.shape
    tr = _row_tile(R)

    def body(own_ref, r_ref, w_ref, m_ref, v_ref, g_ref, d_ref, m2_ref, v2_ref):
        g = own_ref[...] + r_ref[0].astype(F32) + r_ref[1].astype(F32) + r_ref[2].astype(F32)
        g_ref[...] = g
        d_ref[...], m2_ref[...], v2_ref[...] = _adam_math(w_ref[...], g, m_ref[...], v_ref[...])

    spec = pl.BlockSpec((tr, C), lambda i: (i, 0))
    out = jax.ShapeDtypeStruct((R, C), F32)
    return pl.pallas_call(
        body, name=name, grid=(R // tr,), out_shape=(out, out, out, out),
        in_specs=[spec, pl.BlockSpec((3, tr, C), lambda i: (0, i, 0)), spec, spec, spec],
        out_specs=(spec, spec, spec, spec), compiler_params=_cp(("parallel",)))(own, recv, w, m, v)


def _rs_add(gblocks, recv, idx, name):
    _, R, C = gblocks.shape
    tr = _row_tile(R)

    def body(t_ref, g_ref, r_ref, own_ref, send_ref):
        j = pl.program_id(1)
        s = g_ref[...] + r_ref[...]

        @pl.when(j == 0)
        def _():
            own_ref[...] = s

        @pl.when(j > 0)
        def _():
            send_ref[...] = s.astype(BF)

    grid_spec = pltpu.PrefetchScalarGridSpec(
        num_scalar_prefetch=1, grid=(R // tr, 4),
        in_specs=[pl.BlockSpec((None, tr, C), lambda i, j, t: (t[j], i, 0)),
                  pl.BlockSpec((None, tr, C), lambda i, j, t: (t[4 + j], i, 0))],
        out_specs=[pl.BlockSpec((tr, C), lambda i, j, t: (i, 0)),
                   pl.BlockSpec((None, tr, C), lambda i, j, t: (jnp.maximum(j - 1, 0), i, 0))])
    return pl.pallas_call(
        body, name=name, grid_spec=grid_spec,
        out_shape=(jax.ShapeDtypeStruct((R, C), F32), jax.ShapeDtypeStruct((3, R, C), BF)),
        compiler_params=_cp(("parallel", "arbitrary")))(idx, gblocks, recv)


def _mesh_pos():
    return lax.axis_index("x"), lax.axis_index("y"), lax.axis_index("c")


_HBM = pl.BlockSpec(memory_space=pltpu.HBM)
_SEM = pl.BlockSpec(memory_space=pltpu.SEMAPHORE)
_EFFECT = pltpu.SideEffectType.DATAFLOW_SIDE_EFFECTING
_RELATIONS = [(dx, dy, dc) for dx in (0, 1) for dy in (0, 1) for dc in (0, 1)][1:]


def _flip(v, d):
    return 1 - v if d else v


def _plan_gather(n):
    def plan(x, y, c):
        return [(k, None, 4 * x + 2 * y + c, (_flip(x, dx), _flip(y, dy), _flip(c, dc)))
                for k in range(n) for dx, dy, dc in _RELATIONS]
    return plan


def _plan_gather_near(x, y, c):
    chips = [(1 - x, y), (x, 1 - y), (1 - x, 1 - y)]
    return [(0, None, 4 * x + 2 * y + c, (x, y, 1 - c))] + [(0, None, 4 * x + 2 * y + c, (*chip, c)) for chip in chips]


def _plan_gather_pass(from_landing):
    def plan(x, y, c):
        blocks = [4 * cx + 2 * cy + c for cx, cy in ((1 - x, y), (x, 1 - y), (1 - x, 1 - y))]
        return [(0, b if from_landing else None, b, (x, y, 1 - c)) for b in blocks]
    return plan


def _plan_d2d(n):
    def plan(x, y, c):
        return [(k, 2 * kk + 1 - c, kk, (x, y, 1 - c)) for k in range(n) for kk in range(4)]
    return plan


def _plan_ici(n):
    def plan(x, y, c):
        return [(k, j, j, (_flip(x, dx), _flip(y, dy), c))
                for k in range(n) for j, (dx, dy) in enumerate(((1, 0), (0, 1), (1, 1)))]
    return plan


def _plan_copies(plan, src_refs, land_refs, send_sems, recv_sems):
    x, y, c = _mesh_pos()
    return [pltpu.make_async_remote_copy(
        src_ref=src_refs[k] if si is None else src_refs[k].at[si], dst_ref=land_refs[k].at[di],
        send_sem=send_sems.at[n], recv_sem=recv_sems.at[n], device_id=dev, device_id_type=MESH)
        for n, (k, si, di, dev) in enumerate(plan(x, y, c))]


def _exchange_start(srcs, land_shapes, plan, ncopies, name, after):
    n = len(srcs)

    def body(*refs):
        src_refs, land_refs = refs[:n], refs[n:2 * n]
        send_sems, recv_sems = refs[2 * n + len(after)], refs[2 * n + len(after) + 1]
        token = refs[-1]
        for cp in _plan_copies(plan, src_refs, land_refs, send_sems, recv_sems):
            cp.start()
        token[...] = jnp.zeros_like(token)

    lands = [pltpu.with_memory_space_constraint(lax.empty(s, a.dtype), pltpu.HBM) for s, a in zip(land_shapes, srcs)]
    srcs = [pltpu.with_memory_space_constraint(a, pltpu.HBM) for a in srcs]
    outs = pl.pallas_call(
        body, name=name,
        out_shape=(pltpu.SemaphoreType.DMA((ncopies,)), pltpu.SemaphoreType.DMA((ncopies,)))
        + tuple(pltpu.HBM(a.shape, a.dtype) for a in srcs) + tuple(pltpu.HBM(a.shape, a.dtype) for a in lands)
        + (jax.ShapeDtypeStruct((8, 128), F32),),
        in_specs=[_HBM] * (2 * n) + [pl.BlockSpec(memory_space=pl.ANY)] * len(after),
        out_specs=(_SEM, _SEM) + (_HBM,) * (2 * n) + (pl.BlockSpec(memory_space=pltpu.VMEM),),
        input_output_aliases={i: 2 + i for i in range(2 * n)},
        compiler_params=pltpu.CompilerParams(has_side_effects=_EFFECT))(*srcs, *lands, *after)
    return (outs[0], outs[1], list(outs[2:2 + n]), list(outs[2 + n:2 + 2 * n])), outs[-1]


def _exchange_forward(handle, plan, plan_fwd, nfwd, name, after):
    send_sems, recv_sems, srcs, lands = handle
    n = len(srcs)

    def body(*refs):
        src_refs, land_refs = refs[:n], refs[n:2 * n]
        outs = refs[2 * n + 2 + len(after):]
        for cp in _plan_copies(plan, src_refs, land_refs, refs[2 * n], refs[2 * n + 1]):
            cp.wait_send()
            cp.wait_recv()
        for cp in _plan_copies(plan_fwd, land_refs, land_refs, outs[0], outs[1]):
            cp.start()

    outs = pl.pallas_call(
        body, name=name,
        out_shape=(pltpu.SemaphoreType.DMA((nfwd,)), pltpu.SemaphoreType.DMA((nfwd,)))
        + tuple(pltpu.HBM(a.shape, a.dtype) for a in srcs) + tuple(pltpu.HBM(a.shape, a.dtype) for a in lands),
        in_specs=[_HBM] * (2 * n) + [_SEM, _SEM] + [pl.BlockSpec(memory_space=pl.ANY)] * len(after),
        out_specs=(_SEM, _SEM) + (_HBM,) * (2 * n), input_output_aliases={i: 2 + i for i in range(2 * n)},
        compiler_params=pltpu.CompilerParams(has_side_effects=_EFFECT))(*srcs, *lands, send_sems, recv_sems, *after)
    return outs[0], outs[1], list(outs[2:2 + n]), list(outs[2 + n:2 + 2 * n])


def _exchange_wait(handle, plan, name, after):
    send_sems, recv_sems, srcs, lands = handle
    n = len(srcs)

    def body(*refs):
        src_refs, land_refs = refs[:n], refs[n:2 * n]
        for cp in _plan_copies(plan, src_refs, land_refs, refs[2 * n], refs[2 * n + 1]):
            cp.wait_send()
            cp.wait_recv()

    outs = pl.pallas_call(
        body, name=name,
        out_shape=tuple(pltpu.HBM(a.shape, a.dtype) for a in srcs) + tuple(pltpu.HBM(a.shape, a.dtype) for a in lands),
        in_specs=[_HBM] * (2 * n) + [_SEM, _SEM] + [pl.BlockSpec(memory_space=pl.ANY)] * len(after),
        out_specs=(_HBM,) * (2 * n), input_output_aliases={i: i for i in range(2 * n)},
        compiler_params=pltpu.CompilerParams(has_side_effects=_EFFECT))(*srcs, *lands, send_sems, recv_sems, *after)
    return list(outs[:n]), list(outs[n:])


SMALL = ("g1", "g2", "g3", "g4", "ln_g", "ln_b", "b_sp", "w_sp")


def _small_sum_adam(block, parts, lands, ws, ms, vs):
    n, ns = len(parts), len(SMALL)

    def body(blk_ref, *refs):
        p_refs, l_refs = refs[:n], refs[n:2 * n]
        w_refs, m_refs, v_refs = (refs[2 * n + i * ns:2 * n + (i + 1) * ns] for i in range(3))
        o = 2 * n + 3 * ns
        loss_ref = refs[o]
        g_out, d_out, m_out, v_out = (refs[o + 1 + i * ns:o + 1 + (i + 1) * ns] for i in range(4))
        me = blk_ref[0]
        sums = []
        for k in range(n):
            acc = jnp.where(me == 0, p_refs[k][...], l_refs[k][0])
            for b in range(1, 8):
                acc = acc + jnp.where(me == b, p_refs[k][...], l_refs[k][b])
            sums.append(acc)
        loss_ref[...] = sums[0]
        for i in range(ns):
            g_out[i][...] = sums[1 + i]
            d_out[i][...], m_out[i][...], v_out[i][...] = _adam_math(w_refs[i][...], sums[1 + i], m_refs[i][...],
                                                                     v_refs[i][...])

    args = list(parts) + list(lands) + [t[k] for t in (ws, ms, vs) for k in SMALL]
    shapes = [jax.ShapeDtypeStruct(p.shape, F32) for p in parts[:1]] + [jax.ShapeDtypeStruct(p.shape, F32) for p in parts[1:]] * 4
    vmem = pl.BlockSpec(memory_space=pltpu.VMEM)
    outs = pl.pallas_call(
        body, name="small_sum_adam", out_shape=tuple(shapes),
        in_specs=[pl.BlockSpec(memory_space=pltpu.SMEM)] + [vmem] * len(args), out_specs=(vmem,) * len(shapes),
        compiler_params=pltpu.CompilerParams(vmem_limit_bytes=VMEM_LIMIT))(block, *args)
    groups = [dict(zip(SMALL, outs[1 + i * ns:1 + (i + 1) * ns])) for i in range(4)]
    return (outs[0], *groups)


def _rope_tables(S):
    half = HD // 2
    inv_freq = jnp.tile(ROPE_THETA ** (-jnp.arange(half, dtype=F32) / half), 4)
    sign = jnp.tile(jnp.concatenate([-jnp.ones(half, F32), jnp.ones(half, F32)]), 2)
    ang = jnp.arange(S, dtype=F32)[:, None] * inv_freq[None, :]
    return jnp.cos(ang), jnp.sin(ang) * sign[None, :]


def _to_blocks(g, col_sharded):
    if col_sharded:
        return g.reshape(g.shape[0], 8, g.shape[1] // 8).transpose(1, 0, 2)
    return g.reshape(8, g.shape[0] // 8, g.shape[1])


def _from_blocks(t, col_sharded):
    if col_sharded:
        return t.transpose(1, 0, 2).reshape(t.shape[1], 8 * t.shape[2])
    return t.reshape(8 * t.shape[1], t.shape[2])


class _NoComm:
    def __init__(self, late_weights):
        self._late = late_weights
        self.grads = {}

    def start_tie(self):
        return jnp.zeros((8, 128), F32)

    def late_weights(self, after):
        return self._late

    def small_start(self, loss_row, grads):
        self.small = (loss_row, grads)
        return jnp.zeros((8, 128), F32)

    def rs_start(self, key, gblocks, after=()):
        self.grads[key] = gblocks
        return jnp.zeros((8, 128), F32)

    def rs_mid(self, key, after):
        return jnp.zeros((8, 128), F32)


class _FsdpComm:
    def __init__(self, late_shards, col_sharded, after, idx, block):
        self._col, self._idx, self._block, self._rs = col_sharded, idx, block, {}
        n = len(late_shards)
        self._gather, self._token = _exchange_start(
            late_shards, [(8,) + s.shape for s in late_shards], _plan_gather(n), 7 * n, "ag_late_start", (after,))

    def start_tie(self):
        return self._token

    def late_weights(self, after):
        shards, lands = _exchange_wait(self._gather, _plan_gather(len(self._col)), "ag_late_wait", after)
        lands = [lax.dynamic_update_index_in_dim(t, s, self._block, 0) for t, s in zip(lands, shards)]
        return [_from_blocks(t, cs) for t, cs in zip(lands, self._col)]

    def small_start(self, loss_row, grads):
        parts = [loss_row] + [grads[k] for k in SMALL]
        n = len(parts)
        self._small, token = _exchange_start(parts, [(8,) + p.shape for p in parts], _plan_gather(n), 7 * n,
                                             "small_start", (self._token,))
        return token

    def small_finish(self, ws, ms, vs, after):
        n = 1 + len(SMALL)
        parts, lands = _exchange_wait(self._small, _plan_gather(n), "small_wait", after)
        return _small_sum_adam(jnp.reshape(self._block, (1,)).astype(jnp.int32), parts, lands, ws, ms, vs)

    def rs_start(self, key, gblocks, after=()):
        n = len(gblocks)
        d2d, token = _exchange_start(gblocks, [(4,) + g.shape[1:] for g in gblocks], _plan_d2d(n), 4 * n,
                                     "rs_%s_d2d_start" % key, (self._token,) + tuple(after))
        self._rs[key] = dict(n=n, d2d=d2d)
        return token

    def rs_mid(self, key, after):
        st = self._rs[key]
        gblocks, from_sib = _exchange_wait(st["d2d"], _plan_d2d(st["n"]), "rs_%s_d2d_wait" % key, after)
        halves = [_rs_add(g, r, self._idx, "rs_add_%s_%d" % (key, k)) for k, (g, r) in enumerate(zip(gblocks, from_sib))]
        st["own"] = [own for own, _ in halves]
        sends = [send for _, send in halves]
        st["ici"], token = _exchange_start(sends, [t.shape for t in sends], _plan_ici(st["n"]), 3 * st["n"],
                                           "rs_%s_ici_start" % key, (self._token,))
        return token

    def rs_end(self, key, after):
        st = self._rs[key]
        return st["own"], _exchange_wait(st["ici"], _plan_ici(st["n"]), "rs_%s_ici_wait" % key, after)[1]


def _local_step(x, tgt, h1, cos_t, sin_t, win_t, comm, g1, g2, g3, g4, ln_g, ln_b, w_sp, b_sp):
    b_sp_t = b_sp.T
    w_sp_t = w_sp.transpose(0, 2, 1)

    qkvs = [_proj_qkv(h1, win_t, cos_t, sin_t, g, d, "proj_qkv_d%d" % d, comm.start_tie()) for g, d in enumerate(DILS)]
    puz = _proj(h1, win_t, W_UZ0, W_G0 - W_UZ0, "none", "proj_uz", 1024)
    gates = _proj(h1, win_t, W_G0, 2 * D, "sigmoid", "proj_gates", 512)
    fwd = [_attn_fwd(t, "attn_fwd_d%d" % d) for t, d in zip(qkvs, DILS)]
    ya, *lses = _attn_combine([o for o, _ in fwd], [l for _, l in fwd])
    yg = _gmlp_fwd(puz, ln_g, ln_b, w_sp, b_sp_t)
    wba, wbg, wout, wmi, wmo = comm.late_weights(after=(ya, yg, gates))
    merged, y, x2, h3 = _merge_fwd(ya, yg, gates, wba, wbg, wout, x, g2, g3)
    a, dy2, dout, loss_row, dg4 = _mlp_fwd(h3, wmi, wmo, x2, tgt, g4)

    dpre, dx2, dy, dg3, dg2 = _mlp_bwd(dy2, a, wmo, wmi, x2, y, dout, g2, g3)
    dwmo = _mm_tn(a, dy2, "dw_mlp_out", square_a=True)
    dwmi = _mm_tn(h3, dpre, "dw_mlp_in", col_blocks=8)
    tie = comm.rs_start("mlp", [dwmi, _to_blocks(dwmo, False)])
    dgates, da, db, dyg, *rest = _outproj_bwd(dy, wout, ya, yg, gates, wba, wbg, tie)
    dyas, dsums = rest[:3], rest[3:]
    tie = comm.rs_mid("mlp", after=(dyg,))
    dpuz, dwsp, dbs, dlng, dlnb = _gmlp_bwd(puz, dyg, ln_g + tie[0, 0], ln_b, w_sp, w_sp_t, b_sp_t)
    dqkvs = [_attn_bwd(qkvs[g], dyas[g], dsums[g], lses[g], "attn_bwd_d%d" % d) for g, d in enumerate(DILS)]
    grad_x, dg1, dqkv = _inproj_bwd(dqkvs, dpuz, dgates, win_t, cos_t, sin_t, x, dx2, g1)
    small = dict(g1=dg1, g2=dg2, g3=dg3, g4=dg4, ln_g=dlng, ln_b=dlnb, b_sp=dbs.reshape(4, CHUNK),
                 w_sp=dwsp.reshape(4 * CHUNK, CHUNK))
    tie = comm.small_start(loss_row, small)
    dwin_t = jnp.concatenate([_mm_tn(t, h1, "dw_in_%d" % n, tie=tie) for n, t in enumerate((dqkv, dpuz, dgates))],
                             axis=0)
    tie = comm.rs_start("win", [_to_blocks(dwin_t, False)])
    dwout = _mm_tn(merged, dy, "dw_out", tie=tie)
    dwba = _mm_tn(ya, da, "dw_branch_attn", tie=tie)
    dwbg = _mm_tn(yg, db, "dw_branch_gmlp", tie=tie)
    tie = comm.rs_mid("win", after=(dwout, dwba, dwbg))
    tie = comm.rs_start("mid", [_to_blocks(dwba, True), _to_blocks(dwbg, True), _to_blocks(dwout, False)], after=(tie,))
    return grad_x, tie


def kernel(x, norm_pre_mix, w_in, w_spatial, b_spatial, ln_v_gain, ln_v_bias, w_branch_attn, w_branch_gmlp, w_out, norm_post_mix, norm_pre_mlp, w_mlp_in, w_mlp_out, norm_post_mlp, loss_target, m_norm_pre_mix, m_w_in, m_w_spatial, m_b_spatial, m_ln_v_gain, m_ln_v_bias, m_w_branch_attn, m_w_branch_gmlp, m_w_out, m_norm_post_mix, m_norm_pre_mlp, m_w_mlp_in, m_w_mlp_out, m_norm_post_mlp, v_norm_pre_mix, v_w_in, v_w_spatial, v_b_spatial, v_ln_v_gain, v_ln_v_bias, v_w_branch_attn, v_w_branch_gmlp, v_w_out, v_norm_post_mix, v_norm_pre_mlp, v_w_mlp_in, v_w_mlp_out, v_norm_post_mlp):
    mx, my, mc = _mesh_pos()
    rel = [(0, 0), (1, 0), (0, 1), (1, 1)]
    chip_of = [2 * (mx ^ dx) + (my ^ dy) for dx, dy in rel]
    idx = jnp.stack([2 * k + mc for k in chip_of] + chip_of).astype(jnp.int32)

    w_in_t, m_w_in_t, v_w_in_t = (t[0].T for t in (w_in, m_w_in, v_w_in))
    shard = w_in_t.astype(BF)
    gather, token = _exchange_start([shard], [(8,) + shard.shape], _plan_gather_near, 4, "ag_win_start", ())
    h1 = _rms_fwd(x[0], norm_pre_mix + token[0, 0], "rms_pre_mix")
    cos_t, sin_t = _rope_tables(x.shape[1])
    gather = _exchange_forward(gather, _plan_gather_near, _plan_gather_pass(True), 3, "ag_win_pass", (h1, cos_t, sin_t))
    (shard,), (g_win,) = _exchange_wait(gather, _plan_gather_pass(False), "ag_win_wait", ())
    g_win = lax.dynamic_update_index_in_dim(g_win, shard, 4 * mx + 2 * my + mc, 0)
    win_t = _from_blocks(g_win, False)
    late = [w_branch_attn[0], w_branch_gmlp[0], w_out[0], w_mlp_in[0], w_mlp_out[0]]
    comm = _FsdpComm([w.astype(BF) for w in late], [True, True, False, True, False], g_win, idx, 4 * mx + 2 * my + mc)

    grad_x, tie = _local_step(
        x[0], loss_target[0], h1, cos_t, sin_t, win_t, comm,
        norm_pre_mix, norm_post_mix, norm_pre_mlp, norm_post_mlp, ln_v_gain, ln_v_bias, w_spatial[0], b_spatial[0])

    flat = lambda t: t.reshape(-1, t.shape[-1])
    small_w = dict(zip(SMALL, map(flat, (norm_pre_mix, norm_post_mix, norm_pre_mlp, norm_post_mlp, ln_v_gain, ln_v_bias,
                                         b_spatial, w_spatial))))
    small_m = dict(zip(SMALL, map(flat, (m_norm_pre_mix, m_norm_post_mix, m_norm_pre_mlp, m_norm_post_mlp, m_ln_v_gain,
                                         m_ln_v_bias, m_b_spatial, m_w_spatial))))
    small_v = dict(zip(SMALL, map(flat, (v_norm_pre_mix, v_norm_post_mix, v_norm_pre_mlp, v_norm_post_mlp, v_ln_v_gain,
                                         v_ln_v_bias, v_b_spatial, v_w_spatial))))
    tie = comm.rs_mid("mid", after=(tie,))
    mlp_own, mlp_chips = comm.rs_end("mlp", after=(tie,))
    adam = lambda nm, own, r, w, m, v: _adam_shard(own, r, w[0], m[0], v[0], "adam_" + nm)
    upd = {
        "w_mlp_in": adam("w_mlp_in", mlp_own[0], mlp_chips[0], w_mlp_in, m_w_mlp_in, v_w_mlp_in),
        "w_mlp_out": adam("w_mlp_out", mlp_own[1], mlp_chips[1], w_mlp_out, m_w_mlp_out, v_w_mlp_out),
    }
    loss_out, sg, sd, sm, sv = comm.small_finish(small_w, small_m, small_v, after=(upd["w_mlp_in"][0], upd["w_mlp_out"][0]))
    loss = loss_out[0, 0]
    win_own, win_chips = comm.rs_end("win", after=(loss_out,))
    upd["w_in"] = tuple(t.T for t in _adam_shard(win_own[0], win_chips[0], w_in_t, m_w_in_t, v_w_in_t, "adam_w_in"))
    mid_own, mid_chips = comm.rs_end("mid", after=(upd["w_in"][0],))
    upd["w_branch_attn"] = adam("w_branch_attn", mid_own[0], mid_chips[0], w_branch_attn, m_w_branch_attn, v_w_branch_attn)
    upd["w_branch_gmlp"] = adam("w_branch_gmlp", mid_own[1], mid_chips[1], w_branch_gmlp, m_w_branch_gmlp, v_w_branch_gmlp)
    upd["w_out"] = adam("w_out", mid_own[2], mid_chips[2], w_out, m_w_out, v_w_out)
    order = ["g1", "w_in", "w_sp", "b_sp", "ln_g", "ln_b", "w_branch_attn", "w_branch_gmlp", "w_out", "g2", "g3",
             "w_mlp_in", "w_mlp_out", "g4"]
    small_shape = dict(g1=norm_pre_mix.shape, g2=norm_post_mix.shape, g3=norm_pre_mlp.shape, g4=norm_post_mlp.shape,
                       ln_g=ln_v_gain.shape, ln_b=ln_v_bias.shape, b_sp=b_spatial.shape, w_sp=w_spatial.shape)

    def pick(which):
        return [upd[nm][which][None] if nm in upd else (sg, sd, sm, sv)[which][nm].reshape(small_shape[nm])
                for nm in order]

    return (loss, grad_x[None], *pick(0), *pick(1), *pick(2), *pick(3))
```

```python
import functools
import math

import jax
import jax.numpy as jnp
from jax import lax
from jax.experimental import pallas as pl
from jax.experimental.pallas import tpu as pltpu

D = 1024
HD = 64
NSLOT = 4
GW = NSLOT * HD
DILS = (1, 4, 16)
QB = 128
ATTN_NSUB = 4
PROJ_QKV_TM = 1024
MLP_TM_FWD = 512
MLP_TM_BWD = 256
AW = 3 * GW
W_UZ0, W_G0 = 3 * AW, 3 * AW + 1024
GMLP_W = 512
CHUNK = 128
DFF = 4096
EPS = 1e-6
ROPE_THETA = 10000.0
SCALE = HD ** -0.5
NEG = -1e30

ADAM_LR = 0.001
ADAM_B1 = 0.9
ADAM_B2 = 0.999
ADAM_EPS = 1e-08
ADAM_WD = 0.01
ADAM_STEP = 10

BF = jnp.bfloat16
F32 = jnp.float32
MESH = pl.DeviceIdType.MESH
VMEM_LIMIT = 56 * 1024 * 1024


def _cp(sem):
    return pltpu.CompilerParams(dimension_semantics=sem, vmem_limit_bytes=VMEM_LIMIT)


def _dot(a, b):
    return jnp.dot(a, b, preferred_element_type=F32)


def _dot_nt(a, b):
    return lax.dot_general(a, b, (((1,), (1,)), ((), ())), preferred_element_type=F32)


def _dot_tn(a, b):
    return lax.dot_general(a, b, (((0,), (0,)), ((), ())), preferred_element_type=F32)


def _gelu(x):
    return jax.nn.gelu(x, approximate=True)


def _gelu_grad(x):
    k = math.sqrt(2.0 / math.pi)
    t = jnp.tanh(k * (x + 0.044715 * x * x * x))
    return 0.5 * (1.0 + t) + 0.5 * x * (1.0 - t * t) * (k * (1.0 + 3.0 * 0.044715 * x * x))


def _swap_halves(t):
    w = t.shape[1]
    lane = lax.broadcasted_iota(jnp.int32, t.shape, 1)
    first = (lane & (HD - 1)) < (HD // 2)
    return jnp.where(first, pltpu.roll(t, w - HD // 2, 1), pltpu.roll(t, HD // 2, 1))


def _head_mask(hh):
    lane = lax.broadcasted_iota(jnp.int32, (1, GW), 1)
    return jnp.logical_and(lane >= hh * HD, lane < (hh + 1) * HD)


def _rms_stats(xf):
    return lax.rsqrt(jnp.mean(xf * xf, axis=-1, keepdims=True) + EPS)


def _rms_bwd(xf, r, gain, dout):
    n = xf * r
    t = dout * gain
    dx = r * (t - n * jnp.mean(t * n, axis=-1, keepdims=True))
    return dx, jnp.sum(dout * n, axis=0, keepdims=True)


def _rms_fwd(x, gain, name):
    S = x.shape[0]
    tm = min(512, S)

    def body(x_ref, g_ref, h_ref):
        xf = x_ref[...]
        h_ref[...] = (xf * _rms_stats(xf) * g_ref[...]).astype(BF)

    return pl.pallas_call(
        body, name=name, out_shape=jax.ShapeDtypeStruct((S, D), BF), grid=(S // tm,),
        in_specs=[pl.BlockSpec((tm, D), lambda i: (i, 0)), pl.BlockSpec((1, D), lambda i: (0, 0))],
        out_specs=pl.BlockSpec((tm, D), lambda i: (i, 0)), compiler_params=_cp(("parallel",)))(x, gain)


def _proj(h, w_t, row0, N, epi, name, tm):
    S, K = h.shape
    tm = min(tm, S)
    nblk = N // GW

    def body(h_ref, *refs):
        hv = h_ref[...]
        o_ref = refs[-1]
        for b in range(nblk):
            y = _dot_nt(hv, refs[b][...])
            if epi == "sigmoid":
                y = jax.nn.sigmoid(y)
            o_ref[:, b * GW:(b + 1) * GW] = y.astype(BF)

    blocks = [pl.BlockSpec((GW, K), functools.partial(lambda i, b: (b, 0), b=row0 // GW + b),
                           pipeline_mode=pl.Buffered(1)) for b in range(nblk)]
    return pl.pallas_call(
        body, name=name, out_shape=jax.ShapeDtypeStruct((S, N), BF), grid=(S // tm,),
        in_specs=[pl.BlockSpec((tm, K), lambda i: (i, 0))] + blocks,
        out_specs=pl.BlockSpec((tm, N), lambda i: (i, 0)),
        compiler_params=_cp(("parallel",)))(h, *([w_t] * nblk))


def _dilate_store(val, scr, o_ref, lead, d):
    rows = val.shape[0]
    if d == 1:
        o_ref[lead + (0,)] = val.astype(o_ref.dtype)
        return
    for hf in range(2):
        scr[hf, pl.ds(0, rows), :] = val[:, hf * 128:(hf + 1) * 128]
    for r in range(d):
        for hf in range(2):
            o_ref[lead + (r, slice(None), slice(hf * 128, (hf + 1) * 128))] = (
                scr[hf, pl.ds(r, rows // d, stride=d), :].astype(o_ref.dtype))


def _undilate_load(i_ref, lead, d, scr, rows):
    if d == 1:
        return i_ref[lead + (0,)].astype(F32)
    for r in range(d):
        for hf in range(2):
            scr[hf, pl.ds(r, rows // d, stride=d), :] = (
                i_ref[lead + (r, slice(None), slice(hf * 128, (hf + 1) * 128))].astype(F32))
    return jnp.concatenate([scr[0, pl.ds(0, rows), :], scr[1, pl.ds(0, rows), :]], axis=1)


def _rope_fwd(y, c_ref, s_ref):
    cosv = jnp.concatenate([c_ref[...]] * 2, axis=1)
    sinv = jnp.concatenate([s_ref[...]] * 2, axis=1)
    return y * cosv + _swap_halves(y) * sinv


def _rope_bwd(dy, c_ref, s_ref):
    cosv = jnp.concatenate([c_ref[...]] * 2, axis=1)
    sinv = jnp.concatenate([s_ref[...]] * 2, axis=1)
    return dy * cosv + _swap_halves(dy * sinv)


def _proj_qkv(h, wqkv_t, cos_t, sin_t, g, d, name, tie):
    S, K = h.shape
    tm = min(PROJ_QKV_TM, S)

    def body(h_ref, wq_ref, wk_ref, wv_ref, c_ref, s_ref, tie_ref, o_ref, scr):
        hv = h_ref[...]
        ys = [_dot_nt(hv, w_ref[...]) for w_ref in (wq_ref, wk_ref, wv_ref)]
        for t in range(3):
            y = _rope_fwd(ys[t], c_ref, s_ref) if t < 2 else ys[t]
            _dilate_store(y, scr.at[t], o_ref, (t,), d)

    w_spec = lambda t: pl.BlockSpec((GW, K), lambda i: (3 * t + g, 0))
    return pl.pallas_call(
        body, name=name, out_shape=jax.ShapeDtypeStruct((3, d, S // d, GW), BF), grid=(S // tm,),
        in_specs=[pl.BlockSpec((tm, K), lambda i: (i, 0)), w_spec(0), w_spec(1), w_spec(2),
                  pl.BlockSpec((tm, 128), lambda i: (i, 0)), pl.BlockSpec((tm, 128), lambda i: (i, 0)),
                  pl.BlockSpec(memory_space=pl.ANY)],
        out_specs=pl.BlockSpec((3, d, tm // d, GW), lambda i: (0, 0, i, 0)),
        scratch_shapes=[pltpu.VMEM((3, 2, tm, 128), F32)],
        compiler_params=_cp(("parallel",)))(h, wqkv_t, wqkv_t, wqkv_t, cos_t, sin_t, tie)


def _band_masks(first_step):
    row = lax.broadcasted_iota(jnp.int32, (QB, 2 * QB), 0)
    col = lax.broadcasted_iota(jnp.int32, (QB, 2 * QB), 1)
    band = jnp.logical_and(col >= row, col <= row + QB)
    return band, jnp.logical_and(band, jnp.logical_or(col >= QB, jnp.logical_not(first_step)))


def _attn_fwd(qkv, name):
    _, d, L, _ = qkv.shape
    nsub = min(ATTN_NSUB, L // QB)
    R = nsub * QB
    nsteps = L // R

    def body(q_ref, kp_ref, kc_ref, vp_ref, vc_ref, o_ref, lse_ref):
        i = pl.program_id(1)
        band, band_first = _band_masks(i == 0)
        kfull = jnp.concatenate([kp_ref[...], kc_ref[...]], axis=0)
        vfull = jnp.concatenate([vp_ref[...], vc_ref[...]], axis=0)
        chains = [(sb, hh) for sb in range(nsub) for hh in range(NSLOT)]
        win = lambda t, sb: t[sb * QB:(sb + 2) * QB]
        scores = []
        for sb, hh in chains:
            qh = jnp.where(_head_mask(hh), q_ref[sb * QB:(sb + 1) * QB, :], 0)
            scores.append(_dot_nt(qh, win(kfull, sb)))
        soft = []
        for (sb, hh), sc in zip(chains, scores):
            sc = jnp.where(band_first if sb == 0 else band, sc * SCALE, NEG)
            m = jnp.max(sc, axis=1, keepdims=True)
            p = jnp.exp(sc - m)
            den = jnp.sum(p, axis=1, keepdims=True)
            soft.append((p.astype(BF), den, m + jnp.log(den)))
        accs = [_dot(p, win(vfull, sb)) for (sb, hh), (p, _, _) in zip(chains, soft)]
        for sb in range(nsub):
            o = jnp.zeros((QB, GW), F32)
            lse = jnp.zeros((QB, GW), F32)
            for hh in range(NSLOT):
                hm = _head_mask(hh)
                _, den, lrow = soft[sb * NSLOT + hh]
                o = o + jnp.where(hm, accs[sb * NSLOT + hh] / den, 0.0)
                lse = lse + jnp.where(hm, lrow, 0.0)
            o_ref[sb * QB:(sb + 1) * QB, :] = o
            lse_ref[sb * QB:(sb + 1) * QB, :] = lse

    prev = lambda i: jnp.maximum(i * nsub - 1, 0)
    cur = lambda t: pl.BlockSpec((None, None, R, GW), lambda r, i: (t, r, i, 0))
    prv = lambda t: pl.BlockSpec((None, None, QB, GW), lambda r, i: (t, r, prev(i), 0))
    out = pl.BlockSpec((None, R, GW), lambda r, i: (r, i, 0))
    return pl.pallas_call(
        body, name=name, grid=(d, nsteps),
        out_shape=(jax.ShapeDtypeStruct((d, L, GW), F32), jax.ShapeDtypeStruct((d, L, GW), F32)),
        in_specs=[cur(0), prv(1), cur(1), prv(2), cur(2)],
        out_specs=(out, out), compiler_params=_cp(("parallel", "arbitrary")))(qkv, qkv, qkv, qkv, qkv)


def _attn_combine(os_, lses):
    S = os_[0].shape[1]
    tm = min(1024, S)

    def body(o0, o1, o2, l0, l1, l2, y_ref, j0, j1, j2, scr):
        os_nat = [_undilate_load(o, (), d, scr, tm) for o, d in zip((o0, o1, o2), DILS)]
        a, b, c = [_undilate_load(l, (), d, scr, tm) for l, d in zip((l0, l1, l2), DILS)]
        m = jnp.maximum(jnp.maximum(a, b), c)
        wa, wb, wc = jnp.exp(a - m), jnp.exp(b - m), jnp.exp(c - m)
        den = wa + wb + wc
        y_ref[...] = ((wa * os_nat[0] + wb * os_nat[1] + wc * os_nat[2]) / den).astype(BF)
        lse = m + jnp.log(den)
        for j_ref, d in zip((j0, j1, j2), DILS):
            _dilate_store(lse, scr, j_ref, (), d)

    dil = lambda d: pl.BlockSpec((d, tm // d, GW), lambda i: (0, i, 0))
    dshape = lambda d: jax.ShapeDtypeStruct((d, S // d, GW), F32)
    return pl.pallas_call(
        body, name="attn_combine", grid=(S // tm,),
        out_shape=(jax.ShapeDtypeStruct((S, GW), BF),) + tuple(dshape(d) for d in DILS),
        in_specs=[dil(d) for d in DILS] * 2,
        out_specs=(pl.BlockSpec((tm, GW), lambda i: (i, 0)),) + tuple(dil(d) for d in DILS),
        scratch_shapes=[pltpu.VMEM((2, tm, 128), F32)],
        compiler_params=_cp(("parallel",)))(*os_, *lses)


def _tril(upper=False):
    row = lax.broadcasted_iota(jnp.int32, (CHUNK, CHUNK), 0)
    col = lax.broadcasted_iota(jnp.int32, (CHUNK, CHUNK), 1)
    return row <= col if upper else col <= row


def _ln_fwd(z, gain, bias):
    mu = jnp.mean(z, axis=-1, keepdims=True)
    zc = z - mu
    rstd = lax.rsqrt(jnp.mean(zc * zc, axis=-1, keepdims=True) + EPS)
    zhat = zc * rstd
    return zhat, rstd, zhat * gain + bias


def _gmlp_fwd(puz, ln_g, ln_b, w_sp, b_sp_t):
    S = puz.shape[0]
    tm = min(512, S)
    nch = tm // CHUNK

    def body(p_ref, g_ref, b_ref, w_ref, bt_ref, o_ref):
        tril = _tril()
        ws = [jnp.where(tril, w_ref[gg], 0.0).astype(BF) for gg in range(4)]
        for ch in range(nch):
            rows = slice(ch * CHUNK, (ch + 1) * CHUNK)
            z = _gelu(p_ref[rows, GMLP_W:].astype(F32))
            _, _, zn = _ln_fwd(z, g_ref[...], b_ref[...])
            zn = zn.astype(BF)
            for gg in range(4):
                cols = slice(gg * CHUNK, (gg + 1) * CHUNK)
                sz = _dot(ws[gg], zn[:, cols]) + bt_ref[:, gg:gg + 1]
                u = _gelu(p_ref[rows, cols].astype(F32))
                o_ref[rows, cols] = (u * sz).astype(BF)

    return pl.pallas_call(
        body, name="gmlp_fwd", out_shape=jax.ShapeDtypeStruct((S, GMLP_W), BF), grid=(S // tm,),
        in_specs=[pl.BlockSpec((tm, 2 * GMLP_W), lambda i: (i, 0)),
                  pl.BlockSpec((1, GMLP_W), lambda i: (0, 0)), pl.BlockSpec((1, GMLP_W), lambda i: (0, 0)),
                  pl.BlockSpec((4, CHUNK, CHUNK), lambda i: (0, 0, 0)), pl.BlockSpec((CHUNK, 4), lambda i: (0, 0))],
        out_specs=pl.BlockSpec((tm, GMLP_W), lambda i: (i, 0)),
        compiler_params=_cp(("parallel",)))(puz, ln_g, ln_b, w_sp, b_sp_t)


def _merge_fwd(ya, yg, gates, wba, wbg, wout, x, g2, g3):
    S = x.shape[0]
    tm = min(512, S)

    def body(ya_ref, yg_ref, gt_ref, wba_ref, wbg_ref, wo_ref, x_ref, g2_ref, g3_ref,
             mg_ref, y_ref, x2_ref, h3_ref):
        a = _dot(ya_ref[...], wba_ref[...])
        b = _dot(yg_ref[...], wbg_ref[...])
        merged = (gt_ref[:, :D].astype(F32) * a + gt_ref[:, D:].astype(F32) * b).astype(BF)
        mg_ref[...] = merged
        y = _dot(merged, wo_ref[...])
        y_ref[...] = y
        x2 = x_ref[...] + y * _rms_stats(y) * g2_ref[...]
        x2_ref[...] = x2
        h3_ref[...] = (x2 * _rms_stats(x2) * g3_ref[...]).astype(BF)

    row = lambda w: pl.BlockSpec((tm, w), lambda i: (i, 0))
    full = lambda s: pl.BlockSpec(s, lambda i: (0, 0))
    return pl.pallas_call(
        body, name="merge_fwd", grid=(S // tm,),
        out_shape=(jax.ShapeDtypeStruct((S, D), BF), jax.ShapeDtypeStruct((S, D), F32), jax.ShapeDtypeStruct((S, D), F32),
                   jax.ShapeDtypeStruct((S, D), BF)),
        in_specs=[row(GW), row(GMLP_W), row(2 * D), full((GW, D)), full((GMLP_W, D)), full((D, D)), row(D),
                  full((1, D)), full((1, D))],
        out_specs=(row(D), row(D), row(D), row(D)),
        compiler_params=_cp(("parallel",)))(ya, yg, gates, wba, wbg, wout, x, g2, g3)


def _resident(shape):
    return pl.BlockSpec(shape, lambda i: (0,) * len(shape), pipeline_mode=pl.Buffered(1))


def _mlp_fwd(h3, wmi, wmo, x2, tgt, g4):
    S = x2.shape[0]
    tm = min(MLP_TM_FWD, S)

    def body(h_ref, wi_ref, wo_ref, x2_ref, t_ref, g4_ref, a_ref, dy2_ref, dout_ref, loss_ref, dg4_ref):
        @pl.when(pl.program_id(0) == 0)
        def _():
            loss_ref[...] = jnp.zeros_like(loss_ref)
            dg4_ref[...] = jnp.zeros_like(dg4_ref)

        halves = [slice(hh * (tm // 2), (hh + 1) * (tm // 2)) for hh in range(2)]
        acts = []
        for rows in halves:
            a = jnp.maximum(_dot(h_ref[rows, :], wi_ref[...]), 0.0)
            a_ref[rows, :] = a.astype(BF)
            acts.append((a * a).astype(BF))
        y2s = [_dot(a2, wo_ref[...]) for a2 in acts]
        lane = lax.broadcasted_iota(jnp.int32, (1, 128), 1)
        for rows, y2 in zip(halves, y2s):
            r = _rms_stats(y2)
            out = x2_ref[rows, :] + y2 * r * g4_ref[...]
            err = out - t_ref[rows, :]
            tot = jnp.sum(jnp.sum(err * err, axis=1, keepdims=True), axis=0, keepdims=True) * (0.5 / D)
            loss_ref[...] += jnp.where(lane == 0, tot, 0.0)
            dout = err * (1.0 / D)
            dout_ref[rows, :] = dout
            dy2, dg = _rms_bwd(y2, r, g4_ref[...], dout)
            dy2_ref[rows, :] = dy2.astype(BF)
            dg4_ref[...] += dg

    row = pl.BlockSpec((tm, D), lambda i: (i, 0))
    return pl.pallas_call(
        body, name="mlp_fwd", grid=(S // tm,),
        out_shape=(jax.ShapeDtypeStruct((S, DFF), BF), jax.ShapeDtypeStruct((S, D), BF), jax.ShapeDtypeStruct((S, D), F32),
                   jax.ShapeDtypeStruct((1, 128), F32), jax.ShapeDtypeStruct((1, D), F32)),
        in_specs=[row, _resident((D, DFF)), _resident((DFF, D)), row, row, pl.BlockSpec((1, D), lambda i: (0, 0))],
        out_specs=(pl.BlockSpec((tm, DFF), lambda i: (i, 0)), row, row,
                   pl.BlockSpec((1, 128), lambda i: (0, 0)), pl.BlockSpec((1, D), lambda i: (0, 0))),
        compiler_params=_cp(("arbitrary",)))(h3, wmi, wmo, x2, tgt, g4)


def _mlp_bwd(dy2, a, wmo, wmi, x2, y, dout, g2, g3):
    S = x2.shape[0]
    tm = min(MLP_TM_BWD, S)

    def body(dy2_ref, a_ref, wo_ref, wi_ref, x2_ref, y_ref, dout_ref, g2_ref, g3_ref,
             dpre_ref, dx2_ref, dy_ref, dg3_ref, dg2_ref):
        @pl.when(pl.program_id(0) == 0)
        def _():
            dg3_ref[...] = jnp.zeros_like(dg3_ref)
            dg2_ref[...] = jnp.zeros_like(dg2_ref)

        da2 = _dot_nt(dy2_ref[...], wo_ref[...])
        dpre = (2.0 * a_ref[...].astype(F32) * da2).astype(BF)
        dpre_ref[...] = dpre
        dh3 = _dot_nt(dpre, wi_ref[...])
        x2 = x2_ref[...]
        dx3, dg3 = _rms_bwd(x2, _rms_stats(x2), g3_ref[...], dh3)
        dx2 = dout_ref[...] + dx3
        dx2_ref[...] = dx2
        dg3_ref[...] += dg3
        yv = y_ref[...]
        dy, dg2 = _rms_bwd(yv, _rms_stats(yv), g2_ref[...], dx2)
        dy_ref[...] = dy.astype(BF)
        dg2_ref[...] += dg2

    row = pl.BlockSpec((tm, D), lambda i: (i, 0))
    wide = pl.BlockSpec((tm, DFF), lambda i: (i, 0))
    vec = pl.BlockSpec((1, D), lambda i: (0, 0))
    return pl.pallas_call(
        body, name="mlp_bwd", grid=(S // tm,),
        out_shape=(jax.ShapeDtypeStruct((S, DFF), BF), jax.ShapeDtypeStruct((S, D), F32), jax.ShapeDtypeStruct((S, D), BF),
                   jax.ShapeDtypeStruct((1, D), F32), jax.ShapeDtypeStruct((1, D), F32)),
        in_specs=[row, wide, _resident((DFF, D)), _resident((D, DFF)), row, row, row, vec, vec],
        out_specs=(wide, row, row, vec, vec),
        compiler_params=_cp(("arbitrary",)))(dy2, a, wmo, wmi, x2, y, dout, g2, g3)


def _mm_tn(a, b, name, square_a=False, tm=1024, tn=1024, tk=2048, tie=None, col_blocks=None):
    S, M = a.shape
    N = b.shape[1]
    tk = min(tk, S)
    tm = max(t for t in range(128, min(tm, M) + 1, 128) if M % t == 0)
    tn = max(t for t in range(128, min(tn, N) + 1, 128) if N % t == 0)
    cb = N // col_blocks if col_blocks else tn
    assert M % tm == 0 and S % tk == 0 and tn % cb == 0
    nk = S // tk
    ties = () if tie is None else (tie,)

    def body(a_ref, b_ref, *rest):
        o_ref = rest[-1]
        k = pl.program_id(2)
        av = a_ref[...]
        if square_a:
            av = av * av
        part = _dot_tn(av, b_ref[...])
        if col_blocks:
            part = jnp.stack([part[:, t * cb:(t + 1) * cb] for t in range(tn // cb)])

        @pl.when(k == 0)
        def _():
            o_ref[...] = part

        @pl.when(k > 0)
        def _():
            o_ref[...] += part

    if col_blocks:
        out_shape, out_spec = (col_blocks, M, cb), pl.BlockSpec((tn // cb, tm, cb), lambda i, j, k: (j, i, 0))
    else:
        out_shape, out_spec = (M, N), pl.BlockSpec((tm, tn), lambda i, j, k: (i, j))
    return pl.pallas_call(
        body, name=name, out_shape=jax.ShapeDtypeStruct(out_shape, F32), grid=(M // tm, N // tn, nk),
        in_specs=[pl.BlockSpec((tk, tm), lambda i, j, k: (k, i)), pl.BlockSpec((tk, tn), lambda i, j, k: (k, j))]
        + [pl.BlockSpec(memory_space=pl.ANY)] * len(ties),
        out_specs=out_spec,
        compiler_params=_cp(("parallel", "parallel", "arbitrary")))(a, b, *ties)


def _outproj_bwd(dy, wout, ya, yg, gates, wba, wbg, tie):
    S = dy.shape[0]
    tm = min(512, S)

    def body(dy_ref, wo_ref, ya_ref, yg_ref, gt_ref, wba_ref, wbg_ref, tie_ref,
             dgt_ref, da_ref, db_ref, dyg_ref, e0, e1, e2, s0, s1, s2, scr):
        dm = _dot_nt(dy_ref[...], wo_ref[...])
        ga, gb = gt_ref[:, :D].astype(F32), gt_ref[:, D:].astype(F32)
        dgt_ref[:, :D] = (dm * _dot(ya_ref[...], wba_ref[...]) * ga * (1.0 - ga)).astype(BF)
        dgt_ref[:, D:] = (dm * _dot(yg_ref[...], wbg_ref[...]) * gb * (1.0 - gb)).astype(BF)
        da = (dm * ga).astype(BF)
        db = (dm * gb).astype(BF)
        da_ref[...] = da
        db_ref[...] = db
        dyg_ref[...] = _dot_nt(db, wbg_ref[...]).astype(BF)
        dya = _dot_nt(da, wba_ref[...]).astype(BF).astype(F32)
        dyy = dya * ya_ref[...].astype(F32)
        dsum = jnp.zeros((tm, GW), F32)
        for hh in range(NSLOT):
            hm = _head_mask(hh)
            dsum = dsum + jnp.where(hm, jnp.sum(jnp.where(hm, dyy, 0.0), axis=1, keepdims=True), 0.0)
        for e_ref, s_ref, d in zip((e0, e1, e2), (s0, s1, s2), DILS):
            _dilate_store(dya, scr, e_ref, (), d)
            _dilate_store(dsum, scr, s_ref, (), d)

    row = lambda w: pl.BlockSpec((tm, w), lambda i: (i, 0))
    full = lambda s: pl.BlockSpec(s, lambda i: (0, 0))
    dil = lambda d: pl.BlockSpec((d, tm // d, GW), lambda i: (0, i, 0))
    dshape = lambda d, t: jax.ShapeDtypeStruct((d, S // d, GW), t)
    return pl.pallas_call(
        body, name="outproj_bwd", grid=(S // tm,),
        out_shape=(jax.ShapeDtypeStruct((S, 2 * D), BF), jax.ShapeDtypeStruct((S, D), BF), jax.ShapeDtypeStruct((S, D), BF),
                   jax.ShapeDtypeStruct((S, GMLP_W), BF)) + tuple(dshape(d, BF) for d in DILS)
        + tuple(dshape(d, F32) for d in DILS),
        in_specs=[row(D), full((D, D)), row(GW), row(GMLP_W), row(2 * D), full((GW, D)), full((GMLP_W, D)),
                  pl.BlockSpec(memory_space=pl.ANY)],
        out_specs=(row(2 * D), row(D), row(D), row(GMLP_W)) + tuple(dil(d) for d in DILS) * 2,
        scratch_shapes=[pltpu.VMEM((2, tm, 128), F32)],
        compiler_params=_cp(("parallel",)))(dy, wout, ya, yg, gates, wba, wbg, tie)


def _gmlp_bwd(puz, dyg, ln_g, ln_b, w_sp, w_sp_t, b_sp_t):
    S = puz.shape[0]
    tm = min(512, S)
    nch = tm // CHUNK

    def body(p_ref, dy_ref, g_ref, b_ref, w_ref, wt_ref, bt_ref,
             dp_ref, dw_ref, dbs_ref, dg_ref, dbias_ref, dbacc_ref):
        i = pl.program_id(0)

        @pl.when(i == 0)
        def _():
            dw_ref[...] = jnp.zeros_like(dw_ref)
            dbacc_ref[...] = jnp.zeros_like(dbacc_ref)
            dg_ref[...] = jnp.zeros_like(dg_ref)
            dbias_ref[...] = jnp.zeros_like(dbias_ref)

        tril = _tril()
        ws = [jnp.where(tril, w_ref[gg], 0.0).astype(BF) for gg in range(4)]
        triu = _tril(upper=True)
        wts = [jnp.where(triu, wt_ref[gg], 0.0).astype(BF) for gg in range(4)]
        gain = g_ref[...]
        for ch in range(nch):
            rows = slice(ch * CHUNK, (ch + 1) * CHUNK)
            pz = p_ref[rows, GMLP_W:].astype(F32)
            z = _gelu(pz)
            zhat, rstd, zn = _ln_fwd(z, gain, b_ref[...])
            znb = zn.astype(BF)
            dzn_parts = []
            for gg in range(4):
                cols = slice(gg * CHUNK, (gg + 1) * CHUNK)
                pu = p_ref[rows, cols].astype(F32)
                u = _gelu(pu)
                sz = _dot(ws[gg], znb[:, cols]) + bt_ref[:, gg:gg + 1]
                dyv = dy_ref[rows, cols].astype(F32)
                dp_ref[rows, cols] = (dyv * sz * _gelu_grad(pu)).astype(BF)
                dsz = dyv * u
                dbacc_ref[gg] += dsz
                dszb = dsz.astype(BF)
                dw_ref[gg] += _dot_nt(dszb, znb[:, cols])
                dzn_parts.append(_dot(wts[gg], dszb))
            dzn = jnp.concatenate(dzn_parts, axis=1)
            dg_ref[...] += jnp.sum(dzn * zhat, axis=0, keepdims=True)
            dbias_ref[...] += jnp.sum(dzn, axis=0, keepdims=True)
            dzh = dzn * gain
            dz = rstd * (dzh - jnp.mean(dzh, axis=-1, keepdims=True)
                         - zhat * jnp.mean(dzh * zhat, axis=-1, keepdims=True))
            dp_ref[rows, GMLP_W:] = (dz * _gelu_grad(pz)).astype(BF)

        @pl.when(i == pl.num_programs(0) - 1)
        def _():
            for gg in range(4):
                dw_ref[gg] = jnp.where(tril, dw_ref[gg], 0.0)
                dbs_ref[gg] = jnp.sum(dbacc_ref[gg], axis=1, keepdims=True)

    full2 = lambda s: pl.BlockSpec(s, lambda i: (0, 0))
    full3 = lambda s: pl.BlockSpec(s, lambda i: (0, 0, 0))
    return pl.pallas_call(
        body, name="gmlp_bwd", grid=(S // tm,),
        out_shape=(jax.ShapeDtypeStruct((S, 2 * GMLP_W), BF), jax.ShapeDtypeStruct((4, CHUNK, CHUNK), F32),
                   jax.ShapeDtypeStruct((4, CHUNK, 1), F32), jax.ShapeDtypeStruct((1, GMLP_W), F32),
                   jax.ShapeDtypeStruct((1, GMLP_W), F32)),
        in_specs=[pl.BlockSpec((tm, 2 * GMLP_W), lambda i: (i, 0)), pl.BlockSpec((tm, GMLP_W), lambda i: (i, 0)),
                  full2((1, GMLP_W)), full2((1, GMLP_W)), full3((4, CHUNK, CHUNK)), full3((4, CHUNK, CHUNK)),
                  full2((CHUNK, 4))],
        out_specs=(pl.BlockSpec((tm, 2 * GMLP_W), lambda i: (i, 0)), full3((4, CHUNK, CHUNK)), full3((4, CHUNK, 1)),
                   full2((1, GMLP_W)), full2((1, GMLP_W))),
        scratch_shapes=[pltpu.VMEM((4, CHUNK, CHUNK), F32)],
        compiler_params=_cp(("arbitrary",)))(puz, dyg, ln_g, ln_b, w_sp, w_sp_t, b_sp_t)


def _attn_bwd(qkv, dya, dsums, lse, name):
    _, d, L, _ = qkv.shape
    nsub = min(ATTN_NSUB, L // QB)
    R = nsub * QB
    nsteps = L // R

    def body(q_ref, qn_ref, kp_ref, kc_ref, vp_ref, vc_ref, dy_ref, dyn_ref, e_ref, en_ref, l_ref, ln_ref, o_ref):
        i = pl.program_id(1)
        band, band_first = _band_masks(i == 0)
        row = lax.broadcasted_iota(jnp.int32, (QB, QB), 0)
        col = lax.broadcasted_iota(jnp.int32, (QB, QB), 1)
        mask_next = jnp.logical_and(col >= row, i < nsteps - 1)
        kc, vc = kc_ref[...], vc_ref[...]
        kfull = jnp.concatenate([kp_ref[...], kc], axis=0)
        vfull = jnp.concatenate([vp_ref[...], vc], axis=0)
        k_last, v_last = kc[(nsub - 1) * QB:], vc[(nsub - 1) * QB:]
        q_ext = jnp.concatenate([q_ref[...], qn_ref[...]], axis=0)
        dy_ext = jnp.concatenate([dy_ref[...], dyn_ref[...]], axis=0)
        esum, esum_n, lse, lse_n = e_ref[...], en_ref[...], l_ref[...], ln_ref[...]
        win = lambda t, sb: t[sb * QB:(sb + 2) * QB]
        blk = lambda t, sb: t[sb * QB:(sb + 1) * QB]
        hms = [_head_mask(hh) for hh in range(NSLOT)]
        q_hs = [jnp.where(hm, q_ext, 0) for hm in hms]
        dy_hs = [jnp.where(hm, dy_ext, 0) for hm in hms]
        raw = []
        for hh in range(NSLOT):
            tiles = [(_dot_nt(blk(q_hs[hh], sb), win(kfull, sb)), _dot_nt(blk(dy_hs[hh], sb), win(vfull, sb)))
                     for sb in range(nsub)]
            tiles.append((_dot_nt(blk(q_hs[hh], nsub), k_last), _dot_nt(blk(dy_hs[hh], nsub), v_last)))
            raw.append(tiles)
        ps, dss = [], []
        for hh in range(NSLOT):
            rowstat = lambda t: jnp.max(jnp.where(hms[hh], t, -jnp.inf), axis=1, keepdims=True)
            p_h, ds_h = [], []
            for sb in range(nsub + 1):
                sc, dp = raw[hh][sb]
                if sb < nsub:
                    msk, lrow, erow = (band_first if sb == 0 else band), rowstat(blk(lse, sb)), rowstat(blk(esum, sb))
                else:
                    msk, lrow, erow = mask_next, rowstat(lse_n), rowstat(esum_n)
                p = jnp.where(msk, jnp.exp(sc * SCALE - lrow), 0.0)
                p_h.append(p.astype(BF))
                ds_h.append((p * (dp - erow)).astype(BF))
            ps.append(p_h)
            dss.append(ds_h)
        dq = [jnp.zeros((QB, GW), F32) for _ in range(nsub)]
        dk = [jnp.zeros((QB, GW), F32) for _ in range(nsub)]
        dv = [jnp.zeros((QB, GW), F32) for _ in range(nsub)]
        for hh in range(NSLOT):
            for sb in range(nsub):
                dq[sb] = dq[sb] + jnp.where(hms[hh], _dot(dss[hh][sb], win(kfull, sb)), 0.0)
                nxt = lambda t: t[sb + 1][:, :QB] if sb + 1 < nsub else t[nsub]
                dk[sb] = dk[sb] + _dot_tn(jnp.concatenate([dss[hh][sb][:, QB:], nxt(dss[hh])], axis=0), win(q_hs[hh], sb))
                dv[sb] = dv[sb] + _dot_tn(jnp.concatenate([ps[hh][sb][:, QB:], nxt(ps[hh])], axis=0), win(dy_hs[hh], sb))
        for sb in range(nsub):
            rows = slice(sb * QB, (sb + 1) * QB)
            o_ref[0, rows, :] = (dq[sb] * SCALE).astype(BF)
            o_ref[1, rows, :] = (dk[sb] * SCALE).astype(BF)
            o_ref[2, rows, :] = dv[sb].astype(BF)

    prev = lambda i: jnp.maximum(i * nsub - 1, 0)
    nxt = lambda i: jnp.minimum((i + 1) * nsub, L // QB - 1)
    cur4 = lambda t: pl.BlockSpec((None, None, R, GW), lambda r, i: (t, r, i, 0))
    prv4 = lambda t: pl.BlockSpec((None, None, QB, GW), lambda r, i: (t, r, prev(i), 0))
    nxt4 = lambda t: pl.BlockSpec((None, None, QB, GW), lambda r, i: (t, r, nxt(i), 0))
    cur3 = pl.BlockSpec((None, R, GW), lambda r, i: (r, i, 0))
    nxt3 = pl.BlockSpec((None, QB, GW), lambda r, i: (r, nxt(i), 0))
    return pl.pallas_call(
        body, name=name, grid=(d, nsteps), out_shape=jax.ShapeDtypeStruct((3, d, L, GW), BF),
        in_specs=[cur4(0), nxt4(0), prv4(1), cur4(1), prv4(2), cur4(2), cur3, nxt3, cur3, nxt3, cur3, nxt3],
        out_specs=pl.BlockSpec((3, None, R, GW), lambda r, i: (0, r, i, 0)),
        compiler_params=_cp(("parallel", "arbitrary")))(qkv, qkv, qkv, qkv, qkv, qkv, dya, dya, dsums, dsums, lse, lse)


def _inproj_bwd(dqkvs, dpuz, dgates, win_t, cos_t, sin_t, x, dx2, g1):
    S = x.shape[0]
    tm = min(512, S)

    def body(d0_ref, d1_ref, d2_ref, dp_ref, dg_ref, w_ref, c_ref, s_ref,
             x_ref, dx2_ref, g1_ref, gx_ref, dg1_ref, dn_ref, scr):
        i = pl.program_id(0)

        @pl.when(i == 0)
        def _():
            dg1_ref[...] = jnp.zeros_like(dg1_ref)

        dh = _dot(dp_ref[...], w_ref[W_UZ0:W_G0, :]) + _dot(dg_ref[...], w_ref[W_G0:, :])
        dn_ref[:, W_UZ0:W_G0] = dp_ref[...]
        dn_ref[:, W_G0:] = dg_ref[...]
        for t in range(3):
            for g, (d_ref, d) in enumerate(zip((d0_ref, d1_ref, d2_ref), DILS)):
                piece = _undilate_load(d_ref, (t,), d, scr, tm)
                if t < 2:
                    piece = _rope_bwd(piece, c_ref, s_ref)
                dn_ref[:, (3 * t + g) * GW:(3 * t + g + 1) * GW] = piece.astype(BF)
        dh = dh + _dot(dn_ref[:, :W_UZ0], w_ref[:W_UZ0, :])
        xv = x_ref[...]
        dx1, dg1 = _rms_bwd(xv, _rms_stats(xv), g1_ref[...], dh)
        gx_ref[...] = dx2_ref[...] + dx1
        dg1_ref[...] += dg1

    row = lambda w: pl.BlockSpec((tm, w), lambda i: (i, 0))
    full = lambda s: pl.BlockSpec(s, lambda i: (0, 0))
    dil = lambda d: pl.BlockSpec((3, d, tm // d, GW), lambda i: (0, 0, i, 0))
    return pl.pallas_call(
        body, name="inproj_bwd", grid=(S // tm,),
        out_shape=(jax.ShapeDtypeStruct((S, D), F32), jax.ShapeDtypeStruct((1, D), F32),
                   jax.ShapeDtypeStruct((S, win_t.shape[0]), BF)),
        in_specs=[dil(d) for d in DILS] + [row(2 * GMLP_W), row(2 * D), _resident(win_t.shape), row(128), row(128), row(D), row(D),
                                            full((1, D))],
        out_specs=(row(D), full((1, D)), row(win_t.shape[0])),
        scratch_shapes=[pltpu.VMEM((2, tm, 128), F32)],
        compiler_params=_cp(("arbitrary",)))(*dqkvs, dpuz, dgates, win_t, cos_t, sin_t, x, dx2, g1)


def _row_tile(rows, cap=256):
    return max(t for t in range(16, cap + 1, 16) if rows % t == 0)


def _adam_math(w, g, m, v):
    m2 = ADAM_B1 * m + (1.0 - ADAM_B1) * g
    v2 = ADAM_B2 * v + (1.0 - ADAM_B2) * (g * g)
    m_hat = m2 / (1.0 - ADAM_B1 ** ADAM_STEP)
    v_hat = v2 / (1.0 - ADAM_B2 ** ADAM_STEP)
    delta = -ADAM_LR * (m_hat / (jnp.sqrt(v_hat) + ADAM_EPS) + ADAM_WD * w)
    return delta, m2, v2


def _adam_shard(own, recv, w, m, v, name):
    R, C = w.shape
    tr = _row_tile(R)

    def body(own_ref, r_ref, w_ref, m_ref, v_ref, g_ref, d_ref, m2_ref, v2_ref):
        g = own_ref[...] + r_ref[0].astype(F32) + r_ref[1].astype(F32) + r_ref[2].astype(F32)
        g_ref[...] = g
        d_ref[...], m2_ref[...], v2_ref[...] = _adam_math(w_ref[...], g, m_ref[...], v_ref[...])

    spec = pl.BlockSpec((tr, C), lambda i: (i, 0))
    out = jax.ShapeDtypeStruct((R, C), F32)
    return pl.pallas_call(
        body, name=name, grid=(R // tr,), out_shape=(out, out, out, out),
        in_specs=[spec, pl.BlockSpec((3, tr, C), lambda i: (0, i, 0)), spec, spec, spec],
        out_specs=(spec, spec, spec, spec), compiler_params=_cp(("parallel",)))(own, recv, w, m, v)


def _rs_add(gblocks, recv, idx, name):
    _, R, C = gblocks.shape
    tr = _row_tile(R)

    def body(t_ref, g_ref, r_ref, own_ref, send_ref):
        j = pl.program_id(1)
        s = g_ref[...] + r_ref[...]

        @pl.when(j == 0)
        def _():
            own_ref[...] = s

        @pl.when(j > 0)
        def _():
            send_ref[...] = s.astype(BF)

    grid_spec = pltpu.PrefetchScalarGridSpec(
        num_scalar_prefetch=1, grid=(R // tr, 4),
        in_specs=[pl.BlockSpec((None, tr, C), lambda i, j, t: (t[j], i, 0)),
                  pl.BlockSpec((None, tr, C), lambda i, j, t: (t[4 + j], i, 0))],
        out_specs=[pl.BlockSpec((tr, C), lambda i, j, t: (i, 0)),
                   pl.BlockSpec((None, tr, C), lambda i, j, t: (jnp.maximum(j - 1, 0), i, 0))])
    return pl.pallas_call(
        body, name=name, grid_spec=grid_spec,
        out_shape=(jax.ShapeDtypeStruct((R, C), F32), jax.ShapeDtypeStruct((3, R, C), BF)),
        compiler_params=_cp(("parallel", "arbitrary")))(idx, gblocks, recv)


def _mesh_pos():
    return lax.axis_index("x"), lax.axis_index("y"), lax.axis_index("c")


_HBM = pl.BlockSpec(memory_space=pltpu.HBM)
_SEM = pl.BlockSpec(memory_space=pltpu.SEMAPHORE)
_EFFECT = pltpu.SideEffectType.DATAFLOW_SIDE_EFFECTING
_RELATIONS = [(dx, dy, dc) for dx in (0, 1) for dy in (0, 1) for dc in (0, 1)][1:]


def _flip(v, d):
    return 1 - v if d else v


def _plan_gather(n):
    def plan(x, y, c):
        return [(k, None, 4 * x + 2 * y + c, (_flip(x, dx), _flip(y, dy), _flip(c, dc)))
                for k in range(n) for dx, dy, dc in _RELATIONS]
    return plan


def _plan_gather_near(x, y, c):
    chips = [(1 - x, y), (x, 1 - y), (1 - x, 1 - y)]
    return [(0, None, 4 * x + 2 * y + c, (x, y, 1 - c))] + [(0, None, 4 * x + 2 * y + c, (*chip, c)) for chip in chips]


def _plan_gather_pass(from_landing):
    def plan(x, y, c):
        blocks = [4 * cx + 2 * cy + c for cx, cy in ((1 - x, y), (x, 1 - y), (1 - x, 1 - y))]
        return [(0, b if from_landing else None, b, (x, y, 1 - c)) for b in blocks]
    return plan


def _plan_d2d(n):
    def plan(x, y, c):
        return [(k, 2 * kk + 1 - c, kk, (x, y, 1 - c)) for k in range(n) for kk in range(4)]
    return plan


def _plan_ici(n):
    def plan(x, y, c):
        return [(k, j, j, (_flip(x, dx), _flip(y, dy), c))
                for k in range(n) for j, (dx, dy) in enumerate(((1, 0), (0, 1), (1, 1)))]
    return plan


def _plan_copies(plan, src_refs, land_refs, send_sems, recv_sems):
    x, y, c = _mesh_pos()
    return [pltpu.make_async_remote_copy(
        src_ref=src_refs[k] if si is None else src_refs[k].at[si], dst_ref=land_refs[k].at[di],
        send_sem=send_sems.at[n], recv_sem=recv_sems.at[n], device_id=dev, device_id_type=MESH)
        for n, (k, si, di, dev) in enumerate(plan(x, y, c))]


def _exchange_start(srcs, land_shapes, plan, ncopies, name, after):
    n = len(srcs)

    def body(*refs):
        src_refs, land_refs = refs[:n], refs[n:2 * n]
        send_sems, recv_sems = refs[2 * n + len(after)], refs[2 * n + len(after) + 1]
        token = refs[-1]
        for cp in _plan_copies(plan, src_refs, land_refs, send_sems, recv_sems):
            cp.start()
        token[...] = jnp.zeros_like(token)

    lands = [pltpu.with_memory_space_constraint(lax.empty(s, a.dtype), pltpu.HBM) for s, a in zip(land_shapes, srcs)]
    srcs = [pltpu.with_memory_space_constraint(a, pltpu.HBM) for a in srcs]
    outs = pl.pallas_call(
        body, name=name,
        out_shape=(pltpu.SemaphoreType.DMA((ncopies,)), pltpu.SemaphoreType.DMA((ncopies,)))
        + tuple(pltpu.HBM(a.shape, a.dtype) for a in srcs) + tuple(pltpu.HBM(a.shape, a.dtype) for a in lands)
        + (jax.ShapeDtypeStruct((8, 128), F32),),
        in_specs=[_HBM] * (2 * n) + [pl.BlockSpec(memory_space=pl.ANY)] * len(after),
        out_specs=(_SEM, _SEM) + (_HBM,) * (2 * n) + (pl.BlockSpec(memory_space=pltpu.VMEM),),
        input_output_aliases={i: 2 + i for i in range(2 * n)},
        compiler_params=pltpu.CompilerParams(has_side_effects=_EFFECT))(*srcs, *lands, *after)
    return (outs[0], outs[1], list(outs[2:2 + n]), list(outs[2 + n:2 + 2 * n])), outs[-1]


def _exchange_forward(handle, plan, plan_fwd, nfwd, name, after):
    send_sems, recv_sems, srcs, lands = handle
    n = len(srcs)

    def body(*refs):
        src_refs, land_refs = refs[:n], refs[n:2 * n]
        outs = refs[2 * n + 2 + len(after):]
        for cp in _plan_copies(plan, src_refs, land_refs, refs[2 * n], refs[2 * n + 1]):
            cp.wait_send()
            cp.wait_recv()
        for cp in _plan_copies(plan_fwd, land_refs, land_refs, outs[0], outs[1]):
            cp.start()

    outs = pl.pallas_call(
        body, name=name,
        out_shape=(pltpu.SemaphoreType.DMA((nfwd,)), pltpu.SemaphoreType.DMA((nfwd,)))
        + tuple(pltpu.HBM(a.shape, a.dtype) for a in srcs) + tuple(pltpu.HBM(a.shape, a.dtype) for a in lands),
        in_specs=[_HBM] * (2 * n) + [_SEM, _SEM] + [pl.BlockSpec(memory_space=pl.ANY)] * len(after),
        out_specs=(_SEM, _SEM) + (_HBM,) * (2 * n), input_output_aliases={i: 2 + i for i in range(2 * n)},
        compiler_params=pltpu.CompilerParams(has_side_effects=_EFFECT))(*srcs, *lands, send_sems, recv_sems, *after)
    return outs[0], outs[1], list(outs[2:2 + n]), list(outs[2 + n:2 + 2 * n])


def _exchange_wait(handle, plan, name, after):
    send_sems, recv_sems, srcs, lands = handle
    n = len(srcs)

    def body(*refs):
        src_refs, land_refs = refs[:n], refs[n:2 * n]
        for cp in _plan_copies(plan, src_refs, land_refs, refs[2 * n], refs[2 * n + 1]):
            cp.wait_send()
            cp.wait_recv()

    outs = pl.pallas_call(
        body, name=name,
        out_shape=tuple(pltpu.HBM(a.shape, a.dtype) for a in srcs) + tuple(pltpu.HBM(a.shape, a.dtype) for a in lands),
        in_specs=[_HBM] * (2 * n) + [_SEM, _SEM] + [pl.BlockSpec(memory_space=pl.ANY)] * len(after),
        out_specs=(_HBM,) * (2 * n), input_output_aliases={i: i for i in range(2 * n)},
        compiler_params=pltpu.CompilerParams(has_side_effects=_EFFECT))(*srcs, *lands, send_sems, recv_sems, *after)
    return list(outs[:n]), list(outs[n:])


SMALL = ("g1", "g2", "g3", "g4", "ln_g", "ln_b", "b_sp", "w_sp")


def _small_sum_adam(block, parts, lands, ws, ms, vs):
    n, ns = len(parts), len(SMALL)

    def body(blk_ref, *refs):
        p_refs, l_refs = refs[:n], refs[n:2 * n]
        w_refs, m_refs, v_refs = (refs[2 * n + i * ns:2 * n + (i + 1) * ns] for i in range(3))
        o = 2 * n + 3 * ns
        loss_ref = refs[o]
        g_out, d_out, m_out, v_out = (refs[o + 1 + i * ns:o + 1 + (i + 1) * ns] for i in range(4))
        me = blk_ref[0]
        sums = []
        for k in range(n):
            acc = jnp.where(me == 0, p_refs[k][...], l_refs[k][0])
            for b in range(1, 8):
                acc = acc + jnp.where(me == b, p_refs[k][...], l_refs[k][b])
            sums.append(acc)
        loss_ref[...] = sums[0]
        for i in range(ns):
            g_out[i][...] = sums[1 + i]
            d_out[i][...], m_out[i][...], v_out[i][...] = _adam_math(w_refs[i][...], sums[1 + i], m_refs[i][...],
                                                                     v_refs[i][...])

    args = list(parts) + list(lands) + [t[k] for t in (ws, ms, vs) for k in SMALL]
    shapes = [jax.ShapeDtypeStruct(p.shape, F32) for p in parts[:1]] + [jax.ShapeDtypeStruct(p.shape, F32) for p in parts[1:]] * 4
    vmem = pl.BlockSpec(memory_space=pltpu.VMEM)
    outs = pl.pallas_call(
        body, name="small_sum_adam", out_shape=tuple(shapes),
        in_specs=[pl.BlockSpec(memory_space=pltpu.SMEM)] + [vmem] * len(args), out_specs=(vmem,) * len(shapes),
        compiler_params=pltpu.CompilerParams(vmem_limit_bytes=VMEM_LIMIT))(block, *args)
    groups = [dict(zip(SMALL, outs[1 + i * ns:1 + (i + 1) * ns])) for i in range(4)]
    return (outs[0], *groups)


def _rope_tables(S):
    half = HD // 2
    inv_freq = jnp.tile(ROPE_THETA ** (-jnp.arange(half, dtype=F32) / half), 4)
    sign = jnp.tile(jnp.concatenate([-jnp.ones(half, F32), jnp.ones(half, F32)]), 2)
    ang = jnp.arange(S, dtype=F32)[:, None] * inv_freq[None, :]
    return jnp.cos(ang), jnp.sin(ang) * sign[None, :]


def _to_blocks(g, col_sharded):
    if col_sharded:
        return g.reshape(g.shape[0], 8, g.shape[1] // 8).transpose(1, 0, 2)
    return g.reshape(8, g.shape[0] // 8, g.shape[1])


def _from_blocks(t, col_sharded):
    if col_sharded:
        return t.transpose(1, 0, 2).reshape(t.shape[1], 8 * t.shape[2])
    return t.reshape(8 * t.shape[1], t.shape[2])


class _NoComm:
    def __init__(self, late_weights):
        self._late = late_weights
        self.grads = {}

    def start_tie(self):
        return jnp.zeros((8, 128), F32)

    def late_weights(self, after):
        return self._late

    def small_start(self, loss_row, grads):
        self.small = (loss_row, grads)
        return jnp.zeros((8, 128), F32)

    def rs_start(self, key, gblocks, after=()):
        self.grads[key] = gblocks
        return jnp.zeros((8, 128), F32)

    def rs_mid(self, key, after):
        return jnp.zeros((8, 128), F32)


class _FsdpComm:
    def __init__(self, late_shards, col_sharded, after, idx, block):
        self._col, self._idx, self._block, self._rs = col_sharded, idx, block, {}
        n = len(late_shards)
        self._gather, self._token = _exchange_start(
            late_shards, [(8,) + s.shape for s in late_shards], _plan_gather(n), 7 * n, "ag_late_start", (after,))

    def start_tie(self):
        return self._token

    def late_weights(self, after):
        shards, lands = _exchange_wait(self._gather, _plan_gather(len(self._col)), "ag_late_wait", after)
        lands = [lax.dynamic_update_index_in_dim(t, s, self._block, 0) for t, s in zip(lands, shards)]
        return [_from_blocks(t, cs) for t, cs in zip(lands, self._col)]

    def small_start(self, loss_row, grads):
        parts = [loss_row] + [grads[k] for k in SMALL]
        n = len(parts)
        self._small, token = _exchange_start(parts, [(8,) + p.shape for p in parts], _plan_gather(n), 7 * n,
                                             "small_start", (self._token,))
        return token

    def small_finish(self, ws, ms, vs, after):
        n = 1 + len(SMALL)
        parts, lands = _exchange_wait(self._small, _plan_gather(n), "small_wait", after)
        return _small_sum_adam(jnp.reshape(self._block, (1,)).astype(jnp.int32), parts, lands, ws, ms, vs)

    def rs_start(self, key, gblocks, after=()):
        n = len(gblocks)
        d2d, token = _exchange_start(gblocks, [(4,) + g.shape[1:] for g in gblocks], _plan_d2d(n), 4 * n,
                                     "rs_%s_d2d_start" % key, (self._token,) + tuple(after))
        self._rs[key] = dict(n=n, d2d=d2d)
        return token

    def rs_mid(self, key, after):
        st = self._rs[key]
        gblocks, from_sib = _exchange_wait(st["d2d"], _plan_d2d(st["n"]), "rs_%s_d2d_wait" % key, after)
        halves = [_rs_add(g, r, self._idx, "rs_add_%s_%d" % (key, k)) for k, (g, r) in enumerate(zip(gblocks, from_sib))]
        st["own"] = [own for own, _ in halves]
        sends = [send for _, send in halves]
        st["ici"], token = _exchange_start(sends, [t.shape for t in sends], _plan_ici(st["n"]), 3 * st["n"],
                                           "rs_%s_ici_start" % key, (self._token,))
        return token

    def rs_end(self, key, after):
        st = self._rs[key]
        return st["own"], _exchange_wait(st["ici"], _plan_ici(st["n"]), "rs_%s_ici_wait" % key, after)[1]


def _local_step(x, tgt, h1, cos_t, sin_t, win_t, comm, g1, g2, g3, g4, ln_g, ln_b, w_sp, b_sp):
    b_sp_t = b_sp.T
    w_sp_t = w_sp.transpose(0, 2, 1)

    qkvs = [_proj_qkv(h1, win_t, cos_t, sin_t, g, d, "proj_qkv_d%d" % d, comm.start_tie()) for g, d in enumerate(DILS)]
    puz = _proj(h1, win_t, W_UZ0, W_G0 - W_UZ0, "none", "proj_uz", 1024)
    gates = _proj(h1, win_t, W_G0, 2 * D, "sigmoid", "proj_gates", 512)
    fwd = [_attn_fwd(t, "attn_fwd_d%d" % d) for t, d in zip(qkvs, DILS)]
    ya, *lses = _attn_combine([o for o, _ in fwd], [l for _, l in fwd])
    yg = _gmlp_fwd(puz, ln_g, ln_b, w_sp, b_sp_t)
    wba, wbg, wout, wmi, wmo = comm.late_weights(after=(ya, yg, gates))
    merged, y, x2, h3 = _merge_fwd(ya, yg, gates, wba, wbg, wout, x, g2, g3)
    a, dy2, dout, loss_row, dg4 = _mlp_fwd(h3, wmi, wmo, x2, tgt, g4)

    dpre, dx2, dy, dg3, dg2 = _mlp_bwd(dy2, a, wmo, wmi, x2, y, dout, g2, g3)
    dwmo = _mm_tn(a, dy2, "dw_mlp_out", square_a=True)
    dwmi = _mm_tn(h3, dpre, "dw_mlp_in", col_blocks=8)
    tie = comm.rs_start("mlp", [dwmi, _to_blocks(dwmo, False)])
    dgates, da, db, dyg, *rest = _outproj_bwd(dy, wout, ya, yg, gates, wba, wbg, tie)
    dyas, dsums = rest[:3], rest[3:]
    tie = comm.rs_mid("mlp", after=(dyg,))
    dpuz, dwsp, dbs, dlng, dlnb = _gmlp_bwd(puz, dyg, ln_g + tie[0, 0], ln_b, w_sp, w_sp_t, b_sp_t)
    dqkvs = [_attn_bwd(qkvs[g], dyas[g], dsums[g], lses[g], "attn_bwd_d%d" % d) for g, d in enumerate(DILS)]
    grad_x, dg1, dproj = _inproj_bwd(dqkvs, dpuz, dgates, win_t, cos_t, sin_t, x, dx2, g1)
    small = dict(g1=dg1, g2=dg2, g3=dg3, g4=dg4, ln_g=dlng, ln_b=dlnb, b_sp=dbs.reshape(4, CHUNK),
                 w_sp=dwsp.reshape(4 * CHUNK, CHUNK))
    tie = comm.small_start(loss_row, small)
    dwin_t = _mm_tn(dproj, h1, "dw_in", tie=tie)
    tie = comm.rs_start("win", [_to_blocks(dwin_t, False)])
    dwout = _mm_tn(merged, dy, "dw_out", tie=tie)
    dwba = _mm_tn(ya, da, "dw_branch_attn", tie=tie, col_blocks=8)
    dwbg = _mm_tn(yg, db, "dw_branch_gmlp", tie=tie, col_blocks=8)
    tie = comm.rs_mid("win", after=(dwout, dwba, dwbg))
    tie = comm.rs_start("mid", [dwba, dwbg, _to_blocks(dwout, False)], after=(tie,))
    return grad_x, tie


def kernel(x, norm_pre_mix, w_in, w_spatial, b_spatial, ln_v_gain, ln_v_bias, w_branch_attn, w_branch_gmlp, w_out, norm_post_mix, norm_pre_mlp, w_mlp_in, w_mlp_out, norm_post_mlp, loss_target, m_norm_pre_mix, m_w_in, m_w_spatial, m_b_spatial, m_ln_v_gain, m_ln_v_bias, m_w_branch_attn, m_w_branch_gmlp, m_w_out, m_norm_post_mix, m_norm_pre_mlp, m_w_mlp_in, m_w_mlp_out, m_norm_post_mlp, v_norm_pre_mix, v_w_in, v_w_spatial, v_b_spatial, v_ln_v_gain, v_ln_v_bias, v_w_branch_attn, v_w_branch_gmlp, v_w_out, v_norm_post_mix, v_norm_pre_mlp, v_w_mlp_in, v_w_mlp_out, v_norm_post_mlp):
    mx, my, mc = _mesh_pos()
    rel = [(0, 0), (1, 0), (0, 1), (1, 1)]
    chip_of = [2 * (mx ^ dx) + (my ^ dy) for dx, dy in rel]
    idx = jnp.stack([2 * k + mc for k in chip_of] + chip_of).astype(jnp.int32)

    w_in_t, m_w_in_t, v_w_in_t = (t[0].T for t in (w_in, m_w_in, v_w_in))
    shard = w_in_t.astype(BF)
    gather, token = _exchange_start([shard], [(8,) + shard.shape], _plan_gather_near, 4, "ag_win_start", ())
    h1 = _rms_fwd(x[0], norm_pre_mix + token[0, 0], "rms_pre_mix")
    cos_t, sin_t = _rope_tables(x.shape[1])
    gather = _exchange_forward(gather, _plan_gather_near, _plan_gather_pass(True), 3, "ag_win_pass", (h1, cos_t, sin_t))
    (shard,), (g_win,) = _exchange_wait(gather, _plan_gather_pass(False), "ag_win_wait", ())
    g_win = lax.dynamic_update_index_in_dim(g_win, shard, 4 * mx + 2 * my + mc, 0)
    win_t = _from_blocks(g_win, False)
    late = [w_branch_attn[0], w_branch_gmlp[0], w_out[0], w_mlp_in[0], w_mlp_out[0]]
    comm = _FsdpComm([w.astype(BF) for w in late], [True, True, False, True, False], g_win, idx, 4 * mx + 2 * my + mc)

    grad_x, tie = _local_step(
        x[0], loss_target[0], h1, cos_t, sin_t, win_t, comm,
        norm_pre_mix, norm_post_mix, norm_pre_mlp, norm_post_mlp, ln_v_gain, ln_v_bias, w_spatial[0], b_spatial[0])

    flat = lambda t: t.reshape(-1, t.shape[-1])
    small_w = dict(zip(SMALL, map(flat, (norm_pre_mix, norm_post_mix, norm_pre_mlp, norm_post_mlp, ln_v_gain, ln_v_bias,
                                         b_spatial, w_spatial))))
    small_m = dict(zip(SMALL, map(flat, (m_norm_pre_mix, m_norm_post_mix, m_norm_pre_mlp, m_norm_post_mlp, m_ln_v_gain,
                                         m_ln_v_bias, m_b_spatial, m_w_spatial))))
    small_v = dict(zip(SMALL, map(flat, (v_norm_pre_mix, v_norm_post_mix, v_norm_pre_mlp, v_norm_post_mlp, v_ln_v_gain,
                                         v_ln_v_bias, v_b_spatial, v_w_spatial))))
    tie = comm.rs_mid("mid", after=(tie,))
    mlp_own, mlp_chips = comm.rs_end("mlp", after=(tie,))
    adam = lambda nm, own, r, w, m, v: _adam_shard(own, r, w[0], m[0], v[0], "adam_" + nm)
    upd = {
        "w_mlp_in": adam("w_mlp_in", mlp_own[0], mlp_chips[0], w_mlp_in, m_w_mlp_in, v_w_mlp_in),
        "w_mlp_out": adam("w_mlp_out", mlp_own[1], mlp_chips[1], w_mlp_out, m_w_mlp_out, v_w_mlp_out),
    }
    loss_out, sg, sd, sm, sv = comm.small_finish(small_w, small_m, small_v, after=(upd["w_mlp_in"][0], upd["w_mlp_out"][0]))
    loss = loss_out[0, 0]
    win_own, win_chips = comm.rs_end("win", after=(loss_out,))
    upd["w_in"] = tuple(t.T for t in _adam_shard(win_own[0], win_chips[0], w_in_t, m_w_in_t, v_w_in_t, "adam_w_in"))
    mid_own, mid_chips = comm.rs_end("mid", after=(upd["w_in"][0],))
    upd["w_branch_attn"] = adam("w_branch_attn", mid_own[0], mid_chips[0], w_branch_attn, m_w_branch_attn, v_w_branch_attn)
    upd["w_branch_gmlp"] = adam("w_branch_gmlp", mid_own[1], mid_chips[1], w_branch_gmlp, m_w_branch_gmlp, v_w_branch_gmlp)
    upd["w_out"] = adam("w_out", mid_own[2], mid_chips[2], w_out, m_w_out, v_w_out)
    order = ["g1", "w_in", "w_sp", "b_sp", "ln_g", "ln_b", "w_branch_attn", "w_branch_gmlp", "w_out", "g2", "g3",
             "w_mlp_in", "w_mlp_out", "g4"]
    small_shape = dict(g1=norm_pre_mix.shape, g2=norm_post_mix.shape, g3=norm_pre_mlp.shape, g4=norm_post_mlp.shape,
                       ln_g=ln_v_gain.shape, ln_b=ln_v_bias.shape, b_sp=b_spatial.shape, w_sp=w_spatial.shape)

    def pick(which):
        return [upd[nm][which][None] if nm in upd else (sg, sd, sm, sv)[which][nm].reshape(small_shape[nm])
                for nm in order]

    return (loss, grad_x[None], *pick(0), *pick(1), *pick(2), *pick(3))
```

```python
import functools
import math

import jax
import jax.numpy as jnp
from jax import lax
from jax.experimental import pallas as pl
from jax.experimental.pallas import tpu as pltpu

D = 1024
HD = 64
NSLOT = 4
GW = NSLOT * HD
DILS = (1, 4, 16)
QB = 128
ATTN_NSUB = 4
PROJ_TM = 1024
MLP_TM_FWD = 512
MLP_TM_BWD = 256
AW = 3 * GW
W_UZ0, W_G0 = 3 * AW, 3 * AW + 1024
GMLP_W = 512
CHUNK = 128
DFF = 4096
EPS = 1e-6
ROPE_THETA = 10000.0
SCALE = HD ** -0.5
NEG = -1e30

ADAM_LR = 0.001
ADAM_B1 = 0.9
ADAM_B2 = 0.999
ADAM_EPS = 1e-08
ADAM_WD = 0.01
ADAM_STEP = 10

BF = jnp.bfloat16
F32 = jnp.float32
MESH = pl.DeviceIdType.MESH
VMEM_LIMIT = 56 * 1024 * 1024


def _cp(sem):
    return pltpu.CompilerParams(dimension_semantics=sem, vmem_limit_bytes=VMEM_LIMIT)


def _dot(a, b):
    return jnp.dot(a, b, preferred_element_type=F32)


def _dot_nt(a, b):
    return lax.dot_general(a, b, (((1,), (1,)), ((), ())), preferred_element_type=F32)


def _dot_tn(a, b):
    return lax.dot_general(a, b, (((0,), (0,)), ((), ())), preferred_element_type=F32)


def _gelu(x):
    return jax.nn.gelu(x, approximate=True)


def _gelu_grad(x):
    k = math.sqrt(2.0 / math.pi)
    t = jnp.tanh(k * (x + 0.044715 * x * x * x))
    return 0.5 * (1.0 + t) + 0.5 * x * (1.0 - t * t) * (k * (1.0 + 3.0 * 0.044715 * x * x))


def _swap_halves(t):
    w = t.shape[1]
    lane = lax.broadcasted_iota(jnp.int32, t.shape, 1)
    first = (lane & (HD - 1)) < (HD // 2)
    return jnp.where(first, pltpu.roll(t, w - HD // 2, 1), pltpu.roll(t, HD // 2, 1))


def _head_mask(hh):
    lane = lax.broadcasted_iota(jnp.int32, (1, GW), 1)
    return jnp.logical_and(lane >= hh * HD, lane < (hh + 1) * HD)


def _rms_stats(xf):
    return lax.rsqrt(jnp.mean(xf * xf, axis=-1, keepdims=True) + EPS)


def _rms_bwd(xf, r, gain, dout):
    n = xf * r
    t = dout * gain
    dx = r * (t - n * jnp.mean(t * n, axis=-1, keepdims=True))
    return dx, jnp.sum(dout * n, axis=0, keepdims=True)


def _rms_fwd(x, gain, name):
    S = x.shape[0]
    tm = min(512, S)

    def body(x_ref, g_ref, h_ref):
        xf = x_ref[...]
        h_ref[...] = (xf * _rms_stats(xf) * g_ref[...]).astype(BF)

    return pl.pallas_call(
        body, name=name, out_shape=jax.ShapeDtypeStruct((S, D), BF), grid=(S // tm,),
        in_specs=[pl.BlockSpec((tm, D), lambda i: (i, 0)), pl.BlockSpec((1, D), lambda i: (0, 0))],
        out_specs=pl.BlockSpec((tm, D), lambda i: (i, 0)), compiler_params=_cp(("parallel",)))(x, gain)


def _dilate_store(val, scr, o_ref, lead, d):
    rows = val.shape[0]
    if d == 1:
        o_ref[lead + (0,)] = val.astype(o_ref.dtype)
        return
    for hf in range(2):
        scr[hf, pl.ds(0, rows), :] = val[:, hf * 128:(hf + 1) * 128]
    for r in range(d):
        for hf in range(2):
            o_ref[lead + (r, slice(None), slice(hf * 128, (hf + 1) * 128))] = (
                scr[hf, pl.ds(r, rows // d, stride=d), :].astype(o_ref.dtype))


def _undilate_load(i_ref, lead, d, scr, rows):
    if d == 1:
        return i_ref[lead + (0,)].astype(F32)
    for r in range(d):
        for hf in range(2):
            scr[hf, pl.ds(r, rows // d, stride=d), :] = (
                i_ref[lead + (r, slice(None), slice(hf * 128, (hf + 1) * 128))].astype(F32))
    return jnp.concatenate([scr[0, pl.ds(0, rows), :], scr[1, pl.ds(0, rows), :]], axis=1)


def _rope_fwd(y, c_ref, s_ref):
    cosv = jnp.concatenate([c_ref[...]] * 2, axis=1)
    sinv = jnp.concatenate([s_ref[...]] * 2, axis=1)
    return y * cosv + _swap_halves(y) * sinv


def _rope_bwd(dy, c_ref, s_ref):
    cosv = jnp.concatenate([c_ref[...]] * 2, axis=1)
    sinv = jnp.concatenate([s_ref[...]] * 2, axis=1)
    return dy * cosv + _swap_halves(dy * sinv)


def _proj_all(h, win_t, cos_t, sin_t, tie):
    S, K = h.shape
    tm = min(PROJ_TM, S)

    def body(h_ref, w_ref, c_ref, s_ref, tie_ref, o0_ref, o1_ref, o2_ref, p_ref, gt_ref, scr):
        hv = h_ref[...]
        col = lambda b: _dot_nt(hv, w_ref[b * GW:(b + 1) * GW, :])
        ys = [col(3 * t) for t in range(3)]
        for g, (o_ref, d) in enumerate(zip((o0_ref, o1_ref, o2_ref), DILS)):
            nxt = [col(3 * t + g + 1) if g < 2 else col(W_UZ0 // GW + t) for t in range(3)]
            for t in range(3):
                y = _rope_fwd(ys[t], c_ref, s_ref) if t < 2 else ys[t]
                _dilate_store(y, scr.at[t], o_ref, (t,), d)
            ys = nxt
        for b in range((W_G0 - W_UZ0) // GW):
            y = ys[b] if b < 3 else col(W_UZ0 // GW + b)
            p_ref[:, b * GW:(b + 1) * GW] = y.astype(BF)
        for b in range(2 * D // GW):
            gt_ref[:, b * GW:(b + 1) * GW] = jax.nn.sigmoid(col(W_G0 // GW + b)).astype(BF)

    row = lambda w: pl.BlockSpec((tm, w), lambda i: (i, 0))
    return pl.pallas_call(
        body, name="proj_all", grid=(S // tm,),
        out_shape=tuple(jax.ShapeDtypeStruct((3, d, S // d, GW), BF) for d in DILS)
        + (jax.ShapeDtypeStruct((S, W_G0 - W_UZ0), BF), jax.ShapeDtypeStruct((S, 2 * D), BF)),
        in_specs=[row(K), _resident(win_t.shape), row(128), row(128), pl.BlockSpec(memory_space=pl.ANY)],
        out_specs=tuple(pl.BlockSpec((3, d, tm // d, GW), lambda i: (0, 0, i, 0)) for d in DILS)
        + (row(W_G0 - W_UZ0), row(2 * D)),
        scratch_shapes=[pltpu.VMEM((3, 2, tm, 128), F32)],
        compiler_params=_cp(("parallel",)))(h, win_t, cos_t, sin_t, tie)


def _band_masks(first_step):
    row = lax.broadcasted_iota(jnp.int32, (QB, 2 * QB), 0)
    col = lax.broadcasted_iota(jnp.int32, (QB, 2 * QB), 1)
    band = jnp.logical_and(col >= row, col <= row + QB)
    return band, jnp.logical_and(band, jnp.logical_or(col >= QB, jnp.logical_not(first_step)))


def _attn_fwd(qkv, name):
    _, d, L, _ = qkv.shape
    nsub = min(ATTN_NSUB, L // QB)
    R = nsub * QB
    nsteps = L // R

    def body(q_ref, kp_ref, kc_ref, vp_ref, vc_ref, o_ref, lse_ref):
        i = pl.program_id(1)
        band, band_first = _band_masks(i == 0)
        kfull = jnp.concatenate([kp_ref[...], kc_ref[...]], axis=0)
        vfull = jnp.concatenate([vp_ref[...], vc_ref[...]], axis=0)
        chains = [(sb, hh) for sb in range(nsub) for hh in range(NSLOT)]
        win = lambda t, sb: t[sb * QB:(sb + 2) * QB]
        scores = []
        for sb, hh in chains:
            qh = jnp.where(_head_mask(hh), q_ref[sb * QB:(sb + 1) * QB, :], 0)
            scores.append(_dot_nt(qh, win(kfull, sb)))
        soft = []
        for (sb, hh), sc in zip(chains, scores):
            sc = jnp.where(band_first if sb == 0 else band, sc * SCALE, NEG)
            m = jnp.max(sc, axis=1, keepdims=True)
            p = jnp.exp(sc - m)
            den = jnp.sum(p, axis=1, keepdims=True)
            soft.append((p.astype(BF), den, m + jnp.log(den)))
        accs = [_dot(p, win(vfull, sb)) for (sb, hh), (p, _, _) in zip(chains, soft)]
        for sb in range(nsub):
            o = jnp.zeros((QB, GW), F32)
            lse = jnp.zeros((QB, GW), F32)
            for hh in range(NSLOT):
                hm = _head_mask(hh)
                _, den, lrow = soft[sb * NSLOT + hh]
                o = o + jnp.where(hm, accs[sb * NSLOT + hh] / den, 0.0)
                lse = lse + jnp.where(hm, lrow, 0.0)
            o_ref[sb * QB:(sb + 1) * QB, :] = o
            lse_ref[sb * QB:(sb + 1) * QB, :] = lse

    prev = lambda i: jnp.maximum(i * nsub - 1, 0)
    cur = lambda t: pl.BlockSpec((None, None, R, GW), lambda r, i: (t, r, i, 0))
    prv = lambda t: pl.BlockSpec((None, None, QB, GW), lambda r, i: (t, r, prev(i), 0))
    out = pl.BlockSpec((None, R, GW), lambda r, i: (r, i, 0))
    return pl.pallas_call(
        body, name=name, grid=(d, nsteps),
        out_shape=(jax.ShapeDtypeStruct((d, L, GW), F32), jax.ShapeDtypeStruct((d, L, GW), F32)),
        in_specs=[cur(0), prv(1), cur(1), prv(2), cur(2)],
        out_specs=(out, out), compiler_params=_cp(("parallel", "arbitrary")))(qkv, qkv, qkv, qkv, qkv)


def _attn_combine(os_, lses):
    S = os_[0].shape[1]
    tm = min(1024, S)

    def body(o0, o1, o2, l0, l1, l2, y_ref, j0, j1, j2, scr):
        os_nat = [_undilate_load(o, (), d, scr, tm) for o, d in zip((o0, o1, o2), DILS)]
        a, b, c = [_undilate_load(l, (), d, scr, tm) for l, d in zip((l0, l1, l2), DILS)]
        m = jnp.maximum(jnp.maximum(a, b), c)
        wa, wb, wc = jnp.exp(a - m), jnp.exp(b - m), jnp.exp(c - m)
        den = wa + wb + wc
        y_ref[...] = ((wa * os_nat[0] + wb * os_nat[1] + wc * os_nat[2]) / den).astype(BF)
        lse = m + jnp.log(den)
        for j_ref, d in zip((j0, j1, j2), DILS):
            _dilate_store(lse, scr, j_ref, (), d)

    dil = lambda d: pl.BlockSpec((d, tm // d, GW), lambda i: (0, i, 0))
    dshape = lambda d: jax.ShapeDtypeStruct((d, S // d, GW), F32)
    return pl.pallas_call(
        body, name="attn_combine", grid=(S // tm,),
        out_shape=(jax.ShapeDtypeStruct((S, GW), BF),) + tuple(dshape(d) for d in DILS),
        in_specs=[dil(d) for d in DILS] * 2,
        out_specs=(pl.BlockSpec((tm, GW), lambda i: (i, 0)),) + tuple(dil(d) for d in DILS),
        scratch_shapes=[pltpu.VMEM((2, tm, 128), F32)],
        compiler_params=_cp(("parallel",)))(*os_, *lses)


def _tril(upper=False):
    row = lax.broadcasted_iota(jnp.int32, (CHUNK, CHUNK), 0)
    col = lax.broadcasted_iota(jnp.int32, (CHUNK, CHUNK), 1)
    return row <= col if upper else col <= row


def _ln_fwd(z, gain, bias):
    mu = jnp.mean(z, axis=-1, keepdims=True)
    zc = z - mu
    rstd = lax.rsqrt(jnp.mean(zc * zc, axis=-1, keepdims=True) + EPS)
    zhat = zc * rstd
    return zhat, rstd, zhat * gain + bias


def _gmlp_fwd(puz, ln_g, ln_b, w_sp, b_sp_t):
    S = puz.shape[0]
    tm = min(512, S)
    nch = tm // CHUNK

    def body(p_ref, g_ref, b_ref, w_ref, bt_ref, o_ref):
        tril = _tril()
        ws = [jnp.where(tril, w_ref[gg], 0.0).astype(BF) for gg in range(4)]
        for ch in range(nch):
            rows = slice(ch * CHUNK, (ch + 1) * CHUNK)
            z = _gelu(p_ref[rows, GMLP_W:].astype(F32))
            _, _, zn = _ln_fwd(z, g_ref[...], b_ref[...])
            zn = zn.astype(BF)
            for gg in range(4):
                cols = slice(gg * CHUNK, (gg + 1) * CHUNK)
                sz = _dot(ws[gg], zn[:, cols]) + bt_ref[:, gg:gg + 1]
                u = _gelu(p_ref[rows, cols].astype(F32))
                o_ref[rows, cols] = (u * sz).astype(BF)

    return pl.pallas_call(
        body, name="gmlp_fwd", out_shape=jax.ShapeDtypeStruct((S, GMLP_W), BF), grid=(S // tm,),
        in_specs=[pl.BlockSpec((tm, 2 * GMLP_W), lambda i: (i, 0)),
                  pl.BlockSpec((1, GMLP_W), lambda i: (0, 0)), pl.BlockSpec((1, GMLP_W), lambda i: (0, 0)),
                  pl.BlockSpec((4, CHUNK, CHUNK), lambda i: (0, 0, 0)), pl.BlockSpec((CHUNK, 4), lambda i: (0, 0))],
        out_specs=pl.BlockSpec((tm, GMLP_W), lambda i: (i, 0)),
        compiler_params=_cp(("parallel",)))(puz, ln_g, ln_b, w_sp, b_sp_t)


def _merge_fwd(ya, yg, gates, wba, wbg, wout, x, g2, g3):
    S = x.shape[0]
    tm = min(512, S)

    def body(ya_ref, yg_ref, gt_ref, wba_ref, wbg_ref, wo_ref, x_ref, g2_ref, g3_ref,
             mg_ref, y_ref, x2_ref, h3_ref):
        a = _dot(ya_ref[...], wba_ref[...])
        b = _dot(yg_ref[...], wbg_ref[...])
        merged = (gt_ref[:, :D].astype(F32) * a + gt_ref[:, D:].astype(F32) * b).astype(BF)
        mg_ref[...] = merged
        y = _dot(merged, wo_ref[...])
        y_ref[...] = y
        x2 = x_ref[...] + y * _rms_stats(y) * g2_ref[...]
        x2_ref[...] = x2
        h3_ref[...] = (x2 * _rms_stats(x2) * g3_ref[...]).astype(BF)

    row = lambda w: pl.BlockSpec((tm, w), lambda i: (i, 0))
    full = lambda s: pl.BlockSpec(s, lambda i: (0, 0))
    return pl.pallas_call(
        body, name="merge_fwd", grid=(S // tm,),
        out_shape=(jax.ShapeDtypeStruct((S, D), BF), jax.ShapeDtypeStruct((S, D), F32), jax.ShapeDtypeStruct((S, D), F32),
                   jax.ShapeDtypeStruct((S, D), BF)),
        in_specs=[row(GW), row(GMLP_W), row(2 * D), full((GW, D)), full((GMLP_W, D)), full((D, D)), row(D),
                  full((1, D)), full((1, D))],
        out_specs=(row(D), row(D), row(D), row(D)),
        compiler_params=_cp(("parallel",)))(ya, yg, gates, wba, wbg, wout, x, g2, g3)


def _resident(shape):
    return pl.BlockSpec(shape, lambda i: (0,) * len(shape), pipeline_mode=pl.Buffered(1))


def _mlp_fwd(h3, wmi, wmo, x2, tgt, g4):
    S = x2.shape[0]
    tm = min(MLP_TM_FWD, S)

    def body(h_ref, wi_ref, wo_ref, x2_ref, t_ref, g4_ref, a_ref, dy2_ref, dout_ref, loss_ref, dg4_ref):
        @pl.when(pl.program_id(0) == 0)
        def _():
            loss_ref[...] = jnp.zeros_like(loss_ref)
            dg4_ref[...] = jnp.zeros_like(dg4_ref)

        halves = [slice(hh * (tm // 2), (hh + 1) * (tm // 2)) for hh in range(2)]
        acts = []
        for rows in halves:
            a = jnp.maximum(_dot(h_ref[rows, :], wi_ref[...]), 0.0)
            a_ref[rows, :] = a.astype(BF)
            acts.append((a * a).astype(BF))
        y2s = [_dot(a2, wo_ref[...]) for a2 in acts]
        lane = lax.broadcasted_iota(jnp.int32, (1, 128), 1)
        for rows, y2 in zip(halves, y2s):
            r = _rms_stats(y2)
            out = x2_ref[rows, :] + y2 * r * g4_ref[...]
            err = out - t_ref[rows, :]
            tot = jnp.sum(jnp.sum(err * err, axis=1, keepdims=True), axis=0, keepdims=True) * (0.5 / D)
            loss_ref[...] += jnp.where(lane == 0, tot, 0.0)
            dout = err * (1.0 / D)
            dout_ref[rows, :] = dout
            dy2, dg = _rms_bwd(y2, r, g4_ref[...], dout)
            dy2_ref[rows, :] = dy2.astype(BF)
            dg4_ref[...] += dg

    row = pl.BlockSpec((tm, D), lambda i: (i, 0))
    return pl.pallas_call(
        body, name="mlp_fwd", grid=(S // tm,),
        out_shape=(jax.ShapeDtypeStruct((S, DFF), BF), jax.ShapeDtypeStruct((S, D), BF), jax.ShapeDtypeStruct((S, D), F32),
                   jax.ShapeDtypeStruct((1, 128), F32), jax.ShapeDtypeStruct((1, D), F32)),
        in_specs=[row, _resident((D, DFF)), _resident((DFF, D)), row, row, pl.BlockSpec((1, D), lambda i: (0, 0))],
        out_specs=(pl.BlockSpec((tm, DFF), lambda i: (i, 0)), row, row,
                   pl.BlockSpec((1, 128), lambda i: (0, 0)), pl.BlockSpec((1, D), lambda i: (0, 0))),
        compiler_params=_cp(("arbitrary",)))(h3, wmi, wmo, x2, tgt, g4)


def _mlp_bwd(dy2, a, wmo, wmi, x2, y, dout, g2, g3):
    S = x2.shape[0]
    tm = min(MLP_TM_BWD, S)

    def body(dy2_ref, a_ref, wo_ref, wi_ref, x2_ref, y_ref, dout_ref, g2_ref, g3_ref,
             dpre_ref, dx2_ref, dy_ref, dg3_ref, dg2_ref):
        @pl.when(pl.program_id(0) == 0)
        def _():
            dg3_ref[...] = jnp.zeros_like(dg3_ref)
            dg2_ref[...] = jnp.zeros_like(dg2_ref)

        da2 = _dot_nt(dy2_ref[...], wo_ref[...])
        dpre = (2.0 * a_ref[...].astype(F32) * da2).astype(BF)
        dpre_ref[...] = dpre
        dh3 = _dot_nt(dpre, wi_ref[...])
        x2 = x2_ref[...]
        dx3, dg3 = _rms_bwd(x2, _rms_stats(x2), g3_ref[...], dh3)
        dx2 = dout_ref[...] + dx3
        dx2_ref[...] = dx2
        dg3_ref[...] += dg3
        yv = y_ref[...]
        dy, dg2 = _rms_bwd(yv, _rms_stats(yv), g2_ref[...], dx2)
        dy_ref[...] = dy.astype(BF)
        dg2_ref[...] += dg2

    row = pl.BlockSpec((tm, D), lambda i: (i, 0))
    wide = pl.BlockSpec((tm, DFF), lambda i: (i, 0))
    vec = pl.BlockSpec((1, D), lambda i: (0, 0))
    return pl.pallas_call(
        body, name="mlp_bwd", grid=(S // tm,),
        out_shape=(jax.ShapeDtypeStruct((S, DFF), BF), jax.ShapeDtypeStruct((S, D), F32), jax.ShapeDtypeStruct((S, D), BF),
                   jax.ShapeDtypeStruct((1, D), F32), jax.ShapeDtypeStruct((1, D), F32)),
        in_specs=[row, wide, _resident((DFF, D)), _resident((D, DFF)), row, row, row, vec, vec],
        out_specs=(wide, row, row, vec, vec),
        compiler_params=_cp(("arbitrary",)))(dy2, a, wmo, wmi, x2, y, dout, g2, g3)


def _mm_tn(a, b, name, square_a=False, tm=1024, tn=1024, tk=2048, tie=None, col_blocks=None):
    S, M = a.shape
    N = b.shape[1]
    tk = min(tk, S)
    tm = max(t for t in range(128, min(tm, M) + 1, 128) if M % t == 0)
    tn = max(t for t in range(128, min(tn, N) + 1, 128) if N % t == 0)
    cb = N // col_blocks if col_blocks else tn
    assert M % tm == 0 and S % tk == 0 and tn % cb == 0
    nk = S // tk
    ties = () if tie is None else (tie,)

    def body(a_ref, b_ref, *rest):
        o_ref = rest[-1]
        k = pl.program_id(2)
        av = a_ref[...]
        if square_a:
            av = av * av
        part = _dot_tn(av, b_ref[...])
        if col_blocks:
            part = jnp.stack([part[:, t * cb:(t + 1) * cb] for t in range(tn // cb)])

        @pl.when(k == 0)
        def _():
            o_ref[...] = part

        @pl.when(k > 0)
        def _():
            o_ref[...] += part

    if col_blocks:
        out_shape, out_spec = (col_blocks, M, cb), pl.BlockSpec((tn // cb, tm, cb), lambda i, j, k: (j, i, 0))
    else:
        out_shape, out_spec = (M, N), pl.BlockSpec((tm, tn), lambda i, j, k: (i, j))
    return pl.pallas_call(
        body, name=name, out_shape=jax.ShapeDtypeStruct(out_shape, F32), grid=(M // tm, N // tn, nk),
        in_specs=[pl.BlockSpec((tk, tm), lambda i, j, k: (k, i)), pl.BlockSpec((tk, tn), lambda i, j, k: (k, j))]
        + [pl.BlockSpec(memory_space=pl.ANY)] * len(ties),
        out_specs=out_spec,
        compiler_params=_cp(("parallel", "parallel", "arbitrary")))(a, b, *ties)


def _outproj_bwd(dy, wout, ya, yg, gates, wba, wbg, tie):
    S = dy.shape[0]
    tm = min(512, S)

    def body(dy_ref, wo_ref, ya_ref, yg_ref, gt_ref, wba_ref, wbg_ref, tie_ref,
             dgt_ref, da_ref, db_ref, dyg_ref, e0, e1, e2, s0, s1, s2, scr):
        dm = _dot_nt(dy_ref[...], wo_ref[...])
        ga, gb = gt_ref[:, :D].astype(F32), gt_ref[:, D:].astype(F32)
        dgt_ref[:, :D] = (dm * _dot(ya_ref[...], wba_ref[...]) * ga * (1.0 - ga)).astype(BF)
        dgt_ref[:, D:] = (dm * _dot(yg_ref[...], wbg_ref[...]) * gb * (1.0 - gb)).astype(BF)
        da = (dm * ga).astype(BF)
        db = (dm * gb).astype(BF)
        da_ref[...] = da
        db_ref[...] = db
        dyg_ref[...] = _dot_nt(db, wbg_ref[...]).astype(BF)
        dya = _dot_nt(da, wba_ref[...]).astype(BF).astype(F32)
        dyy = dya * ya_ref[...].astype(F32)
        dsum = jnp.zeros((tm, GW), F32)
        for hh in range(NSLOT):
            hm = _head_mask(hh)
            dsum = dsum + jnp.where(hm, jnp.sum(jnp.where(hm, dyy, 0.0), axis=1, keepdims=True), 0.0)
        for e_ref, s_ref, d in zip((e0, e1, e2), (s0, s1, s2), DILS):
            _dilate_store(dya, scr, e_ref, (), d)
            _dilate_store(dsum, scr, s_ref, (), d)

    row = lambda w: pl.BlockSpec((tm, w), lambda i: (i, 0))
    full = lambda s: pl.BlockSpec(s, lambda i: (0, 0))
    dil = lambda d: pl.BlockSpec((d, tm // d, GW), lambda i: (0, i, 0))
    dshape = lambda d, t: jax.ShapeDtypeStruct((d, S // d, GW), t)
    return pl.pallas_call(
        body, name="outproj_bwd", grid=(S // tm,),
        out_shape=(jax.ShapeDtypeStruct((S, 2 * D), BF), jax.ShapeDtypeStruct((S, D), BF), jax.ShapeDtypeStruct((S, D), BF),
                   jax.ShapeDtypeStruct((S, GMLP_W), BF)) + tuple(dshape(d, BF) for d in DILS)
        + tuple(dshape(d, F32) for d in DILS),
        in_specs=[row(D), full((D, D)), row(GW), row(GMLP_W), row(2 * D), full((GW, D)), full((GMLP_W, D)),
                  pl.BlockSpec(memory_space=pl.ANY)],
        out_specs=(row(2 * D), row(D), row(D), row(GMLP_W)) + tuple(dil(d) for d in DILS) * 2,
        scratch_shapes=[pltpu.VMEM((2, tm, 128), F32)],
        compiler_params=_cp(("parallel",)))(dy, wout, ya, yg, gates, wba, wbg, tie)


def _gmlp_bwd(puz, dyg, ln_g, ln_b, w_sp, w_sp_t, b_sp_t):
    S = puz.shape[0]
    tm = min(512, S)
    nch = tm // CHUNK

    def body(p_ref, dy_ref, g_ref, b_ref, w_ref, wt_ref, bt_ref,
             dp_ref, dw_ref, dbs_ref, dg_ref, dbias_ref, dbacc_ref):
        i = pl.program_id(0)

        @pl.when(i == 0)
        def _():
            dw_ref[...] = jnp.zeros_like(dw_ref)
            dbacc_ref[...] = jnp.zeros_like(dbacc_ref)
            dg_ref[...] = jnp.zeros_like(dg_ref)
            dbias_ref[...] = jnp.zeros_like(dbias_ref)

        tril = _tril()
        ws = [jnp.where(tril, w_ref[gg], 0.0).astype(BF) for gg in range(4)]
        triu = _tril(upper=True)
        wts = [jnp.where(triu, wt_ref[gg], 0.0).astype(BF) for gg in range(4)]
        gain = g_ref[...]
        for ch in range(nch):
            rows = slice(ch * CHUNK, (ch + 1) * CHUNK)
            pz = p_ref[rows, GMLP_W:].astype(F32)
            z = _gelu(pz)
            zhat, rstd, zn = _ln_fwd(z, gain, b_ref[...])
            znb = zn.astype(BF)
            dzn_parts = []
            for gg in range(4):
                cols = slice(gg * CHUNK, (gg + 1) * CHUNK)
                pu = p_ref[rows, cols].astype(F32)
                u = _gelu(pu)
                sz = _dot(ws[gg], znb[:, cols]) + bt_ref[:, gg:gg + 1]
                dyv = dy_ref[rows, cols].astype(F32)
                dp_ref[rows, cols] = (dyv * sz * _gelu_grad(pu)).astype(BF)
                dsz = dyv * u
                dbacc_ref[gg] += dsz
                dszb = dsz.astype(BF)
                dw_ref[gg] += _dot_nt(dszb, znb[:, cols])
                dzn_parts.append(_dot(wts[gg], dszb))
            dzn = jnp.concatenate(dzn_parts, axis=1)
            dg_ref[...] += jnp.sum(dzn * zhat, axis=0, keepdims=True)
            dbias_ref[...] += jnp.sum(dzn, axis=0, keepdims=True)
            dzh = dzn * gain
            dz = rstd * (dzh - jnp.mean(dzh, axis=-1, keepdims=True)
                         - zhat * jnp.mean(dzh * zhat, axis=-1, keepdims=True))
            dp_ref[rows, GMLP_W:] = (dz * _gelu_grad(pz)).astype(BF)

        @pl.when(i == pl.num_programs(0) - 1)
        def _():
            for gg in range(4):
                dw_ref[gg] = jnp.where(tril, dw_ref[gg], 0.0)
                dbs_ref[gg] = jnp.sum(dbacc_ref[gg], axis=1, keepdims=True)

    full2 = lambda s: pl.BlockSpec(s, lambda i: (0, 0))
    full3 = lambda s: pl.BlockSpec(s, lambda i: (0, 0, 0))
    return pl.pallas_call(
        body, name="gmlp_bwd", grid=(S // tm,),
        out_shape=(jax.ShapeDtypeStruct((S, 2 * GMLP_W), BF), jax.ShapeDtypeStruct((4, CHUNK, CHUNK), F32),
                   jax.ShapeDtypeStruct((4, CHUNK, 1), F32), jax.ShapeDtypeStruct((1, GMLP_W), F32),
                   jax.ShapeDtypeStruct((1, GMLP_W), F32)),
        in_specs=[pl.BlockSpec((tm, 2 * GMLP_W), lambda i: (i, 0)), pl.BlockSpec((tm, GMLP_W), lambda i: (i, 0)),
                  full2((1, GMLP_W)), full2((1, GMLP_W)), full3((4, CHUNK, CHUNK)), full3((4, CHUNK, CHUNK)),
                  full2((CHUNK, 4))],
        out_specs=(pl.BlockSpec((tm, 2 * GMLP_W), lambda i: (i, 0)), full3((4, CHUNK, CHUNK)), full3((4, CHUNK, 1)),
                   full2((1, GMLP_W)), full2((1, GMLP_W))),
        scratch_shapes=[pltpu.VMEM((4, CHUNK, CHUNK), F32)],
        compiler_params=_cp(("arbitrary",)))(puz, dyg, ln_g, ln_b, w_sp, w_sp_t, b_sp_t)


def _attn_bwd(qkv, dya, dsums, lse, name):
    _, d, L, _ = qkv.shape
    nsub = min(ATTN_NSUB, L // QB)
    R = nsub * QB
    nsteps = L // R

    def body(q_ref, qn_ref, kp_ref, kc_ref, vp_ref, vc_ref, dy_ref, dyn_ref, e_ref, en_ref, l_ref, ln_ref, o_ref):
        i = pl.program_id(1)
        band, band_first = _band_masks(i == 0)
        row = lax.broadcasted_iota(jnp.int32, (QB, QB), 0)
        col = lax.broadcasted_iota(jnp.int32, (QB, QB), 1)
        mask_next = jnp.logical_and(col >= row, i < nsteps - 1)
        kc, vc = kc_ref[...], vc_ref[...]
        kfull = jnp.concatenate([kp_ref[...], kc], axis=0)
        vfull = jnp.concatenate([vp_ref[...], vc], axis=0)
        k_last, v_last = kc[(nsub - 1) * QB:], vc[(nsub - 1) * QB:]
        q_ext = jnp.concatenate([q_ref[...], qn_ref[...]], axis=0)
        dy_ext = jnp.concatenate([dy_ref[...], dyn_ref[...]], axis=0)
        esum, esum_n, lse, lse_n = e_ref[...], en_ref[...], l_ref[...], ln_ref[...]
        win = lambda t, sb: t[sb * QB:(sb + 2) * QB]
        blk = lambda t, sb: t[sb * QB:(sb + 1) * QB]
        hms = [_head_mask(hh) for hh in range(NSLOT)]
        q_hs = [jnp.where(hm, q_ext, 0) for hm in hms]
        dy_hs = [jnp.where(hm, dy_ext, 0) for hm in hms]
        raw = []
        for hh in range(NSLOT):
            tiles = [(_dot_nt(blk(q_hs[hh], sb), win(kfull, sb)), _dot_nt(blk(dy_hs[hh], sb), win(vfull, sb)))
                     for sb in range(nsub)]
            tiles.append((_dot_nt(blk(q_hs[hh], nsub), k_last), _dot_nt(blk(dy_hs[hh], nsub), v_last)))
            raw.append(tiles)
        ps, dss = [], []
        for hh in range(NSLOT):
            rowstat = lambda t: jnp.max(jnp.where(hms[hh], t, -jnp.inf), axis=1, keepdims=True)
            p_h, ds_h = [], []
            for sb in range(nsub + 1):
                sc, dp = raw[hh][sb]
                if sb < nsub:
                    msk, lrow, erow = (band_first if sb == 0 else band), rowstat(blk(lse, sb)), rowstat(blk(esum, sb))
                else:
                    msk, lrow, erow = mask_next, rowstat(lse_n), rowstat(esum_n)
                p = jnp.where(msk, jnp.exp(sc * SCALE - lrow), 0.0)
                p_h.append(p.astype(BF))
                ds_h.append((p * (dp - erow)).astype(BF))
            ps.append(p_h)
            dss.append(ds_h)
        dq = [jnp.zeros((QB, GW), F32) for _ in range(nsub)]
        dk = [jnp.zeros((QB, GW), F32) for _ in range(nsub)]
        dv = [jnp.zeros((QB, GW), F32) for _ in range(nsub)]
        for hh in range(NSLOT):
            for sb in range(nsub):
                dq[sb] = dq[sb] + jnp.where(hms[hh], _dot(dss[hh][sb], win(kfull, sb)), 0.0)
                nxt = lambda t: t[sb + 1][:, :QB] if sb + 1 < nsub else t[nsub]
                dk[sb] = dk[sb] + _dot_tn(jnp.concatenate([dss[hh][sb][:, QB:], nxt(dss[hh])], axis=0), win(q_hs[hh], sb))
                dv[sb] = dv[sb] + _dot_tn(jnp.concatenate([ps[hh][sb][:, QB:], nxt(ps[hh])], axis=0), win(dy_hs[hh], sb))
        for sb in range(nsub):
            rows = slice(sb * QB, (sb + 1) * QB)
            o_ref[0, rows, :] = (dq[sb] * SCALE).astype(BF)
            o_ref[1, rows, :] = (dk[sb] * SCALE).astype(BF)
            o_ref[2, rows, :] = dv[sb].astype(BF)

    prev = lambda i: jnp.maximum(i * nsub - 1, 0)
    nxt = lambda i: jnp.minimum((i + 1) * nsub, L // QB - 1)
    cur4 = lambda t: pl.BlockSpec((None, None, R, GW), lambda r, i: (t, r, i, 0))
    prv4 = lambda t: pl.BlockSpec((None, None, QB, GW), lambda r, i: (t, r, prev(i), 0))
    nxt4 = lambda t: pl.BlockSpec((None, None, QB, GW), lambda r, i: (t, r, nxt(i), 0))
    cur3 = pl.BlockSpec((None, R, GW), lambda r, i: (r, i, 0))
    nxt3 = pl.BlockSpec((None, QB, GW), lambda r, i: (r, nxt(i), 0))
    return pl.pallas_call(
        body, name=name, grid=(d, nsteps), out_shape=jax.ShapeDtypeStruct((3, d, L, GW), BF),
        in_specs=[cur4(0), nxt4(0), prv4(1), cur4(1), prv4(2), cur4(2), cur3, nxt3, cur3, nxt3, cur3, nxt3],
        out_specs=pl.BlockSpec((3, None, R, GW), lambda r, i: (0, r, i, 0)),
        compiler_params=_cp(("parallel", "arbitrary")))(qkv, qkv, qkv, qkv, qkv, qkv, dya, dya, dsums, dsums, lse, lse)


def _inproj_bwd(dqkvs, dpuz, dgates, win_t, cos_t, sin_t, x, dx2, g1):
    S = x.shape[0]
    tm = min(512, S)

    def body(d0_ref, d1_ref, d2_ref, dp_ref, dg_ref, w_ref, c_ref, s_ref,
             x_ref, dx2_ref, g1_ref, gx_ref, dg1_ref, dn_ref, scr):
        i = pl.program_id(0)

        @pl.when(i == 0)
        def _():
            dg1_ref[...] = jnp.zeros_like(dg1_ref)

        dh = _dot(dp_ref[...], w_ref[W_UZ0:W_G0, :]) + _dot(dg_ref[...], w_ref[W_G0:, :])
        dn_ref[:, W_UZ0:W_G0] = dp_ref[...]
        dn_ref[:, W_G0:] = dg_ref[...]
        for t in range(3):
            for g, (d_ref, d) in enumerate(zip((d0_ref, d1_ref, d2_ref), DILS)):
                piece = _undilate_load(d_ref, (t,), d, scr, tm)
                if t < 2:
                    piece = _rope_bwd(piece, c_ref, s_ref)
                dn_ref[:, (3 * t + g) * GW:(3 * t + g + 1) * GW] = piece.astype(BF)
        dh = dh + _dot(dn_ref[:, :W_UZ0], w_ref[:W_UZ0, :])
        xv = x_ref[...]
        dx1, dg1 = _rms_bwd(xv, _rms_stats(xv), g1_ref[...], dh)
        gx_ref[...] = dx2_ref[...] + dx1
        dg1_ref[...] += dg1

    row = lambda w: pl.BlockSpec((tm, w), lambda i: (i, 0))
    full = lambda s: pl.BlockSpec(s, lambda i: (0, 0))
    dil = lambda d: pl.BlockSpec((3, d, tm // d, GW), lambda i: (0, 0, i, 0))
    return pl.pallas_call(
        body, name="inproj_bwd", grid=(S // tm,),
        out_shape=(jax.ShapeDtypeStruct((S, D), F32), jax.ShapeDtypeStruct((1, D), F32),
                   jax.ShapeDtypeStruct((S, win_t.shape[0]), BF)),
        in_specs=[dil(d) for d in DILS] + [row(2 * GMLP_W), row(2 * D), _resident(win_t.shape), row(128), row(128), row(D), row(D),
                                            full((1, D))],
        out_specs=(row(D), full((1, D)), row(win_t.shape[0])),
        scratch_shapes=[pltpu.VMEM((2, tm, 128), F32)],
        compiler_params=_cp(("arbitrary",)))(*dqkvs, dpuz, dgates, win_t, cos_t, sin_t, x, dx2, g1)


def _row_tile(rows, cap=256):
    return max(t for t in range(16, cap + 1, 16) if rows % t == 0)


def _adam_math(w, g, m, v):
    m2 = ADAM_B1 * m + (1.0 - ADAM_B1) * g
    v2 = ADAM_B2 * v + (1.0 - ADAM_B2) * (g * g)
    m_hat = m2 / (1.0 - ADAM_B1 ** ADAM_STEP)
    v_hat = v2 / (1.0 - ADAM_B2 ** ADAM_STEP)
    delta = -ADAM_LR * (m_hat / (jnp.sqrt(v_hat) + ADAM_EPS) + ADAM_WD * w)
    return delta, m2, v2


def _adam_shard(own, recv, w, m, v, name):
    R, C = w.shape
    tr = _row_tile(R)

    def body(own_ref, r_ref, w_ref, m_ref, v_ref, g_ref, d_ref, m2_ref, v2_ref):
        g = own_ref[...] + r_ref[0].astype(F32) + r_ref[1].astype(F32) + r_ref[2].astype(F32)
        g_ref[...] = g
        d_ref[...], m2_ref[...], v2_ref[...] = _adam_math(w_ref[...], g, m_ref[...], v_ref[...])

    spec = pl.BlockSpec((tr, C), lambda i: (i, 0))
    out = jax.ShapeDtypeStruct((R, C), F32)
    return pl.pallas_call(
        body, name=name, grid=(R // tr,), out_shape=(out, out, out, out),
        in_specs=[spec, pl.BlockSpec((3, tr, C), lambda i: (0, i, 0)), spec, spec, spec],
        out_specs=(spec, spec, spec, spec), compiler_params=_cp(("parallel",)))(own, recv, w, m, v)


def _rs_add(gblocks, recv, idx, name):
    _, R, C = gblocks.shape
    tr = _row_tile(R)

    def body(t_ref, g_ref, r_ref, own_ref, send_ref):
        j = pl.program_id(1)
        s = g_ref[...] + r_ref[...]

        @pl.when(j == 0)
        def _():
            own_ref[...] = s

        @pl.when(j > 0)
        def _():
            send_ref[...] = s.astype(BF)

    grid_spec = pltpu.PrefetchScalarGridSpec(
        num_scalar_prefetch=1, grid=(R // tr, 4),
        in_specs=[pl.BlockSpec((None, tr, C), lambda i, j, t: (t[j], i, 0)),
                  pl.BlockSpec((None, tr, C), lambda i, j, t: (t[4 + j], i, 0))],
        out_specs=[pl.BlockSpec((tr, C), lambda i, j, t: (i, 0)),
                   pl.BlockSpec((None, tr, C), lambda i, j, t: (jnp.maximum(j - 1, 0), i, 0))])
    return pl.pallas_call(
        body, name=name, grid_spec=grid_spec,
        out_shape=(jax.ShapeDtypeStruct((R, C), F32), jax.ShapeDtypeStruct((3, R, C), BF)),
        compiler_params=_cp(("parallel", "arbitrary")))(idx, gblocks, recv)


def _mesh_pos():
    return lax.axis_index("x"), lax.axis_index("y"), lax.axis_index("c")


_HBM = pl.BlockSpec(memory_space=pltpu.HBM)
_SEM = pl.BlockSpec(memory_space=pltpu.SEMAPHORE)
_EFFECT = pltpu.SideEffectType.DATAFLOW_SIDE_EFFECTING
_RELATIONS = [(dx, dy, dc) for dx in (0, 1) for dy in (0, 1) for dc in (0, 1)][1:]


def _flip(v, d):
    return 1 - v if d else v


def _plan_gather(n):
    def plan(x, y, c):
        return [(k, None, 4 * x + 2 * y + c, (_flip(x, dx), _flip(y, dy), _flip(c, dc)))
                for k in range(n) for dx, dy, dc in _RELATIONS]
    return plan


def _plan_gather_near(x, y, c):
    chips = [(1 - x, y), (x, 1 - y), (1 - x, 1 - y)]
    return [(0, None, 4 * x + 2 * y + c, (x, y, 1 - c))] + [(0, None, 4 * x + 2 * y + c, (*chip, c)) for chip in chips]


def _plan_gather_pass(from_landing):
    def plan(x, y, c):
        blocks = [4 * cx + 2 * cy + c for cx, cy in ((1 - x, y), (x, 1 - y), (1 - x, 1 - y))]
        return [(0, b if from_landing else None, b, (x, y, 1 - c)) for b in blocks]
    return plan


def _plan_d2d(n):
    def plan(x, y, c):
        return [(k, 2 * kk + 1 - c, kk, (x, y, 1 - c)) for k in range(n) for kk in range(4)]
    return plan


def _plan_ici(n):
    def plan(x, y, c):
        return [(k, j, j, (_flip(x, dx), _flip(y, dy), c))
                for k in range(n) for j, (dx, dy) in enumerate(((1, 0), (0, 1), (1, 1)))]
    return plan


def _plan_copies(plan, src_refs, land_refs, send_sems, recv_sems):
    x, y, c = _mesh_pos()
    return [pltpu.make_async_remote_copy(
        src_ref=src_refs[k] if si is None else src_refs[k].at[si], dst_ref=land_refs[k].at[di],
        send_sem=send_sems.at[n], recv_sem=recv_sems.at[n], device_id=dev, device_id_type=MESH)
        for n, (k, si, di, dev) in enumerate(plan(x, y, c))]


def _exchange_start(srcs, land_shapes, plan, ncopies, name, after):
    n = len(srcs)

    def body(*refs):
        src_refs, land_refs = refs[:n], refs[n:2 * n]
        send_sems, recv_sems = refs[2 * n + len(after)], refs[2 * n + len(after) + 1]
        token = refs[-1]
        for cp in _plan_copies(plan, src_refs, land_refs, send_sems, recv_sems):
            cp.start()
        token[...] = jnp.zeros_like(token)

    lands = [pltpu.with_memory_space_constraint(lax.empty(s, a.dtype), pltpu.HBM) for s, a in zip(land_shapes, srcs)]
    srcs = [pltpu.with_memory_space_constraint(a, pltpu.HBM) for a in srcs]
    outs = pl.pallas_call(
        body, name=name,
        out_shape=(pltpu.SemaphoreType.DMA((ncopies,)), pltpu.SemaphoreType.DMA((ncopies,)))
        + tuple(pltpu.HBM(a.shape, a.dtype) for a in srcs) + tuple(pltpu.HBM(a.shape, a.dtype) for a in lands)
        + (jax.ShapeDtypeStruct((8, 128), F32),),
        in_specs=[_HBM] * (2 * n) + [pl.BlockSpec(memory_space=pl.ANY)] * len(after),
        out_specs=(_SEM, _SEM) + (_HBM,) * (2 * n) + (pl.BlockSpec(memory_space=pltpu.VMEM),),
        input_output_aliases={i: 2 + i for i in range(2 * n)},
        compiler_params=pltpu.CompilerParams(has_side_effects=_EFFECT))(*srcs, *lands, *after)
    return (outs[0], outs[1], list(outs[2:2 + n]), list(outs[2 + n:2 + 2 * n])), outs[-1]


def _exchange_forward(handle, plan, plan_fwd, nfwd, name, after):
    send_sems, recv_sems, srcs, lands = handle
    n = len(srcs)

    def body(*refs):
        src_refs, land_refs = refs[:n], refs[n:2 * n]
        outs = refs[2 * n + 2 + len(after):]
        for cp in _plan_copies(plan, src_refs, land_refs, refs[2 * n], refs[2 * n + 1]):
            cp.wait_send()
            cp.wait_recv()
        for cp in _plan_copies(plan_fwd, land_refs, land_refs, outs[0], outs[1]):
            cp.start()

    outs = pl.pallas_call(
        body, name=name,
        out_shape=(pltpu.SemaphoreType.DMA((nfwd,)), pltpu.SemaphoreType.DMA((nfwd,)))
        + tuple(pltpu.HBM(a.shape, a.dtype) for a in srcs) + tuple(pltpu.HBM(a.shape, a.dtype) for a in lands),
        in_specs=[_HBM] * (2 * n) + [_SEM, _SEM] + [pl.BlockSpec(memory_space=pl.ANY)] * len(after),
        out_specs=(_SEM, _SEM) + (_HBM,) * (2 * n), input_output_aliases={i: 2 + i for i in range(2 * n)},
        compiler_params=pltpu.CompilerParams(has_side_effects=_EFFECT))(*srcs, *lands, send_sems, recv_sems, *after)
    return outs[0], outs[1], list(outs[2:2 + n]), list(outs[2 + n:2 + 2 * n])


def _exchange_wait(handle, plan, name, after):
    send_sems, recv_sems, srcs, lands = handle
    n = len(srcs)

    def body(*refs):
        src_refs, land_refs = refs[:n], refs[n:2 * n]
        for cp in _plan_copies(plan, src_refs, land_refs, refs[2 * n], refs[2 * n + 1]):
            cp.wait_send()
            cp.wait_recv()

    outs = pl.pallas_call(
        body, name=name,
        out_shape=tuple(pltpu.HBM(a.shape, a.dtype) for a in srcs) + tuple(pltpu.HBM(a.shape, a.dtype) for a in lands),
        in_specs=[_HBM] * (2 * n) + [_SEM, _SEM] + [pl.BlockSpec(memory_space=pl.ANY)] * len(after),
        out_specs=(_HBM,) * (2 * n), input_output_aliases={i: i for i in range(2 * n)},
        compiler_params=pltpu.CompilerParams(has_side_effects=_EFFECT))(*srcs, *lands, send_sems, recv_sems, *after)
    return list(outs[:n]), list(outs[n:])


SMALL = ("g1", "g2", "g3", "g4", "ln_g", "ln_b", "b_sp", "w_sp")


def _small_sum_adam(block, parts, lands, ws, ms, vs):
    n, ns = len(parts), len(SMALL)

    def body(blk_ref, *refs):
        p_refs, l_refs = refs[:n], refs[n:2 * n]
        w_refs, m_refs, v_refs = (refs[2 * n + i * ns:2 * n + (i + 1) * ns] for i in range(3))
        o = 2 * n + 3 * ns
        loss_ref = refs[o]
        g_out, d_out, m_out, v_out = (refs[o + 1 + i * ns:o + 1 + (i + 1) * ns] for i in range(4))
        me = blk_ref[0]
        sums = []
        for k in range(n):
            acc = jnp.where(me == 0, p_refs[k][...], l_refs[k][0])
            for b in range(1, 8):
                acc = acc + jnp.where(me == b, p_refs[k][...], l_refs[k][b])
            sums.append(acc)
        loss_ref[...] = sums[0]
        for i in range(ns):
            g_out[i][...] = sums[1 + i]
            d_out[i][...], m_out[i][...], v_out[i][...] = _adam_math(w_refs[i][...], sums[1 + i], m_refs[i][...],
                                                                     v_refs[i][...])

    args = list(parts) + list(lands) + [t[k] for t in (ws, ms, vs) for k in SMALL]
    shapes = [jax.ShapeDtypeStruct(p.shape, F32) for p in parts[:1]] + [jax.ShapeDtypeStruct(p.shape, F32) for p in parts[1:]] * 4
    vmem = pl.BlockSpec(memory_space=pltpu.VMEM)
    outs = pl.pallas_call(
        body, name="small_sum_adam", out_shape=tuple(shapes),
        in_specs=[pl.BlockSpec(memory_space=pltpu.SMEM)] + [vmem] * len(args), out_specs=(vmem,) * len(shapes),
        compiler_params=pltpu.CompilerParams(vmem_limit_bytes=VMEM_LIMIT))(block, *args)
    groups = [dict(zip(SMALL, outs[1 + i * ns:1 + (i + 1) * ns])) for i in range(4)]
    return (outs[0], *groups)


def _rope_tables(S):
    half = HD // 2
    inv_freq = jnp.tile(ROPE_THETA ** (-jnp.arange(half, dtype=F32) / half), 4)
    sign = jnp.tile(jnp.concatenate([-jnp.ones(half, F32), jnp.ones(half, F32)]), 2)
    ang = jnp.arange(S, dtype=F32)[:, None] * inv_freq[None, :]
    return jnp.cos(ang), jnp.sin(ang) * sign[None, :]


def _to_blocks(g, col_sharded):
    if col_sharded:
        return g.reshape(g.shape[0], 8, g.shape[1] // 8).transpose(1, 0, 2)
    return g.reshape(8, g.shape[0] // 8, g.shape[1])


def _from_blocks(t, col_sharded):
    if col_sharded:
        return t.transpose(1, 0, 2).reshape(t.shape[1], 8 * t.shape[2])
    return t.reshape(8 * t.shape[1], t.shape[2])


class _NoComm:
    def __init__(self, late_weights):
        self._late = late_weights
        self.grads = {}

    def start_tie(self):
        return jnp.zeros((8, 128), F32)

    def late_weights(self, after):
        return self._late

    def small_start(self, loss_row, grads):
        self.small = (loss_row, grads)
        return jnp.zeros((8, 128), F32)

    def rs_start(self, key, gblocks, after=()):
        self.grads[key] = gblocks
        return jnp.zeros((8, 128), F32)

    def rs_mid(self, key, after):
        return jnp.zeros((8, 128), F32)


class _FsdpComm:
    def __init__(self, late_shards, col_sharded, after, idx, block):
        self._col, self._idx, self._block, self._rs = col_sharded, idx, block, {}
        n = len(late_shards)
        self._gather, self._token = _exchange_start(
            late_shards, [(8,) + s.shape for s in late_shards], _plan_gather(n), 7 * n, "ag_late_start", (after,))

    def start_tie(self):
        return self._token

    def late_weights(self, after):
        shards, lands = _exchange_wait(self._gather, _plan_gather(len(self._col)), "ag_late_wait", after)
        lands = [lax.dynamic_update_index_in_dim(t, s, self._block, 0) for t, s in zip(lands, shards)]
        return [_from_blocks(t, cs) for t, cs in zip(lands, self._col)]

    def small_start(self, loss_row, grads):
        parts = [loss_row] + [grads[k] for k in SMALL]
        n = len(parts)
        self._small, token = _exchange_start(parts, [(8,) + p.shape for p in parts], _plan_gather(n), 7 * n,
                                             "small_start", (self._token,))
        return token

    def small_finish(self, ws, ms, vs, after):
        n = 1 + len(SMALL)
        parts, lands = _exchange_wait(self._small, _plan_gather(n), "small_wait", after)
        return _small_sum_adam(jnp.reshape(self._block, (1,)).astype(jnp.int32), parts, lands, ws, ms, vs)

    def rs_start(self, key, gblocks, after=()):
        n = len(gblocks)
        d2d, token = _exchange_start(gblocks, [(4,) + g.shape[1:] for g in gblocks], _plan_d2d(n), 4 * n,
                                     "rs_%s_d2d_start" % key, (self._token,) + tuple(after))
        self._rs[key] = dict(n=n, d2d=d2d)
        return token

    def rs_mid(self, key, after):
        st = self._rs[key]
        gblocks, from_sib = _exchange_wait(st["d2d"], _plan_d2d(st["n"]), "rs_%s_d2d_wait" % key, after)
        halves = [_rs_add(g, r, self._idx, "rs_add_%s_%d" % (key, k)) for k, (g, r) in enumerate(zip(gblocks, from_sib))]
        st["own"] = [own for own, _ in halves]
        sends = [send for _, send in halves]
        st["ici"], token = _exchange_start(sends, [t.shape for t in sends], _plan_ici(st["n"]), 3 * st["n"],
                                           "rs_%s_ici_start" % key, (self._token,))
        return token

    def rs_end(self, key, after):
        st = self._rs[key]
        return st["own"], _exchange_wait(st["ici"], _plan_ici(st["n"]), "rs_%s_ici_wait" % key, after)[1]


def _local_step(x, tgt, h1, cos_t, sin_t, win_t, comm, g1, g2, g3, g4, ln_g, ln_b, w_sp, b_sp):
    b_sp_t = b_sp.T
    w_sp_t = w_sp.transpose(0, 2, 1)

    *qkvs, puz, gates = _proj_all(h1, win_t, cos_t, sin_t, comm.start_tie())
    fwd = [_attn_fwd(t, "attn_fwd_d%d" % d) for t, d in zip(qkvs, DILS)]
    ya, *lses = _attn_combine([o for o, _ in fwd], [l for _, l in fwd])
    yg = _gmlp_fwd(puz, ln_g, ln_b, w_sp, b_sp_t)
    wba, wbg, wout, wmi, wmo = comm.late_weights(after=(ya, yg, gates))
    merged, y, x2, h3 = _merge_fwd(ya, yg, gates, wba, wbg, wout, x, g2, g3)
    a, dy2, dout, loss_row, dg4 = _mlp_fwd(h3, wmi, wmo, x2, tgt, g4)

    dpre, dx2, dy, dg3, dg2 = _mlp_bwd(dy2, a, wmo, wmi, x2, y, dout, g2, g3)
    dwmo = _mm_tn(a, dy2, "dw_mlp_out", square_a=True)
    dwmi = _mm_tn(h3, dpre, "dw_mlp_in", col_blocks=8)
    tie = comm.rs_start("mlp", [dwmi, _to_blocks(dwmo, False)])
    dgates, da, db, dyg, *rest = _outproj_bwd(dy, wout, ya, yg, gates, wba, wbg, tie)
    dyas, dsums = rest[:3], rest[3:]
    tie = comm.rs_mid("mlp", after=(dyg,))
    dpuz, dwsp, dbs, dlng, dlnb = _gmlp_bwd(puz, dyg, ln_g + tie[0, 0], ln_b, w_sp, w_sp_t, b_sp_t)
    dqkvs = [_attn_bwd(qkvs[g], dyas[g], dsums[g], lses[g], "attn_bwd_d%d" % d) for g, d in enumerate(DILS)]
    grad_x, dg1, dproj = _inproj_bwd(dqkvs, dpuz, dgates, win_t, cos_t, sin_t, x, dx2, g1)
    small = dict(g1=dg1, g2=dg2, g3=dg3, g4=dg4, ln_g=dlng, ln_b=dlnb, b_sp=dbs.reshape(4, CHUNK),
                 w_sp=dwsp.reshape(4 * CHUNK, CHUNK))
    tie = comm.small_start(loss_row, small)
    dwin_t = _mm_tn(dproj, h1, "dw_in", tie=tie)
    tie = comm.rs_start("win", [_to_blocks(dwin_t, False)])
    dwout = _mm_tn(merged, dy, "dw_out", tie=tie)
    dwba = _mm_tn(ya, da, "dw_branch_attn", tie=tie, col_blocks=8)
    dwbg = _mm_tn(yg, db, "dw_branch_gmlp", tie=tie, col_blocks=8)
    tie = comm.rs_start("mid", [dwba, dwbg, _to_blocks(dwout, False)])
    tie = comm.rs_mid("win", after=(tie,))
    return grad_x, tie


def kernel(x, norm_pre_mix, w_in, w_spatial, b_spatial, ln_v_gain, ln_v_bias, w_branch_attn, w_branch_gmlp, w_out, norm_post_mix, norm_pre_mlp, w_mlp_in, w_mlp_out, norm_post_mlp, loss_target, m_norm_pre_mix, m_w_in, m_w_spatial, m_b_spatial, m_ln_v_gain, m_ln_v_bias, m_w_branch_attn, m_w_branch_gmlp, m_w_out, m_norm_post_mix, m_norm_pre_mlp, m_w_mlp_in, m_w_mlp_out, m_norm_post_mlp, v_norm_pre_mix, v_w_in, v_w_spatial, v_b_spatial, v_ln_v_gain, v_ln_v_bias, v_w_branch_attn, v_w_branch_gmlp, v_w_out, v_norm_post_mix, v_norm_pre_mlp, v_w_mlp_in, v_w_mlp_out, v_norm_post_mlp):
    mx, my, mc = _mesh_pos()
    rel = [(0, 0), (1, 0), (0, 1), (1, 1)]
    chip_of = [2 * (mx ^ dx) + (my ^ dy) for dx, dy in rel]
    idx = jnp.stack([2 * k + mc for k in chip_of] + chip_of).astype(jnp.int32)

    w_in_t, m_w_in_t, v_w_in_t = (t[0].T for t in (w_in, m_w_in, v_w_in))
    shard = w_in_t.astype(BF)
    gather, token = _exchange_start([shard], [(8,) + shard.shape], _plan_gather_near, 4, "ag_win_start", ())
    h1 = _rms_fwd(x[0], norm_pre_mix + token[0, 0], "rms_pre_mix")
    cos_t, sin_t = _rope_tables(x.shape[1])
    gather = _exchange_forward(gather, _plan_gather_near, _plan_gather_pass(True), 3, "ag_win_pass", (h1, cos_t, sin_t))
    (shard,), (g_win,) = _exchange_wait(gather, _plan_gather_pass(False), "ag_win_wait", ())
    g_win = lax.dynamic_update_index_in_dim(g_win, shard, 4 * mx + 2 * my + mc, 0)
    win_t = _from_blocks(g_win, False)
    late = [w_branch_attn[0], w_branch_gmlp[0], w_out[0], w_mlp_in[0], w_mlp_out[0]]
    comm = _FsdpComm([w.astype(BF) for w in late], [True, True, False, True, False], g_win, idx, 4 * mx + 2 * my + mc)

    grad_x, tie = _local_step(
        x[0], loss_target[0], h1, cos_t, sin_t, win_t, comm,
        norm_pre_mix, norm_post_mix, norm_pre_mlp, norm_post_mlp, ln_v_gain, ln_v_bias, w_spatial[0], b_spatial[0])

    flat = lambda t: t.reshape(-1, t.shape[-1])
    small_w = dict(zip(SMALL, map(flat, (norm_pre_mix, norm_post_mix, norm_pre_mlp, norm_post_mlp, ln_v_gain, ln_v_bias,
                                         b_spatial, w_spatial))))
    small_m = dict(zip(SMALL, map(flat, (m_norm_pre_mix, m_norm_post_mix, m_norm_pre_mlp, m_norm_post_mlp, m_ln_v_gain,
                                         m_ln_v_bias, m_b_spatial, m_w_spatial))))
    small_v = dict(zip(SMALL, map(flat, (v_norm_pre_mix, v_norm_post_mix, v_norm_pre_mlp, v_norm_post_mlp, v_ln_v_gain,
                                         v_ln_v_bias, v_b_spatial, v_w_spatial))))
    tie = comm.rs_mid("mid", after=(tie,))
    mlp_own, mlp_chips = comm.rs_end("mlp", after=(tie,))
    adam = lambda nm, own, r, w, m, v: _adam_shard(own, r, w[0], m[0], v[0], "adam_" + nm)
    upd = {
        "w_mlp_in": adam("w_mlp_in", mlp_own[0], mlp_chips[0], w_mlp_in, m_w_mlp_in, v_w_mlp_in),
        "w_mlp_out": adam("w_mlp_out", mlp_own[1], mlp_chips[1], w_mlp_out, m_w_mlp_out, v_w_mlp_out),
    }
    loss_out, sg, sd, sm, sv = comm.small_finish(small_w, small_m, small_v, after=(upd["w_mlp_in"][0], upd["w_mlp_out"][0]))
    loss = loss_out[0, 0]
    win_own, win_chips = comm.rs_end("win", after=(loss_out,))
    upd["w_in"] = tuple(t.T for t in _adam_shard(win_own[0], win_chips[0], w_in_t, m_w_in_t, v_w_in_t, "adam_w_in"))
    mid_own, mid_chips = comm.rs_end("mid", after=(upd["w_in"][0],))
    upd["w_branch_attn"] = adam("w_branch_attn", mid_own[0], mid_chips[0], w_branch_attn, m_w_branch_attn, v_w_branch_attn)
    upd["w_branch_gmlp"] = adam("w_branch_gmlp", mid_own[1], mid_chips[1], w_branch_gmlp, m_w_branch_gmlp, v_w_branch_gmlp)
    upd["w_out"] = adam("w_out", mid_own[2], mid_chips[2], w_out, m_w_out, v_w_out)
    order = ["g1", "w_in", "w_sp", "b_sp", "ln_g", "ln_b", "w_branch_attn", "w_branch_gmlp", "w_out", "g2", "g3",
             "w_mlp_in", "w_mlp_out", "g4"]
    small_shape = dict(g1=norm_pre_mix.shape, g2=norm_post_mix.shape, g3=norm_pre_mlp.shape, g4=norm_post_mlp.shape,
                       ln_g=ln_v_gain.shape, ln_b=ln_v_bias.shape, b_sp=b_spatial.shape, w_sp=w_spatial.shape)

    def pick(which):
        return [upd[nm][which][None] if nm in upd else (sg, sd, sm, sv)[which][nm].reshape(small_shape[nm])
                for nm in order]

    return (loss, grad_x[None], *pick(0), *pick(1), *pick(2), *pick(3))
```

```python
import functools
import math

import jax
import jax.numpy as jnp
from jax import lax
from jax.experimental import pallas as pl
from jax.experimental.pallas import tpu as pltpu

D = 1024
HD = 64
NSLOT = 4
GW = NSLOT * HD
DILS = (1, 4, 16)
QB = 128
ATTN_NSUB = 4
PROJ_TM = 1024
MLP_TM_FWD = 512
MLP_TM_BWD = 256
AW = 3 * GW
W_UZ0, W_G0 = 3 * AW, 3 * AW + 1024
GMLP_W = 512
CHUNK = 128
DFF = 4096
EPS = 1e-6
ROPE_THETA = 10000.0
SCALE = HD ** -0.5
NEG = -1e30

ADAM_LR = 0.001
ADAM_B1 = 0.9
ADAM_B2 = 0.999
ADAM_EPS = 1e-08
ADAM_WD = 0.01
ADAM_STEP = 10

BF = jnp.bfloat16
F32 = jnp.float32
MESH = pl.DeviceIdType.MESH
VMEM_LIMIT = 56 * 1024 * 1024


def _cp(sem):
    return pltpu.CompilerParams(dimension_semantics=sem, vmem_limit_bytes=VMEM_LIMIT)


def _dot(a, b):
    return jnp.dot(a, b, preferred_element_type=F32)


def _dot_nt(a, b):
    return lax.dot_general(a, b, (((1,), (1,)), ((), ())), preferred_element_type=F32)


def _dot_tn(a, b):
    return lax.dot_general(a, b, (((0,), (0,)), ((), ())), preferred_element_type=F32)


def _gelu(x):
    return jax.nn.gelu(x, approximate=True)


def _gelu_grad(x):
    k = math.sqrt(2.0 / math.pi)
    t = jnp.tanh(k * (x + 0.044715 * x * x * x))
    return 0.5 * (1.0 + t) + 0.5 * x * (1.0 - t * t) * (k * (1.0 + 3.0 * 0.044715 * x * x))


def _swap_halves(t):
    w = t.shape[1]
    lane = lax.broadcasted_iota(jnp.int32, t.shape, 1)
    first = (lane & (HD - 1)) < (HD // 2)
    return jnp.where(first, pltpu.roll(t, w - HD // 2, 1), pltpu.roll(t, HD // 2, 1))


def _head_mask(hh):
    lane = lax.broadcasted_iota(jnp.int32, (1, GW), 1)
    return jnp.logical_and(lane >= hh * HD, lane < (hh + 1) * HD)


def _rms_stats(xf):
    return lax.rsqrt(jnp.mean(xf * xf, axis=-1, keepdims=True) + EPS)


def _rms_bwd(xf, r, gain, dout):
    n = xf * r
    t = dout * gain
    dx = r * (t - n * jnp.mean(t * n, axis=-1, keepdims=True))
    return dx, jnp.sum(dout * n, axis=0, keepdims=True)


def _rms_fwd(x, gain, name):
    S = x.shape[0]
    tm = min(512, S)

    def body(x_ref, g_ref, h_ref):
        xf = x_ref[...]
        h_ref[...] = (xf * _rms_stats(xf) * g_ref[...]).astype(BF)

    return pl.pallas_call(
        body, name=name, out_shape=jax.ShapeDtypeStruct((S, D), BF), grid=(S // tm,),
        in_specs=[pl.BlockSpec((tm, D), lambda i: (i, 0)), pl.BlockSpec((1, D), lambda i: (0, 0))],
        out_specs=pl.BlockSpec((tm, D), lambda i: (i, 0)), compiler_params=_cp(("parallel",)))(x, gain)


def _dilate_store(val, scr, o_ref, lead, d):
    rows = val.shape[0]
    if d == 1:
        o_ref[lead + (0,)] = val.astype(o_ref.dtype)
        return
    for hf in range(2):
        scr[hf, pl.ds(0, rows), :] = val[:, hf * 128:(hf + 1) * 128]
    for r in range(d):
        for hf in range(2):
            o_ref[lead + (r, slice(None), slice(hf * 128, (hf + 1) * 128))] = (
                scr[hf, pl.ds(r, rows // d, stride=d), :].astype(o_ref.dtype))


def _undilate_load(i_ref, lead, d, scr, rows):
    if d == 1:
        return i_ref[lead + (0,)].astype(F32)
    for r in range(d):
        for hf in range(2):
            scr[hf, pl.ds(r, rows // d, stride=d), :] = (
                i_ref[lead + (r, slice(None), slice(hf * 128, (hf + 1) * 128))].astype(F32))
    return jnp.concatenate([scr[0, pl.ds(0, rows), :], scr[1, pl.ds(0, rows), :]], axis=1)


def _rope_fwd(y, c_ref, s_ref):
    cosv = jnp.concatenate([c_ref[...]] * 2, axis=1)
    sinv = jnp.concatenate([s_ref[...]] * 2, axis=1)
    return y * cosv + _swap_halves(y) * sinv


def _rope_bwd(dy, c_ref, s_ref):
    cosv = jnp.concatenate([c_ref[...]] * 2, axis=1)
    sinv = jnp.concatenate([s_ref[...]] * 2, axis=1)
    return dy * cosv + _swap_halves(dy * sinv)


def _proj_all(h, win_t, cos_t, sin_t, tie):
    S, K = h.shape
    tm = min(PROJ_TM, S)

    def body(h_ref, w_ref, c_ref, s_ref, tie_ref, o0_ref, o1_ref, o2_ref, p_ref, gt_ref, scr):
        hv = h_ref[...]
        col = lambda b: _dot_nt(hv, w_ref[b * GW:(b + 1) * GW, :])
        ys = [col(3 * t) for t in range(3)]
        for g, (o_ref, d) in enumerate(zip((o0_ref, o1_ref, o2_ref), DILS)):
            nxt = [col(3 * t + g + 1) if g < 2 else col(W_UZ0 // GW + t) for t in range(3)]
            for t in range(3):
                y = _rope_fwd(ys[t], c_ref, s_ref) if t < 2 else ys[t]
                _dilate_store(y, scr.at[t], o_ref, (t,), d)
            ys = nxt
        for b in range((W_G0 - W_UZ0) // GW):
            y = ys[b] if b < 3 else col(W_UZ0 // GW + b)
            p_ref[:, b * GW:(b + 1) * GW] = y.astype(BF)
        for b in range(2 * D // GW):
            gt_ref[:, b * GW:(b + 1) * GW] = jax.nn.sigmoid(col(W_G0 // GW + b)).astype(BF)

    row = lambda w: pl.BlockSpec((tm, w), lambda i: (i, 0))
    return pl.pallas_call(
        body, name="proj_all", grid=(S // tm,),
        out_shape=tuple(jax.ShapeDtypeStruct((3, d, S // d, GW), BF) for d in DILS)
        + (jax.ShapeDtypeStruct((S, W_G0 - W_UZ0), BF), jax.ShapeDtypeStruct((S, 2 * D), BF)),
        in_specs=[row(K), _resident(win_t.shape), row(128), row(128), pl.BlockSpec(memory_space=pl.ANY)],
        out_specs=tuple(pl.BlockSpec((3, d, tm // d, GW), lambda i: (0, 0, i, 0)) for d in DILS)
        + (row(W_G0 - W_UZ0), row(2 * D)),
        scratch_shapes=[pltpu.VMEM((3, 2, tm, 128), F32)],
        compiler_params=_cp(("parallel",)))(h, win_t, cos_t, sin_t, tie)


def _band_masks(first_step):
    row = lax.broadcasted_iota(jnp.int32, (QB, 2 * QB), 0)
    col = lax.broadcasted_iota(jnp.int32, (QB, 2 * QB), 1)
    band = jnp.logical_and(col >= row, col <= row + QB)
    return band, jnp.logical_and(band, jnp.logical_or(col >= QB, jnp.logical_not(first_step)))


def _attn_fwd(qkv, name):
    _, d, L, _ = qkv.shape
    nsub = min(ATTN_NSUB, L // QB)
    R = nsub * QB
    nsteps = L // R

    def body(q_ref, kp_ref, kc_ref, vp_ref, vc_ref, o_ref, lse_ref):
        i = pl.program_id(1)
        band, band_first = _band_masks(i == 0)
        kfull = jnp.concatenate([kp_ref[...], kc_ref[...]], axis=0)
        vfull = jnp.concatenate([vp_ref[...], vc_ref[...]], axis=0)
        chains = [(sb, hh) for sb in range(nsub) for hh in range(NSLOT)]
        win = lambda t, sb: t[sb * QB:(sb + 2) * QB]
        scores = []
        for sb, hh in chains:
            qh = jnp.where(_head_mask(hh), q_ref[sb * QB:(sb + 1) * QB, :], 0)
            scores.append(_dot_nt(qh, win(kfull, sb)))
        soft = []
        for (sb, hh), sc in zip(chains, scores):
            sc = jnp.where(band_first if sb == 0 else band, sc * SCALE, NEG)
            m = jnp.max(sc, axis=1, keepdims=True)
            p = jnp.exp(sc - m)
            den = jnp.sum(p, axis=1, keepdims=True)
            soft.append((p.astype(BF), den, m + jnp.log(den)))
        accs = [_dot(p, win(vfull, sb)) for (sb, hh), (p, _, _) in zip(chains, soft)]
        for sb in range(nsub):
            o = jnp.zeros((QB, GW), F32)
            lse = jnp.zeros((QB, GW), F32)
            for hh in range(NSLOT):
                hm = _head_mask(hh)
                _, den, lrow = soft[sb * NSLOT + hh]
                o = o + jnp.where(hm, accs[sb * NSLOT + hh] / den, 0.0)
                lse = lse + jnp.where(hm, lrow, 0.0)
            o_ref[sb * QB:(sb + 1) * QB, :] = o
            lse_ref[sb * QB:(sb + 1) * QB, :] = lse

    prev = lambda i: jnp.maximum(i * nsub - 1, 0)
    cur = lambda t: pl.BlockSpec((None, None, R, GW), lambda r, i: (t, r, i, 0))
    prv = lambda t: pl.BlockSpec((None, None, QB, GW), lambda r, i: (t, r, prev(i), 0))
    out = pl.BlockSpec((None, R, GW), lambda r, i: (r, i, 0))
    return pl.pallas_call(
        body, name=name, grid=(d, nsteps),
        out_shape=(jax.ShapeDtypeStruct((d, L, GW), F32), jax.ShapeDtypeStruct((d, L, GW), F32)),
        in_specs=[cur(0), prv(1), cur(1), prv(2), cur(2)],
        out_specs=(out, out), compiler_params=_cp(("parallel", "arbitrary")))(qkv, qkv, qkv, qkv, qkv)


def _attn_combine(os_, lses):
    S = os_[0].shape[1]
    tm = min(1024, S)

    def body(o0, o1, o2, l0, l1, l2, y_ref, j0, j1, j2, scr):
        os_nat = [_undilate_load(o, (), d, scr, tm) for o, d in zip((o0, o1, o2), DILS)]
        a, b, c = [_undilate_load(l, (), d, scr, tm) for l, d in zip((l0, l1, l2), DILS)]
        m = jnp.maximum(jnp.maximum(a, b), c)
        wa, wb, wc = jnp.exp(a - m), jnp.exp(b - m), jnp.exp(c - m)
        den = wa + wb + wc
        y_ref[...] = ((wa * os_nat[0] + wb * os_nat[1] + wc * os_nat[2]) / den).astype(BF)
        lse = m + jnp.log(den)
        for j_ref, d in zip((j0, j1, j2), DILS):
            _dilate_store(lse, scr, j_ref, (), d)

    dil = lambda d: pl.BlockSpec((d, tm // d, GW), lambda i: (0, i, 0))
    dshape = lambda d: jax.ShapeDtypeStruct((d, S // d, GW), F32)
    return pl.pallas_call(
        body, name="attn_combine", grid=(S // tm,),
        out_shape=(jax.ShapeDtypeStruct((S, GW), BF),) + tuple(dshape(d) for d in DILS),
        in_specs=[dil(d) for d in DILS] * 2,
        out_specs=(pl.BlockSpec((tm, GW), lambda i: (i, 0)),) + tuple(dil(d) for d in DILS),
        scratch_shapes=[pltpu.VMEM((2, tm, 128), F32)],
        compiler_params=_cp(("parallel",)))(*os_, *lses)


def _tril(upper=False):
    row = lax.broadcasted_iota(jnp.int32, (CHUNK, CHUNK), 0)
    col = lax.broadcasted_iota(jnp.int32, (CHUNK, CHUNK), 1)
    return row <= col if upper else col <= row


def _ln_fwd(z, gain, bias):
    mu = jnp.mean(z, axis=-1, keepdims=True)
    zc = z - mu
    rstd = lax.rsqrt(jnp.mean(zc * zc, axis=-1, keepdims=True) + EPS)
    zhat = zc * rstd
    return zhat, rstd, zhat * gain + bias


def _gmlp_fwd(puz, ln_g, ln_b, w_sp, b_sp_t):
    S = puz.shape[0]
    tm = min(512, S)
    nch = tm // CHUNK

    def body(p_ref, g_ref, b_ref, w_ref, bt_ref, o_ref):
        tril = _tril()
        ws = [jnp.where(tril, w_ref[gg], 0.0).astype(BF) for gg in range(4)]
        for ch in range(nch):
            rows = slice(ch * CHUNK, (ch + 1) * CHUNK)
            z = _gelu(p_ref[rows, GMLP_W:].astype(F32))
            _, _, zn = _ln_fwd(z, g_ref[...], b_ref[...])
            zn = zn.astype(BF)
            for gg in range(4):
                cols = slice(gg * CHUNK, (gg + 1) * CHUNK)
                sz = _dot(ws[gg], zn[:, cols]) + bt_ref[:, gg:gg + 1]
                u = _gelu(p_ref[rows, cols].astype(F32))
                o_ref[rows, cols] = (u * sz).astype(BF)

    return pl.pallas_call(
        body, name="gmlp_fwd", out_shape=jax.ShapeDtypeStruct((S, GMLP_W), BF), grid=(S // tm,),
        in_specs=[pl.BlockSpec((tm, 2 * GMLP_W), lambda i: (i, 0)),
                  pl.BlockSpec((1, GMLP_W), lambda i: (0, 0)), pl.BlockSpec((1, GMLP_W), lambda i: (0, 0)),
                  pl.BlockSpec((4, CHUNK, CHUNK), lambda i: (0, 0, 0)), pl.BlockSpec((CHUNK, 4), lambda i: (0, 0))],
        out_specs=pl.BlockSpec((tm, GMLP_W), lambda i: (i, 0)),
        compiler_params=_cp(("parallel",)))(puz, ln_g, ln_b, w_sp, b_sp_t)


def _merge_fwd(ya, yg, gates, wba, wbg, wout, x, g2, g3):
    S = x.shape[0]
    tm = min(512, S)

    def body(ya_ref, yg_ref, gt_ref, wba_ref, wbg_ref, wo_ref, x_ref, g2_ref, g3_ref,
             mg_ref, y_ref, x2_ref, h3_ref):
        a = _dot(ya_ref[...], wba_ref[...])
        b = _dot(yg_ref[...], wbg_ref[...])
        merged = (gt_ref[:, :D].astype(F32) * a + gt_ref[:, D:].astype(F32) * b).astype(BF)
        mg_ref[...] = merged
        y = _dot(merged, wo_ref[...])
        y_ref[...] = y
        x2 = x_ref[...] + y * _rms_stats(y) * g2_ref[...]
        x2_ref[...] = x2
        h3_ref[...] = (x2 * _rms_stats(x2) * g3_ref[...]).astype(BF)

    row = lambda w: pl.BlockSpec((tm, w), lambda i: (i, 0))
    full = lambda s: pl.BlockSpec(s, lambda i: (0, 0))
    return pl.pallas_call(
        body, name="merge_fwd", grid=(S // tm,),
        out_shape=(jax.ShapeDtypeStruct((S, D), BF), jax.ShapeDtypeStruct((S, D), F32), jax.ShapeDtypeStruct((S, D), F32),
                   jax.ShapeDtypeStruct((S, D), BF)),
        in_specs=[row(GW), row(GMLP_W), row(2 * D), full((GW, D)), full((GMLP_W, D)), full((D, D)), row(D),
                  full((1, D)), full((1, D))],
        out_specs=(row(D), row(D), row(D), row(D)),
        compiler_params=_cp(("parallel",)))(ya, yg, gates, wba, wbg, wout, x, g2, g3)


def _resident(shape):
    return pl.BlockSpec(shape, lambda i: (0,) * len(shape), pipeline_mode=pl.Buffered(1))


def _mlp_fwd(h3, wmi, wmo, x2, tgt, g4):
    S = x2.shape[0]
    tm = min(MLP_TM_FWD, S)

    def body(h_ref, wi_ref, wo_ref, x2_ref, t_ref, g4_ref, a_ref, dy2_ref, dout_ref, loss_ref, dg4_ref):
        @pl.when(pl.program_id(0) == 0)
        def _():
            loss_ref[...] = jnp.zeros_like(loss_ref)
            dg4_ref[...] = jnp.zeros_like(dg4_ref)

        halves = [slice(hh * (tm // 2), (hh + 1) * (tm // 2)) for hh in range(2)]
        acts = []
        for rows in halves:
            a = jnp.maximum(_dot(h_ref[rows, :], wi_ref[...]), 0.0)
            a_ref[rows, :] = a.astype(BF)
            acts.append((a * a).astype(BF))
        y2s = [_dot(a2, wo_ref[...]) for a2 in acts]
        lane = lax.broadcasted_iota(jnp.int32, (1, 128), 1)
        for rows, y2 in zip(halves, y2s):
            r = _rms_stats(y2)
            out = x2_ref[rows, :] + y2 * r * g4_ref[...]
            err = out - t_ref[rows, :]
            tot = jnp.sum(jnp.sum(err * err, axis=1, keepdims=True), axis=0, keepdims=True) * (0.5 / D)
            loss_ref[...] += jnp.where(lane == 0, tot, 0.0)
            dout = err * (1.0 / D)
            dout_ref[rows, :] = dout
            dy2, dg = _rms_bwd(y2, r, g4_ref[...], dout)
            dy2_ref[rows, :] = dy2.astype(BF)
            dg4_ref[...] += dg

    row = pl.BlockSpec((tm, D), lambda i: (i, 0))
    return pl.pallas_call(
        body, name="mlp_fwd", grid=(S // tm,),
        out_shape=(jax.ShapeDtypeStruct((S, DFF), BF), jax.ShapeDtypeStruct((S, D), BF), jax.ShapeDtypeStruct((S, D), F32),
                   jax.ShapeDtypeStruct((1, 128), F32), jax.ShapeDtypeStruct((1, D), F32)),
        in_specs=[row, _resident((D, DFF)), _resident((DFF, D)), row, row, pl.BlockSpec((1, D), lambda i: (0, 0))],
        out_specs=(pl.BlockSpec((tm, DFF), lambda i: (i, 0)), row, row,
                   pl.BlockSpec((1, 128), lambda i: (0, 0)), pl.BlockSpec((1, D), lambda i: (0, 0))),
        compiler_params=_cp(("arbitrary",)))(h3, wmi, wmo, x2, tgt, g4)


def _mlp_bwd(dy2, a, wmo, wmi, x2, y, dout, g2, g3):
    S = x2.shape[0]
    tm = min(MLP_TM_BWD, S)

    def body(dy2_ref, a_ref, wo_ref, wi_ref, x2_ref, y_ref, dout_ref, g2_ref, g3_ref,
             dpre_ref, dx2_ref, dy_ref, dg3_ref, dg2_ref):
        @pl.when(pl.program_id(0) == 0)
        def _():
            dg3_ref[...] = jnp.zeros_like(dg3_ref)
            dg2_ref[...] = jnp.zeros_like(dg2_ref)

        da2 = _dot_nt(dy2_ref[...], wo_ref[...])
        dpre = (2.0 * a_ref[...].astype(F32) * da2).astype(BF)
        dpre_ref[...] = dpre
        dh3 = _dot_nt(dpre, wi_ref[...])
        x2 = x2_ref[...]
        dx3, dg3 = _rms_bwd(x2, _rms_stats(x2), g3_ref[...], dh3)
        dx2 = dout_ref[...] + dx3
        dx2_ref[...] = dx2
        dg3_ref[...] += dg3
        yv = y_ref[...]
        dy, dg2 = _rms_bwd(yv, _rms_stats(yv), g2_ref[...], dx2)
        dy_ref[...] = dy.astype(BF)
        dg2_ref[...] += dg2

    row = pl.BlockSpec((tm, D), lambda i: (i, 0))
    wide = pl.BlockSpec((tm, DFF), lambda i: (i, 0))
    vec = pl.BlockSpec((1, D), lambda i: (0, 0))
    return pl.pallas_call(
        body, name="mlp_bwd", grid=(S // tm,),
        out_shape=(jax.ShapeDtypeStruct((S, DFF), BF), jax.ShapeDtypeStruct((S, D), F32), jax.ShapeDtypeStruct((S, D), BF),
                   jax.ShapeDtypeStruct((1, D), F32), jax.ShapeDtypeStruct((1, D), F32)),
        in_specs=[row, wide, _resident((DFF, D)), _resident((D, DFF)), row, row, row, vec, vec],
        out_specs=(wide, row, row, vec, vec),
        compiler_params=_cp(("arbitrary",)))(dy2, a, wmo, wmi, x2, y, dout, g2, g3)


def _mm_tn(a, b, name, square_a=False, tm=1024, tn=1024, tk=2048, tie=None, col_blocks=None):
    S, M = a.shape
    N = b.shape[1]
    tk = min(tk, S)
    tm = max(t for t in range(128, min(tm, M) + 1, 128) if M % t == 0)
    tn = max(t for t in range(128, min(tn, N) + 1, 128) if N % t == 0)
    cb = N // col_blocks if col_blocks else tn
    assert M % tm == 0 and S % tk == 0 and tn % cb == 0
    nk = S // tk
    ties = () if tie is None else (tie,)

    def body(a_ref, b_ref, *rest):
        o_ref = rest[-1]
        k = pl.program_id(2)
        av = a_ref[...]
        if square_a:
            av = av * av
        part = _dot_tn(av, b_ref[...])
        if col_blocks:
            part = jnp.stack([part[:, t * cb:(t + 1) * cb] for t in range(tn // cb)])

        @pl.when(k == 0)
        def _():
            o_ref[...] = part

        @pl.when(k > 0)
        def _():
            o_ref[...] += part

    if col_blocks:
        out_shape, out_spec = (col_blocks, M, cb), pl.BlockSpec((tn // cb, tm, cb), lambda i, j, k: (j, i, 0))
    else:
        out_shape, out_spec = (M, N), pl.BlockSpec((tm, tn), lambda i, j, k: (i, j))
    return pl.pallas_call(
        body, name=name, out_shape=jax.ShapeDtypeStruct(out_shape, F32), grid=(M // tm, N // tn, nk),
        in_specs=[pl.BlockSpec((tk, tm), lambda i, j, k: (k, i)), pl.BlockSpec((tk, tn), lambda i, j, k: (k, j))]
        + [pl.BlockSpec(memory_space=pl.ANY)] * len(ties),
        out_specs=out_spec,
        compiler_params=_cp(("parallel", "parallel", "arbitrary")))(a, b, *ties)


def _outproj_bwd(dy, wout, ya, yg, gates, wba, wbg, tie):
    S = dy.shape[0]
    tm = min(512, S)

    def body(dy_ref, wo_ref, ya_ref, yg_ref, gt_ref, wba_ref, wbg_ref, tie_ref,
             dgt_ref, da_ref, db_ref, dyg_ref, e0, e1, e2, s0, s1, s2, scr):
        dm = _dot_nt(dy_ref[...], wo_ref[...])
        ga, gb = gt_ref[:, :D].astype(F32), gt_ref[:, D:].astype(F32)
        dgt_ref[:, :D] = (dm * _dot(ya_ref[...], wba_ref[...]) * ga * (1.0 - ga)).astype(BF)
        dgt_ref[:, D:] = (dm * _dot(yg_ref[...], wbg_ref[...]) * gb * (1.0 - gb)).astype(BF)
        da = (dm * ga).astype(BF)
        db = (dm * gb).astype(BF)
        da_ref[...] = da
        db_ref[...] = db
        dyg_ref[...] = _dot_nt(db, wbg_ref[...]).astype(BF)
        dya = _dot_nt(da, wba_ref[...]).astype(BF).astype(F32)
        dyy = dya * ya_ref[...].astype(F32)
        dsum = jnp.zeros((tm, GW), F32)
        for hh in range(NSLOT):
            hm = _head_mask(hh)
            dsum = dsum + jnp.where(hm, jnp.sum(jnp.where(hm, dyy, 0.0), axis=1, keepdims=True), 0.0)
        for e_ref, s_ref, d in zip((e0, e1, e2), (s0, s1, s2), DILS):
            _dilate_store(dya, scr, e_ref, (), d)
            _dilate_store(dsum, scr, s_ref, (), d)

    row = lambda w: pl.BlockSpec((tm, w), lambda i: (i, 0))
    full = lambda s: pl.BlockSpec(s, lambda i: (0, 0))
    dil = lambda d: pl.BlockSpec((d, tm // d, GW), lambda i: (0, i, 0))
    dshape = lambda d, t: jax.ShapeDtypeStruct((d, S // d, GW), t)
    return pl.pallas_call(
        body, name="outproj_bwd", grid=(S // tm,),
        out_shape=(jax.ShapeDtypeStruct((S, 2 * D), BF), jax.ShapeDtypeStruct((S, D), BF), jax.ShapeDtypeStruct((S, D), BF),
                   jax.ShapeDtypeStruct((S, GMLP_W), BF)) + tuple(dshape(d, BF) for d in DILS)
        + tuple(dshape(d, F32) for d in DILS),
        in_specs=[row(D), full((D, D)), row(GW), row(GMLP_W), row(2 * D), full((GW, D)), full((GMLP_W, D)),
                  pl.BlockSpec(memory_space=pl.ANY)],
        out_specs=(row(2 * D), row(D), row(D), row(GMLP_W)) + tuple(dil(d) for d in DILS) * 2,
        scratch_shapes=[pltpu.VMEM((2, tm, 128), F32)],
        compiler_params=_cp(("parallel",)))(dy, wout, ya, yg, gates, wba, wbg, tie)


def _gmlp_bwd(puz, dyg, ln_g, ln_b, w_sp, w_sp_t, b_sp_t):
    S = puz.shape[0]
    tm = min(512, S)
    nch = tm // CHUNK

    def body(p_ref, dy_ref, g_ref, b_ref, w_ref, wt_ref, bt_ref,
             dp_ref, dw_ref, dbs_ref, dg_ref, dbias_ref, dbacc_ref):
        i = pl.program_id(0)

        @pl.when(i == 0)
        def _():
            dw_ref[...] = jnp.zeros_like(dw_ref)
            dbacc_ref[...] = jnp.zeros_like(dbacc_ref)
            dg_ref[...] = jnp.zeros_like(dg_ref)
            dbias_ref[...] = jnp.zeros_like(dbias_ref)

        tril = _tril()
        ws = [jnp.where(tril, w_ref[gg], 0.0).astype(BF) for gg in range(4)]
        triu = _tril(upper=True)
        wts = [jnp.where(triu, wt_ref[gg], 0.0).astype(BF) for gg in range(4)]
        gain = g_ref[...]
        for ch in range(nch):
            rows = slice(ch * CHUNK, (ch + 1) * CHUNK)
            pz = p_ref[rows, GMLP_W:].astype(F32)
            z = _gelu(pz)
            zhat, rstd, zn = _ln_fwd(z, gain, b_ref[...])
            znb = zn.astype(BF)
            dzn_parts = []
            for gg in range(4):
                cols = slice(gg * CHUNK, (gg + 1) * CHUNK)
                pu = p_ref[rows, cols].astype(F32)
                u = _gelu(pu)
                sz = _dot(ws[gg], znb[:, cols]) + bt_ref[:, gg:gg + 1]
                dyv = dy_ref[rows, cols].astype(F32)
                dp_ref[rows, cols] = (dyv * sz * _gelu_grad(pu)).astype(BF)
                dsz = dyv * u
                dbacc_ref[gg] += dsz
                dszb = dsz.astype(BF)
                dw_ref[gg] += _dot_nt(dszb, znb[:, cols])
                dzn_parts.append(_dot(wts[gg], dszb))
            dzn = jnp.concatenate(dzn_parts, axis=1)
            dg_ref[...] += jnp.sum(dzn * zhat, axis=0, keepdims=True)
            dbias_ref[...] += jnp.sum(dzn, axis=0, keepdims=True)
            dzh = dzn * gain
            dz = rstd * (dzh - jnp.mean(dzh, axis=-1, keepdims=True)
                         - zhat * jnp.mean(dzh * zhat, axis=-1, keepdims=True))
            dp_ref[rows, GMLP_W:] = (dz * _gelu_grad(pz)).astype(BF)

        @pl.when(i == pl.num_programs(0) - 1)
        def _():
            for gg in range(4):
                dw_ref[gg] = jnp.where(tril, dw_ref[gg], 0.0)
                dbs_ref[gg] = jnp.sum(dbacc_ref[gg], axis=1, keepdims=True)

    full2 = lambda s: pl.BlockSpec(s, lambda i: (0, 0))
    full3 = lambda s: pl.BlockSpec(s, lambda i: (0, 0, 0))
    return pl.pallas_call(
        body, name="gmlp_bwd", grid=(S // tm,),
        out_shape=(jax.ShapeDtypeStruct((S, 2 * GMLP_W), BF), jax.ShapeDtypeStruct((4, CHUNK, CHUNK), F32),
                   jax.ShapeDtypeStruct((4, CHUNK, 1), F32), jax.ShapeDtypeStruct((1, GMLP_W), F32),
                   jax.ShapeDtypeStruct((1, GMLP_W), F32)),
        in_specs=[pl.BlockSpec((tm, 2 * GMLP_W), lambda i: (i, 0)), pl.BlockSpec((tm, GMLP_W), lambda i: (i, 0)),
                  full2((1, GMLP_W)), full2((1, GMLP_W)), full3((4, CHUNK, CHUNK)), full3((4, CHUNK, CHUNK)),
                  full2((CHUNK, 4))],
        out_specs=(pl.BlockSpec((tm, 2 * GMLP_W), lambda i: (i, 0)), full3((4, CHUNK, CHUNK)), full3((4, CHUNK, 1)),
                   full2((1, GMLP_W)), full2((1, GMLP_W))),
        scratch_shapes=[pltpu.VMEM((4, CHUNK, CHUNK), F32)],
        compiler_params=_cp(("arbitrary",)))(puz, dyg, ln_g, ln_b, w_sp, w_sp_t, b_sp_t)


def _attn_bwd(qkv, dya, dsums, lse, name):
    _, d, L, _ = qkv.shape
    nsub = min(ATTN_NSUB, L // QB)
    R = nsub * QB
    nsteps = L // R

    def body(q_ref, qn_ref, kp_ref, kc_ref, vp_ref, vc_ref, dy_ref, dyn_ref, e_ref, en_ref, l_ref, ln_ref, o_ref):
        i = pl.program_id(1)
        band, band_first = _band_masks(i == 0)
        row = lax.broadcasted_iota(jnp.int32, (QB, QB), 0)
        col = lax.broadcasted_iota(jnp.int32, (QB, QB), 1)
        mask_next = jnp.logical_and(col >= row, i < nsteps - 1)
        kc, vc = kc_ref[...], vc_ref[...]
        kfull = jnp.concatenate([kp_ref[...], kc], axis=0)
        vfull = jnp.concatenate([vp_ref[...], vc], axis=0)
        k_last, v_last = kc[(nsub - 1) * QB:], vc[(nsub - 1) * QB:]
        q_ext = jnp.concatenate([q_ref[...], qn_ref[...]], axis=0)
        dy_ext = jnp.concatenate([dy_ref[...], dyn_ref[...]], axis=0)
        esum, esum_n, lse, lse_n = e_ref[...], en_ref[...], l_ref[...], ln_ref[...]
        win = lambda t, sb: t[sb * QB:(sb + 2) * QB]
        blk = lambda t, sb: t[sb * QB:(sb + 1) * QB]
        hms = [_head_mask(hh) for hh in range(NSLOT)]
        q_hs = [jnp.where(hm, q_ext, 0) for hm in hms]
        dy_hs = [jnp.where(hm, dy_ext, 0) for hm in hms]
        raw = []
        for hh in range(NSLOT):
            tiles = [(_dot_nt(blk(q_hs[hh], sb), win(kfull, sb)), _dot_nt(blk(dy_hs[hh], sb), win(vfull, sb)))
                     for sb in range(nsub)]
            tiles.append((_dot_nt(blk(q_hs[hh], nsub), k_last), _dot_nt(blk(dy_hs[hh], nsub), v_last)))
            raw.append(tiles)
        ps, dss = [], []
        for hh in range(NSLOT):
            rowstat = lambda t: jnp.max(jnp.where(hms[hh], t, -jnp.inf), axis=1, keepdims=True)
            p_h, ds_h = [], []
            for sb in range(nsub + 1):
                sc, dp = raw[hh][sb]
                if sb < nsub:
                    msk, lrow, erow = (band_first if sb == 0 else band), rowstat(blk(lse, sb)), rowstat(blk(esum, sb))
                else:
                    msk, lrow, erow = mask_next, rowstat(lse_n), rowstat(esum_n)
                p = jnp.where(msk, jnp.exp(sc * SCALE - lrow), 0.0)
                p_h.append(p.astype(BF))
                ds_h.append((p * (dp - erow)).astype(BF))
            ps.append(p_h)
            dss.append(ds_h)
        dq = [jnp.zeros((QB, GW), F32) for _ in range(nsub)]
        dk = [jnp.zeros((QB, GW), F32) for _ in range(nsub)]
        dv = [jnp.zeros((QB, GW), F32) for _ in range(nsub)]
        for hh in range(NSLOT):
            for sb in range(nsub):
                dq[sb] = dq[sb] + jnp.where(hms[hh], _dot(dss[hh][sb], win(kfull, sb)), 0.0)
                nxt = lambda t: t[sb + 1][:, :QB] if sb + 1 < nsub else t[nsub]
                dk[sb] = dk[sb] + _dot_tn(jnp.concatenate([dss[hh][sb][:, QB:], nxt(dss[hh])], axis=0), win(q_hs[hh], sb))
                dv[sb] = dv[sb] + _dot_tn(jnp.concatenate([ps[hh][sb][:, QB:], nxt(ps[hh])], axis=0), win(dy_hs[hh], sb))
        for sb in range(nsub):
            rows = slice(sb * QB, (sb + 1) * QB)
            o_ref[0, rows, :] = (dq[sb] * SCALE).astype(BF)
            o_ref[1, rows, :] = (dk[sb] * SCALE).astype(BF)
            o_ref[2, rows, :] = dv[sb].astype(BF)

    prev = lambda i: jnp.maximum(i * nsub - 1, 0)
    nxt = lambda i: jnp.minimum((i + 1) * nsub, L // QB - 1)
    cur4 = lambda t: pl.BlockSpec((None, None, R, GW), lambda r, i: (t, r, i, 0))
    prv4 = lambda t: pl.BlockSpec((None, None, QB, GW), lambda r, i: (t, r, prev(i), 0))
    nxt4 = lambda t: pl.BlockSpec((None, None, QB, GW), lambda r, i: (t, r, nxt(i), 0))
    cur3 = pl.BlockSpec((None, R, GW), lambda r, i: (r, i, 0))
    nxt3 = pl.BlockSpec((None, QB, GW), lambda r, i: (r, nxt(i), 0))
    return pl.pallas_call(
        body, name=name, grid=(d, nsteps), out_shape=jax.ShapeDtypeStruct((3, d, L, GW), BF),
        in_specs=[cur4(0), nxt4(0), prv4(1), cur4(1), prv4(2), cur4(2), cur3, nxt3, cur3, nxt3, cur3, nxt3],
        out_specs=pl.BlockSpec((3, None, R, GW), lambda r, i: (0, r, i, 0)),
        compiler_params=_cp(("parallel", "arbitrary")))(qkv, qkv, qkv, qkv, qkv, qkv, dya, dya, dsums, dsums, lse, lse)


def _inproj_bwd(dqkvs, dpuz, dgates, win_t, cos_t, sin_t, x, dx2, g1):
    S = x.shape[0]
    tm = min(512, S)

    def body(d0_ref, d1_ref, d2_ref, dp_ref, dg_ref, w_ref, c_ref, s_ref,
             x_ref, dx2_ref, g1_ref, gx_ref, dg1_ref, dn_ref, scr):
        i = pl.program_id(0)

        @pl.when(i == 0)
        def _():
            dg1_ref[...] = jnp.zeros_like(dg1_ref)

        dh = _dot(dp_ref[...], w_ref[W_UZ0:W_G0, :]) + _dot(dg_ref[...], w_ref[W_G0:, :])
        dn_ref[:, W_UZ0:W_G0] = dp_ref[...]
        dn_ref[:, W_G0:] = dg_ref[...]
        for t in range(3):
            for g, (d_ref, d) in enumerate(zip((d0_ref, d1_ref, d2_ref), DILS)):
                piece = _undilate_load(d_ref, (t,), d, scr, tm)
                if t < 2:
                    piece = _rope_bwd(piece, c_ref, s_ref)
                dn_ref[:, (3 * t + g) * GW:(3 * t + g + 1) * GW] = piece.astype(BF)
        dh = dh + _dot(dn_ref[:, :W_UZ0], w_ref[:W_UZ0, :])
        xv = x_ref[...]
        dx1, dg1 = _rms_bwd(xv, _rms_stats(xv), g1_ref[...], dh)
        gx_ref[...] = dx2_ref[...] + dx1
        dg1_ref[...] += dg1

    row = lambda w: pl.BlockSpec((tm, w), lambda i: (i, 0))
    full = lambda s: pl.BlockSpec(s, lambda i: (0, 0))
    dil = lambda d: pl.BlockSpec((3, d, tm // d, GW), lambda i: (0, 0, i, 0))
    return pl.pallas_call(
        body, name="inproj_bwd", grid=(S // tm,),
        out_shape=(jax.ShapeDtypeStruct((S, D), F32), jax.ShapeDtypeStruct((1, D), F32),
                   jax.ShapeDtypeStruct((S, win_t.shape[0]), BF)),
        in_specs=[dil(d) for d in DILS] + [row(2 * GMLP_W), row(2 * D), _resident(win_t.shape), row(128), row(128), row(D), row(D),
                                            full((1, D))],
        out_specs=(row(D), full((1, D)), row(win_t.shape[0])),
        scratch_shapes=[pltpu.VMEM((2, tm, 128), F32)],
        compiler_params=_cp(("arbitrary",)))(*dqkvs, dpuz, dgates, win_t, cos_t, sin_t, x, dx2, g1)


def _row_tile(rows, cap=256):
    return max(t for t in range(16, cap + 1, 16) if rows % t == 0)


def _adam_math(w, g, m, v):
    m2 = ADAM_B1 * m + (1.0 - ADAM_B1) * g
    v2 = ADAM_B2 * v + (1.0 - ADAM_B2) * (g * g)
    m_hat = m2 / (1.0 - ADAM_B1 ** ADAM_STEP)
    v_hat = v2 / (1.0 - ADAM_B2 ** ADAM_STEP)
    delta = -ADAM_LR * (m_hat / (jnp.sqrt(v_hat) + ADAM_EPS) + ADAM_WD * w)
    return delta, m2, v2


def _adam_shard(own, recv, w, m, v, name):
    R, C = w.shape
    tr = _row_tile(R)

    def body(own_ref, r_ref, w_ref, m_ref, v_ref, g_ref, d_ref, m2_ref, v2_ref):
        g = own_ref[...] + r_ref[0].astype(F32) + r_ref[1].astype(F32) + r_ref[2].astype(F32)
        g_ref[...] = g
        d_ref[...], m2_ref[...], v2_ref[...] = _adam_math(w_ref[...], g, m_ref[...], v_ref[...])

    spec = pl.BlockSpec((tr, C), lambda i: (i, 0))
    out = jax.ShapeDtypeStruct((R, C), F32)
    return pl.pallas_call(
        body, name=name, grid=(R // tr,), out_shape=(out, out, out, out),
        in_specs=[spec, pl.BlockSpec((3, tr, C), lambda i: (0, i, 0)), spec, spec, spec],
        out_specs=(spec, spec, spec, spec), compiler_params=_cp(("parallel",)))(own, recv, w, m, v)


def _rs_add(gblocks, recv, idx, name):
    _, R, C = gblocks.shape
    tr = _row_tile(R)

    def body(t_ref, g_ref, r_ref, own_ref, send_ref):
        j = pl.program_id(1)
        s = g_ref[...] + r_ref[...]

        @pl.when(j == 0)
        def _():
            own_ref[...] = s

        @pl.when(j > 0)
        def _():
            send_ref[...] = s.astype(BF)

    grid_spec = pltpu.PrefetchScalarGridSpec(
        num_scalar_prefetch=1, grid=(R // tr, 4),
        in_specs=[pl.BlockSpec((None, tr, C), lambda i, j, t: (t[j], i, 0)),
                  pl.BlockSpec((None, tr, C), lambda i, j, t: (t[4 + j], i, 0))],
        out_specs=[pl.BlockSpec((tr, C), lambda i, j, t: (i, 0)),
                   pl.BlockSpec((None, tr, C), lambda i, j, t: (jnp.maximum(j - 1, 0), i, 0))])
    return pl.pallas_call(
        body, name=name, grid_spec=grid_spec,
        out_shape=(jax.ShapeDtypeStruct((R, C), F32), jax.ShapeDtypeStruct((3, R, C), BF)),
        compiler_params=_cp(("parallel", "arbitrary")))(idx, gblocks, recv)


def _mesh_pos():
    return lax.axis_index("x"), lax.axis_index("y"), lax.axis_index("c")


_HBM = pl.BlockSpec(memory_space=pltpu.HBM)
_SEM = pl.BlockSpec(memory_space=pltpu.SEMAPHORE)
_EFFECT = pltpu.SideEffectType.DATAFLOW_SIDE_EFFECTING
_RELATIONS = [(dx, dy, dc) for dx in (0, 1) for dy in (0, 1) for dc in (0, 1)][1:]


def _flip(v, d):
    return 1 - v if d else v


def _plan_gather(n):
    def plan(x, y, c):
        return [(k, None, 4 * x + 2 * y + c, (_flip(x, dx), _flip(y, dy), _flip(c, dc)))
                for k in range(n) for dx, dy, dc in _RELATIONS]
    return plan


def _plan_gather_near(x, y, c):
    chips = [(1 - x, y), (x, 1 - y), (1 - x, 1 - y)]
    return [(0, None, 4 * x + 2 * y + c, (x, y, 1 - c))] + [(0, None, 4 * x + 2 * y + c, (*chip, c)) for chip in chips]


def _plan_gather_pass(from_landing):
    def plan(x, y, c):
        blocks = [4 * cx + 2 * cy + c for cx, cy in ((1 - x, y), (x, 1 - y), (1 - x, 1 - y))]
        return [(0, b if from_landing else None, b, (x, y, 1 - c)) for b in blocks]
    return plan


def _plan_d2d(n):
    def plan(x, y, c):
        return [(k, 2 * kk + 1 - c, kk, (x, y, 1 - c)) for k in range(n) for kk in range(4)]
    return plan


def _plan_ici(n):
    def plan(x, y, c):
        return [(k, j, j, (_flip(x, dx), _flip(y, dy), c))
                for k in range(n) for j, (dx, dy) in enumerate(((1, 0), (0, 1), (1, 1)))]
    return plan


def _plan_copies(plan, src_refs, land_refs, send_sems, recv_sems):
    x, y, c = _mesh_pos()
    return [pltpu.make_async_remote_copy(
        src_ref=src_refs[k] if si is None else src_refs[k].at[si], dst_ref=land_refs[k].at[di],
        send_sem=send_sems.at[n], recv_sem=recv_sems.at[n], device_id=dev, device_id_type=MESH)
        for n, (k, si, di, dev) in enumerate(plan(x, y, c))]


def _exchange_start(srcs, land_shapes, plan, ncopies, name, after):
    n = len(srcs)

    def body(*refs):
        src_refs, land_refs = refs[:n], refs[n:2 * n]
        send_sems, recv_sems = refs[2 * n + len(after)], refs[2 * n + len(after) + 1]
        token = refs[-1]
        for cp in _plan_copies(plan, src_refs, land_refs, send_sems, recv_sems):
            cp.start()
        token[...] = jnp.zeros_like(token)

    lands = [pltpu.with_memory_space_constraint(lax.empty(s, a.dtype), pltpu.HBM) for s, a in zip(land_shapes, srcs)]
    srcs = [pltpu.with_memory_space_constraint(a, pltpu.HBM) for a in srcs]
    outs = pl.pallas_call(
        body, name=name,
        out_shape=(pltpu.SemaphoreType.DMA((ncopies,)), pltpu.SemaphoreType.DMA((ncopies,)))
        + tuple(pltpu.HBM(a.shape, a.dtype) for a in srcs) + tuple(pltpu.HBM(a.shape, a.dtype) for a in lands)
        + (jax.ShapeDtypeStruct((8, 128), F32),),
        in_specs=[_HBM] * (2 * n) + [pl.BlockSpec(memory_space=pl.ANY)] * len(after),
        out_specs=(_SEM, _SEM) + (_HBM,) * (2 * n) + (pl.BlockSpec(memory_space=pltpu.VMEM),),
        input_output_aliases={i: 2 + i for i in range(2 * n)},
        compiler_params=pltpu.CompilerParams(has_side_effects=_EFFECT))(*srcs, *lands, *after)
    return (outs[0], outs[1], list(outs[2:2 + n]), list(outs[2 + n:2 + 2 * n])), outs[-1]


def _exchange_forward(handle, plan, plan_fwd, nfwd, name, after):
    send_sems, recv_sems, srcs, lands = handle
    n = len(srcs)

    def body(*refs):
        src_refs, land_refs = refs[:n], refs[n:2 * n]
        outs = refs[2 * n + 2 + len(after):]
        for cp in _plan_copies(plan, src_refs, land_refs, refs[2 * n], refs[2 * n + 1]):
            cp.wait_send()
            cp.wait_recv()
        for cp in _plan_copies(plan_fwd, land_refs, land_refs, outs[0], outs[1]):
            cp.start()

    outs = pl.pallas_call(
        body, name=name,
        out_shape=(pltpu.SemaphoreType.DMA((nfwd,)), pltpu.SemaphoreType.DMA((nfwd,)))
        + tuple(pltpu.HBM(a.shape, a.dtype) for a in srcs) + tuple(pltpu.HBM(a.shape, a.dtype) for a in lands),
        in_specs=[_HBM] * (2 * n) + [_SEM, _SEM] + [pl.BlockSpec(memory_space=pl.ANY)] * len(after),
        out_specs=(_SEM, _SEM) + (_HBM,) * (2 * n), input_output_aliases={i: 2 + i for i in range(2 * n)},
        compiler_params=pltpu.CompilerParams(has_side_effects=_EFFECT))(*srcs, *lands, send_sems, recv_sems, *after)
    return outs[0], outs[1], list(outs[2:2 + n]), list(outs[2 + n:2 + 2 * n])


def _exchange_wait(handle, plan, name, after):
    send_sems, recv_sems, srcs, lands = handle
    n = len(srcs)

    def body(*refs):
        src_refs, land_refs = refs[:n], refs[n:2 * n]
        for cp in _plan_copies(plan, src_refs, land_refs, refs[2 * n], refs[2 * n + 1]):
            cp.wait_send()
            cp.wait_recv()

    outs = pl.pallas_call(
        body, name=name,
        out_shape=tuple(pltpu.HBM(a.shape, a.dtype) for a in srcs) + tuple(pltpu.HBM(a.shape, a.dtype) for a in lands),
        in_specs=[_HBM] * (2 * n) + [_SEM, _SEM] + [pl.BlockSpec(memory_space=pl.ANY)] * len(after),
        out_specs=(_HBM,) * (2 * n), input_output_aliases={i: i for i in range(2 * n)},
        compiler_params=pltpu.CompilerParams(has_side_effects=_EFFECT))(*srcs, *lands, send_sems, recv_sems, *after)
    return list(outs[:n]), list(outs[n:])


SMALL = ("g1", "g2", "g3", "g4", "ln_g", "ln_b", "b_sp", "w_sp")


def _small_sum_adam(block, parts, lands, ws, ms, vs):
    n, ns = len(parts), len(SMALL)

    def body(blk_ref, *refs):
        p_refs, l_refs = refs[:n], refs[n:2 * n]
        w_refs, m_refs, v_refs = (refs[2 * n + i * ns:2 * n + (i + 1) * ns] for i in range(3))
        o = 2 * n + 3 * ns
        loss_ref = refs[o]
        g_out, d_out, m_out, v_out = (refs[o + 1 + i * ns:o + 1 + (i + 1) * ns] for i in range(4))
        me = blk_ref[0]
        sums = []
        for k in range(n):
            acc = jnp.where(me == 0, p_refs[k][...], l_refs[k][0])
            for b in range(1, 8):
                acc = acc + jnp.where(me == b, p_refs[k][...], l_refs[k][b])
            sums.append(acc)
        loss_ref[...] = sums[0]
        for i in range(ns):
            g_out[i][...] = sums[1 + i]
            d_out[i][...], m_out[i][...], v_out[i][...] = _adam_math(w_refs[i][...], sums[1 + i], m_refs[i][...],
                                                                     v_refs[i][...])

    args = list(parts) + list(lands) + [t[k] for t in (ws, ms, vs) for k in SMALL]
    shapes = [jax.ShapeDtypeStruct(p.shape, F32) for p in parts[:1]] + [jax.ShapeDtypeStruct(p.shape, F32) for p in parts[1:]] * 4
    vmem = pl.BlockSpec(memory_space=pltpu.VMEM)
    outs = pl.pallas_call(
        body, name="small_sum_adam", out_shape=tuple(shapes),
        in_specs=[pl.BlockSpec(memory_space=pltpu.SMEM)] + [vmem] * len(args), out_specs=(vmem,) * len(shapes),
        compiler_params=pltpu.CompilerParams(vmem_limit_bytes=VMEM_LIMIT))(block, *args)
    groups = [dict(zip(SMALL, outs[1 + i * ns:1 + (i + 1) * ns])) for i in range(4)]
    return (outs[0], *groups)


def _rope_tables(S):
    half = HD // 2
    inv_freq = jnp.tile(ROPE_THETA ** (-jnp.arange(half, dtype=F32) / half), 4)
    sign = jnp.tile(jnp.concatenate([-jnp.ones(half, F32), jnp.ones(half, F32)]), 2)
    ang = jnp.arange(S, dtype=F32)[:, None] * inv_freq[None, :]
    return jnp.cos(ang), jnp.sin(ang) * sign[None, :]


def _to_blocks(g, col_sharded):
    if col_sharded:
        return g.reshape(g.shape[0], 8, g.shape[1] // 8).transpose(1, 0, 2)
    return g.reshape(8, g.shape[0] // 8, g.shape[1])


def _from_blocks(t, col_sharded):
    if col_sharded:
        return t.transpose(1, 0, 2).reshape(t.shape[1], 8 * t.shape[2])
    return t.reshape(8 * t.shape[1], t.shape[2])


class _NoComm:
    def __init__(self, late_weights):
        self._late = late_weights
        self.grads = {}

    def start_tie(self):
        return jnp.zeros((8, 128), F32)

    def late_weights(self, after):
        return self._late

    def small_start(self, loss_row, grads):
        self.small = (loss_row, grads)
        return jnp.zeros((8, 128), F32)

    def rs_start(self, key, gblocks, after=()):
        self.grads[key] = gblocks
        return jnp.zeros((8, 128), F32)

    def rs_mid(self, key, after):
        return jnp.zeros((8, 128), F32)


class _FsdpComm:
    def __init__(self, late_shards, col_sharded, after, idx, block):
        self._col, self._idx, self._block, self._rs = col_sharded, idx, block, {}
        n = len(late_shards)
        self._gather, self._token = _exchange_start(
            late_shards, [(8,) + s.shape for s in late_shards], _plan_gather(n), 7 * n, "ag_late_start", (after,))

    def start_tie(self):
        return self._token

    def late_weights(self, after):
        shards, lands = _exchange_wait(self._gather, _plan_gather(len(self._col)), "ag_late_wait", after)
        lands = [lax.dynamic_update_index_in_dim(t, s, self._block, 0) for t, s in zip(lands, shards)]
        return [_from_blocks(t, cs) for t, cs in zip(lands, self._col)]

    def small_start(self, loss_row, grads):
        parts = [loss_row] + [grads[k] for k in SMALL]
        n = len(parts)
        self._small, token = _exchange_start(parts, [(8,) + p.shape for p in parts], _plan_gather(n), 7 * n,
                                             "small_start", (self._token,))
        return token

    def small_finish(self, ws, ms, vs, after):
        n = 1 + len(SMALL)
        parts, lands = _exchange_wait(self._small, _plan_gather(n), "small_wait", after)
        return _small_sum_adam(jnp.reshape(self._block, (1,)).astype(jnp.int32), parts, lands, ws, ms, vs)

    def rs_start(self, key, gblocks, after=()):
        n = len(gblocks)
        d2d, token = _exchange_start(gblocks, [(4,) + g.shape[1:] for g in gblocks], _plan_d2d(n), 4 * n,
                                     "rs_%s_d2d_start" % key, (self._token,) + tuple(after))
        self._rs[key] = dict(n=n, d2d=d2d)
        return token

    def rs_mid(self, key, after):
        st = self._rs[key]
        gblocks, from_sib = _exchange_wait(st["d2d"], _plan_d2d(st["n"]), "rs_%s_d2d_wait" % key, after)
        halves = [_rs_add(g, r, self._idx, "rs_add_%s_%d" % (key, k)) for k, (g, r) in enumerate(zip(gblocks, from_sib))]
        st["own"] = [own for own, _ in halves]
        sends = [send for _, send in halves]
        st["ici"], token = _exchange_start(sends, [t.shape for t in sends], _plan_ici(st["n"]), 3 * st["n"],
                                           "rs_%s_ici_start" % key, (self._token,))
        return token

    def rs_end(self, key, after):
        st = self._rs[key]
        return st["own"], _exchange_wait(st["ici"], _plan_ici(st["n"]), "rs_%s_ici_wait" % key, after)[1]


def _local_step(x, tgt, h1, cos_t, sin_t, win_t, comm, g1, g2, g3, g4, ln_g, ln_b, w_sp, b_sp):
    b_sp_t = b_sp.T
    w_sp_t = w_sp.transpose(0, 2, 1)

    *qkvs, puz, gates = _proj_all(h1, win_t, cos_t, sin_t, comm.start_tie())
    fwd = [_attn_fwd(t, "attn_fwd_d%d" % d) for t, d in zip(qkvs, DILS)]
    ya, *lses = _attn_combine([o for o, _ in fwd], [l for _, l in fwd])
    yg = _gmlp_fwd(puz, ln_g, ln_b, w_sp, b_sp_t)
    wba, wbg, wout, wmi, wmo = comm.late_weights(after=(ya, yg, gates))
    merged, y, x2, h3 = _merge_fwd(ya, yg, gates, wba, wbg, wout, x, g2, g3)
    a, dy2, dout, loss_row, dg4 = _mlp_fwd(h3, wmi, wmo, x2, tgt, g4)

    dpre, dx2, dy, dg3, dg2 = _mlp_bwd(dy2, a, wmo, wmi, x2, y, dout, g2, g3)
    dwmo = _mm_tn(a, dy2, "dw_mlp_out", square_a=True)
    dwmi = _mm_tn(h3, dpre, "dw_mlp_in", col_blocks=8)
    tie = comm.rs_start("mlp", [dwmi, _to_blocks(dwmo, False)])
    dgates, da, db, dyg, *rest = _outproj_bwd(dy, wout, ya, yg, gates, wba, wbg, tie)
    dyas, dsums = rest[:3], rest[3:]
    tie = comm.rs_mid("mlp", after=(dyg,))
    dpuz, dwsp, dbs, dlng, dlnb = _gmlp_bwd(puz, dyg, ln_g + tie[0, 0], ln_b, w_sp, w_sp_t, b_sp_t)
    dqkvs = [_attn_bwd(qkvs[g], dyas[g], dsums[g], lses[g], "attn_bwd_d%d" % d) for g, d in enumerate(DILS)]
    grad_x, dg1, dproj = _inproj_bwd(dqkvs, dpuz, dgates, win_t, cos_t, sin_t, x, dx2, g1)
    small = dict(g1=dg1, g2=dg2, g3=dg3, g4=dg4, ln_g=dlng, ln_b=dlnb, b_sp=dbs.reshape(4, CHUNK),
                 w_sp=dwsp.reshape(4 * CHUNK, CHUNK))
    tie = comm.small_start(loss_row, small)
    dwin_t = _mm_tn(dproj, h1, "dw_in", tie=tie)
    tie = comm.rs_start("win", [_to_blocks(dwin_t, False)])
    dwout = _mm_tn(merged, dy, "dw_out", tie=tie)
    tie = comm.rs_mid("win", after=(dwout,))
    dwba = _mm_tn(ya, da, "dw_branch_attn", tie=tie, col_blocks=8)
    dwbg = _mm_tn(yg, db, "dw_branch_gmlp", tie=tie, col_blocks=8)
    tie = comm.rs_start("mid", [dwba, dwbg, _to_blocks(dwout, False)], after=(tie,))
    return grad_x, tie


def kernel(x, norm_pre_mix, w_in, w_spatial, b_spatial, ln_v_gain, ln_v_bias, w_branch_attn, w_branch_gmlp, w_out, norm_post_mix, norm_pre_mlp, w_mlp_in, w_mlp_out, norm_post_mlp, loss_target, m_norm_pre_mix, m_w_in, m_w_spatial, m_b_spatial, m_ln_v_gain, m_ln_v_bias, m_w_branch_attn, m_w_branch_gmlp, m_w_out, m_norm_post_mix, m_norm_pre_mlp, m_w_mlp_in, m_w_mlp_out, m_norm_post_mlp, v_norm_pre_mix, v_w_in, v_w_spatial, v_b_spatial, v_ln_v_gain, v_ln_v_bias, v_w_branch_attn, v_w_branch_gmlp, v_w_out, v_norm_post_mix, v_norm_pre_mlp, v_w_mlp_in, v_w_mlp_out, v_norm_post_mlp):
    mx, my, mc = _mesh_pos()
    rel = [(0, 0), (1, 0), (0, 1), (1, 1)]
    chip_of = [2 * (mx ^ dx) + (my ^ dy) for dx, dy in rel]
    idx = jnp.stack([2 * k + mc for k in chip_of] + chip_of).astype(jnp.int32)

    w_in_t, m_w_in_t, v_w_in_t = (t[0].T for t in (w_in, m_w_in, v_w_in))
    shard = w_in_t.astype(BF)
    gather, token = _exchange_start([shard], [(8,) + shard.shape], _plan_gather_near, 4, "ag_win_start", ())
    h1 = _rms_fwd(x[0], norm_pre_mix + token[0, 0], "rms_pre_mix")
    cos_t, sin_t = _rope_tables(x.shape[1])
    gather = _exchange_forward(gather, _plan_gather_near, _plan_gather_pass(True), 3, "ag_win_pass", (h1, cos_t, sin_t))
    (shard,), (g_win,) = _exchange_wait(gather, _plan_gather_pass(False), "ag_win_wait", ())
    g_win = lax.dynamic_update_index_in_dim(g_win, shard, 4 * mx + 2 * my + mc, 0)
    win_t = _from_blocks(g_win, False)
    late = [w_branch_attn[0], w_branch_gmlp[0], w_out[0], w_mlp_in[0], w_mlp_out[0]]
    comm = _FsdpComm([w.astype(BF) for w in late], [True, True, False, True, False], g_win, idx, 4 * mx + 2 * my + mc)

    grad_x, tie = _local_step(
        x[0], loss_target[0], h1, cos_t, sin_t, win_t, comm,
        norm_pre_mix, norm_post_mix, norm_pre_mlp, norm_post_mlp, ln_v_gain, ln_v_bias, w_spatial[0], b_spatial[0])

    flat = lambda t: t.reshape(-1, t.shape[-1])
    small_w = dict(zip(SMALL, map(flat, (norm_pre_mix, norm_post_mix, norm_pre_mlp, norm_post_mlp, ln_v_gain, ln_v_bias,
                                         b_spatial, w_spatial))))
    small_m = dict(zip(SMALL, map(flat, (m_norm_pre_mix, m_norm_post_mix, m_norm_pre_mlp, m_norm_post_mlp, m_ln_v_gain,
                                         m_ln_v_bias, m_b_spatial, m_w_spatial))))
    small_v = dict(zip(SMALL, map(flat, (v_norm_pre_mix, v_norm_post_mix, v_norm_pre_mlp, v_norm_post_mlp, v_ln_v_gain,
                                         v_ln_v_bias, v_b_spatial, v_w_spatial))))
    mlp_own, mlp_chips = comm.rs_end("mlp", after=(tie,))
    adam = lambda nm, own, r, w, m, v: _adam_shard(own, r, w[0], m[0], v[0], "adam_" + nm)
    upd = {
        "w_mlp_in": adam("w_mlp_in", mlp_own[0], mlp_chips[0], w_mlp_in, m_w_mlp_in, v_w_mlp_in),
        "w_mlp_out": adam("w_mlp_out", mlp_own[1], mlp_chips[1], w_mlp_out, m_w_mlp_out, v_w_mlp_out),
    }
    loss_out, sg, sd, sm, sv = comm.small_finish(small_w, small_m, small_v, after=(upd["w_mlp_in"][0], upd["w_mlp_out"][0]))
    loss = loss_out[0, 0]
    tie = comm.rs_mid("mid", after=(loss_out,))
    win_own, win_chips = comm.rs_end("win", after=(tie,))
    upd["w_in"] = tuple(t.T for t in _adam_shard(win_own[0], win_chips[0], w_in_t, m_w_in_t, v_w_in_t, "adam_w_in"))
    mid_own, mid_chips = comm.rs_end("mid", after=(upd["w_in"][0],))
    upd["w_branch_attn"] = adam("w_branch_attn", mid_own[0], mid_chips[0], w_branch_attn, m_w_branch_attn, v_w_branch_attn)
    upd["w_branch_gmlp"] = adam("w_branch_gmlp", mid_own[1], mid_chips[1], w_branch_gmlp, m_w_branch_gmlp, v_w_branch_gmlp)
    upd["w_out"] = adam("w_out", mid_own[2], mid_chips[2], w_out, m_w_out, v_w_out)
    order = ["g1", "w_in", "w_sp", "b_sp", "ln_g", "ln_b", "w_branch_attn", "w_branch_gmlp", "w_out", "g2", "g3",
             "w_mlp_in", "w_mlp_out", "g4"]
    small_shape = dict(g1=norm_pre_mix.shape, g2=norm_post_mix.shape, g3=norm_pre_mlp.shape, g4=norm_post_mlp.shape,
                       ln_g=ln_v_gain.shape, ln_b=ln_v_bias.shape, b_sp=b_spatial.shape, w_sp=w_spatial.shape)

    def pick(which):
        return [upd[nm][which][None] if nm in upd else (sg, sd, sm, sv)[which][nm].reshape(small_shape[nm])
                for nm in order]

    return (loss, grad_x[None], *pick(0), *pick(1), *pick(2), *pick(3))
```

```python
import functools
import math

import jax
import jax.numpy as jnp
from jax import lax
from jax.experimental import pallas as pl
from jax.experimental.pallas import tpu as pltpu

D = 1024
HD = 64
NSLOT = 4
GW = NSLOT * HD
DILS = (1, 4, 16)
QB = 128
ATTN_NSUB = 4
PROJ_TM = 1024
MLP_TM_FWD = 512
MLP_TM_BWD = 256
ELEMENTWISE_BLOCK_BYTES = 1 << 20
AW = 3 * GW
W_UZ0, W_G0 = 3 * AW, 3 * AW + 1024
GMLP_W = 512
CHUNK = 128
DFF = 4096
EPS = 1e-6
ROPE_THETA = 10000.0
SCALE = HD ** -0.5
NEG = -1e30

ADAM_LR = 0.001
ADAM_B1 = 0.9
ADAM_B2 = 0.999
ADAM_EPS = 1e-08
ADAM_WD = 0.01
ADAM_STEP = 10

BF = jnp.bfloat16
F32 = jnp.float32
MESH = pl.DeviceIdType.MESH
VMEM_LIMIT = 56 * 1024 * 1024


def _cp(sem):
    return pltpu.CompilerParams(dimension_semantics=sem, vmem_limit_bytes=VMEM_LIMIT)


def _dot(a, b):
    return jnp.dot(a, b, preferred_element_type=F32)


def _dot_nt(a, b):
    return lax.dot_general(a, b, (((1,), (1,)), ((), ())), preferred_element_type=F32)


def _dot_tn(a, b):
    return lax.dot_general(a, b, (((0,), (0,)), ((), ())), preferred_element_type=F32)


def _gelu(x):
    return jax.nn.gelu(x, approximate=True)


def _gelu_grad(x):
    k = math.sqrt(2.0 / math.pi)
    t = jnp.tanh(k * (x + 0.044715 * x * x * x))
    return 0.5 * (1.0 + t) + 0.5 * x * (1.0 - t * t) * (k * (1.0 + 3.0 * 0.044715 * x * x))


def _swap_halves(t):
    w = t.shape[1]
    lane = lax.broadcasted_iota(jnp.int32, t.shape, 1)
    first = (lane & (HD - 1)) < (HD // 2)
    return jnp.where(first, pltpu.roll(t, w - HD // 2, 1), pltpu.roll(t, HD // 2, 1))


def _head_mask(hh):
    lane = lax.broadcasted_iota(jnp.int32, (1, GW), 1)
    return jnp.logical_and(lane >= hh * HD, lane < (hh + 1) * HD)


def _rms_stats(xf):
    return lax.rsqrt(jnp.mean(xf * xf, axis=-1, keepdims=True) + EPS)


def _rms_bwd(xf, r, gain, dout):
    n = xf * r
    t = dout * gain
    dx = r * (t - n * jnp.mean(t * n, axis=-1, keepdims=True))
    return dx, jnp.sum(dout * n, axis=0, keepdims=True)


def _rms_fwd(x, gain, name):
    S = x.shape[0]
    tm = min(512, S)

    def body(x_ref, g_ref, h_ref):
        xf = x_ref[...]
        h_ref[...] = (xf * _rms_stats(xf) * g_ref[...]).astype(BF)

    return pl.pallas_call(
        body, name=name, out_shape=jax.ShapeDtypeStruct((S, D), BF), grid=(S // tm,),
        in_specs=[pl.BlockSpec((tm, D), lambda i: (i, 0)), pl.BlockSpec((1, D), lambda i: (0, 0))],
        out_specs=pl.BlockSpec((tm, D), lambda i: (i, 0)), compiler_params=_cp(("parallel",)))(x, gain)


def _dilate_store(val, scr, o_ref, lead, d):
    rows = val.shape[0]
    if d == 1:
        o_ref[lead + (0,)] = val.astype(o_ref.dtype)
        return
    for hf in range(2):
        scr[hf, pl.ds(0, rows), :] = val[:, hf * 128:(hf + 1) * 128]
    for r in range(d):
        for hf in range(2):
            o_ref[lead + (r, slice(None), slice(hf * 128, (hf + 1) * 128))] = (
                scr[hf, pl.ds(r, rows // d, stride=d), :].astype(o_ref.dtype))


def _undilate_load(i_ref, lead, d, scr, rows):
    if d == 1:
        return i_ref[lead + (0,)].astype(F32)
    for r in range(d):
        for hf in range(2):
            scr[hf, pl.ds(r, rows // d, stride=d), :] = (
                i_ref[lead + (r, slice(None), slice(hf * 128, (hf + 1) * 128))].astype(F32))
    return jnp.concatenate([scr[0, pl.ds(0, rows), :], scr[1, pl.ds(0, rows), :]], axis=1)


def _rope_fwd(y, c_ref, s_ref):
    cosv = jnp.concatenate([c_ref[...]] * 2, axis=1)
    sinv = jnp.concatenate([s_ref[...]] * 2, axis=1)
    return y * cosv + _swap_halves(y) * sinv


def _rope_bwd(dy, c_ref, s_ref):
    cosv = jnp.concatenate([c_ref[...]] * 2, axis=1)
    sinv = jnp.concatenate([s_ref[...]] * 2, axis=1)
    return dy * cosv + _swap_halves(dy * sinv)


def _proj_all(h, win_t, cos_t, sin_t, tie):
    S, K = h.shape
    tm = min(PROJ_TM, S)

    def body(h_ref, w_ref, c_ref, s_ref, tie_ref, o0_ref, o1_ref, o2_ref, p_ref, gt_ref, scr):
        hv = h_ref[...]
        col = lambda b: _dot_nt(hv, w_ref[b * GW:(b + 1) * GW, :])
        ys = [col(3 * t) for t in range(3)]
        for g, (o_ref, d) in enumerate(zip((o0_ref, o1_ref, o2_ref), DILS)):
            nxt = [col(3 * t + g + 1) if g < 2 else col(W_UZ0 // GW + t) for t in range(3)]
            for t in range(3):
                y = _rope_fwd(ys[t], c_ref, s_ref) if t < 2 else ys[t]
                _dilate_store(y, scr.at[t], o_ref, (t,), d)
            ys = nxt
        for b in range((W_G0 - W_UZ0) // GW):
            y = ys[b] if b < 3 else col(W_UZ0 // GW + b)
            p_ref[:, b * GW:(b + 1) * GW] = y.astype(BF)
        for b in range(2 * D // GW):
            gt_ref[:, b * GW:(b + 1) * GW] = jax.nn.sigmoid(col(W_G0 // GW + b)).astype(BF)

    row = lambda w: pl.BlockSpec((tm, w), lambda i: (i, 0))
    return pl.pallas_call(
        body, name="proj_all", grid=(S // tm,),
        out_shape=tuple(jax.ShapeDtypeStruct((3, d, S // d, GW), BF) for d in DILS)
        + (jax.ShapeDtypeStruct((S, W_G0 - W_UZ0), BF), jax.ShapeDtypeStruct((S, 2 * D), BF)),
        in_specs=[row(K), _resident(win_t.shape), row(128), row(128), pl.BlockSpec(memory_space=pl.ANY)],
        out_specs=tuple(pl.BlockSpec((3, d, tm // d, GW), lambda i: (0, 0, i, 0)) for d in DILS)
        + (row(W_G0 - W_UZ0), row(2 * D)),
        scratch_shapes=[pltpu.VMEM((3, 2, tm, 128), F32)],
        compiler_params=_cp(("parallel",)))(h, win_t, cos_t, sin_t, tie)


def _band_masks(first_step):
    row = lax.broadcasted_iota(jnp.int32, (QB, 2 * QB), 0)
    col = lax.broadcasted_iota(jnp.int32, (QB, 2 * QB), 1)
    band = jnp.logical_and(col >= row, col <= row + QB)
    return band, jnp.logical_and(band, jnp.logical_or(col >= QB, jnp.logical_not(first_step)))


def _group_step(k, nsteps):
    return lambda g, n: jnp.where(g == k, n, jnp.where(g < k, 0, nsteps - 1))


def _attn_fwd(qkvs):
    S = qkvs[0].shape[1] * qkvs[0].shape[2]
    nsub = min(ATTN_NSUB, S // max(DILS) // QB)
    R = nsub * QB
    nsteps = S // R

    def group_body(first, q_ref, kp_ref, kc_ref, vp_ref, vc_ref, o_ref, lse_ref):
        band, band_first = _band_masks(first)
        kfull = jnp.concatenate([kp_ref[...], kc_ref[...]], axis=0)
        vfull = jnp.concatenate([vp_ref[...], vc_ref[...]], axis=0)
        chains = [(sb, hh) for sb in range(nsub) for hh in range(NSLOT)]
        win = lambda t, sb: t[sb * QB:(sb + 2) * QB]
        scores = []
        for sb, hh in chains:
            qh = jnp.where(_head_mask(hh), q_ref[sb * QB:(sb + 1) * QB, :], 0)
            scores.append(_dot_nt(qh, win(kfull, sb)))
        soft = []
        for (sb, hh), sc in zip(chains, scores):
            sc = jnp.where(band_first if sb == 0 else band, sc * SCALE, NEG)
            m = jnp.max(sc, axis=1, keepdims=True)
            p = jnp.exp(sc - m)
            den = jnp.sum(p, axis=1, keepdims=True)
            soft.append((p.astype(BF), den, m + jnp.log(den)))
        accs = [_dot(p, win(vfull, sb)) for (sb, hh), (p, _, _) in zip(chains, soft)]
        for sb in range(nsub):
            o = jnp.zeros((QB, GW), F32)
            lse = jnp.zeros((QB, GW), F32)
            for hh in range(NSLOT):
                hm = _head_mask(hh)
                _, den, lrow = soft[sb * NSLOT + hh]
                o = o + jnp.where(hm, accs[sb * NSLOT + hh] / den, 0.0)
                lse = lse + jnp.where(hm, lrow, 0.0)
            o_ref[sb * QB:(sb + 1) * QB, :] = o
            lse_ref[sb * QB:(sb + 1) * QB, :] = lse

    def body(*refs):
        g, n = pl.program_id(0), pl.program_id(1)
        for k, d in enumerate(DILS):
            @pl.when(g == k)
            def _():
                group_body(n % (nsteps // d) == 0, *refs[5 * k:5 * k + 5], *refs[15 + 2 * k:17 + 2 * k])

    in_specs, out_specs, operands = [], [], []
    for k in range(len(DILS)):
        at = _group_step(k, nsteps)
        cur = lambda t, at=at: pl.BlockSpec((None, R, GW), lambda g, n: (t, at(g, n), 0))
        prv = lambda t, at=at: pl.BlockSpec((None, QB, GW), lambda g, n: (t, jnp.maximum(at(g, n) * nsub - 1, 0), 0))
        in_specs += [cur(0), prv(1), cur(1), prv(2), cur(2)]
        out_specs += [pl.BlockSpec((R, GW), lambda g, n, at=at: (at(g, n), 0))] * 2
        operands += [qkvs[k].reshape(3, S, GW)] * 5
    outs = pl.pallas_call(
        body, name="attn_fwd", grid=(len(DILS), nsteps),
        out_shape=(jax.ShapeDtypeStruct((S, GW), F32),) * (2 * len(DILS)),
        in_specs=in_specs, out_specs=tuple(out_specs),
        compiler_params=_cp(("arbitrary", "arbitrary")))(*operands)
    return [(outs[2 * k].reshape(d, S // d, GW), outs[2 * k + 1].reshape(d, S // d, GW)) for k, d in enumerate(DILS)]


def _attn_combine(os_, lses):
    S = os_[0].shape[1]
    tm = min(1024, S)

    def body(o0, o1, o2, l0, l1, l2, y_ref, j0, j1, j2, scr):
        os_nat = [_undilate_load(o, (), d, scr, tm) for o, d in zip((o0, o1, o2), DILS)]
        a, b, c = [_undilate_load(l, (), d, scr, tm) for l, d in zip((l0, l1, l2), DILS)]
        m = jnp.maximum(jnp.maximum(a, b), c)
        wa, wb, wc = jnp.exp(a - m), jnp.exp(b - m), jnp.exp(c - m)
        den = wa + wb + wc
        y_ref[...] = ((wa * os_nat[0] + wb * os_nat[1] + wc * os_nat[2]) / den).astype(BF)
        lse = m + jnp.log(den)
        for j_ref, d in zip((j0, j1, j2), DILS):
            _dilate_store(lse, scr, j_ref, (), d)

    dil = lambda d: pl.BlockSpec((d, tm // d, GW), lambda i: (0, i, 0))
    dshape = lambda d: jax.ShapeDtypeStruct((d, S // d, GW), F32)
    return pl.pallas_call(
        body, name="attn_combine", grid=(S // tm,),
        out_shape=(jax.ShapeDtypeStruct((S, GW), BF),) + tuple(dshape(d) for d in DILS),
        in_specs=[dil(d) for d in DILS] * 2,
        out_specs=(pl.BlockSpec((tm, GW), lambda i: (i, 0)),) + tuple(dil(d) for d in DILS),
        scratch_shapes=[pltpu.VMEM((2, tm, 128), F32)],
        compiler_params=_cp(("parallel",)))(*os_, *lses)


def _tril(upper=False):
    row = lax.broadcasted_iota(jnp.int32, (CHUNK, CHUNK), 0)
    col = lax.broadcasted_iota(jnp.int32, (CHUNK, CHUNK), 1)
    return row <= col if upper else col <= row


def _ln_fwd(z, gain, bias):
    mu = jnp.mean(z, axis=-1, keepdims=True)
    zc = z - mu
    rstd = lax.rsqrt(jnp.mean(zc * zc, axis=-1, keepdims=True) + EPS)
    zhat = zc * rstd
    return zhat, rstd, zhat * gain + bias


def _gmlp_fwd(puz, ln_g, ln_b, w_sp, b_sp_t):
    S = puz.shape[0]
    tm = min(512, S)
    nch = tm // CHUNK

    def body(p_ref, g_ref, b_ref, w_ref, bt_ref, o_ref):
        tril = _tril()
        ws = [jnp.where(tril, w_ref[gg], 0.0).astype(BF) for gg in range(4)]
        for ch in range(nch):
            rows = slice(ch * CHUNK, (ch + 1) * CHUNK)
            z = _gelu(p_ref[rows, GMLP_W:].astype(F32))
            _, _, zn = _ln_fwd(z, g_ref[...], b_ref[...])
            zn = zn.astype(BF)
            for gg in range(4):
                cols = slice(gg * CHUNK, (gg + 1) * CHUNK)
                sz = _dot(ws[gg], zn[:, cols]) + bt_ref[:, gg:gg + 1]
                u = _gelu(p_ref[rows, cols].astype(F32))
                o_ref[rows, cols] = (u * sz).astype(BF)

    return pl.pallas_call(
        body, name="gmlp_fwd", out_shape=jax.ShapeDtypeStruct((S, GMLP_W), BF), grid=(S // tm,),
        in_specs=[pl.BlockSpec((tm, 2 * GMLP_W), lambda i: (i, 0)),
                  pl.BlockSpec((1, GMLP_W), lambda i: (0, 0)), pl.BlockSpec((1, GMLP_W), lambda i: (0, 0)),
                  pl.BlockSpec((4, CHUNK, CHUNK), lambda i: (0, 0, 0)), pl.BlockSpec((CHUNK, 4), lambda i: (0, 0))],
        out_specs=pl.BlockSpec((tm, GMLP_W), lambda i: (i, 0)),
        compiler_params=_cp(("parallel",)))(puz, ln_g, ln_b, w_sp, b_sp_t)


def _merge_fwd(ya, yg, gates, wba, wbg, wout, x, g2, g3):
    S = x.shape[0]
    tm = min(512, S)

    def body(ya_ref, yg_ref, gt_ref, wba_ref, wbg_ref, wo_ref, x_ref, g2_ref, g3_ref,
             mg_ref, y_ref, x2_ref, h3_ref):
        a = _dot(ya_ref[...], wba_ref[...])
        b = _dot(yg_ref[...], wbg_ref[...])
        merged = (gt_ref[:, :D].astype(F32) * a + gt_ref[:, D:].astype(F32) * b).astype(BF)
        mg_ref[...] = merged
        y = _dot(merged, wo_ref[...])
        y_ref[...] = y
        x2 = x_ref[...] + y * _rms_stats(y) * g2_ref[...]
        x2_ref[...] = x2
        h3_ref[...] = (x2 * _rms_stats(x2) * g3_ref[...]).astype(BF)

    row = lambda w: pl.BlockSpec((tm, w), lambda i: (i, 0))
    full = lambda s: pl.BlockSpec(s, lambda i: (0, 0))
    return pl.pallas_call(
        body, name="merge_fwd", grid=(S // tm,),
        out_shape=(jax.ShapeDtypeStruct((S, D), BF), jax.ShapeDtypeStruct((S, D), F32), jax.ShapeDtypeStruct((S, D), F32),
                   jax.ShapeDtypeStruct((S, D), BF)),
        in_specs=[row(GW), row(GMLP_W), row(2 * D), full((GW, D)), full((GMLP_W, D)), full((D, D)), row(D),
                  full((1, D)), full((1, D))],
        out_specs=(row(D), row(D), row(D), row(D)),
        compiler_params=_cp(("parallel",)))(ya, yg, gates, wba, wbg, wout, x, g2, g3)


def _resident(shape):
    return pl.BlockSpec(shape, lambda i: (0,) * len(shape), pipeline_mode=pl.Buffered(1))


def _mlp_fwd(h3, wmi, wmo, x2, tgt, g4):
    S = x2.shape[0]
    tm = min(MLP_TM_FWD, S)

    def body(h_ref, wi_ref, wo_ref, x2_ref, t_ref, g4_ref, a_ref, dy2_ref, dout_ref, loss_ref, dg4_ref):
        @pl.when(pl.program_id(0) == 0)
        def _():
            loss_ref[...] = jnp.zeros_like(loss_ref)
            dg4_ref[...] = jnp.zeros_like(dg4_ref)

        halves = [slice(hh * (tm // 2), (hh + 1) * (tm // 2)) for hh in range(2)]
        acts = []
        for rows in halves:
            a = jnp.maximum(_dot(h_ref[rows, :], wi_ref[...]), 0.0)
            a_ref[rows, :] = a.astype(BF)
            acts.append((a * a).astype(BF))
        y2s = [_dot(a2, wo_ref[...]) for a2 in acts]
        lane = lax.broadcasted_iota(jnp.int32, (1, 128), 1)
        for rows, y2 in zip(halves, y2s):
            r = _rms_stats(y2)
            out = x2_ref[rows, :] + y2 * r * g4_ref[...]
            err = out - t_ref[rows, :]
            tot = jnp.sum(jnp.sum(err * err, axis=1, keepdims=True), axis=0, keepdims=True) * (0.5 / D)
            loss_ref[...] += jnp.where(lane == 0, tot, 0.0)
            dout = err * (1.0 / D)
            dout_ref[rows, :] = dout
            dy2, dg = _rms_bwd(y2, r, g4_ref[...], dout)
            dy2_ref[rows, :] = dy2.astype(BF)
            dg4_ref[...] += dg

    row = pl.BlockSpec((tm, D), lambda i: (i, 0))
    return pl.pallas_call(
        body, name="mlp_fwd", grid=(S // tm,),
        out_shape=(jax.ShapeDtypeStruct((S, DFF), BF), jax.ShapeDtypeStruct((S, D), BF), jax.ShapeDtypeStruct((S, D), F32),
                   jax.ShapeDtypeStruct((1, 128), F32), jax.ShapeDtypeStruct((1, D), F32)),
        in_specs=[row, _resident((D, DFF)), _resident((DFF, D)), row, row, pl.BlockSpec((1, D), lambda i: (0, 0))],
        out_specs=(pl.BlockSpec((tm, DFF), lambda i: (i, 0)), row, row,
                   pl.BlockSpec((1, 128), lambda i: (0, 0)), pl.BlockSpec((1, D), lambda i: (0, 0))),
        compiler_params=_cp(("arbitrary",)))(h3, wmi, wmo, x2, tgt, g4)


def _mlp_bwd(dy2, a, wmo, wmi, x2, y, dout, g2, g3):
    S = x2.shape[0]
    tm = min(MLP_TM_BWD, S)

    def body(dy2_ref, a_ref, wo_ref, wi_ref, x2_ref, y_ref, dout_ref, g2_ref, g3_ref,
             dpre_ref, dx2_ref, dy_ref, dg3_ref, dg2_ref):
        @pl.when(pl.program_id(0) == 0)
        def _():
            dg3_ref[...] = jnp.zeros_like(dg3_ref)
            dg2_ref[...] = jnp.zeros_like(dg2_ref)

        da2 = _dot_nt(dy2_ref[...], wo_ref[...])
        dpre = (2.0 * a_ref[...].astype(F32) * da2).astype(BF)
        dpre_ref[...] = dpre
        dh3 = _dot_nt(dpre, wi_ref[...])
        x2 = x2_ref[...]
        dx3, dg3 = _rms_bwd(x2, _rms_stats(x2), g3_ref[...], dh3)
        dx2 = dout_ref[...] + dx3
        dx2_ref[...] = dx2
        dg3_ref[...] += dg3
        yv = y_ref[...]
        dy, dg2 = _rms_bwd(yv, _rms_stats(yv), g2_ref[...], dx2)
        dy_ref[...] = dy.astype(BF)
        dg2_ref[...] += dg2

    row = pl.BlockSpec((tm, D), lambda i: (i, 0))
    wide = pl.BlockSpec((tm, DFF), lambda i: (i, 0))
    vec = pl.BlockSpec((1, D), lambda i: (0, 0))
    return pl.pallas_call(
        body, name="mlp_bwd", grid=(S // tm,),
        out_shape=(jax.ShapeDtypeStruct((S, DFF), BF), jax.ShapeDtypeStruct((S, D), F32), jax.ShapeDtypeStruct((S, D), BF),
                   jax.ShapeDtypeStruct((1, D), F32), jax.ShapeDtypeStruct((1, D), F32)),
        in_specs=[row, wide, _resident((DFF, D)), _resident((D, DFF)), row, row, row, vec, vec],
        out_specs=(wide, row, row, vec, vec),
        compiler_params=_cp(("arbitrary",)))(dy2, a, wmo, wmi, x2, y, dout, g2, g3)


def _mm_tn(a, b, name, square_a=False, tm=1024, tn=1024, tk=2048, tie=None, col_blocks=None):
    S, M = a.shape
    N = b.shape[1]
    tk = min(tk, S)
    tm = max(t for t in range(128, min(tm, M) + 1, 128) if M % t == 0)
    tn = max(t for t in range(128, min(tn, N) + 1, 128) if N % t == 0)
    cb = N // col_blocks if col_blocks else tn
    assert M % tm == 0 and S % tk == 0 and tn % cb == 0
    nk = S // tk
    ties = () if tie is None else (tie,)

    def body(a_ref, b_ref, *rest):
        o_ref = rest[-1]
        k = pl.program_id(2)
        av = a_ref[...]
        if square_a:
            av = av * av
        part = _dot_tn(av, b_ref[...])
        if col_blocks:
            part = jnp.stack([part[:, t * cb:(t + 1) * cb] for t in range(tn // cb)])

        @pl.when(k == 0)
        def _():
            o_ref[...] = part

        @pl.when(k > 0)
        def _():
            o_ref[...] += part

    if col_blocks:
        out_shape, out_spec = (col_blocks, M, cb), pl.BlockSpec((tn // cb, tm, cb), lambda i, j, k: (j, i, 0))
    else:
        out_shape, out_spec = (M, N), pl.BlockSpec((tm, tn), lambda i, j, k: (i, j))
    return pl.pallas_call(
        body, name=name, out_shape=jax.ShapeDtypeStruct(out_shape, F32), grid=(M // tm, N // tn, nk),
        in_specs=[pl.BlockSpec((tk, tm), lambda i, j, k: (k, i)), pl.BlockSpec((tk, tn), lambda i, j, k: (k, j))]
        + [pl.BlockSpec(memory_space=pl.ANY)] * len(ties),
        out_specs=out_spec,
        compiler_params=_cp(("parallel", "parallel", "arbitrary")))(a, b, *ties)


def _outproj_bwd(dy, wout, ya, yg, gates, wba, wbg, tie):
    S = dy.shape[0]
    tm = min(512, S)

    def body(dy_ref, wo_ref, ya_ref, yg_ref, gt_ref, wba_ref, wbg_ref, tie_ref,
             dgt_ref, da_ref, db_ref, dyg_ref, e0, e1, e2, s0, s1, s2, scr):
        dm = _dot_nt(dy_ref[...], wo_ref[...])
        ga, gb = gt_ref[:, :D].astype(F32), gt_ref[:, D:].astype(F32)
        dgt_ref[:, :D] = (dm * _dot(ya_ref[...], wba_ref[...]) * ga * (1.0 - ga)).astype(BF)
        dgt_ref[:, D:] = (dm * _dot(yg_ref[...], wbg_ref[...]) * gb * (1.0 - gb)).astype(BF)
        da = (dm * ga).astype(BF)
        db = (dm * gb).astype(BF)
        da_ref[...] = da
        db_ref[...] = db
        dyg_ref[...] = _dot_nt(db, wbg_ref[...]).astype(BF)
        dya = _dot_nt(da, wba_ref[...]).astype(BF).astype(F32)
        dyy = dya * ya_ref[...].astype(F32)
        dsum = jnp.zeros((tm, GW), F32)
        for hh in range(NSLOT):
            hm = _head_mask(hh)
            dsum = dsum + jnp.where(hm, jnp.sum(jnp.where(hm, dyy, 0.0), axis=1, keepdims=True), 0.0)
        for e_ref, s_ref, d in zip((e0, e1, e2), (s0, s1, s2), DILS):
            _dilate_store(dya, scr, e_ref, (), d)
            _dilate_store(dsum, scr, s_ref, (), d)

    row = lambda w: pl.BlockSpec((tm, w), lambda i: (i, 0))
    full = lambda s: pl.BlockSpec(s, lambda i: (0, 0))
    dil = lambda d: pl.BlockSpec((d, tm // d, GW), lambda i: (0, i, 0))
    dshape = lambda d, t: jax.ShapeDtypeStruct((d, S // d, GW), t)
    return pl.pallas_call(
        body, name="outproj_bwd", grid=(S // tm,),
        out_shape=(jax.ShapeDtypeStruct((S, 2 * D), BF), jax.ShapeDtypeStruct((S, D), BF), jax.ShapeDtypeStruct((S, D), BF),
                   jax.ShapeDtypeStruct((S, GMLP_W), BF)) + tuple(dshape(d, BF) for d in DILS)
        + tuple(dshape(d, F32) for d in DILS),
        in_specs=[row(D), full((D, D)), row(GW), row(GMLP_W), row(2 * D), full((GW, D)), full((GMLP_W, D)),
                  pl.BlockSpec(memory_space=pl.ANY)],
        out_specs=(row(2 * D), row(D), row(D), row(GMLP_W)) + tuple(dil(d) for d in DILS) * 2,
        scratch_shapes=[pltpu.VMEM((2, tm, 128), F32)],
        compiler_params=_cp(("parallel",)))(dy, wout, ya, yg, gates, wba, wbg, tie)


def _gmlp_bwd(puz, dyg, ln_g, ln_b, w_sp, w_sp_t, b_sp_t):
    S = puz.shape[0]
    tm = min(512, S)
    nch = tm // CHUNK

    def body(p_ref, dy_ref, g_ref, b_ref, w_ref, wt_ref, bt_ref,
             dp_ref, dw_ref, dbs_ref, dg_ref, dbias_ref, dbacc_ref):
        i = pl.program_id(0)

        @pl.when(i == 0)
        def _():
            dw_ref[...] = jnp.zeros_like(dw_ref)
            dbacc_ref[...] = jnp.zeros_like(dbacc_ref)
            dg_ref[...] = jnp.zeros_like(dg_ref)
            dbias_ref[...] = jnp.zeros_like(dbias_ref)

        tril = _tril()
        ws = [jnp.where(tril, w_ref[gg], 0.0).astype(BF) for gg in range(4)]
        triu = _tril(upper=True)
        wts = [jnp.where(triu, wt_ref[gg], 0.0).astype(BF) for gg in range(4)]
        gain = g_ref[...]
        for ch in range(nch):
            rows = slice(ch * CHUNK, (ch + 1) * CHUNK)
            pz = p_ref[rows, GMLP_W:].astype(F32)
            z = _gelu(pz)
            zhat, rstd, zn = _ln_fwd(z, gain, b_ref[...])
            znb = zn.astype(BF)
            dzn_parts = []
            for gg in range(4):
                cols = slice(gg * CHUNK, (gg + 1) * CHUNK)
                pu = p_ref[rows, cols].astype(F32)
                u = _gelu(pu)
                sz = _dot(ws[gg], znb[:, cols]) + bt_ref[:, gg:gg + 1]
                dyv = dy_ref[rows, cols].astype(F32)
                dp_ref[rows, cols] = (dyv * sz * _gelu_grad(pu)).astype(BF)
                dsz = dyv * u
                dbacc_ref[gg] += dsz
                dszb = dsz.astype(BF)
                dw_ref[gg] += _dot_nt(dszb, znb[:, cols])
                dzn_parts.append(_dot(wts[gg], dszb))
            dzn = jnp.concatenate(dzn_parts, axis=1)
            dg_ref[...] += jnp.sum(dzn * zhat, axis=0, keepdims=True)
            dbias_ref[...] += jnp.sum(dzn, axis=0, keepdims=True)
            dzh = dzn * gain
            dz = rstd * (dzh - jnp.mean(dzh, axis=-1, keepdims=True)
                         - zhat * jnp.mean(dzh * zhat, axis=-1, keepdims=True))
            dp_ref[rows, GMLP_W:] = (dz * _gelu_grad(pz)).astype(BF)

        @pl.when(i == pl.num_programs(0) - 1)
        def _():
            for gg in range(4):
                dw_ref[gg] = jnp.where(tril, dw_ref[gg], 0.0)
                dbs_ref[gg] = jnp.sum(dbacc_ref[gg], axis=1, keepdims=True)

    full2 = lambda s: pl.BlockSpec(s, lambda i: (0, 0))
    full3 = lambda s: pl.BlockSpec(s, lambda i: (0, 0, 0))
    return pl.pallas_call(
        body, name="gmlp_bwd", grid=(S // tm,),
        out_shape=(jax.ShapeDtypeStruct((S, 2 * GMLP_W), BF), jax.ShapeDtypeStruct((4, CHUNK, CHUNK), F32),
                   jax.ShapeDtypeStruct((4, CHUNK, 1), F32), jax.ShapeDtypeStruct((1, GMLP_W), F32),
                   jax.ShapeDtypeStruct((1, GMLP_W), F32)),
        in_specs=[pl.BlockSpec((tm, 2 * GMLP_W), lambda i: (i, 0)), pl.BlockSpec((tm, GMLP_W), lambda i: (i, 0)),
                  full2((1, GMLP_W)), full2((1, GMLP_W)), full3((4, CHUNK, CHUNK)), full3((4, CHUNK, CHUNK)),
                  full2((CHUNK, 4))],
        out_specs=(pl.BlockSpec((tm, 2 * GMLP_W), lambda i: (i, 0)), full3((4, CHUNK, CHUNK)), full3((4, CHUNK, 1)),
                   full2((1, GMLP_W)), full2((1, GMLP_W))),
        scratch_shapes=[pltpu.VMEM((4, CHUNK, CHUNK), F32)],
        compiler_params=_cp(("arbitrary",)))(puz, dyg, ln_g, ln_b, w_sp, w_sp_t, b_sp_t)


def _attn_bwd(qkvs, dyas, dsums, lses):
    S = qkvs[0].shape[1] * qkvs[0].shape[2]
    nsub = min(ATTN_NSUB, S // max(DILS) // QB)
    R = nsub * QB
    nsteps = S // R
    NIN = 12

    def group_body(first, last, q_ref, qn_ref, kp_ref, kc_ref, vp_ref, vc_ref, dy_ref, dyn_ref, e_ref, en_ref,
                   l_ref, ln_ref, o_ref):
        band, band_first = _band_masks(first)
        row = lax.broadcasted_iota(jnp.int32, (QB, QB), 0)
        col = lax.broadcasted_iota(jnp.int32, (QB, QB), 1)
        mask_next = jnp.logical_and(col >= row, jnp.logical_not(last))
        kc, vc = kc_ref[...], vc_ref[...]
        kfull = jnp.concatenate([kp_ref[...], kc], axis=0)
        vfull = jnp.concatenate([vp_ref[...], vc], axis=0)
        k_last, v_last = kc[(nsub - 1) * QB:], vc[(nsub - 1) * QB:]
        q_ext = jnp.concatenate([q_ref[...], qn_ref[...]], axis=0)
        dy_ext = jnp.concatenate([dy_ref[...], dyn_ref[...]], axis=0)
        esum, esum_n, lse, lse_n = e_ref[...], en_ref[...], l_ref[...], ln_ref[...]
        win = lambda t, sb: t[sb * QB:(sb + 2) * QB]
        blk = lambda t, sb: t[sb * QB:(sb + 1) * QB]
        hms = [_head_mask(hh) for hh in range(NSLOT)]
        q_hs = [jnp.where(hm, q_ext, 0) for hm in hms]
        dy_hs = [jnp.where(hm, dy_ext, 0) for hm in hms]
        raw = []
        for hh in range(NSLOT):
            tiles = [(_dot_nt(blk(q_hs[hh], sb), win(kfull, sb)), _dot_nt(blk(dy_hs[hh], sb), win(vfull, sb)))
                     for sb in range(nsub)]
            tiles.append((_dot_nt(blk(q_hs[hh], nsub), k_last), _dot_nt(blk(dy_hs[hh], nsub), v_last)))
            raw.append(tiles)
        ps, dss = [], []
        for hh in range(NSLOT):
            rowstat = lambda t: jnp.max(jnp.where(hms[hh], t, -jnp.inf), axis=1, keepdims=True)
            p_h, ds_h = [], []
            for sb in range(nsub + 1):
                sc, dp = raw[hh][sb]
                if sb < nsub:
                    msk, lrow, erow = (band_first if sb == 0 else band), rowstat(blk(lse, sb)), rowstat(blk(esum, sb))
                else:
                    msk, lrow, erow = mask_next, rowstat(lse_n), rowstat(esum_n)
                p = jnp.where(msk, jnp.exp(sc * SCALE - lrow), 0.0)
                p_h.append(p.astype(BF))
                ds_h.append((p * (dp - erow)).astype(BF))
            ps.append(p_h)
            dss.append(ds_h)
        dq = [jnp.zeros((QB, GW), F32) for _ in range(nsub)]
        dk = [jnp.zeros((QB, GW), F32) for _ in range(nsub)]
        dv = [jnp.zeros((QB, GW), F32) for _ in range(nsub)]
        for hh in range(NSLOT):
            for sb in range(nsub):
                dq[sb] = dq[sb] + jnp.where(hms[hh], _dot(dss[hh][sb], win(kfull, sb)), 0.0)
                nxt = lambda t: t[sb + 1][:, :QB] if sb + 1 < nsub else t[nsub]
                dk[sb] = dk[sb] + _dot_tn(jnp.concatenate([dss[hh][sb][:, QB:], nxt(dss[hh])], axis=0), win(q_hs[hh], sb))
                dv[sb] = dv[sb] + _dot_tn(jnp.concatenate([ps[hh][sb][:, QB:], nxt(ps[hh])], axis=0), win(dy_hs[hh], sb))
        for sb in range(nsub):
            rows = slice(sb * QB, (sb + 1) * QB)
            o_ref[0, rows, :] = (dq[sb] * SCALE).astype(BF)
            o_ref[1, rows, :] = (dk[sb] * SCALE).astype(BF)
            o_ref[2, rows, :] = dv[sb].astype(BF)

    def body(*refs):
        g, n = pl.program_id(0), pl.program_id(1)
        for k, d in enumerate(DILS):
            @pl.when(g == k)
            def _():
                per = nsteps // d
                group_body(n % per == 0, n % per == per - 1, *refs[NIN * k:NIN * (k + 1)], refs[NIN * len(DILS) + k])

    in_specs, out_specs, operands = [], [], []
    for k in range(len(DILS)):
        at = _group_step(k, nsteps)
        prev = lambda g, n, at=at: jnp.maximum(at(g, n) * nsub - 1, 0)
        nxt = lambda g, n, at=at: jnp.minimum((at(g, n) + 1) * nsub, S // QB - 1)
        cur4 = lambda t, at=at: pl.BlockSpec((None, R, GW), lambda g, n: (t, at(g, n), 0))
        prv4 = lambda t, prev=prev: pl.BlockSpec((None, QB, GW), lambda g, n: (t, prev(g, n), 0))
        nxt4 = lambda t, nxt=nxt: pl.BlockSpec((None, QB, GW), lambda g, n: (t, nxt(g, n), 0))
        cur3 = pl.BlockSpec((R, GW), lambda g, n, at=at: (at(g, n), 0))
        nxt3 = pl.BlockSpec((QB, GW), lambda g, n, nxt=nxt: (nxt(g, n), 0))
        in_specs += [cur4(0), nxt4(0), prv4(1), cur4(1), prv4(2), cur4(2), cur3, nxt3, cur3, nxt3, cur3, nxt3]
        out_specs.append(pl.BlockSpec((3, R, GW), lambda g, n, at=at: (0, at(g, n), 0)))
        flat = lambda t: t.reshape(S, GW)
        operands += [qkvs[k].reshape(3, S, GW)] * 6 + [flat(dyas[k])] * 2 + [flat(dsums[k])] * 2 + [flat(lses[k])] * 2
    outs = pl.pallas_call(
        body, name="attn_bwd", grid=(len(DILS), nsteps),
        out_shape=(jax.ShapeDtypeStruct((3, S, GW), BF),) * len(DILS),
        in_specs=in_specs, out_specs=tuple(out_specs),
        compiler_params=_cp(("arbitrary", "arbitrary")))(*operands)
    return [t.reshape(3, d, S // d, GW) for t, d in zip(outs, DILS)]


def _inproj_bwd(dqkvs, dpuz, dgates, win_t, cos_t, sin_t, x, dx2, g1):
    S = x.shape[0]
    tm = min(512, S)

    def body(d0_ref, d1_ref, d2_ref, dp_ref, dg_ref, w_ref, c_ref, s_ref,
             x_ref, dx2_ref, g1_ref, gx_ref, dg1_ref, dn_ref, scr):
        i = pl.program_id(0)

        @pl.when(i == 0)
        def _():
            dg1_ref[...] = jnp.zeros_like(dg1_ref)

        dh = _dot(dp_ref[...], w_ref[W_UZ0:W_G0, :]) + _dot(dg_ref[...], w_ref[W_G0:, :])
        dn_ref[:, W_UZ0:W_G0] = dp_ref[...]
        dn_ref[:, W_G0:] = dg_ref[...]
        for t in range(3):
            for g, (d_ref, d) in enumerate(zip((d0_ref, d1_ref, d2_ref), DILS)):
                piece = _undilate_load(d_ref, (t,), d, scr, tm)
                if t < 2:
                    piece = _rope_bwd(piece, c_ref, s_ref)
                dn_ref[:, (3 * t + g) * GW:(3 * t + g + 1) * GW] = piece.astype(BF)
        dh = dh + _dot(dn_ref[:, :W_UZ0], w_ref[:W_UZ0, :])
        xv = x_ref[...]
        dx1, dg1 = _rms_bwd(xv, _rms_stats(xv), g1_ref[...], dh)
        gx_ref[...] = dx2_ref[...] + dx1
        dg1_ref[...] += dg1

    row = lambda w: pl.BlockSpec((tm, w), lambda i: (i, 0))
    full = lambda s: pl.BlockSpec(s, lambda i: (0, 0))
    dil = lambda d: pl.BlockSpec((3, d, tm // d, GW), lambda i: (0, 0, i, 0))
    return pl.pallas_call(
        body, name="inproj_bwd", grid=(S // tm,),
        out_shape=(jax.ShapeDtypeStruct((S, D), F32), jax.ShapeDtypeStruct((1, D), F32),
                   jax.ShapeDtypeStruct((S, win_t.shape[0]), BF)),
        in_specs=[dil(d) for d in DILS] + [row(2 * GMLP_W), row(2 * D), _resident(win_t.shape), row(128), row(128), row(D), row(D),
                                            full((1, D))],
        out_specs=(row(D), full((1, D)), row(win_t.shape[0])),
        scratch_shapes=[pltpu.VMEM((2, tm, 128), F32)],
        compiler_params=_cp(("arbitrary",)))(*dqkvs, dpuz, dgates, win_t, cos_t, sin_t, x, dx2, g1)


def _row_tile(rows, cols):
    cap = max(16, ELEMENTWISE_BLOCK_BYTES // (4 * cols))
    return max(t for t in range(16, cap + 1, 16) if rows % t == 0)


def _adam_math(w, g, m, v):
    m2 = ADAM_B1 * m + (1.0 - ADAM_B1) * g
    v2 = ADAM_B2 * v + (1.0 - ADAM_B2) * (g * g)
    m_hat = m2 / (1.0 - ADAM_B1 ** ADAM_STEP)
    v_hat = v2 / (1.0 - ADAM_B2 ** ADAM_STEP)
    delta = -ADAM_LR * (m_hat / (jnp.sqrt(v_hat) + ADAM_EPS) + ADAM_WD * w)
    return delta, m2, v2


def _adam_shard(own, recv, w, m, v, name):
    R, C = w.shape
    tr = _row_tile(R, C)

    def body(own_ref, r_ref, w_ref, m_ref, v_ref, g_ref, d_ref, m2_ref, v2_ref):
        g = own_ref[...] + r_ref[0].astype(F32) + r_ref[1].astype(F32) + r_ref[2].astype(F32)
        g_ref[...] = g
        d_ref[...], m2_ref[...], v2_ref[...] = _adam_math(w_ref[...], g, m_ref[...], v_ref[...])

    spec = pl.BlockSpec((tr, C), lambda i: (i, 0))
    out = jax.ShapeDtypeStruct((R, C), F32)
    return pl.pallas_call(
        body, name=name, grid=(R // tr,), out_shape=(out, out, out, out),
        in_specs=[spec, pl.BlockSpec((3, tr, C), lambda i: (0, i, 0)), spec, spec, spec],
        out_specs=(spec, spec, spec, spec), compiler_params=_cp(("parallel",)))(own, recv, w, m, v)


def _rs_add(gblocks, recv, idx, name):
    _, R, C = gblocks.shape
    tr = _row_tile(R, C)

    def body(t_ref, g_ref, r_ref, own_ref, send_ref):
        j = pl.program_id(1)
        s = g_ref[...] + r_ref[...]

        @pl.when(j == 0)
        def _():
            own_ref[...] = s

        @pl.when(j > 0)
        def _():
            send_ref[...] = s.astype(BF)

    grid_spec = pltpu.PrefetchScalarGridSpec(
        num_scalar_prefetch=1, grid=(R // tr, 4),
        in_specs=[pl.BlockSpec((None, tr, C), lambda i, j, t: (t[j], i, 0)),
                  pl.BlockSpec((None, tr, C), lambda i, j, t: (t[4 + j], i, 0))],
        out_specs=[pl.BlockSpec((tr, C), lambda i, j, t: (i, 0)),
                   pl.BlockSpec((None, tr, C), lambda i, j, t: (jnp.maximum(j - 1, 0), i, 0))])
    return pl.pallas_call(
        body, name=name, grid_spec=grid_spec,
        out_shape=(jax.ShapeDtypeStruct((R, C), F32), jax.ShapeDtypeStruct((3, R, C), BF)),
        compiler_params=_cp(("parallel", "arbitrary")))(idx, gblocks, recv)


def _mesh_pos():
    return lax.axis_index("x"), lax.axis_index("y"), lax.axis_index("c")


_HBM = pl.BlockSpec(memory_space=pltpu.HBM)
_SEM = pl.BlockSpec(memory_space=pltpu.SEMAPHORE)
_EFFECT = pltpu.SideEffectType.DATAFLOW_SIDE_EFFECTING
_RELATIONS = [(dx, dy, dc) for dx in (0, 1) for dy in (0, 1) for dc in (0, 1)][1:]


def _flip(v, d):
    return 1 - v if d else v


def _plan_gather(n):
    def plan(x, y, c):
        return [(k, None, 4 * x + 2 * y + c, (_flip(x, dx), _flip(y, dy), _flip(c, dc)))
                for k in range(n) for dx, dy, dc in _RELATIONS]
    return plan


def _plan_gather_near(x, y, c):
    chips = [(1 - x, y), (x, 1 - y), (1 - x, 1 - y)]
    return [(0, None, 4 * x + 2 * y + c, (x, y, 1 - c))] + [(0, None, 4 * x + 2 * y + c, (*chip, c)) for chip in chips]


def _plan_gather_pass(from_landing):
    def plan(x, y, c):
        blocks = [4 * cx + 2 * cy + c for cx, cy in ((1 - x, y), (x, 1 - y), (1 - x, 1 - y))]
        return [(0, b if from_landing else None, b, (x, y, 1 - c)) for b in blocks]
    return plan


def _plan_d2d(n):
    def plan(x, y, c):
        return [(k, 2 * kk + 1 - c, kk, (x, y, 1 - c)) for k in range(n) for kk in range(4)]
    return plan


def _plan_ici(n):
    def plan(x, y, c):
        return [(k, j, j, (_flip(x, dx), _flip(y, dy), c))
                for k in range(n) for j, (dx, dy) in enumerate(((1, 0), (0, 1), (1, 1)))]
    return plan


def _plan_copies(plan, src_refs, land_refs, send_sems, recv_sems):
    x, y, c = _mesh_pos()
    return [pltpu.make_async_remote_copy(
        src_ref=src_refs[k] if si is None else src_refs[k].at[si], dst_ref=land_refs[k].at[di],
        send_sem=send_sems.at[n], recv_sem=recv_sems.at[n], device_id=dev, device_id_type=MESH)
        for n, (k, si, di, dev) in enumerate(plan(x, y, c))]


def _exchange_start(srcs, land_shapes, plan, ncopies, name, after):
    n = len(srcs)

    def body(*refs):
        src_refs, land_refs = refs[:n], refs[n:2 * n]
        send_sems, recv_sems = refs[2 * n + len(after)], refs[2 * n + len(after) + 1]
        token = refs[-1]
        for cp in _plan_copies(plan, src_refs, land_refs, send_sems, recv_sems):
            cp.start()
        token[...] = jnp.zeros_like(token)

    lands = [pltpu.with_memory_space_constraint(lax.empty(s, a.dtype), pltpu.HBM) for s, a in zip(land_shapes, srcs)]
    srcs = [pltpu.with_memory_space_constraint(a, pltpu.HBM) for a in srcs]
    outs = pl.pallas_call(
        body, name=name,
        out_shape=(pltpu.SemaphoreType.DMA((ncopies,)), pltpu.SemaphoreType.DMA((ncopies,)))
        + tuple(pltpu.HBM(a.shape, a.dtype) for a in srcs) + tuple(pltpu.HBM(a.shape, a.dtype) for a in lands)
        + (jax.ShapeDtypeStruct((8, 128), F32),),
        in_specs=[_HBM] * (2 * n) + [pl.BlockSpec(memory_space=pl.ANY)] * len(after),
        out_specs=(_SEM, _SEM) + (_HBM,) * (2 * n) + (pl.BlockSpec(memory_space=pltpu.VMEM),),
        input_output_aliases={i: 2 + i for i in range(2 * n)},
        compiler_params=pltpu.CompilerParams(has_side_effects=_EFFECT))(*srcs, *lands, *after)
    return (outs[0], outs[1], list(outs[2:2 + n]), list(outs[2 + n:2 + 2 * n])), outs[-1]


def _exchange_forward(handle, plan, plan_fwd, nfwd, name, after):
    send_sems, recv_sems, srcs, lands = handle
    n = len(srcs)

    def body(*refs):
        src_refs, land_refs = refs[:n], refs[n:2 * n]
        outs = refs[2 * n + 2 + len(after):]
        for cp in _plan_copies(plan, src_refs, land_refs, refs[2 * n], refs[2 * n + 1]):
            cp.wait_send()
            cp.wait_recv()
        for cp in _plan_copies(plan_fwd, land_refs, land_refs, outs[0], outs[1]):
            cp.start()

    outs = pl.pallas_call(
        body, name=name,
        out_shape=(pltpu.SemaphoreType.DMA((nfwd,)), pltpu.SemaphoreType.DMA((nfwd,)))
        + tuple(pltpu.HBM(a.shape, a.dtype) for a in srcs) + tuple(pltpu.HBM(a.shape, a.dtype) for a in lands),
        in_specs=[_HBM] * (2 * n) + [_SEM, _SEM] + [pl.BlockSpec(memory_space=pl.ANY)] * len(after),
        out_specs=(_SEM, _SEM) + (_HBM,) * (2 * n), input_output_aliases={i: 2 + i for i in range(2 * n)},
        compiler_params=pltpu.CompilerParams(has_side_effects=_EFFECT))(*srcs, *lands, send_sems, recv_sems, *after)
    return outs[0], outs[1], list(outs[2:2 + n]), list(outs[2 + n:2 + 2 * n])


def _exchange_wait(handle, plan, name, after):
    send_sems, recv_sems, srcs, lands = handle
    n = len(srcs)

    def body(*refs):
        src_refs, land_refs = refs[:n], refs[n:2 * n]
        for cp in _plan_copies(plan, src_refs, land_refs, refs[2 * n], refs[2 * n + 1]):
            cp.wait_send()
            cp.wait_recv()

    outs = pl.pallas_call(
        body, name=name,
        out_shape=tuple(pltpu.HBM(a.shape, a.dtype) for a in srcs) + tuple(pltpu.HBM(a.shape, a.dtype) for a in lands),
        in_specs=[_HBM] * (2 * n) + [_SEM, _SEM] + [pl.BlockSpec(memory_space=pl.ANY)] * len(after),
        out_specs=(_HBM,) * (2 * n), input_output_aliases={i: i for i in range(2 * n)},
        compiler_params=pltpu.CompilerParams(has_side_effects=_EFFECT))(*srcs, *lands, send_sems, recv_sems, *after)
    return list(outs[:n]), list(outs[n:])


SMALL = ("g1", "g2", "g3", "g4", "ln_g", "ln_b", "b_sp", "w_sp")


def _small_sum_adam(block, parts, lands, ws, ms, vs):
    n, ns = len(parts), len(SMALL)

    def body(blk_ref, *refs):
        p_refs, l_refs = refs[:n], refs[n:2 * n]
        w_refs, m_refs, v_refs = (refs[2 * n + i * ns:2 * n + (i + 1) * ns] for i in range(3))
        o = 2 * n + 3 * ns
        loss_ref = refs[o]
        g_out, d_out, m_out, v_out = (refs[o + 1 + i * ns:o + 1 + (i + 1) * ns] for i in range(4))
        me = blk_ref[0]
        sums = []
        for k in range(n):
            acc = jnp.where(me == 0, p_refs[k][...], l_refs[k][0])
            for b in range(1, 8):
                acc = acc + jnp.where(me == b, p_refs[k][...], l_refs[k][b])
            sums.append(acc)
        loss_ref[...] = sums[0]
        for i in range(ns):
            g_out[i][...] = sums[1 + i]
            d_out[i][...], m_out[i][...], v_out[i][...] = _adam_math(w_refs[i][...], sums[1 + i], m_refs[i][...],
                                                                     v_refs[i][...])

    args = list(parts) + list(lands) + [t[k] for t in (ws, ms, vs) for k in SMALL]
    shapes = [jax.ShapeDtypeStruct(p.shape, F32) for p in parts[:1]] + [jax.ShapeDtypeStruct(p.shape, F32) for p in parts[1:]] * 4
    vmem = pl.BlockSpec(memory_space=pltpu.VMEM)
    outs = pl.pallas_call(
        body, name="small_sum_adam", out_shape=tuple(shapes),
        in_specs=[pl.BlockSpec(memory_space=pltpu.SMEM)] + [vmem] * len(args), out_specs=(vmem,) * len(shapes),
        compiler_params=pltpu.CompilerParams(vmem_limit_bytes=VMEM_LIMIT))(block, *args)
    groups = [dict(zip(SMALL, outs[1 + i * ns:1 + (i + 1) * ns])) for i in range(4)]
    return (outs[0], *groups)


def _rope_tables(S):
    half = HD // 2
    inv_freq = jnp.tile(ROPE_THETA ** (-jnp.arange(half, dtype=F32) / half), 4)
    sign = jnp.tile(jnp.concatenate([-jnp.ones(half, F32), jnp.ones(half, F32)]), 2)
    ang = jnp.arange(S, dtype=F32)[:, None] * inv_freq[None, :]
    return jnp.cos(ang), jnp.sin(ang) * sign[None, :]


def _to_blocks(g, col_sharded):
    if col_sharded:
        return g.reshape(g.shape[0], 8, g.shape[1] // 8).transpose(1, 0, 2)
    return g.reshape(8, g.shape[0] // 8, g.shape[1])


def _from_blocks(t, col_sharded):
    if col_sharded:
        return t.transpose(1, 0, 2).reshape(t.shape[1], 8 * t.shape[2])
    return t.reshape(8 * t.shape[1], t.shape[2])


class _NoComm:
    def __init__(self, late_weights):
        self._late = late_weights
        self.grads = {}

    def start_tie(self):
        return jnp.zeros((8, 128), F32)

    def late_weights(self, after):
        return self._late

    def small_start(self, loss_row, grads):
        self.small = (loss_row, grads)
        return jnp.zeros((8, 128), F32)

    def rs_start(self, key, gblocks, after=()):
        self.grads[key] = gblocks
        return jnp.zeros((8, 128), F32)

    def rs_mid(self, key, after):
        return jnp.zeros((8, 128), F32)


class _FsdpComm:
    def __init__(self, late_shards, col_sharded, after, idx, block):
        self._col, self._idx, self._block, self._rs = col_sharded, idx, block, {}
        n = len(late_shards)
        self._gather, self._token = _exchange_start(
            late_shards, [(8,) + s.shape for s in late_shards], _plan_gather(n), 7 * n, "ag_late_start", (after,))

    def start_tie(self):
        return self._token

    def late_weights(self, after):
        shards, lands = _exchange_wait(self._gather, _plan_gather(len(self._col)), "ag_late_wait", after)
        lands = [lax.dynamic_update_index_in_dim(t, s, self._block, 0) for t, s in zip(lands, shards)]
        return [_from_blocks(t, cs) for t, cs in zip(lands, self._col)]

    def small_start(self, loss_row, grads):
        parts = [loss_row] + [grads[k] for k in SMALL]
        n = len(parts)
        self._small, token = _exchange_start(parts, [(8,) + p.shape for p in parts], _plan_gather(n), 7 * n,
                                             "small_start", (self._token,))
        return token

    def small_finish(self, ws, ms, vs, after):
        n = 1 + len(SMALL)
        parts, lands = _exchange_wait(self._small, _plan_gather(n), "small_wait", after)
        return _small_sum_adam(jnp.reshape(self._block, (1,)).astype(jnp.int32), parts, lands, ws, ms, vs)

    def rs_start(self, key, gblocks, after=()):
        n = len(gblocks)
        d2d, token = _exchange_start(gblocks, [(4,) + g.shape[1:] for g in gblocks], _plan_d2d(n), 4 * n,
                                     "rs_%s_d2d_start" % key, (self._token,) + tuple(after))
        self._rs[key] = dict(n=n, d2d=d2d)
        return token

    def rs_mid(self, key, after):
        st = self._rs[key]
        gblocks, from_sib = _exchange_wait(st["d2d"], _plan_d2d(st["n"]), "rs_%s_d2d_wait" % key, after)
        halves = [_rs_add(g, r, self._idx, "rs_add_%s_%d" % (key, k)) for k, (g, r) in enumerate(zip(gblocks, from_sib))]
        st["own"] = [own for own, _ in halves]
        sends = [send for _, send in halves]
        st["ici"], token = _exchange_start(sends, [t.shape for t in sends], _plan_ici(st["n"]), 3 * st["n"],
                                           "rs_%s_ici_start" % key, (self._token,))
        return token

    def rs_end(self, key, after):
        st = self._rs[key]
        return st["own"], _exchange_wait(st["ici"], _plan_ici(st["n"]), "rs_%s_ici_wait" % key, after)[1]


def _local_step(x, tgt, h1, cos_t, sin_t, win_t, comm, g1, g2, g3, g4, ln_g, ln_b, w_sp, b_sp):
    b_sp_t = b_sp.T
    w_sp_t = w_sp.transpose(0, 2, 1)

    *qkvs, puz, gates = _proj_all(h1, win_t, cos_t, sin_t, comm.start_tie())
    fwd = _attn_fwd(qkvs)
    ya, *lses = _attn_combine([o for o, _ in fwd], [l for _, l in fwd])
    yg = _gmlp_fwd(puz, ln_g, ln_b, w_sp, b_sp_t)
    wba, wbg, wout, wmi, wmo = comm.late_weights(after=(ya, yg, gates))
    merged, y, x2, h3 = _merge_fwd(ya, yg, gates, wba, wbg, wout, x, g2, g3)
    a, dy2, dout, loss_row, dg4 = _mlp_fwd(h3, wmi, wmo, x2, tgt, g4)

    dpre, dx2, dy, dg3, dg2 = _mlp_bwd(dy2, a, wmo, wmi, x2, y, dout, g2, g3)
    dwmo = _mm_tn(a, dy2, "dw_mlp_out", square_a=True)
    dwmi = _mm_tn(h3, dpre, "dw_mlp_in", col_blocks=8)
    tie = comm.rs_start("mlp", [dwmi, _to_blocks(dwmo, False)])
    dgates, da, db, dyg, *rest = _outproj_bwd(dy, wout, ya, yg, gates, wba, wbg, tie)
    dyas, dsums = rest[:3], rest[3:]
    tie = comm.rs_mid("mlp", after=(dyg,))
    dpuz, dwsp, dbs, dlng, dlnb = _gmlp_bwd(puz, dyg, ln_g + tie[0, 0], ln_b, w_sp, w_sp_t, b_sp_t)
    dqkvs = _attn_bwd(qkvs, dyas, dsums, lses)
    grad_x, dg1, dproj = _inproj_bwd(dqkvs, dpuz, dgates, win_t, cos_t, sin_t, x, dx2, g1)
    small = dict(g1=dg1, g2=dg2, g3=dg3, g4=dg4, ln_g=dlng, ln_b=dlnb, b_sp=dbs.reshape(4, CHUNK),
                 w_sp=dwsp.reshape(4 * CHUNK, CHUNK))
    tie = comm.small_start(loss_row, small)
    dwin_t = _mm_tn(dproj, h1, "dw_in", tie=tie)
    tie = comm.rs_start("win", [_to_blocks(dwin_t, False)])
    dwout = _mm_tn(merged, dy, "dw_out", tie=tie)
    tie = comm.rs_mid("win", after=(dwout,))
    dwba = _mm_tn(ya, da, "dw_branch_attn", tie=tie, col_blocks=8)
    dwbg = _mm_tn(yg, db, "dw_branch_gmlp", tie=tie, col_blocks=8)
    tie = comm.rs_start("mid", [dwba, dwbg, _to_blocks(dwout, False)], after=(tie,))
    return grad_x, tie


def kernel(x, norm_pre_mix, w_in, w_spatial, b_spatial, ln_v_gain, ln_v_bias, w_branch_attn, w_branch_gmlp, w_out, norm_post_mix, norm_pre_mlp, w_mlp_in, w_mlp_out, norm_post_mlp, loss_target, m_norm_pre_mix, m_w_in, m_w_spatial, m_b_spatial, m_ln_v_gain, m_ln_v_bias, m_w_branch_attn, m_w_branch_gmlp, m_w_out, m_norm_post_mix, m_norm_pre_mlp, m_w_mlp_in, m_w_mlp_out, m_norm_post_mlp, v_norm_pre_mix, v_w_in, v_w_spatial, v_b_spatial, v_ln_v_gain, v_ln_v_bias, v_w_branch_attn, v_w_branch_gmlp, v_w_out, v_norm_post_mix, v_norm_pre_mlp, v_w_mlp_in, v_w_mlp_out, v_norm_post_mlp):
    mx, my, mc = _mesh_pos()
    rel = [(0, 0), (1, 0), (0, 1), (1, 1)]
    chip_of = [2 * (mx ^ dx) + (my ^ dy) for dx, dy in rel]
    idx = jnp.stack([2 * k + mc for k in chip_of] + chip_of).astype(jnp.int32)

    w_in_t, m_w_in_t, v_w_in_t = (t[0].T for t in (w_in, m_w_in, v_w_in))
    shard = w_in_t.astype(BF)
    gather, token = _exchange_start([shard], [(8,) + shard.shape], _plan_gather_near, 4, "ag_win_start", ())
    h1 = _rms_fwd(x[0], norm_pre_mix + token[0, 0], "rms_pre_mix")
    cos_t, sin_t = _rope_tables(x.shape[1])
    gather = _exchange_forward(gather, _plan_gather_near, _plan_gather_pass(True), 3, "ag_win_pass", (h1, cos_t, sin_t))
    (shard,), (g_win,) = _exchange_wait(gather, _plan_gather_pass(False), "ag_win_wait", ())
    g_win = lax.dynamic_update_index_in_dim(g_win, shard, 4 * mx + 2 * my + mc, 0)
    win_t = _from_blocks(g_win, False)
    late = [w_branch_attn[0], w_branch_gmlp[0], w_out[0], w_mlp_in[0], w_mlp_out[0]]
    comm = _FsdpComm([w.astype(BF) for w in late], [True, True, False, True, False], g_win, idx, 4 * mx + 2 * my + mc)

    grad_x, tie = _local_step(
        x[0], loss_target[0], h1, cos_t, sin_t, win_t, comm,
        norm_pre_mix, norm_post_mix, norm_pre_mlp, norm_post_mlp, ln_v_gain, ln_v_bias, w_spatial[0], b_spatial[0])

    flat = lambda t: t.reshape(-1, t.shape[-1])
    small_w = dict(zip(SMALL, map(flat, (norm_pre_mix, norm_post_mix, norm_pre_mlp, norm_post_mlp, ln_v_gain, ln_v_bias,
                                         b_spatial, w_spatial))))
    small_m = dict(zip(SMALL, map(flat, (m_norm_pre_mix, m_norm_post_mix, m_norm_pre_mlp, m_norm_post_mlp, m_ln_v_gain,
                                         m_ln_v_bias, m_b_spatial, m_w_spatial))))
    small_v = dict(zip(SMALL, map(flat, (v_norm_pre_mix, v_norm_post_mix, v_norm_pre_mlp, v_norm_post_mlp, v_ln_v_gain,
                                         v_ln_v_bias, v_b_spatial, v_w_spatial))))
    mlp_own, mlp_chips = comm.rs_end("mlp", after=(tie,))
    adam = lambda nm, own, r, w, m, v: _adam_shard(own, r, w[0], m[0], v[0], "adam_" + nm)
    upd = {
        "w_mlp_in": adam("w_mlp_in", mlp_own[0], mlp_chips[0], w_mlp_in, m_w_mlp_in, v_w_mlp_in),
        "w_mlp_out": adam("w_mlp_out", mlp_own[1], mlp_chips[1], w_mlp_out, m_w_mlp_out, v_w_mlp_out),
    }
    loss_out, sg, sd, sm, sv = comm.small_finish(small_w, small_m, small_v, after=(upd["w_mlp_in"][0], upd["w_mlp_out"][0]))
    loss = loss_out[0, 0]
    tie = comm.rs_mid("mid", after=(loss_out,))
    win_own, win_chips = comm.rs_end("win", after=(tie,))
    upd["w_in"] = tuple(t.T for t in _adam_shard(win_own[0], win_chips[0], w_in_t, m_w_in_t, v_w_in_t, "adam_w_in"))
    mid_own, mid_chips = comm.rs_end("mid", after=(upd["w_in"][0],))
    upd["w_branch_attn"] = adam("w_branch_attn", mid_own[0], mid_chips[0], w_branch_attn, m_w_branch_attn, v_w_branch_attn)
    upd["w_branch_gmlp"] = adam("w_branch_gmlp", mid_own[1], mid_chips[1], w_branch_gmlp, m_w_branch_gmlp, v_w_branch_gmlp)
    upd["w_out"] = adam("w_out", mid_own[2], mid_chips[2], w_out, m_w_out, v_w_out)
    order = ["g1", "w_in", "w_sp", "b_sp", "ln_g", "ln_b", "w_branch_attn", "w_branch_gmlp", "w_out", "g2", "g3",
             "w_mlp_in", "w_mlp_out", "g4"]
    small_shape = dict(g1=norm_pre_mix.shape, g2=norm_post_mix.shape, g3=norm_pre_mlp.shape, g4=norm_post_mlp.shape,
                       ln_g=ln_v_gain.shape, ln_b=ln_v_bias.shape, b_sp=b_spatial.shape, w_sp=w_spatial.shape)

    def pick(which):
        return [upd[nm][which][None] if nm in upd else (sg, sd, sm, sv)[which][nm].reshape(small_shape[nm])
                for nm in order]

    return (loss, grad_x[None], *pick(0), *pick(1), *pick(2), *pick(3))
```

```python
import functools
import math

import jax
import jax.numpy as jnp
from jax import lax
from jax.experimental import pallas as pl
from jax.experimental.pallas import tpu as pltpu

D = 1024
HD = 64
NSLOT = 4
GW = NSLOT * HD
DILS = (1, 4, 16)
QB = 128
ATTN_NSUB = 4
PROJ_TM = 1024
MLP_TM_FWD = 512
MLP_TM_BWD = 256
MM_TN_TK = 4096
ELEMENTWISE_BLOCK_BYTES = 1 << 20
AW = 3 * GW
W_UZ0, W_G0 = 3 * AW, 3 * AW + 1024
GMLP_W = 512
CHUNK = 128
DFF = 4096
EPS = 1e-6
ROPE_THETA = 10000.0
SCALE = HD ** -0.5
NEG = -1e30

ADAM_LR = 0.001
ADAM_B1 = 0.9
ADAM_B2 = 0.999
ADAM_EPS = 1e-08
ADAM_WD = 0.01
ADAM_STEP = 10

BF = jnp.bfloat16
F32 = jnp.float32
MESH = pl.DeviceIdType.MESH
VMEM_LIMIT = 56 * 1024 * 1024


def _cp(sem):
    return pltpu.CompilerParams(dimension_semantics=sem, vmem_limit_bytes=VMEM_LIMIT)


def _dot(a, b):
    return jnp.dot(a, b, preferred_element_type=F32)


def _dot_nt(a, b):
    return lax.dot_general(a, b, (((1,), (1,)), ((), ())), preferred_element_type=F32)


def _dot_tn(a, b):
    return lax.dot_general(a, b, (((0,), (0,)), ((), ())), preferred_element_type=F32)


def _gelu(x):
    return jax.nn.gelu(x, approximate=True)


def _gelu_grad(x):
    k = math.sqrt(2.0 / math.pi)
    t = jnp.tanh(k * (x + 0.044715 * x * x * x))
    return 0.5 * (1.0 + t) + 0.5 * x * (1.0 - t * t) * (k * (1.0 + 3.0 * 0.044715 * x * x))


def _swap_halves(t):
    w = t.shape[1]
    lane = lax.broadcasted_iota(jnp.int32, t.shape, 1)
    first = (lane & (HD - 1)) < (HD // 2)
    return jnp.where(first, pltpu.roll(t, w - HD // 2, 1), pltpu.roll(t, HD // 2, 1))


def _head_mask(hh):
    lane = lax.broadcasted_iota(jnp.int32, (1, GW), 1)
    return jnp.logical_and(lane >= hh * HD, lane < (hh + 1) * HD)


def _rms_stats(xf):
    return lax.rsqrt(jnp.mean(xf * xf, axis=-1, keepdims=True) + EPS)


def _rms_bwd(xf, r, gain, dout):
    n = xf * r
    t = dout * gain
    dx = r * (t - n * jnp.mean(t * n, axis=-1, keepdims=True))
    return dx, jnp.sum(dout * n, axis=0, keepdims=True)


def _rms_fwd(x, gain, name):
    S = x.shape[0]
    tm = min(512, S)

    def body(x_ref, g_ref, h_ref):
        xf = x_ref[...]
        h_ref[...] = (xf * _rms_stats(xf) * g_ref[...]).astype(BF)

    return pl.pallas_call(
        body, name=name, out_shape=jax.ShapeDtypeStruct((S, D), BF), grid=(S // tm,),
        in_specs=[pl.BlockSpec((tm, D), lambda i: (i, 0)), pl.BlockSpec((1, D), lambda i: (0, 0))],
        out_specs=pl.BlockSpec((tm, D), lambda i: (i, 0)), compiler_params=_cp(("parallel",)))(x, gain)


def _dilate_store(val, scr, o_ref, lead, d):
    rows = val.shape[0]
    if d == 1:
        o_ref[lead + (0,)] = val.astype(o_ref.dtype)
        return
    for hf in range(2):
        scr[hf, pl.ds(0, rows), :] = val[:, hf * 128:(hf + 1) * 128]
    for r in range(d):
        for hf in range(2):
            o_ref[lead + (r, slice(None), slice(hf * 128, (hf + 1) * 128))] = (
                scr[hf, pl.ds(r, rows // d, stride=d), :].astype(o_ref.dtype))


def _undilate_load(i_ref, lead, d, scr, rows):
    if d == 1:
        return i_ref[lead + (0,)].astype(F32)
    for r in range(d):
        for hf in range(2):
            scr[hf, pl.ds(r, rows // d, stride=d), :] = (
                i_ref[lead + (r, slice(None), slice(hf * 128, (hf + 1) * 128))].astype(F32))
    return jnp.concatenate([scr[0, pl.ds(0, rows), :], scr[1, pl.ds(0, rows), :]], axis=1)


def _rope_fwd(y, c_ref, s_ref):
    cosv = jnp.concatenate([c_ref[...]] * 2, axis=1)
    sinv = jnp.concatenate([s_ref[...]] * 2, axis=1)
    return y * cosv + _swap_halves(y) * sinv


def _rope_bwd(dy, c_ref, s_ref):
    cosv = jnp.concatenate([c_ref[...]] * 2, axis=1)
    sinv = jnp.concatenate([s_ref[...]] * 2, axis=1)
    return dy * cosv + _swap_halves(dy * sinv)


def _proj_all(h, win_t, cos_t, sin_t, tie):
    S, K = h.shape
    tm = min(PROJ_TM, S)

    def body(h_ref, w_ref, c_ref, s_ref, tie_ref, o0_ref, o1_ref, o2_ref, p_ref, gt_ref, scr):
        hv = h_ref[...]
        col = lambda b: _dot_nt(hv, w_ref[b * GW:(b + 1) * GW, :])
        ys = [col(3 * t) for t in range(3)]
        for g, (o_ref, d) in enumerate(zip((o0_ref, o1_ref, o2_ref), DILS)):
            nxt = [col(3 * t + g + 1) if g < 2 else col(W_UZ0 // GW + t) for t in range(3)]
            for t in range(3):
                y = _rope_fwd(ys[t], c_ref, s_ref) if t < 2 else ys[t]
                _dilate_store(y, scr.at[t], o_ref, (t,), d)
            ys = nxt
        for b in range((W_G0 - W_UZ0) // GW):
            y = ys[b] if b < 3 else col(W_UZ0 // GW + b)
            p_ref[:, b * GW:(b + 1) * GW] = y.astype(BF)
        for b in range(2 * D // GW):
            gt_ref[:, b * GW:(b + 1) * GW] = jax.nn.sigmoid(col(W_G0 // GW + b)).astype(BF)

    row = lambda w: pl.BlockSpec((tm, w), lambda i: (i, 0))
    return pl.pallas_call(
        body, name="proj_all", grid=(S // tm,),
        out_shape=tuple(jax.ShapeDtypeStruct((3, d, S // d, GW), BF) for d in DILS)
        + (jax.ShapeDtypeStruct((S, W_G0 - W_UZ0), BF), jax.ShapeDtypeStruct((S, 2 * D), BF)),
        in_specs=[row(K), _resident(win_t.shape), row(128), row(128), pl.BlockSpec(memory_space=pl.ANY)],
        out_specs=tuple(pl.BlockSpec((3, d, tm // d, GW), lambda i: (0, 0, i, 0)) for d in DILS)
        + (row(W_G0 - W_UZ0), row(2 * D)),
        scratch_shapes=[pltpu.VMEM((3, 2, tm, 128), F32)],
        compiler_params=_cp(("parallel",)))(h, win_t, cos_t, sin_t, tie)


def _band_masks(first_step):
    row = lax.broadcasted_iota(jnp.int32, (QB, 2 * QB), 0)
    col = lax.broadcasted_iota(jnp.int32, (QB, 2 * QB), 1)
    band = jnp.logical_and(col >= row, col <= row + QB)
    return band, jnp.logical_and(band, jnp.logical_or(col >= QB, jnp.logical_not(first_step)))


def _group_step(k, nsteps):
    return lambda g, n: jnp.where(g == k, n, jnp.where(g < k, 0, nsteps - 1))


def _attn_fwd(qkvs):
    S = qkvs[0].shape[1] * qkvs[0].shape[2]
    nsub = min(ATTN_NSUB, S // max(DILS) // QB)
    R = nsub * QB
    nsteps = S // R

    def group_body(first, q_ref, kp_ref, kc_ref, vp_ref, vc_ref, o_ref, lse_ref):
        band, band_first = _band_masks(first)
        kfull = jnp.concatenate([kp_ref[...], kc_ref[...]], axis=0)
        vfull = jnp.concatenate([vp_ref[...], vc_ref[...]], axis=0)
        chains = [(sb, hh) for sb in range(nsub) for hh in range(NSLOT)]
        win = lambda t, sb: t[sb * QB:(sb + 2) * QB]
        scores = []
        for sb, hh in chains:
            qh = jnp.where(_head_mask(hh), q_ref[sb * QB:(sb + 1) * QB, :], 0)
            scores.append(_dot_nt(qh, win(kfull, sb)))
        soft = []
        for (sb, hh), sc in zip(chains, scores):
            sc = jnp.where(band_first if sb == 0 else band, sc * SCALE, NEG)
            m = jnp.max(sc, axis=1, keepdims=True)
            p = jnp.exp(sc - m)
            den = jnp.sum(p, axis=1, keepdims=True)
            soft.append((p.astype(BF), den, m + jnp.log(den)))
        accs = [_dot(p, win(vfull, sb)) for (sb, hh), (p, _, _) in zip(chains, soft)]
        for sb in range(nsub):
            o = jnp.zeros((QB, GW), F32)
            lse = jnp.zeros((QB, GW), F32)
            for hh in range(NSLOT):
                hm = _head_mask(hh)
                _, den, lrow = soft[sb * NSLOT + hh]
                o = o + jnp.where(hm, accs[sb * NSLOT + hh] / den, 0.0)
                lse = lse + jnp.where(hm, lrow, 0.0)
            o_ref[sb * QB:(sb + 1) * QB, :] = o
            lse_ref[sb * QB:(sb + 1) * QB, :] = lse

    def body(*refs):
        g, n = pl.program_id(0), pl.program_id(1)
        for k, d in enumerate(DILS):
            @pl.when(g == k)
            def _():
                group_body(n % (nsteps // d) == 0, *refs[5 * k:5 * k + 5], *refs[15 + 2 * k:17 + 2 * k])

    in_specs, out_specs, operands = [], [], []
    for k in range(len(DILS)):
        at = _group_step(k, nsteps)
        cur = lambda t, at=at: pl.BlockSpec((None, R, GW), lambda g, n: (t, at(g, n), 0))
        prv = lambda t, at=at: pl.BlockSpec((None, QB, GW), lambda g, n: (t, jnp.maximum(at(g, n) * nsub - 1, 0), 0))
        in_specs += [cur(0), prv(1), cur(1), prv(2), cur(2)]
        out_specs += [pl.BlockSpec((R, GW), lambda g, n, at=at: (at(g, n), 0))] * 2
        operands += [qkvs[k].reshape(3, S, GW)] * 5
    outs = pl.pallas_call(
        body, name="attn_fwd", grid=(len(DILS), nsteps),
        out_shape=(jax.ShapeDtypeStruct((S, GW), F32),) * (2 * len(DILS)),
        in_specs=in_specs, out_specs=tuple(out_specs),
        compiler_params=_cp(("arbitrary", "arbitrary")))(*operands)
    return [(outs[2 * k].reshape(d, S // d, GW), outs[2 * k + 1].reshape(d, S // d, GW)) for k, d in enumerate(DILS)]


def _attn_combine(os_, lses):
    S = os_[0].shape[1]
    tm = min(1024, S)

    def body(o0, o1, o2, l0, l1, l2, y_ref, j0, j1, j2, scr):
        os_nat = [_undilate_load(o, (), d, scr, tm) for o, d in zip((o0, o1, o2), DILS)]
        a, b, c = [_undilate_load(l, (), d, scr, tm) for l, d in zip((l0, l1, l2), DILS)]
        m = jnp.maximum(jnp.maximum(a, b), c)
        wa, wb, wc = jnp.exp(a - m), jnp.exp(b - m), jnp.exp(c - m)
        den = wa + wb + wc
        y_ref[...] = ((wa * os_nat[0] + wb * os_nat[1] + wc * os_nat[2]) / den).astype(BF)
        lse = m + jnp.log(den)
        for j_ref, d in zip((j0, j1, j2), DILS):
            _dilate_store(lse, scr, j_ref, (), d)

    dil = lambda d: pl.BlockSpec((d, tm // d, GW), lambda i: (0, i, 0))
    dshape = lambda d: jax.ShapeDtypeStruct((d, S // d, GW), F32)
    return pl.pallas_call(
        body, name="attn_combine", grid=(S // tm,),
        out_shape=(jax.ShapeDtypeStruct((S, GW), BF),) + tuple(dshape(d) for d in DILS),
        in_specs=[dil(d) for d in DILS] * 2,
        out_specs=(pl.BlockSpec((tm, GW), lambda i: (i, 0)),) + tuple(dil(d) for d in DILS),
        scratch_shapes=[pltpu.VMEM((2, tm, 128), F32)],
        compiler_params=_cp(("parallel",)))(*os_, *lses)


def _tril(upper=False):
    row = lax.broadcasted_iota(jnp.int32, (CHUNK, CHUNK), 0)
    col = lax.broadcasted_iota(jnp.int32, (CHUNK, CHUNK), 1)
    return row <= col if upper else col <= row


def _ln_fwd(z, gain, bias):
    mu = jnp.mean(z, axis=-1, keepdims=True)
    zc = z - mu
    rstd = lax.rsqrt(jnp.mean(zc * zc, axis=-1, keepdims=True) + EPS)
    zhat = zc * rstd
    return zhat, rstd, zhat * gain + bias


def _gmlp_fwd(puz, ln_g, ln_b, w_sp, b_sp_t):
    S = puz.shape[0]
    tm = min(512, S)
    nch = tm // CHUNK

    def body(p_ref, g_ref, b_ref, w_ref, bt_ref, o_ref):
        tril = _tril()
        ws = [jnp.where(tril, w_ref[gg], 0.0).astype(BF) for gg in range(4)]
        for ch in range(nch):
            rows = slice(ch * CHUNK, (ch + 1) * CHUNK)
            z = _gelu(p_ref[rows, GMLP_W:].astype(F32))
            _, _, zn = _ln_fwd(z, g_ref[...], b_ref[...])
            zn = zn.astype(BF)
            for gg in range(4):
                cols = slice(gg * CHUNK, (gg + 1) * CHUNK)
                sz = _dot(ws[gg], zn[:, cols]) + bt_ref[:, gg:gg + 1]
                u = _gelu(p_ref[rows, cols].astype(F32))
                o_ref[rows, cols] = (u * sz).astype(BF)

    return pl.pallas_call(
        body, name="gmlp_fwd", out_shape=jax.ShapeDtypeStruct((S, GMLP_W), BF), grid=(S // tm,),
        in_specs=[pl.BlockSpec((tm, 2 * GMLP_W), lambda i: (i, 0)),
                  pl.BlockSpec((1, GMLP_W), lambda i: (0, 0)), pl.BlockSpec((1, GMLP_W), lambda i: (0, 0)),
                  pl.BlockSpec((4, CHUNK, CHUNK), lambda i: (0, 0, 0)), pl.BlockSpec((CHUNK, 4), lambda i: (0, 0))],
        out_specs=pl.BlockSpec((tm, GMLP_W), lambda i: (i, 0)),
        compiler_params=_cp(("parallel",)))(puz, ln_g, ln_b, w_sp, b_sp_t)


def _merge_fwd(ya, yg, gates, wba, wbg, wout, x, g2, g3):
    S = x.shape[0]
    tm = min(512, S)

    def body(ya_ref, yg_ref, gt_ref, wba_ref, wbg_ref, wo_ref, x_ref, g2_ref, g3_ref,
             mg_ref, y_ref, x2_ref, h3_ref):
        a = _dot(ya_ref[...], wba_ref[...])
        b = _dot(yg_ref[...], wbg_ref[...])
        merged = (gt_ref[:, :D].astype(F32) * a + gt_ref[:, D:].astype(F32) * b).astype(BF)
        mg_ref[...] = merged
        y = _dot(merged, wo_ref[...])
        y_ref[...] = y
        x2 = x_ref[...] + y * _rms_stats(y) * g2_ref[...]
        x2_ref[...] = x2
        h3_ref[...] = (x2 * _rms_stats(x2) * g3_ref[...]).astype(BF)

    row = lambda w: pl.BlockSpec((tm, w), lambda i: (i, 0))
    full = lambda s: pl.BlockSpec(s, lambda i: (0, 0))
    return pl.pallas_call(
        body, name="merge_fwd", grid=(S // tm,),
        out_shape=(jax.ShapeDtypeStruct((S, D), BF), jax.ShapeDtypeStruct((S, D), F32), jax.ShapeDtypeStruct((S, D), F32),
                   jax.ShapeDtypeStruct((S, D), BF)),
        in_specs=[row(GW), row(GMLP_W), row(2 * D), full((GW, D)), full((GMLP_W, D)), full((D, D)), row(D),
                  full((1, D)), full((1, D))],
        out_specs=(row(D), row(D), row(D), row(D)),
        compiler_params=_cp(("parallel",)))(ya, yg, gates, wba, wbg, wout, x, g2, g3)


def _resident(shape):
    return pl.BlockSpec(shape, lambda i: (0,) * len(shape), pipeline_mode=pl.Buffered(1))


def _mlp_fwd(h3, wmi, wmo, x2, tgt, g4):
    S = x2.shape[0]
    tm = min(MLP_TM_FWD, S)

    def body(h_ref, wi_ref, wo_ref, x2_ref, t_ref, g4_ref, a_ref, dy2_ref, dout_ref, loss_ref, dg4_ref):
        @pl.when(pl.program_id(0) == 0)
        def _():
            loss_ref[...] = jnp.zeros_like(loss_ref)
            dg4_ref[...] = jnp.zeros_like(dg4_ref)

        halves = [slice(hh * (tm // 2), (hh + 1) * (tm // 2)) for hh in range(2)]
        acts = []
        for rows in halves:
            a = jnp.maximum(_dot(h_ref[rows, :], wi_ref[...]), 0.0)
            a_ref[rows, :] = a.astype(BF)
            acts.append((a * a).astype(BF))
        y2s = [_dot(a2, wo_ref[...]) for a2 in acts]
        lane = lax.broadcasted_iota(jnp.int32, (1, 128), 1)
        for rows, y2 in zip(halves, y2s):
            r = _rms_stats(y2)
            out = x2_ref[rows, :] + y2 * r * g4_ref[...]
            err = out - t_ref[rows, :]
            tot = jnp.sum(jnp.sum(err * err, axis=1, keepdims=True), axis=0, keepdims=True) * (0.5 / D)
            loss_ref[...] += jnp.where(lane == 0, tot, 0.0)
            dout = err * (1.0 / D)
            dout_ref[rows, :] = dout
            dy2, dg = _rms_bwd(y2, r, g4_ref[...], dout)
            dy2_ref[rows, :] = dy2.astype(BF)
            dg4_ref[...] += dg

    row = pl.BlockSpec((tm, D), lambda i: (i, 0))
    return pl.pallas_call(
        body, name="mlp_fwd", grid=(S // tm,),
        out_shape=(jax.ShapeDtypeStruct((S, DFF), BF), jax.ShapeDtypeStruct((S, D), BF), jax.ShapeDtypeStruct((S, D), F32),
                   jax.ShapeDtypeStruct((1, 128), F32), jax.ShapeDtypeStruct((1, D), F32)),
        in_specs=[row, _resident((D, DFF)), _resident((DFF, D)), row, row, pl.BlockSpec((1, D), lambda i: (0, 0))],
        out_specs=(pl.BlockSpec((tm, DFF), lambda i: (i, 0)), row, row,
                   pl.BlockSpec((1, 128), lambda i: (0, 0)), pl.BlockSpec((1, D), lambda i: (0, 0))),
        compiler_params=_cp(("arbitrary",)))(h3, wmi, wmo, x2, tgt, g4)


def _mlp_bwd(dy2, a, wmo, wmi, x2, y, dout, g2, g3):
    S = x2.shape[0]
    tm = min(MLP_TM_BWD, S)

    def body(dy2_ref, a_ref, wo_ref, wi_ref, x2_ref, y_ref, dout_ref, g2_ref, g3_ref,
             dpre_ref, dx2_ref, dy_ref, dg3_ref, dg2_ref):
        @pl.when(pl.program_id(0) == 0)
        def _():
            dg3_ref[...] = jnp.zeros_like(dg3_ref)
            dg2_ref[...] = jnp.zeros_like(dg2_ref)

        da2 = _dot_nt(dy2_ref[...], wo_ref[...])
        dpre = (2.0 * a_ref[...].astype(F32) * da2).astype(BF)
        dpre_ref[...] = dpre
        dh3 = _dot_nt(dpre, wi_ref[...])
        x2 = x2_ref[...]
        dx3, dg3 = _rms_bwd(x2, _rms_stats(x2), g3_ref[...], dh3)
        dx2 = dout_ref[...] + dx3
        dx2_ref[...] = dx2
        dg3_ref[...] += dg3
        yv = y_ref[...]
        dy, dg2 = _rms_bwd(yv, _rms_stats(yv), g2_ref[...], dx2)
        dy_ref[...] = dy.astype(BF)
        dg2_ref[...] += dg2

    row = pl.BlockSpec((tm, D), lambda i: (i, 0))
    wide = pl.BlockSpec((tm, DFF), lambda i: (i, 0))
    vec = pl.BlockSpec((1, D), lambda i: (0, 0))
    return pl.pallas_call(
        body, name="mlp_bwd", grid=(S // tm,),
        out_shape=(jax.ShapeDtypeStruct((S, DFF), BF), jax.ShapeDtypeStruct((S, D), F32), jax.ShapeDtypeStruct((S, D), BF),
                   jax.ShapeDtypeStruct((1, D), F32), jax.ShapeDtypeStruct((1, D), F32)),
        in_specs=[row, wide, _resident((DFF, D)), _resident((D, DFF)), row, row, row, vec, vec],
        out_specs=(wide, row, row, vec, vec),
        compiler_params=_cp(("arbitrary",)))(dy2, a, wmo, wmi, x2, y, dout, g2, g3)


def _mm_tn(a, b, name, square_a=False, tm=1024, tn=1024, tk=MM_TN_TK, tie=None, col_blocks=None):
    S, M = a.shape
    N = b.shape[1]
    tk = min(tk, S)
    tm = max(t for t in range(128, min(tm, M) + 1, 128) if M % t == 0)
    tn = max(t for t in range(128, min(tn, N) + 1, 128) if N % t == 0)
    cb = N // col_blocks if col_blocks else tn
    assert M % tm == 0 and S % tk == 0 and tn % cb == 0
    nk = S // tk
    ties = () if tie is None else (tie,)

    def body(a_ref, b_ref, *rest):
        o_ref = rest[-1]
        k = pl.program_id(2)
        av = a_ref[...]
        if square_a:
            av = av * av
        part = _dot_tn(av, b_ref[...])
        if col_blocks:
            part = jnp.stack([part[:, t * cb:(t + 1) * cb] for t in range(tn // cb)])

        @pl.when(k == 0)
        def _():
            o_ref[...] = part

        @pl.when(k > 0)
        def _():
            o_ref[...] += part

    if col_blocks:
        out_shape, out_spec = (col_blocks, M, cb), pl.BlockSpec((tn // cb, tm, cb), lambda i, j, k: (j, i, 0))
    else:
        out_shape, out_spec = (M, N), pl.BlockSpec((tm, tn), lambda i, j, k: (i, j))
    return pl.pallas_call(
        body, name=name, out_shape=jax.ShapeDtypeStruct(out_shape, F32), grid=(M // tm, N // tn, nk),
        in_specs=[pl.BlockSpec((tk, tm), lambda i, j, k: (k, i)), pl.BlockSpec((tk, tn), lambda i, j, k: (k, j))]
        + [pl.BlockSpec(memory_space=pl.ANY)] * len(ties),
        out_specs=out_spec,
        compiler_params=_cp(("parallel", "parallel", "arbitrary")))(a, b, *ties)


def _outproj_bwd(dy, wout, ya, yg, gates, wba, wbg, tie):
    S = dy.shape[0]
    tm = min(512, S)

    def body(dy_ref, wo_ref, ya_ref, yg_ref, gt_ref, wba_ref, wbg_ref, tie_ref,
             dgt_ref, da_ref, db_ref, dyg_ref, e0, e1, e2, s0, s1, s2, scr):
        dm = _dot_nt(dy_ref[...], wo_ref[...])
        ga, gb = gt_ref[:, :D].astype(F32), gt_ref[:, D:].astype(F32)
        dgt_ref[:, :D] = (dm * _dot(ya_ref[...], wba_ref[...]) * ga * (1.0 - ga)).astype(BF)
        dgt_ref[:, D:] = (dm * _dot(yg_ref[...], wbg_ref[...]) * gb * (1.0 - gb)).astype(BF)
        da = (dm * ga).astype(BF)
        db = (dm * gb).astype(BF)
        da_ref[...] = da
        db_ref[...] = db
        dyg_ref[...] = _dot_nt(db, wbg_ref[...]).astype(BF)
        dya = _dot_nt(da, wba_ref[...]).astype(BF).astype(F32)
        dyy = dya * ya_ref[...].astype(F32)
        dsum = jnp.zeros((tm, GW), F32)
        for hh in range(NSLOT):
            hm = _head_mask(hh)
            dsum = dsum + jnp.where(hm, jnp.sum(jnp.where(hm, dyy, 0.0), axis=1, keepdims=True), 0.0)
        for e_ref, s_ref, d in zip((e0, e1, e2), (s0, s1, s2), DILS):
            _dilate_store(dya, scr, e_ref, (), d)
            _dilate_store(dsum, scr, s_ref, (), d)

    row = lambda w: pl.BlockSpec((tm, w), lambda i: (i, 0))
    full = lambda s: pl.BlockSpec(s, lambda i: (0, 0))
    dil = lambda d: pl.BlockSpec((d, tm // d, GW), lambda i: (0, i, 0))
    dshape = lambda d, t: jax.ShapeDtypeStruct((d, S // d, GW), t)
    return pl.pallas_call(
        body, name="outproj_bwd", grid=(S // tm,),
        out_shape=(jax.ShapeDtypeStruct((S, 2 * D), BF), jax.ShapeDtypeStruct((S, D), BF), jax.ShapeDtypeStruct((S, D), BF),
                   jax.ShapeDtypeStruct((S, GMLP_W), BF)) + tuple(dshape(d, BF) for d in DILS)
        + tuple(dshape(d, F32) for d in DILS),
        in_specs=[row(D), full((D, D)), row(GW), row(GMLP_W), row(2 * D), full((GW, D)), full((GMLP_W, D)),
                  pl.BlockSpec(memory_space=pl.ANY)],
        out_specs=(row(2 * D), row(D), row(D), row(GMLP_W)) + tuple(dil(d) for d in DILS) * 2,
        scratch_shapes=[pltpu.VMEM((2, tm, 128), F32)],
        compiler_params=_cp(("parallel",)))(dy, wout, ya, yg, gates, wba, wbg, tie)


def _gmlp_bwd(puz, dyg, ln_g, ln_b, w_sp, w_sp_t, b_sp_t):
    S = puz.shape[0]
    tm = min(512, S)
    nch = tm // CHUNK

    def body(p_ref, dy_ref, g_ref, b_ref, w_ref, wt_ref, bt_ref,
             dp_ref, dw_ref, dbs_ref, dg_ref, dbias_ref, dbacc_ref):
        i = pl.program_id(0)

        @pl.when(i == 0)
        def _():
            dw_ref[...] = jnp.zeros_like(dw_ref)
            dbacc_ref[...] = jnp.zeros_like(dbacc_ref)
            dg_ref[...] = jnp.zeros_like(dg_ref)
            dbias_ref[...] = jnp.zeros_like(dbias_ref)

        tril = _tril()
        ws = [jnp.where(tril, w_ref[gg], 0.0).astype(BF) for gg in range(4)]
        triu = _tril(upper=True)
        wts = [jnp.where(triu, wt_ref[gg], 0.0).astype(BF) for gg in range(4)]
        gain = g_ref[...]
        for ch in range(nch):
            rows = slice(ch * CHUNK, (ch + 1) * CHUNK)
            pz = p_ref[rows, GMLP_W:].astype(F32)
            z = _gelu(pz)
            zhat, rstd, zn = _ln_fwd(z, gain, b_ref[...])
            znb = zn.astype(BF)
            dzn_parts = []
            for gg in range(4):
                cols = slice(gg * CHUNK, (gg + 1) * CHUNK)
                pu = p_ref[rows, cols].astype(F32)
                u = _gelu(pu)
                sz = _dot(ws[gg], znb[:, cols]) + bt_ref[:, gg:gg + 1]
                dyv = dy_ref[rows, cols].astype(F32)
                dp_ref[rows, cols] = (dyv * sz * _gelu_grad(pu)).astype(BF)
                dsz = dyv * u
                dbacc_ref[gg] += dsz
                dszb = dsz.astype(BF)
                dw_ref[gg] += _dot_nt(dszb, znb[:, cols])
                dzn_parts.append(_dot(wts[gg], dszb))
            dzn = jnp.concatenate(dzn_parts, axis=1)
            dg_ref[...] += jnp.sum(dzn * zhat, axis=0, keepdims=True)
            dbias_ref[...] += jnp.sum(dzn, axis=0, keepdims=True)
            dzh = dzn * gain
            dz = rstd * (dzh - jnp.mean(dzh, axis=-1, keepdims=True)
                         - zhat * jnp.mean(dzh * zhat, axis=-1, keepdims=True))
            dp_ref[rows, GMLP_W:] = (dz * _gelu_grad(pz)).astype(BF)

        @pl.when(i == pl.num_programs(0) - 1)
        def _():
            for gg in range(4):
                dw_ref[gg] = jnp.where(tril, dw_ref[gg], 0.0)
                dbs_ref[gg] = jnp.sum(dbacc_ref[gg], axis=1, keepdims=True)

    full2 = lambda s: pl.BlockSpec(s, lambda i: (0, 0))
    full3 = lambda s: pl.BlockSpec(s, lambda i: (0, 0, 0))
    return pl.pallas_call(
        body, name="gmlp_bwd", grid=(S // tm,),
        out_shape=(jax.ShapeDtypeStruct((S, 2 * GMLP_W), BF), jax.ShapeDtypeStruct((4, CHUNK, CHUNK), F32),
                   jax.ShapeDtypeStruct((4, CHUNK, 1), F32), jax.ShapeDtypeStruct((1, GMLP_W), F32),
                   jax.ShapeDtypeStruct((1, GMLP_W), F32)),
        in_specs=[pl.BlockSpec((tm, 2 * GMLP_W), lambda i: (i, 0)), pl.BlockSpec((tm, GMLP_W), lambda i: (i, 0)),
                  full2((1, GMLP_W)), full2((1, GMLP_W)), full3((4, CHUNK, CHUNK)), full3((4, CHUNK, CHUNK)),
                  full2((CHUNK, 4))],
        out_specs=(pl.BlockSpec((tm, 2 * GMLP_W), lambda i: (i, 0)), full3((4, CHUNK, CHUNK)), full3((4, CHUNK, 1)),
                   full2((1, GMLP_W)), full2((1, GMLP_W))),
        scratch_shapes=[pltpu.VMEM((4, CHUNK, CHUNK), F32)],
        compiler_params=_cp(("arbitrary",)))(puz, dyg, ln_g, ln_b, w_sp, w_sp_t, b_sp_t)


def _attn_bwd(qkvs, dyas, dsums, lses):
    S = qkvs[0].shape[1] * qkvs[0].shape[2]
    nsub = min(ATTN_NSUB, S // max(DILS) // QB)
    R = nsub * QB
    nsteps = S // R
    NIN = 12

    def group_body(first, last, q_ref, qn_ref, kp_ref, kc_ref, vp_ref, vc_ref, dy_ref, dyn_ref, e_ref, en_ref,
                   l_ref, ln_ref, o_ref):
        band, band_first = _band_masks(first)
        row = lax.broadcasted_iota(jnp.int32, (QB, QB), 0)
        col = lax.broadcasted_iota(jnp.int32, (QB, QB), 1)
        mask_next = jnp.logical_and(col >= row, jnp.logical_not(last))
        kc, vc = kc_ref[...], vc_ref[...]
        kfull = jnp.concatenate([kp_ref[...], kc], axis=0)
        vfull = jnp.concatenate([vp_ref[...], vc], axis=0)
        k_last, v_last = kc[(nsub - 1) * QB:], vc[(nsub - 1) * QB:]
        q_ext = jnp.concatenate([q_ref[...], qn_ref[...]], axis=0)
        dy_ext = jnp.concatenate([dy_ref[...], dyn_ref[...]], axis=0)
        esum, esum_n, lse, lse_n = e_ref[...], en_ref[...], l_ref[...], ln_ref[...]
        win = lambda t, sb: t[sb * QB:(sb + 2) * QB]
        blk = lambda t, sb: t[sb * QB:(sb + 1) * QB]
        hms = [_head_mask(hh) for hh in range(NSLOT)]
        q_hs = [jnp.where(hm, q_ext, 0) for hm in hms]
        dy_hs = [jnp.where(hm, dy_ext, 0) for hm in hms]
        raw = []
        for hh in range(NSLOT):
            tiles = [(_dot_nt(blk(q_hs[hh], sb), win(kfull, sb)), _dot_nt(blk(dy_hs[hh], sb), win(vfull, sb)))
                     for sb in range(nsub)]
            tiles.append((_dot_nt(blk(q_hs[hh], nsub), k_last), _dot_nt(blk(dy_hs[hh], nsub), v_last)))
            raw.append(tiles)
        ps, dss = [], []
        for hh in range(NSLOT):
            rowstat = lambda t: jnp.max(jnp.where(hms[hh], t, -jnp.inf), axis=1, keepdims=True)
            p_h, ds_h = [], []
            for sb in range(nsub + 1):
                sc, dp = raw[hh][sb]
                if sb < nsub:
                    msk, lrow, erow = (band_first if sb == 0 else band), rowstat(blk(lse, sb)), rowstat(blk(esum, sb))
                else:
                    msk, lrow, erow = mask_next, rowstat(lse_n), rowstat(esum_n)
                p = jnp.where(msk, jnp.exp(sc * SCALE - lrow), 0.0)
                p_h.append(p.astype(BF))
                ds_h.append((p * (dp - erow)).astype(BF))
            ps.append(p_h)
            dss.append(ds_h)
        dq = [jnp.zeros((QB, GW), F32) for _ in range(nsub)]
        dk = [jnp.zeros((QB, GW), F32) for _ in range(nsub)]
        dv = [jnp.zeros((QB, GW), F32) for _ in range(nsub)]
        for hh in range(NSLOT):
            for sb in range(nsub):
                dq[sb] = dq[sb] + jnp.where(hms[hh], _dot(dss[hh][sb], win(kfull, sb)), 0.0)
                nxt = lambda t: t[sb + 1][:, :QB] if sb + 1 < nsub else t[nsub]
                dk[sb] = dk[sb] + _dot_tn(jnp.concatenate([dss[hh][sb][:, QB:], nxt(dss[hh])], axis=0), win(q_hs[hh], sb))
                dv[sb] = dv[sb] + _dot_tn(jnp.concatenate([ps[hh][sb][:, QB:], nxt(ps[hh])], axis=0), win(dy_hs[hh], sb))
        for sb in range(nsub):
            rows = slice(sb * QB, (sb + 1) * QB)
            o_ref[0, rows, :] = (dq[sb] * SCALE).astype(BF)
            o_ref[1, rows, :] = (dk[sb] * SCALE).astype(BF)
            o_ref[2, rows, :] = dv[sb].astype(BF)

    def body(*refs):
        g, n = pl.program_id(0), pl.program_id(1)
        for k, d in enumerate(DILS):
            @pl.when(g == k)
            def _():
                per = nsteps // d
                group_body(n % per == 0, n % per == per - 1, *refs[NIN * k:NIN * (k + 1)], refs[NIN * len(DILS) + k])

    in_specs, out_specs, operands = [], [], []
    for k in range(len(DILS)):
        at = _group_step(k, nsteps)
        prev = lambda g, n, at=at: jnp.maximum(at(g, n) * nsub - 1, 0)
        nxt = lambda g, n, at=at: jnp.minimum((at(g, n) + 1) * nsub, S // QB - 1)
        cur4 = lambda t, at=at: pl.BlockSpec((None, R, GW), lambda g, n: (t, at(g, n), 0))
        prv4 = lambda t, prev=prev: pl.BlockSpec((None, QB, GW), lambda g, n: (t, prev(g, n), 0))
        nxt4 = lambda t, nxt=nxt: pl.BlockSpec((None, QB, GW), lambda g, n: (t, nxt(g, n), 0))
        cur3 = pl.BlockSpec((R, GW), lambda g, n, at=at: (at(g, n), 0))
        nxt3 = pl.BlockSpec((QB, GW), lambda g, n, nxt=nxt: (nxt(g, n), 0))
        in_specs += [cur4(0), nxt4(0), prv4(1), cur4(1), prv4(2), cur4(2), cur3, nxt3, cur3, nxt3, cur3, nxt3]
        out_specs.append(pl.BlockSpec((3, R, GW), lambda g, n, at=at: (0, at(g, n), 0)))
        flat = lambda t: t.reshape(S, GW)
        operands += [qkvs[k].reshape(3, S, GW)] * 6 + [flat(dyas[k])] * 2 + [flat(dsums[k])] * 2 + [flat(lses[k])] * 2
    outs = pl.pallas_call(
        body, name="attn_bwd", grid=(len(DILS), nsteps),
        out_shape=(jax.ShapeDtypeStruct((3, S, GW), BF),) * len(DILS),
        in_specs=in_specs, out_specs=tuple(out_specs),
        compiler_params=_cp(("arbitrary", "arbitrary")))(*operands)
    return [t.reshape(3, d, S // d, GW) for t, d in zip(outs, DILS)]


def _inproj_bwd(dqkvs, dpuz, dgates, win_t, cos_t, sin_t, x, dx2, g1):
    S = x.shape[0]
    tm = min(512, S)

    def body(d0_ref, d1_ref, d2_ref, dp_ref, dg_ref, w_ref, c_ref, s_ref,
             x_ref, dx2_ref, g1_ref, gx_ref, dg1_ref, dn_ref, scr):
        i = pl.program_id(0)

        @pl.when(i == 0)
        def _():
            dg1_ref[...] = jnp.zeros_like(dg1_ref)

        dh = _dot(dp_ref[...], w_ref[W_UZ0:W_G0, :]) + _dot(dg_ref[...], w_ref[W_G0:, :])
        dn_ref[:, W_UZ0:W_G0] = dp_ref[...]
        dn_ref[:, W_G0:] = dg_ref[...]
        for t in range(3):
            for g, (d_ref, d) in enumerate(zip((d0_ref, d1_ref, d2_ref), DILS)):
                piece = _undilate_load(d_ref, (t,), d, scr, tm)
                if t < 2:
                    piece = _rope_bwd(piece, c_ref, s_ref)
                dn_ref[:, (3 * t + g) * GW:(3 * t + g + 1) * GW] = piece.astype(BF)
        dh = dh + _dot(dn_ref[:, :W_UZ0], w_ref[:W_UZ0, :])
        xv = x_ref[...]
        dx1, dg1 = _rms_bwd(xv, _rms_stats(xv), g1_ref[...], dh)
        gx_ref[...] = dx2_ref[...] + dx1
        dg1_ref[...] += dg1

    row = lambda w: pl.BlockSpec((tm, w), lambda i: (i, 0))
    full = lambda s: pl.BlockSpec(s, lambda i: (0, 0))
    dil = lambda d: pl.BlockSpec((3, d, tm // d, GW), lambda i: (0, 0, i, 0))
    return pl.pallas_call(
        body, name="inproj_bwd", grid=(S // tm,),
        out_shape=(jax.ShapeDtypeStruct((S, D), F32), jax.ShapeDtypeStruct((1, D), F32),
                   jax.ShapeDtypeStruct((S, win_t.shape[0]), BF)),
        in_specs=[dil(d) for d in DILS] + [row(2 * GMLP_W), row(2 * D), _resident(win_t.shape), row(128), row(128), row(D), row(D),
                                            full((1, D))],
        out_specs=(row(D), full((1, D)), row(win_t.shape[0])),
        scratch_shapes=[pltpu.VMEM((2, tm, 128), F32)],
        compiler_params=_cp(("arbitrary",)))(*dqkvs, dpuz, dgates, win_t, cos_t, sin_t, x, dx2, g1)


def _row_tile(rows, cols):
    cap = max(16, ELEMENTWISE_BLOCK_BYTES // (4 * cols))
    return max(t for t in range(16, cap + 1, 16) if rows % t == 0)


def _adam_math(w, g, m, v):
    m2 = ADAM_B1 * m + (1.0 - ADAM_B1) * g
    v2 = ADAM_B2 * v + (1.0 - ADAM_B2) * (g * g)
    m_hat = m2 / (1.0 - ADAM_B1 ** ADAM_STEP)
    v_hat = v2 / (1.0 - ADAM_B2 ** ADAM_STEP)
    delta = -ADAM_LR * (m_hat / (jnp.sqrt(v_hat) + ADAM_EPS) + ADAM_WD * w)
    return delta, m2, v2


def _adam_shard(own, recv, w, m, v, name):
    R, C = w.shape
    tr = _row_tile(R, C)

    def body(own_ref, r_ref, w_ref, m_ref, v_ref, g_ref, d_ref, m2_ref, v2_ref):
        g = own_ref[...] + r_ref[0].astype(F32) + r_ref[1].astype(F32) + r_ref[2].astype(F32)
        g_ref[...] = g
        d_ref[...], m2_ref[...], v2_ref[...] = _adam_math(w_ref[...], g, m_ref[...], v_ref[...])

    spec = pl.BlockSpec((tr, C), lambda i: (i, 0))
    out = jax.ShapeDtypeStruct((R, C), F32)
    return pl.pallas_call(
        body, name=name, grid=(R // tr,), out_shape=(out, out, out, out),
        in_specs=[spec, pl.BlockSpec((3, tr, C), lambda i: (0, i, 0)), spec, spec, spec],
        out_specs=(spec, spec, spec, spec), compiler_params=_cp(("parallel",)))(own, recv, w, m, v)


def _rs_add(gblocks, recv, idx, name):
    _, R, C = gblocks.shape
    tr = _row_tile(R, C)

    def body(t_ref, g_ref, r_ref, own_ref, send_ref):
        j = pl.program_id(1)
        s = g_ref[...] + r_ref[...]

        @pl.when(j == 0)
        def _():
            own_ref[...] = s

        @pl.when(j > 0)
        def _():
            send_ref[...] = s.astype(BF)

    grid_spec = pltpu.PrefetchScalarGridSpec(
        num_scalar_prefetch=1, grid=(R // tr, 4),
        in_specs=[pl.BlockSpec((None, tr, C), lambda i, j, t: (t[j], i, 0)),
                  pl.BlockSpec((None, tr, C), lambda i, j, t: (t[4 + j], i, 0))],
        out_specs=[pl.BlockSpec((tr, C), lambda i, j, t: (i, 0)),
                   pl.BlockSpec((None, tr, C), lambda i, j, t: (jnp.maximum(j - 1, 0), i, 0))])
    return pl.pallas_call(
        body, name=name, grid_spec=grid_spec,
        out_shape=(jax.ShapeDtypeStruct((R, C), F32), jax.ShapeDtypeStruct((3, R, C), BF)),
        compiler_params=_cp(("parallel", "arbitrary")))(idx, gblocks, recv)


def _mesh_pos():
    return lax.axis_index("x"), lax.axis_index("y"), lax.axis_index("c")


_HBM = pl.BlockSpec(memory_space=pltpu.HBM)
_SEM = pl.BlockSpec(memory_space=pltpu.SEMAPHORE)
_EFFECT = pltpu.SideEffectType.DATAFLOW_SIDE_EFFECTING
_RELATIONS = [(dx, dy, dc) for dx in (0, 1) for dy in (0, 1) for dc in (0, 1)][1:]


def _flip(v, d):
    return 1 - v if d else v


def _plan_gather(n):
    def plan(x, y, c):
        return [(k, None, 4 * x + 2 * y + c, (_flip(x, dx), _flip(y, dy), _flip(c, dc)))
                for k in range(n) for dx, dy, dc in _RELATIONS]
    return plan


def _plan_gather_near(x, y, c):
    chips = [(1 - x, y), (x, 1 - y), (1 - x, 1 - y)]
    return [(0, None, 4 * x + 2 * y + c, (x, y, 1 - c))] + [(0, None, 4 * x + 2 * y + c, (*chip, c)) for chip in chips]


def _plan_gather_pass(from_landing):
    def plan(x, y, c):
        blocks = [4 * cx + 2 * cy + c for cx, cy in ((1 - x, y), (x, 1 - y), (1 - x, 1 - y))]
        return [(0, b if from_landing else None, b, (x, y, 1 - c)) for b in blocks]
    return plan


def _plan_d2d(n):
    def plan(x, y, c):
        return [(k, 2 * kk + 1 - c, kk, (x, y, 1 - c)) for k in range(n) for kk in range(4)]
    return plan


def _plan_ici(n):
    def plan(x, y, c):
        return [(k, j, j, (_flip(x, dx), _flip(y, dy), c))
                for k in range(n) for j, (dx, dy) in enumerate(((1, 0), (0, 1), (1, 1)))]
    return plan


def _plan_copies(plan, src_refs, land_refs, send_sems, recv_sems):
    x, y, c = _mesh_pos()
    return [pltpu.make_async_remote_copy(
        src_ref=src_refs[k] if si is None else src_refs[k].at[si], dst_ref=land_refs[k].at[di],
        send_sem=send_sems.at[n], recv_sem=recv_sems.at[n], device_id=dev, device_id_type=MESH)
        for n, (k, si, di, dev) in enumerate(plan(x, y, c))]


def _exchange_start(srcs, land_shapes, plan, ncopies, name, after):
    n = len(srcs)

    def body(*refs):
        src_refs, land_refs = refs[:n], refs[n:2 * n]
        send_sems, recv_sems = refs[2 * n + len(after)], refs[2 * n + len(after) + 1]
        token = refs[-1]
        for cp in _plan_copies(plan, src_refs, land_refs, send_sems, recv_sems):
            cp.start()
        token[...] = jnp.zeros_like(token)

    lands = [pltpu.with_memory_space_constraint(lax.empty(s, a.dtype), pltpu.HBM) for s, a in zip(land_shapes, srcs)]
    srcs = [pltpu.with_memory_space_constraint(a, pltpu.HBM) for a in srcs]
    outs = pl.pallas_call(
        body, name=name,
        out_shape=(pltpu.SemaphoreType.DMA((ncopies,)), pltpu.SemaphoreType.DMA((ncopies,)))
        + tuple(pltpu.HBM(a.shape, a.dtype) for a in srcs) + tuple(pltpu.HBM(a.shape, a.dtype) for a in lands)
        + (jax.ShapeDtypeStruct((8, 128), F32),),
        in_specs=[_HBM] * (2 * n) + [pl.BlockSpec(memory_space=pl.ANY)] * len(after),
        out_specs=(_SEM, _SEM) + (_HBM,) * (2 * n) + (pl.BlockSpec(memory_space=pltpu.VMEM),),
        input_output_aliases={i: 2 + i for i in range(2 * n)},
        compiler_params=pltpu.CompilerParams(has_side_effects=_EFFECT))(*srcs, *lands, *after)
    return (outs[0], outs[1], list(outs[2:2 + n]), list(outs[2 + n:2 + 2 * n])), outs[-1]


def _exchange_forward(handle, plan, plan_fwd, needs, name, after):
    send_sems, recv_sems, srcs, lands = handle
    n, nfwd = len(srcs), len(needs)

    def body(*refs):
        src_refs, land_refs = refs[:n], refs[n:2 * n]
        outs = refs[2 * n + 2 + len(after):]
        first = _plan_copies(plan, src_refs, land_refs, refs[2 * n], refs[2 * n + 1])
        for cp, need in zip(_plan_copies(plan_fwd, land_refs, land_refs, outs[0], outs[1]), needs):
            first[need].wait_recv()
            cp.start()
        for k, cp in enumerate(first):
            cp.wait_send()
            if k not in needs:
                cp.wait_recv()

    outs = pl.pallas_call(
        body, name=name,
        out_shape=(pltpu.SemaphoreType.DMA((nfwd,)), pltpu.SemaphoreType.DMA((nfwd,)))
        + tuple(pltpu.HBM(a.shape, a.dtype) for a in srcs) + tuple(pltpu.HBM(a.shape, a.dtype) for a in lands),
        in_specs=[_HBM] * (2 * n) + [_SEM, _SEM] + [pl.BlockSpec(memory_space=pl.ANY)] * len(after),
        out_specs=(_SEM, _SEM) + (_HBM,) * (2 * n), input_output_aliases={i: 2 + i for i in range(2 * n)},
        compiler_params=pltpu.CompilerParams(has_side_effects=_EFFECT))(*srcs, *lands, send_sems, recv_sems, *after)
    return outs[0], outs[1], list(outs[2:2 + n]), list(outs[2 + n:2 + 2 * n])


def _exchange_wait(handle, plan, name, after):
    send_sems, recv_sems, srcs, lands = handle
    n = len(srcs)

    def body(*refs):
        src_refs, land_refs = refs[:n], refs[n:2 * n]
        for cp in _plan_copies(plan, src_refs, land_refs, refs[2 * n], refs[2 * n + 1]):
            cp.wait_send()
            cp.wait_recv()

    outs = pl.pallas_call(
        body, name=name,
        out_shape=tuple(pltpu.HBM(a.shape, a.dtype) for a in srcs) + tuple(pltpu.HBM(a.shape, a.dtype) for a in lands),
        in_specs=[_HBM] * (2 * n) + [_SEM, _SEM] + [pl.BlockSpec(memory_space=pl.ANY)] * len(after),
        out_specs=(_HBM,) * (2 * n), input_output_aliases={i: i for i in range(2 * n)},
        compiler_params=pltpu.CompilerParams(has_side_effects=_EFFECT))(*srcs, *lands, send_sems, recv_sems, *after)
    return list(outs[:n]), list(outs[n:])


SMALL = ("g1", "g2", "g3", "g4", "ln_g", "ln_b", "b_sp", "w_sp")


def _small_sum_adam(block, parts, lands, ws, ms, vs):
    n, ns = len(parts), len(SMALL)

    def body(blk_ref, *refs):
        p_refs, l_refs = refs[:n], refs[n:2 * n]
        w_refs, m_refs, v_refs = (refs[2 * n + i * ns:2 * n + (i + 1) * ns] for i in range(3))
        o = 2 * n + 3 * ns
        loss_ref = refs[o]
        g_out, d_out, m_out, v_out = (refs[o + 1 + i * ns:o + 1 + (i + 1) * ns] for i in range(4))
        me = blk_ref[0]
        sums = []
        for k in range(n):
            acc = jnp.where(me == 0, p_refs[k][...], l_refs[k][0])
            for b in range(1, 8):
                acc = acc + jnp.where(me == b, p_refs[k][...], l_refs[k][b])
            sums.append(acc)
        loss_ref[...] = sums[0]
        for i in range(ns):
            g_out[i][...] = sums[1 + i]
            d_out[i][...], m_out[i][...], v_out[i][...] = _adam_math(w_refs[i][...], sums[1 + i], m_refs[i][...],
                                                                     v_refs[i][...])

    args = list(parts) + list(lands) + [t[k] for t in (ws, ms, vs) for k in SMALL]
    shapes = [jax.ShapeDtypeStruct(p.shape, F32) for p in parts[:1]] + [jax.ShapeDtypeStruct(p.shape, F32) for p in parts[1:]] * 4
    vmem = pl.BlockSpec(memory_space=pltpu.VMEM)
    outs = pl.pallas_call(
        body, name="small_sum_adam", out_shape=tuple(shapes),
        in_specs=[pl.BlockSpec(memory_space=pltpu.SMEM)] + [vmem] * len(args), out_specs=(vmem,) * len(shapes),
        compiler_params=pltpu.CompilerParams(vmem_limit_bytes=VMEM_LIMIT))(block, *args)
    groups = [dict(zip(SMALL, outs[1 + i * ns:1 + (i + 1) * ns])) for i in range(4)]
    return (outs[0], *groups)


def _rope_tables(S):
    half = HD // 2
    inv_freq = jnp.tile(ROPE_THETA ** (-jnp.arange(half, dtype=F32) / half), 4)
    sign = jnp.tile(jnp.concatenate([-jnp.ones(half, F32), jnp.ones(half, F32)]), 2)
    ang = jnp.arange(S, dtype=F32)[:, None] * inv_freq[None, :]
    return jnp.cos(ang), jnp.sin(ang) * sign[None, :]


def _to_blocks(g, col_sharded):
    if col_sharded:
        return g.reshape(g.shape[0], 8, g.shape[1] // 8).transpose(1, 0, 2)
    return g.reshape(8, g.shape[0] // 8, g.shape[1])


def _from_blocks(t, col_sharded):
    if col_sharded:
        return t.transpose(1, 0, 2).reshape(t.shape[1], 8 * t.shape[2])
    return t.reshape(8 * t.shape[1], t.shape[2])


class _NoComm:
    def __init__(self, late_weights):
        self._late = late_weights
        self.grads = {}

    def start_tie(self):
        return jnp.zeros((8, 128), F32)

    def late_weights(self, after):
        return self._late

    def small_start(self, loss_row, grads):
        self.small = (loss_row, grads)
        return jnp.zeros((8, 128), F32)

    def rs_start(self, key, gblocks, after=()):
        self.grads[key] = gblocks
        return jnp.zeros((8, 128), F32)

    def rs_mid(self, key, after):
        return jnp.zeros((8, 128), F32)


class _FsdpComm:
    def __init__(self, late_shards, col_sharded, after, idx, block):
        self._col, self._idx, self._block, self._rs = col_sharded, idx, block, {}
        n = len(late_shards)
        self._gather, self._token = _exchange_start(
            late_shards, [(8,) + s.shape for s in late_shards], _plan_gather(n), 7 * n, "ag_late_start", (after,))

    def start_tie(self):
        return self._token

    def late_weights(self, after):
        shards, lands = _exchange_wait(self._gather, _plan_gather(len(self._col)), "ag_late_wait", after)
        lands = [lax.dynamic_update_index_in_dim(t, s, self._block, 0) for t, s in zip(lands, shards)]
        return [_from_blocks(t, cs) for t, cs in zip(lands, self._col)]

    def small_start(self, loss_row, grads):
        parts = [loss_row] + [grads[k] for k in SMALL]
        n = len(parts)
        self._small, token = _exchange_start(parts, [(8,) + p.shape for p in parts], _plan_gather(n), 7 * n,
                                             "small_start", (self._token,))
        return token

    def small_finish(self, ws, ms, vs, after):
        n = 1 + len(SMALL)
        parts, lands = _exchange_wait(self._small, _plan_gather(n), "small_wait", after)
        return _small_sum_adam(jnp.reshape(self._block, (1,)).astype(jnp.int32), parts, lands, ws, ms, vs)

    def rs_start(self, key, gblocks, after=()):
        n = len(gblocks)
        d2d, token = _exchange_start(gblocks, [(4,) + g.shape[1:] for g in gblocks], _plan_d2d(n), 4 * n,
                                     "rs_%s_d2d_start" % key, (self._token,) + tuple(after))
        self._rs[key] = dict(n=n, d2d=d2d)
        return token

    def rs_mid(self, key, after):
        st = self._rs[key]
        gblocks, from_sib = _exchange_wait(st["d2d"], _plan_d2d(st["n"]), "rs_%s_d2d_wait" % key, after)
        halves = [_rs_add(g, r, self._idx, "rs_add_%s_%d" % (key, k)) for k, (g, r) in enumerate(zip(gblocks, from_sib))]
        st["own"] = [own for own, _ in halves]
        sends = [send for _, send in halves]
        st["ici"], token = _exchange_start(sends, [t.shape for t in sends], _plan_ici(st["n"]), 3 * st["n"],
                                           "rs_%s_ici_start" % key, (self._token,))
        return token

    def rs_end(self, key, after):
        st = self._rs[key]
        return st["own"], _exchange_wait(st["ici"], _plan_ici(st["n"]), "rs_%s_ici_wait" % key, after)[1]


def _local_step(x, tgt, h1, cos_t, sin_t, win_t, comm, g1, g2, g3, g4, ln_g, ln_b, w_sp, b_sp):
    b_sp_t = b_sp.T
    w_sp_t = w_sp.transpose(0, 2, 1)

    *qkvs, puz, gates = _proj_all(h1, win_t, cos_t, sin_t, comm.start_tie())
    fwd = _attn_fwd(qkvs)
    ya, *lses = _attn_combine([o for o, _ in fwd], [l for _, l in fwd])
    yg = _gmlp_fwd(puz, ln_g, ln_b, w_sp, b_sp_t)
    wba, wbg, wout, wmi, wmo = comm.late_weights(after=(ya, yg, gates))
    merged, y, x2, h3 = _merge_fwd(ya, yg, gates, wba, wbg, wout, x, g2, g3)
    a, dy2, dout, loss_row, dg4 = _mlp_fwd(h3, wmi, wmo, x2, tgt, g4)

    dpre, dx2, dy, dg3, dg2 = _mlp_bwd(dy2, a, wmo, wmi, x2, y, dout, g2, g3)
    dwmo = _mm_tn(a, dy2, "dw_mlp_out", square_a=True, tk=MM_TN_TK // 2)
    dwmi = _mm_tn(h3, dpre, "dw_mlp_in", col_blocks=8)
    tie = comm.rs_start("mlp", [dwmi, _to_blocks(dwmo, False)])
    dgates, da, db, dyg, *rest = _outproj_bwd(dy, wout, ya, yg, gates, wba, wbg, tie)
    dyas, dsums = rest[:3], rest[3:]
    tie = comm.rs_mid("mlp", after=(dyg,))
    dpuz, dwsp, dbs, dlng, dlnb = _gmlp_bwd(puz, dyg, ln_g + tie[0, 0], ln_b, w_sp, w_sp_t, b_sp_t)
    dqkvs = _attn_bwd(qkvs, dyas, dsums, lses)
    grad_x, dg1, dproj = _inproj_bwd(dqkvs, dpuz, dgates, win_t, cos_t, sin_t, x, dx2, g1)
    small = dict(g1=dg1, g2=dg2, g3=dg3, g4=dg4, ln_g=dlng, ln_b=dlnb, b_sp=dbs.reshape(4, CHUNK),
                 w_sp=dwsp.reshape(4 * CHUNK, CHUNK))
    tie = comm.small_start(loss_row, small)
    dwin_t = _mm_tn(dproj, h1, "dw_in", tie=tie)
    tie = comm.rs_start("win", [_to_blocks(dwin_t, False)])
    dwout = _mm_tn(merged, dy, "dw_out", tie=tie)
    tie = comm.rs_mid("win", after=(dwout,))
    dwba = _mm_tn(ya, da, "dw_branch_attn", tie=tie, col_blocks=8)
    dwbg = _mm_tn(yg, db, "dw_branch_gmlp", tie=tie, col_blocks=8)
    tie = comm.rs_start("mid", [dwba, dwbg, _to_blocks(dwout, False)], after=(tie,))
    return grad_x, tie


def kernel(x, norm_pre_mix, w_in, w_spatial, b_spatial, ln_v_gain, ln_v_bias, w_branch_attn, w_branch_gmlp, w_out, norm_post_mix, norm_pre_mlp, w_mlp_in, w_mlp_out, norm_post_mlp, loss_target, m_norm_pre_mix, m_w_in, m_w_spatial, m_b_spatial, m_ln_v_gain, m_ln_v_bias, m_w_branch_attn, m_w_branch_gmlp, m_w_out, m_norm_post_mix, m_norm_pre_mlp, m_w_mlp_in, m_w_mlp_out, m_norm_post_mlp, v_norm_pre_mix, v_w_in, v_w_spatial, v_b_spatial, v_ln_v_gain, v_ln_v_bias, v_w_branch_attn, v_w_branch_gmlp, v_w_out, v_norm_post_mix, v_norm_pre_mlp, v_w_mlp_in, v_w_mlp_out, v_norm_post_mlp):
    mx, my, mc = _mesh_pos()
    rel = [(0, 0), (1, 0), (0, 1), (1, 1)]
    chip_of = [2 * (mx ^ dx) + (my ^ dy) for dx, dy in rel]
    idx = jnp.stack([2 * k + mc for k in chip_of] + chip_of).astype(jnp.int32)

    w_in_t, m_w_in_t, v_w_in_t = (t[0].T for t in (w_in, m_w_in, v_w_in))
    shard = w_in_t.astype(BF)
    gather, token = _exchange_start([shard], [(8,) + shard.shape], _plan_gather_near, 4, "ag_win_start", ())
    h1 = _rms_fwd(x[0], norm_pre_mix + token[0, 0], "rms_pre_mix")
    cos_t, sin_t = _rope_tables(x.shape[1])
    gather = _exchange_forward(gather, _plan_gather_near, _plan_gather_pass(True), (1, 2, 3), "ag_win_pass",
                               (h1, cos_t, sin_t))
    (shard,), (g_win,) = _exchange_wait(gather, _plan_gather_pass(False), "ag_win_wait", ())
    g_win = lax.dynamic_update_index_in_dim(g_win, shard, 4 * mx + 2 * my + mc, 0)
    win_t = _from_blocks(g_win, False)
    late = [w_branch_attn[0], w_branch_gmlp[0], w_out[0], w_mlp_in[0], w_mlp_out[0]]
    comm = _FsdpComm([w.astype(BF) for w in late], [True, True, False, True, False], g_win, idx, 4 * mx + 2 * my + mc)

    grad_x, tie = _local_step(
        x[0], loss_target[0], h1, cos_t, sin_t, win_t, comm,
        norm_pre_mix, norm_post_mix, norm_pre_mlp, norm_post_mlp, ln_v_gain, ln_v_bias, w_spatial[0], b_spatial[0])

    flat = lambda t: t.reshape(-1, t.shape[-1])
    small_w = dict(zip(SMALL, map(flat, (norm_pre_mix, norm_post_mix, norm_pre_mlp, norm_post_mlp, ln_v_gain, ln_v_bias,
                                         b_spatial, w_spatial))))
    small_m = dict(zip(SMALL, map(flat, (m_norm_pre_mix, m_norm_post_mix, m_norm_pre_mlp, m_norm_post_mlp, m_ln_v_gain,
                                         m_ln_v_bias, m_b_spatial, m_w_spatial))))
    small_v = dict(zip(SMALL, map(flat, (v_norm_pre_mix, v_norm_post_mix, v_norm_pre_mlp, v_norm_post_mlp, v_ln_v_gain,
                                         v_ln_v_bias, v_b_spatial, v_w_spatial))))
    mlp_own, mlp_chips = comm.rs_end("mlp", after=(tie,))
    adam = lambda nm, own, r, w, m, v: _adam_shard(own, r, w[0], m[0], v[0], "adam_" + nm)
    upd = {
        "w_mlp_in": adam("w_mlp_in", mlp_own[0], mlp_chips[0], w_mlp_in, m_w_mlp_in, v_w_mlp_in),
        "w_mlp_out": adam("w_mlp_out", mlp_own[1], mlp_chips[1], w_mlp_out, m_w_mlp_out, v_w_mlp_out),
    }
    loss_out, sg, sd, sm, sv = comm.small_finish(small_w, small_m, small_v, after=(upd["w_mlp_in"][0], upd["w_mlp_out"][0]))
    loss = loss_out[0, 0]
    tie = comm.rs_mid("mid", after=(loss_out,))
    win_own, win_chips = comm.rs_end("win", after=(tie,))
    upd["w_in"] = tuple(t.T for t in _adam_shard(win_own[0], win_chips[0], w_in_t, m_w_in_t, v_w_in_t, "adam_w_in"))
    mid_own, mid_chips = comm.rs_end("mid", after=(upd["w_in"][0],))
    upd["w_branch_attn"] = adam("w_branch_attn", mid_own[0], mid_chips[0], w_branch_attn, m_w_branch_attn, v_w_branch_attn)
    upd["w_branch_gmlp"] = adam("w_branch_gmlp", mid_own[1], mid_chips[1], w_branch_gmlp, m_w_branch_gmlp, v_w_branch_gmlp)
    upd["w_out"] = adam("w_out", mid_own[2], mid_chips[2], w_out, m_w_out, v_w_out)
    order = ["g1", "w_in", "w_sp", "b_sp", "ln_g", "ln_b", "w_branch_attn", "w_branch_gmlp", "w_out", "g2", "g3",
             "w_mlp_in", "w_mlp_out", "g4"]
    small_shape = dict(g1=norm_pre_mix.shape, g2=norm_post_mix.shape, g3=norm_pre_mlp.shape, g4=norm_post_mlp.shape,
                       ln_g=ln_v_gain.shape, ln_b=ln_v_bias.shape, b_sp=b_spatial.shape, w_sp=w_spatial.shape)

    def pick(which):
        return [upd[nm][which][None] if nm in upd else (sg, sd, sm, sv)[which][nm].reshape(small_shape[nm])
                for nm in order]

    return (loss, grad_x[None], *pick(0), *pick(1), *pick(2), *pick(3))
```

```python
import functools
import math

import jax
import jax.numpy as jnp
from jax import lax
from jax.experimental import pallas as pl
from jax.experimental.pallas import tpu as pltpu

D = 1024
HD = 64
NSLOT = 4
GW = NSLOT * HD
DILS = (1, 4, 16)
QB = 128
ATTN_NSUB = 4
PROJ_TM = 1024
MLP_TM_FWD = 512
MLP_TM_BWD = 256
MM_TN_TK = 4096
ELEMENTWISE_BLOCK_BYTES = 1 << 20
AW = 3 * GW
W_UZ0, W_G0 = 3 * AW, 3 * AW + 1024
GMLP_W = 512
CHUNK = 128
DFF = 4096
EPS = 1e-6
ROPE_THETA = 10000.0
SCALE = HD ** -0.5
NEG = -1e30

ADAM_LR = 0.001
ADAM_B1 = 0.9
ADAM_B2 = 0.999
ADAM_EPS = 1e-08
ADAM_WD = 0.01
ADAM_STEP = 10

BF = jnp.bfloat16
F32 = jnp.float32
MESH = pl.DeviceIdType.MESH
VMEM_LIMIT = 56 * 1024 * 1024


def _cp(sem):
    return pltpu.CompilerParams(dimension_semantics=sem, vmem_limit_bytes=VMEM_LIMIT)


def _dot(a, b):
    return jnp.dot(a, b, preferred_element_type=F32)


def _dot_nt(a, b):
    return lax.dot_general(a, b, (((1,), (1,)), ((), ())), preferred_element_type=F32)


def _dot_tn(a, b):
    return lax.dot_general(a, b, (((0,), (0,)), ((), ())), preferred_element_type=F32)


def _gelu(x):
    return jax.nn.gelu(x, approximate=True)


def _gelu_grad(x):
    k = math.sqrt(2.0 / math.pi)
    t = jnp.tanh(k * (x + 0.044715 * x * x * x))
    return 0.5 * (1.0 + t) + 0.5 * x * (1.0 - t * t) * (k * (1.0 + 3.0 * 0.044715 * x * x))


def _swap_halves(t):
    w = t.shape[1]
    lane = lax.broadcasted_iota(jnp.int32, t.shape, 1)
    first = (lane & (HD - 1)) < (HD // 2)
    return jnp.where(first, pltpu.roll(t, w - HD // 2, 1), pltpu.roll(t, HD // 2, 1))


def _head_mask(hh):
    lane = lax.broadcasted_iota(jnp.int32, (1, GW), 1)
    return jnp.logical_and(lane >= hh * HD, lane < (hh + 1) * HD)


def _rms_stats(xf):
    return lax.rsqrt(jnp.mean(xf * xf, axis=-1, keepdims=True) + EPS)


def _rms_bwd(xf, r, gain, dout):
    n = xf * r
    t = dout * gain
    dx = r * (t - n * jnp.mean(t * n, axis=-1, keepdims=True))
    return dx, jnp.sum(dout * n, axis=0, keepdims=True)


def _rms_fwd(x, gain, name):
    S = x.shape[0]
    tm = min(512, S)

    def body(x_ref, g_ref, h_ref):
        xf = x_ref[...]
        h_ref[...] = (xf * _rms_stats(xf) * g_ref[...]).astype(BF)

    return pl.pallas_call(
        body, name=name, out_shape=jax.ShapeDtypeStruct((S, D), BF), grid=(S // tm,),
        in_specs=[pl.BlockSpec((tm, D), lambda i: (i, 0)), pl.BlockSpec((1, D), lambda i: (0, 0))],
        out_specs=pl.BlockSpec((tm, D), lambda i: (i, 0)), compiler_params=_cp(("parallel",)))(x, gain)


def _dilate_store(val, scr, o_ref, lead, d):
    rows = val.shape[0]
    if d == 1:
        o_ref[lead + (0,)] = val.astype(o_ref.dtype)
        return
    for hf in range(2):
        scr[hf, pl.ds(0, rows), :] = val[:, hf * 128:(hf + 1) * 128]
    for r in range(d):
        for hf in range(2):
            o_ref[lead + (r, slice(None), slice(hf * 128, (hf + 1) * 128))] = (
                scr[hf, pl.ds(r, rows // d, stride=d), :].astype(o_ref.dtype))


def _undilate_load(i_ref, lead, d, scr, rows):
    if d == 1:
        return i_ref[lead + (0,)].astype(F32)
    for r in range(d):
        for hf in range(2):
            scr[hf, pl.ds(r, rows // d, stride=d), :] = (
                i_ref[lead + (r, slice(None), slice(hf * 128, (hf + 1) * 128))].astype(F32))
    return jnp.concatenate([scr[0, pl.ds(0, rows), :], scr[1, pl.ds(0, rows), :]], axis=1)


def _rope_fwd(y, c_ref, s_ref):
    cosv = jnp.concatenate([c_ref[...]] * 2, axis=1)
    sinv = jnp.concatenate([s_ref[...]] * 2, axis=1)
    return y * cosv + _swap_halves(y) * sinv


def _rope_bwd(dy, c_ref, s_ref):
    cosv = jnp.concatenate([c_ref[...]] * 2, axis=1)
    sinv = jnp.concatenate([s_ref[...]] * 2, axis=1)
    return dy * cosv + _swap_halves(dy * sinv)


def _proj_all(h, win_t, cos_t, sin_t, tie):
    S, K = h.shape
    tm = min(PROJ_TM, S)

    def body(h_ref, w_ref, c_ref, s_ref, tie_ref, o0_ref, o1_ref, o2_ref, p_ref, gt_ref, scr):
        hv = h_ref[...]
        col = lambda b: _dot_nt(hv, w_ref[b * GW:(b + 1) * GW, :])
        ys = [col(3 * t) for t in range(3)]
        for g, (o_ref, d) in enumerate(zip((o0_ref, o1_ref, o2_ref), DILS)):
            nxt = [col(3 * t + g + 1) if g < 2 else col(W_UZ0 // GW + t) for t in range(3)]
            for t in range(3):
                y = _rope_fwd(ys[t], c_ref, s_ref) if t < 2 else ys[t]
                _dilate_store(y, scr.at[t], o_ref, (t,), d)
            ys = nxt
        for b in range((W_G0 - W_UZ0) // GW):
            y = ys[b] if b < 3 else col(W_UZ0 // GW + b)
            p_ref[:, b * GW:(b + 1) * GW] = y.astype(BF)
        for b in range(2 * D // GW):
            gt_ref[:, b * GW:(b + 1) * GW] = jax.nn.sigmoid(col(W_G0 // GW + b)).astype(BF)

    row = lambda w: pl.BlockSpec((tm, w), lambda i: (i, 0))
    return pl.pallas_call(
        body, name="proj_all", grid=(S // tm,),
        out_shape=tuple(jax.ShapeDtypeStruct((3, d, S // d, GW), BF) for d in DILS)
        + (jax.ShapeDtypeStruct((S, W_G0 - W_UZ0), BF), jax.ShapeDtypeStruct((S, 2 * D), BF)),
        in_specs=[row(K), _resident(win_t.shape), row(128), row(128), pl.BlockSpec(memory_space=pl.ANY)],
        out_specs=tuple(pl.BlockSpec((3, d, tm // d, GW), lambda i: (0, 0, i, 0)) for d in DILS)
        + (row(W_G0 - W_UZ0), row(2 * D)),
        scratch_shapes=[pltpu.VMEM((3, 2, tm, 128), F32)],
        compiler_params=_cp(("parallel",)))(h, win_t, cos_t, sin_t, tie)


def _band_masks(first_step):
    row = lax.broadcasted_iota(jnp.int32, (QB, 2 * QB), 0)
    col = lax.broadcasted_iota(jnp.int32, (QB, 2 * QB), 1)
    band = jnp.logical_and(col >= row, col <= row + QB)
    return band, jnp.logical_and(band, jnp.logical_or(col >= QB, jnp.logical_not(first_step)))


def _group_step(k, nsteps):
    return lambda g, n: jnp.where(g == k, n, jnp.where(g < k, 0, nsteps - 1))


def _attn_fwd(qkvs):
    S = qkvs[0].shape[1] * qkvs[0].shape[2]
    nsub = min(ATTN_NSUB, S // max(DILS) // QB)
    R = nsub * QB
    nsteps = S // R

    def group_body(first, q_ref, kp_ref, kc_ref, vp_ref, vc_ref, o_ref, lse_ref):
        band, band_first = _band_masks(first)
        kfull = jnp.concatenate([kp_ref[...], kc_ref[...]], axis=0)
        vfull = jnp.concatenate([vp_ref[...], vc_ref[...]], axis=0)
        chains = [(sb, hh) for sb in range(nsub) for hh in range(NSLOT)]
        win = lambda t, sb: t[sb * QB:(sb + 2) * QB]
        scores = []
        for sb, hh in chains:
            qh = jnp.where(_head_mask(hh), q_ref[sb * QB:(sb + 1) * QB, :], 0)
            scores.append(_dot_nt(qh, win(kfull, sb)))
        soft = []
        for (sb, hh), sc in zip(chains, scores):
            sc = jnp.where(band_first if sb == 0 else band, sc * SCALE, NEG)
            m = jnp.max(sc, axis=1, keepdims=True)
            p = jnp.exp(sc - m)
            den = jnp.sum(p, axis=1, keepdims=True)
            soft.append((p.astype(BF), den, m + jnp.log(den)))
        accs = [_dot(p, win(vfull, sb)) for (sb, hh), (p, _, _) in zip(chains, soft)]
        for sb in range(nsub):
            o = jnp.zeros((QB, GW), F32)
            lse = jnp.zeros((QB, GW), F32)
            for hh in range(NSLOT):
                hm = _head_mask(hh)
                _, den, lrow = soft[sb * NSLOT + hh]
                o = o + jnp.where(hm, accs[sb * NSLOT + hh] / den, 0.0)
                lse = lse + jnp.where(hm, lrow, 0.0)
            o_ref[sb * QB:(sb + 1) * QB, :] = o
            lse_ref[sb * QB:(sb + 1) * QB, :] = lse

    def body(*refs):
        g, n = pl.program_id(0), pl.program_id(1)
        for k, d in enumerate(DILS):
            @pl.when(g == k)
            def _():
                group_body(n % (nsteps // d) == 0, *refs[5 * k:5 * k + 5], *refs[15 + 2 * k:17 + 2 * k])

    in_specs, out_specs, operands = [], [], []
    for k in range(len(DILS)):
        at = _group_step(k, nsteps)
        cur = lambda t, at=at: pl.BlockSpec((None, R, GW), lambda g, n: (t, at(g, n), 0))
        prv = lambda t, at=at: pl.BlockSpec((None, QB, GW), lambda g, n: (t, jnp.maximum(at(g, n) * nsub - 1, 0), 0))
        in_specs += [cur(0), prv(1), cur(1), prv(2), cur(2)]
        out_specs += [pl.BlockSpec((R, GW), lambda g, n, at=at: (at(g, n), 0))] * 2
        operands += [qkvs[k].reshape(3, S, GW)] * 5
    outs = pl.pallas_call(
        body, name="attn_fwd", grid=(len(DILS), nsteps),
        out_shape=(jax.ShapeDtypeStruct((S, GW), F32),) * (2 * len(DILS)),
        in_specs=in_specs, out_specs=tuple(out_specs),
        compiler_params=_cp(("arbitrary", "arbitrary")))(*operands)
    return [(outs[2 * k].reshape(d, S // d, GW), outs[2 * k + 1].reshape(d, S // d, GW)) for k, d in enumerate(DILS)]


def _attn_combine(os_, lses):
    S = os_[0].shape[1]
    tm = min(1024, S)

    def body(o0, o1, o2, l0, l1, l2, y_ref, j0, j1, j2, scr):
        os_nat = [_undilate_load(o, (), d, scr, tm) for o, d in zip((o0, o1, o2), DILS)]
        a, b, c = [_undilate_load(l, (), d, scr, tm) for l, d in zip((l0, l1, l2), DILS)]
        m = jnp.maximum(jnp.maximum(a, b), c)
        wa, wb, wc = jnp.exp(a - m), jnp.exp(b - m), jnp.exp(c - m)
        den = wa + wb + wc
        y_ref[...] = ((wa * os_nat[0] + wb * os_nat[1] + wc * os_nat[2]) / den).astype(BF)
        lse = m + jnp.log(den)
        for j_ref, d in zip((j0, j1, j2), DILS):
            _dilate_store(lse, scr, j_ref, (), d)

    dil = lambda d: pl.BlockSpec((d, tm // d, GW), lambda i: (0, i, 0))
    dshape = lambda d: jax.ShapeDtypeStruct((d, S // d, GW), F32)
    return pl.pallas_call(
        body, name="attn_combine", grid=(S // tm,),
        out_shape=(jax.ShapeDtypeStruct((S, GW), BF),) + tuple(dshape(d) for d in DILS),
        in_specs=[dil(d) for d in DILS] * 2,
        out_specs=(pl.BlockSpec((tm, GW), lambda i: (i, 0)),) + tuple(dil(d) for d in DILS),
        scratch_shapes=[pltpu.VMEM((2, tm, 128), F32)],
        compiler_params=_cp(("parallel",)))(*os_, *lses)


def _tril(upper=False):
    row = lax.broadcasted_iota(jnp.int32, (CHUNK, CHUNK), 0)
    col = lax.broadcasted_iota(jnp.int32, (CHUNK, CHUNK), 1)
    return row <= col if upper else col <= row


def _ln_fwd(z, gain, bias):
    mu = jnp.mean(z, axis=-1, keepdims=True)
    zc = z - mu
    rstd = lax.rsqrt(jnp.mean(zc * zc, axis=-1, keepdims=True) + EPS)
    zhat = zc * rstd
    return zhat, rstd, zhat * gain + bias


def _gmlp_fwd(puz, ln_g, ln_b, w_sp, b_sp_t):
    S = puz.shape[0]
    tm = min(512, S)
    nch = tm // CHUNK

    def body(p_ref, g_ref, b_ref, w_ref, bt_ref, o_ref):
        tril = _tril()
        ws = [jnp.where(tril, w_ref[gg], 0.0).astype(BF) for gg in range(4)]
        for ch in range(nch):
            rows = slice(ch * CHUNK, (ch + 1) * CHUNK)
            z = _gelu(p_ref[rows, GMLP_W:].astype(F32))
            _, _, zn = _ln_fwd(z, g_ref[...], b_ref[...])
            zn = zn.astype(BF)
            for gg in range(4):
                cols = slice(gg * CHUNK, (gg + 1) * CHUNK)
                sz = _dot(ws[gg], zn[:, cols]) + bt_ref[:, gg:gg + 1]
                u = _gelu(p_ref[rows, cols].astype(F32))
                o_ref[rows, cols] = (u * sz).astype(BF)

    return pl.pallas_call(
        body, name="gmlp_fwd", out_shape=jax.ShapeDtypeStruct((S, GMLP_W), BF), grid=(S // tm,),
        in_specs=[pl.BlockSpec((tm, 2 * GMLP_W), lambda i: (i, 0)),
                  pl.BlockSpec((1, GMLP_W), lambda i: (0, 0)), pl.BlockSpec((1, GMLP_W), lambda i: (0, 0)),
                  pl.BlockSpec((4, CHUNK, CHUNK), lambda i: (0, 0, 0)), pl.BlockSpec((CHUNK, 4), lambda i: (0, 0))],
        out_specs=pl.BlockSpec((tm, GMLP_W), lambda i: (i, 0)),
        compiler_params=_cp(("parallel",)))(puz, ln_g, ln_b, w_sp, b_sp_t)


def _merge_fwd(ya, yg, gates, wba, wbg, wout, x, g2, g3):
    S = x.shape[0]
    tm = min(512, S)

    def body(ya_ref, yg_ref, gt_ref, wba_ref, wbg_ref, wo_ref, x_ref, g2_ref, g3_ref,
             mg_ref, y_ref, x2_ref, h3_ref):
        a = _dot(ya_ref[...], wba_ref[...])
        b = _dot(yg_ref[...], wbg_ref[...])
        merged = (gt_ref[:, :D].astype(F32) * a + gt_ref[:, D:].astype(F32) * b).astype(BF)
        mg_ref[...] = merged
        y = _dot(merged, wo_ref[...])
        y_ref[...] = y
        x2 = x_ref[...] + y * _rms_stats(y) * g2_ref[...]
        x2_ref[...] = x2
        h3_ref[...] = (x2 * _rms_stats(x2) * g3_ref[...]).astype(BF)

    row = lambda w: pl.BlockSpec((tm, w), lambda i: (i, 0))
    full = lambda s: pl.BlockSpec(s, lambda i: (0, 0))
    return pl.pallas_call(
        body, name="merge_fwd", grid=(S // tm,),
        out_shape=(jax.ShapeDtypeStruct((S, D), BF), jax.ShapeDtypeStruct((S, D), F32), jax.ShapeDtypeStruct((S, D), F32),
                   jax.ShapeDtypeStruct((S, D), BF)),
        in_specs=[row(GW), row(GMLP_W), row(2 * D), full((GW, D)), full((GMLP_W, D)), full((D, D)), row(D),
                  full((1, D)), full((1, D))],
        out_specs=(row(D), row(D), row(D), row(D)),
        compiler_params=_cp(("parallel",)))(ya, yg, gates, wba, wbg, wout, x, g2, g3)


def _resident(shape):
    return pl.BlockSpec(shape, lambda i: (0,) * len(shape), pipeline_mode=pl.Buffered(1))


def _mlp_fwd(h3, wmi, wmo, x2, tgt, g4):
    S = x2.shape[0]
    tm = min(MLP_TM_FWD, S)

    def body(h_ref, wi_ref, wo_ref, x2_ref, t_ref, g4_ref, a_ref, dy2_ref, dout_ref, loss_ref, dg4_ref):
        @pl.when(pl.program_id(0) == 0)
        def _():
            loss_ref[...] = jnp.zeros_like(loss_ref)
            dg4_ref[...] = jnp.zeros_like(dg4_ref)

        halves = [slice(hh * (tm // 2), (hh + 1) * (tm // 2)) for hh in range(2)]
        acts = []
        for rows in halves:
            a = jnp.maximum(_dot(h_ref[rows, :], wi_ref[...]), 0.0)
            a_ref[rows, :] = a.astype(BF)
            acts.append((a * a).astype(BF))
        y2s = [_dot(a2, wo_ref[...]) for a2 in acts]
        lane = lax.broadcasted_iota(jnp.int32, (1, 128), 1)
        for rows, y2 in zip(halves, y2s):
            r = _rms_stats(y2)
            out = x2_ref[rows, :] + y2 * r * g4_ref[...]
            err = out - t_ref[rows, :]
            tot = jnp.sum(jnp.sum(err * err, axis=1, keepdims=True), axis=0, keepdims=True) * (0.5 / D)
            loss_ref[...] += jnp.where(lane == 0, tot, 0.0)
            dout = err * (1.0 / D)
            dout_ref[rows, :] = dout
            dy2, dg = _rms_bwd(y2, r, g4_ref[...], dout)
            dy2_ref[rows, :] = dy2.astype(BF)
            dg4_ref[...] += dg

    row = pl.BlockSpec((tm, D), lambda i: (i, 0))
    return pl.pallas_call(
        body, name="mlp_fwd", grid=(S // tm,),
        out_shape=(jax.ShapeDtypeStruct((S, DFF), BF), jax.ShapeDtypeStruct((S, D), BF), jax.ShapeDtypeStruct((S, D), F32),
                   jax.ShapeDtypeStruct((1, 128), F32), jax.ShapeDtypeStruct((1, D), F32)),
        in_specs=[row, _resident((D, DFF)), _resident((DFF, D)), row, row, pl.BlockSpec((1, D), lambda i: (0, 0))],
        out_specs=(pl.BlockSpec((tm, DFF), lambda i: (i, 0)), row, row,
                   pl.BlockSpec((1, 128), lambda i: (0, 0)), pl.BlockSpec((1, D), lambda i: (0, 0))),
        compiler_params=_cp(("arbitrary",)))(h3, wmi, wmo, x2, tgt, g4)


def _mlp_bwd(dy2, a, wmo, wmi, x2, y, dout, g2, g3):
    S = x2.shape[0]
    tm = min(MLP_TM_BWD, S)

    def body(dy2_ref, a_ref, wo_ref, wi_ref, x2_ref, y_ref, dout_ref, g2_ref, g3_ref,
             dpre_ref, dx2_ref, dy_ref, dg3_ref, dg2_ref):
        @pl.when(pl.program_id(0) == 0)
        def _():
            dg3_ref[...] = jnp.zeros_like(dg3_ref)
            dg2_ref[...] = jnp.zeros_like(dg2_ref)

        da2 = _dot_nt(dy2_ref[...], wo_ref[...])
        dpre = (2.0 * a_ref[...].astype(F32) * da2).astype(BF)
        dpre_ref[...] = dpre
        dh3 = _dot_nt(dpre, wi_ref[...])
        x2 = x2_ref[...]
        dx3, dg3 = _rms_bwd(x2, _rms_stats(x2), g3_ref[...], dh3)
        dx2 = dout_ref[...] + dx3
        dx2_ref[...] = dx2
        dg3_ref[...] += dg3
        yv = y_ref[...]
        dy, dg2 = _rms_bwd(yv, _rms_stats(yv), g2_ref[...], dx2)
        dy_ref[...] = dy.astype(BF)
        dg2_ref[...] += dg2

    row = pl.BlockSpec((tm, D), lambda i: (i, 0))
    wide = pl.BlockSpec((tm, DFF), lambda i: (i, 0))
    vec = pl.BlockSpec((1, D), lambda i: (0, 0))
    return pl.pallas_call(
        body, name="mlp_bwd", grid=(S // tm,),
        out_shape=(jax.ShapeDtypeStruct((S, DFF), BF), jax.ShapeDtypeStruct((S, D), F32), jax.ShapeDtypeStruct((S, D), BF),
                   jax.ShapeDtypeStruct((1, D), F32), jax.ShapeDtypeStruct((1, D), F32)),
        in_specs=[row, wide, _resident((DFF, D)), _resident((D, DFF)), row, row, row, vec, vec],
        out_specs=(wide, row, row, vec, vec),
        compiler_params=_cp(("arbitrary",)))(dy2, a, wmo, wmi, x2, y, dout, g2, g3)


def _mm_tn(a, b, name, square_a=False, tm=1024, tn=1024, tk=MM_TN_TK, tie=None, col_blocks=None):
    S, M = a.shape
    N = b.shape[1]
    tk = min(tk, S)
    tm = max(t for t in range(128, min(tm, M) + 1, 128) if M % t == 0)
    tn = max(t for t in range(128, min(tn, N) + 1, 128) if N % t == 0)
    cb = N // col_blocks if col_blocks else tn
    while (M // tm) * (N // tn) * (S // tk) < 4 and tk % 256 == 0:
        tk //= 2
    assert M % tm == 0 and S % tk == 0 and tn % cb == 0
    nk = S // tk
    ties = () if tie is None else (tie,)

    def body(a_ref, b_ref, *rest):
        o_ref = rest[-1]
        k = pl.program_id(2)
        av = a_ref[...]
        if square_a:
            av = av * av
        part = _dot_tn(av, b_ref[...])
        if col_blocks:
            part = jnp.stack([part[:, t * cb:(t + 1) * cb] for t in range(tn // cb)])

        @pl.when(k == 0)
        def _():
            o_ref[...] = part

        @pl.when(k > 0)
        def _():
            o_ref[...] += part

    if col_blocks:
        out_shape, out_spec = (col_blocks, M, cb), pl.BlockSpec((tn // cb, tm, cb), lambda i, j, k: (j, i, 0))
    else:
        out_shape, out_spec = (M, N), pl.BlockSpec((tm, tn), lambda i, j, k: (i, j))
    return pl.pallas_call(
        body, name=name, out_shape=jax.ShapeDtypeStruct(out_shape, F32), grid=(M // tm, N // tn, nk),
        in_specs=[pl.BlockSpec((tk, tm), lambda i, j, k: (k, i)), pl.BlockSpec((tk, tn), lambda i, j, k: (k, j))]
        + [pl.BlockSpec(memory_space=pl.ANY)] * len(ties),
        out_specs=out_spec,
        compiler_params=_cp(("parallel", "parallel", "arbitrary")))(a, b, *ties)


def _outproj_bwd(dy, wout, ya, yg, gates, wba, wbg, tie):
    S = dy.shape[0]
    tm = min(512, S)

    def body(dy_ref, wo_ref, ya_ref, yg_ref, gt_ref, wba_ref, wbg_ref, tie_ref,
             dgt_ref, da_ref, db_ref, dyg_ref, e0, e1, e2, s0, s1, s2, scr):
        dm = _dot_nt(dy_ref[...], wo_ref[...])
        ga, gb = gt_ref[:, :D].astype(F32), gt_ref[:, D:].astype(F32)
        dgt_ref[:, :D] = (dm * _dot(ya_ref[...], wba_ref[...]) * ga * (1.0 - ga)).astype(BF)
        dgt_ref[:, D:] = (dm * _dot(yg_ref[...], wbg_ref[...]) * gb * (1.0 - gb)).astype(BF)
        da = (dm * ga).astype(BF)
        db = (dm * gb).astype(BF)
        da_ref[...] = da
        db_ref[...] = db
        dyg_ref[...] = _dot_nt(db, wbg_ref[...]).astype(BF)
        dya = _dot_nt(da, wba_ref[...]).astype(BF).astype(F32)
        dyy = dya * ya_ref[...].astype(F32)
        dsum = jnp.zeros((tm, GW), F32)
        for hh in range(NSLOT):
            hm = _head_mask(hh)
            dsum = dsum + jnp.where(hm, jnp.sum(jnp.where(hm, dyy, 0.0), axis=1, keepdims=True), 0.0)
        for e_ref, s_ref, d in zip((e0, e1, e2), (s0, s1, s2), DILS):
            _dilate_store(dya, scr, e_ref, (), d)
            _dilate_store(dsum, scr, s_ref, (), d)

    row = lambda w: pl.BlockSpec((tm, w), lambda i: (i, 0))
    full = lambda s: pl.BlockSpec(s, lambda i: (0, 0))
    dil = lambda d: pl.BlockSpec((d, tm // d, GW), lambda i: (0, i, 0))
    dshape = lambda d, t: jax.ShapeDtypeStruct((d, S // d, GW), t)
    return pl.pallas_call(
        body, name="outproj_bwd", grid=(S // tm,),
        out_shape=(jax.ShapeDtypeStruct((S, 2 * D), BF), jax.ShapeDtypeStruct((S, D), BF), jax.ShapeDtypeStruct((S, D), BF),
                   jax.ShapeDtypeStruct((S, GMLP_W), BF)) + tuple(dshape(d, BF) for d in DILS)
        + tuple(dshape(d, F32) for d in DILS),
        in_specs=[row(D), full((D, D)), row(GW), row(GMLP_W), row(2 * D), full((GW, D)), full((GMLP_W, D)),
                  pl.BlockSpec(memory_space=pl.ANY)],
        out_specs=(row(2 * D), row(D), row(D), row(GMLP_W)) + tuple(dil(d) for d in DILS) * 2,
        scratch_shapes=[pltpu.VMEM((2, tm, 128), F32)],
        compiler_params=_cp(("parallel",)))(dy, wout, ya, yg, gates, wba, wbg, tie)


def _gmlp_bwd(puz, dyg, ln_g, ln_b, w_sp, w_sp_t, b_sp_t):
    S = puz.shape[0]
    tm = min(512, S)
    nch = tm // CHUNK

    def body(p_ref, dy_ref, g_ref, b_ref, w_ref, wt_ref, bt_ref,
             dp_ref, dw_ref, dbs_ref, dg_ref, dbias_ref, dbacc_ref):
        i = pl.program_id(0)

        @pl.when(i == 0)
        def _():
            dw_ref[...] = jnp.zeros_like(dw_ref)
            dbacc_ref[...] = jnp.zeros_like(dbacc_ref)
            dg_ref[...] = jnp.zeros_like(dg_ref)
            dbias_ref[...] = jnp.zeros_like(dbias_ref)

        tril = _tril()
        ws = [jnp.where(tril, w_ref[gg], 0.0).astype(BF) for gg in range(4)]
        triu = _tril(upper=True)
        wts = [jnp.where(triu, wt_ref[gg], 0.0).astype(BF) for gg in range(4)]
        gain = g_ref[...]
        for ch in range(nch):
            rows = slice(ch * CHUNK, (ch + 1) * CHUNK)
            pz = p_ref[rows, GMLP_W:].astype(F32)
            z = _gelu(pz)
            zhat, rstd, zn = _ln_fwd(z, gain, b_ref[...])
            znb = zn.astype(BF)
            dzn_parts = []
            for gg in range(4):
                cols = slice(gg * CHUNK, (gg + 1) * CHUNK)
                pu = p_ref[rows, cols].astype(F32)
                u = _gelu(pu)
                sz = _dot(ws[gg], znb[:, cols]) + bt_ref[:, gg:gg + 1]
                dyv = dy_ref[rows, cols].astype(F32)
                dp_ref[rows, cols] = (dyv * sz * _gelu_grad(pu)).astype(BF)
                dsz = dyv * u
                dbacc_ref[gg] += dsz
                dszb = dsz.astype(BF)
                dw_ref[gg] += _dot_nt(dszb, znb[:, cols])
                dzn_parts.append(_dot(wts[gg], dszb))
            dzn = jnp.concatenate(dzn_parts, axis=1)
            dg_ref[...] += jnp.sum(dzn * zhat, axis=0, keepdims=True)
            dbias_ref[...] += jnp.sum(dzn, axis=0, keepdims=True)
            dzh = dzn * gain
            dz = rstd * (dzh - jnp.mean(dzh, axis=-1, keepdims=True)
                         - zhat * jnp.mean(dzh * zhat, axis=-1, keepdims=True))
            dp_ref[rows, GMLP_W:] = (dz * _gelu_grad(pz)).astype(BF)

        @pl.when(i == pl.num_programs(0) - 1)
        def _():
            for gg in range(4):
                dw_ref[gg] = jnp.where(tril, dw_ref[gg], 0.0)
                dbs_ref[gg] = jnp.sum(dbacc_ref[gg], axis=1, keepdims=True)

    full2 = lambda s: pl.BlockSpec(s, lambda i: (0, 0))
    full3 = lambda s: pl.BlockSpec(s, lambda i: (0, 0, 0))
    return pl.pallas_call(
        body, name="gmlp_bwd", grid=(S // tm,),
        out_shape=(jax.ShapeDtypeStruct((S, 2 * GMLP_W), BF), jax.ShapeDtypeStruct((4, CHUNK, CHUNK), F32),
                   jax.ShapeDtypeStruct((4, CHUNK, 1), F32), jax.ShapeDtypeStruct((1, GMLP_W), F32),
                   jax.ShapeDtypeStruct((1, GMLP_W), F32)),
        in_specs=[pl.BlockSpec((tm, 2 * GMLP_W), lambda i: (i, 0)), pl.BlockSpec((tm, GMLP_W), lambda i: (i, 0)),
                  full2((1, GMLP_W)), full2((1, GMLP_W)), full3((4, CHUNK, CHUNK)), full3((4, CHUNK, CHUNK)),
                  full2((CHUNK, 4))],
        out_specs=(pl.BlockSpec((tm, 2 * GMLP_W), lambda i: (i, 0)), full3((4, CHUNK, CHUNK)), full3((4, CHUNK, 1)),
                   full2((1, GMLP_W)), full2((1, GMLP_W))),
        scratch_shapes=[pltpu.VMEM((4, CHUNK, CHUNK), F32)],
        compiler_params=_cp(("arbitrary",)))(puz, dyg, ln_g, ln_b, w_sp, w_sp_t, b_sp_t)


def _attn_bwd(qkvs, dyas, dsums, lses):
    S = qkvs[0].shape[1] * qkvs[0].shape[2]
    nsub = min(ATTN_NSUB, S // max(DILS) // QB)
    R = nsub * QB
    nsteps = S // R
    NIN = 12

    def group_body(first, last, q_ref, qn_ref, kp_ref, kc_ref, vp_ref, vc_ref, dy_ref, dyn_ref, e_ref, en_ref,
                   l_ref, ln_ref, o_ref):
        band, band_first = _band_masks(first)
        row = lax.broadcasted_iota(jnp.int32, (QB, QB), 0)
        col = lax.broadcasted_iota(jnp.int32, (QB, QB), 1)
        mask_next = jnp.logical_and(col >= row, jnp.logical_not(last))
        kc, vc = kc_ref[...], vc_ref[...]
        kfull = jnp.concatenate([kp_ref[...], kc], axis=0)
        vfull = jnp.concatenate([vp_ref[...], vc], axis=0)
        k_last, v_last = kc[(nsub - 1) * QB:], vc[(nsub - 1) * QB:]
        q_ext = jnp.concatenate([q_ref[...], qn_ref[...]], axis=0)
        dy_ext = jnp.concatenate([dy_ref[...], dyn_ref[...]], axis=0)
        esum, esum_n, lse, lse_n = e_ref[...], en_ref[...], l_ref[...], ln_ref[...]
        win = lambda t, sb: t[sb * QB:(sb + 2) * QB]
        blk = lambda t, sb: t[sb * QB:(sb + 1) * QB]
        hms = [_head_mask(hh) for hh in range(NSLOT)]
        q_hs = [jnp.where(hm, q_ext, 0) for hm in hms]
        dy_hs = [jnp.where(hm, dy_ext, 0) for hm in hms]
        raw = []
        for hh in range(NSLOT):
            tiles = [(_dot_nt(blk(q_hs[hh], sb), win(kfull, sb)), _dot_nt(blk(dy_hs[hh], sb), win(vfull, sb)))
                     for sb in range(nsub)]
            tiles.append((_dot_nt(blk(q_hs[hh], nsub), k_last), _dot_nt(blk(dy_hs[hh], nsub), v_last)))
            raw.append(tiles)
        ps, dss = [], []
        for hh in range(NSLOT):
            rowstat = lambda t: jnp.max(jnp.where(hms[hh], t, -jnp.inf), axis=1, keepdims=True)
            p_h, ds_h = [], []
            for sb in range(nsub + 1):
                sc, dp = raw[hh][sb]
                if sb < nsub:
                    msk, lrow, erow = (band_first if sb == 0 else band), rowstat(blk(lse, sb)), rowstat(blk(esum, sb))
                else:
                    msk, lrow, erow = mask_next, rowstat(lse_n), rowstat(esum_n)
                p = jnp.where(msk, jnp.exp(sc * SCALE - lrow), 0.0)
                p_h.append(p.astype(BF))
                ds_h.append((p * (dp - erow)).astype(BF))
            ps.append(p_h)
            dss.append(ds_h)
        dq = [jnp.zeros((QB, GW), F32) for _ in range(nsub)]
        dk = [jnp.zeros((QB, GW), F32) for _ in range(nsub)]
        dv = [jnp.zeros((QB, GW), F32) for _ in range(nsub)]
        for hh in range(NSLOT):
            for sb in range(nsub):
                dq[sb] = dq[sb] + jnp.where(hms[hh], _dot(dss[hh][sb], win(kfull, sb)), 0.0)
                nxt = lambda t: t[sb + 1][:, :QB] if sb + 1 < nsub else t[nsub]
                dk[sb] = dk[sb] + _dot_tn(jnp.concatenate([dss[hh][sb][:, QB:], nxt(dss[hh])], axis=0), win(q_hs[hh], sb))
                dv[sb] = dv[sb] + _dot_tn(jnp.concatenate([ps[hh][sb][:, QB:], nxt(ps[hh])], axis=0), win(dy_hs[hh], sb))
        for sb in range(nsub):
            rows = slice(sb * QB, (sb + 1) * QB)
            o_ref[0, rows, :] = (dq[sb] * SCALE).astype(BF)
            o_ref[1, rows, :] = (dk[sb] * SCALE).astype(BF)
            o_ref[2, rows, :] = dv[sb].astype(BF)

    def body(*refs):
        g, n = pl.program_id(0), pl.program_id(1)
        for k, d in enumerate(DILS):
            @pl.when(g == k)
            def _():
                per = nsteps // d
                group_body(n % per == 0, n % per == per - 1, *refs[NIN * k:NIN * (k + 1)], refs[NIN * len(DILS) + k])

    in_specs, out_specs, operands = [], [], []
    for k in range(len(DILS)):
        at = _group_step(k, nsteps)
        prev = lambda g, n, at=at: jnp.maximum(at(g, n) * nsub - 1, 0)
        nxt = lambda g, n, at=at: jnp.minimum((at(g, n) + 1) * nsub, S // QB - 1)
        cur4 = lambda t, at=at: pl.BlockSpec((None, R, GW), lambda g, n: (t, at(g, n), 0))
        prv4 = lambda t, prev=prev: pl.BlockSpec((None, QB, GW), lambda g, n: (t, prev(g, n), 0))
        nxt4 = lambda t, nxt=nxt: pl.BlockSpec((None, QB, GW), lambda g, n: (t, nxt(g, n), 0))
        cur3 = pl.BlockSpec((R, GW), lambda g, n, at=at: (at(g, n), 0))
        nxt3 = pl.BlockSpec((QB, GW), lambda g, n, nxt=nxt: (nxt(g, n), 0))
        in_specs += [cur4(0), nxt4(0), prv4(1), cur4(1), prv4(2), cur4(2), cur3, nxt3, cur3, nxt3, cur3, nxt3]
        out_specs.append(pl.BlockSpec((3, R, GW), lambda g, n, at=at: (0, at(g, n), 0)))
        flat = lambda t: t.reshape(S, GW)
        operands += [qkvs[k].reshape(3, S, GW)] * 6 + [flat(dyas[k])] * 2 + [flat(dsums[k])] * 2 + [flat(lses[k])] * 2
    outs = pl.pallas_call(
        body, name="attn_bwd", grid=(len(DILS), nsteps),
        out_shape=(jax.ShapeDtypeStruct((3, S, GW), BF),) * len(DILS),
        in_specs=in_specs, out_specs=tuple(out_specs),
        compiler_params=_cp(("arbitrary", "arbitrary")))(*operands)
    return [t.reshape(3, d, S // d, GW) for t, d in zip(outs, DILS)]


def _inproj_bwd(dqkvs, dpuz, dgates, win_t, cos_t, sin_t, x, dx2, g1):
    S = x.shape[0]
    tm = min(512, S)

    def body(d0_ref, d1_ref, d2_ref, dp_ref, dg_ref, w_ref, c_ref, s_ref,
             x_ref, dx2_ref, g1_ref, gx_ref, dg1_ref, dn_ref, scr):
        i = pl.program_id(0)

        @pl.when(i == 0)
        def _():
            dg1_ref[...] = jnp.zeros_like(dg1_ref)

        dh = _dot(dp_ref[...], w_ref[W_UZ0:W_G0, :]) + _dot(dg_ref[...], w_ref[W_G0:, :])
        dn_ref[:, W_UZ0:W_G0] = dp_ref[...]
        dn_ref[:, W_G0:] = dg_ref[...]
        for t in range(3):
            for g, (d_ref, d) in enumerate(zip((d0_ref, d1_ref, d2_ref), DILS)):
                piece = _undilate_load(d_ref, (t,), d, scr, tm)
                if t < 2:
                    piece = _rope_bwd(piece, c_ref, s_ref)
                dn_ref[:, (3 * t + g) * GW:(3 * t + g + 1) * GW] = piece.astype(BF)
        dh = dh + _dot(dn_ref[:, :W_UZ0], w_ref[:W_UZ0, :])
        xv = x_ref[...]
        dx1, dg1 = _rms_bwd(xv, _rms_stats(xv), g1_ref[...], dh)
        gx_ref[...] = dx2_ref[...] + dx1
        dg1_ref[...] += dg1

    row = lambda w: pl.BlockSpec((tm, w), lambda i: (i, 0))
    full = lambda s: pl.BlockSpec(s, lambda i: (0, 0))
    dil = lambda d: pl.BlockSpec((3, d, tm // d, GW), lambda i: (0, 0, i, 0))
    return pl.pallas_call(
        body, name="inproj_bwd", grid=(S // tm,),
        out_shape=(jax.ShapeDtypeStruct((S, D), F32), jax.ShapeDtypeStruct((1, D), F32),
                   jax.ShapeDtypeStruct((S, win_t.shape[0]), BF)),
        in_specs=[dil(d) for d in DILS] + [row(2 * GMLP_W), row(2 * D), _resident(win_t.shape), row(128), row(128), row(D), row(D),
                                            full((1, D))],
        out_specs=(row(D), full((1, D)), row(win_t.shape[0])),
        scratch_shapes=[pltpu.VMEM((2, tm, 128), F32)],
        compiler_params=_cp(("arbitrary",)))(*dqkvs, dpuz, dgates, win_t, cos_t, sin_t, x, dx2, g1)


def _row_tile(rows, cols):
    cap = max(16, ELEMENTWISE_BLOCK_BYTES // (4 * cols))
    return max(t for t in range(16, cap + 1, 16) if rows % t == 0)


def _adam_math(w, g, m, v):
    m2 = ADAM_B1 * m + (1.0 - ADAM_B1) * g
    v2 = ADAM_B2 * v + (1.0 - ADAM_B2) * (g * g)
    m_hat = m2 / (1.0 - ADAM_B1 ** ADAM_STEP)
    v_hat = v2 / (1.0 - ADAM_B2 ** ADAM_STEP)
    delta = -ADAM_LR * (m_hat / (jnp.sqrt(v_hat) + ADAM_EPS) + ADAM_WD * w)
    return delta, m2, v2


def _adam_shard(own, recv, w, m, v, name):
    R, C = w.shape
    tr = _row_tile(R, C)

    def body(own_ref, r_ref, w_ref, m_ref, v_ref, g_ref, d_ref, m2_ref, v2_ref):
        g = own_ref[...] + r_ref[0].astype(F32) + r_ref[1].astype(F32) + r_ref[2].astype(F32)
        g_ref[...] = g
        d_ref[...], m2_ref[...], v2_ref[...] = _adam_math(w_ref[...], g, m_ref[...], v_ref[...])

    spec = pl.BlockSpec((tr, C), lambda i: (i, 0))
    out = jax.ShapeDtypeStruct((R, C), F32)
    return pl.pallas_call(
        body, name=name, grid=(R // tr,), out_shape=(out, out, out, out),
        in_specs=[spec, pl.BlockSpec((3, tr, C), lambda i: (0, i, 0)), spec, spec, spec],
        out_specs=(spec, spec, spec, spec), compiler_params=_cp(("parallel",)))(own, recv, w, m, v)


def _rs_add(gblocks, recv, idx, name):
    _, R, C = gblocks.shape
    tr = _row_tile(R, C)

    def body(t_ref, g_ref, r_ref, own_ref, send_ref):
        j = pl.program_id(1)
        s = g_ref[...] + r_ref[...]

        @pl.when(j == 0)
        def _():
            own_ref[...] = s

        @pl.when(j > 0)
        def _():
            send_ref[...] = s.astype(BF)

    grid_spec = pltpu.PrefetchScalarGridSpec(
        num_scalar_prefetch=1, grid=(R // tr, 4),
        in_specs=[pl.BlockSpec((None, tr, C), lambda i, j, t: (t[j], i, 0)),
                  pl.BlockSpec((None, tr, C), lambda i, j, t: (t[4 + j], i, 0))],
        out_specs=[pl.BlockSpec((tr, C), lambda i, j, t: (i, 0)),
                   pl.BlockSpec((None, tr, C), lambda i, j, t: (jnp.maximum(j - 1, 0), i, 0))])
    return pl.pallas_call(
        body, name=name, grid_spec=grid_spec,
        out_shape=(jax.ShapeDtypeStruct((R, C), F32), jax.ShapeDtypeStruct((3, R, C), BF)),
        compiler_params=_cp(("parallel", "arbitrary")))(idx, gblocks, recv)


def _mesh_pos():
    return lax.axis_index("x"), lax.axis_index("y"), lax.axis_index("c")


_HBM = pl.BlockSpec(memory_space=pltpu.HBM)
_SEM = pl.BlockSpec(memory_space=pltpu.SEMAPHORE)
_EFFECT = pltpu.SideEffectType.DATAFLOW_SIDE_EFFECTING
_RELATIONS = [(dx, dy, dc) for dx in (0, 1) for dy in (0, 1) for dc in (0, 1)][1:]


def _flip(v, d):
    return 1 - v if d else v


def _plan_gather(n):
    def plan(x, y, c):
        return [(k, None, 4 * x + 2 * y + c, (_flip(x, dx), _flip(y, dy), _flip(c, dc)))
                for k in range(n) for dx, dy, dc in _RELATIONS]
    return plan


def _plan_gather_near(x, y, c):
    chips = [(1 - x, y), (x, 1 - y), (1 - x, 1 - y)]
    return [(0, None, 4 * x + 2 * y + c, (x, y, 1 - c))] + [(0, None, 4 * x + 2 * y + c, (*chip, c)) for chip in chips]


def _plan_gather_pass(from_landing):
    def plan(x, y, c):
        blocks = [4 * cx + 2 * cy + c for cx, cy in ((1 - x, y), (x, 1 - y), (1 - x, 1 - y))]
        return [(0, b if from_landing else None, b, (x, y, 1 - c)) for b in blocks]
    return plan


def _plan_d2d(n):
    def plan(x, y, c):
        return [(k, 2 * kk + 1 - c, kk, (x, y, 1 - c)) for k in range(n) for kk in range(4)]
    return plan


def _plan_ici(n):
    def plan(x, y, c):
        return [(k, j, j, (_flip(x, dx), _flip(y, dy), c))
                for k in range(n) for j, (dx, dy) in enumerate(((1, 0), (0, 1), (1, 1)))]
    return plan


def _plan_copies(plan, src_refs, land_refs, send_sems, recv_sems):
    x, y, c = _mesh_pos()
    return [pltpu.make_async_remote_copy(
        src_ref=src_refs[k] if si is None else src_refs[k].at[si], dst_ref=land_refs[k].at[di],
        send_sem=send_sems.at[n], recv_sem=recv_sems.at[n], device_id=dev, device_id_type=MESH)
        for n, (k, si, di, dev) in enumerate(plan(x, y, c))]


def _exchange_start(srcs, land_shapes, plan, ncopies, name, after):
    n = len(srcs)

    def body(*refs):
        src_refs, land_refs = refs[:n], refs[n:2 * n]
        send_sems, recv_sems = refs[2 * n + len(after)], refs[2 * n + len(after) + 1]
        token = refs[-1]
        for cp in _plan_copies(plan, src_refs, land_refs, send_sems, recv_sems):
            cp.start()
        token[...] = jnp.zeros_like(token)

    lands = [pltpu.with_memory_space_constraint(lax.empty(s, a.dtype), pltpu.HBM) for s, a in zip(land_shapes, srcs)]
    srcs = [pltpu.with_memory_space_constraint(a, pltpu.HBM) for a in srcs]
    outs = pl.pallas_call(
        body, name=name,
        out_shape=(pltpu.SemaphoreType.DMA((ncopies,)), pltpu.SemaphoreType.DMA((ncopies,)))
        + tuple(pltpu.HBM(a.shape, a.dtype) for a in srcs) + tuple(pltpu.HBM(a.shape, a.dtype) for a in lands)
        + (jax.ShapeDtypeStruct((8, 128), F32),),
        in_specs=[_HBM] * (2 * n) + [pl.BlockSpec(memory_space=pl.ANY)] * len(after),
        out_specs=(_SEM, _SEM) + (_HBM,) * (2 * n) + (pl.BlockSpec(memory_space=pltpu.VMEM),),
        input_output_aliases={i: 2 + i for i in range(2 * n)},
        compiler_params=pltpu.CompilerParams(has_side_effects=_EFFECT))(*srcs, *lands, *after)
    return (outs[0], outs[1], list(outs[2:2 + n]), list(outs[2 + n:2 + 2 * n])), outs[-1]


def _exchange_forward(handle, plan, plan_fwd, needs, name, after):
    send_sems, recv_sems, srcs, lands = handle
    n, nfwd = len(srcs), len(needs)

    def body(*refs):
        src_refs, land_refs = refs[:n], refs[n:2 * n]
        outs = refs[2 * n + 2 + len(after):]
        first = _plan_copies(plan, src_refs, land_refs, refs[2 * n], refs[2 * n + 1])
        for cp, need in zip(_plan_copies(plan_fwd, land_refs, land_refs, outs[0], outs[1]), needs):
            first[need].wait_recv()
            cp.start()
        for k, cp in enumerate(first):
            cp.wait_send()
            if k not in needs:
                cp.wait_recv()

    outs = pl.pallas_call(
        body, name=name,
        out_shape=(pltpu.SemaphoreType.DMA((nfwd,)), pltpu.SemaphoreType.DMA((nfwd,)))
        + tuple(pltpu.HBM(a.shape, a.dtype) for a in srcs) + tuple(pltpu.HBM(a.shape, a.dtype) for a in lands),
        in_specs=[_HBM] * (2 * n) + [_SEM, _SEM] + [pl.BlockSpec(memory_space=pl.ANY)] * len(after),
        out_specs=(_SEM, _SEM) + (_HBM,) * (2 * n), input_output_aliases={i: 2 + i for i in range(2 * n)},
        compiler_params=pltpu.CompilerParams(has_side_effects=_EFFECT))(*srcs, *lands, send_sems, recv_sems, *after)
    return outs[0], outs[1], list(outs[2:2 + n]), list(outs[2 + n:2 + 2 * n])


def _exchange_wait(handle, plan, name, after):
    send_sems, recv_sems, srcs, lands = handle
    n = len(srcs)

    def body(*refs):
        src_refs, land_refs = refs[:n], refs[n:2 * n]
        for cp in _plan_copies(plan, src_refs, land_refs, refs[2 * n], refs[2 * n + 1]):
            cp.wait_send()
            cp.wait_recv()

    outs = pl.pallas_call(
        body, name=name,
        out_shape=tuple(pltpu.HBM(a.shape, a.dtype) for a in srcs) + tuple(pltpu.HBM(a.shape, a.dtype) for a in lands),
        in_specs=[_HBM] * (2 * n) + [_SEM, _SEM] + [pl.BlockSpec(memory_space=pl.ANY)] * len(after),
        out_specs=(_HBM,) * (2 * n), input_output_aliases={i: i for i in range(2 * n)},
        compiler_params=pltpu.CompilerParams(has_side_effects=_EFFECT))(*srcs, *lands, send_sems, recv_sems, *after)
    return list(outs[:n]), list(outs[n:])


SMALL = ("g1", "g2", "g3", "g4", "ln_g", "ln_b", "b_sp", "w_sp")


def _small_sum_adam(block, parts, lands, ws, ms, vs):
    n, ns = len(parts), len(SMALL)

    def body(blk_ref, *refs):
        p_refs, l_refs = refs[:n], refs[n:2 * n]
        w_refs, m_refs, v_refs = (refs[2 * n + i * ns:2 * n + (i + 1) * ns] for i in range(3))
        o = 2 * n + 3 * ns
        loss_ref = refs[o]
        g_out, d_out, m_out, v_out = (refs[o + 1 + i * ns:o + 1 + (i + 1) * ns] for i in range(4))
        me = blk_ref[0]
        sums = []
        for k in range(n):
            acc = jnp.where(me == 0, p_refs[k][...], l_refs[k][0])
            for b in range(1, 8):
                acc = acc + jnp.where(me == b, p_refs[k][...], l_refs[k][b])
            sums.append(acc)
        loss_ref[...] = sums[0]
        for i in range(ns):
            g_out[i][...] = sums[1 + i]
            d_out[i][...], m_out[i][...], v_out[i][...] = _adam_math(w_refs[i][...], sums[1 + i], m_refs[i][...],
                                                                     v_refs[i][...])

    args = list(parts) + list(lands) + [t[k] for t in (ws, ms, vs) for k in SMALL]
    shapes = [jax.ShapeDtypeStruct(p.shape, F32) for p in parts[:1]] + [jax.ShapeDtypeStruct(p.shape, F32) for p in parts[1:]] * 4
    vmem = pl.BlockSpec(memory_space=pltpu.VMEM)
    outs = pl.pallas_call(
        body, name="small_sum_adam", out_shape=tuple(shapes),
        in_specs=[pl.BlockSpec(memory_space=pltpu.SMEM)] + [vmem] * len(args), out_specs=(vmem,) * len(shapes),
        compiler_params=pltpu.CompilerParams(vmem_limit_bytes=VMEM_LIMIT))(block, *args)
    groups = [dict(zip(SMALL, outs[1 + i * ns:1 + (i + 1) * ns])) for i in range(4)]
    return (outs[0], *groups)


def _rope_tables(S):
    half = HD // 2
    inv_freq = jnp.tile(ROPE_THETA ** (-jnp.arange(half, dtype=F32) / half), 4)
    sign = jnp.tile(jnp.concatenate([-jnp.ones(half, F32), jnp.ones(half, F32)]), 2)
    ang = jnp.arange(S, dtype=F32)[:, None] * inv_freq[None, :]
    return jnp.cos(ang), jnp.sin(ang) * sign[None, :]


def _to_blocks(g, col_sharded):
    if col_sharded:
        return g.reshape(g.shape[0], 8, g.shape[1] // 8).transpose(1, 0, 2)
    return g.reshape(8, g.shape[0] // 8, g.shape[1])


def _from_blocks(t, col_sharded):
    if col_sharded:
        return t.transpose(1, 0, 2).reshape(t.shape[1], 8 * t.shape[2])
    return t.reshape(8 * t.shape[1], t.shape[2])


class _NoComm:
    def __init__(self, late_weights):
        self._late = late_weights
        self.grads = {}

    def start_tie(self):
        return jnp.zeros((8, 128), F32)

    def late_weights(self, after):
        return self._late

    def small_start(self, loss_row, grads):
        self.small = (loss_row, grads)
        return jnp.zeros((8, 128), F32)

    def rs_start(self, key, gblocks, after=()):
        self.grads[key] = gblocks
        return jnp.zeros((8, 128), F32)

    def rs_mid(self, key, after):
        return jnp.zeros((8, 128), F32)


class _FsdpComm:
    def __init__(self, late_shards, col_sharded, after, idx, block):
        self._col, self._idx, self._block, self._rs = col_sharded, idx, block, {}
        n = len(late_shards)
        self._gather, self._token = _exchange_start(
            late_shards, [(8,) + s.shape for s in late_shards], _plan_gather(n), 7 * n, "ag_late_start", (after,))

    def start_tie(self):
        return self._token

    def late_weights(self, after):
        shards, lands = _exchange_wait(self._gather, _plan_gather(len(self._col)), "ag_late_wait", after)
        lands = [lax.dynamic_update_index_in_dim(t, s, self._block, 0) for t, s in zip(lands, shards)]
        return [_from_blocks(t, cs) for t, cs in zip(lands, self._col)]

    def small_start(self, loss_row, grads):
        parts = [loss_row] + [grads[k] for k in SMALL]
        n = len(parts)
        self._small, token = _exchange_start(parts, [(8,) + p.shape for p in parts], _plan_gather(n), 7 * n,
                                             "small_start", (self._token,))
        return token

    def small_finish(self, ws, ms, vs, after):
        n = 1 + len(SMALL)
        parts, lands = _exchange_wait(self._small, _plan_gather(n), "small_wait", after)
        return _small_sum_adam(jnp.reshape(self._block, (1,)).astype(jnp.int32), parts, lands, ws, ms, vs)

    def rs_start(self, key, gblocks, after=()):
        n = len(gblocks)
        d2d, token = _exchange_start(gblocks, [(4,) + g.shape[1:] for g in gblocks], _plan_d2d(n), 4 * n,
                                     "rs_%s_d2d_start" % key, (self._token,) + tuple(after))
        self._rs[key] = dict(n=n, d2d=d2d)
        return token

    def rs_mid(self, key, after):
        st = self._rs[key]
        gblocks, from_sib = _exchange_wait(st["d2d"], _plan_d2d(st["n"]), "rs_%s_d2d_wait" % key, after)
        halves = [_rs_add(g, r, self._idx, "rs_add_%s_%d" % (key, k)) for k, (g, r) in enumerate(zip(gblocks, from_sib))]
        st["own"] = [own for own, _ in halves]
        sends = [send for _, send in halves]
        st["ici"], token = _exchange_start(sends, [t.shape for t in sends], _plan_ici(st["n"]), 3 * st["n"],
                                           "rs_%s_ici_start" % key, (self._token,))
        return token

    def rs_end(self, key, after):
        st = self._rs[key]
        return st["own"], _exchange_wait(st["ici"], _plan_ici(st["n"]), "rs_%s_ici_wait" % key, after)[1]


def _local_step(x, tgt, h1, cos_t, sin_t, win_t, comm, g1, g2, g3, g4, ln_g, ln_b, w_sp, b_sp):
    b_sp_t = b_sp.T
    w_sp_t = w_sp.transpose(0, 2, 1)

    *qkvs, puz, gates = _proj_all(h1, win_t, cos_t, sin_t, comm.start_tie())
    fwd = _attn_fwd(qkvs)
    ya, *lses = _attn_combine([o for o, _ in fwd], [l for _, l in fwd])
    yg = _gmlp_fwd(puz, ln_g, ln_b, w_sp, b_sp_t)
    wba, wbg, wout, wmi, wmo = comm.late_weights(after=(ya, yg, gates))
    merged, y, x2, h3 = _merge_fwd(ya, yg, gates, wba, wbg, wout, x, g2, g3)
    a, dy2, dout, loss_row, dg4 = _mlp_fwd(h3, wmi, wmo, x2, tgt, g4)

    dpre, dx2, dy, dg3, dg2 = _mlp_bwd(dy2, a, wmo, wmi, x2, y, dout, g2, g3)
    dwmo = _mm_tn(a, dy2, "dw_mlp_out", square_a=True, tk=MM_TN_TK // 2)
    dwmi = _mm_tn(h3, dpre, "dw_mlp_in", col_blocks=8)
    tie = comm.rs_start("mlp", [dwmi, _to_blocks(dwmo, False)])
    dgates, da, db, dyg, *rest = _outproj_bwd(dy, wout, ya, yg, gates, wba, wbg, tie)
    dyas, dsums = rest[:3], rest[3:]
    tie = comm.rs_mid("mlp", after=(dyg,))
    dpuz, dwsp, dbs, dlng, dlnb = _gmlp_bwd(puz, dyg, ln_g + tie[0, 0], ln_b, w_sp, w_sp_t, b_sp_t)
    dqkvs = _attn_bwd(qkvs, dyas, dsums, lses)
    grad_x, dg1, dproj = _inproj_bwd(dqkvs, dpuz, dgates, win_t, cos_t, sin_t, x, dx2, g1)
    small = dict(g1=dg1, g2=dg2, g3=dg3, g4=dg4, ln_g=dlng, ln_b=dlnb, b_sp=dbs.reshape(4, CHUNK),
                 w_sp=dwsp.reshape(4 * CHUNK, CHUNK))
    tie = comm.small_start(loss_row, small)
    dwin_t = _mm_tn(dproj, h1, "dw_in", tie=tie)
    tie = comm.rs_start("win", [_to_blocks(dwin_t, False)])
    dwout = _mm_tn(merged, dy, "dw_out", tie=tie)
    tie = comm.rs_mid("win", after=(dwout,))
    dwba = _mm_tn(ya, da, "dw_branch_attn", tie=tie, col_blocks=8)
    dwbg = _mm_tn(yg, db, "dw_branch_gmlp", tie=tie, col_blocks=8)
    tie = comm.rs_start("mid", [dwba, dwbg, _to_blocks(dwout, False)], after=(tie,))
    return grad_x, tie


def kernel(x, norm_pre_mix, w_in, w_spatial, b_spatial, ln_v_gain, ln_v_bias, w_branch_attn, w_branch_gmlp, w_out, norm_post_mix, norm_pre_mlp, w_mlp_in, w_mlp_out, norm_post_mlp, loss_target, m_norm_pre_mix, m_w_in, m_w_spatial, m_b_spatial, m_ln_v_gain, m_ln_v_bias, m_w_branch_attn, m_w_branch_gmlp, m_w_out, m_norm_post_mix, m_norm_pre_mlp, m_w_mlp_in, m_w_mlp_out, m_norm_post_mlp, v_norm_pre_mix, v_w_in, v_w_spatial, v_b_spatial, v_ln_v_gain, v_ln_v_bias, v_w_branch_attn, v_w_branch_gmlp, v_w_out, v_norm_post_mix, v_norm_pre_mlp, v_w_mlp_in, v_w_mlp_out, v_norm_post_mlp):
    mx, my, mc = _mesh_pos()
    rel = [(0, 0), (1, 0), (0, 1), (1, 1)]
    chip_of = [2 * (mx ^ dx) + (my ^ dy) for dx, dy in rel]
    idx = jnp.stack([2 * k + mc for k in chip_of] + chip_of).astype(jnp.int32)

    w_in_t, m_w_in_t, v_w_in_t = (t[0].T for t in (w_in, m_w_in, v_w_in))
    shard = w_in_t.astype(BF)
    gather, token = _exchange_start([shard], [(8,) + shard.shape], _plan_gather_near, 4, "ag_win_start", ())
    h1 = _rms_fwd(x[0], norm_pre_mix + token[0, 0], "rms_pre_mix")
    cos_t, sin_t = _rope_tables(x.shape[1])
    gather = _exchange_forward(gather, _plan_gather_near, _plan_gather_pass(True), (1, 2, 3), "ag_win_pass",
                               (h1, cos_t, sin_t))
    (shard,), (g_win,) = _exchange_wait(gather, _plan_gather_pass(False), "ag_win_wait", ())
    g_win = lax.dynamic_update_index_in_dim(g_win, shard, 4 * mx + 2 * my + mc, 0)
    win_t = _from_blocks(g_win, False)
    late = [w_branch_attn[0], w_branch_gmlp[0], w_out[0], w_mlp_in[0], w_mlp_out[0]]
    comm = _FsdpComm([w.astype(BF) for w in late], [True, True, False, True, False], g_win, idx, 4 * mx + 2 * my + mc)

    grad_x, tie = _local_step(
        x[0], loss_target[0], h1, cos_t, sin_t, win_t, comm,
        norm_pre_mix, norm_post_mix, norm_pre_mlp, norm_post_mlp, ln_v_gain, ln_v_bias, w_spatial[0], b_spatial[0])

    flat = lambda t: t.reshape(-1, t.shape[-1])
    small_w = dict(zip(SMALL, map(flat, (norm_pre_mix, norm_post_mix, norm_pre_mlp, norm_post_mlp, ln_v_gain, ln_v_bias,
                                         b_spatial, w_spatial))))
    small_m = dict(zip(SMALL, map(flat, (m_norm_pre_mix, m_norm_post_mix, m_norm_pre_mlp, m_norm_post_mlp, m_ln_v_gain,
                                         m_ln_v_bias, m_b_spatial, m_w_spatial))))
    small_v = dict(zip(SMALL, map(flat, (v_norm_pre_mix, v_norm_post_mix, v_norm_pre_mlp, v_norm_post_mlp, v_ln_v_gain,
                                         v_ln_v_bias, v_b_spatial, v_w_spatial))))
    loss_out, sg, sd, sm, sv = comm.small_finish(small_w, small_m, small_v, after=(tie,))
    loss = loss_out[0, 0]
    tie = comm.rs_mid("mid", after=(loss_out,))
    mlp_own, mlp_chips = comm.rs_end("mlp", after=(tie,))
    adam = lambda nm, own, r, w, m, v: _adam_shard(own, r, w[0], m[0], v[0], "adam_" + nm)
    upd = {
        "w_mlp_in": adam("w_mlp_in", mlp_own[0], mlp_chips[0], w_mlp_in, m_w_mlp_in, v_w_mlp_in),
        "w_mlp_out": adam("w_mlp_out", mlp_own[1], mlp_chips[1], w_mlp_out, m_w_mlp_out, v_w_mlp_out),
    }
    win_own, win_chips = comm.rs_end("win", after=(upd["w_mlp_in"][0], upd["w_mlp_out"][0]))
    upd["w_in"] = tuple(t.T for t in _adam_shard(win_own[0], win_chips[0], w_in_t, m_w_in_t, v_w_in_t, "adam_w_in"))
    mid_own, mid_chips = comm.rs_end("mid", after=(upd["w_in"][0],))
    upd["w_branch_attn"] = adam("w_branch_attn", mid_own[0], mid_chips[0], w_branch_attn, m_w_branch_attn, v_w_branch_attn)
    upd["w_branch_gmlp"] = adam("w_branch_gmlp", mid_own[1], mid_chips[1], w_branch_gmlp, m_w_branch_gmlp, v_w_branch_gmlp)
    upd["w_out"] = adam("w_out", mid_own[2], mid_chips[2], w_out, m_w_out, v_w_out)
    order = ["g1", "w_in", "w_sp", "b_sp", "ln_g", "ln_b", "w_branch_attn", "w_branch_gmlp", "w_out", "g2", "g3",
             "w_mlp_in", "w_mlp_out", "g4"]
    small_shape = dict(g1=norm_pre_mix.shape, g2=norm_post_mix.shape, g3=norm_pre_mlp.shape, g4=norm_post_mlp.shape,
                       ln_g=ln_v_gain.shape, ln_b=ln_v_bias.shape, b_sp=b_spatial.shape, w_sp=w_spatial.shape)

    def pick(which):
        return [upd[nm][which][None] if nm in upd else (sg, sd, sm, sv)[which][nm].reshape(small_shape[nm])
                for nm in order]

    return (loss, grad_x[None], *pick(0), *pick(1), *pick(2), *pick(3))
```

```python
import math

import jax
import jax.numpy as jnp
from jax import lax
from jax.experimental import pallas as pl
from jax.experimental.pallas import tpu as pltpu

D = 1024
HD = 64
NSLOT = 4
GW = NSLOT * HD
DILS = (1, 4, 16)
QB = 128
ATTN_NSUB = 4
PROJ_TM = 1024
MLP_TM_FWD = 512
MLP_TM_BWD = 256
MM_TN_TK = 4096
MM_TN_TILE = 1024
ELEMENTWISE_BLOCK_BYTES = 1 << 20
AW = 3 * GW
W_UZ0, W_G0 = 3 * AW, 3 * AW + 1024
GMLP_W = 512
CHUNK = 128
DFF = 4096
EPS = 1e-6
ROPE_THETA = 10000.0
SCALE = HD ** -0.5
NEG = -1e30

ADAM_LR = 0.001
ADAM_B1 = 0.9
ADAM_B2 = 0.999
ADAM_EPS = 1e-08
ADAM_WD = 0.01
ADAM_STEP = 10

BF = jnp.bfloat16
F32 = jnp.float32
MESH = pl.DeviceIdType.MESH
VMEM_LIMIT = 56 * 1024 * 1024


def _cp(sem):
    return pltpu.CompilerParams(dimension_semantics=sem, vmem_limit_bytes=VMEM_LIMIT)


def _dot(a, b):
    return jnp.dot(a, b, preferred_element_type=F32)


def _dot_nt(a, b):
    return lax.dot_general(a, b, (((1,), (1,)), ((), ())), preferred_element_type=F32)


def _dot_tn(a, b):
    return lax.dot_general(a, b, (((0,), (0,)), ((), ())), preferred_element_type=F32)


def _gelu(x):
    return jax.nn.gelu(x, approximate=True)


def _gelu_grad(x):
    k = math.sqrt(2.0 / math.pi)
    t = jnp.tanh(k * (x + 0.044715 * x * x * x))
    return 0.5 * (1.0 + t) + 0.5 * x * (1.0 - t * t) * (k * (1.0 + 3.0 * 0.044715 * x * x))


def _swap_halves(t):
    w = t.shape[1]
    lane = lax.broadcasted_iota(jnp.int32, t.shape, 1)
    first = (lane & (HD - 1)) < (HD // 2)
    return jnp.where(first, pltpu.roll(t, w - HD // 2, 1), pltpu.roll(t, HD // 2, 1))


def _head_mask(hh):
    lane = lax.broadcasted_iota(jnp.int32, (1, GW), 1)
    return jnp.logical_and(lane >= hh * HD, lane < (hh + 1) * HD)


def _rms_stats(xf):
    return lax.rsqrt(jnp.mean(xf * xf, axis=-1, keepdims=True) + EPS)


def _rms_bwd(xf, r, gain, dout):
    n = xf * r
    t = dout * gain
    dx = r * (t - n * jnp.mean(t * n, axis=-1, keepdims=True))
    return dx, jnp.sum(dout * n, axis=0, keepdims=True)


def _rms_fwd(x, gain, name):
    S = x.shape[0]
    tm = min(512, S)

    def body(x_ref, g_ref, h_ref):
        xf = x_ref[...]
        h_ref[...] = (xf * _rms_stats(xf) * g_ref[...]).astype(BF)

    return pl.pallas_call(
        body, name=name, out_shape=jax.ShapeDtypeStruct((S, D), BF), grid=(S // tm,),
        in_specs=[pl.BlockSpec((tm, D), lambda i: (i, 0)), pl.BlockSpec((1, D), lambda i: (0, 0))],
        out_specs=pl.BlockSpec((tm, D), lambda i: (i, 0)), compiler_params=_cp(("parallel",)))(x, gain)


def _dilate_store(val, scr, o_ref, lead, d):
    rows = val.shape[0]
    if d == 1:
        o_ref[lead + (0,)] = val.astype(o_ref.dtype)
        return
    for hf in range(2):
        scr[hf, pl.ds(0, rows), :] = val[:, hf * 128:(hf + 1) * 128]
    for r in range(d):
        for hf in range(2):
            o_ref[lead + (r, slice(None), slice(hf * 128, (hf + 1) * 128))] = (
                scr[hf, pl.ds(r, rows // d, stride=d), :].astype(o_ref.dtype))


def _undilate_load(i_ref, lead, d, scr, rows):
    if d == 1:
        return i_ref[lead + (0,)].astype(F32)
    for r in range(d):
        for hf in range(2):
            scr[hf, pl.ds(r, rows // d, stride=d), :] = (
                i_ref[lead + (r, slice(None), slice(hf * 128, (hf + 1) * 128))].astype(F32))
    return jnp.concatenate([scr[0, pl.ds(0, rows), :], scr[1, pl.ds(0, rows), :]], axis=1)


def _rope_fwd(y, c_ref, s_ref):
    cosv = jnp.concatenate([c_ref[...]] * 2, axis=1)
    sinv = jnp.concatenate([s_ref[...]] * 2, axis=1)
    return y * cosv + _swap_halves(y) * sinv


def _rope_bwd(dy, c_ref, s_ref):
    cosv = jnp.concatenate([c_ref[...]] * 2, axis=1)
    sinv = jnp.concatenate([s_ref[...]] * 2, axis=1)
    return dy * cosv + _swap_halves(dy * sinv)


def _proj_all(h, win_t, cos_t, sin_t, tie):
    S, K = h.shape
    tm = min(PROJ_TM, S)

    def body(h_ref, w_ref, c_ref, s_ref, tie_ref, o0_ref, o1_ref, o2_ref, p_ref, gt_ref, scr):
        hv = h_ref[...]
        col = lambda b: _dot_nt(hv, w_ref[b * GW:(b + 1) * GW, :])
        ys = [col(3 * t) for t in range(3)]
        for g, (o_ref, d) in enumerate(zip((o0_ref, o1_ref, o2_ref), DILS)):
            nxt = [col(3 * t + g + 1) if g < 2 else col(W_UZ0 // GW + t) for t in range(3)]
            for t in range(3):
                y = _rope_fwd(ys[t], c_ref, s_ref) if t < 2 else ys[t]
                _dilate_store(y, scr.at[t], o_ref, (t,), d)
            ys = nxt
        for b in range((W_G0 - W_UZ0) // GW):
            y = ys[b] if b < 3 else col(W_UZ0 // GW + b)
            p_ref[:, b * GW:(b + 1) * GW] = y.astype(BF)
        for b in range(2 * D // GW):
            gt_ref[:, b * GW:(b + 1) * GW] = jax.nn.sigmoid(col(W_G0 // GW + b)).astype(BF)

    row = lambda w: pl.BlockSpec((tm, w), lambda i: (i, 0))
    return pl.pallas_call(
        body, name="proj_all", grid=(S // tm,),
        out_shape=tuple(jax.ShapeDtypeStruct((3, d, S // d, GW), BF) for d in DILS)
        + (jax.ShapeDtypeStruct((S, W_G0 - W_UZ0), BF), jax.ShapeDtypeStruct((S, 2 * D), BF)),
        in_specs=[row(K), _resident(win_t.shape), row(128), row(128), pl.BlockSpec(memory_space=pl.ANY)],
        out_specs=tuple(pl.BlockSpec((3, d, tm // d, GW), lambda i: (0, 0, i, 0)) for d in DILS)
        + (row(W_G0 - W_UZ0), row(2 * D)),
        scratch_shapes=[pltpu.VMEM((3, 2, tm, 128), F32)],
        compiler_params=_cp(("parallel",)))(h, win_t, cos_t, sin_t, tie)


def _band_masks(first_step):
    row = lax.broadcasted_iota(jnp.int32, (QB, 2 * QB), 0)
    col = lax.broadcasted_iota(jnp.int32, (QB, 2 * QB), 1)
    band = jnp.logical_and(col >= row, col <= row + QB)
    return band, jnp.logical_and(band, jnp.logical_or(col >= QB, jnp.logical_not(first_step)))


def _group_step(k, nsteps):
    return lambda g, n: jnp.where(g == k, n, jnp.where(g < k, 0, nsteps - 1))


def _attn_fwd(qkvs):
    S = qkvs[0].shape[1] * qkvs[0].shape[2]
    nsub = min(ATTN_NSUB, S // max(DILS) // QB)
    R = nsub * QB
    nsteps = S // R

    def group_body(first, q_ref, kp_ref, kc_ref, vp_ref, vc_ref, o_ref, lse_ref):
        band, band_first = _band_masks(first)
        kfull = jnp.concatenate([kp_ref[...], kc_ref[...]], axis=0)
        vfull = jnp.concatenate([vp_ref[...], vc_ref[...]], axis=0)
        chains = [(sb, hh) for sb in range(nsub) for hh in range(NSLOT)]
        win = lambda t, sb: t[sb * QB:(sb + 2) * QB]
        scores = []
        for sb, hh in chains:
            qh = jnp.where(_head_mask(hh), q_ref[sb * QB:(sb + 1) * QB, :], 0)
            scores.append(_dot_nt(qh, win(kfull, sb)))
        soft = []
        for (sb, hh), sc in zip(chains, scores):
            sc = jnp.where(band_first if sb == 0 else band, sc * SCALE, NEG)
            m = jnp.max(sc, axis=1, keepdims=True)
            p = jnp.exp(sc - m)
            den = jnp.sum(p, axis=1, keepdims=True)
            soft.append((p.astype(BF), den, m + jnp.log(den)))
        accs = [_dot(p, win(vfull, sb)) for (sb, hh), (p, _, _) in zip(chains, soft)]
        for sb in range(nsub):
            o = jnp.zeros((QB, GW), F32)
            lse = jnp.zeros((QB, GW), F32)
            for hh in range(NSLOT):
                hm = _head_mask(hh)
                _, den, lrow = soft[sb * NSLOT + hh]
                o = o + jnp.where(hm, accs[sb * NSLOT + hh] / den, 0.0)
                lse = lse + jnp.where(hm, lrow, 0.0)
            o_ref[sb * QB:(sb + 1) * QB, :] = o
            lse_ref[sb * QB:(sb + 1) * QB, :] = lse

    def body(*refs):
        g, n = pl.program_id(0), pl.program_id(1)
        for k, d in enumerate(DILS):
            @pl.when(g == k)
            def _():
                group_body(n % (nsteps // d) == 0, *refs[5 * k:5 * k + 5], *refs[15 + 2 * k:17 + 2 * k])

    in_specs, out_specs, operands = [], [], []
    for k in range(len(DILS)):
        at = _group_step(k, nsteps)
        cur = lambda t, at=at: pl.BlockSpec((None, R, GW), lambda g, n: (t, at(g, n), 0))
        prv = lambda t, at=at: pl.BlockSpec((None, QB, GW), lambda g, n: (t, jnp.maximum(at(g, n) * nsub - 1, 0), 0))
        in_specs += [cur(0), prv(1), cur(1), prv(2), cur(2)]
        out_specs += [pl.BlockSpec((R, GW), lambda g, n, at=at: (at(g, n), 0))] * 2
        operands += [qkvs[k].reshape(3, S, GW)] * 5
    outs = pl.pallas_call(
        body, name="attn_fwd", grid=(len(DILS), nsteps),
        out_shape=(jax.ShapeDtypeStruct((S, GW), F32),) * (2 * len(DILS)),
        in_specs=in_specs, out_specs=tuple(out_specs),
        compiler_params=_cp(("arbitrary", "arbitrary")))(*operands)
    return [(outs[2 * k].reshape(d, S // d, GW), outs[2 * k + 1].reshape(d, S // d, GW)) for k, d in enumerate(DILS)]


def _attn_combine(os_, lses):
    S = os_[0].shape[1]
    tm = min(1024, S)

    def body(o0, o1, o2, l0, l1, l2, y_ref, j0, j1, j2, scr):
        os_nat = [_undilate_load(o, (), d, scr, tm) for o, d in zip((o0, o1, o2), DILS)]
        a, b, c = [_undilate_load(l, (), d, scr, tm) for l, d in zip((l0, l1, l2), DILS)]
        m = jnp.maximum(jnp.maximum(a, b), c)
        wa, wb, wc = jnp.exp(a - m), jnp.exp(b - m), jnp.exp(c - m)
        den = wa + wb + wc
        y_ref[...] = ((wa * os_nat[0] + wb * os_nat[1] + wc * os_nat[2]) / den).astype(BF)
        lse = m + jnp.log(den)
        for j_ref, d in zip((j0, j1, j2), DILS):
            _dilate_store(lse, scr, j_ref, (), d)

    dil = lambda d: pl.BlockSpec((d, tm // d, GW), lambda i: (0, i, 0))
    dshape = lambda d: jax.ShapeDtypeStruct((d, S // d, GW), F32)
    return pl.pallas_call(
        body, name="attn_combine", grid=(S // tm,),
        out_shape=(jax.ShapeDtypeStruct((S, GW), BF),) + tuple(dshape(d) for d in DILS),
        in_specs=[dil(d) for d in DILS] * 2,
        out_specs=(pl.BlockSpec((tm, GW), lambda i: (i, 0)),) + tuple(dil(d) for d in DILS),
        scratch_shapes=[pltpu.VMEM((2, tm, 128), F32)],
        compiler_params=_cp(("parallel",)))(*os_, *lses)


def _tril(upper=False):
    row = lax.broadcasted_iota(jnp.int32, (CHUNK, CHUNK), 0)
    col = lax.broadcasted_iota(jnp.int32, (CHUNK, CHUNK), 1)
    return row <= col if upper else col <= row


def _ln_fwd(z, gain, bias):
    mu = jnp.mean(z, axis=-1, keepdims=True)
    zc = z - mu
    rstd = lax.rsqrt(jnp.mean(zc * zc, axis=-1, keepdims=True) + EPS)
    zhat = zc * rstd
    return zhat, rstd, zhat * gain + bias


def _gmlp_fwd(puz, ln_g, ln_b, w_sp, b_sp_t):
    S = puz.shape[0]
    tm = min(512, S)
    nch = tm // CHUNK

    def body(p_ref, g_ref, b_ref, w_ref, bt_ref, o_ref):
        tril = _tril()
        ws = [jnp.where(tril, w_ref[gg], 0.0).astype(BF) for gg in range(4)]
        for ch in range(nch):
            rows = slice(ch * CHUNK, (ch + 1) * CHUNK)
            z = _gelu(p_ref[rows, GMLP_W:].astype(F32))
            _, _, zn = _ln_fwd(z, g_ref[...], b_ref[...])
            zn = zn.astype(BF)
            for gg in range(4):
                cols = slice(gg * CHUNK, (gg + 1) * CHUNK)
                sz = _dot(ws[gg], zn[:, cols]) + bt_ref[:, gg:gg + 1]
                u = _gelu(p_ref[rows, cols].astype(F32))
                o_ref[rows, cols] = (u * sz).astype(BF)

    return pl.pallas_call(
        body, name="gmlp_fwd", out_shape=jax.ShapeDtypeStruct((S, GMLP_W), BF), grid=(S // tm,),
        in_specs=[pl.BlockSpec((tm, 2 * GMLP_W), lambda i: (i, 0)),
                  pl.BlockSpec((1, GMLP_W), lambda i: (0, 0)), pl.BlockSpec((1, GMLP_W), lambda i: (0, 0)),
                  pl.BlockSpec((4, CHUNK, CHUNK), lambda i: (0, 0, 0)), pl.BlockSpec((CHUNK, 4), lambda i: (0, 0))],
        out_specs=pl.BlockSpec((tm, GMLP_W), lambda i: (i, 0)),
        compiler_params=_cp(("parallel",)))(puz, ln_g, ln_b, w_sp, b_sp_t)


def _merge_fwd(ya, yg, gates, wba, wbg, wout, x, g2, g3):
    S = x.shape[0]
    tm = min(512, S)

    def body(ya_ref, yg_ref, gt_ref, wba_ref, wbg_ref, wo_ref, x_ref, g2_ref, g3_ref,
             mg_ref, y_ref, x2_ref, h3_ref):
        a = _dot(ya_ref[...], wba_ref[...])
        b = _dot(yg_ref[...], wbg_ref[...])
        merged = (gt_ref[:, :D].astype(F32) * a + gt_ref[:, D:].astype(F32) * b).astype(BF)
        mg_ref[...] = merged
        y = _dot(merged, wo_ref[...])
        y_ref[...] = y
        x2 = x_ref[...] + y * _rms_stats(y) * g2_ref[...]
        x2_ref[...] = x2
        h3_ref[...] = (x2 * _rms_stats(x2) * g3_ref[...]).astype(BF)

    row = lambda w: pl.BlockSpec((tm, w), lambda i: (i, 0))
    full = lambda s: pl.BlockSpec(s, lambda i: (0, 0))
    return pl.pallas_call(
        body, name="merge_fwd", grid=(S // tm,),
        out_shape=(jax.ShapeDtypeStruct((S, D), BF), jax.ShapeDtypeStruct((S, D), F32), jax.ShapeDtypeStruct((S, D), F32),
                   jax.ShapeDtypeStruct((S, D), BF)),
        in_specs=[row(GW), row(GMLP_W), row(2 * D), full((GW, D)), full((GMLP_W, D)), full((D, D)), row(D),
                  full((1, D)), full((1, D))],
        out_specs=(row(D), row(D), row(D), row(D)),
        compiler_params=_cp(("parallel",)))(ya, yg, gates, wba, wbg, wout, x, g2, g3)


def _resident(shape):
    return pl.BlockSpec(shape, lambda i: (0,) * len(shape), pipeline_mode=pl.Buffered(1))


def _mlp_fwd(h3, wmi, wmo, x2, tgt, g4):
    S = x2.shape[0]
    tm = min(MLP_TM_FWD, S)

    def body(h_ref, wi_ref, wo_ref, x2_ref, t_ref, g4_ref, a_ref, dy2_ref, dout_ref, loss_ref, dg4_ref):
        @pl.when(pl.program_id(0) == 0)
        def _():
            loss_ref[...] = jnp.zeros_like(loss_ref)
            dg4_ref[...] = jnp.zeros_like(dg4_ref)

        halves = [slice(hh * (tm // 2), (hh + 1) * (tm // 2)) for hh in range(2)]
        acts = []
        for rows in halves:
            a = jnp.maximum(_dot(h_ref[rows, :], wi_ref[...]), 0.0)
            a_ref[rows, :] = a.astype(BF)
            acts.append((a * a).astype(BF))
        y2s = [_dot(a2, wo_ref[...]) for a2 in acts]
        lane = lax.broadcasted_iota(jnp.int32, (1, 128), 1)
        for rows, y2 in zip(halves, y2s):
            r = _rms_stats(y2)
            out = x2_ref[rows, :] + y2 * r * g4_ref[...]
            err = out - t_ref[rows, :]
            tot = jnp.sum(jnp.sum(err * err, axis=1, keepdims=True), axis=0, keepdims=True) * (0.5 / D)
            loss_ref[...] += jnp.where(lane == 0, tot, 0.0)
            dout = err * (1.0 / D)
            dout_ref[rows, :] = dout
            dy2, dg = _rms_bwd(y2, r, g4_ref[...], dout)
            dy2_ref[rows, :] = dy2.astype(BF)
            dg4_ref[...] += dg

    row = pl.BlockSpec((tm, D), lambda i: (i, 0))
    return pl.pallas_call(
        body, name="mlp_fwd", grid=(S // tm,),
        out_shape=(jax.ShapeDtypeStruct((S, DFF), BF), jax.ShapeDtypeStruct((S, D), BF), jax.ShapeDtypeStruct((S, D), F32),
                   jax.ShapeDtypeStruct((1, 128), F32), jax.ShapeDtypeStruct((1, D), F32)),
        in_specs=[row, _resident((D, DFF)), _resident((DFF, D)), row, row, pl.BlockSpec((1, D), lambda i: (0, 0))],
        out_specs=(pl.BlockSpec((tm, DFF), lambda i: (i, 0)), row, row,
                   pl.BlockSpec((1, 128), lambda i: (0, 0)), pl.BlockSpec((1, D), lambda i: (0, 0))),
        compiler_params=_cp(("arbitrary",)))(h3, wmi, wmo, x2, tgt, g4)


def _mlp_bwd(dy2, a, wmo, wmi, x2, y, dout, g2, g3):
    S = x2.shape[0]
    tm = min(MLP_TM_BWD, S)

    def body(dy2_ref, a_ref, wo_ref, wi_ref, x2_ref, y_ref, dout_ref, g2_ref, g3_ref,
             dpre_ref, dx2_ref, dy_ref, dg3_ref, dg2_ref):
        @pl.when(pl.program_id(0) == 0)
        def _():
            dg3_ref[...] = jnp.zeros_like(dg3_ref)
            dg2_ref[...] = jnp.zeros_like(dg2_ref)

        da2 = _dot_nt(dy2_ref[...], wo_ref[...])
        dpre = (2.0 * a_ref[...].astype(F32) * da2).astype(BF)
        dpre_ref[...] = dpre
        dh3 = _dot_nt(dpre, wi_ref[...])
        x2 = x2_ref[...]
        dx3, dg3 = _rms_bwd(x2, _rms_stats(x2), g3_ref[...], dh3)
        dx2 = dout_ref[...] + dx3
        dx2_ref[...] = dx2
        dg3_ref[...] += dg3
        yv = y_ref[...]
        dy, dg2 = _rms_bwd(yv, _rms_stats(yv), g2_ref[...], dx2)
        dy_ref[...] = dy.astype(BF)
        dg2_ref[...] += dg2

    row = pl.BlockSpec((tm, D), lambda i: (i, 0))
    wide = pl.BlockSpec((tm, DFF), lambda i: (i, 0))
    vec = pl.BlockSpec((1, D), lambda i: (0, 0))
    return pl.pallas_call(
        body, name="mlp_bwd", grid=(S // tm,),
        out_shape=(jax.ShapeDtypeStruct((S, DFF), BF), jax.ShapeDtypeStruct((S, D), F32), jax.ShapeDtypeStruct((S, D), BF),
                   jax.ShapeDtypeStruct((1, D), F32), jax.ShapeDtypeStruct((1, D), F32)),
        in_specs=[row, wide, _resident((DFF, D)), _resident((D, DFF)), row, row, row, vec, vec],
        out_specs=(wide, row, row, vec, vec),
        compiler_params=_cp(("arbitrary",)))(dy2, a, wmo, wmi, x2, y, dout, g2, g3)


def _mm_tn(a, b, name, square_a=False, tk=MM_TN_TK, tie=None, col_blocks=None):
    S, M = a.shape
    N = b.shape[1]
    tk = min(tk, S)
    tm = max(t for t in range(128, min(MM_TN_TILE, M) + 1, 128) if M % t == 0)
    tn = max(t for t in range(128, min(MM_TN_TILE, N) + 1, 128) if N % t == 0)
    cb = N // col_blocks if col_blocks else tn
    while (M // tm) * (N // tn) * (S // tk) < 4 and tk % 256 == 0:
        tk //= 2
    assert M % tm == 0 and S % tk == 0 and tn % cb == 0
    nk = S // tk
    ties = () if tie is None else (tie,)

    def body(a_ref, b_ref, *rest):
        o_ref = rest[-1]
        k = pl.program_id(2)
        av = a_ref[...]
        if square_a:
            av = av * av
        part = _dot_tn(av, b_ref[...])
        if col_blocks:
            part = jnp.stack([part[:, t * cb:(t + 1) * cb] for t in range(tn // cb)])

        @pl.when(k == 0)
        def _():
            o_ref[...] = part

        @pl.when(k > 0)
        def _():
            o_ref[...] += part

    if col_blocks:
        out_shape, out_spec = (col_blocks, M, cb), pl.BlockSpec((tn // cb, tm, cb), lambda i, j, k: (j, i, 0))
    else:
        out_shape, out_spec = (M, N), pl.BlockSpec((tm, tn), lambda i, j, k: (i, j))
    return pl.pallas_call(
        body, name=name, out_shape=jax.ShapeDtypeStruct(out_shape, F32), grid=(M // tm, N // tn, nk),
        in_specs=[pl.BlockSpec((tk, tm), lambda i, j, k: (k, i)), pl.BlockSpec((tk, tn), lambda i, j, k: (k, j))]
        + [pl.BlockSpec(memory_space=pl.ANY)] * len(ties),
        out_specs=out_spec,
        compiler_params=_cp(("parallel", "parallel", "arbitrary")))(a, b, *ties)


def _outproj_bwd(dy, wout, ya, yg, gates, wba, wbg, tie):
    S = dy.shape[0]
    tm = min(512, S)

    def body(dy_ref, wo_ref, ya_ref, yg_ref, gt_ref, wba_ref, wbg_ref, tie_ref,
             dgt_ref, da_ref, db_ref, dyg_ref, e0, e1, e2, s0, s1, s2, scr):
        dm = _dot_nt(dy_ref[...], wo_ref[...])
        ga, gb = gt_ref[:, :D].astype(F32), gt_ref[:, D:].astype(F32)
        dgt_ref[:, :D] = (dm * _dot(ya_ref[...], wba_ref[...]) * ga * (1.0 - ga)).astype(BF)
        dgt_ref[:, D:] = (dm * _dot(yg_ref[...], wbg_ref[...]) * gb * (1.0 - gb)).astype(BF)
        da = (dm * ga).astype(BF)
        db = (dm * gb).astype(BF)
        da_ref[...] = da
        db_ref[...] = db
        dyg_ref[...] = _dot_nt(db, wbg_ref[...]).astype(BF)
        dya = _dot_nt(da, wba_ref[...]).astype(BF).astype(F32)
        dyy = dya * ya_ref[...].astype(F32)
        dsum = jnp.zeros((tm, GW), F32)
        for hh in range(NSLOT):
            hm = _head_mask(hh)
            dsum = dsum + jnp.where(hm, jnp.sum(jnp.where(hm, dyy, 0.0), axis=1, keepdims=True), 0.0)
        for e_ref, s_ref, d in zip((e0, e1, e2), (s0, s1, s2), DILS):
            _dilate_store(dya, scr, e_ref, (), d)
            _dilate_store(dsum, scr, s_ref, (), d)

    row = lambda w: pl.BlockSpec((tm, w), lambda i: (i, 0))
    full = lambda s: pl.BlockSpec(s, lambda i: (0, 0))
    dil = lambda d: pl.BlockSpec((d, tm // d, GW), lambda i: (0, i, 0))
    dshape = lambda d, t: jax.ShapeDtypeStruct((d, S // d, GW), t)
    return pl.pallas_call(
        body, name="outproj_bwd", grid=(S // tm,),
        out_shape=(jax.ShapeDtypeStruct((S, 2 * D), BF), jax.ShapeDtypeStruct((S, D), BF), jax.ShapeDtypeStruct((S, D), BF),
                   jax.ShapeDtypeStruct((S, GMLP_W), BF)) + tuple(dshape(d, BF) for d in DILS)
        + tuple(dshape(d, F32) for d in DILS),
        in_specs=[row(D), full((D, D)), row(GW), row(GMLP_W), row(2 * D), full((GW, D)), full((GMLP_W, D)),
                  pl.BlockSpec(memory_space=pl.ANY)],
        out_specs=(row(2 * D), row(D), row(D), row(GMLP_W)) + tuple(dil(d) for d in DILS) * 2,
        scratch_shapes=[pltpu.VMEM((2, tm, 128), F32)],
        compiler_params=_cp(("parallel",)))(dy, wout, ya, yg, gates, wba, wbg, tie)


def _gmlp_bwd(puz, dyg, ln_g, ln_b, w_sp, w_sp_t, b_sp_t):
    S = puz.shape[0]
    tm = min(512, S)
    nch = tm // CHUNK

    def body(p_ref, dy_ref, g_ref, b_ref, w_ref, wt_ref, bt_ref,
             dp_ref, dw_ref, dbs_ref, dg_ref, dbias_ref, dbacc_ref):
        i = pl.program_id(0)

        @pl.when(i == 0)
        def _():
            dw_ref[...] = jnp.zeros_like(dw_ref)
            dbacc_ref[...] = jnp.zeros_like(dbacc_ref)
            dg_ref[...] = jnp.zeros_like(dg_ref)
            dbias_ref[...] = jnp.zeros_like(dbias_ref)

        tril = _tril()
        ws = [jnp.where(tril, w_ref[gg], 0.0).astype(BF) for gg in range(4)]
        triu = _tril(upper=True)
        wts = [jnp.where(triu, wt_ref[gg], 0.0).astype(BF) for gg in range(4)]
        gain = g_ref[...]
        for ch in range(nch):
            rows = slice(ch * CHUNK, (ch + 1) * CHUNK)
            pz = p_ref[rows, GMLP_W:].astype(F32)
            z = _gelu(pz)
            zhat, rstd, zn = _ln_fwd(z, gain, b_ref[...])
            znb = zn.astype(BF)
            dzn_parts = []
            for gg in range(4):
                cols = slice(gg * CHUNK, (gg + 1) * CHUNK)
                pu = p_ref[rows, cols].astype(F32)
                u = _gelu(pu)
                sz = _dot(ws[gg], znb[:, cols]) + bt_ref[:, gg:gg + 1]
                dyv = dy_ref[rows, cols].astype(F32)
                dp_ref[rows, cols] = (dyv * sz * _gelu_grad(pu)).astype(BF)
                dsz = dyv * u
                dbacc_ref[gg] += dsz
                dszb = dsz.astype(BF)
                dw_ref[gg] += _dot_nt(dszb, znb[:, cols])
                dzn_parts.append(_dot(wts[gg], dszb))
            dzn = jnp.concatenate(dzn_parts, axis=1)
            dg_ref[...] += jnp.sum(dzn * zhat, axis=0, keepdims=True)
            dbias_ref[...] += jnp.sum(dzn, axis=0, keepdims=True)
            dzh = dzn * gain
            dz = rstd * (dzh - jnp.mean(dzh, axis=-1, keepdims=True)
                         - zhat * jnp.mean(dzh * zhat, axis=-1, keepdims=True))
            dp_ref[rows, GMLP_W:] = (dz * _gelu_grad(pz)).astype(BF)

        @pl.when(i == pl.num_programs(0) - 1)
        def _():
            for gg in range(4):
                dw_ref[gg] = jnp.where(tril, dw_ref[gg], 0.0)
                dbs_ref[gg] = jnp.sum(dbacc_ref[gg], axis=1, keepdims=True)

    full2 = lambda s: pl.BlockSpec(s, lambda i: (0, 0))
    full3 = lambda s: pl.BlockSpec(s, lambda i: (0, 0, 0))
    return pl.pallas_call(
        body, name="gmlp_bwd", grid=(S // tm,),
        out_shape=(jax.ShapeDtypeStruct((S, 2 * GMLP_W), BF), jax.ShapeDtypeStruct((4, CHUNK, CHUNK), F32),
                   jax.ShapeDtypeStruct((4, CHUNK, 1), F32), jax.ShapeDtypeStruct((1, GMLP_W), F32),
                   jax.ShapeDtypeStruct((1, GMLP_W), F32)),
        in_specs=[pl.BlockSpec((tm, 2 * GMLP_W), lambda i: (i, 0)), pl.BlockSpec((tm, GMLP_W), lambda i: (i, 0)),
                  full2((1, GMLP_W)), full2((1, GMLP_W)), full3((4, CHUNK, CHUNK)), full3((4, CHUNK, CHUNK)),
                  full2((CHUNK, 4))],
        out_specs=(pl.BlockSpec((tm, 2 * GMLP_W), lambda i: (i, 0)), full3((4, CHUNK, CHUNK)), full3((4, CHUNK, 1)),
                   full2((1, GMLP_W)), full2((1, GMLP_W))),
        scratch_shapes=[pltpu.VMEM((4, CHUNK, CHUNK), F32)],
        compiler_params=_cp(("arbitrary",)))(puz, dyg, ln_g, ln_b, w_sp, w_sp_t, b_sp_t)


def _attn_bwd(qkvs, dyas, dsums, lses):
    S = qkvs[0].shape[1] * qkvs[0].shape[2]
    nsub = min(ATTN_NSUB, S // max(DILS) // QB)
    R = nsub * QB
    nsteps = S // R
    NIN = 12

    def group_body(first, last, q_ref, qn_ref, kp_ref, kc_ref, vp_ref, vc_ref, dy_ref, dyn_ref, e_ref, en_ref,
                   l_ref, ln_ref, o_ref):
        band, band_first = _band_masks(first)
        row = lax.broadcasted_iota(jnp.int32, (QB, QB), 0)
        col = lax.broadcasted_iota(jnp.int32, (QB, QB), 1)
        mask_next = jnp.logical_and(col >= row, jnp.logical_not(last))
        kc, vc = kc_ref[...], vc_ref[...]
        kfull = jnp.concatenate([kp_ref[...], kc], axis=0)
        vfull = jnp.concatenate([vp_ref[...], vc], axis=0)
        k_last, v_last = kc[(nsub - 1) * QB:], vc[(nsub - 1) * QB:]
        q_ext = jnp.concatenate([q_ref[...], qn_ref[...]], axis=0)
        dy_ext = jnp.concatenate([dy_ref[...], dyn_ref[...]], axis=0)
        esum, esum_n, lse, lse_n = e_ref[...], en_ref[...], l_ref[...], ln_ref[...]
        win = lambda t, sb: t[sb * QB:(sb + 2) * QB]
        blk = lambda t, sb: t[sb * QB:(sb + 1) * QB]
        hms = [_head_mask(hh) for hh in range(NSLOT)]
        q_hs = [jnp.where(hm, q_ext, 0) for hm in hms]
        dy_hs = [jnp.where(hm, dy_ext, 0) for hm in hms]
        raw = []
        for hh in range(NSLOT):
            tiles = [(_dot_nt(blk(q_hs[hh], sb), win(kfull, sb)), _dot_nt(blk(dy_hs[hh], sb), win(vfull, sb)))
                     for sb in range(nsub)]
            tiles.append((_dot_nt(blk(q_hs[hh], nsub), k_last), _dot_nt(blk(dy_hs[hh], nsub), v_last)))
            raw.append(tiles)
        ps, dss = [], []
        for hh in range(NSLOT):
            rowstat = lambda t: jnp.max(jnp.where(hms[hh], t, -jnp.inf), axis=1, keepdims=True)
            p_h, ds_h = [], []
            for sb in range(nsub + 1):
                sc, dp = raw[hh][sb]
                if sb < nsub:
                    msk, lrow, erow = (band_first if sb == 0 else band), rowstat(blk(lse, sb)), rowstat(blk(esum, sb))
                else:
                    msk, lrow, erow = mask_next, rowstat(lse_n), rowstat(esum_n)
                p = jnp.where(msk, jnp.exp(sc * SCALE - lrow), 0.0)
                p_h.append(p.astype(BF))
                ds_h.append((p * (dp - erow)).astype(BF))
            ps.append(p_h)
            dss.append(ds_h)
        dq = [jnp.zeros((QB, GW), F32) for _ in range(nsub)]
        dk = [jnp.zeros((QB, GW), F32) for _ in range(nsub)]
        dv = [jnp.zeros((QB, GW), F32) for _ in range(nsub)]
        for hh in range(NSLOT):
            for sb in range(nsub):
                dq[sb] = dq[sb] + jnp.where(hms[hh], _dot(dss[hh][sb], win(kfull, sb)), 0.0)
                nxt = lambda t: t[sb + 1][:, :QB] if sb + 1 < nsub else t[nsub]
                dk[sb] = dk[sb] + _dot_tn(jnp.concatenate([dss[hh][sb][:, QB:], nxt(dss[hh])], axis=0), win(q_hs[hh], sb))
                dv[sb] = dv[sb] + _dot_tn(jnp.concatenate([ps[hh][sb][:, QB:], nxt(ps[hh])], axis=0), win(dy_hs[hh], sb))
        for sb in range(nsub):
            rows = slice(sb * QB, (sb + 1) * QB)
            o_ref[0, rows, :] = (dq[sb] * SCALE).astype(BF)
            o_ref[1, rows, :] = (dk[sb] * SCALE).astype(BF)
            o_ref[2, rows, :] = dv[sb].astype(BF)

    def body(*refs):
        g, n = pl.program_id(0), pl.program_id(1)
        for k, d in enumerate(DILS):
            @pl.when(g == k)
            def _():
                per = nsteps // d
                group_body(n % per == 0, n % per == per - 1, *refs[NIN * k:NIN * (k + 1)], refs[NIN * len(DILS) + k])

    in_specs, out_specs, operands = [], [], []
    for k in range(len(DILS)):
        at = _group_step(k, nsteps)
        prev = lambda g, n, at=at: jnp.maximum(at(g, n) * nsub - 1, 0)
        nxt = lambda g, n, at=at: jnp.minimum((at(g, n) + 1) * nsub, S // QB - 1)
        cur4 = lambda t, at=at: pl.BlockSpec((None, R, GW), lambda g, n: (t, at(g, n), 0))
        prv4 = lambda t, prev=prev: pl.BlockSpec((None, QB, GW), lambda g, n: (t, prev(g, n), 0))
        nxt4 = lambda t, nxt=nxt: pl.BlockSpec((None, QB, GW), lambda g, n: (t, nxt(g, n), 0))
        cur3 = pl.BlockSpec((R, GW), lambda g, n, at=at: (at(g, n), 0))
        nxt3 = pl.BlockSpec((QB, GW), lambda g, n, nxt=nxt: (nxt(g, n), 0))
        in_specs += [cur4(0), nxt4(0), prv4(1), cur4(1), prv4(2), cur4(2), cur3, nxt3, cur3, nxt3, cur3, nxt3]
        out_specs.append(pl.BlockSpec((3, R, GW), lambda g, n, at=at: (0, at(g, n), 0)))
        flat = lambda t: t.reshape(S, GW)
        operands += [qkvs[k].reshape(3, S, GW)] * 6 + [flat(dyas[k])] * 2 + [flat(dsums[k])] * 2 + [flat(lses[k])] * 2
    outs = pl.pallas_call(
        body, name="attn_bwd", grid=(len(DILS), nsteps),
        out_shape=(jax.ShapeDtypeStruct((3, S, GW), BF),) * len(DILS),
        in_specs=in_specs, out_specs=tuple(out_specs),
        compiler_params=_cp(("arbitrary", "arbitrary")))(*operands)
    return [t.reshape(3, d, S // d, GW) for t, d in zip(outs, DILS)]


def _inproj_bwd(dqkvs, dpuz, dgates, win_t, cos_t, sin_t, x, dx2, g1):
    S = x.shape[0]
    tm = min(512, S)

    def body(d0_ref, d1_ref, d2_ref, dp_ref, dg_ref, w_ref, c_ref, s_ref,
             x_ref, dx2_ref, g1_ref, gx_ref, dg1_ref, dn_ref, scr):
        i = pl.program_id(0)

        @pl.when(i == 0)
        def _():
            dg1_ref[...] = jnp.zeros_like(dg1_ref)

        dh = _dot(dp_ref[...], w_ref[W_UZ0:W_G0, :]) + _dot(dg_ref[...], w_ref[W_G0:, :])
        dn_ref[:, W_UZ0:W_G0] = dp_ref[...]
        dn_ref[:, W_G0:] = dg_ref[...]
        for t in range(3):
            for g, (d_ref, d) in enumerate(zip((d0_ref, d1_ref, d2_ref), DILS)):
                piece = _undilate_load(d_ref, (t,), d, scr, tm)
                if t < 2:
                    piece = _rope_bwd(piece, c_ref, s_ref)
                dn_ref[:, (3 * t + g) * GW:(3 * t + g + 1) * GW] = piece.astype(BF)
        dh = dh + _dot(dn_ref[:, :W_UZ0], w_ref[:W_UZ0, :])
        xv = x_ref[...]
        dx1, dg1 = _rms_bwd(xv, _rms_stats(xv), g1_ref[...], dh)
        gx_ref[...] = dx2_ref[...] + dx1
        dg1_ref[...] += dg1

    row = lambda w: pl.BlockSpec((tm, w), lambda i: (i, 0))
    full = lambda s: pl.BlockSpec(s, lambda i: (0, 0))
    dil = lambda d: pl.BlockSpec((3, d, tm // d, GW), lambda i: (0, 0, i, 0))
    return pl.pallas_call(
        body, name="inproj_bwd", grid=(S // tm,),
        out_shape=(jax.ShapeDtypeStruct((S, D), F32), jax.ShapeDtypeStruct((1, D), F32),
                   jax.ShapeDtypeStruct((S, win_t.shape[0]), BF)),
        in_specs=[dil(d) for d in DILS] + [row(2 * GMLP_W), row(2 * D), _resident(win_t.shape), row(128), row(128), row(D), row(D),
                                            full((1, D))],
        out_specs=(row(D), full((1, D)), row(win_t.shape[0])),
        scratch_shapes=[pltpu.VMEM((2, tm, 128), F32)],
        compiler_params=_cp(("arbitrary",)))(*dqkvs, dpuz, dgates, win_t, cos_t, sin_t, x, dx2, g1)


def _row_tile(rows, cols):
    cap = max(16, ELEMENTWISE_BLOCK_BYTES // (4 * cols))
    return max(t for t in range(16, cap + 1, 16) if rows % t == 0)


def _adam_math(w, g, m, v):
    m2 = ADAM_B1 * m + (1.0 - ADAM_B1) * g
    v2 = ADAM_B2 * v + (1.0 - ADAM_B2) * (g * g)
    m_hat = m2 / (1.0 - ADAM_B1 ** ADAM_STEP)
    v_hat = v2 / (1.0 - ADAM_B2 ** ADAM_STEP)
    delta = -ADAM_LR * (m_hat / (jnp.sqrt(v_hat) + ADAM_EPS) + ADAM_WD * w)
    return delta, m2, v2


def _row_tiles(shapes):
    steps = min(R // _row_tile(R, C) for R, C in shapes)
    assert all(R % (16 * steps) == 0 for R, _ in shapes)
    return steps, [R // steps for R, _ in shapes]


def _adam_shards(owns, recvs, ws, ms, vs, name):
    n = len(ws)
    steps, trs = _row_tiles([w.shape for w in ws])

    def body(*refs):
        for k in range(n):
            own_ref, r_ref, w_ref, m_ref, v_ref = refs[k:5 * n:n]
            g_ref, d_ref, m2_ref, v2_ref = refs[5 * n + k::n]
            g = own_ref[...] + r_ref[0].astype(F32) + r_ref[1].astype(F32) + r_ref[2].astype(F32)
            g_ref[...] = g
            d_ref[...], m2_ref[...], v2_ref[...] = _adam_math(w_ref[...], g, m_ref[...], v_ref[...])

    specs = [pl.BlockSpec((tr, w.shape[1]), lambda i: (i, 0)) for tr, w in zip(trs, ws)]
    rspecs = [pl.BlockSpec((3, tr, w.shape[1]), lambda i: (0, i, 0)) for tr, w in zip(trs, ws)]
    outs = pl.pallas_call(
        body, name=name, grid=(steps,), out_shape=tuple(jax.ShapeDtypeStruct(w.shape, F32) for w in ws) * 4,
        in_specs=specs + rspecs + specs * 3, out_specs=tuple(specs * 4),
        compiler_params=_cp(("parallel",)))(*owns, *recvs, *ws, *ms, *vs)
    return [tuple(outs[k::n]) for k in range(n)]


def _rs_add(gblocks, recvs, idx, name):
    n = len(gblocks)
    shapes = [g.shape[1:] for g in gblocks]
    steps, trs = _row_tiles(shapes)

    def body(t_ref, *refs):
        j = pl.program_id(1)
        for k in range(n):
            g_ref, r_ref, own_ref, send_ref = refs[k::n]
            s = g_ref[...] + r_ref[...]

            @pl.when(j == 0)
            def _():
                own_ref[...] = s

            @pl.when(j > 0)
            def _():
                send_ref[...] = s.astype(BF)

    blk = lambda tr, C, which: pl.BlockSpec((None, tr, C), lambda i, j, t: (which(j, t), i, 0))
    tiles = [(tr, C) for tr, (_, C) in zip(trs, shapes)]
    grid_spec = pltpu.PrefetchScalarGridSpec(
        num_scalar_prefetch=1, grid=(steps, 4),
        in_specs=[blk(tr, C, lambda j, t: t[j]) for tr, C in tiles] + [blk(tr, C, lambda j, t: t[4 + j]) for tr, C in tiles],
        out_specs=[pl.BlockSpec((tr, C), lambda i, j, t: (i, 0)) for tr, C in tiles]
        + [blk(tr, C, lambda j, t: jnp.maximum(j - 1, 0)) for tr, C in tiles])
    outs = pl.pallas_call(
        body, name=name, grid_spec=grid_spec,
        out_shape=tuple(jax.ShapeDtypeStruct(s, F32) for s in shapes)
        + tuple(jax.ShapeDtypeStruct((3,) + s, BF) for s in shapes),
        compiler_params=_cp(("parallel", "arbitrary")))(idx, *gblocks, *recvs)
    return list(outs[:n]), list(outs[n:])


def _mesh_pos():
    return lax.axis_index("x"), lax.axis_index("y"), lax.axis_index("c")


_HBM = pl.BlockSpec(memory_space=pltpu.HBM)
_SEM = pl.BlockSpec(memory_space=pltpu.SEMAPHORE)
_EFFECT = pltpu.SideEffectType.DATAFLOW_SIDE_EFFECTING
_RELATIONS = [(dx, dy, dc) for dx in (0, 1) for dy in (0, 1) for dc in (0, 1)][1:]


def _flip(v, d):
    return 1 - v if d else v


def _plan_gather(n):
    def plan(x, y, c):
        return [(k, None, 4 * x + 2 * y + c, (_flip(x, dx), _flip(y, dy), _flip(c, dc)))
                for k in range(n) for dx, dy, dc in _RELATIONS]
    return plan


def _plan_gather_near(x, y, c):
    chips = [(1 - x, y), (x, 1 - y), (1 - x, 1 - y)]
    return [(0, None, 4 * x + 2 * y + c, (x, y, 1 - c))] + [(0, None, 4 * x + 2 * y + c, (*chip, c)) for chip in chips]


def _plan_gather_pass(from_landing):
    def plan(x, y, c):
        blocks = [4 * cx + 2 * cy + c for cx, cy in ((1 - x, y), (x, 1 - y), (1 - x, 1 - y))]
        return [(0, b if from_landing else None, b, (x, y, 1 - c)) for b in blocks]
    return plan


def _plan_d2d(n):
    def plan(x, y, c):
        return [(k, 2 * kk + 1 - c, kk, (x, y, 1 - c)) for k in range(n) for kk in range(4)]
    return plan


def _plan_ici(n):
    def plan(x, y, c):
        return [(k, j, j, (_flip(x, dx), _flip(y, dy), c))
                for k in range(n) for j, (dx, dy) in enumerate(((1, 0), (0, 1), (1, 1)))]
    return plan


def _plan_copies(plan, src_refs, land_refs, send_sems, recv_sems):
    x, y, c = _mesh_pos()
    return [pltpu.make_async_remote_copy(
        src_ref=src_refs[k] if si is None else src_refs[k].at[si], dst_ref=land_refs[k].at[di],
        send_sem=send_sems.at[n], recv_sem=recv_sems.at[n], device_id=dev, device_id_type=MESH)
        for n, (k, si, di, dev) in enumerate(plan(x, y, c))]


def _exchange_start(srcs, land_shapes, plan, ncopies, name, after):
    n = len(srcs)

    def body(*refs):
        src_refs, land_refs = refs[:n], refs[n:2 * n]
        send_sems, recv_sems = refs[2 * n + len(after)], refs[2 * n + len(after) + 1]
        token = refs[-1]
        for cp in _plan_copies(plan, src_refs, land_refs, send_sems, recv_sems):
            cp.start()
        token[...] = jnp.zeros_like(token)

    lands = [pltpu.with_memory_space_constraint(lax.empty(s, a.dtype), pltpu.HBM) for s, a in zip(land_shapes, srcs)]
    srcs = [pltpu.with_memory_space_constraint(a, pltpu.HBM) for a in srcs]
    outs = pl.pallas_call(
        body, name=name,
        out_shape=(pltpu.SemaphoreType.DMA((ncopies,)), pltpu.SemaphoreType.DMA((ncopies,)))
        + tuple(pltpu.HBM(a.shape, a.dtype) for a in srcs) + tuple(pltpu.HBM(a.shape, a.dtype) for a in lands)
        + (jax.ShapeDtypeStruct((8, 128), F32),),
        in_specs=[_HBM] * (2 * n) + [pl.BlockSpec(memory_space=pl.ANY)] * len(after),
        out_specs=(_SEM, _SEM) + (_HBM,) * (2 * n) + (pl.BlockSpec(memory_space=pltpu.VMEM),),
        input_output_aliases={i: 2 + i for i in range(2 * n)},
        compiler_params=pltpu.CompilerParams(has_side_effects=_EFFECT))(*srcs, *lands, *after)
    return (outs[0], outs[1], list(outs[2:2 + n]), list(outs[2 + n:2 + 2 * n])), outs[-1]


def _exchange_forward(handle, plan, plan_fwd, needs, name, after):
    send_sems, recv_sems, srcs, lands = handle
    n, nfwd = len(srcs), len(needs)

    def body(*refs):
        src_refs, land_refs = refs[:n], refs[n:2 * n]
        outs = refs[2 * n + 2 + len(after):]
        first = _plan_copies(plan, src_refs, land_refs, refs[2 * n], refs[2 * n + 1])
        for cp, need in zip(_plan_copies(plan_fwd, land_refs, land_refs, outs[0], outs[1]), needs):
            first[need].wait_recv()
            cp.start()
        for k, cp in enumerate(first):
            cp.wait_send()
            if k not in needs:
                cp.wait_recv()

    outs = pl.pallas_call(
        body, name=name,
        out_shape=(pltpu.SemaphoreType.DMA((nfwd,)), pltpu.SemaphoreType.DMA((nfwd,)))
        + tuple(pltpu.HBM(a.shape, a.dtype) for a in srcs) + tuple(pltpu.HBM(a.shape, a.dtype) for a in lands),
        in_specs=[_HBM] * (2 * n) + [_SEM, _SEM] + [pl.BlockSpec(memory_space=pl.ANY)] * len(after),
        out_specs=(_SEM, _SEM) + (_HBM,) * (2 * n), input_output_aliases={i: 2 + i for i in range(2 * n)},
        compiler_params=pltpu.CompilerParams(has_side_effects=_EFFECT))(*srcs, *lands, send_sems, recv_sems, *after)
    return outs[0], outs[1], list(outs[2:2 + n]), list(outs[2 + n:2 + 2 * n])


def _exchange_wait(handle, plan, name, after):
    send_sems, recv_sems, srcs, lands = handle
    n = len(srcs)

    def body(*refs):
        src_refs, land_refs = refs[:n], refs[n:2 * n]
        for cp in _plan_copies(plan, src_refs, land_refs, refs[2 * n], refs[2 * n + 1]):
            cp.wait_send()
            cp.wait_recv()

    outs = pl.pallas_call(
        body, name=name,
        out_shape=tuple(pltpu.HBM(a.shape, a.dtype) for a in srcs) + tuple(pltpu.HBM(a.shape, a.dtype) for a in lands),
        in_specs=[_HBM] * (2 * n) + [_SEM, _SEM] + [pl.BlockSpec(memory_space=pl.ANY)] * len(after),
        out_specs=(_HBM,) * (2 * n), input_output_aliases={i: i for i in range(2 * n)},
        compiler_params=pltpu.CompilerParams(has_side_effects=_EFFECT))(*srcs, *lands, send_sems, recv_sems, *after)
    return list(outs[:n]), list(outs[n:])


SMALL = ("g1", "g2", "g3", "g4", "ln_g", "ln_b", "b_sp", "w_sp")


def _small_sum_adam(block, parts, lands, ws, ms, vs):
    n, ns = len(parts), len(SMALL)

    def body(blk_ref, *refs):
        p_refs, l_refs = refs[:n], refs[n:2 * n]
        w_refs, m_refs, v_refs = (refs[2 * n + i * ns:2 * n + (i + 1) * ns] for i in range(3))
        o = 2 * n + 3 * ns
        loss_ref = refs[o]
        g_out, d_out, m_out, v_out = (refs[o + 1 + i * ns:o + 1 + (i + 1) * ns] for i in range(4))
        me = blk_ref[0]
        sums = []
        for k in range(n):
            acc = jnp.where(me == 0, p_refs[k][...], l_refs[k][0])
            for b in range(1, 8):
                acc = acc + jnp.where(me == b, p_refs[k][...], l_refs[k][b])
            sums.append(acc)
        loss_ref[...] = sums[0]
        for i in range(ns):
            g_out[i][...] = sums[1 + i]
            d_out[i][...], m_out[i][...], v_out[i][...] = _adam_math(w_refs[i][...], sums[1 + i], m_refs[i][...],
                                                                     v_refs[i][...])

    args = list(parts) + list(lands) + [t[k] for t in (ws, ms, vs) for k in SMALL]
    shapes = [jax.ShapeDtypeStruct(p.shape, F32) for p in parts[:1]] + [jax.ShapeDtypeStruct(p.shape, F32) for p in parts[1:]] * 4
    vmem = pl.BlockSpec(memory_space=pltpu.VMEM)
    outs = pl.pallas_call(
        body, name="small_sum_adam", out_shape=tuple(shapes),
        in_specs=[pl.BlockSpec(memory_space=pltpu.SMEM)] + [vmem] * len(args), out_specs=(vmem,) * len(shapes),
        compiler_params=pltpu.CompilerParams(vmem_limit_bytes=VMEM_LIMIT))(block, *args)
    groups = [dict(zip(SMALL, outs[1 + i * ns:1 + (i + 1) * ns])) for i in range(4)]
    return (outs[0], *groups)


def _rope_tables(S):
    half = HD // 2
    inv_freq = jnp.tile(ROPE_THETA ** (-jnp.arange(half, dtype=F32) / half), 4)
    sign = jnp.tile(jnp.concatenate([-jnp.ones(half, F32), jnp.ones(half, F32)]), 2)
    ang = jnp.arange(S, dtype=F32)[:, None] * inv_freq[None, :]
    return jnp.cos(ang), jnp.sin(ang) * sign[None, :]


def _to_blocks(g):
    return g.reshape(8, g.shape[0] // 8, g.shape[1])


def _from_blocks(t, col_sharded):
    if col_sharded:
        return t.transpose(1, 0, 2).reshape(t.shape[1], 8 * t.shape[2])
    return t.reshape(8 * t.shape[1], t.shape[2])


class _NoComm:
    def __init__(self, late_weights):
        self._late = late_weights
        self.grads = {}

    def start_tie(self):
        return jnp.zeros((8, 128), F32)

    def late_weights(self, after):
        return self._late

    def small_start(self, loss_row, grads):
        self.small = (loss_row, grads)
        return jnp.zeros((8, 128), F32)

    def rs_start(self, key, gblocks, after=()):
        self.grads[key] = gblocks
        return jnp.zeros((8, 128), F32)

    def rs_mid(self, key, after):
        return jnp.zeros((8, 128), F32)


class _FsdpComm:
    def __init__(self, late_shards, col_sharded, after, idx, block):
        self._col, self._idx, self._block, self._rs = col_sharded, idx, block, {}
        n = len(late_shards)
        self._gather, self._token = _exchange_start(
            late_shards, [(8,) + s.shape for s in late_shards], _plan_gather(n), 7 * n, "ag_late_start", (after,))

    def start_tie(self):
        return self._token

    def late_weights(self, after):
        shards, lands = _exchange_wait(self._gather, _plan_gather(len(self._col)), "ag_late_wait", after)
        lands = [lax.dynamic_update_index_in_dim(t, s, self._block, 0) for t, s in zip(lands, shards)]
        return [_from_blocks(t, cs) for t, cs in zip(lands, self._col)]

    def small_start(self, loss_row, grads):
        parts = [loss_row] + [grads[k] for k in SMALL]
        n = len(parts)
        self._small, token = _exchange_start(parts, [(8,) + p.shape for p in parts], _plan_gather(n), 7 * n,
                                             "small_start", (self._token,))
        return token

    def small_finish(self, ws, ms, vs, after):
        n = 1 + len(SMALL)
        parts, lands = _exchange_wait(self._small, _plan_gather(n), "small_wait", after)
        return _small_sum_adam(jnp.reshape(self._block, (1,)).astype(jnp.int32), parts, lands, ws, ms, vs)

    def rs_start(self, key, gblocks, after=()):
        n = len(gblocks)
        d2d, token = _exchange_start(gblocks, [(4,) + g.shape[1:] for g in gblocks], _plan_d2d(n), 4 * n,
                                     "rs_%s_d2d_start" % key, (self._token,) + tuple(after))
        self._rs[key] = dict(n=n, d2d=d2d)
        return token

    def rs_mid(self, key, after):
        st = self._rs[key]
        gblocks, from_sib = _exchange_wait(st["d2d"], _plan_d2d(st["n"]), "rs_%s_d2d_wait" % key, after)
        st["own"], sends = _rs_add(gblocks, from_sib, self._idx, "rs_add_" + key)
        st["ici"], token = _exchange_start(sends, [t.shape for t in sends], _plan_ici(st["n"]), 3 * st["n"],
                                           "rs_%s_ici_start" % key, (self._token,))
        return token

    def rs_end(self, key, after):
        st = self._rs[key]
        return st["own"], _exchange_wait(st["ici"], _plan_ici(st["n"]), "rs_%s_ici_wait" % key, after)[1]


def _local_step(x, tgt, h1, cos_t, sin_t, win_t, comm, g1, g2, g3, g4, ln_g, ln_b, w_sp, b_sp):
    b_sp_t = b_sp.T
    w_sp_t = w_sp.transpose(0, 2, 1)

    *qkvs, puz, gates = _proj_all(h1, win_t, cos_t, sin_t, comm.start_tie())
    fwd = _attn_fwd(qkvs)
    ya, *lses = _attn_combine([o for o, _ in fwd], [l for _, l in fwd])
    yg = _gmlp_fwd(puz, ln_g, ln_b, w_sp, b_sp_t)
    wba, wbg, wout, wmi, wmo = comm.late_weights(after=(ya, yg, gates))
    merged, y, x2, h3 = _merge_fwd(ya, yg, gates, wba, wbg, wout, x, g2, g3)
    a, dy2, dout, loss_row, dg4 = _mlp_fwd(h3, wmi, wmo, x2, tgt, g4)

    dpre, dx2, dy, dg3, dg2 = _mlp_bwd(dy2, a, wmo, wmi, x2, y, dout, g2, g3)
    dwmo = _mm_tn(a, dy2, "dw_mlp_out", square_a=True, tk=MM_TN_TK // 2)
    dwmi = _mm_tn(h3, dpre, "dw_mlp_in", col_blocks=8)
    tie = comm.rs_start("mlp", [dwmi, _to_blocks(dwmo)])
    dgates, da, db, dyg, *rest = _outproj_bwd(dy, wout, ya, yg, gates, wba, wbg, tie)
    dyas, dsums = rest[:3], rest[3:]
    tie = comm.rs_mid("mlp", after=(dyg,))
    dpuz, dwsp, dbs, dlng, dlnb = _gmlp_bwd(puz, dyg, ln_g + tie[0, 0], ln_b, w_sp, w_sp_t, b_sp_t)
    dqkvs = _attn_bwd(qkvs, dyas, dsums, lses)
    grad_x, dg1, dproj = _inproj_bwd(dqkvs, dpuz, dgates, win_t, cos_t, sin_t, x, dx2, g1)
    small = dict(g1=dg1, g2=dg2, g3=dg3, g4=dg4, ln_g=dlng, ln_b=dlnb, b_sp=dbs.reshape(4, CHUNK),
                 w_sp=dwsp.reshape(4 * CHUNK, CHUNK))
    tie = comm.small_start(loss_row, small)
    dwin_t = _mm_tn(dproj, h1, "dw_in", tie=tie)
    tie = comm.rs_start("win", [_to_blocks(dwin_t)])
    dwout = _mm_tn(merged, dy, "dw_out", tie=tie)
    tie = comm.rs_mid("win", after=(dwout,))
    dwba = _mm_tn(ya, da, "dw_branch_attn", tie=tie, col_blocks=8)
    dwbg = _mm_tn(yg, db, "dw_branch_gmlp", tie=tie, col_blocks=8)
    tie = comm.rs_start("mid", [dwba, dwbg, _to_blocks(dwout)], after=(tie,))
    return grad_x, tie


def kernel(x, norm_pre_mix, w_in, w_spatial, b_spatial, ln_v_gain, ln_v_bias, w_branch_attn, w_branch_gmlp, w_out, norm_post_mix, norm_pre_mlp, w_mlp_in, w_mlp_out, norm_post_mlp, loss_target, m_norm_pre_mix, m_w_in, m_w_spatial, m_b_spatial, m_ln_v_gain, m_ln_v_bias, m_w_branch_attn, m_w_branch_gmlp, m_w_out, m_norm_post_mix, m_norm_pre_mlp, m_w_mlp_in, m_w_mlp_out, m_norm_post_mlp, v_norm_pre_mix, v_w_in, v_w_spatial, v_b_spatial, v_ln_v_gain, v_ln_v_bias, v_w_branch_attn, v_w_branch_gmlp, v_w_out, v_norm_post_mix, v_norm_pre_mlp, v_w_mlp_in, v_w_mlp_out, v_norm_post_mlp):
    mx, my, mc = _mesh_pos()
    rel = [(0, 0), (1, 0), (0, 1), (1, 1)]
    chip_of = [2 * (mx ^ dx) + (my ^ dy) for dx, dy in rel]
    idx = jnp.stack([2 * k + mc for k in chip_of] + chip_of).astype(jnp.int32)

    w_in_t, m_w_in_t, v_w_in_t = (t[0].T for t in (w_in, m_w_in, v_w_in))
    shard = w_in_t.astype(BF)
    gather, token = _exchange_start([shard], [(8,) + shard.shape], _plan_gather_near, 4, "ag_win_start", ())
    h1 = _rms_fwd(x[0], norm_pre_mix + token[0, 0], "rms_pre_mix")
    cos_t, sin_t = _rope_tables(x.shape[1])
    gather = _exchange_forward(gather, _plan_gather_near, _plan_gather_pass(True), (1, 2, 3), "ag_win_pass",
                               (h1, cos_t, sin_t))
    (shard,), (g_win,) = _exchange_wait(gather, _plan_gather_pass(False), "ag_win_wait", ())
    g_win = lax.dynamic_update_index_in_dim(g_win, shard, 4 * mx + 2 * my + mc, 0)
    win_t = _from_blocks(g_win, False)
    late = [w_branch_attn[0], w_branch_gmlp[0], w_out[0], w_mlp_in[0], w_mlp_out[0]]
    comm = _FsdpComm([w.astype(BF) for w in late], [True, True, False, True, False], g_win, idx, 4 * mx + 2 * my + mc)

    grad_x, tie = _local_step(
        x[0], loss_target[0], h1, cos_t, sin_t, win_t, comm,
        norm_pre_mix, norm_post_mix, norm_pre_mlp, norm_post_mlp, ln_v_gain, ln_v_bias, w_spatial[0], b_spatial[0])

    flat = lambda t: t.reshape(-1, t.shape[-1])
    small_w = dict(zip(SMALL, map(flat, (norm_pre_mix, norm_post_mix, norm_pre_mlp, norm_post_mlp, ln_v_gain, ln_v_bias,
                                         b_spatial, w_spatial))))
    small_m = dict(zip(SMALL, map(flat, (m_norm_pre_mix, m_norm_post_mix, m_norm_pre_mlp, m_norm_post_mlp, m_ln_v_gain,
                                         m_ln_v_bias, m_b_spatial, m_w_spatial))))
    small_v = dict(zip(SMALL, map(flat, (v_norm_pre_mix, v_norm_post_mix, v_norm_pre_mlp, v_norm_post_mlp, v_ln_v_gain,
                                         v_ln_v_bias, v_b_spatial, v_w_spatial))))
    loss_out, sg, sd, sm, sv = comm.small_finish(small_w, small_m, small_v, after=(tie,))
    loss = loss_out[0, 0]
    tie = comm.rs_mid("mid", after=(loss_out,))
    mlp_own, mlp_chips = comm.rs_end("mlp", after=(tie,))
    first = lambda ts: [t[0] for t in ts]
    upd = {
        "w_mlp_in": _adam_shards(mlp_own[:1], mlp_chips[:1], [w_mlp_in[0]], [m_w_mlp_in[0]], [v_w_mlp_in[0]],
                                 "adam_w_mlp_in")[0],
        "w_mlp_out": _adam_shards(mlp_own[1:], mlp_chips[1:], [w_mlp_out[0]], [m_w_mlp_out[0]], [v_w_mlp_out[0]],
                                  "adam_w_mlp_out")[0],
    }
    win_own, win_chips = comm.rs_end("win", after=(upd["w_mlp_in"][0], upd["w_mlp_out"][0]))
    upd["w_in"] = tuple(t.T for t in _adam_shards(win_own, win_chips, [w_in_t], [m_w_in_t], [v_w_in_t], "adam_w_in")[0])
    mid_own, mid_chips = comm.rs_end("mid", after=(upd["w_in"][0],))
    upd.update(zip(("w_branch_attn", "w_branch_gmlp", "w_out"), _adam_shards(
        mid_own, mid_chips, first((w_branch_attn, w_branch_gmlp, w_out)),
        first((m_w_branch_attn, m_w_branch_gmlp, m_w_out)), first((v_w_branch_attn, v_w_branch_gmlp, v_w_out)), "adam_mid")))
    order = ["g1", "w_in", "w_sp", "b_sp", "ln_g", "ln_b", "w_branch_attn", "w_branch_gmlp", "w_out", "g2", "g3",
             "w_mlp_in", "w_mlp_out", "g4"]
    small_shape = dict(g1=norm_pre_mix.shape, g2=norm_post_mix.shape, g3=norm_pre_mlp.shape, g4=norm_post_mlp.shape,
                       ln_g=ln_v_gain.shape, ln_b=ln_v_bias.shape, b_sp=b_spatial.shape, w_sp=w_spatial.shape)

    def pick(which):
        return [upd[nm][which][None] if nm in upd else (sg, sd, sm, sv)[which][nm].reshape(small_shape[nm])
                for nm in order]

    return (loss, grad_x[None], *pick(0), *pick(1), *pick(2), *pick(3))
```

```python
import math

import jax
import jax.numpy as jnp
from jax import lax
from jax.experimental import pallas as pl
from jax.experimental.pallas import tpu as pltpu

D = 1024
HD = 64
NSLOT = 4
GW = NSLOT * HD
DILS = (1, 4, 16)
QB = 128
ATTN_NSUB = 4
PROJ_TM = 1024
MLP_TM_FWD = 512
MLP_TM_BWD = 256
MM_TN_TK = 4096
MM_TN_TILE = 1024
ELEMENTWISE_BLOCK_BYTES = 1 << 20
AW = 3 * GW
W_UZ0, W_G0 = 3 * AW, 3 * AW + 1024
GMLP_W = 512
CHUNK = 128
DFF = 4096
EPS = 1e-6
ROPE_THETA = 10000.0
SCALE = HD ** -0.5
NEG = -1e30

ADAM_LR = 0.001
ADAM_B1 = 0.9
ADAM_B2 = 0.999
ADAM_EPS = 1e-08
ADAM_WD = 0.01
ADAM_STEP = 10

BF = jnp.bfloat16
F32 = jnp.float32
MESH = pl.DeviceIdType.MESH
VMEM_LIMIT = 56 * 1024 * 1024


def _cp(sem):
    return pltpu.CompilerParams(dimension_semantics=sem, vmem_limit_bytes=VMEM_LIMIT)


def _dot(a, b):
    return jnp.dot(a, b, preferred_element_type=F32)


def _dot_nt(a, b):
    return lax.dot_general(a, b, (((1,), (1,)), ((), ())), preferred_element_type=F32)


def _dot_tn(a, b):
    return lax.dot_general(a, b, (((0,), (0,)), ((), ())), preferred_element_type=F32)


def _gelu(x):
    return jax.nn.gelu(x, approximate=True)


def _gelu_with_grad(x):
    k = math.sqrt(2.0 / math.pi)
    t = jnp.tanh(k * (x + 0.044715 * (x * x * x)))
    cdf = 0.5 * (1.0 + t)
    return x * cdf, cdf + 0.5 * x * (1.0 - t * t) * (k * (1.0 + 3.0 * 0.044715 * x * x))


def _swap_halves(t):
    w = t.shape[1]
    lane = lax.broadcasted_iota(jnp.int32, t.shape, 1)
    first = (lane & (HD - 1)) < (HD // 2)
    return jnp.where(first, pltpu.roll(t, w - HD // 2, 1), pltpu.roll(t, HD // 2, 1))


def _head_mask(hh):
    lane = lax.broadcasted_iota(jnp.int32, (1, GW), 1)
    return jnp.logical_and(lane >= hh * HD, lane < (hh + 1) * HD)


def _rms_stats(xf):
    return lax.rsqrt(jnp.mean(xf * xf, axis=-1, keepdims=True) + EPS)


def _rms_bwd(xf, r, gain, dout):
    n = xf * r
    t = dout * gain
    dx = r * (t - n * jnp.mean(t * n, axis=-1, keepdims=True))
    return dx, jnp.sum(dout * n, axis=0, keepdims=True)


def _rms_fwd(x, gain, name):
    S = x.shape[0]
    tm = min(512, S)

    def body(x_ref, g_ref, h_ref):
        xf = x_ref[...]
        h_ref[...] = (xf * _rms_stats(xf) * g_ref[...]).astype(BF)

    return pl.pallas_call(
        body, name=name, out_shape=jax.ShapeDtypeStruct((S, D), BF), grid=(S // tm,),
        in_specs=[pl.BlockSpec((tm, D), lambda i: (i, 0)), pl.BlockSpec((1, D), lambda i: (0, 0))],
        out_specs=pl.BlockSpec((tm, D), lambda i: (i, 0)), compiler_params=_cp(("parallel",)))(x, gain)


def _dilate_store(val, scr, o_ref, lead, d):
    rows = val.shape[0]
    if d == 1:
        o_ref[lead + (0,)] = val.astype(o_ref.dtype)
        return
    for hf in range(2):
        scr[hf, pl.ds(0, rows), :] = val[:, hf * 128:(hf + 1) * 128]
    for r in range(d):
        for hf in range(2):
            o_ref[lead + (r, slice(None), slice(hf * 128, (hf + 1) * 128))] = (
                scr[hf, pl.ds(r, rows // d, stride=d), :].astype(o_ref.dtype))


def _undilate_load(i_ref, lead, d, scr, rows):
    if d == 1:
        return i_ref[lead + (0,)].astype(F32)
    for r in range(d):
        for hf in range(2):
            scr[hf, pl.ds(r, rows // d, stride=d), :] = (
                i_ref[lead + (r, slice(None), slice(hf * 128, (hf + 1) * 128))].astype(F32))
    return jnp.concatenate([scr[0, pl.ds(0, rows), :], scr[1, pl.ds(0, rows), :]], axis=1)


def _rope_fwd(y, c_ref, s_ref):
    cosv = jnp.concatenate([c_ref[...]] * 2, axis=1)
    sinv = jnp.concatenate([s_ref[...]] * 2, axis=1)
    return y * cosv + _swap_halves(y) * sinv


def _rope_bwd(dy, c_ref, s_ref):
    cosv = jnp.concatenate([c_ref[...]] * 2, axis=1)
    sinv = jnp.concatenate([s_ref[...]] * 2, axis=1)
    return dy * cosv + _swap_halves(dy * sinv)


def _proj_all(h, win_t, cos_t, sin_t, tie):
    S, K = h.shape
    tm = min(PROJ_TM, S)

    def body(h_ref, w_ref, c_ref, s_ref, tie_ref, o0_ref, o1_ref, o2_ref, p_ref, gt_ref, scr):
        hv = h_ref[...]
        col = lambda b: _dot_nt(hv, w_ref[b * GW:(b + 1) * GW, :])
        ys = [col(3 * t) for t in range(3)]
        for g, (o_ref, d) in enumerate(zip((o0_ref, o1_ref, o2_ref), DILS)):
            nxt = [col(3 * t + g + 1) if g < 2 else col(W_UZ0 // GW + t) for t in range(3)]
            for t in range(3):
                y = _rope_fwd(ys[t], c_ref, s_ref) if t < 2 else ys[t]
                _dilate_store(y, scr.at[t], o_ref, (t,), d)
            ys = nxt
        for b in range((W_G0 - W_UZ0) // GW):
            y = ys[b] if b < 3 else col(W_UZ0 // GW + b)
            p_ref[:, b * GW:(b + 1) * GW] = y.astype(BF)
        for b in range(2 * D // GW):
            gt_ref[:, b * GW:(b + 1) * GW] = jax.nn.sigmoid(col(W_G0 // GW + b)).astype(BF)

    row = lambda w: pl.BlockSpec((tm, w), lambda i: (i, 0))
    return pl.pallas_call(
        body, name="proj_all", grid=(S // tm,),
        out_shape=tuple(jax.ShapeDtypeStruct((3, d, S // d, GW), BF) for d in DILS)
        + (jax.ShapeDtypeStruct((S, W_G0 - W_UZ0), BF), jax.ShapeDtypeStruct((S, 2 * D), BF)),
        in_specs=[row(K), _resident(win_t.shape), row(128), row(128), pl.BlockSpec(memory_space=pl.ANY)],
        out_specs=tuple(pl.BlockSpec((3, d, tm // d, GW), lambda i: (0, 0, i, 0)) for d in DILS)
        + (row(W_G0 - W_UZ0), row(2 * D)),
        scratch_shapes=[pltpu.VMEM((3, 2, tm, 128), F32)],
        compiler_params=_cp(("parallel",)))(h, win_t, cos_t, sin_t, tie)


def _band_masks(first_step):
    row = lax.broadcasted_iota(jnp.int32, (QB, 2 * QB), 0)
    col = lax.broadcasted_iota(jnp.int32, (QB, 2 * QB), 1)
    band = jnp.logical_and(col >= row, col <= row + QB)
    return band, jnp.logical_and(band, jnp.logical_or(col >= QB, jnp.logical_not(first_step)))


def _group_step(k, nsteps):
    return lambda g, n: jnp.where(g == k, n, jnp.where(g < k, 0, nsteps - 1))


def _attn_fwd(qkvs):
    S = qkvs[0].shape[1] * qkvs[0].shape[2]
    nsub = min(ATTN_NSUB, S // max(DILS) // QB)
    R = nsub * QB
    nsteps = S // R

    def group_body(first, q_ref, kp_ref, kc_ref, vp_ref, vc_ref, o_ref, lse_ref):
        band, band_first = _band_masks(first)
        kfull = jnp.concatenate([kp_ref[...], kc_ref[...]], axis=0)
        vfull = jnp.concatenate([vp_ref[...], vc_ref[...]], axis=0)
        chains = [(sb, hh) for sb in range(nsub) for hh in range(NSLOT)]
        win = lambda t, sb: t[sb * QB:(sb + 2) * QB]
        scores = []
        for sb, hh in chains:
            qh = jnp.where(_head_mask(hh), q_ref[sb * QB:(sb + 1) * QB, :], 0)
            scores.append(_dot_nt(qh, win(kfull, sb)))
        soft = []
        for (sb, hh), sc in zip(chains, scores):
            sc = jnp.where(band_first if sb == 0 else band, sc * SCALE, NEG)
            m = jnp.max(sc, axis=1, keepdims=True)
            p = jnp.exp(sc - m)
            den = jnp.sum(p, axis=1, keepdims=True)
            soft.append((p.astype(BF), den, m + jnp.log(den)))
        accs = [_dot(p, win(vfull, sb)) for (sb, hh), (p, _, _) in zip(chains, soft)]
        for sb in range(nsub):
            o = jnp.zeros((QB, GW), F32)
            lse = jnp.zeros((QB, GW), F32)
            for hh in range(NSLOT):
                hm = _head_mask(hh)
                _, den, lrow = soft[sb * NSLOT + hh]
                o = o + jnp.where(hm, accs[sb * NSLOT + hh] / den, 0.0)
                lse = lse + jnp.where(hm, lrow, 0.0)
            o_ref[sb * QB:(sb + 1) * QB, :] = o
            lse_ref[sb * QB:(sb + 1) * QB, :] = lse

    def body(*refs):
        g, n = pl.program_id(0), pl.program_id(1)
        for k, d in enumerate(DILS):
            @pl.when(g == k)
            def _():
                group_body(n % (nsteps // d) == 0, *refs[5 * k:5 * k + 5], *refs[15 + 2 * k:17 + 2 * k])

    in_specs, out_specs, operands = [], [], []
    for k in range(len(DILS)):
        at = _group_step(k, nsteps)
        cur = lambda t, at=at: pl.BlockSpec((None, R, GW), lambda g, n: (t, at(g, n), 0))
        prv = lambda t, at=at: pl.BlockSpec((None, QB, GW), lambda g, n: (t, jnp.maximum(at(g, n) * nsub - 1, 0), 0))
        in_specs += [cur(0), prv(1), cur(1), prv(2), cur(2)]
        out_specs += [pl.BlockSpec((R, GW), lambda g, n, at=at: (at(g, n), 0))] * 2
        operands += [qkvs[k].reshape(3, S, GW)] * 5
    outs = pl.pallas_call(
        body, name="attn_fwd", grid=(len(DILS), nsteps),
        out_shape=(jax.ShapeDtypeStruct((S, GW), F32),) * (2 * len(DILS)),
        in_specs=in_specs, out_specs=tuple(out_specs),
        compiler_params=_cp(("arbitrary", "arbitrary")))(*operands)
    return [(outs[2 * k].reshape(d, S // d, GW), outs[2 * k + 1].reshape(d, S // d, GW)) for k, d in enumerate(DILS)]


def _attn_combine(os_, lses):
    S = os_[0].shape[1]
    tm = min(1024, S)

    def body(o0, o1, o2, l0, l1, l2, y_ref, j0, j1, j2, scr):
        os_nat = [_undilate_load(o, (), d, scr, tm) for o, d in zip((o0, o1, o2), DILS)]
        a, b, c = [_undilate_load(l, (), d, scr, tm) for l, d in zip((l0, l1, l2), DILS)]
        m = jnp.maximum(jnp.maximum(a, b), c)
        wa, wb, wc = jnp.exp(a - m), jnp.exp(b - m), jnp.exp(c - m)
        den = wa + wb + wc
        y_ref[...] = ((wa * os_nat[0] + wb * os_nat[1] + wc * os_nat[2]) / den).astype(BF)
        lse = m + jnp.log(den)
        for j_ref, d in zip((j0, j1, j2), DILS):
            _dilate_store(lse, scr, j_ref, (), d)

    dil = lambda d: pl.BlockSpec((d, tm // d, GW), lambda i: (0, i, 0))
    dshape = lambda d: jax.ShapeDtypeStruct((d, S // d, GW), F32)
    return pl.pallas_call(
        body, name="attn_combine", grid=(S // tm,),
        out_shape=(jax.ShapeDtypeStruct((S, GW), BF),) + tuple(dshape(d) for d in DILS),
        in_specs=[dil(d) for d in DILS] * 2,
        out_specs=(pl.BlockSpec((tm, GW), lambda i: (i, 0)),) + tuple(dil(d) for d in DILS),
        scratch_shapes=[pltpu.VMEM((2, tm, 128), F32)],
        compiler_params=_cp(("parallel",)))(*os_, *lses)


def _tril(upper=False):
    row = lax.broadcasted_iota(jnp.int32, (CHUNK, CHUNK), 0)
    col = lax.broadcasted_iota(jnp.int32, (CHUNK, CHUNK), 1)
    return row <= col if upper else col <= row


def _ln_fwd(z, gain, bias):
    mu = jnp.mean(z, axis=-1, keepdims=True)
    zc = z - mu
    rstd = lax.rsqrt(jnp.mean(zc * zc, axis=-1, keepdims=True) + EPS)
    zhat = zc * rstd
    return zhat, rstd, zhat * gain + bias


def _gmlp_fwd(puz, ln_g, ln_b, w_sp, b_sp_t):
    S = puz.shape[0]
    tm = min(512, S)
    nch = tm // CHUNK

    def body(p_ref, g_ref, b_ref, w_ref, bt_ref, o_ref):
        tril = _tril()
        ws = [jnp.where(tril, w_ref[gg], 0.0).astype(BF) for gg in range(4)]
        for ch in range(nch):
            rows = slice(ch * CHUNK, (ch + 1) * CHUNK)
            z = _gelu(p_ref[rows, GMLP_W:].astype(F32))
            _, _, zn = _ln_fwd(z, g_ref[...], b_ref[...])
            zn = zn.astype(BF)
            for gg in range(4):
                cols = slice(gg * CHUNK, (gg + 1) * CHUNK)
                sz = _dot(ws[gg], zn[:, cols]) + bt_ref[:, gg:gg + 1]
                u = _gelu(p_ref[rows, cols].astype(F32))
                o_ref[rows, cols] = (u * sz).astype(BF)

    return pl.pallas_call(
        body, name="gmlp_fwd", out_shape=jax.ShapeDtypeStruct((S, GMLP_W), BF), grid=(S // tm,),
        in_specs=[pl.BlockSpec((tm, 2 * GMLP_W), lambda i: (i, 0)),
                  pl.BlockSpec((1, GMLP_W), lambda i: (0, 0)), pl.BlockSpec((1, GMLP_W), lambda i: (0, 0)),
                  pl.BlockSpec((4, CHUNK, CHUNK), lambda i: (0, 0, 0)), pl.BlockSpec((CHUNK, 4), lambda i: (0, 0))],
        out_specs=pl.BlockSpec((tm, GMLP_W), lambda i: (i, 0)),
        compiler_params=_cp(("parallel",)))(puz, ln_g, ln_b, w_sp, b_sp_t)


def _merge_fwd(ya, yg, gates, wba, wbg, wout, x, g2, g3):
    S = x.shape[0]
    tm = min(512, S)

    def body(ya_ref, yg_ref, gt_ref, wba_ref, wbg_ref, wo_ref, x_ref, g2_ref, g3_ref,
             mg_ref, y_ref, x2_ref, h3_ref):
        a = _dot(ya_ref[...], wba_ref[...])
        b = _dot(yg_ref[...], wbg_ref[...])
        merged = (gt_ref[:, :D].astype(F32) * a + gt_ref[:, D:].astype(F32) * b).astype(BF)
        mg_ref[...] = merged
        y = _dot(merged, wo_ref[...])
        y_ref[...] = y
        x2 = x_ref[...] + y * _rms_stats(y) * g2_ref[...]
        x2_ref[...] = x2
        h3_ref[...] = (x2 * _rms_stats(x2) * g3_ref[...]).astype(BF)

    row = lambda w: pl.BlockSpec((tm, w), lambda i: (i, 0))
    full = lambda s: pl.BlockSpec(s, lambda i: (0, 0))
    return pl.pallas_call(
        body, name="merge_fwd", grid=(S // tm,),
        out_shape=(jax.ShapeDtypeStruct((S, D), BF), jax.ShapeDtypeStruct((S, D), F32), jax.ShapeDtypeStruct((S, D), F32),
                   jax.ShapeDtypeStruct((S, D), BF)),
        in_specs=[row(GW), row(GMLP_W), row(2 * D), full((GW, D)), full((GMLP_W, D)), full((D, D)), row(D),
                  full((1, D)), full((1, D))],
        out_specs=(row(D), row(D), row(D), row(D)),
        compiler_params=_cp(("parallel",)))(ya, yg, gates, wba, wbg, wout, x, g2, g3)


def _resident(shape):
    return pl.BlockSpec(shape, lambda i: (0,) * len(shape), pipeline_mode=pl.Buffered(1))


def _mlp_fwd(h3, wmi, wmo, x2, tgt, g4):
    S = x2.shape[0]
    tm = min(MLP_TM_FWD, S)

    def body(h_ref, wi_ref, wo_ref, x2_ref, t_ref, g4_ref, a_ref, dy2_ref, dout_ref, loss_ref, dg4_ref):
        @pl.when(pl.program_id(0) == 0)
        def _():
            loss_ref[...] = jnp.zeros_like(loss_ref)
            dg4_ref[...] = jnp.zeros_like(dg4_ref)

        halves = [slice(hh * (tm // 2), (hh + 1) * (tm // 2)) for hh in range(2)]
        acts = []
        for rows in halves:
            a = jnp.maximum(_dot(h_ref[rows, :], wi_ref[...]), 0.0)
            a_ref[rows, :] = a.astype(BF)
            acts.append((a * a).astype(BF))
        y2s = [_dot(a2, wo_ref[...]) for a2 in acts]
        lane = lax.broadcasted_iota(jnp.int32, (1, 128), 1)
        for rows, y2 in zip(halves, y2s):
            r = _rms_stats(y2)
            out = x2_ref[rows, :] + y2 * r * g4_ref[...]
            err = out - t_ref[rows, :]
            tot = jnp.sum(jnp.sum(err * err, axis=1, keepdims=True), axis=0, keepdims=True) * (0.5 / D)
            loss_ref[...] += jnp.where(lane == 0, tot, 0.0)
            dout = err * (1.0 / D)
            dout_ref[rows, :] = dout
            dy2, dg = _rms_bwd(y2, r, g4_ref[...], dout)
            dy2_ref[rows, :] = dy2.astype(BF)
            dg4_ref[...] += dg

    row = pl.BlockSpec((tm, D), lambda i: (i, 0))
    return pl.pallas_call(
        body, name="mlp_fwd", grid=(S // tm,),
        out_shape=(jax.ShapeDtypeStruct((S, DFF), BF), jax.ShapeDtypeStruct((S, D), BF), jax.ShapeDtypeStruct((S, D), F32),
                   jax.ShapeDtypeStruct((1, 128), F32), jax.ShapeDtypeStruct((1, D), F32)),
        in_specs=[row, _resident((D, DFF)), _resident((DFF, D)), row, row, pl.BlockSpec((1, D), lambda i: (0, 0))],
        out_specs=(pl.BlockSpec((tm, DFF), lambda i: (i, 0)), row, row,
                   pl.BlockSpec((1, 128), lambda i: (0, 0)), pl.BlockSpec((1, D), lambda i: (0, 0))),
        compiler_params=_cp(("arbitrary",)))(h3, wmi, wmo, x2, tgt, g4)


def _mlp_bwd(dy2, a, wmo, wmi, x2, y, dout, g2, g3):
    S = x2.shape[0]
    tm = min(MLP_TM_BWD, S)

    def body(dy2_ref, a_ref, wo_ref, wi_ref, x2_ref, y_ref, dout_ref, g2_ref, g3_ref,
             dpre_ref, dx2_ref, dy_ref, dg3_ref, dg2_ref):
        @pl.when(pl.program_id(0) == 0)
        def _():
            dg3_ref[...] = jnp.zeros_like(dg3_ref)
            dg2_ref[...] = jnp.zeros_like(dg2_ref)

        da2 = _dot_nt(dy2_ref[...], wo_ref[...])
        dpre = (2.0 * a_ref[...].astype(F32) * da2).astype(BF)
        dpre_ref[...] = dpre
        dh3 = _dot_nt(dpre, wi_ref[...])
        x2 = x2_ref[...]
        dx3, dg3 = _rms_bwd(x2, _rms_stats(x2), g3_ref[...], dh3)
        dx2 = dout_ref[...] + dx3
        dx2_ref[...] = dx2
        dg3_ref[...] += dg3
        yv = y_ref[...]
        dy, dg2 = _rms_bwd(yv, _rms_stats(yv), g2_ref[...], dx2)
        dy_ref[...] = dy.astype(BF)
        dg2_ref[...] += dg2

    row = pl.BlockSpec((tm, D), lambda i: (i, 0))
    wide = pl.BlockSpec((tm, DFF), lambda i: (i, 0))
    vec = pl.BlockSpec((1, D), lambda i: (0, 0))
    return pl.pallas_call(
        body, name="mlp_bwd", grid=(S // tm,),
        out_shape=(jax.ShapeDtypeStruct((S, DFF), BF), jax.ShapeDtypeStruct((S, D), F32), jax.ShapeDtypeStruct((S, D), BF),
                   jax.ShapeDtypeStruct((1, D), F32), jax.ShapeDtypeStruct((1, D), F32)),
        in_specs=[row, wide, _resident((DFF, D)), _resident((D, DFF)), row, row, row, vec, vec],
        out_specs=(wide, row, row, vec, vec),
        compiler_params=_cp(("arbitrary",)))(dy2, a, wmo, wmi, x2, y, dout, g2, g3)


def _mm_tn(a, b, name, square_a=False, tk=MM_TN_TK, tie=None, col_blocks=None):
    S, M = a.shape
    N = b.shape[1]
    tk = min(tk, S)
    tm = max(t for t in range(128, min(MM_TN_TILE, M) + 1, 128) if M % t == 0)
    tn = max(t for t in range(128, min(MM_TN_TILE, N) + 1, 128) if N % t == 0)
    cb = N // col_blocks if col_blocks else tn
    while (M // tm) * (N // tn) * (S // tk) < 4 and tk % 256 == 0:
        tk //= 2
    assert M % tm == 0 and S % tk == 0 and tn % cb == 0
    nk = S // tk
    ties = () if tie is None else (tie,)

    def body(a_ref, b_ref, *rest):
        o_ref = rest[-1]
        k = pl.program_id(2)
        av = a_ref[...]
        if square_a:
            av = av * av
        part = _dot_tn(av, b_ref[...])
        if col_blocks:
            part = jnp.stack([part[:, t * cb:(t + 1) * cb] for t in range(tn // cb)])

        @pl.when(k == 0)
        def _():
            o_ref[...] = part

        @pl.when(k > 0)
        def _():
            o_ref[...] += part

    if col_blocks:
        out_shape, out_spec = (col_blocks, M, cb), pl.BlockSpec((tn // cb, tm, cb), lambda i, j, k: (j, i, 0))
    else:
        out_shape, out_spec = (M, N), pl.BlockSpec((tm, tn), lambda i, j, k: (i, j))
    return pl.pallas_call(
        body, name=name, out_shape=jax.ShapeDtypeStruct(out_shape, F32), grid=(M // tm, N // tn, nk),
        in_specs=[pl.BlockSpec((tk, tm), lambda i, j, k: (k, i)), pl.BlockSpec((tk, tn), lambda i, j, k: (k, j))]
        + [pl.BlockSpec(memory_space=pl.ANY)] * len(ties),
        out_specs=out_spec,
        compiler_params=_cp(("parallel", "parallel", "arbitrary")))(a, b, *ties)


def _outproj_bwd(dy, wout, ya, yg, gates, wba, wbg, tie):
    S = dy.shape[0]
    tm = min(512, S)

    def body(dy_ref, wo_ref, ya_ref, yg_ref, gt_ref, wba_ref, wbg_ref, tie_ref,
             dgt_ref, da_ref, db_ref, dyg_ref, e0, e1, e2, s0, s1, s2, scr):
        dm = _dot_nt(dy_ref[...], wo_ref[...])
        ga, gb = gt_ref[:, :D].astype(F32), gt_ref[:, D:].astype(F32)
        dgt_ref[:, :D] = (dm * _dot(ya_ref[...], wba_ref[...]) * ga * (1.0 - ga)).astype(BF)
        dgt_ref[:, D:] = (dm * _dot(yg_ref[...], wbg_ref[...]) * gb * (1.0 - gb)).astype(BF)
        da = (dm * ga).astype(BF)
        db = (dm * gb).astype(BF)
        da_ref[...] = da
        db_ref[...] = db
        dyg_ref[...] = _dot_nt(db, wbg_ref[...]).astype(BF)
        dya = _dot_nt(da, wba_ref[...]).astype(BF).astype(F32)
        dyy = dya * ya_ref[...].astype(F32)
        dsum = jnp.zeros((tm, GW), F32)
        for hh in range(NSLOT):
            hm = _head_mask(hh)
            dsum = dsum + jnp.where(hm, jnp.sum(jnp.where(hm, dyy, 0.0), axis=1, keepdims=True), 0.0)
        for e_ref, s_ref, d in zip((e0, e1, e2), (s0, s1, s2), DILS):
            _dilate_store(dya, scr, e_ref, (), d)
            _dilate_store(dsum, scr, s_ref, (), d)

    row = lambda w: pl.BlockSpec((tm, w), lambda i: (i, 0))
    full = lambda s: pl.BlockSpec(s, lambda i: (0, 0))
    dil = lambda d: pl.BlockSpec((d, tm // d, GW), lambda i: (0, i, 0))
    dshape = lambda d, t: jax.ShapeDtypeStruct((d, S // d, GW), t)
    return pl.pallas_call(
        body, name="outproj_bwd", grid=(S // tm,),
        out_shape=(jax.ShapeDtypeStruct((S, 2 * D), BF), jax.ShapeDtypeStruct((S, D), BF), jax.ShapeDtypeStruct((S, D), BF),
                   jax.ShapeDtypeStruct((S, GMLP_W), BF)) + tuple(dshape(d, BF) for d in DILS)
        + tuple(dshape(d, F32) for d in DILS),
        in_specs=[row(D), full((D, D)), row(GW), row(GMLP_W), row(2 * D), full((GW, D)), full((GMLP_W, D)),
                  pl.BlockSpec(memory_space=pl.ANY)],
        out_specs=(row(2 * D), row(D), row(D), row(GMLP_W)) + tuple(dil(d) for d in DILS) * 2,
        scratch_shapes=[pltpu.VMEM((2, tm, 128), F32)],
        compiler_params=_cp(("parallel",)))(dy, wout, ya, yg, gates, wba, wbg, tie)


def _gmlp_bwd(puz, dyg, ln_g, ln_b, w_sp, w_sp_t, b_sp_t):
    S = puz.shape[0]
    tm = min(512, S)
    nch = tm // CHUNK

    def body(p_ref, dy_ref, g_ref, b_ref, w_ref, wt_ref, bt_ref,
             dp_ref, dw_ref, dbs_ref, dg_ref, dbias_ref, dbacc_ref):
        i = pl.program_id(0)

        @pl.when(i == 0)
        def _():
            dw_ref[...] = jnp.zeros_like(dw_ref)
            dbacc_ref[...] = jnp.zeros_like(dbacc_ref)
            dg_ref[...] = jnp.zeros_like(dg_ref)
            dbias_ref[...] = jnp.zeros_like(dbias_ref)

        tril = _tril()
        ws = [jnp.where(tril, w_ref[gg], 0.0).astype(BF) for gg in range(4)]
        triu = _tril(upper=True)
        wts = [jnp.where(triu, wt_ref[gg], 0.0).astype(BF) for gg in range(4)]
        gain = g_ref[...]
        for ch in range(nch):
            rows = slice(ch * CHUNK, (ch + 1) * CHUNK)
            pz = p_ref[rows, GMLP_W:].astype(F32)
            z, z_grad = _gelu_with_grad(pz)
            zhat, rstd, zn = _ln_fwd(z, gain, b_ref[...])
            znb = zn.astype(BF)
            dzn_parts = []
            for gg in range(4):
                cols = slice(gg * CHUNK, (gg + 1) * CHUNK)
                pu = p_ref[rows, cols].astype(F32)
                u, u_grad = _gelu_with_grad(pu)
                sz = _dot(ws[gg], znb[:, cols]) + bt_ref[:, gg:gg + 1]
                dyv = dy_ref[rows, cols].astype(F32)
                dp_ref[rows, cols] = (dyv * sz * u_grad).astype(BF)
                dsz = dyv * u
                dbacc_ref[gg] += dsz
                dszb = dsz.astype(BF)
                dw_ref[gg] += _dot_nt(dszb, znb[:, cols])
                dzn_parts.append(_dot(wts[gg], dszb))
            dzn = jnp.concatenate(dzn_parts, axis=1)
            dg_ref[...] += jnp.sum(dzn * zhat, axis=0, keepdims=True)
            dbias_ref[...] += jnp.sum(dzn, axis=0, keepdims=True)
            dzh = dzn * gain
            dz = rstd * (dzh - jnp.mean(dzh, axis=-1, keepdims=True)
                         - zhat * jnp.mean(dzh * zhat, axis=-1, keepdims=True))
            dp_ref[rows, GMLP_W:] = (dz * z_grad).astype(BF)

        @pl.when(i == pl.num_programs(0) - 1)
        def _():
            for gg in range(4):
                dw_ref[gg] = jnp.where(tril, dw_ref[gg], 0.0)
                dbs_ref[gg] = jnp.sum(dbacc_ref[gg], axis=1, keepdims=True)

    full2 = lambda s: pl.BlockSpec(s, lambda i: (0, 0))
    full3 = lambda s: pl.BlockSpec(s, lambda i: (0, 0, 0))
    return pl.pallas_call(
        body, name="gmlp_bwd", grid=(S // tm,),
        out_shape=(jax.ShapeDtypeStruct((S, 2 * GMLP_W), BF), jax.ShapeDtypeStruct((4, CHUNK, CHUNK), F32),
                   jax.ShapeDtypeStruct((4, CHUNK, 1), F32), jax.ShapeDtypeStruct((1, GMLP_W), F32),
                   jax.ShapeDtypeStruct((1, GMLP_W), F32)),
        in_specs=[pl.BlockSpec((tm, 2 * GMLP_W), lambda i: (i, 0)), pl.BlockSpec((tm, GMLP_W), lambda i: (i, 0)),
                  full2((1, GMLP_W)), full2((1, GMLP_W)), full3((4, CHUNK, CHUNK)), full3((4, CHUNK, CHUNK)),
                  full2((CHUNK, 4))],
        out_specs=(pl.BlockSpec((tm, 2 * GMLP_W), lambda i: (i, 0)), full3((4, CHUNK, CHUNK)), full3((4, CHUNK, 1)),
                   full2((1, GMLP_W)), full2((1, GMLP_W))),
        scratch_shapes=[pltpu.VMEM((4, CHUNK, CHUNK), F32)],
        compiler_params=_cp(("arbitrary",)))(puz, dyg, ln_g, ln_b, w_sp, w_sp_t, b_sp_t)


def _attn_bwd(qkvs, dyas, dsums, lses):
    S = qkvs[0].shape[1] * qkvs[0].shape[2]
    nsub = min(ATTN_NSUB, S // max(DILS) // QB)
    R = nsub * QB
    nsteps = S // R
    NIN = 12

    def group_body(first, last, q_ref, qn_ref, kp_ref, kc_ref, vp_ref, vc_ref, dy_ref, dyn_ref, e_ref, en_ref,
                   l_ref, ln_ref, o_ref):
        band, band_first = _band_masks(first)
        row = lax.broadcasted_iota(jnp.int32, (QB, QB), 0)
        col = lax.broadcasted_iota(jnp.int32, (QB, QB), 1)
        mask_next = jnp.logical_and(col >= row, jnp.logical_not(last))
        kc, vc = kc_ref[...], vc_ref[...]
        kfull = jnp.concatenate([kp_ref[...], kc], axis=0)
        vfull = jnp.concatenate([vp_ref[...], vc], axis=0)
        k_last, v_last = kc[(nsub - 1) * QB:], vc[(nsub - 1) * QB:]
        q_ext = jnp.concatenate([q_ref[...], qn_ref[...]], axis=0)
        dy_ext = jnp.concatenate([dy_ref[...], dyn_ref[...]], axis=0)
        esum, esum_n, lse, lse_n = e_ref[...], en_ref[...], l_ref[...], ln_ref[...]
        win = lambda t, sb: t[sb * QB:(sb + 2) * QB]
        blk = lambda t, sb: t[sb * QB:(sb + 1) * QB]
        hms = [_head_mask(hh) for hh in range(NSLOT)]
        q_hs = [jnp.where(hm, q_ext, 0) for hm in hms]
        dy_hs = [jnp.where(hm, dy_ext, 0) for hm in hms]
        raw = []
        for hh in range(NSLOT):
            tiles = [(_dot_nt(blk(q_hs[hh], sb), win(kfull, sb)), _dot_nt(blk(dy_hs[hh], sb), win(vfull, sb)))
                     for sb in range(nsub)]
            tiles.append((_dot_nt(blk(q_hs[hh], nsub), k_last), _dot_nt(blk(dy_hs[hh], nsub), v_last)))
            raw.append(tiles)
        ps, dss = [], []
        for hh in range(NSLOT):
            rowstat = lambda t: jnp.max(jnp.where(hms[hh], t, -jnp.inf), axis=1, keepdims=True)
            p_h, ds_h = [], []
            for sb in range(nsub + 1):
                sc, dp = raw[hh][sb]
                if sb < nsub:
                    msk, lrow, erow = (band_first if sb == 0 else band), rowstat(blk(lse, sb)), rowstat(blk(esum, sb))
                else:
                    msk, lrow, erow = mask_next, rowstat(lse_n), rowstat(esum_n)
                p = jnp.where(msk, jnp.exp(sc * SCALE - lrow), 0.0)
                p_h.append(p.astype(BF))
                ds_h.append((p * (dp - erow)).astype(BF))
            ps.append(p_h)
            dss.append(ds_h)
        dq = [jnp.zeros((QB, GW), F32) for _ in range(nsub)]
        dk = [jnp.zeros((QB, GW), F32) for _ in range(nsub)]
        dv = [jnp.zeros((QB, GW), F32) for _ in range(nsub)]
        for hh in range(NSLOT):
            for sb in range(nsub):
                dq[sb] = dq[sb] + jnp.where(hms[hh], _dot(dss[hh][sb], win(kfull, sb)), 0.0)
                nxt = lambda t: t[sb + 1][:, :QB] if sb + 1 < nsub else t[nsub]
                dk[sb] = dk[sb] + _dot_tn(jnp.concatenate([dss[hh][sb][:, QB:], nxt(dss[hh])], axis=0), win(q_hs[hh], sb))
                dv[sb] = dv[sb] + _dot_tn(jnp.concatenate([ps[hh][sb][:, QB:], nxt(ps[hh])], axis=0), win(dy_hs[hh], sb))
        for sb in range(nsub):
            rows = slice(sb * QB, (sb + 1) * QB)
            o_ref[0, rows, :] = (dq[sb] * SCALE).astype(BF)
            o_ref[1, rows, :] = (dk[sb] * SCALE).astype(BF)
            o_ref[2, rows, :] = dv[sb].astype(BF)

    def body(*refs):
        g, n = pl.program_id(0), pl.program_id(1)
        for k, d in enumerate(DILS):
            @pl.when(g == k)
            def _():
                per = nsteps // d
                group_body(n % per == 0, n % per == per - 1, *refs[NIN * k:NIN * (k + 1)], refs[NIN * len(DILS) + k])

    in_specs, out_specs, operands = [], [], []
    for k in range(len(DILS)):
        at = _group_step(k, nsteps)
        prev = lambda g, n, at=at: jnp.maximum(at(g, n) * nsub - 1, 0)
        nxt = lambda g, n, at=at: jnp.minimum((at(g, n) + 1) * nsub, S // QB - 1)
        cur4 = lambda t, at=at: pl.BlockSpec((None, R, GW), lambda g, n: (t, at(g, n), 0))
        prv4 = lambda t, prev=prev: pl.BlockSpec((None, QB, GW), lambda g, n: (t, prev(g, n), 0))
        nxt4 = lambda t, nxt=nxt: pl.BlockSpec((None, QB, GW), lambda g, n: (t, nxt(g, n), 0))
        cur3 = pl.BlockSpec((R, GW), lambda g, n, at=at: (at(g, n), 0))
        nxt3 = pl.BlockSpec((QB, GW), lambda g, n, nxt=nxt: (nxt(g, n), 0))
        in_specs += [cur4(0), nxt4(0), prv4(1), cur4(1), prv4(2), cur4(2), cur3, nxt3, cur3, nxt3, cur3, nxt3]
        out_specs.append(pl.BlockSpec((3, R, GW), lambda g, n, at=at: (0, at(g, n), 0)))
        flat = lambda t: t.reshape(S, GW)
        operands += [qkvs[k].reshape(3, S, GW)] * 6 + [flat(dyas[k])] * 2 + [flat(dsums[k])] * 2 + [flat(lses[k])] * 2
    outs = pl.pallas_call(
        body, name="attn_bwd", grid=(len(DILS), nsteps),
        out_shape=(jax.ShapeDtypeStruct((3, S, GW), BF),) * len(DILS),
        in_specs=in_specs, out_specs=tuple(out_specs),
        compiler_params=_cp(("arbitrary", "arbitrary")))(*operands)
    return [t.reshape(3, d, S // d, GW) for t, d in zip(outs, DILS)]


def _inproj_bwd(dqkvs, dpuz, dgates, win_t, cos_t, sin_t, x, dx2, g1):
    S = x.shape[0]
    tm = min(512, S)

    def body(d0_ref, d1_ref, d2_ref, dp_ref, dg_ref, w_ref, c_ref, s_ref,
             x_ref, dx2_ref, g1_ref, gx_ref, dg1_ref, dn_ref, scr):
        i = pl.program_id(0)

        @pl.when(i == 0)
        def _():
            dg1_ref[...] = jnp.zeros_like(dg1_ref)

        dh = _dot(dp_ref[...], w_ref[W_UZ0:W_G0, :]) + _dot(dg_ref[...], w_ref[W_G0:, :])
        dn_ref[:, W_UZ0:W_G0] = dp_ref[...]
        dn_ref[:, W_G0:] = dg_ref[...]
        for t in range(3):
            for g, (d_ref, d) in enumerate(zip((d0_ref, d1_ref, d2_ref), DILS)):
                piece = _undilate_load(d_ref, (t,), d, scr, tm)
                if t < 2:
                    piece = _rope_bwd(piece, c_ref, s_ref)
                dn_ref[:, (3 * t + g) * GW:(3 * t + g + 1) * GW] = piece.astype(BF)
        dh = dh + _dot(dn_ref[:, :W_UZ0], w_ref[:W_UZ0, :])
        xv = x_ref[...]
        dx1, dg1 = _rms_bwd(xv, _rms_stats(xv), g1_ref[...], dh)
        gx_ref[...] = dx2_ref[...] + dx1
        dg1_ref[...] += dg1

    row = lambda w: pl.BlockSpec((tm, w), lambda i: (i, 0))
    full = lambda s: pl.BlockSpec(s, lambda i: (0, 0))
    dil = lambda d: pl.BlockSpec((3, d, tm // d, GW), lambda i: (0, 0, i, 0))
    return pl.pallas_call(
        body, name="inproj_bwd", grid=(S // tm,),
        out_shape=(jax.ShapeDtypeStruct((S, D), F32), jax.ShapeDtypeStruct((1, D), F32),
                   jax.ShapeDtypeStruct((S, win_t.shape[0]), BF)),
        in_specs=[dil(d) for d in DILS] + [row(2 * GMLP_W), row(2 * D), _resident(win_t.shape), row(128), row(128), row(D), row(D),
                                            full((1, D))],
        out_specs=(row(D), full((1, D)), row(win_t.shape[0])),
        scratch_shapes=[pltpu.VMEM((2, tm, 128), F32)],
        compiler_params=_cp(("arbitrary",)))(*dqkvs, dpuz, dgates, win_t, cos_t, sin_t, x, dx2, g1)


def _row_tile(rows, cols):
    cap = max(16, ELEMENTWISE_BLOCK_BYTES // (4 * cols))
    return max(t for t in range(16, cap + 1, 16) if rows % t == 0)


def _adam_math(w, g, m, v):
    m2 = ADAM_B1 * m + (1.0 - ADAM_B1) * g
    v2 = ADAM_B2 * v + (1.0 - ADAM_B2) * (g * g)
    m_hat = m2 / (1.0 - ADAM_B1 ** ADAM_STEP)
    v_hat = v2 / (1.0 - ADAM_B2 ** ADAM_STEP)
    delta = -ADAM_LR * (m_hat / (jnp.sqrt(v_hat) + ADAM_EPS) + ADAM_WD * w)
    return delta, m2, v2


def _row_tiles(shapes):
    steps = min(R // _row_tile(R, C) for R, C in shapes)
    assert all(R % (16 * steps) == 0 for R, _ in shapes)
    return steps, [R // steps for R, _ in shapes]


def _adam_shards(owns, recvs, ws, ms, vs, name):
    n = len(ws)
    steps, trs = _row_tiles([w.shape for w in ws])

    def body(*refs):
        for k in range(n):
            own_ref, r_ref, w_ref, m_ref, v_ref = refs[k:5 * n:n]
            g_ref, d_ref, m2_ref, v2_ref = refs[5 * n + k::n]
            g = own_ref[...] + r_ref[0].astype(F32) + r_ref[1].astype(F32) + r_ref[2].astype(F32)
            g_ref[...] = g
            d_ref[...], m2_ref[...], v2_ref[...] = _adam_math(w_ref[...], g, m_ref[...], v_ref[...])

    specs = [pl.BlockSpec((tr, w.shape[1]), lambda i: (i, 0)) for tr, w in zip(trs, ws)]
    rspecs = [pl.BlockSpec((3, tr, w.shape[1]), lambda i: (0, i, 0)) for tr, w in zip(trs, ws)]
    outs = pl.pallas_call(
        body, name=name, grid=(steps,), out_shape=tuple(jax.ShapeDtypeStruct(w.shape, F32) for w in ws) * 4,
        in_specs=specs + rspecs + specs * 3, out_specs=tuple(specs * 4),
        compiler_params=_cp(("parallel",)))(*owns, *recvs, *ws, *ms, *vs)
    return [tuple(outs[k::n]) for k in range(n)]


def _rs_add(gblocks, recvs, idx, name):
    n = len(gblocks)
    shapes = [g.shape[1:] for g in gblocks]
    steps, trs = _row_tiles(shapes)

    def body(t_ref, *refs):
        j = pl.program_id(1)
        for k in range(n):
            g_ref, r_ref, own_ref, send_ref = refs[k::n]
            s = g_ref[...] + r_ref[...]

            @pl.when(j == 0)
            def _():
                own_ref[...] = s

            @pl.when(j > 0)
            def _():
                send_ref[...] = s.astype(BF)

    blk = lambda tr, C, which: pl.BlockSpec((None, tr, C), lambda i, j, t: (which(j, t), i, 0))
    tiles = [(tr, C) for tr, (_, C) in zip(trs, shapes)]
    grid_spec = pltpu.PrefetchScalarGridSpec(
        num_scalar_prefetch=1, grid=(steps, 4),
        in_specs=[blk(tr, C, lambda j, t: t[j]) for tr, C in tiles] + [blk(tr, C, lambda j, t: t[4 + j]) for tr, C in tiles],
        out_specs=[pl.BlockSpec((tr, C), lambda i, j, t: (i, 0)) for tr, C in tiles]
        + [blk(tr, C, lambda j, t: jnp.maximum(j - 1, 0)) for tr, C in tiles])
    outs = pl.pallas_call(
        body, name=name, grid_spec=grid_spec,
        out_shape=tuple(jax.ShapeDtypeStruct(s, F32) for s in shapes)
        + tuple(jax.ShapeDtypeStruct((3,) + s, BF) for s in shapes),
        compiler_params=_cp(("parallel", "arbitrary")))(idx, *gblocks, *recvs)
    return list(outs[:n]), list(outs[n:])


def _mesh_pos():
    return lax.axis_index("x"), lax.axis_index("y"), lax.axis_index("c")


_HBM = pl.BlockSpec(memory_space=pltpu.HBM)
_SEM = pl.BlockSpec(memory_space=pltpu.SEMAPHORE)
_EFFECT = pltpu.SideEffectType.DATAFLOW_SIDE_EFFECTING
_RELATIONS = [(dx, dy, dc) for dx in (0, 1) for dy in (0, 1) for dc in (0, 1)][1:]


def _flip(v, d):
    return 1 - v if d else v


def _plan_gather(n):
    def plan(x, y, c):
        return [(k, None, 4 * x + 2 * y + c, (_flip(x, dx), _flip(y, dy), _flip(c, dc)))
                for k in range(n) for dx, dy, dc in _RELATIONS]
    return plan


def _plan_gather_near(x, y, c):
    chips = [(1 - x, y), (x, 1 - y), (1 - x, 1 - y)]
    return [(0, None, 4 * x + 2 * y + c, (x, y, 1 - c))] + [(0, None, 4 * x + 2 * y + c, (*chip, c)) for chip in chips]


def _plan_gather_pass(from_landing):
    def plan(x, y, c):
        blocks = [4 * cx + 2 * cy + c for cx, cy in ((1 - x, y), (x, 1 - y), (1 - x, 1 - y))]
        return [(0, b if from_landing else None, b, (x, y, 1 - c)) for b in blocks]
    return plan


def _plan_d2d(n):
    def plan(x, y, c):
        return [(k, 2 * kk + 1 - c, kk, (x, y, 1 - c)) for k in range(n) for kk in range(4)]
    return plan


def _plan_ici(n):
    def plan(x, y, c):
        return [(k, j, j, (_flip(x, dx), _flip(y, dy), c))
                for k in range(n) for j, (dx, dy) in enumerate(((1, 0), (0, 1), (1, 1)))]
    return plan


def _plan_copies(plan, src_refs, land_refs, send_sems, recv_sems):
    x, y, c = _mesh_pos()
    return [pltpu.make_async_remote_copy(
        src_ref=src_refs[k] if si is None else src_refs[k].at[si], dst_ref=land_refs[k].at[di],
        send_sem=send_sems.at[n], recv_sem=recv_sems.at[n], device_id=dev, device_id_type=MESH)
        for n, (k, si, di, dev) in enumerate(plan(x, y, c))]


def _exchange_start(srcs, land_shapes, plan, ncopies, name, after):
    n = len(srcs)

    def body(*refs):
        src_refs, land_refs = refs[:n], refs[n:2 * n]
        send_sems, recv_sems = refs[2 * n + len(after)], refs[2 * n + len(after) + 1]
        token = refs[-1]
        for cp in _plan_copies(plan, src_refs, land_refs, send_sems, recv_sems):
            cp.start()
        token[...] = jnp.zeros_like(token)

    lands = [pltpu.with_memory_space_constraint(lax.empty(s, a.dtype), pltpu.HBM) for s, a in zip(land_shapes, srcs)]
    srcs = [pltpu.with_memory_space_constraint(a, pltpu.HBM) for a in srcs]
    outs = pl.pallas_call(
        body, name=name,
        out_shape=(pltpu.SemaphoreType.DMA((ncopies,)), pltpu.SemaphoreType.DMA((ncopies,)))
        + tuple(pltpu.HBM(a.shape, a.dtype) for a in srcs) + tuple(pltpu.HBM(a.shape, a.dtype) for a in lands)
        + (jax.ShapeDtypeStruct((8, 128), F32),),
        in_specs=[_HBM] * (2 * n) + [pl.BlockSpec(memory_space=pl.ANY)] * len(after),
        out_specs=(_SEM, _SEM) + (_HBM,) * (2 * n) + (pl.BlockSpec(memory_space=pltpu.VMEM),),
        input_output_aliases={i: 2 + i for i in range(2 * n)},
        compiler_params=pltpu.CompilerParams(has_side_effects=_EFFECT))(*srcs, *lands, *after)
    return (outs[0], outs[1], list(outs[2:2 + n]), list(outs[2 + n:2 + 2 * n])), outs[-1]


def _exchange_forward(handle, plan, plan_fwd, needs, name, after):
    send_sems, recv_sems, srcs, lands = handle
    n, nfwd = len(srcs), len(needs)

    def body(*refs):
        src_refs, land_refs = refs[:n], refs[n:2 * n]
        outs = refs[2 * n + 2 + len(after):]
        first = _plan_copies(plan, src_refs, land_refs, refs[2 * n], refs[2 * n + 1])
        for cp, need in zip(_plan_copies(plan_fwd, land_refs, land_refs, outs[0], outs[1]), needs):
            first[need].wait_recv()
            cp.start()
        for k, cp in enumerate(first):
            cp.wait_send()
            if k not in needs:
                cp.wait_recv()

    outs = pl.pallas_call(
        body, name=name,
        out_shape=(pltpu.SemaphoreType.DMA((nfwd,)), pltpu.SemaphoreType.DMA((nfwd,)))
        + tuple(pltpu.HBM(a.shape, a.dtype) for a in srcs) + tuple(pltpu.HBM(a.shape, a.dtype) for a in lands),
        in_specs=[_HBM] * (2 * n) + [_SEM, _SEM] + [pl.BlockSpec(memory_space=pl.ANY)] * len(after),
        out_specs=(_SEM, _SEM) + (_HBM,) * (2 * n), input_output_aliases={i: 2 + i for i in range(2 * n)},
        compiler_params=pltpu.CompilerParams(has_side_effects=_EFFECT))(*srcs, *lands, send_sems, recv_sems, *after)
    return outs[0], outs[1], list(outs[2:2 + n]), list(outs[2 + n:2 + 2 * n])


def _exchange_wait(handle, plan, name, after):
    send_sems, recv_sems, srcs, lands = handle
    n = len(srcs)

    def body(*refs):
        src_refs, land_refs = refs[:n], refs[n:2 * n]
        for cp in _plan_copies(plan, src_refs, land_refs, refs[2 * n], refs[2 * n + 1]):
            cp.wait_send()
            cp.wait_recv()

    outs = pl.pallas_call(
        body, name=name,
        out_shape=tuple(pltpu.HBM(a.shape, a.dtype) for a in srcs) + tuple(pltpu.HBM(a.shape, a.dtype) for a in lands),
        in_specs=[_HBM] * (2 * n) + [_SEM, _SEM] + [pl.BlockSpec(memory_space=pl.ANY)] * len(after),
        out_specs=(_HBM,) * (2 * n), input_output_aliases={i: i for i in range(2 * n)},
        compiler_params=pltpu.CompilerParams(has_side_effects=_EFFECT))(*srcs, *lands, send_sems, recv_sems, *after)
    return list(outs[:n]), list(outs[n:])


SMALL = ("g1", "g2", "g3", "g4", "ln_g", "ln_b", "b_sp", "w_sp")


def _small_sum_adam(block, parts, lands, ws, ms, vs):
    n, ns = len(parts), len(SMALL)

    def body(blk_ref, *refs):
        p_refs, l_refs = refs[:n], refs[n:2 * n]
        w_refs, m_refs, v_refs = (refs[2 * n + i * ns:2 * n + (i + 1) * ns] for i in range(3))
        o = 2 * n + 3 * ns
        loss_ref = refs[o]
        g_out, d_out, m_out, v_out = (refs[o + 1 + i * ns:o + 1 + (i + 1) * ns] for i in range(4))
        me = blk_ref[0]
        sums = []
        for k in range(n):
            acc = jnp.where(me == 0, p_refs[k][...], l_refs[k][0])
            for b in range(1, 8):
                acc = acc + jnp.where(me == b, p_refs[k][...], l_refs[k][b])
            sums.append(acc)
        loss_ref[...] = sums[0]
        for i in range(ns):
            g_out[i][...] = sums[1 + i]
            d_out[i][...], m_out[i][...], v_out[i][...] = _adam_math(w_refs[i][...], sums[1 + i], m_refs[i][...],
                                                                     v_refs[i][...])

    args = list(parts) + list(lands) + [t[k] for t in (ws, ms, vs) for k in SMALL]
    shapes = [jax.ShapeDtypeStruct(p.shape, F32) for p in parts[:1]] + [jax.ShapeDtypeStruct(p.shape, F32) for p in parts[1:]] * 4
    vmem = pl.BlockSpec(memory_space=pltpu.VMEM)
    outs = pl.pallas_call(
        body, name="small_sum_adam", out_shape=tuple(shapes),
        in_specs=[pl.BlockSpec(memory_space=pltpu.SMEM)] + [vmem] * len(args), out_specs=(vmem,) * len(shapes),
        compiler_params=pltpu.CompilerParams(vmem_limit_bytes=VMEM_LIMIT))(block, *args)
    groups = [dict(zip(SMALL, outs[1 + i * ns:1 + (i + 1) * ns])) for i in range(4)]
    return (outs[0], *groups)


def _rope_tables(S):
    half = HD // 2
    inv_freq = jnp.tile(ROPE_THETA ** (-jnp.arange(half, dtype=F32) / half), 4)
    sign = jnp.tile(jnp.concatenate([-jnp.ones(half, F32), jnp.ones(half, F32)]), 2)
    ang = jnp.arange(S, dtype=F32)[:, None] * inv_freq[None, :]
    return jnp.cos(ang), jnp.sin(ang) * sign[None, :]


def _to_blocks(g):
    return g.reshape(8, g.shape[0] // 8, g.shape[1])


def _from_blocks(t, col_sharded):
    if col_sharded:
        return t.transpose(1, 0, 2).reshape(t.shape[1], 8 * t.shape[2])
    return t.reshape(8 * t.shape[1], t.shape[2])


class _NoComm:
    def __init__(self, late_weights):
        self._late = late_weights
        self.grads = {}

    def start_tie(self):
        return jnp.zeros((8, 128), F32)

    def late_weights(self, after):
        return self._late

    def small_start(self, loss_row, grads):
        self.small = (loss_row, grads)
        return jnp.zeros((8, 128), F32)

    def rs_start(self, key, gblocks, after=()):
        self.grads[key] = gblocks
        return jnp.zeros((8, 128), F32)

    def rs_mid(self, key, after):
        return jnp.zeros((8, 128), F32)


class _FsdpComm:
    def __init__(self, late_shards, col_sharded, after, idx, block):
        self._col, self._idx, self._block, self._rs = col_sharded, idx, block, {}
        n = len(late_shards)
        self._gather, self._token = _exchange_start(
            late_shards, [(8,) + s.shape for s in late_shards], _plan_gather(n), 7 * n, "ag_late_start", (after,))

    def start_tie(self):
        return self._token

    def late_weights(self, after):
        shards, lands = _exchange_wait(self._gather, _plan_gather(len(self._col)), "ag_late_wait", after)
        lands = [lax.dynamic_update_index_in_dim(t, s, self._block, 0) for t, s in zip(lands, shards)]
        return [_from_blocks(t, cs) for t, cs in zip(lands, self._col)]

    def small_start(self, loss_row, grads):
        parts = [loss_row] + [grads[k] for k in SMALL]
        n = len(parts)
        self._small, token = _exchange_start(parts, [(8,) + p.shape for p in parts], _plan_gather(n), 7 * n,
                                             "small_start", (self._token,))
        return token

    def small_finish(self, ws, ms, vs, after):
        n = 1 + len(SMALL)
        parts, lands = _exchange_wait(self._small, _plan_gather(n), "small_wait", after)
        return _small_sum_adam(jnp.reshape(self._block, (1,)).astype(jnp.int32), parts, lands, ws, ms, vs)

    def rs_start(self, key, gblocks, after=()):
        n = len(gblocks)
        d2d, token = _exchange_start(gblocks, [(4,) + g.shape[1:] for g in gblocks], _plan_d2d(n), 4 * n,
                                     "rs_%s_d2d_start" % key, (self._token,) + tuple(after))
        self._rs[key] = dict(n=n, d2d=d2d)
        return token

    def rs_mid(self, key, after):
        st = self._rs[key]
        gblocks, from_sib = _exchange_wait(st["d2d"], _plan_d2d(st["n"]), "rs_%s_d2d_wait" % key, after)
        st["own"], sends = _rs_add(gblocks, from_sib, self._idx, "rs_add_" + key)
        st["ici"], token = _exchange_start(sends, [t.shape for t in sends], _plan_ici(st["n"]), 3 * st["n"],
                                           "rs_%s_ici_start" % key, (self._token,))
        return token

    def rs_end(self, key, after):
        st = self._rs[key]
        return st["own"], _exchange_wait(st["ici"], _plan_ici(st["n"]), "rs_%s_ici_wait" % key, after)[1]


def _local_step(x, tgt, h1, cos_t, sin_t, win_t, comm, g1, g2, g3, g4, ln_g, ln_b, w_sp, b_sp):
    b_sp_t = b_sp.T
    w_sp_t = w_sp.transpose(0, 2, 1)

    *qkvs, puz, gates = _proj_all(h1, win_t, cos_t, sin_t, comm.start_tie())
    fwd = _attn_fwd(qkvs)
    ya, *lses = _attn_combine([o for o, _ in fwd], [l for _, l in fwd])
    yg = _gmlp_fwd(puz, ln_g, ln_b, w_sp, b_sp_t)
    wba, wbg, wout, wmi, wmo = comm.late_weights(after=(ya, yg, gates))
    merged, y, x2, h3 = _merge_fwd(ya, yg, gates, wba, wbg, wout, x, g2, g3)
    a, dy2, dout, loss_row, dg4 = _mlp_fwd(h3, wmi, wmo, x2, tgt, g4)

    dpre, dx2, dy, dg3, dg2 = _mlp_bwd(dy2, a, wmo, wmi, x2, y, dout, g2, g3)
    dwmo = _mm_tn(a, dy2, "dw_mlp_out", square_a=True, tk=MM_TN_TK // 2)
    dwmi = _mm_tn(h3, dpre, "dw_mlp_in", col_blocks=8)
    tie = comm.rs_start("mlp", [dwmi, _to_blocks(dwmo)])
    dgates, da, db, dyg, *rest = _outproj_bwd(dy, wout, ya, yg, gates, wba, wbg, tie)
    dyas, dsums = rest[:3], rest[3:]
    tie = comm.rs_mid("mlp", after=(dyg,))
    dpuz, dwsp, dbs, dlng, dlnb = _gmlp_bwd(puz, dyg, ln_g + tie[0, 0], ln_b, w_sp, w_sp_t, b_sp_t)
    dqkvs = _attn_bwd(qkvs, dyas, dsums, lses)
    grad_x, dg1, dproj = _inproj_bwd(dqkvs, dpuz, dgates, win_t, cos_t, sin_t, x, dx2, g1)
    small = dict(g1=dg1, g2=dg2, g3=dg3, g4=dg4, ln_g=dlng, ln_b=dlnb, b_sp=dbs.reshape(4, CHUNK),
                 w_sp=dwsp.reshape(4 * CHUNK, CHUNK))
    tie = comm.small_start(loss_row, small)
    dwin_t = _mm_tn(dproj, h1, "dw_in", tie=tie)
    tie = comm.rs_start("win", [_to_blocks(dwin_t)])
    dwout = _mm_tn(merged, dy, "dw_out", tie=tie)
    tie = comm.rs_mid("win", after=(dwout,))
    dwba = _mm_tn(ya, da, "dw_branch_attn", tie=tie, col_blocks=8)
    dwbg = _mm_tn(yg, db, "dw_branch_gmlp", tie=tie, col_blocks=8)
    tie = comm.rs_start("mid", [dwba, dwbg, _to_blocks(dwout)], after=(tie,))
    return grad_x, tie


def kernel(x, norm_pre_mix, w_in, w_spatial, b_spatial, ln_v_gain, ln_v_bias, w_branch_attn, w_branch_gmlp, w_out, norm_post_mix, norm_pre_mlp, w_mlp_in, w_mlp_out, norm_post_mlp, loss_target, m_norm_pre_mix, m_w_in, m_w_spatial, m_b_spatial, m_ln_v_gain, m_ln_v_bias, m_w_branch_attn, m_w_branch_gmlp, m_w_out, m_norm_post_mix, m_norm_pre_mlp, m_w_mlp_in, m_w_mlp_out, m_norm_post_mlp, v_norm_pre_mix, v_w_in, v_w_spatial, v_b_spatial, v_ln_v_gain, v_ln_v_bias, v_w_branch_attn, v_w_branch_gmlp, v_w_out, v_norm_post_mix, v_norm_pre_mlp, v_w_mlp_in, v_w_mlp_out, v_norm_post_mlp):
    mx, my, mc = _mesh_pos()
    rel = [(0, 0), (1, 0), (0, 1), (1, 1)]
    chip_of = [2 * (mx ^ dx) + (my ^ dy) for dx, dy in rel]
    idx = jnp.stack([2 * k + mc for k in chip_of] + chip_of).astype(jnp.int32)

    w_in_t, m_w_in_t, v_w_in_t = (t[0].T for t in (w_in, m_w_in, v_w_in))
    shard = w_in_t.astype(BF)
    gather, token = _exchange_start([shard], [(8,) + shard.shape], _plan_gather_near, 4, "ag_win_start", ())
    h1 = _rms_fwd(x[0], norm_pre_mix + token[0, 0], "rms_pre_mix")
    cos_t, sin_t = _rope_tables(x.shape[1])
    gather = _exchange_forward(gather, _plan_gather_near, _plan_gather_pass(True), (1, 2, 3), "ag_win_pass",
                               (h1, cos_t, sin_t))
    (shard,), (g_win,) = _exchange_wait(gather, _plan_gather_pass(False), "ag_win_wait", ())
    g_win = lax.dynamic_update_index_in_dim(g_win, shard, 4 * mx + 2 * my + mc, 0)
    win_t = _from_blocks(g_win, False)
    late = [w_branch_attn[0], w_branch_gmlp[0], w_out[0], w_mlp_in[0], w_mlp_out[0]]
    comm = _FsdpComm([w.astype(BF) for w in late], [True, True, False, True, False], g_win, idx, 4 * mx + 2 * my + mc)

    grad_x, tie = _local_step(
        x[0], loss_target[0], h1, cos_t, sin_t, win_t, comm,
        norm_pre_mix, norm_post_mix, norm_pre_mlp, norm_post_mlp, ln_v_gain, ln_v_bias, w_spatial[0], b_spatial[0])

    flat = lambda t: t.reshape(-1, t.shape[-1])
    small_w = dict(zip(SMALL, map(flat, (norm_pre_mix, norm_post_mix, norm_pre_mlp, norm_post_mlp, ln_v_gain, ln_v_bias,
                                         b_spatial, w_spatial))))
    small_m = dict(zip(SMALL, map(flat, (m_norm_pre_mix, m_norm_post_mix, m_norm_pre_mlp, m_norm_post_mlp, m_ln_v_gain,
                                         m_ln_v_bias, m_b_spatial, m_w_spatial))))
    small_v = dict(zip(SMALL, map(flat, (v_norm_pre_mix, v_norm_post_mix, v_norm_pre_mlp, v_norm_post_mlp, v_ln_v_gain,
                                         v_ln_v_bias, v_b_spatial, v_w_spatial))))
    loss_out, sg, sd, sm, sv = comm.small_finish(small_w, small_m, small_v, after=(tie,))
    loss = loss_out[0, 0]
    tie = comm.rs_mid("mid", after=(loss_out,))
    mlp_own, mlp_chips = comm.rs_end("mlp", after=(tie,))
    first = lambda ts: [t[0] for t in ts]
    upd = {
        "w_mlp_in": _adam_shards(mlp_own[:1], mlp_chips[:1], [w_mlp_in[0]], [m_w_mlp_in[0]], [v_w_mlp_in[0]],
                                 "adam_w_mlp_in")[0],
        "w_mlp_out": _adam_shards(mlp_own[1:], mlp_chips[1:], [w_mlp_out[0]], [m_w_mlp_out[0]], [v_w_mlp_out[0]],
                                  "adam_w_mlp_out")[0],
    }
    win_own, win_chips = comm.rs_end("win", after=(upd["w_mlp_in"][0], upd["w_mlp_out"][0]))
    upd["w_in"] = tuple(t.T for t in _adam_shards(win_own, win_chips, [w_in_t], [m_w_in_t], [v_w_in_t], "adam_w_in")[0])
    mid_own, mid_chips = comm.rs_end("mid", after=(upd["w_in"][0],))
    upd.update(zip(("w_branch_attn", "w_branch_gmlp", "w_out"), _adam_shards(
        mid_own, mid_chips, first((w_branch_attn, w_branch_gmlp, w_out)),
        first((m_w_branch_attn, m_w_branch_gmlp, m_w_out)), first((v_w_branch_attn, v_w_branch_gmlp, v_w_out)), "adam_mid")))
    order = ["g1", "w_in", "w_sp", "b_sp", "ln_g", "ln_b", "w_branch_attn", "w_branch_gmlp", "w_out", "g2", "g3",
             "w_mlp_in", "w_mlp_out", "g4"]
    small_shape = dict(g1=norm_pre_mix.shape, g2=norm_post_mix.shape, g3=norm_pre_mlp.shape, g4=norm_post_mlp.shape,
                       ln_g=ln_v_gain.shape, ln_b=ln_v_bias.shape, b_sp=b_spatial.shape, w_sp=w_spatial.shape)

    def pick(which):
        return [upd[nm][which][None] if nm in upd else (sg, sd, sm, sv)[which][nm].reshape(small_shape[nm])
                for nm in order]

    return (loss, grad_x[None], *pick(0), *pick(1), *pick(2), *pick(3))
```

```python
import math

import jax
import jax.numpy as jnp
from jax import lax
from jax.experimental import pallas as pl
from jax.experimental.pallas import tpu as pltpu

D = 1024
HD = 64
NSLOT = 4
GW = NSLOT * HD
DILS = (1, 4, 16)
QB = 128
ATTN_NSUB = 4
PROJ_TM = 1024
MLP_TM_FWD = 512
MLP_TM_BWD = 256
MM_TN_TK = 4096
MM_TN_TILE = 1024
ELEMENTWISE_BLOCK_BYTES = 1 << 20
AW = 3 * GW
W_UZ0, W_G0 = 3 * AW, 3 * AW + 1024
GMLP_W = 512
CHUNK = 128
DFF = 4096
EPS = 1e-6
ROPE_THETA = 10000.0
SCALE = HD ** -0.5
NEG = -1e30

ADAM_LR = 0.001
ADAM_B1 = 0.9
ADAM_B2 = 0.999
ADAM_EPS = 1e-08
ADAM_WD = 0.01
ADAM_STEP = 10

BF = jnp.bfloat16
F32 = jnp.float32
MESH = pl.DeviceIdType.MESH
VMEM_LIMIT = 56 * 1024 * 1024


def _cp(sem):
    return pltpu.CompilerParams(dimension_semantics=sem, vmem_limit_bytes=VMEM_LIMIT)


def _dot(a, b):
    return jnp.dot(a, b, preferred_element_type=F32)


def _dot_nt(a, b):
    return lax.dot_general(a, b, (((1,), (1,)), ((), ())), preferred_element_type=F32)


def _dot_tn(a, b):
    return lax.dot_general(a, b, (((0,), (0,)), ((), ())), preferred_element_type=F32)


def _gelu(x):
    return jax.nn.gelu(x, approximate=True)


def _gelu_with_grad(x):
    k = math.sqrt(2.0 / math.pi)
    t = jnp.tanh(k * (x + 0.044715 * (x * x * x)))
    cdf = 0.5 * (1.0 + t)
    return x * cdf, cdf + 0.5 * x * (1.0 - t * t) * (k * (1.0 + 3.0 * 0.044715 * x * x))


def _swap_halves(t):
    w = t.shape[1]
    lane = lax.broadcasted_iota(jnp.int32, t.shape, 1)
    first = (lane & (HD - 1)) < (HD // 2)
    return jnp.where(first, pltpu.roll(t, w - HD // 2, 1), pltpu.roll(t, HD // 2, 1))


def _head_mask(hh):
    lane = lax.broadcasted_iota(jnp.int32, (1, GW), 1)
    return jnp.logical_and(lane >= hh * HD, lane < (hh + 1) * HD)


def _rms_stats(xf):
    return lax.rsqrt(jnp.mean(xf * xf, axis=-1, keepdims=True) + EPS)


def _rms_bwd(xf, r, gain, dout):
    n = xf * r
    t = dout * gain
    dx = r * (t - n * jnp.mean(t * n, axis=-1, keepdims=True))
    return dx, jnp.sum(dout * n, axis=0, keepdims=True)


def _rms_fwd(x, gain, name):
    S = x.shape[0]
    tm = min(512, S)

    def body(x_ref, g_ref, h_ref):
        xf = x_ref[...]
        h_ref[...] = (xf * _rms_stats(xf) * g_ref[...]).astype(BF)

    return pl.pallas_call(
        body, name=name, out_shape=jax.ShapeDtypeStruct((S, D), BF), grid=(S // tm,),
        in_specs=[pl.BlockSpec((tm, D), lambda i: (i, 0)), pl.BlockSpec((1, D), lambda i: (0, 0))],
        out_specs=pl.BlockSpec((tm, D), lambda i: (i, 0)), compiler_params=_cp(("parallel",)))(x, gain)


def _dilate_store(val, scr, o_ref, lead, d):
    rows = val.shape[0]
    if d == 1:
        o_ref[lead + (0,)] = val.astype(o_ref.dtype)
        return
    for hf in range(2):
        scr[hf, pl.ds(0, rows), :] = val[:, hf * 128:(hf + 1) * 128]
    for r in range(d):
        for hf in range(2):
            o_ref[lead + (r, slice(None), slice(hf * 128, (hf + 1) * 128))] = (
                scr[hf, pl.ds(r, rows // d, stride=d), :].astype(o_ref.dtype))


def _undilate_load(i_ref, lead, d, scr, rows):
    if d == 1:
        return i_ref[lead + (0,)].astype(F32)
    for r in range(d):
        for hf in range(2):
            scr[hf, pl.ds(r, rows // d, stride=d), :] = (
                i_ref[lead + (r, slice(None), slice(hf * 128, (hf + 1) * 128))].astype(F32))
    return jnp.concatenate([scr[0, pl.ds(0, rows), :], scr[1, pl.ds(0, rows), :]], axis=1)


def _rope_fwd(y, c_ref, s_ref):
    cosv = jnp.concatenate([c_ref[...]] * 2, axis=1)
    sinv = jnp.concatenate([s_ref[...]] * 2, axis=1)
    return y * cosv + _swap_halves(y) * sinv


def _rope_bwd(dy, c_ref, s_ref):
    cosv = jnp.concatenate([c_ref[...]] * 2, axis=1)
    sinv = jnp.concatenate([s_ref[...]] * 2, axis=1)
    return dy * cosv + _swap_halves(dy * sinv)


def _proj_all(h, win_t, cos_t, sin_t, tie):
    S, K = h.shape
    tm = min(PROJ_TM, S)

    def body(h_ref, w_ref, c_ref, s_ref, tie_ref, o0_ref, o1_ref, o2_ref, p_ref, gt_ref, scr):
        hv = h_ref[...]
        col = lambda b: _dot_nt(hv, w_ref[b * GW:(b + 1) * GW, :])
        ys = [col(3 * t) for t in range(3)]
        for g, (o_ref, d) in enumerate(zip((o0_ref, o1_ref, o2_ref), DILS)):
            nxt = [col(3 * t + g + 1) if g < 2 else col(W_UZ0 // GW + t) for t in range(3)]
            for t in range(3):
                y = _rope_fwd(ys[t], c_ref, s_ref) if t < 2 else ys[t]
                _dilate_store(y, scr.at[t], o_ref, (t,), d)
            ys = nxt
        for b in range((W_G0 - W_UZ0) // GW):
            y = ys[b] if b < 3 else col(W_UZ0 // GW + b)
            p_ref[:, b * GW:(b + 1) * GW] = y.astype(BF)
        for b in range(2 * D // GW):
            gt_ref[:, b * GW:(b + 1) * GW] = jax.nn.sigmoid(col(W_G0 // GW + b)).astype(BF)

    row = lambda w: pl.BlockSpec((tm, w), lambda i: (i, 0))
    return pl.pallas_call(
        body, name="proj_all", grid=(S // tm,),
        out_shape=tuple(jax.ShapeDtypeStruct((3, d, S // d, GW), BF) for d in DILS)
        + (jax.ShapeDtypeStruct((S, W_G0 - W_UZ0), BF), jax.ShapeDtypeStruct((S, 2 * D), BF)),
        in_specs=[row(K), _resident(win_t.shape), row(128), row(128), pl.BlockSpec(memory_space=pl.ANY)],
        out_specs=tuple(pl.BlockSpec((3, d, tm // d, GW), lambda i: (0, 0, i, 0)) for d in DILS)
        + (row(W_G0 - W_UZ0), row(2 * D)),
        scratch_shapes=[pltpu.VMEM((3, 2, tm, 128), F32)],
        compiler_params=_cp(("parallel",)))(h, win_t, cos_t, sin_t, tie)


def _band_masks(first_step):
    row = lax.broadcasted_iota(jnp.int32, (QB, 2 * QB), 0)
    col = lax.broadcasted_iota(jnp.int32, (QB, 2 * QB), 1)
    band = jnp.logical_and(col >= row, col <= row + QB)
    return band, jnp.logical_and(band, jnp.logical_or(col >= QB, jnp.logical_not(first_step)))


def _group_step(k, nsteps):
    return lambda g, n: jnp.where(g == k, n, jnp.where(g < k, 0, nsteps - 1))


def _attn_fwd(qkvs):
    S = qkvs[0].shape[1] * qkvs[0].shape[2]
    nsub = min(ATTN_NSUB, S // max(DILS) // QB)
    R = nsub * QB
    nsteps = S // R

    def group_body(first, q_ref, kp_ref, kc_ref, vp_ref, vc_ref, o_ref, lse_ref):
        band, band_first = _band_masks(first)
        kfull = jnp.concatenate([kp_ref[...], kc_ref[...]], axis=0)
        vfull = jnp.concatenate([vp_ref[...], vc_ref[...]], axis=0)
        chains = [(sb, hh) for sb in range(nsub) for hh in range(NSLOT)]
        win = lambda t, sb: t[sb * QB:(sb + 2) * QB]
        scores = []
        for sb, hh in chains:
            qh = jnp.where(_head_mask(hh), q_ref[sb * QB:(sb + 1) * QB, :], 0)
            scores.append(_dot_nt(qh, win(kfull, sb)))
        soft = []
        for (sb, hh), sc in zip(chains, scores):
            sc = jnp.where(band_first if sb == 0 else band, sc * SCALE, NEG)
            m = jnp.max(sc, axis=1, keepdims=True)
            p = jnp.exp(sc - m)
            den = jnp.sum(p, axis=1, keepdims=True)
            soft.append((p.astype(BF), den, m + jnp.log(den)))
        accs = [_dot(p, win(vfull, sb)) for (sb, hh), (p, _, _) in zip(chains, soft)]
        for sb in range(nsub):
            o = jnp.zeros((QB, GW), F32)
            lse = jnp.zeros((QB, GW), F32)
            for hh in range(NSLOT):
                hm = _head_mask(hh)
                _, den, lrow = soft[sb * NSLOT + hh]
                o = o + jnp.where(hm, accs[sb * NSLOT + hh] / den, 0.0)
                lse = lse + jnp.where(hm, lrow, 0.0)
            o_ref[sb * QB:(sb + 1) * QB, :] = o
            lse_ref[sb * QB:(sb + 1) * QB, :] = lse

    def body(*refs):
        g, n = pl.program_id(0), pl.program_id(1)
        for k, d in enumerate(DILS):
            @pl.when(g == k)
            def _():
                group_body(n % (nsteps // d) == 0, *refs[5 * k:5 * k + 5], *refs[15 + 2 * k:17 + 2 * k])

    in_specs, out_specs, operands = [], [], []
    for k in range(len(DILS)):
        at = _group_step(k, nsteps)
        cur = lambda t, at=at: pl.BlockSpec((None, R, GW), lambda g, n: (t, at(g, n), 0))
        prv = lambda t, at=at: pl.BlockSpec((None, QB, GW), lambda g, n: (t, jnp.maximum(at(g, n) * nsub - 1, 0), 0))
        in_specs += [cur(0), prv(1), cur(1), prv(2), cur(2)]
        out_specs += [pl.BlockSpec((R, GW), lambda g, n, at=at: (at(g, n), 0))] * 2
        operands += [qkvs[k].reshape(3, S, GW)] * 5
    outs = pl.pallas_call(
        body, name="attn_fwd", grid=(len(DILS), nsteps),
        out_shape=(jax.ShapeDtypeStruct((S, GW), F32),) * (2 * len(DILS)),
        in_specs=in_specs, out_specs=tuple(out_specs),
        compiler_params=_cp(("arbitrary", "arbitrary")))(*operands)
    return [(outs[2 * k].reshape(d, S // d, GW), outs[2 * k + 1].reshape(d, S // d, GW)) for k, d in enumerate(DILS)]


def _attn_combine(os_, lses):
    S = os_[0].shape[1]
    tm = min(1024, S)

    def body(o0, o1, o2, l0, l1, l2, y_ref, j0, j1, j2, scr):
        os_nat = [_undilate_load(o, (), d, scr, tm) for o, d in zip((o0, o1, o2), DILS)]
        a, b, c = [_undilate_load(l, (), d, scr, tm) for l, d in zip((l0, l1, l2), DILS)]
        m = jnp.maximum(jnp.maximum(a, b), c)
        wa, wb, wc = jnp.exp(a - m), jnp.exp(b - m), jnp.exp(c - m)
        den = wa + wb + wc
        y_ref[...] = ((wa * os_nat[0] + wb * os_nat[1] + wc * os_nat[2]) / den).astype(BF)
        lse = m + jnp.log(den)
        for j_ref, d in zip((j0, j1, j2), DILS):
            _dilate_store(lse, scr, j_ref, (), d)

    dil = lambda d: pl.BlockSpec((d, tm // d, GW), lambda i: (0, i, 0))
    dshape = lambda d: jax.ShapeDtypeStruct((d, S // d, GW), F32)
    return pl.pallas_call(
        body, name="attn_combine", grid=(S // tm,),
        out_shape=(jax.ShapeDtypeStruct((S, GW), BF),) + tuple(dshape(d) for d in DILS),
        in_specs=[dil(d) for d in DILS] * 2,
        out_specs=(pl.BlockSpec((tm, GW), lambda i: (i, 0)),) + tuple(dil(d) for d in DILS),
        scratch_shapes=[pltpu.VMEM((2, tm, 128), F32)],
        compiler_params=_cp(("parallel",)))(*os_, *lses)


def _tril(upper=False):
    row = lax.broadcasted_iota(jnp.int32, (CHUNK, CHUNK), 0)
    col = lax.broadcasted_iota(jnp.int32, (CHUNK, CHUNK), 1)
    return row <= col if upper else col <= row


def _ln_fwd(z, gain, bias):
    mu = jnp.mean(z, axis=-1, keepdims=True)
    zc = z - mu
    rstd = lax.rsqrt(jnp.mean(zc * zc, axis=-1, keepdims=True) + EPS)
    zhat = zc * rstd
    return zhat, rstd, zhat * gain + bias


def _gmlp_fwd(puz, ln_g, ln_b, w_sp, b_sp_t):
    S = puz.shape[0]
    tm = min(512, S)
    nch = tm // CHUNK

    def body(p_ref, g_ref, b_ref, w_ref, bt_ref, o_ref):
        tril = _tril()
        ws = [jnp.where(tril, w_ref[gg], 0.0).astype(BF) for gg in range(4)]
        for ch in range(nch):
            rows = slice(ch * CHUNK, (ch + 1) * CHUNK)
            z = _gelu(p_ref[rows, GMLP_W:].astype(F32))
            _, _, zn = _ln_fwd(z, g_ref[...], b_ref[...])
            zn = zn.astype(BF)
            for gg in range(4):
                cols = slice(gg * CHUNK, (gg + 1) * CHUNK)
                sz = _dot(ws[gg], zn[:, cols]) + bt_ref[:, gg:gg + 1]
                u = _gelu(p_ref[rows, cols].astype(F32))
                o_ref[rows, cols] = (u * sz).astype(BF)

    return pl.pallas_call(
        body, name="gmlp_fwd", out_shape=jax.ShapeDtypeStruct((S, GMLP_W), BF), grid=(S // tm,),
        in_specs=[pl.BlockSpec((tm, 2 * GMLP_W), lambda i: (i, 0)),
                  pl.BlockSpec((1, GMLP_W), lambda i: (0, 0)), pl.BlockSpec((1, GMLP_W), lambda i: (0, 0)),
                  pl.BlockSpec((4, CHUNK, CHUNK), lambda i: (0, 0, 0)), pl.BlockSpec((CHUNK, 4), lambda i: (0, 0))],
        out_specs=pl.BlockSpec((tm, GMLP_W), lambda i: (i, 0)),
        compiler_params=_cp(("parallel",)))(puz, ln_g, ln_b, w_sp, b_sp_t)


def _merge_fwd(ya, yg, gates, wba, wbg, wout, x, g2, g3):
    S = x.shape[0]
    tm = min(512, S)

    def body(ya_ref, yg_ref, gt_ref, wba_ref, wbg_ref, wo_ref, x_ref, g2_ref, g3_ref,
             mg_ref, y_ref, x2_ref, h3_ref):
        a = _dot(ya_ref[...], wba_ref[...])
        b = _dot(yg_ref[...], wbg_ref[...])
        merged = (gt_ref[:, :D].astype(F32) * a + gt_ref[:, D:].astype(F32) * b).astype(BF)
        mg_ref[...] = merged
        y = _dot(merged, wo_ref[...])
        y_ref[...] = y
        x2 = x_ref[...] + y * _rms_stats(y) * g2_ref[...]
        x2_ref[...] = x2
        h3_ref[...] = (x2 * _rms_stats(x2) * g3_ref[...]).astype(BF)

    row = lambda w: pl.BlockSpec((tm, w), lambda i: (i, 0))
    full = lambda s: pl.BlockSpec(s, lambda i: (0, 0))
    return pl.pallas_call(
        body, name="merge_fwd", grid=(S // tm,),
        out_shape=(jax.ShapeDtypeStruct((S, D), BF), jax.ShapeDtypeStruct((S, D), F32), jax.ShapeDtypeStruct((S, D), F32),
                   jax.ShapeDtypeStruct((S, D), BF)),
        in_specs=[row(GW), row(GMLP_W), row(2 * D), full((GW, D)), full((GMLP_W, D)), full((D, D)), row(D),
                  full((1, D)), full((1, D))],
        out_specs=(row(D), row(D), row(D), row(D)),
        compiler_params=_cp(("parallel",)))(ya, yg, gates, wba, wbg, wout, x, g2, g3)


def _resident(shape):
    return pl.BlockSpec(shape, lambda i: (0,) * len(shape), pipeline_mode=pl.Buffered(1))


def _mlp_fwd(h3, wmi, wmo, x2, tgt, g4):
    S = x2.shape[0]
    tm = min(MLP_TM_FWD, S)

    def body(h_ref, wi_ref, wo_ref, x2_ref, t_ref, g4_ref, a_ref, dy2_ref, dout_ref, loss_ref, dg4_ref):
        @pl.when(pl.program_id(0) == 0)
        def _():
            loss_ref[...] = jnp.zeros_like(loss_ref)
            dg4_ref[...] = jnp.zeros_like(dg4_ref)

        halves = [slice(hh * (tm // 2), (hh + 1) * (tm // 2)) for hh in range(2)]
        acts = []
        for rows in halves:
            a = jnp.maximum(_dot(h_ref[rows, :], wi_ref[...]), 0.0)
            a_ref[rows, :] = a.astype(BF)
            acts.append((a * a).astype(BF))
        y2s = [_dot(a2, wo_ref[...]) for a2 in acts]
        lane = lax.broadcasted_iota(jnp.int32, (1, 128), 1)
        for rows, y2 in zip(halves, y2s):
            r = _rms_stats(y2)
            out = x2_ref[rows, :] + y2 * r * g4_ref[...]
            err = out - t_ref[rows, :]
            tot = jnp.sum(jnp.sum(err * err, axis=1, keepdims=True), axis=0, keepdims=True) * (0.5 / D)
            loss_ref[...] += jnp.where(lane == 0, tot, 0.0)
            dout = err * (1.0 / D)
            dout_ref[rows, :] = dout
            dy2, dg = _rms_bwd(y2, r, g4_ref[...], dout)
            dy2_ref[rows, :] = dy2.astype(BF)
            dg4_ref[...] += dg

    row = pl.BlockSpec((tm, D), lambda i: (i, 0))
    return pl.pallas_call(
        body, name="mlp_fwd", grid=(S // tm,),
        out_shape=(jax.ShapeDtypeStruct((S, DFF), BF), jax.ShapeDtypeStruct((S, D), BF), jax.ShapeDtypeStruct((S, D), F32),
                   jax.ShapeDtypeStruct((1, 128), F32), jax.ShapeDtypeStruct((1, D), F32)),
        in_specs=[row, _resident((D, DFF)), _resident((DFF, D)), row, row, pl.BlockSpec((1, D), lambda i: (0, 0))],
        out_specs=(pl.BlockSpec((tm, DFF), lambda i: (i, 0)), row, row,
                   pl.BlockSpec((1, 128), lambda i: (0, 0)), pl.BlockSpec((1, D), lambda i: (0, 0))),
        compiler_params=_cp(("arbitrary",)))(h3, wmi, wmo, x2, tgt, g4)


def _mlp_bwd(dy2, a, wmo, wmi, x2, y, dout, g2, g3):
    S = x2.shape[0]
    tm = min(MLP_TM_BWD, S)

    def body(dy2_ref, a_ref, wo_ref, wi_ref, x2_ref, y_ref, dout_ref, g2_ref, g3_ref,
             dpre_ref, dx2_ref, dy_ref, dg3_ref, dg2_ref):
        @pl.when(pl.program_id(0) == 0)
        def _():
            dg3_ref[...] = jnp.zeros_like(dg3_ref)
            dg2_ref[...] = jnp.zeros_like(dg2_ref)

        da2 = _dot_nt(dy2_ref[...], wo_ref[...])
        dpre = (2.0 * a_ref[...].astype(F32) * da2).astype(BF)
        dpre_ref[...] = dpre
        dh3 = _dot_nt(dpre, wi_ref[...])
        x2 = x2_ref[...]
        dx3, dg3 = _rms_bwd(x2, _rms_stats(x2), g3_ref[...], dh3)
        dx2 = dout_ref[...] + dx3
        dx2_ref[...] = dx2
        dg3_ref[...] += dg3
        yv = y_ref[...]
        dy, dg2 = _rms_bwd(yv, _rms_stats(yv), g2_ref[...], dx2)
        dy_ref[...] = dy.astype(BF)
        dg2_ref[...] += dg2

    row = pl.BlockSpec((tm, D), lambda i: (i, 0))
    wide = pl.BlockSpec((tm, DFF), lambda i: (i, 0))
    vec = pl.BlockSpec((1, D), lambda i: (0, 0))
    return pl.pallas_call(
        body, name="mlp_bwd", grid=(S // tm,),
        out_shape=(jax.ShapeDtypeStruct((S, DFF), BF), jax.ShapeDtypeStruct((S, D), F32), jax.ShapeDtypeStruct((S, D), BF),
                   jax.ShapeDtypeStruct((1, D), F32), jax.ShapeDtypeStruct((1, D), F32)),
        in_specs=[row, wide, _resident((DFF, D)), _resident((D, DFF)), row, row, row, vec, vec],
        out_specs=(wide, row, row, vec, vec),
        compiler_params=_cp(("arbitrary",)))(dy2, a, wmo, wmi, x2, y, dout, g2, g3)


def _mm_tn(a, b, name, square_a=False, tk=MM_TN_TK, tie=None, col_blocks=None):
    S, M = a.shape
    N = b.shape[1]
    tk = min(tk, S)
    tm = max(t for t in range(128, min(MM_TN_TILE, M) + 1, 128) if M % t == 0)
    tn = max(t for t in range(128, min(MM_TN_TILE, N) + 1, 128) if N % t == 0)
    cb = N // col_blocks if col_blocks else tn
    while (M // tm) * (N // tn) * (S // tk) < 4 and tk % 256 == 0:
        tk //= 2
    assert M % tm == 0 and S % tk == 0 and tn % cb == 0
    nk = S // tk
    ties = () if tie is None else (tie,)

    def body(a_ref, b_ref, *rest):
        o_ref = rest[-1]
        k = pl.program_id(2)
        av = a_ref[...]
        if square_a:
            av = av * av
        part = _dot_tn(av, b_ref[...])
        if col_blocks:
            part = jnp.stack([part[:, t * cb:(t + 1) * cb] for t in range(tn // cb)])

        @pl.when(k == 0)
        def _():
            o_ref[...] = part

        @pl.when(k > 0)
        def _():
            o_ref[...] += part

    if col_blocks:
        out_shape, out_spec = (col_blocks, M, cb), pl.BlockSpec((tn // cb, tm, cb), lambda i, j, k: (j, i, 0))
    else:
        out_shape, out_spec = (M, N), pl.BlockSpec((tm, tn), lambda i, j, k: (i, j))
    return pl.pallas_call(
        body, name=name, out_shape=jax.ShapeDtypeStruct(out_shape, F32), grid=(M // tm, N // tn, nk),
        in_specs=[pl.BlockSpec((tk, tm), lambda i, j, k: (k, i)), pl.BlockSpec((tk, tn), lambda i, j, k: (k, j))]
        + [pl.BlockSpec(memory_space=pl.ANY)] * len(ties),
        out_specs=out_spec,
        compiler_params=_cp(("parallel", "parallel", "arbitrary")))(a, b, *ties)


def _outproj_bwd(dy, wout, ya, yg, gates, wba, wbg, tie):
    S = dy.shape[0]
    tm = min(512, S)

    def body(dy_ref, wo_ref, ya_ref, yg_ref, gt_ref, wba_ref, wbg_ref, tie_ref,
             dgt_ref, da_ref, db_ref, dyg_ref, e0, e1, e2, s0, s1, s2, scr):
        dm = _dot_nt(dy_ref[...], wo_ref[...])
        ga, gb = gt_ref[:, :D].astype(F32), gt_ref[:, D:].astype(F32)
        dgt_ref[:, :D] = (dm * _dot(ya_ref[...], wba_ref[...]) * ga * (1.0 - ga)).astype(BF)
        dgt_ref[:, D:] = (dm * _dot(yg_ref[...], wbg_ref[...]) * gb * (1.0 - gb)).astype(BF)
        da = (dm * ga).astype(BF)
        db = (dm * gb).astype(BF)
        da_ref[...] = da
        db_ref[...] = db
        dyg_ref[...] = _dot_nt(db, wbg_ref[...]).astype(BF)
        dya = _dot_nt(da, wba_ref[...]).astype(BF).astype(F32)
        dyy = dya * ya_ref[...].astype(F32)
        dsum = jnp.zeros((tm, GW), F32)
        for hh in range(NSLOT):
            hm = _head_mask(hh)
            dsum = dsum + jnp.where(hm, jnp.sum(jnp.where(hm, dyy, 0.0), axis=1, keepdims=True), 0.0)
        for e_ref, s_ref, d in zip((e0, e1, e2), (s0, s1, s2), DILS):
            _dilate_store(dya, scr, e_ref, (), d)
            _dilate_store(dsum, scr, s_ref, (), d)

    row = lambda w: pl.BlockSpec((tm, w), lambda i: (i, 0))
    full = lambda s: pl.BlockSpec(s, lambda i: (0, 0))
    dil = lambda d: pl.BlockSpec((d, tm // d, GW), lambda i: (0, i, 0))
    dshape = lambda d, t: jax.ShapeDtypeStruct((d, S // d, GW), t)
    return pl.pallas_call(
        body, name="outproj_bwd", grid=(S // tm,),
        out_shape=(jax.ShapeDtypeStruct((S, 2 * D), BF), jax.ShapeDtypeStruct((S, D), BF), jax.ShapeDtypeStruct((S, D), BF),
                   jax.ShapeDtypeStruct((S, GMLP_W), BF)) + tuple(dshape(d, BF) for d in DILS)
        + tuple(dshape(d, F32) for d in DILS),
        in_specs=[row(D), full((D, D)), row(GW), row(GMLP_W), row(2 * D), full((GW, D)), full((GMLP_W, D)),
                  pl.BlockSpec(memory_space=pl.ANY)],
        out_specs=(row(2 * D), row(D), row(D), row(GMLP_W)) + tuple(dil(d) for d in DILS) * 2,
        scratch_shapes=[pltpu.VMEM((2, tm, 128), F32)],
        compiler_params=_cp(("parallel",)))(dy, wout, ya, yg, gates, wba, wbg, tie)


def _gmlp_bwd(puz, dyg, ln_g, ln_b, w_sp, w_sp_t, b_sp_t):
    S = puz.shape[0]
    tm = min(512, S)
    nch = tm // CHUNK

    def body(p_ref, dy_ref, g_ref, b_ref, w_ref, wt_ref, bt_ref,
             dp_ref, dw_ref, dbs_ref, dg_ref, dbias_ref, dbacc_ref):
        i = pl.program_id(0)

        @pl.when(i == 0)
        def _():
            dw_ref[...] = jnp.zeros_like(dw_ref)
            dbacc_ref[...] = jnp.zeros_like(dbacc_ref)
            dg_ref[...] = jnp.zeros_like(dg_ref)
            dbias_ref[...] = jnp.zeros_like(dbias_ref)

        tril = _tril()
        ws = [jnp.where(tril, w_ref[gg], 0.0).astype(BF) for gg in range(4)]
        triu = _tril(upper=True)
        wts = [jnp.where(triu, wt_ref[gg], 0.0).astype(BF) for gg in range(4)]
        gain = g_ref[...]
        for ch in range(nch):
            rows = slice(ch * CHUNK, (ch + 1) * CHUNK)
            pz = p_ref[rows, GMLP_W:].astype(F32)
            z, z_grad = _gelu_with_grad(pz)
            zhat, rstd, zn = _ln_fwd(z, gain, b_ref[...])
            znb = zn.astype(BF)
            dzn_parts = []
            for gg in range(4):
                cols = slice(gg * CHUNK, (gg + 1) * CHUNK)
                pu = p_ref[rows, cols].astype(F32)
                u, u_grad = _gelu_with_grad(pu)
                sz = _dot(ws[gg], znb[:, cols]) + bt_ref[:, gg:gg + 1]
                dyv = dy_ref[rows, cols].astype(F32)
                dp_ref[rows, cols] = (dyv * sz * u_grad).astype(BF)
                dsz = dyv * u
                dbacc_ref[gg] += dsz
                dszb = dsz.astype(BF)
                dw_ref[gg] += _dot_nt(dszb, znb[:, cols])
                dzn_parts.append(_dot(wts[gg], dszb))
            dzn = jnp.concatenate(dzn_parts, axis=1)
            dg_ref[...] += jnp.sum(dzn * zhat, axis=0, keepdims=True)
            dbias_ref[...] += jnp.sum(dzn, axis=0, keepdims=True)
            dzh = dzn * gain
            dz = rstd * (dzh - jnp.mean(dzh, axis=-1, keepdims=True)
                         - zhat * jnp.mean(dzh * zhat, axis=-1, keepdims=True))
            dp_ref[rows, GMLP_W:] = (dz * z_grad).astype(BF)

        @pl.when(i == pl.num_programs(0) - 1)
        def _():
            for gg in range(4):
                dw_ref[gg] = jnp.where(tril, dw_ref[gg], 0.0)
                dbs_ref[gg] = jnp.sum(dbacc_ref[gg], axis=1, keepdims=True)

    full2 = lambda s: pl.BlockSpec(s, lambda i: (0, 0))
    full3 = lambda s: pl.BlockSpec(s, lambda i: (0, 0, 0))
    return pl.pallas_call(
        body, name="gmlp_bwd", grid=(S // tm,),
        out_shape=(jax.ShapeDtypeStruct((S, 2 * GMLP_W), BF), jax.ShapeDtypeStruct((4, CHUNK, CHUNK), F32),
                   jax.ShapeDtypeStruct((4, CHUNK, 1), F32), jax.ShapeDtypeStruct((1, GMLP_W), F32),
                   jax.ShapeDtypeStruct((1, GMLP_W), F32)),
        in_specs=[pl.BlockSpec((tm, 2 * GMLP_W), lambda i: (i, 0)), pl.BlockSpec((tm, GMLP_W), lambda i: (i, 0)),
                  full2((1, GMLP_W)), full2((1, GMLP_W)), full3((4, CHUNK, CHUNK)), full3((4, CHUNK, CHUNK)),
                  full2((CHUNK, 4))],
        out_specs=(pl.BlockSpec((tm, 2 * GMLP_W), lambda i: (i, 0)), full3((4, CHUNK, CHUNK)), full3((4, CHUNK, 1)),
                   full2((1, GMLP_W)), full2((1, GMLP_W))),
        scratch_shapes=[pltpu.VMEM((4, CHUNK, CHUNK), F32)],
        compiler_params=_cp(("arbitrary",)))(puz, dyg, ln_g, ln_b, w_sp, w_sp_t, b_sp_t)


def _attn_bwd(qkvs, dyas, dsums, lses):
    S = qkvs[0].shape[1] * qkvs[0].shape[2]
    nsub = min(ATTN_NSUB, S // max(DILS) // QB)
    R = nsub * QB
    nsteps = S // R
    NIN = 12

    def group_body(first, last, q_ref, qn_ref, kp_ref, kc_ref, vp_ref, vc_ref, dy_ref, dyn_ref, e_ref, en_ref,
                   l_ref, ln_ref, o_ref):
        band, band_first = _band_masks(first)
        row = lax.broadcasted_iota(jnp.int32, (QB, QB), 0)
        col = lax.broadcasted_iota(jnp.int32, (QB, QB), 1)
        mask_next = jnp.logical_and(col >= row, jnp.logical_not(last))
        kc, vc = kc_ref[...], vc_ref[...]
        kfull = jnp.concatenate([kp_ref[...], kc], axis=0)
        vfull = jnp.concatenate([vp_ref[...], vc], axis=0)
        k_last, v_last = kc[(nsub - 1) * QB:], vc[(nsub - 1) * QB:]
        q_ext = jnp.concatenate([q_ref[...], qn_ref[...]], axis=0)
        dy_ext = jnp.concatenate([dy_ref[...], dyn_ref[...]], axis=0)
        esum, esum_n, lse, lse_n = e_ref[...], en_ref[...], l_ref[...], ln_ref[...]
        win = lambda t, sb: t[sb * QB:(sb + 2) * QB]
        blk = lambda t, sb: t[sb * QB:(sb + 1) * QB]
        hms = [_head_mask(hh) for hh in range(NSLOT)]
        q_hs = [jnp.where(hm, q_ext, 0) for hm in hms]
        dy_hs = [jnp.where(hm, dy_ext, 0) for hm in hms]
        raw = []
        for hh in range(NSLOT):
            tiles = [(_dot_nt(blk(q_hs[hh], sb), win(kfull, sb)), _dot_nt(blk(dy_hs[hh], sb), win(vfull, sb)))
                     for sb in range(nsub)]
            tiles.append((_dot_nt(blk(q_hs[hh], nsub), k_last), _dot_nt(blk(dy_hs[hh], nsub), v_last)))
            raw.append(tiles)
        ps, dss = [], []
        for hh in range(NSLOT):
            rowstat = lambda t: jnp.max(jnp.where(hms[hh], t, -jnp.inf), axis=1, keepdims=True)
            p_h, ds_h = [], []
            for sb in range(nsub + 1):
                sc, dp = raw[hh][sb]
                if sb < nsub:
                    msk, lrow, erow = (band_first if sb == 0 else band), rowstat(blk(lse, sb)), rowstat(blk(esum, sb))
                else:
                    msk, lrow, erow = mask_next, rowstat(lse_n), rowstat(esum_n)
                p = jnp.where(msk, jnp.exp(sc * SCALE - lrow), 0.0)
                p_h.append(p.astype(BF))
                ds_h.append((p * (dp - erow)).astype(BF))
            ps.append(p_h)
            dss.append(ds_h)
        dq = [jnp.zeros((QB, GW), F32) for _ in range(nsub)]
        dk = [jnp.zeros((QB, GW), F32) for _ in range(nsub)]
        dv = [jnp.zeros((QB, GW), F32) for _ in range(nsub)]
        for hh in range(NSLOT):
            for sb in range(nsub):
                dq[sb] = dq[sb] + jnp.where(hms[hh], _dot(dss[hh][sb], win(kfull, sb)), 0.0)
                nxt = lambda t: t[sb + 1][:, :QB] if sb + 1 < nsub else t[nsub]
                dk[sb] = dk[sb] + _dot_tn(jnp.concatenate([dss[hh][sb][:, QB:], nxt(dss[hh])], axis=0), win(q_hs[hh], sb))
                dv[sb] = dv[sb] + _dot_tn(jnp.concatenate([ps[hh][sb][:, QB:], nxt(ps[hh])], axis=0), win(dy_hs[hh], sb))
        for sb in range(nsub):
            rows = slice(sb * QB, (sb + 1) * QB)
            o_ref[0, rows, :] = (dq[sb] * SCALE).astype(BF)
            o_ref[1, rows, :] = (dk[sb] * SCALE).astype(BF)
            o_ref[2, rows, :] = dv[sb].astype(BF)

    def body(*refs):
        g, n = pl.program_id(0), pl.program_id(1)
        for k, d in enumerate(DILS):
            @pl.when(g == k)
            def _():
                per = nsteps // d
                group_body(n % per == 0, n % per == per - 1, *refs[NIN * k:NIN * (k + 1)], refs[NIN * len(DILS) + k])

    in_specs, out_specs, operands = [], [], []
    for k in range(len(DILS)):
        at = _group_step(k, nsteps)
        prev = lambda g, n, at=at: jnp.maximum(at(g, n) * nsub - 1, 0)
        nxt = lambda g, n, at=at: jnp.minimum((at(g, n) + 1) * nsub, S // QB - 1)
        cur4 = lambda t, at=at: pl.BlockSpec((None, R, GW), lambda g, n: (t, at(g, n), 0))
        prv4 = lambda t, prev=prev: pl.BlockSpec((None, QB, GW), lambda g, n: (t, prev(g, n), 0))
        nxt4 = lambda t, nxt=nxt: pl.BlockSpec((None, QB, GW), lambda g, n: (t, nxt(g, n), 0))
        cur3 = pl.BlockSpec((R, GW), lambda g, n, at=at: (at(g, n), 0))
        nxt3 = pl.BlockSpec((QB, GW), lambda g, n, nxt=nxt: (nxt(g, n), 0))
        in_specs += [cur4(0), nxt4(0), prv4(1), cur4(1), prv4(2), cur4(2), cur3, nxt3, cur3, nxt3, cur3, nxt3]
        out_specs.append(pl.BlockSpec((3, R, GW), lambda g, n, at=at: (0, at(g, n), 0)))
        flat = lambda t: t.reshape(S, GW)
        operands += [qkvs[k].reshape(3, S, GW)] * 6 + [flat(dyas[k])] * 2 + [flat(dsums[k])] * 2 + [flat(lses[k])] * 2
    outs = pl.pallas_call(
        body, name="attn_bwd", grid=(len(DILS), nsteps),
        out_shape=(jax.ShapeDtypeStruct((3, S, GW), BF),) * len(DILS),
        in_specs=in_specs, out_specs=tuple(out_specs),
        compiler_params=_cp(("arbitrary", "arbitrary")))(*operands)
    return [t.reshape(3, d, S // d, GW) for t, d in zip(outs, DILS)]


def _inproj_bwd(dqkvs, dpuz, dgates, win_t, cos_t, sin_t, x, dx2, g1):
    S = x.shape[0]
    tm = min(512, S)

    def body(d0_ref, d1_ref, d2_ref, dp_ref, dg_ref, w_ref, c_ref, s_ref,
             x_ref, dx2_ref, g1_ref, gx_ref, dg1_ref, dn_ref, scr):
        i = pl.program_id(0)

        @pl.when(i == 0)
        def _():
            dg1_ref[...] = jnp.zeros_like(dg1_ref)

        dh = _dot(dp_ref[...], w_ref[W_UZ0:W_G0, :]) + _dot(dg_ref[...], w_ref[W_G0:, :])
        dn_ref[:, W_UZ0:W_G0] = dp_ref[...]
        dn_ref[:, W_G0:] = dg_ref[...]
        for t in range(3):
            for g, (d_ref, d) in enumerate(zip((d0_ref, d1_ref, d2_ref), DILS)):
                piece = _undilate_load(d_ref, (t,), d, scr, tm)
                if t < 2:
                    piece = _rope_bwd(piece, c_ref, s_ref)
                dn_ref[:, (3 * t + g) * GW:(3 * t + g + 1) * GW] = piece.astype(BF)
        dh = dh + _dot(dn_ref[:, :W_UZ0], w_ref[:W_UZ0, :])
        xv = x_ref[...]
        dx1, dg1 = _rms_bwd(xv, _rms_stats(xv), g1_ref[...], dh)
        gx_ref[...] = dx2_ref[...] + dx1
        dg1_ref[...] += dg1

    row = lambda w: pl.BlockSpec((tm, w), lambda i: (i, 0))
    full = lambda s: pl.BlockSpec(s, lambda i: (0, 0))
    dil = lambda d: pl.BlockSpec((3, d, tm // d, GW), lambda i: (0, 0, i, 0))
    return pl.pallas_call(
        body, name="inproj_bwd", grid=(S // tm,),
        out_shape=(jax.ShapeDtypeStruct((S, D), F32), jax.ShapeDtypeStruct((1, D), F32),
                   jax.ShapeDtypeStruct((S, win_t.shape[0]), BF)),
        in_specs=[dil(d) for d in DILS] + [row(2 * GMLP_W), row(2 * D), _resident(win_t.shape), row(128), row(128), row(D), row(D),
                                            full((1, D))],
        out_specs=(row(D), full((1, D)), row(win_t.shape[0])),
        scratch_shapes=[pltpu.VMEM((2, tm, 128), F32)],
        compiler_params=_cp(("arbitrary",)))(*dqkvs, dpuz, dgates, win_t, cos_t, sin_t, x, dx2, g1)


def _row_tile(rows, cols):
    cap = max(16, ELEMENTWISE_BLOCK_BYTES // (4 * cols))
    return max(t for t in range(16, cap + 1, 16) if rows % t == 0)


def _adam_math(w, g, m, v):
    m2 = ADAM_B1 * m + (1.0 - ADAM_B1) * g
    v2 = ADAM_B2 * v + (1.0 - ADAM_B2) * (g * g)
    m_hat = m2 / (1.0 - ADAM_B1 ** ADAM_STEP)
    v_hat = v2 / (1.0 - ADAM_B2 ** ADAM_STEP)
    delta = -ADAM_LR * (m_hat / (jnp.sqrt(v_hat) + ADAM_EPS) + ADAM_WD * w)
    return delta, m2, v2


def _row_tiles(shapes):
    steps = min(R // _row_tile(R, C) for R, C in shapes)
    assert all(R % (16 * steps) == 0 for R, _ in shapes)
    return steps, [R // steps for R, _ in shapes]


def _adam_shards(owns, recvs, ws, ms, vs, name):
    n = len(ws)
    steps, trs = _row_tiles([w.shape for w in ws])

    def body(*refs):
        for k in range(n):
            own_ref, r_ref, w_ref, m_ref, v_ref = refs[k:5 * n:n]
            g_ref, d_ref, m2_ref, v2_ref = refs[5 * n + k::n]
            g = own_ref[...] + r_ref[0].astype(F32) + r_ref[1].astype(F32) + r_ref[2].astype(F32)
            g_ref[...] = g
            d_ref[...], m2_ref[...], v2_ref[...] = _adam_math(w_ref[...], g, m_ref[...], v_ref[...])

    specs = [pl.BlockSpec((tr, w.shape[1]), lambda i: (i, 0)) for tr, w in zip(trs, ws)]
    rspecs = [pl.BlockSpec((3, tr, w.shape[1]), lambda i: (0, i, 0)) for tr, w in zip(trs, ws)]
    outs = pl.pallas_call(
        body, name=name, grid=(steps,), out_shape=tuple(jax.ShapeDtypeStruct(w.shape, F32) for w in ws) * 4,
        in_specs=specs + rspecs + specs * 3, out_specs=tuple(specs * 4),
        compiler_params=_cp(("parallel",)))(*owns, *recvs, *ws, *ms, *vs)
    return [tuple(outs[k::n]) for k in range(n)]


def _rs_add(gblocks, recvs, idx, name):
    n = len(gblocks)
    shapes = [g.shape[1:] for g in gblocks]
    steps, trs = _row_tiles(shapes)

    def body(t_ref, *refs):
        j = pl.program_id(1)
        for k in range(n):
            g_ref, r_ref, own_ref, send_ref = refs[k::n]
            s = g_ref[...] + r_ref[...]

            @pl.when(j == 0)
            def _():
                own_ref[...] = s

            @pl.when(j > 0)
            def _():
                send_ref[...] = s.astype(BF)

    blk = lambda tr, C, which: pl.BlockSpec((None, tr, C), lambda i, j, t: (which(j, t), i, 0))
    tiles = [(tr, C) for tr, (_, C) in zip(trs, shapes)]
    grid_spec = pltpu.PrefetchScalarGridSpec(
        num_scalar_prefetch=1, grid=(steps, 4),
        in_specs=[blk(tr, C, lambda j, t: t[j]) for tr, C in tiles] + [blk(tr, C, lambda j, t: t[4 + j]) for tr, C in tiles],
        out_specs=[pl.BlockSpec((tr, C), lambda i, j, t: (i, 0)) for tr, C in tiles]
        + [blk(tr, C, lambda j, t: jnp.maximum(j - 1, 0)) for tr, C in tiles])
    outs = pl.pallas_call(
        body, name=name, grid_spec=grid_spec,
        out_shape=tuple(jax.ShapeDtypeStruct(s, F32) for s in shapes)
        + tuple(jax.ShapeDtypeStruct((3,) + s, BF) for s in shapes),
        compiler_params=_cp(("parallel", "arbitrary")))(idx, *gblocks, *recvs)
    return list(outs[:n]), list(outs[n:])


def _mesh_pos():
    return lax.axis_index("x"), lax.axis_index("y"), lax.axis_index("c")


_HBM = pl.BlockSpec(memory_space=pltpu.HBM)
_SEM = pl.BlockSpec(memory_space=pltpu.SEMAPHORE)
_EFFECT = pltpu.SideEffectType.DATAFLOW_SIDE_EFFECTING
_RELATIONS = [(dx, dy, dc) for dx in (0, 1) for dy in (0, 1) for dc in (0, 1)][1:]


def _flip(v, d):
    return 1 - v if d else v


def _plan_gather(n):
    def plan(x, y, c):
        return [(k, None, 4 * x + 2 * y + c, (_flip(x, dx), _flip(y, dy), _flip(c, dc)))
                for k in range(n) for dx, dy, dc in _RELATIONS]
    return plan


def _plan_gather_near(x, y, c):
    chips = [(1 - x, y), (x, 1 - y), (1 - x, 1 - y)]
    return [(0, None, 4 * x + 2 * y + c, (x, y, 1 - c))] + [(0, None, 4 * x + 2 * y + c, (*chip, c)) for chip in chips]


def _plan_gather_pass(from_landing):
    def plan(x, y, c):
        blocks = [4 * cx + 2 * cy + c for cx, cy in ((1 - x, y), (x, 1 - y), (1 - x, 1 - y))]
        return [(0, b if from_landing else None, b, (x, y, 1 - c)) for b in blocks]
    return plan


def _plan_d2d(n):
    def plan(x, y, c):
        return [(k, 2 * kk + 1 - c, kk, (x, y, 1 - c)) for k in range(n) for kk in range(4)]
    return plan


def _plan_ici(n):
    def plan(x, y, c):
        return [(k, j, j, (_flip(x, dx), _flip(y, dy), c))
                for k in range(n) for j, (dx, dy) in enumerate(((1, 0), (0, 1), (1, 1)))]
    return plan


def _plan_copies(plan, src_refs, land_refs, send_sems, recv_sems):
    x, y, c = _mesh_pos()
    return [pltpu.make_async_remote_copy(
        src_ref=src_refs[k] if si is None else src_refs[k].at[si], dst_ref=land_refs[k].at[di],
        send_sem=send_sems.at[n], recv_sem=recv_sems.at[n], device_id=dev, device_id_type=MESH)
        for n, (k, si, di, dev) in enumerate(plan(x, y, c))]


def _exchange_start(srcs, land_shapes, plan, ncopies, name, after):
    n = len(srcs)

    def body(*refs):
        src_refs, land_refs = refs[:n], refs[n:2 * n]
        send_sems, recv_sems = refs[2 * n + len(after)], refs[2 * n + len(after) + 1]
        token = refs[-1]
        for cp in _plan_copies(plan, src_refs, land_refs, send_sems, recv_sems):
            cp.start()
        token[...] = jnp.zeros_like(token)

    lands = [pltpu.with_memory_space_constraint(lax.empty(s, a.dtype), pltpu.HBM) for s, a in zip(land_shapes, srcs)]
    srcs = [pltpu.with_memory_space_constraint(a, pltpu.HBM) for a in srcs]
    outs = pl.pallas_call(
        body, name=name,
        out_shape=(pltpu.SemaphoreType.DMA((ncopies,)), pltpu.SemaphoreType.DMA((ncopies,)))
        + tuple(pltpu.HBM(a.shape, a.dtype) for a in srcs) + tuple(pltpu.HBM(a.shape, a.dtype) for a in lands)
        + (jax.ShapeDtypeStruct((8, 128), F32),),
        in_specs=[_HBM] * (2 * n) + [pl.BlockSpec(memory_space=pl.ANY)] * len(after),
        out_specs=(_SEM, _SEM) + (_HBM,) * (2 * n) + (pl.BlockSpec(memory_space=pltpu.VMEM),),
        input_output_aliases={i: 2 + i for i in range(2 * n)},
        compiler_params=pltpu.CompilerParams(has_side_effects=_EFFECT))(*srcs, *lands, *after)
    return (outs[0], outs[1], list(outs[2:2 + n]), list(outs[2 + n:2 + 2 * n])), outs[-1]


def _exchange_forward(handle, plan, plan_fwd, needs, name, after):
    send_sems, recv_sems, srcs, lands = handle
    n, nfwd = len(srcs), len(needs)

    def body(*refs):
        src_refs, land_refs = refs[:n], refs[n:2 * n]
        outs = refs[2 * n + 2 + len(after):]
        first = _plan_copies(plan, src_refs, land_refs, refs[2 * n], refs[2 * n + 1])
        for cp, need in zip(_plan_copies(plan_fwd, land_refs, land_refs, outs[0], outs[1]), needs):
            first[need].wait_recv()
            cp.start()
        for k, cp in enumerate(first):
            cp.wait_send()
            if k not in needs:
                cp.wait_recv()

    outs = pl.pallas_call(
        body, name=name,
        out_shape=(pltpu.SemaphoreType.DMA((nfwd,)), pltpu.SemaphoreType.DMA((nfwd,)))
        + tuple(pltpu.HBM(a.shape, a.dtype) for a in srcs) + tuple(pltpu.HBM(a.shape, a.dtype) for a in lands),
        in_specs=[_HBM] * (2 * n) + [_SEM, _SEM] + [pl.BlockSpec(memory_space=pl.ANY)] * len(after),
        out_specs=(_SEM, _SEM) + (_HBM,) * (2 * n), input_output_aliases={i: 2 + i for i in range(2 * n)},
        compiler_params=pltpu.CompilerParams(has_side_effects=_EFFECT))(*srcs, *lands, send_sems, recv_sems, *after)
    return outs[0], outs[1], list(outs[2:2 + n]), list(outs[2 + n:2 + 2 * n])


def _exchange_wait(handle, plan, name, after):
    send_sems, recv_sems, srcs, lands = handle
    n = len(srcs)

    def body(*refs):
        src_refs, land_refs = refs[:n], refs[n:2 * n]
        for cp in _plan_copies(plan, src_refs, land_refs, refs[2 * n], refs[2 * n + 1]):
            cp.wait_send()
            cp.wait_recv()

    outs = pl.pallas_call(
        body, name=name,
        out_shape=tuple(pltpu.HBM(a.shape, a.dtype) for a in srcs) + tuple(pltpu.HBM(a.shape, a.dtype) for a in lands),
        in_specs=[_HBM] * (2 * n) + [_SEM, _SEM] + [pl.BlockSpec(memory_space=pl.ANY)] * len(after),
        out_specs=(_HBM,) * (2 * n), input_output_aliases={i: i for i in range(2 * n)},
        compiler_params=pltpu.CompilerParams(has_side_effects=_EFFECT))(*srcs, *lands, send_sems, recv_sems, *after)
    return list(outs[:n]), list(outs[n:])


SMALL = ("g1", "g2", "g3", "g4", "ln_g", "ln_b", "b_sp", "w_sp")


def _small_sum_adam(block, parts, lands, ws, ms, vs):
    n, ns = len(parts), len(SMALL)

    def body(blk_ref, *refs):
        p_refs, l_refs = refs[:n], refs[n:2 * n]
        w_refs, m_refs, v_refs = (refs[2 * n + i * ns:2 * n + (i + 1) * ns] for i in range(3))
        o = 2 * n + 3 * ns
        loss_ref = refs[o]
        g_out, d_out, m_out, v_out = (refs[o + 1 + i * ns:o + 1 + (i + 1) * ns] for i in range(4))
        me = blk_ref[0]
        sums = []
        for k in range(n):
            acc = jnp.where(me == 0, p_refs[k][...], l_refs[k][0])
            for b in range(1, 8):
                acc = acc + jnp.where(me == b, p_refs[k][...], l_refs[k][b])
            sums.append(acc)
        loss_ref[...] = sums[0]
        for i in range(ns):
            g_out[i][...] = sums[1 + i]
            d_out[i][...], m_out[i][...], v_out[i][...] = _adam_math(w_refs[i][...], sums[1 + i], m_refs[i][...],
                                                                     v_refs[i][...])

    args = list(parts) + list(lands) + [t[k] for t in (ws, ms, vs) for k in SMALL]
    shapes = [jax.ShapeDtypeStruct(p.shape, F32) for p in parts[:1]] + [jax.ShapeDtypeStruct(p.shape, F32) for p in parts[1:]] * 4
    vmem = pl.BlockSpec(memory_space=pltpu.VMEM)
    outs = pl.pallas_call(
        body, name="small_sum_adam", out_shape=tuple(shapes),
        in_specs=[pl.BlockSpec(memory_space=pltpu.SMEM)] + [vmem] * len(args), out_specs=(vmem,) * len(shapes),
        compiler_params=pltpu.CompilerParams(vmem_limit_bytes=VMEM_LIMIT))(block, *args)
    groups = [dict(zip(SMALL, outs[1 + i * ns:1 + (i + 1) * ns])) for i in range(4)]
    return (outs[0], *groups)


def _rope_tables(S):
    half = HD // 2
    inv_freq = jnp.tile(ROPE_THETA ** (-jnp.arange(half, dtype=F32) / half), 4)
    sign = jnp.tile(jnp.concatenate([-jnp.ones(half, F32), jnp.ones(half, F32)]), 2)
    ang = jnp.arange(S, dtype=F32)[:, None] * inv_freq[None, :]
    return jnp.cos(ang), jnp.sin(ang) * sign[None, :]


def _to_blocks(g):
    return g.reshape(8, g.shape[0] // 8, g.shape[1])


def _from_blocks(t, col_sharded):
    if col_sharded:
        return t.transpose(1, 0, 2).reshape(t.shape[1], 8 * t.shape[2])
    return t.reshape(8 * t.shape[1], t.shape[2])


class _NoComm:
    def __init__(self, late_weights):
        self._late = late_weights
        self.grads = {}

    def start_tie(self):
        return jnp.zeros((8, 128), F32)

    def late_weights(self, after):
        return self._late

    def small_start(self, loss_row, grads):
        self.small = (loss_row, grads)
        return jnp.zeros((8, 128), F32)

    def rs_start(self, key, gblocks, after=()):
        self.grads[key] = gblocks
        return jnp.zeros((8, 128), F32)

    def rs_mid(self, key, after):
        return jnp.zeros((8, 128), F32)


class _FsdpComm:
    def __init__(self, late_shards, col_sharded, after, idx, block):
        self._col, self._idx, self._block, self._rs = col_sharded, idx, block, {}
        n = len(late_shards)
        self._gather, self._token = _exchange_start(
            late_shards, [(8,) + s.shape for s in late_shards], _plan_gather(n), 7 * n, "ag_late_start", (after,))

    def start_tie(self):
        return self._token

    def late_weights(self, after):
        shards, lands = _exchange_wait(self._gather, _plan_gather(len(self._col)), "ag_late_wait", after)
        lands = [lax.dynamic_update_index_in_dim(t, s, self._block, 0) for t, s in zip(lands, shards)]
        return [_from_blocks(t, cs) for t, cs in zip(lands, self._col)]

    def small_start(self, loss_row, grads):
        parts = [loss_row] + [grads[k] for k in SMALL]
        n = len(parts)
        self._small, token = _exchange_start(parts, [(8,) + p.shape for p in parts], _plan_gather(n), 7 * n,
                                             "small_start", (self._token,))
        return token

    def small_finish(self, ws, ms, vs, after):
        n = 1 + len(SMALL)
        parts, lands = _exchange_wait(self._small, _plan_gather(n), "small_wait", after)
        return _small_sum_adam(jnp.reshape(self._block, (1,)).astype(jnp.int32), parts, lands, ws, ms, vs)

    def rs_start(self, key, gblocks, after=()):
        n = len(gblocks)
        d2d, token = _exchange_start(gblocks, [(4,) + g.shape[1:] for g in gblocks], _plan_d2d(n), 4 * n,
                                     "rs_%s_d2d_start" % key, (self._token,) + tuple(after))
        self._rs[key] = dict(n=n, d2d=d2d)
        return token

    def rs_mid(self, key, after):
        st = self._rs[key]
        gblocks, from_sib = _exchange_wait(st["d2d"], _plan_d2d(st["n"]), "rs_%s_d2d_wait" % key, after)
        st["own"], sends = _rs_add(gblocks, from_sib, self._idx, "rs_add_" + key)
        st["ici"], token = _exchange_start(sends, [t.shape for t in sends], _plan_ici(st["n"]), 3 * st["n"],
                                           "rs_%s_ici_start" % key, (self._token,))
        return token

    def rs_end(self, key, after):
        st = self._rs[key]
        return st["own"], _exchange_wait(st["ici"], _plan_ici(st["n"]), "rs_%s_ici_wait" % key, after)[1]


def _local_step(x, tgt, h1, cos_t, sin_t, win_t, comm, g1, g2, g3, g4, ln_g, ln_b, w_sp, b_sp):
    b_sp_t = b_sp.T
    w_sp_t = w_sp.transpose(0, 2, 1)

    *qkvs, puz, gates = _proj_all(h1, win_t, cos_t, sin_t, comm.start_tie())
    fwd = _attn_fwd(qkvs)
    ya, *lses = _attn_combine([o for o, _ in fwd], [l for _, l in fwd])
    yg = _gmlp_fwd(puz, ln_g, ln_b, w_sp, b_sp_t)
    wba, wbg, wout, wmi, wmo = comm.late_weights(after=(ya, yg, gates))
    merged, y, x2, h3 = _merge_fwd(ya, yg, gates, wba, wbg, wout, x, g2, g3)
    a, dy2, dout, loss_row, dg4 = _mlp_fwd(h3, wmi, wmo, x2, tgt, g4)

    dpre, dx2, dy, dg3, dg2 = _mlp_bwd(dy2, a, wmo, wmi, x2, y, dout, g2, g3)
    dwmo = _mm_tn(a, dy2, "dw_mlp_out", square_a=True)
    dwmi = _mm_tn(h3, dpre, "dw_mlp_in", col_blocks=8)
    tie = comm.rs_start("mlp", [dwmi, _to_blocks(dwmo)])
    dgates, da, db, dyg, *rest = _outproj_bwd(dy, wout, ya, yg, gates, wba, wbg, tie)
    dyas, dsums = rest[:3], rest[3:]
    tie = comm.rs_mid("mlp", after=(dyg,))
    dpuz, dwsp, dbs, dlng, dlnb = _gmlp_bwd(puz, dyg, ln_g + tie[0, 0], ln_b, w_sp, w_sp_t, b_sp_t)
    dqkvs = _attn_bwd(qkvs, dyas, dsums, lses)
    grad_x, dg1, dproj = _inproj_bwd(dqkvs, dpuz, dgates, win_t, cos_t, sin_t, x, dx2, g1)
    small = dict(g1=dg1, g2=dg2, g3=dg3, g4=dg4, ln_g=dlng, ln_b=dlnb, b_sp=dbs.reshape(4, CHUNK),
                 w_sp=dwsp.reshape(4 * CHUNK, CHUNK))
    tie = comm.small_start(loss_row, small)
    dwin_t = _mm_tn(dproj, h1, "dw_in", tie=tie)
    tie = comm.rs_start("win", [_to_blocks(dwin_t)])
    dwout = _mm_tn(merged, dy, "dw_out", tie=tie)
    tie = comm.rs_mid("win", after=(dwout,))
    dwba = _mm_tn(ya, da, "dw_branch_attn", tie=tie, col_blocks=8)
    dwbg = _mm_tn(yg, db, "dw_branch_gmlp", tie=tie, col_blocks=8)
    tie = comm.rs_start("mid", [dwba, dwbg, _to_blocks(dwout)], after=(tie,))
    return grad_x, tie


def kernel(x, norm_pre_mix, w_in, w_spatial, b_spatial, ln_v_gain, ln_v_bias, w_branch_attn, w_branch_gmlp, w_out, norm_post_mix, norm_pre_mlp, w_mlp_in, w_mlp_out, norm_post_mlp, loss_target, m_norm_pre_mix, m_w_in, m_w_spatial, m_b_spatial, m_ln_v_gain, m_ln_v_bias, m_w_branch_attn, m_w_branch_gmlp, m_w_out, m_norm_post_mix, m_norm_pre_mlp, m_w_mlp_in, m_w_mlp_out, m_norm_post_mlp, v_norm_pre_mix, v_w_in, v_w_spatial, v_b_spatial, v_ln_v_gain, v_ln_v_bias, v_w_branch_attn, v_w_branch_gmlp, v_w_out, v_norm_post_mix, v_norm_pre_mlp, v_w_mlp_in, v_w_mlp_out, v_norm_post_mlp):
    mx, my, mc = _mesh_pos()
    rel = [(0, 0), (1, 0), (0, 1), (1, 1)]
    chip_of = [2 * (mx ^ dx) + (my ^ dy) for dx, dy in rel]
    idx = jnp.stack([2 * k + mc for k in chip_of] + chip_of).astype(jnp.int32)

    w_in_t, m_w_in_t, v_w_in_t = (t[0].T for t in (w_in, m_w_in, v_w_in))
    shard = w_in_t.astype(BF)
    gather, token = _exchange_start([shard], [(8,) + shard.shape], _plan_gather_near, 4, "ag_win_start", ())
    h1 = _rms_fwd(x[0], norm_pre_mix + token[0, 0], "rms_pre_mix")
    cos_t, sin_t = _rope_tables(x.shape[1])
    gather = _exchange_forward(gather, _plan_gather_near, _plan_gather_pass(True), (1, 2, 3), "ag_win_pass",
                               (h1, cos_t, sin_t))
    (shard,), (g_win,) = _exchange_wait(gather, _plan_gather_pass(False), "ag_win_wait", ())
    g_win = lax.dynamic_update_index_in_dim(g_win, shard, 4 * mx + 2 * my + mc, 0)
    win_t = _from_blocks(g_win, False)
    late = [w_branch_attn[0], w_branch_gmlp[0], w_out[0], w_mlp_in[0], w_mlp_out[0]]
    comm = _FsdpComm([w.astype(BF) for w in late], [True, True, False, True, False], g_win, idx, 4 * mx + 2 * my + mc)

    grad_x, tie = _local_step(
        x[0], loss_target[0], h1, cos_t, sin_t, win_t, comm,
        norm_pre_mix, norm_post_mix, norm_pre_mlp, norm_post_mlp, ln_v_gain, ln_v_bias, w_spatial[0], b_spatial[0])

    flat = lambda t: t.reshape(-1, t.shape[-1])
    small_w = dict(zip(SMALL, map(flat, (norm_pre_mix, norm_post_mix, norm_pre_mlp, norm_post_mlp, ln_v_gain, ln_v_bias,
                                         b_spatial, w_spatial))))
    small_m = dict(zip(SMALL, map(flat, (m_norm_pre_mix, m_norm_post_mix, m_norm_pre_mlp, m_norm_post_mlp, m_ln_v_gain,
                                         m_ln_v_bias, m_b_spatial, m_w_spatial))))
    small_v = dict(zip(SMALL, map(flat, (v_norm_pre_mix, v_norm_post_mix, v_norm_pre_mlp, v_norm_post_mlp, v_ln_v_gain,
                                         v_ln_v_bias, v_b_spatial, v_w_spatial))))
    loss_out, sg, sd, sm, sv = comm.small_finish(small_w, small_m, small_v, after=(tie,))
    loss = loss_out[0, 0]
    tie = comm.rs_mid("mid", after=(loss_out,))
    mlp_own, mlp_chips = comm.rs_end("mlp", after=(tie,))
    first = lambda ts: [t[0] for t in ts]
    upd = dict(zip(("w_mlp_in", "w_mlp_out"), _adam_shards(
        mlp_own, mlp_chips, first((w_mlp_in, w_mlp_out)), first((m_w_mlp_in, m_w_mlp_out)),
        first((v_w_mlp_in, v_w_mlp_out)), "adam_mlp")))
    win_own, win_chips = comm.rs_end("win", after=(upd["w_mlp_in"][0], upd["w_mlp_out"][0]))
    upd["w_in"] = tuple(t.T for t in _adam_shards(win_own, win_chips, [w_in_t], [m_w_in_t], [v_w_in_t], "adam_w_in")[0])
    mid_own, mid_chips = comm.rs_end("mid", after=(upd["w_in"][0],))
    upd.update(zip(("w_branch_attn", "w_branch_gmlp", "w_out"), _adam_shards(
        mid_own, mid_chips, first((w_branch_attn, w_branch_gmlp, w_out)),
        first((m_w_branch_attn, m_w_branch_gmlp, m_w_out)), first((v_w_branch_attn, v_w_branch_gmlp, v_w_out)), "adam_mid")))
    order = ["g1", "w_in", "w_sp", "b_sp", "ln_g", "ln_b", "w_branch_attn", "w_branch_gmlp", "w_out", "g2", "g3",
             "w_mlp_in", "w_mlp_out", "g4"]
    small_shape = dict(g1=norm_pre_mix.shape, g2=norm_post_mix.shape, g3=norm_pre_mlp.shape, g4=norm_post_mlp.shape,
                       ln_g=ln_v_gain.shape, ln_b=ln_v_bias.shape, b_sp=b_spatial.shape, w_sp=w_spatial.shape)

    def pick(which):
        return [upd[nm][which][None] if nm in upd else (sg, sd, sm, sv)[which][nm].reshape(small_shape[nm])
                for nm in order]

    return (loss, grad_x[None], *pick(0), *pick(1), *pick(2), *pick(3))
```

```python
import math

import jax
import jax.numpy as jnp
from jax import lax
from jax.experimental import pallas as pl
from jax.experimental.pallas import tpu as pltpu

D = 1024
HD = 64
NSLOT = 4
GW = NSLOT * HD
DILS = (1, 4, 16)
QB = 128
ATTN_NSUB = 4
PROJ_TM = 1024
MLP_TM_FWD = 512
MLP_TM_BWD = 256
MM_TN_TK = 4096
MM_TN_TILE = 1024
RING = 3
ELEMENTWISE_BLOCK_BYTES = 1 << 20
AW = 3 * GW
W_UZ0, W_G0 = 3 * AW, 3 * AW + 1024
GMLP_W = 512
CHUNK = 128
DFF = 4096
EPS = 1e-6
ROPE_THETA = 10000.0
SCALE = HD ** -0.5
NEG = -1e30

ADAM_LR = 0.001
ADAM_B1 = 0.9
ADAM_B2 = 0.999
ADAM_EPS = 1e-08
ADAM_WD = 0.01
ADAM_STEP = 10

BF = jnp.bfloat16
F32 = jnp.float32
MESH = pl.DeviceIdType.MESH
VMEM_LIMIT = 56 * 1024 * 1024


def _cp(sem):
    return pltpu.CompilerParams(dimension_semantics=sem, vmem_limit_bytes=VMEM_LIMIT)


def _dot(a, b):
    return jnp.dot(a, b, preferred_element_type=F32)


def _dot_nt(a, b):
    return lax.dot_general(a, b, (((1,), (1,)), ((), ())), preferred_element_type=F32)


def _dot_tn(a, b):
    return lax.dot_general(a, b, (((0,), (0,)), ((), ())), preferred_element_type=F32)


def _gelu(x):
    return jax.nn.gelu(x, approximate=True)


def _gelu_with_grad(x):
    k = math.sqrt(2.0 / math.pi)
    t = jnp.tanh(k * (x + 0.044715 * (x * x * x)))
    cdf = 0.5 * (1.0 + t)
    return x * cdf, cdf + 0.5 * x * (1.0 - t * t) * (k * (1.0 + 3.0 * 0.044715 * x * x))


def _swap_halves(t):
    w = t.shape[1]
    lane = lax.broadcasted_iota(jnp.int32, t.shape, 1)
    first = (lane & (HD - 1)) < (HD // 2)
    return jnp.where(first, pltpu.roll(t, w - HD // 2, 1), pltpu.roll(t, HD // 2, 1))


def _head_mask(hh):
    lane = lax.broadcasted_iota(jnp.int32, (1, GW), 1)
    return jnp.logical_and(lane >= hh * HD, lane < (hh + 1) * HD)


def _rms_stats(xf):
    return lax.rsqrt(jnp.mean(xf * xf, axis=-1, keepdims=True) + EPS)


def _rms_bwd(xf, r, gain, dout):
    n = xf * r
    t = dout * gain
    dx = r * (t - n * jnp.mean(t * n, axis=-1, keepdims=True))
    return dx, jnp.sum(dout * n, axis=0, keepdims=True)


def _rms_fwd(x, gain, name):
    S = x.shape[0]
    tm = min(512, S)

    def body(x_ref, g_ref, h_ref):
        xf = x_ref[...]
        h_ref[...] = (xf * _rms_stats(xf) * g_ref[...]).astype(BF)

    return pl.pallas_call(
        body, name=name, out_shape=jax.ShapeDtypeStruct((S, D), BF), grid=(S // tm,),
        in_specs=[pl.BlockSpec((tm, D), lambda i: (i, 0)), pl.BlockSpec((1, D), lambda i: (0, 0))],
        out_specs=pl.BlockSpec((tm, D), lambda i: (i, 0)), compiler_params=_cp(("parallel",)))(x, gain)


def _dilate_store(val, scr, o_ref, lead, d):
    rows = val.shape[0]
    if d == 1:
        o_ref[lead + (0,)] = val.astype(o_ref.dtype)
        return
    for hf in range(2):
        scr[hf, pl.ds(0, rows), :] = val[:, hf * 128:(hf + 1) * 128]
    for r in range(d):
        for hf in range(2):
            o_ref[lead + (r, slice(None), slice(hf * 128, (hf + 1) * 128))] = (
                scr[hf, pl.ds(r, rows // d, stride=d), :].astype(o_ref.dtype))


def _undilate_load(i_ref, lead, d, scr, rows):
    if d == 1:
        return i_ref[lead + (0,)].astype(F32)
    for r in range(d):
        for hf in range(2):
            scr[hf, pl.ds(r, rows // d, stride=d), :] = (
                i_ref[lead + (r, slice(None), slice(hf * 128, (hf + 1) * 128))].astype(F32))
    return jnp.concatenate([scr[0, pl.ds(0, rows), :], scr[1, pl.ds(0, rows), :]], axis=1)


def _rope_fwd(y, c_ref, s_ref):
    cosv = jnp.concatenate([c_ref[...]] * 2, axis=1)
    sinv = jnp.concatenate([s_ref[...]] * 2, axis=1)
    return y * cosv + _swap_halves(y) * sinv


def _rope_bwd(dy, c_ref, s_ref):
    cosv = jnp.concatenate([c_ref[...]] * 2, axis=1)
    sinv = jnp.concatenate([s_ref[...]] * 2, axis=1)
    return dy * cosv + _swap_halves(dy * sinv)


def _proj_all(h, win_t, cos_t, sin_t, tie):
    S, K = h.shape
    tm = min(PROJ_TM, S)

    def body(h_ref, w_ref, c_ref, s_ref, tie_ref, o0_ref, o1_ref, o2_ref, p_ref, gt_ref, scr):
        hv = h_ref[...]
        col = lambda b: _dot_nt(hv, w_ref[b * GW:(b + 1) * GW, :])
        ys = [col(3 * t) for t in range(3)]
        for g, (o_ref, d) in enumerate(zip((o0_ref, o1_ref, o2_ref), DILS)):
            nxt = [col(3 * t + g + 1) if g < 2 else col(W_UZ0 // GW + t) for t in range(3)]
            for t in range(3):
                y = _rope_fwd(ys[t], c_ref, s_ref) if t < 2 else ys[t]
                _dilate_store(y, scr.at[t], o_ref, (t,), d)
            ys = nxt
        for b in range((W_G0 - W_UZ0) // GW):
            y = ys[b] if b < 3 else col(W_UZ0 // GW + b)
            p_ref[:, b * GW:(b + 1) * GW] = y.astype(BF)
        for b in range(2 * D // GW):
            gt_ref[:, b * GW:(b + 1) * GW] = jax.nn.sigmoid(col(W_G0 // GW + b)).astype(BF)

    row = lambda w: pl.BlockSpec((tm, w), lambda i: (i, 0))
    return pl.pallas_call(
        body, name="proj_all", grid=(S // tm,),
        out_shape=tuple(jax.ShapeDtypeStruct((3, d, S // d, GW), BF) for d in DILS)
        + (jax.ShapeDtypeStruct((S, W_G0 - W_UZ0), BF), jax.ShapeDtypeStruct((S, 2 * D), BF)),
        in_specs=[row(K), _resident(win_t.shape), row(128), row(128), pl.BlockSpec(memory_space=pl.ANY)],
        out_specs=tuple(pl.BlockSpec((3, d, tm // d, GW), lambda i: (0, 0, i, 0)) for d in DILS)
        + (row(W_G0 - W_UZ0), row(2 * D)),
        scratch_shapes=[pltpu.VMEM((3, 2, tm, 128), F32)],
        compiler_params=_cp(("parallel",)))(h, win_t, cos_t, sin_t, tie)


def _band_masks(first_step):
    row = lax.broadcasted_iota(jnp.int32, (QB, 2 * QB), 0)
    col = lax.broadcasted_iota(jnp.int32, (QB, 2 * QB), 1)
    band = jnp.logical_and(col >= row, col <= row + QB)
    return band, jnp.logical_and(band, jnp.logical_or(col >= QB, jnp.logical_not(first_step)))


def _group_step(k, nsteps):
    return lambda g, n: jnp.where(g == k, n, jnp.where(g < k, 0, nsteps - 1))


def _attn_fwd(qkvs):
    S = qkvs[0].shape[1] * qkvs[0].shape[2]
    nsub = min(ATTN_NSUB, S // max(DILS) // QB)
    R = nsub * QB
    nsteps = S // R

    def group_body(first, q_ref, kp_ref, kc_ref, vp_ref, vc_ref, o_ref, lse_ref):
        band, band_first = _band_masks(first)
        kfull = jnp.concatenate([kp_ref[...], kc_ref[...]], axis=0)
        vfull = jnp.concatenate([vp_ref[...], vc_ref[...]], axis=0)
        chains = [(sb, hh) for sb in range(nsub) for hh in range(NSLOT)]
        win = lambda t, sb: t[sb * QB:(sb + 2) * QB]
        scores = []
        for sb, hh in chains:
            qh = jnp.where(_head_mask(hh), q_ref[sb * QB:(sb + 1) * QB, :], 0)
            scores.append(_dot_nt(qh, win(kfull, sb)))
        soft = []
        for (sb, hh), sc in zip(chains, scores):
            sc = jnp.where(band_first if sb == 0 else band, sc * SCALE, NEG)
            m = jnp.max(sc, axis=1, keepdims=True)
            p = jnp.exp(sc - m)
            den = jnp.sum(p, axis=1, keepdims=True)
            soft.append((p.astype(BF), den, m + jnp.log(den)))
        accs = [_dot(p, win(vfull, sb)) for (sb, hh), (p, _, _) in zip(chains, soft)]
        for sb in range(nsub):
            o = jnp.zeros((QB, GW), F32)
            lse = jnp.zeros((QB, GW), F32)
            for hh in range(NSLOT):
                hm = _head_mask(hh)
                _, den, lrow = soft[sb * NSLOT + hh]
                o = o + jnp.where(hm, accs[sb * NSLOT + hh] / den, 0.0)
                lse = lse + jnp.where(hm, lrow, 0.0)
            o_ref[sb * QB:(sb + 1) * QB, :] = o
            lse_ref[sb * QB:(sb + 1) * QB, :] = lse

    def body(*refs):
        g, n = pl.program_id(0), pl.program_id(1)
        for k, d in enumerate(DILS):
            @pl.when(g == k)
            def _():
                group_body(n % (nsteps // d) == 0, *refs[5 * k:5 * k + 5], *refs[15 + 2 * k:17 + 2 * k])

    in_specs, out_specs, operands = [], [], []
    for k in range(len(DILS)):
        at = _group_step(k, nsteps)
        cur = lambda t, at=at: pl.BlockSpec((None, R, GW), lambda g, n: (t, at(g, n), 0))
        prv = lambda t, at=at: pl.BlockSpec((None, QB, GW), lambda g, n: (t, jnp.maximum(at(g, n) * nsub - 1, 0), 0))
        in_specs += [cur(0), prv(1), cur(1), prv(2), cur(2)]
        out_specs += [pl.BlockSpec((R, GW), lambda g, n, at=at: (at(g, n), 0))] * 2
        operands += [qkvs[k].reshape(3, S, GW)] * 5
    outs = pl.pallas_call(
        body, name="attn_fwd", grid=(len(DILS), nsteps),
        out_shape=(jax.ShapeDtypeStruct((S, GW), F32),) * (2 * len(DILS)),
        in_specs=in_specs, out_specs=tuple(out_specs),
        compiler_params=_cp(("arbitrary", "arbitrary")))(*operands)
    return [(outs[2 * k].reshape(d, S // d, GW), outs[2 * k + 1].reshape(d, S // d, GW)) for k, d in enumerate(DILS)]


def _attn_combine(os_, lses):
    S = os_[0].shape[1]
    tm = min(1024, S)

    def body(o0, o1, o2, l0, l1, l2, y_ref, j0, j1, j2, scr):
        os_nat = [_undilate_load(o, (), d, scr, tm) for o, d in zip((o0, o1, o2), DILS)]
        a, b, c = [_undilate_load(l, (), d, scr, tm) for l, d in zip((l0, l1, l2), DILS)]
        m = jnp.maximum(jnp.maximum(a, b), c)
        wa, wb, wc = jnp.exp(a - m), jnp.exp(b - m), jnp.exp(c - m)
        den = wa + wb + wc
        y_ref[...] = ((wa * os_nat[0] + wb * os_nat[1] + wc * os_nat[2]) / den).astype(BF)
        lse = m + jnp.log(den)
        for j_ref, d in zip((j0, j1, j2), DILS):
            _dilate_store(lse, scr, j_ref, (), d)

    dil = lambda d: pl.BlockSpec((d, tm // d, GW), lambda i: (0, i, 0))
    dshape = lambda d: jax.ShapeDtypeStruct((d, S // d, GW), F32)
    return pl.pallas_call(
        body, name="attn_combine", grid=(S // tm,),
        out_shape=(jax.ShapeDtypeStruct((S, GW), BF),) + tuple(dshape(d) for d in DILS),
        in_specs=[dil(d) for d in DILS] * 2,
        out_specs=(pl.BlockSpec((tm, GW), lambda i: (i, 0)),) + tuple(dil(d) for d in DILS),
        scratch_shapes=[pltpu.VMEM((2, tm, 128), F32)],
        compiler_params=_cp(("parallel",)))(*os_, *lses)


def _tril(upper=False):
    row = lax.broadcasted_iota(jnp.int32, (CHUNK, CHUNK), 0)
    col = lax.broadcasted_iota(jnp.int32, (CHUNK, CHUNK), 1)
    return row <= col if upper else col <= row


def _ln_fwd(z, gain, bias):
    mu = jnp.mean(z, axis=-1, keepdims=True)
    zc = z - mu
    rstd = lax.rsqrt(jnp.mean(zc * zc, axis=-1, keepdims=True) + EPS)
    zhat = zc * rstd
    return zhat, rstd, zhat * gain + bias


def _gmlp_fwd(puz, ln_g, ln_b, w_sp, b_sp_t):
    S = puz.shape[0]
    tm = min(512, S)
    nch = tm // CHUNK

    def body(p_ref, g_ref, b_ref, w_ref, bt_ref, o_ref):
        tril = _tril()
        ws = [jnp.where(tril, w_ref[gg], 0.0).astype(BF) for gg in range(4)]
        for ch in range(nch):
            rows = slice(ch * CHUNK, (ch + 1) * CHUNK)
            z = _gelu(p_ref[rows, GMLP_W:].astype(F32))
            _, _, zn = _ln_fwd(z, g_ref[...], b_ref[...])
            zn = zn.astype(BF)
            for gg in range(4):
                cols = slice(gg * CHUNK, (gg + 1) * CHUNK)
                sz = _dot(ws[gg], zn[:, cols]) + bt_ref[:, gg:gg + 1]
                u = _gelu(p_ref[rows, cols].astype(F32))
                o_ref[rows, cols] = (u * sz).astype(BF)

    return pl.pallas_call(
        body, name="gmlp_fwd", out_shape=jax.ShapeDtypeStruct((S, GMLP_W), BF), grid=(S // tm,),
        in_specs=[pl.BlockSpec((tm, 2 * GMLP_W), lambda i: (i, 0)),
                  pl.BlockSpec((1, GMLP_W), lambda i: (0, 0)), pl.BlockSpec((1, GMLP_W), lambda i: (0, 0)),
                  pl.BlockSpec((4, CHUNK, CHUNK), lambda i: (0, 0, 0)), pl.BlockSpec((CHUNK, 4), lambda i: (0, 0))],
        out_specs=pl.BlockSpec((tm, GMLP_W), lambda i: (i, 0)),
        compiler_params=_cp(("parallel",)))(puz, ln_g, ln_b, w_sp, b_sp_t)


def _merge_fwd(ya, yg, gates, wba, wbg, wout, x, g2, g3):
    S = x.shape[0]
    tm = min(512, S)
    nt = S // tm

    def body(ya_ref, yg_ref, gt_hbm, wba_ref, wbg_ref, wo_ref, x_hbm, g2_ref, g3_ref,
             mg_ref, y_ref, x2_ref, h3_ref, gt_buf, x_buf, sems):
        i = pl.program_id(0)

        def copies(step):
            rows = pl.ds(step * tm, tm)
            return (pltpu.make_async_copy(gt_hbm.at[rows, :], gt_buf.at[step % RING], sems.at[0, step % RING]),
                    pltpu.make_async_copy(x_hbm.at[rows, :], x_buf.at[step % RING], sems.at[1, step % RING]))

        @pl.when(i == 0)
        def _():
            for step in range(min(RING - 1, nt)):
                for cp in copies(step):
                    cp.start()

        @pl.when(i + RING - 1 < nt)
        def _():
            for cp in copies(i + RING - 1):
                cp.start()

        for cp in copies(i):
            cp.wait()
        gt_ref, x_ref = gt_buf.at[i % RING], x_buf.at[i % RING]
        a = _dot(ya_ref[...], wba_ref[...])
        b = _dot(yg_ref[...], wbg_ref[...])
        merged = (gt_ref[:, :D].astype(F32) * a + gt_ref[:, D:].astype(F32) * b).astype(BF)
        mg_ref[...] = merged
        y = _dot(merged, wo_ref[...])
        y_ref[...] = y
        x2 = x_ref[...] + y * _rms_stats(y) * g2_ref[...]
        x2_ref[...] = x2
        h3_ref[...] = (x2 * _rms_stats(x2) * g3_ref[...]).astype(BF)

    row = lambda w: pl.BlockSpec((tm, w), lambda i: (i, 0))
    full = lambda s: pl.BlockSpec(s, lambda i: (0, 0))
    hbm = pl.BlockSpec(memory_space=pl.ANY)
    return pl.pallas_call(
        body, name="merge_fwd", grid=(nt,),
        out_shape=(jax.ShapeDtypeStruct((S, D), BF), jax.ShapeDtypeStruct((S, D), F32), jax.ShapeDtypeStruct((S, D), F32),
                   jax.ShapeDtypeStruct((S, D), BF)),
        in_specs=[row(GW), row(GMLP_W), hbm, full((GW, D)), full((GMLP_W, D)), full((D, D)), hbm,
                  full((1, D)), full((1, D))],
        out_specs=(row(D), row(D), row(D), row(D)),
        scratch_shapes=[pltpu.VMEM((RING, tm, 2 * D), BF), pltpu.VMEM((RING, tm, D), F32),
                        pltpu.SemaphoreType.DMA((2, RING))],
        compiler_params=_cp(("arbitrary",)))(ya, yg, gates, wba, wbg, wout, x, g2, g3)


def _resident(shape):
    return pl.BlockSpec(shape, lambda i: (0,) * len(shape), pipeline_mode=pl.Buffered(1))


def _mlp_fwd(h3, wmi, wmo, x2, tgt, g4):
    S = x2.shape[0]
    tm = min(MLP_TM_FWD, S)

    def body(h_ref, wi_ref, wo_ref, x2_ref, t_ref, g4_ref, a_ref, dy2_ref, dout_ref, loss_ref, dg4_ref):
        @pl.when(pl.program_id(0) == 0)
        def _():
            loss_ref[...] = jnp.zeros_like(loss_ref)
            dg4_ref[...] = jnp.zeros_like(dg4_ref)

        halves = [slice(hh * (tm // 2), (hh + 1) * (tm // 2)) for hh in range(2)]
        acts = []
        for rows in halves:
            a = jnp.maximum(_dot(h_ref[rows, :], wi_ref[...]), 0.0)
            a_ref[rows, :] = a.astype(BF)
            acts.append((a * a).astype(BF))
        y2s = [_dot(a2, wo_ref[...]) for a2 in acts]
        lane = lax.broadcasted_iota(jnp.int32, (1, 128), 1)
        for rows, y2 in zip(halves, y2s):
            r = _rms_stats(y2)
            out = x2_ref[rows, :] + y2 * r * g4_ref[...]
            err = out - t_ref[rows, :]
            tot = jnp.sum(jnp.sum(err * err, axis=1, keepdims=True), axis=0, keepdims=True) * (0.5 / D)
            loss_ref[...] += jnp.where(lane == 0, tot, 0.0)
            dout = err * (1.0 / D)
            dout_ref[rows, :] = dout
            dy2, dg = _rms_bwd(y2, r, g4_ref[...], dout)
            dy2_ref[rows, :] = dy2.astype(BF)
            dg4_ref[...] += dg

    row = pl.BlockSpec((tm, D), lambda i: (i, 0))
    return pl.pallas_call(
        body, name="mlp_fwd", grid=(S // tm,),
        out_shape=(jax.ShapeDtypeStruct((S, DFF), BF), jax.ShapeDtypeStruct((S, D), BF), jax.ShapeDtypeStruct((S, D), F32),
                   jax.ShapeDtypeStruct((1, 128), F32), jax.ShapeDtypeStruct((1, D), F32)),
        in_specs=[row, _resident((D, DFF)), _resident((DFF, D)), row, row, pl.BlockSpec((1, D), lambda i: (0, 0))],
        out_specs=(pl.BlockSpec((tm, DFF), lambda i: (i, 0)), row, row,
                   pl.BlockSpec((1, 128), lambda i: (0, 0)), pl.BlockSpec((1, D), lambda i: (0, 0))),
        compiler_params=_cp(("arbitrary",)))(h3, wmi, wmo, x2, tgt, g4)


def _mlp_bwd(dy2, a, wmo, wmi, x2, y, dout, g2, g3):
    S = x2.shape[0]
    tm = min(MLP_TM_BWD, S)

    def body(dy2_ref, a_ref, wo_ref, wi_ref, x2_ref, y_ref, dout_ref, g2_ref, g3_ref,
             dpre_ref, dx2_ref, dy_ref, dg3_ref, dg2_ref):
        @pl.when(pl.program_id(0) == 0)
        def _():
            dg3_ref[...] = jnp.zeros_like(dg3_ref)
            dg2_ref[...] = jnp.zeros_like(dg2_ref)

        da2 = _dot_nt(dy2_ref[...], wo_ref[...])
        dpre = (2.0 * a_ref[...].astype(F32) * da2).astype(BF)
        dpre_ref[...] = dpre
        dh3 = _dot_nt(dpre, wi_ref[...])
        x2 = x2_ref[...]
        dx3, dg3 = _rms_bwd(x2, _rms_stats(x2), g3_ref[...], dh3)
        dx2 = dout_ref[...] + dx3
        dx2_ref[...] = dx2
        dg3_ref[...] += dg3
        yv = y_ref[...]
        dy, dg2 = _rms_bwd(yv, _rms_stats(yv), g2_ref[...], dx2)
        dy_ref[...] = dy.astype(BF)
        dg2_ref[...] += dg2

    row = pl.BlockSpec((tm, D), lambda i: (i, 0))
    wide = pl.BlockSpec((tm, DFF), lambda i: (i, 0))
    vec = pl.BlockSpec((1, D), lambda i: (0, 0))
    return pl.pallas_call(
        body, name="mlp_bwd", grid=(S // tm,),
        out_shape=(jax.ShapeDtypeStruct((S, DFF), BF), jax.ShapeDtypeStruct((S, D), F32), jax.ShapeDtypeStruct((S, D), BF),
                   jax.ShapeDtypeStruct((1, D), F32), jax.ShapeDtypeStruct((1, D), F32)),
        in_specs=[row, wide, _resident((DFF, D)), _resident((D, DFF)), row, row, row, vec, vec],
        out_specs=(wide, row, row, vec, vec),
        compiler_params=_cp(("arbitrary",)))(dy2, a, wmo, wmi, x2, y, dout, g2, g3)


def _mm_tn(a, b, name, square_a=False, tk=MM_TN_TK, tie=None, col_blocks=None):
    S, M = a.shape
    N = b.shape[1]
    tk = min(tk, S)
    tm = max(t for t in range(128, min(MM_TN_TILE, M) + 1, 128) if M % t == 0)
    tn = max(t for t in range(128, min(MM_TN_TILE, N) + 1, 128) if N % t == 0)
    cb = N // col_blocks if col_blocks else tn
    while (M // tm) * (N // tn) * (S // tk) < 4 and tk % 256 == 0:
        tk //= 2
    assert M % tm == 0 and S % tk == 0 and tn % cb == 0
    nk = S // tk
    ties = () if tie is None else (tie,)

    def body(a_ref, b_ref, *rest):
        o_ref = rest[-1]
        k = pl.program_id(2)
        av = a_ref[...]
        if square_a:
            av = av * av
        part = _dot_tn(av, b_ref[...])
        if col_blocks:
            part = jnp.stack([part[:, t * cb:(t + 1) * cb] for t in range(tn // cb)])

        @pl.when(k == 0)
        def _():
            o_ref[...] = part

        @pl.when(k > 0)
        def _():
            o_ref[...] += part

    if col_blocks:
        out_shape, out_spec = (col_blocks, M, cb), pl.BlockSpec((tn // cb, tm, cb), lambda i, j, k: (j, i, 0))
    else:
        out_shape, out_spec = (M, N), pl.BlockSpec((tm, tn), lambda i, j, k: (i, j))
    return pl.pallas_call(
        body, name=name, out_shape=jax.ShapeDtypeStruct(out_shape, F32), grid=(M // tm, N // tn, nk),
        in_specs=[pl.BlockSpec((tk, tm), lambda i, j, k: (k, i)), pl.BlockSpec((tk, tn), lambda i, j, k: (k, j))]
        + [pl.BlockSpec(memory_space=pl.ANY)] * len(ties),
        out_specs=out_spec,
        compiler_params=_cp(("parallel", "parallel", "arbitrary")))(a, b, *ties)


def _outproj_bwd(dy, wout, ya, yg, gates, wba, wbg, tie):
    S = dy.shape[0]
    tm = min(512, S)

    def body(dy_ref, wo_ref, ya_ref, yg_ref, gt_ref, wba_ref, wbg_ref, tie_ref,
             dgt_ref, da_ref, db_ref, dyg_ref, e0, e1, e2, s0, s1, s2, scr):
        dm = _dot_nt(dy_ref[...], wo_ref[...])
        ga, gb = gt_ref[:, :D].astype(F32), gt_ref[:, D:].astype(F32)
        dgt_ref[:, :D] = (dm * _dot(ya_ref[...], wba_ref[...]) * ga * (1.0 - ga)).astype(BF)
        dgt_ref[:, D:] = (dm * _dot(yg_ref[...], wbg_ref[...]) * gb * (1.0 - gb)).astype(BF)
        da = (dm * ga).astype(BF)
        db = (dm * gb).astype(BF)
        da_ref[...] = da
        db_ref[...] = db
        dyg_ref[...] = _dot_nt(db, wbg_ref[...]).astype(BF)
        dya = _dot_nt(da, wba_ref[...]).astype(BF).astype(F32)
        dyy = dya * ya_ref[...].astype(F32)
        dsum = jnp.zeros((tm, GW), F32)
        for hh in range(NSLOT):
            hm = _head_mask(hh)
            dsum = dsum + jnp.where(hm, jnp.sum(jnp.where(hm, dyy, 0.0), axis=1, keepdims=True), 0.0)
        for e_ref, s_ref, d in zip((e0, e1, e2), (s0, s1, s2), DILS):
            _dilate_store(dya, scr, e_ref, (), d)
            _dilate_store(dsum, scr, s_ref, (), d)

    row = lambda w: pl.BlockSpec((tm, w), lambda i: (i, 0))
    full = lambda s: pl.BlockSpec(s, lambda i: (0, 0))
    dil = lambda d: pl.BlockSpec((d, tm // d, GW), lambda i: (0, i, 0))
    dshape = lambda d, t: jax.ShapeDtypeStruct((d, S // d, GW), t)
    return pl.pallas_call(
        body, name="outproj_bwd", grid=(S // tm,),
        out_shape=(jax.ShapeDtypeStruct((S, 2 * D), BF), jax.ShapeDtypeStruct((S, D), BF), jax.ShapeDtypeStruct((S, D), BF),
                   jax.ShapeDtypeStruct((S, GMLP_W), BF)) + tuple(dshape(d, BF) for d in DILS)
        + tuple(dshape(d, F32) for d in DILS),
        in_specs=[row(D), full((D, D)), row(GW), row(GMLP_W), row(2 * D), full((GW, D)), full((GMLP_W, D)),
                  pl.BlockSpec(memory_space=pl.ANY)],
        out_specs=(row(2 * D), row(D), row(D), row(GMLP_W)) + tuple(dil(d) for d in DILS) * 2,
        scratch_shapes=[pltpu.VMEM((2, tm, 128), F32)],
        compiler_params=_cp(("parallel",)))(dy, wout, ya, yg, gates, wba, wbg, tie)


def _gmlp_bwd(puz, dyg, ln_g, ln_b, w_sp, w_sp_t, b_sp_t):
    S = puz.shape[0]
    tm = min(512, S)
    nch = tm // CHUNK

    def body(p_ref, dy_ref, g_ref, b_ref, w_ref, wt_ref, bt_ref,
             dp_ref, dw_ref, dbs_ref, dg_ref, dbias_ref, dbacc_ref):
        i = pl.program_id(0)

        @pl.when(i == 0)
        def _():
            dw_ref[...] = jnp.zeros_like(dw_ref)
            dbacc_ref[...] = jnp.zeros_like(dbacc_ref)
            dg_ref[...] = jnp.zeros_like(dg_ref)
            dbias_ref[...] = jnp.zeros_like(dbias_ref)

        tril = _tril()
        ws = [jnp.where(tril, w_ref[gg], 0.0).astype(BF) for gg in range(4)]
        triu = _tril(upper=True)
        wts = [jnp.where(triu, wt_ref[gg], 0.0).astype(BF) for gg in range(4)]
        gain = g_ref[...]
        for ch in range(nch):
            rows = slice(ch * CHUNK, (ch + 1) * CHUNK)
            pz = p_ref[rows, GMLP_W:].astype(F32)
            z, z_grad = _gelu_with_grad(pz)
            zhat, rstd, zn = _ln_fwd(z, gain, b_ref[...])
            znb = zn.astype(BF)
            dzn_parts = []
            for gg in range(4):
                cols = slice(gg * CHUNK, (gg + 1) * CHUNK)
                pu = p_ref[rows, cols].astype(F32)
                u, u_grad = _gelu_with_grad(pu)
                sz = _dot(ws[gg], znb[:, cols]) + bt_ref[:, gg:gg + 1]
                dyv = dy_ref[rows, cols].astype(F32)
                dp_ref[rows, cols] = (dyv * sz * u_grad).astype(BF)
                dsz = dyv * u
                dbacc_ref[gg] += dsz
                dszb = dsz.astype(BF)
                dw_ref[gg] += _dot_nt(dszb, znb[:, cols])
                dzn_parts.append(_dot(wts[gg], dszb))
            dzn = jnp.concatenate(dzn_parts, axis=1)
            dg_ref[...] += jnp.sum(dzn * zhat, axis=0, keepdims=True)
            dbias_ref[...] += jnp.sum(dzn, axis=0, keepdims=True)
            dzh = dzn * gain
            dz = rstd * (dzh - jnp.mean(dzh, axis=-1, keepdims=True)
                         - zhat * jnp.mean(dzh * zhat, axis=-1, keepdims=True))
            dp_ref[rows, GMLP_W:] = (dz * z_grad).astype(BF)

        @pl.when(i == pl.num_programs(0) - 1)
        def _():
            for gg in range(4):
                dw_ref[gg] = jnp.where(tril, dw_ref[gg], 0.0)
                dbs_ref[gg] = jnp.sum(dbacc_ref[gg], axis=1, keepdims=True)

    full2 = lambda s: pl.BlockSpec(s, lambda i: (0, 0))
    full3 = lambda s: pl.BlockSpec(s, lambda i: (0, 0, 0))
    return pl.pallas_call(
        body, name="gmlp_bwd", grid=(S // tm,),
        out_shape=(jax.ShapeDtypeStruct((S, 2 * GMLP_W), BF), jax.ShapeDtypeStruct((4, CHUNK, CHUNK), F32),
                   jax.ShapeDtypeStruct((4, CHUNK, 1), F32), jax.ShapeDtypeStruct((1, GMLP_W), F32),
                   jax.ShapeDtypeStruct((1, GMLP_W), F32)),
        in_specs=[pl.BlockSpec((tm, 2 * GMLP_W), lambda i: (i, 0)), pl.BlockSpec((tm, GMLP_W), lambda i: (i, 0)),
                  full2((1, GMLP_W)), full2((1, GMLP_W)), full3((4, CHUNK, CHUNK)), full3((4, CHUNK, CHUNK)),
                  full2((CHUNK, 4))],
        out_specs=(pl.BlockSpec((tm, 2 * GMLP_W), lambda i: (i, 0)), full3((4, CHUNK, CHUNK)), full3((4, CHUNK, 1)),
                   full2((1, GMLP_W)), full2((1, GMLP_W))),
        scratch_shapes=[pltpu.VMEM((4, CHUNK, CHUNK), F32)],
        compiler_params=_cp(("arbitrary",)))(puz, dyg, ln_g, ln_b, w_sp, w_sp_t, b_sp_t)


def _attn_bwd(qkvs, dyas, dsums, lses):
    S = qkvs[0].shape[1] * qkvs[0].shape[2]
    nsub = min(ATTN_NSUB, S // max(DILS) // QB)
    R = nsub * QB
    nsteps = S // R
    NIN = 12

    def group_body(first, last, q_ref, qn_ref, kp_ref, kc_ref, vp_ref, vc_ref, dy_ref, dyn_ref, e_ref, en_ref,
                   l_ref, ln_ref, o_ref):
        band, band_first = _band_masks(first)
        row = lax.broadcasted_iota(jnp.int32, (QB, QB), 0)
        col = lax.broadcasted_iota(jnp.int32, (QB, QB), 1)
        mask_next = jnp.logical_and(col >= row, jnp.logical_not(last))
        kc, vc = kc_ref[...], vc_ref[...]
        kfull = jnp.concatenate([kp_ref[...], kc], axis=0)
        vfull = jnp.concatenate([vp_ref[...], vc], axis=0)
        k_last, v_last = kc[(nsub - 1) * QB:], vc[(nsub - 1) * QB:]
        q_ext = jnp.concatenate([q_ref[...], qn_ref[...]], axis=0)
        dy_ext = jnp.concatenate([dy_ref[...], dyn_ref[...]], axis=0)
        esum, esum_n, lse, lse_n = e_ref[...], en_ref[...], l_ref[...], ln_ref[...]
        win = lambda t, sb: t[sb * QB:(sb + 2) * QB]
        blk = lambda t, sb: t[sb * QB:(sb + 1) * QB]
        hms = [_head_mask(hh) for hh in range(NSLOT)]
        q_hs = [jnp.where(hm, q_ext, 0) for hm in hms]
        dy_hs = [jnp.where(hm, dy_ext, 0) for hm in hms]
        raw = []
        for hh in range(NSLOT):
            tiles = [(_dot_nt(blk(q_hs[hh], sb), win(kfull, sb)), _dot_nt(blk(dy_hs[hh], sb), win(vfull, sb)))
                     for sb in range(nsub)]
            tiles.append((_dot_nt(blk(q_hs[hh], nsub), k_last), _dot_nt(blk(dy_hs[hh], nsub), v_last)))
            raw.append(tiles)
        ps, dss = [], []
        for hh in range(NSLOT):
            rowstat = lambda t: jnp.max(jnp.where(hms[hh], t, -jnp.inf), axis=1, keepdims=True)
            p_h, ds_h = [], []
            for sb in range(nsub + 1):
                sc, dp = raw[hh][sb]
                if sb < nsub:
                    msk, lrow, erow = (band_first if sb == 0 else band), rowstat(blk(lse, sb)), rowstat(blk(esum, sb))
                else:
                    msk, lrow, erow = mask_next, rowstat(lse_n), rowstat(esum_n)
                p = jnp.where(msk, jnp.exp(sc * SCALE - lrow), 0.0)
                p_h.append(p.astype(BF))
                ds_h.append((p * (dp - erow)).astype(BF))
            ps.append(p_h)
            dss.append(ds_h)
        dq = [jnp.zeros((QB, GW), F32) for _ in range(nsub)]
        dk = [jnp.zeros((QB, GW), F32) for _ in range(nsub)]
        dv = [jnp.zeros((QB, GW), F32) for _ in range(nsub)]
        for hh in range(NSLOT):
            for sb in range(nsub):
                dq[sb] = dq[sb] + jnp.where(hms[hh], _dot(dss[hh][sb], win(kfull, sb)), 0.0)
                nxt = lambda t: t[sb + 1][:, :QB] if sb + 1 < nsub else t[nsub]
                dk[sb] = dk[sb] + _dot_tn(jnp.concatenate([dss[hh][sb][:, QB:], nxt(dss[hh])], axis=0), win(q_hs[hh], sb))
                dv[sb] = dv[sb] + _dot_tn(jnp.concatenate([ps[hh][sb][:, QB:], nxt(ps[hh])], axis=0), win(dy_hs[hh], sb))
        for sb in range(nsub):
            rows = slice(sb * QB, (sb + 1) * QB)
            o_ref[0, rows, :] = (dq[sb] * SCALE).astype(BF)
            o_ref[1, rows, :] = (dk[sb] * SCALE).astype(BF)
            o_ref[2, rows, :] = dv[sb].astype(BF)

    def body(*refs):
        g, n = pl.program_id(0), pl.program_id(1)
        for k, d in enumerate(DILS):
            @pl.when(g == k)
            def _():
                per = nsteps // d
                group_body(n % per == 0, n % per == per - 1, *refs[NIN * k:NIN * (k + 1)], refs[NIN * len(DILS) + k])

    in_specs, out_specs, operands = [], [], []
    for k in range(len(DILS)):
        at = _group_step(k, nsteps)
        prev = lambda g, n, at=at: jnp.maximum(at(g, n) * nsub - 1, 0)
        nxt = lambda g, n, at=at: jnp.minimum((at(g, n) + 1) * nsub, S // QB - 1)
        cur4 = lambda t, at=at: pl.BlockSpec((None, R, GW), lambda g, n: (t, at(g, n), 0))
        prv4 = lambda t, prev=prev: pl.BlockSpec((None, QB, GW), lambda g, n: (t, prev(g, n), 0))
        nxt4 = lambda t, nxt=nxt: pl.BlockSpec((None, QB, GW), lambda g, n: (t, nxt(g, n), 0))
        cur3 = pl.BlockSpec((R, GW), lambda g, n, at=at: (at(g, n), 0))
        nxt3 = pl.BlockSpec((QB, GW), lambda g, n, nxt=nxt: (nxt(g, n), 0))
        in_specs += [cur4(0), nxt4(0), prv4(1), cur4(1), prv4(2), cur4(2), cur3, nxt3, cur3, nxt3, cur3, nxt3]
        out_specs.append(pl.BlockSpec((3, R, GW), lambda g, n, at=at: (0, at(g, n), 0)))
        flat = lambda t: t.reshape(S, GW)
        operands += [qkvs[k].reshape(3, S, GW)] * 6 + [flat(dyas[k])] * 2 + [flat(dsums[k])] * 2 + [flat(lses[k])] * 2
    outs = pl.pallas_call(
        body, name="attn_bwd", grid=(len(DILS), nsteps),
        out_shape=(jax.ShapeDtypeStruct((3, S, GW), BF),) * len(DILS),
        in_specs=in_specs, out_specs=tuple(out_specs),
        compiler_params=_cp(("arbitrary", "arbitrary")))(*operands)
    return [t.reshape(3, d, S // d, GW) for t, d in zip(outs, DILS)]


def _inproj_bwd(dqkvs, dpuz, dgates, win_t, cos_t, sin_t, x, dx2, g1):
    S = x.shape[0]
    tm = min(512, S)

    def body(d0_ref, d1_ref, d2_ref, dp_ref, dg_ref, w_ref, c_ref, s_ref,
             x_ref, dx2_ref, g1_ref, gx_ref, dg1_ref, dn_ref, scr):
        i = pl.program_id(0)

        @pl.when(i == 0)
        def _():
            dg1_ref[...] = jnp.zeros_like(dg1_ref)

        dh = _dot(dp_ref[...], w_ref[W_UZ0:W_G0, :]) + _dot(dg_ref[...], w_ref[W_G0:, :])
        dn_ref[:, W_UZ0:W_G0] = dp_ref[...]
        dn_ref[:, W_G0:] = dg_ref[...]
        for t in range(3):
            for g, (d_ref, d) in enumerate(zip((d0_ref, d1_ref, d2_ref), DILS)):
                piece = _undilate_load(d_ref, (t,), d, scr, tm)
                if t < 2:
                    piece = _rope_bwd(piece, c_ref, s_ref)
                dn_ref[:, (3 * t + g) * GW:(3 * t + g + 1) * GW] = piece.astype(BF)
        dh = dh + _dot(dn_ref[:, :W_UZ0], w_ref[:W_UZ0, :])
        xv = x_ref[...]
        dx1, dg1 = _rms_bwd(xv, _rms_stats(xv), g1_ref[...], dh)
        gx_ref[...] = dx2_ref[...] + dx1
        dg1_ref[...] += dg1

    row = lambda w: pl.BlockSpec((tm, w), lambda i: (i, 0))
    full = lambda s: pl.BlockSpec(s, lambda i: (0, 0))
    dil = lambda d: pl.BlockSpec((3, d, tm // d, GW), lambda i: (0, 0, i, 0))
    return pl.pallas_call(
        body, name="inproj_bwd", grid=(S // tm,),
        out_shape=(jax.ShapeDtypeStruct((S, D), F32), jax.ShapeDtypeStruct((1, D), F32),
                   jax.ShapeDtypeStruct((S, win_t.shape[0]), BF)),
        in_specs=[dil(d) for d in DILS] + [row(2 * GMLP_W), row(2 * D), _resident(win_t.shape), row(128), row(128), row(D), row(D),
                                            full((1, D))],
        out_specs=(row(D), full((1, D)), row(win_t.shape[0])),
        scratch_shapes=[pltpu.VMEM((2, tm, 128), F32)],
        compiler_params=_cp(("arbitrary",)))(*dqkvs, dpuz, dgates, win_t, cos_t, sin_t, x, dx2, g1)


def _row_tile(rows, cols):
    cap = max(16, ELEMENTWISE_BLOCK_BYTES // (4 * cols))
    return max(t for t in range(16, cap + 1, 16) if rows % t == 0)


def _adam_math(w, g, m, v):
    m2 = ADAM_B1 * m + (1.0 - ADAM_B1) * g
    v2 = ADAM_B2 * v + (1.0 - ADAM_B2) * (g * g)
    m_hat = m2 / (1.0 - ADAM_B1 ** ADAM_STEP)
    v_hat = v2 / (1.0 - ADAM_B2 ** ADAM_STEP)
    delta = -ADAM_LR * (m_hat / (jnp.sqrt(v_hat) + ADAM_EPS) + ADAM_WD * w)
    return delta, m2, v2


def _row_tiles(shapes):
    steps = min(R // _row_tile(R, C) for R, C in shapes)
    assert all(R % (16 * steps) == 0 for R, _ in shapes)
    return steps, [R // steps for R, _ in shapes]


def _adam_shards(owns, recvs, ws, ms, vs, name):
    n = len(ws)
    steps, trs = _row_tiles([w.shape for w in ws])

    def body(*refs):
        for k in range(n):
            own_ref, r_ref, w_ref, m_ref, v_ref = refs[k:5 * n:n]
            g_ref, d_ref, m2_ref, v2_ref = refs[5 * n + k::n]
            g = own_ref[...] + r_ref[0].astype(F32) + r_ref[1].astype(F32) + r_ref[2].astype(F32)
            g_ref[...] = g
            d_ref[...], m2_ref[...], v2_ref[...] = _adam_math(w_ref[...], g, m_ref[...], v_ref[...])

    specs = [pl.BlockSpec((tr, w.shape[1]), lambda i: (i, 0)) for tr, w in zip(trs, ws)]
    rspecs = [pl.BlockSpec((3, tr, w.shape[1]), lambda i: (0, i, 0)) for tr, w in zip(trs, ws)]
    outs = pl.pallas_call(
        body, name=name, grid=(steps,), out_shape=tuple(jax.ShapeDtypeStruct(w.shape, F32) for w in ws) * 4,
        in_specs=specs + rspecs + specs * 3, out_specs=tuple(specs * 4),
        compiler_params=_cp(("parallel",)))(*owns, *recvs, *ws, *ms, *vs)
    return [tuple(outs[k::n]) for k in range(n)]


def _rs_add(gblocks, recvs, idx, name):
    n = len(gblocks)
    shapes = [g.shape[1:] for g in gblocks]
    steps, trs = _row_tiles(shapes)

    def body(t_ref, *refs):
        j = pl.program_id(1)
        for k in range(n):
            g_ref, r_ref, own_ref, send_ref = refs[k::n]
            s = g_ref[...] + r_ref[...]

            @pl.when(j == 0)
            def _():
                own_ref[...] = s

            @pl.when(j > 0)
            def _():
                send_ref[...] = s.astype(BF)

    blk = lambda tr, C, which: pl.BlockSpec((None, tr, C), lambda i, j, t: (which(j, t), i, 0))
    tiles = [(tr, C) for tr, (_, C) in zip(trs, shapes)]
    grid_spec = pltpu.PrefetchScalarGridSpec(
        num_scalar_prefetch=1, grid=(steps, 4),
        in_specs=[blk(tr, C, lambda j, t: t[j]) for tr, C in tiles] + [blk(tr, C, lambda j, t: t[4 + j]) for tr, C in tiles],
        out_specs=[pl.BlockSpec((tr, C), lambda i, j, t: (i, 0)) for tr, C in tiles]
        + [blk(tr, C, lambda j, t: jnp.maximum(j - 1, 0)) for tr, C in tiles])
    outs = pl.pallas_call(
        body, name=name, grid_spec=grid_spec,
        out_shape=tuple(jax.ShapeDtypeStruct(s, F32) for s in shapes)
        + tuple(jax.ShapeDtypeStruct((3,) + s, BF) for s in shapes),
        compiler_params=_cp(("parallel", "arbitrary")))(idx, *gblocks, *recvs)
    return list(outs[:n]), list(outs[n:])


def _mesh_pos():
    return lax.axis_index("x"), lax.axis_index("y"), lax.axis_index("c")


_HBM = pl.BlockSpec(memory_space=pltpu.HBM)
_SEM = pl.BlockSpec(memory_space=pltpu.SEMAPHORE)
_EFFECT = pltpu.SideEffectType.DATAFLOW_SIDE_EFFECTING
_RELATIONS = [(dx, dy, dc) for dx in (0, 1) for dy in (0, 1) for dc in (0, 1)][1:]


def _flip(v, d):
    return 1 - v if d else v


def _plan_gather(n):
    def plan(x, y, c):
        return [(k, None, 4 * x + 2 * y + c, (_flip(x, dx), _flip(y, dy), _flip(c, dc)))
                for k in range(n) for dx, dy, dc in _RELATIONS]
    return plan


def _plan_gather_near(x, y, c):
    chips = [(1 - x, y), (x, 1 - y), (1 - x, 1 - y)]
    return [(0, None, 4 * x + 2 * y + c, (x, y, 1 - c))] + [(0, None, 4 * x + 2 * y + c, (*chip, c)) for chip in chips]


def _plan_gather_pass(from_landing):
    def plan(x, y, c):
        blocks = [4 * cx + 2 * cy + c for cx, cy in ((1 - x, y), (x, 1 - y), (1 - x, 1 - y))]
        return [(0, b if from_landing else None, b, (x, y, 1 - c)) for b in blocks]
    return plan


def _plan_d2d(n):
    def plan(x, y, c):
        return [(k, 2 * kk + 1 - c, kk, (x, y, 1 - c)) for k in range(n) for kk in range(4)]
    return plan


def _plan_ici(n):
    def plan(x, y, c):
        return [(k, j, j, (_flip(x, dx), _flip(y, dy), c))
                for k in range(n) for j, (dx, dy) in enumerate(((1, 0), (0, 1), (1, 1)))]
    return plan


def _plan_copies(plan, src_refs, land_refs, send_sems, recv_sems):
    x, y, c = _mesh_pos()
    return [pltpu.make_async_remote_copy(
        src_ref=src_refs[k] if si is None else src_refs[k].at[si], dst_ref=land_refs[k].at[di],
        send_sem=send_sems.at[n], recv_sem=recv_sems.at[n], device_id=dev, device_id_type=MESH)
        for n, (k, si, di, dev) in enumerate(plan(x, y, c))]


def _exchange_start(srcs, land_shapes, plan, ncopies, name, after):
    n = len(srcs)

    def body(*refs):
        src_refs, land_refs = refs[:n], refs[n:2 * n]
        send_sems, recv_sems = refs[2 * n + len(after)], refs[2 * n + len(after) + 1]
        token = refs[-1]
        for cp in _plan_copies(plan, src_refs, land_refs, send_sems, recv_sems):
            cp.start()
        token[...] = jnp.zeros_like(token)

    lands = [pltpu.with_memory_space_constraint(lax.empty(s, a.dtype), pltpu.HBM) for s, a in zip(land_shapes, srcs)]
    srcs = [pltpu.with_memory_space_constraint(a, pltpu.HBM) for a in srcs]
    outs = pl.pallas_call(
        body, name=name,
        out_shape=(pltpu.SemaphoreType.DMA((ncopies,)), pltpu.SemaphoreType.DMA((ncopies,)))
        + tuple(pltpu.HBM(a.shape, a.dtype) for a in srcs) + tuple(pltpu.HBM(a.shape, a.dtype) for a in lands)
        + (jax.ShapeDtypeStruct((8, 128), F32),),
        in_specs=[_HBM] * (2 * n) + [pl.BlockSpec(memory_space=pl.ANY)] * len(after),
        out_specs=(_SEM, _SEM) + (_HBM,) * (2 * n) + (pl.BlockSpec(memory_space=pltpu.VMEM),),
        input_output_aliases={i: 2 + i for i in range(2 * n)},
        compiler_params=pltpu.CompilerParams(has_side_effects=_EFFECT))(*srcs, *lands, *after)
    return (outs[0], outs[1], list(outs[2:2 + n]), list(outs[2 + n:2 + 2 * n])), outs[-1]


def _exchange_forward(handle, plan, plan_fwd, needs, name, after):
    send_sems, recv_sems, srcs, lands = handle
    n, nfwd = len(srcs), len(needs)

    def body(*refs):
        src_refs, land_refs = refs[:n], refs[n:2 * n]
        outs = refs[2 * n + 2 + len(after):]
        first = _plan_copies(plan, src_refs, land_refs, refs[2 * n], refs[2 * n + 1])
        for cp, need in zip(_plan_copies(plan_fwd, land_refs, land_refs, outs[0], outs[1]), needs):
            first[need].wait_recv()
            cp.start()
        for k, cp in enumerate(first):
            cp.wait_send()
            if k not in needs:
                cp.wait_recv()

    outs = pl.pallas_call(
        body, name=name,
        out_shape=(pltpu.SemaphoreType.DMA((nfwd,)), pltpu.SemaphoreType.DMA((nfwd,)))
        + tuple(pltpu.HBM(a.shape, a.dtype) for a in srcs) + tuple(pltpu.HBM(a.shape, a.dtype) for a in lands),
        in_specs=[_HBM] * (2 * n) + [_SEM, _SEM] + [pl.BlockSpec(memory_space=pl.ANY)] * len(after),
        out_specs=(_SEM, _SEM) + (_HBM,) * (2 * n), input_output_aliases={i: 2 + i for i in range(2 * n)},
        compiler_params=pltpu.CompilerParams(has_side_effects=_EFFECT))(*srcs, *lands, send_sems, recv_sems, *after)
    return outs[0], outs[1], list(outs[2:2 + n]), list(outs[2 + n:2 + 2 * n])


def _exchange_wait(handle, plan, name, after):
    send_sems, recv_sems, srcs, lands = handle
    n = len(srcs)

    def body(*refs):
        src_refs, land_refs = refs[:n], refs[n:2 * n]
        for cp in _plan_copies(plan, src_refs, land_refs, refs[2 * n], refs[2 * n + 1]):
            cp.wait_send()
            cp.wait_recv()

    outs = pl.pallas_call(
        body, name=name,
        out_shape=tuple(pltpu.HBM(a.shape, a.dtype) for a in srcs) + tuple(pltpu.HBM(a.shape, a.dtype) for a in lands),
        in_specs=[_HBM] * (2 * n) + [_SEM, _SEM] + [pl.BlockSpec(memory_space=pl.ANY)] * len(after),
        out_specs=(_HBM,) * (2 * n), input_output_aliases={i: i for i in range(2 * n)},
        compiler_params=pltpu.CompilerParams(has_side_effects=_EFFECT))(*srcs, *lands, send_sems, recv_sems, *after)
    return list(outs[:n]), list(outs[n:])


SMALL = ("g1", "g2", "g3", "g4", "ln_g", "ln_b", "b_sp", "w_sp")


def _small_sum_adam(block, parts, lands, ws, ms, vs):
    n, ns = len(parts), len(SMALL)

    def body(blk_ref, *refs):
        p_refs, l_refs = refs[:n], refs[n:2 * n]
        w_refs, m_refs, v_refs = (refs[2 * n + i * ns:2 * n + (i + 1) * ns] for i in range(3))
        o = 2 * n + 3 * ns
        loss_ref = refs[o]
        g_out, d_out, m_out, v_out = (refs[o + 1 + i * ns:o + 1 + (i + 1) * ns] for i in range(4))
        me = blk_ref[0]
        sums = []
        for k in range(n):
            acc = jnp.where(me == 0, p_refs[k][...], l_refs[k][0])
            for b in range(1, 8):
                acc = acc + jnp.where(me == b, p_refs[k][...], l_refs[k][b])
            sums.append(acc)
        loss_ref[...] = sums[0]
        for i in range(ns):
            g_out[i][...] = sums[1 + i]
            d_out[i][...], m_out[i][...], v_out[i][...] = _adam_math(w_refs[i][...], sums[1 + i], m_refs[i][...],
                                                                     v_refs[i][...])

    args = list(parts) + list(lands) + [t[k] for t in (ws, ms, vs) for k in SMALL]
    shapes = [jax.ShapeDtypeStruct(p.shape, F32) for p in parts[:1]] + [jax.ShapeDtypeStruct(p.shape, F32) for p in parts[1:]] * 4
    vmem = pl.BlockSpec(memory_space=pltpu.VMEM)
    outs = pl.pallas_call(
        body, name="small_sum_adam", out_shape=tuple(shapes),
        in_specs=[pl.BlockSpec(memory_space=pltpu.SMEM)] + [vmem] * len(args), out_specs=(vmem,) * len(shapes),
        compiler_params=pltpu.CompilerParams(vmem_limit_bytes=VMEM_LIMIT))(block, *args)
    groups = [dict(zip(SMALL, outs[1 + i * ns:1 + (i + 1) * ns])) for i in range(4)]
    return (outs[0], *groups)


def _rope_tables(S):
    half = HD // 2
    inv_freq = jnp.tile(ROPE_THETA ** (-jnp.arange(half, dtype=F32) / half), 4)
    sign = jnp.tile(jnp.concatenate([-jnp.ones(half, F32), jnp.ones(half, F32)]), 2)
    ang = jnp.arange(S, dtype=F32)[:, None] * inv_freq[None, :]
    return jnp.cos(ang), jnp.sin(ang) * sign[None, :]


def _to_blocks(g):
    return g.reshape(8, g.shape[0] // 8, g.shape[1])


def _from_blocks(t, col_sharded):
    if col_sharded:
        return t.transpose(1, 0, 2).reshape(t.shape[1], 8 * t.shape[2])
    return t.reshape(8 * t.shape[1], t.shape[2])


class _NoComm:
    def __init__(self, late_weights):
        self._late = late_weights
        self.grads = {}

    def start_tie(self):
        return jnp.zeros((8, 128), F32)

    def late_weights(self, after):
        return self._late

    def small_start(self, loss_row, grads):
        self.small = (loss_row, grads)
        return jnp.zeros((8, 128), F32)

    def rs_start(self, key, gblocks, after=()):
        self.grads[key] = gblocks
        return jnp.zeros((8, 128), F32)

    def rs_mid(self, key, after):
        return jnp.zeros((8, 128), F32)


class _FsdpComm:
    def __init__(self, late_shards, col_sharded, after, idx, block):
        self._col, self._idx, self._block, self._rs = col_sharded, idx, block, {}
        n = len(late_shards)
        self._gather, self._token = _exchange_start(
            late_shards, [(8,) + s.shape for s in late_shards], _plan_gather(n), 7 * n, "ag_late_start", (after,))

    def start_tie(self):
        return self._token

    def late_weights(self, after):
        shards, lands = _exchange_wait(self._gather, _plan_gather(len(self._col)), "ag_late_wait", after)
        lands = [lax.dynamic_update_index_in_dim(t, s, self._block, 0) for t, s in zip(lands, shards)]
        return [_from_blocks(t, cs) for t, cs in zip(lands, self._col)]

    def small_start(self, loss_row, grads):
        parts = [loss_row] + [grads[k] for k in SMALL]
        n = len(parts)
        self._small, token = _exchange_start(parts, [(8,) + p.shape for p in parts], _plan_gather(n), 7 * n,
                                             "small_start", (self._token,))
        return token

    def small_finish(self, ws, ms, vs, after):
        n = 1 + len(SMALL)
        parts, lands = _exchange_wait(self._small, _plan_gather(n), "small_wait", after)
        return _small_sum_adam(jnp.reshape(self._block, (1,)).astype(jnp.int32), parts, lands, ws, ms, vs)

    def rs_start(self, key, gblocks, after=()):
        n = len(gblocks)
        d2d, token = _exchange_start(gblocks, [(4,) + g.shape[1:] for g in gblocks], _plan_d2d(n), 4 * n,
                                     "rs_%s_d2d_start" % key, (self._token,) + tuple(after))
        self._rs[key] = dict(n=n, d2d=d2d)
        return token

    def rs_mid(self, key, after):
        st = self._rs[key]
        gblocks, from_sib = _exchange_wait(st["d2d"], _plan_d2d(st["n"]), "rs_%s_d2d_wait" % key, after)
        st["own"], sends = _rs_add(gblocks, from_sib, self._idx, "rs_add_" + key)
        st["ici"], token = _exchange_start(sends, [t.shape for t in sends], _plan_ici(st["n"]), 3 * st["n"],
                                           "rs_%s_ici_start" % key, (self._token,))
        return token

    def rs_end(self, key, after):
        st = self._rs[key]
        return st["own"], _exchange_wait(st["ici"], _plan_ici(st["n"]), "rs_%s_ici_wait" % key, after)[1]


def _local_step(x, tgt, h1, cos_t, sin_t, win_t, comm, g1, g2, g3, g4, ln_g, ln_b, w_sp, b_sp):
    b_sp_t = b_sp.T
    w_sp_t = w_sp.transpose(0, 2, 1)

    *qkvs, puz, gates = _proj_all(h1, win_t, cos_t, sin_t, comm.start_tie())
    fwd = _attn_fwd(qkvs)
    ya, *lses = _attn_combine([o for o, _ in fwd], [l for _, l in fwd])
    yg = _gmlp_fwd(puz, ln_g, ln_b, w_sp, b_sp_t)
    wba, wbg, wout, wmi, wmo = comm.late_weights(after=(ya, yg, gates))
    merged, y, x2, h3 = _merge_fwd(ya, yg, gates, wba, wbg, wout, x, g2, g3)
    a, dy2, dout, loss_row, dg4 = _mlp_fwd(h3, wmi, wmo, x2, tgt, g4)

    dpre, dx2, dy, dg3, dg2 = _mlp_bwd(dy2, a, wmo, wmi, x2, y, dout, g2, g3)
    dwmo = _mm_tn(a, dy2, "dw_mlp_out", square_a=True)
    dwmi = _mm_tn(h3, dpre, "dw_mlp_in", col_blocks=8)
    tie = comm.rs_start("mlp", [dwmi, _to_blocks(dwmo)])
    dgates, da, db, dyg, *rest = _outproj_bwd(dy, wout, ya, yg, gates, wba, wbg, tie)
    dyas, dsums = rest[:3], rest[3:]
    tie = comm.rs_mid("mlp", after=(dyg,))
    dpuz, dwsp, dbs, dlng, dlnb = _gmlp_bwd(puz, dyg, ln_g + tie[0, 0], ln_b, w_sp, w_sp_t, b_sp_t)
    dqkvs = _attn_bwd(qkvs, dyas, dsums, lses)
    grad_x, dg1, dproj = _inproj_bwd(dqkvs, dpuz, dgates, win_t, cos_t, sin_t, x, dx2, g1)
    small = dict(g1=dg1, g2=dg2, g3=dg3, g4=dg4, ln_g=dlng, ln_b=dlnb, b_sp=dbs.reshape(4, CHUNK),
                 w_sp=dwsp.reshape(4 * CHUNK, CHUNK))
    tie = comm.small_start(loss_row, small)
    dwin_t = _mm_tn(dproj, h1, "dw_in", tie=tie)
    tie = comm.rs_start("win", [_to_blocks(dwin_t)])
    dwout = _mm_tn(merged, dy, "dw_out", tie=tie)
    tie = comm.rs_mid("win", after=(dwout,))
    dwba = _mm_tn(ya, da, "dw_branch_attn", tie=tie, col_blocks=8)
    dwbg = _mm_tn(yg, db, "dw_branch_gmlp", tie=tie, col_blocks=8)
    tie = comm.rs_start("mid", [dwba, dwbg, _to_blocks(dwout)], after=(tie,))
    return grad_x, tie


def kernel(x, norm_pre_mix, w_in, w_spatial, b_spatial, ln_v_gain, ln_v_bias, w_branch_attn, w_branch_gmlp, w_out, norm_post_mix, norm_pre_mlp, w_mlp_in, w_mlp_out, norm_post_mlp, loss_target, m_norm_pre_mix, m_w_in, m_w_spatial, m_b_spatial, m_ln_v_gain, m_ln_v_bias, m_w_branch_attn, m_w_branch_gmlp, m_w_out, m_norm_post_mix, m_norm_pre_mlp, m_w_mlp_in, m_w_mlp_out, m_norm_post_mlp, v_norm_pre_mix, v_w_in, v_w_spatial, v_b_spatial, v_ln_v_gain, v_ln_v_bias, v_w_branch_attn, v_w_branch_gmlp, v_w_out, v_norm_post_mix, v_norm_pre_mlp, v_w_mlp_in, v_w_mlp_out, v_norm_post_mlp):
    mx, my, mc = _mesh_pos()
    rel = [(0, 0), (1, 0), (0, 1), (1, 1)]
    chip_of = [2 * (mx ^ dx) + (my ^ dy) for dx, dy in rel]
    idx = jnp.stack([2 * k + mc for k in chip_of] + chip_of).astype(jnp.int32)

    w_in_t, m_w_in_t, v_w_in_t = (t[0].T for t in (w_in, m_w_in, v_w_in))
    shard = w_in_t.astype(BF)
    gather, token = _exchange_start([shard], [(8,) + shard.shape], _plan_gather_near, 4, "ag_win_start", ())
    h1 = _rms_fwd(x[0], norm_pre_mix + token[0, 0], "rms_pre_mix")
    cos_t, sin_t = _rope_tables(x.shape[1])
    gather = _exchange_forward(gather, _plan_gather_near, _plan_gather_pass(True), (1, 2, 3), "ag_win_pass",
                               (h1, cos_t, sin_t))
    (shard,), (g_win,) = _exchange_wait(gather, _plan_gather_pass(False), "ag_win_wait", ())
    g_win = lax.dynamic_update_index_in_dim(g_win, shard, 4 * mx + 2 * my + mc, 0)
    win_t = _from_blocks(g_win, False)
    late = [w_branch_attn[0], w_branch_gmlp[0], w_out[0], w_mlp_in[0], w_mlp_out[0]]
    comm = _FsdpComm([w.astype(BF) for w in late], [True, True, False, True, False], g_win, idx, 4 * mx + 2 * my + mc)

    grad_x, tie = _local_step(
        x[0], loss_target[0], h1, cos_t, sin_t, win_t, comm,
        norm_pre_mix, norm_post_mix, norm_pre_mlp, norm_post_mlp, ln_v_gain, ln_v_bias, w_spatial[0], b_spatial[0])

    flat = lambda t: t.reshape(-1, t.shape[-1])
    small_w = dict(zip(SMALL, map(flat, (norm_pre_mix, norm_post_mix, norm_pre_mlp, norm_post_mlp, ln_v_gain, ln_v_bias,
                                         b_spatial, w_spatial))))
    small_m = dict(zip(SMALL, map(flat, (m_norm_pre_mix, m_norm_post_mix, m_norm_pre_mlp, m_norm_post_mlp, m_ln_v_gain,
                                         m_ln_v_bias, m_b_spatial, m_w_spatial))))
    small_v = dict(zip(SMALL, map(flat, (v_norm_pre_mix, v_norm_post_mix, v_norm_pre_mlp, v_norm_post_mlp, v_ln_v_gain,
                                         v_ln_v_bias, v_b_spatial, v_w_spatial))))
    loss_out, sg, sd, sm, sv = comm.small_finish(small_w, small_m, small_v, after=(tie,))
    loss = loss_out[0, 0]
    tie = comm.rs_mid("mid", after=(loss_out,))
    mlp_own, mlp_chips = comm.rs_end("mlp", after=(tie,))
    first = lambda ts: [t[0] for t in ts]
    upd = dict(zip(("w_mlp_in", "w_mlp_out"), _adam_shards(
        mlp_own, mlp_chips, first((w_mlp_in, w_mlp_out)), first((m_w_mlp_in, m_w_mlp_out)),
        first((v_w_mlp_in, v_w_mlp_out)), "adam_mlp")))
    win_own, win_chips = comm.rs_end("win", after=(upd["w_mlp_in"][0], upd["w_mlp_out"][0]))
    upd["w_in"] = tuple(t.T for t in _adam_shards(win_own, win_chips, [w_in_t], [m_w_in_t], [v_w_in_t], "adam_w_in")[0])
    mid_own, mid_chips = comm.rs_end("mid", after=(upd["w_in"][0],))
    upd.update(zip(("w_branch_attn", "w_branch_gmlp", "w_out"), _adam_shards(
        mid_own, mid_chips, first((w_branch_attn, w_branch_gmlp, w_out)),
        first((m_w_branch_attn, m_w_branch_gmlp, m_w_out)), first((v_w_branch_attn, v_w_branch_gmlp, v_w_out)), "adam_mid")))
    order = ["g1", "w_in", "w_sp", "b_sp", "ln_g", "ln_b", "w_branch_attn", "w_branch_gmlp", "w_out", "g2", "g3",
             "w_mlp_in", "w_mlp_out", "g4"]
    small_shape = dict(g1=norm_pre_mix.shape, g2=norm_post_mix.shape, g3=norm_pre_mlp.shape, g4=norm_post_mlp.shape,
                       ln_g=ln_v_gain.shape, ln_b=ln_v_bias.shape, b_sp=b_spatial.shape, w_sp=w_spatial.shape)

    def pick(which):
        return [upd[nm][which][None] if nm in upd else (sg, sd, sm, sv)[which][nm].reshape(small_shape[nm])
                for nm in order]

    return (loss, grad_x[None], *pick(0), *pick(1), *pick(2), *pick(3))
```

```python
import math

import jax
import jax.numpy as jnp
from jax import lax
from jax.experimental import pallas as pl
from jax.experimental.pallas import tpu as pltpu

D = 1024
HD = 64
NSLOT = 4
GW = NSLOT * HD
DILS = (1, 4, 16)
QB = 128
ATTN_NSUB = 4
PROJ_TM = 1024
MLP_TM_FWD = 512
MLP_TM_BWD = 256
MM_TN_TK = 4096
MM_TN_TILE = 1024
RING = 3
ELEMENTWISE_BLOCK_BYTES = 1 << 20
AW = 3 * GW
W_UZ0, W_G0 = 3 * AW, 3 * AW + 1024
GMLP_W = 512
CHUNK = 128
DFF = 4096
EPS = 1e-6
ROPE_THETA = 10000.0
SCALE = HD ** -0.5
NEG = -1e30

ADAM_LR = 0.001
ADAM_B1 = 0.9
ADAM_B2 = 0.999
ADAM_EPS = 1e-08
ADAM_WD = 0.01
ADAM_STEP = 10

BF = jnp.bfloat16
F32 = jnp.float32
MESH = pl.DeviceIdType.MESH
VMEM_LIMIT = 56 * 1024 * 1024


def _cp(sem):
    return pltpu.CompilerParams(dimension_semantics=sem, vmem_limit_bytes=VMEM_LIMIT)


def _dot(a, b):
    return jnp.dot(a, b, preferred_element_type=F32)


def _dot_nt(a, b):
    return lax.dot_general(a, b, (((1,), (1,)), ((), ())), preferred_element_type=F32)


def _dot_tn(a, b):
    return lax.dot_general(a, b, (((0,), (0,)), ((), ())), preferred_element_type=F32)


def _gelu(x):
    return jax.nn.gelu(x, approximate=True)


def _gelu_with_grad(x):
    k = math.sqrt(2.0 / math.pi)
    t = jnp.tanh(k * (x + 0.044715 * (x * x * x)))
    cdf = 0.5 * (1.0 + t)
    return x * cdf, cdf + 0.5 * x * (1.0 - t * t) * (k * (1.0 + 3.0 * 0.044715 * x * x))


def _swap_halves(t):
    w = t.shape[1]
    lane = lax.broadcasted_iota(jnp.int32, t.shape, 1)
    first = (lane & (HD - 1)) < (HD // 2)
    return jnp.where(first, pltpu.roll(t, w - HD // 2, 1), pltpu.roll(t, HD // 2, 1))


def _head_mask(hh):
    lane = lax.broadcasted_iota(jnp.int32, (1, GW), 1)
    return jnp.logical_and(lane >= hh * HD, lane < (hh + 1) * HD)


def _rms_stats(xf):
    return lax.rsqrt(jnp.mean(xf * xf, axis=-1, keepdims=True) + EPS)


def _rms_bwd(xf, r, gain, dout):
    n = xf * r
    t = dout * gain
    dx = r * (t - n * jnp.mean(t * n, axis=-1, keepdims=True))
    return dx, jnp.sum(dout * n, axis=0, keepdims=True)


def _rms_fwd(x, gain, name):
    S = x.shape[0]
    tm = min(512, S)

    def body(x_ref, g_ref, h_ref):
        xf = x_ref[...]
        h_ref[...] = (xf * _rms_stats(xf) * g_ref[...]).astype(BF)

    return pl.pallas_call(
        body, name=name, out_shape=jax.ShapeDtypeStruct((S, D), BF), grid=(S // tm,),
        in_specs=[pl.BlockSpec((tm, D), lambda i: (i, 0)), pl.BlockSpec((1, D), lambda i: (0, 0))],
        out_specs=pl.BlockSpec((tm, D), lambda i: (i, 0)), compiler_params=_cp(("parallel",)))(x, gain)


def _dilate_store(val, scr, o_ref, lead, d):
    rows = val.shape[0]
    if d == 1:
        o_ref[lead + (0,)] = val.astype(o_ref.dtype)
        return
    for hf in range(2):
        scr[hf, pl.ds(0, rows), :] = val[:, hf * 128:(hf + 1) * 128]
    for r in range(d):
        for hf in range(2):
            o_ref[lead + (r, slice(None), slice(hf * 128, (hf + 1) * 128))] = (
                scr[hf, pl.ds(r, rows // d, stride=d), :].astype(o_ref.dtype))


def _undilate_load(i_ref, lead, d, scr, rows):
    if d == 1:
        return i_ref[lead + (0,)].astype(F32)
    for r in range(d):
        for hf in range(2):
            scr[hf, pl.ds(r, rows // d, stride=d), :] = (
                i_ref[lead + (r, slice(None), slice(hf * 128, (hf + 1) * 128))].astype(F32))
    return jnp.concatenate([scr[0, pl.ds(0, rows), :], scr[1, pl.ds(0, rows), :]], axis=1)


def _rope_fwd(y, c_ref, s_ref):
    cosv = jnp.concatenate([c_ref[...]] * 2, axis=1)
    sinv = jnp.concatenate([s_ref[...]] * 2, axis=1)
    return y * cosv + _swap_halves(y) * sinv


def _rope_bwd(dy, c_ref, s_ref):
    cosv = jnp.concatenate([c_ref[...]] * 2, axis=1)
    sinv = jnp.concatenate([s_ref[...]] * 2, axis=1)
    return dy * cosv + _swap_halves(dy * sinv)


def _proj_all(h, win_t, cos_t, sin_t, tie):
    S, K = h.shape
    tm = min(PROJ_TM, S)

    def body(h_ref, w_ref, c_ref, s_ref, tie_ref, o0_ref, o1_ref, o2_ref, p_ref, gt_ref, scr):
        hv = h_ref[...]
        col = lambda b: _dot_nt(hv, w_ref[b * GW:(b + 1) * GW, :])
        ys = [col(3 * t) for t in range(3)]
        for g, (o_ref, d) in enumerate(zip((o0_ref, o1_ref, o2_ref), DILS)):
            nxt = [col(3 * t + g + 1) if g < 2 else col(W_UZ0 // GW + t) for t in range(3)]
            for t in range(3):
                y = _rope_fwd(ys[t], c_ref, s_ref) if t < 2 else ys[t]
                _dilate_store(y, scr.at[t], o_ref, (t,), d)
            ys = nxt
        for b in range((W_G0 - W_UZ0) // GW):
            y = ys[b] if b < 3 else col(W_UZ0 // GW + b)
            p_ref[:, b * GW:(b + 1) * GW] = y.astype(BF)
        for b in range(2 * D // GW):
            gt_ref[:, b * GW:(b + 1) * GW] = jax.nn.sigmoid(col(W_G0 // GW + b)).astype(BF)

    row = lambda w: pl.BlockSpec((tm, w), lambda i: (i, 0))
    return pl.pallas_call(
        body, name="proj_all", grid=(S // tm,),
        out_shape=tuple(jax.ShapeDtypeStruct((3, d, S // d, GW), BF) for d in DILS)
        + (jax.ShapeDtypeStruct((S, W_G0 - W_UZ0), BF), jax.ShapeDtypeStruct((S, 2 * D), BF)),
        in_specs=[row(K), _resident(win_t.shape), row(128), row(128), pl.BlockSpec(memory_space=pl.ANY)],
        out_specs=tuple(pl.BlockSpec((3, d, tm // d, GW), lambda i: (0, 0, i, 0)) for d in DILS)
        + (row(W_G0 - W_UZ0), row(2 * D)),
        scratch_shapes=[pltpu.VMEM((3, 2, tm, 128), F32)],
        compiler_params=_cp(("parallel",)))(h, win_t, cos_t, sin_t, tie)


def _band_masks(first_step):
    row = lax.broadcasted_iota(jnp.int32, (QB, 2 * QB), 0)
    col = lax.broadcasted_iota(jnp.int32, (QB, 2 * QB), 1)
    band = jnp.logical_and(col >= row, col <= row + QB)
    return band, jnp.logical_and(band, jnp.logical_or(col >= QB, jnp.logical_not(first_step)))


def _group_step(k, nsteps):
    return lambda g, n: jnp.where(g == k, n, jnp.where(g < k, 0, nsteps - 1))


def _attn_fwd(qkvs):
    S = qkvs[0].shape[1] * qkvs[0].shape[2]
    nsub = min(ATTN_NSUB, S // max(DILS) // QB)
    R = nsub * QB
    nsteps = S // R

    def group_body(first, q_ref, kp_ref, kc_ref, vp_ref, vc_ref, o_ref, lse_ref):
        band, band_first = _band_masks(first)
        kfull = jnp.concatenate([kp_ref[...], kc_ref[...]], axis=0)
        vfull = jnp.concatenate([vp_ref[...], vc_ref[...]], axis=0)
        chains = [(sb, hh) for sb in range(nsub) for hh in range(NSLOT)]
        win = lambda t, sb: t[sb * QB:(sb + 2) * QB]
        scores = []
        for sb, hh in chains:
            qh = jnp.where(_head_mask(hh), q_ref[sb * QB:(sb + 1) * QB, :], 0)
            scores.append(_dot_nt(qh, win(kfull, sb)))
        soft = []
        for (sb, hh), sc in zip(chains, scores):
            sc = jnp.where(band_first if sb == 0 else band, sc * SCALE, NEG)
            m = jnp.max(sc, axis=1, keepdims=True)
            p = jnp.exp(sc - m)
            den = jnp.sum(p, axis=1, keepdims=True)
            soft.append((p.astype(BF), den, m + jnp.log(den)))
        accs = [_dot(p, win(vfull, sb)) for (sb, hh), (p, _, _) in zip(chains, soft)]
        for sb in range(nsub):
            o = jnp.zeros((QB, GW), F32)
            lse = jnp.zeros((QB, GW), F32)
            for hh in range(NSLOT):
                hm = _head_mask(hh)
                _, den, lrow = soft[sb * NSLOT + hh]
                o = o + jnp.where(hm, accs[sb * NSLOT + hh] / den, 0.0)
                lse = lse + jnp.where(hm, lrow, 0.0)
            o_ref[sb * QB:(sb + 1) * QB, :] = o
            lse_ref[sb * QB:(sb + 1) * QB, :] = lse

    def body(*refs):
        g, n = pl.program_id(0), pl.program_id(1)
        for k, d in enumerate(DILS):
            @pl.when(g == k)
            def _():
                group_body(n % (nsteps // d) == 0, *refs[5 * k:5 * k + 5], *refs[15 + 2 * k:17 + 2 * k])

    in_specs, out_specs, operands = [], [], []
    for k in range(len(DILS)):
        at = _group_step(k, nsteps)
        cur = lambda t, at=at: pl.BlockSpec((None, R, GW), lambda g, n: (t, at(g, n), 0))
        prv = lambda t, at=at: pl.BlockSpec((None, QB, GW), lambda g, n: (t, jnp.maximum(at(g, n) * nsub - 1, 0), 0))
        in_specs += [cur(0), prv(1), cur(1), prv(2), cur(2)]
        out_specs += [pl.BlockSpec((R, GW), lambda g, n, at=at: (at(g, n), 0))] * 2
        operands += [qkvs[k].reshape(3, S, GW)] * 5
    outs = pl.pallas_call(
        body, name="attn_fwd", grid=(len(DILS), nsteps),
        out_shape=(jax.ShapeDtypeStruct((S, GW), F32),) * (2 * len(DILS)),
        in_specs=in_specs, out_specs=tuple(out_specs),
        compiler_params=_cp(("arbitrary", "arbitrary")))(*operands)
    return [(outs[2 * k].reshape(d, S // d, GW), outs[2 * k + 1].reshape(d, S // d, GW)) for k, d in enumerate(DILS)]


def _attn_combine(os_, lses):
    S = os_[0].shape[1]
    tm = min(1024, S)

    def body(o0, o1, o2, l0, l1, l2, y_ref, j0, j1, j2, scr):
        os_nat = [_undilate_load(o, (), d, scr, tm) for o, d in zip((o0, o1, o2), DILS)]
        a, b, c = [_undilate_load(l, (), d, scr, tm) for l, d in zip((l0, l1, l2), DILS)]
        m = jnp.maximum(jnp.maximum(a, b), c)
        wa, wb, wc = jnp.exp(a - m), jnp.exp(b - m), jnp.exp(c - m)
        den = wa + wb + wc
        y_ref[...] = ((wa * os_nat[0] + wb * os_nat[1] + wc * os_nat[2]) / den).astype(BF)
        lse = m + jnp.log(den)
        for j_ref, d in zip((j0, j1, j2), DILS):
            _dilate_store(lse, scr, j_ref, (), d)

    dil = lambda d: pl.BlockSpec((d, tm // d, GW), lambda i: (0, i, 0))
    dshape = lambda d: jax.ShapeDtypeStruct((d, S // d, GW), F32)
    return pl.pallas_call(
        body, name="attn_combine", grid=(S // tm,),
        out_shape=(jax.ShapeDtypeStruct((S, GW), BF),) + tuple(dshape(d) for d in DILS),
        in_specs=[dil(d) for d in DILS] * 2,
        out_specs=(pl.BlockSpec((tm, GW), lambda i: (i, 0)),) + tuple(dil(d) for d in DILS),
        scratch_shapes=[pltpu.VMEM((2, tm, 128), F32)],
        compiler_params=_cp(("parallel",)))(*os_, *lses)


def _tril(upper=False):
    row = lax.broadcasted_iota(jnp.int32, (CHUNK, CHUNK), 0)
    col = lax.broadcasted_iota(jnp.int32, (CHUNK, CHUNK), 1)
    return row <= col if upper else col <= row


def _ln_fwd(z, gain, bias):
    mu = jnp.mean(z, axis=-1, keepdims=True)
    zc = z - mu
    rstd = lax.rsqrt(jnp.mean(zc * zc, axis=-1, keepdims=True) + EPS)
    zhat = zc * rstd
    return zhat, rstd, zhat * gain + bias


def _gmlp_fwd(puz, ln_g, ln_b, w_sp, b_sp_t):
    S = puz.shape[0]
    tm = min(512, S)
    nch = tm // CHUNK

    def body(p_ref, g_ref, b_ref, w_ref, bt_ref, o_ref):
        tril = _tril()
        ws = [jnp.where(tril, w_ref[gg], 0.0).astype(BF) for gg in range(4)]
        for ch in range(nch):
            rows = slice(ch * CHUNK, (ch + 1) * CHUNK)
            z = _gelu(p_ref[rows, GMLP_W:].astype(F32))
            _, _, zn = _ln_fwd(z, g_ref[...], b_ref[...])
            zn = zn.astype(BF)
            for gg in range(4):
                cols = slice(gg * CHUNK, (gg + 1) * CHUNK)
                sz = _dot(ws[gg], zn[:, cols]) + bt_ref[:, gg:gg + 1]
                u = _gelu(p_ref[rows, cols].astype(F32))
                o_ref[rows, cols] = (u * sz).astype(BF)

    return pl.pallas_call(
        body, name="gmlp_fwd", out_shape=jax.ShapeDtypeStruct((S, GMLP_W), BF), grid=(S // tm,),
        in_specs=[pl.BlockSpec((tm, 2 * GMLP_W), lambda i: (i, 0)),
                  pl.BlockSpec((1, GMLP_W), lambda i: (0, 0)), pl.BlockSpec((1, GMLP_W), lambda i: (0, 0)),
                  pl.BlockSpec((4, CHUNK, CHUNK), lambda i: (0, 0, 0)), pl.BlockSpec((CHUNK, 4), lambda i: (0, 0))],
        out_specs=pl.BlockSpec((tm, GMLP_W), lambda i: (i, 0)),
        compiler_params=_cp(("parallel",)))(puz, ln_g, ln_b, w_sp, b_sp_t)


def _merge_fwd(ya, yg, gates, wba, wbg, wout, x, g2, g3):
    S = x.shape[0]
    tm = min(512, S)
    nt = S // tm

    def body(ya_ref, yg_ref, gt_hbm, wba_ref, wbg_ref, wo_ref, x_hbm, g2_ref, g3_ref,
             mg_ref, y_ref, x2_ref, h3_ref, gt_buf, x_buf, sems):
        i = pl.program_id(0)

        def copies(step):
            rows = pl.ds(step * tm, tm)
            return (pltpu.make_async_copy(gt_hbm.at[rows, :], gt_buf.at[step % RING], sems.at[0, step % RING]),
                    pltpu.make_async_copy(x_hbm.at[rows, :], x_buf.at[step % RING], sems.at[1, step % RING]))

        @pl.when(i == 0)
        def _():
            for step in range(min(RING - 1, nt)):
                for cp in copies(step):
                    cp.start()

        @pl.when(i + RING - 1 < nt)
        def _():
            for cp in copies(i + RING - 1):
                cp.start()

        for cp in copies(i):
            cp.wait()
        gt_ref, x_ref = gt_buf.at[i % RING], x_buf.at[i % RING]
        a = _dot(ya_ref[...], wba_ref[...])
        b = _dot(yg_ref[...], wbg_ref[...])
        merged = (gt_ref[:, :D].astype(F32) * a + gt_ref[:, D:].astype(F32) * b).astype(BF)
        mg_ref[...] = merged
        y = _dot(merged, wo_ref[...])
        y_ref[...] = y
        x2 = x_ref[...] + y * _rms_stats(y) * g2_ref[...]
        x2_ref[...] = x2
        h3_ref[...] = (x2 * _rms_stats(x2) * g3_ref[...]).astype(BF)

    row = lambda w: pl.BlockSpec((tm, w), lambda i: (i, 0))
    full = lambda s: pl.BlockSpec(s, lambda i: (0, 0))
    hbm = pl.BlockSpec(memory_space=pl.ANY)
    return pl.pallas_call(
        body, name="merge_fwd", grid=(nt,),
        out_shape=(jax.ShapeDtypeStruct((S, D), BF), jax.ShapeDtypeStruct((S, D), F32), jax.ShapeDtypeStruct((S, D), F32),
                   jax.ShapeDtypeStruct((S, D), BF)),
        in_specs=[row(GW), row(GMLP_W), hbm, full((GW, D)), full((GMLP_W, D)), full((D, D)), hbm,
                  full((1, D)), full((1, D))],
        out_specs=(row(D), row(D), row(D), row(D)),
        scratch_shapes=[pltpu.VMEM((RING, tm, 2 * D), BF), pltpu.VMEM((RING, tm, D), F32),
                        pltpu.SemaphoreType.DMA((2, RING))],
        compiler_params=_cp(("arbitrary",)))(ya, yg, gates, wba, wbg, wout, x, g2, g3)


def _resident(shape):
    return pl.BlockSpec(shape, lambda i: (0,) * len(shape), pipeline_mode=pl.Buffered(1))


def _mlp_fwd(h3, wmi, wmo, x2, tgt, g4):
    S = x2.shape[0]
    tm = min(MLP_TM_FWD, S)

    def body(h_ref, wi_ref, wo_ref, x2_ref, t_ref, g4_ref, a_ref, dy2_ref, dout_ref, loss_ref, dg4_ref):
        @pl.when(pl.program_id(0) == 0)
        def _():
            loss_ref[...] = jnp.zeros_like(loss_ref)
            dg4_ref[...] = jnp.zeros_like(dg4_ref)

        halves = [slice(hh * (tm // 2), (hh + 1) * (tm // 2)) for hh in range(2)]
        acts = []
        for rows in halves:
            a = jnp.maximum(_dot(h_ref[rows, :], wi_ref[...]), 0.0)
            a_ref[rows, :] = a.astype(BF)
            acts.append((a * a).astype(BF))
        y2s = [_dot(a2, wo_ref[...]) for a2 in acts]
        lane = lax.broadcasted_iota(jnp.int32, (1, 128), 1)
        for rows, y2 in zip(halves, y2s):
            r = _rms_stats(y2)
            out = x2_ref[rows, :] + y2 * r * g4_ref[...]
            err = out - t_ref[rows, :]
            tot = jnp.sum(jnp.sum(err * err, axis=1, keepdims=True), axis=0, keepdims=True) * (0.5 / D)
            loss_ref[...] += jnp.where(lane == 0, tot, 0.0)
            dout = err * (1.0 / D)
            dout_ref[rows, :] = dout
            dy2, dg = _rms_bwd(y2, r, g4_ref[...], dout)
            dy2_ref[rows, :] = dy2.astype(BF)
            dg4_ref[...] += dg

    row = pl.BlockSpec((tm, D), lambda i: (i, 0))
    return pl.pallas_call(
        body, name="mlp_fwd", grid=(S // tm,),
        out_shape=(jax.ShapeDtypeStruct((S, DFF), BF), jax.ShapeDtypeStruct((S, D), BF), jax.ShapeDtypeStruct((S, D), F32),
                   jax.ShapeDtypeStruct((1, 128), F32), jax.ShapeDtypeStruct((1, D), F32)),
        in_specs=[row, _resident((D, DFF)), _resident((DFF, D)), row, row, pl.BlockSpec((1, D), lambda i: (0, 0))],
        out_specs=(pl.BlockSpec((tm, DFF), lambda i: (i, 0)), row, row,
                   pl.BlockSpec((1, 128), lambda i: (0, 0)), pl.BlockSpec((1, D), lambda i: (0, 0))),
        compiler_params=_cp(("arbitrary",)))(h3, wmi, wmo, x2, tgt, g4)


def _mlp_bwd(dy2, a, wmo, wmi, x2, y, dout, g2, g3):
    S = x2.shape[0]
    tm = min(MLP_TM_BWD, S)

    def body(dy2_ref, a_ref, wo_ref, wi_ref, x2_ref, y_ref, dout_ref, g2_ref, g3_ref,
             dpre_ref, dx2_ref, dy_ref, dg3_ref, dg2_ref):
        @pl.when(pl.program_id(0) == 0)
        def _():
            dg3_ref[...] = jnp.zeros_like(dg3_ref)
            dg2_ref[...] = jnp.zeros_like(dg2_ref)

        da2 = _dot_nt(dy2_ref[...], wo_ref[...])
        dpre = (2.0 * a_ref[...].astype(F32) * da2).astype(BF)
        dpre_ref[...] = dpre
        dh3 = _dot_nt(dpre, wi_ref[...])
        x2 = x2_ref[...]
        dx3, dg3 = _rms_bwd(x2, _rms_stats(x2), g3_ref[...], dh3)
        dx2 = dout_ref[...] + dx3
        dx2_ref[...] = dx2
        dg3_ref[...] += dg3
        yv = y_ref[...]
        dy, dg2 = _rms_bwd(yv, _rms_stats(yv), g2_ref[...], dx2)
        dy_ref[...] = dy.astype(BF)
        dg2_ref[...] += dg2

    row = pl.BlockSpec((tm, D), lambda i: (i, 0))
    wide = pl.BlockSpec((tm, DFF), lambda i: (i, 0))
    vec = pl.BlockSpec((1, D), lambda i: (0, 0))
    return pl.pallas_call(
        body, name="mlp_bwd", grid=(S // tm,),
        out_shape=(jax.ShapeDtypeStruct((S, DFF), BF), jax.ShapeDtypeStruct((S, D), F32), jax.ShapeDtypeStruct((S, D), BF),
                   jax.ShapeDtypeStruct((1, D), F32), jax.ShapeDtypeStruct((1, D), F32)),
        in_specs=[row, wide, _resident((DFF, D)), _resident((D, DFF)), row, row, row, vec, vec],
        out_specs=(wide, row, row, vec, vec),
        compiler_params=_cp(("arbitrary",)))(dy2, a, wmo, wmi, x2, y, dout, g2, g3)


def _mm_tn(a, b, name, square_a=False, tk=MM_TN_TK, tie=None, col_blocks=None):
    S, M = a.shape
    N = b.shape[1]
    tk = min(tk, S)
    tm = max(t for t in range(128, min(MM_TN_TILE, M) + 1, 128) if M % t == 0)
    tn = max(t for t in range(128, min(MM_TN_TILE, N) + 1, 128) if N % t == 0)
    cb = N // col_blocks if col_blocks else tn
    while (M // tm) * (N // tn) * (S // tk) < 4 and tk % 256 == 0:
        tk //= 2
    assert M % tm == 0 and S % tk == 0 and tn % cb == 0
    nk = S // tk
    ties = () if tie is None else (tie,)

    def body(a_ref, b_ref, *rest):
        o_ref = rest[-1]
        k = pl.program_id(2)
        av = a_ref[...]
        if square_a:
            av = av * av
        part = _dot_tn(av, b_ref[...])
        if col_blocks:
            part = jnp.stack([part[:, t * cb:(t + 1) * cb] for t in range(tn // cb)])

        @pl.when(k == 0)
        def _():
            o_ref[...] = part

        @pl.when(k > 0)
        def _():
            o_ref[...] += part

    if col_blocks:
        out_shape, out_spec = (col_blocks, M, cb), pl.BlockSpec((tn // cb, tm, cb), lambda i, j, k: (j, i, 0))
    else:
        out_shape, out_spec = (M, N), pl.BlockSpec((tm, tn), lambda i, j, k: (i, j))
    return pl.pallas_call(
        body, name=name, out_shape=jax.ShapeDtypeStruct(out_shape, F32), grid=(M // tm, N // tn, nk),
        in_specs=[pl.BlockSpec((tk, tm), lambda i, j, k: (k, i)), pl.BlockSpec((tk, tn), lambda i, j, k: (k, j))]
        + [pl.BlockSpec(memory_space=pl.ANY)] * len(ties),
        out_specs=out_spec,
        compiler_params=_cp(("parallel", "parallel", "arbitrary")))(a, b, *ties)


def _outproj_bwd(dy, wout, ya, yg, gates, wba, wbg, tie):
    S = dy.shape[0]
    tm = min(512, S)

    nt = S // tm

    def body(dy_hbm, wo_ref, ya_ref, yg_ref, gt_hbm, wba_ref, wbg_ref, tie_ref,
             dgt_ref, da_ref, db_ref, dyg_ref, e0, e1, e2, s0, s1, s2, scr, dy_buf, gt_buf, sems):
        i = pl.program_id(0)

        def copies(step):
            rows = pl.ds(step * tm, tm)
            return (pltpu.make_async_copy(dy_hbm.at[rows, :], dy_buf.at[step % RING], sems.at[0, step % RING]),
                    pltpu.make_async_copy(gt_hbm.at[rows, :], gt_buf.at[step % RING], sems.at[1, step % RING]))

        @pl.when(i == 0)
        def _():
            for step in range(min(RING - 1, nt)):
                for cp in copies(step):
                    cp.start()

        @pl.when(i + RING - 1 < nt)
        def _():
            for cp in copies(i + RING - 1):
                cp.start()

        for cp in copies(i):
            cp.wait()
        dy_ref, gt_ref = dy_buf.at[i % RING], gt_buf.at[i % RING]
        dm = _dot_nt(dy_ref[...], wo_ref[...])
        ga, gb = gt_ref[:, :D].astype(F32), gt_ref[:, D:].astype(F32)
        dgt_ref[:, :D] = (dm * _dot(ya_ref[...], wba_ref[...]) * ga * (1.0 - ga)).astype(BF)
        dgt_ref[:, D:] = (dm * _dot(yg_ref[...], wbg_ref[...]) * gb * (1.0 - gb)).astype(BF)
        da = (dm * ga).astype(BF)
        db = (dm * gb).astype(BF)
        da_ref[...] = da
        db_ref[...] = db
        dyg_ref[...] = _dot_nt(db, wbg_ref[...]).astype(BF)
        dya = _dot_nt(da, wba_ref[...]).astype(BF).astype(F32)
        dyy = dya * ya_ref[...].astype(F32)
        dsum = jnp.zeros((tm, GW), F32)
        for hh in range(NSLOT):
            hm = _head_mask(hh)
            dsum = dsum + jnp.where(hm, jnp.sum(jnp.where(hm, dyy, 0.0), axis=1, keepdims=True), 0.0)
        for e_ref, s_ref, d in zip((e0, e1, e2), (s0, s1, s2), DILS):
            _dilate_store(dya, scr, e_ref, (), d)
            _dilate_store(dsum, scr, s_ref, (), d)

    row = lambda w: pl.BlockSpec((tm, w), lambda i: (i, 0))
    full = lambda s: pl.BlockSpec(s, lambda i: (0, 0))
    dil = lambda d: pl.BlockSpec((d, tm // d, GW), lambda i: (0, i, 0))
    dshape = lambda d, t: jax.ShapeDtypeStruct((d, S // d, GW), t)
    return pl.pallas_call(
        body, name="outproj_bwd", grid=(S // tm,),
        out_shape=(jax.ShapeDtypeStruct((S, 2 * D), BF), jax.ShapeDtypeStruct((S, D), BF), jax.ShapeDtypeStruct((S, D), BF),
                   jax.ShapeDtypeStruct((S, GMLP_W), BF)) + tuple(dshape(d, BF) for d in DILS)
        + tuple(dshape(d, F32) for d in DILS),
        in_specs=[pl.BlockSpec(memory_space=pl.ANY), full((D, D)), row(GW), row(GMLP_W),
                  pl.BlockSpec(memory_space=pl.ANY), full((GW, D)), full((GMLP_W, D)), pl.BlockSpec(memory_space=pl.ANY)],
        out_specs=(row(2 * D), row(D), row(D), row(GMLP_W)) + tuple(dil(d) for d in DILS) * 2,
        scratch_shapes=[pltpu.VMEM((2, tm, 128), F32), pltpu.VMEM((RING, tm, D), BF), pltpu.VMEM((RING, tm, 2 * D), BF),
                        pltpu.SemaphoreType.DMA((2, RING))],
        compiler_params=_cp(("arbitrary",)))(dy, wout, ya, yg, gates, wba, wbg, tie)


def _gmlp_bwd(puz, dyg, ln_g, ln_b, w_sp, w_sp_t, b_sp_t):
    S = puz.shape[0]
    tm = min(512, S)
    nch = tm // CHUNK

    def body(p_ref, dy_ref, g_ref, b_ref, w_ref, wt_ref, bt_ref,
             dp_ref, dw_ref, dbs_ref, dg_ref, dbias_ref, dbacc_ref):
        i = pl.program_id(0)

        @pl.when(i == 0)
        def _():
            dw_ref[...] = jnp.zeros_like(dw_ref)
            dbacc_ref[...] = jnp.zeros_like(dbacc_ref)
            dg_ref[...] = jnp.zeros_like(dg_ref)
            dbias_ref[...] = jnp.zeros_like(dbias_ref)

        tril = _tril()
        ws = [jnp.where(tril, w_ref[gg], 0.0).astype(BF) for gg in range(4)]
        triu = _tril(upper=True)
        wts = [jnp.where(triu, wt_ref[gg], 0.0).astype(BF) for gg in range(4)]
        gain = g_ref[...]
        for ch in range(nch):
            rows = slice(ch * CHUNK, (ch + 1) * CHUNK)
            pz = p_ref[rows, GMLP_W:].astype(F32)
            z, z_grad = _gelu_with_grad(pz)
            zhat, rstd, zn = _ln_fwd(z, gain, b_ref[...])
            znb = zn.astype(BF)
            dzn_parts = []
            for gg in range(4):
                cols = slice(gg * CHUNK, (gg + 1) * CHUNK)
                pu = p_ref[rows, cols].astype(F32)
                u, u_grad = _gelu_with_grad(pu)
                sz = _dot(ws[gg], znb[:, cols]) + bt_ref[:, gg:gg + 1]
                dyv = dy_ref[rows, cols].astype(F32)
                dp_ref[rows, cols] = (dyv * sz * u_grad).astype(BF)
                dsz = dyv * u
                dbacc_ref[gg] += dsz
                dszb = dsz.astype(BF)
                dw_ref[gg] += _dot_nt(dszb, znb[:, cols])
                dzn_parts.append(_dot(wts[gg], dszb))
            dzn = jnp.concatenate(dzn_parts, axis=1)
            dg_ref[...] += jnp.sum(dzn * zhat, axis=0, keepdims=True)
            dbias_ref[...] += jnp.sum(dzn, axis=0, keepdims=True)
            dzh = dzn * gain
            dz = rstd * (dzh - jnp.mean(dzh, axis=-1, keepdims=True)
                         - zhat * jnp.mean(dzh * zhat, axis=-1, keepdims=True))
            dp_ref[rows, GMLP_W:] = (dz * z_grad).astype(BF)

        @pl.when(i == pl.num_programs(0) - 1)
        def _():
            for gg in range(4):
                dw_ref[gg] = jnp.where(tril, dw_ref[gg], 0.0)
                dbs_ref[gg] = jnp.sum(dbacc_ref[gg], axis=1, keepdims=True)

    full2 = lambda s: pl.BlockSpec(s, lambda i: (0, 0))
    full3 = lambda s: pl.BlockSpec(s, lambda i: (0, 0, 0))
    return pl.pallas_call(
        body, name="gmlp_bwd", grid=(S // tm,),
        out_shape=(jax.ShapeDtypeStruct((S, 2 * GMLP_W), BF), jax.ShapeDtypeStruct((4, CHUNK, CHUNK), F32),
                   jax.ShapeDtypeStruct((4, CHUNK, 1), F32), jax.ShapeDtypeStruct((1, GMLP_W), F32),
                   jax.ShapeDtypeStruct((1, GMLP_W), F32)),
        in_specs=[pl.BlockSpec((tm, 2 * GMLP_W), lambda i: (i, 0)), pl.BlockSpec((tm, GMLP_W), lambda i: (i, 0)),
                  full2((1, GMLP_W)), full2((1, GMLP_W)), full3((4, CHUNK, CHUNK)), full3((4, CHUNK, CHUNK)),
                  full2((CHUNK, 4))],
        out_specs=(pl.BlockSpec((tm, 2 * GMLP_W), lambda i: (i, 0)), full3((4, CHUNK, CHUNK)), full3((4, CHUNK, 1)),
                   full2((1, GMLP_W)), full2((1, GMLP_W))),
        scratch_shapes=[pltpu.VMEM((4, CHUNK, CHUNK), F32)],
        compiler_params=_cp(("arbitrary",)))(puz, dyg, ln_g, ln_b, w_sp, w_sp_t, b_sp_t)


def _attn_bwd(qkvs, dyas, dsums, lses):
    S = qkvs[0].shape[1] * qkvs[0].shape[2]
    nsub = min(ATTN_NSUB, S // max(DILS) // QB)
    R = nsub * QB
    nsteps = S // R
    NIN = 12

    def group_body(first, last, q_ref, qn_ref, kp_ref, kc_ref, vp_ref, vc_ref, dy_ref, dyn_ref, e_ref, en_ref,
                   l_ref, ln_ref, o_ref):
        band, band_first = _band_masks(first)
        row = lax.broadcasted_iota(jnp.int32, (QB, QB), 0)
        col = lax.broadcasted_iota(jnp.int32, (QB, QB), 1)
        mask_next = jnp.logical_and(col >= row, jnp.logical_not(last))
        kc, vc = kc_ref[...], vc_ref[...]
        kfull = jnp.concatenate([kp_ref[...], kc], axis=0)
        vfull = jnp.concatenate([vp_ref[...], vc], axis=0)
        k_last, v_last = kc[(nsub - 1) * QB:], vc[(nsub - 1) * QB:]
        q_ext = jnp.concatenate([q_ref[...], qn_ref[...]], axis=0)
        dy_ext = jnp.concatenate([dy_ref[...], dyn_ref[...]], axis=0)
        esum, esum_n, lse, lse_n = e_ref[...], en_ref[...], l_ref[...], ln_ref[...]
        win = lambda t, sb: t[sb * QB:(sb + 2) * QB]
        blk = lambda t, sb: t[sb * QB:(sb + 1) * QB]
        hms = [_head_mask(hh) for hh in range(NSLOT)]
        q_hs = [jnp.where(hm, q_ext, 0) for hm in hms]
        dy_hs = [jnp.where(hm, dy_ext, 0) for hm in hms]
        raw = []
        for hh in range(NSLOT):
            tiles = [(_dot_nt(blk(q_hs[hh], sb), win(kfull, sb)), _dot_nt(blk(dy_hs[hh], sb), win(vfull, sb)))
                     for sb in range(nsub)]
            tiles.append((_dot_nt(blk(q_hs[hh], nsub), k_last), _dot_nt(blk(dy_hs[hh], nsub), v_last)))
            raw.append(tiles)
        ps, dss = [], []
        for hh in range(NSLOT):
            rowstat = lambda t: jnp.max(jnp.where(hms[hh], t, -jnp.inf), axis=1, keepdims=True)
            p_h, ds_h = [], []
            for sb in range(nsub + 1):
                sc, dp = raw[hh][sb]
                if sb < nsub:
                    msk, lrow, erow = (band_first if sb == 0 else band), rowstat(blk(lse, sb)), rowstat(blk(esum, sb))
                else:
                    msk, lrow, erow = mask_next, rowstat(lse_n), rowstat(esum_n)
                p = jnp.where(msk, jnp.exp(sc * SCALE - lrow), 0.0)
                p_h.append(p.astype(BF))
                ds_h.append((p * (dp - erow)).astype(BF))
            ps.append(p_h)
            dss.append(ds_h)
        dq = [jnp.zeros((QB, GW), F32) for _ in range(nsub)]
        dk = [jnp.zeros((QB, GW), F32) for _ in range(nsub)]
        dv = [jnp.zeros((QB, GW), F32) for _ in range(nsub)]
        for hh in range(NSLOT):
            for sb in range(nsub):
                dq[sb] = dq[sb] + jnp.where(hms[hh], _dot(dss[hh][sb], win(kfull, sb)), 0.0)
                nxt = lambda t: t[sb + 1][:, :QB] if sb + 1 < nsub else t[nsub]
                dk[sb] = dk[sb] + _dot_tn(jnp.concatenate([dss[hh][sb][:, QB:], nxt(dss[hh])], axis=0), win(q_hs[hh], sb))
                dv[sb] = dv[sb] + _dot_tn(jnp.concatenate([ps[hh][sb][:, QB:], nxt(ps[hh])], axis=0), win(dy_hs[hh], sb))
        for sb in range(nsub):
            rows = slice(sb * QB, (sb + 1) * QB)
            o_ref[0, rows, :] = (dq[sb] * SCALE).astype(BF)
            o_ref[1, rows, :] = (dk[sb] * SCALE).astype(BF)
            o_ref[2, rows, :] = dv[sb].astype(BF)

    def body(*refs):
        g, n = pl.program_id(0), pl.program_id(1)
        for k, d in enumerate(DILS):
            @pl.when(g == k)
            def _():
                per = nsteps // d
                group_body(n % per == 0, n % per == per - 1, *refs[NIN * k:NIN * (k + 1)], refs[NIN * len(DILS) + k])

    in_specs, out_specs, operands = [], [], []
    for k in range(len(DILS)):
        at = _group_step(k, nsteps)
        prev = lambda g, n, at=at: jnp.maximum(at(g, n) * nsub - 1, 0)
        nxt = lambda g, n, at=at: jnp.minimum((at(g, n) + 1) * nsub, S // QB - 1)
        cur4 = lambda t, at=at: pl.BlockSpec((None, R, GW), lambda g, n: (t, at(g, n), 0))
        prv4 = lambda t, prev=prev: pl.BlockSpec((None, QB, GW), lambda g, n: (t, prev(g, n), 0))
        nxt4 = lambda t, nxt=nxt: pl.BlockSpec((None, QB, GW), lambda g, n: (t, nxt(g, n), 0))
        cur3 = pl.BlockSpec((R, GW), lambda g, n, at=at: (at(g, n), 0))
        nxt3 = pl.BlockSpec((QB, GW), lambda g, n, nxt=nxt: (nxt(g, n), 0))
        in_specs += [cur4(0), nxt4(0), prv4(1), cur4(1), prv4(2), cur4(2), cur3, nxt3, cur3, nxt3, cur3, nxt3]
        out_specs.append(pl.BlockSpec((3, R, GW), lambda g, n, at=at: (0, at(g, n), 0)))
        flat = lambda t: t.reshape(S, GW)
        operands += [qkvs[k].reshape(3, S, GW)] * 6 + [flat(dyas[k])] * 2 + [flat(dsums[k])] * 2 + [flat(lses[k])] * 2
    outs = pl.pallas_call(
        body, name="attn_bwd", grid=(len(DILS), nsteps),
        out_shape=(jax.ShapeDtypeStruct((3, S, GW), BF),) * len(DILS),
        in_specs=in_specs, out_specs=tuple(out_specs),
        compiler_params=_cp(("arbitrary", "arbitrary")))(*operands)
    return [t.reshape(3, d, S // d, GW) for t, d in zip(outs, DILS)]


def _inproj_bwd(dqkvs, dpuz, dgates, win_t, cos_t, sin_t, x, dx2, g1):
    S = x.shape[0]
    tm = min(512, S)

    def body(d0_ref, d1_ref, d2_ref, dp_ref, dg_ref, w_ref, c_ref, s_ref,
             x_ref, dx2_ref, g1_ref, gx_ref, dg1_ref, dn_ref, scr):
        i = pl.program_id(0)

        @pl.when(i == 0)
        def _():
            dg1_ref[...] = jnp.zeros_like(dg1_ref)

        dh = _dot(dp_ref[...], w_ref[W_UZ0:W_G0, :]) + _dot(dg_ref[...], w_ref[W_G0:, :])
        dn_ref[:, W_UZ0:W_G0] = dp_ref[...]
        dn_ref[:, W_G0:] = dg_ref[...]
        for t in range(3):
            for g, (d_ref, d) in enumerate(zip((d0_ref, d1_ref, d2_ref), DILS)):
                piece = _undilate_load(d_ref, (t,), d, scr, tm)
                if t < 2:
                    piece = _rope_bwd(piece, c_ref, s_ref)
                dn_ref[:, (3 * t + g) * GW:(3 * t + g + 1) * GW] = piece.astype(BF)
        dh = dh + _dot(dn_ref[:, :W_UZ0], w_ref[:W_UZ0, :])
        xv = x_ref[...]
        dx1, dg1 = _rms_bwd(xv, _rms_stats(xv), g1_ref[...], dh)
        gx_ref[...] = dx2_ref[...] + dx1
        dg1_ref[...] += dg1

    row = lambda w: pl.BlockSpec((tm, w), lambda i: (i, 0))
    full = lambda s: pl.BlockSpec(s, lambda i: (0, 0))
    dil = lambda d: pl.BlockSpec((3, d, tm // d, GW), lambda i: (0, 0, i, 0))
    return pl.pallas_call(
        body, name="inproj_bwd", grid=(S // tm,),
        out_shape=(jax.ShapeDtypeStruct((S, D), F32), jax.ShapeDtypeStruct((1, D), F32),
                   jax.ShapeDtypeStruct((S, win_t.shape[0]), BF)),
        in_specs=[dil(d) for d in DILS] + [row(2 * GMLP_W), row(2 * D), _resident(win_t.shape), row(128), row(128), row(D), row(D),
                                            full((1, D))],
        out_specs=(row(D), full((1, D)), row(win_t.shape[0])),
        scratch_shapes=[pltpu.VMEM((2, tm, 128), F32)],
        compiler_params=_cp(("arbitrary",)))(*dqkvs, dpuz, dgates, win_t, cos_t, sin_t, x, dx2, g1)


def _row_tile(rows, cols):
    cap = max(16, ELEMENTWISE_BLOCK_BYTES // (4 * cols))
    return max(t for t in range(16, cap + 1, 16) if rows % t == 0)


def _adam_math(w, g, m, v):
    m2 = ADAM_B1 * m + (1.0 - ADAM_B1) * g
    v2 = ADAM_B2 * v + (1.0 - ADAM_B2) * (g * g)
    m_hat = m2 / (1.0 - ADAM_B1 ** ADAM_STEP)
    v_hat = v2 / (1.0 - ADAM_B2 ** ADAM_STEP)
    delta = -ADAM_LR * (m_hat / (jnp.sqrt(v_hat) + ADAM_EPS) + ADAM_WD * w)
    return delta, m2, v2


def _row_tiles(shapes):
    steps = min(R // _row_tile(R, C) for R, C in shapes)
    assert all(R % (16 * steps) == 0 for R, _ in shapes)
    return steps, [R // steps for R, _ in shapes]


def _adam_shards(owns, recvs, ws, ms, vs, name):
    n = len(ws)
    steps, trs = _row_tiles([w.shape for w in ws])

    def body(*refs):
        for k in range(n):
            own_ref, r_ref, w_ref, m_ref, v_ref = refs[k:5 * n:n]
            g_ref, d_ref, m2_ref, v2_ref = refs[5 * n + k::n]
            g = own_ref[...] + r_ref[0].astype(F32) + r_ref[1].astype(F32) + r_ref[2].astype(F32)
            g_ref[...] = g
            d_ref[...], m2_ref[...], v2_ref[...] = _adam_math(w_ref[...], g, m_ref[...], v_ref[...])

    specs = [pl.BlockSpec((tr, w.shape[1]), lambda i: (i, 0)) for tr, w in zip(trs, ws)]
    rspecs = [pl.BlockSpec((3, tr, w.shape[1]), lambda i: (0, i, 0)) for tr, w in zip(trs, ws)]
    outs = pl.pallas_call(
        body, name=name, grid=(steps,), out_shape=tuple(jax.ShapeDtypeStruct(w.shape, F32) for w in ws) * 4,
        in_specs=specs + rspecs + specs * 3, out_specs=tuple(specs * 4),
        compiler_params=_cp(("parallel",)))(*owns, *recvs, *ws, *ms, *vs)
    return [tuple(outs[k::n]) for k in range(n)]


def _rs_add(gblocks, recvs, idx, name):
    n = len(gblocks)
    shapes = [g.shape[1:] for g in gblocks]
    steps, trs = _row_tiles(shapes)

    def body(t_ref, *refs):
        j = pl.program_id(1)
        for k in range(n):
            g_ref, r_ref, own_ref, send_ref = refs[k::n]
            s = g_ref[...] + r_ref[...]

            @pl.when(j == 0)
            def _():
                own_ref[...] = s

            @pl.when(j > 0)
            def _():
                send_ref[...] = s.astype(BF)

    blk = lambda tr, C, which: pl.BlockSpec((None, tr, C), lambda i, j, t: (which(j, t), i, 0))
    tiles = [(tr, C) for tr, (_, C) in zip(trs, shapes)]
    grid_spec = pltpu.PrefetchScalarGridSpec(
        num_scalar_prefetch=1, grid=(steps, 4),
        in_specs=[blk(tr, C, lambda j, t: t[j]) for tr, C in tiles] + [blk(tr, C, lambda j, t: t[4 + j]) for tr, C in tiles],
        out_specs=[pl.BlockSpec((tr, C), lambda i, j, t: (i, 0)) for tr, C in tiles]
        + [blk(tr, C, lambda j, t: jnp.maximum(j - 1, 0)) for tr, C in tiles])
    outs = pl.pallas_call(
        body, name=name, grid_spec=grid_spec,
        out_shape=tuple(jax.ShapeDtypeStruct(s, F32) for s in shapes)
        + tuple(jax.ShapeDtypeStruct((3,) + s, BF) for s in shapes),
        compiler_params=_cp(("parallel", "arbitrary")))(idx, *gblocks, *recvs)
    return list(outs[:n]), list(outs[n:])


def _mesh_pos():
    return lax.axis_index("x"), lax.axis_index("y"), lax.axis_index("c")


_HBM = pl.BlockSpec(memory_space=pltpu.HBM)
_SEM = pl.BlockSpec(memory_space=pltpu.SEMAPHORE)
_EFFECT = pltpu.SideEffectType.DATAFLOW_SIDE_EFFECTING
_RELATIONS = [(dx, dy, dc) for dx in (0, 1) for dy in (0, 1) for dc in (0, 1)][1:]


def _flip(v, d):
    return 1 - v if d else v


def _plan_gather(n):
    def plan(x, y, c):
        return [(k, None, 4 * x + 2 * y + c, (_flip(x, dx), _flip(y, dy), _flip(c, dc)))
                for k in range(n) for dx, dy, dc in _RELATIONS]
    return plan


def _plan_gather_near(x, y, c):
    chips = [(1 - x, y), (x, 1 - y), (1 - x, 1 - y)]
    return [(0, None, 4 * x + 2 * y + c, (x, y, 1 - c))] + [(0, None, 4 * x + 2 * y + c, (*chip, c)) for chip in chips]


def _plan_gather_pass(from_landing):
    def plan(x, y, c):
        blocks = [4 * cx + 2 * cy + c for cx, cy in ((1 - x, y), (x, 1 - y), (1 - x, 1 - y))]
        return [(0, b if from_landing else None, b, (x, y, 1 - c)) for b in blocks]
    return plan


def _plan_d2d(n):
    def plan(x, y, c):
        return [(k, 2 * kk + 1 - c, kk, (x, y, 1 - c)) for k in range(n) for kk in range(4)]
    return plan


def _plan_ici(n):
    def plan(x, y, c):
        return [(k, j, j, (_flip(x, dx), _flip(y, dy), c))
                for k in range(n) for j, (dx, dy) in enumerate(((1, 0), (0, 1), (1, 1)))]
    return plan


def _plan_copies(plan, src_refs, land_refs, send_sems, recv_sems):
    x, y, c = _mesh_pos()
    return [pltpu.make_async_remote_copy(
        src_ref=src_refs[k] if si is None else src_refs[k].at[si], dst_ref=land_refs[k].at[di],
        send_sem=send_sems.at[n], recv_sem=recv_sems.at[n], device_id=dev, device_id_type=MESH)
        for n, (k, si, di, dev) in enumerate(plan(x, y, c))]


def _exchange_start(srcs, land_shapes, plan, ncopies, name, after):
    n = len(srcs)

    def body(*refs):
        src_refs, land_refs = refs[:n], refs[n:2 * n]
        send_sems, recv_sems = refs[2 * n + len(after)], refs[2 * n + len(after) + 1]
        token = refs[-1]
        for cp in _plan_copies(plan, src_refs, land_refs, send_sems, recv_sems):
            cp.start()
        token[...] = jnp.zeros_like(token)

    lands = [pltpu.with_memory_space_constraint(lax.empty(s, a.dtype), pltpu.HBM) for s, a in zip(land_shapes, srcs)]
    srcs = [pltpu.with_memory_space_constraint(a, pltpu.HBM) for a in srcs]
    outs = pl.pallas_call(
        body, name=name,
        out_shape=(pltpu.SemaphoreType.DMA((ncopies,)), pltpu.SemaphoreType.DMA((ncopies,)))
        + tuple(pltpu.HBM(a.shape, a.dtype) for a in srcs) + tuple(pltpu.HBM(a.shape, a.dtype) for a in lands)
        + (jax.ShapeDtypeStruct((8, 128), F32),),
        in_specs=[_HBM] * (2 * n) + [pl.BlockSpec(memory_space=pl.ANY)] * len(after),
        out_specs=(_SEM, _SEM) + (_HBM,) * (2 * n) + (pl.BlockSpec(memory_space=pltpu.VMEM),),
        input_output_aliases={i: 2 + i for i in range(2 * n)},
        compiler_params=pltpu.CompilerParams(has_side_effects=_EFFECT))(*srcs, *lands, *after)
    return (outs[0], outs[1], list(outs[2:2 + n]), list(outs[2 + n:2 + 2 * n])), outs[-1]


def _exchange_forward(handle, plan, plan_fwd, needs, name, after):
    send_sems, recv_sems, srcs, lands = handle
    n, nfwd = len(srcs), len(needs)

    def body(*refs):
        src_refs, land_refs = refs[:n], refs[n:2 * n]
        outs = refs[2 * n + 2 + len(after):]
        first = _plan_copies(plan, src_refs, land_refs, refs[2 * n], refs[2 * n + 1])
        for cp, need in zip(_plan_copies(plan_fwd, land_refs, land_refs, outs[0], outs[1]), needs):
            first[need].wait_recv()
            cp.start()
        for k, cp in enumerate(first):
            cp.wait_send()
            if k not in needs:
                cp.wait_recv()

    outs = pl.pallas_call(
        body, name=name,
        out_shape=(pltpu.SemaphoreType.DMA((nfwd,)), pltpu.SemaphoreType.DMA((nfwd,)))
        + tuple(pltpu.HBM(a.shape, a.dtype) for a in srcs) + tuple(pltpu.HBM(a.shape, a.dtype) for a in lands),
        in_specs=[_HBM] * (2 * n) + [_SEM, _SEM] + [pl.BlockSpec(memory_space=pl.ANY)] * len(after),
        out_specs=(_SEM, _SEM) + (_HBM,) * (2 * n), input_output_aliases={i: 2 + i for i in range(2 * n)},
        compiler_params=pltpu.CompilerParams(has_side_effects=_EFFECT))(*srcs, *lands, send_sems, recv_sems, *after)
    return outs[0], outs[1], list(outs[2:2 + n]), list(outs[2 + n:2 + 2 * n])


def _exchange_wait(handle, plan, name, after):
    send_sems, recv_sems, srcs, lands = handle
    n = len(srcs)

    def body(*refs):
        src_refs, land_refs = refs[:n], refs[n:2 * n]
        for cp in _plan_copies(plan, src_refs, land_refs, refs[2 * n], refs[2 * n + 1]):
            cp.wait_send()
            cp.wait_recv()

    outs = pl.pallas_call(
        body, name=name,
        out_shape=tuple(pltpu.HBM(a.shape, a.dtype) for a in srcs) + tuple(pltpu.HBM(a.shape, a.dtype) for a in lands),
        in_specs=[_HBM] * (2 * n) + [_SEM, _SEM] + [pl.BlockSpec(memory_space=pl.ANY)] * len(after),
        out_specs=(_HBM,) * (2 * n), input_output_aliases={i: i for i in range(2 * n)},
        compiler_params=pltpu.CompilerParams(has_side_effects=_EFFECT))(*srcs, *lands, send_sems, recv_sems, *after)
    return list(outs[:n]), list(outs[n:])


SMALL = ("g1", "g2", "g3", "g4", "ln_g", "ln_b", "b_sp", "w_sp")


def _small_sum_adam(block, parts, lands, ws, ms, vs):
    n, ns = len(parts), len(SMALL)

    def body(blk_ref, *refs):
        p_refs, l_refs = refs[:n], refs[n:2 * n]
        w_refs, m_refs, v_refs = (refs[2 * n + i * ns:2 * n + (i + 1) * ns] for i in range(3))
        o = 2 * n + 3 * ns
        loss_ref = refs[o]
        g_out, d_out, m_out, v_out = (refs[o + 1 + i * ns:o + 1 + (i + 1) * ns] for i in range(4))
        me = blk_ref[0]
        sums = []
        for k in range(n):
            acc = jnp.where(me == 0, p_refs[k][...], l_refs[k][0])
            for b in range(1, 8):
                acc = acc + jnp.where(me == b, p_refs[k][...], l_refs[k][b])
            sums.append(acc)
        loss_ref[...] = sums[0]
        for i in range(ns):
            g_out[i][...] = sums[1 + i]
            d_out[i][...], m_out[i][...], v_out[i][...] = _adam_math(w_refs[i][...], sums[1 + i], m_refs[i][...],
                                                                     v_refs[i][...])

    args = list(parts) + list(lands) + [t[k] for t in (ws, ms, vs) for k in SMALL]
    shapes = [jax.ShapeDtypeStruct(p.shape, F32) for p in parts[:1]] + [jax.ShapeDtypeStruct(p.shape, F32) for p in parts[1:]] * 4
    vmem = pl.BlockSpec(memory_space=pltpu.VMEM)
    outs = pl.pallas_call(
        body, name="small_sum_adam", out_shape=tuple(shapes),
        in_specs=[pl.BlockSpec(memory_space=pltpu.SMEM)] + [vmem] * len(args), out_specs=(vmem,) * len(shapes),
        compiler_params=pltpu.CompilerParams(vmem_limit_bytes=VMEM_LIMIT))(block, *args)
    groups = [dict(zip(SMALL, outs[1 + i * ns:1 + (i + 1) * ns])) for i in range(4)]
    return (outs[0], *groups)


def _rope_tables(S):
    half = HD // 2
    inv_freq = jnp.tile(ROPE_THETA ** (-jnp.arange(half, dtype=F32) / half), 4)
    sign = jnp.tile(jnp.concatenate([-jnp.ones(half, F32), jnp.ones(half, F32)]), 2)
    ang = jnp.arange(S, dtype=F32)[:, None] * inv_freq[None, :]
    return jnp.cos(ang), jnp.sin(ang) * sign[None, :]


def _to_blocks(g):
    return g.reshape(8, g.shape[0] // 8, g.shape[1])


def _from_blocks(t, col_sharded):
    if col_sharded:
        return t.transpose(1, 0, 2).reshape(t.shape[1], 8 * t.shape[2])
    return t.reshape(8 * t.shape[1], t.shape[2])


class _NoComm:
    def __init__(self, late_weights):
        self._late = late_weights
        self.grads = {}

    def start_tie(self):
        return jnp.zeros((8, 128), F32)

    def late_weights(self, after):
        return self._late

    def small_start(self, loss_row, grads):
        self.small = (loss_row, grads)
        return jnp.zeros((8, 128), F32)

    def rs_start(self, key, gblocks, after=()):
        self.grads[key] = gblocks
        return jnp.zeros((8, 128), F32)

    def rs_mid(self, key, after):
        return jnp.zeros((8, 128), F32)


class _FsdpComm:
    def __init__(self, late_shards, col_sharded, after, idx, block):
        self._col, self._idx, self._block, self._rs = col_sharded, idx, block, {}
        n = len(late_shards)
        self._gather, self._token = _exchange_start(
            late_shards, [(8,) + s.shape for s in late_shards], _plan_gather(n), 7 * n, "ag_late_start", (after,))

    def start_tie(self):
        return self._token

    def late_weights(self, after):
        shards, lands = _exchange_wait(self._gather, _plan_gather(len(self._col)), "ag_late_wait", after)
        lands = [lax.dynamic_update_index_in_dim(t, s, self._block, 0) for t, s in zip(lands, shards)]
        return [_from_blocks(t, cs) for t, cs in zip(lands, self._col)]

    def small_start(self, loss_row, grads):
        parts = [loss_row] + [grads[k] for k in SMALL]
        n = len(parts)
        self._small, token = _exchange_start(parts, [(8,) + p.shape for p in parts], _plan_gather(n), 7 * n,
                                             "small_start", (self._token,))
        return token

    def small_finish(self, ws, ms, vs, after):
        n = 1 + len(SMALL)
        parts, lands = _exchange_wait(self._small, _plan_gather(n), "small_wait", after)
        return _small_sum_adam(jnp.reshape(self._block, (1,)).astype(jnp.int32), parts, lands, ws, ms, vs)

    def rs_start(self, key, gblocks, after=()):
        n = len(gblocks)
        d2d, token = _exchange_start(gblocks, [(4,) + g.shape[1:] for g in gblocks], _plan_d2d(n), 4 * n,
                                     "rs_%s_d2d_start" % key, (self._token,) + tuple(after))
        self._rs[key] = dict(n=n, d2d=d2d)
        return token

    def rs_mid(self, key, after):
        st = self._rs[key]
        gblocks, from_sib = _exchange_wait(st["d2d"], _plan_d2d(st["n"]), "rs_%s_d2d_wait" % key, after)
        st["own"], sends = _rs_add(gblocks, from_sib, self._idx, "rs_add_" + key)
        st["ici"], token = _exchange_start(sends, [t.shape for t in sends], _plan_ici(st["n"]), 3 * st["n"],
                                           "rs_%s_ici_start" % key, (self._token,))
        return token

    def rs_end(self, key, after):
        st = self._rs[key]
        return st["own"], _exchange_wait(st["ici"], _plan_ici(st["n"]), "rs_%s_ici_wait" % key, after)[1]


def _local_step(x, tgt, h1, cos_t, sin_t, win_t, comm, g1, g2, g3, g4, ln_g, ln_b, w_sp, b_sp):
    b_sp_t = b_sp.T
    w_sp_t = w_sp.transpose(0, 2, 1)

    *qkvs, puz, gates = _proj_all(h1, win_t, cos_t, sin_t, comm.start_tie())
    fwd = _attn_fwd(qkvs)
    ya, *lses = _attn_combine([o for o, _ in fwd], [l for _, l in fwd])
    yg = _gmlp_fwd(puz, ln_g, ln_b, w_sp, b_sp_t)
    wba, wbg, wout, wmi, wmo = comm.late_weights(after=(ya, yg, gates))
    merged, y, x2, h3 = _merge_fwd(ya, yg, gates, wba, wbg, wout, x, g2, g3)
    a, dy2, dout, loss_row, dg4 = _mlp_fwd(h3, wmi, wmo, x2, tgt, g4)

    dpre, dx2, dy, dg3, dg2 = _mlp_bwd(dy2, a, wmo, wmi, x2, y, dout, g2, g3)
    dwmo = _mm_tn(a, dy2, "dw_mlp_out", square_a=True)
    dwmi = _mm_tn(h3, dpre, "dw_mlp_in", col_blocks=8)
    tie = comm.rs_start("mlp", [dwmi, _to_blocks(dwmo)])
    dgates, da, db, dyg, *rest = _outproj_bwd(dy, wout, ya, yg, gates, wba, wbg, tie)
    dyas, dsums = rest[:3], rest[3:]
    tie = comm.rs_mid("mlp", after=(dyg,))
    dpuz, dwsp, dbs, dlng, dlnb = _gmlp_bwd(puz, dyg, ln_g + tie[0, 0], ln_b, w_sp, w_sp_t, b_sp_t)
    dqkvs = _attn_bwd(qkvs, dyas, dsums, lses)
    grad_x, dg1, dproj = _inproj_bwd(dqkvs, dpuz, dgates, win_t, cos_t, sin_t, x, dx2, g1)
    small = dict(g1=dg1, g2=dg2, g3=dg3, g4=dg4, ln_g=dlng, ln_b=dlnb, b_sp=dbs.reshape(4, CHUNK),
                 w_sp=dwsp.reshape(4 * CHUNK, CHUNK))
    tie = comm.small_start(loss_row, small)
    dwin_t = _mm_tn(dproj, h1, "dw_in", tie=tie)
    tie = comm.rs_start("win", [_to_blocks(dwin_t)])
    dwout = _mm_tn(merged, dy, "dw_out", tie=tie)
    tie = comm.rs_mid("win", after=(dwout,))
    dwba = _mm_tn(ya, da, "dw_branch_attn", tie=tie, col_blocks=8)
    dwbg = _mm_tn(yg, db, "dw_branch_gmlp", tie=tie, col_blocks=8)
    tie = comm.rs_start("mid", [dwba, dwbg, _to_blocks(dwout)], after=(tie,))
    return grad_x, tie


def kernel(x, norm_pre_mix, w_in, w_spatial, b_spatial, ln_v_gain, ln_v_bias, w_branch_attn, w_branch_gmlp, w_out, norm_post_mix, norm_pre_mlp, w_mlp_in, w_mlp_out, norm_post_mlp, loss_target, m_norm_pre_mix, m_w_in, m_w_spatial, m_b_spatial, m_ln_v_gain, m_ln_v_bias, m_w_branch_attn, m_w_branch_gmlp, m_w_out, m_norm_post_mix, m_norm_pre_mlp, m_w_mlp_in, m_w_mlp_out, m_norm_post_mlp, v_norm_pre_mix, v_w_in, v_w_spatial, v_b_spatial, v_ln_v_gain, v_ln_v_bias, v_w_branch_attn, v_w_branch_gmlp, v_w_out, v_norm_post_mix, v_norm_pre_mlp, v_w_mlp_in, v_w_mlp_out, v_norm_post_mlp):
    mx, my, mc = _mesh_pos()
    rel = [(0, 0), (1, 0), (0, 1), (1, 1)]
    chip_of = [2 * (mx ^ dx) + (my ^ dy) for dx, dy in rel]
    idx = jnp.stack([2 * k + mc for k in chip_of] + chip_of).astype(jnp.int32)

    w_in_t, m_w_in_t, v_w_in_t = (t[0].T for t in (w_in, m_w_in, v_w_in))
    shard = w_in_t.astype(BF)
    gather, token = _exchange_start([shard], [(8,) + shard.shape], _plan_gather_near, 4, "ag_win_start", ())
    h1 = _rms_fwd(x[0], norm_pre_mix + token[0, 0], "rms_pre_mix")
    cos_t, sin_t = _rope_tables(x.shape[1])
    gather = _exchange_forward(gather, _plan_gather_near, _plan_gather_pass(True), (1, 2, 3), "ag_win_pass",
                               (h1, cos_t, sin_t))
    (shard,), (g_win,) = _exchange_wait(gather, _plan_gather_pass(False), "ag_win_wait", ())
    g_win = lax.dynamic_update_index_in_dim(g_win, shard, 4 * mx + 2 * my + mc, 0)
    win_t = _from_blocks(g_win, False)
    late = [w_branch_attn[0], w_branch_gmlp[0], w_out[0], w_mlp_in[0], w_mlp_out[0]]
    comm = _FsdpComm([w.astype(BF) for w in late], [True, True, False, True, False], g_win, idx, 4 * mx + 2 * my + mc)

    grad_x, tie = _local_step(
        x[0], loss_target[0], h1, cos_t, sin_t, win_t, comm,
        norm_pre_mix, norm_post_mix, norm_pre_mlp, norm_post_mlp, ln_v_gain, ln_v_bias, w_spatial[0], b_spatial[0])

    flat = lambda t: t.reshape(-1, t.shape[-1])
    small_w = dict(zip(SMALL, map(flat, (norm_pre_mix, norm_post_mix, norm_pre_mlp, norm_post_mlp, ln_v_gain, ln_v_bias,
                                         b_spatial, w_spatial))))
    small_m = dict(zip(SMALL, map(flat, (m_norm_pre_mix, m_norm_post_mix, m_norm_pre_mlp, m_norm_post_mlp, m_ln_v_gain,
                                         m_ln_v_bias, m_b_spatial, m_w_spatial))))
    small_v = dict(zip(SMALL, map(flat, (v_norm_pre_mix, v_norm_post_mix, v_norm_pre_mlp, v_norm_post_mlp, v_ln_v_gain,
                                         v_ln_v_bias, v_b_spatial, v_w_spatial))))
    loss_out, sg, sd, sm, sv = comm.small_finish(small_w, small_m, small_v, after=(tie,))
    loss = loss_out[0, 0]
    tie = comm.rs_mid("mid", after=(loss_out,))
    mlp_own, mlp_chips = comm.rs_end("mlp", after=(tie,))
    first = lambda ts: [t[0] for t in ts]
    upd = dict(zip(("w_mlp_in", "w_mlp_out"), _adam_shards(
        mlp_own, mlp_chips, first((w_mlp_in, w_mlp_out)), first((m_w_mlp_in, m_w_mlp_out)),
        first((v_w_mlp_in, v_w_mlp_out)), "adam_mlp")))
    win_own, win_chips = comm.rs_end("win", after=(upd["w_mlp_in"][0], upd["w_mlp_out"][0]))
    upd["w_in"] = tuple(t.T for t in _adam_shards(win_own, win_chips, [w_in_t], [m_w_in_t], [v_w_in_t], "adam_w_in")[0])
    mid_own, mid_chips = comm.rs_end("mid", after=(upd["w_in"][0],))
    upd.update(zip(("w_branch_attn", "w_branch_gmlp", "w_out"), _adam_shards(
        mid_own, mid_chips, first((w_branch_attn, w_branch_gmlp, w_out)),
        first((m_w_branch_attn, m_w_branch_gmlp, m_w_out)), first((v_w_branch_attn, v_w_branch_gmlp, v_w_out)), "adam_mid")))
    order = ["g1", "w_in", "w_sp", "b_sp", "ln_g", "ln_b", "w_branch_attn", "w_branch_gmlp", "w_out", "g2", "g3",
             "w_mlp_in", "w_mlp_out", "g4"]
    small_shape = dict(g1=norm_pre_mix.shape, g2=norm_post_mix.shape, g3=norm_pre_mlp.shape, g4=norm_post_mlp.shape,
                       ln_g=ln_v_gain.shape, ln_b=ln_v_bias.shape, b_sp=b_spatial.shape, w_sp=w_spatial.shape)

    def pick(which):
        return [upd[nm][which][None] if nm in upd else (sg, sd, sm, sv)[which][nm].reshape(small_shape[nm])
                for nm in order]

    return (loss, grad_x[None], *pick(0), *pick(1), *pick(2), *pick(3))
```
